```python
import math
import jax, jax.numpy as jnp
from jax import lax
import numpy as np

D_MODEL = 1024
BATCH = 8
SEQ = 4096
DEPTH = 1

HEAD_DIM = 64
BLOCK = 128
WINDOW = 128
A_Q_HEADS = 8
A_KV_HEADS = 2
A_GROUP = A_Q_HEADS // A_KV_HEADS
B_HEADS = 8
N_BRANCHES = 2
ROPE_THETA = 10000.0
RMS_EPS = 1e-6
_ffn_raw = (8 * D_MODEL + 2) // 3
D_FF = ((_ffn_raw + 255) // 256) * 256

A_Q_W = A_Q_HEADS * HEAD_DIM
A_KV_W = A_KV_HEADS * HEAD_DIM
B_W = B_HEADS * HEAD_DIM
GATE_W = N_BRANCHES * D_MODEL
IN_SPLITS = [A_Q_W, A_KV_W, A_KV_W, B_W, B_W, B_W, B_HEADS, GATE_W]
IN_OFFSETS = list(np.cumsum(IN_SPLITS)[:-1])
IN_W = int(sum(IN_SPLITS))
N_ADA = 6

kernel_name = "hybrid_swa_sink_fox_gated_block"


def rms_norm(x, g):
    xf = x.astype(jnp.float32)
    y = xf * lax.rsqrt(jnp.mean(xf * xf, axis=-1, keepdims=True) + RMS_EPS)
    return (y * g.astype(jnp.float32)).astype(x.dtype)


def rope(x, positions):
    dh = x.shape[-1]
    inv_freq = 1.0 / (ROPE_THETA ** (jnp.arange(0, dh, 2, dtype=jnp.float32) / dh))
    ang = positions.astype(jnp.float32)[..., None] * inv_freq
    cos = jnp.cos(ang)[:, :, None, :]
    sin = jnp.sin(ang)[:, :, None, :]
    xf = x.astype(jnp.float32)
    x1, x2 = xf[..., : dh // 2], xf[..., dh // 2:]
    out = jnp.concatenate([x1 * cos - x2 * sin, x2 * cos + x1 * sin], axis=-1)
    return out.astype(x.dtype)


def sliding_window_gqa_sinks(q, k, v, sinks):
    B, S, _, dh = q.shape
    nb = S // BLOCK
    scale = 1.0 / math.sqrt(dh)
    qb = q.reshape(B, nb, BLOCK, A_KV_HEADS, A_GROUP, dh)
    pad = ((0, 0), (BLOCK, 0), (0, 0), (0, 0))
    kp = jnp.pad(k, pad).reshape(B, nb + 1, BLOCK, A_KV_HEADS, dh)
    vp = jnp.pad(v, pad).reshape(B, nb + 1, BLOCK, A_KV_HEADS, dh)
    kb = jnp.concatenate([kp[:, :-1], kp[:, 1:]], axis=2)
    vb = jnp.concatenate([vp[:, :-1], vp[:, 1:]], axis=2)
    s = jnp.einsum('bnqhgd,bnkhd->bnhgqk', qb, kb).astype(jnp.float32) * scale
    q_loc = jnp.arange(BLOCK) + BLOCK
    k_loc = jnp.arange(2 * BLOCK)
    rel = q_loc[:, None] - k_loc[None, :]
    band = (rel >= 0) & (rel < WINDOW)
    k_abs = jnp.arange(nb)[:, None] * BLOCK + k_loc[None, :] - BLOCK
    valid = band[None, :, :] & (k_abs >= 0)[:, None, :]
    s = jnp.where(valid[None, :, None, None], s, -jnp.inf)
    sink = sinks.astype(jnp.float32).reshape(1, 1, A_KV_HEADS, A_GROUP, 1, 1)
    m = jnp.maximum(jnp.max(s, axis=-1, keepdims=True), sink)
    p = jnp.exp(s - m)
    denom = jnp.sum(p, axis=-1, keepdims=True) + jnp.exp(sink - m)
    p = p / denom
    o = jnp.einsum('bnhgqk,bnkhd->bnqhgd', p, vb.astype(jnp.float32))
    return o.reshape(B, S, A_Q_HEADS * dh)


def forgetting_attention(q, k, v, log_f):
    B, S, H, dh = q.shape
    nb = S // BLOCK
    scale = 1.0 / math.sqrt(dh)
    cum = lax.cumsum(log_f, axis=1)
    cum_k = jnp.transpose(cum, (0, 2, 1))
    k_pos = jnp.arange(S)
    vf = v.astype(jnp.float32)

    def one_block(i):
        start = i * BLOCK
        qi = lax.dynamic_slice_in_dim(q, start, BLOCK, axis=1)
        ci = lax.dynamic_slice_in_dim(cum, start, BLOCK, axis=1)
        s = jnp.einsum('bqhd,bkhd->bhqk', qi, k).astype(jnp.float32) * scale
        s = s + jnp.transpose(ci, (0, 2, 1))[..., None] - cum_k[:, :, None, :]
        q_pos = start + jnp.arange(BLOCK)
        s = jnp.where((k_pos[None, :] <= q_pos[:, None])[None, None], s, -jnp.inf)
        p = jax.nn.softmax(s, axis=-1)
        return jnp.einsum('bhqk,bkhd->bqhd', p, vf)

    o = lax.map(one_block, jnp.arange(nb))
    return jnp.transpose(o, (1, 0, 2, 3, 4)).reshape(B, S, H * dh)


def _fwd_setup_inputs(seed: int = 0) -> dict:
    key = jax.random.key(seed)
    ks = jax.random.split(key, 20)
    f32 = jnp.float32
    nrm = lambda k, shape, s: jax.random.normal(k, shape, f32) * s
    x = nrm(ks[0], (BATCH, SEQ, D_MODEL), 1.0)
    c = nrm(ks[1], (BATCH, D_MODEL), 1.0)
    positions = jnp.broadcast_to(jnp.arange(SEQ, dtype=jnp.int32), (BATCH, SEQ))
    w_ada = nrm(ks[2], (DEPTH, D_MODEL, N_ADA * D_MODEL), 0.5 * D_MODEL ** -0.5)
    b_ada = nrm(ks[3], (DEPTH, N_ADA * D_MODEL), 0.01)
    g_pre_mix = 1.0 + nrm(ks[4], (DEPTH, D_MODEL), 0.05)
    g_post_mix = 1.0 + nrm(ks[5], (DEPTH, D_MODEL), 0.05)
    w_in = nrm(ks[6], (DEPTH, D_MODEL, IN_W), D_MODEL ** -0.5)
    b_f = 1.0 + nrm(ks[7], (DEPTH, B_HEADS), 0.1)
    sinks = nrm(ks[8], (DEPTH, A_Q_HEADS), 0.5)
    w_branch_a = nrm(ks[9], (DEPTH, A_Q_W, D_MODEL), A_Q_W ** -0.5)
    w_branch_b = nrm(ks[10], (DEPTH, B_W, D_MODEL), B_W ** -0.5)
    w_out = nrm(ks[11], (DEPTH, D_MODEL, D_MODEL), D_MODEL ** -0.5)
    g_pre_ffn = 1.0 + nrm(ks[12], (DEPTH, D_MODEL), 0.05)
    g_post_ffn = 1.0 + nrm(ks[13], (DEPTH, D_MODEL), 0.05)
    w_ffn_in = nrm(ks[14], (DEPTH, D_MODEL, 2 * D_FF), D_MODEL ** -0.5)
    w_ffn_out = nrm(ks[15], (DEPTH, D_FF, D_MODEL), D_FF ** -0.5)
    return {"x": x, "c": c, "positions": positions, "w_ada": w_ada, "b_ada": b_ada,
            "g_pre_mix": g_pre_mix, "g_post_mix": g_post_mix, "w_in": w_in, "b_f": b_f,
            "sinks": sinks, "w_branch_a": w_branch_a, "w_branch_b": w_branch_b,
            "w_out": w_out, "g_pre_ffn": g_pre_ffn, "g_post_ffn": g_post_ffn,
            "w_ffn_in": w_ffn_in, "w_ffn_out": w_ffn_out}


def _fwd_reference(x, c, positions, w_ada, b_ada, g_pre_mix, g_post_mix, w_in, b_f, sinks,
              w_branch_a, w_branch_b, w_out, g_pre_ffn, g_post_ffn, w_ffn_in, w_ffn_out):
    B, S, D = x.shape
    for l in range(DEPTH):
        ada = (c @ w_ada[l] + b_ada[l]).reshape(B, N_ADA, D)[:, :, None, :]
        shift_m, scale_m, gate_m = ada[:, 0], ada[:, 1], ada[:, 2]
        shift_f, scale_f, gate_f = ada[:, 3], ada[:, 4], ada[:, 5]

        h = rms_norm(x, g_pre_mix[l]) * (1.0 + scale_m) + shift_m
        proj = h @ w_in[l]
        qa, ka, va, qb, kb, vb, f_logit, gate_logit = jnp.split(proj, IN_OFFSETS, axis=-1)
        qa = rope(qa.reshape(B, S, A_Q_HEADS, HEAD_DIM), positions)
        ka = rope(ka.reshape(B, S, A_KV_HEADS, HEAD_DIM), positions)
        va = va.reshape(B, S, A_KV_HEADS, HEAD_DIM)
        o_a = sliding_window_gqa_sinks(qa, ka, va, sinks[l]).astype(x.dtype)

        log_f = jax.nn.log_sigmoid((f_logit + b_f[l]).astype(jnp.float32))
        o_b = forgetting_attention(qb.reshape(B, S, B_HEADS, HEAD_DIM),
                                   kb.reshape(B, S, B_HEADS, HEAD_DIM),
                                   vb.reshape(B, S, B_HEADS, HEAD_DIM), log_f).astype(x.dtype)

        gates = jax.nn.sigmoid(gate_logit).reshape(B, S, N_BRANCHES, D)
        merged = gates[:, :, 0] * (o_a @ w_branch_a[l]) + gates[:, :, 1] * (o_b @ w_branch_b[l])
        y = merged @ w_out[l]
        x = x + gate_m * rms_norm(y, g_post_mix[l])

        h = rms_norm(x, g_pre_ffn[l]) * (1.0 + scale_f) + shift_f
        gu = h @ w_ffn_in[l]
        g_part, u_part = gu[..., :D_FF], gu[..., D_FF:]
        y = (jax.nn.silu(g_part) * u_part) @ w_ffn_out[l]
        x = x + gate_f * rms_norm(y, g_post_ffn[l])
    return x


import jax as _jax
import jax.numpy as _jnp

TWIN_FORMAT = 'train_step'
FWD_PARAMS = ['x', 'c', 'positions', 'w_ada', 'b_ada', 'g_pre_mix', 'g_post_mix', 'w_in', 'b_f', 'sinks', 'w_branch_a', 'w_branch_b', 'w_out', 'g_pre_ffn', 'g_post_ffn', 'w_ffn_in', 'w_ffn_out']
TWIN_WEIGHTS = ['w_ada', 'b_ada', 'g_pre_mix', 'g_post_mix', 'w_in', 'b_f', 'sinks', 'w_branch_a', 'w_branch_b', 'w_out', 'g_pre_ffn', 'g_post_ffn', 'w_ffn_in', 'w_ffn_out']
TWIN_DIFF_INPUT = 'x'
TWIN_INPUTS = ['x', 'c', 'positions', 'w_ada', 'b_ada', 'g_pre_mix', 'g_post_mix', 'w_in', 'b_f', 'sinks', 'w_branch_a', 'w_branch_b', 'w_out', 'g_pre_ffn', 'g_post_ffn', 'w_ffn_in', 'w_ffn_out', 'loss_target', 'm_w_ada', 'm_b_ada', 'm_g_pre_mix', 'm_g_post_mix', 'm_w_in', 'm_b_f', 'm_sinks', 'm_w_branch_a', 'm_w_branch_b', 'm_w_out', 'm_g_pre_ffn', 'm_g_post_ffn', 'm_w_ffn_in', 'm_w_ffn_out', 'v_w_ada', 'v_b_ada', 'v_g_pre_mix', 'v_g_post_mix', 'v_w_in', 'v_b_f', 'v_sinks', 'v_w_branch_a', 'v_w_branch_b', 'v_w_out', 'v_g_pre_ffn', 'v_g_post_ffn', 'v_w_ffn_in', 'v_w_ffn_out']
TWIN_OUTPUTS = ['loss', 'grad_x', 'grad_w_ada', 'grad_b_ada', 'grad_g_pre_mix', 'grad_g_post_mix', 'grad_w_in', 'grad_b_f', 'grad_sinks', 'grad_w_branch_a', 'grad_w_branch_b', 'grad_w_out', 'grad_g_pre_ffn', 'grad_g_post_ffn', 'grad_w_ffn_in', 'grad_w_ffn_out', 'delta_w_ada', 'delta_b_ada', 'delta_g_pre_mix', 'delta_g_post_mix', 'delta_w_in', 'delta_b_f', 'delta_sinks', 'delta_w_branch_a', 'delta_w_branch_b', 'delta_w_out', 'delta_g_pre_ffn', 'delta_g_post_ffn', 'delta_w_ffn_in', 'delta_w_ffn_out', 'new_m_w_ada', 'new_m_b_ada', 'new_m_g_pre_mix', 'new_m_g_post_mix', 'new_m_w_in', 'new_m_b_f', 'new_m_sinks', 'new_m_w_branch_a', 'new_m_w_branch_b', 'new_m_w_out', 'new_m_g_pre_ffn', 'new_m_g_post_ffn', 'new_m_w_ffn_in', 'new_m_w_ffn_out', 'new_v_w_ada', 'new_v_b_ada', 'new_v_g_pre_mix', 'new_v_g_post_mix', 'new_v_w_in', 'new_v_b_f', 'new_v_sinks', 'new_v_w_branch_a', 'new_v_w_branch_b', 'new_v_w_out', 'new_v_g_pre_ffn', 'new_v_g_post_ffn', 'new_v_w_ffn_in', 'new_v_w_ffn_out']
TWIN_LEAF_KINDS = {'loss': 'loss', 'grad_x': 'grad_x', 'grad_w_ada': 'grad_w', 'grad_b_ada': 'grad_w', 'grad_g_pre_mix': 'grad_w', 'grad_g_post_mix': 'grad_w', 'grad_w_in': 'grad_w', 'grad_b_f': 'grad_w', 'grad_sinks': 'grad_w', 'grad_w_branch_a': 'grad_w', 'grad_w_branch_b': 'grad_w', 'grad_w_out': 'grad_w', 'grad_g_pre_ffn': 'grad_w', 'grad_g_post_ffn': 'grad_w', 'grad_w_ffn_in': 'grad_w', 'grad_w_ffn_out': 'grad_w', 'delta_w_ada': 'delta_w', 'delta_b_ada': 'delta_w', 'delta_g_pre_mix': 'delta_w', 'delta_g_post_mix': 'delta_w', 'delta_w_in': 'delta_w', 'delta_b_f': 'delta_w', 'delta_sinks': 'delta_w', 'delta_w_branch_a': 'delta_w', 'delta_w_branch_b': 'delta_w', 'delta_w_out': 'delta_w', 'delta_g_pre_ffn': 'delta_w', 'delta_g_post_ffn': 'delta_w', 'delta_w_ffn_in': 'delta_w', 'delta_w_ffn_out': 'delta_w', 'new_m_w_ada': 'new_m', 'new_m_b_ada': 'new_m', 'new_m_g_pre_mix': 'new_m', 'new_m_g_post_mix': 'new_m', 'new_m_w_in': 'new_m', 'new_m_b_f': 'new_m', 'new_m_sinks': 'new_m', 'new_m_w_branch_a': 'new_m', 'new_m_w_branch_b': 'new_m', 'new_m_w_out': 'new_m', 'new_m_g_pre_ffn': 'new_m', 'new_m_g_post_ffn': 'new_m', 'new_m_w_ffn_in': 'new_m', 'new_m_w_ffn_out': 'new_m', 'new_v_w_ada': 'new_v', 'new_v_b_ada': 'new_v', 'new_v_g_pre_mix': 'new_v', 'new_v_g_post_mix': 'new_v', 'new_v_w_in': 'new_v', 'new_v_b_f': 'new_v', 'new_v_sinks': 'new_v', 'new_v_w_branch_a': 'new_v', 'new_v_w_branch_b': 'new_v', 'new_v_w_out': 'new_v', 'new_v_g_pre_ffn': 'new_v', 'new_v_g_post_ffn': 'new_v', 'new_v_w_ffn_in': 'new_v', 'new_v_w_ffn_out': 'new_v'}


def _forward(args):
    return _fwd_reference(*[args[k] for k in FWD_PARAMS])


def _output_shape():
    out = _jax.eval_shape(lambda: _forward(_fwd_setup_inputs(0)))
    return out.shape, out.dtype

N_MICROBATCH = 1
ADAM_LR = 0.001
ADAM_B1 = 0.9
ADAM_B2 = 0.999
ADAM_EPS = 1e-08
ADAM_WD = 0.01
ADAM_STEP = 10
PER_EXAMPLE_BATCH_AXIS = {'x': 0, 'c': 0, 'positions': 0, 'loss_target': 0}
SHARED_INPUTS = []
_WEIGHT_DTYPES = {'w_ada': _jnp.float32, 'b_ada': _jnp.float32, 'g_pre_mix': _jnp.float32, 'g_post_mix': _jnp.float32, 'w_in': _jnp.float32, 'b_f': _jnp.float32, 'sinks': _jnp.float32, 'w_branch_a': _jnp.float32, 'w_branch_b': _jnp.float32, 'w_out': _jnp.float32, 'g_pre_ffn': _jnp.float32, 'g_post_ffn': _jnp.float32, 'w_ffn_in': _jnp.float32, 'w_ffn_out': _jnp.float32}
MOMENT_SCALE = {'w_ada': 5.574645e+00, 'b_ada': 5.362292e+00, 'g_pre_mix': 2.622385e-01, 'g_post_mix': 1.050470e+01, 'w_in': 1.168240e+00, 'b_f': 1.784030e+00, 'sinks': 6.748555e-02, 'w_branch_a': 1.759797e+00, 'w_branch_b': 1.757161e+00, 'w_out': 2.419594e+00, 'g_pre_ffn': 4.193129e-01, 'g_post_ffn': 9.811408e+00, 'w_ffn_in': 3.526658e-01, 'w_ffn_out': 6.961895e-01}


def _to_microbatches(a, axis):
    t = _jnp.moveaxis(a, axis, 0)
    t = t.reshape((N_MICROBATCH, t.shape[0] // N_MICROBATCH) + t.shape[1:])
    return _jnp.moveaxis(t, 1, axis + 1)


def setup_inputs(seed: int = 0) -> dict:
    inp = _fwd_setup_inputs(seed)
    key = _jax.random.fold_in(_jax.random.key(seed), 7919)
    shape, _ = _output_shape()
    out = dict(inp)
    out["loss_target"] = _jax.random.normal(_jax.random.fold_in(key, 0), shape, _jnp.float32)
    for i, name in enumerate(TWIN_WEIGHTS):
        w = inp[name].astype(_jnp.float32)
        if MOMENT_SCALE is None:
            s = _jnp.sqrt(_jnp.mean(_jnp.square(w)) + 1e-30)
        else:
            s = MOMENT_SCALE[name]
        km, kv = _jax.random.split(_jax.random.fold_in(key, i + 1))
        out[name] = w
        out["m_" + name] = s * _jax.random.normal(km, w.shape, _jnp.float32)
        out["v_" + name] = (s * s) * _jax.random.uniform(kv, w.shape, _jnp.float32, 0.5, 1.5)
    if N_MICROBATCH > 1:
        for name, axis in PER_EXAMPLE_BATCH_AXIS.items():
            out[name] = _to_microbatches(out[name], axis)
    return {'x': out['x'], 'c': out['c'], 'positions': out['positions'], 'w_ada': out['w_ada'], 'b_ada': out['b_ada'], 'g_pre_mix': out['g_pre_mix'], 'g_post_mix': out['g_post_mix'], 'w_in': out['w_in'], 'b_f': out['b_f'], 'sinks': out['sinks'], 'w_branch_a': out['w_branch_a'], 'w_branch_b': out['w_branch_b'], 'w_out': out['w_out'], 'g_pre_ffn': out['g_pre_ffn'], 'g_post_ffn': out['g_post_ffn'], 'w_ffn_in': out['w_ffn_in'], 'w_ffn_out': out['w_ffn_out'], 'loss_target': out['loss_target'], 'm_w_ada': out['m_w_ada'], 'm_b_ada': out['m_b_ada'], 'm_g_pre_mix': out['m_g_pre_mix'], 'm_g_post_mix': out['m_g_post_mix'], 'm_w_in': out['m_w_in'], 'm_b_f': out['m_b_f'], 'm_sinks': out['m_sinks'], 'm_w_branch_a': out['m_w_branch_a'], 'm_w_branch_b': out['m_w_branch_b'], 'm_w_out': out['m_w_out'], 'm_g_pre_ffn': out['m_g_pre_ffn'], 'm_g_post_ffn': out['m_g_post_ffn'], 'm_w_ffn_in': out['m_w_ffn_in'], 'm_w_ffn_out': out['m_w_ffn_out'], 'v_w_ada': out['v_w_ada'], 'v_b_ada': out['v_b_ada'], 'v_g_pre_mix': out['v_g_pre_mix'], 'v_g_post_mix': out['v_g_post_mix'], 'v_w_in': out['v_w_in'], 'v_b_f': out['v_b_f'], 'v_sinks': out['v_sinks'], 'v_w_branch_a': out['v_w_branch_a'], 'v_w_branch_b': out['v_w_branch_b'], 'v_w_out': out['v_w_out'], 'v_g_pre_ffn': out['v_g_pre_ffn'], 'v_g_post_ffn': out['v_g_post_ffn'], 'v_w_ffn_in': out['v_w_ffn_in'], 'v_w_ffn_out': out['v_w_ffn_out']}


def _loss(weights, diff, rest, loss_target):
    with _jax.named_scope("forward"):
        args = {**rest, TWIN_DIFF_INPUT: diff, **{k: w.astype(_WEIGHT_DTYPES[k]) for k, w in weights.items()}}
        y = _forward(args)
    with _jax.named_scope("loss_head"):
        err = _jnp.square(y.astype(_jnp.float32) - loss_target)
        return 0.5 * _jnp.sum(_jnp.mean(err, axis=-1)) if err.ndim else 0.5 * err


def _adamw(w, g, m, v):
    m = ADAM_B1 * m + (1.0 - ADAM_B1) * g
    v = ADAM_B2 * v + (1.0 - ADAM_B2) * _jnp.square(g)
    m_hat = m / (1.0 - ADAM_B1 ** ADAM_STEP)
    v_hat = v / (1.0 - ADAM_B2 ** ADAM_STEP)
    delta = -ADAM_LR * (m_hat / (_jnp.sqrt(v_hat) + ADAM_EPS) + ADAM_WD * w)
    return delta, m, v


def reference(x, c, positions, w_ada, b_ada, g_pre_mix, g_post_mix, w_in, b_f, sinks, w_branch_a, w_branch_b, w_out, g_pre_ffn, g_post_ffn, w_ffn_in, w_ffn_out, loss_target, m_w_ada, m_b_ada, m_g_pre_mix, m_g_post_mix, m_w_in, m_b_f, m_sinks, m_w_branch_a, m_w_branch_b, m_w_out, m_g_pre_ffn, m_g_post_ffn, m_w_ffn_in, m_w_ffn_out, v_w_ada, v_b_ada, v_g_pre_mix, v_g_post_mix, v_w_in, v_b_f, v_sinks, v_w_branch_a, v_w_branch_b, v_w_out, v_g_pre_ffn, v_g_post_ffn, v_w_ffn_in, v_w_ffn_out):
    given = dict(x=x, c=c, positions=positions, w_ada=w_ada, b_ada=b_ada, g_pre_mix=g_pre_mix, g_post_mix=g_post_mix, w_in=w_in, b_f=b_f, sinks=sinks, w_branch_a=w_branch_a, w_branch_b=w_branch_b, w_out=w_out, g_pre_ffn=g_pre_ffn, g_post_ffn=g_post_ffn, w_ffn_in=w_ffn_in, w_ffn_out=w_ffn_out, loss_target=loss_target, m_w_ada=m_w_ada, m_b_ada=m_b_ada, m_g_pre_mix=m_g_pre_mix, m_g_post_mix=m_g_post_mix, m_w_in=m_w_in, m_b_f=m_b_f, m_sinks=m_sinks, m_w_branch_a=m_w_branch_a, m_w_branch_b=m_w_branch_b, m_w_out=m_w_out, m_g_pre_ffn=m_g_pre_ffn, m_g_post_ffn=m_g_post_ffn, m_w_ffn_in=m_w_ffn_in, m_w_ffn_out=m_w_ffn_out, v_w_ada=v_w_ada, v_b_ada=v_b_ada, v_g_pre_mix=v_g_pre_mix, v_g_post_mix=v_g_post_mix, v_w_in=v_w_in, v_b_f=v_b_f, v_sinks=v_sinks, v_w_branch_a=v_w_branch_a, v_w_branch_b=v_w_branch_b, v_w_out=v_w_out, v_g_pre_ffn=v_g_pre_ffn, v_g_post_ffn=v_g_post_ffn, v_w_ffn_in=v_w_ffn_in, v_w_ffn_out=v_w_ffn_out)
    weights = {n: given[n] for n in TWIN_WEIGHTS}
    shared = {n: given[n] for n in SHARED_INPUTS}
    per_example = {n: given[n] for n in ['x', 'c', 'positions']}
    grad_fn = _jax.value_and_grad(_loss, argnums=(0, 1))

    def one_microbatch(ex, loss_target):
        ex = dict(ex)
        diff = ex.pop(TWIN_DIFF_INPUT)
        return grad_fn(weights, diff, {**shared, **ex}, loss_target)

    if N_MICROBATCH == 1:
        loss, (grad_w, grad_x) = one_microbatch(per_example, given["loss_target"])
    else:
        def body(carry, xs):
            loss_sum, grad_sum = carry
            l_k, (gw_k, gx_k) = one_microbatch(xs[0], xs[1])
            with _jax.named_scope("update"):
                return (loss_sum + l_k, _jax.tree.map(_jnp.add, grad_sum, gw_k)), gx_k

        init = (_jnp.zeros((), _jnp.float32), _jax.tree.map(_jnp.zeros_like, weights))
        (loss, grad_w), grad_x = _jax.lax.scan(body, init, (per_example, given["loss_target"]))
    with _jax.named_scope("update"):
        delta_w, new_m, new_v = {}, {}, {}
        for n in TWIN_WEIGHTS:
            delta_w[n], new_m[n], new_v[n] = _adamw(weights[n], grad_w[n], given["m_" + n], given["v_" + n])
    return (loss, grad_x, *[grad_w[n] for n in TWIN_WEIGHTS], *[delta_w[n] for n in TWIN_WEIGHTS],
            *[new_m[n] for n in TWIN_WEIGHTS], *[new_v[n] for n in TWIN_WEIGHTS])
```

```python
import functools
import math

import numpy as np
import jax
import jax.numpy as jnp
from jax import lax
from jax.experimental import pallas as pl
from jax.experimental.pallas import tpu as pltpu

F32 = jnp.float32
BF = jnp.bfloat16

D_MODEL = 1024
HEAD_DIM = 64
LANES = 128
WINDOW = 128
A_Q_HEADS = 8
A_KV_HEADS = 2
B_HEADS = 8
D_FF = 2816
ROPE_THETA = 10000.0
RMS_EPS = 1e-6
N_ADA = 6
N_DEV = 8
N_CHIP = 4

ADAM_LR = 0.001
ADAM_B1 = 0.9
ADAM_B2 = 0.999
ADAM_EPS = 1e-08
ADAM_WD = 0.01
ADAM_STEP = 10

VMEM_LIMIT = 48 * 1024 * 1024
MESH = pl.DeviceIdType.MESH

A_HEAD_ORDER = (0, 4, 1, 5, 2, 6, 3, 7)

OFF_QA, OFF_KA, OFF_F = 0, 512, 640
W_A = 768
OFF_VA, OFF_QB, OFF_KB, OFF_VB = 0, 128, 640, 1152
W_B = 1664
W_G = 2048
W_PERM = W_A + W_B + W_G


def _tile(n, cap, mult=LANES):
    if n <= cap:
        return n
    t = (cap // mult) * mult
    while t >= mult:
        if n % t == 0:
            return t
        t -= mult
    raise ValueError(f"no tile for {n}")


def _cparams(*sem):
    return pltpu.CompilerParams(dimension_semantics=sem, vmem_limit_bytes=VMEM_LIMIT)


def _mm(a, b, mode, out_dtype, name, tm_cap=512, tn_cap=1664, tk_cap=1408):
    if mode == "nn":
        (M, K), (K2, N) = a.shape, b.shape
        dims = (((1,), (0,)), ((), ()))
    elif mode == "nt":
        (M, K), (N, K2) = a.shape, b.shape
        dims = (((1,), (1,)), ((), ()))
    else:
        (K, M), (K2, N) = a.shape, b.shape
        dims = (((0,), (0,)), ((), ()))
    assert K == K2, (a.shape, b.shape, mode)
    tm, tn, tk = _tile(M, tm_cap), _tile(N, tn_cap), _tile(K, tk_cap)
    nk = K // tk
    if mode == "nn":
        a_spec = pl.BlockSpec((tm, tk), lambda i, j, k: (i, k))
        b_spec = pl.BlockSpec((tk, tn), lambda i, j, k: (k, j))
    elif mode == "nt":
        a_spec = pl.BlockSpec((tm, tk), lambda i, j, k: (i, k))
        b_spec = pl.BlockSpec((tn, tk), lambda i, j, k: (j, k))
    else:
        a_spec = pl.BlockSpec((tk, tm), lambda i, j, k: (k, i))
        b_spec = pl.BlockSpec((tk, tn), lambda i, j, k: (k, j))

    def kern(a_ref, b_ref, o_ref, acc_ref):
        k = pl.program_id(2)
        part = lax.dot_general(a_ref[...].astype(BF), b_ref[...].astype(BF), dims,
                               preferred_element_type=F32)

        @pl.when(k == 0)
        def _():
            acc_ref[...] = part

        @pl.when(k > 0)
        def _():
            acc_ref[...] += part

        @pl.when(k == nk - 1)
        def _():
            o_ref[...] = acc_ref[...].astype(o_ref.dtype)

    return pl.pallas_call(
        kern, name=name,
        grid=(M // tm, N // tn, nk),
        in_specs=[a_spec, b_spec],
        out_specs=pl.BlockSpec((tm, tn), lambda i, j, k: (i, j)),
        out_shape=jax.ShapeDtypeStruct((M, N), out_dtype),
        scratch_shapes=[pltpu.VMEM((tm, tn), F32)],
        compiler_params=_cparams("parallel", "parallel", "arbitrary"),
    )(a, b)


ROWS = 256


def _row_spec(tm, width=D_MODEL, col=0):
    return pl.BlockSpec((tm, width), lambda i: (i, col))


def _vec_spec(width=D_MODEL):
    return pl.BlockSpec((1, width), lambda i: (0, 0))


def _rms(x):
    return lax.rsqrt(jnp.mean(x * x, axis=-1, keepdims=True) + RMS_EPS)


def _colsum(x):
    return jnp.sum(x, axis=0, keepdims=True)


def _norm_bwd(d_xn, xn, r):
    return r * (d_xn - xn * jnp.mean(d_xn * xn, axis=-1, keepdims=True))


def _pre_norm(x, g, scale, shift, name):
    S = x.shape[0]
    tm = _tile(S, ROWS, 8)

    def kern(x_ref, g_ref, sc_ref, sh_ref, h_ref):
        xf = x_ref[...]
        y = xf * _rms(xf) * g_ref[...]
        h_ref[...] = (y * (1.0 + sc_ref[...]) + sh_ref[...]).astype(BF)

    return pl.pallas_call(
        kern, name=name, grid=(S // tm,),
        in_specs=[_row_spec(tm), _vec_spec(), _vec_spec(), _vec_spec()],
        out_specs=_row_spec(tm),
        out_shape=jax.ShapeDtypeStruct((S, D_MODEL), BF),
        compiler_params=_cparams("parallel"),
    )(x, g, scale, shift)


def _post_pre(x, y1, g2, gate_m, g3, scale_f, shift_f):
    S = x.shape[0]
    tm = _tile(S, ROWS, 8)

    def kern(x_ref, y_ref, g2_ref, gm_ref, g3_ref, sc_ref, sh_ref, x2_ref, h2_ref):
        y = y_ref[...]
        n2 = y * _rms(y) * g2_ref[...]
        x2 = x_ref[...] + gm_ref[...] * n2
        x2_ref[...] = x2
        n3 = x2 * _rms(x2) * g3_ref[...]
        h2_ref[...] = (n3 * (1.0 + sc_ref[...]) + sh_ref[...]).astype(BF)

    return pl.pallas_call(
        kern, name="post_mix_pre_ffn", grid=(S // tm,),
        in_specs=[_row_spec(tm), _row_spec(tm)] + [_vec_spec()] * 5,
        out_specs=[_row_spec(tm), _row_spec(tm)],
        out_shape=[jax.ShapeDtypeStruct((S, D_MODEL), F32), jax.ShapeDtypeStruct((S, D_MODEL), BF)],
        compiler_params=_cparams("parallel"),
    )(x, y1, g2, gate_m, g3, scale_f, shift_f)


def _stats_spec():
    return pl.BlockSpec((8, D_MODEL), lambda i: (0, 0))


def _final(x2, y2, g4, gate_f, target):
    S = x2.shape[0]
    tm = _tile(S, ROWS, 8)

    def kern(x2_ref, y_ref, g4_ref, gf_ref, t_ref, dout_ref, dy_ref, st_ref):
        @pl.when(pl.program_id(0) == 0)
        def _():
            st_ref[...] = jnp.zeros_like(st_ref)

        y = y_ref[...]
        r = _rms(y)
        yn = y * r
        n4 = yn * g4_ref[...]
        diff = x2_ref[...] + gf_ref[...] * n4 - t_ref[...]
        d_out = diff / D_MODEL
        dout_ref[...] = d_out
        dn = d_out * gf_ref[...]
        dy_ref[...] = _norm_bwd(dn * g4_ref[...], yn, r).astype(BF)
        st_ref[0:1, :] += _colsum(d_out * n4)
        st_ref[1:2, :] += _colsum(dn * yn)
        st_ref[2:3, :] += _colsum(diff * diff)

    return pl.pallas_call(
        kern, name="final_loss", grid=(S // tm,),
        in_specs=[_row_spec(tm), _row_spec(tm), _vec_spec(), _vec_spec(), _row_spec(tm)],
        out_specs=[_row_spec(tm), _row_spec(tm), _stats_spec()],
        out_shape=[jax.ShapeDtypeStruct((S, D_MODEL), F32), jax.ShapeDtypeStruct((S, D_MODEL), BF),
                   jax.ShapeDtypeStruct((8, D_MODEL), F32)],
        compiler_params=_cparams("arbitrary"),
    )(x2, y2, g4, gate_f, target)


def _mid_bwd(d_h2, x2, d_out, y1, g3, scale_f, g2, gate_m):
    S = x2.shape[0]
    tm = _tile(S, ROWS, 8)

    def kern(dh_ref, x2_ref, dout_ref, y_ref, g3_ref, sc_ref, g2_ref, gm_ref, dx2_ref, dy_ref, st_ref):
        @pl.when(pl.program_id(0) == 0)
        def _():
            st_ref[...] = jnp.zeros_like(st_ref)

        dh = dh_ref[...]
        x2 = x2_ref[...]
        r3 = _rms(x2)
        xn = x2 * r3
        one_sc = 1.0 + sc_ref[...]
        d_x2 = dout_ref[...] + _norm_bwd(dh * one_sc * g3_ref[...], xn, r3)
        dx2_ref[...] = d_x2
        y = y_ref[...]
        r2 = _rms(y)
        yn = y * r2
        dn = d_x2 * gm_ref[...]
        dy_ref[...] = _norm_bwd(dn * g2_ref[...], yn, r2).astype(BF)
        st_ref[0:1, :] += _colsum(dh)
        st_ref[1:2, :] += _colsum(dh * (xn * g3_ref[...]))
        st_ref[2:3, :] += _colsum(dh * one_sc * xn)
        st_ref[3:4, :] += _colsum(d_x2 * (yn * g2_ref[...]))
        st_ref[4:5, :] += _colsum(dn * yn)

    return pl.pallas_call(
        kern, name="mid_bwd", grid=(S // tm,),
        in_specs=[_row_spec(tm)] * 4 + [_vec_spec()] * 4,
        out_specs=[_row_spec(tm), _row_spec(tm), _stats_spec()],
        out_shape=[jax.ShapeDtypeStruct((S, D_MODEL), F32), jax.ShapeDtypeStruct((S, D_MODEL), BF),
                   jax.ShapeDtypeStruct((8, D_MODEL), F32)],
        compiler_params=_cparams("arbitrary"),
    )(d_h2, x2, d_out, y1, g3, scale_f, g2, gate_m)


def _pre_bwd(d_h1, x, d_x2, g1, scale_m):
    S = x.shape[0]
    tm = _tile(S, ROWS, 8)

    def kern(dh_ref, x_ref, dx2_ref, g_ref, sc_ref, gx_ref, st_ref):
        @pl.when(pl.program_id(0) == 0)
        def _():
            st_ref[...] = jnp.zeros_like(st_ref)

        dh = dh_ref[...]
        xf = x_ref[...]
        r = _rms(xf)
        xn = xf * r
        one_sc = 1.0 + sc_ref[...]
        gx_ref[...] = dx2_ref[...] + _norm_bwd(dh * one_sc * g_ref[...], xn, r)
        st_ref[0:1, :] += _colsum(dh)
        st_ref[1:2, :] += _colsum(dh * (xn * g_ref[...]))
        st_ref[2:3, :] += _colsum(dh * one_sc * xn)

    return pl.pallas_call(
        kern, name="pre_mix_bwd", grid=(S // tm,),
        in_specs=[_row_spec(tm)] * 3 + [_vec_spec()] * 2,
        out_specs=[_row_spec(tm), _stats_spec()],
        out_shape=[jax.ShapeDtypeStruct((S, D_MODEL), F32), jax.ShapeDtypeStruct((8, D_MODEL), F32)],
        compiler_params=_cparams("arbitrary"),
    )(d_h1, x, d_x2, g1, scale_m)


def _rope(xs, widths, cos_t, sin_t, name):
    S = xs[0].shape[0]
    tm = _tile(S, 512, 8)
    n = len(xs)

    def kern(*refs):
        cos = refs[n][...]
        sin = refs[n + 1][...]
        first = (lax.broadcasted_iota(jnp.int32, cos.shape, 1) % HEAD_DIM) < HEAD_DIM // 2
        for x_ref, o_ref, w in zip(refs[:n], refs[n + 2:], widths):
            for c0 in range(0, w, LANES):
                v = x_ref[:, c0:c0 + LANES]
                partner = jnp.where(first, pltpu.roll(v, LANES - HEAD_DIM // 2, 1),
                                    pltpu.roll(v, HEAD_DIM // 2, 1))
                o_ref[:, c0:c0 + LANES] = (v * cos + partner * sin).astype(BF)

    return pl.pallas_call(
        kern, name=name, grid=(S // tm,),
        in_specs=[_row_spec(tm, w) for w in widths] + [_row_spec(tm, LANES)] * 2,
        out_specs=[_row_spec(tm, w) for w in widths],
        out_shape=[jax.ShapeDtypeStruct((S, w), BF) for w in widths],
        compiler_params=_cparams("parallel"),
    )(*xs, cos_t, sin_t)


def _merge_fwd(pg, pa, pb):
    S = pa.shape[0]
    tm = _tile(S, ROWS, 8)

    def kern(ga_ref, gb_ref, pa_ref, pb_ref, o_ref):
        ga = jax.nn.sigmoid(ga_ref[...].astype(F32))
        gb = jax.nn.sigmoid(gb_ref[...].astype(F32))
        o_ref[...] = (ga * pa_ref[...] + gb * pb_ref[...]).astype(BF)

    return pl.pallas_call(
        kern, name="merge_fwd", grid=(S // tm,),
        in_specs=[_row_spec(tm, col=0), _row_spec(tm, col=1), _row_spec(tm), _row_spec(tm)],
        out_specs=_row_spec(tm),
        out_shape=jax.ShapeDtypeStruct((S, D_MODEL), BF),
        compiler_params=_cparams("parallel"),
    )(pg, pg, pa, pb)


def _merge_bwd(d_merged, pg, pa, pb):
    S = pa.shape[0]
    tm = _tile(S, ROWS, 8)

    def kern(dm_ref, ga_ref, gb_ref, pa_ref, pb_ref, dpa_ref, dpb_ref, dga_ref, dgb_ref):
        dm = dm_ref[...]
        ga = jax.nn.sigmoid(ga_ref[...].astype(F32))
        gb = jax.nn.sigmoid(gb_ref[...].astype(F32))
        dpa_ref[...] = (dm * ga).astype(BF)
        dpb_ref[...] = (dm * gb).astype(BF)
        dga_ref[...] = (dm * pa_ref[...] * ga * (1.0 - ga)).astype(BF)
        dgb_ref[...] = (dm * pb_ref[...] * gb * (1.0 - gb)).astype(BF)

    bf_out = jax.ShapeDtypeStruct((S, D_MODEL), BF)
    return pl.pallas_call(
        kern, name="merge_bwd", grid=(S // tm,),
        in_specs=[_row_spec(tm), _row_spec(tm, col=0), _row_spec(tm, col=1), _row_spec(tm), _row_spec(tm)],
        out_specs=[_row_spec(tm)] * 4,
        out_shape=[bf_out] * 4,
        compiler_params=_cparams("parallel"),
    )(d_merged, pg, pg, pa, pb)


def _swiglu_fwd(gu):
    S = gu.shape[0]
    tm = _tile(S, ROWS, 8)
    tc = _tile(D_FF, 1408)
    nc = D_FF // tc

    def kern(g_ref, u_ref, o_ref):
        g = g_ref[...].astype(F32)
        o_ref[...] = (g * jax.nn.sigmoid(g) * u_ref[...].astype(F32)).astype(BF)

    return pl.pallas_call(
        kern, name="swiglu_fwd", grid=(S // tm, nc),
        in_specs=[pl.BlockSpec((tm, tc), lambda i, j: (i, j)),
                  pl.BlockSpec((tm, tc), lambda i, j: (i, j + nc))],
        out_specs=pl.BlockSpec((tm, tc), lambda i, j: (i, j)),
        out_shape=jax.ShapeDtypeStruct((S, D_FF), BF),
        compiler_params=_cparams("parallel", "parallel"),
    )(gu, gu)


def _swiglu_bwd(d_act, gu):
    S = gu.shape[0]
    tm = _tile(S, ROWS, 8)
    tc = _tile(D_FF, 1408)
    nc = D_FF // tc

    def kern(da_ref, g_ref, u_ref, o_ref):
        g = g_ref[...].astype(F32)
        u = u_ref[...].astype(F32)
        da = da_ref[...]
        sg = jax.nn.sigmoid(g)
        d_g = da * u * (sg * (1.0 + g * (1.0 - sg)))
        d_u = da * (g * sg)
        o_ref[...] = jnp.where(pl.program_id(1) == 0, d_g, d_u).astype(BF)

    return pl.pallas_call(
        kern, name="swiglu_bwd", grid=(S // tm, 2, nc),
        in_specs=[pl.BlockSpec((tm, tc), lambda i, h, j: (i, j)),
                  pl.BlockSpec((tm, tc), lambda i, h, j: (i, j)),
                  pl.BlockSpec((tm, tc), lambda i, h, j: (i, j + nc))],
        out_specs=pl.BlockSpec((tm, tc), lambda i, h, j: (i, h * nc + j)),
        out_shape=jax.ShapeDtypeStruct((S, 2 * D_FF), BF),
        compiler_params=_cparams("parallel", "parallel", "parallel"),
    )(d_act, gu, gu)


def _split3(x):
    hi = x.astype(BF)
    r1 = x - hi.astype(F32)
    mid = r1.astype(BF)
    lo = (r1 - mid.astype(F32)).astype(BF)
    return hi, mid, lo


def _tri_dot(tri, x):
    return sum(jnp.dot(tri, part, preferred_element_type=F32) for part in _split3(x))


def _log_sigmoid(z):
    return jnp.minimum(z, 0.0) - jnp.log(1.0 + jnp.exp(-jnp.abs(z)))


def _fox_gate_fwd(pa, b_f_pad):
    S = pa.shape[0]
    T = _tile(S, 512, 8)
    f_col = OFF_F // LANES

    def kern(z_ref, b_ref, cum_ref, carry_ref):
        @pl.when(pl.program_id(0) == 0)
        def _():
            carry_ref[...] = jnp.zeros_like(carry_ref)

        log_f = _log_sigmoid(z_ref[...] + b_ref[...])
        row = lax.broadcasted_iota(jnp.int32, (T, T), 0)
        col = lax.broadcasted_iota(jnp.int32, (T, T), 1)
        tri = (col <= row).astype(BF)
        cum = _tri_dot(tri, log_f) + carry_ref[...]
        cum_ref[...] = cum
        carry_ref[...] = cum[T - 1:T, :]

    return pl.pallas_call(
        kern, name="fox_gate_fwd", grid=(S // T,),
        in_specs=[_row_spec(T, LANES, f_col), _vec_spec(LANES)],
        out_specs=_row_spec(T, LANES),
        out_shape=jax.ShapeDtypeStruct((S, LANES), F32),
        scratch_shapes=[pltpu.VMEM((1, LANES), F32)],
        compiler_params=_cparams("arbitrary"),
    )(pa, b_f_pad)


def _fox_gate_bwd(rowsum_ds, colsum_ds, pa, b_f_pad):
    S = pa.shape[0]
    T = _tile(S, 512, 8)
    nb = S // T
    f_col = OFF_F // LANES

    def kern(dr_ref, dc_ref, z_ref, b_ref, df_ref, dbf_ref, carry_ref):
        @pl.when(pl.program_id(0) == 0)
        def _():
            carry_ref[...] = jnp.zeros_like(carry_ref)
            dbf_ref[...] = jnp.zeros_like(dbf_ref)

        row = lax.broadcasted_iota(jnp.int32, (T, T), 0)
        col = lax.broadcasted_iota(jnp.int32, (T, T), 1)
        tri = (col >= row).astype(BF)
        rev = _tri_dot(tri, dr_ref[...] - dc_ref[...]) + carry_ref[...]
        carry_ref[...] = rev[0:1, :]
        z = z_ref[...] + b_ref[...]
        lane = lax.broadcasted_iota(jnp.int32, (T, LANES), 1)
        d_z = jnp.where(lane < B_HEADS, rev * jax.nn.sigmoid(-z), 0.0)
        df_ref[...] = d_z.astype(BF)
        dbf_ref[0:1, :] += _colsum(d_z)

    return pl.pallas_call(
        kern, name="fox_gate_bwd", grid=(nb,),
        in_specs=[pl.BlockSpec((T, LANES), lambda i: (nb - 1 - i, 0)),
                  pl.BlockSpec((T, LANES), lambda i: (nb - 1 - i, 0)),
                  pl.BlockSpec((T, LANES), lambda i: (nb - 1 - i, f_col)),
                  _vec_spec(LANES)],
        out_specs=[pl.BlockSpec((T, LANES), lambda i: (nb - 1 - i, 0)),
                   pl.BlockSpec((8, LANES), lambda i: (0, 0))],
        out_shape=[jax.ShapeDtypeStruct((S, LANES), BF), jax.ShapeDtypeStruct((8, LANES), F32)],
        scratch_shapes=[pltpu.VMEM((1, LANES), F32)],
        compiler_params=_cparams("arbitrary"),
    )(rowsum_ds, colsum_ds, pa, b_f_pad)


NEG_INF = float("-inf")
QK_SCALE = 1.0 / math.sqrt(HEAD_DIM)


def _half_mask(shape, half):
    lane = lax.broadcasted_iota(jnp.int32, shape, 1)
    return (lane < HEAD_DIM) if half == 0 else (lane >= HEAD_DIM)


def _valid(i, j, T, rowcol, window):
    rel = (i - j) * T + rowcol
    ok = rel >= 0
    if window is not None:
        ok = ok & (rel < window)
    return ok


def _attn_fwd(q_arr, q_col, k_arr, k_col, v_arr, v_col, n_pairs, kv_shared, T, window,
              cq_arr, ck_arr, sinks, name):
    S = q_arr.shape[0]
    nq = S // T
    use_bias = cq_arr is not None
    use_sink = sinks is not None
    back = 0 if window is None else -(-window // T)

    def kern(*refs):
        refs = list(refs)
        q_ref, k_ref, v_ref = refs[:3]
        pos = 3
        if use_bias:
            cq_ref, ck_ref = refs[pos:pos + 2]
            pos += 2
        if use_sink:
            sink_ref = refs[pos]
            pos += 1
        o_ref, lse_ref = refs[pos:pos + 2]
        p_id = pl.program_id(0)
        i = pl.program_id(1)
        q = q_ref[...]
        rowcol = lax.broadcasted_iota(jnp.int32, (T, T), 0) - lax.broadcasted_iota(jnp.int32, (T, T), 1)
        lo = jnp.maximum(i - back, 0) if window is not None else 0
        outs, lses = [], []
        for half in (0, 1):
            hm = _half_mask((T, LANES), half)
            qh = (jnp.where(hm, q, 0).astype(F32) * QK_SCALE).astype(BF)
            if use_bias:
                cq = cq_ref[:, half * HEAD_DIM:half * HEAD_DIM + 1]
            if use_sink:
                m0 = jnp.full((T, 1), sink_ref[2 * p_id + half], F32)
                l0 = jnp.ones((T, 1), F32)
            else:
                m0 = jnp.full((T, 1), NEG_INF, F32)
                l0 = jnp.zeros((T, 1), F32)

            def body(j, carry):
                m, l, acc = carry
                rows = pl.ds(pl.multiple_of(j * T, T), T)
                kj = k_ref[rows, :].astype(BF)
                vj = v_ref[rows, :].astype(BF)
                s = lax.dot_general(qh, kj, (((1,), (1,)), ((), ())), preferred_element_type=F32)
                if use_bias:
                    s = s + cq - ck_ref[0, half:half + 1, rows]
                s = jnp.where(_valid(i, j, T, rowcol, window), s, NEG_INF)
                m_new = jnp.maximum(m, jnp.max(s, axis=1, keepdims=True))
                alpha = jnp.exp(m - m_new)
                p = jnp.exp(s - m_new)
                l_new = alpha * l + jnp.sum(p, axis=1, keepdims=True)
                acc_new = alpha * acc + jnp.dot(p.astype(BF), vj, preferred_element_type=F32)
                return m_new, l_new, acc_new

            m, l, acc = lax.fori_loop(lo, i + 1, body, (m0, l0, jnp.zeros((T, LANES), F32)))
            outs.append(acc / l)
            lses.append(m + jnp.log(l))
        hm0 = _half_mask((T, LANES), 0)
        o_ref[...] = jnp.where(hm0, outs[0], outs[1])
        lse_ref[...] = jnp.where(hm0, lses[0], lses[1])

    kv_idx = (lambda c0: (lambda p, i: (0, c0))) if kv_shared else (lambda c0: (lambda p, i: (0, c0 + p)))
    in_specs = [pl.BlockSpec((T, LANES), lambda p, i: (i, q_col + p)),
                pl.BlockSpec((S, LANES), kv_idx(k_col)),
                pl.BlockSpec((S, LANES), kv_idx(v_col))]
    args = [q_arr, k_arr, v_arr]
    if use_bias:
        in_specs += [pl.BlockSpec((T, LANES), lambda p, i: (i, p)),
                     pl.BlockSpec((1, 2, S), lambda p, i: (p, 0, 0))]
        args += [cq_arr, ck_arr]
    if use_sink:
        in_specs.append(pl.BlockSpec(memory_space=pltpu.SMEM))
        args.append(sinks)
    out_spec = pl.BlockSpec((T, LANES), lambda p, i: (i, p))
    return pl.pallas_call(
        kern, name=name, grid=(n_pairs, nq),
        in_specs=in_specs, out_specs=[out_spec, out_spec],
        out_shape=[jax.ShapeDtypeStruct((S, n_pairs * LANES), F32)] * 2,
        compiler_params=_cparams("parallel", "arbitrary"),
    )(*args)


def _attn_bwd(q_arr, q_col, k_arr, k_col, v_arr, v_col, o_arr, do_arr, lse_arr, n_pairs, kv_shared, T,
              window, cq_arr, ck_arr, sinks, name):
    S = q_arr.shape[0]
    nq = S // T
    use_bias = cq_arr is not None
    use_sink = sinks is not None
    back = 0 if window is None else -(-window // T)
    kv_w = LANES if kv_shared else n_pairs * LANES

    def kern(*refs):
        refs = list(refs)
        q_ref, k_ref, v_ref, o_ref, do_ref, lse_ref = refs[:6]
        pos = 6
        if use_bias:
            cq_ref, ck_ref = refs[pos:pos + 2]
            pos += 2
        if use_sink:
            sink_ref = refs[pos]
            pos += 1
        dq_ref, dk_ref, dv_ref = refs[pos:pos + 3]
        pos += 3
        if use_bias:
            dck_ref, dcq_ref = refs[pos:pos + 2]
            pos += 2
        if use_sink:
            dsink_ref = refs[pos]
        p_id = pl.program_id(0)
        rowcol = lax.broadcasted_iota(jnp.int32, (T, T), 0) - lax.broadcasted_iota(jnp.int32, (T, T), 1)

        def zero_kv():
            dk_ref[...] = jnp.zeros_like(dk_ref)
            dv_ref[...] = jnp.zeros_like(dv_ref)

        if kv_shared:
            pl.when(p_id == 0)(zero_kv)
        else:
            zero_kv()
        if use_bias:
            dck_ref[...] = jnp.zeros_like(dck_ref)
        if use_sink:
            dsink_ref[...] = jnp.zeros_like(dsink_ref)

        for half in (0, 1):
            hm = _half_mask((T, LANES), half)
            lane0 = half * HEAD_DIM

            def outer(i, carry):
                qrows = pl.ds(pl.multiple_of(i * T, T), T)
                qh = (jnp.where(hm, q_ref[qrows, :], 0).astype(F32) * QK_SCALE).astype(BF)
                do_f = jnp.where(hm, do_ref[qrows, :], 0.0)
                doh = do_f.astype(BF)
                delta = jnp.sum(do_f * o_ref[qrows, :], axis=1, keepdims=True)
                lse = lse_ref[qrows, lane0:lane0 + 1]
                if use_bias:
                    cq = cq_ref[qrows, lane0:lane0 + 1]
                lo = jnp.maximum(i - back, 0) if window is not None else 0

                def inner(j, carry_in):
                    dq, rs = carry_in
                    krows = pl.ds(pl.multiple_of(j * T, T), T)
                    kj = k_ref[krows, :].astype(BF)
                    vj = v_ref[krows, :].astype(BF)
                    s = lax.dot_general(qh, kj, (((1,), (1,)), ((), ())), preferred_element_type=F32)
                    if use_bias:
                        s = s + cq - ck_ref[0, half:half + 1, krows]
                    s = jnp.where(_valid(i, j, T, rowcol, window), s, NEG_INF)
                    p = jnp.exp(s - lse)
                    dp = lax.dot_general(doh, vj, (((1,), (1,)), ((), ())), preferred_element_type=F32)
                    ds = p * (dp - delta)
                    ds_b = ds.astype(BF)
                    dv_ref[krows, :] += lax.dot_general(p.astype(BF), doh, (((0,), (0,)), ((), ())),
                                                        preferred_element_type=F32)
                    dk_ref[krows, :] += lax.dot_general(ds_b, qh, (((0,), (0,)), ((), ())),
                                                        preferred_element_type=F32)
                    if use_bias:
                        dck_ref[0, half:half + 1, krows] += jnp.sum(ds, axis=0, keepdims=True)
                        rs = rs + jnp.sum(ds, axis=1, keepdims=True)
                    kh = jnp.where(hm, kj, 0)
                    return dq + jnp.dot(ds_b, kh, preferred_element_type=F32), rs

                dq, rs = lax.fori_loop(lo, i + 1, inner, (jnp.zeros((T, LANES), F32), jnp.zeros((T, 1), F32)))
                dq = dq * QK_SCALE
                if half == 0:
                    dq_ref[qrows, :] = dq
                else:
                    dq_ref[qrows, :] += dq
                if use_bias:
                    rs_b = jnp.broadcast_to(rs, (T, LANES))
                    dcq_ref[qrows, :] = rs_b if half == 0 else jnp.where(hm, rs_b, dcq_ref[qrows, :])
                if use_sink:
                    p_sink = jnp.exp(sink_ref[2 * p_id + half] - lse)
                    dsink_ref[0, half:half + 1, :] += jnp.broadcast_to(
                        -jnp.sum(p_sink * delta, axis=0, keepdims=True), (1, LANES))
                return carry

            lax.fori_loop(0, nq, outer, 0)

    kv_idx = (lambda c0: (lambda p: (0, c0))) if kv_shared else (lambda c0: (lambda p: (0, c0 + p)))
    pair = lambda c0: pl.BlockSpec((S, LANES), lambda p: (0, c0 + p))
    in_specs = [pair(q_col), pl.BlockSpec((S, LANES), kv_idx(k_col)), pl.BlockSpec((S, LANES), kv_idx(v_col)),
                pair(0), pair(0), pair(0)]
    args = [q_arr, k_arr, v_arr, o_arr, do_arr, lse_arr]
    if use_bias:
        in_specs += [pair(0), pl.BlockSpec((1, 2, S), lambda p: (p, 0, 0))]
        args += [cq_arr, ck_arr]
    if use_sink:
        in_specs.append(pl.BlockSpec(memory_space=pltpu.SMEM))
        args.append(sinks)
    out_specs = [pair(0), pl.BlockSpec((S, LANES), kv_idx(0)), pl.BlockSpec((S, LANES), kv_idx(0))]
    out_shape = [jax.ShapeDtypeStruct((S, n_pairs * LANES), F32),
                 jax.ShapeDtypeStruct((S, kv_w), F32), jax.ShapeDtypeStruct((S, kv_w), F32)]
    if use_bias:
        out_specs += [pl.BlockSpec((1, 2, S), lambda p: (p, 0, 0)), pair(0)]
        out_shape += [jax.ShapeDtypeStruct((n_pairs, 2, S), F32), jax.ShapeDtypeStruct((S, n_pairs * LANES), F32)]
    if use_sink:
        out_specs.append(pl.BlockSpec((1, 8, LANES), lambda p: (p, 0, 0)))
        out_shape.append(jax.ShapeDtypeStruct((n_pairs, 8, LANES), F32))
    return pl.pallas_call(
        kern, name=name, grid=(n_pairs,),
        in_specs=in_specs, out_specs=out_specs, out_shape=out_shape,
        compiler_params=_cparams("arbitrary"),
    )(*args)


def _adamw(w, g, m, v, name):
    R, C = w.shape
    tr = _tile(R, 256, 8)

    def kern(w_ref, g_ref, m_ref, v_ref, d_ref, mo_ref, vo_ref):
        g_ = g_ref[...]
        m_new = ADAM_B1 * m_ref[...] + (1.0 - ADAM_B1) * g_
        v_new = ADAM_B2 * v_ref[...] + (1.0 - ADAM_B2) * (g_ * g_)
        m_hat = m_new / (1.0 - ADAM_B1 ** ADAM_STEP)
        v_hat = v_new / (1.0 - ADAM_B2 ** ADAM_STEP)
        d_ref[...] = -ADAM_LR * (m_hat / (jnp.sqrt(v_hat) + ADAM_EPS) + ADAM_WD * w_ref[...])
        mo_ref[...] = m_new
        vo_ref[...] = v_new

    spec = pl.BlockSpec((tr, C), lambda i: (i, 0))
    shape = jax.ShapeDtypeStruct((R, C), F32)
    return pl.pallas_call(
        kern, name=name, grid=(R // tr,),
        in_specs=[spec] * 4, out_specs=[spec] * 3, out_shape=[shape] * 3,
        compiler_params=_cparams("parallel"),
    )(w, g, m, v)


def _add_pair(a, b, name):
    R, C = a.shape
    tr = _tile(R, 256, 8)

    def kern(a_ref, b_ref, o_ref, ob_ref):
        s = a_ref[...] + b_ref[...].astype(F32)
        o_ref[...] = s
        ob_ref[...] = s.astype(BF)

    spec = pl.BlockSpec((tr, C), lambda i: (i, 0))
    return pl.pallas_call(
        kern, name=name, grid=(R // tr,),
        in_specs=[spec] * 2, out_specs=[spec] * 2,
        out_shape=[jax.ShapeDtypeStruct((R, C), F32), jax.ShapeDtypeStruct((R, C), BF)],
        compiler_params=_cparams("parallel"),
    )(a, b)


def _add_three(own, recv, name):
    R, C = own.shape
    tr = _tile(R, 256, 8)

    def kern(o_ref, r0_ref, r1_ref, r2_ref, out_ref):
        out_ref[...] = ((o_ref[...] + r0_ref[...].astype(F32)) + r1_ref[...].astype(F32)) + r2_ref[...].astype(F32)

    slab = lambda k: pl.BlockSpec((None, tr, C), lambda i: (k, i, 0))
    return pl.pallas_call(
        kern, name=name, grid=(R // tr,),
        in_specs=[pl.BlockSpec((tr, C), lambda i: (i, 0)), slab(0), slab(1), slab(2)],
        out_specs=pl.BlockSpec((tr, C), lambda i: (i, 0)),
        out_shape=jax.ShapeDtypeStruct((R, C), F32),
        compiler_params=_cparams("parallel"),
    )(own, recv, recv, recv)


SM_ADA, SM_G, SM_LOSS, SM_BF, SM_SINK, SM_LEN = 0, 6144, 10240, 11264, 11272, 12288


def _small_finalize(gathered):
    def kern(g_ref, tot_ref, loss_ref):
        tot = g_ref[0:1, :]
        for b in range(1, N_DEV):
            tot = tot + g_ref[b:b + 1, :]
        tot_ref[...] = tot
        sq = jnp.sum(tot[:, SM_LOSS:SM_LOSS + D_MODEL], axis=1, keepdims=True)
        loss_ref[...] = jnp.broadcast_to(sq * (0.5 / D_MODEL), (1, LANES))

    full = lambda shape: pl.BlockSpec(shape, lambda i: (0, 0))
    return pl.pallas_call(
        kern, name="small_finalize", grid=(1,),
        in_specs=[full((N_DEV, SM_LEN))],
        out_specs=[full((1, SM_LEN)), full((1, LANES))],
        out_shape=[jax.ShapeDtypeStruct((1, SM_LEN), F32), jax.ShapeDtypeStruct((1, LANES), F32)],
        compiler_params=_cparams("arbitrary"),
    )(gathered)


def _ada_dw(c_t, d_ada):
    N = d_ada.shape[1]
    tn = _tile(N, 512)

    def kern(c_ref, d_ref, o_ref):
        acc = c_ref[:, 0:1] * d_ref[0:1, :]
        for b in range(1, N_DEV):
            acc = acc + c_ref[:, b:b + 1] * d_ref[b:b + 1, :]
        o_ref[...] = acc

    return pl.pallas_call(
        kern, name="ada_dw", grid=(N // tn,),
        in_specs=[pl.BlockSpec((D_MODEL, N_DEV), lambda j: (0, 0)), pl.BlockSpec((N_DEV, tn), lambda j: (0, j))],
        out_specs=pl.BlockSpec((D_MODEL, tn), lambda j: (0, j)),
        out_shape=jax.ShapeDtypeStruct((D_MODEL, N), F32),
        compiler_params=_cparams("parallel"),
    )(c_t, d_ada)


def _here():
    return lax.axis_index("x"), lax.axis_index("y"), lax.axis_index("c")


def _other_chips(x, y):
    return [(1 - x, y), (x, 1 - y), (1 - x, 1 - y)]


_ANY = pl.BlockSpec(memory_space=pl.ANY)


def _allgather8(blocks, name):
    L = len(blocks)

    def body(*refs):
        ins, outs = refs[:L], refs[L:2 * L]
        send_sems, recv_sems, local_sems = refs[2 * L:]
        x, y, c = _here()
        me, sibling = (x, y, c), (x, y, 1 - c)
        chips = _other_chips(x, y)

        def slot(px, py, pc):
            return 4 * px + 2 * py + pc

        def copy(l, k, block, to, src=None):
            dst = outs[l].at[slot(*block)]
            return pltpu.make_async_remote_copy(
                src_ref=dst if src is None else src, dst_ref=dst,
                send_sem=send_sems.at[l, k], recv_sem=recv_sems.at[l, k],
                device_id=to, device_id_type=MESH)

        mine = [pltpu.make_async_copy(ins[l], outs[l].at[slot(*me)], local_sems.at[l]) for l in range(L)]
        for cp in mine:
            cp.start()
        first = []
        for l in range(L):
            first.append(copy(l, 0, me, sibling, src=ins[l]))
            for j, chip in enumerate(chips):
                first.append(copy(l, 1 + j, me, (*chip, c), src=ins[l]))
        for cp in first:
            cp.start()
        passed = []
        for j, chip in enumerate(chips):
            for l in range(L):
                copy(l, 1 + j, (*chip, c), me).wait_recv()
                fwd = copy(l, 4 + j, (*chip, c), sibling)
                fwd.start()
                passed.append(fwd)
        for l in range(L):
            copy(l, 0, sibling, me).wait_recv()
        for j, chip in enumerate(chips):
            for l in range(L):
                copy(l, 4 + j, (*chip, 1 - c), me).wait_recv()
        for cp in first + passed:
            cp.wait_send()
        for cp in mine:
            cp.wait()

    return pl.pallas_call(
        body, name=name,
        in_specs=[_ANY] * L, out_specs=[_ANY] * L,
        out_shape=[jax.ShapeDtypeStruct((N_DEV,) + b.shape, b.dtype) for b in blocks],
        scratch_shapes=[pltpu.SemaphoreType.DMA((L, 7)), pltpu.SemaphoreType.DMA((L, 7)),
                        pltpu.SemaphoreType.DMA((L,))],
    )(*blocks)


def _sibling_swap(arrs, name):
    L = len(arrs)

    def body(*refs):
        ins, outs = refs[:L], refs[L:2 * L]
        send_sems, recv_sems = refs[2 * L:]
        x, y, c = _here()
        cps = [pltpu.make_async_remote_copy(src_ref=ins[l], dst_ref=outs[l], send_sem=send_sems.at[l],
                                            recv_sem=recv_sems.at[l], device_id=(x, y, 1 - c),
                                            device_id_type=MESH) for l in range(L)]
        for cp in cps:
            cp.start()
        for cp in cps:
            cp.wait()

    return pl.pallas_call(
        body, name=name,
        in_specs=[_ANY] * L, out_specs=[_ANY] * L,
        out_shape=[jax.ShapeDtypeStruct(a.shape, a.dtype) for a in arrs],
        scratch_shapes=[pltpu.SemaphoreType.DMA((L,)), pltpu.SemaphoreType.DMA((L,))],
    )(*arrs)


def _sibling_join(halves, name):
    L = len(halves)

    def body(*refs):
        ins, outs = refs[:L], refs[L:2 * L]
        send_sems, recv_sems, local_sems = refs[2 * L:]
        x, y, c = _here()
        local = [pltpu.make_async_copy(ins[l], outs[l].at[c], local_sems.at[l]) for l in range(L)]
        cps = [pltpu.make_async_remote_copy(src_ref=ins[l], dst_ref=outs[l].at[c], send_sem=send_sems.at[l],
                                            recv_sem=recv_sems.at[l], device_id=(x, y, 1 - c),
                                            device_id_type=MESH) for l in range(L)]
        for cp in local + cps:
            cp.start()
        for l in range(L):
            pltpu.make_async_remote_copy(src_ref=ins[l], dst_ref=outs[l].at[1 - c], send_sem=send_sems.at[l],
                                         recv_sem=recv_sems.at[l], device_id=(x, y, 1 - c),
                                         device_id_type=MESH).wait()
        for cp in local:
            cp.wait()

    return pl.pallas_call(
        body, name=name,
        in_specs=[_ANY] * L, out_specs=[_ANY] * L,
        out_shape=[jax.ShapeDtypeStruct((2,) + a.shape, a.dtype) for a in halves],
        scratch_shapes=[pltpu.SemaphoreType.DMA((L,)), pltpu.SemaphoreType.DMA((L,)),
                        pltpu.SemaphoreType.DMA((L,))],
    )(*halves)


def _chip_scatter(arrs, name):
    L = len(arrs)

    def body(*refs):
        ins, outs = refs[:L], refs[L:2 * L]
        send_sems, recv_sems = refs[2 * L:]
        x, y, c = _here()
        cps = []
        for l in range(L):
            for j, (tx, ty) in enumerate(_other_chips(x, y)):
                cps.append(pltpu.make_async_remote_copy(
                    src_ref=ins[l].at[2 * tx + ty], dst_ref=outs[l].at[j],
                    send_sem=send_sems.at[l, j], recv_sem=recv_sems.at[l, j],
                    device_id=(tx, ty, c), device_id_type=MESH))
        for cp in cps:
            cp.start()
        for cp in cps:
            cp.wait()

    return pl.pallas_call(
        body, name=name,
        in_specs=[_ANY] * L, out_specs=[_ANY] * L,
        out_shape=[jax.ShapeDtypeStruct((3,) + a.shape[1:], a.dtype) for a in arrs],
        scratch_shapes=[pltpu.SemaphoreType.DMA((L, 3)), pltpu.SemaphoreType.DMA((L, 3))],
    )(*arrs)


_A_ORDER = np.array(A_HEAD_ORDER)
_A_INVERSE = np.argsort(_A_ORDER)


def _permute_in_weights(w_in, w_ba):
    qa = w_in[:, 0:512].reshape(D_MODEL, A_Q_HEADS, HEAD_DIM)[:, _A_ORDER, :].reshape(D_MODEL, 512)
    f_pad = jnp.pad(w_in[:, 2304:2312], ((0, 0), (0, LANES - B_HEADS)))
    w_a = jnp.concatenate([qa, w_in[:, 512:640], f_pad], axis=1)
    w_b = w_in[:, 640:2304]
    w_g = w_in[:, 2312:4360]
    w_ba_p = w_ba.reshape(A_Q_HEADS, HEAD_DIM, D_MODEL)[_A_ORDER].reshape(512, D_MODEL)
    return w_a, w_b, w_g, w_ba_p


def _unpermute_in_grads(dw_perm, dw_ba_p):
    qa = dw_perm[:, 0:512].reshape(D_MODEL, A_Q_HEADS, HEAD_DIM)[:, _A_INVERSE, :].reshape(D_MODEL, 512)
    dw_in = jnp.concatenate([qa, dw_perm[:, 512:640], dw_perm[:, W_A:W_A + W_B],
                             dw_perm[:, OFF_F:OFF_F + B_HEADS], dw_perm[:, W_A + W_B:]], axis=1)
    dw_ba = dw_ba_p.reshape(A_Q_HEADS, HEAD_DIM, D_MODEL)[_A_INVERSE].reshape(512, D_MODEL)
    return dw_in, dw_ba


def _rope_tables(pos):
    inv_freq = 1.0 / (ROPE_THETA ** (jnp.arange(0, HEAD_DIM, 2, dtype=F32) / HEAD_DIM))
    ang = pos.astype(F32)[:, None] * inv_freq
    cos, sin = jnp.cos(ang), jnp.sin(ang)
    return jnp.tile(cos, (1, 4)), jnp.tile(jnp.concatenate([-sin, sin], axis=1), (1, 2))


def _local_step(x, pos, ada, g1, g2, g3, g4, b_f, sinks, w_in, w_ba, w_bb, w_out, w_fi, w_fo, target):
    S = x.shape[0]
    t_fox = _tile(S, 512, LANES) if S >= 1024 else S // 2
    shift_m, scale_m, gate_m, shift_f, scale_f, gate_f = [ada[i:i + 1] for i in range(N_ADA)]
    cos_t, sin_t = _rope_tables(pos)
    w_a, w_b, w_g, w_ba_p = _permute_in_weights(w_in, w_ba)
    w_perm = jnp.concatenate([w_a, w_b, w_g], axis=1)
    sinks_p = sinks.reshape(A_KV_HEADS, 4).T.reshape(A_Q_HEADS)
    b_f_pad = jnp.pad(b_f, (0, LANES - B_HEADS)).reshape(1, LANES)

    h1 = _pre_norm(x, g1, scale_m, shift_m, "pre_mix_norm")
    p_a = _mm(h1, w_a, "nn", F32, "proj_a")
    p_b = _mm(h1, w_b, "nn", BF, "proj_b")
    p_g = _mm(h1, w_g, "nn", BF, "proj_g")
    (qk_a,) = _rope([p_a], [640], cos_t, sin_t, "rope_fwd")
    o_a, lse_a = _attn_fwd(qk_a, 0, qk_a, 4, p_b, 0, 4, True, WINDOW, WINDOW, None, None, sinks_p, "swa_fwd")
    cum = _fox_gate_fwd(p_a, b_f_pad)[:, :B_HEADS]
    cq_arr = jnp.repeat(cum, HEAD_DIM, axis=1)
    ck_arr = cum.T.reshape(4, 2, S)
    o_b, lse_b = _attn_fwd(p_b, 1, p_b, 5, p_b, 9, 4, False, t_fox, None, cq_arr, ck_arr, None, "fox_fwd")
    pa = _mm(o_a, w_ba_p, "nn", F32, "branch_a")
    pb = _mm(o_b, w_bb, "nn", F32, "branch_b")
    merged = _merge_fwd(p_g, pa, pb)
    y1 = _mm(merged, w_out, "nn", F32, "out_proj")
    x2, h2 = _post_pre(x, y1, g2, gate_m, g3, scale_f, shift_f)
    gu = _mm(h2, w_fi, "nn", BF, "ffn_in")
    act = _swiglu_fwd(gu)
    y2 = _mm(act, w_fo, "nn", F32, "ffn_out")
    d_out, d_y2, st_f = _final(x2, y2, g4, gate_f, target)

    d_act = _mm(d_y2, w_fo, "nt", F32, "ffn_out_dx")
    dw_fo = _mm(act, d_y2, "tn", F32, "ffn_out_dw")
    d_gu = _swiglu_bwd(d_act, gu)
    d_h2 = _mm(d_gu, w_fi, "nt", F32, "ffn_in_dx")
    dw_fi = _mm(h2, d_gu, "tn", F32, "ffn_in_dw")
    d_x2, d_y1, st_m = _mid_bwd(d_h2, x2, d_out, y1, g3, scale_f, g2, gate_m)
    d_merged = _mm(d_y1, w_out, "nt", F32, "out_proj_dx")
    dw_out = _mm(merged, d_y1, "tn", F32, "out_proj_dw")
    d_pa, d_pb, d_ga, d_gb = _merge_bwd(d_merged, p_g, pa, pb)
    d_oa = _mm(d_pa, w_ba_p, "nt", F32, "branch_a_dx")
    dw_ba_p = _mm(o_a, d_pa, "tn", F32, "branch_a_dw")
    d_ob = _mm(d_pb, w_bb, "nt", F32, "branch_b_dx")
    dw_bb = _mm(o_b, d_pb, "tn", F32, "branch_b_dw")
    dq_a, dk_a, dv_a, d_sink = _attn_bwd(qk_a, 0, qk_a, 4, p_b, 0, o_a, d_oa, lse_a, 4, True, WINDOW, WINDOW,
                                         None, None, sinks_p, "swa_bwd")
    dq_b, dk_b, dv_b, d_ck, d_cq = _attn_bwd(p_b, 1, p_b, 5, p_b, 9, o_b, d_ob, lse_b, 4, False, t_fox, None,
                                             cq_arr, ck_arr, None, "fox_bwd")
    d_qa, d_ka = _rope([dq_a, dk_a], [512, LANES], cos_t, -sin_t, "rope_bwd")
    pad8 = lambda a: jnp.pad(a, ((0, 0), (0, LANES - B_HEADS)))
    d_f, d_bf = _fox_gate_bwd(pad8(d_cq[:, ::HEAD_DIM]), pad8(d_ck.reshape(B_HEADS, S).T), p_a, b_f_pad)
    d_proj = jnp.concatenate([d_qa, d_ka, d_f, dv_a.astype(BF), dq_b.astype(BF), dk_b.astype(BF),
                              dv_b.astype(BF), d_ga, d_gb], axis=1)
    d_h1 = _mm(d_proj, w_perm, "nt", F32, "proj_dx")
    dw_perm = _mm(h1, d_proj, "tn", F32, "proj_dw")
    grad_x, st_p = _pre_bwd(d_h1, x, d_x2, g1, scale_m)

    dw_in, dw_ba = _unpermute_in_grads(dw_perm, dw_ba_p)
    d_sinks = d_sink[:, :2, 0].T.reshape(A_Q_HEADS)
    small = jnp.concatenate([
        st_p[0], st_p[1], st_m[3], st_m[0], st_m[1], st_f[0],
        st_p[2], st_m[4], st_m[2], st_f[1],
        st_f[2], d_bf[0, :B_HEADS], d_sinks,
        jnp.zeros((SM_LEN - SM_SINK - A_Q_HEADS,), F32)])
    return grad_x, (dw_in, dw_ba, dw_bb, dw_out, dw_fi, dw_fo), small


def kernel(x, c, positions, w_ada, b_ada, g_pre_mix, g_post_mix, w_in, b_f, sinks, w_branch_a, w_branch_b, w_out, g_pre_ffn, g_post_ffn, w_ffn_in, w_ffn_out, loss_target, m_w_ada, m_b_ada, m_g_pre_mix, m_g_post_mix, m_w_in, m_b_f, m_sinks, m_w_branch_a, m_w_branch_b, m_w_out, m_g_pre_ffn, m_g_post_ffn, m_w_ffn_in, m_w_ffn_out, v_w_ada, v_b_ada, v_g_pre_mix, v_g_post_mix, v_w_in, v_b_f, v_sinks, v_w_branch_a, v_w_branch_b, v_w_out, v_g_pre_ffn, v_g_post_ffn, v_w_ffn_in, v_w_ffn_out):
    xi, yi, ci = _here()
    chip = 2 * xi + yi
    dev = 2 * chip + ci

    def my_half(a):
        rows = a.shape[0] // 2
        return lax.dynamic_slice_in_dim(a, ci * rows, rows, axis=0)

    shards = [w_in[0], w_branch_a[0], w_branch_b[0], w_out[0], w_ffn_in[0], w_ffn_out[0]]
    gathered = _allgather8([c.reshape(8, LANES)] + [my_half(w).astype(BF) for w in shards], "gather_weights")
    c_all = gathered[0].reshape(N_DEV, D_MODEL)
    col_sharded = lambda g: jnp.transpose(g.reshape(N_CHIP, -1, g.shape[-1]), (1, 0, 2)).reshape(
        2 * g.shape[1], N_CHIP * g.shape[-1])
    row_sharded = lambda g: g.reshape(N_DEV * g.shape[1], g.shape[-1])
    w_in_f, w_ba_f, w_bb_f = col_sharded(gathered[1]), col_sharded(gathered[2]), col_sharded(gathered[3])
    w_out_f, w_fi_f, w_fo_f = row_sharded(gathered[4]), col_sharded(gathered[5]), row_sharded(gathered[6])

    ada_cols = _mm(c_all, w_ada[0], "nn", F32, "ada_fwd")
    (ada_g,) = _allgather8([ada_cols], "gather_ada")
    ada_mine = lax.dynamic_index_in_dim(ada_g.reshape(N_CHIP, 2, N_DEV, -1)[:, 0], dev, axis=1, keepdims=False)
    ada = (ada_mine.reshape(-1) + b_ada[0]).reshape(N_ADA, D_MODEL)

    grad_x, dws, small = _local_step(
        x[0], positions[0], ada, g_pre_mix, g_post_mix, g_pre_ffn, g_post_ffn, b_f[0], sinks[0],
        w_in_f, w_ba_f, w_bb_f, w_out_f, w_fi_f, w_fo_f, loss_target[0])

    (small_g,) = _allgather8([small.reshape(8, SM_LEN // 8)], "gather_small")
    small_all = small_g.reshape(N_DEV, SM_LEN)
    small_tot, loss_row = _small_finalize(small_all)
    loss = loss_row[0, 0]
    d_ada_cols = lax.dynamic_slice_in_dim(small_all[:, :N_ADA * D_MODEL], chip * (N_ADA * D_MODEL // N_CHIP),
                                          N_ADA * D_MODEL // N_CHIP, axis=1)
    g_w_ada = _ada_dw(c_all.T, d_ada_cols)

    def pieces(dw, by_cols):
        if by_cols:
            R, C = dw.shape[0], dw.shape[1] // N_CHIP
            return jnp.transpose(dw.reshape(R, N_CHIP, C), (1, 0, 2))
        return dw.reshape(N_CHIP, dw.shape[0] // N_CHIP, dw.shape[1])

    by_cols = [True, True, True, False, True, False]
    keep, send = [], []
    for dw, bc in zip(dws, by_cols):
        p = pieces(dw, bc)
        rows = p.shape[1] // 2
        keep.append(lax.dynamic_slice_in_dim(p, ci * rows, rows, axis=1))
        send.append(lax.dynamic_slice_in_dim(p, (1 - ci) * rows, rows, axis=1).astype(BF))
    got = _sibling_swap(send, "grads_to_sibling")
    part_f32, part_bf = [], []
    for l, (k_, g_) in enumerate(zip(keep, got)):
        shp = k_.shape
        s32, sbf = _add_pair(k_.reshape(-1, shp[-1]), g_.reshape(-1, shp[-1]), f"chip_sum_{l}")
        part_f32.append(s32.reshape(shp))
        part_bf.append(sbf.reshape(shp))
    recv = _chip_scatter(part_bf, "grads_to_chips")
    halves = [_add_three(lax.dynamic_index_in_dim(p32, chip, axis=0, keepdims=False), r, f"shard_sum_{l}")
              for l, (p32, r) in enumerate(zip(part_f32, recv))]
    joined = _sibling_join(halves, "grads_join")
    g_big = [j.reshape(2 * j.shape[1], j.shape[2]) for j in joined]
    g_w_in, g_w_ba, g_w_bb, g_w_out, g_w_fi, g_w_fo = g_big

    def small_vec(b_ada_, g1_, g2_, g3_, g4_, b_f_, sinks_):
        return jnp.concatenate([b_ada_[0], g1_[0], g2_[0], g3_[0], g4_[0], jnp.zeros((D_MODEL,), F32),
                                b_f_[0], sinks_[0], jnp.zeros((SM_LEN - SM_SINK - A_Q_HEADS,), F32)]
                               ).reshape(8, SM_LEN // 8)

    sw = small_vec(b_ada, g_pre_mix, g_post_mix, g_pre_ffn, g_post_ffn, b_f, sinks)
    sm = small_vec(m_b_ada, m_g_pre_mix, m_g_post_mix, m_g_pre_ffn, m_g_post_ffn, m_b_f, m_sinks)
    sv = small_vec(v_b_ada, v_g_pre_mix, v_g_post_mix, v_g_pre_ffn, v_g_post_ffn, v_b_f, v_sinks)
    s_upd = [u.reshape(SM_LEN) for u in _adamw(sw, small_tot.reshape(8, SM_LEN // 8), sm, sv, "adamw_small")]
    s_grad = small_tot.reshape(SM_LEN)

    def unpack(vec):
        row = lambda a, n: vec[a:a + n].reshape(1, n)
        return dict(b_ada=row(SM_ADA, N_ADA * D_MODEL), g_pre_mix=row(SM_G, D_MODEL),
                    g_post_mix=row(SM_G + D_MODEL, D_MODEL), g_pre_ffn=row(SM_G + 2 * D_MODEL, D_MODEL),
                    g_post_ffn=row(SM_G + 3 * D_MODEL, D_MODEL), b_f=row(SM_BF, B_HEADS),
                    sinks=row(SM_SINK, A_Q_HEADS))

    big = dict(
        w_ada=(w_ada, g_w_ada, m_w_ada, v_w_ada), w_in=(w_in, g_w_in, m_w_in, v_w_in),
        w_branch_a=(w_branch_a, g_w_ba, m_w_branch_a, v_w_branch_a),
        w_branch_b=(w_branch_b, g_w_bb, m_w_branch_b, v_w_branch_b),
        w_out=(w_out, g_w_out, m_w_out, v_w_out), w_ffn_in=(w_ffn_in, g_w_fi, m_w_ffn_in, v_w_ffn_in),
        w_ffn_out=(w_ffn_out, g_w_fo, m_w_ffn_out, v_w_ffn_out))
    grads, deltas, new_m, new_v = unpack(s_grad), unpack(s_upd[0]), unpack(s_upd[1]), unpack(s_upd[2])
    for n, (w_, g_, m_, v_) in big.items():
        d_, nm_, nv_ = _adamw(w_[0], g_, m_[0], v_[0], "adamw_" + n)
        grads[n], deltas[n], new_m[n], new_v[n] = g_[None], d_[None], nm_[None], nv_[None]

    names = ["w_ada", "b_ada", "g_pre_mix", "g_post_mix", "w_in", "b_f", "sinks", "w_branch_a", "w_branch_b",
             "w_out", "g_pre_ffn", "g_post_ffn", "w_ffn_in", "w_ffn_out"]
    return (loss, grad_x[None], *[grads[n] for n in names], *[deltas[n] for n in names],
            *[new_m[n] for n in names], *[new_v[n] for n in names])
```

```python
import functools
import math

import numpy as np
import jax
import jax.numpy as jnp
from jax import lax
from jax.experimental import pallas as pl
from jax.experimental.pallas import tpu as pltpu

F32 = jnp.float32
BF = jnp.bfloat16

D_MODEL = 1024
HEAD_DIM = 64
LANES = 128
WINDOW = 128
A_Q_HEADS = 8
A_KV_HEADS = 2
B_HEADS = 8
D_FF = 2816
ROPE_THETA = 10000.0
RMS_EPS = 1e-6
N_ADA = 6
N_DEV = 8
N_CHIP = 4

ADAM_LR = 0.001
ADAM_B1 = 0.9
ADAM_B2 = 0.999
ADAM_EPS = 1e-08
ADAM_WD = 0.01
ADAM_STEP = 10

VMEM_LIMIT = 48 * 1024 * 1024
MESH = pl.DeviceIdType.MESH

A_HEAD_ORDER = (0, 4, 1, 5, 2, 6, 3, 7)

OFF_QA, OFF_KA, OFF_F = 0, 512, 640
W_A = 768
OFF_VA, OFF_QB, OFF_KB, OFF_VB = 0, 128, 640, 1152
W_B = 1664
W_G = 2048
W_PERM = W_A + W_B + W_G


def _tile(n, cap, mult=LANES):
    if n <= cap:
        return n
    t = (cap // mult) * mult
    while t >= mult:
        if n % t == 0:
            return t
        t -= mult
    raise ValueError(f"no tile for {n}")


def _cparams(*sem):
    return pltpu.CompilerParams(dimension_semantics=sem, vmem_limit_bytes=VMEM_LIMIT)


def _mm(a, b, mode, out_dtype, name, tm_cap=512, tn_cap=1664, tk_cap=1408):
    if mode == "nn":
        (M, K), (K2, N) = a.shape, b.shape
        dims = (((1,), (0,)), ((), ()))
    elif mode == "nt":
        (M, K), (N, K2) = a.shape, b.shape
        dims = (((1,), (1,)), ((), ()))
    else:
        (K, M), (K2, N) = a.shape, b.shape
        dims = (((0,), (0,)), ((), ()))
    assert K == K2, (a.shape, b.shape, mode)
    tm, tn, tk = _tile(M, tm_cap), _tile(N, tn_cap), _tile(K, tk_cap)
    nk = K // tk
    if mode == "nn":
        a_spec = pl.BlockSpec((tm, tk), lambda i, j, k: (i, k))
        b_spec = pl.BlockSpec((tk, tn), lambda i, j, k: (k, j))
    elif mode == "nt":
        a_spec = pl.BlockSpec((tm, tk), lambda i, j, k: (i, k))
        b_spec = pl.BlockSpec((tn, tk), lambda i, j, k: (j, k))
    else:
        a_spec = pl.BlockSpec((tk, tm), lambda i, j, k: (k, i))
        b_spec = pl.BlockSpec((tk, tn), lambda i, j, k: (k, j))

    def kern(a_ref, b_ref, o_ref, acc_ref):
        k = pl.program_id(2)
        part = lax.dot_general(a_ref[...].astype(BF), b_ref[...].astype(BF), dims,
                               preferred_element_type=F32)

        @pl.when(k == 0)
        def _():
            acc_ref[...] = part

        @pl.when(k > 0)
        def _():
            acc_ref[...] += part

        @pl.when(k == nk - 1)
        def _():
            o_ref[...] = acc_ref[...].astype(o_ref.dtype)

    return pl.pallas_call(
        kern, name=name,
        grid=(M // tm, N // tn, nk),
        in_specs=[a_spec, b_spec],
        out_specs=pl.BlockSpec((tm, tn), lambda i, j, k: (i, j)),
        out_shape=jax.ShapeDtypeStruct((M, N), out_dtype),
        scratch_shapes=[pltpu.VMEM((tm, tn), F32)],
        compiler_params=_cparams("parallel", "parallel", "arbitrary"),
    )(a, b)


ROWS = 256


def _row_spec(tm, width=D_MODEL, col=0):
    return pl.BlockSpec((tm, width), lambda i: (i, col))


def _vec_spec(width=D_MODEL):
    return pl.BlockSpec((1, width), lambda i: (0, 0))


def _rms(x):
    return lax.rsqrt(jnp.mean(x * x, axis=-1, keepdims=True) + RMS_EPS)


def _colsum(x):
    return jnp.sum(x, axis=0, keepdims=True)


def _norm_bwd(d_xn, xn, r):
    return r * (d_xn - xn * jnp.mean(d_xn * xn, axis=-1, keepdims=True))


def _pre_norm(x, g, scale, shift, name):
    S = x.shape[0]
    tm = _tile(S, ROWS, 8)

    def kern(x_ref, g_ref, sc_ref, sh_ref, h_ref):
        xf = x_ref[...]
        y = xf * _rms(xf) * g_ref[...]
        h_ref[...] = (y * (1.0 + sc_ref[...]) + sh_ref[...]).astype(BF)

    return pl.pallas_call(
        kern, name=name, grid=(S // tm,),
        in_specs=[_row_spec(tm), _vec_spec(), _vec_spec(), _vec_spec()],
        out_specs=_row_spec(tm),
        out_shape=jax.ShapeDtypeStruct((S, D_MODEL), BF),
        compiler_params=_cparams("parallel"),
    )(x, g, scale, shift)


def _post_pre(x, y1, g2, gate_m, g3, scale_f, shift_f):
    S = x.shape[0]
    tm = _tile(S, ROWS, 8)

    def kern(x_ref, y_ref, g2_ref, gm_ref, g3_ref, sc_ref, sh_ref, x2_ref, h2_ref):
        y = y_ref[...]
        n2 = y * _rms(y) * g2_ref[...]
        x2 = x_ref[...] + gm_ref[...] * n2
        x2_ref[...] = x2
        n3 = x2 * _rms(x2) * g3_ref[...]
        h2_ref[...] = (n3 * (1.0 + sc_ref[...]) + sh_ref[...]).astype(BF)

    return pl.pallas_call(
        kern, name="post_mix_pre_ffn", grid=(S // tm,),
        in_specs=[_row_spec(tm), _row_spec(tm)] + [_vec_spec()] * 5,
        out_specs=[_row_spec(tm), _row_spec(tm)],
        out_shape=[jax.ShapeDtypeStruct((S, D_MODEL), F32), jax.ShapeDtypeStruct((S, D_MODEL), BF)],
        compiler_params=_cparams("parallel"),
    )(x, y1, g2, gate_m, g3, scale_f, shift_f)


def _stats_spec():
    return pl.BlockSpec((8, D_MODEL), lambda i: (0, 0))


def _final(x2, y2, g4, gate_f, target):
    S = x2.shape[0]
    tm = _tile(S, ROWS, 8)

    def kern(x2_ref, y_ref, g4_ref, gf_ref, t_ref, dout_ref, dy_ref, st_ref):
        @pl.when(pl.program_id(0) == 0)
        def _():
            st_ref[...] = jnp.zeros_like(st_ref)

        y = y_ref[...]
        r = _rms(y)
        yn = y * r
        n4 = yn * g4_ref[...]
        diff = x2_ref[...] + gf_ref[...] * n4 - t_ref[...]
        d_out = diff / D_MODEL
        dout_ref[...] = d_out
        dn = d_out * gf_ref[...]
        dy_ref[...] = _norm_bwd(dn * g4_ref[...], yn, r).astype(BF)
        st_ref[0:1, :] += _colsum(d_out * n4)
        st_ref[1:2, :] += _colsum(dn * yn)
        st_ref[2:3, :] += _colsum(diff * diff)

    return pl.pallas_call(
        kern, name="final_loss", grid=(S // tm,),
        in_specs=[_row_spec(tm), _row_spec(tm), _vec_spec(), _vec_spec(), _row_spec(tm)],
        out_specs=[_row_spec(tm), _row_spec(tm), _stats_spec()],
        out_shape=[jax.ShapeDtypeStruct((S, D_MODEL), F32), jax.ShapeDtypeStruct((S, D_MODEL), BF),
                   jax.ShapeDtypeStruct((8, D_MODEL), F32)],
        compiler_params=_cparams("arbitrary"),
    )(x2, y2, g4, gate_f, target)


def _mid_bwd(d_h2, x2, d_out, y1, g3, scale_f, g2, gate_m):
    S = x2.shape[0]
    tm = _tile(S, ROWS, 8)

    def kern(dh_ref, x2_ref, dout_ref, y_ref, g3_ref, sc_ref, g2_ref, gm_ref, dx2_ref, dy_ref, st_ref):
        @pl.when(pl.program_id(0) == 0)
        def _():
            st_ref[...] = jnp.zeros_like(st_ref)

        dh = dh_ref[...]
        x2 = x2_ref[...]
        r3 = _rms(x2)
        xn = x2 * r3
        one_sc = 1.0 + sc_ref[...]
        d_x2 = dout_ref[...] + _norm_bwd(dh * one_sc * g3_ref[...], xn, r3)
        dx2_ref[...] = d_x2
        y = y_ref[...]
        r2 = _rms(y)
        yn = y * r2
        dn = d_x2 * gm_ref[...]
        dy_ref[...] = _norm_bwd(dn * g2_ref[...], yn, r2).astype(BF)
        st_ref[0:1, :] += _colsum(dh)
        st_ref[1:2, :] += _colsum(dh * (xn * g3_ref[...]))
        st_ref[2:3, :] += _colsum(dh * one_sc * xn)
        st_ref[3:4, :] += _colsum(d_x2 * (yn * g2_ref[...]))
        st_ref[4:5, :] += _colsum(dn * yn)

    return pl.pallas_call(
        kern, name="mid_bwd", grid=(S // tm,),
        in_specs=[_row_spec(tm)] * 4 + [_vec_spec()] * 4,
        out_specs=[_row_spec(tm), _row_spec(tm), _stats_spec()],
        out_shape=[jax.ShapeDtypeStruct((S, D_MODEL), F32), jax.ShapeDtypeStruct((S, D_MODEL), BF),
                   jax.ShapeDtypeStruct((8, D_MODEL), F32)],
        compiler_params=_cparams("arbitrary"),
    )(d_h2, x2, d_out, y1, g3, scale_f, g2, gate_m)


def _pre_bwd(d_h1, x, d_x2, g1, scale_m):
    S = x.shape[0]
    tm = _tile(S, ROWS, 8)

    def kern(dh_ref, x_ref, dx2_ref, g_ref, sc_ref, gx_ref, st_ref):
        @pl.when(pl.program_id(0) == 0)
        def _():
            st_ref[...] = jnp.zeros_like(st_ref)

        dh = dh_ref[...]
        xf = x_ref[...]
        r = _rms(xf)
        xn = xf * r
        one_sc = 1.0 + sc_ref[...]
        gx_ref[...] = dx2_ref[...] + _norm_bwd(dh * one_sc * g_ref[...], xn, r)
        st_ref[0:1, :] += _colsum(dh)
        st_ref[1:2, :] += _colsum(dh * (xn * g_ref[...]))
        st_ref[2:3, :] += _colsum(dh * one_sc * xn)

    return pl.pallas_call(
        kern, name="pre_mix_bwd", grid=(S // tm,),
        in_specs=[_row_spec(tm)] * 3 + [_vec_spec()] * 2,
        out_specs=[_row_spec(tm), _stats_spec()],
        out_shape=[jax.ShapeDtypeStruct((S, D_MODEL), F32), jax.ShapeDtypeStruct((8, D_MODEL), F32)],
        compiler_params=_cparams("arbitrary"),
    )(d_h1, x, d_x2, g1, scale_m)


def _rope(xs, widths, cos_t, sin_t, name):
    S = xs[0].shape[0]
    tm = _tile(S, 512, 8)
    n = len(xs)

    def kern(*refs):
        cos = refs[n][...]
        sin = refs[n + 1][...]
        first = (lax.broadcasted_iota(jnp.int32, cos.shape, 1) % HEAD_DIM) < HEAD_DIM // 2
        for x_ref, o_ref, w in zip(refs[:n], refs[n + 2:], widths):
            for c0 in range(0, w, LANES):
                v = x_ref[:, c0:c0 + LANES]
                partner = jnp.where(first, pltpu.roll(v, LANES - HEAD_DIM // 2, 1),
                                    pltpu.roll(v, HEAD_DIM // 2, 1))
                o_ref[:, c0:c0 + LANES] = (v * cos + partner * sin).astype(BF)

    return pl.pallas_call(
        kern, name=name, grid=(S // tm,),
        in_specs=[_row_spec(tm, w) for w in widths] + [_row_spec(tm, LANES)] * 2,
        out_specs=[_row_spec(tm, w) for w in widths],
        out_shape=[jax.ShapeDtypeStruct((S, w), BF) for w in widths],
        compiler_params=_cparams("parallel"),
    )(*xs, cos_t, sin_t)


def _merge_fwd(pg, pa, pb):
    S = pa.shape[0]
    tm = _tile(S, ROWS, 8)

    def kern(ga_ref, gb_ref, pa_ref, pb_ref, o_ref):
        ga = jax.nn.sigmoid(ga_ref[...].astype(F32))
        gb = jax.nn.sigmoid(gb_ref[...].astype(F32))
        o_ref[...] = (ga * pa_ref[...] + gb * pb_ref[...]).astype(BF)

    return pl.pallas_call(
        kern, name="merge_fwd", grid=(S // tm,),
        in_specs=[_row_spec(tm, col=0), _row_spec(tm, col=1), _row_spec(tm), _row_spec(tm)],
        out_specs=_row_spec(tm),
        out_shape=jax.ShapeDtypeStruct((S, D_MODEL), BF),
        compiler_params=_cparams("parallel"),
    )(pg, pg, pa, pb)


def _merge_bwd(d_merged, pg, pa, pb):
    S = pa.shape[0]
    tm = _tile(S, ROWS, 8)

    def kern(dm_ref, ga_ref, gb_ref, pa_ref, pb_ref, dpa_ref, dpb_ref, dga_ref, dgb_ref):
        dm = dm_ref[...]
        ga = jax.nn.sigmoid(ga_ref[...].astype(F32))
        gb = jax.nn.sigmoid(gb_ref[...].astype(F32))
        dpa_ref[...] = (dm * ga).astype(BF)
        dpb_ref[...] = (dm * gb).astype(BF)
        dga_ref[...] = (dm * pa_ref[...] * ga * (1.0 - ga)).astype(BF)
        dgb_ref[...] = (dm * pb_ref[...] * gb * (1.0 - gb)).astype(BF)

    bf_out = jax.ShapeDtypeStruct((S, D_MODEL), BF)
    return pl.pallas_call(
        kern, name="merge_bwd", grid=(S // tm,),
        in_specs=[_row_spec(tm), _row_spec(tm, col=0), _row_spec(tm, col=1), _row_spec(tm), _row_spec(tm)],
        out_specs=[_row_spec(tm)] * 4,
        out_shape=[bf_out] * 4,
        compiler_params=_cparams("parallel"),
    )(d_merged, pg, pg, pa, pb)


def _swiglu_fwd(gu):
    S = gu.shape[0]
    tm = _tile(S, ROWS, 8)
    tc = _tile(D_FF, 1408)
    nc = D_FF // tc

    def kern(g_ref, u_ref, o_ref):
        g = g_ref[...].astype(F32)
        o_ref[...] = (g * jax.nn.sigmoid(g) * u_ref[...].astype(F32)).astype(BF)

    return pl.pallas_call(
        kern, name="swiglu_fwd", grid=(S // tm, nc),
        in_specs=[pl.BlockSpec((tm, tc), lambda i, j: (i, j)),
                  pl.BlockSpec((tm, tc), lambda i, j: (i, j + nc))],
        out_specs=pl.BlockSpec((tm, tc), lambda i, j: (i, j)),
        out_shape=jax.ShapeDtypeStruct((S, D_FF), BF),
        compiler_params=_cparams("parallel", "parallel"),
    )(gu, gu)


def _swiglu_bwd(d_act, gu):
    S = gu.shape[0]
    tm = _tile(S, 128, 8)

    def kern(da_ref, g_ref, u_ref, o_ref):
        g = g_ref[...].astype(F32)
        u = u_ref[...].astype(F32)
        da = da_ref[...]
        sg = jax.nn.sigmoid(g)
        o_ref[:, :D_FF] = (da * u * (sg * (1.0 + g * (1.0 - sg)))).astype(BF)
        o_ref[:, D_FF:] = (da * (g * sg)).astype(BF)

    return pl.pallas_call(
        kern, name="swiglu_bwd", grid=(S // tm,),
        in_specs=[_row_spec(tm, D_FF), _row_spec(tm, D_FF, 0), _row_spec(tm, D_FF, 1)],
        out_specs=_row_spec(tm, 2 * D_FF),
        out_shape=jax.ShapeDtypeStruct((S, 2 * D_FF), BF),
        compiler_params=_cparams("parallel"),
    )(d_act, gu, gu)


def _split3(x):
    hi = x.astype(BF)
    r1 = x - hi.astype(F32)
    mid = r1.astype(BF)
    lo = (r1 - mid.astype(F32)).astype(BF)
    return hi, mid, lo


def _tri_dot(tri, x):
    return sum(jnp.dot(tri, part, preferred_element_type=F32) for part in _split3(x))


def _log_sigmoid(z):
    return jnp.minimum(z, 0.0) - jnp.log(1.0 + jnp.exp(-jnp.abs(z)))


def _fox_gate_fwd(pa, b_f_pad):
    S = pa.shape[0]
    T = _tile(S, 512, 8)
    f_col = OFF_F // LANES

    def kern(z_ref, b_ref, cum_ref, carry_ref):
        @pl.when(pl.program_id(0) == 0)
        def _():
            carry_ref[...] = jnp.zeros_like(carry_ref)

        log_f = _log_sigmoid(z_ref[...] + b_ref[...])
        row = lax.broadcasted_iota(jnp.int32, (T, T), 0)
        col = lax.broadcasted_iota(jnp.int32, (T, T), 1)
        tri = (col <= row).astype(BF)
        cum = _tri_dot(tri, log_f) + carry_ref[...]
        cum_ref[...] = cum
        carry_ref[...] = cum[T - 1:T, :]

    return pl.pallas_call(
        kern, name="fox_gate_fwd", grid=(S // T,),
        in_specs=[_row_spec(T, LANES, f_col), _vec_spec(LANES)],
        out_specs=_row_spec(T, LANES),
        out_shape=jax.ShapeDtypeStruct((S, LANES), F32),
        scratch_shapes=[pltpu.VMEM((1, LANES), F32)],
        compiler_params=_cparams("arbitrary"),
    )(pa, b_f_pad)


def _fox_gate_bwd(rowsum_ds, colsum_ds, pa, b_f_pad):
    S = pa.shape[0]
    T = _tile(S, 512, 8)
    nb = S // T
    f_col = OFF_F // LANES

    def kern(dr_ref, dc_ref, z_ref, b_ref, df_ref, dbf_ref, carry_ref):
        @pl.when(pl.program_id(0) == 0)
        def _():
            carry_ref[...] = jnp.zeros_like(carry_ref)
            dbf_ref[...] = jnp.zeros_like(dbf_ref)

        row = lax.broadcasted_iota(jnp.int32, (T, T), 0)
        col = lax.broadcasted_iota(jnp.int32, (T, T), 1)
        tri = (col >= row).astype(BF)
        rev = _tri_dot(tri, dr_ref[...] - dc_ref[...]) + carry_ref[...]
        carry_ref[...] = rev[0:1, :]
        z = z_ref[...] + b_ref[...]
        lane = lax.broadcasted_iota(jnp.int32, (T, LANES), 1)
        d_z = jnp.where(lane < B_HEADS, rev * jax.nn.sigmoid(-z), 0.0)
        df_ref[...] = d_z.astype(BF)
        dbf_ref[0:1, :] += _colsum(d_z)

    return pl.pallas_call(
        kern, name="fox_gate_bwd", grid=(nb,),
        in_specs=[pl.BlockSpec((T, LANES), lambda i: (nb - 1 - i, 0)),
                  pl.BlockSpec((T, LANES), lambda i: (nb - 1 - i, 0)),
                  pl.BlockSpec((T, LANES), lambda i: (nb - 1 - i, f_col)),
                  _vec_spec(LANES)],
        out_specs=[pl.BlockSpec((T, LANES), lambda i: (nb - 1 - i, 0)),
                   pl.BlockSpec((8, LANES), lambda i: (0, 0))],
        out_shape=[jax.ShapeDtypeStruct((S, LANES), BF), jax.ShapeDtypeStruct((8, LANES), F32)],
        scratch_shapes=[pltpu.VMEM((1, LANES), F32)],
        compiler_params=_cparams("arbitrary"),
    )(rowsum_ds, colsum_ds, pa, b_f_pad)


NEG_INF = float("-inf")
QK_SCALE = 1.0 / math.sqrt(HEAD_DIM)


def _half_mask(shape, half):
    lane = lax.broadcasted_iota(jnp.int32, shape, 1)
    return (lane < HEAD_DIM) if half == 0 else (lane >= HEAD_DIM)


def _valid(i, j, T, rowcol, window):
    rel = (i - j) * T + rowcol
    ok = rel >= 0
    if window is not None:
        ok = ok & (rel < window)
    return ok


def _attn_fwd(q_arr, q_col, k_arr, k_col, v_arr, v_col, n_pairs, kv_shared, T, window,
              cq_arr, ck_arr, sinks, name):
    S = q_arr.shape[0]
    nq = S // T
    use_bias = cq_arr is not None
    use_sink = sinks is not None
    back = 0 if window is None else -(-window // T)

    def kern(*refs):
        refs = list(refs)
        q_ref, k_ref, v_ref = refs[:3]
        pos = 3
        if use_bias:
            cq_ref, ck_ref = refs[pos:pos + 2]
            pos += 2
        if use_sink:
            sink_ref = refs[pos]
            pos += 1
        o_ref, lse_ref = refs[pos:pos + 2]
        p_id = pl.program_id(0)
        i = pl.program_id(1)
        q = q_ref[...]
        rowcol = lax.broadcasted_iota(jnp.int32, (T, T), 0) - lax.broadcasted_iota(jnp.int32, (T, T), 1)
        lo = jnp.maximum(i - back, 0) if window is not None else 0
        outs, lses = [], []
        for half in (0, 1):
            hm = _half_mask((T, LANES), half)
            qh = (jnp.where(hm, q, 0).astype(F32) * QK_SCALE).astype(BF)
            if use_bias:
                cq = cq_ref[:, half * HEAD_DIM:half * HEAD_DIM + 1]
            if use_sink:
                m0 = jnp.full((T, 1), sink_ref[2 * p_id + half], F32)
                l0 = jnp.ones((T, 1), F32)
            else:
                m0 = jnp.full((T, 1), NEG_INF, F32)
                l0 = jnp.zeros((T, 1), F32)

            def step(j, carry, masked):
                m, l, acc = carry
                rows = pl.ds(pl.multiple_of(j * T, T), T)
                kj = k_ref[rows, :].astype(BF)
                vj = v_ref[rows, :].astype(BF)
                s = lax.dot_general(qh, kj, (((1,), (1,)), ((), ())), preferred_element_type=F32)
                if use_bias:
                    s = s + cq - ck_ref[0, half:half + 1, rows]
                if masked:
                    s = jnp.where(_valid(i, j, T, rowcol, window), s, NEG_INF)
                m_new = jnp.maximum(m, jnp.max(s, axis=1, keepdims=True))
                alpha = jnp.exp(m - m_new)
                p = jnp.exp(s - m_new)
                l_new = alpha * l + jnp.sum(p, axis=1, keepdims=True)
                acc_new = alpha * acc + jnp.dot(p.astype(BF), vj, preferred_element_type=F32)
                return m_new, l_new, acc_new

            init = (m0, l0, jnp.zeros((T, LANES), F32))
            if window is None:
                init = lax.fori_loop(0, i, functools.partial(step, masked=False), init)
                m, l, acc = step(i, init, True)
            else:
                m, l, acc = lax.fori_loop(lo, i + 1, functools.partial(step, masked=True), init)
            outs.append(acc / l)
            lses.append(m + jnp.log(l))
        hm0 = _half_mask((T, LANES), 0)
        o_ref[...] = jnp.where(hm0, outs[0], outs[1])
        lse_ref[...] = jnp.where(hm0, lses[0], lses[1])

    kv_idx = (lambda c0: (lambda p, i: (0, c0))) if kv_shared else (lambda c0: (lambda p, i: (0, c0 + p)))
    in_specs = [pl.BlockSpec((T, LANES), lambda p, i: (i, q_col + p)),
                pl.BlockSpec((S, LANES), kv_idx(k_col)),
                pl.BlockSpec((S, LANES), kv_idx(v_col))]
    args = [q_arr, k_arr, v_arr]
    if use_bias:
        in_specs += [pl.BlockSpec((T, LANES), lambda p, i: (i, p)),
                     pl.BlockSpec((1, 2, S), lambda p, i: (p, 0, 0))]
        args += [cq_arr, ck_arr]
    if use_sink:
        in_specs.append(pl.BlockSpec(memory_space=pltpu.SMEM))
        args.append(sinks)
    out_spec = pl.BlockSpec((T, LANES), lambda p, i: (i, p))
    return pl.pallas_call(
        kern, name=name, grid=(n_pairs, nq),
        in_specs=in_specs, out_specs=[out_spec, out_spec],
        out_shape=[jax.ShapeDtypeStruct((S, n_pairs * LANES), F32)] * 2,
        compiler_params=_cparams("parallel", "arbitrary"),
    )(*args)


def _attn_bwd(q_arr, q_col, k_arr, k_col, v_arr, v_col, o_arr, do_arr, lse_arr, n_pairs, kv_shared, T,
              window, cq_arr, ck_arr, sinks, name):
    S = q_arr.shape[0]
    nq = S // T
    use_bias = cq_arr is not None
    use_sink = sinks is not None
    back = 0 if window is None else -(-window // T)
    kv_w = LANES if kv_shared else n_pairs * LANES

    def kern(*refs):
        refs = list(refs)
        q_ref, k_ref, v_ref, o_ref, do_ref, lse_ref = refs[:6]
        pos = 6
        if use_bias:
            cq_ref, ck_ref = refs[pos:pos + 2]
            pos += 2
        if use_sink:
            sink_ref = refs[pos]
            pos += 1
        dq_ref, dk_ref, dv_ref = refs[pos:pos + 3]
        pos += 3
        if use_bias:
            dck_ref, dcq_ref = refs[pos:pos + 2]
            pos += 2
        if use_sink:
            dsink_ref = refs[pos]
        p_id = pl.program_id(0)
        rowcol = lax.broadcasted_iota(jnp.int32, (T, T), 0) - lax.broadcasted_iota(jnp.int32, (T, T), 1)

        def zero_kv():
            dk_ref[...] = jnp.zeros_like(dk_ref)
            dv_ref[...] = jnp.zeros_like(dv_ref)

        if kv_shared:
            pl.when(p_id == 0)(zero_kv)
        else:
            zero_kv()
        if use_bias:
            dck_ref[...] = jnp.zeros_like(dck_ref)
        if use_sink:
            dsink_ref[...] = jnp.zeros_like(dsink_ref)

        for half in (0, 1):
            hm = _half_mask((T, LANES), half)
            lane0 = half * HEAD_DIM

            def outer(i, carry):
                qrows = pl.ds(pl.multiple_of(i * T, T), T)
                qh = (jnp.where(hm, q_ref[qrows, :], 0).astype(F32) * QK_SCALE).astype(BF)
                do_f = jnp.where(hm, do_ref[qrows, :], 0.0)
                doh = do_f.astype(BF)
                delta = jnp.sum(do_f * o_ref[qrows, :], axis=1, keepdims=True)
                lse = lse_ref[qrows, lane0:lane0 + 1]
                if use_bias:
                    cq = cq_ref[qrows, lane0:lane0 + 1]
                lo = jnp.maximum(i - back, 0) if window is not None else 0

                def inner(j, carry_in, masked):
                    dq, rs = carry_in
                    krows = pl.ds(pl.multiple_of(j * T, T), T)
                    kj = k_ref[krows, :].astype(BF)
                    vj = v_ref[krows, :].astype(BF)
                    s = lax.dot_general(qh, kj, (((1,), (1,)), ((), ())), preferred_element_type=F32)
                    if use_bias:
                        s = s + cq - ck_ref[0, half:half + 1, krows]
                    if masked:
                        s = jnp.where(_valid(i, j, T, rowcol, window), s, NEG_INF)
                    p = jnp.exp(s - lse)
                    dp = lax.dot_general(doh, vj, (((1,), (1,)), ((), ())), preferred_element_type=F32)
                    ds = p * (dp - delta)
                    ds_b = ds.astype(BF)
                    dv_ref[krows, :] += lax.dot_general(p.astype(BF), doh, (((0,), (0,)), ((), ())),
                                                        preferred_element_type=F32)
                    dk_ref[krows, :] += lax.dot_general(ds_b, qh, (((0,), (0,)), ((), ())),
                                                        preferred_element_type=F32)
                    if use_bias:
                        dck_ref[0, half:half + 1, krows] += jnp.sum(ds, axis=0, keepdims=True)
                        rs = rs + jnp.sum(ds, axis=1, keepdims=True)
                    kh = jnp.where(hm, kj, 0)
                    return dq + jnp.dot(ds_b, kh, preferred_element_type=F32), rs

                init = (jnp.zeros((T, LANES), F32), jnp.zeros((T, 1), F32))
                if window is None:
                    init = lax.fori_loop(0, i, functools.partial(inner, masked=False), init)
                    dq, rs = inner(i, init, True)
                else:
                    dq, rs = lax.fori_loop(lo, i + 1, functools.partial(inner, masked=True), init)
                dq = dq * QK_SCALE
                if half == 0:
                    dq_ref[qrows, :] = dq
                else:
                    dq_ref[qrows, :] += dq
                if use_bias:
                    rs_b = jnp.broadcast_to(rs, (T, LANES))
                    dcq_ref[qrows, :] = rs_b if half == 0 else jnp.where(hm, rs_b, dcq_ref[qrows, :])
                if use_sink:
                    p_sink = jnp.exp(sink_ref[2 * p_id + half] - lse)
                    dsink_ref[0, half:half + 1, :] += jnp.broadcast_to(
                        -jnp.sum(p_sink * delta, axis=0, keepdims=True), (1, LANES))
                return carry

            lax.fori_loop(0, nq, outer, 0)

    kv_idx = (lambda c0: (lambda p: (0, c0))) if kv_shared else (lambda c0: (lambda p: (0, c0 + p)))
    pair = lambda c0: pl.BlockSpec((S, LANES), lambda p: (0, c0 + p))
    in_specs = [pair(q_col), pl.BlockSpec((S, LANES), kv_idx(k_col)), pl.BlockSpec((S, LANES), kv_idx(v_col)),
                pair(0), pair(0), pair(0)]
    args = [q_arr, k_arr, v_arr, o_arr, do_arr, lse_arr]
    if use_bias:
        in_specs += [pair(0), pl.BlockSpec((1, 2, S), lambda p: (p, 0, 0))]
        args += [cq_arr, ck_arr]
    if use_sink:
        in_specs.append(pl.BlockSpec(memory_space=pltpu.SMEM))
        args.append(sinks)
    out_specs = [pair(0), pl.BlockSpec((S, LANES), kv_idx(0)), pl.BlockSpec((S, LANES), kv_idx(0))]
    out_shape = [jax.ShapeDtypeStruct((S, n_pairs * LANES), F32),
                 jax.ShapeDtypeStruct((S, kv_w), F32), jax.ShapeDtypeStruct((S, kv_w), F32)]
    if use_bias:
        out_specs += [pl.BlockSpec((1, 2, S), lambda p: (p, 0, 0)), pair(0)]
        out_shape += [jax.ShapeDtypeStruct((n_pairs, 2, S), F32), jax.ShapeDtypeStruct((S, n_pairs * LANES), F32)]
    if use_sink:
        out_specs.append(pl.BlockSpec((1, 8, LANES), lambda p: (p, 0, 0)))
        out_shape.append(jax.ShapeDtypeStruct((n_pairs, 8, LANES), F32))
    return pl.pallas_call(
        kern, name=name, grid=(n_pairs,),
        in_specs=in_specs, out_specs=out_specs, out_shape=out_shape,
        compiler_params=_cparams("arbitrary"),
    )(*args)


SWA_TQ = 256


def _swa_window(i, tq):
    start = pl.multiple_of(jnp.maximum(i * tq - WINDOW, 0), LANES)
    return start, i * tq - start


def _swa_valid(offset, tq):
    rel = offset + lax.broadcasted_iota(jnp.int32, (tq, tq + WINDOW), 0) \
        - lax.broadcasted_iota(jnp.int32, (tq, tq + WINDOW), 1)
    return (rel >= 0) & (rel < WINDOW)


def _swa_fwd(qk, v_arr, v_col, sinks):
    S = qk.shape[0]
    tq = min(SWA_TQ, S - WINDOW)
    win = tq + WINDOW

    def kern(q_ref, k_ref, v_ref, sink_ref, o_ref, lse_ref):
        p_id, i = pl.program_id(0), pl.program_id(1)
        start, offset = _swa_window(i, tq)
        kw = k_ref[pl.ds(start, win), :]
        vw = v_ref[pl.ds(start, win), :].astype(BF)
        valid = _swa_valid(offset, tq)
        q = q_ref[...]
        outs, lses = [], []
        for half in (0, 1):
            hm = _half_mask((tq, LANES), half)
            qh = (jnp.where(hm, q, 0).astype(F32) * QK_SCALE).astype(BF)
            s = lax.dot_general(qh, kw, (((1,), (1,)), ((), ())), preferred_element_type=F32)
            s = jnp.where(valid, s, NEG_INF)
            sink = sink_ref[2 * p_id + half]
            m = jnp.maximum(jnp.max(s, axis=1, keepdims=True), sink)
            p = jnp.exp(s - m)
            denom = jnp.sum(p, axis=1, keepdims=True) + jnp.exp(sink - m)
            outs.append(jnp.dot(p.astype(BF), vw, preferred_element_type=F32) / denom)
            lses.append(m + jnp.log(denom))
        hm0 = _half_mask((tq, LANES), 0)
        o_ref[...] = jnp.where(hm0, outs[0], outs[1])
        lse_ref[...] = jnp.where(hm0, lses[0], lses[1])

    tile = pl.BlockSpec((tq, LANES), lambda p, i: (i, p))
    return pl.pallas_call(
        kern, name="swa_fwd", grid=(A_Q_HEADS // 2, S // tq),
        in_specs=[tile, pl.BlockSpec((S, LANES), lambda p, i: (0, A_Q_HEADS // 2)),
                  pl.BlockSpec((S, LANES), lambda p, i: (0, v_col)),
                  pl.BlockSpec(memory_space=pltpu.SMEM)],
        out_specs=[tile, tile],
        out_shape=[jax.ShapeDtypeStruct((S, A_Q_HEADS * HEAD_DIM), F32)] * 2,
        compiler_params=_cparams("parallel", "arbitrary"),
    )(qk, qk, v_arr, sinks)


def _swa_bwd(qk, v_arr, v_col, o_arr, do_arr, lse_arr, sinks):
    S = qk.shape[0]
    tq = min(SWA_TQ, S - WINDOW)
    win = tq + WINDOW
    n_pairs = A_Q_HEADS // 2

    def kern(q_ref, k_ref, v_ref, o_ref, do_ref, lse_ref, sink_ref, dq_ref, dk_ref, dv_ref, dsink_ref):
        p_id, i = pl.program_id(0), pl.program_id(1)

        @pl.when((p_id == 0) & (i == 0))
        def _():
            dk_ref[...] = jnp.zeros_like(dk_ref)
            dv_ref[...] = jnp.zeros_like(dv_ref)

        @pl.when(i == 0)
        def _():
            dsink_ref[...] = jnp.zeros_like(dsink_ref)

        start, offset = _swa_window(i, tq)
        wrows = pl.ds(start, win)
        kw = k_ref[wrows, :]
        vw = v_ref[wrows, :].astype(BF)
        valid = _swa_valid(offset, tq)
        q, do, o, lse2 = q_ref[...], do_ref[...], o_ref[...], lse_ref[...]
        dq = jnp.zeros((tq, LANES), F32)
        dk = jnp.zeros((win, LANES), F32)
        dv = jnp.zeros((win, LANES), F32)
        for half in (0, 1):
            hm = _half_mask((tq, LANES), half)
            lane0 = half * HEAD_DIM
            qh = (jnp.where(hm, q, 0).astype(F32) * QK_SCALE).astype(BF)
            do_f = jnp.where(hm, do, 0.0)
            doh = do_f.astype(BF)
            delta = jnp.sum(do_f * o, axis=1, keepdims=True)
            lse = lse2[:, lane0:lane0 + 1]
            s = lax.dot_general(qh, kw, (((1,), (1,)), ((), ())), preferred_element_type=F32)
            p = jnp.exp(jnp.where(valid, s, NEG_INF) - lse)
            dp = lax.dot_general(doh, vw, (((1,), (1,)), ((), ())), preferred_element_type=F32)
            ds_b = (p * (dp - delta)).astype(BF)
            dv = dv + lax.dot_general(p.astype(BF), doh, (((0,), (0,)), ((), ())), preferred_element_type=F32)
            dk = dk + lax.dot_general(ds_b, qh, (((0,), (0,)), ((), ())), preferred_element_type=F32)
            kh = jnp.where(_half_mask((win, LANES), half), kw, 0)
            dq = dq + jnp.dot(ds_b, kh, preferred_element_type=F32)
            p_sink = jnp.exp(sink_ref[2 * p_id + half] - lse)
            dsink_ref[0, half:half + 1, :] += jnp.broadcast_to(
                -jnp.sum(p_sink * delta, axis=0, keepdims=True), (1, LANES))
        dq_ref[...] = dq * QK_SCALE
        dk_ref[wrows, :] += dk
        dv_ref[wrows, :] += dv

    tile = pl.BlockSpec((tq, LANES), lambda p, i: (i, p))
    whole = lambda col: pl.BlockSpec((S, LANES), lambda p, i: (0, col))
    return pl.pallas_call(
        kern, name="swa_bwd", grid=(n_pairs, S // tq),
        in_specs=[tile, whole(n_pairs), whole(v_col), tile, tile, tile, pl.BlockSpec(memory_space=pltpu.SMEM)],
        out_specs=[tile, whole(0), whole(0), pl.BlockSpec((1, 8, LANES), lambda p, i: (p, 0, 0))],
        out_shape=[jax.ShapeDtypeStruct((S, A_Q_HEADS * HEAD_DIM), F32),
                   jax.ShapeDtypeStruct((S, LANES), F32), jax.ShapeDtypeStruct((S, LANES), F32),
                   jax.ShapeDtypeStruct((n_pairs, 8, LANES), F32)],
        compiler_params=_cparams("arbitrary", "arbitrary"),
    )(qk, qk, v_arr, o_arr, do_arr, lse_arr, sinks)


def _adamw(w, g, m, v, name):
    R, C = w.shape
    tr = _tile(R, 256, 8)

    def kern(w_ref, g_ref, m_ref, v_ref, d_ref, mo_ref, vo_ref):
        g_ = g_ref[...]
        m_new = ADAM_B1 * m_ref[...] + (1.0 - ADAM_B1) * g_
        v_new = ADAM_B2 * v_ref[...] + (1.0 - ADAM_B2) * (g_ * g_)
        m_hat = m_new / (1.0 - ADAM_B1 ** ADAM_STEP)
        v_hat = v_new / (1.0 - ADAM_B2 ** ADAM_STEP)
        d_ref[...] = -ADAM_LR * (m_hat / (jnp.sqrt(v_hat) + ADAM_EPS) + ADAM_WD * w_ref[...])
        mo_ref[...] = m_new
        vo_ref[...] = v_new

    spec = pl.BlockSpec((tr, C), lambda i: (i, 0))
    shape = jax.ShapeDtypeStruct((R, C), F32)
    return pl.pallas_call(
        kern, name=name, grid=(R // tr,),
        in_specs=[spec] * 4, out_specs=[spec] * 3, out_shape=[shape] * 3,
        compiler_params=_cparams("parallel"),
    )(w, g, m, v)


def _add_pair(a, b, name):
    R, C = a.shape
    tr = _tile(R, 256, 8)

    def kern(a_ref, b_ref, o_ref, ob_ref):
        s = a_ref[...] + b_ref[...].astype(F32)
        o_ref[...] = s
        ob_ref[...] = s.astype(BF)

    spec = pl.BlockSpec((tr, C), lambda i: (i, 0))
    return pl.pallas_call(
        kern, name=name, grid=(R // tr,),
        in_specs=[spec] * 2, out_specs=[spec] * 2,
        out_shape=[jax.ShapeDtypeStruct((R, C), F32), jax.ShapeDtypeStruct((R, C), BF)],
        compiler_params=_cparams("parallel"),
    )(a, b)


def _add_three(own, recv, name):
    R, C = own.shape
    tr = _tile(R, 256, 8)

    def kern(o_ref, r0_ref, r1_ref, r2_ref, out_ref):
        s = ((o_ref[...] + r0_ref[...].astype(F32)) + r1_ref[...].astype(F32)) + r2_ref[...].astype(F32)
        out_ref[0] = s
        out_ref[1] = s

    slab = lambda k: pl.BlockSpec((None, tr, C), lambda i: (k, i, 0))
    return pl.pallas_call(
        kern, name=name, grid=(R // tr,),
        in_specs=[pl.BlockSpec((tr, C), lambda i: (i, 0)), slab(0), slab(1), slab(2)],
        out_specs=pl.BlockSpec((2, tr, C), lambda i: (0, i, 0)),
        out_shape=jax.ShapeDtypeStruct((2, R, C), F32),
        compiler_params=_cparams("parallel"),
    )(own, recv, recv, recv)


SM_ADA, SM_G, SM_LOSS, SM_BF, SM_SINK, SM_LEN = 0, 6144, 10240, 11264, 11272, 12288


def _small_finalize(gathered):
    def kern(g_ref, tot_ref, loss_ref):
        tot = g_ref[0:1, :]
        for b in range(1, N_DEV):
            tot = tot + g_ref[b:b + 1, :]
        tot_ref[...] = tot
        sq = jnp.sum(tot[:, SM_LOSS:SM_LOSS + D_MODEL], axis=1, keepdims=True)
        loss_ref[...] = jnp.broadcast_to(sq * (0.5 / D_MODEL), (1, LANES))

    full = lambda shape: pl.BlockSpec(shape, lambda i: (0, 0))
    return pl.pallas_call(
        kern, name="small_finalize", grid=(1,),
        in_specs=[full((N_DEV, SM_LEN))],
        out_specs=[full((1, SM_LEN)), full((1, LANES))],
        out_shape=[jax.ShapeDtypeStruct((1, SM_LEN), F32), jax.ShapeDtypeStruct((1, LANES), F32)],
        compiler_params=_cparams("arbitrary"),
    )(gathered)


def _ada_dw(c_t, d_ada):
    N = d_ada.shape[1]
    tn = _tile(N, 512)

    def kern(c_ref, d_ref, o_ref):
        acc = c_ref[:, 0:1] * d_ref[0:1, :]
        for b in range(1, N_DEV):
            acc = acc + c_ref[:, b:b + 1] * d_ref[b:b + 1, :]
        o_ref[...] = acc

    return pl.pallas_call(
        kern, name="ada_dw", grid=(N // tn,),
        in_specs=[pl.BlockSpec((D_MODEL, N_DEV), lambda j: (0, 0)), pl.BlockSpec((N_DEV, tn), lambda j: (0, j))],
        out_specs=pl.BlockSpec((D_MODEL, tn), lambda j: (0, j)),
        out_shape=jax.ShapeDtypeStruct((D_MODEL, N), F32),
        compiler_params=_cparams("parallel"),
    )(c_t, d_ada)


def _here():
    return lax.axis_index("x"), lax.axis_index("y"), lax.axis_index("c")


def _other_chips(x, y):
    return [(1 - x, y), (x, 1 - y), (1 - x, 1 - y)]


_ANY = pl.BlockSpec(memory_space=pl.ANY)


def _allgather8(blocks, name):
    L = len(blocks)

    def body(*refs):
        ins, outs = refs[:L], refs[L:2 * L]
        send_sems, recv_sems, local_sems = refs[2 * L:]
        x, y, c = _here()
        me, sibling = (x, y, c), (x, y, 1 - c)
        chips = _other_chips(x, y)

        def slot(px, py, pc):
            return 4 * px + 2 * py + pc

        def copy(l, k, block, to, src=None):
            dst = outs[l].at[slot(*block)]
            return pltpu.make_async_remote_copy(
                src_ref=dst if src is None else src, dst_ref=dst,
                send_sem=send_sems.at[l, k], recv_sem=recv_sems.at[l, k],
                device_id=to, device_id_type=MESH)

        mine = [pltpu.make_async_copy(ins[l], outs[l].at[slot(*me)], local_sems.at[l]) for l in range(L)]
        for cp in mine:
            cp.start()
        first = []
        for l in range(L):
            first.append(copy(l, 0, me, sibling, src=ins[l]))
            for j, chip in enumerate(chips):
                first.append(copy(l, 1 + j, me, (*chip, c), src=ins[l]))
        for cp in first:
            cp.start()
        passed = []
        for j, chip in enumerate(chips):
            for l in range(L):
                copy(l, 1 + j, (*chip, c), me).wait_recv()
                fwd = copy(l, 4 + j, (*chip, c), sibling)
                fwd.start()
                passed.append(fwd)
        for l in range(L):
            copy(l, 0, sibling, me).wait_recv()
        for j, chip in enumerate(chips):
            for l in range(L):
                copy(l, 4 + j, (*chip, 1 - c), me).wait_recv()
        for cp in first + passed:
            cp.wait_send()
        for cp in mine:
            cp.wait()

    return pl.pallas_call(
        body, name=name,
        in_specs=[_ANY] * L, out_specs=[_ANY] * L,
        out_shape=[jax.ShapeDtypeStruct((N_DEV,) + b.shape, b.dtype) for b in blocks],
        scratch_shapes=[pltpu.SemaphoreType.DMA((L, 7)), pltpu.SemaphoreType.DMA((L, 7)),
                        pltpu.SemaphoreType.DMA((L,))],
    )(*blocks)


def _sibling_swap(arrs, name):
    L = len(arrs)

    def body(*refs):
        ins, outs = refs[:L], refs[L:2 * L]
        send_sems, recv_sems = refs[2 * L:]
        x, y, c = _here()
        cps = [pltpu.make_async_remote_copy(src_ref=ins[l], dst_ref=outs[l], send_sem=send_sems.at[l],
                                            recv_sem=recv_sems.at[l], device_id=(x, y, 1 - c),
                                            device_id_type=MESH) for l in range(L)]
        for cp in cps:
            cp.start()
        for cp in cps:
            cp.wait()

    return pl.pallas_call(
        body, name=name,
        in_specs=[_ANY] * L, out_specs=[_ANY] * L,
        out_shape=[jax.ShapeDtypeStruct(a.shape, a.dtype) for a in arrs],
        scratch_shapes=[pltpu.SemaphoreType.DMA((L,)), pltpu.SemaphoreType.DMA((L,))],
    )(*arrs)


def _sibling_join(bufs, name):
    L = len(bufs)

    def body(*refs):
        outs = refs[L:2 * L]
        send_sems, recv_sems = refs[2 * L:]
        x, y, c = _here()
        for l in range(L):
            pltpu.make_async_remote_copy(src_ref=outs[l].at[c], dst_ref=outs[l].at[c], send_sem=send_sems.at[l],
                                         recv_sem=recv_sems.at[l], device_id=(x, y, 1 - c),
                                         device_id_type=MESH).start()
        for l in range(L):
            pltpu.make_async_remote_copy(src_ref=outs[l].at[c], dst_ref=outs[l].at[1 - c],
                                         send_sem=send_sems.at[l], recv_sem=recv_sems.at[l],
                                         device_id=(x, y, 1 - c), device_id_type=MESH).wait()

    return pl.pallas_call(
        body, name=name,
        in_specs=[_ANY] * L, out_specs=[_ANY] * L,
        out_shape=[jax.ShapeDtypeStruct(a.shape, a.dtype) for a in bufs],
        input_output_aliases={l: l for l in range(L)},
        scratch_shapes=[pltpu.SemaphoreType.DMA((L,)), pltpu.SemaphoreType.DMA((L,))],
    )(*bufs)


def _chip_scatter(arrs, name):
    L = len(arrs)

    def body(*refs):
        ins, outs = refs[:L], refs[L:2 * L]
        send_sems, recv_sems = refs[2 * L:]
        x, y, c = _here()
        cps = []
        for l in range(L):
            for j, (tx, ty) in enumerate(_other_chips(x, y)):
                cps.append(pltpu.make_async_remote_copy(
                    src_ref=ins[l].at[2 * tx + ty], dst_ref=outs[l].at[j],
                    send_sem=send_sems.at[l, j], recv_sem=recv_sems.at[l, j],
                    device_id=(tx, ty, c), device_id_type=MESH))
        for cp in cps:
            cp.start()
        for cp in cps:
            cp.wait()

    return pl.pallas_call(
        body, name=name,
        in_specs=[_ANY] * L, out_specs=[_ANY] * L,
        out_shape=[jax.ShapeDtypeStruct((3,) + a.shape[1:], a.dtype) for a in arrs],
        scratch_shapes=[pltpu.SemaphoreType.DMA((L, 3)), pltpu.SemaphoreType.DMA((L, 3))],
    )(*arrs)


_A_ORDER = np.array(A_HEAD_ORDER)
_A_INVERSE = np.argsort(_A_ORDER)


def _permute_in_weights(w_in, w_ba):
    qa = w_in[:, 0:512].reshape(D_MODEL, A_Q_HEADS, HEAD_DIM)[:, _A_ORDER, :].reshape(D_MODEL, 512)
    f_pad = jnp.pad(w_in[:, 2304:2312], ((0, 0), (0, LANES - B_HEADS)))
    w_a = jnp.concatenate([qa, w_in[:, 512:640], f_pad], axis=1)
    w_b = w_in[:, 640:2304]
    w_g = w_in[:, 2312:4360]
    w_ba_p = w_ba.reshape(A_Q_HEADS, HEAD_DIM, D_MODEL)[_A_ORDER].reshape(512, D_MODEL)
    return w_a, w_b, w_g, w_ba_p


def _unpermute_in_grads(dw_perm, dw_ba_p):
    qa = dw_perm[:, 0:512].reshape(D_MODEL, A_Q_HEADS, HEAD_DIM)[:, _A_INVERSE, :].reshape(D_MODEL, 512)
    dw_in = jnp.concatenate([qa, dw_perm[:, 512:640], dw_perm[:, W_A:W_A + W_B],
                             dw_perm[:, OFF_F:OFF_F + B_HEADS], dw_perm[:, W_A + W_B:]], axis=1)
    dw_ba = dw_ba_p.reshape(A_Q_HEADS, HEAD_DIM, D_MODEL)[_A_INVERSE].reshape(512, D_MODEL)
    return dw_in, dw_ba


def _rope_tables(pos):
    inv_freq = 1.0 / (ROPE_THETA ** (jnp.arange(0, HEAD_DIM, 2, dtype=F32) / HEAD_DIM))
    ang = pos.astype(F32)[:, None] * inv_freq
    cos, sin = jnp.cos(ang), jnp.sin(ang)
    return jnp.tile(cos, (1, 4)), jnp.tile(jnp.concatenate([-sin, sin], axis=1), (1, 2))


def _local_step(x, pos, ada, g1, g2, g3, g4, b_f, sinks, w_in, w_ba, w_bb, w_out, w_fi, w_fo, target):
    S = x.shape[0]
    t_fox = _tile(S, 512, LANES) if S >= 1024 else S // 2
    shift_m, scale_m, gate_m, shift_f, scale_f, gate_f = [ada[i:i + 1] for i in range(N_ADA)]
    cos_t, sin_t = _rope_tables(pos)
    w_a, w_b, w_g, w_ba_p = _permute_in_weights(w_in, w_ba)
    w_perm = jnp.concatenate([w_a, w_b, w_g], axis=1)
    sinks_p = sinks.reshape(A_KV_HEADS, 4).T.reshape(A_Q_HEADS)
    b_f_pad = jnp.pad(b_f, (0, LANES - B_HEADS)).reshape(1, LANES)

    h1 = _pre_norm(x, g1, scale_m, shift_m, "pre_mix_norm")
    p_a = _mm(h1, w_a, "nn", F32, "proj_a")
    p_b = _mm(h1, w_b, "nn", BF, "proj_b")
    p_g = _mm(h1, w_g, "nn", BF, "proj_g")
    (qk_a,) = _rope([p_a], [640], cos_t, sin_t, "rope_fwd")
    o_a, lse_a = _swa_fwd(qk_a, p_b, 0, sinks_p)
    cum = _fox_gate_fwd(p_a, b_f_pad)[:, :B_HEADS]
    cq_arr = jnp.repeat(cum, HEAD_DIM, axis=1)
    ck_arr = cum.T.reshape(4, 2, S)
    o_b, lse_b = _attn_fwd(p_b, 1, p_b, 5, p_b, 9, 4, False, t_fox, None, cq_arr, ck_arr, None, "fox_fwd")
    pa = _mm(o_a, w_ba_p, "nn", F32, "branch_a")
    pb = _mm(o_b, w_bb, "nn", F32, "branch_b")
    merged = _merge_fwd(p_g, pa, pb)
    y1 = _mm(merged, w_out, "nn", F32, "out_proj")
    x2, h2 = _post_pre(x, y1, g2, gate_m, g3, scale_f, shift_f)
    gu = _mm(h2, w_fi, "nn", BF, "ffn_in")
    act = _swiglu_fwd(gu)
    y2 = _mm(act, w_fo, "nn", F32, "ffn_out")
    d_out, d_y2, st_f = _final(x2, y2, g4, gate_f, target)

    d_act = _mm(d_y2, w_fo, "nt", F32, "ffn_out_dx")
    dw_fo = _mm(act, d_y2, "tn", F32, "ffn_out_dw")
    d_gu = _swiglu_bwd(d_act, gu)
    d_h2 = _mm(d_gu, w_fi, "nt", F32, "ffn_in_dx")
    dw_fi = _mm(h2, d_gu, "tn", F32, "ffn_in_dw")
    d_x2, d_y1, st_m = _mid_bwd(d_h2, x2, d_out, y1, g3, scale_f, g2, gate_m)
    d_merged = _mm(d_y1, w_out, "nt", F32, "out_proj_dx")
    dw_out = _mm(merged, d_y1, "tn", F32, "out_proj_dw")
    d_pa, d_pb, d_ga, d_gb = _merge_bwd(d_merged, p_g, pa, pb)
    d_oa = _mm(d_pa, w_ba_p, "nt", F32, "branch_a_dx")
    dw_ba_p = _mm(o_a, d_pa, "tn", F32, "branch_a_dw")
    d_ob = _mm(d_pb, w_bb, "nt", F32, "branch_b_dx")
    dw_bb = _mm(o_b, d_pb, "tn", F32, "branch_b_dw")
    dq_a, dk_a, dv_a, d_sink = _swa_bwd(qk_a, p_b, 0, o_a, d_oa, lse_a, sinks_p)
    dq_b, dk_b, dv_b, d_ck, d_cq = _attn_bwd(p_b, 1, p_b, 5, p_b, 9, o_b, d_ob, lse_b, 4, False, t_fox, None,
                                             cq_arr, ck_arr, None, "fox_bwd")
    d_qa, d_ka = _rope([dq_a, dk_a], [512, LANES], cos_t, -sin_t, "rope_bwd")
    pad8 = lambda a: jnp.pad(a, ((0, 0), (0, LANES - B_HEADS)))
    d_f, d_bf = _fox_gate_bwd(pad8(d_cq[:, ::HEAD_DIM]), pad8(d_ck.reshape(B_HEADS, S).T), p_a, b_f_pad)
    d_proj = jnp.concatenate([d_qa, d_ka, d_f, dv_a.astype(BF), dq_b.astype(BF), dk_b.astype(BF),
                              dv_b.astype(BF), d_ga, d_gb], axis=1)
    d_h1 = _mm(d_proj, w_perm, "nt", F32, "proj_dx")
    dw_perm = _mm(h1, d_proj, "tn", F32, "proj_dw")
    grad_x, st_p = _pre_bwd(d_h1, x, d_x2, g1, scale_m)

    dw_in, dw_ba = _unpermute_in_grads(dw_perm, dw_ba_p)
    d_sinks = d_sink[:, :2, 0].T.reshape(A_Q_HEADS)
    small = jnp.concatenate([
        st_p[0], st_p[1], st_m[3], st_m[0], st_m[1], st_f[0],
        st_p[2], st_m[4], st_m[2], st_f[1],
        st_f[2], d_bf[0, :B_HEADS], d_sinks,
        jnp.zeros((SM_LEN - SM_SINK - A_Q_HEADS,), F32)])
    return grad_x, (dw_in, dw_ba, dw_bb, dw_out, dw_fi, dw_fo), small


def kernel(x, c, positions, w_ada, b_ada, g_pre_mix, g_post_mix, w_in, b_f, sinks, w_branch_a, w_branch_b, w_out, g_pre_ffn, g_post_ffn, w_ffn_in, w_ffn_out, loss_target, m_w_ada, m_b_ada, m_g_pre_mix, m_g_post_mix, m_w_in, m_b_f, m_sinks, m_w_branch_a, m_w_branch_b, m_w_out, m_g_pre_ffn, m_g_post_ffn, m_w_ffn_in, m_w_ffn_out, v_w_ada, v_b_ada, v_g_pre_mix, v_g_post_mix, v_w_in, v_b_f, v_sinks, v_w_branch_a, v_w_branch_b, v_w_out, v_g_pre_ffn, v_g_post_ffn, v_w_ffn_in, v_w_ffn_out):
    xi, yi, ci = _here()
    chip = 2 * xi + yi
    dev = 2 * chip + ci

    def my_half(a):
        rows = a.shape[0] // 2
        return lax.dynamic_slice_in_dim(a, ci * rows, rows, axis=0)

    shards = [w_in[0], w_branch_a[0], w_branch_b[0], w_out[0], w_ffn_in[0], w_ffn_out[0]]
    gathered = _allgather8([c.reshape(8, LANES)] + [my_half(w).astype(BF) for w in shards], "gather_weights")
    c_all = gathered[0].reshape(N_DEV, D_MODEL)
    col_sharded = lambda g: jnp.transpose(g.reshape(N_CHIP, -1, g.shape[-1]), (1, 0, 2)).reshape(
        2 * g.shape[1], N_CHIP * g.shape[-1])
    row_sharded = lambda g: g.reshape(N_DEV * g.shape[1], g.shape[-1])
    w_in_f, w_ba_f, w_bb_f = col_sharded(gathered[1]), col_sharded(gathered[2]), col_sharded(gathered[3])
    w_out_f, w_fi_f, w_fo_f = row_sharded(gathered[4]), col_sharded(gathered[5]), row_sharded(gathered[6])

    ada_cols = _mm(c_all, w_ada[0], "nn", F32, "ada_fwd")
    (ada_g,) = _allgather8([ada_cols], "gather_ada")
    ada_mine = lax.dynamic_index_in_dim(ada_g.reshape(N_CHIP, 2, N_DEV, -1)[:, 0], dev, axis=1, keepdims=False)
    ada = (ada_mine.reshape(-1) + b_ada[0]).reshape(N_ADA, D_MODEL)

    grad_x, dws, small = _local_step(
        x[0], positions[0], ada, g_pre_mix, g_post_mix, g_pre_ffn, g_post_ffn, b_f[0], sinks[0],
        w_in_f, w_ba_f, w_bb_f, w_out_f, w_fi_f, w_fo_f, loss_target[0])

    (small_g,) = _allgather8([small.reshape(8, SM_LEN // 8)], "gather_small")
    small_all = small_g.reshape(N_DEV, SM_LEN)
    small_tot, loss_row = _small_finalize(small_all)
    loss = loss_row[0, 0]
    d_ada_cols = lax.dynamic_slice_in_dim(small_all[:, :N_ADA * D_MODEL], chip * (N_ADA * D_MODEL // N_CHIP),
                                          N_ADA * D_MODEL // N_CHIP, axis=1)
    g_w_ada = _ada_dw(c_all.T, d_ada_cols)

    def pieces(dw, by_cols):
        if by_cols:
            R, C = dw.shape[0], dw.shape[1] // N_CHIP
            return jnp.transpose(dw.reshape(R, N_CHIP, C), (1, 0, 2))
        return dw.reshape(N_CHIP, dw.shape[0] // N_CHIP, dw.shape[1])

    by_cols = [True, True, True, False, True, False]
    keep, send = [], []
    for dw, bc in zip(dws, by_cols):
        p = pieces(dw, bc)
        rows = p.shape[1] // 2
        keep.append(lax.dynamic_slice_in_dim(p, ci * rows, rows, axis=1))
        send.append(lax.dynamic_slice_in_dim(p, (1 - ci) * rows, rows, axis=1).astype(BF))
    got = _sibling_swap(send, "grads_to_sibling")
    part_f32, part_bf = [], []
    for l, (k_, g_) in enumerate(zip(keep, got)):
        shp = k_.shape
        s32, sbf = _add_pair(k_.reshape(-1, shp[-1]), g_.reshape(-1, shp[-1]), f"chip_sum_{l}")
        part_f32.append(s32.reshape(shp))
        part_bf.append(sbf.reshape(shp))
    recv = _chip_scatter(part_bf, "grads_to_chips")
    halves = [_add_three(lax.dynamic_index_in_dim(p32, chip, axis=0, keepdims=False), r, f"shard_sum_{l}")
              for l, (p32, r) in enumerate(zip(part_f32, recv))]
    joined = _sibling_join(halves, "grads_join")
    g_big = [j.reshape(2 * j.shape[1], j.shape[2]) for j in joined]
    g_w_in, g_w_ba, g_w_bb, g_w_out, g_w_fi, g_w_fo = g_big

    def small_vec(b_ada_, g1_, g2_, g3_, g4_, b_f_, sinks_):
        return jnp.concatenate([b_ada_[0], g1_[0], g2_[0], g3_[0], g4_[0], jnp.zeros((D_MODEL,), F32),
                                b_f_[0], sinks_[0], jnp.zeros((SM_LEN - SM_SINK - A_Q_HEADS,), F32)]
                               ).reshape(8, SM_LEN // 8)

    sw = small_vec(b_ada, g_pre_mix, g_post_mix, g_pre_ffn, g_post_ffn, b_f, sinks)
    sm = small_vec(m_b_ada, m_g_pre_mix, m_g_post_mix, m_g_pre_ffn, m_g_post_ffn, m_b_f, m_sinks)
    sv = small_vec(v_b_ada, v_g_pre_mix, v_g_post_mix, v_g_pre_ffn, v_g_post_ffn, v_b_f, v_sinks)
    s_upd = [u.reshape(SM_LEN) for u in _adamw(sw, small_tot.reshape(8, SM_LEN // 8), sm, sv, "adamw_small")]
    s_grad = small_tot.reshape(SM_LEN)

    def unpack(vec):
        row = lambda a, n: vec[a:a + n].reshape(1, n)
        return dict(b_ada=row(SM_ADA, N_ADA * D_MODEL), g_pre_mix=row(SM_G, D_MODEL),
                    g_post_mix=row(SM_G + D_MODEL, D_MODEL), g_pre_ffn=row(SM_G + 2 * D_MODEL, D_MODEL),
                    g_post_ffn=row(SM_G + 3 * D_MODEL, D_MODEL), b_f=row(SM_BF, B_HEADS),
                    sinks=row(SM_SINK, A_Q_HEADS))

    big = dict(
        w_ada=(w_ada, g_w_ada, m_w_ada, v_w_ada), w_in=(w_in, g_w_in, m_w_in, v_w_in),
        w_branch_a=(w_branch_a, g_w_ba, m_w_branch_a, v_w_branch_a),
        w_branch_b=(w_branch_b, g_w_bb, m_w_branch_b, v_w_branch_b),
        w_out=(w_out, g_w_out, m_w_out, v_w_out), w_ffn_in=(w_ffn_in, g_w_fi, m_w_ffn_in, v_w_ffn_in),
        w_ffn_out=(w_ffn_out, g_w_fo, m_w_ffn_out, v_w_ffn_out))
    grads, deltas, new_m, new_v = unpack(s_grad), unpack(s_upd[0]), unpack(s_upd[1]), unpack(s_upd[2])
    for n, (w_, g_, m_, v_) in big.items():
        d_, nm_, nv_ = _adamw(w_[0], g_, m_[0], v_[0], "adamw_" + n)
        grads[n], deltas[n], new_m[n], new_v[n] = g_[None], d_[None], nm_[None], nv_[None]

    names = ["w_ada", "b_ada", "g_pre_mix", "g_post_mix", "w_in", "b_f", "sinks", "w_branch_a", "w_branch_b",
             "w_out", "g_pre_ffn", "g_post_ffn", "w_ffn_in", "w_ffn_out"]
    return (loss, grad_x[None], *[grads[n] for n in names], *[deltas[n] for n in names],
            *[new_m[n] for n in names], *[new_v[n] for n in names])
```

```python
import functools
import math

import numpy as np
import jax
import jax.numpy as jnp
from jax import lax
from jax.experimental import pallas as pl
from jax.experimental.pallas import tpu as pltpu

F32 = jnp.float32
BF = jnp.bfloat16

D_MODEL = 1024
HEAD_DIM = 64
LANES = 128
WINDOW = 128
A_Q_HEADS = 8
A_KV_HEADS = 2
B_HEADS = 8
D_FF = 2816
ROPE_THETA = 10000.0
RMS_EPS = 1e-6
N_ADA = 6
N_DEV = 8
N_CHIP = 4

ADAM_LR = 0.001
ADAM_B1 = 0.9
ADAM_B2 = 0.999
ADAM_EPS = 1e-08
ADAM_WD = 0.01
ADAM_STEP = 10

VMEM_LIMIT = 48 * 1024 * 1024
MESH = pl.DeviceIdType.MESH

A_HEAD_ORDER = (0, 4, 1, 5, 2, 6, 3, 7)

OFF_QA, OFF_KA, OFF_F = 0, 512, 640
W_A = 768
OFF_VA, OFF_QB, OFF_KB, OFF_VB = 0, 128, 640, 1152
W_B = 1664
W_G = 2048
W_PERM = W_A + W_B + W_G


def _tile(n, cap, mult=LANES):
    if n <= cap:
        return n
    t = (cap // mult) * mult
    while t >= mult:
        if n % t == 0:
            return t
        t -= mult
    raise ValueError(f"no tile for {n}")


def _cparams(*sem):
    return pltpu.CompilerParams(dimension_semantics=sem, vmem_limit_bytes=VMEM_LIMIT)


def _own_refs(refs, comm, n_in, n_out, n_scratch):
    if comm is None:
        return list(refs), None
    return comm.split(refs, n_in, n_out, n_scratch)


def _comm_specs(comm, side):
    if comm is None:
        return []
    return [pl.BlockSpec(memory_space=pl.ANY)] * len(comm.ins if side == "in" else comm.out_shapes)


def _comm_edge(comm, comm_refs, grid, first):
    if comm is None:
        return
    at_edge = None
    for axis, n in enumerate(grid):
        here = pl.program_id(axis) == (0 if first else n - 1)
        at_edge = here if at_edge is None else at_edge & here
    pl.when(at_edge)(lambda: (comm.start if first else comm.finish)(*comm_refs))


def _mm(a, b, mode, out_dtype, name, tm_cap=512, tn_cap=1664, tk_cap=1408, comm=None):
    if mode == "nn":
        (M, K), (K2, N) = a.shape, b.shape
        dims = (((1,), (0,)), ((), ()))
    elif mode == "nt":
        (M, K), (N, K2) = a.shape, b.shape
        dims = (((1,), (1,)), ((), ()))
    else:
        (K, M), (K2, N) = a.shape, b.shape
        dims = (((0,), (0,)), ((), ()))
    assert K == K2, (a.shape, b.shape, mode)
    tm, tn, tk = _tile(M, tm_cap), _tile(N, tn_cap), _tile(K, tk_cap)
    nk = K // tk
    if mode == "nn":
        a_spec = pl.BlockSpec((tm, tk), lambda i, j, k: (i, k))
        b_spec = pl.BlockSpec((tk, tn), lambda i, j, k: (k, j))
    elif mode == "nt":
        a_spec = pl.BlockSpec((tm, tk), lambda i, j, k: (i, k))
        b_spec = pl.BlockSpec((tn, tk), lambda i, j, k: (j, k))
    else:
        a_spec = pl.BlockSpec((tk, tm), lambda i, j, k: (k, i))
        b_spec = pl.BlockSpec((tk, tn), lambda i, j, k: (k, j))

    grid = (M // tm, N // tn, nk)

    def kern(*refs):
        (a_ref, b_ref, o_ref, acc_ref), comm_refs = _own_refs(refs, comm, 2, 1, 1)
        k = pl.program_id(2)
        _comm_edge(comm, comm_refs, grid, first=True)
        part = lax.dot_general(a_ref[...].astype(BF), b_ref[...].astype(BF), dims,
                               preferred_element_type=F32)

        @pl.when(k == 0)
        def _():
            acc_ref[...] = part

        @pl.when(k > 0)
        def _():
            acc_ref[...] += part

        @pl.when(k == nk - 1)
        def _():
            o_ref[...] = acc_ref[...].astype(o_ref.dtype)

        _comm_edge(comm, comm_refs, grid, first=False)

    res = pl.pallas_call(
        kern, name=name, grid=grid,
        in_specs=[a_spec, b_spec] + _comm_specs(comm, "in"),
        out_specs=[pl.BlockSpec((tm, tn), lambda i, j, k: (i, j))] + _comm_specs(comm, "out"),
        out_shape=[jax.ShapeDtypeStruct((M, N), out_dtype)] + (comm.out_shapes if comm else []),
        scratch_shapes=[pltpu.VMEM((tm, tn), F32)] + (comm.sem_shapes if comm else []),
        compiler_params=_cparams("parallel", "parallel", "arbitrary"),
    )(a, b, *(comm.ins if comm else []))
    return (res[0], res[1:]) if comm else res[0]


ROWS = 256


def _row_spec(tm, width=D_MODEL, col=0):
    return pl.BlockSpec((tm, width), lambda i: (i, col))


def _vec_spec(width=D_MODEL):
    return pl.BlockSpec((1, width), lambda i: (0, 0))


def _rms(x):
    return lax.rsqrt(jnp.mean(x * x, axis=-1, keepdims=True) + RMS_EPS)


def _colsum(x):
    return jnp.sum(x, axis=0, keepdims=True)


def _norm_bwd(d_xn, xn, r):
    return r * (d_xn - xn * jnp.mean(d_xn * xn, axis=-1, keepdims=True))


def _pre_norm(x, g, scale, shift, name):
    S = x.shape[0]
    tm = _tile(S, ROWS, 8)

    def kern(x_ref, g_ref, sc_ref, sh_ref, h_ref):
        xf = x_ref[...]
        y = xf * _rms(xf) * g_ref[...]
        h_ref[...] = (y * (1.0 + sc_ref[...]) + sh_ref[...]).astype(BF)

    return pl.pallas_call(
        kern, name=name, grid=(S // tm,),
        in_specs=[_row_spec(tm), _vec_spec(), _vec_spec(), _vec_spec()],
        out_specs=_row_spec(tm),
        out_shape=jax.ShapeDtypeStruct((S, D_MODEL), BF),
        compiler_params=_cparams("parallel"),
    )(x, g, scale, shift)


def _post_pre(x, y1, g2, gate_m, g3, scale_f, shift_f):
    S = x.shape[0]
    tm = _tile(S, ROWS, 8)

    def kern(x_ref, y_ref, g2_ref, gm_ref, g3_ref, sc_ref, sh_ref, x2_ref, h2_ref):
        y = y_ref[...]
        n2 = y * _rms(y) * g2_ref[...]
        x2 = x_ref[...] + gm_ref[...] * n2
        x2_ref[...] = x2
        n3 = x2 * _rms(x2) * g3_ref[...]
        h2_ref[...] = (n3 * (1.0 + sc_ref[...]) + sh_ref[...]).astype(BF)

    return pl.pallas_call(
        kern, name="post_mix_pre_ffn", grid=(S // tm,),
        in_specs=[_row_spec(tm), _row_spec(tm)] + [_vec_spec()] * 5,
        out_specs=[_row_spec(tm), _row_spec(tm)],
        out_shape=[jax.ShapeDtypeStruct((S, D_MODEL), F32), jax.ShapeDtypeStruct((S, D_MODEL), BF)],
        compiler_params=_cparams("parallel"),
    )(x, y1, g2, gate_m, g3, scale_f, shift_f)


def _stats_spec():
    return pl.BlockSpec((8, D_MODEL), lambda i: (0, 0))


def _final(x2, y2, g4, gate_f, target):
    S = x2.shape[0]
    tm = _tile(S, ROWS, 8)

    def kern(x2_ref, y_ref, g4_ref, gf_ref, t_ref, dout_ref, dy_ref, st_ref):
        @pl.when(pl.program_id(0) == 0)
        def _():
            st_ref[...] = jnp.zeros_like(st_ref)

        y = y_ref[...]
        r = _rms(y)
        yn = y * r
        n4 = yn * g4_ref[...]
        diff = x2_ref[...] + gf_ref[...] * n4 - t_ref[...]
        d_out = diff / D_MODEL
        dout_ref[...] = d_out
        dn = d_out * gf_ref[...]
        dy_ref[...] = _norm_bwd(dn * g4_ref[...], yn, r).astype(BF)
        st_ref[0:1, :] += _colsum(d_out * n4)
        st_ref[1:2, :] += _colsum(dn * yn)
        st_ref[2:3, :] += _colsum(diff * diff)

    return pl.pallas_call(
        kern, name="final_loss", grid=(S // tm,),
        in_specs=[_row_spec(tm), _row_spec(tm), _vec_spec(), _vec_spec(), _row_spec(tm)],
        out_specs=[_row_spec(tm), _row_spec(tm), _stats_spec()],
        out_shape=[jax.ShapeDtypeStruct((S, D_MODEL), F32), jax.ShapeDtypeStruct((S, D_MODEL), BF),
                   jax.ShapeDtypeStruct((8, D_MODEL), F32)],
        compiler_params=_cparams("arbitrary"),
    )(x2, y2, g4, gate_f, target)


def _mid_bwd(d_h2, x2, d_out, y1, g3, scale_f, g2, gate_m):
    S = x2.shape[0]
    tm = _tile(S, ROWS, 8)

    def kern(dh_ref, x2_ref, dout_ref, y_ref, g3_ref, sc_ref, g2_ref, gm_ref, dx2_ref, dy_ref, st_ref):
        @pl.when(pl.program_id(0) == 0)
        def _():
            st_ref[...] = jnp.zeros_like(st_ref)

        dh = dh_ref[...]
        x2 = x2_ref[...]
        r3 = _rms(x2)
        xn = x2 * r3
        one_sc = 1.0 + sc_ref[...]
        d_x2 = dout_ref[...] + _norm_bwd(dh * one_sc * g3_ref[...], xn, r3)
        dx2_ref[...] = d_x2
        y = y_ref[...]
        r2 = _rms(y)
        yn = y * r2
        dn = d_x2 * gm_ref[...]
        dy_ref[...] = _norm_bwd(dn * g2_ref[...], yn, r2).astype(BF)
        st_ref[0:1, :] += _colsum(dh)
        st_ref[1:2, :] += _colsum(dh * (xn * g3_ref[...]))
        st_ref[2:3, :] += _colsum(dh * one_sc * xn)
        st_ref[3:4, :] += _colsum(d_x2 * (yn * g2_ref[...]))
        st_ref[4:5, :] += _colsum(dn * yn)

    return pl.pallas_call(
        kern, name="mid_bwd", grid=(S // tm,),
        in_specs=[_row_spec(tm)] * 4 + [_vec_spec()] * 4,
        out_specs=[_row_spec(tm), _row_spec(tm), _stats_spec()],
        out_shape=[jax.ShapeDtypeStruct((S, D_MODEL), F32), jax.ShapeDtypeStruct((S, D_MODEL), BF),
                   jax.ShapeDtypeStruct((8, D_MODEL), F32)],
        compiler_params=_cparams("arbitrary"),
    )(d_h2, x2, d_out, y1, g3, scale_f, g2, gate_m)


def _pre_bwd(d_h1, x, d_x2, g1, scale_m):
    S = x.shape[0]
    tm = _tile(S, ROWS, 8)

    def kern(dh_ref, x_ref, dx2_ref, g_ref, sc_ref, gx_ref, st_ref):
        @pl.when(pl.program_id(0) == 0)
        def _():
            st_ref[...] = jnp.zeros_like(st_ref)

        dh = dh_ref[...]
        xf = x_ref[...]
        r = _rms(xf)
        xn = xf * r
        one_sc = 1.0 + sc_ref[...]
        gx_ref[...] = dx2_ref[...] + _norm_bwd(dh * one_sc * g_ref[...], xn, r)
        st_ref[0:1, :] += _colsum(dh)
        st_ref[1:2, :] += _colsum(dh * (xn * g_ref[...]))
        st_ref[2:3, :] += _colsum(dh * one_sc * xn)

    return pl.pallas_call(
        kern, name="pre_mix_bwd", grid=(S // tm,),
        in_specs=[_row_spec(tm)] * 3 + [_vec_spec()] * 2,
        out_specs=[_row_spec(tm), _stats_spec()],
        out_shape=[jax.ShapeDtypeStruct((S, D_MODEL), F32), jax.ShapeDtypeStruct((8, D_MODEL), F32)],
        compiler_params=_cparams("arbitrary"),
    )(d_h1, x, d_x2, g1, scale_m)


def _rope(xs, widths, cos_t, sin_t, name):
    S = xs[0].shape[0]
    tm = _tile(S, 512, 8)
    n = len(xs)

    def kern(*refs):
        cos = refs[n][...]
        sin = refs[n + 1][...]
        first = (lax.broadcasted_iota(jnp.int32, cos.shape, 1) % HEAD_DIM) < HEAD_DIM // 2
        for x_ref, o_ref, w in zip(refs[:n], refs[n + 2:], widths):
            for c0 in range(0, w, LANES):
                v = x_ref[:, c0:c0 + LANES]
                partner = jnp.where(first, pltpu.roll(v, LANES - HEAD_DIM // 2, 1),
                                    pltpu.roll(v, HEAD_DIM // 2, 1))
                o_ref[:, c0:c0 + LANES] = (v * cos + partner * sin).astype(BF)

    return pl.pallas_call(
        kern, name=name, grid=(S // tm,),
        in_specs=[_row_spec(tm, w) for w in widths] + [_row_spec(tm, LANES)] * 2,
        out_specs=[_row_spec(tm, w) for w in widths],
        out_shape=[jax.ShapeDtypeStruct((S, w), BF) for w in widths],
        compiler_params=_cparams("parallel"),
    )(*xs, cos_t, sin_t)


def _merge_fwd(pg, pa, pb):
    S = pa.shape[0]
    tm = _tile(S, ROWS, 8)

    def kern(ga_ref, gb_ref, pa_ref, pb_ref, o_ref):
        ga = jax.nn.sigmoid(ga_ref[...].astype(F32))
        gb = jax.nn.sigmoid(gb_ref[...].astype(F32))
        o_ref[...] = (ga * pa_ref[...] + gb * pb_ref[...]).astype(BF)

    return pl.pallas_call(
        kern, name="merge_fwd", grid=(S // tm,),
        in_specs=[_row_spec(tm, col=0), _row_spec(tm, col=1), _row_spec(tm), _row_spec(tm)],
        out_specs=_row_spec(tm),
        out_shape=jax.ShapeDtypeStruct((S, D_MODEL), BF),
        compiler_params=_cparams("parallel"),
    )(pg, pg, pa, pb)


def _merge_bwd(d_merged, pg, pa, pb):
    S = pa.shape[0]
    tm = _tile(S, ROWS, 8)

    def kern(dm_ref, ga_ref, gb_ref, pa_ref, pb_ref, dpa_ref, dpb_ref, dga_ref, dgb_ref):
        dm = dm_ref[...]
        ga = jax.nn.sigmoid(ga_ref[...].astype(F32))
        gb = jax.nn.sigmoid(gb_ref[...].astype(F32))
        dpa_ref[...] = (dm * ga).astype(BF)
        dpb_ref[...] = (dm * gb).astype(BF)
        dga_ref[...] = (dm * pa_ref[...] * ga * (1.0 - ga)).astype(BF)
        dgb_ref[...] = (dm * pb_ref[...] * gb * (1.0 - gb)).astype(BF)

    bf_out = jax.ShapeDtypeStruct((S, D_MODEL), BF)
    return pl.pallas_call(
        kern, name="merge_bwd", grid=(S // tm,),
        in_specs=[_row_spec(tm), _row_spec(tm, col=0), _row_spec(tm, col=1), _row_spec(tm), _row_spec(tm)],
        out_specs=[_row_spec(tm)] * 4,
        out_shape=[bf_out] * 4,
        compiler_params=_cparams("parallel"),
    )(d_merged, pg, pg, pa, pb)


def _swiglu_fwd(gu):
    S = gu.shape[0]
    tm = _tile(S, ROWS, 8)
    tc = _tile(D_FF, 1408)
    nc = D_FF // tc

    def kern(g_ref, u_ref, o_ref):
        g = g_ref[...].astype(F32)
        o_ref[...] = (g * jax.nn.sigmoid(g) * u_ref[...].astype(F32)).astype(BF)

    return pl.pallas_call(
        kern, name="swiglu_fwd", grid=(S // tm, nc),
        in_specs=[pl.BlockSpec((tm, tc), lambda i, j: (i, j)),
                  pl.BlockSpec((tm, tc), lambda i, j: (i, j + nc))],
        out_specs=pl.BlockSpec((tm, tc), lambda i, j: (i, j)),
        out_shape=jax.ShapeDtypeStruct((S, D_FF), BF),
        compiler_params=_cparams("parallel", "parallel"),
    )(gu, gu)


def _swiglu_bwd(d_act, gu):
    S = gu.shape[0]
    tm = _tile(S, 128, 8)

    def kern(da_ref, g_ref, u_ref, o_ref):
        g = g_ref[...].astype(F32)
        u = u_ref[...].astype(F32)
        da = da_ref[...]
        sg = jax.nn.sigmoid(g)
        o_ref[:, :D_FF] = (da * u * (sg * (1.0 + g * (1.0 - sg)))).astype(BF)
        o_ref[:, D_FF:] = (da * (g * sg)).astype(BF)

    return pl.pallas_call(
        kern, name="swiglu_bwd", grid=(S // tm,),
        in_specs=[_row_spec(tm, D_FF), _row_spec(tm, D_FF, 0), _row_spec(tm, D_FF, 1)],
        out_specs=_row_spec(tm, 2 * D_FF),
        out_shape=jax.ShapeDtypeStruct((S, 2 * D_FF), BF),
        compiler_params=_cparams("parallel"),
    )(d_act, gu, gu)


def _split3(x):
    hi = x.astype(BF)
    r1 = x - hi.astype(F32)
    mid = r1.astype(BF)
    lo = (r1 - mid.astype(F32)).astype(BF)
    return hi, mid, lo


def _tri_dot(tri, x):
    return sum(jnp.dot(tri, part, preferred_element_type=F32) for part in _split3(x))


def _log_sigmoid(z):
    return jnp.minimum(z, 0.0) - jnp.log(1.0 + jnp.exp(-jnp.abs(z)))


def _fox_gate_fwd(pa, b_f_pad):
    S = pa.shape[0]
    T = _tile(S, 512, 8)
    f_col = OFF_F // LANES

    def kern(z_ref, b_ref, cum_ref, carry_ref):
        @pl.when(pl.program_id(0) == 0)
        def _():
            carry_ref[...] = jnp.zeros_like(carry_ref)

        log_f = _log_sigmoid(z_ref[...] + b_ref[...])
        row = lax.broadcasted_iota(jnp.int32, (T, T), 0)
        col = lax.broadcasted_iota(jnp.int32, (T, T), 1)
        tri = (col <= row).astype(BF)
        cum = _tri_dot(tri, log_f) + carry_ref[...]
        cum_ref[...] = cum
        carry_ref[...] = cum[T - 1:T, :]

    return pl.pallas_call(
        kern, name="fox_gate_fwd", grid=(S // T,),
        in_specs=[_row_spec(T, LANES, f_col), _vec_spec(LANES)],
        out_specs=_row_spec(T, LANES),
        out_shape=jax.ShapeDtypeStruct((S, LANES), F32),
        scratch_shapes=[pltpu.VMEM((1, LANES), F32)],
        compiler_params=_cparams("arbitrary"),
    )(pa, b_f_pad)


def _fox_gate_bwd(rowsum_ds, colsum_ds, pa, b_f_pad):
    S = pa.shape[0]
    T = _tile(S, 512, 8)
    nb = S // T
    f_col = OFF_F // LANES

    def kern(dr_ref, dc_ref, z_ref, b_ref, df_ref, dbf_ref, carry_ref):
        @pl.when(pl.program_id(0) == 0)
        def _():
            carry_ref[...] = jnp.zeros_like(carry_ref)
            dbf_ref[...] = jnp.zeros_like(dbf_ref)

        row = lax.broadcasted_iota(jnp.int32, (T, T), 0)
        col = lax.broadcasted_iota(jnp.int32, (T, T), 1)
        tri = (col >= row).astype(BF)
        rev = _tri_dot(tri, dr_ref[...] - dc_ref[...]) + carry_ref[...]
        carry_ref[...] = rev[0:1, :]
        z = z_ref[...] + b_ref[...]
        lane = lax.broadcasted_iota(jnp.int32, (T, LANES), 1)
        d_z = jnp.where(lane < B_HEADS, rev * jax.nn.sigmoid(-z), 0.0)
        df_ref[...] = d_z.astype(BF)
        dbf_ref[0:1, :] += _colsum(d_z)

    return pl.pallas_call(
        kern, name="fox_gate_bwd", grid=(nb,),
        in_specs=[pl.BlockSpec((T, LANES), lambda i: (nb - 1 - i, 0)),
                  pl.BlockSpec((T, LANES), lambda i: (nb - 1 - i, 0)),
                  pl.BlockSpec((T, LANES), lambda i: (nb - 1 - i, f_col)),
                  _vec_spec(LANES)],
        out_specs=[pl.BlockSpec((T, LANES), lambda i: (nb - 1 - i, 0)),
                   pl.BlockSpec((8, LANES), lambda i: (0, 0))],
        out_shape=[jax.ShapeDtypeStruct((S, LANES), BF), jax.ShapeDtypeStruct((8, LANES), F32)],
        scratch_shapes=[pltpu.VMEM((1, LANES), F32)],
        compiler_params=_cparams("arbitrary"),
    )(rowsum_ds, colsum_ds, pa, b_f_pad)


NEG_INF = float("-inf")
QK_SCALE = 1.0 / math.sqrt(HEAD_DIM)


def _half_mask(shape, half):
    lane = lax.broadcasted_iota(jnp.int32, shape, 1)
    return (lane < HEAD_DIM) if half == 0 else (lane >= HEAD_DIM)


def _valid(i, j, T, rowcol, window):
    rel = (i - j) * T + rowcol
    ok = rel >= 0
    if window is not None:
        ok = ok & (rel < window)
    return ok


def _attn_fwd(q_arr, q_col, k_arr, k_col, v_arr, v_col, n_pairs, kv_shared, T, window,
              cq_arr, ck_arr, sinks, name, comm=None):
    S = q_arr.shape[0]
    nq = S // T
    use_bias = cq_arr is not None
    use_sink = sinks is not None
    back = 0 if window is None else -(-window // T)
    grid = (n_pairs, nq)
    n_in = 3 + 2 * use_bias + use_sink

    def kern(*refs):
        refs, comm_refs = _own_refs(refs, comm, n_in, 2, 0)
        _comm_edge(comm, comm_refs, grid, first=True)
        q_ref, k_ref, v_ref = refs[:3]
        pos = 3
        if use_bias:
            cq_ref, ck_ref = refs[pos:pos + 2]
            pos += 2
        if use_sink:
            sink_ref = refs[pos]
            pos += 1
        o_ref, lse_ref = refs[pos:pos + 2]
        p_id = pl.program_id(0)
        i = pl.program_id(1)
        q = q_ref[...]
        rowcol = lax.broadcasted_iota(jnp.int32, (T, T), 0) - lax.broadcasted_iota(jnp.int32, (T, T), 1)
        lo = jnp.maximum(i - back, 0) if window is not None else 0
        outs, lses = [], []
        for half in (0, 1):
            hm = _half_mask((T, LANES), half)
            qh = (jnp.where(hm, q, 0).astype(F32) * QK_SCALE).astype(BF)
            if use_bias:
                cq = cq_ref[:, half * HEAD_DIM:half * HEAD_DIM + 1]
            if use_sink:
                m0 = jnp.full((T, 1), sink_ref[2 * p_id + half], F32)
                l0 = jnp.ones((T, 1), F32)
            else:
                m0 = jnp.full((T, 1), NEG_INF, F32)
                l0 = jnp.zeros((T, 1), F32)

            def step(j, carry, masked):
                m, l, acc = carry
                rows = pl.ds(pl.multiple_of(j * T, T), T)
                kj = k_ref[rows, :].astype(BF)
                vj = v_ref[rows, :].astype(BF)
                s = lax.dot_general(qh, kj, (((1,), (1,)), ((), ())), preferred_element_type=F32)
                if use_bias:
                    s = s + cq - ck_ref[0, half:half + 1, rows]
                if masked:
                    s = jnp.where(_valid(i, j, T, rowcol, window), s, NEG_INF)
                m_new = jnp.maximum(m, jnp.max(s, axis=1, keepdims=True))
                alpha = jnp.exp(m - m_new)
                p = jnp.exp(s - m_new)
                l_new = alpha * l + jnp.sum(p, axis=1, keepdims=True)
                acc_new = alpha * acc + jnp.dot(p.astype(BF), vj, preferred_element_type=F32)
                return m_new, l_new, acc_new

            init = (m0, l0, jnp.zeros((T, LANES), F32))
            if window is None:
                init = lax.fori_loop(0, i, functools.partial(step, masked=False), init)
                m, l, acc = step(i, init, True)
            else:
                m, l, acc = lax.fori_loop(lo, i + 1, functools.partial(step, masked=True), init)
            outs.append(acc / l)
            lses.append(m + jnp.log(l))
        hm0 = _half_mask((T, LANES), 0)
        o_ref[...] = jnp.where(hm0, outs[0], outs[1])
        lse_ref[...] = jnp.where(hm0, lses[0], lses[1])
        _comm_edge(comm, comm_refs, grid, first=False)

    kv_idx = (lambda c0: (lambda p, i: (0, c0))) if kv_shared else (lambda c0: (lambda p, i: (0, c0 + p)))
    in_specs = [pl.BlockSpec((T, LANES), lambda p, i: (i, q_col + p)),
                pl.BlockSpec((S, LANES), kv_idx(k_col)),
                pl.BlockSpec((S, LANES), kv_idx(v_col))]
    args = [q_arr, k_arr, v_arr]
    if use_bias:
        in_specs += [pl.BlockSpec((T, LANES), lambda p, i: (i, p)),
                     pl.BlockSpec((1, 2, S), lambda p, i: (p, 0, 0))]
        args += [cq_arr, ck_arr]
    if use_sink:
        in_specs.append(pl.BlockSpec(memory_space=pltpu.SMEM))
        args.append(sinks)
    out_spec = pl.BlockSpec((T, LANES), lambda p, i: (i, p))
    res = pl.pallas_call(
        kern, name=name, grid=grid,
        in_specs=in_specs + _comm_specs(comm, "in"),
        out_specs=[out_spec, out_spec] + _comm_specs(comm, "out"),
        out_shape=[jax.ShapeDtypeStruct((S, n_pairs * LANES), F32)] * 2 + (comm.out_shapes if comm else []),
        scratch_shapes=comm.sem_shapes if comm else [],
        compiler_params=_cparams("arbitrary", "arbitrary"),
    )(*args, *(comm.ins if comm else []))
    return (res[0], res[1], res[2:]) if comm else (res[0], res[1])


def _attn_bwd(q_arr, q_col, k_arr, k_col, v_arr, v_col, o_arr, do_arr, lse_arr, n_pairs, kv_shared, T,
              window, cq_arr, ck_arr, sinks, name, comm=None):
    S = q_arr.shape[0]
    nq = S // T
    use_bias = cq_arr is not None
    use_sink = sinks is not None
    back = 0 if window is None else -(-window // T)
    kv_w = LANES if kv_shared else n_pairs * LANES
    grid = (n_pairs,)
    n_in = 6 + 2 * use_bias + use_sink
    n_out = 3 + 2 * use_bias + use_sink

    def kern(*refs):
        refs, comm_refs = _own_refs(refs, comm, n_in, n_out, 0)
        _comm_edge(comm, comm_refs, grid, first=True)
        q_ref, k_ref, v_ref, o_ref, do_ref, lse_ref = refs[:6]
        pos = 6
        if use_bias:
            cq_ref, ck_ref = refs[pos:pos + 2]
            pos += 2
        if use_sink:
            sink_ref = refs[pos]
            pos += 1
        dq_ref, dk_ref, dv_ref = refs[pos:pos + 3]
        pos += 3
        if use_bias:
            dck_ref, dcq_ref = refs[pos:pos + 2]
            pos += 2
        if use_sink:
            dsink_ref = refs[pos]
        p_id = pl.program_id(0)
        rowcol = lax.broadcasted_iota(jnp.int32, (T, T), 0) - lax.broadcasted_iota(jnp.int32, (T, T), 1)

        def zero_kv():
            dk_ref[...] = jnp.zeros_like(dk_ref)
            dv_ref[...] = jnp.zeros_like(dv_ref)

        if kv_shared:
            pl.when(p_id == 0)(zero_kv)
        else:
            zero_kv()
        if use_bias:
            dck_ref[...] = jnp.zeros_like(dck_ref)
        if use_sink:
            dsink_ref[...] = jnp.zeros_like(dsink_ref)

        for half in (0, 1):
            hm = _half_mask((T, LANES), half)
            lane0 = half * HEAD_DIM

            def outer(i, carry):
                qrows = pl.ds(pl.multiple_of(i * T, T), T)
                qh = (jnp.where(hm, q_ref[qrows, :], 0).astype(F32) * QK_SCALE).astype(BF)
                do_f = jnp.where(hm, do_ref[qrows, :], 0.0)
                doh = do_f.astype(BF)
                delta = jnp.sum(do_f * o_ref[qrows, :], axis=1, keepdims=True)
                lse = lse_ref[qrows, lane0:lane0 + 1]
                if use_bias:
                    cq = cq_ref[qrows, lane0:lane0 + 1]
                lo = jnp.maximum(i - back, 0) if window is not None else 0

                def inner(j, carry_in, masked):
                    dq, rs = carry_in
                    krows = pl.ds(pl.multiple_of(j * T, T), T)
                    kj = k_ref[krows, :].astype(BF)
                    vj = v_ref[krows, :].astype(BF)
                    s = lax.dot_general(qh, kj, (((1,), (1,)), ((), ())), preferred_element_type=F32)
                    if use_bias:
                        s = s + cq - ck_ref[0, half:half + 1, krows]
                    if masked:
                        s = jnp.where(_valid(i, j, T, rowcol, window), s, NEG_INF)
                    p = jnp.exp(s - lse)
                    dp = lax.dot_general(doh, vj, (((1,), (1,)), ((), ())), preferred_element_type=F32)
                    ds = p * (dp - delta)
                    ds_b = ds.astype(BF)
                    dv_ref[krows, :] += lax.dot_general(p.astype(BF), doh, (((0,), (0,)), ((), ())),
                                                        preferred_element_type=F32)
                    dk_ref[krows, :] += lax.dot_general(ds_b, qh, (((0,), (0,)), ((), ())),
                                                        preferred_element_type=F32)
                    if use_bias:
                        dck_ref[0, half:half + 1, krows] += jnp.sum(ds, axis=0, keepdims=True)
                        rs = rs + jnp.sum(ds, axis=1, keepdims=True)
                    kh = jnp.where(hm, kj, 0)
                    return dq + jnp.dot(ds_b, kh, preferred_element_type=F32), rs

                init = (jnp.zeros((T, LANES), F32), jnp.zeros((T, 1), F32))
                if window is None:
                    init = lax.fori_loop(0, i, functools.partial(inner, masked=False), init)
                    dq, rs = inner(i, init, True)
                else:
                    dq, rs = lax.fori_loop(lo, i + 1, functools.partial(inner, masked=True), init)
                dq = dq * QK_SCALE
                if half == 0:
                    dq_ref[qrows, :] = dq
                else:
                    dq_ref[qrows, :] += dq
                if use_bias:
                    rs_b = jnp.broadcast_to(rs, (T, LANES))
                    dcq_ref[qrows, :] = rs_b if half == 0 else jnp.where(hm, rs_b, dcq_ref[qrows, :])
                if use_sink:
                    p_sink = jnp.exp(sink_ref[2 * p_id + half] - lse)
                    dsink_ref[0, half:half + 1, :] += jnp.broadcast_to(
                        -jnp.sum(p_sink * delta, axis=0, keepdims=True), (1, LANES))
                return carry

            lax.fori_loop(0, nq, outer, 0)
        _comm_edge(comm, comm_refs, grid, first=False)

    kv_idx = (lambda c0: (lambda p: (0, c0))) if kv_shared else (lambda c0: (lambda p: (0, c0 + p)))
    pair = lambda c0: pl.BlockSpec((S, LANES), lambda p: (0, c0 + p))
    in_specs = [pair(q_col), pl.BlockSpec((S, LANES), kv_idx(k_col)), pl.BlockSpec((S, LANES), kv_idx(v_col)),
                pair(0), pair(0), pair(0)]
    args = [q_arr, k_arr, v_arr, o_arr, do_arr, lse_arr]
    if use_bias:
        in_specs += [pair(0), pl.BlockSpec((1, 2, S), lambda p: (p, 0, 0))]
        args += [cq_arr, ck_arr]
    if use_sink:
        in_specs.append(pl.BlockSpec(memory_space=pltpu.SMEM))
        args.append(sinks)
    out_specs = [pair(0), pl.BlockSpec((S, LANES), kv_idx(0)), pl.BlockSpec((S, LANES), kv_idx(0))]
    out_shape = [jax.ShapeDtypeStruct((S, n_pairs * LANES), F32),
                 jax.ShapeDtypeStruct((S, kv_w), F32), jax.ShapeDtypeStruct((S, kv_w), F32)]
    if use_bias:
        out_specs += [pl.BlockSpec((1, 2, S), lambda p: (p, 0, 0)), pair(0)]
        out_shape += [jax.ShapeDtypeStruct((n_pairs, 2, S), F32), jax.ShapeDtypeStruct((S, n_pairs * LANES), F32)]
    if use_sink:
        out_specs.append(pl.BlockSpec((1, 8, LANES), lambda p: (p, 0, 0)))
        out_shape.append(jax.ShapeDtypeStruct((n_pairs, 8, LANES), F32))
    res = pl.pallas_call(
        kern, name=name, grid=grid,
        in_specs=in_specs + _comm_specs(comm, "in"),
        out_specs=out_specs + _comm_specs(comm, "out"),
        out_shape=out_shape + (comm.out_shapes if comm else []),
        scratch_shapes=comm.sem_shapes if comm else [],
        compiler_params=_cparams("arbitrary"),
    )(*args, *(comm.ins if comm else []))
    return (*res[:n_out], res[n_out:]) if comm else res


SWA_TQ = 256


def _swa_window(i, tq):
    start = pl.multiple_of(jnp.maximum(i * tq - WINDOW, 0), LANES)
    return start, i * tq - start


def _swa_valid(offset, tq):
    rel = offset + lax.broadcasted_iota(jnp.int32, (tq, tq + WINDOW), 0) \
        - lax.broadcasted_iota(jnp.int32, (tq, tq + WINDOW), 1)
    return (rel >= 0) & (rel < WINDOW)


def _swa_fwd(qk, v_arr, v_col, sinks):
    S = qk.shape[0]
    tq = min(SWA_TQ, S - WINDOW)
    win = tq + WINDOW

    def kern(q_ref, k_ref, v_ref, sink_ref, o_ref, lse_ref):
        p_id, i = pl.program_id(0), pl.program_id(1)
        start, offset = _swa_window(i, tq)
        kw = k_ref[pl.ds(start, win), :]
        vw = v_ref[pl.ds(start, win), :].astype(BF)
        valid = _swa_valid(offset, tq)
        q = q_ref[...]
        outs, lses = [], []
        for half in (0, 1):
            hm = _half_mask((tq, LANES), half)
            qh = (jnp.where(hm, q, 0).astype(F32) * QK_SCALE).astype(BF)
            s = lax.dot_general(qh, kw, (((1,), (1,)), ((), ())), preferred_element_type=F32)
            s = jnp.where(valid, s, NEG_INF)
            sink = sink_ref[2 * p_id + half]
            m = jnp.maximum(jnp.max(s, axis=1, keepdims=True), sink)
            p = jnp.exp(s - m)
            denom = jnp.sum(p, axis=1, keepdims=True) + jnp.exp(sink - m)
            outs.append(jnp.dot(p.astype(BF), vw, preferred_element_type=F32) / denom)
            lses.append(m + jnp.log(denom))
        hm0 = _half_mask((tq, LANES), 0)
        o_ref[...] = jnp.where(hm0, outs[0], outs[1])
        lse_ref[...] = jnp.where(hm0, lses[0], lses[1])

    tile = pl.BlockSpec((tq, LANES), lambda p, i: (i, p))
    return pl.pallas_call(
        kern, name="swa_fwd", grid=(A_Q_HEADS // 2, S // tq),
        in_specs=[tile, pl.BlockSpec((S, LANES), lambda p, i: (0, A_Q_HEADS // 2)),
                  pl.BlockSpec((S, LANES), lambda p, i: (0, v_col)),
                  pl.BlockSpec(memory_space=pltpu.SMEM)],
        out_specs=[tile, tile],
        out_shape=[jax.ShapeDtypeStruct((S, A_Q_HEADS * HEAD_DIM), F32)] * 2,
        compiler_params=_cparams("parallel", "arbitrary"),
    )(qk, qk, v_arr, sinks)


def _swa_bwd(qk, v_arr, v_col, o_arr, do_arr, lse_arr, sinks):
    S = qk.shape[0]
    tq = min(SWA_TQ, S - WINDOW)
    win = tq + WINDOW
    n_pairs = A_Q_HEADS // 2

    def kern(q_ref, k_ref, v_ref, o_ref, do_ref, lse_ref, sink_ref, dq_ref, dk_ref, dv_ref, dsink_ref):
        p_id, i = pl.program_id(0), pl.program_id(1)

        @pl.when((p_id == 0) & (i == 0))
        def _():
            dk_ref[...] = jnp.zeros_like(dk_ref)
            dv_ref[...] = jnp.zeros_like(dv_ref)

        @pl.when(i == 0)
        def _():
            dsink_ref[...] = jnp.zeros_like(dsink_ref)

        start, offset = _swa_window(i, tq)
        wrows = pl.ds(start, win)
        kw = k_ref[wrows, :]
        vw = v_ref[wrows, :].astype(BF)
        valid = _swa_valid(offset, tq)
        q, do, o, lse2 = q_ref[...], do_ref[...], o_ref[...], lse_ref[...]
        dq = jnp.zeros((tq, LANES), F32)
        dk = jnp.zeros((win, LANES), F32)
        dv = jnp.zeros((win, LANES), F32)
        for half in (0, 1):
            hm = _half_mask((tq, LANES), half)
            lane0 = half * HEAD_DIM
            qh = (jnp.where(hm, q, 0).astype(F32) * QK_SCALE).astype(BF)
            do_f = jnp.where(hm, do, 0.0)
            doh = do_f.astype(BF)
            delta = jnp.sum(do_f * o, axis=1, keepdims=True)
            lse = lse2[:, lane0:lane0 + 1]
            s = lax.dot_general(qh, kw, (((1,), (1,)), ((), ())), preferred_element_type=F32)
            p = jnp.exp(jnp.where(valid, s, NEG_INF) - lse)
            dp = lax.dot_general(doh, vw, (((1,), (1,)), ((), ())), preferred_element_type=F32)
            ds_b = (p * (dp - delta)).astype(BF)
            dv = dv + lax.dot_general(p.astype(BF), doh, (((0,), (0,)), ((), ())), preferred_element_type=F32)
            dk = dk + lax.dot_general(ds_b, qh, (((0,), (0,)), ((), ())), preferred_element_type=F32)
            kh = jnp.where(_half_mask((win, LANES), half), kw, 0)
            dq = dq + jnp.dot(ds_b, kh, preferred_element_type=F32)
            p_sink = jnp.exp(sink_ref[2 * p_id + half] - lse)
            dsink_ref[0, half:half + 1, :] += jnp.broadcast_to(
                -jnp.sum(p_sink * delta, axis=0, keepdims=True), (1, LANES))
        dq_ref[...] = dq * QK_SCALE
        dk_ref[wrows, :] += dk
        dv_ref[wrows, :] += dv

    tile = pl.BlockSpec((tq, LANES), lambda p, i: (i, p))
    whole = lambda col: pl.BlockSpec((S, LANES), lambda p, i: (0, col))
    return pl.pallas_call(
        kern, name="swa_bwd", grid=(n_pairs, S // tq),
        in_specs=[tile, whole(n_pairs), whole(v_col), tile, tile, tile, pl.BlockSpec(memory_space=pltpu.SMEM)],
        out_specs=[tile, whole(0), whole(0), pl.BlockSpec((1, 8, LANES), lambda p, i: (p, 0, 0))],
        out_shape=[jax.ShapeDtypeStruct((S, A_Q_HEADS * HEAD_DIM), F32),
                   jax.ShapeDtypeStruct((S, LANES), F32), jax.ShapeDtypeStruct((S, LANES), F32),
                   jax.ShapeDtypeStruct((n_pairs, 8, LANES), F32)],
        compiler_params=_cparams("arbitrary", "arbitrary"),
    )(qk, qk, v_arr, o_arr, do_arr, lse_arr, sinks)


def _adamw(w, g, m, v, name):
    R, C = w.shape
    tr = _tile(R, 256, 8)

    def kern(w_ref, g_ref, m_ref, v_ref, d_ref, mo_ref, vo_ref):
        g_ = g_ref[...]
        m_new = ADAM_B1 * m_ref[...] + (1.0 - ADAM_B1) * g_
        v_new = ADAM_B2 * v_ref[...] + (1.0 - ADAM_B2) * (g_ * g_)
        m_hat = m_new / (1.0 - ADAM_B1 ** ADAM_STEP)
        v_hat = v_new / (1.0 - ADAM_B2 ** ADAM_STEP)
        d_ref[...] = -ADAM_LR * (m_hat / (jnp.sqrt(v_hat) + ADAM_EPS) + ADAM_WD * w_ref[...])
        mo_ref[...] = m_new
        vo_ref[...] = v_new

    spec = pl.BlockSpec((tr, C), lambda i: (i, 0))
    shape = jax.ShapeDtypeStruct((R, C), F32)
    return pl.pallas_call(
        kern, name=name, grid=(R // tr,),
        in_specs=[spec] * 4, out_specs=[spec] * 3, out_shape=[shape] * 3,
        compiler_params=_cparams("parallel"),
    )(w, g, m, v)


def _add_pair(a, b, name):
    R, C = a.shape
    tr = _tile(R, 256, 8)

    def kern(a_ref, b_ref, o_ref, ob_ref):
        s = a_ref[...] + b_ref[...].astype(F32)
        o_ref[...] = s
        ob_ref[...] = s.astype(BF)

    spec = pl.BlockSpec((tr, C), lambda i: (i, 0))
    return pl.pallas_call(
        kern, name=name, grid=(R // tr,),
        in_specs=[spec] * 2, out_specs=[spec] * 2,
        out_shape=[jax.ShapeDtypeStruct((R, C), F32), jax.ShapeDtypeStruct((R, C), BF)],
        compiler_params=_cparams("parallel"),
    )(a, b)


def _add_three(own, recv, name):
    R, C = own.shape
    tr = _tile(R, 256, 8)

    def kern(o_ref, r0_ref, r1_ref, r2_ref, out_ref):
        s = ((o_ref[...] + r0_ref[...].astype(F32)) + r1_ref[...].astype(F32)) + r2_ref[...].astype(F32)
        out_ref[0] = s
        out_ref[1] = s

    slab = lambda k: pl.BlockSpec((None, tr, C), lambda i: (k, i, 0))
    return pl.pallas_call(
        kern, name=name, grid=(R // tr,),
        in_specs=[pl.BlockSpec((tr, C), lambda i: (i, 0)), slab(0), slab(1), slab(2)],
        out_specs=pl.BlockSpec((2, tr, C), lambda i: (0, i, 0)),
        out_shape=jax.ShapeDtypeStruct((2, R, C), F32),
        compiler_params=_cparams("parallel"),
    )(own, recv, recv, recv)


SM_ADA, SM_G, SM_LOSS, SM_BF, SM_SINK, SM_LEN = 0, 6144, 10240, 11264, 11272, 12288


def _small_finalize(gathered):
    def kern(g_ref, tot_ref, loss_ref):
        tot = g_ref[0:1, :]
        for b in range(1, N_DEV):
            tot = tot + g_ref[b:b + 1, :]
        tot_ref[...] = tot
        sq = jnp.sum(tot[:, SM_LOSS:SM_LOSS + D_MODEL], axis=1, keepdims=True)
        loss_ref[...] = jnp.broadcast_to(sq * (0.5 / D_MODEL), (1, LANES))

    full = lambda shape: pl.BlockSpec(shape, lambda i: (0, 0))
    return pl.pallas_call(
        kern, name="small_finalize", grid=(1,),
        in_specs=[full((N_DEV, SM_LEN))],
        out_specs=[full((1, SM_LEN)), full((1, LANES))],
        out_shape=[jax.ShapeDtypeStruct((1, SM_LEN), F32), jax.ShapeDtypeStruct((1, LANES), F32)],
        compiler_params=_cparams("arbitrary"),
    )(gathered)


def _ada_dw(c_t, d_ada):
    N = d_ada.shape[1]
    tn = _tile(N, 512)

    def kern(c_ref, d_ref, o_ref):
        acc = c_ref[:, 0:1] * d_ref[0:1, :]
        for b in range(1, N_DEV):
            acc = acc + c_ref[:, b:b + 1] * d_ref[b:b + 1, :]
        o_ref[...] = acc

    return pl.pallas_call(
        kern, name="ada_dw", grid=(N // tn,),
        in_specs=[pl.BlockSpec((D_MODEL, N_DEV), lambda j: (0, 0)), pl.BlockSpec((N_DEV, tn), lambda j: (0, j))],
        out_specs=pl.BlockSpec((D_MODEL, tn), lambda j: (0, j)),
        out_shape=jax.ShapeDtypeStruct((D_MODEL, N), F32),
        compiler_params=_cparams("parallel"),
    )(c_t, d_ada)


def _here():
    return lax.axis_index("x"), lax.axis_index("y"), lax.axis_index("c")


def _other_chips(x, y):
    return [(1 - x, y), (x, 1 - y), (1 - x, 1 - y)]


_ANY = pl.BlockSpec(memory_space=pl.ANY)


class _Comm:
    def __init__(self, ins, out_shapes, sem_shapes, start, finish):
        self.ins, self.out_shapes, self.sem_shapes = list(ins), list(out_shapes), list(sem_shapes)
        self.start, self.finish = start, finish

    def split(self, refs, n_in, n_out, n_scratch):
        a = n_in + len(self.ins)
        b = a + n_out + len(self.out_shapes)
        own = list(refs[:n_in]) + list(refs[a:a + n_out]) + list(refs[b:b + n_scratch])
        mine = (refs[n_in:a], refs[a + n_out:b], refs[b + n_scratch:])
        return own, mine


def _run_comm(comm, name):
    n_in, n_out = len(comm.ins), len(comm.out_shapes)

    def body(*refs):
        parts = (refs[:n_in], refs[n_in:n_in + n_out], refs[n_in + n_out:])
        comm.start(*parts)
        comm.finish(*parts)

    return pl.pallas_call(
        body, name=name,
        in_specs=[_ANY] * n_in, out_specs=[_ANY] * n_out,
        out_shape=comm.out_shapes, scratch_shapes=comm.sem_shapes,
    )(*comm.ins)


def _gather_comm(blocks):
    L = len(blocks)

    def parts(ins, outs, sems):
        send_sems, recv_sems, local_sems = sems
        x, y, c = _here()
        me, sibling = (x, y, c), (x, y, 1 - c)
        chips = _other_chips(x, y)

        def slot(px, py, pc):
            return 4 * px + 2 * py + pc

        def copy(l, k, block, to, src=None):
            dst = outs[l].at[slot(*block)]
            return pltpu.make_async_remote_copy(
                src_ref=dst if src is None else src, dst_ref=dst,
                send_sem=send_sems.at[l, k], recv_sem=recv_sems.at[l, k],
                device_id=to, device_id_type=MESH)

        mine = [pltpu.make_async_copy(ins[l], outs[l].at[slot(*me)], local_sems.at[l]) for l in range(L)]
        first = []
        for l in range(L):
            first.append(copy(l, 0, me, sibling, src=ins[l]))
            for j, chip in enumerate(chips):
                first.append(copy(l, 1 + j, me, (*chip, c), src=ins[l]))
        return c, me, sibling, chips, copy, mine, first

    def start(ins, outs, sems):
        *_, mine, first = parts(ins, outs, sems)
        for cp in mine + first:
            cp.start()

    def finish(ins, outs, sems):
        c, me, sibling, chips, copy, mine, first = parts(ins, outs, sems)
        passed = []
        for j, chip in enumerate(chips):
            for l in range(L):
                copy(l, 1 + j, (*chip, c), me).wait_recv()
                fwd = copy(l, 4 + j, (*chip, c), sibling)
                fwd.start()
                passed.append(fwd)
        for l in range(L):
            copy(l, 0, sibling, me).wait_recv()
        for j, chip in enumerate(chips):
            for l in range(L):
                copy(l, 4 + j, (*chip, 1 - c), me).wait_recv()
        for cp in first + passed:
            cp.wait_send()
        for cp in mine:
            cp.wait()

    return _Comm(blocks, [jax.ShapeDtypeStruct((N_DEV,) + b.shape, b.dtype) for b in blocks],
                 [pltpu.SemaphoreType.DMA((L, 7)), pltpu.SemaphoreType.DMA((L, 7)), pltpu.SemaphoreType.DMA((L,))],
                 start, finish)


def _allgather8(blocks, name):
    return _run_comm(_gather_comm(blocks), name)


def _sibling_swap(arrs, name):
    L = len(arrs)

    def body(*refs):
        ins, outs = refs[:L], refs[L:2 * L]
        send_sems, recv_sems = refs[2 * L:]
        x, y, c = _here()
        cps = [pltpu.make_async_remote_copy(src_ref=ins[l], dst_ref=outs[l], send_sem=send_sems.at[l],
                                            recv_sem=recv_sems.at[l], device_id=(x, y, 1 - c),
                                            device_id_type=MESH) for l in range(L)]
        for cp in cps:
            cp.start()
        for cp in cps:
            cp.wait()

    return pl.pallas_call(
        body, name=name,
        in_specs=[_ANY] * L, out_specs=[_ANY] * L,
        out_shape=[jax.ShapeDtypeStruct(a.shape, a.dtype) for a in arrs],
        scratch_shapes=[pltpu.SemaphoreType.DMA((L,)), pltpu.SemaphoreType.DMA((L,))],
    )(*arrs)


def _sibling_join(bufs, name):
    L = len(bufs)

    def body(*refs):
        outs = refs[L:2 * L]
        send_sems, recv_sems = refs[2 * L:]
        x, y, c = _here()
        for l in range(L):
            pltpu.make_async_remote_copy(src_ref=outs[l].at[c], dst_ref=outs[l].at[c], send_sem=send_sems.at[l],
                                         recv_sem=recv_sems.at[l], device_id=(x, y, 1 - c),
                                         device_id_type=MESH).start()
        for l in range(L):
            pltpu.make_async_remote_copy(src_ref=outs[l].at[c], dst_ref=outs[l].at[1 - c],
                                         send_sem=send_sems.at[l], recv_sem=recv_sems.at[l],
                                         device_id=(x, y, 1 - c), device_id_type=MESH).wait()

    return pl.pallas_call(
        body, name=name,
        in_specs=[_ANY] * L, out_specs=[_ANY] * L,
        out_shape=[jax.ShapeDtypeStruct(a.shape, a.dtype) for a in bufs],
        input_output_aliases={l: l for l in range(L)},
        scratch_shapes=[pltpu.SemaphoreType.DMA((L,)), pltpu.SemaphoreType.DMA((L,))],
    )(*bufs)


def _scatter_comm(arrs):
    L = len(arrs)

    def copies(ins, outs, sems):
        send_sems, recv_sems = sems
        x, y, c = _here()
        return [pltpu.make_async_remote_copy(
            src_ref=ins[l].at[2 * tx + ty], dst_ref=outs[l].at[j],
            send_sem=send_sems.at[l, j], recv_sem=recv_sems.at[l, j],
            device_id=(tx, ty, c), device_id_type=MESH)
            for l in range(L) for j, (tx, ty) in enumerate(_other_chips(x, y))]

    def start(ins, outs, sems):
        for cp in copies(ins, outs, sems):
            cp.start()

    def finish(ins, outs, sems):
        for cp in copies(ins, outs, sems):
            cp.wait()

    return _Comm(arrs, [jax.ShapeDtypeStruct((3,) + a.shape[1:], a.dtype) for a in arrs],
                 [pltpu.SemaphoreType.DMA((L, 3)), pltpu.SemaphoreType.DMA((L, 3))], start, finish)


_A_ORDER = np.array(A_HEAD_ORDER)
_A_INVERSE = np.argsort(_A_ORDER)


def _permute_in_weights(w_in):
    qa = w_in[:, 0:512].reshape(D_MODEL, A_Q_HEADS, HEAD_DIM)[:, _A_ORDER, :].reshape(D_MODEL, 512)
    f_pad = jnp.pad(w_in[:, 2304:2312], ((0, 0), (0, LANES - B_HEADS)))
    w_a = jnp.concatenate([qa, w_in[:, 512:640], f_pad], axis=1)
    return w_a, w_in[:, 640:2304], w_in[:, 2312:4360]


def _unpermute_in_grads(dw_perm):
    qa = dw_perm[:, 0:512].reshape(D_MODEL, A_Q_HEADS, HEAD_DIM)[:, _A_INVERSE, :].reshape(D_MODEL, 512)
    return jnp.concatenate([qa, dw_perm[:, 512:640], dw_perm[:, W_A:W_A + W_B],
                            dw_perm[:, OFF_F:OFF_F + B_HEADS], dw_perm[:, W_A + W_B:]], axis=1)


class _NoExchange:
    def __init__(self, rest):
        self.rest, self.grads = rest, {}

    def rest_weights_comm(self):
        return None

    def rest_weights(self, outs):
        return self.rest

    def reduce_comm(self, dws, by_cols, tag):
        self.grads[tag] = dws
        return None

    def reduce_done(self, outs, tag):
        pass


class _Exchange:
    def __init__(self, ci, chip, rest_shards):
        self.ci, self.chip, self.rest_shards = ci, chip, rest_shards
        self.part_f32, self.halves = {}, {}

    def _my_half(self, a, axis=0, other=False):
        rows = a.shape[axis] // 2
        return lax.dynamic_slice_in_dim(a, ((1 - self.ci) if other else self.ci) * rows, rows, axis=axis)

    def rest_weights_comm(self):
        return _gather_comm([self._my_half(w).astype(BF) for w in self.rest_shards])

    def rest_weights(self, outs):
        w_ba, w_bb, w_out, w_fi, w_fo = outs
        return (_col_sharded(w_ba), _col_sharded(w_bb), _row_sharded(w_out), _col_sharded(w_fi),
                _row_sharded(w_fo))

    def reduce_comm(self, dws, by_cols, tag):
        keep, send = [], []
        for dw, bc in zip(dws, by_cols):
            if bc:
                p = jnp.transpose(dw.reshape(dw.shape[0], N_CHIP, dw.shape[1] // N_CHIP), (1, 0, 2))
            else:
                p = dw.reshape(N_CHIP, dw.shape[0] // N_CHIP, dw.shape[1])
            keep.append(self._my_half(p, axis=1))
            send.append(self._my_half(p, axis=1, other=True).astype(BF))
        got = _sibling_swap(send, f"grads_to_sibling_{tag}")
        self.part_f32[tag], part_bf = [], []
        for l, (k_, g_) in enumerate(zip(keep, got)):
            shp = k_.shape
            s32, sbf = _add_pair(k_.reshape(-1, shp[-1]), g_.reshape(-1, shp[-1]), f"chip_sum_{tag}_{l}")
            self.part_f32[tag].append(s32.reshape(shp))
            part_bf.append(sbf.reshape(shp))
        return _scatter_comm(part_bf)

    def reduce_done(self, outs, tag):
        self.halves[tag] = [
            _add_three(lax.dynamic_index_in_dim(p32, self.chip, axis=0, keepdims=False), r, f"shard_sum_{tag}_{l}")
            for l, (p32, r) in enumerate(zip(self.part_f32[tag], outs))]


def _col_sharded(g):
    return jnp.transpose(g.reshape(N_CHIP, -1, g.shape[-1]), (1, 0, 2)).reshape(2 * g.shape[1], N_CHIP * g.shape[-1])


def _row_sharded(g):
    return g.reshape(N_DEV * g.shape[1], g.shape[-1])


def _rope_tables(pos):
    inv_freq = 1.0 / (ROPE_THETA ** (jnp.arange(0, HEAD_DIM, 2, dtype=F32) / HEAD_DIM))
    ang = pos.astype(F32)[:, None] * inv_freq
    cos, sin = jnp.cos(ang), jnp.sin(ang)
    return jnp.tile(cos, (1, 4)), jnp.tile(jnp.concatenate([-sin, sin], axis=1), (1, 2))


def _local_step(x, pos, ada, g1, g2, g3, g4, b_f, sinks, w_in, exch, target):
    S = x.shape[0]
    t_fox = _tile(S, 512, LANES) if S >= 1024 else S // 2
    shift_m, scale_m, gate_m, shift_f, scale_f, gate_f = [ada[i:i + 1] for i in range(N_ADA)]
    cos_t, sin_t = _rope_tables(pos)
    w_a, w_b, w_g = _permute_in_weights(w_in)
    w_perm = jnp.concatenate([w_a, w_b, w_g], axis=1)
    sinks_p = sinks.reshape(A_KV_HEADS, 4).T.reshape(A_Q_HEADS)
    b_f_pad = jnp.pad(b_f, (0, LANES - B_HEADS)).reshape(1, LANES)

    h1 = _pre_norm(x, g1, scale_m, shift_m, "pre_mix_norm")
    p_a = _mm(h1, w_a, "nn", F32, "proj_a")
    p_b = _mm(h1, w_b, "nn", BF, "proj_b")
    p_g = _mm(h1, w_g, "nn", BF, "proj_g")
    (qk_a,) = _rope([p_a], [640], cos_t, sin_t, "rope_fwd")
    o_a, lse_a = _swa_fwd(qk_a, p_b, 0, sinks_p)
    cum = _fox_gate_fwd(p_a, b_f_pad)[:, :B_HEADS]
    cq_arr = jnp.repeat(cum, HEAD_DIM, axis=1)
    ck_arr = cum.T.reshape(4, 2, S)
    comm = exch.rest_weights_comm()
    res = _attn_fwd(p_b, 1, p_b, 5, p_b, 9, 4, False, t_fox, None, cq_arr, ck_arr, None, "fox_fwd", comm=comm)
    o_b, lse_b = res[0], res[1]
    w_ba, w_bb, w_out, w_fi, w_fo = exch.rest_weights(res[2] if comm else None)
    w_ba_p = w_ba.reshape(A_Q_HEADS, HEAD_DIM, D_MODEL)[_A_ORDER].reshape(512, D_MODEL)
    pa = _mm(o_a, w_ba_p, "nn", F32, "branch_a")
    pb = _mm(o_b, w_bb, "nn", F32, "branch_b")
    merged = _merge_fwd(p_g, pa, pb)
    y1 = _mm(merged, w_out, "nn", F32, "out_proj")
    x2, h2 = _post_pre(x, y1, g2, gate_m, g3, scale_f, shift_f)
    gu = _mm(h2, w_fi, "nn", BF, "ffn_in")
    act = _swiglu_fwd(gu)
    y2 = _mm(act, w_fo, "nn", F32, "ffn_out")
    d_out, d_y2, st_f = _final(x2, y2, g4, gate_f, target)

    d_act = _mm(d_y2, w_fo, "nt", F32, "ffn_out_dx")
    dw_fo = _mm(act, d_y2, "tn", F32, "ffn_out_dw")
    d_gu = _swiglu_bwd(d_act, gu)
    d_h2 = _mm(d_gu, w_fi, "nt", F32, "ffn_in_dx")
    dw_fi = _mm(h2, d_gu, "tn", F32, "ffn_in_dw")
    d_x2, d_y1, st_m = _mid_bwd(d_h2, x2, d_out, y1, g3, scale_f, g2, gate_m)
    d_merged = _mm(d_y1, w_out, "nt", F32, "out_proj_dx")
    dw_out = _mm(merged, d_y1, "tn", F32, "out_proj_dw")
    d_pa, d_pb, d_ga, d_gb = _merge_bwd(d_merged, p_g, pa, pb)
    d_oa = _mm(d_pa, w_ba_p, "nt", F32, "branch_a_dx")
    dw_ba_p = _mm(o_a, d_pa, "tn", F32, "branch_a_dw")
    d_ob = _mm(d_pb, w_bb, "nt", F32, "branch_b_dx")
    dw_bb = _mm(o_b, d_pb, "tn", F32, "branch_b_dw")
    dq_a, dk_a, dv_a, d_sink = _swa_bwd(qk_a, p_b, 0, o_a, d_oa, lse_a, sinks_p)
    dw_ba = dw_ba_p.reshape(A_Q_HEADS, HEAD_DIM, D_MODEL)[_A_INVERSE].reshape(512, D_MODEL)
    comm = exch.reduce_comm([dw_ba, dw_bb, dw_out, dw_fi, dw_fo], [True, True, False, True, False], "early")
    res = _attn_bwd(p_b, 1, p_b, 5, p_b, 9, o_b, d_ob, lse_b, 4, False, t_fox, None,
                    cq_arr, ck_arr, None, "fox_bwd", comm=comm)
    dq_b, dk_b, dv_b, d_ck, d_cq = res[:5]
    exch.reduce_done(res[5] if comm else None, "early")
    d_qa, d_ka = _rope([dq_a, dk_a], [512, LANES], cos_t, -sin_t, "rope_bwd")
    pad8 = lambda a: jnp.pad(a, ((0, 0), (0, LANES - B_HEADS)))
    d_f, d_bf = _fox_gate_bwd(pad8(d_cq[:, ::HEAD_DIM]), pad8(d_ck.reshape(B_HEADS, S).T), p_a, b_f_pad)
    d_proj = jnp.concatenate([d_qa, d_ka, d_f, dv_a.astype(BF), dq_b.astype(BF), dk_b.astype(BF),
                              dv_b.astype(BF), d_ga, d_gb], axis=1)
    dw_perm = _mm(h1, d_proj, "tn", F32, "proj_dw")
    comm = exch.reduce_comm([_unpermute_in_grads(dw_perm)], [True], "late")
    res = _mm(d_proj, w_perm, "nt", F32, "proj_dx", comm=comm)
    d_h1 = res[0] if comm else res
    exch.reduce_done(res[1] if comm else None, "late")
    grad_x, st_p = _pre_bwd(d_h1, x, d_x2, g1, scale_m)

    d_sinks = d_sink[:, :2, 0].T.reshape(A_Q_HEADS)
    small = jnp.concatenate([
        st_p[0], st_p[1], st_m[3], st_m[0], st_m[1], st_f[0],
        st_p[2], st_m[4], st_m[2], st_f[1],
        st_f[2], d_bf[0, :B_HEADS], d_sinks,
        jnp.zeros((SM_LEN - SM_SINK - A_Q_HEADS,), F32)])
    return grad_x, small


def kernel(x, c, positions, w_ada, b_ada, g_pre_mix, g_post_mix, w_in, b_f, sinks, w_branch_a, w_branch_b, w_out, g_pre_ffn, g_post_ffn, w_ffn_in, w_ffn_out, loss_target, m_w_ada, m_b_ada, m_g_pre_mix, m_g_post_mix, m_w_in, m_b_f, m_sinks, m_w_branch_a, m_w_branch_b, m_w_out, m_g_pre_ffn, m_g_post_ffn, m_w_ffn_in, m_w_ffn_out, v_w_ada, v_b_ada, v_g_pre_mix, v_g_post_mix, v_w_in, v_b_f, v_sinks, v_w_branch_a, v_w_branch_b, v_w_out, v_g_pre_ffn, v_g_post_ffn, v_w_ffn_in, v_w_ffn_out):
    xi, yi, ci = _here()
    chip = 2 * xi + yi
    dev = 2 * chip + ci

    def my_half(a):
        rows = a.shape[0] // 2
        return lax.dynamic_slice_in_dim(a, ci * rows, rows, axis=0)

    c_g, w_in_g = _allgather8([c.reshape(8, LANES), my_half(w_in[0]).astype(BF)], "gather_w_in")
    c_all = c_g.reshape(N_DEV, D_MODEL)
    w_in_f = _col_sharded(w_in_g)
    exch = _Exchange(ci, chip, [w_branch_a[0], w_branch_b[0], w_out[0], w_ffn_in[0], w_ffn_out[0]])

    ada_cols = _mm(c_all, w_ada[0], "nn", F32, "ada_fwd")
    (ada_g,) = _allgather8([ada_cols], "gather_ada")
    ada_mine = lax.dynamic_index_in_dim(ada_g.reshape(N_CHIP, 2, N_DEV, -1)[:, 0], dev, axis=1, keepdims=False)
    ada = (ada_mine.reshape(-1) + b_ada[0]).reshape(N_ADA, D_MODEL)

    grad_x, small = _local_step(
        x[0], positions[0], ada, g_pre_mix, g_post_mix, g_pre_ffn, g_post_ffn, b_f[0], sinks[0],
        w_in_f, exch, loss_target[0])

    (small_g,) = _allgather8([small.reshape(8, SM_LEN // 8)], "gather_small")
    small_all = small_g.reshape(N_DEV, SM_LEN)
    small_tot, loss_row = _small_finalize(small_all)
    loss = loss_row[0, 0]
    d_ada_cols = lax.dynamic_slice_in_dim(small_all[:, :N_ADA * D_MODEL], chip * (N_ADA * D_MODEL // N_CHIP),
                                          N_ADA * D_MODEL // N_CHIP, axis=1)
    g_w_ada = _ada_dw(c_all.T, d_ada_cols)

    joined = _sibling_join(exch.halves["late"] + exch.halves["early"], "grads_join")
    g_w_in, g_w_ba, g_w_bb, g_w_out, g_w_fi, g_w_fo = [j.reshape(2 * j.shape[1], j.shape[2]) for j in joined]

    def small_vec(b_ada_, g1_, g2_, g3_, g4_, b_f_, sinks_):
        return jnp.concatenate([b_ada_[0], g1_[0], g2_[0], g3_[0], g4_[0], jnp.zeros((D_MODEL,), F32),
                                b_f_[0], sinks_[0], jnp.zeros((SM_LEN - SM_SINK - A_Q_HEADS,), F32)]
                               ).reshape(8, SM_LEN // 8)

    sw = small_vec(b_ada, g_pre_mix, g_post_mix, g_pre_ffn, g_post_ffn, b_f, sinks)
    sm = small_vec(m_b_ada, m_g_pre_mix, m_g_post_mix, m_g_pre_ffn, m_g_post_ffn, m_b_f, m_sinks)
    sv = small_vec(v_b_ada, v_g_pre_mix, v_g_post_mix, v_g_pre_ffn, v_g_post_ffn, v_b_f, v_sinks)
    s_upd = [u.reshape(SM_LEN) for u in _adamw(sw, small_tot.reshape(8, SM_LEN // 8), sm, sv, "adamw_small")]
    s_grad = small_tot.reshape(SM_LEN)

    def unpack(vec):
        row = lambda a, n: vec[a:a + n].reshape(1, n)
        return dict(b_ada=row(SM_ADA, N_ADA * D_MODEL), g_pre_mix=row(SM_G, D_MODEL),
                    g_post_mix=row(SM_G + D_MODEL, D_MODEL), g_pre_ffn=row(SM_G + 2 * D_MODEL, D_MODEL),
                    g_post_ffn=row(SM_G + 3 * D_MODEL, D_MODEL), b_f=row(SM_BF, B_HEADS),
                    sinks=row(SM_SINK, A_Q_HEADS))

    big = dict(
        w_ada=(w_ada, g_w_ada, m_w_ada, v_w_ada), w_in=(w_in, g_w_in, m_w_in, v_w_in),
        w_branch_a=(w_branch_a, g_w_ba, m_w_branch_a, v_w_branch_a),
        w_branch_b=(w_branch_b, g_w_bb, m_w_branch_b, v_w_branch_b),
        w_out=(w_out, g_w_out, m_w_out, v_w_out), w_ffn_in=(w_ffn_in, g_w_fi, m_w_ffn_in, v_w_ffn_in),
        w_ffn_out=(w_ffn_out, g_w_fo, m_w_ffn_out, v_w_ffn_out))
    grads, deltas, new_m, new_v = unpack(s_grad), unpack(s_upd[0]), unpack(s_upd[1]), unpack(s_upd[2])
    for n, (w_, g_, m_, v_) in big.items():
        d_, nm_, nv_ = _adamw(w_[0], g_, m_[0], v_[0], "adamw_" + n)
        grads[n], deltas[n], new_m[n], new_v[n] = g_[None], d_[None], nm_[None], nv_[None]

    names = ["w_ada", "b_ada", "g_pre_mix", "g_post_mix", "w_in", "b_f", "sinks", "w_branch_a", "w_branch_b",
             "w_out", "g_pre_ffn", "g_post_ffn", "w_ffn_in", "w_ffn_out"]
    return (loss, grad_x[None], *[grads[n] for n in names], *[deltas[n] for n in names],
            *[new_m[n] for n in names], *[new_v[n] for n in names])
```

```python
import functools
import math

import numpy as np
import jax
import jax.numpy as jnp
from jax import lax
from jax.experimental import pallas as pl
from jax.experimental.pallas import tpu as pltpu

F32 = jnp.float32
BF = jnp.bfloat16

D_MODEL = 1024
HEAD_DIM = 64
LANES = 128
WINDOW = 128
A_Q_HEADS = 8
A_KV_HEADS = 2
B_HEADS = 8
D_FF = 2816
ROPE_THETA = 10000.0
RMS_EPS = 1e-6
N_ADA = 6
N_DEV = 8
N_CHIP = 4

ADAM_LR = 0.001
ADAM_B1 = 0.9
ADAM_B2 = 0.999
ADAM_EPS = 1e-08
ADAM_WD = 0.01
ADAM_STEP = 10

VMEM_LIMIT = 48 * 1024 * 1024
MESH = pl.DeviceIdType.MESH

A_HEAD_ORDER = (0, 4, 1, 5, 2, 6, 3, 7)

OFF_QA, OFF_KA, OFF_F = 0, 512, 640
W_A = 768
OFF_VA, OFF_QB, OFF_KB, OFF_VB = 0, 128, 640, 1152
W_B = 1664
W_G = 2048
W_PERM = W_A + W_B + W_G


def _tile(n, cap, mult=LANES):
    if n <= cap:
        return n
    t = (cap // mult) * mult
    while t >= mult:
        if n % t == 0:
            return t
        t -= mult
    raise ValueError(f"no tile for {n}")


MXU_WIDTH = 256
MM_OPERAND_BYTES = 28 * 1024 * 1024


def _mm_tiles(M, N, K, a_bytes, b_bytes, tm_cap, tn_cap):
    tm = _tile(M, tm_cap)
    try:
        tn = _tile(N, tn_cap, MXU_WIDTH)
    except ValueError:
        tn = _tile(N, tn_cap)
    fits = lambda tk: 2 * tk * (tm * a_bytes + tn * b_bytes) <= MM_OPERAND_BYTES
    tk = K if fits(K) else next(t for t in range(K // LANES * LANES, 0, -LANES) if K % t == 0 and fits(t))
    return tm, tn, tk


def _cparams(*sem):
    return pltpu.CompilerParams(dimension_semantics=sem, vmem_limit_bytes=VMEM_LIMIT)


def _own_refs(refs, comm, n_in, n_out, n_scratch):
    if comm is None:
        return list(refs), None
    return comm.split(refs, n_in, n_out, n_scratch)


def _comm_specs(comm, side):
    if comm is None:
        return []
    return [pl.BlockSpec(memory_space=pl.ANY)] * len(comm.ins if side == "in" else comm.out_shapes)


def _comm_edge(comm, comm_refs, grid, first):
    if comm is None:
        return
    at_edge = None
    for axis, n in enumerate(grid):
        here = pl.program_id(axis) == (0 if first else n - 1)
        at_edge = here if at_edge is None else at_edge & here
    pl.when(at_edge)(lambda: (comm.start if first else comm.finish)(*comm_refs))


def _mm(a, b, mode, out_dtype, name, tm_cap=512, tn_cap=2816, comm=None, col_pieces=1, twin=False):
    if mode == "nn":
        (M, K), (K2, N) = a.shape, b.shape
        dims = (((1,), (0,)), ((), ()))
    elif mode == "nt":
        (M, K), (N, K2) = a.shape, b.shape
        dims = (((1,), (1,)), ((), ()))
    else:
        (K, M), (K2, N) = a.shape, b.shape
        dims = (((0,), (0,)), ((), ()))
    assert K == K2, (a.shape, b.shape, mode)
    tm, tn, tk = _mm_tiles(M, N // col_pieces, K, a.dtype.itemsize, b.dtype.itemsize, tm_cap, tn_cap)
    nk = K // tk
    n_out = 2 if twin else 1
    n_scratch = 1 if nk > 1 else 0
    if mode == "nn":
        a_spec = pl.BlockSpec((tm, tk), lambda i, j, k: (i, k))
        b_spec = pl.BlockSpec((tk, tn), lambda i, j, k: (k, j))
    elif mode == "nt":
        a_spec = pl.BlockSpec((tm, tk), lambda i, j, k: (i, k))
        b_spec = pl.BlockSpec((tn, tk), lambda i, j, k: (j, k))
    else:
        a_spec = pl.BlockSpec((tk, tm), lambda i, j, k: (k, i))
        b_spec = pl.BlockSpec((tk, tn), lambda i, j, k: (k, j))

    grid = (M // tm, N // tn, nk)

    def kern(*refs):
        own, comm_refs = _own_refs(refs, comm, 2, n_out, n_scratch)
        a_ref, b_ref, o_refs = own[0], own[1], own[2:2 + n_out]
        k = pl.program_id(2)
        _comm_edge(comm, comm_refs, grid, first=True)
        part = lax.dot_general(a_ref[...].astype(BF), b_ref[...].astype(BF), dims,
                               preferred_element_type=F32)
        if nk == 1:
            for o_ref in o_refs:
                o_ref[...] = part.astype(o_ref.dtype)
        else:
            acc_ref = own[2 + n_out]

            @pl.when(k == 0)
            def _():
                acc_ref[...] = part

            @pl.when(k > 0)
            def _():
                acc_ref[...] += part

            @pl.when(k == nk - 1)
            def _():
                for o_ref in o_refs:
                    o_ref[...] = acc_ref[...].astype(o_ref.dtype)

        _comm_edge(comm, comm_refs, grid, first=False)

    if col_pieces > 1:
        per = N // col_pieces // tn
        out_spec = pl.BlockSpec((None, tm, tn), lambda i, j, k: (j // per, i, j % per))
        shape = (col_pieces, M, N // col_pieces)
    else:
        out_spec = pl.BlockSpec((tm, tn), lambda i, j, k: (i, j))
        shape = (M, N)
    dtypes = [out_dtype, BF] if twin else [out_dtype]
    res = pl.pallas_call(
        kern, name=name, grid=grid,
        in_specs=[a_spec, b_spec] + _comm_specs(comm, "in"),
        out_specs=[out_spec] * n_out + _comm_specs(comm, "out"),
        out_shape=[jax.ShapeDtypeStruct(shape, d) for d in dtypes] + (comm.out_shapes if comm else []),
        scratch_shapes=[pltpu.VMEM((tm, tn), F32)] * n_scratch + (comm.sem_shapes if comm else []),
        compiler_params=_cparams("parallel", "parallel", "arbitrary"),
    )(a, b, *(comm.ins if comm else []))
    own = res[0] if n_out == 1 else tuple(res[:n_out])
    return (own, res[n_out:]) if comm else own


ROWS = 256


def _row_spec(tm, width=D_MODEL, col=0):
    return pl.BlockSpec((tm, width), lambda i: (i, col))


def _vec_spec(width=D_MODEL):
    return pl.BlockSpec((1, width), lambda i: (0, 0))


def _rms(x):
    return lax.rsqrt(jnp.mean(x * x, axis=-1, keepdims=True) + RMS_EPS)


def _colsum(x):
    return jnp.sum(x, axis=0, keepdims=True)


def _norm_bwd(d_xn, xn, r):
    return r * (d_xn - xn * jnp.mean(d_xn * xn, axis=-1, keepdims=True))


def _pre_norm(x, g, scale, shift, name):
    S = x.shape[0]
    tm = _tile(S, ROWS, 8)

    def kern(x_ref, g_ref, sc_ref, sh_ref, h_ref):
        xf = x_ref[...]
        y = xf * _rms(xf) * g_ref[...]
        h_ref[...] = (y * (1.0 + sc_ref[...]) + sh_ref[...]).astype(BF)

    return pl.pallas_call(
        kern, name=name, grid=(S // tm,),
        in_specs=[_row_spec(tm), _vec_spec(), _vec_spec(), _vec_spec()],
        out_specs=_row_spec(tm),
        out_shape=jax.ShapeDtypeStruct((S, D_MODEL), BF),
        compiler_params=_cparams("parallel"),
    )(x, g, scale, shift)


def _post_pre(x, y1, g2, gate_m, g3, scale_f, shift_f):
    S = x.shape[0]
    tm = _tile(S, ROWS, 8)

    def kern(x_ref, y_ref, g2_ref, gm_ref, g3_ref, sc_ref, sh_ref, x2_ref, h2_ref):
        y = y_ref[...]
        n2 = y * _rms(y) * g2_ref[...]
        x2 = x_ref[...] + gm_ref[...] * n2
        x2_ref[...] = x2
        n3 = x2 * _rms(x2) * g3_ref[...]
        h2_ref[...] = (n3 * (1.0 + sc_ref[...]) + sh_ref[...]).astype(BF)

    return pl.pallas_call(
        kern, name="post_mix_pre_ffn", grid=(S // tm,),
        in_specs=[_row_spec(tm), _row_spec(tm)] + [_vec_spec()] * 5,
        out_specs=[_row_spec(tm), _row_spec(tm)],
        out_shape=[jax.ShapeDtypeStruct((S, D_MODEL), F32), jax.ShapeDtypeStruct((S, D_MODEL), BF)],
        compiler_params=_cparams("parallel"),
    )(x, y1, g2, gate_m, g3, scale_f, shift_f)


def _stats_spec():
    return pl.BlockSpec((8, D_MODEL), lambda i: (0, 0))


def _final(x2, y2, g4, gate_f, target):
    S = x2.shape[0]
    tm = _tile(S, ROWS, 8)

    def kern(x2_ref, y_ref, g4_ref, gf_ref, t_ref, dout_ref, dy_ref, st_ref):
        @pl.when(pl.program_id(0) == 0)
        def _():
            st_ref[...] = jnp.zeros_like(st_ref)

        y = y_ref[...]
        r = _rms(y)
        yn = y * r
        n4 = yn * g4_ref[...]
        diff = x2_ref[...] + gf_ref[...] * n4 - t_ref[...]
        d_out = diff / D_MODEL
        dout_ref[...] = d_out
        dn = d_out * gf_ref[...]
        dy_ref[...] = _norm_bwd(dn * g4_ref[...], yn, r).astype(BF)
        st_ref[0:1, :] += _colsum(d_out * n4)
        st_ref[1:2, :] += _colsum(dn * yn)
        st_ref[2:3, :] += _colsum(diff * diff)

    return pl.pallas_call(
        kern, name="final_loss", grid=(S // tm,),
        in_specs=[_row_spec(tm), _row_spec(tm), _vec_spec(), _vec_spec(), _row_spec(tm)],
        out_specs=[_row_spec(tm), _row_spec(tm), _stats_spec()],
        out_shape=[jax.ShapeDtypeStruct((S, D_MODEL), F32), jax.ShapeDtypeStruct((S, D_MODEL), BF),
                   jax.ShapeDtypeStruct((8, D_MODEL), F32)],
        compiler_params=_cparams("arbitrary"),
    )(x2, y2, g4, gate_f, target)


def _mid_bwd(d_h2, x2, d_out, y1, g3, scale_f, g2, gate_m):
    S = x2.shape[0]
    tm = _tile(S, ROWS, 8)

    def kern(dh_ref, x2_ref, dout_ref, y_ref, g3_ref, sc_ref, g2_ref, gm_ref, dx2_ref, dy_ref, st_ref):
        @pl.when(pl.program_id(0) == 0)
        def _():
            st_ref[...] = jnp.zeros_like(st_ref)

        dh = dh_ref[...]
        x2 = x2_ref[...]
        r3 = _rms(x2)
        xn = x2 * r3
        one_sc = 1.0 + sc_ref[...]
        d_x2 = dout_ref[...] + _norm_bwd(dh * one_sc * g3_ref[...], xn, r3)
        dx2_ref[...] = d_x2
        y = y_ref[...]
        r2 = _rms(y)
        yn = y * r2
        dn = d_x2 * gm_ref[...]
        dy_ref[...] = _norm_bwd(dn * g2_ref[...], yn, r2).astype(BF)
        st_ref[0:1, :] += _colsum(dh)
        st_ref[1:2, :] += _colsum(dh * (xn * g3_ref[...]))
        st_ref[2:3, :] += _colsum(dh * one_sc * xn)
        st_ref[3:4, :] += _colsum(d_x2 * (yn * g2_ref[...]))
        st_ref[4:5, :] += _colsum(dn * yn)

    return pl.pallas_call(
        kern, name="mid_bwd", grid=(S // tm,),
        in_specs=[_row_spec(tm)] * 4 + [_vec_spec()] * 4,
        out_specs=[_row_spec(tm), _row_spec(tm), _stats_spec()],
        out_shape=[jax.ShapeDtypeStruct((S, D_MODEL), F32), jax.ShapeDtypeStruct((S, D_MODEL), BF),
                   jax.ShapeDtypeStruct((8, D_MODEL), F32)],
        compiler_params=_cparams("arbitrary"),
    )(d_h2, x2, d_out, y1, g3, scale_f, g2, gate_m)


def _pre_bwd(d_h1, x, d_x2, g1, scale_m):
    S = x.shape[0]
    tm = _tile(S, ROWS, 8)

    def kern(dh_ref, x_ref, dx2_ref, g_ref, sc_ref, gx_ref, st_ref):
        @pl.when(pl.program_id(0) == 0)
        def _():
            st_ref[...] = jnp.zeros_like(st_ref)

        dh = dh_ref[...]
        xf = x_ref[...]
        r = _rms(xf)
        xn = xf * r
        one_sc = 1.0 + sc_ref[...]
        gx_ref[...] = dx2_ref[...] + _norm_bwd(dh * one_sc * g_ref[...], xn, r)
        st_ref[0:1, :] += _colsum(dh)
        st_ref[1:2, :] += _colsum(dh * (xn * g_ref[...]))
        st_ref[2:3, :] += _colsum(dh * one_sc * xn)

    return pl.pallas_call(
        kern, name="pre_mix_bwd", grid=(S // tm,),
        in_specs=[_row_spec(tm)] * 3 + [_vec_spec()] * 2,
        out_specs=[_row_spec(tm), _stats_spec()],
        out_shape=[jax.ShapeDtypeStruct((S, D_MODEL), F32), jax.ShapeDtypeStruct((8, D_MODEL), F32)],
        compiler_params=_cparams("arbitrary"),
    )(d_h1, x, d_x2, g1, scale_m)


def _rope(xs, widths, cos_t, sin_t, name):
    S = xs[0].shape[0]
    tm = _tile(S, 512, 8)
    n = len(xs)

    def kern(*refs):
        cos = refs[n][...]
        sin = refs[n + 1][...]
        first = (lax.broadcasted_iota(jnp.int32, cos.shape, 1) % HEAD_DIM) < HEAD_DIM // 2
        for x_ref, o_ref, w in zip(refs[:n], refs[n + 2:], widths):
            for c0 in range(0, w, LANES):
                v = x_ref[:, c0:c0 + LANES]
                partner = jnp.where(first, pltpu.roll(v, LANES - HEAD_DIM // 2, 1),
                                    pltpu.roll(v, HEAD_DIM // 2, 1))
                o_ref[:, c0:c0 + LANES] = (v * cos + partner * sin).astype(BF)

    return pl.pallas_call(
        kern, name=name, grid=(S // tm,),
        in_specs=[_row_spec(tm, w) for w in widths] + [_row_spec(tm, LANES)] * 2,
        out_specs=[_row_spec(tm, w) for w in widths],
        out_shape=[jax.ShapeDtypeStruct((S, w), BF) for w in widths],
        compiler_params=_cparams("parallel"),
    )(*xs, cos_t, sin_t)


def _merge_fwd(pg, pa, pb):
    S = pa.shape[0]
    tm = _tile(S, ROWS, 8)

    def kern(ga_ref, gb_ref, pa_ref, pb_ref, o_ref):
        ga = jax.nn.sigmoid(ga_ref[...].astype(F32))
        gb = jax.nn.sigmoid(gb_ref[...].astype(F32))
        o_ref[...] = (ga * pa_ref[...] + gb * pb_ref[...]).astype(BF)

    return pl.pallas_call(
        kern, name="merge_fwd", grid=(S // tm,),
        in_specs=[_row_spec(tm, col=0), _row_spec(tm, col=1), _row_spec(tm), _row_spec(tm)],
        out_specs=_row_spec(tm),
        out_shape=jax.ShapeDtypeStruct((S, D_MODEL), BF),
        compiler_params=_cparams("parallel"),
    )(pg, pg, pa, pb)


def _merge_bwd(d_merged, pg, pa, pb):
    S = pa.shape[0]
    tm = _tile(S, ROWS, 8)

    def kern(dm_ref, ga_ref, gb_ref, pa_ref, pb_ref, dpa_ref, dpb_ref, dga_ref, dgb_ref):
        dm = dm_ref[...]
        ga = jax.nn.sigmoid(ga_ref[...].astype(F32))
        gb = jax.nn.sigmoid(gb_ref[...].astype(F32))
        dpa_ref[...] = (dm * ga).astype(BF)
        dpb_ref[...] = (dm * gb).astype(BF)
        dga_ref[...] = (dm * pa_ref[...] * ga * (1.0 - ga)).astype(BF)
        dgb_ref[...] = (dm * pb_ref[...] * gb * (1.0 - gb)).astype(BF)

    bf_out = jax.ShapeDtypeStruct((S, D_MODEL), BF)
    return pl.pallas_call(
        kern, name="merge_bwd", grid=(S // tm,),
        in_specs=[_row_spec(tm), _row_spec(tm, col=0), _row_spec(tm, col=1), _row_spec(tm), _row_spec(tm)],
        out_specs=[_row_spec(tm)] * 4,
        out_shape=[bf_out] * 4,
        compiler_params=_cparams("parallel"),
    )(d_merged, pg, pg, pa, pb)


def _swiglu_fwd(gu):
    S = gu.shape[0]
    tm = _tile(S, ROWS, 8)
    tc = _tile(D_FF, 1408)
    nc = D_FF // tc

    def kern(g_ref, u_ref, o_ref):
        g = g_ref[...].astype(F32)
        o_ref[...] = (g * jax.nn.sigmoid(g) * u_ref[...].astype(F32)).astype(BF)

    return pl.pallas_call(
        kern, name="swiglu_fwd", grid=(S // tm, nc),
        in_specs=[pl.BlockSpec((tm, tc), lambda i, j: (i, j)),
                  pl.BlockSpec((tm, tc), lambda i, j: (i, j + nc))],
        out_specs=pl.BlockSpec((tm, tc), lambda i, j: (i, j)),
        out_shape=jax.ShapeDtypeStruct((S, D_FF), BF),
        compiler_params=_cparams("parallel", "parallel"),
    )(gu, gu)


def _swiglu_bwd(d_act, gu):
    S = gu.shape[0]
    tm = _tile(S, 128, 8)

    def kern(da_ref, g_ref, u_ref, o_ref):
        g = g_ref[...].astype(F32)
        u = u_ref[...].astype(F32)
        da = da_ref[...]
        sg = jax.nn.sigmoid(g)
        o_ref[:, :D_FF] = (da * u * (sg * (1.0 + g * (1.0 - sg)))).astype(BF)
        o_ref[:, D_FF:] = (da * (g * sg)).astype(BF)

    return pl.pallas_call(
        kern, name="swiglu_bwd", grid=(S // tm,),
        in_specs=[_row_spec(tm, D_FF), _row_spec(tm, D_FF, 0), _row_spec(tm, D_FF, 1)],
        out_specs=_row_spec(tm, 2 * D_FF),
        out_shape=jax.ShapeDtypeStruct((S, 2 * D_FF), BF),
        compiler_params=_cparams("parallel"),
    )(d_act, gu, gu)


def _split3(x):
    hi = x.astype(BF)
    r1 = x - hi.astype(F32)
    mid = r1.astype(BF)
    lo = (r1 - mid.astype(F32)).astype(BF)
    return hi, mid, lo


def _tri_dot(tri, x):
    return sum(jnp.dot(tri, part, preferred_element_type=F32) for part in _split3(x))


def _log_sigmoid(z):
    return jnp.minimum(z, 0.0) - jnp.log(1.0 + jnp.exp(-jnp.abs(z)))


def _fox_gate_fwd(pa, b_f_pad):
    S = pa.shape[0]
    T = _tile(S, 512, 8)
    f_col = OFF_F // LANES

    def kern(z_ref, b_ref, cum_ref, carry_ref):
        @pl.when(pl.program_id(0) == 0)
        def _():
            carry_ref[...] = jnp.zeros_like(carry_ref)

        log_f = _log_sigmoid(z_ref[...] + b_ref[...])
        row = lax.broadcasted_iota(jnp.int32, (T, T), 0)
        col = lax.broadcasted_iota(jnp.int32, (T, T), 1)
        tri = (col <= row).astype(BF)
        cum = _tri_dot(tri, log_f) + carry_ref[...]
        cum_ref[...] = cum
        carry_ref[...] = cum[T - 1:T, :]

    return pl.pallas_call(
        kern, name="fox_gate_fwd", grid=(S // T,),
        in_specs=[_row_spec(T, LANES, f_col), _vec_spec(LANES)],
        out_specs=_row_spec(T, LANES),
        out_shape=jax.ShapeDtypeStruct((S, LANES), F32),
        scratch_shapes=[pltpu.VMEM((1, LANES), F32)],
        compiler_params=_cparams("arbitrary"),
    )(pa, b_f_pad)


def _fox_gate_bwd(rowsum_ds, colsum_ds, pa, b_f_pad):
    S = pa.shape[0]
    T = _tile(S, 512, 8)
    nb = S // T
    f_col = OFF_F // LANES

    def kern(dr_ref, dc_ref, z_ref, b_ref, df_ref, dbf_ref, carry_ref):
        @pl.when(pl.program_id(0) == 0)
        def _():
            carry_ref[...] = jnp.zeros_like(carry_ref)
            dbf_ref[...] = jnp.zeros_like(dbf_ref)

        row = lax.broadcasted_iota(jnp.int32, (T, T), 0)
        col = lax.broadcasted_iota(jnp.int32, (T, T), 1)
        tri = (col >= row).astype(BF)
        rev = _tri_dot(tri, dr_ref[...] - dc_ref[...]) + carry_ref[...]
        carry_ref[...] = rev[0:1, :]
        z = z_ref[...] + b_ref[...]
        lane = lax.broadcasted_iota(jnp.int32, (T, LANES), 1)
        d_z = jnp.where(lane < B_HEADS, rev * jax.nn.sigmoid(-z), 0.0)
        df_ref[...] = d_z.astype(BF)
        dbf_ref[0:1, :] += _colsum(d_z)

    return pl.pallas_call(
        kern, name="fox_gate_bwd", grid=(nb,),
        in_specs=[pl.BlockSpec((T, LANES), lambda i: (nb - 1 - i, 0)),
                  pl.BlockSpec((T, LANES), lambda i: (nb - 1 - i, 0)),
                  pl.BlockSpec((T, LANES), lambda i: (nb - 1 - i, f_col)),
                  _vec_spec(LANES)],
        out_specs=[pl.BlockSpec((T, LANES), lambda i: (nb - 1 - i, 0)),
                   pl.BlockSpec((8, LANES), lambda i: (0, 0))],
        out_shape=[jax.ShapeDtypeStruct((S, LANES), BF), jax.ShapeDtypeStruct((8, LANES), F32)],
        scratch_shapes=[pltpu.VMEM((1, LANES), F32)],
        compiler_params=_cparams("arbitrary"),
    )(rowsum_ds, colsum_ds, pa, b_f_pad)


NEG_INF = float("-inf")
QK_SCALE = 1.0 / math.sqrt(HEAD_DIM)


def _half_mask(shape, half):
    lane = lax.broadcasted_iota(jnp.int32, shape, 1)
    return (lane < HEAD_DIM) if half == 0 else (lane >= HEAD_DIM)


def _valid(i, j, T, rowcol, window):
    rel = (i - j) * T + rowcol
    ok = rel >= 0
    if window is not None:
        ok = ok & (rel < window)
    return ok


def _attn_fwd(q_arr, q_col, k_arr, k_col, v_arr, v_col, n_pairs, kv_shared, T, window,
              cq_arr, ck_arr, sinks, name, comm=None):
    S = q_arr.shape[0]
    nq = S // T
    use_bias = cq_arr is not None
    use_sink = sinks is not None
    back = 0 if window is None else -(-window // T)
    grid = (n_pairs, nq)
    n_in = 3 + 2 * use_bias + use_sink

    def kern(*refs):
        refs, comm_refs = _own_refs(refs, comm, n_in, 2, 0)
        _comm_edge(comm, comm_refs, grid, first=True)
        q_ref, k_ref, v_ref = refs[:3]
        pos = 3
        if use_bias:
            cq_ref, ck_ref = refs[pos:pos + 2]
            pos += 2
        if use_sink:
            sink_ref = refs[pos]
            pos += 1
        o_ref, lse_ref = refs[pos:pos + 2]
        p_id = pl.program_id(0)
        i = pl.program_id(1)
        q = q_ref[...]
        rowcol = lax.broadcasted_iota(jnp.int32, (T, T), 0) - lax.broadcasted_iota(jnp.int32, (T, T), 1)
        lo = jnp.maximum(i - back, 0) if window is not None else 0
        outs, lses = [], []
        for half in (0, 1):
            hm = _half_mask((T, LANES), half)
            qh = (jnp.where(hm, q, 0).astype(F32) * QK_SCALE).astype(BF)
            if use_bias:
                cq = cq_ref[:, half * HEAD_DIM:half * HEAD_DIM + 1]
            if use_sink:
                m0 = jnp.full((T, 1), sink_ref[2 * p_id + half], F32)
                l0 = jnp.ones((T, 1), F32)
            else:
                m0 = jnp.full((T, 1), NEG_INF, F32)
                l0 = jnp.zeros((T, 1), F32)

            def step(j, carry, masked):
                m, l, acc = carry
                rows = pl.ds(pl.multiple_of(j * T, T), T)
                kj = k_ref[rows, :].astype(BF)
                vj = v_ref[rows, :].astype(BF)
                s = lax.dot_general(qh, kj, (((1,), (1,)), ((), ())), preferred_element_type=F32)
                if use_bias:
                    s = s + cq - ck_ref[0, half:half + 1, rows]
                if masked:
                    s = jnp.where(_valid(i, j, T, rowcol, window), s, NEG_INF)
                m_new = jnp.maximum(m, jnp.max(s, axis=1, keepdims=True))
                alpha = jnp.exp(m - m_new)
                p = jnp.exp(s - m_new)
                l_new = alpha * l + jnp.sum(p, axis=1, keepdims=True)
                acc_new = alpha * acc + jnp.dot(p.astype(BF), vj, preferred_element_type=F32)
                return m_new, l_new, acc_new

            init = (m0, l0, jnp.zeros((T, LANES), F32))
            if window is None:
                init = lax.fori_loop(0, i, functools.partial(step, masked=False), init)
                m, l, acc = step(i, init, True)
            else:
                m, l, acc = lax.fori_loop(lo, i + 1, functools.partial(step, masked=True), init)
            outs.append(acc / l)
            lses.append(m + jnp.log(l))
        hm0 = _half_mask((T, LANES), 0)
        o_ref[...] = jnp.where(hm0, outs[0], outs[1])
        lse_ref[...] = jnp.where(hm0, lses[0], lses[1])
        _comm_edge(comm, comm_refs, grid, first=False)

    kv_idx = (lambda c0: (lambda p, i: (0, c0))) if kv_shared else (lambda c0: (lambda p, i: (0, c0 + p)))
    in_specs = [pl.BlockSpec((T, LANES), lambda p, i: (i, q_col + p)),
                pl.BlockSpec((S, LANES), kv_idx(k_col)),
                pl.BlockSpec((S, LANES), kv_idx(v_col))]
    args = [q_arr, k_arr, v_arr]
    if use_bias:
        in_specs += [pl.BlockSpec((T, LANES), lambda p, i: (i, p)),
                     pl.BlockSpec((1, 2, S), lambda p, i: (p, 0, 0))]
        args += [cq_arr, ck_arr]
    if use_sink:
        in_specs.append(pl.BlockSpec(memory_space=pltpu.SMEM))
        args.append(sinks)
    out_spec = pl.BlockSpec((T, LANES), lambda p, i: (i, p))
    res = pl.pallas_call(
        kern, name=name, grid=grid,
        in_specs=in_specs + _comm_specs(comm, "in"),
        out_specs=[out_spec, out_spec] + _comm_specs(comm, "out"),
        out_shape=[jax.ShapeDtypeStruct((S, n_pairs * LANES), F32)] * 2 + (comm.out_shapes if comm else []),
        scratch_shapes=comm.sem_shapes if comm else [],
        compiler_params=_cparams("arbitrary", "arbitrary"),
    )(*args, *(comm.ins if comm else []))
    return (res[0], res[1], res[2:]) if comm else (res[0], res[1])


def _attn_bwd(q_arr, q_col, k_arr, k_col, v_arr, v_col, o_arr, do_arr, lse_arr, n_pairs, kv_shared, T,
              window, cq_arr, ck_arr, sinks, name, comm=None):
    S = q_arr.shape[0]
    nq = S // T
    use_bias = cq_arr is not None
    use_sink = sinks is not None
    back = 0 if window is None else -(-window // T)
    kv_w = LANES if kv_shared else n_pairs * LANES
    grid = (n_pairs,)
    n_in = 6 + 2 * use_bias + use_sink
    n_out = 3 + 2 * use_bias + use_sink

    def kern(*refs):
        refs, comm_refs = _own_refs(refs, comm, n_in, n_out, 0)
        _comm_edge(comm, comm_refs, grid, first=True)
        q_ref, k_ref, v_ref, o_ref, do_ref, lse_ref = refs[:6]
        pos = 6
        if use_bias:
            cq_ref, ck_ref = refs[pos:pos + 2]
            pos += 2
        if use_sink:
            sink_ref = refs[pos]
            pos += 1
        dq_ref, dk_ref, dv_ref = refs[pos:pos + 3]
        pos += 3
        if use_bias:
            dck_ref, dcq_ref = refs[pos:pos + 2]
            pos += 2
        if use_sink:
            dsink_ref = refs[pos]
        p_id = pl.program_id(0)
        rowcol = lax.broadcasted_iota(jnp.int32, (T, T), 0) - lax.broadcasted_iota(jnp.int32, (T, T), 1)

        def zero_kv():
            dk_ref[...] = jnp.zeros_like(dk_ref)
            dv_ref[...] = jnp.zeros_like(dv_ref)

        if kv_shared:
            pl.when(p_id == 0)(zero_kv)
        else:
            zero_kv()
        if use_bias:
            dck_ref[...] = jnp.zeros_like(dck_ref)
        if use_sink:
            dsink_ref[...] = jnp.zeros_like(dsink_ref)

        for half in (0, 1):
            hm = _half_mask((T, LANES), half)
            lane0 = half * HEAD_DIM

            def outer(i, carry):
                qrows = pl.ds(pl.multiple_of(i * T, T), T)
                qh = (jnp.where(hm, q_ref[qrows, :], 0).astype(F32) * QK_SCALE).astype(BF)
                do_f = jnp.where(hm, do_ref[qrows, :], 0.0)
                doh = do_f.astype(BF)
                delta = jnp.sum(do_f * o_ref[qrows, :], axis=1, keepdims=True)
                lse = lse_ref[qrows, lane0:lane0 + 1]
                if use_bias:
                    cq = cq_ref[qrows, lane0:lane0 + 1]
                lo = jnp.maximum(i - back, 0) if window is not None else 0

                def inner(j, carry_in, masked):
                    dq, rs = carry_in
                    krows = pl.ds(pl.multiple_of(j * T, T), T)
                    kj = k_ref[krows, :].astype(BF)
                    vj = v_ref[krows, :].astype(BF)
                    s = lax.dot_general(qh, kj, (((1,), (1,)), ((), ())), preferred_element_type=F32)
                    if use_bias:
                        s = s + cq - ck_ref[0, half:half + 1, krows]
                    if masked:
                        s = jnp.where(_valid(i, j, T, rowcol, window), s, NEG_INF)
                    p = jnp.exp(s - lse)
                    dp = lax.dot_general(doh, vj, (((1,), (1,)), ((), ())), preferred_element_type=F32)
                    ds = p * (dp - delta)
                    ds_b = ds.astype(BF)
                    dv_ref[krows, :] += lax.dot_general(p.astype(BF), doh, (((0,), (0,)), ((), ())),
                                                        preferred_element_type=F32)
                    dk_ref[krows, :] += lax.dot_general(ds_b, qh, (((0,), (0,)), ((), ())),
                                                        preferred_element_type=F32)
                    if use_bias:
                        dck_ref[0, half:half + 1, krows] += jnp.sum(ds, axis=0, keepdims=True)
                        rs = rs + jnp.sum(ds, axis=1, keepdims=True)
                    kh = jnp.where(hm, kj, 0)
                    return dq + jnp.dot(ds_b, kh, preferred_element_type=F32), rs

                init = (jnp.zeros((T, LANES), F32), jnp.zeros((T, 1), F32))
                if window is None:
                    init = lax.fori_loop(0, i, functools.partial(inner, masked=False), init)
                    dq, rs = inner(i, init, True)
                else:
                    dq, rs = lax.fori_loop(lo, i + 1, functools.partial(inner, masked=True), init)
                dq = dq * QK_SCALE
                if half == 0:
                    dq_ref[qrows, :] = dq
                else:
                    dq_ref[qrows, :] += dq
                if use_bias:
                    rs_b = jnp.broadcast_to(rs, (T, LANES))
                    dcq_ref[qrows, :] = rs_b if half == 0 else jnp.where(hm, rs_b, dcq_ref[qrows, :])
                if use_sink:
                    p_sink = jnp.exp(sink_ref[2 * p_id + half] - lse)
                    dsink_ref[0, half:half + 1, :] += jnp.broadcast_to(
                        -jnp.sum(p_sink * delta, axis=0, keepdims=True), (1, LANES))
                return carry

            lax.fori_loop(0, nq, outer, 0)
        _comm_edge(comm, comm_refs, grid, first=False)

    kv_idx = (lambda c0: (lambda p: (0, c0))) if kv_shared else (lambda c0: (lambda p: (0, c0 + p)))
    pair = lambda c0: pl.BlockSpec((S, LANES), lambda p: (0, c0 + p))
    in_specs = [pair(q_col), pl.BlockSpec((S, LANES), kv_idx(k_col)), pl.BlockSpec((S, LANES), kv_idx(v_col)),
                pair(0), pair(0), pair(0)]
    args = [q_arr, k_arr, v_arr, o_arr, do_arr, lse_arr]
    if use_bias:
        in_specs += [pair(0), pl.BlockSpec((1, 2, S), lambda p: (p, 0, 0))]
        args += [cq_arr, ck_arr]
    if use_sink:
        in_specs.append(pl.BlockSpec(memory_space=pltpu.SMEM))
        args.append(sinks)
    out_specs = [pair(0), pl.BlockSpec((S, LANES), kv_idx(0)), pl.BlockSpec((S, LANES), kv_idx(0))]
    out_shape = [jax.ShapeDtypeStruct((S, n_pairs * LANES), F32),
                 jax.ShapeDtypeStruct((S, kv_w), F32), jax.ShapeDtypeStruct((S, kv_w), F32)]
    if use_bias:
        out_specs += [pl.BlockSpec((1, 2, S), lambda p: (p, 0, 0)), pair(0)]
        out_shape += [jax.ShapeDtypeStruct((n_pairs, 2, S), F32), jax.ShapeDtypeStruct((S, n_pairs * LANES), F32)]
    if use_sink:
        out_specs.append(pl.BlockSpec((1, 8, LANES), lambda p: (p, 0, 0)))
        out_shape.append(jax.ShapeDtypeStruct((n_pairs, 8, LANES), F32))
    res = pl.pallas_call(
        kern, name=name, grid=grid,
        in_specs=in_specs + _comm_specs(comm, "in"),
        out_specs=out_specs + _comm_specs(comm, "out"),
        out_shape=out_shape + (comm.out_shapes if comm else []),
        scratch_shapes=comm.sem_shapes if comm else [],
        compiler_params=_cparams("arbitrary"),
    )(*args, *(comm.ins if comm else []))
    return (*res[:n_out], res[n_out:]) if comm else res


SWA_TQ = 256


def _swa_window(i, tq):
    start = pl.multiple_of(jnp.maximum(i * tq - WINDOW, 0), LANES)
    return start, i * tq - start


def _swa_valid(offset, tq):
    rel = offset + lax.broadcasted_iota(jnp.int32, (tq, tq + WINDOW), 0) \
        - lax.broadcasted_iota(jnp.int32, (tq, tq + WINDOW), 1)
    return (rel >= 0) & (rel < WINDOW)


def _swa_fwd(qk, v_arr, v_col, sinks):
    S = qk.shape[0]
    tq = min(SWA_TQ, S - WINDOW)
    win = tq + WINDOW

    def kern(q_ref, k_ref, v_ref, sink_ref, o_ref, lse_ref):
        p_id, i = pl.program_id(0), pl.program_id(1)
        start, offset = _swa_window(i, tq)
        kw = k_ref[pl.ds(start, win), :]
        vw = v_ref[pl.ds(start, win), :].astype(BF)
        valid = _swa_valid(offset, tq)
        q = q_ref[...]
        outs, lses = [], []
        for half in (0, 1):
            hm = _half_mask((tq, LANES), half)
            qh = (jnp.where(hm, q, 0).astype(F32) * QK_SCALE).astype(BF)
            s = lax.dot_general(qh, kw, (((1,), (1,)), ((), ())), preferred_element_type=F32)
            s = jnp.where(valid, s, NEG_INF)
            sink = sink_ref[2 * p_id + half]
            m = jnp.maximum(jnp.max(s, axis=1, keepdims=True), sink)
            p = jnp.exp(s - m)
            denom = jnp.sum(p, axis=1, keepdims=True) + jnp.exp(sink - m)
            outs.append(jnp.dot(p.astype(BF), vw, preferred_element_type=F32) / denom)
            lses.append(m + jnp.log(denom))
        hm0 = _half_mask((tq, LANES), 0)
        o_ref[...] = jnp.where(hm0, outs[0], outs[1])
        lse_ref[...] = jnp.where(hm0, lses[0], lses[1])

    tile = pl.BlockSpec((tq, LANES), lambda p, i: (i, p))
    return pl.pallas_call(
        kern, name="swa_fwd", grid=(A_Q_HEADS // 2, S // tq),
        in_specs=[tile, pl.BlockSpec((S, LANES), lambda p, i: (0, A_Q_HEADS // 2)),
                  pl.BlockSpec((S, LANES), lambda p, i: (0, v_col)),
                  pl.BlockSpec(memory_space=pltpu.SMEM)],
        out_specs=[tile, tile],
        out_shape=[jax.ShapeDtypeStruct((S, A_Q_HEADS * HEAD_DIM), F32)] * 2,
        compiler_params=_cparams("parallel", "arbitrary"),
    )(qk, qk, v_arr, sinks)


def _swa_bwd(qk, v_arr, v_col, o_arr, do_arr, lse_arr, sinks):
    S = qk.shape[0]
    tq = min(SWA_TQ, S - WINDOW)
    win = tq + WINDOW
    n_pairs = A_Q_HEADS // 2

    def kern(q_ref, k_ref, v_ref, o_ref, do_ref, lse_ref, sink_ref, dq_ref, dk_ref, dv_ref, dsink_ref):
        p_id, i = pl.program_id(0), pl.program_id(1)

        @pl.when((p_id == 0) & (i == 0))
        def _():
            dk_ref[...] = jnp.zeros_like(dk_ref)
            dv_ref[...] = jnp.zeros_like(dv_ref)

        @pl.when(i == 0)
        def _():
            dsink_ref[...] = jnp.zeros_like(dsink_ref)

        start, offset = _swa_window(i, tq)
        wrows = pl.ds(start, win)
        kw = k_ref[wrows, :]
        vw = v_ref[wrows, :].astype(BF)
        valid = _swa_valid(offset, tq)
        q, do, o, lse2 = q_ref[...], do_ref[...], o_ref[...], lse_ref[...]
        dq = jnp.zeros((tq, LANES), F32)
        dk = jnp.zeros((win, LANES), F32)
        dv = jnp.zeros((win, LANES), F32)
        for half in (0, 1):
            hm = _half_mask((tq, LANES), half)
            lane0 = half * HEAD_DIM
            qh = (jnp.where(hm, q, 0).astype(F32) * QK_SCALE).astype(BF)
            do_f = jnp.where(hm, do, 0.0)
            doh = do_f.astype(BF)
            delta = jnp.sum(do_f * o, axis=1, keepdims=True)
            lse = lse2[:, lane0:lane0 + 1]
            s = lax.dot_general(qh, kw, (((1,), (1,)), ((), ())), preferred_element_type=F32)
            p = jnp.exp(jnp.where(valid, s, NEG_INF) - lse)
            dp = lax.dot_general(doh, vw, (((1,), (1,)), ((), ())), preferred_element_type=F32)
            ds_b = (p * (dp - delta)).astype(BF)
            dv = dv + lax.dot_general(p.astype(BF), doh, (((0,), (0,)), ((), ())), preferred_element_type=F32)
            dk = dk + lax.dot_general(ds_b, qh, (((0,), (0,)), ((), ())), preferred_element_type=F32)
            kh = jnp.where(_half_mask((win, LANES), half), kw, 0)
            dq = dq + jnp.dot(ds_b, kh, preferred_element_type=F32)
            p_sink = jnp.exp(sink_ref[2 * p_id + half] - lse)
            dsink_ref[0, half:half + 1, :] += jnp.broadcast_to(
                -jnp.sum(p_sink * delta, axis=0, keepdims=True), (1, LANES))
        dq_ref[...] = dq * QK_SCALE
        dk_ref[wrows, :] += dk
        dv_ref[wrows, :] += dv

    tile = pl.BlockSpec((tq, LANES), lambda p, i: (i, p))
    whole = lambda col: pl.BlockSpec((S, LANES), lambda p, i: (0, col))
    return pl.pallas_call(
        kern, name="swa_bwd", grid=(n_pairs, S // tq),
        in_specs=[tile, whole(n_pairs), whole(v_col), tile, tile, tile, pl.BlockSpec(memory_space=pltpu.SMEM)],
        out_specs=[tile, whole(0), whole(0), pl.BlockSpec((1, 8, LANES), lambda p, i: (p, 0, 0))],
        out_shape=[jax.ShapeDtypeStruct((S, A_Q_HEADS * HEAD_DIM), F32),
                   jax.ShapeDtypeStruct((S, LANES), F32), jax.ShapeDtypeStruct((S, LANES), F32),
                   jax.ShapeDtypeStruct((n_pairs, 8, LANES), F32)],
        compiler_params=_cparams("arbitrary", "arbitrary"),
    )(qk, qk, v_arr, o_arr, do_arr, lse_arr, sinks)


def _adamw(w, g, m, v, name):
    R, C = w.shape
    tr = _tile(R, 256, 8)

    def kern(w_ref, g_ref, m_ref, v_ref, d_ref, mo_ref, vo_ref):
        g_ = g_ref[...]
        m_new = ADAM_B1 * m_ref[...] + (1.0 - ADAM_B1) * g_
        v_new = ADAM_B2 * v_ref[...] + (1.0 - ADAM_B2) * (g_ * g_)
        m_hat = m_new / (1.0 - ADAM_B1 ** ADAM_STEP)
        v_hat = v_new / (1.0 - ADAM_B2 ** ADAM_STEP)
        d_ref[...] = -ADAM_LR * (m_hat / (jnp.sqrt(v_hat) + ADAM_EPS) + ADAM_WD * w_ref[...])
        mo_ref[...] = m_new
        vo_ref[...] = v_new

    spec = pl.BlockSpec((tr, C), lambda i: (i, 0))
    shape = jax.ShapeDtypeStruct((R, C), F32)
    return pl.pallas_call(
        kern, name=name, grid=(R // tr,),
        in_specs=[spec] * 4, out_specs=[spec] * 3, out_shape=[shape] * 3,
        compiler_params=_cparams("parallel"),
    )(w, g, m, v)


def _index_operand(i):
    return jnp.reshape(i, (1,)).astype(jnp.int32)


def _add_pair(whole, got, ci, name):
    P, R, C = whole.shape
    half = R // 2
    tr = _tile(half, 256, 16)
    nb = half // tr

    def kern(ci_ref, a_ref, b_ref, o_ref, ob_ref):
        s = a_ref[...] + b_ref[...].astype(F32)
        o_ref[...] = s
        ob_ref[...] = s.astype(BF)

    spec = pl.BlockSpec((None, tr, C), lambda p, i, ci_ref: (p, i, 0))
    return pl.pallas_call(
        kern, name=name,
        grid_spec=pltpu.PrefetchScalarGridSpec(
            num_scalar_prefetch=1, grid=(P, nb),
            in_specs=[pl.BlockSpec((None, tr, C), lambda p, i, ci_ref: (p, ci_ref[0] * nb + i, 0)), spec],
            out_specs=[spec, spec]),
        out_shape=[jax.ShapeDtypeStruct((P, half, C), F32), jax.ShapeDtypeStruct((P, half, C), BF)],
        compiler_params=_cparams("parallel", "parallel"),
    )(_index_operand(ci), whole, got)


def _add_three(parts, recv, chip, name):
    _, R, C = parts.shape
    tr = _tile(R, 256, 16)

    def kern(chip_ref, o_ref, r0_ref, r1_ref, r2_ref, out_ref):
        s = ((o_ref[...] + r0_ref[...].astype(F32)) + r1_ref[...].astype(F32)) + r2_ref[...].astype(F32)
        out_ref[0] = s
        out_ref[1] = s

    slab = lambda k: pl.BlockSpec((None, tr, C), lambda i, chip_ref: (k, i, 0))
    return pl.pallas_call(
        kern, name=name,
        grid_spec=pltpu.PrefetchScalarGridSpec(
            num_scalar_prefetch=1, grid=(R // tr,),
            in_specs=[pl.BlockSpec((None, tr, C), lambda i, chip_ref: (chip_ref[0], i, 0)),
                      slab(0), slab(1), slab(2)],
            out_specs=pl.BlockSpec((2, tr, C), lambda i, chip_ref: (0, i, 0))),
        out_shape=jax.ShapeDtypeStruct((2, R, C), F32),
        compiler_params=_cparams("parallel"),
    )(_index_operand(chip), parts, recv, recv, recv)


SM_ADA, SM_G, SM_LOSS, SM_BF, SM_SINK, SM_LEN = 0, 6144, 10240, 11264, 11272, 12288


def _small_finalize(gathered):
    def kern(g_ref, tot_ref, loss_ref):
        tot = g_ref[0:1, :]
        for b in range(1, N_DEV):
            tot = tot + g_ref[b:b + 1, :]
        tot_ref[...] = tot
        sq = jnp.sum(tot[:, SM_LOSS:SM_LOSS + D_MODEL], axis=1, keepdims=True)
        loss_ref[...] = jnp.broadcast_to(sq * (0.5 / D_MODEL), (1, LANES))

    full = lambda shape: pl.BlockSpec(shape, lambda i: (0, 0))
    return pl.pallas_call(
        kern, name="small_finalize", grid=(1,),
        in_specs=[full((N_DEV, SM_LEN))],
        out_specs=[full((1, SM_LEN)), full((1, LANES))],
        out_shape=[jax.ShapeDtypeStruct((1, SM_LEN), F32), jax.ShapeDtypeStruct((1, LANES), F32)],
        compiler_params=_cparams("arbitrary"),
    )(gathered)


def _ada_dw(c_t, d_ada):
    N = d_ada.shape[1]
    tn = _tile(N, 512)

    def kern(c_ref, d_ref, o_ref):
        acc = c_ref[:, 0:1] * d_ref[0:1, :]
        for b in range(1, N_DEV):
            acc = acc + c_ref[:, b:b + 1] * d_ref[b:b + 1, :]
        o_ref[...] = acc

    return pl.pallas_call(
        kern, name="ada_dw", grid=(N // tn,),
        in_specs=[pl.BlockSpec((D_MODEL, N_DEV), lambda j: (0, 0)), pl.BlockSpec((N_DEV, tn), lambda j: (0, j))],
        out_specs=pl.BlockSpec((D_MODEL, tn), lambda j: (0, j)),
        out_shape=jax.ShapeDtypeStruct((D_MODEL, N), F32),
        compiler_params=_cparams("parallel"),
    )(c_t, d_ada)


def _here():
    return lax.axis_index("x"), lax.axis_index("y"), lax.axis_index("c")


def _other_chips(x, y):
    return [(1 - x, y), (x, 1 - y), (1 - x, 1 - y)]


_ANY = pl.BlockSpec(memory_space=pl.ANY)


class _Comm:
    def __init__(self, ins, out_shapes, sem_shapes, start, finish):
        self.ins, self.out_shapes, self.sem_shapes = list(ins), list(out_shapes), list(sem_shapes)
        self.start, self.finish = start, finish

    def split(self, refs, n_in, n_out, n_scratch):
        a = n_in + len(self.ins)
        b = a + n_out + len(self.out_shapes)
        own = list(refs[:n_in]) + list(refs[a:a + n_out]) + list(refs[b:b + n_scratch])
        mine = (refs[n_in:a], refs[a + n_out:b], refs[b + n_scratch:])
        return own, mine


def _run_comm(comm, name):
    n_in, n_out = len(comm.ins), len(comm.out_shapes)

    def body(*refs):
        parts = (refs[:n_in], refs[n_in:n_in + n_out], refs[n_in + n_out:])
        comm.start(*parts)
        comm.finish(*parts)

    return pl.pallas_call(
        body, name=name,
        in_specs=[_ANY] * n_in, out_specs=[_ANY] * n_out,
        out_shape=comm.out_shapes, scratch_shapes=comm.sem_shapes,
    )(*comm.ins)


def _gather_comm(blocks):
    L = len(blocks)

    def parts(ins, outs, sems):
        send_sems, recv_sems, local_sems = sems
        x, y, c = _here()
        me, sibling = (x, y, c), (x, y, 1 - c)
        chips = _other_chips(x, y)

        def slot(px, py, pc):
            return 4 * px + 2 * py + pc

        def copy(l, k, block, to, src=None):
            dst = outs[l].at[slot(*block)]
            return pltpu.make_async_remote_copy(
                src_ref=dst if src is None else src, dst_ref=dst,
                send_sem=send_sems.at[l, k], recv_sem=recv_sems.at[l, k],
                device_id=to, device_id_type=MESH)

        mine = [pltpu.make_async_copy(ins[l], outs[l].at[slot(*me)], local_sems.at[l]) for l in range(L)]
        first = []
        for l in range(L):
            first.append(copy(l, 0, me, sibling, src=ins[l]))
            for j, chip in enumerate(chips):
                first.append(copy(l, 1 + j, me, (*chip, c), src=ins[l]))
        return c, me, sibling, chips, copy, mine, first

    def start(ins, outs, sems):
        *_, mine, first = parts(ins, outs, sems)
        for cp in mine + first:
            cp.start()

    def finish(ins, outs, sems):
        c, me, sibling, chips, copy, mine, first = parts(ins, outs, sems)
        passed = []
        for j, chip in enumerate(chips):
            for l in range(L):
                copy(l, 1 + j, (*chip, c), me).wait_recv()
                fwd = copy(l, 4 + j, (*chip, c), sibling)
                fwd.start()
                passed.append(fwd)
        for l in range(L):
            copy(l, 0, sibling, me).wait_recv()
        for j, chip in enumerate(chips):
            for l in range(L):
                copy(l, 4 + j, (*chip, 1 - c), me).wait_recv()
        for cp in first + passed:
            cp.wait_send()
        for cp in mine:
            cp.wait()

    return _Comm(blocks, [jax.ShapeDtypeStruct((N_DEV,) + b.shape, b.dtype) for b in blocks],
                 [pltpu.SemaphoreType.DMA((L, 7)), pltpu.SemaphoreType.DMA((L, 7)), pltpu.SemaphoreType.DMA((L,))],
                 start, finish)


def _allgather8(blocks, name):
    return _run_comm(_gather_comm(blocks), name)


def _sibling_swap(arrs, name):
    L = len(arrs)

    def body(*refs):
        ins, outs = refs[:L], refs[L:2 * L]
        send_sems, recv_sems = refs[2 * L:]
        x, y, c = _here()
        cps = []
        for l in range(L):
            half = arrs[l].shape[1] // 2
            rows = pl.ds(pl.multiple_of((1 - c) * half, 16), half)
            cps.append(pltpu.make_async_remote_copy(
                src_ref=ins[l].at[:, rows, :], dst_ref=outs[l], send_sem=send_sems.at[l],
                recv_sem=recv_sems.at[l], device_id=(x, y, 1 - c), device_id_type=MESH))
        for cp in cps:
            cp.start()
        for cp in cps:
            cp.wait()

    return pl.pallas_call(
        body, name=name,
        in_specs=[_ANY] * L, out_specs=[_ANY] * L,
        out_shape=[jax.ShapeDtypeStruct((a.shape[0], a.shape[1] // 2, a.shape[2]), a.dtype) for a in arrs],
        scratch_shapes=[pltpu.SemaphoreType.DMA((L,)), pltpu.SemaphoreType.DMA((L,))],
    )(*arrs)


def _sibling_join(bufs, name):
    L = len(bufs)

    def body(*refs):
        outs = refs[L:2 * L]
        send_sems, recv_sems = refs[2 * L:]
        x, y, c = _here()
        for l in range(L):
            pltpu.make_async_remote_copy(src_ref=outs[l].at[c], dst_ref=outs[l].at[c], send_sem=send_sems.at[l],
                                         recv_sem=recv_sems.at[l], device_id=(x, y, 1 - c),
                                         device_id_type=MESH).start()
        for l in range(L):
            pltpu.make_async_remote_copy(src_ref=outs[l].at[c], dst_ref=outs[l].at[1 - c],
                                         send_sem=send_sems.at[l], recv_sem=recv_sems.at[l],
                                         device_id=(x, y, 1 - c), device_id_type=MESH).wait()

    return pl.pallas_call(
        body, name=name,
        in_specs=[_ANY] * L, out_specs=[_ANY] * L,
        out_shape=[jax.ShapeDtypeStruct(a.shape, a.dtype) for a in bufs],
        input_output_aliases={l: l for l in range(L)},
        scratch_shapes=[pltpu.SemaphoreType.DMA((L,)), pltpu.SemaphoreType.DMA((L,))],
    )(*bufs)


def _scatter_comm(arrs):
    L = len(arrs)

    def copies(ins, outs, sems):
        send_sems, recv_sems = sems
        x, y, c = _here()
        return [pltpu.make_async_remote_copy(
            src_ref=ins[l].at[2 * tx + ty], dst_ref=outs[l].at[j],
            send_sem=send_sems.at[l, j], recv_sem=recv_sems.at[l, j],
            device_id=(tx, ty, c), device_id_type=MESH)
            for l in range(L) for j, (tx, ty) in enumerate(_other_chips(x, y))]

    def start(ins, outs, sems):
        for cp in copies(ins, outs, sems):
            cp.start()

    def finish(ins, outs, sems):
        for cp in copies(ins, outs, sems):
            cp.wait()

    return _Comm(arrs, [jax.ShapeDtypeStruct((3,) + a.shape[1:], a.dtype) for a in arrs],
                 [pltpu.SemaphoreType.DMA((L, 3)), pltpu.SemaphoreType.DMA((L, 3))], start, finish)


_A_ORDER = np.array(A_HEAD_ORDER)
_A_INVERSE = np.argsort(_A_ORDER)


def _permute_in_weights(w_in):
    qa = w_in[:, 0:512].reshape(D_MODEL, A_Q_HEADS, HEAD_DIM)[:, _A_ORDER, :].reshape(D_MODEL, 512)
    f_pad = jnp.pad(w_in[:, 2304:2312], ((0, 0), (0, LANES - B_HEADS)))
    w_a = jnp.concatenate([qa, w_in[:, 512:640], f_pad], axis=1)
    return w_a, w_in[:, 640:2304], w_in[:, 2312:4360]


def _unpermute_in_grads(dw_perm):
    qa = dw_perm[:, 0:512].reshape(D_MODEL, A_Q_HEADS, HEAD_DIM)[:, _A_INVERSE, :].reshape(D_MODEL, 512)
    return jnp.concatenate([qa, dw_perm[:, 512:640], dw_perm[:, W_A:W_A + W_B],
                            dw_perm[:, OFF_F:OFF_F + B_HEADS], dw_perm[:, W_A + W_B:]], axis=1)


class _NoExchange:
    def __init__(self, rest):
        self.rest, self.grads = rest, {}

    def rest_weights_comm(self):
        return None

    def rest_weights(self, outs):
        return self.rest

    def reduce_comm(self, pieces, tag):
        self.grads[tag] = [p32 for p32, _ in pieces]
        return None

    def reduce_done(self, outs, tag):
        pass


class _Exchange:
    def __init__(self, ci, chip, rest_shards):
        self.ci, self.chip, self.rest_shards = ci, chip, rest_shards
        self.part_f32, self.halves = {}, {}

    def _my_half(self, a, axis=0, other=False):
        rows = a.shape[axis] // 2
        return lax.dynamic_slice_in_dim(a, ((1 - self.ci) if other else self.ci) * rows, rows, axis=axis)

    def rest_weights_comm(self):
        return _gather_comm([self._my_half(w).astype(BF) for w in self.rest_shards])

    def rest_weights(self, outs):
        w_ba, w_bb, w_out, w_fi, w_fo = outs
        return (_col_sharded(w_ba), _col_sharded(w_bb), _row_sharded(w_out), _col_sharded(w_fi),
                _row_sharded(w_fo))

    def reduce_comm(self, pieces, tag):
        got = _sibling_swap([pbf for _, pbf in pieces], f"grads_to_sibling_{tag}")
        self.part_f32[tag], part_bf = [], []
        for l, ((p32, _), g_) in enumerate(zip(pieces, got)):
            s32, sbf = _add_pair(p32, g_, self.ci, f"chip_sum_{tag}_{l}")
            self.part_f32[tag].append(s32)
            part_bf.append(sbf)
        return _scatter_comm(part_bf)

    def reduce_done(self, outs, tag):
        self.halves[tag] = [_add_three(p32, r, self.chip, f"shard_sum_{tag}_{l}")
                            for l, (p32, r) in enumerate(zip(self.part_f32[tag], outs))]


def _col_sharded(g):
    return jnp.transpose(g.reshape(N_CHIP, -1, g.shape[-1]), (1, 0, 2)).reshape(2 * g.shape[1], N_CHIP * g.shape[-1])


def _row_sharded(g):
    return g.reshape(N_DEV * g.shape[1], g.shape[-1])


def _rope_tables(pos):
    inv_freq = 1.0 / (ROPE_THETA ** (jnp.arange(0, HEAD_DIM, 2, dtype=F32) / HEAD_DIM))
    ang = pos.astype(F32)[:, None] * inv_freq
    cos, sin = jnp.cos(ang), jnp.sin(ang)
    return jnp.tile(cos, (1, 4)), jnp.tile(jnp.concatenate([-sin, sin], axis=1), (1, 2))


def _local_step(x, pos, ada, g1, g2, g3, g4, b_f, sinks, w_in, exch, target):
    S = x.shape[0]
    t_fox = _tile(S, 512, LANES) if S >= 1024 else S // 2
    shift_m, scale_m, gate_m, shift_f, scale_f, gate_f = [ada[i:i + 1] for i in range(N_ADA)]
    cos_t, sin_t = _rope_tables(pos)
    w_a, w_b, w_g = _permute_in_weights(w_in)
    w_perm = jnp.concatenate([w_a, w_b, w_g], axis=1)
    sinks_p = sinks.reshape(A_KV_HEADS, 4).T.reshape(A_Q_HEADS)
    b_f_pad = jnp.pad(b_f, (0, LANES - B_HEADS)).reshape(1, LANES)

    h1 = _pre_norm(x, g1, scale_m, shift_m, "pre_mix_norm")
    p_a = _mm(h1, w_a, "nn", F32, "proj_a")
    p_b = _mm(h1, w_b, "nn", BF, "proj_b")
    p_g = _mm(h1, w_g, "nn", BF, "proj_g")
    (qk_a,) = _rope([p_a], [640], cos_t, sin_t, "rope_fwd")
    o_a, lse_a = _swa_fwd(qk_a, p_b, 0, sinks_p)
    cum = _fox_gate_fwd(p_a, b_f_pad)[:, :B_HEADS]
    cq_arr = jnp.repeat(cum, HEAD_DIM, axis=1)
    ck_arr = cum.T.reshape(4, 2, S)
    comm = exch.rest_weights_comm()
    res = _attn_fwd(p_b, 1, p_b, 5, p_b, 9, 4, False, t_fox, None, cq_arr, ck_arr, None, "fox_fwd", comm=comm)
    o_b, lse_b = res[0], res[1]
    w_ba, w_bb, w_out, w_fi, w_fo = exch.rest_weights(res[2] if comm else None)
    w_ba_p = w_ba.reshape(A_Q_HEADS, HEAD_DIM, D_MODEL)[_A_ORDER].reshape(512, D_MODEL)
    pa = _mm(o_a, w_ba_p, "nn", F32, "branch_a")
    pb = _mm(o_b, w_bb, "nn", F32, "branch_b")
    merged = _merge_fwd(p_g, pa, pb)
    y1 = _mm(merged, w_out, "nn", F32, "out_proj")
    x2, h2 = _post_pre(x, y1, g2, gate_m, g3, scale_f, shift_f)
    gu = _mm(h2, w_fi, "nn", BF, "ffn_in")
    act = _swiglu_fwd(gu)
    y2 = _mm(act, w_fo, "nn", F32, "ffn_out")
    d_out, d_y2, st_f = _final(x2, y2, g4, gate_f, target)

    d_act = _mm(d_y2, w_fo, "nt", F32, "ffn_out_dx")
    row_pieces = lambda pair: tuple(t.reshape(N_CHIP, t.shape[0] // N_CHIP, t.shape[1]) for t in pair)
    dw_fo = row_pieces(_mm(act, d_y2, "tn", F32, "ffn_out_dw", twin=True))
    d_gu = _swiglu_bwd(d_act, gu)
    d_h2 = _mm(d_gu, w_fi, "nt", F32, "ffn_in_dx")
    dw_fi = _mm(h2, d_gu, "tn", F32, "ffn_in_dw", col_pieces=N_CHIP, twin=True)
    d_x2, d_y1, st_m = _mid_bwd(d_h2, x2, d_out, y1, g3, scale_f, g2, gate_m)
    d_merged = _mm(d_y1, w_out, "nt", F32, "out_proj_dx")
    dw_out = row_pieces(_mm(merged, d_y1, "tn", F32, "out_proj_dw", twin=True))
    d_pa, d_pb, d_ga, d_gb = _merge_bwd(d_merged, p_g, pa, pb)
    d_oa = _mm(d_pa, w_ba_p, "nt", F32, "branch_a_dx")
    dw_ba_p = _mm(o_a, d_pa, "tn", F32, "branch_a_dw", col_pieces=N_CHIP, twin=True)
    d_ob = _mm(d_pb, w_bb, "nt", F32, "branch_b_dx")
    dw_bb = _mm(o_b, d_pb, "tn", F32, "branch_b_dw", col_pieces=N_CHIP, twin=True)
    dq_a, dk_a, dv_a, d_sink = _swa_bwd(qk_a, p_b, 0, o_a, d_oa, lse_a, sinks_p)
    head_rows = lambda t: t.reshape(N_CHIP, A_Q_HEADS, HEAD_DIM, -1)[:, _A_INVERSE].reshape(t.shape)
    dw_ba = tuple(head_rows(t) for t in dw_ba_p)
    comm = exch.reduce_comm([dw_ba, dw_bb, dw_out, dw_fi, dw_fo], "early")
    res = _attn_bwd(p_b, 1, p_b, 5, p_b, 9, o_b, d_ob, lse_b, 4, False, t_fox, None,
                    cq_arr, ck_arr, None, "fox_bwd", comm=comm)
    dq_b, dk_b, dv_b, d_ck, d_cq = res[:5]
    exch.reduce_done(res[5] if comm else None, "early")
    d_qa, d_ka = _rope([dq_a, dk_a], [512, LANES], cos_t, -sin_t, "rope_bwd")
    pad8 = lambda a: jnp.pad(a, ((0, 0), (0, LANES - B_HEADS)))
    d_f, d_bf = _fox_gate_bwd(pad8(d_cq[:, ::HEAD_DIM]), pad8(d_ck.reshape(B_HEADS, S).T), p_a, b_f_pad)
    d_proj = jnp.concatenate([d_qa, d_ka, d_f, dv_a.astype(BF), dq_b.astype(BF), dk_b.astype(BF),
                              dv_b.astype(BF), d_ga, d_gb], axis=1)
    dw_perm = _mm(h1, d_proj, "tn", F32, "proj_dw")
    dw_in = jnp.transpose(_unpermute_in_grads(dw_perm).reshape(D_MODEL, N_CHIP, -1), (1, 0, 2))
    comm = exch.reduce_comm([(dw_in, dw_in.astype(BF))], "late")
    res = _mm(d_proj, w_perm, "nt", F32, "proj_dx", comm=comm)
    d_h1 = res[0] if comm else res
    exch.reduce_done(res[1] if comm else None, "late")
    grad_x, st_p = _pre_bwd(d_h1, x, d_x2, g1, scale_m)

    d_sinks = d_sink[:, :2, 0].T.reshape(A_Q_HEADS)
    small = jnp.concatenate([
        st_p[0], st_p[1], st_m[3], st_m[0], st_m[1], st_f[0],
        st_p[2], st_m[4], st_m[2], st_f[1],
        st_f[2], d_bf[0, :B_HEADS], d_sinks,
        jnp.zeros((SM_LEN - SM_SINK - A_Q_HEADS,), F32)])
    return grad_x, small


def kernel(x, c, positions, w_ada, b_ada, g_pre_mix, g_post_mix, w_in, b_f, sinks, w_branch_a, w_branch_b, w_out, g_pre_ffn, g_post_ffn, w_ffn_in, w_ffn_out, loss_target, m_w_ada, m_b_ada, m_g_pre_mix, m_g_post_mix, m_w_in, m_b_f, m_sinks, m_w_branch_a, m_w_branch_b, m_w_out, m_g_pre_ffn, m_g_post_ffn, m_w_ffn_in, m_w_ffn_out, v_w_ada, v_b_ada, v_g_pre_mix, v_g_post_mix, v_w_in, v_b_f, v_sinks, v_w_branch_a, v_w_branch_b, v_w_out, v_g_pre_ffn, v_g_post_ffn, v_w_ffn_in, v_w_ffn_out):
    xi, yi, ci = _here()
    chip = 2 * xi + yi
    dev = 2 * chip + ci

    def my_half(a):
        rows = a.shape[0] // 2
        return lax.dynamic_slice_in_dim(a, ci * rows, rows, axis=0)

    c_g, w_in_g = _allgather8([c.reshape(8, LANES), my_half(w_in[0]).astype(BF)], "gather_w_in")
    c_all = c_g.reshape(N_DEV, D_MODEL)
    w_in_f = _col_sharded(w_in_g)
    exch = _Exchange(ci, chip, [w_branch_a[0], w_branch_b[0], w_out[0], w_ffn_in[0], w_ffn_out[0]])

    ada_cols = _mm(c_all, w_ada[0], "nn", F32, "ada_fwd")
    (ada_g,) = _allgather8([ada_cols], "gather_ada")
    ada_mine = lax.dynamic_index_in_dim(ada_g.reshape(N_CHIP, 2, N_DEV, -1)[:, 0], dev, axis=1, keepdims=False)
    ada = (ada_mine.reshape(-1) + b_ada[0]).reshape(N_ADA, D_MODEL)

    grad_x, small = _local_step(
        x[0], positions[0], ada, g_pre_mix, g_post_mix, g_pre_ffn, g_post_ffn, b_f[0], sinks[0],
        w_in_f, exch, loss_target[0])

    (small_g,) = _allgather8([small.reshape(8, SM_LEN // 8)], "gather_small")
    small_all = small_g.reshape(N_DEV, SM_LEN)
    small_tot, loss_row = _small_finalize(small_all)
    loss = loss_row[0, 0]
    d_ada_cols = lax.dynamic_slice_in_dim(small_all[:, :N_ADA * D_MODEL], chip * (N_ADA * D_MODEL // N_CHIP),
                                          N_ADA * D_MODEL // N_CHIP, axis=1)
    g_w_ada = _ada_dw(c_all.T, d_ada_cols)

    joined = _sibling_join(exch.halves["late"] + exch.halves["early"], "grads_join")
    g_w_in, g_w_ba, g_w_bb, g_w_out, g_w_fi, g_w_fo = [j.reshape(2 * j.shape[1], j.shape[2]) for j in joined]

    def small_vec(b_ada_, g1_, g2_, g3_, g4_, b_f_, sinks_):
        return jnp.concatenate([b_ada_[0], g1_[0], g2_[0], g3_[0], g4_[0], jnp.zeros((D_MODEL,), F32),
                                b_f_[0], sinks_[0], jnp.zeros((SM_LEN - SM_SINK - A_Q_HEADS,), F32)]
                               ).reshape(8, SM_LEN // 8)

    sw = small_vec(b_ada, g_pre_mix, g_post_mix, g_pre_ffn, g_post_ffn, b_f, sinks)
    sm = small_vec(m_b_ada, m_g_pre_mix, m_g_post_mix, m_g_pre_ffn, m_g_post_ffn, m_b_f, m_sinks)
    sv = small_vec(v_b_ada, v_g_pre_mix, v_g_post_mix, v_g_pre_ffn, v_g_post_ffn, v_b_f, v_sinks)
    s_upd = [u.reshape(SM_LEN) for u in _adamw(sw, small_tot.reshape(8, SM_LEN // 8), sm, sv, "adamw_small")]
    s_grad = small_tot.reshape(SM_LEN)

    def unpack(vec):
        row = lambda a, n: vec[a:a + n].reshape(1, n)
        return dict(b_ada=row(SM_ADA, N_ADA * D_MODEL), g_pre_mix=row(SM_G, D_MODEL),
                    g_post_mix=row(SM_G + D_MODEL, D_MODEL), g_pre_ffn=row(SM_G + 2 * D_MODEL, D_MODEL),
                    g_post_ffn=row(SM_G + 3 * D_MODEL, D_MODEL), b_f=row(SM_BF, B_HEADS),
                    sinks=row(SM_SINK, A_Q_HEADS))

    big = dict(
        w_ada=(w_ada, g_w_ada, m_w_ada, v_w_ada), w_in=(w_in, g_w_in, m_w_in, v_w_in),
        w_branch_a=(w_branch_a, g_w_ba, m_w_branch_a, v_w_branch_a),
        w_branch_b=(w_branch_b, g_w_bb, m_w_branch_b, v_w_branch_b),
        w_out=(w_out, g_w_out, m_w_out, v_w_out), w_ffn_in=(w_ffn_in, g_w_fi, m_w_ffn_in, v_w_ffn_in),
        w_ffn_out=(w_ffn_out, g_w_fo, m_w_ffn_out, v_w_ffn_out))
    grads, deltas, new_m, new_v = unpack(s_grad), unpack(s_upd[0]), unpack(s_upd[1]), unpack(s_upd[2])
    for n, (w_, g_, m_, v_) in big.items():
        d_, nm_, nv_ = _adamw(w_[0], g_, m_[0], v_[0], "adamw_" + n)
        grads[n], deltas[n], new_m[n], new_v[n] = g_[None], d_[None], nm_[None], nv_[None]

    names = ["w_ada", "b_ada", "g_pre_mix", "g_post_mix", "w_in", "b_f", "sinks", "w_branch_a", "w_branch_b",
             "w_out", "g_pre_ffn", "g_post_ffn", "w_ffn_in", "w_ffn_out"]
    return (loss, grad_x[None], *[grads[n] for n in names], *[deltas[n] for n in names],
            *[new_m[n] for n in names], *[new_v[n] for n in names])
```

```python
import functools
import math

import numpy as np
import jax
import jax.numpy as jnp
from jax import lax
from jax.experimental import pallas as pl
from jax.experimental.pallas import tpu as pltpu

F32 = jnp.float32
BF = jnp.bfloat16

D_MODEL = 1024
HEAD_DIM = 64
LANES = 128
WINDOW = 128
A_Q_HEADS = 8
A_KV_HEADS = 2
B_HEADS = 8
D_FF = 2816
ROPE_THETA = 10000.0
RMS_EPS = 1e-6
N_ADA = 6
N_DEV = 8
N_CHIP = 4

ADAM_LR = 0.001
ADAM_B1 = 0.9
ADAM_B2 = 0.999
ADAM_EPS = 1e-08
ADAM_WD = 0.01
ADAM_STEP = 10

VMEM_LIMIT = 48 * 1024 * 1024
MESH = pl.DeviceIdType.MESH

A_HEAD_ORDER = (0, 4, 1, 5, 2, 6, 3, 7)

OFF_QA, OFF_KA, OFF_F = 0, 512, 640
W_A = 768
OFF_VA, OFF_QB, OFF_KB, OFF_VB = 0, 128, 640, 1152
W_B = 1664
W_G = 2048
W_PERM = W_A + W_B + W_G


def _tile(n, cap, mult=LANES):
    if n <= cap:
        return n
    t = (cap // mult) * mult
    while t >= mult:
        if n % t == 0:
            return t
        t -= mult
    raise ValueError(f"no tile for {n}")


MXU_WIDTH = 256
MM_OPERAND_BYTES = 28 * 1024 * 1024


def _mm_tiles(M, N, K, a_bytes, b_bytes, tm_cap, tn_cap):
    tm = _tile(M, tm_cap)
    try:
        tn = _tile(N, tn_cap, MXU_WIDTH)
    except ValueError:
        tn = _tile(N, tn_cap)
    fits = lambda tk: 2 * tk * (tm * a_bytes + tn * b_bytes) <= MM_OPERAND_BYTES
    tk = K if fits(K) else next(t for t in range(K // LANES * LANES, 0, -LANES) if K % t == 0 and fits(t))
    return tm, tn, tk


def _cparams(*sem):
    return pltpu.CompilerParams(dimension_semantics=sem, vmem_limit_bytes=VMEM_LIMIT)


def _own_refs(refs, comm, n_in, n_out, n_scratch):
    if comm is None:
        return list(refs), None
    return comm.split(refs, n_in, n_out, n_scratch)


def _comm_specs(comm, side):
    if comm is None:
        return []
    return [pl.BlockSpec(memory_space=pl.ANY)] * len(comm.ins if side == "in" else comm.out_shapes)


def _comm_edge(comm, comm_refs, grid, first):
    if comm is None:
        return
    at_edge = None
    for axis, n in enumerate(grid):
        here = pl.program_id(axis) == (0 if first else n - 1)
        at_edge = here if at_edge is None else at_edge & here
    pl.when(at_edge)(lambda: (comm.start if first else comm.finish)(*comm_refs))


def _mm(a, b, mode, out_dtype, name, tm_cap=512, tn_cap=2816, comm=None, col_pieces=1, twin=False):
    if mode == "nn":
        (M, K), (K2, N) = a.shape, b.shape
        dims = (((1,), (0,)), ((), ()))
    elif mode == "nt":
        (M, K), (N, K2) = a.shape, b.shape
        dims = (((1,), (1,)), ((), ()))
    else:
        (K, M), (K2, N) = a.shape, b.shape
        dims = (((0,), (0,)), ((), ()))
    assert K == K2, (a.shape, b.shape, mode)
    tm, tn, tk = _mm_tiles(M, N // col_pieces, K, a.dtype.itemsize, b.dtype.itemsize, tm_cap, tn_cap)
    nk = K // tk
    n_out = 2 if twin else 1
    n_scratch = 1 if nk > 1 else 0
    if mode == "nn":
        a_spec = pl.BlockSpec((tm, tk), lambda i, j, k: (i, k))
        b_spec = pl.BlockSpec((tk, tn), lambda i, j, k: (k, j))
    elif mode == "nt":
        a_spec = pl.BlockSpec((tm, tk), lambda i, j, k: (i, k))
        b_spec = pl.BlockSpec((tn, tk), lambda i, j, k: (j, k))
    else:
        a_spec = pl.BlockSpec((tk, tm), lambda i, j, k: (k, i))
        b_spec = pl.BlockSpec((tk, tn), lambda i, j, k: (k, j))

    grid = (M // tm, N // tn, nk)

    def kern(*refs):
        own, comm_refs = _own_refs(refs, comm, 2, n_out, n_scratch)
        a_ref, b_ref, o_refs = own[0], own[1], own[2:2 + n_out]
        k = pl.program_id(2)
        _comm_edge(comm, comm_refs, grid, first=True)
        part = lax.dot_general(a_ref[...].astype(BF), b_ref[...].astype(BF), dims,
                               preferred_element_type=F32)
        if nk == 1:
            for o_ref in o_refs:
                o_ref[...] = part.astype(o_ref.dtype)
        else:
            acc_ref = own[2 + n_out]

            @pl.when(k == 0)
            def _():
                acc_ref[...] = part

            @pl.when(k > 0)
            def _():
                acc_ref[...] += part

            @pl.when(k == nk - 1)
            def _():
                for o_ref in o_refs:
                    o_ref[...] = acc_ref[...].astype(o_ref.dtype)

        _comm_edge(comm, comm_refs, grid, first=False)

    if col_pieces > 1:
        per = N // col_pieces // tn
        out_spec = pl.BlockSpec((None, tm, tn), lambda i, j, k: (j // per, i, j % per))
        shape = (col_pieces, M, N // col_pieces)
    else:
        out_spec = pl.BlockSpec((tm, tn), lambda i, j, k: (i, j))
        shape = (M, N)
    dtypes = [out_dtype, BF] if twin else [out_dtype]
    res = pl.pallas_call(
        kern, name=name, grid=grid,
        in_specs=[a_spec, b_spec] + _comm_specs(comm, "in"),
        out_specs=[out_spec] * n_out + _comm_specs(comm, "out"),
        out_shape=[jax.ShapeDtypeStruct(shape, d) for d in dtypes] + (comm.out_shapes if comm else []),
        scratch_shapes=[pltpu.VMEM((tm, tn), F32)] * n_scratch + (comm.sem_shapes if comm else []),
        compiler_params=_cparams("parallel", "parallel", "arbitrary"),
    )(a, b, *(comm.ins if comm else []))
    own = res[0] if n_out == 1 else tuple(res[:n_out])
    return (own, res[n_out:]) if comm else own


ROWS = 256


def _row_spec(tm, width=D_MODEL, col=0):
    return pl.BlockSpec((tm, width), lambda i: (i, col))


def _vec_spec(width=D_MODEL):
    return pl.BlockSpec((1, width), lambda i: (0, 0))


def _rms(x):
    return lax.rsqrt(jnp.mean(x * x, axis=-1, keepdims=True) + RMS_EPS)


def _colsum(x):
    return jnp.sum(x, axis=0, keepdims=True)


def _norm_bwd(d_xn, xn, r):
    return r * (d_xn - xn * jnp.mean(d_xn * xn, axis=-1, keepdims=True))


def _pre_norm(x, g, scale, shift, name):
    S = x.shape[0]
    tm = _tile(S, ROWS, 8)

    def kern(x_ref, g_ref, sc_ref, sh_ref, h_ref):
        xf = x_ref[...]
        y = xf * _rms(xf) * g_ref[...]
        h_ref[...] = (y * (1.0 + sc_ref[...]) + sh_ref[...]).astype(BF)

    return pl.pallas_call(
        kern, name=name, grid=(S // tm,),
        in_specs=[_row_spec(tm), _vec_spec(), _vec_spec(), _vec_spec()],
        out_specs=_row_spec(tm),
        out_shape=jax.ShapeDtypeStruct((S, D_MODEL), BF),
        compiler_params=_cparams("parallel"),
    )(x, g, scale, shift)


def _post_pre(x, y1, g2, gate_m, g3, scale_f, shift_f):
    S = x.shape[0]
    tm = _tile(S, ROWS, 8)

    def kern(x_ref, y_ref, g2_ref, gm_ref, g3_ref, sc_ref, sh_ref, x2_ref, h2_ref):
        y = y_ref[...]
        n2 = y * _rms(y) * g2_ref[...]
        x2 = x_ref[...] + gm_ref[...] * n2
        x2_ref[...] = x2
        n3 = x2 * _rms(x2) * g3_ref[...]
        h2_ref[...] = (n3 * (1.0 + sc_ref[...]) + sh_ref[...]).astype(BF)

    return pl.pallas_call(
        kern, name="post_mix_pre_ffn", grid=(S // tm,),
        in_specs=[_row_spec(tm), _row_spec(tm)] + [_vec_spec()] * 5,
        out_specs=[_row_spec(tm), _row_spec(tm)],
        out_shape=[jax.ShapeDtypeStruct((S, D_MODEL), F32), jax.ShapeDtypeStruct((S, D_MODEL), BF)],
        compiler_params=_cparams("parallel"),
    )(x, y1, g2, gate_m, g3, scale_f, shift_f)


def _stats_spec():
    return pl.BlockSpec((8, D_MODEL), lambda i: (0, 0))


def _final(x2, y2, g4, gate_f, target):
    S = x2.shape[0]
    tm = _tile(S, ROWS, 8)

    def kern(x2_ref, y_ref, g4_ref, gf_ref, t_ref, dout_ref, dy_ref, st_ref):
        @pl.when(pl.program_id(0) == 0)
        def _():
            st_ref[...] = jnp.zeros_like(st_ref)

        y = y_ref[...]
        r = _rms(y)
        yn = y * r
        n4 = yn * g4_ref[...]
        diff = x2_ref[...] + gf_ref[...] * n4 - t_ref[...]
        d_out = diff / D_MODEL
        dout_ref[...] = d_out
        dn = d_out * gf_ref[...]
        dy_ref[...] = _norm_bwd(dn * g4_ref[...], yn, r).astype(BF)
        st_ref[0:1, :] += _colsum(d_out * n4)
        st_ref[1:2, :] += _colsum(dn * yn)
        st_ref[2:3, :] += _colsum(diff * diff)

    return pl.pallas_call(
        kern, name="final_loss", grid=(S // tm,),
        in_specs=[_row_spec(tm), _row_spec(tm), _vec_spec(), _vec_spec(), _row_spec(tm)],
        out_specs=[_row_spec(tm), _row_spec(tm), _stats_spec()],
        out_shape=[jax.ShapeDtypeStruct((S, D_MODEL), F32), jax.ShapeDtypeStruct((S, D_MODEL), BF),
                   jax.ShapeDtypeStruct((8, D_MODEL), F32)],
        compiler_params=_cparams("arbitrary"),
    )(x2, y2, g4, gate_f, target)


def _mid_bwd(d_h2, x2, d_out, y1, g3, scale_f, g2, gate_m):
    S = x2.shape[0]
    tm = _tile(S, ROWS, 8)

    def kern(dh_ref, x2_ref, dout_ref, y_ref, g3_ref, sc_ref, g2_ref, gm_ref, dx2_ref, dy_ref, st_ref):
        @pl.when(pl.program_id(0) == 0)
        def _():
            st_ref[...] = jnp.zeros_like(st_ref)

        dh = dh_ref[...]
        x2 = x2_ref[...]
        r3 = _rms(x2)
        xn = x2 * r3
        one_sc = 1.0 + sc_ref[...]
        d_x2 = dout_ref[...] + _norm_bwd(dh * one_sc * g3_ref[...], xn, r3)
        dx2_ref[...] = d_x2
        y = y_ref[...]
        r2 = _rms(y)
        yn = y * r2
        dn = d_x2 * gm_ref[...]
        dy_ref[...] = _norm_bwd(dn * g2_ref[...], yn, r2).astype(BF)
        st_ref[0:1, :] += _colsum(dh)
        st_ref[1:2, :] += _colsum(dh * (xn * g3_ref[...]))
        st_ref[2:3, :] += _colsum(dh * one_sc * xn)
        st_ref[3:4, :] += _colsum(d_x2 * (yn * g2_ref[...]))
        st_ref[4:5, :] += _colsum(dn * yn)

    return pl.pallas_call(
        kern, name="mid_bwd", grid=(S // tm,),
        in_specs=[_row_spec(tm)] * 4 + [_vec_spec()] * 4,
        out_specs=[_row_spec(tm), _row_spec(tm), _stats_spec()],
        out_shape=[jax.ShapeDtypeStruct((S, D_MODEL), F32), jax.ShapeDtypeStruct((S, D_MODEL), BF),
                   jax.ShapeDtypeStruct((8, D_MODEL), F32)],
        compiler_params=_cparams("arbitrary"),
    )(d_h2, x2, d_out, y1, g3, scale_f, g2, gate_m)


def _pre_bwd(d_h1, x, d_x2, g1, scale_m):
    S = x.shape[0]
    tm = _tile(S, ROWS, 8)

    def kern(dh_ref, x_ref, dx2_ref, g_ref, sc_ref, gx_ref, st_ref):
        @pl.when(pl.program_id(0) == 0)
        def _():
            st_ref[...] = jnp.zeros_like(st_ref)

        dh = dh_ref[...]
        xf = x_ref[...]
        r = _rms(xf)
        xn = xf * r
        one_sc = 1.0 + sc_ref[...]
        gx_ref[...] = dx2_ref[...] + _norm_bwd(dh * one_sc * g_ref[...], xn, r)
        st_ref[0:1, :] += _colsum(dh)
        st_ref[1:2, :] += _colsum(dh * (xn * g_ref[...]))
        st_ref[2:3, :] += _colsum(dh * one_sc * xn)

    return pl.pallas_call(
        kern, name="pre_mix_bwd", grid=(S // tm,),
        in_specs=[_row_spec(tm)] * 3 + [_vec_spec()] * 2,
        out_specs=[_row_spec(tm), _stats_spec()],
        out_shape=[jax.ShapeDtypeStruct((S, D_MODEL), F32), jax.ShapeDtypeStruct((8, D_MODEL), F32)],
        compiler_params=_cparams("arbitrary"),
    )(d_h1, x, d_x2, g1, scale_m)


def _rope(xs, widths, cos_t, sin_t, name):
    S = xs[0].shape[0]
    tm = _tile(S, 512, 8)
    n = len(xs)

    def kern(*refs):
        cos = refs[n][...]
        sin = refs[n + 1][...]
        first = (lax.broadcasted_iota(jnp.int32, cos.shape, 1) % HEAD_DIM) < HEAD_DIM // 2
        for x_ref, o_ref, w in zip(refs[:n], refs[n + 2:], widths):
            for c0 in range(0, w, LANES):
                v = x_ref[:, c0:c0 + LANES]
                partner = jnp.where(first, pltpu.roll(v, LANES - HEAD_DIM // 2, 1),
                                    pltpu.roll(v, HEAD_DIM // 2, 1))
                o_ref[:, c0:c0 + LANES] = (v * cos + partner * sin).astype(BF)

    return pl.pallas_call(
        kern, name=name, grid=(S // tm,),
        in_specs=[_row_spec(tm, w) for w in widths] + [_row_spec(tm, LANES)] * 2,
        out_specs=[_row_spec(tm, w) for w in widths],
        out_shape=[jax.ShapeDtypeStruct((S, w), BF) for w in widths],
        compiler_params=_cparams("parallel"),
    )(*xs, cos_t, sin_t)


def _merge_fwd(pg, pa, pb):
    S = pa.shape[0]
    tm = _tile(S, ROWS, 8)

    def kern(ga_ref, gb_ref, pa_ref, pb_ref, o_ref):
        ga = jax.nn.sigmoid(ga_ref[...].astype(F32))
        gb = jax.nn.sigmoid(gb_ref[...].astype(F32))
        o_ref[...] = (ga * pa_ref[...] + gb * pb_ref[...]).astype(BF)

    return pl.pallas_call(
        kern, name="merge_fwd", grid=(S // tm,),
        in_specs=[_row_spec(tm, col=0), _row_spec(tm, col=1), _row_spec(tm), _row_spec(tm)],
        out_specs=_row_spec(tm),
        out_shape=jax.ShapeDtypeStruct((S, D_MODEL), BF),
        compiler_params=_cparams("parallel"),
    )(pg, pg, pa, pb)


def _merge_bwd(d_merged, pg, pa, pb):
    S = pa.shape[0]
    tm = _tile(S, ROWS, 8)

    def kern(dm_ref, ga_ref, gb_ref, pa_ref, pb_ref, dpa_ref, dpb_ref, dga_ref, dgb_ref):
        dm = dm_ref[...]
        ga = jax.nn.sigmoid(ga_ref[...].astype(F32))
        gb = jax.nn.sigmoid(gb_ref[...].astype(F32))
        dpa_ref[...] = (dm * ga).astype(BF)
        dpb_ref[...] = (dm * gb).astype(BF)
        dga_ref[...] = (dm * pa_ref[...] * ga * (1.0 - ga)).astype(BF)
        dgb_ref[...] = (dm * pb_ref[...] * gb * (1.0 - gb)).astype(BF)

    bf_out = jax.ShapeDtypeStruct((S, D_MODEL), BF)
    return pl.pallas_call(
        kern, name="merge_bwd", grid=(S // tm,),
        in_specs=[_row_spec(tm), _row_spec(tm, col=0), _row_spec(tm, col=1), _row_spec(tm), _row_spec(tm)],
        out_specs=[_row_spec(tm)] * 4,
        out_shape=[bf_out] * 4,
        compiler_params=_cparams("parallel"),
    )(d_merged, pg, pg, pa, pb)


def _swiglu_fwd(gu):
    S = gu.shape[0]
    tm = _tile(S, ROWS, 8)
    tc = _tile(D_FF, 1408)
    nc = D_FF // tc

    def kern(g_ref, u_ref, o_ref):
        g = g_ref[...].astype(F32)
        o_ref[...] = (g * jax.nn.sigmoid(g) * u_ref[...].astype(F32)).astype(BF)

    return pl.pallas_call(
        kern, name="swiglu_fwd", grid=(S // tm, nc),
        in_specs=[pl.BlockSpec((tm, tc), lambda i, j: (i, j)),
                  pl.BlockSpec((tm, tc), lambda i, j: (i, j + nc))],
        out_specs=pl.BlockSpec((tm, tc), lambda i, j: (i, j)),
        out_shape=jax.ShapeDtypeStruct((S, D_FF), BF),
        compiler_params=_cparams("parallel", "parallel"),
    )(gu, gu)


def _swiglu_bwd(d_act, gu):
    S = gu.shape[0]
    tm = _tile(S, 128, 8)

    def kern(da_ref, g_ref, u_ref, o_ref):
        g = g_ref[...].astype(F32)
        u = u_ref[...].astype(F32)
        da = da_ref[...]
        sg = jax.nn.sigmoid(g)
        o_ref[:, :D_FF] = (da * u * (sg * (1.0 + g * (1.0 - sg)))).astype(BF)
        o_ref[:, D_FF:] = (da * (g * sg)).astype(BF)

    return pl.pallas_call(
        kern, name="swiglu_bwd", grid=(S // tm,),
        in_specs=[_row_spec(tm, D_FF), _row_spec(tm, D_FF, 0), _row_spec(tm, D_FF, 1)],
        out_specs=_row_spec(tm, 2 * D_FF),
        out_shape=jax.ShapeDtypeStruct((S, 2 * D_FF), BF),
        compiler_params=_cparams("parallel"),
    )(d_act, gu, gu)


def _split3(x):
    hi = x.astype(BF)
    r1 = x - hi.astype(F32)
    mid = r1.astype(BF)
    lo = (r1 - mid.astype(F32)).astype(BF)
    return hi, mid, lo


def _tri_dot(tri, x):
    return sum(jnp.dot(tri, part, preferred_element_type=F32) for part in _split3(x))


def _log_sigmoid(z):
    return jnp.minimum(z, 0.0) - jnp.log(1.0 + jnp.exp(-jnp.abs(z)))


def _fox_gate_fwd(pa, b_f_pad):
    S = pa.shape[0]
    T = _tile(S, 512, 8)
    f_col = OFF_F // LANES

    def kern(z_ref, b_ref, cum_ref, carry_ref):
        @pl.when(pl.program_id(0) == 0)
        def _():
            carry_ref[...] = jnp.zeros_like(carry_ref)

        log_f = _log_sigmoid(z_ref[...] + b_ref[...])
        row = lax.broadcasted_iota(jnp.int32, (T, T), 0)
        col = lax.broadcasted_iota(jnp.int32, (T, T), 1)
        tri = (col <= row).astype(BF)
        cum = _tri_dot(tri, log_f) + carry_ref[...]
        cum_ref[...] = cum
        carry_ref[...] = cum[T - 1:T, :]

    return pl.pallas_call(
        kern, name="fox_gate_fwd", grid=(S // T,),
        in_specs=[_row_spec(T, LANES, f_col), _vec_spec(LANES)],
        out_specs=_row_spec(T, LANES),
        out_shape=jax.ShapeDtypeStruct((S, LANES), F32),
        scratch_shapes=[pltpu.VMEM((1, LANES), F32)],
        compiler_params=_cparams("arbitrary"),
    )(pa, b_f_pad)


def _fox_gate_bwd(rowsum_ds, colsum_ds, pa, b_f_pad):
    S = pa.shape[0]
    T = _tile(S, 512, 8)
    nb = S // T
    f_col = OFF_F // LANES

    def kern(dr_ref, dc_ref, z_ref, b_ref, df_ref, dbf_ref, carry_ref):
        @pl.when(pl.program_id(0) == 0)
        def _():
            carry_ref[...] = jnp.zeros_like(carry_ref)
            dbf_ref[...] = jnp.zeros_like(dbf_ref)

        row = lax.broadcasted_iota(jnp.int32, (T, T), 0)
        col = lax.broadcasted_iota(jnp.int32, (T, T), 1)
        tri = (col >= row).astype(BF)
        rev = _tri_dot(tri, dr_ref[...] - dc_ref[...]) + carry_ref[...]
        carry_ref[...] = rev[0:1, :]
        z = z_ref[...] + b_ref[...]
        lane = lax.broadcasted_iota(jnp.int32, (T, LANES), 1)
        d_z = jnp.where(lane < B_HEADS, rev * jax.nn.sigmoid(-z), 0.0)
        df_ref[...] = d_z.astype(BF)
        dbf_ref[0:1, :] += _colsum(d_z)

    return pl.pallas_call(
        kern, name="fox_gate_bwd", grid=(nb,),
        in_specs=[pl.BlockSpec((T, LANES), lambda i: (nb - 1 - i, 0)),
                  pl.BlockSpec((T, LANES), lambda i: (nb - 1 - i, 0)),
                  pl.BlockSpec((T, LANES), lambda i: (nb - 1 - i, f_col)),
                  _vec_spec(LANES)],
        out_specs=[pl.BlockSpec((T, LANES), lambda i: (nb - 1 - i, 0)),
                   pl.BlockSpec((8, LANES), lambda i: (0, 0))],
        out_shape=[jax.ShapeDtypeStruct((S, LANES), BF), jax.ShapeDtypeStruct((8, LANES), F32)],
        scratch_shapes=[pltpu.VMEM((1, LANES), F32)],
        compiler_params=_cparams("arbitrary"),
    )(rowsum_ds, colsum_ds, pa, b_f_pad)


NEG_INF = float("-inf")
QK_SCALE = 1.0 / math.sqrt(HEAD_DIM)


def _half_mask(shape, half):
    lane = lax.broadcasted_iota(jnp.int32, shape, 1)
    return (lane < HEAD_DIM) if half == 0 else (lane >= HEAD_DIM)


def _valid(i, j, T, rowcol, window):
    rel = (i - j) * T + rowcol
    ok = rel >= 0
    if window is not None:
        ok = ok & (rel < window)
    return ok


def _attn_fwd(q_arr, q_col, k_arr, k_col, v_arr, v_col, n_pairs, kv_shared, T, window,
              cq_arr, ck_arr, sinks, name, comm=None):
    S = q_arr.shape[0]
    nq = S // T
    use_bias = cq_arr is not None
    use_sink = sinks is not None
    back = 0 if window is None else -(-window // T)
    grid = (n_pairs, nq)
    n_in = 3 + 2 * use_bias + use_sink

    def kern(*refs):
        refs, comm_refs = _own_refs(refs, comm, n_in, 2, 0)
        _comm_edge(comm, comm_refs, grid, first=True)
        q_ref, k_ref, v_ref = refs[:3]
        pos = 3
        if use_bias:
            cq_ref, ck_ref = refs[pos:pos + 2]
            pos += 2
        if use_sink:
            sink_ref = refs[pos]
            pos += 1
        o_ref, lse_ref = refs[pos:pos + 2]
        p_id = pl.program_id(0)
        i = pl.program_id(1)
        q = q_ref[...]
        rowcol = lax.broadcasted_iota(jnp.int32, (T, T), 0) - lax.broadcasted_iota(jnp.int32, (T, T), 1)
        lo = jnp.maximum(i - back, 0) if window is not None else 0
        outs, lses = [], []
        for half in (0, 1):
            hm = _half_mask((T, LANES), half)
            qh = (jnp.where(hm, q, 0).astype(F32) * QK_SCALE).astype(BF)
            if use_bias:
                cq = cq_ref[:, half * HEAD_DIM:half * HEAD_DIM + 1]
            if use_sink:
                m0 = jnp.full((T, 1), sink_ref[2 * p_id + half], F32)
                l0 = jnp.ones((T, 1), F32)
            else:
                m0 = jnp.full((T, 1), NEG_INF, F32)
                l0 = jnp.zeros((T, 1), F32)

            def step(j, carry, masked):
                m, l, acc = carry
                rows = pl.ds(pl.multiple_of(j * T, T), T)
                kj = k_ref[rows, :].astype(BF)
                vj = v_ref[rows, :].astype(BF)
                s = lax.dot_general(qh, kj, (((1,), (1,)), ((), ())), preferred_element_type=F32)
                if use_bias:
                    s = s + cq - ck_ref[0, half:half + 1, rows]
                if masked:
                    s = jnp.where(_valid(i, j, T, rowcol, window), s, NEG_INF)
                m_new = jnp.maximum(m, jnp.max(s, axis=1, keepdims=True))
                alpha = jnp.exp(m - m_new)
                p = jnp.exp(s - m_new)
                l_new = alpha * l + jnp.sum(p, axis=1, keepdims=True)
                acc_new = alpha * acc + jnp.dot(p.astype(BF), vj, preferred_element_type=F32)
                return m_new, l_new, acc_new

            init = (m0, l0, jnp.zeros((T, LANES), F32))
            if window is None:
                init = lax.fori_loop(0, i, functools.partial(step, masked=False), init)
                m, l, acc = step(i, init, True)
            else:
                m, l, acc = lax.fori_loop(lo, i + 1, functools.partial(step, masked=True), init)
            outs.append(acc / l)
            lses.append(m + jnp.log(l))
        hm0 = _half_mask((T, LANES), 0)
        o_ref[...] = jnp.where(hm0, outs[0], outs[1])
        lse_ref[...] = jnp.where(hm0, lses[0], lses[1])
        _comm_edge(comm, comm_refs, grid, first=False)

    kv_idx = (lambda c0: (lambda p, i: (0, c0))) if kv_shared else (lambda c0: (lambda p, i: (0, c0 + p)))
    in_specs = [pl.BlockSpec((T, LANES), lambda p, i: (i, q_col + p)),
                pl.BlockSpec((S, LANES), kv_idx(k_col)),
                pl.BlockSpec((S, LANES), kv_idx(v_col))]
    args = [q_arr, k_arr, v_arr]
    if use_bias:
        in_specs += [pl.BlockSpec((T, LANES), lambda p, i: (i, p)),
                     pl.BlockSpec((1, 2, S), lambda p, i: (p, 0, 0))]
        args += [cq_arr, ck_arr]
    if use_sink:
        in_specs.append(pl.BlockSpec(memory_space=pltpu.SMEM))
        args.append(sinks)
    out_spec = pl.BlockSpec((T, LANES), lambda p, i: (i, p))
    res = pl.pallas_call(
        kern, name=name, grid=grid,
        in_specs=in_specs + _comm_specs(comm, "in"),
        out_specs=[out_spec, out_spec] + _comm_specs(comm, "out"),
        out_shape=[jax.ShapeDtypeStruct((S, n_pairs * LANES), F32)] * 2 + (comm.out_shapes if comm else []),
        scratch_shapes=comm.sem_shapes if comm else [],
        compiler_params=_cparams("arbitrary", "arbitrary"),
    )(*args, *(comm.ins if comm else []))
    return (res[0], res[1], res[2:]) if comm else (res[0], res[1])


def _attn_bwd(q_arr, q_col, k_arr, k_col, v_arr, v_col, o_arr, do_arr, lse_arr, n_pairs, kv_shared, T,
              window, cq_arr, ck_arr, sinks, name, comm=None):
    S = q_arr.shape[0]
    nq = S // T
    use_bias = cq_arr is not None
    use_sink = sinks is not None
    back = 0 if window is None else -(-window // T)
    kv_w = LANES if kv_shared else n_pairs * LANES
    grid = (n_pairs,)
    n_in = 6 + 2 * use_bias + use_sink
    n_out = 3 + 2 * use_bias + use_sink

    def kern(*refs):
        refs, comm_refs = _own_refs(refs, comm, n_in, n_out, 0)
        _comm_edge(comm, comm_refs, grid, first=True)
        q_ref, k_ref, v_ref, o_ref, do_ref, lse_ref = refs[:6]
        pos = 6
        if use_bias:
            cq_ref, ck_ref = refs[pos:pos + 2]
            pos += 2
        if use_sink:
            sink_ref = refs[pos]
            pos += 1
        dq_ref, dk_ref, dv_ref = refs[pos:pos + 3]
        pos += 3
        if use_bias:
            dck_ref, dcq_ref = refs[pos:pos + 2]
            pos += 2
        if use_sink:
            dsink_ref = refs[pos]
        p_id = pl.program_id(0)
        rowcol = lax.broadcasted_iota(jnp.int32, (T, T), 0) - lax.broadcasted_iota(jnp.int32, (T, T), 1)

        def zero_kv():
            dk_ref[...] = jnp.zeros_like(dk_ref)
            dv_ref[...] = jnp.zeros_like(dv_ref)

        if kv_shared:
            pl.when(p_id == 0)(zero_kv)
        else:
            zero_kv()
        if use_bias:
            dck_ref[...] = jnp.zeros_like(dck_ref)
        if use_sink:
            dsink_ref[...] = jnp.zeros_like(dsink_ref)

        for half in (0, 1):
            hm = _half_mask((T, LANES), half)
            lane0 = half * HEAD_DIM

            def outer(i, carry):
                qrows = pl.ds(pl.multiple_of(i * T, T), T)
                qh = (jnp.where(hm, q_ref[qrows, :], 0).astype(F32) * QK_SCALE).astype(BF)
                do_f = jnp.where(hm, do_ref[qrows, :], 0.0)
                doh = do_f.astype(BF)
                delta = jnp.sum(do_f * o_ref[qrows, :], axis=1, keepdims=True)
                lse = lse_ref[qrows, lane0:lane0 + 1]
                if use_bias:
                    cq = cq_ref[qrows, lane0:lane0 + 1]
                lo = jnp.maximum(i - back, 0) if window is not None else 0

                def inner(j, carry_in, masked):
                    dq, rs = carry_in
                    krows = pl.ds(pl.multiple_of(j * T, T), T)
                    kj = k_ref[krows, :].astype(BF)
                    vj = v_ref[krows, :].astype(BF)
                    s = lax.dot_general(qh, kj, (((1,), (1,)), ((), ())), preferred_element_type=F32)
                    if use_bias:
                        s = s + cq - ck_ref[0, half:half + 1, krows]
                    if masked:
                        s = jnp.where(_valid(i, j, T, rowcol, window), s, NEG_INF)
                    p = jnp.exp(s - lse)
                    dp = lax.dot_general(doh, vj, (((1,), (1,)), ((), ())), preferred_element_type=F32)
                    ds = p * (dp - delta)
                    ds_b = ds.astype(BF)
                    dv_ref[krows, :] += lax.dot_general(p.astype(BF), doh, (((0,), (0,)), ((), ())),
                                                        preferred_element_type=F32)
                    dk_ref[krows, :] += lax.dot_general(ds_b, qh, (((0,), (0,)), ((), ())),
                                                        preferred_element_type=F32)
                    if use_bias:
                        dck_ref[0, half:half + 1, krows] += jnp.sum(ds, axis=0, keepdims=True)
                        rs = rs + jnp.sum(ds, axis=1, keepdims=True)
                    kh = jnp.where(hm, kj, 0)
                    return dq + jnp.dot(ds_b, kh, preferred_element_type=F32), rs

                init = (jnp.zeros((T, LANES), F32), jnp.zeros((T, 1), F32))
                if window is None:
                    init = lax.fori_loop(0, i, functools.partial(inner, masked=False), init)
                    dq, rs = inner(i, init, True)
                else:
                    dq, rs = lax.fori_loop(lo, i + 1, functools.partial(inner, masked=True), init)
                dq = dq * QK_SCALE
                if half == 0:
                    dq_ref[qrows, :] = dq
                else:
                    dq_ref[qrows, :] += dq
                if use_bias:
                    rs_b = jnp.broadcast_to(rs, (T, LANES))
                    dcq_ref[qrows, :] = rs_b if half == 0 else jnp.where(hm, rs_b, dcq_ref[qrows, :])
                if use_sink:
                    p_sink = jnp.exp(sink_ref[2 * p_id + half] - lse)
                    dsink_ref[0, half:half + 1, :] += jnp.broadcast_to(
                        -jnp.sum(p_sink * delta, axis=0, keepdims=True), (1, LANES))
                return carry

            lax.fori_loop(0, nq, outer, 0)
        _comm_edge(comm, comm_refs, grid, first=False)

    kv_idx = (lambda c0: (lambda p: (0, c0))) if kv_shared else (lambda c0: (lambda p: (0, c0 + p)))
    pair = lambda c0: pl.BlockSpec((S, LANES), lambda p: (0, c0 + p))
    in_specs = [pair(q_col), pl.BlockSpec((S, LANES), kv_idx(k_col)), pl.BlockSpec((S, LANES), kv_idx(v_col)),
                pair(0), pair(0), pair(0)]
    args = [q_arr, k_arr, v_arr, o_arr, do_arr, lse_arr]
    if use_bias:
        in_specs += [pair(0), pl.BlockSpec((1, 2, S), lambda p: (p, 0, 0))]
        args += [cq_arr, ck_arr]
    if use_sink:
        in_specs.append(pl.BlockSpec(memory_space=pltpu.SMEM))
        args.append(sinks)
    out_specs = [pair(0), pl.BlockSpec((S, LANES), kv_idx(0)), pl.BlockSpec((S, LANES), kv_idx(0))]
    out_shape = [jax.ShapeDtypeStruct((S, n_pairs * LANES), F32),
                 jax.ShapeDtypeStruct((S, kv_w), F32), jax.ShapeDtypeStruct((S, kv_w), F32)]
    if use_bias:
        out_specs += [pl.BlockSpec((1, 2, S), lambda p: (p, 0, 0)), pair(0)]
        out_shape += [jax.ShapeDtypeStruct((n_pairs, 2, S), F32), jax.ShapeDtypeStruct((S, n_pairs * LANES), F32)]
    if use_sink:
        out_specs.append(pl.BlockSpec((1, 8, LANES), lambda p: (p, 0, 0)))
        out_shape.append(jax.ShapeDtypeStruct((n_pairs, 8, LANES), F32))
    res = pl.pallas_call(
        kern, name=name, grid=grid,
        in_specs=in_specs + _comm_specs(comm, "in"),
        out_specs=out_specs + _comm_specs(comm, "out"),
        out_shape=out_shape + (comm.out_shapes if comm else []),
        scratch_shapes=comm.sem_shapes if comm else [],
        compiler_params=_cparams("arbitrary"),
    )(*args, *(comm.ins if comm else []))
    return (*res[:n_out], res[n_out:]) if comm else res


def _bias_lanes(shape, half, q_side_terms, k_side_terms):
    lane = lax.broadcasted_iota(jnp.int32, shape, 1)
    base = HEAD_DIM * (1 - half)
    n_q = len(q_side_terms) if q_side_terms is not None else 3
    n_k = len(k_side_terms) if k_side_terms is not None else 3
    out = jnp.zeros(shape, F32)
    for t in range(n_q):
        out = jnp.where(lane == base + t, q_side_terms[t].astype(F32) if q_side_terms is not None else 1.0, out)
    for t in range(n_k):
        out = jnp.where(lane == base + n_q + t,
                        k_side_terms[t].astype(F32) if k_side_terms is not None else 1.0, out)
    return out


def _head_column(block, head):
    lane = lax.broadcasted_iota(jnp.int32, block.shape, 1)
    return jnp.sum(jnp.where(lane == head, block, 0.0), axis=1, keepdims=True)


def _fox_prep_fwd(p_b, cum, T):
    S = p_b.shape[0]

    def kern(q_ref, k_ref, c_ref, qa_ref, ka_ref):
        p_id = pl.program_id(0)
        q, k, cum_blk = q_ref[...], k_ref[...], c_ref[...]
        for half in (0, 1):
            hm = _half_mask((T, LANES), half)
            c3 = _split3(_head_column(cum_blk, 2 * p_id + half))
            qa_ref[half] = jnp.where(hm, q.astype(F32) * QK_SCALE, _bias_lanes((T, LANES), half, c3, None)).astype(BF)
            ka_ref[half] = jnp.where(hm, k.astype(F32),
                                     _bias_lanes((T, LANES), half, None, [-t.astype(F32) for t in c3])).astype(BF)

    out_spec = pl.BlockSpec((None, 2, T, LANES), lambda p, i: (p, 0, i, 0))
    shape = jax.ShapeDtypeStruct((B_HEADS // 2, 2, S, LANES), BF)
    return pl.pallas_call(
        kern, name="fox_prep_fwd", grid=(B_HEADS // 2, S // T),
        in_specs=[pl.BlockSpec((T, LANES), lambda p, i: (i, OFF_QB // LANES + p)),
                  pl.BlockSpec((T, LANES), lambda p, i: (i, OFF_KB // LANES + p)),
                  pl.BlockSpec((T, LANES), lambda p, i: (i, 0))],
        out_specs=[out_spec, out_spec], out_shape=[shape, shape],
        compiler_params=_cparams("parallel", "parallel"),
    )(p_b, p_b, cum)


def _fox_fwd(q_aug, k_aug, p_b, T, comm=None):
    S = p_b.shape[0]
    nq = S // T
    n_pairs = B_HEADS // 2
    grid = (n_pairs, nq)

    def kern(*refs):
        (q_ref, k_ref, v_ref, o_ref, lse_ref), comm_refs = _own_refs(refs, comm, 3, 2, 0)
        _comm_edge(comm, comm_refs, grid, first=True)
        i = pl.program_id(1)
        rowcol = lax.broadcasted_iota(jnp.int32, (T, T), 0) - lax.broadcasted_iota(jnp.int32, (T, T), 1)
        qs = (q_ref[0], q_ref[1])

        def step(j, carry, masked):
            rows = pl.ds(pl.multiple_of(j * T, T), T)
            vj = v_ref[rows, :]
            new = []
            for half in (0, 1):
                m, l, acc = carry[half]
                s = lax.dot_general(qs[half], k_ref[half, rows, :], (((1,), (1,)), ((), ())),
                                    preferred_element_type=F32)
                if masked:
                    s = jnp.where(rowcol >= 0, s, NEG_INF)
                m_new = jnp.maximum(m, jnp.max(s, axis=1, keepdims=True))
                alpha = jnp.exp(m - m_new)
                p = jnp.exp(s - m_new)
                l_new = alpha * l + jnp.sum(p, axis=1, keepdims=True)
                acc_new = alpha * acc + jnp.dot(p.astype(BF), vj, preferred_element_type=F32)
                new.append((m_new, l_new, acc_new))
            return tuple(new)

        one = (jnp.full((T, 1), NEG_INF, F32), jnp.zeros((T, 1), F32), jnp.zeros((T, LANES), F32))
        carry = lax.fori_loop(0, i, functools.partial(step, masked=False), (one, one))
        (m0, l0, acc0), (m1, l1, acc1) = step(i, carry, True)
        hm0 = _half_mask((T, LANES), 0)
        o_ref[...] = jnp.where(hm0, acc0 / l0, acc1 / l1)
        lse_ref[...] = jnp.where(hm0, m0 + jnp.log(l0), m1 + jnp.log(l1))
        _comm_edge(comm, comm_refs, grid, first=False)

    out_spec = pl.BlockSpec((T, LANES), lambda p, i: (i, p))
    res = pl.pallas_call(
        kern, name="fox_fwd", grid=grid,
        in_specs=[pl.BlockSpec((None, 2, T, LANES), lambda p, i: (p, 0, i, 0)),
                  pl.BlockSpec((None, 2, S, LANES), lambda p, i: (p, 0, 0, 0)),
                  pl.BlockSpec((S, LANES), lambda p, i: (0, OFF_VB // LANES + p))] + _comm_specs(comm, "in"),
        out_specs=[out_spec, out_spec] + _comm_specs(comm, "out"),
        out_shape=[jax.ShapeDtypeStruct((S, n_pairs * LANES), F32)] * 2 + (comm.out_shapes if comm else []),
        scratch_shapes=comm.sem_shapes if comm else [],
        compiler_params=_cparams("arbitrary", "arbitrary"),
    )(q_aug, k_aug, p_b, *(comm.ins if comm else []))
    return res[0], res[1], res[2:]


def _fox_prep_bwd(q_aug, p_b, o, do, lse, T):
    S = p_b.shape[0]

    def kern(qa_ref, v_ref, o_ref, do_ref, lse_ref, qb_ref, dob_ref, vb_ref):
        v, o_blk, do_blk, lse_blk = v_ref[...], o_ref[...], do_ref[...], lse_ref[...]
        lane = lax.broadcasted_iota(jnp.int32, (T, LANES), 1)
        for half in (0, 1):
            hm = _half_mask((T, LANES), half)
            base = HEAD_DIM * (1 - half)
            qa = qa_ref[half].astype(F32)
            cq = jnp.sum(jnp.where((lane >= base) & (lane < base + 3), qa, 0.0), axis=1, keepdims=True)
            b3 = _split3(cq - lse_blk[:, HEAD_DIM * half:HEAD_DIM * half + 1])
            qb_ref[half] = jnp.where(hm, qa, _bias_lanes((T, LANES), half, b3, None)).astype(BF)
            do_f = jnp.where(hm, do_blk, 0.0)
            d3 = _split3(-jnp.sum(do_f * o_blk, axis=1, keepdims=True))
            dob_ref[half] = jnp.where(hm, do_f, _bias_lanes((T, LANES), half, d3, [])).astype(BF)
            vb_ref[half] = jnp.where(hm, v.astype(F32), _bias_lanes((T, LANES), half, None, [])).astype(BF)

    aug = pl.BlockSpec((None, 2, T, LANES), lambda p, i: (p, 0, i, 0))
    tile = pl.BlockSpec((T, LANES), lambda p, i: (i, p))
    shape = jax.ShapeDtypeStruct((B_HEADS // 2, 2, S, LANES), BF)
    return pl.pallas_call(
        kern, name="fox_prep_bwd", grid=(B_HEADS // 2, S // T),
        in_specs=[aug, pl.BlockSpec((T, LANES), lambda p, i: (i, OFF_VB // LANES + p)), tile, tile, tile],
        out_specs=[aug, aug, aug], out_shape=[shape, shape, shape],
        compiler_params=_cparams("parallel", "parallel"),
    )(q_aug, p_b, o, do, lse)


def _fox_bwd(qb_aug, k_aug, dob_aug, vb_aug, T, comm=None):
    n_pairs, _, S, _ = qb_aug.shape
    nq = S // T
    grid = (n_pairs,)

    def kern(*refs):
        own, comm_refs = _own_refs(refs, comm, 4, 5, 0)
        q_ref, k_ref, do_ref, v_ref, dq_ref, dk_ref, dv_ref, dck_ref, dcq_ref = own
        _comm_edge(comm, comm_refs, grid, first=True)
        p_id = pl.program_id(0)
        rowcol = lax.broadcasted_iota(jnp.int32, (T, T), 0) - lax.broadcasted_iota(jnp.int32, (T, T), 1)
        lane = lax.broadcasted_iota(jnp.int32, (T, LANES), 1)
        dk_ref[...] = jnp.zeros_like(dk_ref)
        dv_ref[...] = jnp.zeros_like(dv_ref)
        dck_ref[...] = jnp.zeros_like(dck_ref)

        @pl.when(p_id == 0)
        def _():
            dcq_ref[...] = jnp.zeros_like(dcq_ref)

        hms = (_half_mask((T, LANES), 0), _half_mask((T, LANES), 1))

        def outer(i, carry):
            qrows = pl.ds(pl.multiple_of(i * T, T), T)
            qa = (q_ref[0, qrows, :], q_ref[1, qrows, :])
            doa = (do_ref[0, qrows, :], do_ref[1, qrows, :])
            q_own = [jnp.where(hms[h], qa[h], 0) for h in (0, 1)]
            do_own = [jnp.where(hms[h], doa[h], 0) for h in (0, 1)]

            def inner(j, carry_in, masked):
                krows = pl.ds(pl.multiple_of(j * T, T), T)
                dv_add, dk_add, new = 0.0, 0.0, []
                for half in (0, 1):
                    dq, rs = carry_in[half]
                    ka = k_ref[half, krows, :]
                    s = lax.dot_general(qa[half], ka, (((1,), (1,)), ((), ())), preferred_element_type=F32)
                    if masked:
                        s = jnp.where(rowcol >= 0, s, NEG_INF)
                    p = jnp.exp(s)
                    ds = p * lax.dot_general(doa[half], v_ref[half, krows, :], (((1,), (1,)), ((), ())),
                                             preferred_element_type=F32)
                    ds_b = ds.astype(BF)
                    dv_add = dv_add + lax.dot_general(p.astype(BF), do_own[half], (((0,), (0,)), ((), ())),
                                                      preferred_element_type=F32)
                    dk_add = dk_add + lax.dot_general(ds_b, q_own[half], (((0,), (0,)), ((), ())),
                                                      preferred_element_type=F32)
                    dck_ref[half:half + 1, krows] += jnp.sum(ds, axis=0, keepdims=True)
                    new.append((dq + jnp.dot(ds_b, jnp.where(hms[half], ka, 0), preferred_element_type=F32),
                                rs + jnp.sum(ds, axis=1, keepdims=True)))
                dv_ref[krows, :] += dv_add
                dk_ref[krows, :] += dk_add
                return tuple(new)

            one = (jnp.zeros((T, LANES), F32), jnp.zeros((T, 1), F32))
            carry_in = lax.fori_loop(0, i, functools.partial(inner, masked=False), (one, one))
            (dq0, rs0), (dq1, rs1) = inner(i, carry_in, True)
            dq_ref[qrows, :] = (dq0 + dq1) * QK_SCALE
            dcq_ref[qrows, :] = jnp.where(lane == 2 * p_id, rs0, jnp.where(lane == 2 * p_id + 1, rs1,
                                                                             dcq_ref[qrows, :]))
            return carry

        lax.fori_loop(0, nq, outer, 0)
        _comm_edge(comm, comm_refs, grid, first=False)

    aug = pl.BlockSpec((None, 2, S, LANES), lambda p: (p, 0, 0, 0))
    pair = pl.BlockSpec((S, LANES), lambda p: (0, p))
    wide = jax.ShapeDtypeStruct((S, n_pairs * LANES), F32)
    res = pl.pallas_call(
        kern, name="fox_bwd", grid=grid,
        in_specs=[aug, aug, aug, aug] + _comm_specs(comm, "in"),
        out_specs=[pair, pair, pair, pl.BlockSpec((None, 2, S), lambda p: (p, 0, 0)),
                   pl.BlockSpec((S, LANES), lambda p: (0, 0))] + _comm_specs(comm, "out"),
        out_shape=[wide, wide, wide, jax.ShapeDtypeStruct((n_pairs, 2, S), F32),
                   jax.ShapeDtypeStruct((S, LANES), F32)] + (comm.out_shapes if comm else []),
        scratch_shapes=comm.sem_shapes if comm else [],
        compiler_params=_cparams("arbitrary"),
    )(qb_aug, k_aug, dob_aug, vb_aug, *(comm.ins if comm else []))
    return (*res[:5], res[5:])


SWA_TQ = 256


def _swa_window(i, tq):
    start = pl.multiple_of(jnp.maximum(i * tq - WINDOW, 0), LANES)
    return start, i * tq - start


def _swa_valid(offset, tq):
    rel = offset + lax.broadcasted_iota(jnp.int32, (tq, tq + WINDOW), 0) \
        - lax.broadcasted_iota(jnp.int32, (tq, tq + WINDOW), 1)
    return (rel >= 0) & (rel < WINDOW)


def _swa_fwd(qk, v_arr, v_col, sinks):
    S = qk.shape[0]
    tq = min(SWA_TQ, S - WINDOW)
    win = tq + WINDOW

    def kern(q_ref, k_ref, v_ref, sink_ref, o_ref, lse_ref):
        p_id, i = pl.program_id(0), pl.program_id(1)
        start, offset = _swa_window(i, tq)
        kw = k_ref[pl.ds(start, win), :]
        vw = v_ref[pl.ds(start, win), :].astype(BF)
        valid = _swa_valid(offset, tq)
        q = q_ref[...]
        outs, lses = [], []
        for half in (0, 1):
            hm = _half_mask((tq, LANES), half)
            qh = (jnp.where(hm, q, 0).astype(F32) * QK_SCALE).astype(BF)
            s = lax.dot_general(qh, kw, (((1,), (1,)), ((), ())), preferred_element_type=F32)
            s = jnp.where(valid, s, NEG_INF)
            sink = sink_ref[2 * p_id + half]
            m = jnp.maximum(jnp.max(s, axis=1, keepdims=True), sink)
            p = jnp.exp(s - m)
            denom = jnp.sum(p, axis=1, keepdims=True) + jnp.exp(sink - m)
            outs.append(jnp.dot(p.astype(BF), vw, preferred_element_type=F32) / denom)
            lses.append(m + jnp.log(denom))
        hm0 = _half_mask((tq, LANES), 0)
        o_ref[...] = jnp.where(hm0, outs[0], outs[1])
        lse_ref[...] = jnp.where(hm0, lses[0], lses[1])

    tile = pl.BlockSpec((tq, LANES), lambda p, i: (i, p))
    return pl.pallas_call(
        kern, name="swa_fwd", grid=(A_Q_HEADS // 2, S // tq),
        in_specs=[tile, pl.BlockSpec((S, LANES), lambda p, i: (0, A_Q_HEADS // 2)),
                  pl.BlockSpec((S, LANES), lambda p, i: (0, v_col)),
                  pl.BlockSpec(memory_space=pltpu.SMEM)],
        out_specs=[tile, tile],
        out_shape=[jax.ShapeDtypeStruct((S, A_Q_HEADS * HEAD_DIM), F32)] * 2,
        compiler_params=_cparams("parallel", "arbitrary"),
    )(qk, qk, v_arr, sinks)


def _swa_bwd(qk, v_arr, v_col, o_arr, do_arr, lse_arr, sinks):
    S = qk.shape[0]
    tq = min(SWA_TQ, S - WINDOW)
    win = tq + WINDOW
    n_pairs = A_Q_HEADS // 2

    def kern(q_ref, k_ref, v_ref, o_ref, do_ref, lse_ref, sink_ref, dq_ref, dk_ref, dv_ref, dsink_ref):
        p_id, i = pl.program_id(0), pl.program_id(1)

        @pl.when((p_id == 0) & (i == 0))
        def _():
            dk_ref[...] = jnp.zeros_like(dk_ref)
            dv_ref[...] = jnp.zeros_like(dv_ref)

        @pl.when(i == 0)
        def _():
            dsink_ref[...] = jnp.zeros_like(dsink_ref)

        start, offset = _swa_window(i, tq)
        wrows = pl.ds(start, win)
        kw = k_ref[wrows, :]
        vw = v_ref[wrows, :].astype(BF)
        valid = _swa_valid(offset, tq)
        q, do, o, lse2 = q_ref[...], do_ref[...], o_ref[...], lse_ref[...]
        dq = jnp.zeros((tq, LANES), F32)
        dk = jnp.zeros((win, LANES), F32)
        dv = jnp.zeros((win, LANES), F32)
        for half in (0, 1):
            hm = _half_mask((tq, LANES), half)
            lane0 = half * HEAD_DIM
            qh = (jnp.where(hm, q, 0).astype(F32) * QK_SCALE).astype(BF)
            do_f = jnp.where(hm, do, 0.0)
            doh = do_f.astype(BF)
            delta = jnp.sum(do_f * o, axis=1, keepdims=True)
            lse = lse2[:, lane0:lane0 + 1]
            s = lax.dot_general(qh, kw, (((1,), (1,)), ((), ())), preferred_element_type=F32)
            p = jnp.exp(jnp.where(valid, s, NEG_INF) - lse)
            dp = lax.dot_general(doh, vw, (((1,), (1,)), ((), ())), preferred_element_type=F32)
            ds_b = (p * (dp - delta)).astype(BF)
            dv = dv + lax.dot_general(p.astype(BF), doh, (((0,), (0,)), ((), ())), preferred_element_type=F32)
            dk = dk + lax.dot_general(ds_b, qh, (((0,), (0,)), ((), ())), preferred_element_type=F32)
            kh = jnp.where(_half_mask((win, LANES), half), kw, 0)
            dq = dq + jnp.dot(ds_b, kh, preferred_element_type=F32)
            p_sink = jnp.exp(sink_ref[2 * p_id + half] - lse)
            dsink_ref[0, half:half + 1, :] += jnp.broadcast_to(
                -jnp.sum(p_sink * delta, axis=0, keepdims=True), (1, LANES))
        dq_ref[...] = dq * QK_SCALE
        dk_ref[wrows, :] += dk
        dv_ref[wrows, :] += dv

    tile = pl.BlockSpec((tq, LANES), lambda p, i: (i, p))
    whole = lambda col: pl.BlockSpec((S, LANES), lambda p, i: (0, col))
    return pl.pallas_call(
        kern, name="swa_bwd", grid=(n_pairs, S // tq),
        in_specs=[tile, whole(n_pairs), whole(v_col), tile, tile, tile, pl.BlockSpec(memory_space=pltpu.SMEM)],
        out_specs=[tile, whole(0), whole(0), pl.BlockSpec((1, 8, LANES), lambda p, i: (p, 0, 0))],
        out_shape=[jax.ShapeDtypeStruct((S, A_Q_HEADS * HEAD_DIM), F32),
                   jax.ShapeDtypeStruct((S, LANES), F32), jax.ShapeDtypeStruct((S, LANES), F32),
                   jax.ShapeDtypeStruct((n_pairs, 8, LANES), F32)],
        compiler_params=_cparams("arbitrary", "arbitrary"),
    )(qk, qk, v_arr, o_arr, do_arr, lse_arr, sinks)


ADAMW_BLOCK = 256 * 1024


def _adamw(w, g, m, v, name):
    R, C = w.shape
    if R % 8 == 0:
        tr, tc = _tile(R, max(8, ADAMW_BLOCK // C), 8), C
    else:
        tr, tc = R, _tile(C, 256)

    def kern(w_ref, g_ref, m_ref, v_ref, d_ref, mo_ref, vo_ref):
        g_ = g_ref[...]
        m_new = ADAM_B1 * m_ref[...] + (1.0 - ADAM_B1) * g_
        v_new = ADAM_B2 * v_ref[...] + (1.0 - ADAM_B2) * (g_ * g_)
        m_hat = m_new / (1.0 - ADAM_B1 ** ADAM_STEP)
        v_hat = v_new / (1.0 - ADAM_B2 ** ADAM_STEP)
        d_ref[...] = -ADAM_LR * (m_hat / (jnp.sqrt(v_hat) + ADAM_EPS) + ADAM_WD * w_ref[...])
        mo_ref[...] = m_new
        vo_ref[...] = v_new

    spec = pl.BlockSpec((tr, tc), lambda i, j: (i, j))
    shape = jax.ShapeDtypeStruct((R, C), F32)
    return pl.pallas_call(
        kern, name=name, grid=(R // tr, C // tc),
        in_specs=[spec] * 4, out_specs=[spec] * 3, out_shape=[shape] * 3,
        compiler_params=_cparams("parallel", "parallel"),
    )(w, g, m, v)


def _index_operand(i):
    return jnp.reshape(i, (1,)).astype(jnp.int32)


def _add_pair(whole, got, ci, name):
    P, R, C = whole.shape
    half = R // 2
    tr = _tile(half, 256, 16)
    nb = half // tr

    def kern(ci_ref, a_ref, b_ref, o_ref, ob_ref):
        s = a_ref[...] + b_ref[...].astype(F32)
        o_ref[...] = s
        ob_ref[...] = s.astype(BF)

    spec = pl.BlockSpec((None, tr, C), lambda p, i, ci_ref: (p, i, 0))
    return pl.pallas_call(
        kern, name=name,
        grid_spec=pltpu.PrefetchScalarGridSpec(
            num_scalar_prefetch=1, grid=(P, nb),
            in_specs=[pl.BlockSpec((None, tr, C), lambda p, i, ci_ref: (p, ci_ref[0] * nb + i, 0)), spec],
            out_specs=[spec, spec]),
        out_shape=[jax.ShapeDtypeStruct((P, half, C), F32), jax.ShapeDtypeStruct((P, half, C), BF)],
        compiler_params=_cparams("parallel", "parallel"),
    )(_index_operand(ci), whole, got)


def _add_three(parts, recv, chip, name):
    _, R, C = parts.shape
    tr = _tile(R, 256, 16)

    def kern(chip_ref, o_ref, r0_ref, r1_ref, r2_ref, out_ref):
        s = ((o_ref[...] + r0_ref[...].astype(F32)) + r1_ref[...].astype(F32)) + r2_ref[...].astype(F32)
        out_ref[0] = s
        out_ref[1] = s

    slab = lambda k: pl.BlockSpec((None, tr, C), lambda i, chip_ref: (k, i, 0))
    return pl.pallas_call(
        kern, name=name,
        grid_spec=pltpu.PrefetchScalarGridSpec(
            num_scalar_prefetch=1, grid=(R // tr,),
            in_specs=[pl.BlockSpec((None, tr, C), lambda i, chip_ref: (chip_ref[0], i, 0)),
                      slab(0), slab(1), slab(2)],
            out_specs=pl.BlockSpec((2, tr, C), lambda i, chip_ref: (0, i, 0))),
        out_shape=jax.ShapeDtypeStruct((2, R, C), F32),
        compiler_params=_cparams("parallel"),
    )(_index_operand(chip), parts, recv, recv, recv)


SM_ADA, SM_G, SM_LOSS, SM_BF, SM_SINK, SM_LEN = 0, 6144, 10240, 11264, 11272, 12288


def _small_finalize(gathered):
    def kern(g_ref, tot_ref, loss_ref):
        tot = g_ref[0:1, :]
        for b in range(1, N_DEV):
            tot = tot + g_ref[b:b + 1, :]
        tot_ref[...] = tot
        sq = jnp.sum(tot[:, SM_LOSS:SM_LOSS + D_MODEL], axis=1, keepdims=True)
        loss_ref[...] = jnp.broadcast_to(sq * (0.5 / D_MODEL), (1, LANES))

    full = lambda shape: pl.BlockSpec(shape, lambda i: (0, 0))
    return pl.pallas_call(
        kern, name="small_finalize", grid=(1,),
        in_specs=[full((N_DEV, SM_LEN))],
        out_specs=[full((1, SM_LEN)), full((1, LANES))],
        out_shape=[jax.ShapeDtypeStruct((1, SM_LEN), F32), jax.ShapeDtypeStruct((1, LANES), F32)],
        compiler_params=_cparams("arbitrary"),
    )(gathered)


def _ada_dw(c_t, d_ada):
    N = d_ada.shape[1]
    tn = _tile(N, 512)

    def kern(c_ref, d_ref, o_ref):
        acc = c_ref[:, 0:1] * d_ref[0:1, :]
        for b in range(1, N_DEV):
            acc = acc + c_ref[:, b:b + 1] * d_ref[b:b + 1, :]
        o_ref[...] = acc

    return pl.pallas_call(
        kern, name="ada_dw", grid=(N // tn,),
        in_specs=[pl.BlockSpec((D_MODEL, N_DEV), lambda j: (0, 0)), pl.BlockSpec((N_DEV, tn), lambda j: (0, j))],
        out_specs=pl.BlockSpec((D_MODEL, tn), lambda j: (0, j)),
        out_shape=jax.ShapeDtypeStruct((D_MODEL, N), F32),
        compiler_params=_cparams("parallel"),
    )(c_t, d_ada)


def _here():
    return lax.axis_index("x"), lax.axis_index("y"), lax.axis_index("c")


def _other_chips(x, y):
    return [(1 - x, y), (x, 1 - y), (1 - x, 1 - y)]


_ANY = pl.BlockSpec(memory_space=pl.ANY)


class _Comm:
    def __init__(self, ins, out_shapes, sem_shapes, start, finish):
        self.ins, self.out_shapes, self.sem_shapes = list(ins), list(out_shapes), list(sem_shapes)
        self.start, self.finish = start, finish

    def split(self, refs, n_in, n_out, n_scratch):
        a = n_in + len(self.ins)
        b = a + n_out + len(self.out_shapes)
        own = list(refs[:n_in]) + list(refs[a:a + n_out]) + list(refs[b:b + n_scratch])
        mine = (refs[n_in:a], refs[a + n_out:b], refs[b + n_scratch:])
        return own, mine


def _run_comm(comm, name):
    n_in, n_out = len(comm.ins), len(comm.out_shapes)

    def body(*refs):
        parts = (refs[:n_in], refs[n_in:n_in + n_out], refs[n_in + n_out:])
        comm.start(*parts)
        comm.finish(*parts)

    return pl.pallas_call(
        body, name=name,
        in_specs=[_ANY] * n_in, out_specs=[_ANY] * n_out,
        out_shape=comm.out_shapes, scratch_shapes=comm.sem_shapes,
    )(*comm.ins)


def _gather_comm(blocks):
    L = len(blocks)

    def parts(ins, outs, sems):
        send_sems, recv_sems, local_sems = sems
        x, y, c = _here()
        me, sibling = (x, y, c), (x, y, 1 - c)
        chips = _other_chips(x, y)

        def slot(px, py, pc):
            return 4 * px + 2 * py + pc

        def copy(l, k, block, to, src=None):
            dst = outs[l].at[slot(*block)]
            return pltpu.make_async_remote_copy(
                src_ref=dst if src is None else src, dst_ref=dst,
                send_sem=send_sems.at[l, k], recv_sem=recv_sems.at[l, k],
                device_id=to, device_id_type=MESH)

        mine = [pltpu.make_async_copy(ins[l], outs[l].at[slot(*me)], local_sems.at[l]) for l in range(L)]
        first = []
        for l in range(L):
            first.append(copy(l, 0, me, sibling, src=ins[l]))
            for j, chip in enumerate(chips):
                first.append(copy(l, 1 + j, me, (*chip, c), src=ins[l]))
        return c, me, sibling, chips, copy, mine, first

    def start(ins, outs, sems):
        *_, mine, first = parts(ins, outs, sems)
        for cp in mine + first:
            cp.start()

    def finish(ins, outs, sems):
        c, me, sibling, chips, copy, mine, first = parts(ins, outs, sems)
        passed = []
        for j, chip in enumerate(chips):
            for l in range(L):
                copy(l, 1 + j, (*chip, c), me).wait_recv()
                fwd = copy(l, 4 + j, (*chip, c), sibling)
                fwd.start()
                passed.append(fwd)
        for l in range(L):
            copy(l, 0, sibling, me).wait_recv()
        for j, chip in enumerate(chips):
            for l in range(L):
                copy(l, 4 + j, (*chip, 1 - c), me).wait_recv()
        for cp in first + passed:
            cp.wait_send()
        for cp in mine:
            cp.wait()

    return _Comm(blocks, [jax.ShapeDtypeStruct((N_DEV,) + b.shape, b.dtype) for b in blocks],
                 [pltpu.SemaphoreType.DMA((L, 7)), pltpu.SemaphoreType.DMA((L, 7)), pltpu.SemaphoreType.DMA((L,))],
                 start, finish)


def _allgather8(blocks, name):
    return _run_comm(_gather_comm(blocks), name)


def _sibling_swap(arrs, name):
    L = len(arrs)

    def body(*refs):
        ins, outs = refs[:L], refs[L:2 * L]
        send_sems, recv_sems = refs[2 * L:]
        x, y, c = _here()
        cps = []
        for l in range(L):
            half = arrs[l].shape[1] // 2
            rows = pl.ds(pl.multiple_of((1 - c) * half, 16), half)
            cps.append(pltpu.make_async_remote_copy(
                src_ref=ins[l].at[:, rows, :], dst_ref=outs[l], send_sem=send_sems.at[l],
                recv_sem=recv_sems.at[l], device_id=(x, y, 1 - c), device_id_type=MESH))
        for cp in cps:
            cp.start()
        for cp in cps:
            cp.wait()

    return pl.pallas_call(
        body, name=name,
        in_specs=[_ANY] * L, out_specs=[_ANY] * L,
        out_shape=[jax.ShapeDtypeStruct((a.shape[0], a.shape[1] // 2, a.shape[2]), a.dtype) for a in arrs],
        scratch_shapes=[pltpu.SemaphoreType.DMA((L,)), pltpu.SemaphoreType.DMA((L,))],
    )(*arrs)


def _sibling_join(bufs, name):
    L = len(bufs)

    def body(*refs):
        outs = refs[L:2 * L]
        send_sems, recv_sems = refs[2 * L:]
        x, y, c = _here()
        for l in range(L):
            pltpu.make_async_remote_copy(src_ref=outs[l].at[c], dst_ref=outs[l].at[c], send_sem=send_sems.at[l],
                                         recv_sem=recv_sems.at[l], device_id=(x, y, 1 - c),
                                         device_id_type=MESH).start()
        for l in range(L):
            pltpu.make_async_remote_copy(src_ref=outs[l].at[c], dst_ref=outs[l].at[1 - c],
                                         send_sem=send_sems.at[l], recv_sem=recv_sems.at[l],
                                         device_id=(x, y, 1 - c), device_id_type=MESH).wait()

    return pl.pallas_call(
        body, name=name,
        in_specs=[_ANY] * L, out_specs=[_ANY] * L,
        out_shape=[jax.ShapeDtypeStruct(a.shape, a.dtype) for a in bufs],
        input_output_aliases={l: l for l in range(L)},
        scratch_shapes=[pltpu.SemaphoreType.DMA((L,)), pltpu.SemaphoreType.DMA((L,))],
    )(*bufs)


def _scatter_comm(arrs):
    L = len(arrs)

    def copies(ins, outs, sems):
        send_sems, recv_sems = sems
        x, y, c = _here()
        return [pltpu.make_async_remote_copy(
            src_ref=ins[l].at[2 * tx + ty], dst_ref=outs[l].at[j],
            send_sem=send_sems.at[l, j], recv_sem=recv_sems.at[l, j],
            device_id=(tx, ty, c), device_id_type=MESH)
            for l in range(L) for j, (tx, ty) in enumerate(_other_chips(x, y))]

    def start(ins, outs, sems):
        for cp in copies(ins, outs, sems):
            cp.start()

    def finish(ins, outs, sems):
        for cp in copies(ins, outs, sems):
            cp.wait()

    return _Comm(arrs, [jax.ShapeDtypeStruct((3,) + a.shape[1:], a.dtype) for a in arrs],
                 [pltpu.SemaphoreType.DMA((L, 3)), pltpu.SemaphoreType.DMA((L, 3))], start, finish)


_A_ORDER = np.array(A_HEAD_ORDER)
_A_INVERSE = np.argsort(_A_ORDER)


def _permute_in_weights(w_in):
    qa = w_in[:, 0:512].reshape(D_MODEL, A_Q_HEADS, HEAD_DIM)[:, _A_ORDER, :].reshape(D_MODEL, 512)
    f_pad = jnp.pad(w_in[:, 2304:2312], ((0, 0), (0, LANES - B_HEADS)))
    w_a = jnp.concatenate([qa, w_in[:, 512:640], f_pad], axis=1)
    return w_a, w_in[:, 640:2304], w_in[:, 2312:4360]


def _unpermute_in_grads(dw_perm):
    qa = dw_perm[:, 0:512].reshape(D_MODEL, A_Q_HEADS, HEAD_DIM)[:, _A_INVERSE, :].reshape(D_MODEL, 512)
    return jnp.concatenate([qa, dw_perm[:, 512:640], dw_perm[:, W_A:W_A + W_B],
                            dw_perm[:, OFF_F:OFF_F + B_HEADS], dw_perm[:, W_A + W_B:]], axis=1)


class _NoExchange:
    def __init__(self, rest):
        self.rest, self.grads = rest, {}

    def rest_weights_comm(self):
        return None

    def rest_weights(self, outs):
        return self.rest

    def reduce_comm(self, pieces, tag):
        self.grads[tag] = [p32 for p32, _ in pieces]
        return None

    def reduce_done(self, outs, tag):
        pass


class _Exchange:
    def __init__(self, ci, chip, rest_shards):
        self.ci, self.chip, self.rest_shards = ci, chip, rest_shards
        self.part_f32, self.halves = {}, {}

    def _my_half(self, a, axis=0, other=False):
        rows = a.shape[axis] // 2
        return lax.dynamic_slice_in_dim(a, ((1 - self.ci) if other else self.ci) * rows, rows, axis=axis)

    def rest_weights_comm(self):
        return _gather_comm([self._my_half(w).astype(BF) for w in self.rest_shards])

    def rest_weights(self, outs):
        w_ba, w_bb, w_out, w_fi, w_fo = outs
        return (_col_sharded(w_ba), _col_sharded(w_bb), _row_sharded(w_out), _col_sharded(w_fi),
                _row_sharded(w_fo))

    def reduce_comm(self, pieces, tag):
        got = _sibling_swap([pbf for _, pbf in pieces], f"grads_to_sibling_{tag}")
        self.part_f32[tag], part_bf = [], []
        for l, ((p32, _), g_) in enumerate(zip(pieces, got)):
            s32, sbf = _add_pair(p32, g_, self.ci, f"chip_sum_{tag}_{l}")
            self.part_f32[tag].append(s32)
            part_bf.append(sbf)
        return _scatter_comm(part_bf)

    def reduce_done(self, outs, tag):
        self.halves[tag] = [_add_three(p32, r, self.chip, f"shard_sum_{tag}_{l}")
                            for l, (p32, r) in enumerate(zip(self.part_f32[tag], outs))]


def _col_sharded(g):
    return jnp.transpose(g.reshape(N_CHIP, -1, g.shape[-1]), (1, 0, 2)).reshape(2 * g.shape[1], N_CHIP * g.shape[-1])


def _row_sharded(g):
    return g.reshape(N_DEV * g.shape[1], g.shape[-1])


def _rope_tables(pos):
    inv_freq = 1.0 / (ROPE_THETA ** (jnp.arange(0, HEAD_DIM, 2, dtype=F32) / HEAD_DIM))
    ang = pos.astype(F32)[:, None] * inv_freq
    cos, sin = jnp.cos(ang), jnp.sin(ang)
    return jnp.tile(cos, (1, 4)), jnp.tile(jnp.concatenate([-sin, sin], axis=1), (1, 2))


def _local_step(x, pos, ada, g1, g2, g3, g4, b_f, sinks, w_in, exch, target):
    S = x.shape[0]
    t_fox = _tile(S, 512, LANES) if S >= 1024 else S // 2
    shift_m, scale_m, gate_m, shift_f, scale_f, gate_f = [ada[i:i + 1] for i in range(N_ADA)]
    cos_t, sin_t = _rope_tables(pos)
    w_a, w_b, w_g = _permute_in_weights(w_in)
    w_perm = jnp.concatenate([w_a, w_b, w_g], axis=1)
    sinks_p = sinks.reshape(A_KV_HEADS, 4).T.reshape(A_Q_HEADS)
    b_f_pad = jnp.pad(b_f, (0, LANES - B_HEADS)).reshape(1, LANES)

    h1 = _pre_norm(x, g1, scale_m, shift_m, "pre_mix_norm")
    p_a = _mm(h1, w_a, "nn", F32, "proj_a")
    p_b = _mm(h1, w_b, "nn", BF, "proj_b")
    p_g = _mm(h1, w_g, "nn", BF, "proj_g")
    (qk_a,) = _rope([p_a], [640], cos_t, sin_t, "rope_fwd")
    o_a, lse_a = _swa_fwd(qk_a, p_b, 0, sinks_p)
    q_aug, k_aug = _fox_prep_fwd(p_b, _fox_gate_fwd(p_a, b_f_pad), t_fox)
    comm = exch.rest_weights_comm()
    o_b, lse_b, outs = _fox_fwd(q_aug, k_aug, p_b, t_fox, comm=comm)
    w_ba, w_bb, w_out, w_fi, w_fo = exch.rest_weights(outs)
    w_ba_p = w_ba.reshape(A_Q_HEADS, HEAD_DIM, D_MODEL)[_A_ORDER].reshape(512, D_MODEL)
    pa = _mm(o_a, w_ba_p, "nn", F32, "branch_a")
    pb = _mm(o_b, w_bb, "nn", F32, "branch_b")
    merged = _merge_fwd(p_g, pa, pb)
    y1 = _mm(merged, w_out, "nn", F32, "out_proj")
    x2, h2 = _post_pre(x, y1, g2, gate_m, g3, scale_f, shift_f)
    gu = _mm(h2, w_fi, "nn", BF, "ffn_in")
    act = _swiglu_fwd(gu)
    y2 = _mm(act, w_fo, "nn", F32, "ffn_out")
    d_out, d_y2, st_f = _final(x2, y2, g4, gate_f, target)

    d_act = _mm(d_y2, w_fo, "nt", F32, "ffn_out_dx")
    row_pieces = lambda pair: tuple(t.reshape(N_CHIP, t.shape[0] // N_CHIP, t.shape[1]) for t in pair)
    dw_fo = row_pieces(_mm(act, d_y2, "tn", F32, "ffn_out_dw", twin=True))
    d_gu = _swiglu_bwd(d_act, gu)
    d_h2 = _mm(d_gu, w_fi, "nt", F32, "ffn_in_dx")
    dw_fi = _mm(h2, d_gu, "tn", F32, "ffn_in_dw", col_pieces=N_CHIP, twin=True)
    d_x2, d_y1, st_m = _mid_bwd(d_h2, x2, d_out, y1, g3, scale_f, g2, gate_m)
    d_merged = _mm(d_y1, w_out, "nt", F32, "out_proj_dx")
    dw_out = row_pieces(_mm(merged, d_y1, "tn", F32, "out_proj_dw", twin=True))
    d_pa, d_pb, d_ga, d_gb = _merge_bwd(d_merged, p_g, pa, pb)
    d_oa = _mm(d_pa, w_ba_p, "nt", F32, "branch_a_dx")
    dw_ba_p = _mm(o_a, d_pa, "tn", F32, "branch_a_dw", col_pieces=N_CHIP, twin=True)
    d_ob = _mm(d_pb, w_bb, "nt", F32, "branch_b_dx")
    dw_bb = _mm(o_b, d_pb, "tn", F32, "branch_b_dw", col_pieces=N_CHIP, twin=True)
    dq_a, dk_a, dv_a, d_sink = _swa_bwd(qk_a, p_b, 0, o_a, d_oa, lse_a, sinks_p)
    head_rows = lambda t: t.reshape(N_CHIP, A_Q_HEADS, HEAD_DIM, -1)[:, _A_INVERSE].reshape(t.shape)
    dw_ba = tuple(head_rows(t) for t in dw_ba_p)
    comm = exch.reduce_comm([dw_ba, dw_bb, dw_out, dw_fi, dw_fo], "early")
    qb_aug, dob_aug, vb_aug = _fox_prep_bwd(q_aug, p_b, o_b, d_ob, lse_b, t_fox)
    dq_b, dk_b, dv_b, d_ck, d_cq, outs = _fox_bwd(qb_aug, k_aug, dob_aug, vb_aug, t_fox, comm=comm)
    exch.reduce_done(outs, "early")
    d_qa, d_ka = _rope([dq_a, dk_a], [512, LANES], cos_t, -sin_t, "rope_bwd")
    d_ck_cols = jnp.pad(d_ck.reshape(B_HEADS, S).T, ((0, 0), (0, LANES - B_HEADS)))
    d_f, d_bf = _fox_gate_bwd(d_cq, d_ck_cols, p_a, b_f_pad)
    d_proj = jnp.concatenate([d_qa, d_ka, d_f, dv_a.astype(BF), dq_b.astype(BF), dk_b.astype(BF),
                              dv_b.astype(BF), d_ga, d_gb], axis=1)
    dw_perm = _mm(h1, d_proj, "tn", F32, "proj_dw")
    dw_in = jnp.transpose(_unpermute_in_grads(dw_perm).reshape(D_MODEL, N_CHIP, -1), (1, 0, 2))
    comm = exch.reduce_comm([(dw_in, dw_in.astype(BF))], "late")
    res = _mm(d_proj, w_perm, "nt", F32, "proj_dx", comm=comm)
    d_h1 = res[0] if comm else res
    exch.reduce_done(res[1] if comm else None, "late")
    grad_x, st_p = _pre_bwd(d_h1, x, d_x2, g1, scale_m)

    d_sinks = d_sink[:, :2, 0].T.reshape(A_Q_HEADS)
    small = jnp.concatenate([
        st_p[0], st_p[1], st_m[3], st_m[0], st_m[1], st_f[0],
        st_p[2], st_m[4], st_m[2], st_f[1],
        st_f[2], d_bf[0, :B_HEADS], d_sinks,
        jnp.zeros((SM_LEN - SM_SINK - A_Q_HEADS,), F32)])
    return grad_x, small


def kernel(x, c, positions, w_ada, b_ada, g_pre_mix, g_post_mix, w_in, b_f, sinks, w_branch_a, w_branch_b, w_out, g_pre_ffn, g_post_ffn, w_ffn_in, w_ffn_out, loss_target, m_w_ada, m_b_ada, m_g_pre_mix, m_g_post_mix, m_w_in, m_b_f, m_sinks, m_w_branch_a, m_w_branch_b, m_w_out, m_g_pre_ffn, m_g_post_ffn, m_w_ffn_in, m_w_ffn_out, v_w_ada, v_b_ada, v_g_pre_mix, v_g_post_mix, v_w_in, v_b_f, v_sinks, v_w_branch_a, v_w_branch_b, v_w_out, v_g_pre_ffn, v_g_post_ffn, v_w_ffn_in, v_w_ffn_out):
    xi, yi, ci = _here()
    chip = 2 * xi + yi
    dev = 2 * chip + ci

    def my_half(a):
        rows = a.shape[0] // 2
        return lax.dynamic_slice_in_dim(a, ci * rows, rows, axis=0)

    wt_in, mt_in, vt_in = w_in[0].T, m_w_in[0].T, v_w_in[0].T
    half_t = lax.dynamic_slice_in_dim(wt_in, ci * (D_MODEL // 2), D_MODEL // 2, axis=1).astype(BF)
    c_g, wt_g = _allgather8([c.reshape(8, LANES), half_t], "gather_w_in")
    c_all = c_g.reshape(N_DEV, D_MODEL)
    w_in_f = jnp.transpose(wt_g.reshape(N_CHIP, 2, -1, D_MODEL // 2), (1, 3, 0, 2)).reshape(D_MODEL, -1)
    exch = _Exchange(ci, chip, [w_branch_a[0], w_branch_b[0], w_out[0], w_ffn_in[0], w_ffn_out[0]])

    ada_cols = _mm(c_all, w_ada[0], "nn", F32, "ada_fwd")
    (ada_g,) = _allgather8([ada_cols], "gather_ada")
    ada_mine = lax.dynamic_index_in_dim(ada_g.reshape(N_CHIP, 2, N_DEV, -1)[:, 0], dev, axis=1, keepdims=False)
    ada = (ada_mine.reshape(-1) + b_ada[0]).reshape(N_ADA, D_MODEL)

    grad_x, small = _local_step(
        x[0], positions[0], ada, g_pre_mix, g_post_mix, g_pre_ffn, g_post_ffn, b_f[0], sinks[0],
        w_in_f, exch, loss_target[0])

    (small_g,) = _allgather8([small.reshape(8, SM_LEN // 8)], "gather_small")
    small_all = small_g.reshape(N_DEV, SM_LEN)
    small_tot, loss_row = _small_finalize(small_all)
    loss = loss_row[0, 0]
    d_ada_cols = lax.dynamic_slice_in_dim(small_all[:, :N_ADA * D_MODEL], chip * (N_ADA * D_MODEL // N_CHIP),
                                          N_ADA * D_MODEL // N_CHIP, axis=1)
    g_w_ada = _ada_dw(c_all.T, d_ada_cols)

    joined = _sibling_join(exch.halves["late"] + exch.halves["early"], "grads_join")
    g_w_in, g_w_ba, g_w_bb, g_w_out, g_w_fi, g_w_fo = [j.reshape(2 * j.shape[1], j.shape[2]) for j in joined]

    def small_vec(b_ada_, g1_, g2_, g3_, g4_, b_f_, sinks_):
        return jnp.concatenate([b_ada_[0], g1_[0], g2_[0], g3_[0], g4_[0], jnp.zeros((D_MODEL,), F32),
                                b_f_[0], sinks_[0], jnp.zeros((SM_LEN - SM_SINK - A_Q_HEADS,), F32)]
                               ).reshape(8, SM_LEN // 8)

    sw = small_vec(b_ada, g_pre_mix, g_post_mix, g_pre_ffn, g_post_ffn, b_f, sinks)
    sm = small_vec(m_b_ada, m_g_pre_mix, m_g_post_mix, m_g_pre_ffn, m_g_post_ffn, m_b_f, m_sinks)
    sv = small_vec(v_b_ada, v_g_pre_mix, v_g_post_mix, v_g_pre_ffn, v_g_post_ffn, v_b_f, v_sinks)
    s_upd = [u.reshape(SM_LEN) for u in _adamw(sw, small_tot.reshape(8, SM_LEN // 8), sm, sv, "adamw_small")]
    s_grad = small_tot.reshape(SM_LEN)

    def unpack(vec):
        row = lambda a, n: vec[a:a + n].reshape(1, n)
        return dict(b_ada=row(SM_ADA, N_ADA * D_MODEL), g_pre_mix=row(SM_G, D_MODEL),
                    g_post_mix=row(SM_G + D_MODEL, D_MODEL), g_pre_ffn=row(SM_G + 2 * D_MODEL, D_MODEL),
                    g_post_ffn=row(SM_G + 3 * D_MODEL, D_MODEL), b_f=row(SM_BF, B_HEADS),
                    sinks=row(SM_SINK, A_Q_HEADS))

    big = dict(
        w_ada=(w_ada, g_w_ada, m_w_ada, v_w_ada),
        w_branch_a=(w_branch_a, g_w_ba, m_w_branch_a, v_w_branch_a),
        w_branch_b=(w_branch_b, g_w_bb, m_w_branch_b, v_w_branch_b),
        w_out=(w_out, g_w_out, m_w_out, v_w_out), w_ffn_in=(w_ffn_in, g_w_fi, m_w_ffn_in, v_w_ffn_in),
        w_ffn_out=(w_ffn_out, g_w_fo, m_w_ffn_out, v_w_ffn_out))
    grads, deltas, new_m, new_v = unpack(s_grad), unpack(s_upd[0]), unpack(s_upd[1]), unpack(s_upd[2])
    for n, (w_, g_, m_, v_) in big.items():
        d_, nm_, nv_ = _adamw(w_[0], g_, m_[0], v_[0], "adamw_" + n)
        grads[n], deltas[n], new_m[n], new_v[n] = g_[None], d_[None], nm_[None], nv_[None]
    gt_in = g_w_in.T
    upd_t = _adamw(wt_in, gt_in, mt_in, vt_in, "adamw_w_in")
    grads["w_in"], deltas["w_in"], new_m["w_in"], new_v["w_in"] = [t.T[None] for t in (gt_in, *upd_t)]

    names = ["w_ada", "b_ada", "g_pre_mix", "g_post_mix", "w_in", "b_f", "sinks", "w_branch_a", "w_branch_b",
             "w_out", "g_pre_ffn", "g_post_ffn", "w_ffn_in", "w_ffn_out"]
    return (loss, grad_x[None], *[grads[n] for n in names], *[deltas[n] for n in names],
            *[new_m[n] for n in names], *[new_v[n] for n in names])
```

```python
import functools
import math

import numpy as np
import jax
import jax.numpy as jnp
from jax import lax
from jax.experimental import pallas as pl
from jax.experimental.pallas import tpu as pltpu

F32 = jnp.float32
BF = jnp.bfloat16

D_MODEL = 1024
HEAD_DIM = 64
LANES = 128
WINDOW = 128
A_Q_HEADS = 8
A_KV_HEADS = 2
B_HEADS = 8
D_FF = 2816
ROPE_THETA = 10000.0
RMS_EPS = 1e-6
N_ADA = 6
N_DEV = 8
N_CHIP = 4

ADAM_LR = 0.001
ADAM_B1 = 0.9
ADAM_B2 = 0.999
ADAM_EPS = 1e-08
ADAM_WD = 0.01
ADAM_STEP = 10

VMEM_LIMIT = 48 * 1024 * 1024
MESH = pl.DeviceIdType.MESH

A_HEAD_ORDER = (0, 4, 1, 5, 2, 6, 3, 7)

OFF_QA, OFF_KA, OFF_F = 0, 512, 640
W_A = 768
OFF_VA, OFF_QB, OFF_KB, OFF_VB = 0, 128, 640, 1152
W_B = 1664
W_G = 2048
W_PERM = W_A + W_B + W_G


def _tile(n, cap, mult=LANES):
    if n <= cap:
        return n
    t = (cap // mult) * mult
    while t >= mult:
        if n % t == 0:
            return t
        t -= mult
    raise ValueError(f"no tile for {n}")


MXU_WIDTH = 256
MM_OPERAND_BYTES = 28 * 1024 * 1024


def _mm_tiles(M, N, K, a_bytes, b_bytes, tm_cap, tn_cap):
    tm = _tile(M, tm_cap)
    try:
        tn = _tile(N, tn_cap, MXU_WIDTH)
    except ValueError:
        tn = _tile(N, tn_cap)
    fits = lambda tk: 2 * tk * (tm * a_bytes + tn * b_bytes) <= MM_OPERAND_BYTES
    tk = K if fits(K) else next(t for t in range(K // LANES * LANES, 0, -LANES) if K % t == 0 and fits(t))
    return tm, tn, tk


def _cparams(*sem):
    return pltpu.CompilerParams(dimension_semantics=sem, vmem_limit_bytes=VMEM_LIMIT)


def _own_refs(refs, comm, n_in, n_out, n_scratch):
    if comm is None:
        return list(refs), None
    return comm.split(refs, n_in, n_out, n_scratch)


def _comm_specs(comm, side):
    if comm is None:
        return []
    return [pl.BlockSpec(memory_space=pl.ANY)] * len(comm.ins if side == "in" else comm.out_shapes)


def _comm_edge(comm, comm_refs, grid, first):
    if comm is None:
        return
    at_edge = None
    for axis, n in enumerate(grid):
        here = pl.program_id(axis) == (0 if first else n - 1)
        at_edge = here if at_edge is None else at_edge & here
    pl.when(at_edge)(lambda: (comm.start if first else comm.finish)(*comm_refs))


def _mm(a, b, mode, out_dtype, name, tm_cap=512, tn_cap=2816, comm=None, col_pieces=1, twin=False):
    if mode == "nn":
        (M, K), (K2, N) = a.shape, b.shape
        dims = (((1,), (0,)), ((), ()))
    elif mode == "nt":
        (M, K), (N, K2) = a.shape, b.shape
        dims = (((1,), (1,)), ((), ()))
    else:
        (K, M), (K2, N) = a.shape, b.shape
        dims = (((0,), (0,)), ((), ()))
    assert K == K2, (a.shape, b.shape, mode)
    tm, tn, tk = _mm_tiles(M, N // col_pieces, K, a.dtype.itemsize, b.dtype.itemsize, tm_cap, tn_cap)
    nk = K // tk
    n_out = 2 if twin else 1
    n_scratch = 1 if nk > 1 else 0
    if mode == "nn":
        a_spec = pl.BlockSpec((tm, tk), lambda i, j, k: (i, k))
        b_spec = pl.BlockSpec((tk, tn), lambda i, j, k: (k, j))
    elif mode == "nt":
        a_spec = pl.BlockSpec((tm, tk), lambda i, j, k: (i, k))
        b_spec = pl.BlockSpec((tn, tk), lambda i, j, k: (j, k))
    else:
        a_spec = pl.BlockSpec((tk, tm), lambda i, j, k: (k, i))
        b_spec = pl.BlockSpec((tk, tn), lambda i, j, k: (k, j))

    grid = (M // tm, N // tn, nk)

    def kern(*refs):
        own, comm_refs = _own_refs(refs, comm, 2, n_out, n_scratch)
        a_ref, b_ref, o_refs = own[0], own[1], own[2:2 + n_out]
        k = pl.program_id(2)
        _comm_edge(comm, comm_refs, grid, first=True)
        part = lax.dot_general(a_ref[...].astype(BF), b_ref[...].astype(BF), dims,
                               preferred_element_type=F32)
        if nk == 1:
            for o_ref in o_refs:
                o_ref[...] = part.astype(o_ref.dtype)
        else:
            acc_ref = own[2 + n_out]

            @pl.when(k == 0)
            def _():
                acc_ref[...] = part

            @pl.when(k > 0)
            def _():
                acc_ref[...] += part

            @pl.when(k == nk - 1)
            def _():
                for o_ref in o_refs:
                    o_ref[...] = acc_ref[...].astype(o_ref.dtype)

        _comm_edge(comm, comm_refs, grid, first=False)

    if col_pieces > 1:
        per = N // col_pieces // tn
        out_spec = pl.BlockSpec((None, tm, tn), lambda i, j, k: (j // per, i, j % per))
        shape = (col_pieces, M, N // col_pieces)
    else:
        out_spec = pl.BlockSpec((tm, tn), lambda i, j, k: (i, j))
        shape = (M, N)
    dtypes = [out_dtype, BF] if twin else [out_dtype]
    res = pl.pallas_call(
        kern, name=name, grid=grid,
        in_specs=[a_spec, b_spec] + _comm_specs(comm, "in"),
        out_specs=[out_spec] * n_out + _comm_specs(comm, "out"),
        out_shape=[jax.ShapeDtypeStruct(shape, d) for d in dtypes] + (comm.out_shapes if comm else []),
        scratch_shapes=[pltpu.VMEM((tm, tn), F32)] * n_scratch + (comm.sem_shapes if comm else []),
        compiler_params=_cparams("parallel", "parallel", "arbitrary"),
    )(a, b, *(comm.ins if comm else []))
    own = res[0] if n_out == 1 else tuple(res[:n_out])
    return (own, res[n_out:]) if comm else own


ROWS = 256


def _row_spec(tm, width=D_MODEL, col=0):
    return pl.BlockSpec((tm, width), lambda i: (i, col))


def _vec_spec(width=D_MODEL):
    return pl.BlockSpec((1, width), lambda i: (0, 0))


def _rms(x):
    return lax.rsqrt(jnp.mean(x * x, axis=-1, keepdims=True) + RMS_EPS)


def _colsum(x):
    return jnp.sum(x, axis=0, keepdims=True)


def _norm_bwd(d_xn, xn, r):
    return r * (d_xn - xn * jnp.mean(d_xn * xn, axis=-1, keepdims=True))


def _pre_norm(x, g, scale, shift, name):
    S = x.shape[0]
    tm = _tile(S, ROWS, 8)

    def kern(x_ref, g_ref, sc_ref, sh_ref, h_ref):
        xf = x_ref[...]
        y = xf * _rms(xf) * g_ref[...]
        h_ref[...] = (y * (1.0 + sc_ref[...]) + sh_ref[...]).astype(BF)

    return pl.pallas_call(
        kern, name=name, grid=(S // tm,),
        in_specs=[_row_spec(tm), _vec_spec(), _vec_spec(), _vec_spec()],
        out_specs=_row_spec(tm),
        out_shape=jax.ShapeDtypeStruct((S, D_MODEL), BF),
        compiler_params=_cparams("parallel"),
    )(x, g, scale, shift)


def _post_pre(x, y1, g2, gate_m, g3, scale_f, shift_f):
    S = x.shape[0]
    tm = _tile(S, ROWS, 8)

    def kern(x_ref, y_ref, g2_ref, gm_ref, g3_ref, sc_ref, sh_ref, x2_ref, h2_ref):
        y = y_ref[...]
        n2 = y * _rms(y) * g2_ref[...]
        x2 = x_ref[...] + gm_ref[...] * n2
        x2_ref[...] = x2
        n3 = x2 * _rms(x2) * g3_ref[...]
        h2_ref[...] = (n3 * (1.0 + sc_ref[...]) + sh_ref[...]).astype(BF)

    return pl.pallas_call(
        kern, name="post_mix_pre_ffn", grid=(S // tm,),
        in_specs=[_row_spec(tm), _row_spec(tm)] + [_vec_spec()] * 5,
        out_specs=[_row_spec(tm), _row_spec(tm)],
        out_shape=[jax.ShapeDtypeStruct((S, D_MODEL), F32), jax.ShapeDtypeStruct((S, D_MODEL), BF)],
        compiler_params=_cparams("parallel"),
    )(x, y1, g2, gate_m, g3, scale_f, shift_f)


def _stats_spec():
    return pl.BlockSpec((8, D_MODEL), lambda i: (0, 0))


def _final(x2, y2, g4, gate_f, target):
    S = x2.shape[0]
    tm = _tile(S, ROWS, 8)

    def kern(x2_ref, y_ref, g4_ref, gf_ref, t_ref, dout_ref, dy_ref, st_ref):
        @pl.when(pl.program_id(0) == 0)
        def _():
            st_ref[...] = jnp.zeros_like(st_ref)

        y = y_ref[...]
        r = _rms(y)
        yn = y * r
        n4 = yn * g4_ref[...]
        diff = x2_ref[...] + gf_ref[...] * n4 - t_ref[...]
        d_out = diff / D_MODEL
        dout_ref[...] = d_out
        dn = d_out * gf_ref[...]
        dy_ref[...] = _norm_bwd(dn * g4_ref[...], yn, r).astype(BF)
        st_ref[0:1, :] += _colsum(d_out * n4)
        st_ref[1:2, :] += _colsum(dn * yn)
        st_ref[2:3, :] += _colsum(diff * diff)

    return pl.pallas_call(
        kern, name="final_loss", grid=(S // tm,),
        in_specs=[_row_spec(tm), _row_spec(tm), _vec_spec(), _vec_spec(), _row_spec(tm)],
        out_specs=[_row_spec(tm), _row_spec(tm), _stats_spec()],
        out_shape=[jax.ShapeDtypeStruct((S, D_MODEL), F32), jax.ShapeDtypeStruct((S, D_MODEL), BF),
                   jax.ShapeDtypeStruct((8, D_MODEL), F32)],
        compiler_params=_cparams("arbitrary"),
    )(x2, y2, g4, gate_f, target)


def _mid_bwd(d_h2, x2, d_out, y1, g3, scale_f, g2, gate_m):
    S = x2.shape[0]
    tm = _tile(S, ROWS, 8)

    def kern(dh_ref, x2_ref, dout_ref, y_ref, g3_ref, sc_ref, g2_ref, gm_ref, dx2_ref, dy_ref, st_ref):
        @pl.when(pl.program_id(0) == 0)
        def _():
            st_ref[...] = jnp.zeros_like(st_ref)

        dh = dh_ref[...]
        x2 = x2_ref[...]
        r3 = _rms(x2)
        xn = x2 * r3
        one_sc = 1.0 + sc_ref[...]
        d_x2 = dout_ref[...] + _norm_bwd(dh * one_sc * g3_ref[...], xn, r3)
        dx2_ref[...] = d_x2
        y = y_ref[...]
        r2 = _rms(y)
        yn = y * r2
        dn = d_x2 * gm_ref[...]
        dy_ref[...] = _norm_bwd(dn * g2_ref[...], yn, r2).astype(BF)
        st_ref[0:1, :] += _colsum(dh)
        st_ref[1:2, :] += _colsum(dh * (xn * g3_ref[...]))
        st_ref[2:3, :] += _colsum(dh * one_sc * xn)
        st_ref[3:4, :] += _colsum(d_x2 * (yn * g2_ref[...]))
        st_ref[4:5, :] += _colsum(dn * yn)

    return pl.pallas_call(
        kern, name="mid_bwd", grid=(S // tm,),
        in_specs=[_row_spec(tm)] * 4 + [_vec_spec()] * 4,
        out_specs=[_row_spec(tm), _row_spec(tm), _stats_spec()],
        out_shape=[jax.ShapeDtypeStruct((S, D_MODEL), F32), jax.ShapeDtypeStruct((S, D_MODEL), BF),
                   jax.ShapeDtypeStruct((8, D_MODEL), F32)],
        compiler_params=_cparams("arbitrary"),
    )(d_h2, x2, d_out, y1, g3, scale_f, g2, gate_m)


def _pre_bwd(d_h1, x, d_x2, g1, scale_m):
    S = x.shape[0]
    tm = _tile(S, ROWS, 8)

    def kern(dh_ref, x_ref, dx2_ref, g_ref, sc_ref, gx_ref, st_ref):
        @pl.when(pl.program_id(0) == 0)
        def _():
            st_ref[...] = jnp.zeros_like(st_ref)

        dh = dh_ref[...]
        xf = x_ref[...]
        r = _rms(xf)
        xn = xf * r
        one_sc = 1.0 + sc_ref[...]
        gx_ref[...] = dx2_ref[...] + _norm_bwd(dh * one_sc * g_ref[...], xn, r)
        st_ref[0:1, :] += _colsum(dh)
        st_ref[1:2, :] += _colsum(dh * (xn * g_ref[...]))
        st_ref[2:3, :] += _colsum(dh * one_sc * xn)

    return pl.pallas_call(
        kern, name="pre_mix_bwd", grid=(S // tm,),
        in_specs=[_row_spec(tm)] * 3 + [_vec_spec()] * 2,
        out_specs=[_row_spec(tm), _stats_spec()],
        out_shape=[jax.ShapeDtypeStruct((S, D_MODEL), F32), jax.ShapeDtypeStruct((8, D_MODEL), F32)],
        compiler_params=_cparams("arbitrary"),
    )(d_h1, x, d_x2, g1, scale_m)


def _rope(xs, widths, cos_t, sin_t, name):
    S = xs[0].shape[0]
    tm = _tile(S, 512, 8)
    n = len(xs)

    def kern(*refs):
        cos = refs[n][...]
        sin = refs[n + 1][...]
        first = (lax.broadcasted_iota(jnp.int32, cos.shape, 1) % HEAD_DIM) < HEAD_DIM // 2
        for x_ref, o_ref, w in zip(refs[:n], refs[n + 2:], widths):
            for c0 in range(0, w, LANES):
                v = x_ref[:, c0:c0 + LANES]
                partner = jnp.where(first, pltpu.roll(v, LANES - HEAD_DIM // 2, 1),
                                    pltpu.roll(v, HEAD_DIM // 2, 1))
                o_ref[:, c0:c0 + LANES] = (v * cos + partner * sin).astype(BF)

    return pl.pallas_call(
        kern, name=name, grid=(S // tm,),
        in_specs=[_row_spec(tm, w) for w in widths] + [_row_spec(tm, LANES)] * 2,
        out_specs=[_row_spec(tm, w) for w in widths],
        out_shape=[jax.ShapeDtypeStruct((S, w), BF) for w in widths],
        compiler_params=_cparams("parallel"),
    )(*xs, cos_t, sin_t)


def _merge_fwd(pg, pa, pb):
    S = pa.shape[0]
    tm = _tile(S, ROWS, 8)

    def kern(ga_ref, gb_ref, pa_ref, pb_ref, o_ref):
        ga = jax.nn.sigmoid(ga_ref[...].astype(F32))
        gb = jax.nn.sigmoid(gb_ref[...].astype(F32))
        o_ref[...] = (ga * pa_ref[...] + gb * pb_ref[...]).astype(BF)

    return pl.pallas_call(
        kern, name="merge_fwd", grid=(S // tm,),
        in_specs=[_row_spec(tm, col=0), _row_spec(tm, col=1), _row_spec(tm), _row_spec(tm)],
        out_specs=_row_spec(tm),
        out_shape=jax.ShapeDtypeStruct((S, D_MODEL), BF),
        compiler_params=_cparams("parallel"),
    )(pg, pg, pa, pb)


def _merge_bwd(d_merged, pg, pa, pb):
    S = pa.shape[0]
    tm = _tile(S, ROWS, 8)

    def kern(dm_ref, ga_ref, gb_ref, pa_ref, pb_ref, dpa_ref, dpb_ref, dga_ref, dgb_ref):
        dm = dm_ref[...]
        ga = jax.nn.sigmoid(ga_ref[...].astype(F32))
        gb = jax.nn.sigmoid(gb_ref[...].astype(F32))
        dpa_ref[...] = (dm * ga).astype(BF)
        dpb_ref[...] = (dm * gb).astype(BF)
        dga_ref[...] = (dm * pa_ref[...] * ga * (1.0 - ga)).astype(BF)
        dgb_ref[...] = (dm * pb_ref[...] * gb * (1.0 - gb)).astype(BF)

    bf_out = jax.ShapeDtypeStruct((S, D_MODEL), BF)
    return pl.pallas_call(
        kern, name="merge_bwd", grid=(S // tm,),
        in_specs=[_row_spec(tm), _row_spec(tm, col=0), _row_spec(tm, col=1), _row_spec(tm), _row_spec(tm)],
        out_specs=[_row_spec(tm)] * 4,
        out_shape=[bf_out] * 4,
        compiler_params=_cparams("parallel"),
    )(d_merged, pg, pg, pa, pb)


def _swiglu_fwd(gu):
    S = gu.shape[0]
    tm = _tile(S, ROWS, 8)
    tc = _tile(D_FF, 1408)
    nc = D_FF // tc

    def kern(g_ref, u_ref, o_ref):
        g = g_ref[...]
        o_ref[...] = g * jax.nn.sigmoid(g) * u_ref[...]

    return pl.pallas_call(
        kern, name="swiglu_fwd", grid=(S // tm, nc),
        in_specs=[pl.BlockSpec((tm, tc), lambda i, j: (i, j)),
                  pl.BlockSpec((tm, tc), lambda i, j: (i, j + nc))],
        out_specs=pl.BlockSpec((tm, tc), lambda i, j: (i, j)),
        out_shape=jax.ShapeDtypeStruct((S, D_FF), BF),
        compiler_params=_cparams("parallel", "parallel"),
    )(gu, gu)


def _swiglu_bwd(d_act, gu):
    S = gu.shape[0]
    tm = _tile(S, 128, 8)

    def kern(da_ref, g_ref, u_ref, o_ref):
        g = g_ref[...].astype(F32)
        u = u_ref[...].astype(F32)
        da = da_ref[...]
        sg = jax.nn.sigmoid(g)
        o_ref[:, :D_FF] = (da * u * (sg * (1.0 + g * (1.0 - sg)))).astype(BF)
        o_ref[:, D_FF:] = (da * (g * sg)).astype(BF)

    return pl.pallas_call(
        kern, name="swiglu_bwd", grid=(S // tm,),
        in_specs=[_row_spec(tm, D_FF), _row_spec(tm, D_FF, 0), _row_spec(tm, D_FF, 1)],
        out_specs=_row_spec(tm, 2 * D_FF),
        out_shape=jax.ShapeDtypeStruct((S, 2 * D_FF), BF),
        compiler_params=_cparams("parallel"),
    )(d_act, gu, gu)


def _split3(x):
    hi = x.astype(BF)
    r1 = x - hi.astype(F32)
    mid = r1.astype(BF)
    lo = (r1 - mid.astype(F32)).astype(BF)
    return hi, mid, lo


def _tri_dot(tri, x):
    return sum(jnp.dot(tri, part, preferred_element_type=F32) for part in _split3(x))


def _log_sigmoid(z):
    return jnp.minimum(z, 0.0) - jnp.log(1.0 + jnp.exp(-jnp.abs(z)))


def _fox_gate_fwd(pa, b_f_pad):
    S = pa.shape[0]
    T = _tile(S, 512, 8)
    f_col = OFF_F // LANES

    def kern(z_ref, b_ref, cum_ref, carry_ref):
        @pl.when(pl.program_id(0) == 0)
        def _():
            carry_ref[...] = jnp.zeros_like(carry_ref)

        log_f = _log_sigmoid(z_ref[...] + b_ref[...])
        row = lax.broadcasted_iota(jnp.int32, (T, T), 0)
        col = lax.broadcasted_iota(jnp.int32, (T, T), 1)
        tri = (col <= row).astype(BF)
        cum = _tri_dot(tri, log_f) + carry_ref[...]
        cum_ref[...] = cum
        carry_ref[...] = cum[T - 1:T, :]

    return pl.pallas_call(
        kern, name="fox_gate_fwd", grid=(S // T,),
        in_specs=[_row_spec(T, LANES, f_col), _vec_spec(LANES)],
        out_specs=_row_spec(T, LANES),
        out_shape=jax.ShapeDtypeStruct((S, LANES), F32),
        scratch_shapes=[pltpu.VMEM((1, LANES), F32)],
        compiler_params=_cparams("arbitrary"),
    )(pa, b_f_pad)


def _fox_gate_bwd(rowsum_ds, colsum_ds, pa, b_f_pad):
    S = pa.shape[0]
    T = _tile(S, 512, 8)
    nb = S // T
    f_col = OFF_F // LANES

    def kern(dr_ref, dc_ref, z_ref, b_ref, df_ref, dbf_ref, carry_ref):
        @pl.when(pl.program_id(0) == 0)
        def _():
            carry_ref[...] = jnp.zeros_like(carry_ref)
            dbf_ref[...] = jnp.zeros_like(dbf_ref)

        row = lax.broadcasted_iota(jnp.int32, (T, T), 0)
        col = lax.broadcasted_iota(jnp.int32, (T, T), 1)
        tri = (col >= row).astype(BF)
        rev = _tri_dot(tri, dr_ref[...] - dc_ref[...]) + carry_ref[...]
        carry_ref[...] = rev[0:1, :]
        z = z_ref[...] + b_ref[...]
        lane = lax.broadcasted_iota(jnp.int32, (T, LANES), 1)
        d_z = jnp.where(lane < B_HEADS, rev * jax.nn.sigmoid(-z), 0.0)
        df_ref[...] = d_z.astype(BF)
        dbf_ref[0:1, :] += _colsum(d_z)

    return pl.pallas_call(
        kern, name="fox_gate_bwd", grid=(nb,),
        in_specs=[pl.BlockSpec((T, LANES), lambda i: (nb - 1 - i, 0)),
                  pl.BlockSpec((T, LANES), lambda i: (nb - 1 - i, 0)),
                  pl.BlockSpec((T, LANES), lambda i: (nb - 1 - i, f_col)),
                  _vec_spec(LANES)],
        out_specs=[pl.BlockSpec((T, LANES), lambda i: (nb - 1 - i, 0)),
                   pl.BlockSpec((8, LANES), lambda i: (0, 0))],
        out_shape=[jax.ShapeDtypeStruct((S, LANES), BF), jax.ShapeDtypeStruct((8, LANES), F32)],
        scratch_shapes=[pltpu.VMEM((1, LANES), F32)],
        compiler_params=_cparams("arbitrary"),
    )(rowsum_ds, colsum_ds, pa, b_f_pad)


NEG_INF = float("-inf")
QK_SCALE = 1.0 / math.sqrt(HEAD_DIM)


def _half_mask(shape, half):
    lane = lax.broadcasted_iota(jnp.int32, shape, 1)
    return (lane < HEAD_DIM) if half == 0 else (lane >= HEAD_DIM)


def _valid(i, j, T, rowcol, window):
    rel = (i - j) * T + rowcol
    ok = rel >= 0
    if window is not None:
        ok = ok & (rel < window)
    return ok


def _attn_fwd(q_arr, q_col, k_arr, k_col, v_arr, v_col, n_pairs, kv_shared, T, window,
              cq_arr, ck_arr, sinks, name, comm=None):
    S = q_arr.shape[0]
    nq = S // T
    use_bias = cq_arr is not None
    use_sink = sinks is not None
    back = 0 if window is None else -(-window // T)
    grid = (n_pairs, nq)
    n_in = 3 + 2 * use_bias + use_sink

    def kern(*refs):
        refs, comm_refs = _own_refs(refs, comm, n_in, 2, 0)
        _comm_edge(comm, comm_refs, grid, first=True)
        q_ref, k_ref, v_ref = refs[:3]
        pos = 3
        if use_bias:
            cq_ref, ck_ref = refs[pos:pos + 2]
            pos += 2
        if use_sink:
            sink_ref = refs[pos]
            pos += 1
        o_ref, lse_ref = refs[pos:pos + 2]
        p_id = pl.program_id(0)
        i = pl.program_id(1)
        q = q_ref[...]
        rowcol = lax.broadcasted_iota(jnp.int32, (T, T), 0) - lax.broadcasted_iota(jnp.int32, (T, T), 1)
        lo = jnp.maximum(i - back, 0) if window is not None else 0
        outs, lses = [], []
        for half in (0, 1):
            hm = _half_mask((T, LANES), half)
            qh = (jnp.where(hm, q, 0).astype(F32) * QK_SCALE).astype(BF)
            if use_bias:
                cq = cq_ref[:, half * HEAD_DIM:half * HEAD_DIM + 1]
            if use_sink:
                m0 = jnp.full((T, 1), sink_ref[2 * p_id + half], F32)
                l0 = jnp.ones((T, 1), F32)
            else:
                m0 = jnp.full((T, 1), NEG_INF, F32)
                l0 = jnp.zeros((T, 1), F32)

            def step(j, carry, masked):
                m, l, acc = carry
                rows = pl.ds(pl.multiple_of(j * T, T), T)
                kj = k_ref[rows, :].astype(BF)
                vj = v_ref[rows, :].astype(BF)
                s = lax.dot_general(qh, kj, (((1,), (1,)), ((), ())), preferred_element_type=F32)
                if use_bias:
                    s = s + cq - ck_ref[0, half:half + 1, rows]
                if masked:
                    s = jnp.where(_valid(i, j, T, rowcol, window), s, NEG_INF)
                m_new = jnp.maximum(m, jnp.max(s, axis=1, keepdims=True))
                alpha = jnp.exp(m - m_new)
                p = jnp.exp(s - m_new)
                l_new = alpha * l + jnp.sum(p, axis=1, keepdims=True)
                acc_new = alpha * acc + jnp.dot(p.astype(BF), vj, preferred_element_type=F32)
                return m_new, l_new, acc_new

            init = (m0, l0, jnp.zeros((T, LANES), F32))
            if window is None:
                init = lax.fori_loop(0, i, functools.partial(step, masked=False), init)
                m, l, acc = step(i, init, True)
            else:
                m, l, acc = lax.fori_loop(lo, i + 1, functools.partial(step, masked=True), init)
            outs.append(acc / l)
            lses.append(m + jnp.log(l))
        hm0 = _half_mask((T, LANES), 0)
        o_ref[...] = jnp.where(hm0, outs[0], outs[1])
        lse_ref[...] = jnp.where(hm0, lses[0], lses[1])
        _comm_edge(comm, comm_refs, grid, first=False)

    kv_idx = (lambda c0: (lambda p, i: (0, c0))) if kv_shared else (lambda c0: (lambda p, i: (0, c0 + p)))
    in_specs = [pl.BlockSpec((T, LANES), lambda p, i: (i, q_col + p)),
                pl.BlockSpec((S, LANES), kv_idx(k_col)),
                pl.BlockSpec((S, LANES), kv_idx(v_col))]
    args = [q_arr, k_arr, v_arr]
    if use_bias:
        in_specs += [pl.BlockSpec((T, LANES), lambda p, i: (i, p)),
                     pl.BlockSpec((1, 2, S), lambda p, i: (p, 0, 0))]
        args += [cq_arr, ck_arr]
    if use_sink:
        in_specs.append(pl.BlockSpec(memory_space=pltpu.SMEM))
        args.append(sinks)
    out_spec = pl.BlockSpec((T, LANES), lambda p, i: (i, p))
    res = pl.pallas_call(
        kern, name=name, grid=grid,
        in_specs=in_specs + _comm_specs(comm, "in"),
        out_specs=[out_spec, out_spec] + _comm_specs(comm, "out"),
        out_shape=[jax.ShapeDtypeStruct((S, n_pairs * LANES), F32)] * 2 + (comm.out_shapes if comm else []),
        scratch_shapes=comm.sem_shapes if comm else [],
        compiler_params=_cparams("arbitrary", "arbitrary"),
    )(*args, *(comm.ins if comm else []))
    return (res[0], res[1], res[2:]) if comm else (res[0], res[1])


def _attn_bwd(q_arr, q_col, k_arr, k_col, v_arr, v_col, o_arr, do_arr, lse_arr, n_pairs, kv_shared, T,
              window, cq_arr, ck_arr, sinks, name, comm=None):
    S = q_arr.shape[0]
    nq = S // T
    use_bias = cq_arr is not None
    use_sink = sinks is not None
    back = 0 if window is None else -(-window // T)
    kv_w = LANES if kv_shared else n_pairs * LANES
    grid = (n_pairs,)
    n_in = 6 + 2 * use_bias + use_sink
    n_out = 3 + 2 * use_bias + use_sink

    def kern(*refs):
        refs, comm_refs = _own_refs(refs, comm, n_in, n_out, 0)
        _comm_edge(comm, comm_refs, grid, first=True)
        q_ref, k_ref, v_ref, o_ref, do_ref, lse_ref = refs[:6]
        pos = 6
        if use_bias:
            cq_ref, ck_ref = refs[pos:pos + 2]
            pos += 2
        if use_sink:
            sink_ref = refs[pos]
            pos += 1
        dq_ref, dk_ref, dv_ref = refs[pos:pos + 3]
        pos += 3
        if use_bias:
            dck_ref, dcq_ref = refs[pos:pos + 2]
            pos += 2
        if use_sink:
            dsink_ref = refs[pos]
        p_id = pl.program_id(0)
        rowcol = lax.broadcasted_iota(jnp.int32, (T, T), 0) - lax.broadcasted_iota(jnp.int32, (T, T), 1)

        def zero_kv():
            dk_ref[...] = jnp.zeros_like(dk_ref)
            dv_ref[...] = jnp.zeros_like(dv_ref)

        if kv_shared:
            pl.when(p_id == 0)(zero_kv)
        else:
            zero_kv()
        if use_bias:
            dck_ref[...] = jnp.zeros_like(dck_ref)
        if use_sink:
            dsink_ref[...] = jnp.zeros_like(dsink_ref)

        for half in (0, 1):
            hm = _half_mask((T, LANES), half)
            lane0 = half * HEAD_DIM

            def outer(i, carry):
                qrows = pl.ds(pl.multiple_of(i * T, T), T)
                qh = (jnp.where(hm, q_ref[qrows, :], 0).astype(F32) * QK_SCALE).astype(BF)
                do_f = jnp.where(hm, do_ref[qrows, :], 0.0)
                doh = do_f.astype(BF)
                delta = jnp.sum(do_f * o_ref[qrows, :], axis=1, keepdims=True)
                lse = lse_ref[qrows, lane0:lane0 + 1]
                if use_bias:
                    cq = cq_ref[qrows, lane0:lane0 + 1]
                lo = jnp.maximum(i - back, 0) if window is not None else 0

                def inner(j, carry_in, masked):
                    dq, rs = carry_in
                    krows = pl.ds(pl.multiple_of(j * T, T), T)
                    kj = k_ref[krows, :].astype(BF)
                    vj = v_ref[krows, :].astype(BF)
                    s = lax.dot_general(qh, kj, (((1,), (1,)), ((), ())), preferred_element_type=F32)
                    if use_bias:
                        s = s + cq - ck_ref[0, half:half + 1, krows]
                    if masked:
                        s = jnp.where(_valid(i, j, T, rowcol, window), s, NEG_INF)
                    p = jnp.exp(s - lse)
                    dp = lax.dot_general(doh, vj, (((1,), (1,)), ((), ())), preferred_element_type=F32)
                    ds = p * (dp - delta)
                    ds_b = ds.astype(BF)
                    dv_ref[krows, :] += lax.dot_general(p.astype(BF), doh, (((0,), (0,)), ((), ())),
                                                        preferred_element_type=F32)
                    dk_ref[krows, :] += lax.dot_general(ds_b, qh, (((0,), (0,)), ((), ())),
                                                        preferred_element_type=F32)
                    if use_bias:
                        dck_ref[0, half:half + 1, krows] += jnp.sum(ds, axis=0, keepdims=True)
                        rs = rs + jnp.sum(ds, axis=1, keepdims=True)
                    kh = jnp.where(hm, kj, 0)
                    return dq + jnp.dot(ds_b, kh, preferred_element_type=F32), rs

                init = (jnp.zeros((T, LANES), F32), jnp.zeros((T, 1), F32))
                if window is None:
                    init = lax.fori_loop(0, i, functools.partial(inner, masked=False), init)
                    dq, rs = inner(i, init, True)
                else:
                    dq, rs = lax.fori_loop(lo, i + 1, functools.partial(inner, masked=True), init)
                dq = dq * QK_SCALE
                if half == 0:
                    dq_ref[qrows, :] = dq
                else:
                    dq_ref[qrows, :] += dq
                if use_bias:
                    rs_b = jnp.broadcast_to(rs, (T, LANES))
                    dcq_ref[qrows, :] = rs_b if half == 0 else jnp.where(hm, rs_b, dcq_ref[qrows, :])
                if use_sink:
                    p_sink = jnp.exp(sink_ref[2 * p_id + half] - lse)
                    dsink_ref[0, half:half + 1, :] += jnp.broadcast_to(
                        -jnp.sum(p_sink * delta, axis=0, keepdims=True), (1, LANES))
                return carry

            lax.fori_loop(0, nq, outer, 0)
        _comm_edge(comm, comm_refs, grid, first=False)

    kv_idx = (lambda c0: (lambda p: (0, c0))) if kv_shared else (lambda c0: (lambda p: (0, c0 + p)))
    pair = lambda c0: pl.BlockSpec((S, LANES), lambda p: (0, c0 + p))
    in_specs = [pair(q_col), pl.BlockSpec((S, LANES), kv_idx(k_col)), pl.BlockSpec((S, LANES), kv_idx(v_col)),
                pair(0), pair(0), pair(0)]
    args = [q_arr, k_arr, v_arr, o_arr, do_arr, lse_arr]
    if use_bias:
        in_specs += [pair(0), pl.BlockSpec((1, 2, S), lambda p: (p, 0, 0))]
        args += [cq_arr, ck_arr]
    if use_sink:
        in_specs.append(pl.BlockSpec(memory_space=pltpu.SMEM))
        args.append(sinks)
    out_specs = [pair(0), pl.BlockSpec((S, LANES), kv_idx(0)), pl.BlockSpec((S, LANES), kv_idx(0))]
    out_shape = [jax.ShapeDtypeStruct((S, n_pairs * LANES), F32),
                 jax.ShapeDtypeStruct((S, kv_w), F32), jax.ShapeDtypeStruct((S, kv_w), F32)]
    if use_bias:
        out_specs += [pl.BlockSpec((1, 2, S), lambda p: (p, 0, 0)), pair(0)]
        out_shape += [jax.ShapeDtypeStruct((n_pairs, 2, S), F32), jax.ShapeDtypeStruct((S, n_pairs * LANES), F32)]
    if use_sink:
        out_specs.append(pl.BlockSpec((1, 8, LANES), lambda p: (p, 0, 0)))
        out_shape.append(jax.ShapeDtypeStruct((n_pairs, 8, LANES), F32))
    res = pl.pallas_call(
        kern, name=name, grid=grid,
        in_specs=in_specs + _comm_specs(comm, "in"),
        out_specs=out_specs + _comm_specs(comm, "out"),
        out_shape=out_shape + (comm.out_shapes if comm else []),
        scratch_shapes=comm.sem_shapes if comm else [],
        compiler_params=_cparams("arbitrary"),
    )(*args, *(comm.ins if comm else []))
    return (*res[:n_out], res[n_out:]) if comm else res


def _bias_lanes(shape, half, q_side_terms, k_side_terms):
    lane = lax.broadcasted_iota(jnp.int32, shape, 1)
    base = HEAD_DIM * (1 - half)
    n_q = len(q_side_terms) if q_side_terms is not None else 3
    n_k = len(k_side_terms) if k_side_terms is not None else 3
    out = jnp.zeros(shape, F32)
    for t in range(n_q):
        out = jnp.where(lane == base + t, q_side_terms[t].astype(F32) if q_side_terms is not None else 1.0, out)
    for t in range(n_k):
        out = jnp.where(lane == base + n_q + t,
                        k_side_terms[t].astype(F32) if k_side_terms is not None else 1.0, out)
    return out


def _head_column(block, head):
    lane = lax.broadcasted_iota(jnp.int32, block.shape, 1)
    return jnp.sum(jnp.where(lane == head, block, 0.0), axis=1, keepdims=True)


def _fox_prep_fwd(p_b, cum, T):
    S = p_b.shape[0]

    def kern(q_ref, k_ref, c_ref, qa_ref, ka_ref):
        p_id = pl.program_id(0)
        q, k, cum_blk = q_ref[...], k_ref[...], c_ref[...]
        for half in (0, 1):
            hm = _half_mask((T, LANES), half)
            c3 = _split3(_head_column(cum_blk, 2 * p_id + half))
            qa_ref[half] = jnp.where(hm, q.astype(F32) * QK_SCALE, _bias_lanes((T, LANES), half, c3, None)).astype(BF)
            ka_ref[half] = jnp.where(hm, k.astype(F32),
                                     _bias_lanes((T, LANES), half, None, [-t.astype(F32) for t in c3])).astype(BF)

    out_spec = pl.BlockSpec((None, 2, T, LANES), lambda p, i: (p, 0, i, 0))
    shape = jax.ShapeDtypeStruct((B_HEADS // 2, 2, S, LANES), BF)
    return pl.pallas_call(
        kern, name="fox_prep_fwd", grid=(B_HEADS // 2, S // T),
        in_specs=[pl.BlockSpec((T, LANES), lambda p, i: (i, OFF_QB // LANES + p)),
                  pl.BlockSpec((T, LANES), lambda p, i: (i, OFF_KB // LANES + p)),
                  pl.BlockSpec((T, LANES), lambda p, i: (i, 0))],
        out_specs=[out_spec, out_spec], out_shape=[shape, shape],
        compiler_params=_cparams("parallel", "parallel"),
    )(p_b, p_b, cum)


def _fox_fwd(q_aug, k_aug, p_b, T, comm=None):
    S = p_b.shape[0]
    nq = S // T
    n_pairs = B_HEADS // 2
    grid = (n_pairs, nq)

    def kern(*refs):
        (q_ref, k_ref, v_ref, o_ref, lse_ref), comm_refs = _own_refs(refs, comm, 3, 2, 0)
        _comm_edge(comm, comm_refs, grid, first=True)
        i = pl.program_id(1)
        rowcol = lax.broadcasted_iota(jnp.int32, (T, T), 0) - lax.broadcasted_iota(jnp.int32, (T, T), 1)
        qs = (q_ref[0], q_ref[1])

        def step(j, carry, masked):
            rows = pl.ds(pl.multiple_of(j * T, T), T)
            vj = v_ref[rows, :]
            new = []
            for half in (0, 1):
                m, l, acc = carry[half]
                s = lax.dot_general(qs[half], k_ref[half, rows, :], (((1,), (1,)), ((), ())),
                                    preferred_element_type=F32)
                if masked:
                    s = jnp.where(rowcol >= 0, s, NEG_INF)
                m_new = jnp.maximum(m, jnp.max(s, axis=1, keepdims=True))
                alpha = jnp.exp(m - m_new)
                p = jnp.exp(s - m_new)
                l_new = alpha * l + jnp.sum(p, axis=1, keepdims=True)
                acc_new = alpha * acc + jnp.dot(p.astype(BF), vj, preferred_element_type=F32)
                new.append((m_new, l_new, acc_new))
            return tuple(new)

        one = (jnp.full((T, 1), NEG_INF, F32), jnp.zeros((T, 1), F32), jnp.zeros((T, LANES), F32))
        carry = lax.fori_loop(0, i, functools.partial(step, masked=False), (one, one))
        (m0, l0, acc0), (m1, l1, acc1) = step(i, carry, True)
        hm0 = _half_mask((T, LANES), 0)
        o_ref[...] = jnp.where(hm0, acc0 / l0, acc1 / l1)
        lse_ref[...] = jnp.where(hm0, m0 + jnp.log(l0), m1 + jnp.log(l1))
        _comm_edge(comm, comm_refs, grid, first=False)

    out_spec = pl.BlockSpec((T, LANES), lambda p, i: (i, p))
    res = pl.pallas_call(
        kern, name="fox_fwd", grid=grid,
        in_specs=[pl.BlockSpec((None, 2, T, LANES), lambda p, i: (p, 0, i, 0)),
                  pl.BlockSpec((None, 2, S, LANES), lambda p, i: (p, 0, 0, 0)),
                  pl.BlockSpec((S, LANES), lambda p, i: (0, OFF_VB // LANES + p))] + _comm_specs(comm, "in"),
        out_specs=[out_spec, out_spec] + _comm_specs(comm, "out"),
        out_shape=[jax.ShapeDtypeStruct((S, n_pairs * LANES), F32)] * 2 + (comm.out_shapes if comm else []),
        scratch_shapes=comm.sem_shapes if comm else [],
        compiler_params=_cparams("arbitrary", "arbitrary"),
    )(q_aug, k_aug, p_b, *(comm.ins if comm else []))
    return res[0], res[1], res[2:]


def _fox_prep_bwd(q_aug, p_b, o, do, lse, T):
    S = p_b.shape[0]

    def kern(qa_ref, v_ref, o_ref, do_ref, lse_ref, qb_ref, dob_ref, vb_ref):
        v, o_blk, do_blk, lse_blk = v_ref[...], o_ref[...], do_ref[...], lse_ref[...]
        lane = lax.broadcasted_iota(jnp.int32, (T, LANES), 1)
        for half in (0, 1):
            hm = _half_mask((T, LANES), half)
            base = HEAD_DIM * (1 - half)
            qa = qa_ref[half].astype(F32)
            cq = jnp.sum(jnp.where((lane >= base) & (lane < base + 3), qa, 0.0), axis=1, keepdims=True)
            b3 = _split3(cq - lse_blk[:, HEAD_DIM * half:HEAD_DIM * half + 1])
            qb_ref[half] = jnp.where(hm, qa, _bias_lanes((T, LANES), half, b3, None)).astype(BF)
            do_f = jnp.where(hm, do_blk, 0.0)
            d3 = _split3(-jnp.sum(do_f * o_blk, axis=1, keepdims=True))
            dob_ref[half] = jnp.where(hm, do_f, _bias_lanes((T, LANES), half, d3, [])).astype(BF)
            vb_ref[half] = jnp.where(hm, v.astype(F32), _bias_lanes((T, LANES), half, None, [])).astype(BF)

    aug = pl.BlockSpec((None, 2, T, LANES), lambda p, i: (p, 0, i, 0))
    tile = pl.BlockSpec((T, LANES), lambda p, i: (i, p))
    shape = jax.ShapeDtypeStruct((B_HEADS // 2, 2, S, LANES), BF)
    return pl.pallas_call(
        kern, name="fox_prep_bwd", grid=(B_HEADS // 2, S // T),
        in_specs=[aug, pl.BlockSpec((T, LANES), lambda p, i: (i, OFF_VB // LANES + p)), tile, tile, tile],
        out_specs=[aug, aug, aug], out_shape=[shape, shape, shape],
        compiler_params=_cparams("parallel", "parallel"),
    )(q_aug, p_b, o, do, lse)


def _fox_bwd(qb_aug, k_aug, dob_aug, vb_aug, T, comm=None):
    n_pairs, _, S, _ = qb_aug.shape
    nq = S // T
    grid = (n_pairs,)

    def kern(*refs):
        own, comm_refs = _own_refs(refs, comm, 4, 5, 0)
        q_ref, k_ref, do_ref, v_ref, dq_ref, dk_ref, dv_ref, dck_ref, dcq_ref = own
        _comm_edge(comm, comm_refs, grid, first=True)
        p_id = pl.program_id(0)
        rowcol = lax.broadcasted_iota(jnp.int32, (T, T), 0) - lax.broadcasted_iota(jnp.int32, (T, T), 1)
        lane = lax.broadcasted_iota(jnp.int32, (T, LANES), 1)
        dk_ref[...] = jnp.zeros_like(dk_ref)
        dv_ref[...] = jnp.zeros_like(dv_ref)
        dck_ref[...] = jnp.zeros_like(dck_ref)

        @pl.when(p_id == 0)
        def _():
            dcq_ref[...] = jnp.zeros_like(dcq_ref)

        hms = (_half_mask((T, LANES), 0), _half_mask((T, LANES), 1))

        def outer(i, carry):
            qrows = pl.ds(pl.multiple_of(i * T, T), T)
            qa = (q_ref[0, qrows, :], q_ref[1, qrows, :])
            doa = (do_ref[0, qrows, :], do_ref[1, qrows, :])
            q_own = [jnp.where(hms[h], qa[h], 0) for h in (0, 1)]
            do_own = [jnp.where(hms[h], doa[h], 0) for h in (0, 1)]

            def inner(j, carry_in, masked):
                krows = pl.ds(pl.multiple_of(j * T, T), T)
                dv_add, dk_add, new = 0.0, 0.0, []
                for half in (0, 1):
                    dq, rs = carry_in[half]
                    ka = k_ref[half, krows, :]
                    s = lax.dot_general(qa[half], ka, (((1,), (1,)), ((), ())), preferred_element_type=F32)
                    if masked:
                        s = jnp.where(rowcol >= 0, s, NEG_INF)
                    p = jnp.exp(s)
                    ds = p * lax.dot_general(doa[half], v_ref[half, krows, :], (((1,), (1,)), ((), ())),
                                             preferred_element_type=F32)
                    ds_b = ds.astype(BF)
                    dv_add = dv_add + lax.dot_general(p.astype(BF), do_own[half], (((0,), (0,)), ((), ())),
                                                      preferred_element_type=F32)
                    dk_add = dk_add + lax.dot_general(ds_b, q_own[half], (((0,), (0,)), ((), ())),
                                                      preferred_element_type=F32)
                    dck_ref[half:half + 1, krows] += jnp.sum(ds, axis=0, keepdims=True)
                    new.append((dq + jnp.dot(ds_b, jnp.where(hms[half], ka, 0), preferred_element_type=F32),
                                rs + jnp.sum(ds, axis=1, keepdims=True)))
                dv_ref[krows, :] += dv_add
                dk_ref[krows, :] += dk_add
                return tuple(new)

            one = (jnp.zeros((T, LANES), F32), jnp.zeros((T, 1), F32))
            carry_in = lax.fori_loop(0, i, functools.partial(inner, masked=False), (one, one))
            (dq0, rs0), (dq1, rs1) = inner(i, carry_in, True)
            dq_ref[qrows, :] = (dq0 + dq1) * QK_SCALE
            dcq_ref[qrows, :] = jnp.where(lane == 2 * p_id, rs0, jnp.where(lane == 2 * p_id + 1, rs1,
                                                                             dcq_ref[qrows, :]))
            return carry

        lax.fori_loop(0, nq, outer, 0)
        _comm_edge(comm, comm_refs, grid, first=False)

    aug = pl.BlockSpec((None, 2, S, LANES), lambda p: (p, 0, 0, 0))
    pair = pl.BlockSpec((S, LANES), lambda p: (0, p))
    wide = jax.ShapeDtypeStruct((S, n_pairs * LANES), F32)
    res = pl.pallas_call(
        kern, name="fox_bwd", grid=grid,
        in_specs=[aug, aug, aug, aug] + _comm_specs(comm, "in"),
        out_specs=[pair, pair, pair, pl.BlockSpec((None, 2, S), lambda p: (p, 0, 0)),
                   pl.BlockSpec((S, LANES), lambda p: (0, 0))] + _comm_specs(comm, "out"),
        out_shape=[wide, wide, wide, jax.ShapeDtypeStruct((n_pairs, 2, S), F32),
                   jax.ShapeDtypeStruct((S, LANES), F32)] + (comm.out_shapes if comm else []),
        scratch_shapes=comm.sem_shapes if comm else [],
        compiler_params=_cparams("arbitrary"),
    )(qb_aug, k_aug, dob_aug, vb_aug, *(comm.ins if comm else []))
    return (*res[:5], res[5:])


SWA_TQ = 256


def _swa_window(i, tq):
    start = pl.multiple_of(jnp.maximum(i * tq - WINDOW, 0), LANES)
    return start, i * tq - start


def _swa_valid(offset, tq):
    rel = offset + lax.broadcasted_iota(jnp.int32, (tq, tq + WINDOW), 0) \
        - lax.broadcasted_iota(jnp.int32, (tq, tq + WINDOW), 1)
    return (rel >= 0) & (rel < WINDOW)


def _swa_fwd(qk, v_arr, v_col, sinks):
    S = qk.shape[0]
    tq = min(SWA_TQ, S - WINDOW)
    win = tq + WINDOW

    def kern(q_ref, k_ref, v_ref, sink_ref, o_ref, lse_ref):
        p_id, i = pl.program_id(0), pl.program_id(1)
        start, offset = _swa_window(i, tq)
        kw = k_ref[pl.ds(start, win), :]
        vw = v_ref[pl.ds(start, win), :].astype(BF)
        valid = _swa_valid(offset, tq)
        q = q_ref[...]
        outs, lses = [], []
        for half in (0, 1):
            hm = _half_mask((tq, LANES), half)
            qh = (jnp.where(hm, q, 0).astype(F32) * QK_SCALE).astype(BF)
            s = lax.dot_general(qh, kw, (((1,), (1,)), ((), ())), preferred_element_type=F32)
            s = jnp.where(valid, s, NEG_INF)
            sink = sink_ref[2 * p_id + half]
            m = jnp.maximum(jnp.max(s, axis=1, keepdims=True), sink)
            p = jnp.exp(s - m)
            denom = jnp.sum(p, axis=1, keepdims=True) + jnp.exp(sink - m)
            outs.append(jnp.dot(p.astype(BF), vw, preferred_element_type=F32) / denom)
            lses.append(m + jnp.log(denom))
        hm0 = _half_mask((tq, LANES), 0)
        o_ref[...] = jnp.where(hm0, outs[0], outs[1])
        lse_ref[...] = jnp.where(hm0, lses[0], lses[1])

    tile = pl.BlockSpec((tq, LANES), lambda p, i: (i, p))
    return pl.pallas_call(
        kern, name="swa_fwd", grid=(A_Q_HEADS // 2, S // tq),
        in_specs=[tile, pl.BlockSpec((S, LANES), lambda p, i: (0, A_Q_HEADS // 2)),
                  pl.BlockSpec((S, LANES), lambda p, i: (0, v_col)),
                  pl.BlockSpec(memory_space=pltpu.SMEM)],
        out_specs=[tile, tile],
        out_shape=[jax.ShapeDtypeStruct((S, A_Q_HEADS * HEAD_DIM), F32)] * 2,
        compiler_params=_cparams("parallel", "arbitrary"),
    )(qk, qk, v_arr, sinks)


def _swa_bwd(qk, v_arr, v_col, o_arr, do_arr, lse_arr, sinks, comm=None):
    S = qk.shape[0]
    tq = min(SWA_TQ, S - WINDOW)
    win = tq + WINDOW
    n_pairs = A_Q_HEADS // 2
    grid = (n_pairs, S // tq)

    def kern(*refs):
        own, comm_refs = _own_refs(refs, comm, 7, 4, 0)
        q_ref, k_ref, v_ref, o_ref, do_ref, lse_ref, sink_ref, dq_ref, dk_ref, dv_ref, dsink_ref = own
        _comm_edge(comm, comm_refs, grid, first=True)
        p_id, i = pl.program_id(0), pl.program_id(1)

        @pl.when((p_id == 0) & (i == 0))
        def _():
            dk_ref[...] = jnp.zeros_like(dk_ref)
            dv_ref[...] = jnp.zeros_like(dv_ref)

        @pl.when(i == 0)
        def _():
            dsink_ref[...] = jnp.zeros_like(dsink_ref)

        start, offset = _swa_window(i, tq)
        wrows = pl.ds(start, win)
        kw = k_ref[wrows, :]
        vw = v_ref[wrows, :].astype(BF)
        valid = _swa_valid(offset, tq)
        q, do, o, lse2 = q_ref[...], do_ref[...], o_ref[...], lse_ref[...]
        dq = jnp.zeros((tq, LANES), F32)
        dk = jnp.zeros((win, LANES), F32)
        dv = jnp.zeros((win, LANES), F32)
        for half in (0, 1):
            hm = _half_mask((tq, LANES), half)
            lane0 = half * HEAD_DIM
            qh = (jnp.where(hm, q, 0).astype(F32) * QK_SCALE).astype(BF)
            do_f = jnp.where(hm, do, 0.0)
            doh = do_f.astype(BF)
            delta = jnp.sum(do_f * o, axis=1, keepdims=True)
            lse = lse2[:, lane0:lane0 + 1]
            s = lax.dot_general(qh, kw, (((1,), (1,)), ((), ())), preferred_element_type=F32)
            p = jnp.exp(jnp.where(valid, s, NEG_INF) - lse)
            dp = lax.dot_general(doh, vw, (((1,), (1,)), ((), ())), preferred_element_type=F32)
            ds_b = (p * (dp - delta)).astype(BF)
            dv = dv + lax.dot_general(p.astype(BF), doh, (((0,), (0,)), ((), ())), preferred_element_type=F32)
            dk = dk + lax.dot_general(ds_b, qh, (((0,), (0,)), ((), ())), preferred_element_type=F32)
            kh = jnp.where(_half_mask((win, LANES), half), kw, 0)
            dq = dq + jnp.dot(ds_b, kh, preferred_element_type=F32)
            p_sink = jnp.exp(sink_ref[2 * p_id + half] - lse)
            dsink_ref[0, half:half + 1, :] += jnp.broadcast_to(
                -jnp.sum(p_sink * delta, axis=0, keepdims=True), (1, LANES))
        dq_ref[...] = dq * QK_SCALE
        dk_ref[wrows, :] += dk
        dv_ref[wrows, :] += dv
        _comm_edge(comm, comm_refs, grid, first=False)

    tile = pl.BlockSpec((tq, LANES), lambda p, i: (i, p))
    whole = lambda col: pl.BlockSpec((S, LANES), lambda p, i: (0, col))
    res = pl.pallas_call(
        kern, name="swa_bwd", grid=grid,
        in_specs=[tile, whole(n_pairs), whole(v_col), tile, tile, tile,
                  pl.BlockSpec(memory_space=pltpu.SMEM)] + _comm_specs(comm, "in"),
        out_specs=[tile, whole(0), whole(0),
                   pl.BlockSpec((1, 8, LANES), lambda p, i: (p, 0, 0))] + _comm_specs(comm, "out"),
        out_shape=[jax.ShapeDtypeStruct((S, A_Q_HEADS * HEAD_DIM), F32),
                   jax.ShapeDtypeStruct((S, LANES), F32), jax.ShapeDtypeStruct((S, LANES), F32),
                   jax.ShapeDtypeStruct((n_pairs, 8, LANES), F32)] + (comm.out_shapes if comm else []),
        scratch_shapes=comm.sem_shapes if comm else [],
        compiler_params=_cparams("arbitrary", "arbitrary"),
    )(qk, qk, v_arr, o_arr, do_arr, lse_arr, sinks, *(comm.ins if comm else []))
    return (*res[:4], res[4:])


ADAMW_BLOCK = 256 * 1024


def _adamw(w, g, m, v, name):
    R, C = w.shape
    tr, tc = _tile(R, max(8, ADAMW_BLOCK // C), 8), C

    def kern(w_ref, g_ref, m_ref, v_ref, d_ref, mo_ref, vo_ref):
        g_ = g_ref[...]
        m_new = ADAM_B1 * m_ref[...] + (1.0 - ADAM_B1) * g_
        v_new = ADAM_B2 * v_ref[...] + (1.0 - ADAM_B2) * (g_ * g_)
        m_hat = m_new / (1.0 - ADAM_B1 ** ADAM_STEP)
        v_hat = v_new / (1.0 - ADAM_B2 ** ADAM_STEP)
        d_ref[...] = -ADAM_LR * (m_hat / (jnp.sqrt(v_hat) + ADAM_EPS) + ADAM_WD * w_ref[...])
        mo_ref[...] = m_new
        vo_ref[...] = v_new

    spec = pl.BlockSpec((tr, tc), lambda i, j: (i, j))
    shape = jax.ShapeDtypeStruct((R, C), F32)
    return pl.pallas_call(
        kern, name=name, grid=(R // tr, C // tc),
        in_specs=[spec] * 4, out_specs=[spec] * 3, out_shape=[shape] * 3,
        compiler_params=_cparams("parallel", "parallel"),
    )(w, g, m, v)


def _index_operand(i):
    return jnp.reshape(i, (1,)).astype(jnp.int32)


def _add_pair(whole, got, ci, name):
    P, R, C = whole.shape
    half = R // 2
    tr = _tile(half, 256, 16)
    nb = half // tr

    def kern(ci_ref, a_ref, b_ref, o_ref, ob_ref):
        s = a_ref[...] + b_ref[...].astype(F32)
        o_ref[...] = s
        ob_ref[...] = s.astype(BF)

    spec = pl.BlockSpec((None, tr, C), lambda p, i, ci_ref: (p, i, 0))
    return pl.pallas_call(
        kern, name=name,
        grid_spec=pltpu.PrefetchScalarGridSpec(
            num_scalar_prefetch=1, grid=(P, nb),
            in_specs=[pl.BlockSpec((None, tr, C), lambda p, i, ci_ref: (p, ci_ref[0] * nb + i, 0)), spec],
            out_specs=[spec, spec]),
        out_shape=[jax.ShapeDtypeStruct((P, half, C), F32), jax.ShapeDtypeStruct((P, half, C), BF)],
        compiler_params=_cparams("parallel", "parallel"),
    )(_index_operand(ci), whole, got)


def _add_three(parts, recv, chip, name):
    _, R, C = parts.shape
    tr = _tile(R, 256, 16)

    def kern(chip_ref, o_ref, r0_ref, r1_ref, r2_ref, out_ref):
        s = ((o_ref[...] + r0_ref[...].astype(F32)) + r1_ref[...].astype(F32)) + r2_ref[...].astype(F32)
        out_ref[0] = s
        out_ref[1] = s

    slab = lambda k: pl.BlockSpec((None, tr, C), lambda i, chip_ref: (k, i, 0))
    return pl.pallas_call(
        kern, name=name,
        grid_spec=pltpu.PrefetchScalarGridSpec(
            num_scalar_prefetch=1, grid=(R // tr,),
            in_specs=[pl.BlockSpec((None, tr, C), lambda i, chip_ref: (chip_ref[0], i, 0)),
                      slab(0), slab(1), slab(2)],
            out_specs=pl.BlockSpec((2, tr, C), lambda i, chip_ref: (0, i, 0))),
        out_shape=jax.ShapeDtypeStruct((2, R, C), F32),
        compiler_params=_cparams("parallel"),
    )(_index_operand(chip), parts, recv, recv, recv)


SM_ADA, SM_G, SM_LOSS, SM_BF, SM_SINK, SM_LEN = 0, 6144, 10240, 11264, 11272, 12288


def _small_finalize(gathered):
    def kern(g_ref, tot_ref, loss_ref):
        tot = g_ref[0:1, :]
        for b in range(1, N_DEV):
            tot = tot + g_ref[b:b + 1, :]
        tot_ref[...] = tot
        sq = jnp.sum(tot[:, SM_LOSS:SM_LOSS + D_MODEL], axis=1, keepdims=True)
        loss_ref[...] = jnp.broadcast_to(sq * (0.5 / D_MODEL), (1, LANES))

    full = lambda shape: pl.BlockSpec(shape, lambda i: (0, 0))
    return pl.pallas_call(
        kern, name="small_finalize", grid=(1,),
        in_specs=[full((N_DEV, SM_LEN))],
        out_specs=[full((1, SM_LEN)), full((1, LANES))],
        out_shape=[jax.ShapeDtypeStruct((1, SM_LEN), F32), jax.ShapeDtypeStruct((1, LANES), F32)],
        compiler_params=_cparams("arbitrary"),
    )(gathered)


def _ada_dw(c_t, d_ada):
    N = d_ada.shape[1]
    tn = _tile(N, 512)

    def kern(c_ref, d_ref, o_ref):
        acc = c_ref[:, 0:1] * d_ref[0:1, :]
        for b in range(1, N_DEV):
            acc = acc + c_ref[:, b:b + 1] * d_ref[b:b + 1, :]
        o_ref[...] = acc

    return pl.pallas_call(
        kern, name="ada_dw", grid=(N // tn,),
        in_specs=[pl.BlockSpec((D_MODEL, N_DEV), lambda j: (0, 0)), pl.BlockSpec((N_DEV, tn), lambda j: (0, j))],
        out_specs=pl.BlockSpec((D_MODEL, tn), lambda j: (0, j)),
        out_shape=jax.ShapeDtypeStruct((D_MODEL, N), F32),
        compiler_params=_cparams("parallel"),
    )(c_t, d_ada)


def _here():
    return lax.axis_index("x"), lax.axis_index("y"), lax.axis_index("c")


def _other_chips(x, y):
    return [(1 - x, y), (x, 1 - y), (1 - x, 1 - y)]


_ANY = pl.BlockSpec(memory_space=pl.ANY)


class _Comm:
    def __init__(self, ins, out_shapes, sem_shapes, start, finish):
        self.ins, self.out_shapes, self.sem_shapes = list(ins), list(out_shapes), list(sem_shapes)
        self.start, self.finish = start, finish

    def split(self, refs, n_in, n_out, n_scratch):
        a = n_in + len(self.ins)
        b = a + n_out + len(self.out_shapes)
        own = list(refs[:n_in]) + list(refs[a:a + n_out]) + list(refs[b:b + n_scratch])
        mine = (refs[n_in:a], refs[a + n_out:b], refs[b + n_scratch:])
        return own, mine


def _run_comm(comm, name):
    n_in, n_out = len(comm.ins), len(comm.out_shapes)

    def body(*refs):
        parts = (refs[:n_in], refs[n_in:n_in + n_out], refs[n_in + n_out:])
        comm.start(*parts)
        comm.finish(*parts)

    return pl.pallas_call(
        body, name=name,
        in_specs=[_ANY] * n_in, out_specs=[_ANY] * n_out,
        out_shape=comm.out_shapes, scratch_shapes=comm.sem_shapes,
    )(*comm.ins)


def _gather_comm(blocks):
    L = len(blocks)

    def parts(ins, outs, sems):
        send_sems, recv_sems, local_sems = sems
        x, y, c = _here()
        me, sibling = (x, y, c), (x, y, 1 - c)
        chips = _other_chips(x, y)

        def slot(px, py, pc):
            return 4 * px + 2 * py + pc

        def copy(l, k, block, to, src=None):
            dst = outs[l].at[slot(*block)]
            return pltpu.make_async_remote_copy(
                src_ref=dst if src is None else src, dst_ref=dst,
                send_sem=send_sems.at[l, k], recv_sem=recv_sems.at[l, k],
                device_id=to, device_id_type=MESH)

        mine = [pltpu.make_async_copy(ins[l], outs[l].at[slot(*me)], local_sems.at[l]) for l in range(L)]
        first = []
        for l in range(L):
            first.append(copy(l, 0, me, sibling, src=ins[l]))
            for j, chip in enumerate(chips):
                first.append(copy(l, 1 + j, me, (*chip, c), src=ins[l]))
        return c, me, sibling, chips, copy, mine, first

    def start(ins, outs, sems):
        *_, mine, first = parts(ins, outs, sems)
        for cp in mine + first:
            cp.start()

    def finish(ins, outs, sems):
        c, me, sibling, chips, copy, mine, first = parts(ins, outs, sems)
        passed = []
        for j, chip in enumerate(chips):
            for l in range(L):
                copy(l, 1 + j, (*chip, c), me).wait_recv()
                fwd = copy(l, 4 + j, (*chip, c), sibling)
                fwd.start()
                passed.append(fwd)
        for l in range(L):
            copy(l, 0, sibling, me).wait_recv()
        for j, chip in enumerate(chips):
            for l in range(L):
                copy(l, 4 + j, (*chip, 1 - c), me).wait_recv()
        for cp in first + passed:
            cp.wait_send()
        for cp in mine:
            cp.wait()

    return _Comm(blocks, [jax.ShapeDtypeStruct((N_DEV,) + b.shape, b.dtype) for b in blocks],
                 [pltpu.SemaphoreType.DMA((L, 7)), pltpu.SemaphoreType.DMA((L, 7)), pltpu.SemaphoreType.DMA((L,))],
                 start, finish)


def _allgather8(blocks, name):
    return _run_comm(_gather_comm(blocks), name)


def _swap_comm(arrs):
    L = len(arrs)

    def copies(ins, outs, sems):
        send_sems, recv_sems = sems
        x, y, c = _here()
        cps = []
        for l in range(L):
            half = arrs[l].shape[1] // 2
            rows = pl.ds(pl.multiple_of((1 - c) * half, 16), half)
            cps.append(pltpu.make_async_remote_copy(
                src_ref=ins[l].at[:, rows, :], dst_ref=outs[l], send_sem=send_sems.at[l],
                recv_sem=recv_sems.at[l], device_id=(x, y, 1 - c), device_id_type=MESH))
        return cps

    def start(ins, outs, sems):
        for cp in copies(ins, outs, sems):
            cp.start()

    def finish(ins, outs, sems):
        for cp in copies(ins, outs, sems):
            cp.wait()

    return _Comm(arrs, [jax.ShapeDtypeStruct((a.shape[0], a.shape[1] // 2, a.shape[2]), a.dtype) for a in arrs],
                 [pltpu.SemaphoreType.DMA((L,)), pltpu.SemaphoreType.DMA((L,))], start, finish)


def _sibling_join(bufs, name):
    L = len(bufs)

    def body(*refs):
        outs = refs[L:2 * L]
        send_sems, recv_sems = refs[2 * L:]
        x, y, c = _here()
        for l in range(L):
            pltpu.make_async_remote_copy(src_ref=outs[l].at[c], dst_ref=outs[l].at[c], send_sem=send_sems.at[l],
                                         recv_sem=recv_sems.at[l], device_id=(x, y, 1 - c),
                                         device_id_type=MESH).start()
        for l in range(L):
            pltpu.make_async_remote_copy(src_ref=outs[l].at[c], dst_ref=outs[l].at[1 - c],
                                         send_sem=send_sems.at[l], recv_sem=recv_sems.at[l],
                                         device_id=(x, y, 1 - c), device_id_type=MESH).wait()

    return pl.pallas_call(
        body, name=name,
        in_specs=[_ANY] * L, out_specs=[_ANY] * L,
        out_shape=[jax.ShapeDtypeStruct(a.shape, a.dtype) for a in bufs],
        input_output_aliases={l: l for l in range(L)},
        scratch_shapes=[pltpu.SemaphoreType.DMA((L,)), pltpu.SemaphoreType.DMA((L,))],
    )(*bufs)


def _scatter_comm(arrs):
    L = len(arrs)

    def copies(ins, outs, sems):
        send_sems, recv_sems = sems
        x, y, c = _here()
        return [pltpu.make_async_remote_copy(
            src_ref=ins[l].at[2 * tx + ty], dst_ref=outs[l].at[j],
            send_sem=send_sems.at[l, j], recv_sem=recv_sems.at[l, j],
            device_id=(tx, ty, c), device_id_type=MESH)
            for l in range(L) for j, (tx, ty) in enumerate(_other_chips(x, y))]

    def start(ins, outs, sems):
        for cp in copies(ins, outs, sems):
            cp.start()

    def finish(ins, outs, sems):
        for cp in copies(ins, outs, sems):
            cp.wait()

    return _Comm(arrs, [jax.ShapeDtypeStruct((3,) + a.shape[1:], a.dtype) for a in arrs],
                 [pltpu.SemaphoreType.DMA((L, 3)), pltpu.SemaphoreType.DMA((L, 3))], start, finish)


_A_ORDER = np.array(A_HEAD_ORDER)
_A_INVERSE = np.argsort(_A_ORDER)


def _permute_in_weights(w_in):
    qa = w_in[:, 0:512].reshape(D_MODEL, A_Q_HEADS, HEAD_DIM)[:, _A_ORDER, :].reshape(D_MODEL, 512)
    f_pad = jnp.pad(w_in[:, 2304:2312], ((0, 0), (0, LANES - B_HEADS)))
    w_a = jnp.concatenate([qa, w_in[:, 512:640], f_pad], axis=1)
    return w_a, w_in[:, 640:2304], w_in[:, 2312:4360]


def _unpermute_in_grads(dw_perm):
    qa = dw_perm[:, 0:512].reshape(D_MODEL, A_Q_HEADS, HEAD_DIM)[:, _A_INVERSE, :].reshape(D_MODEL, 512)
    return jnp.concatenate([qa, dw_perm[:, 512:640], dw_perm[:, W_A:W_A + W_B],
                            dw_perm[:, OFF_F:OFF_F + B_HEADS], dw_perm[:, W_A + W_B:]], axis=1)


class _NoExchange:
    def __init__(self, rest):
        self.rest, self.grads = rest, {}

    def rest_weights_comm(self):
        return None

    def rest_weights(self, outs):
        return self.rest

    def swap_comm(self, pieces, tag):
        self.grads[tag] = [p32 for p32, _ in pieces]
        return None

    def swap_done(self, outs, tag):
        return None

    def reduce_done(self, outs, tag):
        pass


class _Exchange:
    def __init__(self, ci, chip, rest_shards):
        self.ci, self.chip, self.rest_shards = ci, chip, rest_shards
        self.pieces, self.part_f32, self.halves = {}, {}, {}

    def _my_half(self, a, axis=0, other=False):
        rows = a.shape[axis] // 2
        return lax.dynamic_slice_in_dim(a, ((1 - self.ci) if other else self.ci) * rows, rows, axis=axis)

    def rest_weights_comm(self):
        return _gather_comm([self._my_half(w).astype(BF) for w in self.rest_shards])

    def rest_weights(self, outs):
        w_ba, w_bb, w_out, w_fi, w_fo = outs
        return (_col_sharded(w_ba), _col_sharded(w_bb), _row_sharded(w_out), _col_sharded(w_fi),
                _row_sharded(w_fo))

    def swap_comm(self, pieces, tag):
        self.pieces[tag] = pieces
        return _swap_comm([pbf for _, pbf in pieces])

    def swap_done(self, got, tag):
        self.part_f32[tag], part_bf = [], []
        for l, ((p32, _), g_) in enumerate(zip(self.pieces[tag], got)):
            s32, sbf = _add_pair(p32, g_, self.ci, f"chip_sum_{tag}_{l}")
            self.part_f32[tag].append(s32)
            part_bf.append(sbf)
        return _scatter_comm(part_bf)

    def reduce_done(self, outs, tag):
        self.halves[tag] = [_add_three(p32, r, self.chip, f"shard_sum_{tag}_{l}")
                            for l, (p32, r) in enumerate(zip(self.part_f32[tag], outs))]


def _col_sharded(g):
    return jnp.transpose(g.reshape(N_CHIP, -1, g.shape[-1]), (1, 0, 2)).reshape(2 * g.shape[1], N_CHIP * g.shape[-1])


def _row_sharded(g):
    return g.reshape(N_DEV * g.shape[1], g.shape[-1])


def _rope_tables(pos):
    inv_freq = 1.0 / (ROPE_THETA ** (jnp.arange(0, HEAD_DIM, 2, dtype=F32) / HEAD_DIM))
    ang = pos.astype(F32)[:, None] * inv_freq
    cos, sin = jnp.cos(ang), jnp.sin(ang)
    return jnp.tile(cos, (1, 4)), jnp.tile(jnp.concatenate([-sin, sin], axis=1), (1, 2))


def _local_step(x, pos, ada, g1, g2, g3, g4, b_f, sinks, w_in, exch, target):
    S = x.shape[0]
    t_fox = _tile(S, 512, LANES) if S >= 1024 else S // 2
    shift_m, scale_m, gate_m, shift_f, scale_f, gate_f = [ada[i:i + 1] for i in range(N_ADA)]
    cos_t, sin_t = _rope_tables(pos)
    w_a, w_b, w_g = _permute_in_weights(w_in)
    w_perm = jnp.concatenate([w_a, w_b, w_g], axis=1)
    sinks_p = sinks.reshape(A_KV_HEADS, 4).T.reshape(A_Q_HEADS)
    b_f_pad = jnp.pad(b_f, (0, LANES - B_HEADS)).reshape(1, LANES)

    h1 = _pre_norm(x, g1, scale_m, shift_m, "pre_mix_norm")
    p_a = _mm(h1, w_a, "nn", F32, "proj_a")
    p_b = _mm(h1, w_b, "nn", BF, "proj_b")
    p_g = _mm(h1, w_g, "nn", BF, "proj_g")
    (qk_a,) = _rope([p_a], [640], cos_t, sin_t, "rope_fwd")
    o_a, lse_a = _swa_fwd(qk_a, p_b, 0, sinks_p)
    q_aug, k_aug = _fox_prep_fwd(p_b, _fox_gate_fwd(p_a, b_f_pad), t_fox)
    comm = exch.rest_weights_comm()
    o_b, lse_b, outs = _fox_fwd(q_aug, k_aug, p_b, t_fox, comm=comm)
    w_ba, w_bb, w_out, w_fi, w_fo = exch.rest_weights(outs)
    w_ba_p = w_ba.reshape(A_Q_HEADS, HEAD_DIM, D_MODEL)[_A_ORDER].reshape(512, D_MODEL)
    pa = _mm(o_a, w_ba_p, "nn", F32, "branch_a")
    pb = _mm(o_b, w_bb, "nn", F32, "branch_b")
    merged = _merge_fwd(p_g, pa, pb)
    y1 = _mm(merged, w_out, "nn", F32, "out_proj")
    x2, h2 = _post_pre(x, y1, g2, gate_m, g3, scale_f, shift_f)
    gu = _mm(h2, w_fi, "nn", BF, "ffn_in")
    act = _swiglu_fwd(gu)
    y2 = _mm(act, w_fo, "nn", F32, "ffn_out")
    d_out, d_y2, st_f = _final(x2, y2, g4, gate_f, target)

    d_act = _mm(d_y2, w_fo, "nt", F32, "ffn_out_dx")
    row_pieces = lambda pair: tuple(t.reshape(N_CHIP, t.shape[0] // N_CHIP, t.shape[1]) for t in pair)
    dw_fo = row_pieces(_mm(act, d_y2, "tn", F32, "ffn_out_dw", twin=True))
    d_gu = _swiglu_bwd(d_act, gu)
    d_h2 = _mm(d_gu, w_fi, "nt", F32, "ffn_in_dx")
    dw_fi = _mm(h2, d_gu, "tn", F32, "ffn_in_dw", col_pieces=N_CHIP, twin=True)
    d_x2, d_y1, st_m = _mid_bwd(d_h2, x2, d_out, y1, g3, scale_f, g2, gate_m)
    d_merged = _mm(d_y1, w_out, "nt", F32, "out_proj_dx")
    dw_out = row_pieces(_mm(merged, d_y1, "tn", F32, "out_proj_dw", twin=True))
    d_pa, d_pb, d_ga, d_gb = _merge_bwd(d_merged, p_g, pa, pb)
    d_oa = _mm(d_pa, w_ba_p, "nt", F32, "branch_a_dx")
    dw_ba_p = _mm(o_a, d_pa, "tn", F32, "branch_a_dw", col_pieces=N_CHIP, twin=True)
    d_ob = _mm(d_pb, w_bb, "nt", F32, "branch_b_dx")
    dw_bb = _mm(o_b, d_pb, "tn", F32, "branch_b_dw", col_pieces=N_CHIP, twin=True)
    head_rows = lambda t: t.reshape(N_CHIP, A_Q_HEADS, HEAD_DIM, -1)[:, _A_INVERSE].reshape(t.shape)
    dw_ba = tuple(head_rows(t) for t in dw_ba_p)
    comm = exch.swap_comm([dw_ba, dw_bb, dw_out, dw_fi, dw_fo], "early")
    dq_a, dk_a, dv_a, d_sink, outs = _swa_bwd(qk_a, p_b, 0, o_a, d_oa, lse_a, sinks_p, comm=comm)
    comm = exch.swap_done(outs, "early")
    qb_aug, dob_aug, vb_aug = _fox_prep_bwd(q_aug, p_b, o_b, d_ob, lse_b, t_fox)
    dq_b, dk_b, dv_b, d_ck, d_cq, outs = _fox_bwd(qb_aug, k_aug, dob_aug, vb_aug, t_fox, comm=comm)
    exch.reduce_done(outs, "early")
    d_qa, d_ka = _rope([dq_a, dk_a], [512, LANES], cos_t, -sin_t, "rope_bwd")
    d_ck_cols = jnp.pad(d_ck.reshape(B_HEADS, S).T, ((0, 0), (0, LANES - B_HEADS)))
    d_f, d_bf = _fox_gate_bwd(d_cq, d_ck_cols, p_a, b_f_pad)
    d_proj = jnp.concatenate([d_qa, d_ka, d_f, dv_a.astype(BF), dq_b.astype(BF), dk_b.astype(BF),
                              dv_b.astype(BF), d_ga, d_gb], axis=1)
    dw_perm = _mm(h1, d_proj, "tn", F32, "proj_dw")
    dw_in = jnp.transpose(_unpermute_in_grads(dw_perm).reshape(D_MODEL, N_CHIP, -1), (1, 0, 2))
    swap = exch.swap_comm([(dw_in, dw_in.astype(BF))], "late")
    comm = exch.swap_done(_run_comm(swap, "grads_to_sibling_late") if swap else None, "late")
    res = _mm(d_proj, w_perm, "nt", F32, "proj_dx", comm=comm)
    d_h1 = res[0] if comm else res
    exch.reduce_done(res[1] if comm else None, "late")
    grad_x, st_p = _pre_bwd(d_h1, x, d_x2, g1, scale_m)

    d_sinks = d_sink[:, :2, 0].T.reshape(A_Q_HEADS)
    small = jnp.concatenate([
        st_p[0], st_p[1], st_m[3], st_m[0], st_m[1], st_f[0],
        st_p[2], st_m[4], st_m[2], st_f[1],
        st_f[2], d_bf[0, :B_HEADS], d_sinks,
        jnp.zeros((SM_LEN - SM_SINK - A_Q_HEADS,), F32)])
    return grad_x, small


def kernel(x, c, positions, w_ada, b_ada, g_pre_mix, g_post_mix, w_in, b_f, sinks, w_branch_a, w_branch_b, w_out, g_pre_ffn, g_post_ffn, w_ffn_in, w_ffn_out, loss_target, m_w_ada, m_b_ada, m_g_pre_mix, m_g_post_mix, m_w_in, m_b_f, m_sinks, m_w_branch_a, m_w_branch_b, m_w_out, m_g_pre_ffn, m_g_post_ffn, m_w_ffn_in, m_w_ffn_out, v_w_ada, v_b_ada, v_g_pre_mix, v_g_post_mix, v_w_in, v_b_f, v_sinks, v_w_branch_a, v_w_branch_b, v_w_out, v_g_pre_ffn, v_g_post_ffn, v_w_ffn_in, v_w_ffn_out):
    xi, yi, ci = _here()
    chip = 2 * xi + yi
    dev = 2 * chip + ci

    def my_half(a):
        rows = a.shape[0] // 2
        return lax.dynamic_slice_in_dim(a, ci * rows, rows, axis=0)

    c_g, w_in_g = _allgather8([c.reshape(8, LANES), my_half(w_in[0]).astype(BF)], "gather_w_in")
    c_all = c_g.reshape(N_DEV, D_MODEL)
    w_in_f = _col_sharded(w_in_g)
    exch = _Exchange(ci, chip, [w_branch_a[0], w_branch_b[0], w_out[0], w_ffn_in[0], w_ffn_out[0]])

    ada_cols = _mm(c_all, w_ada[0], "nn", F32, "ada_fwd")
    (ada_g,) = _allgather8([ada_cols], "gather_ada")
    ada_mine = lax.dynamic_index_in_dim(ada_g.reshape(N_CHIP, 2, N_DEV, -1)[:, 0], dev, axis=1, keepdims=False)
    ada = (ada_mine.reshape(-1) + b_ada[0]).reshape(N_ADA, D_MODEL)

    grad_x, small = _local_step(
        x[0], positions[0], ada, g_pre_mix, g_post_mix, g_pre_ffn, g_post_ffn, b_f[0], sinks[0],
        w_in_f, exch, loss_target[0])

    (small_g,) = _allgather8([small.reshape(8, SM_LEN // 8)], "gather_small")
    small_all = small_g.reshape(N_DEV, SM_LEN)
    small_tot, loss_row = _small_finalize(small_all)
    loss = loss_row[0, 0]
    d_ada_cols = lax.dynamic_slice_in_dim(small_all[:, :N_ADA * D_MODEL], chip * (N_ADA * D_MODEL // N_CHIP),
                                          N_ADA * D_MODEL // N_CHIP, axis=1)
    g_w_ada = _ada_dw(c_all.T, d_ada_cols)

    joined = _sibling_join(exch.halves["late"] + exch.halves["early"], "grads_join")
    g_w_in, g_w_ba, g_w_bb, g_w_out, g_w_fi, g_w_fo = [j.reshape(2 * j.shape[1], j.shape[2]) for j in joined]

    def small_vec(b_ada_, g1_, g2_, g3_, g4_, b_f_, sinks_):
        return jnp.concatenate([b_ada_[0], g1_[0], g2_[0], g3_[0], g4_[0], jnp.zeros((D_MODEL,), F32),
                                b_f_[0], sinks_[0], jnp.zeros((SM_LEN - SM_SINK - A_Q_HEADS,), F32)]
                               ).reshape(8, SM_LEN // 8)

    sw = small_vec(b_ada, g_pre_mix, g_post_mix, g_pre_ffn, g_post_ffn, b_f, sinks)
    sm = small_vec(m_b_ada, m_g_pre_mix, m_g_post_mix, m_g_pre_ffn, m_g_post_ffn, m_b_f, m_sinks)
    sv = small_vec(v_b_ada, v_g_pre_mix, v_g_post_mix, v_g_pre_ffn, v_g_post_ffn, v_b_f, v_sinks)
    s_upd = [u.reshape(SM_LEN) for u in _adamw(sw, small_tot.reshape(8, SM_LEN // 8), sm, sv, "adamw_small")]
    s_grad = small_tot.reshape(SM_LEN)

    def unpack(vec):
        row = lambda a, n: vec[a:a + n].reshape(1, n)
        return dict(b_ada=row(SM_ADA, N_ADA * D_MODEL), g_pre_mix=row(SM_G, D_MODEL),
                    g_post_mix=row(SM_G + D_MODEL, D_MODEL), g_pre_ffn=row(SM_G + 2 * D_MODEL, D_MODEL),
                    g_post_ffn=row(SM_G + 3 * D_MODEL, D_MODEL), b_f=row(SM_BF, B_HEADS),
                    sinks=row(SM_SINK, A_Q_HEADS))

    big = dict(
        w_ada=(w_ada, g_w_ada, m_w_ada, v_w_ada), w_in=(w_in, g_w_in, m_w_in, v_w_in),
        w_branch_a=(w_branch_a, g_w_ba, m_w_branch_a, v_w_branch_a),
        w_branch_b=(w_branch_b, g_w_bb, m_w_branch_b, v_w_branch_b),
        w_out=(w_out, g_w_out, m_w_out, v_w_out), w_ffn_in=(w_ffn_in, g_w_fi, m_w_ffn_in, v_w_ffn_in),
        w_ffn_out=(w_ffn_out, g_w_fo, m_w_ffn_out, v_w_ffn_out))
    grads, deltas, new_m, new_v = unpack(s_grad), unpack(s_upd[0]), unpack(s_upd[1]), unpack(s_upd[2])
    for n, (w_, g_, m_, v_) in big.items():
        d_, nm_, nv_ = _adamw(w_[0], g_, m_[0], v_[0], "adamw_" + n)
        grads[n], deltas[n], new_m[n], new_v[n] = g_[None], d_[None], nm_[None], nv_[None]

    names = ["w_ada", "b_ada", "g_pre_mix", "g_post_mix", "w_in", "b_f", "sinks", "w_branch_a", "w_branch_b",
             "w_out", "g_pre_ffn", "g_post_ffn", "w_ffn_in", "w_ffn_out"]
    return (loss, grad_x[None], *[grads[n] for n in names], *[deltas[n] for n in names],
            *[new_m[n] for n in names], *[new_v[n] for n in names])
```

```python
import functools
import math

import numpy as np
import jax
import jax.numpy as jnp
from jax import lax
from jax.experimental import pallas as pl
from jax.experimental.pallas import tpu as pltpu

F32 = jnp.float32
BF = jnp.bfloat16

D_MODEL = 1024
HEAD_DIM = 64
LANES = 128
WINDOW = 128
A_Q_HEADS = 8
A_KV_HEADS = 2
B_HEADS = 8
D_FF = 2816
ROPE_THETA = 10000.0
RMS_EPS = 1e-6
N_ADA = 6
N_DEV = 8
N_CHIP = 4

ADAM_LR = 0.001
ADAM_B1 = 0.9
ADAM_B2 = 0.999
ADAM_EPS = 1e-08
ADAM_WD = 0.01
ADAM_STEP = 10

VMEM_LIMIT = 48 * 1024 * 1024
MESH = pl.DeviceIdType.MESH

A_HEAD_ORDER = (0, 4, 1, 5, 2, 6, 3, 7)

OFF_QA, OFF_KA, OFF_F = 0, 512, 640
W_A = 768
OFF_VA, OFF_QB, OFF_KB, OFF_VB = 0, 128, 640, 1152
W_B = 1664
W_G = 2048
W_PERM = W_A + W_B + W_G
W_SHARD = 1090
W_SHARD_PAD = 1152


def _tile(n, cap, mult=LANES):
    if n <= cap:
        return n
    t = (cap // mult) * mult
    while t >= mult:
        if n % t == 0:
            return t
        t -= mult
    raise ValueError(f"no tile for {n}")


MXU_WIDTH = 256
MM_OPERAND_BYTES = 28 * 1024 * 1024


def _mm_tiles(M, N, K, a_bytes, b_bytes, tm_cap, tn_cap):
    tm = _tile(M, tm_cap)
    try:
        tn = _tile(N, tn_cap, MXU_WIDTH)
    except ValueError:
        tn = _tile(N, tn_cap)
    fits = lambda tk: 2 * tk * (tm * a_bytes + tn * b_bytes) <= MM_OPERAND_BYTES
    tk = K if fits(K) else next(t for t in range(K // LANES * LANES, 0, -LANES) if K % t == 0 and fits(t))
    return tm, tn, tk


def _cparams(*sem):
    return pltpu.CompilerParams(dimension_semantics=sem, vmem_limit_bytes=VMEM_LIMIT)


def _own_refs(refs, comm, n_in, n_out, n_scratch):
    if comm is None:
        return list(refs), None
    return comm.split(refs, n_in, n_out, n_scratch)


def _comm_specs(comm, side):
    if comm is None:
        return []
    return [pl.BlockSpec(memory_space=pl.ANY)] * len(comm.ins if side == "in" else comm.out_shapes)


def _comm_edge(comm, comm_refs, grid, first):
    if comm is None:
        return
    at_edge = None
    for axis, n in enumerate(grid):
        here = pl.program_id(axis) == (0 if first else n - 1)
        at_edge = here if at_edge is None else at_edge & here
    pl.when(at_edge)(lambda: (comm.start if first else comm.finish)(*comm_refs))


def _mm(a, b, mode, out_dtype, name, tm_cap=512, tn_cap=2816, comm=None, col_pieces=1, twin=False):
    if mode == "nn":
        (M, K), (K2, N) = a.shape, b.shape
        dims = (((1,), (0,)), ((), ()))
    elif mode == "nt":
        (M, K), (N, K2) = a.shape, b.shape
        dims = (((1,), (1,)), ((), ()))
    else:
        (K, M), (K2, N) = a.shape, b.shape
        dims = (((0,), (0,)), ((), ()))
    assert K == K2, (a.shape, b.shape, mode)
    tm, tn, tk = _mm_tiles(M, N // col_pieces, K, a.dtype.itemsize, b.dtype.itemsize, tm_cap, tn_cap)
    nk = K // tk
    n_out = 2 if twin else 1
    n_scratch = 1 if nk > 1 else 0
    if mode == "nn":
        a_spec = pl.BlockSpec((tm, tk), lambda i, j, k: (i, k))
        b_spec = pl.BlockSpec((tk, tn), lambda i, j, k: (k, j))
    elif mode == "nt":
        a_spec = pl.BlockSpec((tm, tk), lambda i, j, k: (i, k))
        b_spec = pl.BlockSpec((tn, tk), lambda i, j, k: (j, k))
    else:
        a_spec = pl.BlockSpec((tk, tm), lambda i, j, k: (k, i))
        b_spec = pl.BlockSpec((tk, tn), lambda i, j, k: (k, j))

    grid = (M // tm, N // tn, nk)

    def kern(*refs):
        own, comm_refs = _own_refs(refs, comm, 2, n_out, n_scratch)
        a_ref, b_ref, o_refs = own[0], own[1], own[2:2 + n_out]
        k = pl.program_id(2)
        _comm_edge(comm, comm_refs, grid, first=True)
        part = lax.dot_general(a_ref[...].astype(BF), b_ref[...].astype(BF), dims,
                               preferred_element_type=F32)
        if nk == 1:
            for o_ref in o_refs:
                o_ref[...] = part.astype(o_ref.dtype)
        else:
            acc_ref = own[2 + n_out]

            @pl.when(k == 0)
            def _():
                acc_ref[...] = part

            @pl.when(k > 0)
            def _():
                acc_ref[...] += part

            @pl.when(k == nk - 1)
            def _():
                for o_ref in o_refs:
                    o_ref[...] = acc_ref[...].astype(o_ref.dtype)

        _comm_edge(comm, comm_refs, grid, first=False)

    if col_pieces > 1:
        per = N // col_pieces // tn
        out_spec = pl.BlockSpec((None, tm, tn), lambda i, j, k: (j // per, i, j % per))
        shape = (col_pieces, M, N // col_pieces)
    else:
        out_spec = pl.BlockSpec((tm, tn), lambda i, j, k: (i, j))
        shape = (M, N)
    dtypes = [out_dtype, BF] if twin else [out_dtype]
    res = pl.pallas_call(
        kern, name=name, grid=grid,
        in_specs=[a_spec, b_spec] + _comm_specs(comm, "in"),
        out_specs=[out_spec] * n_out + _comm_specs(comm, "out"),
        out_shape=[jax.ShapeDtypeStruct(shape, d) for d in dtypes] + (comm.out_shapes if comm else []),
        scratch_shapes=[pltpu.VMEM((tm, tn), F32)] * n_scratch + (comm.sem_shapes if comm else []),
        compiler_params=_cparams("parallel", "parallel", "arbitrary"),
    )(a, b, *(comm.ins if comm else []))
    own = res[0] if n_out == 1 else tuple(res[:n_out])
    return (own, res[n_out:]) if comm else own


ROWS = 256


def _row_spec(tm, width=D_MODEL, col=0):
    return pl.BlockSpec((tm, width), lambda i: (i, col))


def _vec_spec(width=D_MODEL):
    return pl.BlockSpec((1, width), lambda i: (0, 0))


def _rms(x):
    return lax.rsqrt(jnp.mean(x * x, axis=-1, keepdims=True) + RMS_EPS)


def _colsum(x):
    return jnp.sum(x, axis=0, keepdims=True)


def _norm_bwd(d_xn, xn, r):
    return r * (d_xn - xn * jnp.mean(d_xn * xn, axis=-1, keepdims=True))


def _pre_norm(x, g, scale, shift, name):
    S = x.shape[0]
    tm = _tile(S, ROWS, 8)

    def kern(x_ref, g_ref, sc_ref, sh_ref, h_ref):
        xf = x_ref[...]
        y = xf * _rms(xf) * g_ref[...]
        h_ref[...] = (y * (1.0 + sc_ref[...]) + sh_ref[...]).astype(BF)

    return pl.pallas_call(
        kern, name=name, grid=(S // tm,),
        in_specs=[_row_spec(tm), _vec_spec(), _vec_spec(), _vec_spec()],
        out_specs=_row_spec(tm),
        out_shape=jax.ShapeDtypeStruct((S, D_MODEL), BF),
        compiler_params=_cparams("parallel"),
    )(x, g, scale, shift)


def _post_pre(x, y1, g2, gate_m, g3, scale_f, shift_f):
    S = x.shape[0]
    tm = _tile(S, ROWS, 8)

    def kern(x_ref, y_ref, g2_ref, gm_ref, g3_ref, sc_ref, sh_ref, x2_ref, h2_ref):
        y = y_ref[...]
        n2 = y * _rms(y) * g2_ref[...]
        x2 = x_ref[...] + gm_ref[...] * n2
        x2_ref[...] = x2
        n3 = x2 * _rms(x2) * g3_ref[...]
        h2_ref[...] = (n3 * (1.0 + sc_ref[...]) + sh_ref[...]).astype(BF)

    return pl.pallas_call(
        kern, name="post_mix_pre_ffn", grid=(S // tm,),
        in_specs=[_row_spec(tm), _row_spec(tm)] + [_vec_spec()] * 5,
        out_specs=[_row_spec(tm), _row_spec(tm)],
        out_shape=[jax.ShapeDtypeStruct((S, D_MODEL), F32), jax.ShapeDtypeStruct((S, D_MODEL), BF)],
        compiler_params=_cparams("parallel"),
    )(x, y1, g2, gate_m, g3, scale_f, shift_f)


def _stats_spec():
    return pl.BlockSpec((8, D_MODEL), lambda i: (0, 0))


def _final(x2, y2, g4, gate_f, target):
    S = x2.shape[0]
    tm = _tile(S, ROWS, 8)

    def kern(x2_ref, y_ref, g4_ref, gf_ref, t_ref, dout_ref, dy_ref, st_ref):
        @pl.when(pl.program_id(0) == 0)
        def _():
            st_ref[...] = jnp.zeros_like(st_ref)

        y = y_ref[...]
        r = _rms(y)
        yn = y * r
        n4 = yn * g4_ref[...]
        diff = x2_ref[...] + gf_ref[...] * n4 - t_ref[...]
        d_out = diff / D_MODEL
        dout_ref[...] = d_out
        dn = d_out * gf_ref[...]
        dy_ref[...] = _norm_bwd(dn * g4_ref[...], yn, r).astype(BF)
        st_ref[0:1, :] += _colsum(d_out * n4)
        st_ref[1:2, :] += _colsum(dn * yn)
        st_ref[2:3, :] += _colsum(diff * diff)

    return pl.pallas_call(
        kern, name="final_loss", grid=(S // tm,),
        in_specs=[_row_spec(tm), _row_spec(tm), _vec_spec(), _vec_spec(), _row_spec(tm)],
        out_specs=[_row_spec(tm), _row_spec(tm), _stats_spec()],
        out_shape=[jax.ShapeDtypeStruct((S, D_MODEL), F32), jax.ShapeDtypeStruct((S, D_MODEL), BF),
                   jax.ShapeDtypeStruct((8, D_MODEL), F32)],
        compiler_params=_cparams("arbitrary"),
    )(x2, y2, g4, gate_f, target)


def _mid_bwd(d_h2, x2, d_out, y1, g3, scale_f, g2, gate_m):
    S = x2.shape[0]
    tm = _tile(S, ROWS, 8)

    def kern(dh_ref, x2_ref, dout_ref, y_ref, g3_ref, sc_ref, g2_ref, gm_ref, dx2_ref, dy_ref, st_ref):
        @pl.when(pl.program_id(0) == 0)
        def _():
            st_ref[...] = jnp.zeros_like(st_ref)

        dh = dh_ref[...]
        x2 = x2_ref[...]
        r3 = _rms(x2)
        xn = x2 * r3
        one_sc = 1.0 + sc_ref[...]
        d_x2 = dout_ref[...] + _norm_bwd(dh * one_sc * g3_ref[...], xn, r3)
        dx2_ref[...] = d_x2
        y = y_ref[...]
        r2 = _rms(y)
        yn = y * r2
        dn = d_x2 * gm_ref[...]
        dy_ref[...] = _norm_bwd(dn * g2_ref[...], yn, r2).astype(BF)
        st_ref[0:1, :] += _colsum(dh)
        st_ref[1:2, :] += _colsum(dh * (xn * g3_ref[...]))
        st_ref[2:3, :] += _colsum(dh * one_sc * xn)
        st_ref[3:4, :] += _colsum(d_x2 * (yn * g2_ref[...]))
        st_ref[4:5, :] += _colsum(dn * yn)

    return pl.pallas_call(
        kern, name="mid_bwd", grid=(S // tm,),
        in_specs=[_row_spec(tm)] * 4 + [_vec_spec()] * 4,
        out_specs=[_row_spec(tm), _row_spec(tm), _stats_spec()],
        out_shape=[jax.ShapeDtypeStruct((S, D_MODEL), F32), jax.ShapeDtypeStruct((S, D_MODEL), BF),
                   jax.ShapeDtypeStruct((8, D_MODEL), F32)],
        compiler_params=_cparams("arbitrary"),
    )(d_h2, x2, d_out, y1, g3, scale_f, g2, gate_m)


def _pre_bwd(d_h1, x, d_x2, g1, scale_m):
    S = x.shape[0]
    tm = _tile(S, ROWS, 8)

    def kern(dh_ref, x_ref, dx2_ref, g_ref, sc_ref, gx_ref, st_ref):
        @pl.when(pl.program_id(0) == 0)
        def _():
            st_ref[...] = jnp.zeros_like(st_ref)

        dh = dh_ref[...]
        xf = x_ref[...]
        r = _rms(xf)
        xn = xf * r
        one_sc = 1.0 + sc_ref[...]
        gx_ref[...] = dx2_ref[...] + _norm_bwd(dh * one_sc * g_ref[...], xn, r)
        st_ref[0:1, :] += _colsum(dh)
        st_ref[1:2, :] += _colsum(dh * (xn * g_ref[...]))
        st_ref[2:3, :] += _colsum(dh * one_sc * xn)

    return pl.pallas_call(
        kern, name="pre_mix_bwd", grid=(S // tm,),
        in_specs=[_row_spec(tm)] * 3 + [_vec_spec()] * 2,
        out_specs=[_row_spec(tm), _stats_spec()],
        out_shape=[jax.ShapeDtypeStruct((S, D_MODEL), F32), jax.ShapeDtypeStruct((8, D_MODEL), F32)],
        compiler_params=_cparams("arbitrary"),
    )(d_h1, x, d_x2, g1, scale_m)


def _rope(xs, widths, cos_t, sin_t, name):
    S = xs[0].shape[0]
    tm = _tile(S, 512, 8)
    n = len(xs)

    def kern(*refs):
        cos = refs[n][...]
        sin = refs[n + 1][...]
        first = (lax.broadcasted_iota(jnp.int32, cos.shape, 1) % HEAD_DIM) < HEAD_DIM // 2
        for x_ref, o_ref, w in zip(refs[:n], refs[n + 2:], widths):
            for c0 in range(0, w, LANES):
                v = x_ref[:, c0:c0 + LANES]
                partner = jnp.where(first, pltpu.roll(v, LANES - HEAD_DIM // 2, 1),
                                    pltpu.roll(v, HEAD_DIM // 2, 1))
                o_ref[:, c0:c0 + LANES] = (v * cos + partner * sin).astype(BF)

    return pl.pallas_call(
        kern, name=name, grid=(S // tm,),
        in_specs=[_row_spec(tm, w) for w in widths] + [_row_spec(tm, LANES)] * 2,
        out_specs=[_row_spec(tm, w) for w in widths],
        out_shape=[jax.ShapeDtypeStruct((S, w), BF) for w in widths],
        compiler_params=_cparams("parallel"),
    )(*xs, cos_t, sin_t)


def _merge_fwd(pg, pa, pb):
    S = pa.shape[0]
    tm = _tile(S, ROWS, 8)

    def kern(ga_ref, gb_ref, pa_ref, pb_ref, o_ref):
        ga = jax.nn.sigmoid(ga_ref[...].astype(F32))
        gb = jax.nn.sigmoid(gb_ref[...].astype(F32))
        o_ref[...] = (ga * pa_ref[...] + gb * pb_ref[...]).astype(BF)

    return pl.pallas_call(
        kern, name="merge_fwd", grid=(S // tm,),
        in_specs=[_row_spec(tm, col=0), _row_spec(tm, col=1), _row_spec(tm), _row_spec(tm)],
        out_specs=_row_spec(tm),
        out_shape=jax.ShapeDtypeStruct((S, D_MODEL), BF),
        compiler_params=_cparams("parallel"),
    )(pg, pg, pa, pb)


def _merge_bwd(d_merged, pg, pa, pb):
    S = pa.shape[0]
    tm = _tile(S, ROWS, 8)

    def kern(dm_ref, ga_ref, gb_ref, pa_ref, pb_ref, dpa_ref, dpb_ref, dga_ref, dgb_ref):
        dm = dm_ref[...]
        ga = jax.nn.sigmoid(ga_ref[...].astype(F32))
        gb = jax.nn.sigmoid(gb_ref[...].astype(F32))
        dpa_ref[...] = (dm * ga).astype(BF)
        dpb_ref[...] = (dm * gb).astype(BF)
        dga_ref[...] = (dm * pa_ref[...] * ga * (1.0 - ga)).astype(BF)
        dgb_ref[...] = (dm * pb_ref[...] * gb * (1.0 - gb)).astype(BF)

    bf_out = jax.ShapeDtypeStruct((S, D_MODEL), BF)
    return pl.pallas_call(
        kern, name="merge_bwd", grid=(S // tm,),
        in_specs=[_row_spec(tm), _row_spec(tm, col=0), _row_spec(tm, col=1), _row_spec(tm), _row_spec(tm)],
        out_specs=[_row_spec(tm)] * 4,
        out_shape=[bf_out] * 4,
        compiler_params=_cparams("parallel"),
    )(d_merged, pg, pg, pa, pb)


def _swiglu_fwd(gu):
    S = gu.shape[0]
    tm = _tile(S, ROWS, 8)
    tc = _tile(D_FF, 1408)
    nc = D_FF // tc

    def kern(g_ref, u_ref, o_ref):
        g = g_ref[...].astype(F32)
        o_ref[...] = (g * jax.nn.sigmoid(g) * u_ref[...].astype(F32)).astype(BF)

    return pl.pallas_call(
        kern, name="swiglu_fwd", grid=(S // tm, nc),
        in_specs=[pl.BlockSpec((tm, tc), lambda i, j: (i, j)),
                  pl.BlockSpec((tm, tc), lambda i, j: (i, j + nc))],
        out_specs=pl.BlockSpec((tm, tc), lambda i, j: (i, j)),
        out_shape=jax.ShapeDtypeStruct((S, D_FF), BF),
        compiler_params=_cparams("parallel", "parallel"),
    )(gu, gu)


def _swiglu_bwd(d_act, gu):
    S = gu.shape[0]
    tm = _tile(S, 128, 8)

    def kern(da_ref, g_ref, u_ref, o_ref):
        g = g_ref[...].astype(F32)
        u = u_ref[...].astype(F32)
        da = da_ref[...]
        sg = jax.nn.sigmoid(g)
        o_ref[:, :D_FF] = (da * u * (sg * (1.0 + g * (1.0 - sg)))).astype(BF)
        o_ref[:, D_FF:] = (da * (g * sg)).astype(BF)

    return pl.pallas_call(
        kern, name="swiglu_bwd", grid=(S // tm,),
        in_specs=[_row_spec(tm, D_FF), _row_spec(tm, D_FF, 0), _row_spec(tm, D_FF, 1)],
        out_specs=_row_spec(tm, 2 * D_FF),
        out_shape=jax.ShapeDtypeStruct((S, 2 * D_FF), BF),
        compiler_params=_cparams("parallel"),
    )(d_act, gu, gu)


def _split3(x):
    hi = x.astype(BF)
    r1 = x - hi.astype(F32)
    mid = r1.astype(BF)
    lo = (r1 - mid.astype(F32)).astype(BF)
    return hi, mid, lo


def _tri_dot(tri, x):
    return sum(jnp.dot(tri, part, preferred_element_type=F32) for part in _split3(x))


def _log_sigmoid(z):
    return jnp.minimum(z, 0.0) - jnp.log(1.0 + jnp.exp(-jnp.abs(z)))


def _fox_gate_fwd(pa, b_f_pad):
    S = pa.shape[0]
    T = _tile(S, 512, 8)
    f_col = OFF_F // LANES

    def kern(z_ref, b_ref, cum_ref, carry_ref):
        @pl.when(pl.program_id(0) == 0)
        def _():
            carry_ref[...] = jnp.zeros_like(carry_ref)

        log_f = _log_sigmoid(z_ref[...] + b_ref[...])
        row = lax.broadcasted_iota(jnp.int32, (T, T), 0)
        col = lax.broadcasted_iota(jnp.int32, (T, T), 1)
        tri = (col <= row).astype(BF)
        cum = _tri_dot(tri, log_f) + carry_ref[...]
        cum_ref[...] = cum
        carry_ref[...] = cum[T - 1:T, :]

    return pl.pallas_call(
        kern, name="fox_gate_fwd", grid=(S // T,),
        in_specs=[_row_spec(T, LANES, f_col), _vec_spec(LANES)],
        out_specs=_row_spec(T, LANES),
        out_shape=jax.ShapeDtypeStruct((S, LANES), F32),
        scratch_shapes=[pltpu.VMEM((1, LANES), F32)],
        compiler_params=_cparams("arbitrary"),
    )(pa, b_f_pad)


def _fox_gate_bwd(rowsum_ds, colsum_ds, pa, b_f_pad):
    S = pa.shape[0]
    T = _tile(S, 512, 8)
    nb = S // T
    f_col = OFF_F // LANES

    def kern(dr_ref, dc_ref, z_ref, b_ref, df_ref, dbf_ref, carry_ref):
        @pl.when(pl.program_id(0) == 0)
        def _():
            carry_ref[...] = jnp.zeros_like(carry_ref)
            dbf_ref[...] = jnp.zeros_like(dbf_ref)

        row = lax.broadcasted_iota(jnp.int32, (T, T), 0)
        col = lax.broadcasted_iota(jnp.int32, (T, T), 1)
        tri = (col >= row).astype(BF)
        rev = _tri_dot(tri, dr_ref[...] - dc_ref[...]) + carry_ref[...]
        carry_ref[...] = rev[0:1, :]
        z = z_ref[...] + b_ref[...]
        lane = lax.broadcasted_iota(jnp.int32, (T, LANES), 1)
        d_z = jnp.where(lane < B_HEADS, rev * jax.nn.sigmoid(-z), 0.0)
        df_ref[...] = d_z.astype(BF)
        dbf_ref[0:1, :] += _colsum(d_z)

    return pl.pallas_call(
        kern, name="fox_gate_bwd", grid=(nb,),
        in_specs=[pl.BlockSpec((T, LANES), lambda i: (nb - 1 - i, 0)),
                  pl.BlockSpec((T, LANES), lambda i: (nb - 1 - i, 0)),
                  pl.BlockSpec((T, LANES), lambda i: (nb - 1 - i, f_col)),
                  _vec_spec(LANES)],
        out_specs=[pl.BlockSpec((T, LANES), lambda i: (nb - 1 - i, 0)),
                   pl.BlockSpec((8, LANES), lambda i: (0, 0))],
        out_shape=[jax.ShapeDtypeStruct((S, LANES), BF), jax.ShapeDtypeStruct((8, LANES), F32)],
        scratch_shapes=[pltpu.VMEM((1, LANES), F32)],
        compiler_params=_cparams("arbitrary"),
    )(rowsum_ds, colsum_ds, pa, b_f_pad)


NEG_INF = float("-inf")
QK_SCALE = 1.0 / math.sqrt(HEAD_DIM)


def _half_mask(shape, half):
    lane = lax.broadcasted_iota(jnp.int32, shape, 1)
    return (lane < HEAD_DIM) if half == 0 else (lane >= HEAD_DIM)


def _valid(i, j, T, rowcol, window):
    rel = (i - j) * T + rowcol
    ok = rel >= 0
    if window is not None:
        ok = ok & (rel < window)
    return ok


def _attn_fwd(q_arr, q_col, k_arr, k_col, v_arr, v_col, n_pairs, kv_shared, T, window,
              cq_arr, ck_arr, sinks, name, comm=None):
    S = q_arr.shape[0]
    nq = S // T
    use_bias = cq_arr is not None
    use_sink = sinks is not None
    back = 0 if window is None else -(-window // T)
    grid = (n_pairs, nq)
    n_in = 3 + 2 * use_bias + use_sink

    def kern(*refs):
        refs, comm_refs = _own_refs(refs, comm, n_in, 2, 0)
        _comm_edge(comm, comm_refs, grid, first=True)
        q_ref, k_ref, v_ref = refs[:3]
        pos = 3
        if use_bias:
            cq_ref, ck_ref = refs[pos:pos + 2]
            pos += 2
        if use_sink:
            sink_ref = refs[pos]
            pos += 1
        o_ref, lse_ref = refs[pos:pos + 2]
        p_id = pl.program_id(0)
        i = pl.program_id(1)
        q = q_ref[...]
        rowcol = lax.broadcasted_iota(jnp.int32, (T, T), 0) - lax.broadcasted_iota(jnp.int32, (T, T), 1)
        lo = jnp.maximum(i - back, 0) if window is not None else 0
        outs, lses = [], []
        for half in (0, 1):
            hm = _half_mask((T, LANES), half)
            qh = (jnp.where(hm, q, 0).astype(F32) * QK_SCALE).astype(BF)
            if use_bias:
                cq = cq_ref[:, half * HEAD_DIM:half * HEAD_DIM + 1]
            if use_sink:
                m0 = jnp.full((T, 1), sink_ref[2 * p_id + half], F32)
                l0 = jnp.ones((T, 1), F32)
            else:
                m0 = jnp.full((T, 1), NEG_INF, F32)
                l0 = jnp.zeros((T, 1), F32)

            def step(j, carry, masked):
                m, l, acc = carry
                rows = pl.ds(pl.multiple_of(j * T, T), T)
                kj = k_ref[rows, :].astype(BF)
                vj = v_ref[rows, :].astype(BF)
                s = lax.dot_general(qh, kj, (((1,), (1,)), ((), ())), preferred_element_type=F32)
                if use_bias:
                    s = s + cq - ck_ref[0, half:half + 1, rows]
                if masked:
                    s = jnp.where(_valid(i, j, T, rowcol, window), s, NEG_INF)
                m_new = jnp.maximum(m, jnp.max(s, axis=1, keepdims=True))
                alpha = jnp.exp(m - m_new)
                p = jnp.exp(s - m_new)
                l_new = alpha * l + jnp.sum(p, axis=1, keepdims=True)
                acc_new = alpha * acc + jnp.dot(p.astype(BF), vj, preferred_element_type=F32)
                return m_new, l_new, acc_new

            init = (m0, l0, jnp.zeros((T, LANES), F32))
            if window is None:
                init = lax.fori_loop(0, i, functools.partial(step, masked=False), init)
                m, l, acc = step(i, init, True)
            else:
                m, l, acc = lax.fori_loop(lo, i + 1, functools.partial(step, masked=True), init)
            outs.append(acc / l)
            lses.append(m + jnp.log(l))
        hm0 = _half_mask((T, LANES), 0)
        o_ref[...] = jnp.where(hm0, outs[0], outs[1])
        lse_ref[...] = jnp.where(hm0, lses[0], lses[1])
        _comm_edge(comm, comm_refs, grid, first=False)

    kv_idx = (lambda c0: (lambda p, i: (0, c0))) if kv_shared else (lambda c0: (lambda p, i: (0, c0 + p)))
    in_specs = [pl.BlockSpec((T, LANES), lambda p, i: (i, q_col + p)),
                pl.BlockSpec((S, LANES), kv_idx(k_col)),
                pl.BlockSpec((S, LANES), kv_idx(v_col))]
    args = [q_arr, k_arr, v_arr]
    if use_bias:
        in_specs += [pl.BlockSpec((T, LANES), lambda p, i: (i, p)),
                     pl.BlockSpec((1, 2, S), lambda p, i: (p, 0, 0))]
        args += [cq_arr, ck_arr]
    if use_sink:
        in_specs.append(pl.BlockSpec(memory_space=pltpu.SMEM))
        args.append(sinks)
    out_spec = pl.BlockSpec((T, LANES), lambda p, i: (i, p))
    res = pl.pallas_call(
        kern, name=name, grid=grid,
        in_specs=in_specs + _comm_specs(comm, "in"),
        out_specs=[out_spec, out_spec] + _comm_specs(comm, "out"),
        out_shape=[jax.ShapeDtypeStruct((S, n_pairs * LANES), F32)] * 2 + (comm.out_shapes if comm else []),
        scratch_shapes=comm.sem_shapes if comm else [],
        compiler_params=_cparams("arbitrary", "arbitrary"),
    )(*args, *(comm.ins if comm else []))
    return (res[0], res[1], res[2:]) if comm else (res[0], res[1])


def _attn_bwd(q_arr, q_col, k_arr, k_col, v_arr, v_col, o_arr, do_arr, lse_arr, n_pairs, kv_shared, T,
              window, cq_arr, ck_arr, sinks, name, comm=None):
    S = q_arr.shape[0]
    nq = S // T
    use_bias = cq_arr is not None
    use_sink = sinks is not None
    back = 0 if window is None else -(-window // T)
    kv_w = LANES if kv_shared else n_pairs * LANES
    grid = (n_pairs,)
    n_in = 6 + 2 * use_bias + use_sink
    n_out = 3 + 2 * use_bias + use_sink

    def kern(*refs):
        refs, comm_refs = _own_refs(refs, comm, n_in, n_out, 0)
        _comm_edge(comm, comm_refs, grid, first=True)
        q_ref, k_ref, v_ref, o_ref, do_ref, lse_ref = refs[:6]
        pos = 6
        if use_bias:
            cq_ref, ck_ref = refs[pos:pos + 2]
            pos += 2
        if use_sink:
            sink_ref = refs[pos]
            pos += 1
        dq_ref, dk_ref, dv_ref = refs[pos:pos + 3]
        pos += 3
        if use_bias:
            dck_ref, dcq_ref = refs[pos:pos + 2]
            pos += 2
        if use_sink:
            dsink_ref = refs[pos]
        p_id = pl.program_id(0)
        rowcol = lax.broadcasted_iota(jnp.int32, (T, T), 0) - lax.broadcasted_iota(jnp.int32, (T, T), 1)

        def zero_kv():
            dk_ref[...] = jnp.zeros_like(dk_ref)
            dv_ref[...] = jnp.zeros_like(dv_ref)

        if kv_shared:
            pl.when(p_id == 0)(zero_kv)
        else:
            zero_kv()
        if use_bias:
            dck_ref[...] = jnp.zeros_like(dck_ref)
        if use_sink:
            dsink_ref[...] = jnp.zeros_like(dsink_ref)

        for half in (0, 1):
            hm = _half_mask((T, LANES), half)
            lane0 = half * HEAD_DIM

            def outer(i, carry):
                qrows = pl.ds(pl.multiple_of(i * T, T), T)
                qh = (jnp.where(hm, q_ref[qrows, :], 0).astype(F32) * QK_SCALE).astype(BF)
                do_f = jnp.where(hm, do_ref[qrows, :], 0.0)
                doh = do_f.astype(BF)
                delta = jnp.sum(do_f * o_ref[qrows, :], axis=1, keepdims=True)
                lse = lse_ref[qrows, lane0:lane0 + 1]
                if use_bias:
                    cq = cq_ref[qrows, lane0:lane0 + 1]
                lo = jnp.maximum(i - back, 0) if window is not None else 0

                def inner(j, carry_in, masked):
                    dq, rs = carry_in
                    krows = pl.ds(pl.multiple_of(j * T, T), T)
                    kj = k_ref[krows, :].astype(BF)
                    vj = v_ref[krows, :].astype(BF)
                    s = lax.dot_general(qh, kj, (((1,), (1,)), ((), ())), preferred_element_type=F32)
                    if use_bias:
                        s = s + cq - ck_ref[0, half:half + 1, krows]
                    if masked:
                        s = jnp.where(_valid(i, j, T, rowcol, window), s, NEG_INF)
                    p = jnp.exp(s - lse)
                    dp = lax.dot_general(doh, vj, (((1,), (1,)), ((), ())), preferred_element_type=F32)
                    ds = p * (dp - delta)
                    ds_b = ds.astype(BF)
                    dv_ref[krows, :] += lax.dot_general(p.astype(BF), doh, (((0,), (0,)), ((), ())),
                                                        preferred_element_type=F32)
                    dk_ref[krows, :] += lax.dot_general(ds_b, qh, (((0,), (0,)), ((), ())),
                                                        preferred_element_type=F32)
                    if use_bias:
                        dck_ref[0, half:half + 1, krows] += jnp.sum(ds, axis=0, keepdims=True)
                        rs = rs + jnp.sum(ds, axis=1, keepdims=True)
                    kh = jnp.where(hm, kj, 0)
                    return dq + jnp.dot(ds_b, kh, preferred_element_type=F32), rs

                init = (jnp.zeros((T, LANES), F32), jnp.zeros((T, 1), F32))
                if window is None:
                    init = lax.fori_loop(0, i, functools.partial(inner, masked=False), init)
                    dq, rs = inner(i, init, True)
                else:
                    dq, rs = lax.fori_loop(lo, i + 1, functools.partial(inner, masked=True), init)
                dq = dq * QK_SCALE
                if half == 0:
                    dq_ref[qrows, :] = dq
                else:
                    dq_ref[qrows, :] += dq
                if use_bias:
                    rs_b = jnp.broadcast_to(rs, (T, LANES))
                    dcq_ref[qrows, :] = rs_b if half == 0 else jnp.where(hm, rs_b, dcq_ref[qrows, :])
                if use_sink:
                    p_sink = jnp.exp(sink_ref[2 * p_id + half] - lse)
                    dsink_ref[0, half:half + 1, :] += jnp.broadcast_to(
                        -jnp.sum(p_sink * delta, axis=0, keepdims=True), (1, LANES))
                return carry

            lax.fori_loop(0, nq, outer, 0)
        _comm_edge(comm, comm_refs, grid, first=False)

    kv_idx = (lambda c0: (lambda p: (0, c0))) if kv_shared else (lambda c0: (lambda p: (0, c0 + p)))
    pair = lambda c0: pl.BlockSpec((S, LANES), lambda p: (0, c0 + p))
    in_specs = [pair(q_col), pl.BlockSpec((S, LANES), kv_idx(k_col)), pl.BlockSpec((S, LANES), kv_idx(v_col)),
                pair(0), pair(0), pair(0)]
    args = [q_arr, k_arr, v_arr, o_arr, do_arr, lse_arr]
    if use_bias:
        in_specs += [pair(0), pl.BlockSpec((1, 2, S), lambda p: (p, 0, 0))]
        args += [cq_arr, ck_arr]
    if use_sink:
        in_specs.append(pl.BlockSpec(memory_space=pltpu.SMEM))
        args.append(sinks)
    out_specs = [pair(0), pl.BlockSpec((S, LANES), kv_idx(0)), pl.BlockSpec((S, LANES), kv_idx(0))]
    out_shape = [jax.ShapeDtypeStruct((S, n_pairs * LANES), F32),
                 jax.ShapeDtypeStruct((S, kv_w), F32), jax.ShapeDtypeStruct((S, kv_w), F32)]
    if use_bias:
        out_specs += [pl.BlockSpec((1, 2, S), lambda p: (p, 0, 0)), pair(0)]
        out_shape += [jax.ShapeDtypeStruct((n_pairs, 2, S), F32), jax.ShapeDtypeStruct((S, n_pairs * LANES), F32)]
    if use_sink:
        out_specs.append(pl.BlockSpec((1, 8, LANES), lambda p: (p, 0, 0)))
        out_shape.append(jax.ShapeDtypeStruct((n_pairs, 8, LANES), F32))
    res = pl.pallas_call(
        kern, name=name, grid=grid,
        in_specs=in_specs + _comm_specs(comm, "in"),
        out_specs=out_specs + _comm_specs(comm, "out"),
        out_shape=out_shape + (comm.out_shapes if comm else []),
        scratch_shapes=comm.sem_shapes if comm else [],
        compiler_params=_cparams("arbitrary"),
    )(*args, *(comm.ins if comm else []))
    return (*res[:n_out], res[n_out:]) if comm else res


def _bias_lanes(shape, half, q_side_terms, k_side_terms):
    lane = lax.broadcasted_iota(jnp.int32, shape, 1)
    base = HEAD_DIM * (1 - half)
    n_q = len(q_side_terms) if q_side_terms is not None else 3
    n_k = len(k_side_terms) if k_side_terms is not None else 3
    out = jnp.zeros(shape, F32)
    for t in range(n_q):
        out = jnp.where(lane == base + t, q_side_terms[t].astype(F32) if q_side_terms is not None else 1.0, out)
    for t in range(n_k):
        out = jnp.where(lane == base + n_q + t,
                        k_side_terms[t].astype(F32) if k_side_terms is not None else 1.0, out)
    return out


def _head_column(block, head):
    lane = lax.broadcasted_iota(jnp.int32, block.shape, 1)
    return jnp.sum(jnp.where(lane == head, block, 0.0), axis=1, keepdims=True)


def _fox_prep_fwd(p_b, cum, T):
    S = p_b.shape[0]

    def kern(q_ref, k_ref, c_ref, qa_ref, ka_ref):
        p_id = pl.program_id(0)
        q, k, cum_blk = q_ref[...], k_ref[...], c_ref[...]
        for half in (0, 1):
            hm = _half_mask((T, LANES), half)
            c3 = _split3(_head_column(cum_blk, 2 * p_id + half))
            qa_ref[half] = jnp.where(hm, q.astype(F32) * QK_SCALE, _bias_lanes((T, LANES), half, c3, None)).astype(BF)
            ka_ref[half] = jnp.where(hm, k.astype(F32),
                                     _bias_lanes((T, LANES), half, None, [-t.astype(F32) for t in c3])).astype(BF)

    out_spec = pl.BlockSpec((None, 2, T, LANES), lambda p, i: (p, 0, i, 0))
    shape = jax.ShapeDtypeStruct((B_HEADS // 2, 2, S, LANES), BF)
    return pl.pallas_call(
        kern, name="fox_prep_fwd", grid=(B_HEADS // 2, S // T),
        in_specs=[pl.BlockSpec((T, LANES), lambda p, i: (i, OFF_QB // LANES + p)),
                  pl.BlockSpec((T, LANES), lambda p, i: (i, OFF_KB // LANES + p)),
                  pl.BlockSpec((T, LANES), lambda p, i: (i, 0))],
        out_specs=[out_spec, out_spec], out_shape=[shape, shape],
        compiler_params=_cparams("parallel", "parallel"),
    )(p_b, p_b, cum)


def _fox_fwd(q_aug, k_aug, p_b, T, comm=None):
    S = p_b.shape[0]
    nq = S // T
    n_pairs = B_HEADS // 2
    grid = (n_pairs, nq)

    def kern(*refs):
        (q_ref, k_ref, v_ref, o_ref, lse_ref), comm_refs = _own_refs(refs, comm, 3, 2, 0)
        _comm_edge(comm, comm_refs, grid, first=True)
        i = pl.program_id(1)
        rowcol = lax.broadcasted_iota(jnp.int32, (T, T), 0) - lax.broadcasted_iota(jnp.int32, (T, T), 1)
        qs = (q_ref[0], q_ref[1])

        def step(j, carry, masked):
            rows = pl.ds(pl.multiple_of(j * T, T), T)
            vj = v_ref[rows, :]
            new = []
            for half in (0, 1):
                m, l, acc = carry[half]
                s = lax.dot_general(qs[half], k_ref[half, rows, :], (((1,), (1,)), ((), ())),
                                    preferred_element_type=F32)
                if masked:
                    s = jnp.where(rowcol >= 0, s, NEG_INF)
                m_new = jnp.maximum(m, jnp.max(s, axis=1, keepdims=True))
                alpha = jnp.exp(m - m_new)
                p = jnp.exp(s - m_new)
                l_new = alpha * l + jnp.sum(p, axis=1, keepdims=True)
                acc_new = alpha * acc + jnp.dot(p.astype(BF), vj, preferred_element_type=F32)
                new.append((m_new, l_new, acc_new))
            return tuple(new)

        one = (jnp.full((T, 1), NEG_INF, F32), jnp.zeros((T, 1), F32), jnp.zeros((T, LANES), F32))
        carry = lax.fori_loop(0, i, functools.partial(step, masked=False), (one, one))
        (m0, l0, acc0), (m1, l1, acc1) = step(i, carry, True)
        hm0 = _half_mask((T, LANES), 0)
        o_ref[...] = jnp.where(hm0, acc0 / l0, acc1 / l1)
        lse_ref[...] = jnp.where(hm0, m0 + jnp.log(l0), m1 + jnp.log(l1))
        _comm_edge(comm, comm_refs, grid, first=False)

    out_spec = pl.BlockSpec((T, LANES), lambda p, i: (i, p))
    res = pl.pallas_call(
        kern, name="fox_fwd", grid=grid,
        in_specs=[pl.BlockSpec((None, 2, T, LANES), lambda p, i: (p, 0, i, 0)),
                  pl.BlockSpec((None, 2, S, LANES), lambda p, i: (p, 0, 0, 0)),
                  pl.BlockSpec((S, LANES), lambda p, i: (0, OFF_VB // LANES + p))] + _comm_specs(comm, "in"),
        out_specs=[out_spec, out_spec] + _comm_specs(comm, "out"),
        out_shape=[jax.ShapeDtypeStruct((S, n_pairs * LANES), F32)] * 2 + (comm.out_shapes if comm else []),
        scratch_shapes=comm.sem_shapes if comm else [],
        compiler_params=_cparams("arbitrary", "arbitrary"),
    )(q_aug, k_aug, p_b, *(comm.ins if comm else []))
    return res[0], res[1], res[2:]


def _fox_prep_bwd(q_aug, p_b, o, do, lse, T):
    S = p_b.shape[0]

    def kern(qa_ref, v_ref, o_ref, do_ref, lse_ref, qb_ref, dob_ref, vb_ref):
        v, o_blk, do_blk, lse_blk = v_ref[...], o_ref[...], do_ref[...], lse_ref[...]
        lane = lax.broadcasted_iota(jnp.int32, (T, LANES), 1)
        for half in (0, 1):
            hm = _half_mask((T, LANES), half)
            base = HEAD_DIM * (1 - half)
            qa = qa_ref[half].astype(F32)
            cq = jnp.sum(jnp.where((lane >= base) & (lane < base + 3), qa, 0.0), axis=1, keepdims=True)
            b3 = _split3(cq - lse_blk[:, HEAD_DIM * half:HEAD_DIM * half + 1])
            qb_ref[half] = jnp.where(hm, qa, _bias_lanes((T, LANES), half, b3, None)).astype(BF)
            do_f = jnp.where(hm, do_blk, 0.0)
            d3 = _split3(-jnp.sum(do_f * o_blk, axis=1, keepdims=True))
            dob_ref[half] = jnp.where(hm, do_f, _bias_lanes((T, LANES), half, d3, [])).astype(BF)
            vb_ref[half] = jnp.where(hm, v.astype(F32), _bias_lanes((T, LANES), half, None, [])).astype(BF)

    aug = pl.BlockSpec((None, 2, T, LANES), lambda p, i: (p, 0, i, 0))
    tile = pl.BlockSpec((T, LANES), lambda p, i: (i, p))
    shape = jax.ShapeDtypeStruct((B_HEADS // 2, 2, S, LANES), BF)
    return pl.pallas_call(
        kern, name="fox_prep_bwd", grid=(B_HEADS // 2, S // T),
        in_specs=[aug, pl.BlockSpec((T, LANES), lambda p, i: (i, OFF_VB // LANES + p)), tile, tile, tile],
        out_specs=[aug, aug, aug], out_shape=[shape, shape, shape],
        compiler_params=_cparams("parallel", "parallel"),
    )(q_aug, p_b, o, do, lse)


def _fox_bwd(qb_aug, k_aug, dob_aug, vb_aug, T, comm=None):
    n_pairs, _, S, _ = qb_aug.shape
    nq = S // T
    grid = (n_pairs,)

    def kern(*refs):
        own, comm_refs = _own_refs(refs, comm, 4, 5, 0)
        q_ref, k_ref, do_ref, v_ref, dq_ref, dk_ref, dv_ref, dck_ref, dcq_ref = own
        _comm_edge(comm, comm_refs, grid, first=True)
        p_id = pl.program_id(0)
        rowcol = lax.broadcasted_iota(jnp.int32, (T, T), 0) - lax.broadcasted_iota(jnp.int32, (T, T), 1)
        lane = lax.broadcasted_iota(jnp.int32, (T, LANES), 1)
        dk_ref[...] = jnp.zeros_like(dk_ref)
        dv_ref[...] = jnp.zeros_like(dv_ref)
        dck_ref[...] = jnp.zeros_like(dck_ref)

        @pl.when(p_id == 0)
        def _():
            dcq_ref[...] = jnp.zeros_like(dcq_ref)

        hms = (_half_mask((T, LANES), 0), _half_mask((T, LANES), 1))

        def outer(i, carry):
            qrows = pl.ds(pl.multiple_of(i * T, T), T)
            qa = (q_ref[0, qrows, :], q_ref[1, qrows, :])
            doa = (do_ref[0, qrows, :], do_ref[1, qrows, :])
            q_own = [jnp.where(hms[h], qa[h], 0) for h in (0, 1)]
            do_own = [jnp.where(hms[h], doa[h], 0) for h in (0, 1)]

            def inner(j, carry_in, masked):
                krows = pl.ds(pl.multiple_of(j * T, T), T)
                dv_add, dk_add, new = 0.0, 0.0, []
                for half in (0, 1):
                    dq, rs = carry_in[half]
                    ka = k_ref[half, krows, :]
                    s = lax.dot_general(qa[half], ka, (((1,), (1,)), ((), ())), preferred_element_type=F32)
                    if masked:
                        s = jnp.where(rowcol >= 0, s, NEG_INF)
                    p = jnp.exp(s)
                    ds = p * lax.dot_general(doa[half], v_ref[half, krows, :], (((1,), (1,)), ((), ())),
                                             preferred_element_type=F32)
                    ds_b = ds.astype(BF)
                    dv_add = dv_add + lax.dot_general(p.astype(BF), do_own[half], (((0,), (0,)), ((), ())),
                                                      preferred_element_type=F32)
                    dk_add = dk_add + lax.dot_general(ds_b, q_own[half], (((0,), (0,)), ((), ())),
                                                      preferred_element_type=F32)
                    dck_ref[half:half + 1, krows] += jnp.sum(ds, axis=0, keepdims=True)
                    new.append((dq + jnp.dot(ds_b, jnp.where(hms[half], ka, 0), preferred_element_type=F32),
                                rs + jnp.sum(ds, axis=1, keepdims=True)))
                dv_ref[krows, :] += dv_add
                dk_ref[krows, :] += dk_add
                return tuple(new)

            one = (jnp.zeros((T, LANES), F32), jnp.zeros((T, 1), F32))
            carry_in = lax.fori_loop(0, i, functools.partial(inner, masked=False), (one, one))
            (dq0, rs0), (dq1, rs1) = inner(i, carry_in, True)
            dq_ref[qrows, :] = (dq0 + dq1) * QK_SCALE
            dcq_ref[qrows, :] = jnp.where(lane == 2 * p_id, rs0, jnp.where(lane == 2 * p_id + 1, rs1,
                                                                             dcq_ref[qrows, :]))
            return carry

        lax.fori_loop(0, nq, outer, 0)
        _comm_edge(comm, comm_refs, grid, first=False)

    aug = pl.BlockSpec((None, 2, S, LANES), lambda p: (p, 0, 0, 0))
    pair = pl.BlockSpec((S, LANES), lambda p: (0, p))
    wide = jax.ShapeDtypeStruct((S, n_pairs * LANES), F32)
    res = pl.pallas_call(
        kern, name="fox_bwd", grid=grid,
        in_specs=[aug, aug, aug, aug] + _comm_specs(comm, "in"),
        out_specs=[pair, pair, pair, pl.BlockSpec((None, 2, S), lambda p: (p, 0, 0)),
                   pl.BlockSpec((S, LANES), lambda p: (0, 0))] + _comm_specs(comm, "out"),
        out_shape=[wide, wide, wide, jax.ShapeDtypeStruct((n_pairs, 2, S), F32),
                   jax.ShapeDtypeStruct((S, LANES), F32)] + (comm.out_shapes if comm else []),
        scratch_shapes=comm.sem_shapes if comm else [],
        compiler_params=_cparams("arbitrary"),
    )(qb_aug, k_aug, dob_aug, vb_aug, *(comm.ins if comm else []))
    return (*res[:5], res[5:])


SWA_TQ = 256


def _swa_window(i, tq):
    start = pl.multiple_of(jnp.maximum(i * tq - WINDOW, 0), LANES)
    return start, i * tq - start


def _swa_valid(offset, tq):
    rel = offset + lax.broadcasted_iota(jnp.int32, (tq, tq + WINDOW), 0) \
        - lax.broadcasted_iota(jnp.int32, (tq, tq + WINDOW), 1)
    return (rel >= 0) & (rel < WINDOW)


def _swa_fwd(qk, v_arr, v_col, sinks):
    S = qk.shape[0]
    tq = min(SWA_TQ, S - WINDOW)
    win = tq + WINDOW

    def kern(q_ref, k_ref, v_ref, sink_ref, o_ref, lse_ref):
        p_id, i = pl.program_id(0), pl.program_id(1)
        start, offset = _swa_window(i, tq)
        kw = k_ref[pl.ds(start, win), :]
        vw = v_ref[pl.ds(start, win), :].astype(BF)
        valid = _swa_valid(offset, tq)
        q = q_ref[...]
        outs, lses = [], []
        for half in (0, 1):
            hm = _half_mask((tq, LANES), half)
            qh = (jnp.where(hm, q, 0).astype(F32) * QK_SCALE).astype(BF)
            s = lax.dot_general(qh, kw, (((1,), (1,)), ((), ())), preferred_element_type=F32)
            s = jnp.where(valid, s, NEG_INF)
            sink = sink_ref[2 * p_id + half]
            m = jnp.maximum(jnp.max(s, axis=1, keepdims=True), sink)
            p = jnp.exp(s - m)
            denom = jnp.sum(p, axis=1, keepdims=True) + jnp.exp(sink - m)
            outs.append(jnp.dot(p.astype(BF), vw, preferred_element_type=F32) / denom)
            lses.append(m + jnp.log(denom))
        hm0 = _half_mask((tq, LANES), 0)
        o_ref[...] = jnp.where(hm0, outs[0], outs[1])
        lse_ref[...] = jnp.where(hm0, lses[0], lses[1])

    tile = pl.BlockSpec((tq, LANES), lambda p, i: (i, p))
    return pl.pallas_call(
        kern, name="swa_fwd", grid=(A_Q_HEADS // 2, S // tq),
        in_specs=[tile, pl.BlockSpec((S, LANES), lambda p, i: (0, A_Q_HEADS // 2)),
                  pl.BlockSpec((S, LANES), lambda p, i: (0, v_col)),
                  pl.BlockSpec(memory_space=pltpu.SMEM)],
        out_specs=[tile, tile],
        out_shape=[jax.ShapeDtypeStruct((S, A_Q_HEADS * HEAD_DIM), F32)] * 2,
        compiler_params=_cparams("parallel", "arbitrary"),
    )(qk, qk, v_arr, sinks)


def _swa_bwd(qk, v_arr, v_col, o_arr, do_arr, lse_arr, sinks, comm=None):
    S = qk.shape[0]
    tq = min(SWA_TQ, S - WINDOW)
    win = tq + WINDOW
    n_pairs = A_Q_HEADS // 2
    grid = (n_pairs, S // tq)

    def kern(*refs):
        own, comm_refs = _own_refs(refs, comm, 7, 4, 0)
        q_ref, k_ref, v_ref, o_ref, do_ref, lse_ref, sink_ref, dq_ref, dk_ref, dv_ref, dsink_ref = own
        _comm_edge(comm, comm_refs, grid, first=True)
        p_id, i = pl.program_id(0), pl.program_id(1)

        @pl.when((p_id == 0) & (i == 0))
        def _():
            dk_ref[...] = jnp.zeros_like(dk_ref)
            dv_ref[...] = jnp.zeros_like(dv_ref)

        @pl.when(i == 0)
        def _():
            dsink_ref[...] = jnp.zeros_like(dsink_ref)

        start, offset = _swa_window(i, tq)
        wrows = pl.ds(start, win)
        kw = k_ref[wrows, :]
        vw = v_ref[wrows, :].astype(BF)
        valid = _swa_valid(offset, tq)
        q, do, o, lse2 = q_ref[...], do_ref[...], o_ref[...], lse_ref[...]
        dq = jnp.zeros((tq, LANES), F32)
        dk = jnp.zeros((win, LANES), F32)
        dv = jnp.zeros((win, LANES), F32)
        for half in (0, 1):
            hm = _half_mask((tq, LANES), half)
            lane0 = half * HEAD_DIM
            qh = (jnp.where(hm, q, 0).astype(F32) * QK_SCALE).astype(BF)
            do_f = jnp.where(hm, do, 0.0)
            doh = do_f.astype(BF)
            delta = jnp.sum(do_f * o, axis=1, keepdims=True)
            lse = lse2[:, lane0:lane0 + 1]
            s = lax.dot_general(qh, kw, (((1,), (1,)), ((), ())), preferred_element_type=F32)
            p = jnp.exp(jnp.where(valid, s, NEG_INF) - lse)
            dp = lax.dot_general(doh, vw, (((1,), (1,)), ((), ())), preferred_element_type=F32)
            ds_b = (p * (dp - delta)).astype(BF)
            dv = dv + lax.dot_general(p.astype(BF), doh, (((0,), (0,)), ((), ())), preferred_element_type=F32)
            dk = dk + lax.dot_general(ds_b, qh, (((0,), (0,)), ((), ())), preferred_element_type=F32)
            kh = jnp.where(_half_mask((win, LANES), half), kw, 0)
            dq = dq + jnp.dot(ds_b, kh, preferred_element_type=F32)
            p_sink = jnp.exp(sink_ref[2 * p_id + half] - lse)
            dsink_ref[0, half:half + 1, :] += jnp.broadcast_to(
                -jnp.sum(p_sink * delta, axis=0, keepdims=True), (1, LANES))
        dq_ref[...] = dq * QK_SCALE
        dk_ref[wrows, :] += dk
        dv_ref[wrows, :] += dv
        _comm_edge(comm, comm_refs, grid, first=False)

    tile = pl.BlockSpec((tq, LANES), lambda p, i: (i, p))
    whole = lambda col: pl.BlockSpec((S, LANES), lambda p, i: (0, col))
    res = pl.pallas_call(
        kern, name="swa_bwd", grid=grid,
        in_specs=[tile, whole(n_pairs), whole(v_col), tile, tile, tile,
                  pl.BlockSpec(memory_space=pltpu.SMEM)] + _comm_specs(comm, "in"),
        out_specs=[tile, whole(0), whole(0),
                   pl.BlockSpec((1, 8, LANES), lambda p, i: (p, 0, 0))] + _comm_specs(comm, "out"),
        out_shape=[jax.ShapeDtypeStruct((S, A_Q_HEADS * HEAD_DIM), F32),
                   jax.ShapeDtypeStruct((S, LANES), F32), jax.ShapeDtypeStruct((S, LANES), F32),
                   jax.ShapeDtypeStruct((n_pairs, 8, LANES), F32)] + (comm.out_shapes if comm else []),
        scratch_shapes=comm.sem_shapes if comm else [],
        compiler_params=_cparams("arbitrary", "arbitrary"),
    )(qk, qk, v_arr, o_arr, do_arr, lse_arr, sinks, *(comm.ins if comm else []))
    return (*res[:4], res[4:])


ADAMW_BLOCK = 256 * 1024


def _adamw(w, g, m, v, name):
    R, C = w.shape
    tr, tc = _tile(R, max(8, ADAMW_BLOCK // C), 8), C

    def kern(w_ref, g_ref, m_ref, v_ref, d_ref, mo_ref, vo_ref):
        g_ = g_ref[...]
        m_new = ADAM_B1 * m_ref[...] + (1.0 - ADAM_B1) * g_
        v_new = ADAM_B2 * v_ref[...] + (1.0 - ADAM_B2) * (g_ * g_)
        m_hat = m_new / (1.0 - ADAM_B1 ** ADAM_STEP)
        v_hat = v_new / (1.0 - ADAM_B2 ** ADAM_STEP)
        d_ref[...] = -ADAM_LR * (m_hat / (jnp.sqrt(v_hat) + ADAM_EPS) + ADAM_WD * w_ref[...])
        mo_ref[...] = m_new
        vo_ref[...] = v_new

    spec = pl.BlockSpec((tr, tc), lambda i, j: (i, j))
    shape = jax.ShapeDtypeStruct((R, C), F32)
    return pl.pallas_call(
        kern, name=name, grid=(R // tr, C // tc),
        in_specs=[spec] * 4, out_specs=[spec] * 3, out_shape=[shape] * 3,
        compiler_params=_cparams("parallel", "parallel"),
    )(w, g, m, v)


def _index_operand(i):
    return jnp.reshape(i, (1,)).astype(jnp.int32)


def _add_pair(whole, got, ci, name):
    P, R, C = whole.shape
    half = R // 2
    tr = _tile(half, 256, 16)
    nb = half // tr

    def kern(ci_ref, a_ref, b_ref, o_ref, ob_ref):
        s = a_ref[...] + b_ref[...].astype(F32)
        o_ref[...] = s
        ob_ref[...] = s.astype(BF)

    spec = pl.BlockSpec((None, tr, C), lambda p, i, ci_ref: (p, i, 0))
    return pl.pallas_call(
        kern, name=name,
        grid_spec=pltpu.PrefetchScalarGridSpec(
            num_scalar_prefetch=1, grid=(P, nb),
            in_specs=[pl.BlockSpec((None, tr, C), lambda p, i, ci_ref: (p, ci_ref[0] * nb + i, 0)), spec],
            out_specs=[spec, spec]),
        out_shape=[jax.ShapeDtypeStruct((P, half, C), F32), jax.ShapeDtypeStruct((P, half, C), BF)],
        compiler_params=_cparams("parallel", "parallel"),
    )(_index_operand(ci), whole, got)


def _add_three(parts, recv, chip, name):
    _, R, C = parts.shape
    tr = _tile(R, 256, 16)

    def kern(chip_ref, o_ref, r0_ref, r1_ref, r2_ref, out_ref):
        s = ((o_ref[...] + r0_ref[...].astype(F32)) + r1_ref[...].astype(F32)) + r2_ref[...].astype(F32)
        out_ref[0] = s
        out_ref[1] = s

    slab = lambda k: pl.BlockSpec((None, tr, C), lambda i, chip_ref: (k, i, 0))
    return pl.pallas_call(
        kern, name=name,
        grid_spec=pltpu.PrefetchScalarGridSpec(
            num_scalar_prefetch=1, grid=(R // tr,),
            in_specs=[pl.BlockSpec((None, tr, C), lambda i, chip_ref: (chip_ref[0], i, 0)),
                      slab(0), slab(1), slab(2)],
            out_specs=pl.BlockSpec((2, tr, C), lambda i, chip_ref: (0, i, 0))),
        out_shape=jax.ShapeDtypeStruct((2, R, C), F32),
        compiler_params=_cparams("parallel"),
    )(_index_operand(chip), parts, recv, recv, recv)


SM_ADA, SM_G, SM_LOSS, SM_BF, SM_SINK, SM_LEN = 0, 6144, 10240, 11264, 11272, 12288


def _small_finalize(gathered):
    def kern(g_ref, tot_ref, loss_ref):
        tot = g_ref[0:1, :]
        for b in range(1, N_DEV):
            tot = tot + g_ref[b:b + 1, :]
        tot_ref[...] = tot
        sq = jnp.sum(tot[:, SM_LOSS:SM_LOSS + D_MODEL], axis=1, keepdims=True)
        loss_ref[...] = jnp.broadcast_to(sq * (0.5 / D_MODEL), (1, LANES))

    full = lambda shape: pl.BlockSpec(shape, lambda i: (0, 0))
    return pl.pallas_call(
        kern, name="small_finalize", grid=(1,),
        in_specs=[full((N_DEV, SM_LEN))],
        out_specs=[full((1, SM_LEN)), full((1, LANES))],
        out_shape=[jax.ShapeDtypeStruct((1, SM_LEN), F32), jax.ShapeDtypeStruct((1, LANES), F32)],
        compiler_params=_cparams("arbitrary"),
    )(gathered)


def _ada_dw(c_t, d_ada):
    N = d_ada.shape[1]
    tn = _tile(N, 512)

    def kern(c_ref, d_ref, o_ref):
        acc = c_ref[:, 0:1] * d_ref[0:1, :]
        for b in range(1, N_DEV):
            acc = acc + c_ref[:, b:b + 1] * d_ref[b:b + 1, :]
        o_ref[...] = acc

    return pl.pallas_call(
        kern, name="ada_dw", grid=(N // tn,),
        in_specs=[pl.BlockSpec((D_MODEL, N_DEV), lambda j: (0, 0)), pl.BlockSpec((N_DEV, tn), lambda j: (0, j))],
        out_specs=pl.BlockSpec((D_MODEL, tn), lambda j: (0, j)),
        out_shape=jax.ShapeDtypeStruct((D_MODEL, N), F32),
        compiler_params=_cparams("parallel"),
    )(c_t, d_ada)


def _here():
    return lax.axis_index("x"), lax.axis_index("y"), lax.axis_index("c")


def _other_chips(x, y):
    return [(1 - x, y), (x, 1 - y), (1 - x, 1 - y)]


_ANY = pl.BlockSpec(memory_space=pl.ANY)


class _Comm:
    def __init__(self, ins, out_shapes, sem_shapes, start, finish):
        self.ins, self.out_shapes, self.sem_shapes = list(ins), list(out_shapes), list(sem_shapes)
        self.start, self.finish = start, finish

    def split(self, refs, n_in, n_out, n_scratch):
        a = n_in + len(self.ins)
        b = a + n_out + len(self.out_shapes)
        own = list(refs[:n_in]) + list(refs[a:a + n_out]) + list(refs[b:b + n_scratch])
        mine = (refs[n_in:a], refs[a + n_out:b], refs[b + n_scratch:])
        return own, mine


def _run_comm(comm, name):
    n_in, n_out = len(comm.ins), len(comm.out_shapes)

    def body(*refs):
        parts = (refs[:n_in], refs[n_in:n_in + n_out], refs[n_in + n_out:])
        comm.start(*parts)
        comm.finish(*parts)

    return pl.pallas_call(
        body, name=name,
        in_specs=[_ANY] * n_in, out_specs=[_ANY] * n_out,
        out_shape=comm.out_shapes, scratch_shapes=comm.sem_shapes,
    )(*comm.ins)


def _gather_comm(blocks):
    L = len(blocks)

    def parts(ins, outs, sems):
        send_sems, recv_sems, local_sems = sems
        x, y, c = _here()
        me, sibling = (x, y, c), (x, y, 1 - c)
        chips = _other_chips(x, y)

        def slot(px, py, pc):
            return 4 * px + 2 * py + pc

        def copy(l, k, block, to, src=None):
            dst = outs[l].at[slot(*block)]
            return pltpu.make_async_remote_copy(
                src_ref=dst if src is None else src, dst_ref=dst,
                send_sem=send_sems.at[l, k], recv_sem=recv_sems.at[l, k],
                device_id=to, device_id_type=MESH)

        mine = [pltpu.make_async_copy(ins[l], outs[l].at[slot(*me)], local_sems.at[l]) for l in range(L)]
        first = []
        for l in range(L):
            first.append(copy(l, 0, me, sibling, src=ins[l]))
            for j, chip in enumerate(chips):
                first.append(copy(l, 1 + j, me, (*chip, c), src=ins[l]))
        return c, me, sibling, chips, copy, mine, first

    def start(ins, outs, sems):
        *_, mine, first = parts(ins, outs, sems)
        for cp in mine + first:
            cp.start()

    def finish(ins, outs, sems):
        c, me, sibling, chips, copy, mine, first = parts(ins, outs, sems)
        passed = []
        for j, chip in enumerate(chips):
            for l in range(L):
                copy(l, 1 + j, (*chip, c), me).wait_recv()
                fwd = copy(l, 4 + j, (*chip, c), sibling)
                fwd.start()
                passed.append(fwd)
        for l in range(L):
            copy(l, 0, sibling, me).wait_recv()
        for j, chip in enumerate(chips):
            for l in range(L):
                copy(l, 4 + j, (*chip, 1 - c), me).wait_recv()
        for cp in first + passed:
            cp.wait_send()
        for cp in mine:
            cp.wait()

    return _Comm(blocks, [jax.ShapeDtypeStruct((N_DEV,) + b.shape, b.dtype) for b in blocks],
                 [pltpu.SemaphoreType.DMA((L, 7)), pltpu.SemaphoreType.DMA((L, 7)), pltpu.SemaphoreType.DMA((L,))],
                 start, finish)


def _allgather8(blocks, name):
    return _run_comm(_gather_comm(blocks), name)


def _swap_comm(arrs):
    L = len(arrs)

    def copies(ins, outs, sems):
        send_sems, recv_sems = sems
        x, y, c = _here()
        cps = []
        for l in range(L):
            half = arrs[l].shape[1] // 2
            rows = pl.ds(pl.multiple_of((1 - c) * half, 16), half)
            cps.append(pltpu.make_async_remote_copy(
                src_ref=ins[l].at[:, rows, :], dst_ref=outs[l], send_sem=send_sems.at[l],
                recv_sem=recv_sems.at[l], device_id=(x, y, 1 - c), device_id_type=MESH))
        return cps

    def start(ins, outs, sems):
        for cp in copies(ins, outs, sems):
            cp.start()

    def finish(ins, outs, sems):
        for cp in copies(ins, outs, sems):
            cp.wait()

    return _Comm(arrs, [jax.ShapeDtypeStruct((a.shape[0], a.shape[1] // 2, a.shape[2]), a.dtype) for a in arrs],
                 [pltpu.SemaphoreType.DMA((L,)), pltpu.SemaphoreType.DMA((L,))], start, finish)


def _sibling_join(bufs, name):
    L = len(bufs)

    def body(*refs):
        outs = refs[L:2 * L]
        send_sems, recv_sems = refs[2 * L:]
        x, y, c = _here()
        for l in range(L):
            pltpu.make_async_remote_copy(src_ref=outs[l].at[c], dst_ref=outs[l].at[c], send_sem=send_sems.at[l],
                                         recv_sem=recv_sems.at[l], device_id=(x, y, 1 - c),
                                         device_id_type=MESH).start()
        for l in range(L):
            pltpu.make_async_remote_copy(src_ref=outs[l].at[c], dst_ref=outs[l].at[1 - c],
                                         send_sem=send_sems.at[l], recv_sem=recv_sems.at[l],
                                         device_id=(x, y, 1 - c), device_id_type=MESH).wait()

    return pl.pallas_call(
        body, name=name,
        in_specs=[_ANY] * L, out_specs=[_ANY] * L,
        out_shape=[jax.ShapeDtypeStruct(a.shape, a.dtype) for a in bufs],
        input_output_aliases={l: l for l in range(L)},
        scratch_shapes=[pltpu.SemaphoreType.DMA((L,)), pltpu.SemaphoreType.DMA((L,))],
    )(*bufs)


def _scatter_comm(arrs):
    L = len(arrs)

    def copies(ins, outs, sems):
        send_sems, recv_sems = sems
        x, y, c = _here()
        return [pltpu.make_async_remote_copy(
            src_ref=ins[l].at[2 * tx + ty], dst_ref=outs[l].at[j],
            send_sem=send_sems.at[l, j], recv_sem=recv_sems.at[l, j],
            device_id=(tx, ty, c), device_id_type=MESH)
            for l in range(L) for j, (tx, ty) in enumerate(_other_chips(x, y))]

    def start(ins, outs, sems):
        for cp in copies(ins, outs, sems):
            cp.start()

    def finish(ins, outs, sems):
        for cp in copies(ins, outs, sems):
            cp.wait()

    return _Comm(arrs, [jax.ShapeDtypeStruct((3,) + a.shape[1:], a.dtype) for a in arrs],
                 [pltpu.SemaphoreType.DMA((L, 3)), pltpu.SemaphoreType.DMA((L, 3))], start, finish)


_A_ORDER = np.array(A_HEAD_ORDER)
_A_INVERSE = np.argsort(_A_ORDER)


def _permute_in_weights(w_in):
    qa = w_in[:, 0:512].reshape(D_MODEL, A_Q_HEADS, HEAD_DIM)[:, _A_ORDER, :].reshape(D_MODEL, 512)
    f_pad = jnp.pad(w_in[:, 2304:2312], ((0, 0), (0, LANES - B_HEADS)))
    w_a = jnp.concatenate([qa, w_in[:, 512:640], f_pad], axis=1)
    return w_a, w_in[:, 640:2304], w_in[:, 2312:4360]


class _NoExchange:
    def __init__(self, rest):
        self.rest, self.grads = rest, {}

    def rest_weights_comm(self):
        return None

    def rest_weights(self, outs):
        return self.rest

    def swap_comm(self, pieces, tag):
        self.grads[tag] = [p32 for p32, _ in pieces]
        return None

    def swap_done(self, outs, tag):
        return None

    def reduce_done(self, outs, tag):
        pass


class _Exchange:
    def __init__(self, ci, chip, rest_shards):
        self.ci, self.chip, self.rest_shards = ci, chip, rest_shards
        self.pieces, self.part_f32, self.halves = {}, {}, {}

    def _my_half(self, a, axis=0, other=False):
        rows = a.shape[axis] // 2
        return lax.dynamic_slice_in_dim(a, ((1 - self.ci) if other else self.ci) * rows, rows, axis=axis)

    def rest_weights_comm(self):
        return _gather_comm([self._my_half(w).astype(BF) for w in self.rest_shards])

    def rest_weights(self, outs):
        w_ba, w_bb, w_out, w_fi, w_fo = outs
        return (_col_sharded(w_ba), _col_sharded(w_bb), _row_sharded(w_out), _col_sharded(w_fi),
                _row_sharded(w_fo))

    def swap_comm(self, pieces, tag):
        self.pieces[tag] = pieces
        return _swap_comm([pbf for _, pbf in pieces])

    def swap_done(self, got, tag):
        self.part_f32[tag], part_bf = [], []
        for l, ((p32, _), g_) in enumerate(zip(self.pieces[tag], got)):
            s32, sbf = _add_pair(p32, g_, self.ci, f"chip_sum_{tag}_{l}")
            self.part_f32[tag].append(s32)
            part_bf.append(sbf)
        return _scatter_comm(part_bf)

    def reduce_done(self, outs, tag):
        self.halves[tag] = [_add_three(p32, r, self.chip, f"shard_sum_{tag}_{l}")
                            for l, (p32, r) in enumerate(zip(self.part_f32[tag], outs))]


def _col_sharded(g):
    return jnp.transpose(g.reshape(N_CHIP, -1, g.shape[-1]), (1, 0, 2)).reshape(2 * g.shape[1], N_CHIP * g.shape[-1])


def _row_sharded(g):
    return g.reshape(N_DEV * g.shape[1], g.shape[-1])


def _rope_tables(pos):
    inv_freq = 1.0 / (ROPE_THETA ** (jnp.arange(0, HEAD_DIM, 2, dtype=F32) / HEAD_DIM))
    ang = pos.astype(F32)[:, None] * inv_freq
    cos, sin = jnp.cos(ang), jnp.sin(ang)
    return jnp.tile(cos, (1, 4)), jnp.tile(jnp.concatenate([-sin, sin], axis=1), (1, 2))


def _local_step(x, pos, ada, g1, g2, g3, g4, b_f, sinks, w_in4, exch, target):
    S = x.shape[0]
    t_fox = _tile(S, 512, LANES) if S >= 1024 else S // 2
    t_fox_fwd = _tile(S, 1024, LANES) if S >= 2048 else S // 2
    shift_m, scale_m, gate_m, shift_f, scale_f, gate_f = [ada[i:i + 1] for i in range(N_ADA)]
    cos_t, sin_t = _rope_tables(pos)
    by_rows = jnp.transpose(w_in4, (1, 0, 2))
    w_a, w_b, w_g = _permute_in_weights(by_rows[:, :, :W_SHARD].reshape(D_MODEL, N_CHIP * W_SHARD))
    w_in_pad = by_rows.reshape(D_MODEL, N_CHIP * W_SHARD_PAD)
    sinks_p = sinks.reshape(A_KV_HEADS, 4).T.reshape(A_Q_HEADS)
    b_f_pad = jnp.pad(b_f, (0, LANES - B_HEADS)).reshape(1, LANES)

    h1 = _pre_norm(x, g1, scale_m, shift_m, "pre_mix_norm")
    p_a = _mm(h1, w_a, "nn", F32, "proj_a")
    p_b = _mm(h1, w_b, "nn", BF, "proj_b")
    p_g = _mm(h1, w_g, "nn", BF, "proj_g")
    (qk_a,) = _rope([p_a], [640], cos_t, sin_t, "rope_fwd")
    o_a, lse_a = _swa_fwd(qk_a, p_b, 0, sinks_p)
    q_aug, k_aug = _fox_prep_fwd(p_b, _fox_gate_fwd(p_a, b_f_pad), t_fox)
    comm = exch.rest_weights_comm()
    o_b, lse_b, outs = _fox_fwd(q_aug, k_aug, p_b, t_fox_fwd, comm=comm)
    w_ba, w_bb, w_out, w_fi, w_fo = exch.rest_weights(outs)
    w_ba_p = w_ba.reshape(A_Q_HEADS, HEAD_DIM, D_MODEL)[_A_ORDER].reshape(512, D_MODEL)
    pa = _mm(o_a, w_ba_p, "nn", F32, "branch_a")
    pb = _mm(o_b, w_bb, "nn", F32, "branch_b")
    merged = _merge_fwd(p_g, pa, pb)
    y1 = _mm(merged, w_out, "nn", F32, "out_proj")
    x2, h2 = _post_pre(x, y1, g2, gate_m, g3, scale_f, shift_f)
    gu = _mm(h2, w_fi, "nn", BF, "ffn_in")
    act = _swiglu_fwd(gu)
    y2 = _mm(act, w_fo, "nn", F32, "ffn_out")
    d_out, d_y2, st_f = _final(x2, y2, g4, gate_f, target)

    d_act = _mm(d_y2, w_fo, "nt", F32, "ffn_out_dx")
    row_pieces = lambda pair: tuple(t.reshape(N_CHIP, t.shape[0] // N_CHIP, t.shape[1]) for t in pair)
    dw_fo = row_pieces(_mm(act, d_y2, "tn", F32, "ffn_out_dw", twin=True))
    d_gu = _swiglu_bwd(d_act, gu)
    d_h2 = _mm(d_gu, w_fi, "nt", F32, "ffn_in_dx")
    dw_fi = _mm(h2, d_gu, "tn", F32, "ffn_in_dw", col_pieces=N_CHIP, twin=True)
    d_x2, d_y1, st_m = _mid_bwd(d_h2, x2, d_out, y1, g3, scale_f, g2, gate_m)
    d_merged = _mm(d_y1, w_out, "nt", F32, "out_proj_dx")
    dw_out = row_pieces(_mm(merged, d_y1, "tn", F32, "out_proj_dw", twin=True))
    d_pa, d_pb, d_ga, d_gb = _merge_bwd(d_merged, p_g, pa, pb)
    d_oa = _mm(d_pa, w_ba_p, "nt", F32, "branch_a_dx")
    dw_ba_p = _mm(o_a, d_pa, "tn", F32, "branch_a_dw", col_pieces=N_CHIP, twin=True)
    d_ob = _mm(d_pb, w_bb, "nt", F32, "branch_b_dx")
    dw_bb = _mm(o_b, d_pb, "tn", F32, "branch_b_dw", col_pieces=N_CHIP, twin=True)
    head_rows = lambda t: t.reshape(N_CHIP, A_Q_HEADS, HEAD_DIM, -1)[:, _A_INVERSE].reshape(t.shape)
    dw_ba = tuple(head_rows(t) for t in dw_ba_p)
    comm = exch.swap_comm([dw_ba, dw_bb, dw_out, dw_fi, dw_fo], "early")
    dq_a, dk_a, dv_a, d_sink, outs = _swa_bwd(qk_a, p_b, 0, o_a, d_oa, lse_a, sinks_p, comm=comm)
    comm = exch.swap_done(outs, "early")
    qb_aug, dob_aug, vb_aug = _fox_prep_bwd(q_aug, p_b, o_b, d_ob, lse_b, t_fox)
    dq_b, dk_b, dv_b, d_ck, d_cq, outs = _fox_bwd(qb_aug, k_aug, dob_aug, vb_aug, t_fox, comm=comm)
    exch.reduce_done(outs, "early")
    d_qa, d_ka = _rope([dq_a, dk_a], [512, LANES], cos_t, -sin_t, "rope_bwd")
    d_ck_cols = jnp.pad(d_ck.reshape(B_HEADS, S).T, ((0, 0), (0, LANES - B_HEADS)))
    d_f, d_bf = _fox_gate_bwd(d_cq, d_ck_cols, p_a, b_f_pad)
    d_qa_heads = d_qa.reshape(S, A_Q_HEADS, HEAD_DIM)[:, _A_INVERSE].reshape(S, A_Q_HEADS * HEAD_DIM)
    d_proj = jnp.concatenate([d_qa_heads, d_ka, dv_a.astype(BF), dq_b.astype(BF), dk_b.astype(BF),
                              dv_b.astype(BF), d_f[:, :B_HEADS], d_ga, d_gb], axis=1)
    d_proj = jnp.pad(d_proj.reshape(S, N_CHIP, W_SHARD), ((0, 0), (0, 0), (0, W_SHARD_PAD - W_SHARD))
                     ).reshape(S, N_CHIP * W_SHARD_PAD)
    dw_in = _mm(h1, d_proj, "tn", F32, "proj_dw", col_pieces=N_CHIP, twin=True)
    swap = exch.swap_comm([dw_in], "late")
    comm = exch.swap_done(_run_comm(swap, "grads_to_sibling_late") if swap else None, "late")
    res = _mm(d_proj, w_in_pad, "nt", F32, "proj_dx", comm=comm)
    d_h1 = res[0] if comm else res
    exch.reduce_done(res[1] if comm else None, "late")
    grad_x, st_p = _pre_bwd(d_h1, x, d_x2, g1, scale_m)

    d_sinks = d_sink[:, :2, 0].T.reshape(A_Q_HEADS)
    small = jnp.concatenate([
        st_p[0], st_p[1], st_m[3], st_m[0], st_m[1], st_f[0],
        st_p[2], st_m[4], st_m[2], st_f[1],
        st_f[2], d_bf[0, :B_HEADS], d_sinks,
        jnp.zeros((SM_LEN - SM_SINK - A_Q_HEADS,), F32)])
    return grad_x, small


def kernel(x, c, positions, w_ada, b_ada, g_pre_mix, g_post_mix, w_in, b_f, sinks, w_branch_a, w_branch_b, w_out, g_pre_ffn, g_post_ffn, w_ffn_in, w_ffn_out, loss_target, m_w_ada, m_b_ada, m_g_pre_mix, m_g_post_mix, m_w_in, m_b_f, m_sinks, m_w_branch_a, m_w_branch_b, m_w_out, m_g_pre_ffn, m_g_post_ffn, m_w_ffn_in, m_w_ffn_out, v_w_ada, v_b_ada, v_g_pre_mix, v_g_post_mix, v_w_in, v_b_f, v_sinks, v_w_branch_a, v_w_branch_b, v_w_out, v_g_pre_ffn, v_g_post_ffn, v_w_ffn_in, v_w_ffn_out):
    xi, yi, ci = _here()
    chip = 2 * xi + yi
    dev = 2 * chip + ci

    def my_half(a):
        rows = a.shape[0] // 2
        return lax.dynamic_slice_in_dim(a, ci * rows, rows, axis=0)

    pad_cols = lambda a: jnp.pad(a, ((0, 0), (0, W_SHARD_PAD - W_SHARD)))
    c_g, w_in_g = _allgather8([c.reshape(8, LANES), pad_cols(my_half(w_in[0])).astype(BF)], "gather_w_in")
    c_all = c_g.reshape(N_DEV, D_MODEL)
    w_in4 = w_in_g.reshape(N_CHIP, D_MODEL, W_SHARD_PAD)
    exch = _Exchange(ci, chip, [w_branch_a[0], w_branch_b[0], w_out[0], w_ffn_in[0], w_ffn_out[0]])

    ada_cols = _mm(c_all, w_ada[0], "nn", F32, "ada_fwd")
    (ada_g,) = _allgather8([ada_cols], "gather_ada")
    ada_mine = lax.dynamic_index_in_dim(ada_g.reshape(N_CHIP, 2, N_DEV, -1)[:, 0], dev, axis=1, keepdims=False)
    ada = (ada_mine.reshape(-1) + b_ada[0]).reshape(N_ADA, D_MODEL)

    grad_x, small = _local_step(
        x[0], positions[0], ada, g_pre_mix, g_post_mix, g_pre_ffn, g_post_ffn, b_f[0], sinks[0],
        w_in4, exch, loss_target[0])

    (small_g,) = _allgather8([small.reshape(8, SM_LEN // 8)], "gather_small")
    small_all = small_g.reshape(N_DEV, SM_LEN)
    small_tot, loss_row = _small_finalize(small_all)
    loss = loss_row[0, 0]
    d_ada_cols = lax.dynamic_slice_in_dim(small_all[:, :N_ADA * D_MODEL], chip * (N_ADA * D_MODEL // N_CHIP),
                                          N_ADA * D_MODEL // N_CHIP, axis=1)
    g_w_ada = _ada_dw(c_all.T, d_ada_cols)

    joined = _sibling_join(exch.halves["late"] + exch.halves["early"], "grads_join")
    g_w_in, g_w_ba, g_w_bb, g_w_out, g_w_fi, g_w_fo = [j.reshape(2 * j.shape[1], j.shape[2]) for j in joined]

    def small_vec(b_ada_, g1_, g2_, g3_, g4_, b_f_, sinks_):
        return jnp.concatenate([b_ada_[0], g1_[0], g2_[0], g3_[0], g4_[0], jnp.zeros((D_MODEL,), F32),
                                b_f_[0], sinks_[0], jnp.zeros((SM_LEN - SM_SINK - A_Q_HEADS,), F32)]
                               ).reshape(8, SM_LEN // 8)

    sw = small_vec(b_ada, g_pre_mix, g_post_mix, g_pre_ffn, g_post_ffn, b_f, sinks)
    sm = small_vec(m_b_ada, m_g_pre_mix, m_g_post_mix, m_g_pre_ffn, m_g_post_ffn, m_b_f, m_sinks)
    sv = small_vec(v_b_ada, v_g_pre_mix, v_g_post_mix, v_g_pre_ffn, v_g_post_ffn, v_b_f, v_sinks)
    s_upd = [u.reshape(SM_LEN) for u in _adamw(sw, small_tot.reshape(8, SM_LEN // 8), sm, sv, "adamw_small")]
    s_grad = small_tot.reshape(SM_LEN)

    def unpack(vec):
        row = lambda a, n: vec[a:a + n].reshape(1, n)
        return dict(b_ada=row(SM_ADA, N_ADA * D_MODEL), g_pre_mix=row(SM_G, D_MODEL),
                    g_post_mix=row(SM_G + D_MODEL, D_MODEL), g_pre_ffn=row(SM_G + 2 * D_MODEL, D_MODEL),
                    g_post_ffn=row(SM_G + 3 * D_MODEL, D_MODEL), b_f=row(SM_BF, B_HEADS),
                    sinks=row(SM_SINK, A_Q_HEADS))

    big = dict(
        w_ada=(w_ada, g_w_ada, m_w_ada, v_w_ada),
        w_branch_a=(w_branch_a, g_w_ba, m_w_branch_a, v_w_branch_a),
        w_branch_b=(w_branch_b, g_w_bb, m_w_branch_b, v_w_branch_b),
        w_out=(w_out, g_w_out, m_w_out, v_w_out), w_ffn_in=(w_ffn_in, g_w_fi, m_w_ffn_in, v_w_ffn_in),
        w_ffn_out=(w_ffn_out, g_w_fo, m_w_ffn_out, v_w_ffn_out))
    grads, deltas, new_m, new_v = unpack(s_grad), unpack(s_upd[0]), unpack(s_upd[1]), unpack(s_upd[2])
    for n, (w_, g_, m_, v_) in big.items():
        d_, nm_, nv_ = _adamw(w_[0], g_, m_[0], v_[0], "adamw_" + n)
        grads[n], deltas[n], new_m[n], new_v[n] = g_[None], d_[None], nm_[None], nv_[None]
    upd = _adamw(pad_cols(w_in[0]), g_w_in, pad_cols(m_w_in[0]), pad_cols(v_w_in[0]), "adamw_w_in")
    grads["w_in"], deltas["w_in"], new_m["w_in"], new_v["w_in"] = [t[None, :, :W_SHARD] for t in (g_w_in, *upd)]

    names = ["w_ada", "b_ada", "g_pre_mix", "g_post_mix", "w_in", "b_f", "sinks", "w_branch_a", "w_branch_b",
             "w_out", "g_pre_ffn", "g_post_ffn", "w_ffn_in", "w_ffn_out"]
    return (loss, grad_x[None], *[grads[n] for n in names], *[deltas[n] for n in names],
            *[new_m[n] for n in names], *[new_v[n] for n in names])
```

```python
import functools
import math

import numpy as np
import jax
import jax.numpy as jnp
from jax import lax
from jax.experimental import pallas as pl
from jax.experimental.pallas import tpu as pltpu

F32 = jnp.float32
BF = jnp.bfloat16

D_MODEL = 1024
HEAD_DIM = 64
LANES = 128
WINDOW = 128
A_Q_HEADS = 8
A_KV_HEADS = 2
B_HEADS = 8
D_FF = 2816
ROPE_THETA = 10000.0
RMS_EPS = 1e-6
N_ADA = 6
N_DEV = 8
N_CHIP = 4

ADAM_LR = 0.001
ADAM_B1 = 0.9
ADAM_B2 = 0.999
ADAM_EPS = 1e-08
ADAM_WD = 0.01
ADAM_STEP = 10

VMEM_LIMIT = 48 * 1024 * 1024
MESH = pl.DeviceIdType.MESH

A_HEAD_ORDER = (0, 4, 1, 5, 2, 6, 3, 7)

OFF_QA, OFF_KA, OFF_F = 0, 512, 640
W_A = 768
OFF_VA, OFF_QB, OFF_KB, OFF_VB = 0, 128, 640, 1152
W_B = 1664
W_G = 2048
W_PERM = W_A + W_B + W_G
W_SHARD = 1090
W_SHARD_PAD = 1152


def _tile(n, cap, mult=LANES):
    if n <= cap:
        return n
    t = (cap // mult) * mult
    while t >= mult:
        if n % t == 0:
            return t
        t -= mult
    raise ValueError(f"no tile for {n}")


MXU_WIDTH = 256
MM_OPERAND_BYTES = 28 * 1024 * 1024


def _mm_tiles(M, N, K, a_bytes, b_bytes, tm_cap, tn_cap):
    tm = _tile(M, tm_cap)
    try:
        tn = _tile(N, tn_cap, MXU_WIDTH)
    except ValueError:
        tn = _tile(N, tn_cap)
    fits = lambda tk: 2 * tk * (tm * a_bytes + tn * b_bytes) <= MM_OPERAND_BYTES
    tk = K if fits(K) else next(t for t in range(K // LANES * LANES, 0, -LANES) if K % t == 0 and fits(t))
    return tm, tn, tk


def _cparams(*sem):
    return pltpu.CompilerParams(dimension_semantics=sem, vmem_limit_bytes=VMEM_LIMIT)


def _own_refs(refs, comm, n_in, n_out, n_scratch):
    if comm is None:
        return list(refs), None
    return comm.split(refs, n_in, n_out, n_scratch)


def _comm_specs(comm, side):
    if comm is None:
        return []
    return [pl.BlockSpec(memory_space=pl.ANY)] * len(comm.ins if side == "in" else comm.out_shapes)


def _comm_edge(comm, comm_refs, grid, first):
    if comm is None:
        return
    at_edge = None
    for axis, n in enumerate(grid):
        here = pl.program_id(axis) == (0 if first else n - 1)
        at_edge = here if at_edge is None else at_edge & here
    pl.when(at_edge)(lambda: (comm.start if first else comm.finish)(*comm_refs))


def _mm(a, b, mode, out_dtype, name, tm_cap=512, tn_cap=2816, comm=None, col_pieces=1, twin=False):
    if mode == "nn":
        (M, K), (K2, N) = a.shape, b.shape
        dims = (((1,), (0,)), ((), ()))
    elif mode == "nt":
        (M, K), (N, K2) = a.shape, b.shape
        dims = (((1,), (1,)), ((), ()))
    else:
        (K, M), (K2, N) = a.shape, b.shape
        dims = (((0,), (0,)), ((), ()))
    assert K == K2, (a.shape, b.shape, mode)
    tm, tn, tk = _mm_tiles(M, N // col_pieces, K, a.dtype.itemsize, b.dtype.itemsize, tm_cap, tn_cap)
    nk = K // tk
    n_out = 2 if twin else 1
    n_scratch = 1 if nk > 1 else 0
    if mode == "nn":
        a_spec = pl.BlockSpec((tm, tk), lambda i, j, k: (i, k))
        b_spec = pl.BlockSpec((tk, tn), lambda i, j, k: (k, j))
    elif mode == "nt":
        a_spec = pl.BlockSpec((tm, tk), lambda i, j, k: (i, k))
        b_spec = pl.BlockSpec((tn, tk), lambda i, j, k: (j, k))
    else:
        a_spec = pl.BlockSpec((tk, tm), lambda i, j, k: (k, i))
        b_spec = pl.BlockSpec((tk, tn), lambda i, j, k: (k, j))

    grid = (M // tm, N // tn, nk)

    def kern(*refs):
        own, comm_refs = _own_refs(refs, comm, 2, n_out, n_scratch)
        a_ref, b_ref, o_refs = own[0], own[1], own[2:2 + n_out]
        k = pl.program_id(2)
        _comm_edge(comm, comm_refs, grid, first=True)
        part = lax.dot_general(a_ref[...].astype(BF), b_ref[...].astype(BF), dims,
                               preferred_element_type=F32)
        if nk == 1:
            for o_ref in o_refs:
                o_ref[...] = part.astype(o_ref.dtype)
        else:
            acc_ref = own[2 + n_out]

            @pl.when(k == 0)
            def _():
                acc_ref[...] = part

            @pl.when(k > 0)
            def _():
                acc_ref[...] += part

            @pl.when(k == nk - 1)
            def _():
                for o_ref in o_refs:
                    o_ref[...] = acc_ref[...].astype(o_ref.dtype)

        _comm_edge(comm, comm_refs, grid, first=False)

    if col_pieces > 1:
        per = N // col_pieces // tn
        out_spec = pl.BlockSpec((None, tm, tn), lambda i, j, k: (j // per, i, j % per))
        shape = (col_pieces, M, N // col_pieces)
    else:
        out_spec = pl.BlockSpec((tm, tn), lambda i, j, k: (i, j))
        shape = (M, N)
    dtypes = [out_dtype, BF] if twin else [out_dtype]
    res = pl.pallas_call(
        kern, name=name, grid=grid,
        in_specs=[a_spec, b_spec] + _comm_specs(comm, "in"),
        out_specs=[out_spec] * n_out + _comm_specs(comm, "out"),
        out_shape=[jax.ShapeDtypeStruct(shape, d) for d in dtypes] + (comm.out_shapes if comm else []),
        scratch_shapes=[pltpu.VMEM((tm, tn), F32)] * n_scratch + (comm.sem_shapes if comm else []),
        compiler_params=_cparams("parallel", "parallel", "arbitrary"),
    )(a, b, *(comm.ins if comm else []))
    own = res[0] if n_out == 1 else tuple(res[:n_out])
    return (own, res[n_out:]) if comm else own


ROWS = 256


def _row_spec(tm, width=D_MODEL, col=0):
    return pl.BlockSpec((tm, width), lambda i: (i, col))


def _vec_spec(width=D_MODEL):
    return pl.BlockSpec((1, width), lambda i: (0, 0))


def _rms(x):
    return lax.rsqrt(jnp.mean(x * x, axis=-1, keepdims=True) + RMS_EPS)


def _colsum(x):
    return jnp.sum(x, axis=0, keepdims=True)


def _norm_bwd(d_xn, xn, r):
    return r * (d_xn - xn * jnp.mean(d_xn * xn, axis=-1, keepdims=True))


def _pre_norm(x, g, scale, shift, name):
    S = x.shape[0]
    tm = _tile(S, ROWS, 8)

    def kern(x_ref, g_ref, sc_ref, sh_ref, h_ref):
        xf = x_ref[...]
        y = xf * _rms(xf) * g_ref[...]
        h_ref[...] = (y * (1.0 + sc_ref[...]) + sh_ref[...]).astype(BF)

    return pl.pallas_call(
        kern, name=name, grid=(S // tm,),
        in_specs=[_row_spec(tm), _vec_spec(), _vec_spec(), _vec_spec()],
        out_specs=_row_spec(tm),
        out_shape=jax.ShapeDtypeStruct((S, D_MODEL), BF),
        compiler_params=_cparams("parallel"),
    )(x, g, scale, shift)


def _post_pre(x, y1, g2, gate_m, g3, scale_f, shift_f):
    S = x.shape[0]
    tm = _tile(S, ROWS, 8)

    def kern(x_ref, y_ref, g2_ref, gm_ref, g3_ref, sc_ref, sh_ref, x2_ref, h2_ref):
        y = y_ref[...]
        n2 = y * _rms(y) * g2_ref[...]
        x2 = x_ref[...] + gm_ref[...] * n2
        x2_ref[...] = x2
        n3 = x2 * _rms(x2) * g3_ref[...]
        h2_ref[...] = (n3 * (1.0 + sc_ref[...]) + sh_ref[...]).astype(BF)

    return pl.pallas_call(
        kern, name="post_mix_pre_ffn", grid=(S // tm,),
        in_specs=[_row_spec(tm), _row_spec(tm)] + [_vec_spec()] * 5,
        out_specs=[_row_spec(tm), _row_spec(tm)],
        out_shape=[jax.ShapeDtypeStruct((S, D_MODEL), F32), jax.ShapeDtypeStruct((S, D_MODEL), BF)],
        compiler_params=_cparams("parallel"),
    )(x, y1, g2, gate_m, g3, scale_f, shift_f)


def _stats_spec():
    return pl.BlockSpec((8, D_MODEL), lambda i: (0, 0))


def _final(x2, y2, g4, gate_f, target):
    S = x2.shape[0]
    tm = _tile(S, ROWS, 8)

    def kern(x2_ref, y_ref, g4_ref, gf_ref, t_ref, dout_ref, dy_ref, st_ref):
        @pl.when(pl.program_id(0) == 0)
        def _():
            st_ref[...] = jnp.zeros_like(st_ref)

        y = y_ref[...]
        r = _rms(y)
        yn = y * r
        n4 = yn * g4_ref[...]
        diff = x2_ref[...] + gf_ref[...] * n4 - t_ref[...]
        d_out = diff / D_MODEL
        dout_ref[...] = d_out
        dn = d_out * gf_ref[...]
        dy_ref[...] = _norm_bwd(dn * g4_ref[...], yn, r).astype(BF)
        st_ref[0:1, :] += _colsum(d_out * n4)
        st_ref[1:2, :] += _colsum(dn * yn)
        st_ref[2:3, :] += _colsum(diff * diff)

    return pl.pallas_call(
        kern, name="final_loss", grid=(S // tm,),
        in_specs=[_row_spec(tm), _row_spec(tm), _vec_spec(), _vec_spec(), _row_spec(tm)],
        out_specs=[_row_spec(tm), _row_spec(tm), _stats_spec()],
        out_shape=[jax.ShapeDtypeStruct((S, D_MODEL), F32), jax.ShapeDtypeStruct((S, D_MODEL), BF),
                   jax.ShapeDtypeStruct((8, D_MODEL), F32)],
        compiler_params=_cparams("arbitrary"),
    )(x2, y2, g4, gate_f, target)


def _mid_bwd(d_h2, x2, d_out, y1, g3, scale_f, g2, gate_m):
    S = x2.shape[0]
    tm = _tile(S, ROWS, 8)

    def kern(dh_ref, x2_ref, dout_ref, y_ref, g3_ref, sc_ref, g2_ref, gm_ref, dx2_ref, dy_ref, st_ref):
        @pl.when(pl.program_id(0) == 0)
        def _():
            st_ref[...] = jnp.zeros_like(st_ref)

        dh = dh_ref[...]
        x2 = x2_ref[...]
        r3 = _rms(x2)
        xn = x2 * r3
        one_sc = 1.0 + sc_ref[...]
        d_x2 = dout_ref[...] + _norm_bwd(dh * one_sc * g3_ref[...], xn, r3)
        dx2_ref[...] = d_x2
        y = y_ref[...]
        r2 = _rms(y)
        yn = y * r2
        dn = d_x2 * gm_ref[...]
        dy_ref[...] = _norm_bwd(dn * g2_ref[...], yn, r2).astype(BF)
        st_ref[0:1, :] += _colsum(dh)
        st_ref[1:2, :] += _colsum(dh * (xn * g3_ref[...]))
        st_ref[2:3, :] += _colsum(dh * one_sc * xn)
        st_ref[3:4, :] += _colsum(d_x2 * (yn * g2_ref[...]))
        st_ref[4:5, :] += _colsum(dn * yn)

    return pl.pallas_call(
        kern, name="mid_bwd", grid=(S // tm,),
        in_specs=[_row_spec(tm)] * 4 + [_vec_spec()] * 4,
        out_specs=[_row_spec(tm), _row_spec(tm), _stats_spec()],
        out_shape=[jax.ShapeDtypeStruct((S, D_MODEL), F32), jax.ShapeDtypeStruct((S, D_MODEL), BF),
                   jax.ShapeDtypeStruct((8, D_MODEL), F32)],
        compiler_params=_cparams("arbitrary"),
    )(d_h2, x2, d_out, y1, g3, scale_f, g2, gate_m)


def _pre_bwd(d_h1, x, d_x2, g1, scale_m):
    S = x.shape[0]
    tm = _tile(S, ROWS, 8)

    def kern(dh_ref, x_ref, dx2_ref, g_ref, sc_ref, gx_ref, st_ref):
        @pl.when(pl.program_id(0) == 0)
        def _():
            st_ref[...] = jnp.zeros_like(st_ref)

        dh = dh_ref[...]
        xf = x_ref[...]
        r = _rms(xf)
        xn = xf * r
        one_sc = 1.0 + sc_ref[...]
        gx_ref[...] = dx2_ref[...] + _norm_bwd(dh * one_sc * g_ref[...], xn, r)
        st_ref[0:1, :] += _colsum(dh)
        st_ref[1:2, :] += _colsum(dh * (xn * g_ref[...]))
        st_ref[2:3, :] += _colsum(dh * one_sc * xn)

    return pl.pallas_call(
        kern, name="pre_mix_bwd", grid=(S // tm,),
        in_specs=[_row_spec(tm)] * 3 + [_vec_spec()] * 2,
        out_specs=[_row_spec(tm), _stats_spec()],
        out_shape=[jax.ShapeDtypeStruct((S, D_MODEL), F32), jax.ShapeDtypeStruct((8, D_MODEL), F32)],
        compiler_params=_cparams("arbitrary"),
    )(d_h1, x, d_x2, g1, scale_m)


def _rope(xs, widths, cos_t, sin_t, name):
    S = xs[0].shape[0]
    tm = _tile(S, 512, 8)
    n = len(xs)

    def kern(*refs):
        cos = refs[n][...]
        sin = refs[n + 1][...]
        first = (lax.broadcasted_iota(jnp.int32, cos.shape, 1) % HEAD_DIM) < HEAD_DIM // 2
        for x_ref, o_ref, w in zip(refs[:n], refs[n + 2:], widths):
            for c0 in range(0, w, LANES):
                v = x_ref[:, c0:c0 + LANES]
                partner = jnp.where(first, pltpu.roll(v, LANES - HEAD_DIM // 2, 1),
                                    pltpu.roll(v, HEAD_DIM // 2, 1))
                o_ref[:, c0:c0 + LANES] = (v * cos + partner * sin).astype(BF)

    return pl.pallas_call(
        kern, name=name, grid=(S // tm,),
        in_specs=[_row_spec(tm, w) for w in widths] + [_row_spec(tm, LANES)] * 2,
        out_specs=[_row_spec(tm, w) for w in widths],
        out_shape=[jax.ShapeDtypeStruct((S, w), BF) for w in widths],
        compiler_params=_cparams("parallel"),
    )(*xs, cos_t, sin_t)


def _merge_fwd(pg, pa, pb):
    S = pa.shape[0]
    tm = _tile(S, ROWS, 8)

    def kern(ga_ref, gb_ref, pa_ref, pb_ref, o_ref):
        ga = jax.nn.sigmoid(ga_ref[...].astype(F32))
        gb = jax.nn.sigmoid(gb_ref[...].astype(F32))
        o_ref[...] = (ga * pa_ref[...] + gb * pb_ref[...]).astype(BF)

    return pl.pallas_call(
        kern, name="merge_fwd", grid=(S // tm,),
        in_specs=[_row_spec(tm, col=0), _row_spec(tm, col=1), _row_spec(tm), _row_spec(tm)],
        out_specs=_row_spec(tm),
        out_shape=jax.ShapeDtypeStruct((S, D_MODEL), BF),
        compiler_params=_cparams("parallel"),
    )(pg, pg, pa, pb)


def _merge_bwd(d_merged, pg, pa, pb):
    S = pa.shape[0]
    tm = _tile(S, ROWS, 8)

    def kern(dm_ref, ga_ref, gb_ref, pa_ref, pb_ref, dpa_ref, dpb_ref, dga_ref, dgb_ref):
        dm = dm_ref[...]
        ga = jax.nn.sigmoid(ga_ref[...].astype(F32))
        gb = jax.nn.sigmoid(gb_ref[...].astype(F32))
        dpa_ref[...] = (dm * ga).astype(BF)
        dpb_ref[...] = (dm * gb).astype(BF)
        dga_ref[...] = (dm * pa_ref[...] * ga * (1.0 - ga)).astype(BF)
        dgb_ref[...] = (dm * pb_ref[...] * gb * (1.0 - gb)).astype(BF)

    bf_out = jax.ShapeDtypeStruct((S, D_MODEL), BF)
    return pl.pallas_call(
        kern, name="merge_bwd", grid=(S // tm,),
        in_specs=[_row_spec(tm), _row_spec(tm, col=0), _row_spec(tm, col=1), _row_spec(tm), _row_spec(tm)],
        out_specs=[_row_spec(tm)] * 4,
        out_shape=[bf_out] * 4,
        compiler_params=_cparams("parallel"),
    )(d_merged, pg, pg, pa, pb)


def _swiglu_fwd(gu):
    S = gu.shape[0]
    tm = _tile(S, ROWS, 8)
    tc = _tile(D_FF, 1408)
    nc = D_FF // tc

    def kern(g_ref, u_ref, o_ref):
        g = g_ref[...].astype(F32)
        o_ref[...] = (g * jax.nn.sigmoid(g) * u_ref[...].astype(F32)).astype(BF)

    return pl.pallas_call(
        kern, name="swiglu_fwd", grid=(S // tm, nc),
        in_specs=[pl.BlockSpec((tm, tc), lambda i, j: (i, j)),
                  pl.BlockSpec((tm, tc), lambda i, j: (i, j + nc))],
        out_specs=pl.BlockSpec((tm, tc), lambda i, j: (i, j)),
        out_shape=jax.ShapeDtypeStruct((S, D_FF), BF),
        compiler_params=_cparams("parallel", "parallel"),
    )(gu, gu)


def _swiglu_bwd(d_act, gu):
    S = gu.shape[0]
    tm = _tile(S, 128, 8)

    def kern(da_ref, g_ref, u_ref, o_ref):
        g = g_ref[...].astype(F32)
        u = u_ref[...].astype(F32)
        da = da_ref[...]
        sg = jax.nn.sigmoid(g)
        o_ref[:, :D_FF] = (da * u * (sg * (1.0 + g * (1.0 - sg)))).astype(BF)
        o_ref[:, D_FF:] = (da * (g * sg)).astype(BF)

    return pl.pallas_call(
        kern, name="swiglu_bwd", grid=(S // tm,),
        in_specs=[_row_spec(tm, D_FF), _row_spec(tm, D_FF, 0), _row_spec(tm, D_FF, 1)],
        out_specs=_row_spec(tm, 2 * D_FF),
        out_shape=jax.ShapeDtypeStruct((S, 2 * D_FF), BF),
        compiler_params=_cparams("parallel"),
    )(d_act, gu, gu)


def _split3(x):
    hi = x.astype(BF)
    r1 = x - hi.astype(F32)
    mid = r1.astype(BF)
    lo = (r1 - mid.astype(F32)).astype(BF)
    return hi, mid, lo


def _tri_dot(tri, x):
    return sum(jnp.dot(tri, part, preferred_element_type=F32) for part in _split3(x))


def _log_sigmoid(z):
    return jnp.minimum(z, 0.0) - jnp.log(1.0 + jnp.exp(-jnp.abs(z)))


def _fox_gate_fwd(pa, b_f_pad):
    S = pa.shape[0]
    T = _tile(S, 512, 8)
    f_col = OFF_F // LANES

    def kern(z_ref, b_ref, cum_ref, carry_ref):
        @pl.when(pl.program_id(0) == 0)
        def _():
            carry_ref[...] = jnp.zeros_like(carry_ref)

        log_f = _log_sigmoid(z_ref[...] + b_ref[...])
        row = lax.broadcasted_iota(jnp.int32, (T, T), 0)
        col = lax.broadcasted_iota(jnp.int32, (T, T), 1)
        tri = (col <= row).astype(BF)
        cum = _tri_dot(tri, log_f) + carry_ref[...]
        cum_ref[...] = cum
        carry_ref[...] = cum[T - 1:T, :]

    return pl.pallas_call(
        kern, name="fox_gate_fwd", grid=(S // T,),
        in_specs=[_row_spec(T, LANES, f_col), _vec_spec(LANES)],
        out_specs=_row_spec(T, LANES),
        out_shape=jax.ShapeDtypeStruct((S, LANES), F32),
        scratch_shapes=[pltpu.VMEM((1, LANES), F32)],
        compiler_params=_cparams("arbitrary"),
    )(pa, b_f_pad)


def _fox_gate_bwd(rowsum_ds, colsum_ds, pa, b_f_pad):
    S = pa.shape[0]
    T = _tile(S, 512, 8)
    nb = S // T
    f_col = OFF_F // LANES

    def kern(dr_ref, dc_ref, z_ref, b_ref, df_ref, dbf_ref, carry_ref):
        @pl.when(pl.program_id(0) == 0)
        def _():
            carry_ref[...] = jnp.zeros_like(carry_ref)
            dbf_ref[...] = jnp.zeros_like(dbf_ref)

        row = lax.broadcasted_iota(jnp.int32, (T, T), 0)
        col = lax.broadcasted_iota(jnp.int32, (T, T), 1)
        tri = (col >= row).astype(BF)
        rev = _tri_dot(tri, dr_ref[...] - dc_ref[...]) + carry_ref[...]
        carry_ref[...] = rev[0:1, :]
        z = z_ref[...] + b_ref[...]
        lane = lax.broadcasted_iota(jnp.int32, (T, LANES), 1)
        d_z = jnp.where(lane < B_HEADS, rev * jax.nn.sigmoid(-z), 0.0)
        df_ref[...] = d_z.astype(BF)
        dbf_ref[0:1, :] += _colsum(d_z)

    return pl.pallas_call(
        kern, name="fox_gate_bwd", grid=(nb,),
        in_specs=[pl.BlockSpec((T, LANES), lambda i: (nb - 1 - i, 0)),
                  pl.BlockSpec((T, LANES), lambda i: (nb - 1 - i, 0)),
                  pl.BlockSpec((T, LANES), lambda i: (nb - 1 - i, f_col)),
                  _vec_spec(LANES)],
        out_specs=[pl.BlockSpec((T, LANES), lambda i: (nb - 1 - i, 0)),
                   pl.BlockSpec((8, LANES), lambda i: (0, 0))],
        out_shape=[jax.ShapeDtypeStruct((S, LANES), BF), jax.ShapeDtypeStruct((8, LANES), F32)],
        scratch_shapes=[pltpu.VMEM((1, LANES), F32)],
        compiler_params=_cparams("arbitrary"),
    )(rowsum_ds, colsum_ds, pa, b_f_pad)


NEG_INF = float("-inf")
QK_SCALE = 1.0 / math.sqrt(HEAD_DIM)


def _half_mask(shape, half):
    lane = lax.broadcasted_iota(jnp.int32, shape, 1)
    return (lane < HEAD_DIM) if half == 0 else (lane >= HEAD_DIM)


def _valid(i, j, T, rowcol, window):
    rel = (i - j) * T + rowcol
    ok = rel >= 0
    if window is not None:
        ok = ok & (rel < window)
    return ok


def _attn_fwd(q_arr, q_col, k_arr, k_col, v_arr, v_col, n_pairs, kv_shared, T, window,
              cq_arr, ck_arr, sinks, name, comm=None):
    S = q_arr.shape[0]
    nq = S // T
    use_bias = cq_arr is not None
    use_sink = sinks is not None
    back = 0 if window is None else -(-window // T)
    grid = (n_pairs, nq)
    n_in = 3 + 2 * use_bias + use_sink

    def kern(*refs):
        refs, comm_refs = _own_refs(refs, comm, n_in, 2, 0)
        _comm_edge(comm, comm_refs, grid, first=True)
        q_ref, k_ref, v_ref = refs[:3]
        pos = 3
        if use_bias:
            cq_ref, ck_ref = refs[pos:pos + 2]
            pos += 2
        if use_sink:
            sink_ref = refs[pos]
            pos += 1
        o_ref, lse_ref = refs[pos:pos + 2]
        p_id = pl.program_id(0)
        i = pl.program_id(1)
        q = q_ref[...]
        rowcol = lax.broadcasted_iota(jnp.int32, (T, T), 0) - lax.broadcasted_iota(jnp.int32, (T, T), 1)
        lo = jnp.maximum(i - back, 0) if window is not None else 0
        outs, lses = [], []
        for half in (0, 1):
            hm = _half_mask((T, LANES), half)
            qh = (jnp.where(hm, q, 0).astype(F32) * QK_SCALE).astype(BF)
            if use_bias:
                cq = cq_ref[:, half * HEAD_DIM:half * HEAD_DIM + 1]
            if use_sink:
                m0 = jnp.full((T, 1), sink_ref[2 * p_id + half], F32)
                l0 = jnp.ones((T, 1), F32)
            else:
                m0 = jnp.full((T, 1), NEG_INF, F32)
                l0 = jnp.zeros((T, 1), F32)

            def step(j, carry, masked):
                m, l, acc = carry
                rows = pl.ds(pl.multiple_of(j * T, T), T)
                kj = k_ref[rows, :].astype(BF)
                vj = v_ref[rows, :].astype(BF)
                s = lax.dot_general(qh, kj, (((1,), (1,)), ((), ())), preferred_element_type=F32)
                if use_bias:
                    s = s + cq - ck_ref[0, half:half + 1, rows]
                if masked:
                    s = jnp.where(_valid(i, j, T, rowcol, window), s, NEG_INF)
                m_new = jnp.maximum(m, jnp.max(s, axis=1, keepdims=True))
                alpha = jnp.exp(m - m_new)
                p = jnp.exp(s - m_new)
                l_new = alpha * l + jnp.sum(p, axis=1, keepdims=True)
                acc_new = alpha * acc + jnp.dot(p.astype(BF), vj, preferred_element_type=F32)
                return m_new, l_new, acc_new

            init = (m0, l0, jnp.zeros((T, LANES), F32))
            if window is None:
                init = lax.fori_loop(0, i, functools.partial(step, masked=False), init)
                m, l, acc = step(i, init, True)
            else:
                m, l, acc = lax.fori_loop(lo, i + 1, functools.partial(step, masked=True), init)
            outs.append(acc / l)
            lses.append(m + jnp.log(l))
        hm0 = _half_mask((T, LANES), 0)
        o_ref[...] = jnp.where(hm0, outs[0], outs[1])
        lse_ref[...] = jnp.where(hm0, lses[0], lses[1])
        _comm_edge(comm, comm_refs, grid, first=False)

    kv_idx = (lambda c0: (lambda p, i: (0, c0))) if kv_shared else (lambda c0: (lambda p, i: (0, c0 + p)))
    in_specs = [pl.BlockSpec((T, LANES), lambda p, i: (i, q_col + p)),
                pl.BlockSpec((S, LANES), kv_idx(k_col)),
                pl.BlockSpec((S, LANES), kv_idx(v_col))]
    args = [q_arr, k_arr, v_arr]
    if use_bias:
        in_specs += [pl.BlockSpec((T, LANES), lambda p, i: (i, p)),
                     pl.BlockSpec((1, 2, S), lambda p, i: (p, 0, 0))]
        args += [cq_arr, ck_arr]
    if use_sink:
        in_specs.append(pl.BlockSpec(memory_space=pltpu.SMEM))
        args.append(sinks)
    out_spec = pl.BlockSpec((T, LANES), lambda p, i: (i, p))
    res = pl.pallas_call(
        kern, name=name, grid=grid,
        in_specs=in_specs + _comm_specs(comm, "in"),
        out_specs=[out_spec, out_spec] + _comm_specs(comm, "out"),
        out_shape=[jax.ShapeDtypeStruct((S, n_pairs * LANES), F32)] * 2 + (comm.out_shapes if comm else []),
        scratch_shapes=comm.sem_shapes if comm else [],
        compiler_params=_cparams("arbitrary", "arbitrary"),
    )(*args, *(comm.ins if comm else []))
    return (res[0], res[1], res[2:]) if comm else (res[0], res[1])


def _attn_bwd(q_arr, q_col, k_arr, k_col, v_arr, v_col, o_arr, do_arr, lse_arr, n_pairs, kv_shared, T,
              window, cq_arr, ck_arr, sinks, name, comm=None):
    S = q_arr.shape[0]
    nq = S // T
    use_bias = cq_arr is not None
    use_sink = sinks is not None
    back = 0 if window is None else -(-window // T)
    kv_w = LANES if kv_shared else n_pairs * LANES
    grid = (n_pairs,)
    n_in = 6 + 2 * use_bias + use_sink
    n_out = 3 + 2 * use_bias + use_sink

    def kern(*refs):
        refs, comm_refs = _own_refs(refs, comm, n_in, n_out, 0)
        _comm_edge(comm, comm_refs, grid, first=True)
        q_ref, k_ref, v_ref, o_ref, do_ref, lse_ref = refs[:6]
        pos = 6
        if use_bias:
            cq_ref, ck_ref = refs[pos:pos + 2]
            pos += 2
        if use_sink:
            sink_ref = refs[pos]
            pos += 1
        dq_ref, dk_ref, dv_ref = refs[pos:pos + 3]
        pos += 3
        if use_bias:
            dck_ref, dcq_ref = refs[pos:pos + 2]
            pos += 2
        if use_sink:
            dsink_ref = refs[pos]
        p_id = pl.program_id(0)
        rowcol = lax.broadcasted_iota(jnp.int32, (T, T), 0) - lax.broadcasted_iota(jnp.int32, (T, T), 1)

        def zero_kv():
            dk_ref[...] = jnp.zeros_like(dk_ref)
            dv_ref[...] = jnp.zeros_like(dv_ref)

        if kv_shared:
            pl.when(p_id == 0)(zero_kv)
        else:
            zero_kv()
        if use_bias:
            dck_ref[...] = jnp.zeros_like(dck_ref)
        if use_sink:
            dsink_ref[...] = jnp.zeros_like(dsink_ref)

        for half in (0, 1):
            hm = _half_mask((T, LANES), half)
            lane0 = half * HEAD_DIM

            def outer(i, carry):
                qrows = pl.ds(pl.multiple_of(i * T, T), T)
                qh = (jnp.where(hm, q_ref[qrows, :], 0).astype(F32) * QK_SCALE).astype(BF)
                do_f = jnp.where(hm, do_ref[qrows, :], 0.0)
                doh = do_f.astype(BF)
                delta = jnp.sum(do_f * o_ref[qrows, :], axis=1, keepdims=True)
                lse = lse_ref[qrows, lane0:lane0 + 1]
                if use_bias:
                    cq = cq_ref[qrows, lane0:lane0 + 1]
                lo = jnp.maximum(i - back, 0) if window is not None else 0

                def inner(j, carry_in, masked):
                    dq, rs = carry_in
                    krows = pl.ds(pl.multiple_of(j * T, T), T)
                    kj = k_ref[krows, :].astype(BF)
                    vj = v_ref[krows, :].astype(BF)
                    s = lax.dot_general(qh, kj, (((1,), (1,)), ((), ())), preferred_element_type=F32)
                    if use_bias:
                        s = s + cq - ck_ref[0, half:half + 1, krows]
                    if masked:
                        s = jnp.where(_valid(i, j, T, rowcol, window), s, NEG_INF)
                    p = jnp.exp(s - lse)
                    dp = lax.dot_general(doh, vj, (((1,), (1,)), ((), ())), preferred_element_type=F32)
                    ds = p * (dp - delta)
                    ds_b = ds.astype(BF)
                    dv_ref[krows, :] += lax.dot_general(p.astype(BF), doh, (((0,), (0,)), ((), ())),
                                                        preferred_element_type=F32)
                    dk_ref[krows, :] += lax.dot_general(ds_b, qh, (((0,), (0,)), ((), ())),
                                                        preferred_element_type=F32)
                    if use_bias:
                        dck_ref[0, half:half + 1, krows] += jnp.sum(ds, axis=0, keepdims=True)
                        rs = rs + jnp.sum(ds, axis=1, keepdims=True)
                    kh = jnp.where(hm, kj, 0)
                    return dq + jnp.dot(ds_b, kh, preferred_element_type=F32), rs

                init = (jnp.zeros((T, LANES), F32), jnp.zeros((T, 1), F32))
                if window is None:
                    init = lax.fori_loop(0, i, functools.partial(inner, masked=False), init)
                    dq, rs = inner(i, init, True)
                else:
                    dq, rs = lax.fori_loop(lo, i + 1, functools.partial(inner, masked=True), init)
                dq = dq * QK_SCALE
                if half == 0:
                    dq_ref[qrows, :] = dq
                else:
                    dq_ref[qrows, :] += dq
                if use_bias:
                    rs_b = jnp.broadcast_to(rs, (T, LANES))
                    dcq_ref[qrows, :] = rs_b if half == 0 else jnp.where(hm, rs_b, dcq_ref[qrows, :])
                if use_sink:
                    p_sink = jnp.exp(sink_ref[2 * p_id + half] - lse)
                    dsink_ref[0, half:half + 1, :] += jnp.broadcast_to(
                        -jnp.sum(p_sink * delta, axis=0, keepdims=True), (1, LANES))
                return carry

            lax.fori_loop(0, nq, outer, 0)
        _comm_edge(comm, comm_refs, grid, first=False)

    kv_idx = (lambda c0: (lambda p: (0, c0))) if kv_shared else (lambda c0: (lambda p: (0, c0 + p)))
    pair = lambda c0: pl.BlockSpec((S, LANES), lambda p: (0, c0 + p))
    in_specs = [pair(q_col), pl.BlockSpec((S, LANES), kv_idx(k_col)), pl.BlockSpec((S, LANES), kv_idx(v_col)),
                pair(0), pair(0), pair(0)]
    args = [q_arr, k_arr, v_arr, o_arr, do_arr, lse_arr]
    if use_bias:
        in_specs += [pair(0), pl.BlockSpec((1, 2, S), lambda p: (p, 0, 0))]
        args += [cq_arr, ck_arr]
    if use_sink:
        in_specs.append(pl.BlockSpec(memory_space=pltpu.SMEM))
        args.append(sinks)
    out_specs = [pair(0), pl.BlockSpec((S, LANES), kv_idx(0)), pl.BlockSpec((S, LANES), kv_idx(0))]
    out_shape = [jax.ShapeDtypeStruct((S, n_pairs * LANES), F32),
                 jax.ShapeDtypeStruct((S, kv_w), F32), jax.ShapeDtypeStruct((S, kv_w), F32)]
    if use_bias:
        out_specs += [pl.BlockSpec((1, 2, S), lambda p: (p, 0, 0)), pair(0)]
        out_shape += [jax.ShapeDtypeStruct((n_pairs, 2, S), F32), jax.ShapeDtypeStruct((S, n_pairs * LANES), F32)]
    if use_sink:
        out_specs.append(pl.BlockSpec((1, 8, LANES), lambda p: (p, 0, 0)))
        out_shape.append(jax.ShapeDtypeStruct((n_pairs, 8, LANES), F32))
    res = pl.pallas_call(
        kern, name=name, grid=grid,
        in_specs=in_specs + _comm_specs(comm, "in"),
        out_specs=out_specs + _comm_specs(comm, "out"),
        out_shape=out_shape + (comm.out_shapes if comm else []),
        scratch_shapes=comm.sem_shapes if comm else [],
        compiler_params=_cparams("arbitrary"),
    )(*args, *(comm.ins if comm else []))
    return (*res[:n_out], res[n_out:]) if comm else res


def _bias_lanes(shape, half, q_side_terms, k_side_terms):
    lane = lax.broadcasted_iota(jnp.int32, shape, 1)
    base = HEAD_DIM * (1 - half)
    n_q = len(q_side_terms) if q_side_terms is not None else 3
    n_k = len(k_side_terms) if k_side_terms is not None else 3
    out = jnp.zeros(shape, F32)
    for t in range(n_q):
        out = jnp.where(lane == base + t, q_side_terms[t].astype(F32) if q_side_terms is not None else 1.0, out)
    for t in range(n_k):
        out = jnp.where(lane == base + n_q + t,
                        k_side_terms[t].astype(F32) if k_side_terms is not None else 1.0, out)
    return out


def _head_column(block, head):
    lane = lax.broadcasted_iota(jnp.int32, block.shape, 1)
    return jnp.sum(jnp.where(lane == head, block, 0.0), axis=1, keepdims=True)


def _fox_prep_fwd(p_b, cum, T):
    S = p_b.shape[0]

    def kern(q_ref, k_ref, c_ref, qa_ref, ka_ref):
        p_id = pl.program_id(0)
        q, k, cum_blk = q_ref[...], k_ref[...], c_ref[...]
        for half in (0, 1):
            hm = _half_mask((T, LANES), half)
            c3 = _split3(_head_column(cum_blk, 2 * p_id + half))
            qa_ref[half] = jnp.where(hm, q.astype(F32) * QK_SCALE, _bias_lanes((T, LANES), half, c3, None)).astype(BF)
            ka_ref[half] = jnp.where(hm, k.astype(F32),
                                     _bias_lanes((T, LANES), half, None, [-t.astype(F32) for t in c3])).astype(BF)

    out_spec = pl.BlockSpec((None, 2, T, LANES), lambda p, i: (p, 0, i, 0))
    shape = jax.ShapeDtypeStruct((B_HEADS // 2, 2, S, LANES), BF)
    return pl.pallas_call(
        kern, name="fox_prep_fwd", grid=(B_HEADS // 2, S // T),
        in_specs=[pl.BlockSpec((T, LANES), lambda p, i: (i, OFF_QB // LANES + p)),
                  pl.BlockSpec((T, LANES), lambda p, i: (i, OFF_KB // LANES + p)),
                  pl.BlockSpec((T, LANES), lambda p, i: (i, 0))],
        out_specs=[out_spec, out_spec], out_shape=[shape, shape],
        compiler_params=_cparams("parallel", "parallel"),
    )(p_b, p_b, cum)


def _fox_fwd(q_aug, k_aug, p_b, T, comm=None):
    S = p_b.shape[0]
    nq = S // T
    n_pairs = B_HEADS // 2
    grid = (n_pairs, nq)

    def kern(*refs):
        (q_ref, k_ref, v_ref, o_ref, lse_ref), comm_refs = _own_refs(refs, comm, 3, 2, 0)
        _comm_edge(comm, comm_refs, grid, first=True)
        i = pl.program_id(1)
        rowcol = lax.broadcasted_iota(jnp.int32, (T, T), 0) - lax.broadcasted_iota(jnp.int32, (T, T), 1)
        qs = (q_ref[0], q_ref[1])

        def step(j, carry, masked):
            rows = pl.ds(pl.multiple_of(j * T, T), T)
            vj = v_ref[rows, :]
            new = []
            for half in (0, 1):
                m, l, acc = carry[half]
                s = lax.dot_general(qs[half], k_ref[half, rows, :], (((1,), (1,)), ((), ())),
                                    preferred_element_type=F32)
                if masked:
                    s = jnp.where(rowcol >= 0, s, NEG_INF)
                m_new = jnp.maximum(m, jnp.max(s, axis=1, keepdims=True))
                alpha = jnp.exp(m - m_new)
                p = jnp.exp(s - m_new)
                l_new = alpha * l + jnp.sum(p, axis=1, keepdims=True)
                acc_new = alpha * acc + jnp.dot(p.astype(BF), vj, preferred_element_type=F32)
                new.append((m_new, l_new, acc_new))
            return tuple(new)

        one = (jnp.full((T, 1), NEG_INF, F32), jnp.zeros((T, 1), F32), jnp.zeros((T, LANES), F32))
        carry = lax.fori_loop(0, i, functools.partial(step, masked=False), (one, one))
        (m0, l0, acc0), (m1, l1, acc1) = step(i, carry, True)
        hm0 = _half_mask((T, LANES), 0)
        o_ref[...] = jnp.where(hm0, acc0 / l0, acc1 / l1)
        lse_ref[...] = jnp.where(hm0, m0 + jnp.log(l0), m1 + jnp.log(l1))
        _comm_edge(comm, comm_refs, grid, first=False)

    out_spec = pl.BlockSpec((T, LANES), lambda p, i: (i, p))
    res = pl.pallas_call(
        kern, name="fox_fwd", grid=grid,
        in_specs=[pl.BlockSpec((None, 2, T, LANES), lambda p, i: (p, 0, i, 0)),
                  pl.BlockSpec((None, 2, S, LANES), lambda p, i: (p, 0, 0, 0)),
                  pl.BlockSpec((S, LANES), lambda p, i: (0, OFF_VB // LANES + p))] + _comm_specs(comm, "in"),
        out_specs=[out_spec, out_spec] + _comm_specs(comm, "out"),
        out_shape=[jax.ShapeDtypeStruct((S, n_pairs * LANES), F32)] * 2 + (comm.out_shapes if comm else []),
        scratch_shapes=comm.sem_shapes if comm else [],
        compiler_params=_cparams("arbitrary", "arbitrary"),
    )(q_aug, k_aug, p_b, *(comm.ins if comm else []))
    return res[0], res[1], res[2:]


def _fox_prep_bwd(q_aug, p_b, o, do, lse, T):
    S = p_b.shape[0]

    def kern(qa_ref, v_ref, o_ref, do_ref, lse_ref, qb_ref, dob_ref, vb_ref):
        v, o_blk, do_blk, lse_blk = v_ref[...], o_ref[...], do_ref[...], lse_ref[...]
        lane = lax.broadcasted_iota(jnp.int32, (T, LANES), 1)
        for half in (0, 1):
            hm = _half_mask((T, LANES), half)
            base = HEAD_DIM * (1 - half)
            qa = qa_ref[half].astype(F32)
            cq = jnp.sum(jnp.where((lane >= base) & (lane < base + 3), qa, 0.0), axis=1, keepdims=True)
            b3 = _split3(cq - lse_blk[:, HEAD_DIM * half:HEAD_DIM * half + 1])
            qb_ref[half] = jnp.where(hm, qa, _bias_lanes((T, LANES), half, b3, None)).astype(BF)
            do_f = jnp.where(hm, do_blk, 0.0)
            d3 = _split3(-jnp.sum(do_f * o_blk, axis=1, keepdims=True))
            dob_ref[half] = jnp.where(hm, do_f, _bias_lanes((T, LANES), half, d3, [])).astype(BF)
            vb_ref[half] = jnp.where(hm, v.astype(F32), _bias_lanes((T, LANES), half, None, [])).astype(BF)

    aug = pl.BlockSpec((None, 2, T, LANES), lambda p, i: (p, 0, i, 0))
    tile = pl.BlockSpec((T, LANES), lambda p, i: (i, p))
    shape = jax.ShapeDtypeStruct((B_HEADS // 2, 2, S, LANES), BF)
    return pl.pallas_call(
        kern, name="fox_prep_bwd", grid=(B_HEADS // 2, S // T),
        in_specs=[aug, pl.BlockSpec((T, LANES), lambda p, i: (i, OFF_VB // LANES + p)), tile, tile, tile],
        out_specs=[aug, aug, aug], out_shape=[shape, shape, shape],
        compiler_params=_cparams("parallel", "parallel"),
    )(q_aug, p_b, o, do, lse)


def _fox_bwd(qb_aug, k_aug, dob_aug, vb_aug, T, comm=None):
    n_pairs, _, S, _ = qb_aug.shape
    nq = S // T
    grid = (n_pairs,)

    def kern(*refs):
        own, comm_refs = _own_refs(refs, comm, 4, 5, 0)
        q_ref, k_ref, do_ref, v_ref, dq_ref, dk_ref, dv_ref, dck_ref, dcq_ref = own
        _comm_edge(comm, comm_refs, grid, first=True)
        p_id = pl.program_id(0)
        rowcol = lax.broadcasted_iota(jnp.int32, (T, T), 0) - lax.broadcasted_iota(jnp.int32, (T, T), 1)
        lane = lax.broadcasted_iota(jnp.int32, (T, LANES), 1)
        dk_ref[...] = jnp.zeros_like(dk_ref)
        dv_ref[...] = jnp.zeros_like(dv_ref)
        dck_ref[...] = jnp.zeros_like(dck_ref)

        @pl.when(p_id == 0)
        def _():
            dcq_ref[...] = jnp.zeros_like(dcq_ref)

        hms = (_half_mask((T, LANES), 0), _half_mask((T, LANES), 1))

        def outer(i, carry):
            qrows = pl.ds(pl.multiple_of(i * T, T), T)
            qa = (q_ref[0, qrows, :], q_ref[1, qrows, :])
            doa = (do_ref[0, qrows, :], do_ref[1, qrows, :])
            q_own = [jnp.where(hms[h], qa[h], 0) for h in (0, 1)]
            do_own = [jnp.where(hms[h], doa[h], 0) for h in (0, 1)]

            def inner(j, carry_in, masked):
                krows = pl.ds(pl.multiple_of(j * T, T), T)
                dv_add, dk_add, new = 0.0, 0.0, []
                for half in (0, 1):
                    dq, rs = carry_in[half]
                    ka = k_ref[half, krows, :]
                    s = lax.dot_general(qa[half], ka, (((1,), (1,)), ((), ())), preferred_element_type=F32)
                    if masked:
                        s = jnp.where(rowcol >= 0, s, NEG_INF)
                    p = jnp.exp(s)
                    ds = p * lax.dot_general(doa[half], v_ref[half, krows, :], (((1,), (1,)), ((), ())),
                                             preferred_element_type=F32)
                    ds_b = ds.astype(BF)
                    dv_add = dv_add + lax.dot_general(p.astype(BF), do_own[half], (((0,), (0,)), ((), ())),
                                                      preferred_element_type=F32)
                    dk_add = dk_add + lax.dot_general(ds_b, q_own[half], (((0,), (0,)), ((), ())),
                                                      preferred_element_type=F32)
                    dck_ref[half:half + 1, krows] += jnp.sum(ds, axis=0, keepdims=True)
                    new.append((dq + jnp.dot(ds_b, jnp.where(hms[half], ka, 0), preferred_element_type=F32),
                                rs + jnp.sum(ds, axis=1, keepdims=True)))
                dv_ref[krows, :] += dv_add
                dk_ref[krows, :] += dk_add
                return tuple(new)

            one = (jnp.zeros((T, LANES), F32), jnp.zeros((T, 1), F32))
            carry_in = lax.fori_loop(0, i, functools.partial(inner, masked=False), (one, one))
            (dq0, rs0), (dq1, rs1) = inner(i, carry_in, True)
            dq_ref[qrows, :] = (dq0 + dq1) * QK_SCALE
            dcq_ref[qrows, :] = jnp.where(lane == 2 * p_id, rs0, jnp.where(lane == 2 * p_id + 1, rs1,
                                                                             dcq_ref[qrows, :]))
            return carry

        lax.fori_loop(0, nq, outer, 0)
        _comm_edge(comm, comm_refs, grid, first=False)

    aug = pl.BlockSpec((None, 2, S, LANES), lambda p: (p, 0, 0, 0))
    pair = pl.BlockSpec((S, LANES), lambda p: (0, p))
    wide = jax.ShapeDtypeStruct((S, n_pairs * LANES), F32)
    res = pl.pallas_call(
        kern, name="fox_bwd", grid=grid,
        in_specs=[aug, aug, aug, aug] + _comm_specs(comm, "in"),
        out_specs=[pair, pair, pair, pl.BlockSpec((None, 2, S), lambda p: (p, 0, 0)),
                   pl.BlockSpec((S, LANES), lambda p: (0, 0))] + _comm_specs(comm, "out"),
        out_shape=[wide, wide, wide, jax.ShapeDtypeStruct((n_pairs, 2, S), F32),
                   jax.ShapeDtypeStruct((S, LANES), F32)] + (comm.out_shapes if comm else []),
        scratch_shapes=comm.sem_shapes if comm else [],
        compiler_params=_cparams("arbitrary"),
    )(qb_aug, k_aug, dob_aug, vb_aug, *(comm.ins if comm else []))
    return (*res[:5], res[5:])


SWA_TQ = 256


def _swa_window(i, tq):
    start = pl.multiple_of(jnp.maximum(i * tq - WINDOW, 0), LANES)
    return start, i * tq - start


def _swa_valid(offset, tq):
    rel = offset + lax.broadcasted_iota(jnp.int32, (tq, tq + WINDOW), 0) \
        - lax.broadcasted_iota(jnp.int32, (tq, tq + WINDOW), 1)
    return (rel >= 0) & (rel < WINDOW)


def _swa_fwd(qk, v_arr, v_col, sinks):
    S = qk.shape[0]
    tq = min(SWA_TQ, S - WINDOW)
    win = tq + WINDOW

    def kern(q_ref, k_ref, v_ref, sink_ref, o_ref, lse_ref):
        p_id, i = pl.program_id(0), pl.program_id(1)
        start, offset = _swa_window(i, tq)
        kw = k_ref[pl.ds(start, win), :]
        vw = v_ref[pl.ds(start, win), :].astype(BF)
        valid = _swa_valid(offset, tq)
        q = q_ref[...]
        outs, lses = [], []
        for half in (0, 1):
            hm = _half_mask((tq, LANES), half)
            qh = (jnp.where(hm, q, 0).astype(F32) * QK_SCALE).astype(BF)
            s = lax.dot_general(qh, kw, (((1,), (1,)), ((), ())), preferred_element_type=F32)
            s = jnp.where(valid, s, NEG_INF)
            sink = sink_ref[2 * p_id + half]
            m = jnp.maximum(jnp.max(s, axis=1, keepdims=True), sink)
            p = jnp.exp(s - m)
            denom = jnp.sum(p, axis=1, keepdims=True) + jnp.exp(sink - m)
            outs.append(jnp.dot(p.astype(BF), vw, preferred_element_type=F32) / denom)
            lses.append(m + jnp.log(denom))
        hm0 = _half_mask((tq, LANES), 0)
        o_ref[...] = jnp.where(hm0, outs[0], outs[1])
        lse_ref[...] = jnp.where(hm0, lses[0], lses[1])

    tile = pl.BlockSpec((tq, LANES), lambda p, i: (i, p))
    return pl.pallas_call(
        kern, name="swa_fwd", grid=(A_Q_HEADS // 2, S // tq),
        in_specs=[tile, pl.BlockSpec((S, LANES), lambda p, i: (0, A_Q_HEADS // 2)),
                  pl.BlockSpec((S, LANES), lambda p, i: (0, v_col)),
                  pl.BlockSpec(memory_space=pltpu.SMEM)],
        out_specs=[tile, tile],
        out_shape=[jax.ShapeDtypeStruct((S, A_Q_HEADS * HEAD_DIM), F32)] * 2,
        compiler_params=_cparams("parallel", "arbitrary"),
    )(qk, qk, v_arr, sinks)


def _swa_bwd(qk, v_arr, v_col, o_arr, do_arr, lse_arr, sinks, comm=None):
    S = qk.shape[0]
    tq = min(SWA_TQ, S - WINDOW)
    win = tq + WINDOW
    n_pairs = A_Q_HEADS // 2
    grid = (n_pairs, S // tq)

    def kern(*refs):
        own, comm_refs = _own_refs(refs, comm, 7, 4, 0)
        q_ref, k_ref, v_ref, o_ref, do_ref, lse_ref, sink_ref, dq_ref, dk_ref, dv_ref, dsink_ref = own
        _comm_edge(comm, comm_refs, grid, first=True)
        p_id, i = pl.program_id(0), pl.program_id(1)

        @pl.when((p_id == 0) & (i == 0))
        def _():
            dk_ref[...] = jnp.zeros_like(dk_ref)
            dv_ref[...] = jnp.zeros_like(dv_ref)

        @pl.when(i == 0)
        def _():
            dsink_ref[...] = jnp.zeros_like(dsink_ref)

        start, offset = _swa_window(i, tq)
        wrows = pl.ds(start, win)
        kw = k_ref[wrows, :]
        vw = v_ref[wrows, :].astype(BF)
        valid = _swa_valid(offset, tq)
        q, do, o, lse2 = q_ref[...], do_ref[...], o_ref[...], lse_ref[...]
        dq = jnp.zeros((tq, LANES), F32)
        dk = jnp.zeros((win, LANES), F32)
        dv = jnp.zeros((win, LANES), F32)
        for half in (0, 1):
            hm = _half_mask((tq, LANES), half)
            lane0 = half * HEAD_DIM
            qh = (jnp.where(hm, q, 0).astype(F32) * QK_SCALE).astype(BF)
            do_f = jnp.where(hm, do, 0.0)
            doh = do_f.astype(BF)
            delta = jnp.sum(do_f * o, axis=1, keepdims=True)
            lse = lse2[:, lane0:lane0 + 1]
            s = lax.dot_general(qh, kw, (((1,), (1,)), ((), ())), preferred_element_type=F32)
            p = jnp.exp(jnp.where(valid, s, NEG_INF) - lse)
            dp = lax.dot_general(doh, vw, (((1,), (1,)), ((), ())), preferred_element_type=F32)
            ds_b = (p * (dp - delta)).astype(BF)
            dv = dv + lax.dot_general(p.astype(BF), doh, (((0,), (0,)), ((), ())), preferred_element_type=F32)
            dk = dk + lax.dot_general(ds_b, qh, (((0,), (0,)), ((), ())), preferred_element_type=F32)
            kh = jnp.where(_half_mask((win, LANES), half), kw, 0)
            dq = dq + jnp.dot(ds_b, kh, preferred_element_type=F32)
            p_sink = jnp.exp(sink_ref[2 * p_id + half] - lse)
            dsink_ref[0, half:half + 1, :] += jnp.broadcast_to(
                -jnp.sum(p_sink * delta, axis=0, keepdims=True), (1, LANES))
        dq_ref[...] = dq * QK_SCALE
        dk_ref[wrows, :] += dk
        dv_ref[wrows, :] += dv
        _comm_edge(comm, comm_refs, grid, first=False)

    tile = pl.BlockSpec((tq, LANES), lambda p, i: (i, p))
    whole = lambda col: pl.BlockSpec((S, LANES), lambda p, i: (0, col))
    res = pl.pallas_call(
        kern, name="swa_bwd", grid=grid,
        in_specs=[tile, whole(n_pairs), whole(v_col), tile, tile, tile,
                  pl.BlockSpec(memory_space=pltpu.SMEM)] + _comm_specs(comm, "in"),
        out_specs=[tile, whole(0), whole(0),
                   pl.BlockSpec((1, 8, LANES), lambda p, i: (p, 0, 0))] + _comm_specs(comm, "out"),
        out_shape=[jax.ShapeDtypeStruct((S, A_Q_HEADS * HEAD_DIM), F32),
                   jax.ShapeDtypeStruct((S, LANES), F32), jax.ShapeDtypeStruct((S, LANES), F32),
                   jax.ShapeDtypeStruct((n_pairs, 8, LANES), F32)] + (comm.out_shapes if comm else []),
        scratch_shapes=comm.sem_shapes if comm else [],
        compiler_params=_cparams("arbitrary", "arbitrary"),
    )(qk, qk, v_arr, o_arr, do_arr, lse_arr, sinks, *(comm.ins if comm else []))
    return (*res[:4], res[4:])


ADAMW_BLOCK = 256 * 1024


def _adamw(w, g, m, v, name):
    R, C = w.shape
    tr, tc = _tile(R, max(8, ADAMW_BLOCK // C), 8), C

    def kern(w_ref, g_ref, m_ref, v_ref, d_ref, mo_ref, vo_ref):
        g_ = g_ref[...]
        m_new = ADAM_B1 * m_ref[...] + (1.0 - ADAM_B1) * g_
        v_new = ADAM_B2 * v_ref[...] + (1.0 - ADAM_B2) * (g_ * g_)
        m_hat = m_new / (1.0 - ADAM_B1 ** ADAM_STEP)
        v_hat = v_new / (1.0 - ADAM_B2 ** ADAM_STEP)
        d_ref[...] = -ADAM_LR * (m_hat / (jnp.sqrt(v_hat) + ADAM_EPS) + ADAM_WD * w_ref[...])
        mo_ref[...] = m_new
        vo_ref[...] = v_new

    spec = pl.BlockSpec((tr, tc), lambda i, j: (i, j))
    shape = jax.ShapeDtypeStruct((R, C), F32)
    return pl.pallas_call(
        kern, name=name, grid=(R // tr, C // tc),
        in_specs=[spec] * 4, out_specs=[spec] * 3, out_shape=[shape] * 3,
        compiler_params=_cparams("parallel", "parallel"),
    )(w, g, m, v)


def _index_operand(i):
    return jnp.reshape(i, (1,)).astype(jnp.int32)


def _add_pair(whole, got, ci, name):
    P, R, C = whole.shape
    half = R // 2
    tr = _tile(half, 256, 16)
    nb = half // tr

    def kern(ci_ref, a_ref, b_ref, o_ref, ob_ref):
        s = a_ref[...] + b_ref[...].astype(F32)
        o_ref[...] = s
        ob_ref[...] = s.astype(BF)

    spec = pl.BlockSpec((None, tr, C), lambda p, i, ci_ref: (p, i, 0))
    return pl.pallas_call(
        kern, name=name,
        grid_spec=pltpu.PrefetchScalarGridSpec(
            num_scalar_prefetch=1, grid=(P, nb),
            in_specs=[pl.BlockSpec((None, tr, C), lambda p, i, ci_ref: (p, ci_ref[0] * nb + i, 0)), spec],
            out_specs=[spec, spec]),
        out_shape=[jax.ShapeDtypeStruct((P, half, C), F32), jax.ShapeDtypeStruct((P, half, C), BF)],
        compiler_params=_cparams("parallel", "parallel"),
    )(_index_operand(ci), whole, got)


def _add_three(parts, recv, chip, name):
    _, R, C = parts.shape
    tr = _tile(R, 256, 16)

    def kern(chip_ref, o_ref, r0_ref, r1_ref, r2_ref, out_ref):
        s = ((o_ref[...] + r0_ref[...].astype(F32)) + r1_ref[...].astype(F32)) + r2_ref[...].astype(F32)
        out_ref[0] = s
        out_ref[1] = s

    slab = lambda k: pl.BlockSpec((None, tr, C), lambda i, chip_ref: (k, i, 0))
    return pl.pallas_call(
        kern, name=name,
        grid_spec=pltpu.PrefetchScalarGridSpec(
            num_scalar_prefetch=1, grid=(R // tr,),
            in_specs=[pl.BlockSpec((None, tr, C), lambda i, chip_ref: (chip_ref[0], i, 0)),
                      slab(0), slab(1), slab(2)],
            out_specs=pl.BlockSpec((2, tr, C), lambda i, chip_ref: (0, i, 0))),
        out_shape=jax.ShapeDtypeStruct((2, R, C), F32),
        compiler_params=_cparams("parallel"),
    )(_index_operand(chip), parts, recv, recv, recv)


SM_ADA, SM_G, SM_LOSS, SM_BF, SM_SINK, SM_LEN = 0, 6144, 10240, 11264, 11272, 12288


def _small_finalize(gathered):
    def kern(g_ref, tot_ref, loss_ref):
        tot = g_ref[0:1, :]
        for b in range(1, N_DEV):
            tot = tot + g_ref[b:b + 1, :]
        tot_ref[...] = tot
        sq = jnp.sum(tot[:, SM_LOSS:SM_LOSS + D_MODEL], axis=1, keepdims=True)
        loss_ref[...] = jnp.broadcast_to(sq * (0.5 / D_MODEL), (1, LANES))

    full = lambda shape: pl.BlockSpec(shape, lambda i: (0, 0))
    return pl.pallas_call(
        kern, name="small_finalize", grid=(1,),
        in_specs=[full((N_DEV, SM_LEN))],
        out_specs=[full((1, SM_LEN)), full((1, LANES))],
        out_shape=[jax.ShapeDtypeStruct((1, SM_LEN), F32), jax.ShapeDtypeStruct((1, LANES), F32)],
        compiler_params=_cparams("arbitrary"),
    )(gathered)


def _ada_dw(c_t, d_ada):
    N = d_ada.shape[1]
    tn = _tile(N, 512)

    def kern(c_ref, d_ref, o_ref):
        acc = c_ref[:, 0:1] * d_ref[0:1, :]
        for b in range(1, N_DEV):
            acc = acc + c_ref[:, b:b + 1] * d_ref[b:b + 1, :]
        o_ref[...] = acc

    return pl.pallas_call(
        kern, name="ada_dw", grid=(N // tn,),
        in_specs=[pl.BlockSpec((D_MODEL, N_DEV), lambda j: (0, 0)), pl.BlockSpec((N_DEV, tn), lambda j: (0, j))],
        out_specs=pl.BlockSpec((D_MODEL, tn), lambda j: (0, j)),
        out_shape=jax.ShapeDtypeStruct((D_MODEL, N), F32),
        compiler_params=_cparams("parallel"),
    )(c_t, d_ada)


def _here():
    return lax.axis_index("x"), lax.axis_index("y"), lax.axis_index("c")


def _other_chips(x, y):
    return [(1 - x, y), (x, 1 - y), (1 - x, 1 - y)]


_ANY = pl.BlockSpec(memory_space=pl.ANY)


class _Comm:
    def __init__(self, ins, out_shapes, sem_shapes, start, finish):
        self.ins, self.out_shapes, self.sem_shapes = list(ins), list(out_shapes), list(sem_shapes)
        self.start, self.finish = start, finish

    def split(self, refs, n_in, n_out, n_scratch):
        a = n_in + len(self.ins)
        b = a + n_out + len(self.out_shapes)
        own = list(refs[:n_in]) + list(refs[a:a + n_out]) + list(refs[b:b + n_scratch])
        mine = (refs[n_in:a], refs[a + n_out:b], refs[b + n_scratch:])
        return own, mine


def _run_comm(comm, name):
    n_in, n_out = len(comm.ins), len(comm.out_shapes)

    def body(*refs):
        parts = (refs[:n_in], refs[n_in:n_in + n_out], refs[n_in + n_out:])
        comm.start(*parts)
        comm.finish(*parts)

    return pl.pallas_call(
        body, name=name,
        in_specs=[_ANY] * n_in, out_specs=[_ANY] * n_out,
        out_shape=comm.out_shapes, scratch_shapes=comm.sem_shapes,
    )(*comm.ins)


def _gather_comm(blocks):
    L = len(blocks)

    def parts(ins, outs, sems):
        send_sems, recv_sems, local_sems = sems
        x, y, c = _here()
        me, sibling = (x, y, c), (x, y, 1 - c)
        chips = _other_chips(x, y)

        def slot(px, py, pc):
            return 4 * px + 2 * py + pc

        def copy(l, k, block, to, src=None):
            dst = outs[l].at[slot(*block)]
            return pltpu.make_async_remote_copy(
                src_ref=dst if src is None else src, dst_ref=dst,
                send_sem=send_sems.at[l, k], recv_sem=recv_sems.at[l, k],
                device_id=to, device_id_type=MESH)

        mine = [pltpu.make_async_copy(ins[l], outs[l].at[slot(*me)], local_sems.at[l]) for l in range(L)]
        first = []
        for l in range(L):
            first.append(copy(l, 0, me, sibling, src=ins[l]))
            for j, chip in enumerate(chips):
                first.append(copy(l, 1 + j, me, (*chip, c), src=ins[l]))
        return c, me, sibling, chips, copy, mine, first

    def start(ins, outs, sems):
        *_, mine, first = parts(ins, outs, sems)
        for cp in mine + first:
            cp.start()

    def finish(ins, outs, sems):
        c, me, sibling, chips, copy, mine, first = parts(ins, outs, sems)
        passed = []
        for j, chip in enumerate(chips):
            for l in range(L):
                copy(l, 1 + j, (*chip, c), me).wait_recv()
                fwd = copy(l, 4 + j, (*chip, c), sibling)
                fwd.start()
                passed.append(fwd)
        for l in range(L):
            copy(l, 0, sibling, me).wait_recv()
        for j, chip in enumerate(chips):
            for l in range(L):
                copy(l, 4 + j, (*chip, 1 - c), me).wait_recv()
        for cp in first + passed:
            cp.wait_send()
        for cp in mine:
            cp.wait()

    return _Comm(blocks, [jax.ShapeDtypeStruct((N_DEV,) + b.shape, b.dtype) for b in blocks],
                 [pltpu.SemaphoreType.DMA((L, 7)), pltpu.SemaphoreType.DMA((L, 7)), pltpu.SemaphoreType.DMA((L,))],
                 start, finish)


def _allgather8(blocks, name):
    return _run_comm(_gather_comm(blocks), name)


def _swap_comm(arrs):
    L = len(arrs)

    def copies(ins, outs, sems):
        send_sems, recv_sems = sems
        x, y, c = _here()
        cps = []
        for l in range(L):
            half = arrs[l].shape[1] // 2
            rows = pl.ds(pl.multiple_of((1 - c) * half, 16), half)
            cps.append(pltpu.make_async_remote_copy(
                src_ref=ins[l].at[:, rows, :], dst_ref=outs[l], send_sem=send_sems.at[l],
                recv_sem=recv_sems.at[l], device_id=(x, y, 1 - c), device_id_type=MESH))
        return cps

    def start(ins, outs, sems):
        for cp in copies(ins, outs, sems):
            cp.start()

    def finish(ins, outs, sems):
        for cp in copies(ins, outs, sems):
            cp.wait()

    return _Comm(arrs, [jax.ShapeDtypeStruct((a.shape[0], a.shape[1] // 2, a.shape[2]), a.dtype) for a in arrs],
                 [pltpu.SemaphoreType.DMA((L,)), pltpu.SemaphoreType.DMA((L,))], start, finish)


def _sibling_join(bufs, name):
    L = len(bufs)

    def body(*refs):
        outs = refs[L:2 * L]
        send_sems, recv_sems = refs[2 * L:]
        x, y, c = _here()
        for l in range(L):
            pltpu.make_async_remote_copy(src_ref=outs[l].at[c], dst_ref=outs[l].at[c], send_sem=send_sems.at[l],
                                         recv_sem=recv_sems.at[l], device_id=(x, y, 1 - c),
                                         device_id_type=MESH).start()
        for l in range(L):
            pltpu.make_async_remote_copy(src_ref=outs[l].at[c], dst_ref=outs[l].at[1 - c],
                                         send_sem=send_sems.at[l], recv_sem=recv_sems.at[l],
                                         device_id=(x, y, 1 - c), device_id_type=MESH).wait()

    return pl.pallas_call(
        body, name=name,
        in_specs=[_ANY] * L, out_specs=[_ANY] * L,
        out_shape=[jax.ShapeDtypeStruct(a.shape, a.dtype) for a in bufs],
        input_output_aliases={l: l for l in range(L)},
        scratch_shapes=[pltpu.SemaphoreType.DMA((L,)), pltpu.SemaphoreType.DMA((L,))],
    )(*bufs)


def _scatter_comm(arrs):
    L = len(arrs)

    def copies(ins, outs, sems):
        send_sems, recv_sems = sems
        x, y, c = _here()
        return [pltpu.make_async_remote_copy(
            src_ref=ins[l].at[2 * tx + ty], dst_ref=outs[l].at[j],
            send_sem=send_sems.at[l, j], recv_sem=recv_sems.at[l, j],
            device_id=(tx, ty, c), device_id_type=MESH)
            for l in range(L) for j, (tx, ty) in enumerate(_other_chips(x, y))]

    def start(ins, outs, sems):
        for cp in copies(ins, outs, sems):
            cp.start()

    def finish(ins, outs, sems):
        for cp in copies(ins, outs, sems):
            cp.wait()

    return _Comm(arrs, [jax.ShapeDtypeStruct((3,) + a.shape[1:], a.dtype) for a in arrs],
                 [pltpu.SemaphoreType.DMA((L, 3)), pltpu.SemaphoreType.DMA((L, 3))], start, finish)


_A_ORDER = np.array(A_HEAD_ORDER)
_A_INVERSE = np.argsort(_A_ORDER)


def _permute_in_weights(w_in):
    qa = w_in[:, 0:512].reshape(D_MODEL, A_Q_HEADS, HEAD_DIM)[:, _A_ORDER, :].reshape(D_MODEL, 512)
    f_pad = jnp.pad(w_in[:, 2304:2312], ((0, 0), (0, LANES - B_HEADS)))
    w_a = jnp.concatenate([qa, w_in[:, 512:640], f_pad], axis=1)
    return w_a, w_in[:, 640:2304], w_in[:, 2312:4360]


def _slab_segments():
    segs = [(h * HEAD_DIM, int(_A_INVERSE[h]) * HEAD_DIM, HEAD_DIM) for h in range(A_Q_HEADS)]
    segs += [(512, OFF_KA, 128), (640, W_A + OFF_VA, 128), (768, W_A + OFF_QB, 1536),
             (2304, OFF_F, B_HEADS), (2312, W_A + W_B, W_G)]
    return segs


def _shard_slabs(dw_perm):
    R = dw_perm.shape[0]
    tr = _tile(R, 128, 8)
    plan = []
    for k in range(N_CHIP):
        for b in range(W_SHARD_PAD // LANES):
            lo, hi = k * W_SHARD + b * LANES, min(k * W_SHARD + (b + 1) * LANES, (k + 1) * W_SHARD)
            parts = []
            for o0, s0, n in _slab_segments():
                a, z = max(lo, o0), min(hi, o0 + n)
                while a < z:
                    s = s0 + (a - o0)
                    run = min(z - a, LANES - s % LANES)
                    parts.append((s // LANES, ((a - lo) - s % LANES) % LANES, a - lo, run))
                    a += run
            plan.append((k, b, parts))

    def kern(x_ref, o32_ref, obf_ref):
        lane = lax.broadcasted_iota(jnp.int32, (tr, LANES), 1)
        for k, b, parts in plan:
            acc = jnp.zeros((tr, LANES), F32)
            for src, rot, first, run in parts:
                blk = x_ref[:, src * LANES:(src + 1) * LANES]
                if rot:
                    blk = pltpu.roll(blk, rot, 1)
                acc = jnp.where((lane >= first) & (lane < first + run), blk, acc)
            o32_ref[k, :, b * LANES:(b + 1) * LANES] = acc
            obf_ref[k, :, b * LANES:(b + 1) * LANES] = acc.astype(BF)

    out_spec = pl.BlockSpec((N_CHIP, tr, W_SHARD_PAD), lambda i: (0, i, 0))
    return tuple(pl.pallas_call(
        kern, name="shard_slabs", grid=(R // tr,),
        in_specs=[pl.BlockSpec((tr, W_PERM), lambda i: (i, 0))],
        out_specs=[out_spec, out_spec],
        out_shape=[jax.ShapeDtypeStruct((N_CHIP, R, W_SHARD_PAD), F32),
                   jax.ShapeDtypeStruct((N_CHIP, R, W_SHARD_PAD), BF)],
        compiler_params=_cparams("parallel"),
    )(dw_perm))


class _NoExchange:
    def __init__(self, rest):
        self.rest, self.grads = rest, {}

    def rest_weights_comm(self):
        return None

    def rest_weights(self, outs):
        return self.rest

    def swap_comm(self, pieces, tag):
        self.grads[tag] = [p32 for p32, _ in pieces]
        return None

    def swap_done(self, outs, tag):
        return None

    def reduce_done(self, outs, tag):
        pass


class _Exchange:
    def __init__(self, ci, chip, rest_shards):
        self.ci, self.chip, self.rest_shards = ci, chip, rest_shards
        self.pieces, self.part_f32, self.halves = {}, {}, {}

    def _my_half(self, a, axis=0, other=False):
        rows = a.shape[axis] // 2
        return lax.dynamic_slice_in_dim(a, ((1 - self.ci) if other else self.ci) * rows, rows, axis=axis)

    def rest_weights_comm(self):
        return _gather_comm([self._my_half(w).astype(BF) for w in self.rest_shards])

    def rest_weights(self, outs):
        w_ba, w_bb, w_out, w_fi, w_fo = outs
        return (_col_sharded(w_ba), _col_sharded(w_bb), _row_sharded(w_out), _col_sharded(w_fi),
                _row_sharded(w_fo))

    def swap_comm(self, pieces, tag):
        self.pieces[tag] = pieces
        return _swap_comm([pbf for _, pbf in pieces])

    def swap_done(self, got, tag):
        self.part_f32[tag], part_bf = [], []
        for l, ((p32, _), g_) in enumerate(zip(self.pieces[tag], got)):
            s32, sbf = _add_pair(p32, g_, self.ci, f"chip_sum_{tag}_{l}")
            self.part_f32[tag].append(s32)
            part_bf.append(sbf)
        return _scatter_comm(part_bf)

    def reduce_done(self, outs, tag):
        self.halves[tag] = [_add_three(p32, r, self.chip, f"shard_sum_{tag}_{l}")
                            for l, (p32, r) in enumerate(zip(self.part_f32[tag], outs))]


def _col_sharded(g):
    return jnp.transpose(g.reshape(N_CHIP, -1, g.shape[-1]), (1, 0, 2)).reshape(2 * g.shape[1], N_CHIP * g.shape[-1])


def _row_sharded(g):
    return g.reshape(N_DEV * g.shape[1], g.shape[-1])


def _rope_tables(pos):
    inv_freq = 1.0 / (ROPE_THETA ** (jnp.arange(0, HEAD_DIM, 2, dtype=F32) / HEAD_DIM))
    ang = pos.astype(F32)[:, None] * inv_freq
    cos, sin = jnp.cos(ang), jnp.sin(ang)
    return jnp.tile(cos, (1, 4)), jnp.tile(jnp.concatenate([-sin, sin], axis=1), (1, 2))


def _local_step(x, pos, ada, g1, g2, g3, g4, b_f, sinks, w_in, exch, target):
    S = x.shape[0]
    t_fox = _tile(S, 512, LANES) if S >= 1024 else S // 2
    t_fox_fwd = _tile(S, 1024, LANES) if S >= 2048 else S // 2
    shift_m, scale_m, gate_m, shift_f, scale_f, gate_f = [ada[i:i + 1] for i in range(N_ADA)]
    cos_t, sin_t = _rope_tables(pos)
    w_a, w_b, w_g = _permute_in_weights(w_in)
    w_perm = jnp.concatenate([w_a, w_b, w_g], axis=1)
    sinks_p = sinks.reshape(A_KV_HEADS, 4).T.reshape(A_Q_HEADS)
    b_f_pad = jnp.pad(b_f, (0, LANES - B_HEADS)).reshape(1, LANES)

    h1 = _pre_norm(x, g1, scale_m, shift_m, "pre_mix_norm")
    p_a = _mm(h1, w_a, "nn", F32, "proj_a")
    p_b = _mm(h1, w_b, "nn", BF, "proj_b")
    p_g = _mm(h1, w_g, "nn", BF, "proj_g")
    (qk_a,) = _rope([p_a], [640], cos_t, sin_t, "rope_fwd")
    o_a, lse_a = _swa_fwd(qk_a, p_b, 0, sinks_p)
    q_aug, k_aug = _fox_prep_fwd(p_b, _fox_gate_fwd(p_a, b_f_pad), t_fox)
    comm = exch.rest_weights_comm()
    o_b, lse_b, outs = _fox_fwd(q_aug, k_aug, p_b, t_fox_fwd, comm=comm)
    w_ba, w_bb, w_out, w_fi, w_fo = exch.rest_weights(outs)
    w_ba_p = w_ba.reshape(A_Q_HEADS, HEAD_DIM, D_MODEL)[_A_ORDER].reshape(512, D_MODEL)
    pa = _mm(o_a, w_ba_p, "nn", F32, "branch_a")
    pb = _mm(o_b, w_bb, "nn", F32, "branch_b")
    merged = _merge_fwd(p_g, pa, pb)
    y1 = _mm(merged, w_out, "nn", F32, "out_proj")
    x2, h2 = _post_pre(x, y1, g2, gate_m, g3, scale_f, shift_f)
    gu = _mm(h2, w_fi, "nn", BF, "ffn_in")
    act = _swiglu_fwd(gu)
    y2 = _mm(act, w_fo, "nn", F32, "ffn_out")
    d_out, d_y2, st_f = _final(x2, y2, g4, gate_f, target)

    d_act = _mm(d_y2, w_fo, "nt", F32, "ffn_out_dx")
    row_pieces = lambda pair: tuple(t.reshape(N_CHIP, t.shape[0] // N_CHIP, t.shape[1]) for t in pair)
    dw_fo = row_pieces(_mm(act, d_y2, "tn", F32, "ffn_out_dw", twin=True))
    d_gu = _swiglu_bwd(d_act, gu)
    d_h2 = _mm(d_gu, w_fi, "nt", F32, "ffn_in_dx")
    dw_fi = _mm(h2, d_gu, "tn", F32, "ffn_in_dw", col_pieces=N_CHIP, twin=True)
    d_x2, d_y1, st_m = _mid_bwd(d_h2, x2, d_out, y1, g3, scale_f, g2, gate_m)
    d_merged = _mm(d_y1, w_out, "nt", F32, "out_proj_dx")
    dw_out = row_pieces(_mm(merged, d_y1, "tn", F32, "out_proj_dw", twin=True))
    d_pa, d_pb, d_ga, d_gb = _merge_bwd(d_merged, p_g, pa, pb)
    d_oa = _mm(d_pa, w_ba_p, "nt", F32, "branch_a_dx")
    dw_ba_p = _mm(o_a, d_pa, "tn", F32, "branch_a_dw", col_pieces=N_CHIP, twin=True)
    d_ob = _mm(d_pb, w_bb, "nt", F32, "branch_b_dx")
    dw_bb = _mm(o_b, d_pb, "tn", F32, "branch_b_dw", col_pieces=N_CHIP, twin=True)
    head_rows = lambda t: t.reshape(N_CHIP, A_Q_HEADS, HEAD_DIM, -1)[:, _A_INVERSE].reshape(t.shape)
    dw_ba = tuple(head_rows(t) for t in dw_ba_p)
    comm = exch.swap_comm([dw_ba, dw_bb, dw_out, dw_fi, dw_fo], "early")
    dq_a, dk_a, dv_a, d_sink, outs = _swa_bwd(qk_a, p_b, 0, o_a, d_oa, lse_a, sinks_p, comm=comm)
    comm = exch.swap_done(outs, "early")
    qb_aug, dob_aug, vb_aug = _fox_prep_bwd(q_aug, p_b, o_b, d_ob, lse_b, t_fox)
    dq_b, dk_b, dv_b, d_ck, d_cq, outs = _fox_bwd(qb_aug, k_aug, dob_aug, vb_aug, t_fox, comm=comm)
    exch.reduce_done(outs, "early")
    d_qa, d_ka = _rope([dq_a, dk_a], [512, LANES], cos_t, -sin_t, "rope_bwd")
    d_ck_cols = jnp.pad(d_ck.reshape(B_HEADS, S).T, ((0, 0), (0, LANES - B_HEADS)))
    d_f, d_bf = _fox_gate_bwd(d_cq, d_ck_cols, p_a, b_f_pad)
    d_proj = jnp.concatenate([d_qa, d_ka, d_f, dv_a.astype(BF), dq_b.astype(BF), dk_b.astype(BF),
                              dv_b.astype(BF), d_ga, d_gb], axis=1)
    dw_perm = _mm(h1, d_proj, "tn", F32, "proj_dw")
    swap = exch.swap_comm([_shard_slabs(dw_perm)], "late")
    comm = exch.swap_done(_run_comm(swap, "grads_to_sibling_late") if swap else None, "late")
    res = _mm(d_proj, w_perm, "nt", F32, "proj_dx", comm=comm)
    d_h1 = res[0] if comm else res
    exch.reduce_done(res[1] if comm else None, "late")
    grad_x, st_p = _pre_bwd(d_h1, x, d_x2, g1, scale_m)

    d_sinks = d_sink[:, :2, 0].T.reshape(A_Q_HEADS)
    small = jnp.concatenate([
        st_p[0], st_p[1], st_m[3], st_m[0], st_m[1], st_f[0],
        st_p[2], st_m[4], st_m[2], st_f[1],
        st_f[2], d_bf[0, :B_HEADS], d_sinks,
        jnp.zeros((SM_LEN - SM_SINK - A_Q_HEADS,), F32)])
    return grad_x, small


def kernel(x, c, positions, w_ada, b_ada, g_pre_mix, g_post_mix, w_in, b_f, sinks, w_branch_a, w_branch_b, w_out, g_pre_ffn, g_post_ffn, w_ffn_in, w_ffn_out, loss_target, m_w_ada, m_b_ada, m_g_pre_mix, m_g_post_mix, m_w_in, m_b_f, m_sinks, m_w_branch_a, m_w_branch_b, m_w_out, m_g_pre_ffn, m_g_post_ffn, m_w_ffn_in, m_w_ffn_out, v_w_ada, v_b_ada, v_g_pre_mix, v_g_post_mix, v_w_in, v_b_f, v_sinks, v_w_branch_a, v_w_branch_b, v_w_out, v_g_pre_ffn, v_g_post_ffn, v_w_ffn_in, v_w_ffn_out):
    xi, yi, ci = _here()
    chip = 2 * xi + yi
    dev = 2 * chip + ci

    def my_half(a):
        rows = a.shape[0] // 2
        return lax.dynamic_slice_in_dim(a, ci * rows, rows, axis=0)

    c_g, w_in_g = _allgather8([c.reshape(8, LANES), my_half(w_in[0]).astype(BF)], "gather_w_in")
    c_all = c_g.reshape(N_DEV, D_MODEL)
    w_in_f = _col_sharded(w_in_g)
    exch = _Exchange(ci, chip, [w_branch_a[0], w_branch_b[0], w_out[0], w_ffn_in[0], w_ffn_out[0]])

    ada_cols = _mm(c_all, w_ada[0], "nn", F32, "ada_fwd")
    (ada_g,) = _allgather8([ada_cols], "gather_ada")
    ada_mine = lax.dynamic_index_in_dim(ada_g.reshape(N_CHIP, 2, N_DEV, -1)[:, 0], dev, axis=1, keepdims=False)
    ada = (ada_mine.reshape(-1) + b_ada[0]).reshape(N_ADA, D_MODEL)

    grad_x, small = _local_step(
        x[0], positions[0], ada, g_pre_mix, g_post_mix, g_pre_ffn, g_post_ffn, b_f[0], sinks[0],
        w_in_f, exch, loss_target[0])

    (small_g,) = _allgather8([small.reshape(8, SM_LEN // 8)], "gather_small")
    small_all = small_g.reshape(N_DEV, SM_LEN)
    small_tot, loss_row = _small_finalize(small_all)
    loss = loss_row[0, 0]
    d_ada_cols = lax.dynamic_slice_in_dim(small_all[:, :N_ADA * D_MODEL], chip * (N_ADA * D_MODEL // N_CHIP),
                                          N_ADA * D_MODEL // N_CHIP, axis=1)
    g_w_ada = _ada_dw(c_all.T, d_ada_cols)

    joined = _sibling_join(exch.halves["late"] + exch.halves["early"], "grads_join")
    g_w_in, g_w_ba, g_w_bb, g_w_out, g_w_fi, g_w_fo = [j.reshape(2 * j.shape[1], j.shape[2]) for j in joined]

    def small_vec(b_ada_, g1_, g2_, g3_, g4_, b_f_, sinks_):
        return jnp.concatenate([b_ada_[0], g1_[0], g2_[0], g3_[0], g4_[0], jnp.zeros((D_MODEL,), F32),
                                b_f_[0], sinks_[0], jnp.zeros((SM_LEN - SM_SINK - A_Q_HEADS,), F32)]
                               ).reshape(8, SM_LEN // 8)

    sw = small_vec(b_ada, g_pre_mix, g_post_mix, g_pre_ffn, g_post_ffn, b_f, sinks)
    sm = small_vec(m_b_ada, m_g_pre_mix, m_g_post_mix, m_g_pre_ffn, m_g_post_ffn, m_b_f, m_sinks)
    sv = small_vec(v_b_ada, v_g_pre_mix, v_g_post_mix, v_g_pre_ffn, v_g_post_ffn, v_b_f, v_sinks)
    s_upd = [u.reshape(SM_LEN) for u in _adamw(sw, small_tot.reshape(8, SM_LEN // 8), sm, sv, "adamw_small")]
    s_grad = small_tot.reshape(SM_LEN)

    def unpack(vec):
        row = lambda a, n: vec[a:a + n].reshape(1, n)
        return dict(b_ada=row(SM_ADA, N_ADA * D_MODEL), g_pre_mix=row(SM_G, D_MODEL),
                    g_post_mix=row(SM_G + D_MODEL, D_MODEL), g_pre_ffn=row(SM_G + 2 * D_MODEL, D_MODEL),
                    g_post_ffn=row(SM_G + 3 * D_MODEL, D_MODEL), b_f=row(SM_BF, B_HEADS),
                    sinks=row(SM_SINK, A_Q_HEADS))

    big = dict(
        w_ada=(w_ada, g_w_ada, m_w_ada, v_w_ada),
        w_branch_a=(w_branch_a, g_w_ba, m_w_branch_a, v_w_branch_a),
        w_branch_b=(w_branch_b, g_w_bb, m_w_branch_b, v_w_branch_b),
        w_out=(w_out, g_w_out, m_w_out, v_w_out), w_ffn_in=(w_ffn_in, g_w_fi, m_w_ffn_in, v_w_ffn_in),
        w_ffn_out=(w_ffn_out, g_w_fo, m_w_ffn_out, v_w_ffn_out))
    grads, deltas, new_m, new_v = unpack(s_grad), unpack(s_upd[0]), unpack(s_upd[1]), unpack(s_upd[2])
    for n, (w_, g_, m_, v_) in big.items():
        d_, nm_, nv_ = _adamw(w_[0], g_, m_[0], v_[0], "adamw_" + n)
        grads[n], deltas[n], new_m[n], new_v[n] = g_[None], d_[None], nm_[None], nv_[None]
    pad_cols = lambda a: jnp.pad(a, ((0, 0), (0, W_SHARD_PAD - W_SHARD)))
    upd = _adamw(pad_cols(w_in[0]), g_w_in, pad_cols(m_w_in[0]), pad_cols(v_w_in[0]), "adamw_w_in")
    grads["w_in"], deltas["w_in"], new_m["w_in"], new_v["w_in"] = [t[None, :, :W_SHARD] for t in (g_w_in, *upd)]

    names = ["w_ada", "b_ada", "g_pre_mix", "g_post_mix", "w_in", "b_f", "sinks", "w_branch_a", "w_branch_b",
             "w_out", "g_pre_ffn", "g_post_ffn", "w_ffn_in", "w_ffn_out"]
    return (loss, grad_x[None], *[grads[n] for n in names], *[deltas[n] for n in names],
            *[new_m[n] for n in names], *[new_v[n] for n in names])
```

```python
import functools
import math

import numpy as np
import jax
import jax.numpy as jnp
from jax import lax
from jax.experimental import pallas as pl
from jax.experimental.pallas import tpu as pltpu

F32 = jnp.float32
BF = jnp.bfloat16

D_MODEL = 1024
HEAD_DIM = 64
LANES = 128
WINDOW = 128
A_Q_HEADS = 8
A_KV_HEADS = 2
B_HEADS = 8
D_FF = 2816
ROPE_THETA = 10000.0
RMS_EPS = 1e-6
N_ADA = 6
N_DEV = 8
N_CHIP = 4

ADAM_LR = 0.001
ADAM_B1 = 0.9
ADAM_B2 = 0.999
ADAM_EPS = 1e-08
ADAM_WD = 0.01
ADAM_STEP = 10

VMEM_LIMIT = 48 * 1024 * 1024
MESH = pl.DeviceIdType.MESH

A_HEAD_ORDER = (0, 4, 1, 5, 2, 6, 3, 7)

OFF_QA, OFF_KA, OFF_F = 0, 512, 640
W_A = 768
OFF_VA, OFF_QB, OFF_KB, OFF_VB = 0, 128, 640, 1152
W_B = 1664
W_G = 2048
W_PERM = W_A + W_B + W_G
W_SHARD = 1090
W_SHARD_PAD = 1152


def _tile(n, cap, mult=LANES):
    if n <= cap:
        return n
    t = (cap // mult) * mult
    while t >= mult:
        if n % t == 0:
            return t
        t -= mult
    raise ValueError(f"no tile for {n}")


MXU_WIDTH = 256
MM_OPERAND_BYTES = 28 * 1024 * 1024


def _mm_tiles(M, N, K, a_bytes, b_bytes, tm_cap, tn_cap):
    tm = _tile(M, tm_cap)
    try:
        tn = _tile(N, tn_cap, MXU_WIDTH)
    except ValueError:
        tn = _tile(N, tn_cap)
    fits = lambda tk: 2 * tk * (tm * a_bytes + tn * b_bytes) <= MM_OPERAND_BYTES
    tk = K if fits(K) else next(t for t in range(K // LANES * LANES, 0, -LANES) if K % t == 0 and fits(t))
    return tm, tn, tk


def _cparams(*sem):
    return pltpu.CompilerParams(dimension_semantics=sem, vmem_limit_bytes=VMEM_LIMIT)


def _own_refs(refs, comm, n_in, n_out, n_scratch):
    if comm is None:
        return list(refs), None
    return comm.split(refs, n_in, n_out, n_scratch)


def _comm_specs(comm, side):
    if comm is None:
        return []
    return [pl.BlockSpec(memory_space=pl.ANY)] * len(comm.ins if side == "in" else comm.out_shapes)


def _comm_edge(comm, comm_refs, grid, first):
    if comm is None:
        return
    at_edge = None
    for axis, n in enumerate(grid):
        here = pl.program_id(axis) == (0 if first else n - 1)
        at_edge = here if at_edge is None else at_edge & here
    pl.when(at_edge)(lambda: (comm.start if first else comm.finish)(*comm_refs))


def _mm(a, b, mode, out_dtype, name, tm_cap=512, tn_cap=2816, comm=None, col_pieces=1, twin=False):
    if mode == "nn":
        (M, K), (K2, N) = a.shape, b.shape
        dims = (((1,), (0,)), ((), ()))
    elif mode == "nt":
        (M, K), (N, K2) = a.shape, b.shape
        dims = (((1,), (1,)), ((), ()))
    else:
        (K, M), (K2, N) = a.shape, b.shape
        dims = (((0,), (0,)), ((), ()))
    assert K == K2, (a.shape, b.shape, mode)
    tm, tn, tk = _mm_tiles(M, N // col_pieces, K, a.dtype.itemsize, b.dtype.itemsize, tm_cap, tn_cap)
    nk = K // tk
    n_out = 2 if twin else 1
    n_scratch = 1 if nk > 1 else 0
    if mode == "nn":
        a_spec = pl.BlockSpec((tm, tk), lambda i, j, k: (i, k))
        b_spec = pl.BlockSpec((tk, tn), lambda i, j, k: (k, j))
    elif mode == "nt":
        a_spec = pl.BlockSpec((tm, tk), lambda i, j, k: (i, k))
        b_spec = pl.BlockSpec((tn, tk), lambda i, j, k: (j, k))
    else:
        a_spec = pl.BlockSpec((tk, tm), lambda i, j, k: (k, i))
        b_spec = pl.BlockSpec((tk, tn), lambda i, j, k: (k, j))

    grid = (M // tm, N // tn, nk)

    def kern(*refs):
        own, comm_refs = _own_refs(refs, comm, 2, n_out, n_scratch)
        a_ref, b_ref, o_refs = own[0], own[1], own[2:2 + n_out]
        k = pl.program_id(2)
        _comm_edge(comm, comm_refs, grid, first=True)
        part = lax.dot_general(a_ref[...].astype(BF), b_ref[...].astype(BF), dims,
                               preferred_element_type=F32)
        if nk == 1:
            for o_ref in o_refs:
                o_ref[...] = part.astype(o_ref.dtype)
        else:
            acc_ref = own[2 + n_out]

            @pl.when(k == 0)
            def _():
                acc_ref[...] = part

            @pl.when(k > 0)
            def _():
                acc_ref[...] += part

            @pl.when(k == nk - 1)
            def _():
                for o_ref in o_refs:
                    o_ref[...] = acc_ref[...].astype(o_ref.dtype)

        _comm_edge(comm, comm_refs, grid, first=False)

    if col_pieces > 1:
        per = N // col_pieces // tn
        out_spec = pl.BlockSpec((None, tm, tn), lambda i, j, k: (j // per, i, j % per))
        shape = (col_pieces, M, N // col_pieces)
    else:
        out_spec = pl.BlockSpec((tm, tn), lambda i, j, k: (i, j))
        shape = (M, N)
    dtypes = [out_dtype, BF] if twin else [out_dtype]
    res = pl.pallas_call(
        kern, name=name, grid=grid,
        in_specs=[a_spec, b_spec] + _comm_specs(comm, "in"),
        out_specs=[out_spec] * n_out + _comm_specs(comm, "out"),
        out_shape=[jax.ShapeDtypeStruct(shape, d) for d in dtypes] + (comm.out_shapes if comm else []),
        scratch_shapes=[pltpu.VMEM((tm, tn), F32)] * n_scratch + (comm.sem_shapes if comm else []),
        compiler_params=_cparams("parallel", "parallel", "arbitrary"),
    )(a, b, *(comm.ins if comm else []))
    own = res[0] if n_out == 1 else tuple(res[:n_out])
    return (own, res[n_out:]) if comm else own


ROWS = 256


def _row_spec(tm, width=D_MODEL, col=0):
    return pl.BlockSpec((tm, width), lambda i: (i, col))


def _vec_spec(width=D_MODEL):
    return pl.BlockSpec((1, width), lambda i: (0, 0))


def _rms(x):
    return lax.rsqrt(jnp.mean(x * x, axis=-1, keepdims=True) + RMS_EPS)


def _colsum(x):
    return jnp.sum(x, axis=0, keepdims=True)


def _norm_bwd(d_xn, xn, r):
    return r * (d_xn - xn * jnp.mean(d_xn * xn, axis=-1, keepdims=True))


def _pre_norm(x, g, scale, shift, name, comm=None):
    S = x.shape[0]
    tm = _tile(S, ROWS, 8)
    grid = (S // tm,)

    def kern(*refs):
        (x_ref, g_ref, sc_ref, sh_ref, h_ref), comm_refs = _own_refs(refs, comm, 4, 1, 0)
        _comm_edge(comm, comm_refs, grid, first=True)
        xf = x_ref[...]
        y = xf * _rms(xf) * g_ref[...]
        h_ref[...] = (y * (1.0 + sc_ref[...]) + sh_ref[...]).astype(BF)
        _comm_edge(comm, comm_refs, grid, first=False)

    res = pl.pallas_call(
        kern, name=name, grid=grid,
        in_specs=[_row_spec(tm), _vec_spec(), _vec_spec(), _vec_spec()] + _comm_specs(comm, "in"),
        out_specs=[_row_spec(tm)] + _comm_specs(comm, "out"),
        out_shape=[jax.ShapeDtypeStruct((S, D_MODEL), BF)] + (comm.out_shapes if comm else []),
        scratch_shapes=comm.sem_shapes if comm else [],
        compiler_params=_cparams("arbitrary"),
    )(x, g, scale, shift, *(comm.ins if comm else []))
    return res[0], res[1:]


def _post_pre(x, y1, g2, gate_m, g3, scale_f, shift_f):
    S = x.shape[0]
    tm = _tile(S, ROWS, 8)

    def kern(x_ref, y_ref, g2_ref, gm_ref, g3_ref, sc_ref, sh_ref, x2_ref, h2_ref):
        y = y_ref[...]
        n2 = y * _rms(y) * g2_ref[...]
        x2 = x_ref[...] + gm_ref[...] * n2
        x2_ref[...] = x2
        n3 = x2 * _rms(x2) * g3_ref[...]
        h2_ref[...] = (n3 * (1.0 + sc_ref[...]) + sh_ref[...]).astype(BF)

    return pl.pallas_call(
        kern, name="post_mix_pre_ffn", grid=(S // tm,),
        in_specs=[_row_spec(tm), _row_spec(tm)] + [_vec_spec()] * 5,
        out_specs=[_row_spec(tm), _row_spec(tm)],
        out_shape=[jax.ShapeDtypeStruct((S, D_MODEL), F32), jax.ShapeDtypeStruct((S, D_MODEL), BF)],
        compiler_params=_cparams("parallel"),
    )(x, y1, g2, gate_m, g3, scale_f, shift_f)


def _stats_spec():
    return pl.BlockSpec((8, D_MODEL), lambda i: (0, 0))


def _final(x2, y2, g4, gate_f, target):
    S = x2.shape[0]
    tm = _tile(S, ROWS, 8)

    def kern(x2_ref, y_ref, g4_ref, gf_ref, t_ref, dout_ref, dy_ref, st_ref):
        @pl.when(pl.program_id(0) == 0)
        def _():
            st_ref[...] = jnp.zeros_like(st_ref)

        y = y_ref[...]
        r = _rms(y)
        yn = y * r
        n4 = yn * g4_ref[...]
        diff = x2_ref[...] + gf_ref[...] * n4 - t_ref[...]
        d_out = diff / D_MODEL
        dout_ref[...] = d_out
        dn = d_out * gf_ref[...]
        dy_ref[...] = _norm_bwd(dn * g4_ref[...], yn, r).astype(BF)
        st_ref[0:1, :] += _colsum(d_out * n4)
        st_ref[1:2, :] += _colsum(dn * yn)
        st_ref[2:3, :] += _colsum(diff * diff)

    return pl.pallas_call(
        kern, name="final_loss", grid=(S // tm,),
        in_specs=[_row_spec(tm), _row_spec(tm), _vec_spec(), _vec_spec(), _row_spec(tm)],
        out_specs=[_row_spec(tm), _row_spec(tm), _stats_spec()],
        out_shape=[jax.ShapeDtypeStruct((S, D_MODEL), F32), jax.ShapeDtypeStruct((S, D_MODEL), BF),
                   jax.ShapeDtypeStruct((8, D_MODEL), F32)],
        compiler_params=_cparams("arbitrary"),
    )(x2, y2, g4, gate_f, target)


def _mid_bwd(d_h2, x2, d_out, y1, g3, scale_f, g2, gate_m):
    S = x2.shape[0]
    tm = _tile(S, ROWS, 8)

    def kern(dh_ref, x2_ref, dout_ref, y_ref, g3_ref, sc_ref, g2_ref, gm_ref, dx2_ref, dy_ref, st_ref):
        @pl.when(pl.program_id(0) == 0)
        def _():
            st_ref[...] = jnp.zeros_like(st_ref)

        dh = dh_ref[...]
        x2 = x2_ref[...]
        r3 = _rms(x2)
        xn = x2 * r3
        one_sc = 1.0 + sc_ref[...]
        d_x2 = dout_ref[...] + _norm_bwd(dh * one_sc * g3_ref[...], xn, r3)
        dx2_ref[...] = d_x2
        y = y_ref[...]
        r2 = _rms(y)
        yn = y * r2
        dn = d_x2 * gm_ref[...]
        dy_ref[...] = _norm_bwd(dn * g2_ref[...], yn, r2).astype(BF)
        st_ref[0:1, :] += _colsum(dh)
        st_ref[1:2, :] += _colsum(dh * (xn * g3_ref[...]))
        st_ref[2:3, :] += _colsum(dh * one_sc * xn)
        st_ref[3:4, :] += _colsum(d_x2 * (yn * g2_ref[...]))
        st_ref[4:5, :] += _colsum(dn * yn)

    return pl.pallas_call(
        kern, name="mid_bwd", grid=(S // tm,),
        in_specs=[_row_spec(tm)] * 4 + [_vec_spec()] * 4,
        out_specs=[_row_spec(tm), _row_spec(tm), _stats_spec()],
        out_shape=[jax.ShapeDtypeStruct((S, D_MODEL), F32), jax.ShapeDtypeStruct((S, D_MODEL), BF),
                   jax.ShapeDtypeStruct((8, D_MODEL), F32)],
        compiler_params=_cparams("arbitrary"),
    )(d_h2, x2, d_out, y1, g3, scale_f, g2, gate_m)


def _pre_bwd(d_h1, x, d_x2, g1, scale_m):
    S = x.shape[0]
    tm = _tile(S, ROWS, 8)

    def kern(dh_ref, x_ref, dx2_ref, g_ref, sc_ref, gx_ref, st_ref):
        @pl.when(pl.program_id(0) == 0)
        def _():
            st_ref[...] = jnp.zeros_like(st_ref)

        dh = dh_ref[...]
        xf = x_ref[...]
        r = _rms(xf)
        xn = xf * r
        one_sc = 1.0 + sc_ref[...]
        gx_ref[...] = dx2_ref[...] + _norm_bwd(dh * one_sc * g_ref[...], xn, r)
        st_ref[0:1, :] += _colsum(dh)
        st_ref[1:2, :] += _colsum(dh * (xn * g_ref[...]))
        st_ref[2:3, :] += _colsum(dh * one_sc * xn)

    return pl.pallas_call(
        kern, name="pre_mix_bwd", grid=(S // tm,),
        in_specs=[_row_spec(tm)] * 3 + [_vec_spec()] * 2,
        out_specs=[_row_spec(tm), _stats_spec()],
        out_shape=[jax.ShapeDtypeStruct((S, D_MODEL), F32), jax.ShapeDtypeStruct((8, D_MODEL), F32)],
        compiler_params=_cparams("arbitrary"),
    )(d_h1, x, d_x2, g1, scale_m)


def _rope(xs, widths, cos_t, sin_t, name):
    S = xs[0].shape[0]
    tm = _tile(S, 512, 8)
    n = len(xs)

    def kern(*refs):
        cos = refs[n][...]
        sin = refs[n + 1][...]
        first = (lax.broadcasted_iota(jnp.int32, cos.shape, 1) % HEAD_DIM) < HEAD_DIM // 2
        for x_ref, o_ref, w in zip(refs[:n], refs[n + 2:], widths):
            for c0 in range(0, w, LANES):
                v = x_ref[:, c0:c0 + LANES]
                partner = jnp.where(first, pltpu.roll(v, LANES - HEAD_DIM // 2, 1),
                                    pltpu.roll(v, HEAD_DIM // 2, 1))
                o_ref[:, c0:c0 + LANES] = (v * cos + partner * sin).astype(BF)

    return pl.pallas_call(
        kern, name=name, grid=(S // tm,),
        in_specs=[_row_spec(tm, w) for w in widths] + [_row_spec(tm, LANES)] * 2,
        out_specs=[_row_spec(tm, w) for w in widths],
        out_shape=[jax.ShapeDtypeStruct((S, w), BF) for w in widths],
        compiler_params=_cparams("parallel"),
    )(*xs, cos_t, sin_t)


def _merge_fwd(pg, pa, pb):
    S = pa.shape[0]
    tm = _tile(S, ROWS, 8)

    def kern(ga_ref, gb_ref, pa_ref, pb_ref, o_ref):
        ga = jax.nn.sigmoid(ga_ref[...].astype(F32))
        gb = jax.nn.sigmoid(gb_ref[...].astype(F32))
        o_ref[...] = (ga * pa_ref[...] + gb * pb_ref[...]).astype(BF)

    return pl.pallas_call(
        kern, name="merge_fwd", grid=(S // tm,),
        in_specs=[_row_spec(tm, col=0), _row_spec(tm, col=1), _row_spec(tm), _row_spec(tm)],
        out_specs=_row_spec(tm),
        out_shape=jax.ShapeDtypeStruct((S, D_MODEL), BF),
        compiler_params=_cparams("parallel"),
    )(pg, pg, pa, pb)


def _merge_bwd(d_merged, pg, pa, pb):
    S = pa.shape[0]
    tm = _tile(S, ROWS, 8)

    def kern(dm_ref, ga_ref, gb_ref, pa_ref, pb_ref, dpa_ref, dpb_ref, dga_ref, dgb_ref):
        dm = dm_ref[...]
        ga = jax.nn.sigmoid(ga_ref[...].astype(F32))
        gb = jax.nn.sigmoid(gb_ref[...].astype(F32))
        dpa_ref[...] = (dm * ga).astype(BF)
        dpb_ref[...] = (dm * gb).astype(BF)
        dga_ref[...] = (dm * pa_ref[...] * ga * (1.0 - ga)).astype(BF)
        dgb_ref[...] = (dm * pb_ref[...] * gb * (1.0 - gb)).astype(BF)

    bf_out = jax.ShapeDtypeStruct((S, D_MODEL), BF)
    return pl.pallas_call(
        kern, name="merge_bwd", grid=(S // tm,),
        in_specs=[_row_spec(tm), _row_spec(tm, col=0), _row_spec(tm, col=1), _row_spec(tm), _row_spec(tm)],
        out_specs=[_row_spec(tm)] * 4,
        out_shape=[bf_out] * 4,
        compiler_params=_cparams("parallel"),
    )(d_merged, pg, pg, pa, pb)


def _swiglu_fwd(gu):
    S = gu.shape[0]
    tm = _tile(S, ROWS, 8)
    tc = _tile(D_FF, 1408)
    nc = D_FF // tc

    def kern(g_ref, u_ref, o_ref):
        g = g_ref[...].astype(F32)
        o_ref[...] = (g * jax.nn.sigmoid(g) * u_ref[...].astype(F32)).astype(BF)

    return pl.pallas_call(
        kern, name="swiglu_fwd", grid=(S // tm, nc),
        in_specs=[pl.BlockSpec((tm, tc), lambda i, j: (i, j)),
                  pl.BlockSpec((tm, tc), lambda i, j: (i, j + nc))],
        out_specs=pl.BlockSpec((tm, tc), lambda i, j: (i, j)),
        out_shape=jax.ShapeDtypeStruct((S, D_FF), BF),
        compiler_params=_cparams("parallel", "parallel"),
    )(gu, gu)


def _swiglu_bwd(d_act, gu):
    S = gu.shape[0]
    tm = _tile(S, 128, 8)

    def kern(da_ref, g_ref, u_ref, o_ref):
        g = g_ref[...].astype(F32)
        u = u_ref[...].astype(F32)
        da = da_ref[...]
        sg = jax.nn.sigmoid(g)
        o_ref[:, :D_FF] = (da * u * (sg * (1.0 + g * (1.0 - sg)))).astype(BF)
        o_ref[:, D_FF:] = (da * (g * sg)).astype(BF)

    return pl.pallas_call(
        kern, name="swiglu_bwd", grid=(S // tm,),
        in_specs=[_row_spec(tm, D_FF), _row_spec(tm, D_FF, 0), _row_spec(tm, D_FF, 1)],
        out_specs=_row_spec(tm, 2 * D_FF),
        out_shape=jax.ShapeDtypeStruct((S, 2 * D_FF), BF),
        compiler_params=_cparams("parallel"),
    )(d_act, gu, gu)


def _split3(x):
    hi = x.astype(BF)
    r1 = x - hi.astype(F32)
    mid = r1.astype(BF)
    lo = (r1 - mid.astype(F32)).astype(BF)
    return hi, mid, lo


def _tri_dot(tri, x):
    return sum(jnp.dot(tri, part, preferred_element_type=F32) for part in _split3(x))


def _log_sigmoid(z):
    return jnp.minimum(z, 0.0) - jnp.log(1.0 + jnp.exp(-jnp.abs(z)))


def _fox_gate_fwd(pa, b_f_pad):
    S = pa.shape[0]
    T = _tile(S, 512, 8)
    f_col = OFF_F // LANES

    def kern(z_ref, b_ref, cum_ref, carry_ref):
        @pl.when(pl.program_id(0) == 0)
        def _():
            carry_ref[...] = jnp.zeros_like(carry_ref)

        log_f = _log_sigmoid(z_ref[...] + b_ref[...])
        row = lax.broadcasted_iota(jnp.int32, (T, T), 0)
        col = lax.broadcasted_iota(jnp.int32, (T, T), 1)
        tri = (col <= row).astype(BF)
        cum = _tri_dot(tri, log_f) + carry_ref[...]
        cum_ref[...] = cum
        carry_ref[...] = cum[T - 1:T, :]

    return pl.pallas_call(
        kern, name="fox_gate_fwd", grid=(S // T,),
        in_specs=[_row_spec(T, LANES, f_col), _vec_spec(LANES)],
        out_specs=_row_spec(T, LANES),
        out_shape=jax.ShapeDtypeStruct((S, LANES), F32),
        scratch_shapes=[pltpu.VMEM((1, LANES), F32)],
        compiler_params=_cparams("arbitrary"),
    )(pa, b_f_pad)


def _fox_gate_bwd(rowsum_ds, colsum_ds, pa, b_f_pad):
    S = pa.shape[0]
    T = _tile(S, 512, 8)
    nb = S // T
    f_col = OFF_F // LANES

    def kern(dr_ref, dc_ref, z_ref, b_ref, df_ref, dbf_ref, carry_ref):
        @pl.when(pl.program_id(0) == 0)
        def _():
            carry_ref[...] = jnp.zeros_like(carry_ref)
            dbf_ref[...] = jnp.zeros_like(dbf_ref)

        row = lax.broadcasted_iota(jnp.int32, (T, T), 0)
        col = lax.broadcasted_iota(jnp.int32, (T, T), 1)
        tri = (col >= row).astype(BF)
        rev = _tri_dot(tri, dr_ref[...] - dc_ref[...]) + carry_ref[...]
        carry_ref[...] = rev[0:1, :]
        z = z_ref[...] + b_ref[...]
        lane = lax.broadcasted_iota(jnp.int32, (T, LANES), 1)
        d_z = jnp.where(lane < B_HEADS, rev * jax.nn.sigmoid(-z), 0.0)
        df_ref[...] = d_z.astype(BF)
        dbf_ref[0:1, :] += _colsum(d_z)

    return pl.pallas_call(
        kern, name="fox_gate_bwd", grid=(nb,),
        in_specs=[pl.BlockSpec((T, LANES), lambda i: (nb - 1 - i, 0)),
                  pl.BlockSpec((T, LANES), lambda i: (nb - 1 - i, 0)),
                  pl.BlockSpec((T, LANES), lambda i: (nb - 1 - i, f_col)),
                  _vec_spec(LANES)],
        out_specs=[pl.BlockSpec((T, LANES), lambda i: (nb - 1 - i, 0)),
                   pl.BlockSpec((8, LANES), lambda i: (0, 0))],
        out_shape=[jax.ShapeDtypeStruct((S, LANES), BF), jax.ShapeDtypeStruct((8, LANES), F32)],
        scratch_shapes=[pltpu.VMEM((1, LANES), F32)],
        compiler_params=_cparams("arbitrary"),
    )(rowsum_ds, colsum_ds, pa, b_f_pad)


NEG_INF = float("-inf")
QK_SCALE = 1.0 / math.sqrt(HEAD_DIM)


def _half_mask(shape, half):
    lane = lax.broadcasted_iota(jnp.int32, shape, 1)
    return (lane < HEAD_DIM) if half == 0 else (lane >= HEAD_DIM)


def _valid(i, j, T, rowcol, window):
    rel = (i - j) * T + rowcol
    ok = rel >= 0
    if window is not None:
        ok = ok & (rel < window)
    return ok


def _attn_fwd(q_arr, q_col, k_arr, k_col, v_arr, v_col, n_pairs, kv_shared, T, window,
              cq_arr, ck_arr, sinks, name, comm=None):
    S = q_arr.shape[0]
    nq = S // T
    use_bias = cq_arr is not None
    use_sink = sinks is not None
    back = 0 if window is None else -(-window // T)
    grid = (n_pairs, nq)
    n_in = 3 + 2 * use_bias + use_sink

    def kern(*refs):
        refs, comm_refs = _own_refs(refs, comm, n_in, 2, 0)
        _comm_edge(comm, comm_refs, grid, first=True)
        q_ref, k_ref, v_ref = refs[:3]
        pos = 3
        if use_bias:
            cq_ref, ck_ref = refs[pos:pos + 2]
            pos += 2
        if use_sink:
            sink_ref = refs[pos]
            pos += 1
        o_ref, lse_ref = refs[pos:pos + 2]
        p_id = pl.program_id(0)
        i = pl.program_id(1)
        q = q_ref[...]
        rowcol = lax.broadcasted_iota(jnp.int32, (T, T), 0) - lax.broadcasted_iota(jnp.int32, (T, T), 1)
        lo = jnp.maximum(i - back, 0) if window is not None else 0
        outs, lses = [], []
        for half in (0, 1):
            hm = _half_mask((T, LANES), half)
            qh = (jnp.where(hm, q, 0).astype(F32) * QK_SCALE).astype(BF)
            if use_bias:
                cq = cq_ref[:, half * HEAD_DIM:half * HEAD_DIM + 1]
            if use_sink:
                m0 = jnp.full((T, 1), sink_ref[2 * p_id + half], F32)
                l0 = jnp.ones((T, 1), F32)
            else:
                m0 = jnp.full((T, 1), NEG_INF, F32)
                l0 = jnp.zeros((T, 1), F32)

            def step(j, carry, masked):
                m, l, acc = carry
                rows = pl.ds(pl.multiple_of(j * T, T), T)
                kj = k_ref[rows, :].astype(BF)
                vj = v_ref[rows, :].astype(BF)
                s = lax.dot_general(qh, kj, (((1,), (1,)), ((), ())), preferred_element_type=F32)
                if use_bias:
                    s = s + cq - ck_ref[0, half:half + 1, rows]
                if masked:
                    s = jnp.where(_valid(i, j, T, rowcol, window), s, NEG_INF)
                m_new = jnp.maximum(m, jnp.max(s, axis=1, keepdims=True))
                alpha = jnp.exp(m - m_new)
                p = jnp.exp(s - m_new)
                l_new = alpha * l + jnp.sum(p, axis=1, keepdims=True)
                acc_new = alpha * acc + jnp.dot(p.astype(BF), vj, preferred_element_type=F32)
                return m_new, l_new, acc_new

            init = (m0, l0, jnp.zeros((T, LANES), F32))
            if window is None:
                init = lax.fori_loop(0, i, functools.partial(step, masked=False), init)
                m, l, acc = step(i, init, True)
            else:
                m, l, acc = lax.fori_loop(lo, i + 1, functools.partial(step, masked=True), init)
            outs.append(acc / l)
            lses.append(m + jnp.log(l))
        hm0 = _half_mask((T, LANES), 0)
        o_ref[...] = jnp.where(hm0, outs[0], outs[1])
        lse_ref[...] = jnp.where(hm0, lses[0], lses[1])
        _comm_edge(comm, comm_refs, grid, first=False)

    kv_idx = (lambda c0: (lambda p, i: (0, c0))) if kv_shared else (lambda c0: (lambda p, i: (0, c0 + p)))
    in_specs = [pl.BlockSpec((T, LANES), lambda p, i: (i, q_col + p)),
                pl.BlockSpec((S, LANES), kv_idx(k_col)),
                pl.BlockSpec((S, LANES), kv_idx(v_col))]
    args = [q_arr, k_arr, v_arr]
    if use_bias:
        in_specs += [pl.BlockSpec((T, LANES), lambda p, i: (i, p)),
                     pl.BlockSpec((1, 2, S), lambda p, i: (p, 0, 0))]
        args += [cq_arr, ck_arr]
    if use_sink:
        in_specs.append(pl.BlockSpec(memory_space=pltpu.SMEM))
        args.append(sinks)
    out_spec = pl.BlockSpec((T, LANES), lambda p, i: (i, p))
    res = pl.pallas_call(
        kern, name=name, grid=grid,
        in_specs=in_specs + _comm_specs(comm, "in"),
        out_specs=[out_spec, out_spec] + _comm_specs(comm, "out"),
        out_shape=[jax.ShapeDtypeStruct((S, n_pairs * LANES), F32)] * 2 + (comm.out_shapes if comm else []),
        scratch_shapes=comm.sem_shapes if comm else [],
        compiler_params=_cparams("arbitrary", "arbitrary"),
    )(*args, *(comm.ins if comm else []))
    return (res[0], res[1], res[2:]) if comm else (res[0], res[1])


def _attn_bwd(q_arr, q_col, k_arr, k_col, v_arr, v_col, o_arr, do_arr, lse_arr, n_pairs, kv_shared, T,
              window, cq_arr, ck_arr, sinks, name, comm=None):
    S = q_arr.shape[0]
    nq = S // T
    use_bias = cq_arr is not None
    use_sink = sinks is not None
    back = 0 if window is None else -(-window // T)
    kv_w = LANES if kv_shared else n_pairs * LANES
    grid = (n_pairs,)
    n_in = 6 + 2 * use_bias + use_sink
    n_out = 3 + 2 * use_bias + use_sink

    def kern(*refs):
        refs, comm_refs = _own_refs(refs, comm, n_in, n_out, 0)
        _comm_edge(comm, comm_refs, grid, first=True)
        q_ref, k_ref, v_ref, o_ref, do_ref, lse_ref = refs[:6]
        pos = 6
        if use_bias:
            cq_ref, ck_ref = refs[pos:pos + 2]
            pos += 2
        if use_sink:
            sink_ref = refs[pos]
            pos += 1
        dq_ref, dk_ref, dv_ref = refs[pos:pos + 3]
        pos += 3
        if use_bias:
            dck_ref, dcq_ref = refs[pos:pos + 2]
            pos += 2
        if use_sink:
            dsink_ref = refs[pos]
        p_id = pl.program_id(0)
        rowcol = lax.broadcasted_iota(jnp.int32, (T, T), 0) - lax.broadcasted_iota(jnp.int32, (T, T), 1)

        def zero_kv():
            dk_ref[...] = jnp.zeros_like(dk_ref)
            dv_ref[...] = jnp.zeros_like(dv_ref)

        if kv_shared:
            pl.when(p_id == 0)(zero_kv)
        else:
            zero_kv()
        if use_bias:
            dck_ref[...] = jnp.zeros_like(dck_ref)
        if use_sink:
            dsink_ref[...] = jnp.zeros_like(dsink_ref)

        for half in (0, 1):
            hm = _half_mask((T, LANES), half)
            lane0 = half * HEAD_DIM

            def outer(i, carry):
                qrows = pl.ds(pl.multiple_of(i * T, T), T)
                qh = (jnp.where(hm, q_ref[qrows, :], 0).astype(F32) * QK_SCALE).astype(BF)
                do_f = jnp.where(hm, do_ref[qrows, :], 0.0)
                doh = do_f.astype(BF)
                delta = jnp.sum(do_f * o_ref[qrows, :], axis=1, keepdims=True)
                lse = lse_ref[qrows, lane0:lane0 + 1]
                if use_bias:
                    cq = cq_ref[qrows, lane0:lane0 + 1]
                lo = jnp.maximum(i - back, 0) if window is not None else 0

                def inner(j, carry_in, masked):
                    dq, rs = carry_in
                    krows = pl.ds(pl.multiple_of(j * T, T), T)
                    kj = k_ref[krows, :].astype(BF)
                    vj = v_ref[krows, :].astype(BF)
                    s = lax.dot_general(qh, kj, (((1,), (1,)), ((), ())), preferred_element_type=F32)
                    if use_bias:
                        s = s + cq - ck_ref[0, half:half + 1, krows]
                    if masked:
                        s = jnp.where(_valid(i, j, T, rowcol, window), s, NEG_INF)
                    p = jnp.exp(s - lse)
                    dp = lax.dot_general(doh, vj, (((1,), (1,)), ((), ())), preferred_element_type=F32)
                    ds = p * (dp - delta)
                    ds_b = ds.astype(BF)
                    dv_ref[krows, :] += lax.dot_general(p.astype(BF), doh, (((0,), (0,)), ((), ())),
                                                        preferred_element_type=F32)
                    dk_ref[krows, :] += lax.dot_general(ds_b, qh, (((0,), (0,)), ((), ())),
                                                        preferred_element_type=F32)
                    if use_bias:
                        dck_ref[0, half:half + 1, krows] += jnp.sum(ds, axis=0, keepdims=True)
                        rs = rs + jnp.sum(ds, axis=1, keepdims=True)
                    kh = jnp.where(hm, kj, 0)
                    return dq + jnp.dot(ds_b, kh, preferred_element_type=F32), rs

                init = (jnp.zeros((T, LANES), F32), jnp.zeros((T, 1), F32))
                if window is None:
                    init = lax.fori_loop(0, i, functools.partial(inner, masked=False), init)
                    dq, rs = inner(i, init, True)
                else:
                    dq, rs = lax.fori_loop(lo, i + 1, functools.partial(inner, masked=True), init)
                dq = dq * QK_SCALE
                if half == 0:
                    dq_ref[qrows, :] = dq
                else:
                    dq_ref[qrows, :] += dq
                if use_bias:
                    rs_b = jnp.broadcast_to(rs, (T, LANES))
                    dcq_ref[qrows, :] = rs_b if half == 0 else jnp.where(hm, rs_b, dcq_ref[qrows, :])
                if use_sink:
                    p_sink = jnp.exp(sink_ref[2 * p_id + half] - lse)
                    dsink_ref[0, half:half + 1, :] += jnp.broadcast_to(
                        -jnp.sum(p_sink * delta, axis=0, keepdims=True), (1, LANES))
                return carry

            lax.fori_loop(0, nq, outer, 0)
        _comm_edge(comm, comm_refs, grid, first=False)

    kv_idx = (lambda c0: (lambda p: (0, c0))) if kv_shared else (lambda c0: (lambda p: (0, c0 + p)))
    pair = lambda c0: pl.BlockSpec((S, LANES), lambda p: (0, c0 + p))
    in_specs = [pair(q_col), pl.BlockSpec((S, LANES), kv_idx(k_col)), pl.BlockSpec((S, LANES), kv_idx(v_col)),
                pair(0), pair(0), pair(0)]
    args = [q_arr, k_arr, v_arr, o_arr, do_arr, lse_arr]
    if use_bias:
        in_specs += [pair(0), pl.BlockSpec((1, 2, S), lambda p: (p, 0, 0))]
        args += [cq_arr, ck_arr]
    if use_sink:
        in_specs.append(pl.BlockSpec(memory_space=pltpu.SMEM))
        args.append(sinks)
    out_specs = [pair(0), pl.BlockSpec((S, LANES), kv_idx(0)), pl.BlockSpec((S, LANES), kv_idx(0))]
    out_shape = [jax.ShapeDtypeStruct((S, n_pairs * LANES), F32),
                 jax.ShapeDtypeStruct((S, kv_w), F32), jax.ShapeDtypeStruct((S, kv_w), F32)]
    if use_bias:
        out_specs += [pl.BlockSpec((1, 2, S), lambda p: (p, 0, 0)), pair(0)]
        out_shape += [jax.ShapeDtypeStruct((n_pairs, 2, S), F32), jax.ShapeDtypeStruct((S, n_pairs * LANES), F32)]
    if use_sink:
        out_specs.append(pl.BlockSpec((1, 8, LANES), lambda p: (p, 0, 0)))
        out_shape.append(jax.ShapeDtypeStruct((n_pairs, 8, LANES), F32))
    res = pl.pallas_call(
        kern, name=name, grid=grid,
        in_specs=in_specs + _comm_specs(comm, "in"),
        out_specs=out_specs + _comm_specs(comm, "out"),
        out_shape=out_shape + (comm.out_shapes if comm else []),
        scratch_shapes=comm.sem_shapes if comm else [],
        compiler_params=_cparams("arbitrary"),
    )(*args, *(comm.ins if comm else []))
    return (*res[:n_out], res[n_out:]) if comm else res


def _bias_lanes(shape, half, q_side_terms, k_side_terms):
    lane = lax.broadcasted_iota(jnp.int32, shape, 1)
    base = HEAD_DIM * (1 - half)
    n_q = len(q_side_terms) if q_side_terms is not None else 3
    n_k = len(k_side_terms) if k_side_terms is not None else 3
    out = jnp.zeros(shape, F32)
    for t in range(n_q):
        out = jnp.where(lane == base + t, q_side_terms[t].astype(F32) if q_side_terms is not None else 1.0, out)
    for t in range(n_k):
        out = jnp.where(lane == base + n_q + t,
                        k_side_terms[t].astype(F32) if k_side_terms is not None else 1.0, out)
    return out


def _head_column(block, head):
    lane = lax.broadcasted_iota(jnp.int32, block.shape, 1)
    return jnp.sum(jnp.where(lane == head, block, 0.0), axis=1, keepdims=True)


def _fox_prep_fwd(p_b, cum, T):
    S = p_b.shape[0]

    def kern(q_ref, k_ref, c_ref, qa_ref, ka_ref):
        p_id = pl.program_id(0)
        q, k, cum_blk = q_ref[...], k_ref[...], c_ref[...]
        for half in (0, 1):
            hm = _half_mask((T, LANES), half)
            c3 = _split3(_head_column(cum_blk, 2 * p_id + half))
            qa_ref[half] = jnp.where(hm, q.astype(F32) * QK_SCALE, _bias_lanes((T, LANES), half, c3, None)).astype(BF)
            ka_ref[half] = jnp.where(hm, k.astype(F32),
                                     _bias_lanes((T, LANES), half, None, [-t.astype(F32) for t in c3])).astype(BF)

    out_spec = pl.BlockSpec((None, 2, T, LANES), lambda p, i: (p, 0, i, 0))
    shape = jax.ShapeDtypeStruct((B_HEADS // 2, 2, S, LANES), BF)
    return pl.pallas_call(
        kern, name="fox_prep_fwd", grid=(B_HEADS // 2, S // T),
        in_specs=[pl.BlockSpec((T, LANES), lambda p, i: (i, OFF_QB // LANES + p)),
                  pl.BlockSpec((T, LANES), lambda p, i: (i, OFF_KB // LANES + p)),
                  pl.BlockSpec((T, LANES), lambda p, i: (i, 0))],
        out_specs=[out_spec, out_spec], out_shape=[shape, shape],
        compiler_params=_cparams("parallel", "parallel"),
    )(p_b, p_b, cum)


def _fox_fwd(q_aug, k_aug, p_b, T, comm=None):
    S = p_b.shape[0]
    nq = S // T
    n_pairs = B_HEADS // 2
    grid = (n_pairs, nq)

    def kern(*refs):
        (q_ref, k_ref, v_ref, o_ref, lse_ref), comm_refs = _own_refs(refs, comm, 3, 2, 0)
        _comm_edge(comm, comm_refs, grid, first=True)
        i = pl.program_id(1)
        rowcol = lax.broadcasted_iota(jnp.int32, (T, T), 0) - lax.broadcasted_iota(jnp.int32, (T, T), 1)
        qs = (q_ref[0], q_ref[1])

        def step(j, carry, masked):
            rows = pl.ds(pl.multiple_of(j * T, T), T)
            vj = v_ref[rows, :]
            new = []
            for half in (0, 1):
                m, l, acc = carry[half]
                s = lax.dot_general(qs[half], k_ref[half, rows, :], (((1,), (1,)), ((), ())),
                                    preferred_element_type=F32)
                if masked:
                    s = jnp.where(rowcol >= 0, s, NEG_INF)
                m_new = jnp.maximum(m, jnp.max(s, axis=1, keepdims=True))
                alpha = jnp.exp(m - m_new)
                p = jnp.exp(s - m_new)
                l_new = alpha * l + jnp.sum(p, axis=1, keepdims=True)
                acc_new = alpha * acc + jnp.dot(p.astype(BF), vj, preferred_element_type=F32)
                new.append((m_new, l_new, acc_new))
            return tuple(new)

        one = (jnp.full((T, 1), NEG_INF, F32), jnp.zeros((T, 1), F32), jnp.zeros((T, LANES), F32))
        carry = lax.fori_loop(0, i, functools.partial(step, masked=False), (one, one))
        (m0, l0, acc0), (m1, l1, acc1) = step(i, carry, True)
        hm0 = _half_mask((T, LANES), 0)
        o_ref[...] = jnp.where(hm0, acc0 / l0, acc1 / l1)
        lse_ref[...] = jnp.where(hm0, m0 + jnp.log(l0), m1 + jnp.log(l1))
        _comm_edge(comm, comm_refs, grid, first=False)

    out_spec = pl.BlockSpec((T, LANES), lambda p, i: (i, p))
    res = pl.pallas_call(
        kern, name="fox_fwd", grid=grid,
        in_specs=[pl.BlockSpec((None, 2, T, LANES), lambda p, i: (p, 0, i, 0)),
                  pl.BlockSpec((None, 2, S, LANES), lambda p, i: (p, 0, 0, 0)),
                  pl.BlockSpec((S, LANES), lambda p, i: (0, OFF_VB // LANES + p))] + _comm_specs(comm, "in"),
        out_specs=[out_spec, out_spec] + _comm_specs(comm, "out"),
        out_shape=[jax.ShapeDtypeStruct((S, n_pairs * LANES), F32)] * 2 + (comm.out_shapes if comm else []),
        scratch_shapes=comm.sem_shapes if comm else [],
        compiler_params=_cparams("arbitrary", "arbitrary"),
    )(q_aug, k_aug, p_b, *(comm.ins if comm else []))
    return res[0], res[1], res[2:]


def _fox_prep_bwd(q_aug, p_b, o, do, lse, T):
    S = p_b.shape[0]

    def kern(qa_ref, v_ref, o_ref, do_ref, lse_ref, qb_ref, dob_ref, vb_ref):
        v, o_blk, do_blk, lse_blk = v_ref[...], o_ref[...], do_ref[...], lse_ref[...]
        lane = lax.broadcasted_iota(jnp.int32, (T, LANES), 1)
        for half in (0, 1):
            hm = _half_mask((T, LANES), half)
            base = HEAD_DIM * (1 - half)
            qa = qa_ref[half].astype(F32)
            cq = jnp.sum(jnp.where((lane >= base) & (lane < base + 3), qa, 0.0), axis=1, keepdims=True)
            b3 = _split3(cq - lse_blk[:, HEAD_DIM * half:HEAD_DIM * half + 1])
            qb_ref[half] = jnp.where(hm, qa, _bias_lanes((T, LANES), half, b3, None)).astype(BF)
            do_f = jnp.where(hm, do_blk, 0.0)
            d3 = _split3(-jnp.sum(do_f * o_blk, axis=1, keepdims=True))
            dob_ref[half] = jnp.where(hm, do_f, _bias_lanes((T, LANES), half, d3, [])).astype(BF)
            vb_ref[half] = jnp.where(hm, v.astype(F32), _bias_lanes((T, LANES), half, None, [])).astype(BF)

    aug = pl.BlockSpec((None, 2, T, LANES), lambda p, i: (p, 0, i, 0))
    tile = pl.BlockSpec((T, LANES), lambda p, i: (i, p))
    shape = jax.ShapeDtypeStruct((B_HEADS // 2, 2, S, LANES), BF)
    return pl.pallas_call(
        kern, name="fox_prep_bwd", grid=(B_HEADS // 2, S // T),
        in_specs=[aug, pl.BlockSpec((T, LANES), lambda p, i: (i, OFF_VB // LANES + p)), tile, tile, tile],
        out_specs=[aug, aug, aug], out_shape=[shape, shape, shape],
        compiler_params=_cparams("parallel", "parallel"),
    )(q_aug, p_b, o, do, lse)


def _fox_bwd(qb_aug, k_aug, dob_aug, vb_aug, T, comm=None):
    n_pairs, _, S, _ = qb_aug.shape
    nq = S // T
    grid = (n_pairs,)

    def kern(*refs):
        own, comm_refs = _own_refs(refs, comm, 4, 5, 0)
        q_ref, k_ref, do_ref, v_ref, dq_ref, dk_ref, dv_ref, dck_ref, dcq_ref = own
        _comm_edge(comm, comm_refs, grid, first=True)
        p_id = pl.program_id(0)
        rowcol = lax.broadcasted_iota(jnp.int32, (T, T), 0) - lax.broadcasted_iota(jnp.int32, (T, T), 1)
        lane = lax.broadcasted_iota(jnp.int32, (T, LANES), 1)
        dk_ref[...] = jnp.zeros_like(dk_ref)
        dv_ref[...] = jnp.zeros_like(dv_ref)
        dck_ref[...] = jnp.zeros_like(dck_ref)

        @pl.when(p_id == 0)
        def _():
            dcq_ref[...] = jnp.zeros_like(dcq_ref)

        hms = (_half_mask((T, LANES), 0), _half_mask((T, LANES), 1))

        def outer(i, carry):
            qrows = pl.ds(pl.multiple_of(i * T, T), T)
            qa = (q_ref[0, qrows, :], q_ref[1, qrows, :])
            doa = (do_ref[0, qrows, :], do_ref[1, qrows, :])
            q_own = [jnp.where(hms[h], qa[h], 0) for h in (0, 1)]
            do_own = [jnp.where(hms[h], doa[h], 0) for h in (0, 1)]

            def inner(j, carry_in, masked):
                krows = pl.ds(pl.multiple_of(j * T, T), T)
                dv_add, dk_add, new = 0.0, 0.0, []
                for half in (0, 1):
                    dq, rs = carry_in[half]
                    ka = k_ref[half, krows, :]
                    s = lax.dot_general(qa[half], ka, (((1,), (1,)), ((), ())), preferred_element_type=F32)
                    if masked:
                        s = jnp.where(rowcol >= 0, s, NEG_INF)
                    p = jnp.exp(s)
                    ds = p * lax.dot_general(doa[half], v_ref[half, krows, :], (((1,), (1,)), ((), ())),
                                             preferred_element_type=F32)
                    ds_b = ds.astype(BF)
                    dv_add = dv_add + lax.dot_general(p.astype(BF), do_own[half], (((0,), (0,)), ((), ())),
                                                      preferred_element_type=F32)
                    dk_add = dk_add + lax.dot_general(ds_b, q_own[half], (((0,), (0,)), ((), ())),
                                                      preferred_element_type=F32)
                    dck_ref[half:half + 1, krows] += jnp.sum(ds, axis=0, keepdims=True)
                    new.append((dq + jnp.dot(ds_b, jnp.where(hms[half], ka, 0), preferred_element_type=F32),
                                rs + jnp.sum(ds, axis=1, keepdims=True)))
                dv_ref[krows, :] += dv_add
                dk_ref[krows, :] += dk_add
                return tuple(new)

            one = (jnp.zeros((T, LANES), F32), jnp.zeros((T, 1), F32))
            carry_in = lax.fori_loop(0, i, functools.partial(inner, masked=False), (one, one))
            (dq0, rs0), (dq1, rs1) = inner(i, carry_in, True)
            dq_ref[qrows, :] = (dq0 + dq1) * QK_SCALE
            dcq_ref[qrows, :] = jnp.where(lane == 2 * p_id, rs0, jnp.where(lane == 2 * p_id + 1, rs1,
                                                                             dcq_ref[qrows, :]))
            return carry

        lax.fori_loop(0, nq, outer, 0)
        _comm_edge(comm, comm_refs, grid, first=False)

    aug = pl.BlockSpec((None, 2, S, LANES), lambda p: (p, 0, 0, 0))
    pair = pl.BlockSpec((S, LANES), lambda p: (0, p))
    wide = jax.ShapeDtypeStruct((S, n_pairs * LANES), F32)
    res = pl.pallas_call(
        kern, name="fox_bwd", grid=grid,
        in_specs=[aug, aug, aug, aug] + _comm_specs(comm, "in"),
        out_specs=[pair, pair, pair, pl.BlockSpec((None, 2, S), lambda p: (p, 0, 0)),
                   pl.BlockSpec((S, LANES), lambda p: (0, 0))] + _comm_specs(comm, "out"),
        out_shape=[wide, wide, wide, jax.ShapeDtypeStruct((n_pairs, 2, S), F32),
                   jax.ShapeDtypeStruct((S, LANES), F32)] + (comm.out_shapes if comm else []),
        scratch_shapes=comm.sem_shapes if comm else [],
        compiler_params=_cparams("arbitrary"),
    )(qb_aug, k_aug, dob_aug, vb_aug, *(comm.ins if comm else []))
    return (*res[:5], res[5:])


SWA_TQ = 256


def _swa_window(i, tq):
    start = pl.multiple_of(jnp.maximum(i * tq - WINDOW, 0), LANES)
    return start, i * tq - start


def _swa_valid(offset, tq):
    rel = offset + lax.broadcasted_iota(jnp.int32, (tq, tq + WINDOW), 0) \
        - lax.broadcasted_iota(jnp.int32, (tq, tq + WINDOW), 1)
    return (rel >= 0) & (rel < WINDOW)


def _swa_fwd(qk, v_arr, v_col, sinks):
    S = qk.shape[0]
    tq = min(SWA_TQ, S - WINDOW)
    win = tq + WINDOW

    def kern(q_ref, k_ref, v_ref, sink_ref, o_ref, lse_ref):
        p_id, i = pl.program_id(0), pl.program_id(1)
        start, offset = _swa_window(i, tq)
        kw = k_ref[pl.ds(start, win), :]
        vw = v_ref[pl.ds(start, win), :].astype(BF)
        valid = _swa_valid(offset, tq)
        q = q_ref[...]
        outs, lses = [], []
        for half in (0, 1):
            hm = _half_mask((tq, LANES), half)
            qh = (jnp.where(hm, q, 0).astype(F32) * QK_SCALE).astype(BF)
            s = lax.dot_general(qh, kw, (((1,), (1,)), ((), ())), preferred_element_type=F32)
            s = jnp.where(valid, s, NEG_INF)
            sink = sink_ref[2 * p_id + half]
            m = jnp.maximum(jnp.max(s, axis=1, keepdims=True), sink)
            p = jnp.exp(s - m)
            denom = jnp.sum(p, axis=1, keepdims=True) + jnp.exp(sink - m)
            outs.append(jnp.dot(p.astype(BF), vw, preferred_element_type=F32) / denom)
            lses.append(m + jnp.log(denom))
        hm0 = _half_mask((tq, LANES), 0)
        o_ref[...] = jnp.where(hm0, outs[0], outs[1])
        lse_ref[...] = jnp.where(hm0, lses[0], lses[1])

    tile = pl.BlockSpec((tq, LANES), lambda p, i: (i, p))
    return pl.pallas_call(
        kern, name="swa_fwd", grid=(A_Q_HEADS // 2, S // tq),
        in_specs=[tile, pl.BlockSpec((S, LANES), lambda p, i: (0, A_Q_HEADS // 2)),
                  pl.BlockSpec((S, LANES), lambda p, i: (0, v_col)),
                  pl.BlockSpec(memory_space=pltpu.SMEM)],
        out_specs=[tile, tile],
        out_shape=[jax.ShapeDtypeStruct((S, A_Q_HEADS * HEAD_DIM), F32)] * 2,
        compiler_params=_cparams("parallel", "arbitrary"),
    )(qk, qk, v_arr, sinks)


def _swa_bwd(qk, v_arr, v_col, o_arr, do_arr, lse_arr, sinks, comm=None):
    S = qk.shape[0]
    tq = min(SWA_TQ, S - WINDOW)
    win = tq + WINDOW
    n_pairs = A_Q_HEADS // 2
    grid = (n_pairs, S // tq)

    def kern(*refs):
        own, comm_refs = _own_refs(refs, comm, 7, 4, 0)
        q_ref, k_ref, v_ref, o_ref, do_ref, lse_ref, sink_ref, dq_ref, dk_ref, dv_ref, dsink_ref = own
        _comm_edge(comm, comm_refs, grid, first=True)
        p_id, i = pl.program_id(0), pl.program_id(1)

        @pl.when((p_id == 0) & (i == 0))
        def _():
            dk_ref[...] = jnp.zeros_like(dk_ref)
            dv_ref[...] = jnp.zeros_like(dv_ref)

        @pl.when(i == 0)
        def _():
            dsink_ref[...] = jnp.zeros_like(dsink_ref)

        start, offset = _swa_window(i, tq)
        wrows = pl.ds(start, win)
        kw = k_ref[wrows, :]
        vw = v_ref[wrows, :].astype(BF)
        valid = _swa_valid(offset, tq)
        q, do, o, lse2 = q_ref[...], do_ref[...], o_ref[...], lse_ref[...]
        dq = jnp.zeros((tq, LANES), F32)
        dk = jnp.zeros((win, LANES), F32)
        dv = jnp.zeros((win, LANES), F32)
        for half in (0, 1):
            hm = _half_mask((tq, LANES), half)
            lane0 = half * HEAD_DIM
            qh = (jnp.where(hm, q, 0).astype(F32) * QK_SCALE).astype(BF)
            do_f = jnp.where(hm, do, 0.0)
            doh = do_f.astype(BF)
            delta = jnp.sum(do_f * o, axis=1, keepdims=True)
            lse = lse2[:, lane0:lane0 + 1]
            s = lax.dot_general(qh, kw, (((1,), (1,)), ((), ())), preferred_element_type=F32)
            p = jnp.exp(jnp.where(valid, s, NEG_INF) - lse)
            dp = lax.dot_general(doh, vw, (((1,), (1,)), ((), ())), preferred_element_type=F32)
            ds_b = (p * (dp - delta)).astype(BF)
            dv = dv + lax.dot_general(p.astype(BF), doh, (((0,), (0,)), ((), ())), preferred_element_type=F32)
            dk = dk + lax.dot_general(ds_b, qh, (((0,), (0,)), ((), ())), preferred_element_type=F32)
            kh = jnp.where(_half_mask((win, LANES), half), kw, 0)
            dq = dq + jnp.dot(ds_b, kh, preferred_element_type=F32)
            p_sink = jnp.exp(sink_ref[2 * p_id + half] - lse)
            dsink_ref[0, half:half + 1, :] += jnp.broadcast_to(
                -jnp.sum(p_sink * delta, axis=0, keepdims=True), (1, LANES))
        dq_ref[...] = dq * QK_SCALE
        dk_ref[wrows, :] += dk
        dv_ref[wrows, :] += dv
        _comm_edge(comm, comm_refs, grid, first=False)

    tile = pl.BlockSpec((tq, LANES), lambda p, i: (i, p))
    whole = lambda col: pl.BlockSpec((S, LANES), lambda p, i: (0, col))
    res = pl.pallas_call(
        kern, name="swa_bwd", grid=grid,
        in_specs=[tile, whole(n_pairs), whole(v_col), tile, tile, tile,
                  pl.BlockSpec(memory_space=pltpu.SMEM)] + _comm_specs(comm, "in"),
        out_specs=[tile, whole(0), whole(0),
                   pl.BlockSpec((1, 8, LANES), lambda p, i: (p, 0, 0))] + _comm_specs(comm, "out"),
        out_shape=[jax.ShapeDtypeStruct((S, A_Q_HEADS * HEAD_DIM), F32),
                   jax.ShapeDtypeStruct((S, LANES), F32), jax.ShapeDtypeStruct((S, LANES), F32),
                   jax.ShapeDtypeStruct((n_pairs, 8, LANES), F32)] + (comm.out_shapes if comm else []),
        scratch_shapes=comm.sem_shapes if comm else [],
        compiler_params=_cparams("arbitrary", "arbitrary"),
    )(qk, qk, v_arr, o_arr, do_arr, lse_arr, sinks, *(comm.ins if comm else []))
    return (*res[:4], res[4:])


ADAMW_BLOCK = 256 * 1024


def _adamw(w, g, m, v, name):
    R, C = w.shape
    tr, tc = _tile(R, max(8, ADAMW_BLOCK // C), 8), C

    def kern(w_ref, g_ref, m_ref, v_ref, d_ref, mo_ref, vo_ref):
        g_ = g_ref[...]
        m_new = ADAM_B1 * m_ref[...] + (1.0 - ADAM_B1) * g_
        v_new = ADAM_B2 * v_ref[...] + (1.0 - ADAM_B2) * (g_ * g_)
        m_hat = m_new / (1.0 - ADAM_B1 ** ADAM_STEP)
        v_hat = v_new / (1.0 - ADAM_B2 ** ADAM_STEP)
        d_ref[...] = -ADAM_LR * (m_hat / (jnp.sqrt(v_hat) + ADAM_EPS) + ADAM_WD * w_ref[...])
        mo_ref[...] = m_new
        vo_ref[...] = v_new

    spec = pl.BlockSpec((tr, tc), lambda i, j: (i, j))
    shape = jax.ShapeDtypeStruct((R, C), F32)
    return pl.pallas_call(
        kern, name=name, grid=(R // tr, C // tc),
        in_specs=[spec] * 4, out_specs=[spec] * 3, out_shape=[shape] * 3,
        compiler_params=_cparams("parallel", "parallel"),
    )(w, g, m, v)


def _index_operand(i):
    return jnp.reshape(i, (1,)).astype(jnp.int32)


def _add_pair(whole, got, ci, name):
    P, R, C = whole.shape
    half = R // 2
    tr = _tile(half, 256, 16)
    nb = half // tr

    def kern(ci_ref, a_ref, b_ref, o_ref, ob_ref):
        s = a_ref[...] + b_ref[...].astype(F32)
        o_ref[...] = s
        ob_ref[...] = s.astype(BF)

    spec = pl.BlockSpec((None, tr, C), lambda p, i, ci_ref: (p, i, 0))
    return pl.pallas_call(
        kern, name=name,
        grid_spec=pltpu.PrefetchScalarGridSpec(
            num_scalar_prefetch=1, grid=(P, nb),
            in_specs=[pl.BlockSpec((None, tr, C), lambda p, i, ci_ref: (p, ci_ref[0] * nb + i, 0)), spec],
            out_specs=[spec, spec]),
        out_shape=[jax.ShapeDtypeStruct((P, half, C), F32), jax.ShapeDtypeStruct((P, half, C), BF)],
        compiler_params=_cparams("parallel", "parallel"),
    )(_index_operand(ci), whole, got)


def _add_three(parts, recv, chip, name):
    _, R, C = parts.shape
    tr = _tile(R, 256, 16)

    def kern(chip_ref, o_ref, r0_ref, r1_ref, r2_ref, out_ref):
        s = ((o_ref[...] + r0_ref[...].astype(F32)) + r1_ref[...].astype(F32)) + r2_ref[...].astype(F32)
        out_ref[0] = s
        out_ref[1] = s

    slab = lambda k: pl.BlockSpec((None, tr, C), lambda i, chip_ref: (k, i, 0))
    return pl.pallas_call(
        kern, name=name,
        grid_spec=pltpu.PrefetchScalarGridSpec(
            num_scalar_prefetch=1, grid=(R // tr,),
            in_specs=[pl.BlockSpec((None, tr, C), lambda i, chip_ref: (chip_ref[0], i, 0)),
                      slab(0), slab(1), slab(2)],
            out_specs=pl.BlockSpec((2, tr, C), lambda i, chip_ref: (0, i, 0))),
        out_shape=jax.ShapeDtypeStruct((2, R, C), F32),
        compiler_params=_cparams("parallel"),
    )(_index_operand(chip), parts, recv, recv, recv)


SM_ADA, SM_G, SM_LOSS, SM_BF, SM_SINK, SM_LEN = 0, 6144, 10240, 11264, 11272, 12288


def _small_finalize(gathered):
    def kern(g_ref, tot_ref, loss_ref):
        tot = g_ref[0:1, :]
        for b in range(1, N_DEV):
            tot = tot + g_ref[b:b + 1, :]
        tot_ref[...] = tot
        sq = jnp.sum(tot[:, SM_LOSS:SM_LOSS + D_MODEL], axis=1, keepdims=True)
        loss_ref[...] = jnp.broadcast_to(sq * (0.5 / D_MODEL), (1, LANES))

    full = lambda shape: pl.BlockSpec(shape, lambda i: (0, 0))
    return pl.pallas_call(
        kern, name="small_finalize", grid=(1,),
        in_specs=[full((N_DEV, SM_LEN))],
        out_specs=[full((1, SM_LEN)), full((1, LANES))],
        out_shape=[jax.ShapeDtypeStruct((1, SM_LEN), F32), jax.ShapeDtypeStruct((1, LANES), F32)],
        compiler_params=_cparams("arbitrary"),
    )(gathered)


def _ada_dw(c_t, d_ada):
    N = d_ada.shape[1]
    tn = _tile(N, 512)

    def kern(c_ref, d_ref, o_ref):
        acc = c_ref[:, 0:1] * d_ref[0:1, :]
        for b in range(1, N_DEV):
            acc = acc + c_ref[:, b:b + 1] * d_ref[b:b + 1, :]
        o_ref[...] = acc

    return pl.pallas_call(
        kern, name="ada_dw", grid=(N // tn,),
        in_specs=[pl.BlockSpec((D_MODEL, N_DEV), lambda j: (0, 0)), pl.BlockSpec((N_DEV, tn), lambda j: (0, j))],
        out_specs=pl.BlockSpec((D_MODEL, tn), lambda j: (0, j)),
        out_shape=jax.ShapeDtypeStruct((D_MODEL, N), F32),
        compiler_params=_cparams("parallel"),
    )(c_t, d_ada)


def _here():
    return lax.axis_index("x"), lax.axis_index("y"), lax.axis_index("c")


def _other_chips(x, y):
    return [(1 - x, y), (x, 1 - y), (1 - x, 1 - y)]


_ANY = pl.BlockSpec(memory_space=pl.ANY)


class _Comm:
    def __init__(self, ins, out_shapes, sem_shapes, start, finish):
        self.ins, self.out_shapes, self.sem_shapes = list(ins), list(out_shapes), list(sem_shapes)
        self.start, self.finish = start, finish

    def split(self, refs, n_in, n_out, n_scratch):
        a = n_in + len(self.ins)
        b = a + n_out + len(self.out_shapes)
        own = list(refs[:n_in]) + list(refs[a:a + n_out]) + list(refs[b:b + n_scratch])
        mine = (refs[n_in:a], refs[a + n_out:b], refs[b + n_scratch:])
        return own, mine


def _run_comm(comm, name):
    n_in, n_out = len(comm.ins), len(comm.out_shapes)

    def body(*refs):
        parts = (refs[:n_in], refs[n_in:n_in + n_out], refs[n_in + n_out:])
        comm.start(*parts)
        comm.finish(*parts)

    return pl.pallas_call(
        body, name=name,
        in_specs=[_ANY] * n_in, out_specs=[_ANY] * n_out,
        out_shape=comm.out_shapes, scratch_shapes=comm.sem_shapes,
    )(*comm.ins)


def _gather_comm(blocks):
    L = len(blocks)

    def parts(ins, outs, sems):
        send_sems, recv_sems, local_sems = sems
        x, y, c = _here()
        me, sibling = (x, y, c), (x, y, 1 - c)
        chips = _other_chips(x, y)

        def slot(px, py, pc):
            return 4 * px + 2 * py + pc

        def copy(l, k, block, to, src=None):
            dst = outs[l].at[slot(*block)]
            return pltpu.make_async_remote_copy(
                src_ref=dst if src is None else src, dst_ref=dst,
                send_sem=send_sems.at[l, k], recv_sem=recv_sems.at[l, k],
                device_id=to, device_id_type=MESH)

        mine = [pltpu.make_async_copy(ins[l], outs[l].at[slot(*me)], local_sems.at[l]) for l in range(L)]
        first = []
        for l in range(L):
            first.append(copy(l, 0, me, sibling, src=ins[l]))
            for j, chip in enumerate(chips):
                first.append(copy(l, 1 + j, me, (*chip, c), src=ins[l]))
        return c, me, sibling, chips, copy, mine, first

    def start(ins, outs, sems):
        *_, mine, first = parts(ins, outs, sems)
        for cp in mine + first:
            cp.start()

    def finish(ins, outs, sems):
        c, me, sibling, chips, copy, mine, first = parts(ins, outs, sems)
        passed = []
        for j, chip in enumerate(chips):
            for l in range(L):
                copy(l, 1 + j, (*chip, c), me).wait_recv()
                fwd = copy(l, 4 + j, (*chip, c), sibling)
                fwd.start()
                passed.append(fwd)
        for l in range(L):
            copy(l, 0, sibling, me).wait_recv()
        for j, chip in enumerate(chips):
            for l in range(L):
                copy(l, 4 + j, (*chip, 1 - c), me).wait_recv()
        for cp in first + passed:
            cp.wait_send()
        for cp in mine:
            cp.wait()

    return _Comm(blocks, [jax.ShapeDtypeStruct((N_DEV,) + b.shape, b.dtype) for b in blocks],
                 [pltpu.SemaphoreType.DMA((L, 7)), pltpu.SemaphoreType.DMA((L, 7)), pltpu.SemaphoreType.DMA((L,))],
                 start, finish)


def _allgather8(blocks, name):
    return _run_comm(_gather_comm(blocks), name)


def _swap_comm(arrs):
    L = len(arrs)

    def copies(ins, outs, sems):
        send_sems, recv_sems = sems
        x, y, c = _here()
        cps = []
        for l in range(L):
            half = arrs[l].shape[1] // 2
            rows = pl.ds(pl.multiple_of((1 - c) * half, 16), half)
            cps.append(pltpu.make_async_remote_copy(
                src_ref=ins[l].at[:, rows, :], dst_ref=outs[l], send_sem=send_sems.at[l],
                recv_sem=recv_sems.at[l], device_id=(x, y, 1 - c), device_id_type=MESH))
        return cps

    def start(ins, outs, sems):
        for cp in copies(ins, outs, sems):
            cp.start()

    def finish(ins, outs, sems):
        for cp in copies(ins, outs, sems):
            cp.wait()

    return _Comm(arrs, [jax.ShapeDtypeStruct((a.shape[0], a.shape[1] // 2, a.shape[2]), a.dtype) for a in arrs],
                 [pltpu.SemaphoreType.DMA((L,)), pltpu.SemaphoreType.DMA((L,))], start, finish)


def _sibling_join(bufs, name):
    L = len(bufs)

    def body(*refs):
        outs = refs[L:2 * L]
        send_sems, recv_sems = refs[2 * L:]
        x, y, c = _here()
        for l in range(L):
            pltpu.make_async_remote_copy(src_ref=outs[l].at[c], dst_ref=outs[l].at[c], send_sem=send_sems.at[l],
                                         recv_sem=recv_sems.at[l], device_id=(x, y, 1 - c),
                                         device_id_type=MESH).start()
        for l in range(L):
            pltpu.make_async_remote_copy(src_ref=outs[l].at[c], dst_ref=outs[l].at[1 - c],
                                         send_sem=send_sems.at[l], recv_sem=recv_sems.at[l],
                                         device_id=(x, y, 1 - c), device_id_type=MESH).wait()

    return pl.pallas_call(
        body, name=name,
        in_specs=[_ANY] * L, out_specs=[_ANY] * L,
        out_shape=[jax.ShapeDtypeStruct(a.shape, a.dtype) for a in bufs],
        input_output_aliases={l: l for l in range(L)},
        scratch_shapes=[pltpu.SemaphoreType.DMA((L,)), pltpu.SemaphoreType.DMA((L,))],
    )(*bufs)


def _scatter_comm(arrs):
    L = len(arrs)

    def copies(ins, outs, sems):
        send_sems, recv_sems = sems
        x, y, c = _here()
        return [pltpu.make_async_remote_copy(
            src_ref=ins[l].at[2 * tx + ty], dst_ref=outs[l].at[j],
            send_sem=send_sems.at[l, j], recv_sem=recv_sems.at[l, j],
            device_id=(tx, ty, c), device_id_type=MESH)
            for l in range(L) for j, (tx, ty) in enumerate(_other_chips(x, y))]

    def start(ins, outs, sems):
        for cp in copies(ins, outs, sems):
            cp.start()

    def finish(ins, outs, sems):
        for cp in copies(ins, outs, sems):
            cp.wait()

    return _Comm(arrs, [jax.ShapeDtypeStruct((3,) + a.shape[1:], a.dtype) for a in arrs],
                 [pltpu.SemaphoreType.DMA((L, 3)), pltpu.SemaphoreType.DMA((L, 3))], start, finish)


_A_ORDER = np.array(A_HEAD_ORDER)
_A_INVERSE = np.argsort(_A_ORDER)


def _permute_in_weights(w_in):
    qa = w_in[:, 0:512].reshape(D_MODEL, A_Q_HEADS, HEAD_DIM)[:, _A_ORDER, :].reshape(D_MODEL, 512)
    f_pad = jnp.pad(w_in[:, 2304:2312], ((0, 0), (0, LANES - B_HEADS)))
    w_a = jnp.concatenate([qa, w_in[:, 512:640], f_pad], axis=1)
    return w_a, w_in[:, 640:2304], w_in[:, 2312:4360]


def _slab_segments():
    segs = [(h * HEAD_DIM, int(_A_INVERSE[h]) * HEAD_DIM, HEAD_DIM) for h in range(A_Q_HEADS)]
    segs += [(512, OFF_KA, 128), (640, W_A + OFF_VA, 128), (768, W_A + OFF_QB, 1536),
             (2304, OFF_F, B_HEADS), (2312, W_A + W_B, W_G)]
    return segs


def _shard_slabs(dw_perm):
    R = dw_perm.shape[0]
    tr = _tile(R, 128, 8)
    plan = []
    for k in range(N_CHIP):
        for b in range(W_SHARD_PAD // LANES):
            lo, hi = k * W_SHARD + b * LANES, min(k * W_SHARD + (b + 1) * LANES, (k + 1) * W_SHARD)
            parts = []
            for o0, s0, n in _slab_segments():
                a, z = max(lo, o0), min(hi, o0 + n)
                while a < z:
                    s = s0 + (a - o0)
                    run = min(z - a, LANES - s % LANES)
                    parts.append((s // LANES, ((a - lo) - s % LANES) % LANES, a - lo, run))
                    a += run
            plan.append((k, b, parts))

    def kern(x_ref, o32_ref, obf_ref):
        lane = lax.broadcasted_iota(jnp.int32, (tr, LANES), 1)
        for k, b, parts in plan:
            acc = jnp.zeros((tr, LANES), F32)
            for src, rot, first, run in parts:
                blk = x_ref[:, src * LANES:(src + 1) * LANES]
                if rot:
                    blk = pltpu.roll(blk, rot, 1)
                acc = jnp.where((lane >= first) & (lane < first + run), blk, acc)
            o32_ref[k, :, b * LANES:(b + 1) * LANES] = acc
            obf_ref[k, :, b * LANES:(b + 1) * LANES] = acc.astype(BF)

    out_spec = pl.BlockSpec((N_CHIP, tr, W_SHARD_PAD), lambda i: (0, i, 0))
    return tuple(pl.pallas_call(
        kern, name="shard_slabs", grid=(R // tr,),
        in_specs=[pl.BlockSpec((tr, W_PERM), lambda i: (i, 0))],
        out_specs=[out_spec, out_spec],
        out_shape=[jax.ShapeDtypeStruct((N_CHIP, R, W_SHARD_PAD), F32),
                   jax.ShapeDtypeStruct((N_CHIP, R, W_SHARD_PAD), BF)],
        compiler_params=_cparams("parallel"),
    )(dw_perm))


class _NoExchange:
    def __init__(self, w_in, rest):
        self.w_in_whole, self.rest, self.grads = w_in, rest, {}

    def w_in_comm(self):
        return None

    def w_in(self, outs):
        return self.w_in_whole

    def rest_weights_comm(self):
        return None

    def rest_weights(self, outs):
        return self.rest

    def swap_comm(self, pieces, tag):
        self.grads[tag] = [p32 for p32, _ in pieces]
        return None

    def swap_done(self, outs, tag):
        return None

    def reduce_done(self, outs, tag):
        pass


class _Exchange:
    def __init__(self, ci, chip, w_in_shard, rest_shards):
        self.ci, self.chip, self.w_in_shard, self.rest_shards = ci, chip, w_in_shard, rest_shards
        self.pieces, self.part_f32, self.halves = {}, {}, {}

    def _my_half(self, a, axis=0, other=False):
        rows = a.shape[axis] // 2
        return lax.dynamic_slice_in_dim(a, ((1 - self.ci) if other else self.ci) * rows, rows, axis=axis)

    def w_in_comm(self):
        return _gather_comm([self._my_half(self.w_in_shard).astype(BF)])

    def w_in(self, outs):
        return _col_sharded(outs[0])

    def rest_weights_comm(self):
        return _gather_comm([self._my_half(w).astype(BF) for w in self.rest_shards])

    def rest_weights(self, outs):
        w_ba, w_bb, w_out, w_fi, w_fo = outs
        return (_col_sharded(w_ba), _col_sharded(w_bb), _row_sharded(w_out), _col_sharded(w_fi),
                _row_sharded(w_fo))

    def swap_comm(self, pieces, tag):
        self.pieces[tag] = pieces
        return _swap_comm([pbf for _, pbf in pieces])

    def swap_done(self, got, tag):
        self.part_f32[tag], part_bf = [], []
        for l, ((p32, _), g_) in enumerate(zip(self.pieces[tag], got)):
            s32, sbf = _add_pair(p32, g_, self.ci, f"chip_sum_{tag}_{l}")
            self.part_f32[tag].append(s32)
            part_bf.append(sbf)
        return _scatter_comm(part_bf)

    def reduce_done(self, outs, tag):
        self.halves[tag] = [_add_three(p32, r, self.chip, f"shard_sum_{tag}_{l}")
                            for l, (p32, r) in enumerate(zip(self.part_f32[tag], outs))]


def _col_sharded(g):
    return jnp.transpose(g.reshape(N_CHIP, -1, g.shape[-1]), (1, 0, 2)).reshape(2 * g.shape[1], N_CHIP * g.shape[-1])


def _row_sharded(g):
    return g.reshape(N_DEV * g.shape[1], g.shape[-1])


def _rope_tables(pos):
    inv_freq = 1.0 / (ROPE_THETA ** (jnp.arange(0, HEAD_DIM, 2, dtype=F32) / HEAD_DIM))
    ang = pos.astype(F32)[:, None] * inv_freq
    cos, sin = jnp.cos(ang), jnp.sin(ang)
    return jnp.tile(cos, (1, 4)), jnp.tile(jnp.concatenate([-sin, sin], axis=1), (1, 2))


def _local_step(x, pos, ada, g1, g2, g3, g4, b_f, sinks, exch, target):
    S = x.shape[0]
    t_fox = _tile(S, 512, LANES) if S >= 1024 else S // 2
    t_fox_fwd = _tile(S, 1024, LANES) if S >= 2048 else S // 2
    shift_m, scale_m, gate_m, shift_f, scale_f, gate_f = [ada[i:i + 1] for i in range(N_ADA)]
    cos_t, sin_t = _rope_tables(pos)
    sinks_p = sinks.reshape(A_KV_HEADS, 4).T.reshape(A_Q_HEADS)
    b_f_pad = jnp.pad(b_f, (0, LANES - B_HEADS)).reshape(1, LANES)

    h1, outs = _pre_norm(x, g1, scale_m, shift_m, "pre_mix_norm", comm=exch.w_in_comm())
    w_a, w_b, w_g = _permute_in_weights(exch.w_in(outs))
    w_perm = jnp.concatenate([w_a, w_b, w_g], axis=1)
    p_a = _mm(h1, w_a, "nn", F32, "proj_a")
    p_b = _mm(h1, w_b, "nn", BF, "proj_b")
    p_g = _mm(h1, w_g, "nn", BF, "proj_g")
    (qk_a,) = _rope([p_a], [640], cos_t, sin_t, "rope_fwd")
    o_a, lse_a = _swa_fwd(qk_a, p_b, 0, sinks_p)
    q_aug, k_aug = _fox_prep_fwd(p_b, _fox_gate_fwd(p_a, b_f_pad), t_fox)
    comm = exch.rest_weights_comm()
    o_b, lse_b, outs = _fox_fwd(q_aug, k_aug, p_b, t_fox_fwd, comm=comm)
    w_ba, w_bb, w_out, w_fi, w_fo = exch.rest_weights(outs)
    w_ba_p = w_ba.reshape(A_Q_HEADS, HEAD_DIM, D_MODEL)[_A_ORDER].reshape(512, D_MODEL)
    pa = _mm(o_a, w_ba_p, "nn", F32, "branch_a")
    pb = _mm(o_b, w_bb, "nn", F32, "branch_b")
    merged = _merge_fwd(p_g, pa, pb)
    y1 = _mm(merged, w_out, "nn", F32, "out_proj")
    x2, h2 = _post_pre(x, y1, g2, gate_m, g3, scale_f, shift_f)
    gu = _mm(h2, w_fi, "nn", BF, "ffn_in")
    act = _swiglu_fwd(gu)
    y2 = _mm(act, w_fo, "nn", F32, "ffn_out")
    d_out, d_y2, st_f = _final(x2, y2, g4, gate_f, target)

    d_act = _mm(d_y2, w_fo, "nt", F32, "ffn_out_dx")
    row_pieces = lambda pair: tuple(t.reshape(N_CHIP, t.shape[0] // N_CHIP, t.shape[1]) for t in pair)
    dw_fo = row_pieces(_mm(act, d_y2, "tn", F32, "ffn_out_dw", twin=True))
    d_gu = _swiglu_bwd(d_act, gu)
    d_h2 = _mm(d_gu, w_fi, "nt", F32, "ffn_in_dx")
    dw_fi = _mm(h2, d_gu, "tn", F32, "ffn_in_dw", col_pieces=N_CHIP, twin=True)
    d_x2, d_y1, st_m = _mid_bwd(d_h2, x2, d_out, y1, g3, scale_f, g2, gate_m)
    d_merged = _mm(d_y1, w_out, "nt", F32, "out_proj_dx")
    dw_out = row_pieces(_mm(merged, d_y1, "tn", F32, "out_proj_dw", twin=True))
    d_pa, d_pb, d_ga, d_gb = _merge_bwd(d_merged, p_g, pa, pb)
    d_oa = _mm(d_pa, w_ba_p, "nt", F32, "branch_a_dx")
    dw_ba_p = _mm(o_a, d_pa, "tn", F32, "branch_a_dw", col_pieces=N_CHIP, twin=True)
    d_ob = _mm(d_pb, w_bb, "nt", F32, "branch_b_dx")
    dw_bb = _mm(o_b, d_pb, "tn", F32, "branch_b_dw", col_pieces=N_CHIP, twin=True)
    head_rows = lambda t: t.reshape(N_CHIP, A_Q_HEADS, HEAD_DIM, -1)[:, _A_INVERSE].reshape(t.shape)
    dw_ba = tuple(head_rows(t) for t in dw_ba_p)
    comm = exch.swap_comm([dw_ba, dw_bb, dw_out, dw_fi, dw_fo], "early")
    dq_a, dk_a, dv_a, d_sink, outs = _swa_bwd(qk_a, p_b, 0, o_a, d_oa, lse_a, sinks_p, comm=comm)
    comm = exch.swap_done(outs, "early")
    qb_aug, dob_aug, vb_aug = _fox_prep_bwd(q_aug, p_b, o_b, d_ob, lse_b, t_fox)
    dq_b, dk_b, dv_b, d_ck, d_cq, outs = _fox_bwd(qb_aug, k_aug, dob_aug, vb_aug, t_fox, comm=comm)
    exch.reduce_done(outs, "early")
    d_qa, d_ka = _rope([dq_a, dk_a], [512, LANES], cos_t, -sin_t, "rope_bwd")
    d_ck_cols = jnp.pad(d_ck.reshape(B_HEADS, S).T, ((0, 0), (0, LANES - B_HEADS)))
    d_f, d_bf = _fox_gate_bwd(d_cq, d_ck_cols, p_a, b_f_pad)
    d_proj = jnp.concatenate([d_qa, d_ka, d_f, dv_a.astype(BF), dq_b.astype(BF), dk_b.astype(BF),
                              dv_b.astype(BF), d_ga, d_gb], axis=1)
    dw_perm = _mm(h1, d_proj, "tn", F32, "proj_dw")
    swap = exch.swap_comm([_shard_slabs(dw_perm)], "late")
    comm = exch.swap_done(_run_comm(swap, "grads_to_sibling_late") if swap else None, "late")
    res = _mm(d_proj, w_perm, "nt", F32, "proj_dx", comm=comm)
    d_h1 = res[0] if comm else res
    exch.reduce_done(res[1] if comm else None, "late")
    grad_x, st_p = _pre_bwd(d_h1, x, d_x2, g1, scale_m)

    d_sinks = d_sink[:, :2, 0].T.reshape(A_Q_HEADS)
    small = jnp.concatenate([
        st_p[0], st_p[1], st_m[3], st_m[0], st_m[1], st_f[0],
        st_p[2], st_m[4], st_m[2], st_f[1],
        st_f[2], d_bf[0, :B_HEADS], d_sinks,
        jnp.zeros((SM_LEN - SM_SINK - A_Q_HEADS,), F32)])
    return grad_x, small


def kernel(x, c, positions, w_ada, b_ada, g_pre_mix, g_post_mix, w_in, b_f, sinks, w_branch_a, w_branch_b, w_out, g_pre_ffn, g_post_ffn, w_ffn_in, w_ffn_out, loss_target, m_w_ada, m_b_ada, m_g_pre_mix, m_g_post_mix, m_w_in, m_b_f, m_sinks, m_w_branch_a, m_w_branch_b, m_w_out, m_g_pre_ffn, m_g_post_ffn, m_w_ffn_in, m_w_ffn_out, v_w_ada, v_b_ada, v_g_pre_mix, v_g_post_mix, v_w_in, v_b_f, v_sinks, v_w_branch_a, v_w_branch_b, v_w_out, v_g_pre_ffn, v_g_post_ffn, v_w_ffn_in, v_w_ffn_out):
    xi, yi, ci = _here()
    chip = 2 * xi + yi
    dev = 2 * chip + ci

    (c_g,) = _allgather8([c.reshape(8, LANES)], "gather_c")
    c_all = c_g.reshape(N_DEV, D_MODEL)
    exch = _Exchange(ci, chip, w_in[0], [w_branch_a[0], w_branch_b[0], w_out[0], w_ffn_in[0], w_ffn_out[0]])

    ada_cols = _mm(c_all, w_ada[0], "nn", F32, "ada_fwd")
    (ada_g,) = _allgather8([ada_cols], "gather_ada")
    ada_mine = lax.dynamic_index_in_dim(ada_g.reshape(N_CHIP, 2, N_DEV, -1)[:, 0], dev, axis=1, keepdims=False)
    ada = (ada_mine.reshape(-1) + b_ada[0]).reshape(N_ADA, D_MODEL)

    grad_x, small = _local_step(
        x[0], positions[0], ada, g_pre_mix, g_post_mix, g_pre_ffn, g_post_ffn, b_f[0], sinks[0],
        exch, loss_target[0])

    (small_g,) = _allgather8([small.reshape(8, SM_LEN // 8)], "gather_small")
    small_all = small_g.reshape(N_DEV, SM_LEN)
    small_tot, loss_row = _small_finalize(small_all)
    loss = loss_row[0, 0]
    d_ada_cols = lax.dynamic_slice_in_dim(small_all[:, :N_ADA * D_MODEL], chip * (N_ADA * D_MODEL // N_CHIP),
                                          N_ADA * D_MODEL // N_CHIP, axis=1)
    g_w_ada = _ada_dw(c_all.T, d_ada_cols)

    joined = _sibling_join(exch.halves["late"] + exch.halves["early"], "grads_join")
    g_w_in, g_w_ba, g_w_bb, g_w_out, g_w_fi, g_w_fo = [j.reshape(2 * j.shape[1], j.shape[2]) for j in joined]

    def small_vec(b_ada_, g1_, g2_, g3_, g4_, b_f_, sinks_):
        return jnp.concatenate([b_ada_[0], g1_[0], g2_[0], g3_[0], g4_[0], jnp.zeros((D_MODEL,), F32),
                                b_f_[0], sinks_[0], jnp.zeros((SM_LEN - SM_SINK - A_Q_HEADS,), F32)]
                               ).reshape(8, SM_LEN // 8)

    sw = small_vec(b_ada, g_pre_mix, g_post_mix, g_pre_ffn, g_post_ffn, b_f, sinks)
    sm = small_vec(m_b_ada, m_g_pre_mix, m_g_post_mix, m_g_pre_ffn, m_g_post_ffn, m_b_f, m_sinks)
    sv = small_vec(v_b_ada, v_g_pre_mix, v_g_post_mix, v_g_pre_ffn, v_g_post_ffn, v_b_f, v_sinks)
    s_upd = [u.reshape(SM_LEN) for u in _adamw(sw, small_tot.reshape(8, SM_LEN // 8), sm, sv, "adamw_small")]
    s_grad = small_tot.reshape(SM_LEN)

    def unpack(vec):
        row = lambda a, n: vec[a:a + n].reshape(1, n)
        return dict(b_ada=row(SM_ADA, N_ADA * D_MODEL), g_pre_mix=row(SM_G, D_MODEL),
                    g_post_mix=row(SM_G + D_MODEL, D_MODEL), g_pre_ffn=row(SM_G + 2 * D_MODEL, D_MODEL),
                    g_post_ffn=row(SM_G + 3 * D_MODEL, D_MODEL), b_f=row(SM_BF, B_HEADS),
                    sinks=row(SM_SINK, A_Q_HEADS))

    big = dict(
        w_ada=(w_ada, g_w_ada, m_w_ada, v_w_ada),
        w_branch_a=(w_branch_a, g_w_ba, m_w_branch_a, v_w_branch_a),
        w_branch_b=(w_branch_b, g_w_bb, m_w_branch_b, v_w_branch_b),
        w_out=(w_out, g_w_out, m_w_out, v_w_out), w_ffn_in=(w_ffn_in, g_w_fi, m_w_ffn_in, v_w_ffn_in),
        w_ffn_out=(w_ffn_out, g_w_fo, m_w_ffn_out, v_w_ffn_out))
    grads, deltas, new_m, new_v = unpack(s_grad), unpack(s_upd[0]), unpack(s_upd[1]), unpack(s_upd[2])
    for n, (w_, g_, m_, v_) in big.items():
        d_, nm_, nv_ = _adamw(w_[0], g_, m_[0], v_[0], "adamw_" + n)
        grads[n], deltas[n], new_m[n], new_v[n] = g_[None], d_[None], nm_[None], nv_[None]
    pad_cols = lambda a: jnp.pad(a, ((0, 0), (0, W_SHARD_PAD - W_SHARD)))
    upd = _adamw(pad_cols(w_in[0]), g_w_in, pad_cols(m_w_in[0]), pad_cols(v_w_in[0]), "adamw_w_in")
    grads["w_in"], deltas["w_in"], new_m["w_in"], new_v["w_in"] = [t[None, :, :W_SHARD] for t in (g_w_in, *upd)]

    names = ["w_ada", "b_ada", "g_pre_mix", "g_post_mix", "w_in", "b_f", "sinks", "w_branch_a", "w_branch_b",
             "w_out", "g_pre_ffn", "g_post_ffn", "w_ffn_in", "w_ffn_out"]
    return (loss, grad_x[None], *[grads[n] for n in names], *[deltas[n] for n in names],
            *[new_m[n] for n in names], *[new_v[n] for n in names])
```

```python
import functools
import math

import numpy as np
import jax
import jax.numpy as jnp
from jax import lax
from jax.experimental import pallas as pl
from jax.experimental.pallas import tpu as pltpu

F32 = jnp.float32
BF = jnp.bfloat16

D_MODEL = 1024
HEAD_DIM = 64
LANES = 128
WINDOW = 128
A_Q_HEADS = 8
A_KV_HEADS = 2
B_HEADS = 8
D_FF = 2816
ROPE_THETA = 10000.0
RMS_EPS = 1e-6
N_ADA = 6
N_DEV = 8
N_CHIP = 4

ADAM_LR = 0.001
ADAM_B1 = 0.9
ADAM_B2 = 0.999
ADAM_EPS = 1e-08
ADAM_WD = 0.01
ADAM_STEP = 10

VMEM_LIMIT = 48 * 1024 * 1024
MESH = pl.DeviceIdType.MESH

A_HEAD_ORDER = (0, 4, 1, 5, 2, 6, 3, 7)

OFF_QA, OFF_KA, OFF_F = 0, 512, 640
W_A = 768
OFF_VA, OFF_QB, OFF_KB, OFF_VB = 0, 128, 640, 1152
W_B = 1664
W_G = 2048
W_PERM = W_A + W_B + W_G
W_SHARD = 1090
W_SHARD_PAD = 1152


def _tile(n, cap, mult=LANES):
    if n <= cap:
        return n
    t = (cap // mult) * mult
    while t >= mult:
        if n % t == 0:
            return t
        t -= mult
    raise ValueError(f"no tile for {n}")


MXU_WIDTH = 256
MM_OPERAND_BYTES = 28 * 1024 * 1024


def _mm_tiles(M, N, K, a_bytes, b_bytes, tm_cap, tn_cap):
    tm = _tile(M, tm_cap)
    try:
        tn = _tile(N, tn_cap, MXU_WIDTH)
    except ValueError:
        tn = _tile(N, tn_cap)
    fits = lambda tk: 2 * tk * (tm * a_bytes + tn * b_bytes) <= MM_OPERAND_BYTES
    tk = K if fits(K) else next(t for t in range(K // LANES * LANES, 0, -LANES) if K % t == 0 and fits(t))
    return tm, tn, tk


def _cparams(*sem):
    return pltpu.CompilerParams(dimension_semantics=sem, vmem_limit_bytes=VMEM_LIMIT)


def _own_refs(refs, comm, n_in, n_out, n_scratch):
    if comm is None:
        return list(refs), None
    return comm.split(refs, n_in, n_out, n_scratch)


def _comm_specs(comm, side):
    if comm is None:
        return []
    return [pl.BlockSpec(memory_space=pl.ANY)] * len(comm.ins if side == "in" else comm.out_shapes)


def _comm_edge(comm, comm_refs, grid, first):
    if comm is None:
        return
    at_edge = None
    for axis, n in enumerate(grid):
        here = pl.program_id(axis) == (0 if first else n - 1)
        at_edge = here if at_edge is None else at_edge & here
    pl.when(at_edge)(lambda: (comm.start if first else comm.finish)(*comm_refs))


def _mm(a, b, mode, out_dtype, name, tm_cap=512, tn_cap=2816, comm=None, col_pieces=1, twin=False):
    if mode == "nn":
        (M, K), (K2, N) = a.shape, b.shape
        dims = (((1,), (0,)), ((), ()))
    elif mode == "nt":
        (M, K), (N, K2) = a.shape, b.shape
        dims = (((1,), (1,)), ((), ()))
    else:
        (K, M), (K2, N) = a.shape, b.shape
        dims = (((0,), (0,)), ((), ()))
    assert K == K2, (a.shape, b.shape, mode)
    tm, tn, tk = _mm_tiles(M, N // col_pieces, K, a.dtype.itemsize, b.dtype.itemsize, tm_cap, tn_cap)
    nk = K // tk
    n_out = 2 if twin else 1
    n_scratch = 1 if nk > 1 else 0
    if mode == "nn":
        a_spec = pl.BlockSpec((tm, tk), lambda i, j, k: (i, k))
        b_spec = pl.BlockSpec((tk, tn), lambda i, j, k: (k, j))
    elif mode == "nt":
        a_spec = pl.BlockSpec((tm, tk), lambda i, j, k: (i, k))
        b_spec = pl.BlockSpec((tn, tk), lambda i, j, k: (j, k))
    else:
        a_spec = pl.BlockSpec((tk, tm), lambda i, j, k: (k, i))
        b_spec = pl.BlockSpec((tk, tn), lambda i, j, k: (k, j))

    grid = (M // tm, N // tn, nk)

    def kern(*refs):
        own, comm_refs = _own_refs(refs, comm, 2, n_out, n_scratch)
        a_ref, b_ref, o_refs = own[0], own[1], own[2:2 + n_out]
        k = pl.program_id(2)
        _comm_edge(comm, comm_refs, grid, first=True)
        part = lax.dot_general(a_ref[...].astype(BF), b_ref[...].astype(BF), dims,
                               preferred_element_type=F32)
        if nk == 1:
            for o_ref in o_refs:
                o_ref[...] = part.astype(o_ref.dtype)
        else:
            acc_ref = own[2 + n_out]

            @pl.when(k == 0)
            def _():
                acc_ref[...] = part

            @pl.when(k > 0)
            def _():
                acc_ref[...] += part

            @pl.when(k == nk - 1)
            def _():
                for o_ref in o_refs:
                    o_ref[...] = acc_ref[...].astype(o_ref.dtype)

        _comm_edge(comm, comm_refs, grid, first=False)

    if col_pieces > 1:
        per = N // col_pieces // tn
        out_spec = pl.BlockSpec((None, tm, tn), lambda i, j, k: (j // per, i, j % per))
        shape = (col_pieces, M, N // col_pieces)
    else:
        out_spec = pl.BlockSpec((tm, tn), lambda i, j, k: (i, j))
        shape = (M, N)
    dtypes = [out_dtype, BF] if twin else [out_dtype]
    res = pl.pallas_call(
        kern, name=name, grid=grid,
        in_specs=[a_spec, b_spec] + _comm_specs(comm, "in"),
        out_specs=[out_spec] * n_out + _comm_specs(comm, "out"),
        out_shape=[jax.ShapeDtypeStruct(shape, d) for d in dtypes] + (comm.out_shapes if comm else []),
        scratch_shapes=[pltpu.VMEM((tm, tn), F32)] * n_scratch + (comm.sem_shapes if comm else []),
        compiler_params=_cparams("parallel", "parallel", "arbitrary"),
    )(a, b, *(comm.ins if comm else []))
    own = res[0] if n_out == 1 else tuple(res[:n_out])
    return (own, res[n_out:]) if comm else own


ROWS = 256


def _row_spec(tm, width=D_MODEL, col=0):
    return pl.BlockSpec((tm, width), lambda i: (i, col))


def _vec_spec(width=D_MODEL):
    return pl.BlockSpec((1, width), lambda i: (0, 0))


def _rms(x):
    return lax.rsqrt(jnp.mean(x * x, axis=-1, keepdims=True) + RMS_EPS)


def _colsum(x):
    return jnp.sum(x, axis=0, keepdims=True)


def _norm_bwd(d_xn, xn, r):
    return r * (d_xn - xn * jnp.mean(d_xn * xn, axis=-1, keepdims=True))


def _pre_norm(x, g, scale, shift, name, comm=None):
    S = x.shape[0]
    tm = _tile(S, ROWS, 8)
    grid = (S // tm,)

    def kern(*refs):
        (x_ref, g_ref, sc_ref, sh_ref, h_ref), comm_refs = _own_refs(refs, comm, 4, 1, 0)
        _comm_edge(comm, comm_refs, grid, first=True)
        xf = x_ref[...]
        y = xf * _rms(xf) * g_ref[...]
        h_ref[...] = (y * (1.0 + sc_ref[...]) + sh_ref[...]).astype(BF)
        _comm_edge(comm, comm_refs, grid, first=False)

    res = pl.pallas_call(
        kern, name=name, grid=grid,
        in_specs=[_row_spec(tm), _vec_spec(), _vec_spec(), _vec_spec()] + _comm_specs(comm, "in"),
        out_specs=[_row_spec(tm)] + _comm_specs(comm, "out"),
        out_shape=[jax.ShapeDtypeStruct((S, D_MODEL), BF)] + (comm.out_shapes if comm else []),
        scratch_shapes=comm.sem_shapes if comm else [],
        compiler_params=_cparams("arbitrary"),
    )(x, g, scale, shift, *(comm.ins if comm else []))
    return res[0], res[1:]


def _post_pre(x, y1, g2, gate_m, g3, scale_f, shift_f):
    S = x.shape[0]
    tm = _tile(S, ROWS, 8)

    def kern(x_ref, y_ref, g2_ref, gm_ref, g3_ref, sc_ref, sh_ref, x2_ref, h2_ref):
        y = y_ref[...]
        n2 = y * _rms(y) * g2_ref[...]
        x2 = x_ref[...] + gm_ref[...] * n2
        x2_ref[...] = x2
        n3 = x2 * _rms(x2) * g3_ref[...]
        h2_ref[...] = (n3 * (1.0 + sc_ref[...]) + sh_ref[...]).astype(BF)

    return pl.pallas_call(
        kern, name="post_mix_pre_ffn", grid=(S // tm,),
        in_specs=[_row_spec(tm), _row_spec(tm)] + [_vec_spec()] * 5,
        out_specs=[_row_spec(tm), _row_spec(tm)],
        out_shape=[jax.ShapeDtypeStruct((S, D_MODEL), F32), jax.ShapeDtypeStruct((S, D_MODEL), BF)],
        compiler_params=_cparams("parallel"),
    )(x, y1, g2, gate_m, g3, scale_f, shift_f)


def _stats_spec():
    return pl.BlockSpec((8, D_MODEL), lambda i: (0, 0))


def _final(x2, y2, g4, gate_f, target):
    S = x2.shape[0]
    tm = _tile(S, ROWS, 8)

    def kern(x2_ref, y_ref, g4_ref, gf_ref, t_ref, dout_ref, dy_ref, st_ref):
        @pl.when(pl.program_id(0) == 0)
        def _():
            st_ref[...] = jnp.zeros_like(st_ref)

        y = y_ref[...]
        r = _rms(y)
        yn = y * r
        n4 = yn * g4_ref[...]
        diff = x2_ref[...] + gf_ref[...] * n4 - t_ref[...]
        d_out = diff / D_MODEL
        dout_ref[...] = d_out
        dn = d_out * gf_ref[...]
        dy_ref[...] = _norm_bwd(dn * g4_ref[...], yn, r).astype(BF)
        st_ref[0:1, :] += _colsum(d_out * n4)
        st_ref[1:2, :] += _colsum(dn * yn)
        st_ref[2:3, :] += _colsum(diff * diff)

    return pl.pallas_call(
        kern, name="final_loss", grid=(S // tm,),
        in_specs=[_row_spec(tm), _row_spec(tm), _vec_spec(), _vec_spec(), _row_spec(tm)],
        out_specs=[_row_spec(tm), _row_spec(tm), _stats_spec()],
        out_shape=[jax.ShapeDtypeStruct((S, D_MODEL), F32), jax.ShapeDtypeStruct((S, D_MODEL), BF),
                   jax.ShapeDtypeStruct((8, D_MODEL), F32)],
        compiler_params=_cparams("arbitrary"),
    )(x2, y2, g4, gate_f, target)


def _mid_bwd(d_h2, x2, d_out, y1, g3, scale_f, g2, gate_m):
    S = x2.shape[0]
    tm = _tile(S, ROWS, 8)

    def kern(dh_ref, x2_ref, dout_ref, y_ref, g3_ref, sc_ref, g2_ref, gm_ref, dx2_ref, dy_ref, st_ref):
        @pl.when(pl.program_id(0) == 0)
        def _():
            st_ref[...] = jnp.zeros_like(st_ref)

        dh = dh_ref[...]
        x2 = x2_ref[...]
        r3 = _rms(x2)
        xn = x2 * r3
        one_sc = 1.0 + sc_ref[...]
        d_x2 = dout_ref[...] + _norm_bwd(dh * one_sc * g3_ref[...], xn, r3)
        dx2_ref[...] = d_x2
        y = y_ref[...]
        r2 = _rms(y)
        yn = y * r2
        dn = d_x2 * gm_ref[...]
        dy_ref[...] = _norm_bwd(dn * g2_ref[...], yn, r2).astype(BF)
        st_ref[0:1, :] += _colsum(dh)
        st_ref[1:2, :] += _colsum(dh * (xn * g3_ref[...]))
        st_ref[2:3, :] += _colsum(dh * one_sc * xn)
        st_ref[3:4, :] += _colsum(d_x2 * (yn * g2_ref[...]))
        st_ref[4:5, :] += _colsum(dn * yn)

    return pl.pallas_call(
        kern, name="mid_bwd", grid=(S // tm,),
        in_specs=[_row_spec(tm)] * 4 + [_vec_spec()] * 4,
        out_specs=[_row_spec(tm), _row_spec(tm), _stats_spec()],
        out_shape=[jax.ShapeDtypeStruct((S, D_MODEL), F32), jax.ShapeDtypeStruct((S, D_MODEL), BF),
                   jax.ShapeDtypeStruct((8, D_MODEL), F32)],
        compiler_params=_cparams("arbitrary"),
    )(d_h2, x2, d_out, y1, g3, scale_f, g2, gate_m)


def _pre_bwd(d_h1, x, d_x2, g1, scale_m):
    S = x.shape[0]
    tm = _tile(S, ROWS, 8)

    def kern(dh_ref, x_ref, dx2_ref, g_ref, sc_ref, gx_ref, st_ref):
        @pl.when(pl.program_id(0) == 0)
        def _():
            st_ref[...] = jnp.zeros_like(st_ref)

        dh = dh_ref[...]
        xf = x_ref[...]
        r = _rms(xf)
        xn = xf * r
        one_sc = 1.0 + sc_ref[...]
        gx_ref[...] = dx2_ref[...] + _norm_bwd(dh * one_sc * g_ref[...], xn, r)
        st_ref[0:1, :] += _colsum(dh)
        st_ref[1:2, :] += _colsum(dh * (xn * g_ref[...]))
        st_ref[2:3, :] += _colsum(dh * one_sc * xn)

    return pl.pallas_call(
        kern, name="pre_mix_bwd", grid=(S // tm,),
        in_specs=[_row_spec(tm)] * 3 + [_vec_spec()] * 2,
        out_specs=[_row_spec(tm), _stats_spec()],
        out_shape=[jax.ShapeDtypeStruct((S, D_MODEL), F32), jax.ShapeDtypeStruct((8, D_MODEL), F32)],
        compiler_params=_cparams("arbitrary"),
    )(d_h1, x, d_x2, g1, scale_m)


def _rope(xs, widths, cos_t, sin_t, name):
    S = xs[0].shape[0]
    tm = _tile(S, 512, 8)
    n = len(xs)

    def kern(*refs):
        cos = refs[n][...]
        sin = refs[n + 1][...]
        first = (lax.broadcasted_iota(jnp.int32, cos.shape, 1) % HEAD_DIM) < HEAD_DIM // 2
        for x_ref, o_ref, w in zip(refs[:n], refs[n + 2:], widths):
            for c0 in range(0, w, LANES):
                v = x_ref[:, c0:c0 + LANES]
                partner = jnp.where(first, pltpu.roll(v, LANES - HEAD_DIM // 2, 1),
                                    pltpu.roll(v, HEAD_DIM // 2, 1))
                o_ref[:, c0:c0 + LANES] = (v * cos + partner * sin).astype(BF)

    return pl.pallas_call(
        kern, name=name, grid=(S // tm,),
        in_specs=[_row_spec(tm, w) for w in widths] + [_row_spec(tm, LANES)] * 2,
        out_specs=[_row_spec(tm, w) for w in widths],
        out_shape=[jax.ShapeDtypeStruct((S, w), BF) for w in widths],
        compiler_params=_cparams("parallel"),
    )(*xs, cos_t, sin_t)


def _merge_fwd(pg, pa, pb):
    S = pa.shape[0]
    tm = _tile(S, ROWS, 8)

    def kern(ga_ref, gb_ref, pa_ref, pb_ref, o_ref):
        ga = jax.nn.sigmoid(ga_ref[...].astype(F32))
        gb = jax.nn.sigmoid(gb_ref[...].astype(F32))
        o_ref[...] = (ga * pa_ref[...] + gb * pb_ref[...]).astype(BF)

    return pl.pallas_call(
        kern, name="merge_fwd", grid=(S // tm,),
        in_specs=[_row_spec(tm, col=0), _row_spec(tm, col=1), _row_spec(tm), _row_spec(tm)],
        out_specs=_row_spec(tm),
        out_shape=jax.ShapeDtypeStruct((S, D_MODEL), BF),
        compiler_params=_cparams("parallel"),
    )(pg, pg, pa, pb)


def _merge_bwd(d_merged, pg, pa, pb):
    S = pa.shape[0]
    tm = _tile(S, ROWS, 8)

    def kern(dm_ref, ga_ref, gb_ref, pa_ref, pb_ref, dpa_ref, dpb_ref, dga_ref, dgb_ref):
        dm = dm_ref[...]
        ga = jax.nn.sigmoid(ga_ref[...].astype(F32))
        gb = jax.nn.sigmoid(gb_ref[...].astype(F32))
        dpa_ref[...] = (dm * ga).astype(BF)
        dpb_ref[...] = (dm * gb).astype(BF)
        dga_ref[...] = (dm * pa_ref[...] * ga * (1.0 - ga)).astype(BF)
        dgb_ref[...] = (dm * pb_ref[...] * gb * (1.0 - gb)).astype(BF)

    bf_out = jax.ShapeDtypeStruct((S, D_MODEL), BF)
    return pl.pallas_call(
        kern, name="merge_bwd", grid=(S // tm,),
        in_specs=[_row_spec(tm), _row_spec(tm, col=0), _row_spec(tm, col=1), _row_spec(tm), _row_spec(tm)],
        out_specs=[_row_spec(tm)] * 4,
        out_shape=[bf_out] * 4,
        compiler_params=_cparams("parallel"),
    )(d_merged, pg, pg, pa, pb)


def _swiglu_fwd(gu):
    S = gu.shape[0]
    tm = _tile(S, ROWS, 8)
    tc = _tile(D_FF, 1408)
    nc = D_FF // tc

    def kern(g_ref, u_ref, o_ref):
        g = g_ref[...].astype(F32)
        o_ref[...] = (g * jax.nn.sigmoid(g) * u_ref[...].astype(F32)).astype(BF)

    return pl.pallas_call(
        kern, name="swiglu_fwd", grid=(S // tm, nc),
        in_specs=[pl.BlockSpec((tm, tc), lambda i, j: (i, j)),
                  pl.BlockSpec((tm, tc), lambda i, j: (i, j + nc))],
        out_specs=pl.BlockSpec((tm, tc), lambda i, j: (i, j)),
        out_shape=jax.ShapeDtypeStruct((S, D_FF), BF),
        compiler_params=_cparams("parallel", "parallel"),
    )(gu, gu)


def _swiglu_bwd(d_act, gu):
    S = gu.shape[0]
    tm = _tile(S, 128, 8)

    def kern(da_ref, g_ref, u_ref, o_ref):
        g = g_ref[...].astype(F32)
        u = u_ref[...].astype(F32)
        da = da_ref[...]
        sg = jax.nn.sigmoid(g)
        o_ref[:, :D_FF] = (da * u * (sg * (1.0 + g * (1.0 - sg)))).astype(BF)
        o_ref[:, D_FF:] = (da * (g * sg)).astype(BF)

    return pl.pallas_call(
        kern, name="swiglu_bwd", grid=(S // tm,),
        in_specs=[_row_spec(tm, D_FF), _row_spec(tm, D_FF, 0), _row_spec(tm, D_FF, 1)],
        out_specs=_row_spec(tm, 2 * D_FF),
        out_shape=jax.ShapeDtypeStruct((S, 2 * D_FF), BF),
        compiler_params=_cparams("parallel"),
    )(d_act, gu, gu)


def _split3(x):
    hi = x.astype(BF)
    r1 = x - hi.astype(F32)
    mid = r1.astype(BF)
    lo = (r1 - mid.astype(F32)).astype(BF)
    return hi, mid, lo


def _tri_dot(tri, x):
    return sum(jnp.dot(tri, part, preferred_element_type=F32) for part in _split3(x))


def _log_sigmoid(z):
    return jnp.minimum(z, 0.0) - jnp.log(1.0 + jnp.exp(-jnp.abs(z)))


def _fox_gate_fwd(pa, b_f_pad):
    S = pa.shape[0]
    T = _tile(S, 512, 8)
    f_col = OFF_F // LANES

    def kern(z_ref, b_ref, cum_ref, carry_ref):
        @pl.when(pl.program_id(0) == 0)
        def _():
            carry_ref[...] = jnp.zeros_like(carry_ref)

        log_f = _log_sigmoid(z_ref[...] + b_ref[...])
        row = lax.broadcasted_iota(jnp.int32, (T, T), 0)
        col = lax.broadcasted_iota(jnp.int32, (T, T), 1)
        tri = (col <= row).astype(BF)
        cum = _tri_dot(tri, log_f) + carry_ref[...]
        cum_ref[...] = cum
        carry_ref[...] = cum[T - 1:T, :]

    return pl.pallas_call(
        kern, name="fox_gate_fwd", grid=(S // T,),
        in_specs=[_row_spec(T, LANES, f_col), _vec_spec(LANES)],
        out_specs=_row_spec(T, LANES),
        out_shape=jax.ShapeDtypeStruct((S, LANES), F32),
        scratch_shapes=[pltpu.VMEM((1, LANES), F32)],
        compiler_params=_cparams("arbitrary"),
    )(pa, b_f_pad)


def _fox_gate_bwd(rowsum_ds, colsum_ds, pa, b_f_pad):
    S = pa.shape[0]
    T = _tile(S, 512, 8)
    nb = S // T
    f_col = OFF_F // LANES

    def kern(dr_ref, dc_ref, z_ref, b_ref, df_ref, dbf_ref, carry_ref):
        @pl.when(pl.program_id(0) == 0)
        def _():
            carry_ref[...] = jnp.zeros_like(carry_ref)
            dbf_ref[...] = jnp.zeros_like(dbf_ref)

        row = lax.broadcasted_iota(jnp.int32, (T, T), 0)
        col = lax.broadcasted_iota(jnp.int32, (T, T), 1)
        tri = (col >= row).astype(BF)
        rev = _tri_dot(tri, dr_ref[...] - dc_ref[...]) + carry_ref[...]
        carry_ref[...] = rev[0:1, :]
        z = z_ref[...] + b_ref[...]
        lane = lax.broadcasted_iota(jnp.int32, (T, LANES), 1)
        d_z = jnp.where(lane < B_HEADS, rev * jax.nn.sigmoid(-z), 0.0)
        df_ref[...] = d_z.astype(BF)
        dbf_ref[0:1, :] += _colsum(d_z)

    return pl.pallas_call(
        kern, name="fox_gate_bwd", grid=(nb,),
        in_specs=[pl.BlockSpec((T, LANES), lambda i: (nb - 1 - i, 0)),
                  pl.BlockSpec((T, LANES), lambda i: (nb - 1 - i, 0)),
                  pl.BlockSpec((T, LANES), lambda i: (nb - 1 - i, f_col)),
                  _vec_spec(LANES)],
        out_specs=[pl.BlockSpec((T, LANES), lambda i: (nb - 1 - i, 0)),
                   pl.BlockSpec((8, LANES), lambda i: (0, 0))],
        out_shape=[jax.ShapeDtypeStruct((S, LANES), BF), jax.ShapeDtypeStruct((8, LANES), F32)],
        scratch_shapes=[pltpu.VMEM((1, LANES), F32)],
        compiler_params=_cparams("arbitrary"),
    )(rowsum_ds, colsum_ds, pa, b_f_pad)


NEG_INF = float("-inf")
QK_SCALE = 1.0 / math.sqrt(HEAD_DIM)


def _half_mask(shape, half):
    lane = lax.broadcasted_iota(jnp.int32, shape, 1)
    return (lane < HEAD_DIM) if half == 0 else (lane >= HEAD_DIM)


def _valid(i, j, T, rowcol, window):
    rel = (i - j) * T + rowcol
    ok = rel >= 0
    if window is not None:
        ok = ok & (rel < window)
    return ok


def _attn_fwd(q_arr, q_col, k_arr, k_col, v_arr, v_col, n_pairs, kv_shared, T, window,
              cq_arr, ck_arr, sinks, name, comm=None):
    S = q_arr.shape[0]
    nq = S // T
    use_bias = cq_arr is not None
    use_sink = sinks is not None
    back = 0 if window is None else -(-window // T)
    grid = (n_pairs, nq)
    n_in = 3 + 2 * use_bias + use_sink

    def kern(*refs):
        refs, comm_refs = _own_refs(refs, comm, n_in, 2, 0)
        _comm_edge(comm, comm_refs, grid, first=True)
        q_ref, k_ref, v_ref = refs[:3]
        pos = 3
        if use_bias:
            cq_ref, ck_ref = refs[pos:pos + 2]
            pos += 2
        if use_sink:
            sink_ref = refs[pos]
            pos += 1
        o_ref, lse_ref = refs[pos:pos + 2]
        p_id = pl.program_id(0)
        i = pl.program_id(1)
        q = q_ref[...]
        rowcol = lax.broadcasted_iota(jnp.int32, (T, T), 0) - lax.broadcasted_iota(jnp.int32, (T, T), 1)
        lo = jnp.maximum(i - back, 0) if window is not None else 0
        outs, lses = [], []
        for half in (0, 1):
            hm = _half_mask((T, LANES), half)
            qh = (jnp.where(hm, q, 0).astype(F32) * QK_SCALE).astype(BF)
            if use_bias:
                cq = cq_ref[:, half * HEAD_DIM:half * HEAD_DIM + 1]
            if use_sink:
                m0 = jnp.full((T, 1), sink_ref[2 * p_id + half], F32)
                l0 = jnp.ones((T, 1), F32)
            else:
                m0 = jnp.full((T, 1), NEG_INF, F32)
                l0 = jnp.zeros((T, 1), F32)

            def step(j, carry, masked):
                m, l, acc = carry
                rows = pl.ds(pl.multiple_of(j * T, T), T)
                kj = k_ref[rows, :].astype(BF)
                vj = v_ref[rows, :].astype(BF)
                s = lax.dot_general(qh, kj, (((1,), (1,)), ((), ())), preferred_element_type=F32)
                if use_bias:
                    s = s + cq - ck_ref[0, half:half + 1, rows]
                if masked:
                    s = jnp.where(_valid(i, j, T, rowcol, window), s, NEG_INF)
                m_new = jnp.maximum(m, jnp.max(s, axis=1, keepdims=True))
                alpha = jnp.exp(m - m_new)
                p = jnp.exp(s - m_new)
                l_new = alpha * l + jnp.sum(p, axis=1, keepdims=True)
                acc_new = alpha * acc + jnp.dot(p.astype(BF), vj, preferred_element_type=F32)
                return m_new, l_new, acc_new

            init = (m0, l0, jnp.zeros((T, LANES), F32))
            if window is None:
                init = lax.fori_loop(0, i, functools.partial(step, masked=False), init)
                m, l, acc = step(i, init, True)
            else:
                m, l, acc = lax.fori_loop(lo, i + 1, functools.partial(step, masked=True), init)
            outs.append(acc / l)
            lses.append(m + jnp.log(l))
        hm0 = _half_mask((T, LANES), 0)
        o_ref[...] = jnp.where(hm0, outs[0], outs[1])
        lse_ref[...] = jnp.where(hm0, lses[0], lses[1])
        _comm_edge(comm, comm_refs, grid, first=False)

    kv_idx = (lambda c0: (lambda p, i: (0, c0))) if kv_shared else (lambda c0: (lambda p, i: (0, c0 + p)))
    in_specs = [pl.BlockSpec((T, LANES), lambda p, i: (i, q_col + p)),
                pl.BlockSpec((S, LANES), kv_idx(k_col)),
                pl.BlockSpec((S, LANES), kv_idx(v_col))]
    args = [q_arr, k_arr, v_arr]
    if use_bias:
        in_specs += [pl.BlockSpec((T, LANES), lambda p, i: (i, p)),
                     pl.BlockSpec((1, 2, S), lambda p, i: (p, 0, 0))]
        args += [cq_arr, ck_arr]
    if use_sink:
        in_specs.append(pl.BlockSpec(memory_space=pltpu.SMEM))
        args.append(sinks)
    out_spec = pl.BlockSpec((T, LANES), lambda p, i: (i, p))
    res = pl.pallas_call(
        kern, name=name, grid=grid,
        in_specs=in_specs + _comm_specs(comm, "in"),
        out_specs=[out_spec, out_spec] + _comm_specs(comm, "out"),
        out_shape=[jax.ShapeDtypeStruct((S, n_pairs * LANES), F32)] * 2 + (comm.out_shapes if comm else []),
        scratch_shapes=comm.sem_shapes if comm else [],
        compiler_params=_cparams("arbitrary", "arbitrary"),
    )(*args, *(comm.ins if comm else []))
    return (res[0], res[1], res[2:]) if comm else (res[0], res[1])


def _attn_bwd(q_arr, q_col, k_arr, k_col, v_arr, v_col, o_arr, do_arr, lse_arr, n_pairs, kv_shared, T,
              window, cq_arr, ck_arr, sinks, name, comm=None):
    S = q_arr.shape[0]
    nq = S // T
    use_bias = cq_arr is not None
    use_sink = sinks is not None
    back = 0 if window is None else -(-window // T)
    kv_w = LANES if kv_shared else n_pairs * LANES
    grid = (n_pairs,)
    n_in = 6 + 2 * use_bias + use_sink
    n_out = 3 + 2 * use_bias + use_sink

    def kern(*refs):
        refs, comm_refs = _own_refs(refs, comm, n_in, n_out, 0)
        _comm_edge(comm, comm_refs, grid, first=True)
        q_ref, k_ref, v_ref, o_ref, do_ref, lse_ref = refs[:6]
        pos = 6
        if use_bias:
            cq_ref, ck_ref = refs[pos:pos + 2]
            pos += 2
        if use_sink:
            sink_ref = refs[pos]
            pos += 1
        dq_ref, dk_ref, dv_ref = refs[pos:pos + 3]
        pos += 3
        if use_bias:
            dck_ref, dcq_ref = refs[pos:pos + 2]
            pos += 2
        if use_sink:
            dsink_ref = refs[pos]
        p_id = pl.program_id(0)
        rowcol = lax.broadcasted_iota(jnp.int32, (T, T), 0) - lax.broadcasted_iota(jnp.int32, (T, T), 1)

        def zero_kv():
            dk_ref[...] = jnp.zeros_like(dk_ref)
            dv_ref[...] = jnp.zeros_like(dv_ref)

        if kv_shared:
            pl.when(p_id == 0)(zero_kv)
        else:
            zero_kv()
        if use_bias:
            dck_ref[...] = jnp.zeros_like(dck_ref)
        if use_sink:
            dsink_ref[...] = jnp.zeros_like(dsink_ref)

        for half in (0, 1):
            hm = _half_mask((T, LANES), half)
            lane0 = half * HEAD_DIM

            def outer(i, carry):
                qrows = pl.ds(pl.multiple_of(i * T, T), T)
                qh = (jnp.where(hm, q_ref[qrows, :], 0).astype(F32) * QK_SCALE).astype(BF)
                do_f = jnp.where(hm, do_ref[qrows, :], 0.0)
                doh = do_f.astype(BF)
                delta = jnp.sum(do_f * o_ref[qrows, :], axis=1, keepdims=True)
                lse = lse_ref[qrows, lane0:lane0 + 1]
                if use_bias:
                    cq = cq_ref[qrows, lane0:lane0 + 1]
                lo = jnp.maximum(i - back, 0) if window is not None else 0

                def inner(j, carry_in, masked):
                    dq, rs = carry_in
                    krows = pl.ds(pl.multiple_of(j * T, T), T)
                    kj = k_ref[krows, :].astype(BF)
                    vj = v_ref[krows, :].astype(BF)
                    s = lax.dot_general(qh, kj, (((1,), (1,)), ((), ())), preferred_element_type=F32)
                    if use_bias:
                        s = s + cq - ck_ref[0, half:half + 1, krows]
                    if masked:
                        s = jnp.where(_valid(i, j, T, rowcol, window), s, NEG_INF)
                    p = jnp.exp(s - lse)
                    dp = lax.dot_general(doh, vj, (((1,), (1,)), ((), ())), preferred_element_type=F32)
                    ds = p * (dp - delta)
                    ds_b = ds.astype(BF)
                    dv_ref[krows, :] += lax.dot_general(p.astype(BF), doh, (((0,), (0,)), ((), ())),
                                                        preferred_element_type=F32)
                    dk_ref[krows, :] += lax.dot_general(ds_b, qh, (((0,), (0,)), ((), ())),
                                                        preferred_element_type=F32)
                    if use_bias:
                        dck_ref[0, half:half + 1, krows] += jnp.sum(ds, axis=0, keepdims=True)
                        rs = rs + jnp.sum(ds, axis=1, keepdims=True)
                    kh = jnp.where(hm, kj, 0)
                    return dq + jnp.dot(ds_b, kh, preferred_element_type=F32), rs

                init = (jnp.zeros((T, LANES), F32), jnp.zeros((T, 1), F32))
                if window is None:
                    init = lax.fori_loop(0, i, functools.partial(inner, masked=False), init)
                    dq, rs = inner(i, init, True)
                else:
                    dq, rs = lax.fori_loop(lo, i + 1, functools.partial(inner, masked=True), init)
                dq = dq * QK_SCALE
                if half == 0:
                    dq_ref[qrows, :] = dq
                else:
                    dq_ref[qrows, :] += dq
                if use_bias:
                    rs_b = jnp.broadcast_to(rs, (T, LANES))
                    dcq_ref[qrows, :] = rs_b if half == 0 else jnp.where(hm, rs_b, dcq_ref[qrows, :])
                if use_sink:
                    p_sink = jnp.exp(sink_ref[2 * p_id + half] - lse)
                    dsink_ref[0, half:half + 1, :] += jnp.broadcast_to(
                        -jnp.sum(p_sink * delta, axis=0, keepdims=True), (1, LANES))
                return carry

            lax.fori_loop(0, nq, outer, 0)
        _comm_edge(comm, comm_refs, grid, first=False)

    kv_idx = (lambda c0: (lambda p: (0, c0))) if kv_shared else (lambda c0: (lambda p: (0, c0 + p)))
    pair = lambda c0: pl.BlockSpec((S, LANES), lambda p: (0, c0 + p))
    in_specs = [pair(q_col), pl.BlockSpec((S, LANES), kv_idx(k_col)), pl.BlockSpec((S, LANES), kv_idx(v_col)),
                pair(0), pair(0), pair(0)]
    args = [q_arr, k_arr, v_arr, o_arr, do_arr, lse_arr]
    if use_bias:
        in_specs += [pair(0), pl.BlockSpec((1, 2, S), lambda p: (p, 0, 0))]
        args += [cq_arr, ck_arr]
    if use_sink:
        in_specs.append(pl.BlockSpec(memory_space=pltpu.SMEM))
        args.append(sinks)
    out_specs = [pair(0), pl.BlockSpec((S, LANES), kv_idx(0)), pl.BlockSpec((S, LANES), kv_idx(0))]
    out_shape = [jax.ShapeDtypeStruct((S, n_pairs * LANES), F32),
                 jax.ShapeDtypeStruct((S, kv_w), F32), jax.ShapeDtypeStruct((S, kv_w), F32)]
    if use_bias:
        out_specs += [pl.BlockSpec((1, 2, S), lambda p: (p, 0, 0)), pair(0)]
        out_shape += [jax.ShapeDtypeStruct((n_pairs, 2, S), F32), jax.ShapeDtypeStruct((S, n_pairs * LANES), F32)]
    if use_sink:
        out_specs.append(pl.BlockSpec((1, 8, LANES), lambda p: (p, 0, 0)))
        out_shape.append(jax.ShapeDtypeStruct((n_pairs, 8, LANES), F32))
    res = pl.pallas_call(
        kern, name=name, grid=grid,
        in_specs=in_specs + _comm_specs(comm, "in"),
        out_specs=out_specs + _comm_specs(comm, "out"),
        out_shape=out_shape + (comm.out_shapes if comm else []),
        scratch_shapes=comm.sem_shapes if comm else [],
        compiler_params=_cparams("arbitrary"),
    )(*args, *(comm.ins if comm else []))
    return (*res[:n_out], res[n_out:]) if comm else res


def _bias_lanes(shape, half, q_side_terms, k_side_terms):
    lane = lax.broadcasted_iota(jnp.int32, shape, 1)
    base = HEAD_DIM * (1 - half)
    n_q = len(q_side_terms) if q_side_terms is not None else 3
    n_k = len(k_side_terms) if k_side_terms is not None else 3
    out = jnp.zeros(shape, F32)
    for t in range(n_q):
        out = jnp.where(lane == base + t, q_side_terms[t].astype(F32) if q_side_terms is not None else 1.0, out)
    for t in range(n_k):
        out = jnp.where(lane == base + n_q + t,
                        k_side_terms[t].astype(F32) if k_side_terms is not None else 1.0, out)
    return out


def _head_column(block, head):
    lane = lax.broadcasted_iota(jnp.int32, block.shape, 1)
    return jnp.sum(jnp.where(lane == head, block, 0.0), axis=1, keepdims=True)


def _fox_prep_fwd(p_b, cum, T):
    S = p_b.shape[0]

    def kern(q_ref, k_ref, c_ref, qa_ref, ka_ref):
        p_id = pl.program_id(0)
        q, k, cum_blk = q_ref[...], k_ref[...], c_ref[...]
        for half in (0, 1):
            hm = _half_mask((T, LANES), half)
            c3 = _split3(_head_column(cum_blk, 2 * p_id + half))
            qa_ref[half] = jnp.where(hm, q.astype(F32) * QK_SCALE, _bias_lanes((T, LANES), half, c3, None)).astype(BF)
            ka_ref[half] = jnp.where(hm, k.astype(F32),
                                     _bias_lanes((T, LANES), half, None, [-t.astype(F32) for t in c3])).astype(BF)

    out_spec = pl.BlockSpec((None, 2, T, LANES), lambda p, i: (p, 0, i, 0))
    shape = jax.ShapeDtypeStruct((B_HEADS // 2, 2, S, LANES), BF)
    return pl.pallas_call(
        kern, name="fox_prep_fwd", grid=(B_HEADS // 2, S // T),
        in_specs=[pl.BlockSpec((T, LANES), lambda p, i: (i, OFF_QB // LANES + p)),
                  pl.BlockSpec((T, LANES), lambda p, i: (i, OFF_KB // LANES + p)),
                  pl.BlockSpec((T, LANES), lambda p, i: (i, 0))],
        out_specs=[out_spec, out_spec], out_shape=[shape, shape],
        compiler_params=_cparams("parallel", "parallel"),
    )(p_b, p_b, cum)


def _fox_fwd(q_aug, k_aug, p_b, T, comm=None):
    S = p_b.shape[0]
    nq = S // T
    n_pairs = B_HEADS // 2
    grid = (n_pairs, nq)

    def kern(*refs):
        (q_ref, k_ref, v_ref, o_ref, lse_ref), comm_refs = _own_refs(refs, comm, 3, 2, 0)
        _comm_edge(comm, comm_refs, grid, first=True)
        i = pl.program_id(1)
        rowcol = lax.broadcasted_iota(jnp.int32, (T, T), 0) - lax.broadcasted_iota(jnp.int32, (T, T), 1)
        qs = (q_ref[0], q_ref[1])

        def step(j, carry, masked):
            rows = pl.ds(pl.multiple_of(j * T, T), T)
            vj = v_ref[rows, :]
            new = []
            for half in (0, 1):
                m, l, acc = carry[half]
                s = lax.dot_general(qs[half], k_ref[half, rows, :], (((1,), (1,)), ((), ())),
                                    preferred_element_type=F32)
                if masked:
                    s = jnp.where(rowcol >= 0, s, NEG_INF)
                m_new = jnp.maximum(m, jnp.max(s, axis=1, keepdims=True))
                alpha = jnp.exp(m - m_new)
                p = jnp.exp(s - m_new)
                l_new = alpha * l + jnp.sum(p, axis=1, keepdims=True)
                acc_new = alpha * acc + jnp.dot(p.astype(BF), vj, preferred_element_type=F32)
                new.append((m_new, l_new, acc_new))
            return tuple(new)

        one = (jnp.full((T, 1), NEG_INF, F32), jnp.zeros((T, 1), F32), jnp.zeros((T, LANES), F32))
        carry = lax.fori_loop(0, i, functools.partial(step, masked=False), (one, one))
        (m0, l0, acc0), (m1, l1, acc1) = step(i, carry, True)
        hm0 = _half_mask((T, LANES), 0)
        o_ref[...] = jnp.where(hm0, acc0 / l0, acc1 / l1)
        lse_ref[...] = jnp.where(hm0, m0 + jnp.log(l0), m1 + jnp.log(l1))
        _comm_edge(comm, comm_refs, grid, first=False)

    out_spec = pl.BlockSpec((T, LANES), lambda p, i: (i, p))
    res = pl.pallas_call(
        kern, name="fox_fwd", grid=grid,
        in_specs=[pl.BlockSpec((None, 2, T, LANES), lambda p, i: (p, 0, i, 0)),
                  pl.BlockSpec((None, 2, S, LANES), lambda p, i: (p, 0, 0, 0)),
                  pl.BlockSpec((S, LANES), lambda p, i: (0, OFF_VB // LANES + p))] + _comm_specs(comm, "in"),
        out_specs=[out_spec, out_spec] + _comm_specs(comm, "out"),
        out_shape=[jax.ShapeDtypeStruct((S, n_pairs * LANES), F32)] * 2 + (comm.out_shapes if comm else []),
        scratch_shapes=comm.sem_shapes if comm else [],
        compiler_params=_cparams("arbitrary", "arbitrary"),
    )(q_aug, k_aug, p_b, *(comm.ins if comm else []))
    return res[0], res[1], res[2:]


def _fox_prep_bwd(q_aug, p_b, o, do, lse, T):
    S = p_b.shape[0]

    def kern(qa_ref, v_ref, o_ref, do_ref, lse_ref, qb_ref, dob_ref, vb_ref):
        v, o_blk, do_blk, lse_blk = v_ref[...], o_ref[...], do_ref[...], lse_ref[...]
        lane = lax.broadcasted_iota(jnp.int32, (T, LANES), 1)
        for half in (0, 1):
            hm = _half_mask((T, LANES), half)
            base = HEAD_DIM * (1 - half)
            qa = qa_ref[half].astype(F32)
            cq = jnp.sum(jnp.where((lane >= base) & (lane < base + 3), qa, 0.0), axis=1, keepdims=True)
            b3 = _split3(cq - lse_blk[:, HEAD_DIM * half:HEAD_DIM * half + 1])
            qb_ref[half] = jnp.where(hm, qa, _bias_lanes((T, LANES), half, b3, None)).astype(BF)
            do_f = jnp.where(hm, do_blk, 0.0)
            d3 = _split3(-jnp.sum(do_f * o_blk, axis=1, keepdims=True))
            dob_ref[half] = jnp.where(hm, do_f, _bias_lanes((T, LANES), half, d3, [])).astype(BF)
            vb_ref[half] = jnp.where(hm, v.astype(F32), _bias_lanes((T, LANES), half, None, [])).astype(BF)

    aug = pl.BlockSpec((None, 2, T, LANES), lambda p, i: (p, 0, i, 0))
    tile = pl.BlockSpec((T, LANES), lambda p, i: (i, p))
    shape = jax.ShapeDtypeStruct((B_HEADS // 2, 2, S, LANES), BF)
    return pl.pallas_call(
        kern, name="fox_prep_bwd", grid=(B_HEADS // 2, S // T),
        in_specs=[aug, pl.BlockSpec((T, LANES), lambda p, i: (i, OFF_VB // LANES + p)), tile, tile, tile],
        out_specs=[aug, aug, aug], out_shape=[shape, shape, shape],
        compiler_params=_cparams("parallel", "parallel"),
    )(q_aug, p_b, o, do, lse)


def _fox_bwd(qb_aug, k_aug, dob_aug, vb_aug, T, comm=None):
    n_pairs, _, S, _ = qb_aug.shape
    nq = S // T
    grid = (n_pairs,)

    def kern(*refs):
        own, comm_refs = _own_refs(refs, comm, 4, 5, 0)
        q_ref, k_ref, do_ref, v_ref, dq_ref, dk_ref, dv_ref, dck_ref, dcq_ref = own
        _comm_edge(comm, comm_refs, grid, first=True)
        p_id = pl.program_id(0)
        rowcol = lax.broadcasted_iota(jnp.int32, (T, T), 0) - lax.broadcasted_iota(jnp.int32, (T, T), 1)
        lane = lax.broadcasted_iota(jnp.int32, (T, LANES), 1)
        dk_ref[...] = jnp.zeros_like(dk_ref)
        dv_ref[...] = jnp.zeros_like(dv_ref)
        dck_ref[...] = jnp.zeros_like(dck_ref)

        @pl.when(p_id == 0)
        def _():
            dcq_ref[...] = jnp.zeros_like(dcq_ref)

        hms = (_half_mask((T, LANES), 0), _half_mask((T, LANES), 1))

        def outer(i, carry):
            qrows = pl.ds(pl.multiple_of(i * T, T), T)
            qa = (q_ref[0, qrows, :], q_ref[1, qrows, :])
            doa = (do_ref[0, qrows, :], do_ref[1, qrows, :])
            q_own = [jnp.where(hms[h], qa[h], 0) for h in (0, 1)]
            do_own = [jnp.where(hms[h], doa[h], 0) for h in (0, 1)]

            def inner(j, carry_in, masked):
                krows = pl.ds(pl.multiple_of(j * T, T), T)
                dv_add, dk_add, new = 0.0, 0.0, []
                for half in (0, 1):
                    dq, rs = carry_in[half]
                    ka = k_ref[half, krows, :]
                    s = lax.dot_general(qa[half], ka, (((1,), (1,)), ((), ())), preferred_element_type=F32)
                    if masked:
                        s = jnp.where(rowcol >= 0, s, NEG_INF)
                    p = jnp.exp(s)
                    ds = p * lax.dot_general(doa[half], v_ref[half, krows, :], (((1,), (1,)), ((), ())),
                                             preferred_element_type=F32)
                    ds_b = ds.astype(BF)
                    dv_add = dv_add + lax.dot_general(p.astype(BF), do_own[half], (((0,), (0,)), ((), ())),
                                                      preferred_element_type=F32)
                    dk_add = dk_add + lax.dot_general(ds_b, q_own[half], (((0,), (0,)), ((), ())),
                                                      preferred_element_type=F32)
                    dck_ref[half:half + 1, krows] += jnp.sum(ds, axis=0, keepdims=True)
                    new.append((dq + jnp.dot(ds_b, jnp.where(hms[half], ka, 0), preferred_element_type=F32),
                                rs + jnp.sum(ds, axis=1, keepdims=True)))
                dv_ref[krows, :] += dv_add
                dk_ref[krows, :] += dk_add
                return tuple(new)

            one = (jnp.zeros((T, LANES), F32), jnp.zeros((T, 1), F32))
            carry_in = lax.fori_loop(0, i, functools.partial(inner, masked=False), (one, one))
            (dq0, rs0), (dq1, rs1) = inner(i, carry_in, True)
            dq_ref[qrows, :] = (dq0 + dq1) * QK_SCALE
            dcq_ref[qrows, :] = jnp.where(lane == 2 * p_id, rs0, jnp.where(lane == 2 * p_id + 1, rs1,
                                                                             dcq_ref[qrows, :]))
            return carry

        lax.fori_loop(0, nq, outer, 0)
        _comm_edge(comm, comm_refs, grid, first=False)

    aug = pl.BlockSpec((None, 2, S, LANES), lambda p: (p, 0, 0, 0))
    pair = pl.BlockSpec((S, LANES), lambda p: (0, p))
    wide = jax.ShapeDtypeStruct((S, n_pairs * LANES), F32)
    res = pl.pallas_call(
        kern, name="fox_bwd", grid=grid,
        in_specs=[aug, aug, aug, aug] + _comm_specs(comm, "in"),
        out_specs=[pair, pair, pair, pl.BlockSpec((None, 2, S), lambda p: (p, 0, 0)),
                   pl.BlockSpec((S, LANES), lambda p: (0, 0))] + _comm_specs(comm, "out"),
        out_shape=[wide, wide, wide, jax.ShapeDtypeStruct((n_pairs, 2, S), F32),
                   jax.ShapeDtypeStruct((S, LANES), F32)] + (comm.out_shapes if comm else []),
        scratch_shapes=comm.sem_shapes if comm else [],
        compiler_params=_cparams("arbitrary"),
    )(qb_aug, k_aug, dob_aug, vb_aug, *(comm.ins if comm else []))
    return (*res[:5], res[5:])


SWA_TQ = 256
SWA_SUB = 4


def _swa_window(i, tq):
    start = pl.multiple_of(jnp.maximum(i * tq - WINDOW, 0), LANES)
    return start, i * tq - start


def _swa_valid(offset, tq):
    rel = offset + lax.broadcasted_iota(jnp.int32, (tq, tq + WINDOW), 0) \
        - lax.broadcasted_iota(jnp.int32, (tq, tq + WINDOW), 1)
    return (rel >= 0) & (rel < WINDOW)


def _swa_fwd(qk, v_arr, v_col, sinks):
    S = qk.shape[0]
    tq = min(SWA_TQ, S - WINDOW)
    sub = min(SWA_SUB, S // tq)
    win = tq + WINDOW

    def kern(q_ref, k_ref, v_ref, sink_ref, o_ref, lse_ref):
        p_id, i = pl.program_id(0), pl.program_id(1)
        hm0 = _half_mask((tq, LANES), 0)
        for t in range(sub):
            rows = slice(t * tq, (t + 1) * tq)
            start, offset = _swa_window(i * sub + t, tq)
            kw = k_ref[pl.ds(start, win), :]
            vw = v_ref[pl.ds(start, win), :].astype(BF)
            valid = _swa_valid(offset, tq)
            q = q_ref[rows, :]
            outs, lses = [], []
            for half in (0, 1):
                hm = _half_mask((tq, LANES), half)
                qh = (jnp.where(hm, q, 0).astype(F32) * QK_SCALE).astype(BF)
                s = lax.dot_general(qh, kw, (((1,), (1,)), ((), ())), preferred_element_type=F32)
                s = jnp.where(valid, s, NEG_INF)
                sink = sink_ref[2 * p_id + half]
                m = jnp.maximum(jnp.max(s, axis=1, keepdims=True), sink)
                p = jnp.exp(s - m)
                denom = jnp.sum(p, axis=1, keepdims=True) + jnp.exp(sink - m)
                outs.append(jnp.dot(p.astype(BF), vw, preferred_element_type=F32) / denom)
                lses.append(m + jnp.log(denom))
            o_ref[rows, :] = jnp.where(hm0, outs[0], outs[1])
            lse_ref[rows, :] = jnp.where(hm0, lses[0], lses[1])

    tile = pl.BlockSpec((sub * tq, LANES), lambda p, i: (i, p))
    return pl.pallas_call(
        kern, name="swa_fwd", grid=(A_Q_HEADS // 2, S // (sub * tq)),
        in_specs=[tile, pl.BlockSpec((S, LANES), lambda p, i: (0, A_Q_HEADS // 2)),
                  pl.BlockSpec((S, LANES), lambda p, i: (0, v_col)),
                  pl.BlockSpec(memory_space=pltpu.SMEM)],
        out_specs=[tile, tile],
        out_shape=[jax.ShapeDtypeStruct((S, A_Q_HEADS * HEAD_DIM), F32)] * 2,
        compiler_params=_cparams("parallel", "arbitrary"),
    )(qk, qk, v_arr, sinks)


def _swa_bwd(qk, v_arr, v_col, o_arr, do_arr, lse_arr, sinks, comm=None):
    S = qk.shape[0]
    tq = min(SWA_TQ, S - WINDOW)
    sub = min(SWA_SUB, S // tq)
    win = tq + WINDOW
    n_pairs = A_Q_HEADS // 2
    grid = (n_pairs, S // (sub * tq))

    def kern(*refs):
        own, comm_refs = _own_refs(refs, comm, 7, 4, 0)
        q_ref, k_ref, v_ref, o_ref, do_ref, lse_ref, sink_ref, dq_ref, dk_ref, dv_ref, dsink_ref = own
        _comm_edge(comm, comm_refs, grid, first=True)
        p_id, i = pl.program_id(0), pl.program_id(1)

        @pl.when((p_id == 0) & (i == 0))
        def _():
            dk_ref[...] = jnp.zeros_like(dk_ref)
            dv_ref[...] = jnp.zeros_like(dv_ref)

        @pl.when(i == 0)
        def _():
            dsink_ref[...] = jnp.zeros_like(dsink_ref)

        for t in range(sub):
            rows = slice(t * tq, (t + 1) * tq)
            start, offset = _swa_window(i * sub + t, tq)
            wrows = pl.ds(start, win)
            kw = k_ref[wrows, :]
            vw = v_ref[wrows, :].astype(BF)
            valid = _swa_valid(offset, tq)
            q, do, o, lse2 = q_ref[rows, :], do_ref[rows, :], o_ref[rows, :], lse_ref[rows, :]
            dq = jnp.zeros((tq, LANES), F32)
            dk = jnp.zeros((win, LANES), F32)
            dv = jnp.zeros((win, LANES), F32)
            for half in (0, 1):
                hm = _half_mask((tq, LANES), half)
                lane0 = half * HEAD_DIM
                qh = (jnp.where(hm, q, 0).astype(F32) * QK_SCALE).astype(BF)
                do_f = jnp.where(hm, do, 0.0)
                doh = do_f.astype(BF)
                delta = jnp.sum(do_f * o, axis=1, keepdims=True)
                lse = lse2[:, lane0:lane0 + 1]
                s = lax.dot_general(qh, kw, (((1,), (1,)), ((), ())), preferred_element_type=F32)
                p = jnp.exp(jnp.where(valid, s, NEG_INF) - lse)
                dp = lax.dot_general(doh, vw, (((1,), (1,)), ((), ())), preferred_element_type=F32)
                ds_b = (p * (dp - delta)).astype(BF)
                dv = dv + lax.dot_general(p.astype(BF), doh, (((0,), (0,)), ((), ())),
                                          preferred_element_type=F32)
                dk = dk + lax.dot_general(ds_b, qh, (((0,), (0,)), ((), ())), preferred_element_type=F32)
                kh = jnp.where(_half_mask((win, LANES), half), kw, 0)
                dq = dq + jnp.dot(ds_b, kh, preferred_element_type=F32)
                p_sink = jnp.exp(sink_ref[2 * p_id + half] - lse)
                dsink_ref[0, half:half + 1, :] += jnp.broadcast_to(
                    -jnp.sum(p_sink * delta, axis=0, keepdims=True), (1, LANES))
            dq_ref[rows, :] = dq * QK_SCALE
            dk_ref[wrows, :] += dk
            dv_ref[wrows, :] += dv
        _comm_edge(comm, comm_refs, grid, first=False)

    tile = pl.BlockSpec((sub * tq, LANES), lambda p, i: (i, p))
    whole = lambda col: pl.BlockSpec((S, LANES), lambda p, i: (0, col))
    res = pl.pallas_call(
        kern, name="swa_bwd", grid=grid,
        in_specs=[tile, whole(n_pairs), whole(v_col), tile, tile, tile,
                  pl.BlockSpec(memory_space=pltpu.SMEM)] + _comm_specs(comm, "in"),
        out_specs=[tile, whole(0), whole(0),
                   pl.BlockSpec((1, 8, LANES), lambda p, i: (p, 0, 0))] + _comm_specs(comm, "out"),
        out_shape=[jax.ShapeDtypeStruct((S, A_Q_HEADS * HEAD_DIM), F32),
                   jax.ShapeDtypeStruct((S, LANES), F32), jax.ShapeDtypeStruct((S, LANES), F32),
                   jax.ShapeDtypeStruct((n_pairs, 8, LANES), F32)] + (comm.out_shapes if comm else []),
        scratch_shapes=comm.sem_shapes if comm else [],
        compiler_params=_cparams("arbitrary", "arbitrary"),
    )(qk, qk, v_arr, o_arr, do_arr, lse_arr, sinks, *(comm.ins if comm else []))
    return (*res[:4], res[4:])


ADAMW_BLOCK = 256 * 1024


def _adamw(w, g, m, v, name):
    R, C = w.shape
    tr, tc = _tile(R, max(8, ADAMW_BLOCK // C), 8), C

    def kern(w_ref, g_ref, m_ref, v_ref, d_ref, mo_ref, vo_ref):
        g_ = g_ref[...]
        m_new = ADAM_B1 * m_ref[...] + (1.0 - ADAM_B1) * g_
        v_new = ADAM_B2 * v_ref[...] + (1.0 - ADAM_B2) * (g_ * g_)
        m_hat = m_new / (1.0 - ADAM_B1 ** ADAM_STEP)
        v_hat = v_new / (1.0 - ADAM_B2 ** ADAM_STEP)
        d_ref[...] = -ADAM_LR * (m_hat / (jnp.sqrt(v_hat) + ADAM_EPS) + ADAM_WD * w_ref[...])
        mo_ref[...] = m_new
        vo_ref[...] = v_new

    spec = pl.BlockSpec((tr, tc), lambda i, j: (i, j))
    shape = jax.ShapeDtypeStruct((R, C), F32)
    return pl.pallas_call(
        kern, name=name, grid=(R // tr, C // tc),
        in_specs=[spec] * 4, out_specs=[spec] * 3, out_shape=[shape] * 3,
        compiler_params=_cparams("parallel", "parallel"),
    )(w, g, m, v)


def _index_operand(i):
    return jnp.reshape(i, (1,)).astype(jnp.int32)


def _add_pair(whole, got, ci, name):
    P, R, C = whole.shape
    half = R // 2
    tr = _tile(half, 256, 16)
    nb = half // tr

    def kern(ci_ref, a_ref, b_ref, o_ref, ob_ref):
        s = a_ref[...] + b_ref[...].astype(F32)
        o_ref[...] = s
        ob_ref[...] = s.astype(BF)

    spec = pl.BlockSpec((None, tr, C), lambda p, i, ci_ref: (p, i, 0))
    return pl.pallas_call(
        kern, name=name,
        grid_spec=pltpu.PrefetchScalarGridSpec(
            num_scalar_prefetch=1, grid=(P, nb),
            in_specs=[pl.BlockSpec((None, tr, C), lambda p, i, ci_ref: (p, ci_ref[0] * nb + i, 0)), spec],
            out_specs=[spec, spec]),
        out_shape=[jax.ShapeDtypeStruct((P, half, C), F32), jax.ShapeDtypeStruct((P, half, C), BF)],
        compiler_params=_cparams("parallel", "parallel"),
    )(_index_operand(ci), whole, got)


def _add_three(parts, recv, chip, name):
    _, R, C = parts.shape
    tr = _tile(R, 256, 16)

    def kern(chip_ref, o_ref, r0_ref, r1_ref, r2_ref, out_ref):
        s = ((o_ref[...] + r0_ref[...].astype(F32)) + r1_ref[...].astype(F32)) + r2_ref[...].astype(F32)
        out_ref[0] = s
        out_ref[1] = s

    slab = lambda k: pl.BlockSpec((None, tr, C), lambda i, chip_ref: (k, i, 0))
    return pl.pallas_call(
        kern, name=name,
        grid_spec=pltpu.PrefetchScalarGridSpec(
            num_scalar_prefetch=1, grid=(R // tr,),
            in_specs=[pl.BlockSpec((None, tr, C), lambda i, chip_ref: (chip_ref[0], i, 0)),
                      slab(0), slab(1), slab(2)],
            out_specs=pl.BlockSpec((2, tr, C), lambda i, chip_ref: (0, i, 0))),
        out_shape=jax.ShapeDtypeStruct((2, R, C), F32),
        compiler_params=_cparams("parallel"),
    )(_index_operand(chip), parts, recv, recv, recv)


SM_ADA, SM_G, SM_LOSS, SM_BF, SM_SINK, SM_LEN = 0, 6144, 10240, 11264, 11272, 12288


def _small_finalize(gathered):
    def kern(g_ref, tot_ref, loss_ref):
        tot = g_ref[0:1, :]
        for b in range(1, N_DEV):
            tot = tot + g_ref[b:b + 1, :]
        tot_ref[...] = tot
        sq = jnp.sum(tot[:, SM_LOSS:SM_LOSS + D_MODEL], axis=1, keepdims=True)
        loss_ref[...] = jnp.broadcast_to(sq * (0.5 / D_MODEL), (1, LANES))

    full = lambda shape: pl.BlockSpec(shape, lambda i: (0, 0))
    return pl.pallas_call(
        kern, name="small_finalize", grid=(1,),
        in_specs=[full((N_DEV, SM_LEN))],
        out_specs=[full((1, SM_LEN)), full((1, LANES))],
        out_shape=[jax.ShapeDtypeStruct((1, SM_LEN), F32), jax.ShapeDtypeStruct((1, LANES), F32)],
        compiler_params=_cparams("arbitrary"),
    )(gathered)


def _ada_dw(c_t, d_ada):
    N = d_ada.shape[1]
    tn = _tile(N, 512)

    def kern(c_ref, d_ref, o_ref):
        acc = c_ref[:, 0:1] * d_ref[0:1, :]
        for b in range(1, N_DEV):
            acc = acc + c_ref[:, b:b + 1] * d_ref[b:b + 1, :]
        o_ref[...] = acc

    return pl.pallas_call(
        kern, name="ada_dw", grid=(N // tn,),
        in_specs=[pl.BlockSpec((D_MODEL, N_DEV), lambda j: (0, 0)), pl.BlockSpec((N_DEV, tn), lambda j: (0, j))],
        out_specs=pl.BlockSpec((D_MODEL, tn), lambda j: (0, j)),
        out_shape=jax.ShapeDtypeStruct((D_MODEL, N), F32),
        compiler_params=_cparams("parallel"),
    )(c_t, d_ada)


def _here():
    return lax.axis_index("x"), lax.axis_index("y"), lax.axis_index("c")


def _other_chips(x, y):
    return [(1 - x, y), (x, 1 - y), (1 - x, 1 - y)]


_ANY = pl.BlockSpec(memory_space=pl.ANY)


class _Comm:
    def __init__(self, ins, out_shapes, sem_shapes, start, finish):
        self.ins, self.out_shapes, self.sem_shapes = list(ins), list(out_shapes), list(sem_shapes)
        self.start, self.finish = start, finish

    def split(self, refs, n_in, n_out, n_scratch):
        a = n_in + len(self.ins)
        b = a + n_out + len(self.out_shapes)
        own = list(refs[:n_in]) + list(refs[a:a + n_out]) + list(refs[b:b + n_scratch])
        mine = (refs[n_in:a], refs[a + n_out:b], refs[b + n_scratch:])
        return own, mine


def _run_comm(comm, name):
    n_in, n_out = len(comm.ins), len(comm.out_shapes)

    def body(*refs):
        parts = (refs[:n_in], refs[n_in:n_in + n_out], refs[n_in + n_out:])
        comm.start(*parts)
        comm.finish(*parts)

    return pl.pallas_call(
        body, name=name,
        in_specs=[_ANY] * n_in, out_specs=[_ANY] * n_out,
        out_shape=comm.out_shapes, scratch_shapes=comm.sem_shapes,
    )(*comm.ins)


def _gather_comm(blocks):
    L = len(blocks)

    def parts(ins, outs, sems):
        send_sems, recv_sems, local_sems = sems
        x, y, c = _here()
        me, sibling = (x, y, c), (x, y, 1 - c)
        chips = _other_chips(x, y)

        def slot(px, py, pc):
            return 4 * px + 2 * py + pc

        def copy(l, k, block, to, src=None):
            dst = outs[l].at[slot(*block)]
            return pltpu.make_async_remote_copy(
                src_ref=dst if src is None else src, dst_ref=dst,
                send_sem=send_sems.at[l, k], recv_sem=recv_sems.at[l, k],
                device_id=to, device_id_type=MESH)

        mine = [pltpu.make_async_copy(ins[l], outs[l].at[slot(*me)], local_sems.at[l]) for l in range(L)]
        first = []
        for l in range(L):
            first.append(copy(l, 0, me, sibling, src=ins[l]))
            for j, chip in enumerate(chips):
                first.append(copy(l, 1 + j, me, (*chip, c), src=ins[l]))
        return c, me, sibling, chips, copy, mine, first

    def start(ins, outs, sems):
        *_, mine, first = parts(ins, outs, sems)
        for cp in mine + first:
            cp.start()

    def finish(ins, outs, sems):
        c, me, sibling, chips, copy, mine, first = parts(ins, outs, sems)
        passed = []
        for j, chip in enumerate(chips):
            for l in range(L):
                copy(l, 1 + j, (*chip, c), me).wait_recv()
                fwd = copy(l, 4 + j, (*chip, c), sibling)
                fwd.start()
                passed.append(fwd)
        for l in range(L):
            copy(l, 0, sibling, me).wait_recv()
        for j, chip in enumerate(chips):
            for l in range(L):
                copy(l, 4 + j, (*chip, 1 - c), me).wait_recv()
        for cp in first + passed:
            cp.wait_send()
        for cp in mine:
            cp.wait()

    return _Comm(blocks, [jax.ShapeDtypeStruct((N_DEV,) + b.shape, b.dtype) for b in blocks],
                 [pltpu.SemaphoreType.DMA((L, 7)), pltpu.SemaphoreType.DMA((L, 7)), pltpu.SemaphoreType.DMA((L,))],
                 start, finish)


def _allgather8(blocks, name):
    return _run_comm(_gather_comm(blocks), name)


def _swap_comm(arrs):
    L = len(arrs)

    def copies(ins, outs, sems):
        send_sems, recv_sems = sems
        x, y, c = _here()
        cps = []
        for l in range(L):
            half = arrs[l].shape[1] // 2
            rows = pl.ds(pl.multiple_of((1 - c) * half, 16), half)
            cps.append(pltpu.make_async_remote_copy(
                src_ref=ins[l].at[:, rows, :], dst_ref=outs[l], send_sem=send_sems.at[l],
                recv_sem=recv_sems.at[l], device_id=(x, y, 1 - c), device_id_type=MESH))
        return cps

    def start(ins, outs, sems):
        for cp in copies(ins, outs, sems):
            cp.start()

    def finish(ins, outs, sems):
        for cp in copies(ins, outs, sems):
            cp.wait()

    return _Comm(arrs, [jax.ShapeDtypeStruct((a.shape[0], a.shape[1] // 2, a.shape[2]), a.dtype) for a in arrs],
                 [pltpu.SemaphoreType.DMA((L,)), pltpu.SemaphoreType.DMA((L,))], start, finish)


def _sibling_join(bufs, name):
    L = len(bufs)

    def body(*refs):
        outs = refs[L:2 * L]
        send_sems, recv_sems = refs[2 * L:]
        x, y, c = _here()
        for l in range(L):
            pltpu.make_async_remote_copy(src_ref=outs[l].at[c], dst_ref=outs[l].at[c], send_sem=send_sems.at[l],
                                         recv_sem=recv_sems.at[l], device_id=(x, y, 1 - c),
                                         device_id_type=MESH).start()
        for l in range(L):
            pltpu.make_async_remote_copy(src_ref=outs[l].at[c], dst_ref=outs[l].at[1 - c],
                                         send_sem=send_sems.at[l], recv_sem=recv_sems.at[l],
                                         device_id=(x, y, 1 - c), device_id_type=MESH).wait()

    return pl.pallas_call(
        body, name=name,
        in_specs=[_ANY] * L, out_specs=[_ANY] * L,
        out_shape=[jax.ShapeDtypeStruct(a.shape, a.dtype) for a in bufs],
        input_output_aliases={l: l for l in range(L)},
        scratch_shapes=[pltpu.SemaphoreType.DMA((L,)), pltpu.SemaphoreType.DMA((L,))],
    )(*bufs)


def _scatter_comm(arrs):
    L = len(arrs)

    def copies(ins, outs, sems):
        send_sems, recv_sems = sems
        x, y, c = _here()
        return [pltpu.make_async_remote_copy(
            src_ref=ins[l].at[2 * tx + ty], dst_ref=outs[l].at[j],
            send_sem=send_sems.at[l, j], recv_sem=recv_sems.at[l, j],
            device_id=(tx, ty, c), device_id_type=MESH)
            for l in range(L) for j, (tx, ty) in enumerate(_other_chips(x, y))]

    def start(ins, outs, sems):
        for cp in copies(ins, outs, sems):
            cp.start()

    def finish(ins, outs, sems):
        for cp in copies(ins, outs, sems):
            cp.wait()

    return _Comm(arrs, [jax.ShapeDtypeStruct((3,) + a.shape[1:], a.dtype) for a in arrs],
                 [pltpu.SemaphoreType.DMA((L, 3)), pltpu.SemaphoreType.DMA((L, 3))], start, finish)


_A_ORDER = np.array(A_HEAD_ORDER)
_A_INVERSE = np.argsort(_A_ORDER)


def _permute_in_weights(w_in):
    qa = w_in[:, 0:512].reshape(D_MODEL, A_Q_HEADS, HEAD_DIM)[:, _A_ORDER, :].reshape(D_MODEL, 512)
    f_pad = jnp.pad(w_in[:, 2304:2312], ((0, 0), (0, LANES - B_HEADS)))
    w_a = jnp.concatenate([qa, w_in[:, 512:640], f_pad], axis=1)
    return w_a, w_in[:, 640:2304], w_in[:, 2312:4360]


def _slab_segments():
    segs = [(h * HEAD_DIM, int(_A_INVERSE[h]) * HEAD_DIM, HEAD_DIM) for h in range(A_Q_HEADS)]
    segs += [(512, OFF_KA, 128), (640, W_A + OFF_VA, 128), (768, W_A + OFF_QB, 1536),
             (2304, OFF_F, B_HEADS), (2312, W_A + W_B, W_G)]
    return segs


def _shard_slabs(dw_perm):
    R = dw_perm.shape[0]
    tr = _tile(R, 128, 8)
    plan = []
    for k in range(N_CHIP):
        for b in range(W_SHARD_PAD // LANES):
            lo, hi = k * W_SHARD + b * LANES, min(k * W_SHARD + (b + 1) * LANES, (k + 1) * W_SHARD)
            parts = []
            for o0, s0, n in _slab_segments():
                a, z = max(lo, o0), min(hi, o0 + n)
                while a < z:
                    s = s0 + (a - o0)
                    run = min(z - a, LANES - s % LANES)
                    parts.append((s // LANES, ((a - lo) - s % LANES) % LANES, a - lo, run))
                    a += run
            plan.append((k, b, parts))

    def kern(x_ref, o32_ref, obf_ref):
        lane = lax.broadcasted_iota(jnp.int32, (tr, LANES), 1)
        for k, b, parts in plan:
            acc = jnp.zeros((tr, LANES), F32)
            for src, rot, first, run in parts:
                blk = x_ref[:, src * LANES:(src + 1) * LANES]
                if rot:
                    blk = pltpu.roll(blk, rot, 1)
                acc = jnp.where((lane >= first) & (lane < first + run), blk, acc)
            o32_ref[k, :, b * LANES:(b + 1) * LANES] = acc
            obf_ref[k, :, b * LANES:(b + 1) * LANES] = acc.astype(BF)

    out_spec = pl.BlockSpec((N_CHIP, tr, W_SHARD_PAD), lambda i: (0, i, 0))
    return tuple(pl.pallas_call(
        kern, name="shard_slabs", grid=(R // tr,),
        in_specs=[pl.BlockSpec((tr, W_PERM), lambda i: (i, 0))],
        out_specs=[out_spec, out_spec],
        out_shape=[jax.ShapeDtypeStruct((N_CHIP, R, W_SHARD_PAD), F32),
                   jax.ShapeDtypeStruct((N_CHIP, R, W_SHARD_PAD), BF)],
        compiler_params=_cparams("parallel"),
    )(dw_perm))


class _NoExchange:
    def __init__(self, w_in, rest):
        self.w_in_whole, self.rest, self.grads = w_in, rest, {}

    def w_in_comm(self):
        return None

    def w_in(self, outs):
        return self.w_in_whole

    def rest_weights_comm(self):
        return None

    def rest_weights(self, outs):
        return self.rest

    def swap_comm(self, pieces, tag):
        self.grads[tag] = [p32 for p32, _ in pieces]
        return None

    def swap_done(self, outs, tag):
        return None

    def reduce_done(self, outs, tag):
        pass


class _Exchange:
    def __init__(self, ci, chip, w_in_shard, rest_shards):
        self.ci, self.chip, self.w_in_shard, self.rest_shards = ci, chip, w_in_shard, rest_shards
        self.pieces, self.part_f32, self.halves = {}, {}, {}

    def _my_half(self, a, axis=0, other=False):
        rows = a.shape[axis] // 2
        return lax.dynamic_slice_in_dim(a, ((1 - self.ci) if other else self.ci) * rows, rows, axis=axis)

    def w_in_comm(self):
        return _gather_comm([self._my_half(self.w_in_shard).astype(BF)])

    def w_in(self, outs):
        return _col_sharded(outs[0])

    def rest_weights_comm(self):
        return _gather_comm([self._my_half(w).astype(BF) for w in self.rest_shards])

    def rest_weights(self, outs):
        w_ba, w_bb, w_out, w_fi, w_fo = outs
        return (_col_sharded(w_ba), _col_sharded(w_bb), _row_sharded(w_out), _col_sharded(w_fi),
                _row_sharded(w_fo))

    def swap_comm(self, pieces, tag):
        self.pieces[tag] = pieces
        return _swap_comm([pbf for _, pbf in pieces])

    def swap_done(self, got, tag):
        self.part_f32[tag], part_bf = [], []
        for l, ((p32, _), g_) in enumerate(zip(self.pieces[tag], got)):
            s32, sbf = _add_pair(p32, g_, self.ci, f"chip_sum_{tag}_{l}")
            self.part_f32[tag].append(s32)
            part_bf.append(sbf)
        return _scatter_comm(part_bf)

    def reduce_done(self, outs, tag):
        self.halves[tag] = [_add_three(p32, r, self.chip, f"shard_sum_{tag}_{l}")
                            for l, (p32, r) in enumerate(zip(self.part_f32[tag], outs))]


def _col_sharded(g):
    return jnp.transpose(g.reshape(N_CHIP, -1, g.shape[-1]), (1, 0, 2)).reshape(2 * g.shape[1], N_CHIP * g.shape[-1])


def _row_sharded(g):
    return g.reshape(N_DEV * g.shape[1], g.shape[-1])


def _rope_tables(pos):
    inv_freq = 1.0 / (ROPE_THETA ** (jnp.arange(0, HEAD_DIM, 2, dtype=F32) / HEAD_DIM))
    ang = pos.astype(F32)[:, None] * inv_freq
    cos, sin = jnp.cos(ang), jnp.sin(ang)
    return jnp.tile(cos, (1, 4)), jnp.tile(jnp.concatenate([-sin, sin], axis=1), (1, 2))


def _local_step(x, pos, ada, g1, g2, g3, g4, b_f, sinks, exch, target):
    S = x.shape[0]
    t_fox = _tile(S, 512, LANES) if S >= 1024 else S // 2
    t_fox_fwd = _tile(S, 1024, LANES) if S >= 2048 else S // 2
    shift_m, scale_m, gate_m, shift_f, scale_f, gate_f = [ada[i:i + 1] for i in range(N_ADA)]
    cos_t, sin_t = _rope_tables(pos)
    sinks_p = sinks.reshape(A_KV_HEADS, 4).T.reshape(A_Q_HEADS)
    b_f_pad = jnp.pad(b_f, (0, LANES - B_HEADS)).reshape(1, LANES)

    h1, outs = _pre_norm(x, g1, scale_m, shift_m, "pre_mix_norm", comm=exch.w_in_comm())
    w_a, w_b, w_g = _permute_in_weights(exch.w_in(outs))
    w_perm = jnp.concatenate([w_a, w_b, w_g], axis=1)
    p_a = _mm(h1, w_a, "nn", F32, "proj_a")
    p_b = _mm(h1, w_b, "nn", BF, "proj_b")
    p_g = _mm(h1, w_g, "nn", BF, "proj_g")
    (qk_a,) = _rope([p_a], [640], cos_t, sin_t, "rope_fwd")
    o_a, lse_a = _swa_fwd(qk_a, p_b, 0, sinks_p)
    q_aug, k_aug = _fox_prep_fwd(p_b, _fox_gate_fwd(p_a, b_f_pad), t_fox)
    comm = exch.rest_weights_comm()
    o_b, lse_b, outs = _fox_fwd(q_aug, k_aug, p_b, t_fox_fwd, comm=comm)
    w_ba, w_bb, w_out, w_fi, w_fo = exch.rest_weights(outs)
    w_ba_p = w_ba.reshape(A_Q_HEADS, HEAD_DIM, D_MODEL)[_A_ORDER].reshape(512, D_MODEL)
    pa = _mm(o_a, w_ba_p, "nn", F32, "branch_a")
    pb = _mm(o_b, w_bb, "nn", F32, "branch_b")
    merged = _merge_fwd(p_g, pa, pb)
    y1 = _mm(merged, w_out, "nn", F32, "out_proj")
    x2, h2 = _post_pre(x, y1, g2, gate_m, g3, scale_f, shift_f)
    gu = _mm(h2, w_fi, "nn", BF, "ffn_in")
    act = _swiglu_fwd(gu)
    y2 = _mm(act, w_fo, "nn", F32, "ffn_out")
    d_out, d_y2, st_f = _final(x2, y2, g4, gate_f, target)

    d_act = _mm(d_y2, w_fo, "nt", F32, "ffn_out_dx")
    row_pieces = lambda pair: tuple(t.reshape(N_CHIP, t.shape[0] // N_CHIP, t.shape[1]) for t in pair)
    dw_fo = row_pieces(_mm(act, d_y2, "tn", F32, "ffn_out_dw", twin=True))
    d_gu = _swiglu_bwd(d_act, gu)
    d_h2 = _mm(d_gu, w_fi, "nt", F32, "ffn_in_dx")
    dw_fi = _mm(h2, d_gu, "tn", F32, "ffn_in_dw", col_pieces=N_CHIP, twin=True)
    d_x2, d_y1, st_m = _mid_bwd(d_h2, x2, d_out, y1, g3, scale_f, g2, gate_m)
    d_merged = _mm(d_y1, w_out, "nt", F32, "out_proj_dx")
    dw_out = row_pieces(_mm(merged, d_y1, "tn", F32, "out_proj_dw", twin=True))
    d_pa, d_pb, d_ga, d_gb = _merge_bwd(d_merged, p_g, pa, pb)
    d_oa = _mm(d_pa, w_ba_p, "nt", F32, "branch_a_dx")
    dw_ba_p = _mm(o_a, d_pa, "tn", F32, "branch_a_dw", col_pieces=N_CHIP, twin=True)
    d_ob = _mm(d_pb, w_bb, "nt", F32, "branch_b_dx")
    dw_bb = _mm(o_b, d_pb, "tn", F32, "branch_b_dw", col_pieces=N_CHIP, twin=True)
    head_rows = lambda t: t.reshape(N_CHIP, A_Q_HEADS, HEAD_DIM, -1)[:, _A_INVERSE].reshape(t.shape)
    dw_ba = tuple(head_rows(t) for t in dw_ba_p)
    comm = exch.swap_comm([dw_ba, dw_bb, dw_out, dw_fi, dw_fo], "early")
    dq_a, dk_a, dv_a, d_sink, outs = _swa_bwd(qk_a, p_b, 0, o_a, d_oa, lse_a, sinks_p, comm=comm)
    comm = exch.swap_done(outs, "early")
    qb_aug, dob_aug, vb_aug = _fox_prep_bwd(q_aug, p_b, o_b, d_ob, lse_b, t_fox)
    dq_b, dk_b, dv_b, d_ck, d_cq, outs = _fox_bwd(qb_aug, k_aug, dob_aug, vb_aug, t_fox, comm=comm)
    exch.reduce_done(outs, "early")
    d_qa, d_ka = _rope([dq_a, dk_a], [512, LANES], cos_t, -sin_t, "rope_bwd")
    d_ck_cols = jnp.pad(d_ck.reshape(B_HEADS, S).T, ((0, 0), (0, LANES - B_HEADS)))
    d_f, d_bf = _fox_gate_bwd(d_cq, d_ck_cols, p_a, b_f_pad)
    d_proj = jnp.concatenate([d_qa, d_ka, d_f, dv_a.astype(BF), dq_b.astype(BF), dk_b.astype(BF),
                              dv_b.astype(BF), d_ga, d_gb], axis=1)
    dw_perm = _mm(h1, d_proj, "tn", F32, "proj_dw")
    swap = exch.swap_comm([_shard_slabs(dw_perm)], "late")
    comm = exch.swap_done(_run_comm(swap, "grads_to_sibling_late") if swap else None, "late")
    res = _mm(d_proj, w_perm, "nt", F32, "proj_dx", comm=comm)
    d_h1 = res[0] if comm else res
    exch.reduce_done(res[1] if comm else None, "late")
    grad_x, st_p = _pre_bwd(d_h1, x, d_x2, g1, scale_m)

    d_sinks = d_sink[:, :2, 0].T.reshape(A_Q_HEADS)
    small = jnp.concatenate([
        st_p[0], st_p[1], st_m[3], st_m[0], st_m[1], st_f[0],
        st_p[2], st_m[4], st_m[2], st_f[1],
        st_f[2], d_bf[0, :B_HEADS], d_sinks,
        jnp.zeros((SM_LEN - SM_SINK - A_Q_HEADS,), F32)])
    return grad_x, small


def kernel(x, c, positions, w_ada, b_ada, g_pre_mix, g_post_mix, w_in, b_f, sinks, w_branch_a, w_branch_b, w_out, g_pre_ffn, g_post_ffn, w_ffn_in, w_ffn_out, loss_target, m_w_ada, m_b_ada, m_g_pre_mix, m_g_post_mix, m_w_in, m_b_f, m_sinks, m_w_branch_a, m_w_branch_b, m_w_out, m_g_pre_ffn, m_g_post_ffn, m_w_ffn_in, m_w_ffn_out, v_w_ada, v_b_ada, v_g_pre_mix, v_g_post_mix, v_w_in, v_b_f, v_sinks, v_w_branch_a, v_w_branch_b, v_w_out, v_g_pre_ffn, v_g_post_ffn, v_w_ffn_in, v_w_ffn_out):
    xi, yi, ci = _here()
    chip = 2 * xi + yi
    dev = 2 * chip + ci

    (c_g,) = _allgather8([c.reshape(8, LANES)], "gather_c")
    c_all = c_g.reshape(N_DEV, D_MODEL)
    exch = _Exchange(ci, chip, w_in[0], [w_branch_a[0], w_branch_b[0], w_out[0], w_ffn_in[0], w_ffn_out[0]])

    ada_cols = _mm(c_all, w_ada[0], "nn", F32, "ada_fwd")
    (ada_g,) = _allgather8([ada_cols], "gather_ada")
    ada_mine = lax.dynamic_index_in_dim(ada_g.reshape(N_CHIP, 2, N_DEV, -1)[:, 0], dev, axis=1, keepdims=False)
    ada = (ada_mine.reshape(-1) + b_ada[0]).reshape(N_ADA, D_MODEL)

    grad_x, small = _local_step(
        x[0], positions[0], ada, g_pre_mix, g_post_mix, g_pre_ffn, g_post_ffn, b_f[0], sinks[0],
        exch, loss_target[0])

    (small_g,) = _allgather8([small.reshape(8, SM_LEN // 8)], "gather_small")
    small_all = small_g.reshape(N_DEV, SM_LEN)
    small_tot, loss_row = _small_finalize(small_all)
    loss = loss_row[0, 0]
    d_ada_cols = lax.dynamic_slice_in_dim(small_all[:, :N_ADA * D_MODEL], chip * (N_ADA * D_MODEL // N_CHIP),
                                          N_ADA * D_MODEL // N_CHIP, axis=1)
    g_w_ada = _ada_dw(c_all.T, d_ada_cols)

    joined = _sibling_join(exch.halves["late"] + exch.halves["early"], "grads_join")
    g_w_in, g_w_ba, g_w_bb, g_w_out, g_w_fi, g_w_fo = [j.reshape(2 * j.shape[1], j.shape[2]) for j in joined]

    def small_vec(b_ada_, g1_, g2_, g3_, g4_, b_f_, sinks_):
        return jnp.concatenate([b_ada_[0], g1_[0], g2_[0], g3_[0], g4_[0], jnp.zeros((D_MODEL,), F32),
                                b_f_[0], sinks_[0], jnp.zeros((SM_LEN - SM_SINK - A_Q_HEADS,), F32)]
                               ).reshape(8, SM_LEN // 8)

    sw = small_vec(b_ada, g_pre_mix, g_post_mix, g_pre_ffn, g_post_ffn, b_f, sinks)
    sm = small_vec(m_b_ada, m_g_pre_mix, m_g_post_mix, m_g_pre_ffn, m_g_post_ffn, m_b_f, m_sinks)
    sv = small_vec(v_b_ada, v_g_pre_mix, v_g_post_mix, v_g_pre_ffn, v_g_post_ffn, v_b_f, v_sinks)
    s_upd = [u.reshape(SM_LEN) for u in _adamw(sw, small_tot.reshape(8, SM_LEN // 8), sm, sv, "adamw_small")]
    s_grad = small_tot.reshape(SM_LEN)

    def unpack(vec):
        row = lambda a, n: vec[a:a + n].reshape(1, n)
        return dict(b_ada=row(SM_ADA, N_ADA * D_MODEL), g_pre_mix=row(SM_G, D_MODEL),
                    g_post_mix=row(SM_G + D_MODEL, D_MODEL), g_pre_ffn=row(SM_G + 2 * D_MODEL, D_MODEL),
                    g_post_ffn=row(SM_G + 3 * D_MODEL, D_MODEL), b_f=row(SM_BF, B_HEADS),
                    sinks=row(SM_SINK, A_Q_HEADS))

    big = dict(
        w_ada=(w_ada, g_w_ada, m_w_ada, v_w_ada),
        w_branch_a=(w_branch_a, g_w_ba, m_w_branch_a, v_w_branch_a),
        w_branch_b=(w_branch_b, g_w_bb, m_w_branch_b, v_w_branch_b),
        w_out=(w_out, g_w_out, m_w_out, v_w_out), w_ffn_in=(w_ffn_in, g_w_fi, m_w_ffn_in, v_w_ffn_in),
        w_ffn_out=(w_ffn_out, g_w_fo, m_w_ffn_out, v_w_ffn_out))
    grads, deltas, new_m, new_v = unpack(s_grad), unpack(s_upd[0]), unpack(s_upd[1]), unpack(s_upd[2])
    for n, (w_, g_, m_, v_) in big.items():
        d_, nm_, nv_ = _adamw(w_[0], g_, m_[0], v_[0], "adamw_" + n)
        grads[n], deltas[n], new_m[n], new_v[n] = g_[None], d_[None], nm_[None], nv_[None]
    pad_cols = lambda a: jnp.pad(a, ((0, 0), (0, W_SHARD_PAD - W_SHARD)))
    upd = _adamw(pad_cols(w_in[0]), g_w_in, pad_cols(m_w_in[0]), pad_cols(v_w_in[0]), "adamw_w_in")
    grads["w_in"], deltas["w_in"], new_m["w_in"], new_v["w_in"] = [t[None, :, :W_SHARD] for t in (g_w_in, *upd)]

    names = ["w_ada", "b_ada", "g_pre_mix", "g_post_mix", "w_in", "b_f", "sinks", "w_branch_a", "w_branch_b",
             "w_out", "g_pre_ffn", "g_post_ffn", "w_ffn_in", "w_ffn_out"]
    return (loss, grad_x[None], *[grads[n] for n in names], *[deltas[n] for n in names],
            *[new_m[n] for n in names], *[new_v[n] for n in names])
```

```python
import functools
import math

import numpy as np
import jax
import jax.numpy as jnp
from jax import lax
from jax.experimental import pallas as pl
from jax.experimental.pallas import tpu as pltpu

F32 = jnp.float32
BF = jnp.bfloat16

D_MODEL = 1024
HEAD_DIM = 64
LANES = 128
WINDOW = 128
A_Q_HEADS = 8
A_KV_HEADS = 2
B_HEADS = 8
D_FF = 2816
ROPE_THETA = 10000.0
RMS_EPS = 1e-6
N_ADA = 6
N_DEV = 8
N_CHIP = 4

ADAM_LR = 0.001
ADAM_B1 = 0.9
ADAM_B2 = 0.999
ADAM_EPS = 1e-08
ADAM_WD = 0.01
ADAM_STEP = 10

VMEM_LIMIT = 48 * 1024 * 1024
MESH = pl.DeviceIdType.MESH

A_HEAD_ORDER = (0, 4, 1, 5, 2, 6, 3, 7)

OFF_QA, OFF_KA, OFF_F = 0, 512, 640
W_A = 768
OFF_VA, OFF_QB, OFF_KB, OFF_VB = 0, 128, 640, 1152
W_B = 1664
W_G = 2048
W_PERM = W_A + W_B + W_G
W_SHARD = 1090
W_SHARD_PAD = 1152


def _tile(n, cap, mult=LANES):
    if n <= cap:
        return n
    t = (cap // mult) * mult
    while t >= mult:
        if n % t == 0:
            return t
        t -= mult
    raise ValueError(f"no tile for {n}")


MXU_WIDTH = 256
MM_OPERAND_BYTES = 28 * 1024 * 1024


def _mm_tiles(M, N, K, a_bytes, b_bytes, tm_cap, tn_cap):
    tm = _tile(M, tm_cap)
    try:
        tn = _tile(N, tn_cap, MXU_WIDTH)
    except ValueError:
        tn = _tile(N, tn_cap)
    fits = lambda tk: 2 * tk * (tm * a_bytes + tn * b_bytes) <= MM_OPERAND_BYTES
    tk = K if fits(K) else next(t for t in range(K // LANES * LANES, 0, -LANES) if K % t == 0 and fits(t))
    return tm, tn, tk


def _cparams(*sem):
    return pltpu.CompilerParams(dimension_semantics=sem, vmem_limit_bytes=VMEM_LIMIT)


def _own_refs(refs, comm, n_in, n_out, n_scratch):
    if comm is None:
        return list(refs), None
    return comm.split(refs, n_in, n_out, n_scratch)


def _comm_specs(comm, side):
    if comm is None:
        return []
    return [pl.BlockSpec(memory_space=pl.ANY)] * len(comm.ins if side == "in" else comm.out_shapes)


def _comm_edge(comm, comm_refs, grid, first):
    if comm is None:
        return
    at_edge = None
    for axis, n in enumerate(grid):
        here = pl.program_id(axis) == (0 if first else n - 1)
        at_edge = here if at_edge is None else at_edge & here
    pl.when(at_edge)(lambda: (comm.start if first else comm.finish)(*comm_refs))


def _mm(a, b, mode, out_dtype, name, tm_cap=512, tn_cap=2816, comm=None, col_pieces=1, twin=False):
    if mode == "nn":
        (M, K), (K2, N) = a.shape, b.shape
        dims = (((1,), (0,)), ((), ()))
    elif mode == "nt":
        (M, K), (N, K2) = a.shape, b.shape
        dims = (((1,), (1,)), ((), ()))
    else:
        (K, M), (K2, N) = a.shape, b.shape
        dims = (((0,), (0,)), ((), ()))
    assert K == K2, (a.shape, b.shape, mode)
    tm, tn, tk = _mm_tiles(M, N // col_pieces, K, a.dtype.itemsize, b.dtype.itemsize, tm_cap, tn_cap)
    nk = K // tk
    n_out = 2 if twin else 1
    n_scratch = 1 if nk > 1 else 0
    if mode == "nn":
        a_spec = pl.BlockSpec((tm, tk), lambda i, j, k: (i, k))
        b_spec = pl.BlockSpec((tk, tn), lambda i, j, k: (k, j))
    elif mode == "nt":
        a_spec = pl.BlockSpec((tm, tk), lambda i, j, k: (i, k))
        b_spec = pl.BlockSpec((tn, tk), lambda i, j, k: (j, k))
    else:
        a_spec = pl.BlockSpec((tk, tm), lambda i, j, k: (k, i))
        b_spec = pl.BlockSpec((tk, tn), lambda i, j, k: (k, j))

    grid = (M // tm, N // tn, nk)

    def kern(*refs):
        own, comm_refs = _own_refs(refs, comm, 2, n_out, n_scratch)
        a_ref, b_ref, o_refs = own[0], own[1], own[2:2 + n_out]
        k = pl.program_id(2)
        _comm_edge(comm, comm_refs, grid, first=True)
        part = lax.dot_general(a_ref[...].astype(BF), b_ref[...].astype(BF), dims,
                               preferred_element_type=F32)
        if nk == 1:
            for o_ref in o_refs:
                o_ref[...] = part.astype(o_ref.dtype)
        else:
            acc_ref = own[2 + n_out]

            @pl.when(k == 0)
            def _():
                acc_ref[...] = part

            @pl.when(k > 0)
            def _():
                acc_ref[...] += part

            @pl.when(k == nk - 1)
            def _():
                for o_ref in o_refs:
                    o_ref[...] = acc_ref[...].astype(o_ref.dtype)

        _comm_edge(comm, comm_refs, grid, first=False)

    if col_pieces > 1:
        per = N // col_pieces // tn
        out_spec = pl.BlockSpec((None, tm, tn), lambda i, j, k: (j // per, i, j % per))
        shape = (col_pieces, M, N // col_pieces)
    else:
        out_spec = pl.BlockSpec((tm, tn), lambda i, j, k: (i, j))
        shape = (M, N)
    dtypes = [out_dtype, BF] if twin else [out_dtype]
    res = pl.pallas_call(
        kern, name=name, grid=grid,
        in_specs=[a_spec, b_spec] + _comm_specs(comm, "in"),
        out_specs=[out_spec] * n_out + _comm_specs(comm, "out"),
        out_shape=[jax.ShapeDtypeStruct(shape, d) for d in dtypes] + (comm.out_shapes if comm else []),
        scratch_shapes=[pltpu.VMEM((tm, tn), F32)] * n_scratch + (comm.sem_shapes if comm else []),
        compiler_params=_cparams("parallel", "parallel", "arbitrary"),
    )(a, b, *(comm.ins if comm else []))
    own = res[0] if n_out == 1 else tuple(res[:n_out])
    return (own, res[n_out:]) if comm else own


ROWS = 256


def _row_spec(tm, width=D_MODEL, col=0):
    return pl.BlockSpec((tm, width), lambda i: (i, col))


def _vec_spec(width=D_MODEL):
    return pl.BlockSpec((1, width), lambda i: (0, 0))


def _rms(x):
    return lax.rsqrt(jnp.mean(x * x, axis=-1, keepdims=True) + RMS_EPS)


def _colsum(x):
    return jnp.sum(x, axis=0, keepdims=True)


def _norm_bwd(d_xn, xn, r):
    return r * (d_xn - xn * jnp.mean(d_xn * xn, axis=-1, keepdims=True))


def _pre_norm(x, g, scale, shift, name, comm=None):
    S = x.shape[0]
    tm = _tile(S, ROWS, 8)
    grid = (S // tm,)

    def kern(*refs):
        (x_ref, g_ref, sc_ref, sh_ref, h_ref), comm_refs = _own_refs(refs, comm, 4, 1, 0)
        _comm_edge(comm, comm_refs, grid, first=True)
        xf = x_ref[...]
        y = xf * _rms(xf) * g_ref[...]
        h_ref[...] = (y * (1.0 + sc_ref[...]) + sh_ref[...]).astype(BF)
        _comm_edge(comm, comm_refs, grid, first=False)

    res = pl.pallas_call(
        kern, name=name, grid=grid,
        in_specs=[_row_spec(tm), _vec_spec(), _vec_spec(), _vec_spec()] + _comm_specs(comm, "in"),
        out_specs=[_row_spec(tm)] + _comm_specs(comm, "out"),
        out_shape=[jax.ShapeDtypeStruct((S, D_MODEL), BF)] + (comm.out_shapes if comm else []),
        scratch_shapes=comm.sem_shapes if comm else [],
        compiler_params=_cparams("arbitrary"),
    )(x, g, scale, shift, *(comm.ins if comm else []))
    return res[0], res[1:]


def _post_pre(x, y1, g2, gate_m, g3, scale_f, shift_f):
    S = x.shape[0]
    tm = _tile(S, ROWS, 8)

    def kern(x_ref, y_ref, g2_ref, gm_ref, g3_ref, sc_ref, sh_ref, x2_ref, h2_ref):
        y = y_ref[...].astype(F32)
        n2 = y * _rms(y) * g2_ref[...]
        x2 = x_ref[...] + gm_ref[...] * n2
        x2_ref[...] = x2
        n3 = x2 * _rms(x2) * g3_ref[...]
        h2_ref[...] = (n3 * (1.0 + sc_ref[...]) + sh_ref[...]).astype(BF)

    return pl.pallas_call(
        kern, name="post_mix_pre_ffn", grid=(S // tm,),
        in_specs=[_row_spec(tm), _row_spec(tm)] + [_vec_spec()] * 5,
        out_specs=[_row_spec(tm), _row_spec(tm)],
        out_shape=[jax.ShapeDtypeStruct((S, D_MODEL), F32), jax.ShapeDtypeStruct((S, D_MODEL), BF)],
        compiler_params=_cparams("parallel"),
    )(x, y1, g2, gate_m, g3, scale_f, shift_f)


def _stats_spec():
    return pl.BlockSpec((8, D_MODEL), lambda i: (0, 0))


def _final(x2, y2, g4, gate_f, target):
    S = x2.shape[0]
    tm = _tile(S, ROWS, 8)

    def kern(x2_ref, y_ref, g4_ref, gf_ref, t_ref, dout_ref, dy_ref, st_ref):
        @pl.when(pl.program_id(0) == 0)
        def _():
            st_ref[...] = jnp.zeros_like(st_ref)

        y = y_ref[...].astype(F32)
        r = _rms(y)
        yn = y * r
        n4 = yn * g4_ref[...]
        diff = x2_ref[...] + gf_ref[...] * n4 - t_ref[...]
        d_out = diff / D_MODEL
        dout_ref[...] = d_out
        dn = d_out * gf_ref[...]
        dy_ref[...] = _norm_bwd(dn * g4_ref[...], yn, r).astype(BF)
        st_ref[0:1, :] += _colsum(d_out * n4)
        st_ref[1:2, :] += _colsum(dn * yn)
        st_ref[2:3, :] += _colsum(diff * diff)

    return pl.pallas_call(
        kern, name="final_loss", grid=(S // tm,),
        in_specs=[_row_spec(tm), _row_spec(tm), _vec_spec(), _vec_spec(), _row_spec(tm)],
        out_specs=[_row_spec(tm), _row_spec(tm), _stats_spec()],
        out_shape=[jax.ShapeDtypeStruct((S, D_MODEL), F32), jax.ShapeDtypeStruct((S, D_MODEL), BF),
                   jax.ShapeDtypeStruct((8, D_MODEL), F32)],
        compiler_params=_cparams("arbitrary"),
    )(x2, y2, g4, gate_f, target)


def _mid_bwd(d_h2, x2, d_out, y1, g3, scale_f, g2, gate_m):
    S = x2.shape[0]
    tm = _tile(S, ROWS, 8)

    def kern(dh_ref, x2_ref, dout_ref, y_ref, g3_ref, sc_ref, g2_ref, gm_ref, dx2_ref, dy_ref, st_ref):
        @pl.when(pl.program_id(0) == 0)
        def _():
            st_ref[...] = jnp.zeros_like(st_ref)

        dh = dh_ref[...].astype(F32)
        x2 = x2_ref[...]
        r3 = _rms(x2)
        xn = x2 * r3
        one_sc = 1.0 + sc_ref[...]
        d_x2 = dout_ref[...] + _norm_bwd(dh * one_sc * g3_ref[...], xn, r3)
        dx2_ref[...] = d_x2
        y = y_ref[...].astype(F32)
        r2 = _rms(y)
        yn = y * r2
        dn = d_x2 * gm_ref[...]
        dy_ref[...] = _norm_bwd(dn * g2_ref[...], yn, r2).astype(BF)
        st_ref[0:1, :] += _colsum(dh)
        st_ref[1:2, :] += _colsum(dh * (xn * g3_ref[...]))
        st_ref[2:3, :] += _colsum(dh * one_sc * xn)
        st_ref[3:4, :] += _colsum(d_x2 * (yn * g2_ref[...]))
        st_ref[4:5, :] += _colsum(dn * yn)

    return pl.pallas_call(
        kern, name="mid_bwd", grid=(S // tm,),
        in_specs=[_row_spec(tm)] * 4 + [_vec_spec()] * 4,
        out_specs=[_row_spec(tm), _row_spec(tm), _stats_spec()],
        out_shape=[jax.ShapeDtypeStruct((S, D_MODEL), F32), jax.ShapeDtypeStruct((S, D_MODEL), BF),
                   jax.ShapeDtypeStruct((8, D_MODEL), F32)],
        compiler_params=_cparams("arbitrary"),
    )(d_h2, x2, d_out, y1, g3, scale_f, g2, gate_m)


def _pre_bwd(d_h1, x, d_x2, g1, scale_m):
    S = x.shape[0]
    tm = _tile(S, ROWS, 8)

    def kern(dh_ref, x_ref, dx2_ref, g_ref, sc_ref, gx_ref, st_ref):
        @pl.when(pl.program_id(0) == 0)
        def _():
            st_ref[...] = jnp.zeros_like(st_ref)

        dh = dh_ref[...].astype(F32)
        xf = x_ref[...]
        r = _rms(xf)
        xn = xf * r
        one_sc = 1.0 + sc_ref[...]
        gx_ref[...] = dx2_ref[...] + _norm_bwd(dh * one_sc * g_ref[...], xn, r)
        st_ref[0:1, :] += _colsum(dh)
        st_ref[1:2, :] += _colsum(dh * (xn * g_ref[...]))
        st_ref[2:3, :] += _colsum(dh * one_sc * xn)

    return pl.pallas_call(
        kern, name="pre_mix_bwd", grid=(S // tm,),
        in_specs=[_row_spec(tm)] * 3 + [_vec_spec()] * 2,
        out_specs=[_row_spec(tm), _stats_spec()],
        out_shape=[jax.ShapeDtypeStruct((S, D_MODEL), F32), jax.ShapeDtypeStruct((8, D_MODEL), F32)],
        compiler_params=_cparams("arbitrary"),
    )(d_h1, x, d_x2, g1, scale_m)


def _rope(xs, widths, cos_t, sin_t, name):
    S = xs[0].shape[0]
    tm = _tile(S, 512, 8)
    n = len(xs)

    def kern(*refs):
        cos = refs[n][...]
        sin = refs[n + 1][...]
        first = (lax.broadcasted_iota(jnp.int32, cos.shape, 1) % HEAD_DIM) < HEAD_DIM // 2
        for x_ref, o_ref, w in zip(refs[:n], refs[n + 2:], widths):
            for c0 in range(0, w, LANES):
                v = x_ref[:, c0:c0 + LANES]
                partner = jnp.where(first, pltpu.roll(v, LANES - HEAD_DIM // 2, 1),
                                    pltpu.roll(v, HEAD_DIM // 2, 1))
                o_ref[:, c0:c0 + LANES] = (v * cos + partner * sin).astype(BF)

    return pl.pallas_call(
        kern, name=name, grid=(S // tm,),
        in_specs=[_row_spec(tm, w) for w in widths] + [_row_spec(tm, LANES)] * 2,
        out_specs=[_row_spec(tm, w) for w in widths],
        out_shape=[jax.ShapeDtypeStruct((S, w), BF) for w in widths],
        compiler_params=_cparams("parallel"),
    )(*xs, cos_t, sin_t)


def _merge_fwd(pg, pa, pb):
    S = pa.shape[0]
    tm = _tile(S, ROWS, 8)

    def kern(ga_ref, gb_ref, pa_ref, pb_ref, o_ref):
        ga = jax.nn.sigmoid(ga_ref[...].astype(F32))
        gb = jax.nn.sigmoid(gb_ref[...].astype(F32))
        o_ref[...] = (ga * pa_ref[...].astype(F32) + gb * pb_ref[...].astype(F32)).astype(BF)

    return pl.pallas_call(
        kern, name="merge_fwd", grid=(S // tm,),
        in_specs=[_row_spec(tm, col=0), _row_spec(tm, col=1), _row_spec(tm), _row_spec(tm)],
        out_specs=_row_spec(tm),
        out_shape=jax.ShapeDtypeStruct((S, D_MODEL), BF),
        compiler_params=_cparams("parallel"),
    )(pg, pg, pa, pb)


def _merge_bwd(d_merged, pg, pa, pb):
    S = pa.shape[0]
    tm = _tile(S, ROWS, 8)

    def kern(dm_ref, ga_ref, gb_ref, pa_ref, pb_ref, dpa_ref, dpb_ref, dga_ref, dgb_ref):
        dm = dm_ref[...].astype(F32)
        ga = jax.nn.sigmoid(ga_ref[...].astype(F32))
        gb = jax.nn.sigmoid(gb_ref[...].astype(F32))
        dpa_ref[...] = (dm * ga).astype(BF)
        dpb_ref[...] = (dm * gb).astype(BF)
        dga_ref[...] = (dm * pa_ref[...].astype(F32) * ga * (1.0 - ga)).astype(BF)
        dgb_ref[...] = (dm * pb_ref[...].astype(F32) * gb * (1.0 - gb)).astype(BF)

    bf_out = jax.ShapeDtypeStruct((S, D_MODEL), BF)
    return pl.pallas_call(
        kern, name="merge_bwd", grid=(S // tm,),
        in_specs=[_row_spec(tm), _row_spec(tm, col=0), _row_spec(tm, col=1), _row_spec(tm), _row_spec(tm)],
        out_specs=[_row_spec(tm)] * 4,
        out_shape=[bf_out] * 4,
        compiler_params=_cparams("parallel"),
    )(d_merged, pg, pg, pa, pb)


def _swiglu_fwd(gu):
    S = gu.shape[0]
    tm = _tile(S, ROWS, 8)
    tc = _tile(D_FF, 1408)
    nc = D_FF // tc

    def kern(g_ref, u_ref, o_ref):
        g = g_ref[...].astype(F32)
        o_ref[...] = (g * jax.nn.sigmoid(g) * u_ref[...].astype(F32)).astype(BF)

    return pl.pallas_call(
        kern, name="swiglu_fwd", grid=(S // tm, nc),
        in_specs=[pl.BlockSpec((tm, tc), lambda i, j: (i, j)),
                  pl.BlockSpec((tm, tc), lambda i, j: (i, j + nc))],
        out_specs=pl.BlockSpec((tm, tc), lambda i, j: (i, j)),
        out_shape=jax.ShapeDtypeStruct((S, D_FF), BF),
        compiler_params=_cparams("parallel", "parallel"),
    )(gu, gu)


def _swiglu_bwd(d_act, gu):
    S = gu.shape[0]
    tm = _tile(S, 128, 8)

    def kern(da_ref, g_ref, u_ref, o_ref):
        g = g_ref[...].astype(F32)
        u = u_ref[...].astype(F32)
        da = da_ref[...].astype(F32)
        sg = jax.nn.sigmoid(g)
        o_ref[:, :D_FF] = (da * u * (sg * (1.0 + g * (1.0 - sg)))).astype(BF)
        o_ref[:, D_FF:] = (da * (g * sg)).astype(BF)

    return pl.pallas_call(
        kern, name="swiglu_bwd", grid=(S // tm,),
        in_specs=[_row_spec(tm, D_FF), _row_spec(tm, D_FF, 0), _row_spec(tm, D_FF, 1)],
        out_specs=_row_spec(tm, 2 * D_FF),
        out_shape=jax.ShapeDtypeStruct((S, 2 * D_FF), BF),
        compiler_params=_cparams("parallel"),
    )(d_act, gu, gu)


def _split3(x):
    hi = x.astype(BF)
    r1 = x - hi.astype(F32)
    mid = r1.astype(BF)
    lo = (r1 - mid.astype(F32)).astype(BF)
    return hi, mid, lo


def _tri_dot(tri, x):
    return sum(jnp.dot(tri, part, preferred_element_type=F32) for part in _split3(x))


def _log_sigmoid(z):
    return jnp.minimum(z, 0.0) - jnp.log(1.0 + jnp.exp(-jnp.abs(z)))


def _fox_gate_fwd(pa, b_f_pad):
    S = pa.shape[0]
    T = _tile(S, 512, 8)
    f_col = OFF_F // LANES

    def kern(z_ref, b_ref, cum_ref, carry_ref):
        @pl.when(pl.program_id(0) == 0)
        def _():
            carry_ref[...] = jnp.zeros_like(carry_ref)

        log_f = _log_sigmoid(z_ref[...] + b_ref[...])
        row = lax.broadcasted_iota(jnp.int32, (T, T), 0)
        col = lax.broadcasted_iota(jnp.int32, (T, T), 1)
        tri = (col <= row).astype(BF)
        cum = _tri_dot(tri, log_f) + carry_ref[...]
        cum_ref[...] = cum
        carry_ref[...] = cum[T - 1:T, :]

    return pl.pallas_call(
        kern, name="fox_gate_fwd", grid=(S // T,),
        in_specs=[_row_spec(T, LANES, f_col), _vec_spec(LANES)],
        out_specs=_row_spec(T, LANES),
        out_shape=jax.ShapeDtypeStruct((S, LANES), F32),
        scratch_shapes=[pltpu.VMEM((1, LANES), F32)],
        compiler_params=_cparams("arbitrary"),
    )(pa, b_f_pad)


def _fox_gate_bwd(rowsum_ds, colsum_ds, pa, b_f_pad):
    S = pa.shape[0]
    T = _tile(S, 512, 8)
    nb = S // T
    f_col = OFF_F // LANES

    def kern(dr_ref, dc_ref, z_ref, b_ref, df_ref, dbf_ref, carry_ref):
        @pl.when(pl.program_id(0) == 0)
        def _():
            carry_ref[...] = jnp.zeros_like(carry_ref)
            dbf_ref[...] = jnp.zeros_like(dbf_ref)

        row = lax.broadcasted_iota(jnp.int32, (T, T), 0)
        col = lax.broadcasted_iota(jnp.int32, (T, T), 1)
        tri = (col >= row).astype(BF)
        rev = _tri_dot(tri, dr_ref[...] - dc_ref[...]) + carry_ref[...]
        carry_ref[...] = rev[0:1, :]
        z = z_ref[...] + b_ref[...]
        lane = lax.broadcasted_iota(jnp.int32, (T, LANES), 1)
        d_z = jnp.where(lane < B_HEADS, rev * jax.nn.sigmoid(-z), 0.0)
        df_ref[...] = d_z.astype(BF)
        dbf_ref[0:1, :] += _colsum(d_z)

    return pl.pallas_call(
        kern, name="fox_gate_bwd", grid=(nb,),
        in_specs=[pl.BlockSpec((T, LANES), lambda i: (nb - 1 - i, 0)),
                  pl.BlockSpec((T, LANES), lambda i: (nb - 1 - i, 0)),
                  pl.BlockSpec((T, LANES), lambda i: (nb - 1 - i, f_col)),
                  _vec_spec(LANES)],
        out_specs=[pl.BlockSpec((T, LANES), lambda i: (nb - 1 - i, 0)),
                   pl.BlockSpec((8, LANES), lambda i: (0, 0))],
        out_shape=[jax.ShapeDtypeStruct((S, LANES), BF), jax.ShapeDtypeStruct((8, LANES), F32)],
        scratch_shapes=[pltpu.VMEM((1, LANES), F32)],
        compiler_params=_cparams("arbitrary"),
    )(rowsum_ds, colsum_ds, pa, b_f_pad)


NEG_INF = float("-inf")
QK_SCALE = 1.0 / math.sqrt(HEAD_DIM)


def _half_mask(shape, half):
    lane = lax.broadcasted_iota(jnp.int32, shape, 1)
    return (lane < HEAD_DIM) if half == 0 else (lane >= HEAD_DIM)


def _valid(i, j, T, rowcol, window):
    rel = (i - j) * T + rowcol
    ok = rel >= 0
    if window is not None:
        ok = ok & (rel < window)
    return ok


def _attn_fwd(q_arr, q_col, k_arr, k_col, v_arr, v_col, n_pairs, kv_shared, T, window,
              cq_arr, ck_arr, sinks, name, comm=None):
    S = q_arr.shape[0]
    nq = S // T
    use_bias = cq_arr is not None
    use_sink = sinks is not None
    back = 0 if window is None else -(-window // T)
    grid = (n_pairs, nq)
    n_in = 3 + 2 * use_bias + use_sink

    def kern(*refs):
        refs, comm_refs = _own_refs(refs, comm, n_in, 2, 0)
        _comm_edge(comm, comm_refs, grid, first=True)
        q_ref, k_ref, v_ref = refs[:3]
        pos = 3
        if use_bias:
            cq_ref, ck_ref = refs[pos:pos + 2]
            pos += 2
        if use_sink:
            sink_ref = refs[pos]
            pos += 1
        o_ref, lse_ref = refs[pos:pos + 2]
        p_id = pl.program_id(0)
        i = pl.program_id(1)
        q = q_ref[...]
        rowcol = lax.broadcasted_iota(jnp.int32, (T, T), 0) - lax.broadcasted_iota(jnp.int32, (T, T), 1)
        lo = jnp.maximum(i - back, 0) if window is not None else 0
        outs, lses = [], []
        for half in (0, 1):
            hm = _half_mask((T, LANES), half)
            qh = (jnp.where(hm, q, 0).astype(F32) * QK_SCALE).astype(BF)
            if use_bias:
                cq = cq_ref[:, half * HEAD_DIM:half * HEAD_DIM + 1]
            if use_sink:
                m0 = jnp.full((T, 1), sink_ref[2 * p_id + half], F32)
                l0 = jnp.ones((T, 1), F32)
            else:
                m0 = jnp.full((T, 1), NEG_INF, F32)
                l0 = jnp.zeros((T, 1), F32)

            def step(j, carry, masked):
                m, l, acc = carry
                rows = pl.ds(pl.multiple_of(j * T, T), T)
                kj = k_ref[rows, :].astype(BF)
                vj = v_ref[rows, :].astype(BF)
                s = lax.dot_general(qh, kj, (((1,), (1,)), ((), ())), preferred_element_type=F32)
                if use_bias:
                    s = s + cq - ck_ref[0, half:half + 1, rows]
                if masked:
                    s = jnp.where(_valid(i, j, T, rowcol, window), s, NEG_INF)
                m_new = jnp.maximum(m, jnp.max(s, axis=1, keepdims=True))
                alpha = jnp.exp(m - m_new)
                p = jnp.exp(s - m_new)
                l_new = alpha * l + jnp.sum(p, axis=1, keepdims=True)
                acc_new = alpha * acc + jnp.dot(p.astype(BF), vj, preferred_element_type=F32)
                return m_new, l_new, acc_new

            init = (m0, l0, jnp.zeros((T, LANES), F32))
            if window is None:
                init = lax.fori_loop(0, i, functools.partial(step, masked=False), init)
                m, l, acc = step(i, init, True)
            else:
                m, l, acc = lax.fori_loop(lo, i + 1, functools.partial(step, masked=True), init)
            outs.append(acc / l)
            lses.append(m + jnp.log(l))
        hm0 = _half_mask((T, LANES), 0)
        o_ref[...] = jnp.where(hm0, outs[0], outs[1])
        lse_ref[...] = jnp.where(hm0, lses[0], lses[1])
        _comm_edge(comm, comm_refs, grid, first=False)

    kv_idx = (lambda c0: (lambda p, i: (0, c0))) if kv_shared else (lambda c0: (lambda p, i: (0, c0 + p)))
    in_specs = [pl.BlockSpec((T, LANES), lambda p, i: (i, q_col + p)),
                pl.BlockSpec((S, LANES), kv_idx(k_col)),
                pl.BlockSpec((S, LANES), kv_idx(v_col))]
    args = [q_arr, k_arr, v_arr]
    if use_bias:
        in_specs += [pl.BlockSpec((T, LANES), lambda p, i: (i, p)),
                     pl.BlockSpec((1, 2, S), lambda p, i: (p, 0, 0))]
        args += [cq_arr, ck_arr]
    if use_sink:
        in_specs.append(pl.BlockSpec(memory_space=pltpu.SMEM))
        args.append(sinks)
    out_spec = pl.BlockSpec((T, LANES), lambda p, i: (i, p))
    res = pl.pallas_call(
        kern, name=name, grid=grid,
        in_specs=in_specs + _comm_specs(comm, "in"),
        out_specs=[out_spec, out_spec] + _comm_specs(comm, "out"),
        out_shape=[jax.ShapeDtypeStruct((S, n_pairs * LANES), F32)] * 2 + (comm.out_shapes if comm else []),
        scratch_shapes=comm.sem_shapes if comm else [],
        compiler_params=_cparams("arbitrary", "arbitrary"),
    )(*args, *(comm.ins if comm else []))
    return (res[0], res[1], res[2:]) if comm else (res[0], res[1])


def _attn_bwd(q_arr, q_col, k_arr, k_col, v_arr, v_col, o_arr, do_arr, lse_arr, n_pairs, kv_shared, T,
              window, cq_arr, ck_arr, sinks, name, comm=None):
    S = q_arr.shape[0]
    nq = S // T
    use_bias = cq_arr is not None
    use_sink = sinks is not None
    back = 0 if window is None else -(-window // T)
    kv_w = LANES if kv_shared else n_pairs * LANES
    grid = (n_pairs,)
    n_in = 6 + 2 * use_bias + use_sink
    n_out = 3 + 2 * use_bias + use_sink

    def kern(*refs):
        refs, comm_refs = _own_refs(refs, comm, n_in, n_out, 0)
        _comm_edge(comm, comm_refs, grid, first=True)
        q_ref, k_ref, v_ref, o_ref, do_ref, lse_ref = refs[:6]
        pos = 6
        if use_bias:
            cq_ref, ck_ref = refs[pos:pos + 2]
            pos += 2
        if use_sink:
            sink_ref = refs[pos]
            pos += 1
        dq_ref, dk_ref, dv_ref = refs[pos:pos + 3]
        pos += 3
        if use_bias:
            dck_ref, dcq_ref = refs[pos:pos + 2]
            pos += 2
        if use_sink:
            dsink_ref = refs[pos]
        p_id = pl.program_id(0)
        rowcol = lax.broadcasted_iota(jnp.int32, (T, T), 0) - lax.broadcasted_iota(jnp.int32, (T, T), 1)

        def zero_kv():
            dk_ref[...] = jnp.zeros_like(dk_ref)
            dv_ref[...] = jnp.zeros_like(dv_ref)

        if kv_shared:
            pl.when(p_id == 0)(zero_kv)
        else:
            zero_kv()
        if use_bias:
            dck_ref[...] = jnp.zeros_like(dck_ref)
        if use_sink:
            dsink_ref[...] = jnp.zeros_like(dsink_ref)

        for half in (0, 1):
            hm = _half_mask((T, LANES), half)
            lane0 = half * HEAD_DIM

            def outer(i, carry):
                qrows = pl.ds(pl.multiple_of(i * T, T), T)
                qh = (jnp.where(hm, q_ref[qrows, :], 0).astype(F32) * QK_SCALE).astype(BF)
                do_f = jnp.where(hm, do_ref[qrows, :], 0.0)
                doh = do_f.astype(BF)
                delta = jnp.sum(do_f * o_ref[qrows, :], axis=1, keepdims=True)
                lse = lse_ref[qrows, lane0:lane0 + 1]
                if use_bias:
                    cq = cq_ref[qrows, lane0:lane0 + 1]
                lo = jnp.maximum(i - back, 0) if window is not None else 0

                def inner(j, carry_in, masked):
                    dq, rs = carry_in
                    krows = pl.ds(pl.multiple_of(j * T, T), T)
                    kj = k_ref[krows, :].astype(BF)
                    vj = v_ref[krows, :].astype(BF)
                    s = lax.dot_general(qh, kj, (((1,), (1,)), ((), ())), preferred_element_type=F32)
                    if use_bias:
                        s = s + cq - ck_ref[0, half:half + 1, krows]
                    if masked:
                        s = jnp.where(_valid(i, j, T, rowcol, window), s, NEG_INF)
                    p = jnp.exp(s - lse)
                    dp = lax.dot_general(doh, vj, (((1,), (1,)), ((), ())), preferred_element_type=F32)
                    ds = p * (dp - delta)
                    ds_b = ds.astype(BF)
                    dv_ref[krows, :] += lax.dot_general(p.astype(BF), doh, (((0,), (0,)), ((), ())),
                                                        preferred_element_type=F32)
                    dk_ref[krows, :] += lax.dot_general(ds_b, qh, (((0,), (0,)), ((), ())),
                                                        preferred_element_type=F32)
                    if use_bias:
                        dck_ref[0, half:half + 1, krows] += jnp.sum(ds, axis=0, keepdims=True)
                        rs = rs + jnp.sum(ds, axis=1, keepdims=True)
                    kh = jnp.where(hm, kj, 0)
                    return dq + jnp.dot(ds_b, kh, preferred_element_type=F32), rs

                init = (jnp.zeros((T, LANES), F32), jnp.zeros((T, 1), F32))
                if window is None:
                    init = lax.fori_loop(0, i, functools.partial(inner, masked=False), init)
                    dq, rs = inner(i, init, True)
                else:
                    dq, rs = lax.fori_loop(lo, i + 1, functools.partial(inner, masked=True), init)
                dq = dq * QK_SCALE
                if half == 0:
                    dq_ref[qrows, :] = dq
                else:
                    dq_ref[qrows, :] += dq
                if use_bias:
                    rs_b = jnp.broadcast_to(rs, (T, LANES))
                    dcq_ref[qrows, :] = rs_b if half == 0 else jnp.where(hm, rs_b, dcq_ref[qrows, :])
                if use_sink:
                    p_sink = jnp.exp(sink_ref[2 * p_id + half] - lse)
                    dsink_ref[0, half:half + 1, :] += jnp.broadcast_to(
                        -jnp.sum(p_sink * delta, axis=0, keepdims=True), (1, LANES))
                return carry

            lax.fori_loop(0, nq, outer, 0)
        _comm_edge(comm, comm_refs, grid, first=False)

    kv_idx = (lambda c0: (lambda p: (0, c0))) if kv_shared else (lambda c0: (lambda p: (0, c0 + p)))
    pair = lambda c0: pl.BlockSpec((S, LANES), lambda p: (0, c0 + p))
    in_specs = [pair(q_col), pl.BlockSpec((S, LANES), kv_idx(k_col)), pl.BlockSpec((S, LANES), kv_idx(v_col)),
                pair(0), pair(0), pair(0)]
    args = [q_arr, k_arr, v_arr, o_arr, do_arr, lse_arr]
    if use_bias:
        in_specs += [pair(0), pl.BlockSpec((1, 2, S), lambda p: (p, 0, 0))]
        args += [cq_arr, ck_arr]
    if use_sink:
        in_specs.append(pl.BlockSpec(memory_space=pltpu.SMEM))
        args.append(sinks)
    out_specs = [pair(0), pl.BlockSpec((S, LANES), kv_idx(0)), pl.BlockSpec((S, LANES), kv_idx(0))]
    out_shape = [jax.ShapeDtypeStruct((S, n_pairs * LANES), F32),
                 jax.ShapeDtypeStruct((S, kv_w), F32), jax.ShapeDtypeStruct((S, kv_w), F32)]
    if use_bias:
        out_specs += [pl.BlockSpec((1, 2, S), lambda p: (p, 0, 0)), pair(0)]
        out_shape += [jax.ShapeDtypeStruct((n_pairs, 2, S), F32), jax.ShapeDtypeStruct((S, n_pairs * LANES), F32)]
    if use_sink:
        out_specs.append(pl.BlockSpec((1, 8, LANES), lambda p: (p, 0, 0)))
        out_shape.append(jax.ShapeDtypeStruct((n_pairs, 8, LANES), F32))
    res = pl.pallas_call(
        kern, name=name, grid=grid,
        in_specs=in_specs + _comm_specs(comm, "in"),
        out_specs=out_specs + _comm_specs(comm, "out"),
        out_shape=out_shape + (comm.out_shapes if comm else []),
        scratch_shapes=comm.sem_shapes if comm else [],
        compiler_params=_cparams("arbitrary"),
    )(*args, *(comm.ins if comm else []))
    return (*res[:n_out], res[n_out:]) if comm else res


def _bias_lanes(shape, half, q_side_terms, k_side_terms):
    lane = lax.broadcasted_iota(jnp.int32, shape, 1)
    base = HEAD_DIM * (1 - half)
    n_q = len(q_side_terms) if q_side_terms is not None else 3
    n_k = len(k_side_terms) if k_side_terms is not None else 3
    out = jnp.zeros(shape, F32)
    for t in range(n_q):
        out = jnp.where(lane == base + t, q_side_terms[t].astype(F32) if q_side_terms is not None else 1.0, out)
    for t in range(n_k):
        out = jnp.where(lane == base + n_q + t,
                        k_side_terms[t].astype(F32) if k_side_terms is not None else 1.0, out)
    return out


def _head_column(block, head):
    lane = lax.broadcasted_iota(jnp.int32, block.shape, 1)
    return jnp.sum(jnp.where(lane == head, block, 0.0), axis=1, keepdims=True)


def _fox_prep_fwd(p_b, cum, T):
    S = p_b.shape[0]

    def kern(q_ref, k_ref, c_ref, qa_ref, ka_ref):
        p_id = pl.program_id(0)
        q, k, cum_blk = q_ref[...], k_ref[...], c_ref[...]
        for half in (0, 1):
            hm = _half_mask((T, LANES), half)
            c3 = _split3(_head_column(cum_blk, 2 * p_id + half))
            qa_ref[half] = jnp.where(hm, q.astype(F32) * QK_SCALE, _bias_lanes((T, LANES), half, c3, None)).astype(BF)
            ka_ref[half] = jnp.where(hm, k.astype(F32),
                                     _bias_lanes((T, LANES), half, None, [-t.astype(F32) for t in c3])).astype(BF)

    out_spec = pl.BlockSpec((None, 2, T, LANES), lambda p, i: (p, 0, i, 0))
    shape = jax.ShapeDtypeStruct((B_HEADS // 2, 2, S, LANES), BF)
    return pl.pallas_call(
        kern, name="fox_prep_fwd", grid=(B_HEADS // 2, S // T),
        in_specs=[pl.BlockSpec((T, LANES), lambda p, i: (i, OFF_QB // LANES + p)),
                  pl.BlockSpec((T, LANES), lambda p, i: (i, OFF_KB // LANES + p)),
                  pl.BlockSpec((T, LANES), lambda p, i: (i, 0))],
        out_specs=[out_spec, out_spec], out_shape=[shape, shape],
        compiler_params=_cparams("parallel", "parallel"),
    )(p_b, p_b, cum)


def _fox_fwd(q_aug, k_aug, p_b, T, comm=None):
    S = p_b.shape[0]
    nq = S // T
    n_pairs = B_HEADS // 2
    grid = (n_pairs, nq)

    def kern(*refs):
        (q_ref, k_ref, v_ref, o_ref, lse_ref), comm_refs = _own_refs(refs, comm, 3, 2, 0)
        _comm_edge(comm, comm_refs, grid, first=True)
        i = pl.program_id(1)
        rowcol = lax.broadcasted_iota(jnp.int32, (T, T), 0) - lax.broadcasted_iota(jnp.int32, (T, T), 1)
        qs = (q_ref[0], q_ref[1])

        def step(j, carry, masked):
            rows = pl.ds(pl.multiple_of(j * T, T), T)
            vj = v_ref[rows, :]
            new = []
            for half in (0, 1):
                m, l, acc = carry[half]
                s = lax.dot_general(qs[half], k_ref[half, rows, :], (((1,), (1,)), ((), ())),
                                    preferred_element_type=F32)
                if masked:
                    s = jnp.where(rowcol >= 0, s, NEG_INF)
                m_new = jnp.maximum(m, jnp.max(s, axis=1, keepdims=True))
                alpha = jnp.exp(m - m_new)
                p = jnp.exp(s - m_new)
                l_new = alpha * l + jnp.sum(p, axis=1, keepdims=True)
                acc_new = alpha * acc + jnp.dot(p.astype(BF), vj, preferred_element_type=F32)
                new.append((m_new, l_new, acc_new))
            return tuple(new)

        one = (jnp.full((T, 1), NEG_INF, F32), jnp.zeros((T, 1), F32), jnp.zeros((T, LANES), F32))
        carry = lax.fori_loop(0, i, functools.partial(step, masked=False), (one, one))
        (m0, l0, acc0), (m1, l1, acc1) = step(i, carry, True)
        hm0 = _half_mask((T, LANES), 0)
        o_ref[...] = jnp.where(hm0, acc0 / l0, acc1 / l1)
        lse_ref[...] = jnp.where(hm0, m0 + jnp.log(l0), m1 + jnp.log(l1))
        _comm_edge(comm, comm_refs, grid, first=False)

    out_spec = pl.BlockSpec((T, LANES), lambda p, i: (i, p))
    res = pl.pallas_call(
        kern, name="fox_fwd", grid=grid,
        in_specs=[pl.BlockSpec((None, 2, T, LANES), lambda p, i: (p, 0, i, 0)),
                  pl.BlockSpec((None, 2, S, LANES), lambda p, i: (p, 0, 0, 0)),
                  pl.BlockSpec((S, LANES), lambda p, i: (0, OFF_VB // LANES + p))] + _comm_specs(comm, "in"),
        out_specs=[out_spec, out_spec] + _comm_specs(comm, "out"),
        out_shape=[jax.ShapeDtypeStruct((S, n_pairs * LANES), F32)] * 2 + (comm.out_shapes if comm else []),
        scratch_shapes=comm.sem_shapes if comm else [],
        compiler_params=_cparams("arbitrary", "arbitrary"),
    )(q_aug, k_aug, p_b, *(comm.ins if comm else []))
    return res[0], res[1], res[2:]


def _fox_prep_bwd(q_aug, o, do, lse, T):
    S = o.shape[0]

    def kern(qa_ref, o_ref, do_ref, lse_ref, qb_ref, dob_ref):
        o_blk, do_blk, lse_blk = o_ref[...], do_ref[...], lse_ref[...]
        lane = lax.broadcasted_iota(jnp.int32, (T, LANES), 1)
        for half in (0, 1):
            hm = _half_mask((T, LANES), half)
            base = HEAD_DIM * (1 - half)
            qa = qa_ref[half].astype(F32)
            cq = jnp.sum(jnp.where((lane >= base) & (lane < base + 3), qa, 0.0), axis=1, keepdims=True)
            b3 = _split3(cq - lse_blk[:, HEAD_DIM * half:HEAD_DIM * half + 1])
            qb_ref[half] = jnp.where(hm, qa, _bias_lanes((T, LANES), half, b3, None)).astype(BF)
            do_f = jnp.where(hm, do_blk, 0.0)
            d3 = _split3(-jnp.sum(do_f * o_blk, axis=1, keepdims=True))
            dob_ref[half] = jnp.where(hm, do_f, _bias_lanes((T, LANES), half, d3, [])).astype(BF)

    aug = pl.BlockSpec((None, 2, T, LANES), lambda p, i: (p, 0, i, 0))
    tile = pl.BlockSpec((T, LANES), lambda p, i: (i, p))
    shape = jax.ShapeDtypeStruct((B_HEADS // 2, 2, S, LANES), BF)
    return pl.pallas_call(
        kern, name="fox_prep_bwd", grid=(B_HEADS // 2, S // T),
        in_specs=[aug, tile, tile, tile],
        out_specs=[aug, aug], out_shape=[shape, shape],
        compiler_params=_cparams("parallel", "parallel"),
    )(q_aug, o, do, lse)


def _fox_bwd(qb_aug, k_aug, dob_aug, p_b, T, comm=None):
    n_pairs, _, S, _ = qb_aug.shape
    nq = S // T
    grid = (n_pairs,)

    def kern(*refs):
        own, comm_refs = _own_refs(refs, comm, 4, 5, 0)
        q_ref, k_ref, do_ref, v_ref, dq_ref, dk_ref, dv_ref, dck_ref, dcq_ref = own
        _comm_edge(comm, comm_refs, grid, first=True)
        p_id = pl.program_id(0)
        rowcol = lax.broadcasted_iota(jnp.int32, (T, T), 0) - lax.broadcasted_iota(jnp.int32, (T, T), 1)
        lane = lax.broadcasted_iota(jnp.int32, (T, LANES), 1)
        dk_ref[...] = jnp.zeros_like(dk_ref)
        dv_ref[...] = jnp.zeros_like(dv_ref)
        dck_ref[...] = jnp.zeros_like(dck_ref)

        @pl.when(p_id == 0)
        def _():
            dcq_ref[...] = jnp.zeros_like(dcq_ref)

        hms = (_half_mask((T, LANES), 0), _half_mask((T, LANES), 1))
        v_ones = [_bias_lanes((T, LANES), h, None, []).astype(BF) for h in (0, 1)]

        def outer(i, carry):
            qrows = pl.ds(pl.multiple_of(i * T, T), T)
            qa = (q_ref[0, qrows, :], q_ref[1, qrows, :])
            doa = (do_ref[0, qrows, :], do_ref[1, qrows, :])
            q_own = [jnp.where(hms[h], qa[h], 0) for h in (0, 1)]
            do_own = [jnp.where(hms[h], doa[h], 0) for h in (0, 1)]

            def inner(j, carry_in, masked):
                krows = pl.ds(pl.multiple_of(j * T, T), T)
                vj = v_ref[krows, :]
                dv_add, dk_add, new = 0.0, 0.0, []
                for half in (0, 1):
                    dq, rs = carry_in[half]
                    ka = k_ref[half, krows, :]
                    s = lax.dot_general(qa[half], ka, (((1,), (1,)), ((), ())), preferred_element_type=F32)
                    if masked:
                        s = jnp.where(rowcol >= 0, s, NEG_INF)
                    p = jnp.exp(s)
                    ds = p * lax.dot_general(doa[half], jnp.where(hms[half], vj, v_ones[half]),
                                             (((1,), (1,)), ((), ())), preferred_element_type=F32)
                    ds_b = ds.astype(BF)
                    dv_add = dv_add + lax.dot_general(p.astype(BF), do_own[half], (((0,), (0,)), ((), ())),
                                                      preferred_element_type=F32)
                    dk_add = dk_add + lax.dot_general(ds_b, q_own[half], (((0,), (0,)), ((), ())),
                                                      preferred_element_type=F32)
                    dck_ref[half:half + 1, krows] += jnp.sum(ds, axis=0, keepdims=True)
                    new.append((dq + jnp.dot(ds_b, jnp.where(hms[half], ka, 0), preferred_element_type=F32),
                                rs + jnp.sum(ds, axis=1, keepdims=True)))
                dv_ref[krows, :] += dv_add
                dk_ref[krows, :] += dk_add
                return tuple(new)

            one = (jnp.zeros((T, LANES), F32), jnp.zeros((T, 1), F32))
            carry_in = lax.fori_loop(0, i, functools.partial(inner, masked=False), (one, one))
            (dq0, rs0), (dq1, rs1) = inner(i, carry_in, True)
            dq_ref[qrows, :] = (dq0 + dq1) * QK_SCALE
            dcq_ref[qrows, :] = jnp.where(lane == 2 * p_id, rs0, jnp.where(lane == 2 * p_id + 1, rs1,
                                                                             dcq_ref[qrows, :]))
            return carry

        lax.fori_loop(0, nq, outer, 0)
        _comm_edge(comm, comm_refs, grid, first=False)

    aug = pl.BlockSpec((None, 2, S, LANES), lambda p: (p, 0, 0, 0))
    pair = pl.BlockSpec((S, LANES), lambda p: (0, p))
    wide = jax.ShapeDtypeStruct((S, n_pairs * LANES), F32)
    res = pl.pallas_call(
        kern, name="fox_bwd", grid=grid,
        in_specs=[aug, aug, aug, pl.BlockSpec((S, LANES), lambda p: (0, OFF_VB // LANES + p))]
        + _comm_specs(comm, "in"),
        out_specs=[pair, pair, pair, pl.BlockSpec((None, 2, S), lambda p: (p, 0, 0)),
                   pl.BlockSpec((S, LANES), lambda p: (0, 0))] + _comm_specs(comm, "out"),
        out_shape=[wide, wide, wide, jax.ShapeDtypeStruct((n_pairs, 2, S), F32),
                   jax.ShapeDtypeStruct((S, LANES), F32)] + (comm.out_shapes if comm else []),
        scratch_shapes=comm.sem_shapes if comm else [],
        compiler_params=_cparams("arbitrary"),
    )(qb_aug, k_aug, dob_aug, p_b, *(comm.ins if comm else []))
    return (*res[:5], res[5:])


SWA_TQ = 256
SWA_SUB = 4


def _swa_window(i, tq):
    start = pl.multiple_of(jnp.maximum(i * tq - WINDOW, 0), LANES)
    return start, i * tq - start


def _swa_valid(offset, tq):
    rel = offset + lax.broadcasted_iota(jnp.int32, (tq, tq + WINDOW), 0) \
        - lax.broadcasted_iota(jnp.int32, (tq, tq + WINDOW), 1)
    return (rel >= 0) & (rel < WINDOW)


def _swa_fwd(qk, v_arr, v_col, sinks):
    S = qk.shape[0]
    tq = min(SWA_TQ, S - WINDOW)
    sub = min(SWA_SUB, S // tq)
    win = tq + WINDOW

    def kern(q_ref, k_ref, v_ref, sink_ref, o_ref, lse_ref):
        p_id, i = pl.program_id(0), pl.program_id(1)
        hm0 = _half_mask((tq, LANES), 0)
        for t in range(sub):
            rows = slice(t * tq, (t + 1) * tq)
            start, offset = _swa_window(i * sub + t, tq)
            kw = k_ref[pl.ds(start, win), :]
            vw = v_ref[pl.ds(start, win), :].astype(BF)
            valid = _swa_valid(offset, tq)
            q = q_ref[rows, :]
            outs, lses = [], []
            for half in (0, 1):
                hm = _half_mask((tq, LANES), half)
                qh = (jnp.where(hm, q, 0).astype(F32) * QK_SCALE).astype(BF)
                s = lax.dot_general(qh, kw, (((1,), (1,)), ((), ())), preferred_element_type=F32)
                s = jnp.where(valid, s, NEG_INF)
                sink = sink_ref[2 * p_id + half]
                m = jnp.maximum(jnp.max(s, axis=1, keepdims=True), sink)
                p = jnp.exp(s - m)
                denom = jnp.sum(p, axis=1, keepdims=True) + jnp.exp(sink - m)
                outs.append(jnp.dot(p.astype(BF), vw, preferred_element_type=F32) / denom)
                lses.append(m + jnp.log(denom))
            o_ref[rows, :] = jnp.where(hm0, outs[0], outs[1])
            lse_ref[rows, :] = jnp.where(hm0, lses[0], lses[1])

    tile = pl.BlockSpec((sub * tq, LANES), lambda p, i: (i, p))
    return pl.pallas_call(
        kern, name="swa_fwd", grid=(A_Q_HEADS // 2, S // (sub * tq)),
        in_specs=[tile, pl.BlockSpec((S, LANES), lambda p, i: (0, A_Q_HEADS // 2)),
                  pl.BlockSpec((S, LANES), lambda p, i: (0, v_col)),
                  pl.BlockSpec(memory_space=pltpu.SMEM)],
        out_specs=[tile, tile],
        out_shape=[jax.ShapeDtypeStruct((S, A_Q_HEADS * HEAD_DIM), F32)] * 2,
        compiler_params=_cparams("parallel", "arbitrary"),
    )(qk, qk, v_arr, sinks)


def _swa_bwd(qk, v_arr, v_col, o_arr, do_arr, lse_arr, sinks, comm=None):
    S = qk.shape[0]
    tq = min(SWA_TQ, S - WINDOW)
    sub = min(SWA_SUB, S // tq)
    win = tq + WINDOW
    n_pairs = A_Q_HEADS // 2
    grid = (n_pairs, S // (sub * tq))

    def kern(*refs):
        own, comm_refs = _own_refs(refs, comm, 7, 4, 0)
        q_ref, k_ref, v_ref, o_ref, do_ref, lse_ref, sink_ref, dq_ref, dk_ref, dv_ref, dsink_ref = own
        _comm_edge(comm, comm_refs, grid, first=True)
        p_id, i = pl.program_id(0), pl.program_id(1)

        @pl.when((p_id == 0) & (i == 0))
        def _():
            dk_ref[...] = jnp.zeros_like(dk_ref)
            dv_ref[...] = jnp.zeros_like(dv_ref)

        @pl.when(i == 0)
        def _():
            dsink_ref[...] = jnp.zeros_like(dsink_ref)

        for t in range(sub):
            rows = slice(t * tq, (t + 1) * tq)
            start, offset = _swa_window(i * sub + t, tq)
            wrows = pl.ds(start, win)
            kw = k_ref[wrows, :]
            vw = v_ref[wrows, :].astype(BF)
            valid = _swa_valid(offset, tq)
            q, do, o, lse2 = q_ref[rows, :], do_ref[rows, :], o_ref[rows, :], lse_ref[rows, :]
            dq = jnp.zeros((tq, LANES), F32)
            dk = jnp.zeros((win, LANES), F32)
            dv = jnp.zeros((win, LANES), F32)
            for half in (0, 1):
                hm = _half_mask((tq, LANES), half)
                lane0 = half * HEAD_DIM
                qh = (jnp.where(hm, q, 0).astype(F32) * QK_SCALE).astype(BF)
                do_f = jnp.where(hm, do, 0.0)
                doh = do_f.astype(BF)
                delta = jnp.sum(do_f * o, axis=1, keepdims=True)
                lse = lse2[:, lane0:lane0 + 1]
                s = lax.dot_general(qh, kw, (((1,), (1,)), ((), ())), preferred_element_type=F32)
                p = jnp.exp(jnp.where(valid, s, NEG_INF) - lse)
                dp = lax.dot_general(doh, vw, (((1,), (1,)), ((), ())), preferred_element_type=F32)
                ds_b = (p * (dp - delta)).astype(BF)
                dv = dv + lax.dot_general(p.astype(BF), doh, (((0,), (0,)), ((), ())),
                                          preferred_element_type=F32)
                dk = dk + lax.dot_general(ds_b, qh, (((0,), (0,)), ((), ())), preferred_element_type=F32)
                kh = jnp.where(_half_mask((win, LANES), half), kw, 0)
                dq = dq + jnp.dot(ds_b, kh, preferred_element_type=F32)
                p_sink = jnp.exp(sink_ref[2 * p_id + half] - lse)
                dsink_ref[0, half:half + 1, :] += jnp.broadcast_to(
                    -jnp.sum(p_sink * delta, axis=0, keepdims=True), (1, LANES))
            dq_ref[rows, :] = dq * QK_SCALE
            dk_ref[wrows, :] += dk
            dv_ref[wrows, :] += dv
        _comm_edge(comm, comm_refs, grid, first=False)

    tile = pl.BlockSpec((sub * tq, LANES), lambda p, i: (i, p))
    whole = lambda col: pl.BlockSpec((S, LANES), lambda p, i: (0, col))
    res = pl.pallas_call(
        kern, name="swa_bwd", grid=grid,
        in_specs=[tile, whole(n_pairs), whole(v_col), tile, tile, tile,
                  pl.BlockSpec(memory_space=pltpu.SMEM)] + _comm_specs(comm, "in"),
        out_specs=[tile, whole(0), whole(0),
                   pl.BlockSpec((1, 8, LANES), lambda p, i: (p, 0, 0))] + _comm_specs(comm, "out"),
        out_shape=[jax.ShapeDtypeStruct((S, A_Q_HEADS * HEAD_DIM), F32),
                   jax.ShapeDtypeStruct((S, LANES), F32), jax.ShapeDtypeStruct((S, LANES), F32),
                   jax.ShapeDtypeStruct((n_pairs, 8, LANES), F32)] + (comm.out_shapes if comm else []),
        scratch_shapes=comm.sem_shapes if comm else [],
        compiler_params=_cparams("arbitrary", "arbitrary"),
    )(qk, qk, v_arr, o_arr, do_arr, lse_arr, sinks, *(comm.ins if comm else []))
    return (*res[:4], res[4:])


ADAMW_BLOCK = 256 * 1024


def _adamw(w, g, m, v, name):
    R, C = w.shape
    tr, tc = _tile(R, max(8, ADAMW_BLOCK // C), 8), C

    def kern(w_ref, g_ref, m_ref, v_ref, d_ref, mo_ref, vo_ref):
        g_ = g_ref[...]
        m_new = ADAM_B1 * m_ref[...] + (1.0 - ADAM_B1) * g_
        v_new = ADAM_B2 * v_ref[...] + (1.0 - ADAM_B2) * (g_ * g_)
        m_hat = m_new / (1.0 - ADAM_B1 ** ADAM_STEP)
        v_hat = v_new / (1.0 - ADAM_B2 ** ADAM_STEP)
        d_ref[...] = -ADAM_LR * (m_hat / (jnp.sqrt(v_hat) + ADAM_EPS) + ADAM_WD * w_ref[...])
        mo_ref[...] = m_new
        vo_ref[...] = v_new

    spec = pl.BlockSpec((tr, tc), lambda i, j: (i, j))
    shape = jax.ShapeDtypeStruct((R, C), F32)
    return pl.pallas_call(
        kern, name=name, grid=(R // tr, C // tc),
        in_specs=[spec] * 4, out_specs=[spec] * 3, out_shape=[shape] * 3,
        compiler_params=_cparams("parallel", "parallel"),
    )(w, g, m, v)


def _index_operand(i):
    return jnp.reshape(i, (1,)).astype(jnp.int32)


def _add_pair(whole, got, ci, name):
    P, R, C = whole.shape
    half = R // 2
    tr = _tile(half, 256, 16)
    nb = half // tr

    def kern(ci_ref, a_ref, b_ref, o_ref, ob_ref):
        s = a_ref[...] + b_ref[...].astype(F32)
        o_ref[...] = s
        ob_ref[...] = s.astype(BF)

    spec = pl.BlockSpec((None, tr, C), lambda p, i, ci_ref: (p, i, 0))
    return pl.pallas_call(
        kern, name=name,
        grid_spec=pltpu.PrefetchScalarGridSpec(
            num_scalar_prefetch=1, grid=(P, nb),
            in_specs=[pl.BlockSpec((None, tr, C), lambda p, i, ci_ref: (p, ci_ref[0] * nb + i, 0)), spec],
            out_specs=[spec, spec]),
        out_shape=[jax.ShapeDtypeStruct((P, half, C), F32), jax.ShapeDtypeStruct((P, half, C), BF)],
        compiler_params=_cparams("parallel", "parallel"),
    )(_index_operand(ci), whole, got)


def _add_three(parts, recv, chip, name):
    _, R, C = parts.shape
    tr = _tile(R, 256, 16)

    def kern(chip_ref, o_ref, r0_ref, r1_ref, r2_ref, out_ref):
        s = ((o_ref[...] + r0_ref[...].astype(F32)) + r1_ref[...].astype(F32)) + r2_ref[...].astype(F32)
        out_ref[0] = s
        out_ref[1] = s

    slab = lambda k: pl.BlockSpec((None, tr, C), lambda i, chip_ref: (k, i, 0))
    return pl.pallas_call(
        kern, name=name,
        grid_spec=pltpu.PrefetchScalarGridSpec(
            num_scalar_prefetch=1, grid=(R // tr,),
            in_specs=[pl.BlockSpec((None, tr, C), lambda i, chip_ref: (chip_ref[0], i, 0)),
                      slab(0), slab(1), slab(2)],
            out_specs=pl.BlockSpec((2, tr, C), lambda i, chip_ref: (0, i, 0))),
        out_shape=jax.ShapeDtypeStruct((2, R, C), F32),
        compiler_params=_cparams("parallel"),
    )(_index_operand(chip), parts, recv, recv, recv)


SM_ADA, SM_G, SM_LOSS, SM_BF, SM_SINK, SM_LEN = 0, 6144, 10240, 11264, 11272, 12288


def _small_finalize(gathered):
    def kern(g_ref, tot_ref, loss_ref):
        tot = g_ref[0:1, :]
        for b in range(1, N_DEV):
            tot = tot + g_ref[b:b + 1, :]
        tot_ref[...] = tot
        sq = jnp.sum(tot[:, SM_LOSS:SM_LOSS + D_MODEL], axis=1, keepdims=True)
        loss_ref[...] = jnp.broadcast_to(sq * (0.5 / D_MODEL), (1, LANES))

    full = lambda shape: pl.BlockSpec(shape, lambda i: (0, 0))
    return pl.pallas_call(
        kern, name="small_finalize", grid=(1,),
        in_specs=[full((N_DEV, SM_LEN))],
        out_specs=[full((1, SM_LEN)), full((1, LANES))],
        out_shape=[jax.ShapeDtypeStruct((1, SM_LEN), F32), jax.ShapeDtypeStruct((1, LANES), F32)],
        compiler_params=_cparams("arbitrary"),
    )(gathered)


def _ada_dw(c_t, d_ada):
    N = d_ada.shape[1]
    tn = _tile(N, 512)

    def kern(c_ref, d_ref, o_ref):
        acc = c_ref[:, 0:1] * d_ref[0:1, :]
        for b in range(1, N_DEV):
            acc = acc + c_ref[:, b:b + 1] * d_ref[b:b + 1, :]
        o_ref[...] = acc

    return pl.pallas_call(
        kern, name="ada_dw", grid=(N // tn,),
        in_specs=[pl.BlockSpec((D_MODEL, N_DEV), lambda j: (0, 0)), pl.BlockSpec((N_DEV, tn), lambda j: (0, j))],
        out_specs=pl.BlockSpec((D_MODEL, tn), lambda j: (0, j)),
        out_shape=jax.ShapeDtypeStruct((D_MODEL, N), F32),
        compiler_params=_cparams("parallel"),
    )(c_t, d_ada)


def _here():
    return lax.axis_index("x"), lax.axis_index("y"), lax.axis_index("c")


def _other_chips(x, y):
    return [(1 - x, y), (x, 1 - y), (1 - x, 1 - y)]


_ANY = pl.BlockSpec(memory_space=pl.ANY)


class _Comm:
    def __init__(self, ins, out_shapes, sem_shapes, start, finish):
        self.ins, self.out_shapes, self.sem_shapes = list(ins), list(out_shapes), list(sem_shapes)
        self.start, self.finish = start, finish

    def split(self, refs, n_in, n_out, n_scratch):
        a = n_in + len(self.ins)
        b = a + n_out + len(self.out_shapes)
        own = list(refs[:n_in]) + list(refs[a:a + n_out]) + list(refs[b:b + n_scratch])
        mine = (refs[n_in:a], refs[a + n_out:b], refs[b + n_scratch:])
        return own, mine


def _run_comm(comm, name):
    n_in, n_out = len(comm.ins), len(comm.out_shapes)

    def body(*refs):
        parts = (refs[:n_in], refs[n_in:n_in + n_out], refs[n_in + n_out:])
        comm.start(*parts)
        comm.finish(*parts)

    return pl.pallas_call(
        body, name=name,
        in_specs=[_ANY] * n_in, out_specs=[_ANY] * n_out,
        out_shape=comm.out_shapes, scratch_shapes=comm.sem_shapes,
    )(*comm.ins)


def _gather_comm(blocks):
    L = len(blocks)

    def parts(ins, outs, sems):
        send_sems, recv_sems, local_sems = sems
        x, y, c = _here()
        me, sibling = (x, y, c), (x, y, 1 - c)
        chips = _other_chips(x, y)

        def slot(px, py, pc):
            return 4 * px + 2 * py + pc

        def copy(l, k, block, to, src=None):
            dst = outs[l].at[slot(*block)]
            return pltpu.make_async_remote_copy(
                src_ref=dst if src is None else src, dst_ref=dst,
                send_sem=send_sems.at[l, k], recv_sem=recv_sems.at[l, k],
                device_id=to, device_id_type=MESH)

        mine = [pltpu.make_async_copy(ins[l], outs[l].at[slot(*me)], local_sems.at[l]) for l in range(L)]
        first = []
        for l in range(L):
            first.append(copy(l, 0, me, sibling, src=ins[l]))
            for j, chip in enumerate(chips):
                first.append(copy(l, 1 + j, me, (*chip, c), src=ins[l]))
        return c, me, sibling, chips, copy, mine, first

    def start(ins, outs, sems):
        *_, mine, first = parts(ins, outs, sems)
        for cp in mine + first:
            cp.start()

    def finish(ins, outs, sems):
        c, me, sibling, chips, copy, mine, first = parts(ins, outs, sems)
        passed = []
        for j, chip in enumerate(chips):
            for l in range(L):
                copy(l, 1 + j, (*chip, c), me).wait_recv()
                fwd = copy(l, 4 + j, (*chip, c), sibling)
                fwd.start()
                passed.append(fwd)
        for l in range(L):
            copy(l, 0, sibling, me).wait_recv()
        for j, chip in enumerate(chips):
            for l in range(L):
                copy(l, 4 + j, (*chip, 1 - c), me).wait_recv()
        for cp in first + passed:
            cp.wait_send()
        for cp in mine:
            cp.wait()

    return _Comm(blocks, [jax.ShapeDtypeStruct((N_DEV,) + b.shape, b.dtype) for b in blocks],
                 [pltpu.SemaphoreType.DMA((L, 7)), pltpu.SemaphoreType.DMA((L, 7)), pltpu.SemaphoreType.DMA((L,))],
                 start, finish)


def _allgather8(blocks, name):
    return _run_comm(_gather_comm(blocks), name)


def _swap_comm(arrs):
    L = len(arrs)

    def copies(ins, outs, sems):
        send_sems, recv_sems = sems
        x, y, c = _here()
        cps = []
        for l in range(L):
            half = arrs[l].shape[1] // 2
            rows = pl.ds(pl.multiple_of((1 - c) * half, 16), half)
            cps.append(pltpu.make_async_remote_copy(
                src_ref=ins[l].at[:, rows, :], dst_ref=outs[l], send_sem=send_sems.at[l],
                recv_sem=recv_sems.at[l], device_id=(x, y, 1 - c), device_id_type=MESH))
        return cps

    def start(ins, outs, sems):
        for cp in copies(ins, outs, sems):
            cp.start()

    def finish(ins, outs, sems):
        for cp in copies(ins, outs, sems):
            cp.wait()

    return _Comm(arrs, [jax.ShapeDtypeStruct((a.shape[0], a.shape[1] // 2, a.shape[2]), a.dtype) for a in arrs],
                 [pltpu.SemaphoreType.DMA((L,)), pltpu.SemaphoreType.DMA((L,))], start, finish)


def _sibling_join(bufs, name):
    L = len(bufs)

    def body(*refs):
        outs = refs[L:2 * L]
        send_sems, recv_sems = refs[2 * L:]
        x, y, c = _here()
        for l in range(L):
            pltpu.make_async_remote_copy(src_ref=outs[l].at[c], dst_ref=outs[l].at[c], send_sem=send_sems.at[l],
                                         recv_sem=recv_sems.at[l], device_id=(x, y, 1 - c),
                                         device_id_type=MESH).start()
        for l in range(L):
            pltpu.make_async_remote_copy(src_ref=outs[l].at[c], dst_ref=outs[l].at[1 - c],
                                         send_sem=send_sems.at[l], recv_sem=recv_sems.at[l],
                                         device_id=(x, y, 1 - c), device_id_type=MESH).wait()

    return pl.pallas_call(
        body, name=name,
        in_specs=[_ANY] * L, out_specs=[_ANY] * L,
        out_shape=[jax.ShapeDtypeStruct(a.shape, a.dtype) for a in bufs],
        input_output_aliases={l: l for l in range(L)},
        scratch_shapes=[pltpu.SemaphoreType.DMA((L,)), pltpu.SemaphoreType.DMA((L,))],
    )(*bufs)


def _scatter_comm(arrs):
    L = len(arrs)

    def copies(ins, outs, sems):
        send_sems, recv_sems = sems
        x, y, c = _here()
        return [pltpu.make_async_remote_copy(
            src_ref=ins[l].at[2 * tx + ty], dst_ref=outs[l].at[j],
            send_sem=send_sems.at[l, j], recv_sem=recv_sems.at[l, j],
            device_id=(tx, ty, c), device_id_type=MESH)
            for l in range(L) for j, (tx, ty) in enumerate(_other_chips(x, y))]

    def start(ins, outs, sems):
        for cp in copies(ins, outs, sems):
            cp.start()

    def finish(ins, outs, sems):
        for cp in copies(ins, outs, sems):
            cp.wait()

    return _Comm(arrs, [jax.ShapeDtypeStruct((3,) + a.shape[1:], a.dtype) for a in arrs],
                 [pltpu.SemaphoreType.DMA((L, 3)), pltpu.SemaphoreType.DMA((L, 3))], start, finish)


_A_ORDER = np.array(A_HEAD_ORDER)
_A_INVERSE = np.argsort(_A_ORDER)


def _permute_in_weights(w_in):
    qa = w_in[:, 0:512].reshape(D_MODEL, A_Q_HEADS, HEAD_DIM)[:, _A_ORDER, :].reshape(D_MODEL, 512)
    f_pad = jnp.pad(w_in[:, 2304:2312], ((0, 0), (0, LANES - B_HEADS)))
    w_a = jnp.concatenate([qa, w_in[:, 512:640], f_pad], axis=1)
    return w_a, w_in[:, 640:2304], w_in[:, 2312:4360]


def _slab_segments():
    segs = [(h * HEAD_DIM, int(_A_INVERSE[h]) * HEAD_DIM, HEAD_DIM) for h in range(A_Q_HEADS)]
    segs += [(512, OFF_KA, 128), (640, W_A + OFF_VA, 128), (768, W_A + OFF_QB, 1536),
             (2304, OFF_F, B_HEADS), (2312, W_A + W_B, W_G)]
    return segs


def _shard_slabs(dw_perm):
    R = dw_perm.shape[0]
    tr = _tile(R, 128, 8)
    plan = []
    for k in range(N_CHIP):
        for b in range(W_SHARD_PAD // LANES):
            lo, hi = k * W_SHARD + b * LANES, min(k * W_SHARD + (b + 1) * LANES, (k + 1) * W_SHARD)
            parts = []
            for o0, s0, n in _slab_segments():
                a, z = max(lo, o0), min(hi, o0 + n)
                while a < z:
                    s = s0 + (a - o0)
                    run = min(z - a, LANES - s % LANES)
                    parts.append((s // LANES, ((a - lo) - s % LANES) % LANES, a - lo, run))
                    a += run
            plan.append((k, b, parts))

    def kern(x_ref, o32_ref, obf_ref):
        lane = lax.broadcasted_iota(jnp.int32, (tr, LANES), 1)
        for k, b, parts in plan:
            acc = jnp.zeros((tr, LANES), F32)
            for src, rot, first, run in parts:
                blk = x_ref[:, src * LANES:(src + 1) * LANES]
                if rot:
                    blk = pltpu.roll(blk, rot, 1)
                acc = jnp.where((lane >= first) & (lane < first + run), blk, acc)
            o32_ref[k, :, b * LANES:(b + 1) * LANES] = acc
            obf_ref[k, :, b * LANES:(b + 1) * LANES] = acc.astype(BF)

    out_spec = pl.BlockSpec((N_CHIP, tr, W_SHARD_PAD), lambda i: (0, i, 0))
    return tuple(pl.pallas_call(
        kern, name="shard_slabs", grid=(R // tr,),
        in_specs=[pl.BlockSpec((tr, W_PERM), lambda i: (i, 0))],
        out_specs=[out_spec, out_spec],
        out_shape=[jax.ShapeDtypeStruct((N_CHIP, R, W_SHARD_PAD), F32),
                   jax.ShapeDtypeStruct((N_CHIP, R, W_SHARD_PAD), BF)],
        compiler_params=_cparams("parallel"),
    )(dw_perm))


class _NoExchange:
    def __init__(self, w_in, rest):
        self.w_in_whole, self.rest, self.grads = w_in, rest, {}

    def w_in_comm(self):
        return None

    def w_in(self, outs):
        return self.w_in_whole

    def rest_weights_comm(self):
        return None

    def rest_weights(self, outs):
        return self.rest

    def swap_comm(self, pieces, tag):
        self.grads[tag] = [p32 for p32, _ in pieces]
        return None

    def swap_done(self, outs, tag):
        return None

    def reduce_done(self, outs, tag):
        pass


class _Exchange:
    def __init__(self, ci, chip, w_in_shard, rest_shards):
        self.ci, self.chip, self.w_in_shard, self.rest_shards = ci, chip, w_in_shard, rest_shards
        self.pieces, self.part_f32, self.halves = {}, {}, {}

    def _my_half(self, a, axis=0, other=False):
        rows = a.shape[axis] // 2
        return lax.dynamic_slice_in_dim(a, ((1 - self.ci) if other else self.ci) * rows, rows, axis=axis)

    def w_in_comm(self):
        return _gather_comm([self._my_half(self.w_in_shard).astype(BF)])

    def w_in(self, outs):
        return _col_sharded(outs[0])

    def rest_weights_comm(self):
        return _gather_comm([self._my_half(w).astype(BF) for w in self.rest_shards])

    def rest_weights(self, outs):
        w_ba, w_bb, w_out, w_fi, w_fo = outs
        return (_col_sharded(w_ba), _col_sharded(w_bb), _row_sharded(w_out), _col_sharded(w_fi),
                _row_sharded(w_fo))

    def swap_comm(self, pieces, tag):
        self.pieces[tag] = pieces
        return _swap_comm([pbf for _, pbf in pieces])

    def swap_done(self, got, tag):
        self.part_f32[tag], part_bf = [], []
        for l, ((p32, _), g_) in enumerate(zip(self.pieces[tag], got)):
            s32, sbf = _add_pair(p32, g_, self.ci, f"chip_sum_{tag}_{l}")
            self.part_f32[tag].append(s32)
            part_bf.append(sbf)
        return _scatter_comm(part_bf)

    def reduce_done(self, outs, tag):
        self.halves[tag] = [_add_three(p32, r, self.chip, f"shard_sum_{tag}_{l}")
                            for l, (p32, r) in enumerate(zip(self.part_f32[tag], outs))]


def _col_sharded(g):
    return jnp.transpose(g.reshape(N_CHIP, -1, g.shape[-1]), (1, 0, 2)).reshape(2 * g.shape[1], N_CHIP * g.shape[-1])


def _row_sharded(g):
    return g.reshape(N_DEV * g.shape[1], g.shape[-1])


def _rope_tables(pos):
    inv_freq = 1.0 / (ROPE_THETA ** (jnp.arange(0, HEAD_DIM, 2, dtype=F32) / HEAD_DIM))
    ang = pos.astype(F32)[:, None] * inv_freq
    cos, sin = jnp.cos(ang), jnp.sin(ang)
    return jnp.tile(cos, (1, 4)), jnp.tile(jnp.concatenate([-sin, sin], axis=1), (1, 2))


def _local_step(x, pos, ada, g1, g2, g3, g4, b_f, sinks, exch, target):
    S = x.shape[0]
    t_fox = _tile(S, 512, LANES) if S >= 1024 else S // 2
    t_fox_fwd = _tile(S, 1024, LANES) if S >= 2048 else S // 2
    shift_m, scale_m, gate_m, shift_f, scale_f, gate_f = [ada[i:i + 1] for i in range(N_ADA)]
    cos_t, sin_t = _rope_tables(pos)
    sinks_p = sinks.reshape(A_KV_HEADS, 4).T.reshape(A_Q_HEADS)
    b_f_pad = jnp.pad(b_f, (0, LANES - B_HEADS)).reshape(1, LANES)

    h1, outs = _pre_norm(x, g1, scale_m, shift_m, "pre_mix_norm", comm=exch.w_in_comm())
    w_a, w_b, w_g = _permute_in_weights(exch.w_in(outs))
    w_perm = jnp.concatenate([w_a, w_b, w_g], axis=1)
    p_a = _mm(h1, w_a, "nn", F32, "proj_a")
    p_b = _mm(h1, w_b, "nn", BF, "proj_b")
    p_g = _mm(h1, w_g, "nn", BF, "proj_g")
    (qk_a,) = _rope([p_a], [640], cos_t, sin_t, "rope_fwd")
    o_a, lse_a = _swa_fwd(qk_a, p_b, 0, sinks_p)
    q_aug, k_aug = _fox_prep_fwd(p_b, _fox_gate_fwd(p_a, b_f_pad), t_fox)
    comm = exch.rest_weights_comm()
    o_b, lse_b, outs = _fox_fwd(q_aug, k_aug, p_b, t_fox_fwd, comm=comm)
    w_ba, w_bb, w_out, w_fi, w_fo = exch.rest_weights(outs)
    w_ba_p = w_ba.reshape(A_Q_HEADS, HEAD_DIM, D_MODEL)[_A_ORDER].reshape(512, D_MODEL)
    pa = _mm(o_a, w_ba_p, "nn", BF, "branch_a")
    pb = _mm(o_b, w_bb, "nn", BF, "branch_b")
    merged = _merge_fwd(p_g, pa, pb)
    y1 = _mm(merged, w_out, "nn", BF, "out_proj")
    x2, h2 = _post_pre(x, y1, g2, gate_m, g3, scale_f, shift_f)
    gu = _mm(h2, w_fi, "nn", BF, "ffn_in")
    act = _swiglu_fwd(gu)
    y2 = _mm(act, w_fo, "nn", BF, "ffn_out")
    d_out, d_y2, st_f = _final(x2, y2, g4, gate_f, target)

    d_act = _mm(d_y2, w_fo, "nt", BF, "ffn_out_dx")
    row_pieces = lambda pair: tuple(t.reshape(N_CHIP, t.shape[0] // N_CHIP, t.shape[1]) for t in pair)
    dw_fo = row_pieces(_mm(act, d_y2, "tn", F32, "ffn_out_dw", twin=True))
    d_gu = _swiglu_bwd(d_act, gu)
    d_h2 = _mm(d_gu, w_fi, "nt", BF, "ffn_in_dx")
    dw_fi = _mm(h2, d_gu, "tn", F32, "ffn_in_dw", col_pieces=N_CHIP, twin=True)
    d_x2, d_y1, st_m = _mid_bwd(d_h2, x2, d_out, y1, g3, scale_f, g2, gate_m)
    d_merged = _mm(d_y1, w_out, "nt", BF, "out_proj_dx")
    dw_out = row_pieces(_mm(merged, d_y1, "tn", F32, "out_proj_dw", twin=True))
    d_pa, d_pb, d_ga, d_gb = _merge_bwd(d_merged, p_g, pa, pb)
    d_oa = _mm(d_pa, w_ba_p, "nt", F32, "branch_a_dx")
    dw_ba_p = _mm(o_a, d_pa, "tn", F32, "branch_a_dw", col_pieces=N_CHIP, twin=True)
    d_ob = _mm(d_pb, w_bb, "nt", F32, "branch_b_dx")
    dw_bb = _mm(o_b, d_pb, "tn", F32, "branch_b_dw", col_pieces=N_CHIP, twin=True)
    head_rows = lambda t: t.reshape(N_CHIP, A_Q_HEADS, HEAD_DIM, -1)[:, _A_INVERSE].reshape(t.shape)
    dw_ba = tuple(head_rows(t) for t in dw_ba_p)
    comm = exch.swap_comm([dw_ba, dw_bb, dw_out, dw_fi, dw_fo], "early")
    dq_a, dk_a, dv_a, d_sink, outs = _swa_bwd(qk_a, p_b, 0, o_a, d_oa, lse_a, sinks_p, comm=comm)
    comm = exch.swap_done(outs, "early")
    qb_aug, dob_aug = _fox_prep_bwd(q_aug, o_b, d_ob, lse_b, t_fox)
    dq_b, dk_b, dv_b, d_ck, d_cq, outs = _fox_bwd(qb_aug, k_aug, dob_aug, p_b, t_fox, comm=comm)
    exch.reduce_done(outs, "early")
    d_qa, d_ka = _rope([dq_a, dk_a], [512, LANES], cos_t, -sin_t, "rope_bwd")
    d_ck_cols = jnp.pad(d_ck.reshape(B_HEADS, S).T, ((0, 0), (0, LANES - B_HEADS)))
    d_f, d_bf = _fox_gate_bwd(d_cq, d_ck_cols, p_a, b_f_pad)
    d_proj = jnp.concatenate([d_qa, d_ka, d_f, dv_a.astype(BF), dq_b.astype(BF), dk_b.astype(BF),
                              dv_b.astype(BF), d_ga, d_gb], axis=1)
    dw_perm = _mm(h1, d_proj, "tn", F32, "proj_dw")
    swap = exch.swap_comm([_shard_slabs(dw_perm)], "late")
    comm = exch.swap_done(_run_comm(swap, "grads_to_sibling_late") if swap else None, "late")
    res = _mm(d_proj, w_perm, "nt", BF, "proj_dx", comm=comm)
    d_h1 = res[0] if comm else res
    exch.reduce_done(res[1] if comm else None, "late")
    grad_x, st_p = _pre_bwd(d_h1, x, d_x2, g1, scale_m)

    d_sinks = d_sink[:, :2, 0].T.reshape(A_Q_HEADS)
    small = jnp.concatenate([
        st_p[0], st_p[1], st_m[3], st_m[0], st_m[1], st_f[0],
        st_p[2], st_m[4], st_m[2], st_f[1],
        st_f[2], d_bf[0, :B_HEADS], d_sinks,
        jnp.zeros((SM_LEN - SM_SINK - A_Q_HEADS,), F32)])
    return grad_x, small


def kernel(x, c, positions, w_ada, b_ada, g_pre_mix, g_post_mix, w_in, b_f, sinks, w_branch_a, w_branch_b, w_out, g_pre_ffn, g_post_ffn, w_ffn_in, w_ffn_out, loss_target, m_w_ada, m_b_ada, m_g_pre_mix, m_g_post_mix, m_w_in, m_b_f, m_sinks, m_w_branch_a, m_w_branch_b, m_w_out, m_g_pre_ffn, m_g_post_ffn, m_w_ffn_in, m_w_ffn_out, v_w_ada, v_b_ada, v_g_pre_mix, v_g_post_mix, v_w_in, v_b_f, v_sinks, v_w_branch_a, v_w_branch_b, v_w_out, v_g_pre_ffn, v_g_post_ffn, v_w_ffn_in, v_w_ffn_out):
    xi, yi, ci = _here()
    chip = 2 * xi + yi
    dev = 2 * chip + ci

    (c_g,) = _allgather8([c.reshape(8, LANES)], "gather_c")
    c_all = c_g.reshape(N_DEV, D_MODEL)
    exch = _Exchange(ci, chip, w_in[0], [w_branch_a[0], w_branch_b[0], w_out[0], w_ffn_in[0], w_ffn_out[0]])

    ada_cols = _mm(c_all, w_ada[0], "nn", F32, "ada_fwd")
    (ada_g,) = _allgather8([ada_cols], "gather_ada")
    ada_mine = lax.dynamic_index_in_dim(ada_g.reshape(N_CHIP, 2, N_DEV, -1)[:, 0], dev, axis=1, keepdims=False)
    ada = (ada_mine.reshape(-1) + b_ada[0]).reshape(N_ADA, D_MODEL)

    grad_x, small = _local_step(
        x[0], positions[0], ada, g_pre_mix, g_post_mix, g_pre_ffn, g_post_ffn, b_f[0], sinks[0],
        exch, loss_target[0])

    (small_g,) = _allgather8([small.reshape(8, SM_LEN // 8)], "gather_small")
    small_all = small_g.reshape(N_DEV, SM_LEN)
    small_tot, loss_row = _small_finalize(small_all)
    loss = loss_row[0, 0]
    d_ada_cols = lax.dynamic_slice_in_dim(small_all[:, :N_ADA * D_MODEL], chip * (N_ADA * D_MODEL // N_CHIP),
                                          N_ADA * D_MODEL // N_CHIP, axis=1)
    g_w_ada = _ada_dw(c_all.T, d_ada_cols)

    joined = _sibling_join(exch.halves["late"] + exch.halves["early"], "grads_join")
    g_w_in, g_w_ba, g_w_bb, g_w_out, g_w_fi, g_w_fo = [j.reshape(2 * j.shape[1], j.shape[2]) for j in joined]

    def small_vec(b_ada_, g1_, g2_, g3_, g4_, b_f_, sinks_):
        return jnp.concatenate([b_ada_[0], g1_[0], g2_[0], g3_[0], g4_[0], jnp.zeros((D_MODEL,), F32),
                                b_f_[0], sinks_[0], jnp.zeros((SM_LEN - SM_SINK - A_Q_HEADS,), F32)]
                               ).reshape(8, SM_LEN // 8)

    sw = small_vec(b_ada, g_pre_mix, g_post_mix, g_pre_ffn, g_post_ffn, b_f, sinks)
    sm = small_vec(m_b_ada, m_g_pre_mix, m_g_post_mix, m_g_pre_ffn, m_g_post_ffn, m_b_f, m_sinks)
    sv = small_vec(v_b_ada, v_g_pre_mix, v_g_post_mix, v_g_pre_ffn, v_g_post_ffn, v_b_f, v_sinks)
    s_upd = [u.reshape(SM_LEN) for u in _adamw(sw, small_tot.reshape(8, SM_LEN // 8), sm, sv, "adamw_small")]
    s_grad = small_tot.reshape(SM_LEN)

    def unpack(vec):
        row = lambda a, n: vec[a:a + n].reshape(1, n)
        return dict(b_ada=row(SM_ADA, N_ADA * D_MODEL), g_pre_mix=row(SM_G, D_MODEL),
                    g_post_mix=row(SM_G + D_MODEL, D_MODEL), g_pre_ffn=row(SM_G + 2 * D_MODEL, D_MODEL),
                    g_post_ffn=row(SM_G + 3 * D_MODEL, D_MODEL), b_f=row(SM_BF, B_HEADS),
                    sinks=row(SM_SINK, A_Q_HEADS))

    big = dict(
        w_ada=(w_ada, g_w_ada, m_w_ada, v_w_ada),
        w_branch_a=(w_branch_a, g_w_ba, m_w_branch_a, v_w_branch_a),
        w_branch_b=(w_branch_b, g_w_bb, m_w_branch_b, v_w_branch_b),
        w_out=(w_out, g_w_out, m_w_out, v_w_out), w_ffn_in=(w_ffn_in, g_w_fi, m_w_ffn_in, v_w_ffn_in),
        w_ffn_out=(w_ffn_out, g_w_fo, m_w_ffn_out, v_w_ffn_out))
    grads, deltas, new_m, new_v = unpack(s_grad), unpack(s_upd[0]), unpack(s_upd[1]), unpack(s_upd[2])
    for n, (w_, g_, m_, v_) in big.items():
        d_, nm_, nv_ = _adamw(w_[0], g_, m_[0], v_[0], "adamw_" + n)
        grads[n], deltas[n], new_m[n], new_v[n] = g_[None], d_[None], nm_[None], nv_[None]
    pad_cols = lambda a: jnp.pad(a, ((0, 0), (0, W_SHARD_PAD - W_SHARD)))
    upd = _adamw(pad_cols(w_in[0]), g_w_in, pad_cols(m_w_in[0]), pad_cols(v_w_in[0]), "adamw_w_in")
    grads["w_in"], deltas["w_in"], new_m["w_in"], new_v["w_in"] = [t[None, :, :W_SHARD] for t in (g_w_in, *upd)]

    names = ["w_ada", "b_ada", "g_pre_mix", "g_post_mix", "w_in", "b_f", "sinks", "w_branch_a", "w_branch_b",
             "w_out", "g_pre_ffn", "g_post_ffn", "w_ffn_in", "w_ffn_out"]
    return (loss, grad_x[None], *[grads[n] for n in names], *[deltas[n] for n in names],
            *[new_m[n] for n in names], *[new_v[n] for n in names])
```

```python
import functools
import math

import numpy as np
import jax
import jax.numpy as jnp
from jax import lax
from jax.experimental import pallas as pl
from jax.experimental.pallas import tpu as pltpu

F32 = jnp.float32
BF = jnp.bfloat16

D_MODEL = 1024
HEAD_DIM = 64
LANES = 128
WINDOW = 128
A_Q_HEADS = 8
A_KV_HEADS = 2
B_HEADS = 8
D_FF = 2816
ROPE_THETA = 10000.0
RMS_EPS = 1e-6
N_ADA = 6
N_DEV = 8
N_CHIP = 4

ADAM_LR = 0.001
ADAM_B1 = 0.9
ADAM_B2 = 0.999
ADAM_EPS = 1e-08
ADAM_WD = 0.01
ADAM_STEP = 10

VMEM_LIMIT = 48 * 1024 * 1024
MESH = pl.DeviceIdType.MESH

A_HEAD_ORDER = (0, 4, 1, 5, 2, 6, 3, 7)

OFF_QA, OFF_KA, OFF_F = 0, 512, 640
W_A = 768
OFF_VA, OFF_QB, OFF_KB, OFF_VB = 0, 128, 640, 1152
W_B = 1664
W_G = 2048
W_PERM = W_A + W_B + W_G
W_SHARD = 1090
W_SHARD_PAD = 1152


def _tile(n, cap, mult=LANES):
    if n <= cap:
        return n
    t = (cap // mult) * mult
    while t >= mult:
        if n % t == 0:
            return t
        t -= mult
    raise ValueError(f"no tile for {n}")


MXU_WIDTH = 256
MM_OPERAND_BYTES = 28 * 1024 * 1024


def _mm_tiles(M, N, K, a_bytes, b_bytes, tm_cap, tn_cap):
    tm = _tile(M, tm_cap)
    try:
        tn = _tile(N, tn_cap, MXU_WIDTH)
    except ValueError:
        tn = _tile(N, tn_cap)
    fits = lambda tk: 2 * tk * (tm * a_bytes + tn * b_bytes) <= MM_OPERAND_BYTES
    tk = K if fits(K) else next(t for t in range(K // LANES * LANES, 0, -LANES) if K % t == 0 and fits(t))
    return tm, tn, tk


def _cparams(*sem):
    return pltpu.CompilerParams(dimension_semantics=sem, vmem_limit_bytes=VMEM_LIMIT)


def _own_refs(refs, comm, n_in, n_out, n_scratch):
    if comm is None:
        return list(refs), None
    return comm.split(refs, n_in, n_out, n_scratch)


def _comm_specs(comm, side):
    if comm is None:
        return []
    return [pl.BlockSpec(memory_space=pl.ANY)] * len(comm.ins if side == "in" else comm.out_shapes)


def _comm_edge(comm, comm_refs, grid, first):
    if comm is None:
        return
    at_edge = None
    for axis, n in enumerate(grid):
        here = pl.program_id(axis) == (0 if first else n - 1)
        at_edge = here if at_edge is None else at_edge & here
    pl.when(at_edge)(lambda: (comm.start if first else comm.finish)(*comm_refs))


def _mm(a, b, mode, out_dtype, name, tm_cap=512, tn_cap=2816, comm=None, col_pieces=1, twin=False):
    if mode == "nn":
        (M, K), (K2, N) = a.shape, b.shape
        dims = (((1,), (0,)), ((), ()))
    elif mode == "nt":
        (M, K), (N, K2) = a.shape, b.shape
        dims = (((1,), (1,)), ((), ()))
    else:
        (K, M), (K2, N) = a.shape, b.shape
        dims = (((0,), (0,)), ((), ()))
    assert K == K2, (a.shape, b.shape, mode)
    tm, tn, tk = _mm_tiles(M, N // col_pieces, K, a.dtype.itemsize, b.dtype.itemsize, tm_cap, tn_cap)
    nk = K // tk
    n_out = 2 if twin else 1
    n_scratch = 1 if nk > 1 else 0
    if mode == "nn":
        a_spec = pl.BlockSpec((tm, tk), lambda i, j, k: (i, k))
        b_spec = pl.BlockSpec((tk, tn), lambda i, j, k: (k, j))
    elif mode == "nt":
        a_spec = pl.BlockSpec((tm, tk), lambda i, j, k: (i, k))
        b_spec = pl.BlockSpec((tn, tk), lambda i, j, k: (j, k))
    else:
        a_spec = pl.BlockSpec((tk, tm), lambda i, j, k: (k, i))
        b_spec = pl.BlockSpec((tk, tn), lambda i, j, k: (k, j))

    grid = (M // tm, N // tn, nk)

    def kern(*refs):
        own, comm_refs = _own_refs(refs, comm, 2, n_out, n_scratch)
        a_ref, b_ref, o_refs = own[0], own[1], own[2:2 + n_out]
        k = pl.program_id(2)
        _comm_edge(comm, comm_refs, grid, first=True)
        part = lax.dot_general(a_ref[...].astype(BF), b_ref[...].astype(BF), dims,
                               preferred_element_type=F32)
        if nk == 1:
            for o_ref in o_refs:
                o_ref[...] = part.astype(o_ref.dtype)
        else:
            acc_ref = own[2 + n_out]

            @pl.when(k == 0)
            def _():
                acc_ref[...] = part

            @pl.when(k > 0)
            def _():
                acc_ref[...] += part

            @pl.when(k == nk - 1)
            def _():
                for o_ref in o_refs:
                    o_ref[...] = acc_ref[...].astype(o_ref.dtype)

        _comm_edge(comm, comm_refs, grid, first=False)

    if col_pieces > 1:
        per = N // col_pieces // tn
        out_spec = pl.BlockSpec((None, tm, tn), lambda i, j, k: (j // per, i, j % per))
        shape = (col_pieces, M, N // col_pieces)
    else:
        out_spec = pl.BlockSpec((tm, tn), lambda i, j, k: (i, j))
        shape = (M, N)
    dtypes = [out_dtype, BF] if twin else [out_dtype]
    res = pl.pallas_call(
        kern, name=name, grid=grid,
        in_specs=[a_spec, b_spec] + _comm_specs(comm, "in"),
        out_specs=[out_spec] * n_out + _comm_specs(comm, "out"),
        out_shape=[jax.ShapeDtypeStruct(shape, d) for d in dtypes] + (comm.out_shapes if comm else []),
        scratch_shapes=[pltpu.VMEM((tm, tn), F32)] * n_scratch + (comm.sem_shapes if comm else []),
        compiler_params=_cparams("parallel", "parallel", "arbitrary"),
    )(a, b, *(comm.ins if comm else []))
    own = res[0] if n_out == 1 else tuple(res[:n_out])
    return (own, res[n_out:]) if comm else own


ROWS = 512


def _row_spec(tm, width=D_MODEL, col=0):
    return pl.BlockSpec((tm, width), lambda i: (i, col))


def _vec_spec(width=D_MODEL):
    return pl.BlockSpec((1, width), lambda i: (0, 0))


def _rms(x):
    return lax.rsqrt(jnp.mean(x * x, axis=-1, keepdims=True) + RMS_EPS)


def _colsum(x):
    return jnp.sum(x, axis=0, keepdims=True)


def _norm_bwd(d_xn, xn, r):
    return r * (d_xn - xn * jnp.mean(d_xn * xn, axis=-1, keepdims=True))


def _pre_norm(x, g, scale, shift, name, comm=None):
    S = x.shape[0]
    tm = _tile(S, ROWS, 8)
    grid = (S // tm,)

    def kern(*refs):
        (x_ref, g_ref, sc_ref, sh_ref, h_ref), comm_refs = _own_refs(refs, comm, 4, 1, 0)
        _comm_edge(comm, comm_refs, grid, first=True)
        xf = x_ref[...]
        y = xf * _rms(xf) * g_ref[...]
        h_ref[...] = (y * (1.0 + sc_ref[...]) + sh_ref[...]).astype(BF)
        _comm_edge(comm, comm_refs, grid, first=False)

    res = pl.pallas_call(
        kern, name=name, grid=grid,
        in_specs=[_row_spec(tm), _vec_spec(), _vec_spec(), _vec_spec()] + _comm_specs(comm, "in"),
        out_specs=[_row_spec(tm)] + _comm_specs(comm, "out"),
        out_shape=[jax.ShapeDtypeStruct((S, D_MODEL), BF)] + (comm.out_shapes if comm else []),
        scratch_shapes=comm.sem_shapes if comm else [],
        compiler_params=_cparams("arbitrary"),
    )(x, g, scale, shift, *(comm.ins if comm else []))
    return res[0], res[1:]


def _post_pre(x, y1, g2, gate_m, g3, scale_f, shift_f):
    S = x.shape[0]
    tm = _tile(S, ROWS, 8)

    def kern(x_ref, y_ref, g2_ref, gm_ref, g3_ref, sc_ref, sh_ref, x2_ref, h2_ref):
        y = y_ref[...].astype(F32)
        n2 = y * _rms(y) * g2_ref[...]
        x2 = x_ref[...] + gm_ref[...] * n2
        x2_ref[...] = x2
        n3 = x2 * _rms(x2) * g3_ref[...]
        h2_ref[...] = (n3 * (1.0 + sc_ref[...]) + sh_ref[...]).astype(BF)

    return pl.pallas_call(
        kern, name="post_mix_pre_ffn", grid=(S // tm,),
        in_specs=[_row_spec(tm), _row_spec(tm)] + [_vec_spec()] * 5,
        out_specs=[_row_spec(tm), _row_spec(tm)],
        out_shape=[jax.ShapeDtypeStruct((S, D_MODEL), F32), jax.ShapeDtypeStruct((S, D_MODEL), BF)],
        compiler_params=_cparams("parallel"),
    )(x, y1, g2, gate_m, g3, scale_f, shift_f)


def _stats_spec():
    return pl.BlockSpec((8, D_MODEL), lambda i: (0, 0))


def _final(x2, y2, g4, gate_f, target):
    S = x2.shape[0]
    tm = _tile(S, ROWS, 8)

    def kern(x2_ref, y_ref, g4_ref, gf_ref, t_ref, dout_ref, dy_ref, st_ref):
        @pl.when(pl.program_id(0) == 0)
        def _():
            st_ref[...] = jnp.zeros_like(st_ref)

        y = y_ref[...].astype(F32)
        r = _rms(y)
        yn = y * r
        n4 = yn * g4_ref[...]
        diff = x2_ref[...] + gf_ref[...] * n4 - t_ref[...]
        d_out = diff / D_MODEL
        dout_ref[...] = d_out
        dn = d_out * gf_ref[...]
        dy_ref[...] = _norm_bwd(dn * g4_ref[...], yn, r).astype(BF)
        st_ref[0:1, :] += _colsum(d_out * n4)
        st_ref[1:2, :] += _colsum(dn * yn)
        st_ref[2:3, :] += _colsum(diff * diff)

    return pl.pallas_call(
        kern, name="final_loss", grid=(S // tm,),
        in_specs=[_row_spec(tm), _row_spec(tm), _vec_spec(), _vec_spec(), _row_spec(tm)],
        out_specs=[_row_spec(tm), _row_spec(tm), _stats_spec()],
        out_shape=[jax.ShapeDtypeStruct((S, D_MODEL), F32), jax.ShapeDtypeStruct((S, D_MODEL), BF),
                   jax.ShapeDtypeStruct((8, D_MODEL), F32)],
        compiler_params=_cparams("arbitrary"),
    )(x2, y2, g4, gate_f, target)


def _mid_bwd(d_h2, x2, d_out, y1, g3, scale_f, g2, gate_m):
    S = x2.shape[0]
    tm = _tile(S, ROWS, 8)

    def kern(dh_ref, x2_ref, dout_ref, y_ref, g3_ref, sc_ref, g2_ref, gm_ref, dx2_ref, dy_ref, st_ref):
        @pl.when(pl.program_id(0) == 0)
        def _():
            st_ref[...] = jnp.zeros_like(st_ref)

        dh = dh_ref[...].astype(F32)
        x2 = x2_ref[...]
        r3 = _rms(x2)
        xn = x2 * r3
        one_sc = 1.0 + sc_ref[...]
        d_x2 = dout_ref[...] + _norm_bwd(dh * one_sc * g3_ref[...], xn, r3)
        dx2_ref[...] = d_x2
        y = y_ref[...].astype(F32)
        r2 = _rms(y)
        yn = y * r2
        dn = d_x2 * gm_ref[...]
        dy_ref[...] = _norm_bwd(dn * g2_ref[...], yn, r2).astype(BF)
        st_ref[0:1, :] += _colsum(dh)
        st_ref[1:2, :] += _colsum(dh * (xn * g3_ref[...]))
        st_ref[2:3, :] += _colsum(dh * one_sc * xn)
        st_ref[3:4, :] += _colsum(d_x2 * (yn * g2_ref[...]))
        st_ref[4:5, :] += _colsum(dn * yn)

    return pl.pallas_call(
        kern, name="mid_bwd", grid=(S // tm,),
        in_specs=[_row_spec(tm)] * 4 + [_vec_spec()] * 4,
        out_specs=[_row_spec(tm), _row_spec(tm), _stats_spec()],
        out_shape=[jax.ShapeDtypeStruct((S, D_MODEL), F32), jax.ShapeDtypeStruct((S, D_MODEL), BF),
                   jax.ShapeDtypeStruct((8, D_MODEL), F32)],
        compiler_params=_cparams("arbitrary"),
    )(d_h2, x2, d_out, y1, g3, scale_f, g2, gate_m)


def _pre_bwd(d_h1, x, d_x2, g1, scale_m):
    S = x.shape[0]
    tm = _tile(S, ROWS, 8)

    def kern(dh_ref, x_ref, dx2_ref, g_ref, sc_ref, gx_ref, st_ref):
        @pl.when(pl.program_id(0) == 0)
        def _():
            st_ref[...] = jnp.zeros_like(st_ref)

        dh = dh_ref[...].astype(F32)
        xf = x_ref[...]
        r = _rms(xf)
        xn = xf * r
        one_sc = 1.0 + sc_ref[...]
        gx_ref[...] = dx2_ref[...] + _norm_bwd(dh * one_sc * g_ref[...], xn, r)
        st_ref[0:1, :] += _colsum(dh)
        st_ref[1:2, :] += _colsum(dh * (xn * g_ref[...]))
        st_ref[2:3, :] += _colsum(dh * one_sc * xn)

    return pl.pallas_call(
        kern, name="pre_mix_bwd", grid=(S // tm,),
        in_specs=[_row_spec(tm)] * 3 + [_vec_spec()] * 2,
        out_specs=[_row_spec(tm), _stats_spec()],
        out_shape=[jax.ShapeDtypeStruct((S, D_MODEL), F32), jax.ShapeDtypeStruct((8, D_MODEL), F32)],
        compiler_params=_cparams("arbitrary"),
    )(d_h1, x, d_x2, g1, scale_m)


def _rope(xs, widths, cos_t, sin_t, name):
    S = xs[0].shape[0]
    tm = _tile(S, 512, 8)
    n = len(xs)

    def kern(*refs):
        cos = refs[n][...]
        sin = refs[n + 1][...]
        first = (lax.broadcasted_iota(jnp.int32, cos.shape, 1) % HEAD_DIM) < HEAD_DIM // 2
        for x_ref, o_ref, w in zip(refs[:n], refs[n + 2:], widths):
            for c0 in range(0, w, LANES):
                v = x_ref[:, c0:c0 + LANES]
                partner = jnp.where(first, pltpu.roll(v, LANES - HEAD_DIM // 2, 1),
                                    pltpu.roll(v, HEAD_DIM // 2, 1))
                o_ref[:, c0:c0 + LANES] = (v * cos + partner * sin).astype(BF)

    return pl.pallas_call(
        kern, name=name, grid=(S // tm,),
        in_specs=[_row_spec(tm, w) for w in widths] + [_row_spec(tm, LANES)] * 2,
        out_specs=[_row_spec(tm, w) for w in widths],
        out_shape=[jax.ShapeDtypeStruct((S, w), BF) for w in widths],
        compiler_params=_cparams("parallel"),
    )(*xs, cos_t, sin_t)


def _merge_fwd(pg, pa, pb):
    S = pa.shape[0]
    tm = _tile(S, ROWS, 8)

    def kern(ga_ref, gb_ref, pa_ref, pb_ref, o_ref):
        ga = jax.nn.sigmoid(ga_ref[...].astype(F32))
        gb = jax.nn.sigmoid(gb_ref[...].astype(F32))
        o_ref[...] = (ga * pa_ref[...].astype(F32) + gb * pb_ref[...].astype(F32)).astype(BF)

    return pl.pallas_call(
        kern, name="merge_fwd", grid=(S // tm,),
        in_specs=[_row_spec(tm, col=0), _row_spec(tm, col=1), _row_spec(tm), _row_spec(tm)],
        out_specs=_row_spec(tm),
        out_shape=jax.ShapeDtypeStruct((S, D_MODEL), BF),
        compiler_params=_cparams("parallel"),
    )(pg, pg, pa, pb)


def _merge_bwd(d_merged, pg, pa, pb):
    S = pa.shape[0]
    tm = _tile(S, ROWS, 8)

    def kern(dm_ref, ga_ref, gb_ref, pa_ref, pb_ref, dpa_ref, dpb_ref, dga_ref, dgb_ref):
        dm = dm_ref[...].astype(F32)
        ga = jax.nn.sigmoid(ga_ref[...].astype(F32))
        gb = jax.nn.sigmoid(gb_ref[...].astype(F32))
        dpa_ref[...] = (dm * ga).astype(BF)
        dpb_ref[...] = (dm * gb).astype(BF)
        dga_ref[...] = (dm * pa_ref[...].astype(F32) * ga * (1.0 - ga)).astype(BF)
        dgb_ref[...] = (dm * pb_ref[...].astype(F32) * gb * (1.0 - gb)).astype(BF)

    bf_out = jax.ShapeDtypeStruct((S, D_MODEL), BF)
    return pl.pallas_call(
        kern, name="merge_bwd", grid=(S // tm,),
        in_specs=[_row_spec(tm), _row_spec(tm, col=0), _row_spec(tm, col=1), _row_spec(tm), _row_spec(tm)],
        out_specs=[_row_spec(tm)] * 4,
        out_shape=[bf_out] * 4,
        compiler_params=_cparams("parallel"),
    )(d_merged, pg, pg, pa, pb)


def _swiglu_fwd(gu):
    S = gu.shape[0]
    tm = _tile(S, ROWS, 8)
    tc = _tile(D_FF, 1408)
    nc = D_FF // tc

    def kern(g_ref, u_ref, o_ref):
        g = g_ref[...].astype(F32)
        o_ref[...] = (g * jax.nn.sigmoid(g) * u_ref[...].astype(F32)).astype(BF)

    return pl.pallas_call(
        kern, name="swiglu_fwd", grid=(S // tm, nc),
        in_specs=[pl.BlockSpec((tm, tc), lambda i, j: (i, j)),
                  pl.BlockSpec((tm, tc), lambda i, j: (i, j + nc))],
        out_specs=pl.BlockSpec((tm, tc), lambda i, j: (i, j)),
        out_shape=jax.ShapeDtypeStruct((S, D_FF), BF),
        compiler_params=_cparams("parallel", "parallel"),
    )(gu, gu)


def _swiglu_bwd(d_act, gu):
    S = gu.shape[0]
    tm = _tile(S, 128, 8)

    def kern(da_ref, g_ref, u_ref, o_ref):
        g = g_ref[...].astype(F32)
        u = u_ref[...].astype(F32)
        da = da_ref[...].astype(F32)
        sg = jax.nn.sigmoid(g)
        o_ref[:, :D_FF] = (da * u * (sg * (1.0 + g * (1.0 - sg)))).astype(BF)
        o_ref[:, D_FF:] = (da * (g * sg)).astype(BF)

    return pl.pallas_call(
        kern, name="swiglu_bwd", grid=(S // tm,),
        in_specs=[_row_spec(tm, D_FF), _row_spec(tm, D_FF, 0), _row_spec(tm, D_FF, 1)],
        out_specs=_row_spec(tm, 2 * D_FF),
        out_shape=jax.ShapeDtypeStruct((S, 2 * D_FF), BF),
        compiler_params=_cparams("parallel"),
    )(d_act, gu, gu)


def _split3(x):
    hi = x.astype(BF)
    r1 = x - hi.astype(F32)
    mid = r1.astype(BF)
    lo = (r1 - mid.astype(F32)).astype(BF)
    return hi, mid, lo


def _tri_dot(tri, x):
    return sum(jnp.dot(tri, part, preferred_element_type=F32) for part in _split3(x))


def _log_sigmoid(z):
    return jnp.minimum(z, 0.0) - jnp.log(1.0 + jnp.exp(-jnp.abs(z)))


def _fox_gate_fwd(pa, b_f_pad):
    S = pa.shape[0]
    T = _tile(S, 512, 8)
    f_col = OFF_F // LANES

    def kern(z_ref, b_ref, cum_ref, carry_ref):
        @pl.when(pl.program_id(0) == 0)
        def _():
            carry_ref[...] = jnp.zeros_like(carry_ref)

        log_f = _log_sigmoid(z_ref[...] + b_ref[...])
        row = lax.broadcasted_iota(jnp.int32, (T, T), 0)
        col = lax.broadcasted_iota(jnp.int32, (T, T), 1)
        tri = (col <= row).astype(BF)
        cum = _tri_dot(tri, log_f) + carry_ref[...]
        cum_ref[...] = cum
        carry_ref[...] = cum[T - 1:T, :]

    return pl.pallas_call(
        kern, name="fox_gate_fwd", grid=(S // T,),
        in_specs=[_row_spec(T, LANES, f_col), _vec_spec(LANES)],
        out_specs=_row_spec(T, LANES),
        out_shape=jax.ShapeDtypeStruct((S, LANES), F32),
        scratch_shapes=[pltpu.VMEM((1, LANES), F32)],
        compiler_params=_cparams("arbitrary"),
    )(pa, b_f_pad)


def _fox_gate_bwd(rowsum_ds, colsum_ds, pa, b_f_pad):
    S = pa.shape[0]
    T = _tile(S, 512, 8)
    nb = S // T
    f_col = OFF_F // LANES

    def kern(dr_ref, dc_ref, z_ref, b_ref, df_ref, dbf_ref, carry_ref):
        @pl.when(pl.program_id(0) == 0)
        def _():
            carry_ref[...] = jnp.zeros_like(carry_ref)
            dbf_ref[...] = jnp.zeros_like(dbf_ref)

        row = lax.broadcasted_iota(jnp.int32, (T, T), 0)
        col = lax.broadcasted_iota(jnp.int32, (T, T), 1)
        tri = (col >= row).astype(BF)
        rev = _tri_dot(tri, dr_ref[...] - dc_ref[...]) + carry_ref[...]
        carry_ref[...] = rev[0:1, :]
        z = z_ref[...] + b_ref[...]
        lane = lax.broadcasted_iota(jnp.int32, (T, LANES), 1)
        d_z = jnp.where(lane < B_HEADS, rev * jax.nn.sigmoid(-z), 0.0)
        df_ref[...] = d_z.astype(BF)
        dbf_ref[0:1, :] += _colsum(d_z)

    return pl.pallas_call(
        kern, name="fox_gate_bwd", grid=(nb,),
        in_specs=[pl.BlockSpec((T, LANES), lambda i: (nb - 1 - i, 0)),
                  pl.BlockSpec((T, LANES), lambda i: (nb - 1 - i, 0)),
                  pl.BlockSpec((T, LANES), lambda i: (nb - 1 - i, f_col)),
                  _vec_spec(LANES)],
        out_specs=[pl.BlockSpec((T, LANES), lambda i: (nb - 1 - i, 0)),
                   pl.BlockSpec((8, LANES), lambda i: (0, 0))],
        out_shape=[jax.ShapeDtypeStruct((S, LANES), BF), jax.ShapeDtypeStruct((8, LANES), F32)],
        scratch_shapes=[pltpu.VMEM((1, LANES), F32)],
        compiler_params=_cparams("arbitrary"),
    )(rowsum_ds, colsum_ds, pa, b_f_pad)


NEG_INF = float("-inf")
QK_SCALE = 1.0 / math.sqrt(HEAD_DIM)


def _half_mask(shape, half):
    lane = lax.broadcasted_iota(jnp.int32, shape, 1)
    return (lane < HEAD_DIM) if half == 0 else (lane >= HEAD_DIM)


def _bias_lanes(shape, half, q_side_terms, k_side_terms):
    lane = lax.broadcasted_iota(jnp.int32, shape, 1)
    base = HEAD_DIM * (1 - half)
    n_q = len(q_side_terms) if q_side_terms is not None else 3
    n_k = len(k_side_terms) if k_side_terms is not None else 3
    out = jnp.zeros(shape, F32)
    for t in range(n_q):
        out = jnp.where(lane == base + t, q_side_terms[t].astype(F32) if q_side_terms is not None else 1.0, out)
    for t in range(n_k):
        out = jnp.where(lane == base + n_q + t,
                        k_side_terms[t].astype(F32) if k_side_terms is not None else 1.0, out)
    return out


def _head_column(block, head):
    lane = lax.broadcasted_iota(jnp.int32, block.shape, 1)
    return jnp.sum(jnp.where(lane == head, block, 0.0), axis=1, keepdims=True)


def _fox_prep_fwd(p_b, cum, T):
    S = p_b.shape[0]

    def kern(q_ref, k_ref, c_ref, qa_ref, ka_ref):
        p_id = pl.program_id(0)
        q, k, cum_blk = q_ref[...], k_ref[...], c_ref[...]
        for half in (0, 1):
            hm = _half_mask((T, LANES), half)
            c3 = _split3(_head_column(cum_blk, 2 * p_id + half))
            qa_ref[half] = jnp.where(hm, q.astype(F32) * QK_SCALE, _bias_lanes((T, LANES), half, c3, None)).astype(BF)
            ka_ref[half] = jnp.where(hm, k.astype(F32),
                                     _bias_lanes((T, LANES), half, None, [-t.astype(F32) for t in c3])).astype(BF)

    out_spec = pl.BlockSpec((None, 2, T, LANES), lambda p, i: (p, 0, i, 0))
    shape = jax.ShapeDtypeStruct((B_HEADS // 2, 2, S, LANES), BF)
    return pl.pallas_call(
        kern, name="fox_prep_fwd", grid=(B_HEADS // 2, S // T),
        in_specs=[pl.BlockSpec((T, LANES), lambda p, i: (i, OFF_QB // LANES + p)),
                  pl.BlockSpec((T, LANES), lambda p, i: (i, OFF_KB // LANES + p)),
                  pl.BlockSpec((T, LANES), lambda p, i: (i, 0))],
        out_specs=[out_spec, out_spec], out_shape=[shape, shape],
        compiler_params=_cparams("parallel", "parallel"),
    )(p_b, p_b, cum)


def _fox_fwd(q_aug, k_aug, p_b, T, comm=None):
    S = p_b.shape[0]
    nq = S // T
    n_pairs = B_HEADS // 2
    grid = (n_pairs, nq)

    def kern(*refs):
        (q_ref, k_ref, v_ref, o_ref, lse_ref), comm_refs = _own_refs(refs, comm, 3, 2, 0)
        _comm_edge(comm, comm_refs, grid, first=True)
        i = pl.program_id(1)
        rowcol = lax.broadcasted_iota(jnp.int32, (T, T), 0) - lax.broadcasted_iota(jnp.int32, (T, T), 1)
        qs = (q_ref[0], q_ref[1])

        def step(j, carry, masked):
            rows = pl.ds(pl.multiple_of(j * T, T), T)
            vj = v_ref[rows, :]
            new = []
            for half in (0, 1):
                m, l, acc = carry[half]
                s = lax.dot_general(qs[half], k_ref[half, rows, :], (((1,), (1,)), ((), ())),
                                    preferred_element_type=F32)
                if masked:
                    s = jnp.where(rowcol >= 0, s, NEG_INF)
                m_new = jnp.maximum(m, jnp.max(s, axis=1, keepdims=True))
                alpha = jnp.exp(m - m_new)
                p = jnp.exp(s - m_new)
                l_new = alpha * l + jnp.sum(p, axis=1, keepdims=True)
                acc_new = alpha * acc + jnp.dot(p.astype(BF), vj, preferred_element_type=F32)
                new.append((m_new, l_new, acc_new))
            return tuple(new)

        one = (jnp.full((T, 1), NEG_INF, F32), jnp.zeros((T, 1), F32), jnp.zeros((T, LANES), F32))
        carry = lax.fori_loop(0, i, functools.partial(step, masked=False), (one, one))
        (m0, l0, acc0), (m1, l1, acc1) = step(i, carry, True)
        hm0 = _half_mask((T, LANES), 0)
        o_ref[...] = jnp.where(hm0, acc0 / l0, acc1 / l1)
        lse_ref[...] = jnp.where(hm0, m0 + jnp.log(l0), m1 + jnp.log(l1))
        _comm_edge(comm, comm_refs, grid, first=False)

    out_spec = pl.BlockSpec((T, LANES), lambda p, i: (i, p))
    res = pl.pallas_call(
        kern, name="fox_fwd", grid=grid,
        in_specs=[pl.BlockSpec((None, 2, T, LANES), lambda p, i: (p, 0, i, 0)),
                  pl.BlockSpec((None, 2, S, LANES), lambda p, i: (p, 0, 0, 0)),
                  pl.BlockSpec((S, LANES), lambda p, i: (0, OFF_VB // LANES + p))] + _comm_specs(comm, "in"),
        out_specs=[out_spec, out_spec] + _comm_specs(comm, "out"),
        out_shape=[jax.ShapeDtypeStruct((S, n_pairs * LANES), F32)] * 2 + (comm.out_shapes if comm else []),
        scratch_shapes=comm.sem_shapes if comm else [],
        compiler_params=_cparams("arbitrary", "arbitrary"),
    )(q_aug, k_aug, p_b, *(comm.ins if comm else []))
    return res[0], res[1], res[2:]


def _fox_prep_bwd(q_aug, o, do, lse, T):
    S = o.shape[0]

    def kern(qa_ref, o_ref, do_ref, lse_ref, qb_ref, dob_ref):
        o_blk, do_blk, lse_blk = o_ref[...], do_ref[...], lse_ref[...]
        lane = lax.broadcasted_iota(jnp.int32, (T, LANES), 1)
        for half in (0, 1):
            hm = _half_mask((T, LANES), half)
            base = HEAD_DIM * (1 - half)
            qa = qa_ref[half].astype(F32)
            cq = jnp.sum(jnp.where((lane >= base) & (lane < base + 3), qa, 0.0), axis=1, keepdims=True)
            b3 = _split3(cq - lse_blk[:, HEAD_DIM * half:HEAD_DIM * half + 1])
            qb_ref[half] = jnp.where(hm, qa, _bias_lanes((T, LANES), half, b3, None)).astype(BF)
            do_f = jnp.where(hm, do_blk, 0.0)
            d3 = _split3(-jnp.sum(do_f * o_blk, axis=1, keepdims=True))
            dob_ref[half] = jnp.where(hm, do_f, _bias_lanes((T, LANES), half, d3, [])).astype(BF)

    aug = pl.BlockSpec((None, 2, T, LANES), lambda p, i: (p, 0, i, 0))
    tile = pl.BlockSpec((T, LANES), lambda p, i: (i, p))
    shape = jax.ShapeDtypeStruct((B_HEADS // 2, 2, S, LANES), BF)
    return pl.pallas_call(
        kern, name="fox_prep_bwd", grid=(B_HEADS // 2, S // T),
        in_specs=[aug, tile, tile, tile],
        out_specs=[aug, aug], out_shape=[shape, shape],
        compiler_params=_cparams("parallel", "parallel"),
    )(q_aug, o, do, lse)


def _fox_bwd(qb_aug, k_aug, dob_aug, p_b, T, comm=None):
    n_pairs, _, S, _ = qb_aug.shape
    nq = S // T
    grid = (n_pairs,)

    def kern(*refs):
        own, comm_refs = _own_refs(refs, comm, 4, 5, 0)
        q_ref, k_ref, do_ref, v_ref, dq_ref, dk_ref, dv_ref, dck_ref, dcq_ref = own
        _comm_edge(comm, comm_refs, grid, first=True)
        p_id = pl.program_id(0)
        rowcol = lax.broadcasted_iota(jnp.int32, (T, T), 0) - lax.broadcasted_iota(jnp.int32, (T, T), 1)
        lane = lax.broadcasted_iota(jnp.int32, (T, LANES), 1)
        dk_ref[...] = jnp.zeros_like(dk_ref)
        dv_ref[...] = jnp.zeros_like(dv_ref)
        dck_ref[...] = jnp.zeros_like(dck_ref)

        @pl.when(p_id == 0)
        def _():
            dcq_ref[...] = jnp.zeros_like(dcq_ref)

        hms = (_half_mask((T, LANES), 0), _half_mask((T, LANES), 1))
        v_ones = [_bias_lanes((T, LANES), h, None, []).astype(BF) for h in (0, 1)]

        def outer(i, carry):
            qrows = pl.ds(pl.multiple_of(i * T, T), T)
            qa = (q_ref[0, qrows, :], q_ref[1, qrows, :])
            doa = (do_ref[0, qrows, :], do_ref[1, qrows, :])
            q_own = [jnp.where(hms[h], qa[h], 0) for h in (0, 1)]
            do_own = [jnp.where(hms[h], doa[h], 0) for h in (0, 1)]

            def inner(j, carry_in, masked):
                krows = pl.ds(pl.multiple_of(j * T, T), T)
                vj = v_ref[krows, :]
                dv_add, dk_add, new = 0.0, 0.0, []
                for half in (0, 1):
                    dq, rs = carry_in[half]
                    ka = k_ref[half, krows, :]
                    s = lax.dot_general(qa[half], ka, (((1,), (1,)), ((), ())), preferred_element_type=F32)
                    if masked:
                        s = jnp.where(rowcol >= 0, s, NEG_INF)
                    p = jnp.exp(s)
                    ds = p * lax.dot_general(doa[half], jnp.where(hms[half], vj, v_ones[half]),
                                             (((1,), (1,)), ((), ())), preferred_element_type=F32)
                    ds_b = ds.astype(BF)
                    dv_add = dv_add + lax.dot_general(p.astype(BF), do_own[half], (((0,), (0,)), ((), ())),
                                                      preferred_element_type=F32)
                    dk_add = dk_add + lax.dot_general(ds_b, q_own[half], (((0,), (0,)), ((), ())),
                                                      preferred_element_type=F32)
                    dck_ref[half:half + 1, krows] += jnp.sum(ds, axis=0, keepdims=True)
                    new.append((dq + jnp.dot(ds_b, jnp.where(hms[half], ka, 0), preferred_element_type=F32),
                                rs + jnp.sum(ds, axis=1, keepdims=True)))
                dv_ref[krows, :] += dv_add
                dk_ref[krows, :] += dk_add
                return tuple(new)

            one = (jnp.zeros((T, LANES), F32), jnp.zeros((T, 1), F32))
            carry_in = lax.fori_loop(0, i, functools.partial(inner, masked=False), (one, one))
            (dq0, rs0), (dq1, rs1) = inner(i, carry_in, True)
            dq_ref[qrows, :] = (dq0 + dq1) * QK_SCALE
            dcq_ref[qrows, :] = jnp.where(lane == 2 * p_id, rs0, jnp.where(lane == 2 * p_id + 1, rs1,
                                                                             dcq_ref[qrows, :]))
            return carry

        lax.fori_loop(0, nq, outer, 0)
        _comm_edge(comm, comm_refs, grid, first=False)

    aug = pl.BlockSpec((None, 2, S, LANES), lambda p: (p, 0, 0, 0))
    pair = pl.BlockSpec((S, LANES), lambda p: (0, p))
    wide = jax.ShapeDtypeStruct((S, n_pairs * LANES), F32)
    res = pl.pallas_call(
        kern, name="fox_bwd", grid=grid,
        in_specs=[aug, aug, aug, pl.BlockSpec((S, LANES), lambda p: (0, OFF_VB // LANES + p))]
        + _comm_specs(comm, "in"),
        out_specs=[pair, pair, pair, pl.BlockSpec((None, 2, S), lambda p: (p, 0, 0)),
                   pl.BlockSpec((S, LANES), lambda p: (0, 0))] + _comm_specs(comm, "out"),
        out_shape=[wide, wide, wide, jax.ShapeDtypeStruct((n_pairs, 2, S), F32),
                   jax.ShapeDtypeStruct((S, LANES), F32)] + (comm.out_shapes if comm else []),
        scratch_shapes=comm.sem_shapes if comm else [],
        compiler_params=_cparams("arbitrary"),
    )(qb_aug, k_aug, dob_aug, p_b, *(comm.ins if comm else []))
    return (*res[:5], res[5:])


SWA_TQ = 256
SWA_SUB = 4


def _swa_window(i, tq):
    start = pl.multiple_of(jnp.maximum(i * tq - WINDOW, 0), LANES)
    return start, i * tq - start


def _swa_valid(offset, tq):
    rel = offset + lax.broadcasted_iota(jnp.int32, (tq, tq + WINDOW), 0) \
        - lax.broadcasted_iota(jnp.int32, (tq, tq + WINDOW), 1)
    return (rel >= 0) & (rel < WINDOW)


def _swa_fwd(qk, v_arr, v_col, sinks):
    S = qk.shape[0]
    tq = min(SWA_TQ, S - WINDOW)
    sub = min(SWA_SUB, S // tq)
    win = tq + WINDOW

    def kern(q_ref, k_ref, v_ref, sink_ref, o_ref, lse_ref):
        p_id, i = pl.program_id(0), pl.program_id(1)
        hm0 = _half_mask((tq, LANES), 0)
        for t in range(sub):
            rows = slice(t * tq, (t + 1) * tq)
            start, offset = _swa_window(i * sub + t, tq)
            kw = k_ref[pl.ds(start, win), :]
            vw = v_ref[pl.ds(start, win), :].astype(BF)
            valid = _swa_valid(offset, tq)
            q = q_ref[rows, :]
            outs, lses = [], []
            for half in (0, 1):
                hm = _half_mask((tq, LANES), half)
                qh = (jnp.where(hm, q, 0).astype(F32) * QK_SCALE).astype(BF)
                s = lax.dot_general(qh, kw, (((1,), (1,)), ((), ())), preferred_element_type=F32)
                s = jnp.where(valid, s, NEG_INF)
                sink = sink_ref[2 * p_id + half]
                m = jnp.maximum(jnp.max(s, axis=1, keepdims=True), sink)
                p = jnp.exp(s - m)
                denom = jnp.sum(p, axis=1, keepdims=True) + jnp.exp(sink - m)
                outs.append(jnp.dot(p.astype(BF), vw, preferred_element_type=F32) / denom)
                lses.append(m + jnp.log(denom))
            o_ref[rows, :] = jnp.where(hm0, outs[0], outs[1])
            lse_ref[rows, :] = jnp.where(hm0, lses[0], lses[1])

    tile = pl.BlockSpec((sub * tq, LANES), lambda p, i: (i, p))
    return pl.pallas_call(
        kern, name="swa_fwd", grid=(A_Q_HEADS // 2, S // (sub * tq)),
        in_specs=[tile, pl.BlockSpec((S, LANES), lambda p, i: (0, A_Q_HEADS // 2)),
                  pl.BlockSpec((S, LANES), lambda p, i: (0, v_col)),
                  pl.BlockSpec(memory_space=pltpu.SMEM)],
        out_specs=[tile, tile],
        out_shape=[jax.ShapeDtypeStruct((S, A_Q_HEADS * HEAD_DIM), F32)] * 2,
        compiler_params=_cparams("parallel", "arbitrary"),
    )(qk, qk, v_arr, sinks)


def _swa_bwd(qk, v_arr, v_col, o_arr, do_arr, lse_arr, sinks, comm=None):
    S = qk.shape[0]
    tq = min(SWA_TQ, S - WINDOW)
    sub = min(SWA_SUB, S // tq)
    win = tq + WINDOW
    n_pairs = A_Q_HEADS // 2
    grid = (n_pairs, S // (sub * tq))

    def kern(*refs):
        own, comm_refs = _own_refs(refs, comm, 7, 4, 0)
        q_ref, k_ref, v_ref, o_ref, do_ref, lse_ref, sink_ref, dq_ref, dk_ref, dv_ref, dsink_ref = own
        _comm_edge(comm, comm_refs, grid, first=True)
        p_id, i = pl.program_id(0), pl.program_id(1)

        @pl.when((p_id == 0) & (i == 0))
        def _():
            dk_ref[...] = jnp.zeros_like(dk_ref)
            dv_ref[...] = jnp.zeros_like(dv_ref)

        @pl.when(i == 0)
        def _():
            dsink_ref[...] = jnp.zeros_like(dsink_ref)

        for t in range(sub):
            rows = slice(t * tq, (t + 1) * tq)
            start, offset = _swa_window(i * sub + t, tq)
            wrows = pl.ds(start, win)
            kw = k_ref[wrows, :]
            vw = v_ref[wrows, :].astype(BF)
            valid = _swa_valid(offset, tq)
            q, do, o, lse2 = q_ref[rows, :], do_ref[rows, :], o_ref[rows, :], lse_ref[rows, :]
            dq = jnp.zeros((tq, LANES), F32)
            dk = jnp.zeros((win, LANES), F32)
            dv = jnp.zeros((win, LANES), F32)
            for half in (0, 1):
                hm = _half_mask((tq, LANES), half)
                lane0 = half * HEAD_DIM
                qh = (jnp.where(hm, q, 0).astype(F32) * QK_SCALE).astype(BF)
                do_f = jnp.where(hm, do, 0.0)
                doh = do_f.astype(BF)
                delta = jnp.sum(do_f * o, axis=1, keepdims=True)
                lse = lse2[:, lane0:lane0 + 1]
                s = lax.dot_general(qh, kw, (((1,), (1,)), ((), ())), preferred_element_type=F32)
                p = jnp.exp(jnp.where(valid, s, NEG_INF) - lse)
                dp = lax.dot_general(doh, vw, (((1,), (1,)), ((), ())), preferred_element_type=F32)
                ds_b = (p * (dp - delta)).astype(BF)
                dv = dv + lax.dot_general(p.astype(BF), doh, (((0,), (0,)), ((), ())),
                                          preferred_element_type=F32)
                dk = dk + lax.dot_general(ds_b, qh, (((0,), (0,)), ((), ())), preferred_element_type=F32)
                kh = jnp.where(_half_mask((win, LANES), half), kw, 0)
                dq = dq + jnp.dot(ds_b, kh, preferred_element_type=F32)
                p_sink = jnp.exp(sink_ref[2 * p_id + half] - lse)
                dsink_ref[0, half:half + 1, :] += jnp.broadcast_to(
                    -jnp.sum(p_sink * delta, axis=0, keepdims=True), (1, LANES))
            dq_ref[rows, :] = dq * QK_SCALE
            dk_ref[wrows, :] += dk
            dv_ref[wrows, :] += dv
        _comm_edge(comm, comm_refs, grid, first=False)

    tile = pl.BlockSpec((sub * tq, LANES), lambda p, i: (i, p))
    whole = lambda col: pl.BlockSpec((S, LANES), lambda p, i: (0, col))
    res = pl.pallas_call(
        kern, name="swa_bwd", grid=grid,
        in_specs=[tile, whole(n_pairs), whole(v_col), tile, tile, tile,
                  pl.BlockSpec(memory_space=pltpu.SMEM)] + _comm_specs(comm, "in"),
        out_specs=[tile, whole(0), whole(0),
                   pl.BlockSpec((1, 8, LANES), lambda p, i: (p, 0, 0))] + _comm_specs(comm, "out"),
        out_shape=[jax.ShapeDtypeStruct((S, A_Q_HEADS * HEAD_DIM), F32),
                   jax.ShapeDtypeStruct((S, LANES), F32), jax.ShapeDtypeStruct((S, LANES), F32),
                   jax.ShapeDtypeStruct((n_pairs, 8, LANES), F32)] + (comm.out_shapes if comm else []),
        scratch_shapes=comm.sem_shapes if comm else [],
        compiler_params=_cparams("arbitrary", "arbitrary"),
    )(qk, qk, v_arr, o_arr, do_arr, lse_arr, sinks, *(comm.ins if comm else []))
    return (*res[:4], res[4:])


ADAMW_BLOCK = 256 * 1024


def _adamw(w, g, m, v, name):
    R, C = w.shape
    tr, tc = _tile(R, max(8, ADAMW_BLOCK // C), 8), C

    def kern(w_ref, g_ref, m_ref, v_ref, d_ref, mo_ref, vo_ref):
        g_ = g_ref[...]
        m_new = ADAM_B1 * m_ref[...] + (1.0 - ADAM_B1) * g_
        v_new = ADAM_B2 * v_ref[...] + (1.0 - ADAM_B2) * (g_ * g_)
        m_hat = m_new / (1.0 - ADAM_B1 ** ADAM_STEP)
        v_hat = v_new / (1.0 - ADAM_B2 ** ADAM_STEP)
        d_ref[...] = -ADAM_LR * (m_hat / (jnp.sqrt(v_hat) + ADAM_EPS) + ADAM_WD * w_ref[...])
        mo_ref[...] = m_new
        vo_ref[...] = v_new

    spec = pl.BlockSpec((tr, tc), lambda i, j: (i, j))
    shape = jax.ShapeDtypeStruct((R, C), F32)
    return pl.pallas_call(
        kern, name=name, grid=(R // tr, C // tc),
        in_specs=[spec] * 4, out_specs=[spec] * 3, out_shape=[shape] * 3,
        compiler_params=_cparams("parallel", "parallel"),
    )(w, g, m, v)


def _index_operand(i):
    return jnp.reshape(i, (1,)).astype(jnp.int32)


def _add_pair(whole, got, ci, name):
    P, R, C = whole.shape
    half = R // 2
    tr = _tile(half, 256, 16)
    nb = half // tr

    def kern(ci_ref, a_ref, b_ref, o_ref, ob_ref):
        s = a_ref[...] + b_ref[...].astype(F32)
        o_ref[...] = s
        ob_ref[...] = s.astype(BF)

    spec = pl.BlockSpec((None, tr, C), lambda p, i, ci_ref: (p, i, 0))
    return pl.pallas_call(
        kern, name=name,
        grid_spec=pltpu.PrefetchScalarGridSpec(
            num_scalar_prefetch=1, grid=(P, nb),
            in_specs=[pl.BlockSpec((None, tr, C), lambda p, i, ci_ref: (p, ci_ref[0] * nb + i, 0)), spec],
            out_specs=[spec, spec]),
        out_shape=[jax.ShapeDtypeStruct((P, half, C), F32), jax.ShapeDtypeStruct((P, half, C), BF)],
        compiler_params=_cparams("parallel", "parallel"),
    )(_index_operand(ci), whole, got)


def _add_three(parts, recv, chip, name):
    _, R, C = parts.shape
    tr = _tile(R, 256, 16)

    def kern(chip_ref, o_ref, r0_ref, r1_ref, r2_ref, out_ref):
        s = ((o_ref[...] + r0_ref[...].astype(F32)) + r1_ref[...].astype(F32)) + r2_ref[...].astype(F32)
        out_ref[0] = s
        out_ref[1] = s

    slab = lambda k: pl.BlockSpec((None, tr, C), lambda i, chip_ref: (k, i, 0))
    return pl.pallas_call(
        kern, name=name,
        grid_spec=pltpu.PrefetchScalarGridSpec(
            num_scalar_prefetch=1, grid=(R // tr,),
            in_specs=[pl.BlockSpec((None, tr, C), lambda i, chip_ref: (chip_ref[0], i, 0)),
                      slab(0), slab(1), slab(2)],
            out_specs=pl.BlockSpec((2, tr, C), lambda i, chip_ref: (0, i, 0))),
        out_shape=jax.ShapeDtypeStruct((2, R, C), F32),
        compiler_params=_cparams("parallel"),
    )(_index_operand(chip), parts, recv, recv, recv)


SM_ADA, SM_G, SM_LOSS, SM_BF, SM_SINK, SM_LEN = 0, 6144, 10240, 11264, 11272, 12288


def _small_finalize(gathered):
    def kern(g_ref, tot_ref, loss_ref):
        tot = g_ref[0:1, :]
        for b in range(1, N_DEV):
            tot = tot + g_ref[b:b + 1, :]
        tot_ref[...] = tot
        sq = jnp.sum(tot[:, SM_LOSS:SM_LOSS + D_MODEL], axis=1, keepdims=True)
        loss_ref[...] = jnp.broadcast_to(sq * (0.5 / D_MODEL), (1, LANES))

    full = lambda shape: pl.BlockSpec(shape, lambda i: (0, 0))
    return pl.pallas_call(
        kern, name="small_finalize", grid=(1,),
        in_specs=[full((N_DEV, SM_LEN))],
        out_specs=[full((1, SM_LEN)), full((1, LANES))],
        out_shape=[jax.ShapeDtypeStruct((1, SM_LEN), F32), jax.ShapeDtypeStruct((1, LANES), F32)],
        compiler_params=_cparams("arbitrary"),
    )(gathered)


def _ada_dw(c_t, d_ada):
    N = d_ada.shape[1]
    tn = _tile(N, 512)

    def kern(c_ref, d_ref, o_ref):
        acc = c_ref[:, 0:1] * d_ref[0:1, :]
        for b in range(1, N_DEV):
            acc = acc + c_ref[:, b:b + 1] * d_ref[b:b + 1, :]
        o_ref[...] = acc

    return pl.pallas_call(
        kern, name="ada_dw", grid=(N // tn,),
        in_specs=[pl.BlockSpec((D_MODEL, N_DEV), lambda j: (0, 0)), pl.BlockSpec((N_DEV, tn), lambda j: (0, j))],
        out_specs=pl.BlockSpec((D_MODEL, tn), lambda j: (0, j)),
        out_shape=jax.ShapeDtypeStruct((D_MODEL, N), F32),
        compiler_params=_cparams("parallel"),
    )(c_t, d_ada)


def _here():
    return lax.axis_index("x"), lax.axis_index("y"), lax.axis_index("c")


def _other_chips(x, y):
    return [(1 - x, y), (x, 1 - y), (1 - x, 1 - y)]


_ANY = pl.BlockSpec(memory_space=pl.ANY)


class _Comm:
    def __init__(self, ins, out_shapes, sem_shapes, start, finish):
        self.ins, self.out_shapes, self.sem_shapes = list(ins), list(out_shapes), list(sem_shapes)
        self.start, self.finish = start, finish

    def split(self, refs, n_in, n_out, n_scratch):
        a = n_in + len(self.ins)
        b = a + n_out + len(self.out_shapes)
        own = list(refs[:n_in]) + list(refs[a:a + n_out]) + list(refs[b:b + n_scratch])
        mine = (refs[n_in:a], refs[a + n_out:b], refs[b + n_scratch:])
        return own, mine


def _run_comm(comm, name):
    n_in, n_out = len(comm.ins), len(comm.out_shapes)

    def body(*refs):
        parts = (refs[:n_in], refs[n_in:n_in + n_out], refs[n_in + n_out:])
        comm.start(*parts)
        comm.finish(*parts)

    return pl.pallas_call(
        body, name=name,
        in_specs=[_ANY] * n_in, out_specs=[_ANY] * n_out,
        out_shape=comm.out_shapes, scratch_shapes=comm.sem_shapes,
    )(*comm.ins)


def _gather_comm(blocks):
    L = len(blocks)

    def parts(ins, outs, sems):
        send_sems, recv_sems, local_sems = sems
        x, y, c = _here()
        me, sibling = (x, y, c), (x, y, 1 - c)
        chips = _other_chips(x, y)

        def slot(px, py, pc):
            return 4 * px + 2 * py + pc

        def copy(l, k, block, to, src=None):
            dst = outs[l].at[slot(*block)]
            return pltpu.make_async_remote_copy(
                src_ref=dst if src is None else src, dst_ref=dst,
                send_sem=send_sems.at[l, k], recv_sem=recv_sems.at[l, k],
                device_id=to, device_id_type=MESH)

        mine = [pltpu.make_async_copy(ins[l], outs[l].at[slot(*me)], local_sems.at[l]) for l in range(L)]
        first = []
        for l in range(L):
            first.append(copy(l, 0, me, sibling, src=ins[l]))
            for j, chip in enumerate(chips):
                first.append(copy(l, 1 + j, me, (*chip, c), src=ins[l]))
        return c, me, sibling, chips, copy, mine, first

    def start(ins, outs, sems):
        *_, mine, first = parts(ins, outs, sems)
        for cp in mine + first:
            cp.start()

    def finish(ins, outs, sems):
        c, me, sibling, chips, copy, mine, first = parts(ins, outs, sems)
        passed = []
        for j, chip in enumerate(chips):
            for l in range(L):
                copy(l, 1 + j, (*chip, c), me).wait_recv()
                fwd = copy(l, 4 + j, (*chip, c), sibling)
                fwd.start()
                passed.append(fwd)
        for l in range(L):
            copy(l, 0, sibling, me).wait_recv()
        for j, chip in enumerate(chips):
            for l in range(L):
                copy(l, 4 + j, (*chip, 1 - c), me).wait_recv()
        for cp in first + passed:
            cp.wait_send()
        for cp in mine:
            cp.wait()

    return _Comm(blocks, [jax.ShapeDtypeStruct((N_DEV,) + b.shape, b.dtype) for b in blocks],
                 [pltpu.SemaphoreType.DMA((L, 7)), pltpu.SemaphoreType.DMA((L, 7)), pltpu.SemaphoreType.DMA((L,))],
                 start, finish)


def _allgather8(blocks, name):
    return _run_comm(_gather_comm(blocks), name)


def _swap_comm(arrs):
    L = len(arrs)

    def copies(ins, outs, sems):
        send_sems, recv_sems = sems
        x, y, c = _here()
        cps = []
        for l in range(L):
            half = arrs[l].shape[1] // 2
            rows = pl.ds(pl.multiple_of((1 - c) * half, 16), half)
            cps.append(pltpu.make_async_remote_copy(
                src_ref=ins[l].at[:, rows, :], dst_ref=outs[l], send_sem=send_sems.at[l],
                recv_sem=recv_sems.at[l], device_id=(x, y, 1 - c), device_id_type=MESH))
        return cps

    def start(ins, outs, sems):
        for cp in copies(ins, outs, sems):
            cp.start()

    def finish(ins, outs, sems):
        for cp in copies(ins, outs, sems):
            cp.wait()

    return _Comm(arrs, [jax.ShapeDtypeStruct((a.shape[0], a.shape[1] // 2, a.shape[2]), a.dtype) for a in arrs],
                 [pltpu.SemaphoreType.DMA((L,)), pltpu.SemaphoreType.DMA((L,))], start, finish)


def _sibling_join(bufs, name):
    L = len(bufs)

    def body(*refs):
        outs = refs[L:2 * L]
        send_sems, recv_sems = refs[2 * L:]
        x, y, c = _here()
        for l in range(L):
            pltpu.make_async_remote_copy(src_ref=outs[l].at[c], dst_ref=outs[l].at[c], send_sem=send_sems.at[l],
                                         recv_sem=recv_sems.at[l], device_id=(x, y, 1 - c),
                                         device_id_type=MESH).start()
        for l in range(L):
            pltpu.make_async_remote_copy(src_ref=outs[l].at[c], dst_ref=outs[l].at[1 - c],
                                         send_sem=send_sems.at[l], recv_sem=recv_sems.at[l],
                                         device_id=(x, y, 1 - c), device_id_type=MESH).wait()

    return pl.pallas_call(
        body, name=name,
        in_specs=[_ANY] * L, out_specs=[_ANY] * L,
        out_shape=[jax.ShapeDtypeStruct(a.shape, a.dtype) for a in bufs],
        input_output_aliases={l: l for l in range(L)},
        scratch_shapes=[pltpu.SemaphoreType.DMA((L,)), pltpu.SemaphoreType.DMA((L,))],
    )(*bufs)


def _scatter_comm(arrs):
    L = len(arrs)

    def copies(ins, outs, sems):
        send_sems, recv_sems = sems
        x, y, c = _here()
        return [pltpu.make_async_remote_copy(
            src_ref=ins[l].at[2 * tx + ty], dst_ref=outs[l].at[j],
            send_sem=send_sems.at[l, j], recv_sem=recv_sems.at[l, j],
            device_id=(tx, ty, c), device_id_type=MESH)
            for l in range(L) for j, (tx, ty) in enumerate(_other_chips(x, y))]

    def start(ins, outs, sems):
        for cp in copies(ins, outs, sems):
            cp.start()

    def finish(ins, outs, sems):
        for cp in copies(ins, outs, sems):
            cp.wait()

    return _Comm(arrs, [jax.ShapeDtypeStruct((3,) + a.shape[1:], a.dtype) for a in arrs],
                 [pltpu.SemaphoreType.DMA((L, 3)), pltpu.SemaphoreType.DMA((L, 3))], start, finish)


_A_ORDER = np.array(A_HEAD_ORDER)
_A_INVERSE = np.argsort(_A_ORDER)


def _permute_in_weights(w_in):
    qa = w_in[:, 0:512].reshape(D_MODEL, A_Q_HEADS, HEAD_DIM)[:, _A_ORDER, :].reshape(D_MODEL, 512)
    f_pad = jnp.pad(w_in[:, 2304:2312], ((0, 0), (0, LANES - B_HEADS)))
    w_a = jnp.concatenate([qa, w_in[:, 512:640], f_pad], axis=1)
    return w_a, w_in[:, 640:2304], w_in[:, 2312:4360]


def _slab_segments():
    segs = [(h * HEAD_DIM, int(_A_INVERSE[h]) * HEAD_DIM, HEAD_DIM) for h in range(A_Q_HEADS)]
    segs += [(512, OFF_KA, 128), (640, W_A + OFF_VA, 128), (768, W_A + OFF_QB, 1536),
             (2304, OFF_F, B_HEADS), (2312, W_A + W_B, W_G)]
    return segs


def _shard_slabs(dw_perm):
    R = dw_perm.shape[0]
    tr = _tile(R, 128, 8)
    plan = []
    for k in range(N_CHIP):
        for b in range(W_SHARD_PAD // LANES):
            lo, hi = k * W_SHARD + b * LANES, min(k * W_SHARD + (b + 1) * LANES, (k + 1) * W_SHARD)
            parts = []
            for o0, s0, n in _slab_segments():
                a, z = max(lo, o0), min(hi, o0 + n)
                while a < z:
                    s = s0 + (a - o0)
                    run = min(z - a, LANES - s % LANES)
                    parts.append((s // LANES, ((a - lo) - s % LANES) % LANES, a - lo, run))
                    a += run
            plan.append((k, b, parts))

    def kern(x_ref, o32_ref, obf_ref):
        lane = lax.broadcasted_iota(jnp.int32, (tr, LANES), 1)
        for k, b, parts in plan:
            acc = jnp.zeros((tr, LANES), F32)
            for src, rot, first, run in parts:
                blk = x_ref[:, src * LANES:(src + 1) * LANES]
                if rot:
                    blk = pltpu.roll(blk, rot, 1)
                acc = jnp.where((lane >= first) & (lane < first + run), blk, acc)
            o32_ref[k, :, b * LANES:(b + 1) * LANES] = acc
            obf_ref[k, :, b * LANES:(b + 1) * LANES] = acc.astype(BF)

    out_spec = pl.BlockSpec((N_CHIP, tr, W_SHARD_PAD), lambda i: (0, i, 0))
    return tuple(pl.pallas_call(
        kern, name="shard_slabs", grid=(R // tr,),
        in_specs=[pl.BlockSpec((tr, W_PERM), lambda i: (i, 0))],
        out_specs=[out_spec, out_spec],
        out_shape=[jax.ShapeDtypeStruct((N_CHIP, R, W_SHARD_PAD), F32),
                   jax.ShapeDtypeStruct((N_CHIP, R, W_SHARD_PAD), BF)],
        compiler_params=_cparams("parallel"),
    )(dw_perm))


class _NoExchange:
    def __init__(self, w_in, rest):
        self.w_in_whole, self.rest, self.grads = w_in, rest, {}

    def w_in_comm(self):
        return None

    def w_in(self, outs):
        return self.w_in_whole

    def rest_weights_comm(self):
        return None

    def rest_weights(self, outs):
        return self.rest

    def swap_comm(self, pieces, tag):
        self.grads[tag] = [p32 for p32, _ in pieces]
        return None

    def swap_done(self, outs, tag):
        return None

    def reduce_done(self, outs, tag):
        pass


class _Exchange:
    def __init__(self, ci, chip, w_in_shard, rest_shards):
        self.ci, self.chip, self.w_in_shard, self.rest_shards = ci, chip, w_in_shard, rest_shards
        self.pieces, self.part_f32, self.halves = {}, {}, {}

    def _my_half(self, a, axis=0, other=False):
        rows = a.shape[axis] // 2
        return lax.dynamic_slice_in_dim(a, ((1 - self.ci) if other else self.ci) * rows, rows, axis=axis)

    def w_in_comm(self):
        return _gather_comm([self._my_half(self.w_in_shard).astype(BF)])

    def w_in(self, outs):
        return _col_sharded(outs[0])

    def rest_weights_comm(self):
        return _gather_comm([self._my_half(w).astype(BF) for w in self.rest_shards])

    def rest_weights(self, outs):
        w_ba, w_bb, w_out, w_fi, w_fo = outs
        return (_col_sharded(w_ba), _col_sharded(w_bb), _row_sharded(w_out), _col_sharded(w_fi),
                _row_sharded(w_fo))

    def swap_comm(self, pieces, tag):
        self.pieces[tag] = pieces
        return _swap_comm([pbf for _, pbf in pieces])

    def swap_done(self, got, tag):
        self.part_f32[tag], part_bf = [], []
        for l, ((p32, _), g_) in enumerate(zip(self.pieces[tag], got)):
            s32, sbf = _add_pair(p32, g_, self.ci, f"chip_sum_{tag}_{l}")
            self.part_f32[tag].append(s32)
            part_bf.append(sbf)
        return _scatter_comm(part_bf)

    def reduce_done(self, outs, tag):
        self.halves[tag] = [_add_three(p32, r, self.chip, f"shard_sum_{tag}_{l}")
                            for l, (p32, r) in enumerate(zip(self.part_f32[tag], outs))]


def _col_sharded(g):
    return jnp.transpose(g.reshape(N_CHIP, -1, g.shape[-1]), (1, 0, 2)).reshape(2 * g.shape[1], N_CHIP * g.shape[-1])


def _row_sharded(g):
    return g.reshape(N_DEV * g.shape[1], g.shape[-1])


def _rope_tables(pos):
    inv_freq = 1.0 / (ROPE_THETA ** (jnp.arange(0, HEAD_DIM, 2, dtype=F32) / HEAD_DIM))
    ang = pos.astype(F32)[:, None] * inv_freq
    cos, sin = jnp.cos(ang), jnp.sin(ang)
    return jnp.tile(cos, (1, 4)), jnp.tile(jnp.concatenate([-sin, sin], axis=1), (1, 2))


def _local_step(x, pos, ada, g1, g2, g3, g4, b_f, sinks, exch, target):
    S = x.shape[0]
    t_fox = _tile(S, 512, LANES) if S >= 1024 else S // 2
    t_fox_fwd = _tile(S, 1024, LANES) if S >= 2048 else S // 2
    shift_m, scale_m, gate_m, shift_f, scale_f, gate_f = [ada[i:i + 1] for i in range(N_ADA)]
    cos_t, sin_t = _rope_tables(pos)
    sinks_p = sinks.reshape(A_KV_HEADS, 4).T.reshape(A_Q_HEADS)
    b_f_pad = jnp.pad(b_f, (0, LANES - B_HEADS)).reshape(1, LANES)

    h1, outs = _pre_norm(x, g1, scale_m, shift_m, "pre_mix_norm", comm=exch.w_in_comm())
    w_a, w_b, w_g = _permute_in_weights(exch.w_in(outs))
    w_perm = jnp.concatenate([w_a, w_b, w_g], axis=1)
    p_a = _mm(h1, w_a, "nn", F32, "proj_a")
    p_b = _mm(h1, w_b, "nn", BF, "proj_b")
    p_g = _mm(h1, w_g, "nn", BF, "proj_g")
    (qk_a,) = _rope([p_a], [640], cos_t, sin_t, "rope_fwd")
    o_a, lse_a = _swa_fwd(qk_a, p_b, 0, sinks_p)
    q_aug, k_aug = _fox_prep_fwd(p_b, _fox_gate_fwd(p_a, b_f_pad), t_fox)
    comm = exch.rest_weights_comm()
    o_b, lse_b, outs = _fox_fwd(q_aug, k_aug, p_b, t_fox_fwd, comm=comm)
    w_ba, w_bb, w_out, w_fi, w_fo = exch.rest_weights(outs)
    w_ba_p = w_ba.reshape(A_Q_HEADS, HEAD_DIM, D_MODEL)[_A_ORDER].reshape(512, D_MODEL)
    pa = _mm(o_a, w_ba_p, "nn", BF, "branch_a")
    pb = _mm(o_b, w_bb, "nn", BF, "branch_b")
    merged = _merge_fwd(p_g, pa, pb)
    y1 = _mm(merged, w_out, "nn", BF, "out_proj")
    x2, h2 = _post_pre(x, y1, g2, gate_m, g3, scale_f, shift_f)
    gu = _mm(h2, w_fi, "nn", BF, "ffn_in")
    act = _swiglu_fwd(gu)
    y2 = _mm(act, w_fo, "nn", BF, "ffn_out")
    d_out, d_y2, st_f = _final(x2, y2, g4, gate_f, target)

    d_act = _mm(d_y2, w_fo, "nt", BF, "ffn_out_dx")
    row_pieces = lambda pair: tuple(t.reshape(N_CHIP, t.shape[0] // N_CHIP, t.shape[1]) for t in pair)
    dw_fo = row_pieces(_mm(act, d_y2, "tn", F32, "ffn_out_dw", twin=True))
    d_gu = _swiglu_bwd(d_act, gu)
    d_h2 = _mm(d_gu, w_fi, "nt", BF, "ffn_in_dx")
    dw_fi = _mm(h2, d_gu, "tn", F32, "ffn_in_dw", col_pieces=N_CHIP, twin=True)
    d_x2, d_y1, st_m = _mid_bwd(d_h2, x2, d_out, y1, g3, scale_f, g2, gate_m)
    d_merged = _mm(d_y1, w_out, "nt", BF, "out_proj_dx")
    dw_out = row_pieces(_mm(merged, d_y1, "tn", F32, "out_proj_dw", twin=True))
    d_pa, d_pb, d_ga, d_gb = _merge_bwd(d_merged, p_g, pa, pb)
    d_oa = _mm(d_pa, w_ba_p, "nt", F32, "branch_a_dx")
    dw_ba_p = _mm(o_a, d_pa, "tn", F32, "branch_a_dw", col_pieces=N_CHIP, twin=True)
    d_ob = _mm(d_pb, w_bb, "nt", F32, "branch_b_dx")
    dw_bb = _mm(o_b, d_pb, "tn", F32, "branch_b_dw", col_pieces=N_CHIP, twin=True)
    head_rows = lambda t: t.reshape(N_CHIP, A_Q_HEADS, HEAD_DIM, -1)[:, _A_INVERSE].reshape(t.shape)
    dw_ba = tuple(head_rows(t) for t in dw_ba_p)
    comm = exch.swap_comm([dw_ba, dw_bb, dw_out, dw_fi, dw_fo], "early")
    dq_a, dk_a, dv_a, d_sink, outs = _swa_bwd(qk_a, p_b, 0, o_a, d_oa, lse_a, sinks_p, comm=comm)
    comm = exch.swap_done(outs, "early")
    qb_aug, dob_aug = _fox_prep_bwd(q_aug, o_b, d_ob, lse_b, t_fox)
    dq_b, dk_b, dv_b, d_ck, d_cq, outs = _fox_bwd(qb_aug, k_aug, dob_aug, p_b, t_fox, comm=comm)
    exch.reduce_done(outs, "early")
    d_qa, d_ka = _rope([dq_a, dk_a], [512, LANES], cos_t, -sin_t, "rope_bwd")
    d_ck_cols = jnp.pad(d_ck.reshape(B_HEADS, S).T, ((0, 0), (0, LANES - B_HEADS)))
    d_f, d_bf = _fox_gate_bwd(d_cq, d_ck_cols, p_a, b_f_pad)
    d_proj = jnp.concatenate([d_qa, d_ka, d_f, dv_a.astype(BF), dq_b.astype(BF), dk_b.astype(BF),
                              dv_b.astype(BF), d_ga, d_gb], axis=1)
    dw_perm = _mm(h1, d_proj, "tn", F32, "proj_dw")
    swap = exch.swap_comm([_shard_slabs(dw_perm)], "late")
    comm = exch.swap_done(_run_comm(swap, "grads_to_sibling_late") if swap else None, "late")
    res = _mm(d_proj, w_perm, "nt", BF, "proj_dx", comm=comm)
    d_h1 = res[0] if comm else res
    exch.reduce_done(res[1] if comm else None, "late")
    grad_x, st_p = _pre_bwd(d_h1, x, d_x2, g1, scale_m)

    d_sinks = d_sink[:, :2, 0].T.reshape(A_Q_HEADS)
    small = jnp.concatenate([
        st_p[0], st_p[1], st_m[3], st_m[0], st_m[1], st_f[0],
        st_p[2], st_m[4], st_m[2], st_f[1],
        st_f[2], d_bf[0, :B_HEADS], d_sinks,
        jnp.zeros((SM_LEN - SM_SINK - A_Q_HEADS,), F32)])
    return grad_x, small


def kernel(x, c, positions, w_ada, b_ada, g_pre_mix, g_post_mix, w_in, b_f, sinks, w_branch_a, w_branch_b, w_out, g_pre_ffn, g_post_ffn, w_ffn_in, w_ffn_out, loss_target, m_w_ada, m_b_ada, m_g_pre_mix, m_g_post_mix, m_w_in, m_b_f, m_sinks, m_w_branch_a, m_w_branch_b, m_w_out, m_g_pre_ffn, m_g_post_ffn, m_w_ffn_in, m_w_ffn_out, v_w_ada, v_b_ada, v_g_pre_mix, v_g_post_mix, v_w_in, v_b_f, v_sinks, v_w_branch_a, v_w_branch_b, v_w_out, v_g_pre_ffn, v_g_post_ffn, v_w_ffn_in, v_w_ffn_out):
    xi, yi, ci = _here()
    chip = 2 * xi + yi
    dev = 2 * chip + ci

    (c_g,) = _allgather8([c.reshape(8, LANES)], "gather_c")
    c_all = c_g.reshape(N_DEV, D_MODEL)
    exch = _Exchange(ci, chip, w_in[0], [w_branch_a[0], w_branch_b[0], w_out[0], w_ffn_in[0], w_ffn_out[0]])

    ada_cols = _mm(c_all, w_ada[0], "nn", F32, "ada_fwd")
    (ada_g,) = _allgather8([ada_cols], "gather_ada")
    ada_mine = lax.dynamic_index_in_dim(ada_g.reshape(N_CHIP, 2, N_DEV, -1)[:, 0], dev, axis=1, keepdims=False)
    ada = (ada_mine.reshape(-1) + b_ada[0]).reshape(N_ADA, D_MODEL)

    grad_x, small = _local_step(
        x[0], positions[0], ada, g_pre_mix, g_post_mix, g_pre_ffn, g_post_ffn, b_f[0], sinks[0],
        exch, loss_target[0])

    (small_g,) = _allgather8([small.reshape(8, SM_LEN // 8)], "gather_small")
    small_all = small_g.reshape(N_DEV, SM_LEN)
    small_tot, loss_row = _small_finalize(small_all)
    loss = loss_row[0, 0]
    d_ada_cols = lax.dynamic_slice_in_dim(small_all[:, :N_ADA * D_MODEL], chip * (N_ADA * D_MODEL // N_CHIP),
                                          N_ADA * D_MODEL // N_CHIP, axis=1)
    g_w_ada = _ada_dw(c_all.T, d_ada_cols)

    joined = _sibling_join(exch.halves["late"] + exch.halves["early"], "grads_join")
    g_w_in, g_w_ba, g_w_bb, g_w_out, g_w_fi, g_w_fo = [j.reshape(2 * j.shape[1], j.shape[2]) for j in joined]

    def small_vec(b_ada_, g1_, g2_, g3_, g4_, b_f_, sinks_):
        return jnp.concatenate([b_ada_[0], g1_[0], g2_[0], g3_[0], g4_[0], jnp.zeros((D_MODEL,), F32),
                                b_f_[0], sinks_[0], jnp.zeros((SM_LEN - SM_SINK - A_Q_HEADS,), F32)]
                               ).reshape(8, SM_LEN // 8)

    sw = small_vec(b_ada, g_pre_mix, g_post_mix, g_pre_ffn, g_post_ffn, b_f, sinks)
    sm = small_vec(m_b_ada, m_g_pre_mix, m_g_post_mix, m_g_pre_ffn, m_g_post_ffn, m_b_f, m_sinks)
    sv = small_vec(v_b_ada, v_g_pre_mix, v_g_post_mix, v_g_pre_ffn, v_g_post_ffn, v_b_f, v_sinks)
    s_upd = [u.reshape(SM_LEN) for u in _adamw(sw, small_tot.reshape(8, SM_LEN // 8), sm, sv, "adamw_small")]
    s_grad = small_tot.reshape(SM_LEN)

    def unpack(vec):
        row = lambda a, n: vec[a:a + n].reshape(1, n)
        return dict(b_ada=row(SM_ADA, N_ADA * D_MODEL), g_pre_mix=row(SM_G, D_MODEL),
                    g_post_mix=row(SM_G + D_MODEL, D_MODEL), g_pre_ffn=row(SM_G + 2 * D_MODEL, D_MODEL),
                    g_post_ffn=row(SM_G + 3 * D_MODEL, D_MODEL), b_f=row(SM_BF, B_HEADS),
                    sinks=row(SM_SINK, A_Q_HEADS))

    big = dict(
        w_ada=(w_ada, g_w_ada, m_w_ada, v_w_ada),
        w_branch_a=(w_branch_a, g_w_ba, m_w_branch_a, v_w_branch_a),
        w_branch_b=(w_branch_b, g_w_bb, m_w_branch_b, v_w_branch_b),
        w_out=(w_out, g_w_out, m_w_out, v_w_out), w_ffn_in=(w_ffn_in, g_w_fi, m_w_ffn_in, v_w_ffn_in),
        w_ffn_out=(w_ffn_out, g_w_fo, m_w_ffn_out, v_w_ffn_out))
    grads, deltas, new_m, new_v = unpack(s_grad), unpack(s_upd[0]), unpack(s_upd[1]), unpack(s_upd[2])
    for n, (w_, g_, m_, v_) in big.items():
        d_, nm_, nv_ = _adamw(w_[0], g_, m_[0], v_[0], "adamw_" + n)
        grads[n], deltas[n], new_m[n], new_v[n] = g_[None], d_[None], nm_[None], nv_[None]
    pad_cols = lambda a: jnp.pad(a, ((0, 0), (0, W_SHARD_PAD - W_SHARD)))
    upd = _adamw(pad_cols(w_in[0]), g_w_in, pad_cols(m_w_in[0]), pad_cols(v_w_in[0]), "adamw_w_in")
    grads["w_in"], deltas["w_in"], new_m["w_in"], new_v["w_in"] = [t[None, :, :W_SHARD] for t in (g_w_in, *upd)]

    names = ["w_ada", "b_ada", "g_pre_mix", "g_post_mix", "w_in", "b_f", "sinks", "w_branch_a", "w_branch_b",
             "w_out", "g_pre_ffn", "g_post_ffn", "w_ffn_in", "w_ffn_out"]
    return (loss, grad_x[None], *[grads[n] for n in names], *[deltas[n] for n in names],
            *[new_m[n] for n in names], *[new_v[n] for n in names])
```

```python
import functools
import math

import numpy as np
import jax
import jax.numpy as jnp
from jax import lax
from jax.experimental import pallas as pl
from jax.experimental.pallas import tpu as pltpu

F32 = jnp.float32
BF = jnp.bfloat16

D_MODEL = 1024
HEAD_DIM = 64
LANES = 128
WINDOW = 128
A_Q_HEADS = 8
A_KV_HEADS = 2
B_HEADS = 8
D_FF = 2816
ROPE_THETA = 10000.0
RMS_EPS = 1e-6
N_ADA = 6
N_DEV = 8
N_CHIP = 4

ADAM_LR = 0.001
ADAM_B1 = 0.9
ADAM_B2 = 0.999
ADAM_EPS = 1e-08
ADAM_WD = 0.01
ADAM_STEP = 10

VMEM_LIMIT = 48 * 1024 * 1024
MESH = pl.DeviceIdType.MESH

A_HEAD_ORDER = (0, 4, 1, 5, 2, 6, 3, 7)

OFF_QA, OFF_KA, OFF_F = 0, 512, 640
W_A = 768
OFF_VA, OFF_QB, OFF_KB, OFF_VB = 0, 128, 640, 1152
W_B = 1664
W_G = 2048
W_PERM = W_A + W_B + W_G
W_SHARD = 1090
W_SHARD_PAD = 1152


def _tile(n, cap, mult=LANES):
    if n <= cap:
        return n
    t = (cap // mult) * mult
    while t >= mult:
        if n % t == 0:
            return t
        t -= mult
    raise ValueError(f"no tile for {n}")


MXU_WIDTH = 256
MM_OPERAND_BYTES = 28 * 1024 * 1024


def _mm_tiles(M, N, K, a_bytes, b_bytes, tm_cap, tn_cap):
    tm = _tile(M, tm_cap)
    try:
        tn = _tile(N, tn_cap, MXU_WIDTH)
    except ValueError:
        tn = _tile(N, tn_cap)
    fits = lambda tk: 2 * tk * (tm * a_bytes + tn * b_bytes) <= MM_OPERAND_BYTES
    tk = K if fits(K) else next(t for t in range(K // LANES * LANES, 0, -LANES) if K % t == 0 and fits(t))
    return tm, tn, tk


def _cparams(*sem):
    return pltpu.CompilerParams(dimension_semantics=sem, vmem_limit_bytes=VMEM_LIMIT)


def _own_refs(refs, comm, n_in, n_out, n_scratch):
    if comm is None:
        return list(refs), None
    return comm.split(refs, n_in, n_out, n_scratch)


def _comm_specs(comm, side):
    if comm is None:
        return []
    return [pl.BlockSpec(memory_space=pl.ANY)] * len(comm.ins if side == "in" else comm.out_shapes)


def _comm_edge(comm, comm_refs, grid, first):
    if comm is None:
        return
    at_edge = None
    for axis, n in enumerate(grid):
        here = pl.program_id(axis) == (0 if first else n - 1)
        at_edge = here if at_edge is None else at_edge & here
    pl.when(at_edge)(lambda: (comm.start if first else comm.finish)(*comm_refs))


def _mm(a, b, mode, out_dtype, name, tm_cap=512, tn_cap=2816, comm=None, col_pieces=1, twin=False):
    if mode == "nn":
        (M, K), (K2, N) = a.shape, b.shape
        dims = (((1,), (0,)), ((), ()))
    elif mode == "nt":
        (M, K), (N, K2) = a.shape, b.shape
        dims = (((1,), (1,)), ((), ()))
    else:
        (K, M), (K2, N) = a.shape, b.shape
        dims = (((0,), (0,)), ((), ()))
    assert K == K2, (a.shape, b.shape, mode)
    tm, tn, tk = _mm_tiles(M, N // col_pieces, K, a.dtype.itemsize, b.dtype.itemsize, tm_cap, tn_cap)
    nk = K // tk
    n_out = 2 if twin else 1
    n_scratch = 1 if nk > 1 else 0
    if mode == "nn":
        a_spec = pl.BlockSpec((tm, tk), lambda i, j, k: (i, k))
        b_spec = pl.BlockSpec((tk, tn), lambda i, j, k: (k, j))
    elif mode == "nt":
        a_spec = pl.BlockSpec((tm, tk), lambda i, j, k: (i, k))
        b_spec = pl.BlockSpec((tn, tk), lambda i, j, k: (j, k))
    else:
        a_spec = pl.BlockSpec((tk, tm), lambda i, j, k: (k, i))
        b_spec = pl.BlockSpec((tk, tn), lambda i, j, k: (k, j))

    grid = (M // tm, N // tn, nk)

    def kern(*refs):
        own, comm_refs = _own_refs(refs, comm, 2, n_out, n_scratch)
        a_ref, b_ref, o_refs = own[0], own[1], own[2:2 + n_out]
        k = pl.program_id(2)
        _comm_edge(comm, comm_refs, grid, first=True)
        part = lax.dot_general(a_ref[...].astype(BF), b_ref[...].astype(BF), dims,
                               preferred_element_type=F32)
        if nk == 1:
            for o_ref in o_refs:
                o_ref[...] = part.astype(o_ref.dtype)
        else:
            acc_ref = own[2 + n_out]

            @pl.when(k == 0)
            def _():
                acc_ref[...] = part

            @pl.when(k > 0)
            def _():
                acc_ref[...] += part

            @pl.when(k == nk - 1)
            def _():
                for o_ref in o_refs:
                    o_ref[...] = acc_ref[...].astype(o_ref.dtype)

        _comm_edge(comm, comm_refs, grid, first=False)

    if col_pieces > 1:
        per = N // col_pieces // tn
        out_spec = pl.BlockSpec((None, tm, tn), lambda i, j, k: (j // per, i, j % per))
        shape = (col_pieces, M, N // col_pieces)
    else:
        out_spec = pl.BlockSpec((tm, tn), lambda i, j, k: (i, j))
        shape = (M, N)
    dtypes = [out_dtype, BF] if twin else [out_dtype]
    res = pl.pallas_call(
        kern, name=name, grid=grid,
        in_specs=[a_spec, b_spec] + _comm_specs(comm, "in"),
        out_specs=[out_spec] * n_out + _comm_specs(comm, "out"),
        out_shape=[jax.ShapeDtypeStruct(shape, d) for d in dtypes] + (comm.out_shapes if comm else []),
        scratch_shapes=[pltpu.VMEM((tm, tn), F32)] * n_scratch + (comm.sem_shapes if comm else []),
        compiler_params=_cparams("parallel", "parallel", "arbitrary"),
    )(a, b, *(comm.ins if comm else []))
    own = res[0] if n_out == 1 else tuple(res[:n_out])
    return (own, res[n_out:]) if comm else own


ROWS = 512


def _row_spec(tm, width=D_MODEL, col=0):
    return pl.BlockSpec((tm, width), lambda i: (i, col))


def _vec_spec(width=D_MODEL):
    return pl.BlockSpec((1, width), lambda i: (0, 0))


def _rms(x):
    return lax.rsqrt(jnp.mean(x * x, axis=-1, keepdims=True) + RMS_EPS)


def _colsum(x):
    return jnp.sum(x, axis=0, keepdims=True)


def _norm_bwd(d_xn, xn, r):
    return r * (d_xn - xn * jnp.mean(d_xn * xn, axis=-1, keepdims=True))


def _pre_norm(x, g, scale, shift, name, comm=None):
    S = x.shape[0]
    tm = _tile(S, ROWS, 8)
    grid = (S // tm,)

    def kern(*refs):
        (x_ref, g_ref, sc_ref, sh_ref, h_ref), comm_refs = _own_refs(refs, comm, 4, 1, 0)
        _comm_edge(comm, comm_refs, grid, first=True)
        xf = x_ref[...]
        y = xf * _rms(xf) * g_ref[...]
        h_ref[...] = (y * (1.0 + sc_ref[...]) + sh_ref[...]).astype(BF)
        _comm_edge(comm, comm_refs, grid, first=False)

    res = pl.pallas_call(
        kern, name=name, grid=grid,
        in_specs=[_row_spec(tm), _vec_spec(), _vec_spec(), _vec_spec()] + _comm_specs(comm, "in"),
        out_specs=[_row_spec(tm)] + _comm_specs(comm, "out"),
        out_shape=[jax.ShapeDtypeStruct((S, D_MODEL), BF)] + (comm.out_shapes if comm else []),
        scratch_shapes=comm.sem_shapes if comm else [],
        compiler_params=_cparams("arbitrary"),
    )(x, g, scale, shift, *(comm.ins if comm else []))
    return res[0], res[1:]


def _post_pre(x, y1, g2, gate_m, g3, scale_f, shift_f):
    S = x.shape[0]
    tm = _tile(S, ROWS, 8)

    def kern(x_ref, y_ref, g2_ref, gm_ref, g3_ref, sc_ref, sh_ref, x2_ref, h2_ref):
        y = y_ref[...].astype(F32)
        n2 = y * _rms(y) * g2_ref[...]
        x2 = x_ref[...] + gm_ref[...] * n2
        x2_ref[...] = x2
        n3 = x2 * _rms(x2) * g3_ref[...]
        h2_ref[...] = (n3 * (1.0 + sc_ref[...]) + sh_ref[...]).astype(BF)

    return pl.pallas_call(
        kern, name="post_mix_pre_ffn", grid=(S // tm,),
        in_specs=[_row_spec(tm), _row_spec(tm)] + [_vec_spec()] * 5,
        out_specs=[_row_spec(tm), _row_spec(tm)],
        out_shape=[jax.ShapeDtypeStruct((S, D_MODEL), F32), jax.ShapeDtypeStruct((S, D_MODEL), BF)],
        compiler_params=_cparams("parallel"),
    )(x, y1, g2, gate_m, g3, scale_f, shift_f)


def _stats_spec():
    return pl.BlockSpec((8, D_MODEL), lambda i: (0, 0))


def _final(x2, y2, g4, gate_f, target):
    S = x2.shape[0]
    tm = _tile(S, ROWS, 8)

    def kern(x2_ref, y_ref, g4_ref, gf_ref, t_ref, dout_ref, dy_ref, st_ref):
        @pl.when(pl.program_id(0) == 0)
        def _():
            st_ref[...] = jnp.zeros_like(st_ref)

        y = y_ref[...].astype(F32)
        r = _rms(y)
        yn = y * r
        n4 = yn * g4_ref[...]
        diff = x2_ref[...] + gf_ref[...] * n4 - t_ref[...]
        d_out = diff / D_MODEL
        dout_ref[...] = d_out
        dn = d_out * gf_ref[...]
        dy_ref[...] = _norm_bwd(dn * g4_ref[...], yn, r).astype(BF)
        st_ref[0:1, :] += _colsum(d_out * n4)
        st_ref[1:2, :] += _colsum(dn * yn)
        st_ref[2:3, :] += _colsum(diff * diff)

    return pl.pallas_call(
        kern, name="final_loss", grid=(S // tm,),
        in_specs=[_row_spec(tm), _row_spec(tm), _vec_spec(), _vec_spec(), _row_spec(tm)],
        out_specs=[_row_spec(tm), _row_spec(tm), _stats_spec()],
        out_shape=[jax.ShapeDtypeStruct((S, D_MODEL), F32), jax.ShapeDtypeStruct((S, D_MODEL), BF),
                   jax.ShapeDtypeStruct((8, D_MODEL), F32)],
        compiler_params=_cparams("arbitrary"),
    )(x2, y2, g4, gate_f, target)


def _mid_bwd(d_h2, x2, d_out, y1, g3, scale_f, g2, gate_m):
    S = x2.shape[0]
    tm = _tile(S, ROWS, 8)

    def kern(dh_ref, x2_ref, dout_ref, y_ref, g3_ref, sc_ref, g2_ref, gm_ref, dx2_ref, dy_ref, st_ref):
        @pl.when(pl.program_id(0) == 0)
        def _():
            st_ref[...] = jnp.zeros_like(st_ref)

        dh = dh_ref[...].astype(F32)
        x2 = x2_ref[...]
        r3 = _rms(x2)
        xn = x2 * r3
        one_sc = 1.0 + sc_ref[...]
        d_x2 = dout_ref[...] + _norm_bwd(dh * one_sc * g3_ref[...], xn, r3)
        dx2_ref[...] = d_x2
        y = y_ref[...].astype(F32)
        r2 = _rms(y)
        yn = y * r2
        dn = d_x2 * gm_ref[...]
        dy_ref[...] = _norm_bwd(dn * g2_ref[...], yn, r2).astype(BF)
        st_ref[0:1, :] += _colsum(dh)
        st_ref[1:2, :] += _colsum(dh * (xn * g3_ref[...]))
        st_ref[2:3, :] += _colsum(dh * one_sc * xn)
        st_ref[3:4, :] += _colsum(d_x2 * (yn * g2_ref[...]))
        st_ref[4:5, :] += _colsum(dn * yn)

    return pl.pallas_call(
        kern, name="mid_bwd", grid=(S // tm,),
        in_specs=[_row_spec(tm)] * 4 + [_vec_spec()] * 4,
        out_specs=[_row_spec(tm), _row_spec(tm), _stats_spec()],
        out_shape=[jax.ShapeDtypeStruct((S, D_MODEL), F32), jax.ShapeDtypeStruct((S, D_MODEL), BF),
                   jax.ShapeDtypeStruct((8, D_MODEL), F32)],
        compiler_params=_cparams("arbitrary"),
    )(d_h2, x2, d_out, y1, g3, scale_f, g2, gate_m)


def _pre_bwd(d_h1, x, d_x2, g1, scale_m):
    S = x.shape[0]
    tm = _tile(S, ROWS, 8)

    def kern(dh_ref, x_ref, dx2_ref, g_ref, sc_ref, gx_ref, st_ref):
        @pl.when(pl.program_id(0) == 0)
        def _():
            st_ref[...] = jnp.zeros_like(st_ref)

        dh = dh_ref[...].astype(F32)
        xf = x_ref[...]
        r = _rms(xf)
        xn = xf * r
        one_sc = 1.0 + sc_ref[...]
        gx_ref[...] = dx2_ref[...] + _norm_bwd(dh * one_sc * g_ref[...], xn, r)
        st_ref[0:1, :] += _colsum(dh)
        st_ref[1:2, :] += _colsum(dh * (xn * g_ref[...]))
        st_ref[2:3, :] += _colsum(dh * one_sc * xn)

    return pl.pallas_call(
        kern, name="pre_mix_bwd", grid=(S // tm,),
        in_specs=[_row_spec(tm)] * 3 + [_vec_spec()] * 2,
        out_specs=[_row_spec(tm), _stats_spec()],
        out_shape=[jax.ShapeDtypeStruct((S, D_MODEL), F32), jax.ShapeDtypeStruct((8, D_MODEL), F32)],
        compiler_params=_cparams("arbitrary"),
    )(d_h1, x, d_x2, g1, scale_m)


def _rope(xs, widths, cos_t, sin_t, name):
    S = xs[0].shape[0]
    tm = _tile(S, 512, 8)
    n = len(xs)

    def kern(*refs):
        cos = refs[n][...]
        sin = refs[n + 1][...]
        first = (lax.broadcasted_iota(jnp.int32, cos.shape, 1) % HEAD_DIM) < HEAD_DIM // 2
        for x_ref, o_ref, w in zip(refs[:n], refs[n + 2:], widths):
            for c0 in range(0, w, LANES):
                v = x_ref[:, c0:c0 + LANES]
                partner = jnp.where(first, pltpu.roll(v, LANES - HEAD_DIM // 2, 1),
                                    pltpu.roll(v, HEAD_DIM // 2, 1))
                o_ref[:, c0:c0 + LANES] = (v * cos + partner * sin).astype(BF)

    return pl.pallas_call(
        kern, name=name, grid=(S // tm,),
        in_specs=[_row_spec(tm, w) for w in widths] + [_row_spec(tm, LANES)] * 2,
        out_specs=[_row_spec(tm, w) for w in widths],
        out_shape=[jax.ShapeDtypeStruct((S, w), BF) for w in widths],
        compiler_params=_cparams("parallel"),
    )(*xs, cos_t, sin_t)


def _merge_fwd(pg, pa, pb):
    S = pa.shape[0]
    tm = _tile(S, ROWS, 8)

    def kern(ga_ref, gb_ref, pa_ref, pb_ref, o_ref):
        ga = jax.nn.sigmoid(ga_ref[...].astype(F32))
        gb = jax.nn.sigmoid(gb_ref[...].astype(F32))
        o_ref[...] = (ga * pa_ref[...].astype(F32) + gb * pb_ref[...].astype(F32)).astype(BF)

    return pl.pallas_call(
        kern, name="merge_fwd", grid=(S // tm,),
        in_specs=[_row_spec(tm, col=0), _row_spec(tm, col=1), _row_spec(tm), _row_spec(tm)],
        out_specs=_row_spec(tm),
        out_shape=jax.ShapeDtypeStruct((S, D_MODEL), BF),
        compiler_params=_cparams("parallel"),
    )(pg, pg, pa, pb)


def _merge_bwd(d_merged, pg, pa, pb):
    S = pa.shape[0]
    tm = _tile(S, ROWS, 8)

    def kern(dm_ref, ga_ref, gb_ref, pa_ref, pb_ref, dpa_ref, dpb_ref, dga_ref, dgb_ref):
        dm = dm_ref[...].astype(F32)
        ga = jax.nn.sigmoid(ga_ref[...].astype(F32))
        gb = jax.nn.sigmoid(gb_ref[...].astype(F32))
        dpa_ref[...] = (dm * ga).astype(BF)
        dpb_ref[...] = (dm * gb).astype(BF)
        dga_ref[...] = (dm * pa_ref[...].astype(F32) * ga * (1.0 - ga)).astype(BF)
        dgb_ref[...] = (dm * pb_ref[...].astype(F32) * gb * (1.0 - gb)).astype(BF)

    bf_out = jax.ShapeDtypeStruct((S, D_MODEL), BF)
    return pl.pallas_call(
        kern, name="merge_bwd", grid=(S // tm,),
        in_specs=[_row_spec(tm), _row_spec(tm, col=0), _row_spec(tm, col=1), _row_spec(tm), _row_spec(tm)],
        out_specs=[_row_spec(tm)] * 4,
        out_shape=[bf_out] * 4,
        compiler_params=_cparams("parallel"),
    )(d_merged, pg, pg, pa, pb)


def _swiglu_fwd(gu):
    S = gu.shape[0]
    tm = _tile(S, ROWS, 8)
    tc = _tile(D_FF, 1408)
    nc = D_FF // tc

    def kern(g_ref, u_ref, o_ref):
        g = g_ref[...].astype(F32)
        o_ref[...] = (g * jax.nn.sigmoid(g) * u_ref[...].astype(F32)).astype(BF)

    return pl.pallas_call(
        kern, name="swiglu_fwd", grid=(S // tm, nc),
        in_specs=[pl.BlockSpec((tm, tc), lambda i, j: (i, j)),
                  pl.BlockSpec((tm, tc), lambda i, j: (i, j + nc))],
        out_specs=pl.BlockSpec((tm, tc), lambda i, j: (i, j)),
        out_shape=jax.ShapeDtypeStruct((S, D_FF), BF),
        compiler_params=_cparams("parallel", "parallel"),
    )(gu, gu)


def _swiglu_bwd(d_act, gu):
    S = gu.shape[0]
    tm = _tile(S, ROWS // 2, 8)

    def kern(da_ref, g_ref, u_ref, o_ref):
        g = g_ref[...].astype(F32)
        u = u_ref[...].astype(F32)
        da = da_ref[...].astype(F32)
        sg = jax.nn.sigmoid(g)
        o_ref[:, :D_FF] = (da * u * (sg * (1.0 + g * (1.0 - sg)))).astype(BF)
        o_ref[:, D_FF:] = (da * (g * sg)).astype(BF)

    return pl.pallas_call(
        kern, name="swiglu_bwd", grid=(S // tm,),
        in_specs=[_row_spec(tm, D_FF), _row_spec(tm, D_FF, 0), _row_spec(tm, D_FF, 1)],
        out_specs=_row_spec(tm, 2 * D_FF),
        out_shape=jax.ShapeDtypeStruct((S, 2 * D_FF), BF),
        compiler_params=_cparams("parallel"),
    )(d_act, gu, gu)


def _split3(x):
    hi = x.astype(BF)
    r1 = x - hi.astype(F32)
    mid = r1.astype(BF)
    lo = (r1 - mid.astype(F32)).astype(BF)
    return hi, mid, lo


def _tri_dot(tri, x):
    return sum(jnp.dot(tri, part, preferred_element_type=F32) for part in _split3(x))


def _log_sigmoid(z):
    return jnp.minimum(z, 0.0) - jnp.log(1.0 + jnp.exp(-jnp.abs(z)))


def _fox_gate_fwd(pa, b_f_pad):
    S = pa.shape[0]
    T = _tile(S, 512, 8)
    f_col = OFF_F // LANES

    def kern(z_ref, b_ref, cum_ref, carry_ref):
        @pl.when(pl.program_id(0) == 0)
        def _():
            carry_ref[...] = jnp.zeros_like(carry_ref)

        log_f = _log_sigmoid(z_ref[...] + b_ref[...])
        row = lax.broadcasted_iota(jnp.int32, (T, T), 0)
        col = lax.broadcasted_iota(jnp.int32, (T, T), 1)
        tri = (col <= row).astype(BF)
        cum = _tri_dot(tri, log_f) + carry_ref[...]
        cum_ref[...] = cum
        carry_ref[...] = cum[T - 1:T, :]

    return pl.pallas_call(
        kern, name="fox_gate_fwd", grid=(S // T,),
        in_specs=[_row_spec(T, LANES, f_col), _vec_spec(LANES)],
        out_specs=_row_spec(T, LANES),
        out_shape=jax.ShapeDtypeStruct((S, LANES), F32),
        scratch_shapes=[pltpu.VMEM((1, LANES), F32)],
        compiler_params=_cparams("arbitrary"),
    )(pa, b_f_pad)


def _fox_gate_bwd(rowsum_ds, colsum_ds, pa, b_f_pad):
    S = pa.shape[0]
    T = _tile(S, 512, 8)
    nb = S // T
    f_col = OFF_F // LANES

    def kern(dr_ref, dc_ref, z_ref, b_ref, df_ref, dbf_ref, carry_ref):
        @pl.when(pl.program_id(0) == 0)
        def _():
            carry_ref[...] = jnp.zeros_like(carry_ref)
            dbf_ref[...] = jnp.zeros_like(dbf_ref)

        row = lax.broadcasted_iota(jnp.int32, (T, T), 0)
        col = lax.broadcasted_iota(jnp.int32, (T, T), 1)
        tri = (col >= row).astype(BF)
        rev = _tri_dot(tri, dr_ref[...] - dc_ref[...]) + carry_ref[...]
        carry_ref[...] = rev[0:1, :]
        z = z_ref[...] + b_ref[...]
        lane = lax.broadcasted_iota(jnp.int32, (T, LANES), 1)
        d_z = jnp.where(lane < B_HEADS, rev * jax.nn.sigmoid(-z), 0.0)
        df_ref[...] = d_z.astype(BF)
        dbf_ref[0:1, :] += _colsum(d_z)

    return pl.pallas_call(
        kern, name="fox_gate_bwd", grid=(nb,),
        in_specs=[pl.BlockSpec((T, LANES), lambda i: (nb - 1 - i, 0)),
                  pl.BlockSpec((T, LANES), lambda i: (nb - 1 - i, 0)),
                  pl.BlockSpec((T, LANES), lambda i: (nb - 1 - i, f_col)),
                  _vec_spec(LANES)],
        out_specs=[pl.BlockSpec((T, LANES), lambda i: (nb - 1 - i, 0)),
                   pl.BlockSpec((8, LANES), lambda i: (0, 0))],
        out_shape=[jax.ShapeDtypeStruct((S, LANES), BF), jax.ShapeDtypeStruct((8, LANES), F32)],
        scratch_shapes=[pltpu.VMEM((1, LANES), F32)],
        compiler_params=_cparams("arbitrary"),
    )(rowsum_ds, colsum_ds, pa, b_f_pad)


NEG_INF = float("-inf")
QK_SCALE = 1.0 / math.sqrt(HEAD_DIM)


def _half_mask(shape, half):
    lane = lax.broadcasted_iota(jnp.int32, shape, 1)
    return (lane < HEAD_DIM) if half == 0 else (lane >= HEAD_DIM)


def _bias_lanes(shape, half, q_side_terms, k_side_terms):
    lane = lax.broadcasted_iota(jnp.int32, shape, 1)
    base = HEAD_DIM * (1 - half)
    n_q = len(q_side_terms) if q_side_terms is not None else 3
    n_k = len(k_side_terms) if k_side_terms is not None else 3
    out = jnp.zeros(shape, F32)
    for t in range(n_q):
        out = jnp.where(lane == base + t, q_side_terms[t].astype(F32) if q_side_terms is not None else 1.0, out)
    for t in range(n_k):
        out = jnp.where(lane == base + n_q + t,
                        k_side_terms[t].astype(F32) if k_side_terms is not None else 1.0, out)
    return out


def _head_column(block, head):
    lane = lax.broadcasted_iota(jnp.int32, block.shape, 1)
    return jnp.sum(jnp.where(lane == head, block, 0.0), axis=1, keepdims=True)


def _fox_prep_fwd(p_b, cum, T):
    S = p_b.shape[0]

    def kern(q_ref, k_ref, c_ref, qa_ref, ka_ref):
        p_id = pl.program_id(0)
        q, k, cum_blk = q_ref[...], k_ref[...], c_ref[...]
        for half in (0, 1):
            hm = _half_mask((T, LANES), half)
            c3 = _split3(_head_column(cum_blk, 2 * p_id + half))
            qa_ref[half] = jnp.where(hm, q.astype(F32) * QK_SCALE, _bias_lanes((T, LANES), half, c3, None)).astype(BF)
            ka_ref[half] = jnp.where(hm, k.astype(F32),
                                     _bias_lanes((T, LANES), half, None, [-t.astype(F32) for t in c3])).astype(BF)

    out_spec = pl.BlockSpec((None, 2, T, LANES), lambda p, i: (p, 0, i, 0))
    shape = jax.ShapeDtypeStruct((B_HEADS // 2, 2, S, LANES), BF)
    return pl.pallas_call(
        kern, name="fox_prep_fwd", grid=(B_HEADS // 2, S // T),
        in_specs=[pl.BlockSpec((T, LANES), lambda p, i: (i, OFF_QB // LANES + p)),
                  pl.BlockSpec((T, LANES), lambda p, i: (i, OFF_KB // LANES + p)),
                  pl.BlockSpec((T, LANES), lambda p, i: (i, 0))],
        out_specs=[out_spec, out_spec], out_shape=[shape, shape],
        compiler_params=_cparams("parallel", "parallel"),
    )(p_b, p_b, cum)


def _fox_fwd(q_aug, k_aug, p_b, T, comm=None):
    S = p_b.shape[0]
    nq = S // T
    n_pairs = B_HEADS // 2
    grid = (n_pairs, nq)

    def kern(*refs):
        (q_ref, k_ref, v_ref, o_ref, lse_ref), comm_refs = _own_refs(refs, comm, 3, 2, 0)
        _comm_edge(comm, comm_refs, grid, first=True)
        i = pl.program_id(1)
        rowcol = lax.broadcasted_iota(jnp.int32, (T, T), 0) - lax.broadcasted_iota(jnp.int32, (T, T), 1)
        qs = (q_ref[0], q_ref[1])

        def step(j, carry, masked):
            rows = pl.ds(pl.multiple_of(j * T, T), T)
            vj = v_ref[rows, :]
            new = []
            for half in (0, 1):
                m, l, acc = carry[half]
                s = lax.dot_general(qs[half], k_ref[half, rows, :], (((1,), (1,)), ((), ())),
                                    preferred_element_type=F32)
                if masked:
                    s = jnp.where(rowcol >= 0, s, NEG_INF)
                m_new = jnp.maximum(m, jnp.max(s, axis=1, keepdims=True))
                alpha = jnp.exp(m - m_new)
                p = jnp.exp(s - m_new)
                l_new = alpha * l + jnp.sum(p, axis=1, keepdims=True)
                acc_new = alpha * acc + jnp.dot(p.astype(BF), vj, preferred_element_type=F32)
                new.append((m_new, l_new, acc_new))
            return tuple(new)

        one = (jnp.full((T, 1), NEG_INF, F32), jnp.zeros((T, 1), F32), jnp.zeros((T, LANES), F32))
        carry = lax.fori_loop(0, i, functools.partial(step, masked=False), (one, one))
        (m0, l0, acc0), (m1, l1, acc1) = step(i, carry, True)
        hm0 = _half_mask((T, LANES), 0)
        o_ref[...] = jnp.where(hm0, acc0 / l0, acc1 / l1)
        lse_ref[...] = jnp.where(hm0, m0 + jnp.log(l0), m1 + jnp.log(l1))
        _comm_edge(comm, comm_refs, grid, first=False)

    out_spec = pl.BlockSpec((T, LANES), lambda p, i: (i, p))
    res = pl.pallas_call(
        kern, name="fox_fwd", grid=grid,
        in_specs=[pl.BlockSpec((None, 2, T, LANES), lambda p, i: (p, 0, i, 0)),
                  pl.BlockSpec((None, 2, S, LANES), lambda p, i: (p, 0, 0, 0)),
                  pl.BlockSpec((S, LANES), lambda p, i: (0, OFF_VB // LANES + p))] + _comm_specs(comm, "in"),
        out_specs=[out_spec, out_spec] + _comm_specs(comm, "out"),
        out_shape=[jax.ShapeDtypeStruct((S, n_pairs * LANES), F32)] * 2 + (comm.out_shapes if comm else []),
        scratch_shapes=comm.sem_shapes if comm else [],
        compiler_params=_cparams("arbitrary", "arbitrary"),
    )(q_aug, k_aug, p_b, *(comm.ins if comm else []))
    return res[0], res[1], res[2:]


def _fox_prep_bwd(q_aug, o, do, lse, T):
    S = o.shape[0]

    def kern(qa_ref, o_ref, do_ref, lse_ref, qb_ref, dob_ref):
        o_blk, do_blk, lse_blk = o_ref[...], do_ref[...], lse_ref[...]
        lane = lax.broadcasted_iota(jnp.int32, (T, LANES), 1)
        for half in (0, 1):
            hm = _half_mask((T, LANES), half)
            base = HEAD_DIM * (1 - half)
            qa = qa_ref[half].astype(F32)
            cq = jnp.sum(jnp.where((lane >= base) & (lane < base + 3), qa, 0.0), axis=1, keepdims=True)
            b3 = _split3(cq - lse_blk[:, HEAD_DIM * half:HEAD_DIM * half + 1])
            qb_ref[half] = jnp.where(hm, qa, _bias_lanes((T, LANES), half, b3, None)).astype(BF)
            do_f = jnp.where(hm, do_blk, 0.0)
            d3 = _split3(-jnp.sum(do_f * o_blk, axis=1, keepdims=True))
            dob_ref[half] = jnp.where(hm, do_f, _bias_lanes((T, LANES), half, d3, [])).astype(BF)

    aug = pl.BlockSpec((None, 2, T, LANES), lambda p, i: (p, 0, i, 0))
    tile = pl.BlockSpec((T, LANES), lambda p, i: (i, p))
    shape = jax.ShapeDtypeStruct((B_HEADS // 2, 2, S, LANES), BF)
    return pl.pallas_call(
        kern, name="fox_prep_bwd", grid=(B_HEADS // 2, S // T),
        in_specs=[aug, tile, tile, tile],
        out_specs=[aug, aug], out_shape=[shape, shape],
        compiler_params=_cparams("parallel", "parallel"),
    )(q_aug, o, do, lse)


def _fox_bwd(qb_aug, k_aug, dob_aug, p_b, T, comm=None):
    n_pairs, _, S, _ = qb_aug.shape
    nq = S // T
    grid = (n_pairs,)

    def kern(*refs):
        own, comm_refs = _own_refs(refs, comm, 4, 5, 0)
        q_ref, k_ref, do_ref, v_ref, dq_ref, dk_ref, dv_ref, dck_ref, dcq_ref = own
        _comm_edge(comm, comm_refs, grid, first=True)
        p_id = pl.program_id(0)
        rowcol = lax.broadcasted_iota(jnp.int32, (T, T), 0) - lax.broadcasted_iota(jnp.int32, (T, T), 1)
        lane = lax.broadcasted_iota(jnp.int32, (T, LANES), 1)
        dk_ref[...] = jnp.zeros_like(dk_ref)
        dv_ref[...] = jnp.zeros_like(dv_ref)
        dck_ref[...] = jnp.zeros_like(dck_ref)

        @pl.when(p_id == 0)
        def _():
            dcq_ref[...] = jnp.zeros_like(dcq_ref)

        hms = (_half_mask((T, LANES), 0), _half_mask((T, LANES), 1))
        v_ones = [_bias_lanes((T, LANES), h, None, []).astype(BF) for h in (0, 1)]

        def outer(i, carry):
            qrows = pl.ds(pl.multiple_of(i * T, T), T)
            qa = (q_ref[0, qrows, :], q_ref[1, qrows, :])
            doa = (do_ref[0, qrows, :], do_ref[1, qrows, :])
            q_own = [jnp.where(hms[h], qa[h], 0) for h in (0, 1)]
            do_own = [jnp.where(hms[h], doa[h], 0) for h in (0, 1)]

            def inner(j, carry_in, masked):
                krows = pl.ds(pl.multiple_of(j * T, T), T)
                vj = v_ref[krows, :]
                dv_add, dk_add, new = 0.0, 0.0, []
                for half in (0, 1):
                    dq, rs = carry_in[half]
                    ka = k_ref[half, krows, :]
                    s = lax.dot_general(qa[half], ka, (((1,), (1,)), ((), ())), preferred_element_type=F32)
                    if masked:
                        s = jnp.where(rowcol >= 0, s, NEG_INF)
                    p = jnp.exp(s)
                    ds = p * lax.dot_general(doa[half], jnp.where(hms[half], vj, v_ones[half]),
                                             (((1,), (1,)), ((), ())), preferred_element_type=F32)
                    ds_b = ds.astype(BF)
                    dv_add = dv_add + lax.dot_general(p.astype(BF), do_own[half], (((0,), (0,)), ((), ())),
                                                      preferred_element_type=F32)
                    dk_add = dk_add + lax.dot_general(ds_b, q_own[half], (((0,), (0,)), ((), ())),
                                                      preferred_element_type=F32)
                    dck_ref[half:half + 1, krows] += jnp.sum(ds, axis=0, keepdims=True)
                    new.append((dq + jnp.dot(ds_b, jnp.where(hms[half], ka, 0), preferred_element_type=F32),
                                rs + jnp.sum(ds, axis=1, keepdims=True)))
                dv_ref[krows, :] += dv_add
                dk_ref[krows, :] += dk_add
                return tuple(new)

            one = (jnp.zeros((T, LANES), F32), jnp.zeros((T, 1), F32))
            carry_in = lax.fori_loop(0, i, functools.partial(inner, masked=False), (one, one))
            (dq0, rs0), (dq1, rs1) = inner(i, carry_in, True)
            dq_ref[qrows, :] = (dq0 + dq1) * QK_SCALE
            dcq_ref[qrows, :] = jnp.where(lane == 2 * p_id, rs0, jnp.where(lane == 2 * p_id + 1, rs1,
                                                                             dcq_ref[qrows, :]))
            return carry

        lax.fori_loop(0, nq, outer, 0)
        _comm_edge(comm, comm_refs, grid, first=False)

    aug = pl.BlockSpec((None, 2, S, LANES), lambda p: (p, 0, 0, 0))
    pair = pl.BlockSpec((S, LANES), lambda p: (0, p))
    wide = jax.ShapeDtypeStruct((S, n_pairs * LANES), F32)
    res = pl.pallas_call(
        kern, name="fox_bwd", grid=grid,
        in_specs=[aug, aug, aug, pl.BlockSpec((S, LANES), lambda p: (0, OFF_VB // LANES + p))]
        + _comm_specs(comm, "in"),
        out_specs=[pair, pair, pair, pl.BlockSpec((None, 2, S), lambda p: (p, 0, 0)),
                   pl.BlockSpec((S, LANES), lambda p: (0, 0))] + _comm_specs(comm, "out"),
        out_shape=[wide, wide, wide, jax.ShapeDtypeStruct((n_pairs, 2, S), F32),
                   jax.ShapeDtypeStruct((S, LANES), F32)] + (comm.out_shapes if comm else []),
        scratch_shapes=comm.sem_shapes if comm else [],
        compiler_params=_cparams("arbitrary"),
    )(qb_aug, k_aug, dob_aug, p_b, *(comm.ins if comm else []))
    return (*res[:5], res[5:])


SWA_TQ = 256
SWA_SUB = 4


def _swa_window(i, tq):
    start = pl.multiple_of(jnp.maximum(i * tq - WINDOW, 0), LANES)
    return start, i * tq - start


def _swa_valid(offset, tq):
    rel = offset + lax.broadcasted_iota(jnp.int32, (tq, tq + WINDOW), 0) \
        - lax.broadcasted_iota(jnp.int32, (tq, tq + WINDOW), 1)
    return (rel >= 0) & (rel < WINDOW)


def _swa_fwd(qk, v_arr, v_col, sinks):
    S = qk.shape[0]
    tq = min(SWA_TQ, S - WINDOW)
    sub = min(SWA_SUB, S // tq)
    win = tq + WINDOW

    def kern(q_ref, k_ref, v_ref, sink_ref, o_ref, lse_ref):
        p_id, i = pl.program_id(0), pl.program_id(1)
        hm0 = _half_mask((tq, LANES), 0)
        for t in range(sub):
            rows = slice(t * tq, (t + 1) * tq)
            start, offset = _swa_window(i * sub + t, tq)
            kw = k_ref[pl.ds(start, win), :]
            vw = v_ref[pl.ds(start, win), :].astype(BF)
            valid = _swa_valid(offset, tq)
            q = q_ref[rows, :]
            outs, lses = [], []
            for half in (0, 1):
                hm = _half_mask((tq, LANES), half)
                qh = (jnp.where(hm, q, 0).astype(F32) * QK_SCALE).astype(BF)
                s = lax.dot_general(qh, kw, (((1,), (1,)), ((), ())), preferred_element_type=F32)
                s = jnp.where(valid, s, NEG_INF)
                sink = sink_ref[2 * p_id + half]
                m = jnp.maximum(jnp.max(s, axis=1, keepdims=True), sink)
                p = jnp.exp(s - m)
                denom = jnp.sum(p, axis=1, keepdims=True) + jnp.exp(sink - m)
                outs.append(jnp.dot(p.astype(BF), vw, preferred_element_type=F32) / denom)
                lses.append(m + jnp.log(denom))
            o_ref[rows, :] = jnp.where(hm0, outs[0], outs[1])
            lse_ref[rows, :] = jnp.where(hm0, lses[0], lses[1])

    tile = pl.BlockSpec((sub * tq, LANES), lambda p, i: (i, p))
    return pl.pallas_call(
        kern, name="swa_fwd", grid=(A_Q_HEADS // 2, S // (sub * tq)),
        in_specs=[tile, pl.BlockSpec((S, LANES), lambda p, i: (0, A_Q_HEADS // 2)),
                  pl.BlockSpec((S, LANES), lambda p, i: (0, v_col)),
                  pl.BlockSpec(memory_space=pltpu.SMEM)],
        out_specs=[tile, tile],
        out_shape=[jax.ShapeDtypeStruct((S, A_Q_HEADS * HEAD_DIM), F32)] * 2,
        compiler_params=_cparams("parallel", "arbitrary"),
    )(qk, qk, v_arr, sinks)


def _swa_bwd(qk, v_arr, v_col, o_arr, do_arr, lse_arr, sinks, comm=None):
    S = qk.shape[0]
    tq = min(SWA_TQ, S - WINDOW)
    sub = min(SWA_SUB, S // tq)
    win = tq + WINDOW
    n_pairs = A_Q_HEADS // 2
    grid = (n_pairs, S // (sub * tq))

    def kern(*refs):
        own, comm_refs = _own_refs(refs, comm, 7, 4, 0)
        q_ref, k_ref, v_ref, o_ref, do_ref, lse_ref, sink_ref, dq_ref, dk_ref, dv_ref, dsink_ref = own
        _comm_edge(comm, comm_refs, grid, first=True)
        p_id, i = pl.program_id(0), pl.program_id(1)

        @pl.when((p_id == 0) & (i == 0))
        def _():
            dk_ref[...] = jnp.zeros_like(dk_ref)
            dv_ref[...] = jnp.zeros_like(dv_ref)

        @pl.when(i == 0)
        def _():
            dsink_ref[...] = jnp.zeros_like(dsink_ref)

        for t in range(sub):
            rows = slice(t * tq, (t + 1) * tq)
            start, offset = _swa_window(i * sub + t, tq)
            wrows = pl.ds(start, win)
            kw = k_ref[wrows, :]
            vw = v_ref[wrows, :].astype(BF)
            valid = _swa_valid(offset, tq)
            q, do, o, lse2 = q_ref[rows, :], do_ref[rows, :], o_ref[rows, :], lse_ref[rows, :]
            dq = jnp.zeros((tq, LANES), F32)
            dk = jnp.zeros((win, LANES), F32)
            dv = jnp.zeros((win, LANES), F32)
            for half in (0, 1):
                hm = _half_mask((tq, LANES), half)
                lane0 = half * HEAD_DIM
                qh = (jnp.where(hm, q, 0).astype(F32) * QK_SCALE).astype(BF)
                do_f = jnp.where(hm, do, 0.0)
                doh = do_f.astype(BF)
                delta = jnp.sum(do_f * o, axis=1, keepdims=True)
                lse = lse2[:, lane0:lane0 + 1]
                s = lax.dot_general(qh, kw, (((1,), (1,)), ((), ())), preferred_element_type=F32)
                p = jnp.exp(jnp.where(valid, s, NEG_INF) - lse)
                dp = lax.dot_general(doh, vw, (((1,), (1,)), ((), ())), preferred_element_type=F32)
                ds_b = (p * (dp - delta)).astype(BF)
                dv = dv + lax.dot_general(p.astype(BF), doh, (((0,), (0,)), ((), ())),
                                          preferred_element_type=F32)
                dk = dk + lax.dot_general(ds_b, qh, (((0,), (0,)), ((), ())), preferred_element_type=F32)
                kh = jnp.where(_half_mask((win, LANES), half), kw, 0)
                dq = dq + jnp.dot(ds_b, kh, preferred_element_type=F32)
                p_sink = jnp.exp(sink_ref[2 * p_id + half] - lse)
                dsink_ref[0, half:half + 1, :] += jnp.broadcast_to(
                    -jnp.sum(p_sink * delta, axis=0, keepdims=True), (1, LANES))
            dq_ref[rows, :] = dq * QK_SCALE
            dk_ref[wrows, :] += dk
            dv_ref[wrows, :] += dv
        _comm_edge(comm, comm_refs, grid, first=False)

    tile = pl.BlockSpec((sub * tq, LANES), lambda p, i: (i, p))
    whole = lambda col: pl.BlockSpec((S, LANES), lambda p, i: (0, col))
    res = pl.pallas_call(
        kern, name="swa_bwd", grid=grid,
        in_specs=[tile, whole(n_pairs), whole(v_col), tile, tile, tile,
                  pl.BlockSpec(memory_space=pltpu.SMEM)] + _comm_specs(comm, "in"),
        out_specs=[tile, whole(0), whole(0),
                   pl.BlockSpec((1, 8, LANES), lambda p, i: (p, 0, 0))] + _comm_specs(comm, "out"),
        out_shape=[jax.ShapeDtypeStruct((S, A_Q_HEADS * HEAD_DIM), F32),
                   jax.ShapeDtypeStruct((S, LANES), F32), jax.ShapeDtypeStruct((S, LANES), F32),
                   jax.ShapeDtypeStruct((n_pairs, 8, LANES), F32)] + (comm.out_shapes if comm else []),
        scratch_shapes=comm.sem_shapes if comm else [],
        compiler_params=_cparams("arbitrary", "arbitrary"),
    )(qk, qk, v_arr, o_arr, do_arr, lse_arr, sinks, *(comm.ins if comm else []))
    return (*res[:4], res[4:])


ADAMW_BLOCK = 512 * 1024


def _adamw(w, g, m, v, name):
    R, C = w.shape
    tr, tc = _tile(R, max(8, ADAMW_BLOCK // C), 8), C

    def kern(w_ref, g_ref, m_ref, v_ref, d_ref, mo_ref, vo_ref):
        g_ = g_ref[...]
        m_new = ADAM_B1 * m_ref[...] + (1.0 - ADAM_B1) * g_
        v_new = ADAM_B2 * v_ref[...] + (1.0 - ADAM_B2) * (g_ * g_)
        m_hat = m_new / (1.0 - ADAM_B1 ** ADAM_STEP)
        v_hat = v_new / (1.0 - ADAM_B2 ** ADAM_STEP)
        d_ref[...] = -ADAM_LR * (m_hat / (jnp.sqrt(v_hat) + ADAM_EPS) + ADAM_WD * w_ref[...])
        mo_ref[...] = m_new
        vo_ref[...] = v_new

    spec = pl.BlockSpec((tr, tc), lambda i, j: (i, j))
    shape = jax.ShapeDtypeStruct((R, C), F32)
    return pl.pallas_call(
        kern, name=name, grid=(R // tr, C // tc),
        in_specs=[spec] * 4, out_specs=[spec] * 3, out_shape=[shape] * 3,
        compiler_params=_cparams("parallel", "parallel"),
    )(w, g, m, v)


def _index_operand(i):
    return jnp.reshape(i, (1,)).astype(jnp.int32)


def _add_pair(whole, got, ci, name):
    P, R, C = whole.shape
    half = R // 2
    tr = _tile(half, ROWS, 16)
    nb = half // tr

    def kern(ci_ref, a_ref, b_ref, o_ref, ob_ref):
        s = a_ref[...] + b_ref[...].astype(F32)
        o_ref[...] = s
        ob_ref[...] = s.astype(BF)

    spec = pl.BlockSpec((None, tr, C), lambda p, i, ci_ref: (p, i, 0))
    return pl.pallas_call(
        kern, name=name,
        grid_spec=pltpu.PrefetchScalarGridSpec(
            num_scalar_prefetch=1, grid=(P, nb),
            in_specs=[pl.BlockSpec((None, tr, C), lambda p, i, ci_ref: (p, ci_ref[0] * nb + i, 0)), spec],
            out_specs=[spec, spec]),
        out_shape=[jax.ShapeDtypeStruct((P, half, C), F32), jax.ShapeDtypeStruct((P, half, C), BF)],
        compiler_params=_cparams("parallel", "parallel"),
    )(_index_operand(ci), whole, got)


def _add_three(parts, recv, chip, name):
    _, R, C = parts.shape
    tr = _tile(R, ROWS, 16)

    def kern(chip_ref, o_ref, r0_ref, r1_ref, r2_ref, out_ref):
        s = ((o_ref[...] + r0_ref[...].astype(F32)) + r1_ref[...].astype(F32)) + r2_ref[...].astype(F32)
        out_ref[0] = s
        out_ref[1] = s

    slab = lambda k: pl.BlockSpec((None, tr, C), lambda i, chip_ref: (k, i, 0))
    return pl.pallas_call(
        kern, name=name,
        grid_spec=pltpu.PrefetchScalarGridSpec(
            num_scalar_prefetch=1, grid=(R // tr,),
            in_specs=[pl.BlockSpec((None, tr, C), lambda i, chip_ref: (chip_ref[0], i, 0)),
                      slab(0), slab(1), slab(2)],
            out_specs=pl.BlockSpec((2, tr, C), lambda i, chip_ref: (0, i, 0))),
        out_shape=jax.ShapeDtypeStruct((2, R, C), F32),
        compiler_params=_cparams("parallel"),
    )(_index_operand(chip), parts, recv, recv, recv)


SM_ADA, SM_G, SM_LOSS, SM_BF, SM_SINK, SM_LEN = 0, 6144, 10240, 11264, 11272, 12288


def _small_finalize(gathered):
    def kern(g_ref, tot_ref, loss_ref):
        tot = g_ref[0:1, :]
        for b in range(1, N_DEV):
            tot = tot + g_ref[b:b + 1, :]
        tot_ref[...] = tot
        sq = jnp.sum(tot[:, SM_LOSS:SM_LOSS + D_MODEL], axis=1, keepdims=True)
        loss_ref[...] = jnp.broadcast_to(sq * (0.5 / D_MODEL), (1, LANES))

    full = lambda shape: pl.BlockSpec(shape, lambda i: (0, 0))
    return pl.pallas_call(
        kern, name="small_finalize", grid=(1,),
        in_specs=[full((N_DEV, SM_LEN))],
        out_specs=[full((1, SM_LEN)), full((1, LANES))],
        out_shape=[jax.ShapeDtypeStruct((1, SM_LEN), F32), jax.ShapeDtypeStruct((1, LANES), F32)],
        compiler_params=_cparams("arbitrary"),
    )(gathered)


def _ada_dw(c_t, d_ada):
    N = d_ada.shape[1]
    tn = _tile(N, 512)

    def kern(c_ref, d_ref, o_ref):
        acc = c_ref[:, 0:1] * d_ref[0:1, :]
        for b in range(1, N_DEV):
            acc = acc + c_ref[:, b:b + 1] * d_ref[b:b + 1, :]
        o_ref[...] = acc

    return pl.pallas_call(
        kern, name="ada_dw", grid=(N // tn,),
        in_specs=[pl.BlockSpec((D_MODEL, N_DEV), lambda j: (0, 0)), pl.BlockSpec((N_DEV, tn), lambda j: (0, j))],
        out_specs=pl.BlockSpec((D_MODEL, tn), lambda j: (0, j)),
        out_shape=jax.ShapeDtypeStruct((D_MODEL, N), F32),
        compiler_params=_cparams("parallel"),
    )(c_t, d_ada)


def _here():
    return lax.axis_index("x"), lax.axis_index("y"), lax.axis_index("c")


def _other_chips(x, y):
    return [(1 - x, y), (x, 1 - y), (1 - x, 1 - y)]


_ANY = pl.BlockSpec(memory_space=pl.ANY)


class _Comm:
    def __init__(self, ins, out_shapes, sem_shapes, start, finish):
        self.ins, self.out_shapes, self.sem_shapes = list(ins), list(out_shapes), list(sem_shapes)
        self.start, self.finish = start, finish

    def split(self, refs, n_in, n_out, n_scratch):
        a = n_in + len(self.ins)
        b = a + n_out + len(self.out_shapes)
        own = list(refs[:n_in]) + list(refs[a:a + n_out]) + list(refs[b:b + n_scratch])
        mine = (refs[n_in:a], refs[a + n_out:b], refs[b + n_scratch:])
        return own, mine


def _run_comm(comm, name):
    n_in, n_out = len(comm.ins), len(comm.out_shapes)

    def body(*refs):
        parts = (refs[:n_in], refs[n_in:n_in + n_out], refs[n_in + n_out:])
        comm.start(*parts)
        comm.finish(*parts)

    return pl.pallas_call(
        body, name=name,
        in_specs=[_ANY] * n_in, out_specs=[_ANY] * n_out,
        out_shape=comm.out_shapes, scratch_shapes=comm.sem_shapes,
    )(*comm.ins)


def _gather_comm(blocks):
    L = len(blocks)

    def parts(ins, outs, sems):
        send_sems, recv_sems, local_sems = sems
        x, y, c = _here()
        me, sibling = (x, y, c), (x, y, 1 - c)
        chips = _other_chips(x, y)

        def slot(px, py, pc):
            return 4 * px + 2 * py + pc

        def copy(l, k, block, to, src=None):
            dst = outs[l].at[slot(*block)]
            return pltpu.make_async_remote_copy(
                src_ref=dst if src is None else src, dst_ref=dst,
                send_sem=send_sems.at[l, k], recv_sem=recv_sems.at[l, k],
                device_id=to, device_id_type=MESH)

        mine = [pltpu.make_async_copy(ins[l], outs[l].at[slot(*me)], local_sems.at[l]) for l in range(L)]
        first = []
        for l in range(L):
            first.append(copy(l, 0, me, sibling, src=ins[l]))
            for j, chip in enumerate(chips):
                first.append(copy(l, 1 + j, me, (*chip, c), src=ins[l]))
        return c, me, sibling, chips, copy, mine, first

    def start(ins, outs, sems):
        *_, mine, first = parts(ins, outs, sems)
        for cp in mine + first:
            cp.start()

    def finish(ins, outs, sems):
        c, me, sibling, chips, copy, mine, first = parts(ins, outs, sems)
        passed = []
        for j, chip in enumerate(chips):
            for l in range(L):
                copy(l, 1 + j, (*chip, c), me).wait_recv()
                fwd = copy(l, 4 + j, (*chip, c), sibling)
                fwd.start()
                passed.append(fwd)
        for l in range(L):
            copy(l, 0, sibling, me).wait_recv()
        for j, chip in enumerate(chips):
            for l in range(L):
                copy(l, 4 + j, (*chip, 1 - c), me).wait_recv()
        for cp in first + passed:
            cp.wait_send()
        for cp in mine:
            cp.wait()

    return _Comm(blocks, [jax.ShapeDtypeStruct((N_DEV,) + b.shape, b.dtype) for b in blocks],
                 [pltpu.SemaphoreType.DMA((L, 7)), pltpu.SemaphoreType.DMA((L, 7)), pltpu.SemaphoreType.DMA((L,))],
                 start, finish)


def _allgather8(blocks, name):
    return _run_comm(_gather_comm(blocks), name)


def _swap_comm(arrs):
    L = len(arrs)

    def copies(ins, outs, sems):
        send_sems, recv_sems = sems
        x, y, c = _here()
        cps = []
        for l in range(L):
            half = arrs[l].shape[1] // 2
            rows = pl.ds(pl.multiple_of((1 - c) * half, 16), half)
            cps.append(pltpu.make_async_remote_copy(
                src_ref=ins[l].at[:, rows, :], dst_ref=outs[l], send_sem=send_sems.at[l],
                recv_sem=recv_sems.at[l], device_id=(x, y, 1 - c), device_id_type=MESH))
        return cps

    def start(ins, outs, sems):
        for cp in copies(ins, outs, sems):
            cp.start()

    def finish(ins, outs, sems):
        for cp in copies(ins, outs, sems):
            cp.wait()

    return _Comm(arrs, [jax.ShapeDtypeStruct((a.shape[0], a.shape[1] // 2, a.shape[2]), a.dtype) for a in arrs],
                 [pltpu.SemaphoreType.DMA((L,)), pltpu.SemaphoreType.DMA((L,))], start, finish)


def _sibling_join(bufs, name):
    L = len(bufs)

    def body(*refs):
        outs = refs[L:2 * L]
        send_sems, recv_sems = refs[2 * L:]
        x, y, c = _here()
        for l in range(L):
            pltpu.make_async_remote_copy(src_ref=outs[l].at[c], dst_ref=outs[l].at[c], send_sem=send_sems.at[l],
                                         recv_sem=recv_sems.at[l], device_id=(x, y, 1 - c),
                                         device_id_type=MESH).start()
        for l in range(L):
            pltpu.make_async_remote_copy(src_ref=outs[l].at[c], dst_ref=outs[l].at[1 - c],
                                         send_sem=send_sems.at[l], recv_sem=recv_sems.at[l],
                                         device_id=(x, y, 1 - c), device_id_type=MESH).wait()

    return pl.pallas_call(
        body, name=name,
        in_specs=[_ANY] * L, out_specs=[_ANY] * L,
        out_shape=[jax.ShapeDtypeStruct(a.shape, a.dtype) for a in bufs],
        input_output_aliases={l: l for l in range(L)},
        scratch_shapes=[pltpu.SemaphoreType.DMA((L,)), pltpu.SemaphoreType.DMA((L,))],
    )(*bufs)


def _scatter_comm(arrs):
    L = len(arrs)

    def copies(ins, outs, sems):
        send_sems, recv_sems = sems
        x, y, c = _here()
        return [pltpu.make_async_remote_copy(
            src_ref=ins[l].at[2 * tx + ty], dst_ref=outs[l].at[j],
            send_sem=send_sems.at[l, j], recv_sem=recv_sems.at[l, j],
            device_id=(tx, ty, c), device_id_type=MESH)
            for l in range(L) for j, (tx, ty) in enumerate(_other_chips(x, y))]

    def start(ins, outs, sems):
        for cp in copies(ins, outs, sems):
            cp.start()

    def finish(ins, outs, sems):
        for cp in copies(ins, outs, sems):
            cp.wait()

    return _Comm(arrs, [jax.ShapeDtypeStruct((3,) + a.shape[1:], a.dtype) for a in arrs],
                 [pltpu.SemaphoreType.DMA((L, 3)), pltpu.SemaphoreType.DMA((L, 3))], start, finish)


_A_ORDER = np.array(A_HEAD_ORDER)
_A_INVERSE = np.argsort(_A_ORDER)


def _permute_in_weights(w_in):
    qa = w_in[:, 0:512].reshape(D_MODEL, A_Q_HEADS, HEAD_DIM)[:, _A_ORDER, :].reshape(D_MODEL, 512)
    f_pad = jnp.pad(w_in[:, 2304:2312], ((0, 0), (0, LANES - B_HEADS)))
    w_a = jnp.concatenate([qa, w_in[:, 512:640], f_pad], axis=1)
    return w_a, w_in[:, 640:2304], w_in[:, 2312:4360]


def _slab_segments():
    segs = [(h * HEAD_DIM, int(_A_INVERSE[h]) * HEAD_DIM, HEAD_DIM) for h in range(A_Q_HEADS)]
    segs += [(512, OFF_KA, 128), (640, W_A + OFF_VA, 128), (768, W_A + OFF_QB, 1536),
             (2304, OFF_F, B_HEADS), (2312, W_A + W_B, W_G)]
    return segs


def _shard_slabs(dw_perm):
    R = dw_perm.shape[0]
    tr = _tile(R, 128, 8)
    plan = []
    for k in range(N_CHIP):
        for b in range(W_SHARD_PAD // LANES):
            lo, hi = k * W_SHARD + b * LANES, min(k * W_SHARD + (b + 1) * LANES, (k + 1) * W_SHARD)
            parts = []
            for o0, s0, n in _slab_segments():
                a, z = max(lo, o0), min(hi, o0 + n)
                while a < z:
                    s = s0 + (a - o0)
                    run = min(z - a, LANES - s % LANES)
                    parts.append((s // LANES, ((a - lo) - s % LANES) % LANES, a - lo, run))
                    a += run
            plan.append((k, b, parts))

    def kern(x_ref, o32_ref, obf_ref):
        lane = lax.broadcasted_iota(jnp.int32, (tr, LANES), 1)
        for k, b, parts in plan:
            acc = jnp.zeros((tr, LANES), F32)
            for src, rot, first, run in parts:
                blk = x_ref[:, src * LANES:(src + 1) * LANES]
                if rot:
                    blk = pltpu.roll(blk, rot, 1)
                acc = jnp.where((lane >= first) & (lane < first + run), blk, acc)
            o32_ref[k, :, b * LANES:(b + 1) * LANES] = acc
            obf_ref[k, :, b * LANES:(b + 1) * LANES] = acc.astype(BF)

    out_spec = pl.BlockSpec((N_CHIP, tr, W_SHARD_PAD), lambda i: (0, i, 0))
    return tuple(pl.pallas_call(
        kern, name="shard_slabs", grid=(R // tr,),
        in_specs=[pl.BlockSpec((tr, W_PERM), lambda i: (i, 0))],
        out_specs=[out_spec, out_spec],
        out_shape=[jax.ShapeDtypeStruct((N_CHIP, R, W_SHARD_PAD), F32),
                   jax.ShapeDtypeStruct((N_CHIP, R, W_SHARD_PAD), BF)],
        compiler_params=_cparams("parallel"),
    )(dw_perm))


class _NoExchange:
    def __init__(self, w_in, rest):
        self.w_in_whole, self.rest, self.grads = w_in, rest, {}

    def w_in_comm(self):
        return None

    def w_in(self, outs):
        return self.w_in_whole

    def rest_weights_comm(self):
        return None

    def rest_weights(self, outs):
        return self.rest

    def swap_comm(self, pieces, tag):
        self.grads[tag] = [p32 for p32, _ in pieces]
        return None

    def swap_done(self, outs, tag):
        return None

    def reduce_done(self, outs, tag):
        pass


class _Exchange:
    def __init__(self, ci, chip, w_in_shard, rest_shards):
        self.ci, self.chip, self.w_in_shard, self.rest_shards = ci, chip, w_in_shard, rest_shards
        self.pieces, self.part_f32, self.halves = {}, {}, {}

    def _my_half(self, a, axis=0, other=False):
        rows = a.shape[axis] // 2
        return lax.dynamic_slice_in_dim(a, ((1 - self.ci) if other else self.ci) * rows, rows, axis=axis)

    def w_in_comm(self):
        return _gather_comm([self._my_half(self.w_in_shard).astype(BF)])

    def w_in(self, outs):
        return _col_sharded(outs[0])

    def rest_weights_comm(self):
        return _gather_comm([self._my_half(w).astype(BF) for w in self.rest_shards])

    def rest_weights(self, outs):
        w_ba, w_bb, w_out, w_fi, w_fo = outs
        return (_col_sharded(w_ba), _col_sharded(w_bb), _row_sharded(w_out), _col_sharded(w_fi),
                _row_sharded(w_fo))

    def swap_comm(self, pieces, tag):
        self.pieces[tag] = pieces
        return _swap_comm([pbf for _, pbf in pieces])

    def swap_done(self, got, tag):
        self.part_f32[tag], part_bf = [], []
        for l, ((p32, _), g_) in enumerate(zip(self.pieces[tag], got)):
            s32, sbf = _add_pair(p32, g_, self.ci, f"chip_sum_{tag}_{l}")
            self.part_f32[tag].append(s32)
            part_bf.append(sbf)
        return _scatter_comm(part_bf)

    def reduce_done(self, outs, tag):
        self.halves[tag] = [_add_three(p32, r, self.chip, f"shard_sum_{tag}_{l}")
                            for l, (p32, r) in enumerate(zip(self.part_f32[tag], outs))]


def _col_sharded(g):
    return jnp.transpose(g.reshape(N_CHIP, -1, g.shape[-1]), (1, 0, 2)).reshape(2 * g.shape[1], N_CHIP * g.shape[-1])


def _row_sharded(g):
    return g.reshape(N_DEV * g.shape[1], g.shape[-1])


def _rope_tables(pos):
    inv_freq = 1.0 / (ROPE_THETA ** (jnp.arange(0, HEAD_DIM, 2, dtype=F32) / HEAD_DIM))
    ang = pos.astype(F32)[:, None] * inv_freq
    cos, sin = jnp.cos(ang), jnp.sin(ang)
    return jnp.tile(cos, (1, 4)), jnp.tile(jnp.concatenate([-sin, sin], axis=1), (1, 2))


def _local_step(x, pos, ada, g1, g2, g3, g4, b_f, sinks, exch, target):
    S = x.shape[0]
    t_fox = _tile(S, 512, LANES) if S >= 1024 else S // 2
    t_fox_fwd = _tile(S, 1024, LANES) if S >= 2048 else S // 2
    shift_m, scale_m, gate_m, shift_f, scale_f, gate_f = [ada[i:i + 1] for i in range(N_ADA)]
    cos_t, sin_t = _rope_tables(pos)
    sinks_p = sinks.reshape(A_KV_HEADS, 4).T.reshape(A_Q_HEADS)
    b_f_pad = jnp.pad(b_f, (0, LANES - B_HEADS)).reshape(1, LANES)

    h1, outs = _pre_norm(x, g1, scale_m, shift_m, "pre_mix_norm", comm=exch.w_in_comm())
    w_a, w_b, w_g = _permute_in_weights(exch.w_in(outs))
    w_perm = jnp.concatenate([w_a, w_b, w_g], axis=1)
    p_a = _mm(h1, w_a, "nn", F32, "proj_a")
    p_b = _mm(h1, w_b, "nn", BF, "proj_b")
    p_g = _mm(h1, w_g, "nn", BF, "proj_g")
    (qk_a,) = _rope([p_a], [640], cos_t, sin_t, "rope_fwd")
    o_a, lse_a = _swa_fwd(qk_a, p_b, 0, sinks_p)
    q_aug, k_aug = _fox_prep_fwd(p_b, _fox_gate_fwd(p_a, b_f_pad), t_fox)
    comm = exch.rest_weights_comm()
    o_b, lse_b, outs = _fox_fwd(q_aug, k_aug, p_b, t_fox_fwd, comm=comm)
    w_ba, w_bb, w_out, w_fi, w_fo = exch.rest_weights(outs)
    w_ba_p = w_ba.reshape(A_Q_HEADS, HEAD_DIM, D_MODEL)[_A_ORDER].reshape(512, D_MODEL)
    pa = _mm(o_a, w_ba_p, "nn", BF, "branch_a")
    pb = _mm(o_b, w_bb, "nn", BF, "branch_b")
    merged = _merge_fwd(p_g, pa, pb)
    y1 = _mm(merged, w_out, "nn", BF, "out_proj")
    x2, h2 = _post_pre(x, y1, g2, gate_m, g3, scale_f, shift_f)
    gu = _mm(h2, w_fi, "nn", BF, "ffn_in")
    act = _swiglu_fwd(gu)
    y2 = _mm(act, w_fo, "nn", BF, "ffn_out")
    d_out, d_y2, st_f = _final(x2, y2, g4, gate_f, target)

    d_act = _mm(d_y2, w_fo, "nt", BF, "ffn_out_dx")
    row_pieces = lambda pair: tuple(t.reshape(N_CHIP, t.shape[0] // N_CHIP, t.shape[1]) for t in pair)
    dw_fo = row_pieces(_mm(act, d_y2, "tn", F32, "ffn_out_dw", twin=True))
    d_gu = _swiglu_bwd(d_act, gu)
    d_h2 = _mm(d_gu, w_fi, "nt", BF, "ffn_in_dx")
    dw_fi = _mm(h2, d_gu, "tn", F32, "ffn_in_dw", col_pieces=N_CHIP, twin=True)
    d_x2, d_y1, st_m = _mid_bwd(d_h2, x2, d_out, y1, g3, scale_f, g2, gate_m)
    d_merged = _mm(d_y1, w_out, "nt", BF, "out_proj_dx")
    dw_out = row_pieces(_mm(merged, d_y1, "tn", F32, "out_proj_dw", twin=True))
    d_pa, d_pb, d_ga, d_gb = _merge_bwd(d_merged, p_g, pa, pb)
    d_oa = _mm(d_pa, w_ba_p, "nt", F32, "branch_a_dx")
    dw_ba_p = _mm(o_a, d_pa, "tn", F32, "branch_a_dw", col_pieces=N_CHIP, twin=True)
    d_ob = _mm(d_pb, w_bb, "nt", F32, "branch_b_dx")
    dw_bb = _mm(o_b, d_pb, "tn", F32, "branch_b_dw", col_pieces=N_CHIP, twin=True)
    head_rows = lambda t: t.reshape(N_CHIP, A_Q_HEADS, HEAD_DIM, -1)[:, _A_INVERSE].reshape(t.shape)
    dw_ba = tuple(head_rows(t) for t in dw_ba_p)
    comm = exch.swap_comm([dw_ba, dw_bb, dw_out, dw_fi, dw_fo], "early")
    dq_a, dk_a, dv_a, d_sink, outs = _swa_bwd(qk_a, p_b, 0, o_a, d_oa, lse_a, sinks_p, comm=comm)
    comm = exch.swap_done(outs, "early")
    qb_aug, dob_aug = _fox_prep_bwd(q_aug, o_b, d_ob, lse_b, t_fox)
    dq_b, dk_b, dv_b, d_ck, d_cq, outs = _fox_bwd(qb_aug, k_aug, dob_aug, p_b, t_fox, comm=comm)
    exch.reduce_done(outs, "early")
    d_qa, d_ka = _rope([dq_a, dk_a], [512, LANES], cos_t, -sin_t, "rope_bwd")
    d_ck_cols = jnp.pad(d_ck.reshape(B_HEADS, S).T, ((0, 0), (0, LANES - B_HEADS)))
    d_f, d_bf = _fox_gate_bwd(d_cq, d_ck_cols, p_a, b_f_pad)
    d_proj = jnp.concatenate([d_qa, d_ka, d_f, dv_a.astype(BF), dq_b.astype(BF), dk_b.astype(BF),
                              dv_b.astype(BF), d_ga, d_gb], axis=1)
    dw_perm = _mm(h1, d_proj, "tn", F32, "proj_dw")
    swap = exch.swap_comm([_shard_slabs(dw_perm)], "late")
    comm = exch.swap_done(_run_comm(swap, "grads_to_sibling_late") if swap else None, "late")
    res = _mm(d_proj, w_perm, "nt", BF, "proj_dx", comm=comm)
    d_h1 = res[0] if comm else res
    exch.reduce_done(res[1] if comm else None, "late")
    grad_x, st_p = _pre_bwd(d_h1, x, d_x2, g1, scale_m)

    d_sinks = d_sink[:, :2, 0].T.reshape(A_Q_HEADS)
    small = jnp.concatenate([
        st_p[0], st_p[1], st_m[3], st_m[0], st_m[1], st_f[0],
        st_p[2], st_m[4], st_m[2], st_f[1],
        st_f[2], d_bf[0, :B_HEADS], d_sinks,
        jnp.zeros((SM_LEN - SM_SINK - A_Q_HEADS,), F32)])
    return grad_x, small


def kernel(x, c, positions, w_ada, b_ada, g_pre_mix, g_post_mix, w_in, b_f, sinks, w_branch_a, w_branch_b, w_out, g_pre_ffn, g_post_ffn, w_ffn_in, w_ffn_out, loss_target, m_w_ada, m_b_ada, m_g_pre_mix, m_g_post_mix, m_w_in, m_b_f, m_sinks, m_w_branch_a, m_w_branch_b, m_w_out, m_g_pre_ffn, m_g_post_ffn, m_w_ffn_in, m_w_ffn_out, v_w_ada, v_b_ada, v_g_pre_mix, v_g_post_mix, v_w_in, v_b_f, v_sinks, v_w_branch_a, v_w_branch_b, v_w_out, v_g_pre_ffn, v_g_post_ffn, v_w_ffn_in, v_w_ffn_out):
    xi, yi, ci = _here()
    chip = 2 * xi + yi
    dev = 2 * chip + ci

    (c_g,) = _allgather8([c.reshape(8, LANES)], "gather_c")
    c_all = c_g.reshape(N_DEV, D_MODEL)
    exch = _Exchange(ci, chip, w_in[0], [w_branch_a[0], w_branch_b[0], w_out[0], w_ffn_in[0], w_ffn_out[0]])

    ada_cols = _mm(c_all, w_ada[0], "nn", F32, "ada_fwd")
    (ada_g,) = _allgather8([ada_cols], "gather_ada")
    ada_mine = lax.dynamic_index_in_dim(ada_g.reshape(N_CHIP, 2, N_DEV, -1)[:, 0], dev, axis=1, keepdims=False)
    ada = (ada_mine.reshape(-1) + b_ada[0]).reshape(N_ADA, D_MODEL)

    grad_x, small = _local_step(
        x[0], positions[0], ada, g_pre_mix, g_post_mix, g_pre_ffn, g_post_ffn, b_f[0], sinks[0],
        exch, loss_target[0])

    (small_g,) = _allgather8([small.reshape(8, SM_LEN // 8)], "gather_small")
    small_all = small_g.reshape(N_DEV, SM_LEN)
    small_tot, loss_row = _small_finalize(small_all)
    loss = loss_row[0, 0]
    d_ada_cols = lax.dynamic_slice_in_dim(small_all[:, :N_ADA * D_MODEL], chip * (N_ADA * D_MODEL // N_CHIP),
                                          N_ADA * D_MODEL // N_CHIP, axis=1)
    g_w_ada = _ada_dw(c_all.T, d_ada_cols)

    joined = _sibling_join(exch.halves["late"] + exch.halves["early"], "grads_join")
    g_w_in, g_w_ba, g_w_bb, g_w_out, g_w_fi, g_w_fo = [j.reshape(2 * j.shape[1], j.shape[2]) for j in joined]

    def small_vec(b_ada_, g1_, g2_, g3_, g4_, b_f_, sinks_):
        return jnp.concatenate([b_ada_[0], g1_[0], g2_[0], g3_[0], g4_[0], jnp.zeros((D_MODEL,), F32),
                                b_f_[0], sinks_[0], jnp.zeros((SM_LEN - SM_SINK - A_Q_HEADS,), F32)]
                               ).reshape(8, SM_LEN // 8)

    sw = small_vec(b_ada, g_pre_mix, g_post_mix, g_pre_ffn, g_post_ffn, b_f, sinks)
    sm = small_vec(m_b_ada, m_g_pre_mix, m_g_post_mix, m_g_pre_ffn, m_g_post_ffn, m_b_f, m_sinks)
    sv = small_vec(v_b_ada, v_g_pre_mix, v_g_post_mix, v_g_pre_ffn, v_g_post_ffn, v_b_f, v_sinks)
    s_upd = [u.reshape(SM_LEN) for u in _adamw(sw, small_tot.reshape(8, SM_LEN // 8), sm, sv, "adamw_small")]
    s_grad = small_tot.reshape(SM_LEN)

    def unpack(vec):
        row = lambda a, n: vec[a:a + n].reshape(1, n)
        return dict(b_ada=row(SM_ADA, N_ADA * D_MODEL), g_pre_mix=row(SM_G, D_MODEL),
                    g_post_mix=row(SM_G + D_MODEL, D_MODEL), g_pre_ffn=row(SM_G + 2 * D_MODEL, D_MODEL),
                    g_post_ffn=row(SM_G + 3 * D_MODEL, D_MODEL), b_f=row(SM_BF, B_HEADS),
                    sinks=row(SM_SINK, A_Q_HEADS))

    big = dict(
        w_ada=(w_ada, g_w_ada, m_w_ada, v_w_ada),
        w_branch_a=(w_branch_a, g_w_ba, m_w_branch_a, v_w_branch_a),
        w_branch_b=(w_branch_b, g_w_bb, m_w_branch_b, v_w_branch_b),
        w_out=(w_out, g_w_out, m_w_out, v_w_out), w_ffn_in=(w_ffn_in, g_w_fi, m_w_ffn_in, v_w_ffn_in),
        w_ffn_out=(w_ffn_out, g_w_fo, m_w_ffn_out, v_w_ffn_out))
    grads, deltas, new_m, new_v = unpack(s_grad), unpack(s_upd[0]), unpack(s_upd[1]), unpack(s_upd[2])
    for n, (w_, g_, m_, v_) in big.items():
        d_, nm_, nv_ = _adamw(w_[0], g_, m_[0], v_[0], "adamw_" + n)
        grads[n], deltas[n], new_m[n], new_v[n] = g_[None], d_[None], nm_[None], nv_[None]
    pad_cols = lambda a: jnp.pad(a, ((0, 0), (0, W_SHARD_PAD - W_SHARD)))
    upd = _adamw(pad_cols(w_in[0]), g_w_in, pad_cols(m_w_in[0]), pad_cols(v_w_in[0]), "adamw_w_in")
    grads["w_in"], deltas["w_in"], new_m["w_in"], new_v["w_in"] = [t[None, :, :W_SHARD] for t in (g_w_in, *upd)]

    names = ["w_ada", "b_ada", "g_pre_mix", "g_post_mix", "w_in", "b_f", "sinks", "w_branch_a", "w_branch_b",
             "w_out", "g_pre_ffn", "g_post_ffn", "w_ffn_in", "w_ffn_out"]
    return (loss, grad_x[None], *[grads[n] for n in names], *[deltas[n] for n in names],
            *[new_m[n] for n in names], *[new_v[n] for n in names])
```

```python
import functools
import math

import numpy as np
import jax
import jax.numpy as jnp
from jax import lax
from jax.experimental import pallas as pl
from jax.experimental.pallas import tpu as pltpu

F32 = jnp.float32
BF = jnp.bfloat16

D_MODEL = 1024
HEAD_DIM = 64
LANES = 128
WINDOW = 128
A_Q_HEADS = 8
A_KV_HEADS = 2
B_HEADS = 8
D_FF = 2816
ROPE_THETA = 10000.0
RMS_EPS = 1e-6
N_ADA = 6
N_DEV = 8
N_CHIP = 4

ADAM_LR = 0.001
ADAM_B1 = 0.9
ADAM_B2 = 0.999
ADAM_EPS = 1e-08
ADAM_WD = 0.01
ADAM_STEP = 10

VMEM_LIMIT = 48 * 1024 * 1024
MESH = pl.DeviceIdType.MESH

A_HEAD_ORDER = (0, 4, 1, 5, 2, 6, 3, 7)

OFF_QA, OFF_KA, OFF_F = 0, 512, 640
W_A = 768
OFF_VA, OFF_QB, OFF_KB, OFF_VB = 0, 128, 640, 1152
W_B = 1664
W_G = 2048
W_PERM = W_A + W_B + W_G
W_SHARD = 1090
W_SHARD_PAD = 1152


def _tile(n, cap, mult=LANES):
    if n <= cap:
        return n
    t = (cap // mult) * mult
    while t >= mult:
        if n % t == 0:
            return t
        t -= mult
    raise ValueError(f"no tile for {n}")


MXU_WIDTH = 256
MM_OPERAND_BYTES = 28 * 1024 * 1024


def _mm_tiles(M, N, K, a_bytes, b_bytes, tm_cap, tn_cap):
    tm = _tile(M, tm_cap)
    try:
        tn = _tile(N, tn_cap, MXU_WIDTH)
    except ValueError:
        tn = _tile(N, tn_cap)
    fits = lambda tk: 2 * tk * (tm * a_bytes + tn * b_bytes) <= MM_OPERAND_BYTES
    tk = K if fits(K) else next(t for t in range(K // LANES * LANES, 0, -LANES) if K % t == 0 and fits(t))
    return tm, tn, tk


def _cparams(*sem):
    return pltpu.CompilerParams(dimension_semantics=sem, vmem_limit_bytes=VMEM_LIMIT)


def _own_refs(refs, comm, n_in, n_out, n_scratch):
    if comm is None:
        return list(refs), None
    return comm.split(refs, n_in, n_out, n_scratch)


def _comm_specs(comm, side):
    if comm is None:
        return []
    return [pl.BlockSpec(memory_space=pl.ANY)] * len(comm.ins if side == "in" else comm.out_shapes)


def _comm_edge(comm, comm_refs, grid, first):
    if comm is None:
        return
    at_edge = None
    for axis, n in enumerate(grid):
        here = pl.program_id(axis) == (0 if first else n - 1)
        at_edge = here if at_edge is None else at_edge & here
    pl.when(at_edge)(lambda: (comm.start if first else comm.finish)(*comm_refs))


def _mm(a, b, mode, out_dtype, name, tm_cap=512, tn_cap=2816, comm=None, col_pieces=1, twin=False):
    if mode == "nn":
        (M, K), (K2, N) = a.shape, b.shape
        dims = (((1,), (0,)), ((), ()))
    elif mode == "nt":
        (M, K), (N, K2) = a.shape, b.shape
        dims = (((1,), (1,)), ((), ()))
    else:
        (K, M), (K2, N) = a.shape, b.shape
        dims = (((0,), (0,)), ((), ()))
    assert K == K2, (a.shape, b.shape, mode)
    tm, tn, tk = _mm_tiles(M, N // col_pieces, K, a.dtype.itemsize, b.dtype.itemsize, tm_cap, tn_cap)
    nk = K // tk
    n_out = 2 if twin else 1
    n_scratch = 1 if nk > 1 else 0
    if mode == "nn":
        a_spec = pl.BlockSpec((tm, tk), lambda i, j, k: (i, k))
        b_spec = pl.BlockSpec((tk, tn), lambda i, j, k: (k, j))
    elif mode == "nt":
        a_spec = pl.BlockSpec((tm, tk), lambda i, j, k: (i, k))
        b_spec = pl.BlockSpec((tn, tk), lambda i, j, k: (j, k))
    else:
        a_spec = pl.BlockSpec((tk, tm), lambda i, j, k: (k, i))
        b_spec = pl.BlockSpec((tk, tn), lambda i, j, k: (k, j))

    grid = (M // tm, N // tn, nk)

    def kern(*refs):
        own, comm_refs = _own_refs(refs, comm, 2, n_out, n_scratch)
        a_ref, b_ref, o_refs = own[0], own[1], own[2:2 + n_out]
        k = pl.program_id(2)
        _comm_edge(comm, comm_refs, grid, first=True)
        part = lax.dot_general(a_ref[...].astype(BF), b_ref[...].astype(BF), dims,
                               preferred_element_type=F32)
        if nk == 1:
            for o_ref in o_refs:
                o_ref[...] = part.astype(o_ref.dtype)
        else:
            acc_ref = own[2 + n_out]

            @pl.when(k == 0)
            def _():
                acc_ref[...] = part

            @pl.when(k > 0)
            def _():
                acc_ref[...] += part

            @pl.when(k == nk - 1)
            def _():
                for o_ref in o_refs:
                    o_ref[...] = acc_ref[...].astype(o_ref.dtype)

        _comm_edge(comm, comm_refs, grid, first=False)

    if col_pieces > 1:
        per = N // col_pieces // tn
        out_spec = pl.BlockSpec((None, tm, tn), lambda i, j, k: (j // per, i, j % per))
        shape = (col_pieces, M, N // col_pieces)
    else:
        out_spec = pl.BlockSpec((tm, tn), lambda i, j, k: (i, j))
        shape = (M, N)
    dtypes = [out_dtype, BF] if twin else [out_dtype]
    res = pl.pallas_call(
        kern, name=name, grid=grid,
        in_specs=[a_spec, b_spec] + _comm_specs(comm, "in"),
        out_specs=[out_spec] * n_out + _comm_specs(comm, "out"),
        out_shape=[jax.ShapeDtypeStruct(shape, d) for d in dtypes] + (comm.out_shapes if comm else []),
        scratch_shapes=[pltpu.VMEM((tm, tn), F32)] * n_scratch + (comm.sem_shapes if comm else []),
        compiler_params=_cparams("parallel", "parallel", "arbitrary"),
    )(a, b, *(comm.ins if comm else []))
    own = res[0] if n_out == 1 else tuple(res[:n_out])
    return (own, res[n_out:]) if comm else own


ROWS = 512


def _row_spec(tm, width=D_MODEL, col=0):
    return pl.BlockSpec((tm, width), lambda i: (i, col))


def _vec_spec(width=D_MODEL):
    return pl.BlockSpec((1, width), lambda i: (0, 0))


def _rms(x):
    return lax.rsqrt(jnp.mean(x * x, axis=-1, keepdims=True) + RMS_EPS)


def _colsum(x):
    return jnp.sum(x, axis=0, keepdims=True)


def _norm_bwd(d_xn, xn, r):
    return r * (d_xn - xn * jnp.mean(d_xn * xn, axis=-1, keepdims=True))


def _pre_norm(x, g, scale, shift, name, comm=None):
    S = x.shape[0]
    tm = _tile(S, ROWS, 8)
    grid = (S // tm,)

    def kern(*refs):
        (x_ref, g_ref, sc_ref, sh_ref, h_ref), comm_refs = _own_refs(refs, comm, 4, 1, 0)
        _comm_edge(comm, comm_refs, grid, first=True)
        xf = x_ref[...]
        y = xf * _rms(xf) * g_ref[...]
        h_ref[...] = (y * (1.0 + sc_ref[...]) + sh_ref[...]).astype(BF)
        _comm_edge(comm, comm_refs, grid, first=False)

    res = pl.pallas_call(
        kern, name=name, grid=grid,
        in_specs=[_row_spec(tm), _vec_spec(), _vec_spec(), _vec_spec()] + _comm_specs(comm, "in"),
        out_specs=[_row_spec(tm)] + _comm_specs(comm, "out"),
        out_shape=[jax.ShapeDtypeStruct((S, D_MODEL), BF)] + (comm.out_shapes if comm else []),
        scratch_shapes=comm.sem_shapes if comm else [],
        compiler_params=_cparams("arbitrary"),
    )(x, g, scale, shift, *(comm.ins if comm else []))
    return res[0], res[1:]


def _post_pre(x, y1, g2, gate_m, g3, scale_f, shift_f):
    S = x.shape[0]
    tm = _tile(S, ROWS, 8)

    def kern(x_ref, y_ref, g2_ref, gm_ref, g3_ref, sc_ref, sh_ref, x2_ref, h2_ref):
        y = y_ref[...].astype(F32)
        n2 = y * _rms(y) * g2_ref[...]
        x2 = x_ref[...] + gm_ref[...] * n2
        x2_ref[...] = x2
        n3 = x2 * _rms(x2) * g3_ref[...]
        h2_ref[...] = (n3 * (1.0 + sc_ref[...]) + sh_ref[...]).astype(BF)

    return pl.pallas_call(
        kern, name="post_mix_pre_ffn", grid=(S // tm,),
        in_specs=[_row_spec(tm), _row_spec(tm)] + [_vec_spec()] * 5,
        out_specs=[_row_spec(tm), _row_spec(tm)],
        out_shape=[jax.ShapeDtypeStruct((S, D_MODEL), F32), jax.ShapeDtypeStruct((S, D_MODEL), BF)],
        compiler_params=_cparams("parallel"),
    )(x, y1, g2, gate_m, g3, scale_f, shift_f)


def _stats_spec():
    return pl.BlockSpec((8, D_MODEL), lambda i: (0, 0))


def _final(x2, y2, g4, gate_f, target):
    S = x2.shape[0]
    tm = _tile(S, ROWS, 8)

    def kern(x2_ref, y_ref, g4_ref, gf_ref, t_ref, dout_ref, dy_ref, st_ref):
        @pl.when(pl.program_id(0) == 0)
        def _():
            st_ref[...] = jnp.zeros_like(st_ref)

        y = y_ref[...].astype(F32)
        r = _rms(y)
        yn = y * r
        n4 = yn * g4_ref[...]
        diff = x2_ref[...] + gf_ref[...] * n4 - t_ref[...]
        d_out = diff / D_MODEL
        dout_ref[...] = d_out
        dn = d_out * gf_ref[...]
        dy_ref[...] = _norm_bwd(dn * g4_ref[...], yn, r).astype(BF)
        st_ref[0:1, :] += _colsum(d_out * n4)
        st_ref[1:2, :] += _colsum(dn * yn)
        st_ref[2:3, :] += _colsum(diff * diff)

    return pl.pallas_call(
        kern, name="final_loss", grid=(S // tm,),
        in_specs=[_row_spec(tm), _row_spec(tm), _vec_spec(), _vec_spec(), _row_spec(tm)],
        out_specs=[_row_spec(tm), _row_spec(tm), _stats_spec()],
        out_shape=[jax.ShapeDtypeStruct((S, D_MODEL), F32), jax.ShapeDtypeStruct((S, D_MODEL), BF),
                   jax.ShapeDtypeStruct((8, D_MODEL), F32)],
        compiler_params=_cparams("arbitrary"),
    )(x2, y2, g4, gate_f, target)


def _mid_bwd(d_h2, x2, d_out, y1, g3, scale_f, g2, gate_m):
    S = x2.shape[0]
    tm = _tile(S, ROWS, 8)

    def kern(dh_ref, x2_ref, dout_ref, y_ref, g3_ref, sc_ref, g2_ref, gm_ref, dx2_ref, dy_ref, st_ref):
        @pl.when(pl.program_id(0) == 0)
        def _():
            st_ref[...] = jnp.zeros_like(st_ref)

        dh = dh_ref[...].astype(F32)
        x2 = x2_ref[...]
        r3 = _rms(x2)
        xn = x2 * r3
        one_sc = 1.0 + sc_ref[...]
        d_x2 = dout_ref[...] + _norm_bwd(dh * one_sc * g3_ref[...], xn, r3)
        dx2_ref[...] = d_x2
        y = y_ref[...].astype(F32)
        r2 = _rms(y)
        yn = y * r2
        dn = d_x2 * gm_ref[...]
        dy_ref[...] = _norm_bwd(dn * g2_ref[...], yn, r2).astype(BF)
        st_ref[0:1, :] += _colsum(dh)
        st_ref[1:2, :] += _colsum(dh * (xn * g3_ref[...]))
        st_ref[2:3, :] += _colsum(dh * one_sc * xn)
        st_ref[3:4, :] += _colsum(d_x2 * (yn * g2_ref[...]))
        st_ref[4:5, :] += _colsum(dn * yn)

    return pl.pallas_call(
        kern, name="mid_bwd", grid=(S // tm,),
        in_specs=[_row_spec(tm)] * 4 + [_vec_spec()] * 4,
        out_specs=[_row_spec(tm), _row_spec(tm), _stats_spec()],
        out_shape=[jax.ShapeDtypeStruct((S, D_MODEL), F32), jax.ShapeDtypeStruct((S, D_MODEL), BF),
                   jax.ShapeDtypeStruct((8, D_MODEL), F32)],
        compiler_params=_cparams("arbitrary"),
    )(d_h2, x2, d_out, y1, g3, scale_f, g2, gate_m)


def _pre_bwd(d_h1, x, d_x2, g1, scale_m):
    S = x.shape[0]
    tm = _tile(S, ROWS, 8)

    def kern(dh_ref, x_ref, dx2_ref, g_ref, sc_ref, gx_ref, st_ref):
        @pl.when(pl.program_id(0) == 0)
        def _():
            st_ref[...] = jnp.zeros_like(st_ref)

        dh = dh_ref[...].astype(F32)
        xf = x_ref[...]
        r = _rms(xf)
        xn = xf * r
        one_sc = 1.0 + sc_ref[...]
        gx_ref[...] = dx2_ref[...] + _norm_bwd(dh * one_sc * g_ref[...], xn, r)
        st_ref[0:1, :] += _colsum(dh)
        st_ref[1:2, :] += _colsum(dh * (xn * g_ref[...]))
        st_ref[2:3, :] += _colsum(dh * one_sc * xn)

    return pl.pallas_call(
        kern, name="pre_mix_bwd", grid=(S // tm,),
        in_specs=[_row_spec(tm)] * 3 + [_vec_spec()] * 2,
        out_specs=[_row_spec(tm), _stats_spec()],
        out_shape=[jax.ShapeDtypeStruct((S, D_MODEL), F32), jax.ShapeDtypeStruct((8, D_MODEL), F32)],
        compiler_params=_cparams("arbitrary"),
    )(d_h1, x, d_x2, g1, scale_m)


def _rope(xs, widths, cos_t, sin_t, name):
    S = xs[0].shape[0]
    tm = _tile(S, 512, 8)
    n = len(xs)

    def kern(*refs):
        cos = refs[n][...]
        sin = refs[n + 1][...]
        first = (lax.broadcasted_iota(jnp.int32, cos.shape, 1) % HEAD_DIM) < HEAD_DIM // 2
        for x_ref, o_ref, w in zip(refs[:n], refs[n + 2:], widths):
            for c0 in range(0, w, LANES):
                v = x_ref[:, c0:c0 + LANES]
                partner = jnp.where(first, pltpu.roll(v, LANES - HEAD_DIM // 2, 1),
                                    pltpu.roll(v, HEAD_DIM // 2, 1))
                o_ref[:, c0:c0 + LANES] = (v * cos + partner * sin).astype(BF)

    return pl.pallas_call(
        kern, name=name, grid=(S // tm,),
        in_specs=[_row_spec(tm, w) for w in widths] + [_row_spec(tm, LANES)] * 2,
        out_specs=[_row_spec(tm, w) for w in widths],
        out_shape=[jax.ShapeDtypeStruct((S, w), BF) for w in widths],
        compiler_params=_cparams("parallel"),
    )(*xs, cos_t, sin_t)


def _merge_fwd(pg, pa, pb):
    S = pa.shape[0]
    tm = _tile(S, ROWS, 8)

    def kern(ga_ref, gb_ref, pa_ref, pb_ref, o_ref):
        ga = jax.nn.sigmoid(ga_ref[...].astype(F32))
        gb = jax.nn.sigmoid(gb_ref[...].astype(F32))
        o_ref[...] = (ga * pa_ref[...].astype(F32) + gb * pb_ref[...].astype(F32)).astype(BF)

    return pl.pallas_call(
        kern, name="merge_fwd", grid=(S // tm,),
        in_specs=[_row_spec(tm, col=0), _row_spec(tm, col=1), _row_spec(tm), _row_spec(tm)],
        out_specs=_row_spec(tm),
        out_shape=jax.ShapeDtypeStruct((S, D_MODEL), BF),
        compiler_params=_cparams("parallel"),
    )(pg, pg, pa, pb)


def _merge_bwd(d_merged, pg, pa, pb):
    S = pa.shape[0]
    tm = _tile(S, ROWS, 8)

    def kern(dm_ref, ga_ref, gb_ref, pa_ref, pb_ref, dpa_ref, dpb_ref, dga_ref, dgb_ref):
        dm = dm_ref[...].astype(F32)
        ga = jax.nn.sigmoid(ga_ref[...].astype(F32))
        gb = jax.nn.sigmoid(gb_ref[...].astype(F32))
        dpa_ref[...] = (dm * ga).astype(BF)
        dpb_ref[...] = (dm * gb).astype(BF)
        dga_ref[...] = (dm * pa_ref[...].astype(F32) * ga * (1.0 - ga)).astype(BF)
        dgb_ref[...] = (dm * pb_ref[...].astype(F32) * gb * (1.0 - gb)).astype(BF)

    bf_out = jax.ShapeDtypeStruct((S, D_MODEL), BF)
    return pl.pallas_call(
        kern, name="merge_bwd", grid=(S // tm,),
        in_specs=[_row_spec(tm), _row_spec(tm, col=0), _row_spec(tm, col=1), _row_spec(tm), _row_spec(tm)],
        out_specs=[_row_spec(tm)] * 4,
        out_shape=[bf_out] * 4,
        compiler_params=_cparams("parallel"),
    )(d_merged, pg, pg, pa, pb)


def _swiglu_fwd(gu):
    S = gu.shape[0]
    tm = _tile(S, ROWS, 8)
    tc = _tile(D_FF, 1408)
    nc = D_FF // tc

    def kern(g_ref, u_ref, o_ref):
        g = g_ref[...].astype(F32)
        o_ref[...] = (g * jax.nn.sigmoid(g) * u_ref[...].astype(F32)).astype(BF)

    return pl.pallas_call(
        kern, name="swiglu_fwd", grid=(S // tm, nc),
        in_specs=[pl.BlockSpec((tm, tc), lambda i, j: (i, j)),
                  pl.BlockSpec((tm, tc), lambda i, j: (i, j + nc))],
        out_specs=pl.BlockSpec((tm, tc), lambda i, j: (i, j)),
        out_shape=jax.ShapeDtypeStruct((S, D_FF), BF),
        compiler_params=_cparams("parallel", "parallel"),
    )(gu, gu)


def _swiglu_bwd(d_act, gu):
    S = gu.shape[0]
    tm = _tile(S, ROWS // 2, 8)

    def kern(da_ref, g_ref, u_ref, o_ref):
        g = g_ref[...].astype(F32)
        u = u_ref[...].astype(F32)
        da = da_ref[...].astype(F32)
        sg = jax.nn.sigmoid(g)
        o_ref[:, :D_FF] = (da * u * (sg * (1.0 + g * (1.0 - sg)))).astype(BF)
        o_ref[:, D_FF:] = (da * (g * sg)).astype(BF)

    return pl.pallas_call(
        kern, name="swiglu_bwd", grid=(S // tm,),
        in_specs=[_row_spec(tm, D_FF), _row_spec(tm, D_FF, 0), _row_spec(tm, D_FF, 1)],
        out_specs=_row_spec(tm, 2 * D_FF),
        out_shape=jax.ShapeDtypeStruct((S, 2 * D_FF), BF),
        compiler_params=_cparams("parallel"),
    )(d_act, gu, gu)


def _split3(x):
    hi = x.astype(BF)
    r1 = x - hi.astype(F32)
    mid = r1.astype(BF)
    lo = (r1 - mid.astype(F32)).astype(BF)
    return hi, mid, lo


def _tri_dot(tri, x):
    return sum(jnp.dot(tri, part, preferred_element_type=F32) for part in _split3(x))


def _log_sigmoid(z):
    return jnp.minimum(z, 0.0) - jnp.log(1.0 + jnp.exp(-jnp.abs(z)))


def _fox_gate_fwd(pa, b_f_pad):
    S = pa.shape[0]
    T = _tile(S, 512, 8)
    f_col = OFF_F // LANES

    def kern(z_ref, b_ref, cum_ref, carry_ref):
        @pl.when(pl.program_id(0) == 0)
        def _():
            carry_ref[...] = jnp.zeros_like(carry_ref)

        log_f = _log_sigmoid(z_ref[...] + b_ref[...])
        row = lax.broadcasted_iota(jnp.int32, (T, T), 0)
        col = lax.broadcasted_iota(jnp.int32, (T, T), 1)
        tri = (col <= row).astype(BF)
        cum = _tri_dot(tri, log_f) + carry_ref[...]
        cum_ref[...] = cum
        carry_ref[...] = cum[T - 1:T, :]

    return pl.pallas_call(
        kern, name="fox_gate_fwd", grid=(S // T,),
        in_specs=[_row_spec(T, LANES, f_col), _vec_spec(LANES)],
        out_specs=_row_spec(T, LANES),
        out_shape=jax.ShapeDtypeStruct((S, LANES), F32),
        scratch_shapes=[pltpu.VMEM((1, LANES), F32)],
        compiler_params=_cparams("arbitrary"),
    )(pa, b_f_pad)


def _fox_gate_bwd(rowsum_ds, colsum_ds, pa, b_f_pad):
    S = pa.shape[0]
    T = _tile(S, 512, 8)
    nb = S // T
    f_col = OFF_F // LANES

    def kern(dr_ref, dc_ref, z_ref, b_ref, df_ref, dbf_ref, carry_ref):
        @pl.when(pl.program_id(0) == 0)
        def _():
            carry_ref[...] = jnp.zeros_like(carry_ref)
            dbf_ref[...] = jnp.zeros_like(dbf_ref)

        row = lax.broadcasted_iota(jnp.int32, (T, T), 0)
        col = lax.broadcasted_iota(jnp.int32, (T, T), 1)
        tri = (col >= row).astype(BF)
        rev = _tri_dot(tri, dr_ref[...] - dc_ref[...]) + carry_ref[...]
        carry_ref[...] = rev[0:1, :]
        z = z_ref[...] + b_ref[...]
        lane = lax.broadcasted_iota(jnp.int32, (T, LANES), 1)
        d_z = jnp.where(lane < B_HEADS, rev * jax.nn.sigmoid(-z), 0.0)
        df_ref[...] = d_z.astype(BF)
        dbf_ref[0:1, :] += _colsum(d_z)

    return pl.pallas_call(
        kern, name="fox_gate_bwd", grid=(nb,),
        in_specs=[pl.BlockSpec((T, LANES), lambda i: (nb - 1 - i, 0)),
                  pl.BlockSpec((T, LANES), lambda i: (nb - 1 - i, 0)),
                  pl.BlockSpec((T, LANES), lambda i: (nb - 1 - i, f_col)),
                  _vec_spec(LANES)],
        out_specs=[pl.BlockSpec((T, LANES), lambda i: (nb - 1 - i, 0)),
                   pl.BlockSpec((8, LANES), lambda i: (0, 0))],
        out_shape=[jax.ShapeDtypeStruct((S, LANES), BF), jax.ShapeDtypeStruct((8, LANES), F32)],
        scratch_shapes=[pltpu.VMEM((1, LANES), F32)],
        compiler_params=_cparams("arbitrary"),
    )(rowsum_ds, colsum_ds, pa, b_f_pad)


NEG_INF = float("-inf")
QK_SCALE = 1.0 / math.sqrt(HEAD_DIM)


def _half_mask(shape, half):
    lane = lax.broadcasted_iota(jnp.int32, shape, 1)
    return (lane < HEAD_DIM) if half == 0 else (lane >= HEAD_DIM)


def _bias_block(shape, terms, term_off, ones_lo, ones_hi):
    l64 = lax.broadcasted_iota(jnp.int32, shape, 1) & (HEAD_DIM - 1)
    out = jnp.where((l64 >= ones_lo) & (l64 < ones_hi), 1.0, 0.0)
    for t, term in enumerate(terms):
        out = jnp.where(l64 == term_off + t, term.astype(F32), out)
    return out


def _head_column(block, head):
    lane = lax.broadcasted_iota(jnp.int32, block.shape, 1)
    return jnp.sum(jnp.where(lane == head, block, 0.0), axis=1, keepdims=True)


def _crossed(shape, first, second):
    return jnp.where(_half_mask(shape, 0), second, first)


def _fox_prep_fwd(cum, T):
    S = cum.shape[0]
    shape = (T, LANES)

    def kern(c_ref, bq_ref, bk_ref):
        p_id = pl.program_id(0)
        cum_blk = c_ref[...]
        c3 = _split3(_crossed(shape, _head_column(cum_blk, 2 * p_id), _head_column(cum_blk, 2 * p_id + 1)))
        bq_ref[...] = _bias_block(shape, c3, 0, 3, 6).astype(BF)
        bk_ref[...] = _bias_block(shape, [-t.astype(F32) for t in c3], 3, 0, 3).astype(BF)

    out_spec = pl.BlockSpec((None, T, LANES), lambda p, i: (p, i, 0))
    out_shape = jax.ShapeDtypeStruct((B_HEADS // 2, S, LANES), BF)
    return pl.pallas_call(
        kern, name="fox_prep_fwd", grid=(B_HEADS // 2, S // T),
        in_specs=[pl.BlockSpec((T, LANES), lambda p, i: (i, 0))],
        out_specs=[out_spec, out_spec], out_shape=[out_shape, out_shape],
        compiler_params=_cparams("parallel", "parallel"),
    )(cum)


def _fox_fwd(p_b, bq, bk, T, comm=None):
    S = p_b.shape[0]
    nq = S // T
    n_pairs = B_HEADS // 2
    grid = (n_pairs, nq)

    def kern(*refs):
        (q_ref, k_ref, v_ref, bq_ref, bk_ref, o_ref, lse_ref), comm_refs = _own_refs(refs, comm, 5, 2, 0)
        _comm_edge(comm, comm_refs, grid, first=True)
        i = pl.program_id(1)
        rowcol = lax.broadcasted_iota(jnp.int32, (T, T), 0) - lax.broadcasted_iota(jnp.int32, (T, T), 1)
        hms = (_half_mask((T, LANES), 0), _half_mask((T, LANES), 1))
        q_scaled = (q_ref[...].astype(F32) * QK_SCALE).astype(BF)
        bq_blk = bq_ref[...]
        qs = [jnp.where(hms[h], q_scaled, bq_blk) for h in (0, 1)]

        def step(j, carry, masked):
            rows = pl.ds(pl.multiple_of(j * T, T), T)
            kj, bkj, vj = k_ref[rows, :], bk_ref[rows, :], v_ref[rows, :]
            new = []
            for half in (0, 1):
                m, l, acc = carry[half]
                s = lax.dot_general(qs[half], jnp.where(hms[half], kj, bkj), (((1,), (1,)), ((), ())),
                                    preferred_element_type=F32)
                if masked:
                    s = jnp.where(rowcol >= 0, s, NEG_INF)
                m_new = jnp.maximum(m, jnp.max(s, axis=1, keepdims=True))
                alpha = jnp.exp(m - m_new)
                p = jnp.exp(s - m_new)
                l_new = alpha * l + jnp.sum(p, axis=1, keepdims=True)
                acc_new = alpha * acc + jnp.dot(p.astype(BF), vj, preferred_element_type=F32)
                new.append((m_new, l_new, acc_new))
            return tuple(new)

        one = (jnp.full((T, 1), NEG_INF, F32), jnp.zeros((T, 1), F32), jnp.zeros((T, LANES), F32))
        carry = lax.fori_loop(0, i, functools.partial(step, masked=False), (one, one))
        (m0, l0, acc0), (m1, l1, acc1) = step(i, carry, True)
        hm0 = _half_mask((T, LANES), 0)
        o_ref[...] = jnp.where(hm0, acc0 / l0, acc1 / l1)
        lse_ref[...] = jnp.where(hm0, m0 + jnp.log(l0), m1 + jnp.log(l1))
        _comm_edge(comm, comm_refs, grid, first=False)

    out_spec = pl.BlockSpec((T, LANES), lambda p, i: (i, p))
    res = pl.pallas_call(
        kern, name="fox_fwd", grid=grid,
        in_specs=[pl.BlockSpec((T, LANES), lambda p, i: (i, OFF_QB // LANES + p)),
                  pl.BlockSpec((S, LANES), lambda p, i: (0, OFF_KB // LANES + p)),
                  pl.BlockSpec((S, LANES), lambda p, i: (0, OFF_VB // LANES + p)),
                  pl.BlockSpec((None, T, LANES), lambda p, i: (p, i, 0)),
                  pl.BlockSpec((None, S, LANES), lambda p, i: (p, 0, 0))] + _comm_specs(comm, "in"),
        out_specs=[out_spec, out_spec] + _comm_specs(comm, "out"),
        out_shape=[jax.ShapeDtypeStruct((S, n_pairs * LANES), F32)] * 2 + (comm.out_shapes if comm else []),
        scratch_shapes=comm.sem_shapes if comm else [],
        compiler_params=_cparams("arbitrary", "arbitrary"),
    )(p_b, p_b, p_b, bq, bk, *(comm.ins if comm else []))
    return res[0], res[1], res[2:]


def _fox_prep_bwd(cum, o, do, lse, T):
    S = o.shape[0]
    shape = (T, LANES)

    def kern(c_ref, o_ref, do_ref, lse_ref, bq_ref, bdo_ref):
        p_id = pl.program_id(0)
        cum_blk = c_ref[...]
        cq = _crossed(shape, _head_column(cum_blk, 2 * p_id), _head_column(cum_blk, 2 * p_id + 1))
        b3 = _split3(cq - pltpu.roll(lse_ref[...], HEAD_DIM, 1))
        bq_ref[...] = _bias_block(shape, b3, 0, 3, 6).astype(BF)
        dd = do_ref[...] * o_ref[...]
        delta = [jnp.sum(jnp.where(_half_mask(shape, h), dd, 0.0), axis=1, keepdims=True) for h in (0, 1)]
        d3 = _split3(-_crossed(shape, delta[0], delta[1]))
        bdo_ref[...] = _bias_block(shape, d3, 0, 0, 0).astype(BF)

    block = pl.BlockSpec((None, T, LANES), lambda p, i: (p, i, 0))
    tile = pl.BlockSpec((T, LANES), lambda p, i: (i, p))
    out_shape = jax.ShapeDtypeStruct((B_HEADS // 2, S, LANES), BF)
    return pl.pallas_call(
        kern, name="fox_prep_bwd", grid=(B_HEADS // 2, S // T),
        in_specs=[pl.BlockSpec((T, LANES), lambda p, i: (i, 0)), tile, tile, tile],
        out_specs=[block, block], out_shape=[out_shape, out_shape],
        compiler_params=_cparams("parallel", "parallel"),
    )(cum, o, do, lse)


def _fox_bwd(p_b, do, bq, bk, bdo, T, comm=None):
    S = p_b.shape[0]
    n_pairs = B_HEADS // 2
    nq = S // T
    grid = (n_pairs,)

    def kern(*refs):
        own, comm_refs = _own_refs(refs, comm, 7, 5, 0)
        q_ref, k_ref, v_ref, do_ref, bq_ref, bk_ref, bdo_ref, dq_ref, dk_ref, dv_ref, dck_ref, dcq_ref = own
        _comm_edge(comm, comm_refs, grid, first=True)
        p_id = pl.program_id(0)
        rowcol = lax.broadcasted_iota(jnp.int32, (T, T), 0) - lax.broadcasted_iota(jnp.int32, (T, T), 1)
        lane = lax.broadcasted_iota(jnp.int32, (T, LANES), 1)
        dk_ref[...] = jnp.zeros_like(dk_ref)
        dv_ref[...] = jnp.zeros_like(dv_ref)
        dck_ref[...] = jnp.zeros_like(dck_ref)

        @pl.when(p_id == 0)
        def _():
            dcq_ref[...] = jnp.zeros_like(dcq_ref)

        hms = (_half_mask((T, LANES), 0), _half_mask((T, LANES), 1))
        v_ones = _bias_block((T, LANES), [], 0, 0, 3).astype(BF)

        def outer(i, carry):
            qrows = pl.ds(pl.multiple_of(i * T, T), T)
            q_scaled = (q_ref[qrows, :].astype(F32) * QK_SCALE).astype(BF)
            do_b = do_ref[qrows, :].astype(BF)
            bq_i, bdo_i = bq_ref[qrows, :], bdo_ref[qrows, :]
            qa = [jnp.where(hms[h], q_scaled, bq_i) for h in (0, 1)]
            doa = [jnp.where(hms[h], do_b, bdo_i) for h in (0, 1)]
            q_own = [jnp.where(hms[h], q_scaled, 0) for h in (0, 1)]
            do_own = [jnp.where(hms[h], do_b, 0) for h in (0, 1)]

            def inner(j, carry_in, masked):
                krows = pl.ds(pl.multiple_of(j * T, T), T)
                kj, bkj, vj = k_ref[krows, :], bk_ref[krows, :], v_ref[krows, :]
                dv_add, dk_add, new = 0.0, 0.0, []
                for half in (0, 1):
                    dq, rs = carry_in[half]
                    ka = jnp.where(hms[half], kj, bkj)
                    s = lax.dot_general(qa[half], ka, (((1,), (1,)), ((), ())), preferred_element_type=F32)
                    if masked:
                        s = jnp.where(rowcol >= 0, s, NEG_INF)
                    p = jnp.exp(s)
                    ds = p * lax.dot_general(doa[half], jnp.where(hms[half], vj, v_ones),
                                             (((1,), (1,)), ((), ())), preferred_element_type=F32)
                    ds_b = ds.astype(BF)
                    dv_add = dv_add + lax.dot_general(p.astype(BF), do_own[half], (((0,), (0,)), ((), ())),
                                                      preferred_element_type=F32)
                    dk_add = dk_add + lax.dot_general(ds_b, q_own[half], (((0,), (0,)), ((), ())),
                                                      preferred_element_type=F32)
                    dck_ref[half:half + 1, krows] += jnp.sum(ds, axis=0, keepdims=True)
                    new.append((dq + jnp.dot(ds_b, jnp.where(hms[half], kj, 0), preferred_element_type=F32),
                                rs + jnp.sum(ds, axis=1, keepdims=True)))
                dv_ref[krows, :] += dv_add
                dk_ref[krows, :] += dk_add
                return tuple(new)

            one = (jnp.zeros((T, LANES), F32), jnp.zeros((T, 1), F32))
            carry_in = lax.fori_loop(0, i, functools.partial(inner, masked=False), (one, one))
            (dq0, rs0), (dq1, rs1) = inner(i, carry_in, True)
            dq_ref[qrows, :] = (dq0 + dq1) * QK_SCALE
            dcq_ref[qrows, :] = jnp.where(lane == 2 * p_id, rs0, jnp.where(lane == 2 * p_id + 1, rs1,
                                                                             dcq_ref[qrows, :]))
            return carry

        lax.fori_loop(0, nq, outer, 0)
        _comm_edge(comm, comm_refs, grid, first=False)

    block = pl.BlockSpec((None, S, LANES), lambda p: (p, 0, 0))
    pair = pl.BlockSpec((S, LANES), lambda p: (0, p))
    slab = lambda off: pl.BlockSpec((S, LANES), lambda p: (0, off // LANES + p))
    wide = jax.ShapeDtypeStruct((S, n_pairs * LANES), F32)
    res = pl.pallas_call(
        kern, name="fox_bwd", grid=grid,
        in_specs=[slab(OFF_QB), slab(OFF_KB), slab(OFF_VB), pair, block, block, block]
        + _comm_specs(comm, "in"),
        out_specs=[pair, pair, pair, pl.BlockSpec((None, 2, S), lambda p: (p, 0, 0)),
                   pl.BlockSpec((S, LANES), lambda p: (0, 0))] + _comm_specs(comm, "out"),
        out_shape=[wide, wide, wide, jax.ShapeDtypeStruct((n_pairs, 2, S), F32),
                   jax.ShapeDtypeStruct((S, LANES), F32)] + (comm.out_shapes if comm else []),
        scratch_shapes=comm.sem_shapes if comm else [],
        compiler_params=_cparams("arbitrary"),
    )(p_b, p_b, p_b, do, bq, bk, bdo, *(comm.ins if comm else []))
    return (*res[:5], res[5:])


SWA_TQ = 256
SWA_SUB = 8


def _swa_window(i, tq):
    start = pl.multiple_of(jnp.maximum(i * tq - WINDOW, 0), LANES)
    return start, i * tq - start


def _swa_valid(offset, tq):
    rel = offset + lax.broadcasted_iota(jnp.int32, (tq, tq + WINDOW), 0) \
        - lax.broadcasted_iota(jnp.int32, (tq, tq + WINDOW), 1)
    return (rel >= 0) & (rel < WINDOW)


def _swa_fwd(qk, v_arr, v_col, sinks):
    S = qk.shape[0]
    tq = min(SWA_TQ, S - WINDOW)
    sub = min(SWA_SUB, S // tq)
    win = tq + WINDOW

    def kern(q_ref, k_ref, v_ref, sink_ref, o_ref, lse_ref):
        p_id, i = pl.program_id(0), pl.program_id(1)
        hm0 = _half_mask((tq, LANES), 0)
        for t in range(sub):
            rows = slice(t * tq, (t + 1) * tq)
            start, offset = _swa_window(i * sub + t, tq)
            kw = k_ref[pl.ds(start, win), :]
            vw = v_ref[pl.ds(start, win), :].astype(BF)
            valid = _swa_valid(offset, tq)
            q = q_ref[rows, :]
            outs, lses = [], []
            for half in (0, 1):
                hm = _half_mask((tq, LANES), half)
                qh = (jnp.where(hm, q, 0).astype(F32) * QK_SCALE).astype(BF)
                s = lax.dot_general(qh, kw, (((1,), (1,)), ((), ())), preferred_element_type=F32)
                s = jnp.where(valid, s, NEG_INF)
                sink = sink_ref[2 * p_id + half]
                m = jnp.maximum(jnp.max(s, axis=1, keepdims=True), sink)
                p = jnp.exp(s - m)
                denom = jnp.sum(p, axis=1, keepdims=True) + jnp.exp(sink - m)
                outs.append(jnp.dot(p.astype(BF), vw, preferred_element_type=F32) / denom)
                lses.append(m + jnp.log(denom))
            o_ref[rows, :] = jnp.where(hm0, outs[0], outs[1])
            lse_ref[rows, :] = jnp.where(hm0, lses[0], lses[1])

    tile = pl.BlockSpec((sub * tq, LANES), lambda p, i: (i, p))
    return pl.pallas_call(
        kern, name="swa_fwd", grid=(A_Q_HEADS // 2, S // (sub * tq)),
        in_specs=[tile, pl.BlockSpec((S, LANES), lambda p, i: (0, A_Q_HEADS // 2)),
                  pl.BlockSpec((S, LANES), lambda p, i: (0, v_col)),
                  pl.BlockSpec(memory_space=pltpu.SMEM)],
        out_specs=[tile, tile],
        out_shape=[jax.ShapeDtypeStruct((S, A_Q_HEADS * HEAD_DIM), F32)] * 2,
        compiler_params=_cparams("parallel", "arbitrary"),
    )(qk, qk, v_arr, sinks)


def _swa_bwd(qk, v_arr, v_col, o_arr, do_arr, lse_arr, sinks, comm=None):
    S = qk.shape[0]
    tq = min(SWA_TQ, S - WINDOW)
    sub = min(SWA_SUB, S // tq)
    win = tq + WINDOW
    n_pairs = A_Q_HEADS // 2
    grid = (n_pairs, S // (sub * tq))

    def kern(*refs):
        own, comm_refs = _own_refs(refs, comm, 7, 4, 0)
        q_ref, k_ref, v_ref, o_ref, do_ref, lse_ref, sink_ref, dq_ref, dk_ref, dv_ref, dsink_ref = own
        _comm_edge(comm, comm_refs, grid, first=True)
        p_id, i = pl.program_id(0), pl.program_id(1)

        @pl.when((p_id == 0) & (i == 0))
        def _():
            dk_ref[...] = jnp.zeros_like(dk_ref)
            dv_ref[...] = jnp.zeros_like(dv_ref)

        @pl.when(i == 0)
        def _():
            dsink_ref[...] = jnp.zeros_like(dsink_ref)

        for t in range(sub):
            rows = slice(t * tq, (t + 1) * tq)
            start, offset = _swa_window(i * sub + t, tq)
            wrows = pl.ds(start, win)
            kw = k_ref[wrows, :]
            vw = v_ref[wrows, :].astype(BF)
            valid = _swa_valid(offset, tq)
            q, do, o, lse2 = q_ref[rows, :], do_ref[rows, :], o_ref[rows, :], lse_ref[rows, :]
            dq = jnp.zeros((tq, LANES), F32)
            dk = jnp.zeros((win, LANES), F32)
            dv = jnp.zeros((win, LANES), F32)
            for half in (0, 1):
                hm = _half_mask((tq, LANES), half)
                lane0 = half * HEAD_DIM
                qh = (jnp.where(hm, q, 0).astype(F32) * QK_SCALE).astype(BF)
                do_f = jnp.where(hm, do, 0.0)
                doh = do_f.astype(BF)
                delta = jnp.sum(do_f * o, axis=1, keepdims=True)
                lse = lse2[:, lane0:lane0 + 1]
                s = lax.dot_general(qh, kw, (((1,), (1,)), ((), ())), preferred_element_type=F32)
                p = jnp.exp(jnp.where(valid, s, NEG_INF) - lse)
                dp = lax.dot_general(doh, vw, (((1,), (1,)), ((), ())), preferred_element_type=F32)
                ds_b = (p * (dp - delta)).astype(BF)
                dv = dv + lax.dot_general(p.astype(BF), doh, (((0,), (0,)), ((), ())),
                                          preferred_element_type=F32)
                dk = dk + lax.dot_general(ds_b, qh, (((0,), (0,)), ((), ())), preferred_element_type=F32)
                kh = jnp.where(_half_mask((win, LANES), half), kw, 0)
                dq = dq + jnp.dot(ds_b, kh, preferred_element_type=F32)
                p_sink = jnp.exp(sink_ref[2 * p_id + half] - lse)
                dsink_ref[0, half:half + 1, :] += jnp.broadcast_to(
                    -jnp.sum(p_sink * delta, axis=0, keepdims=True), (1, LANES))
            dq_ref[rows, :] = dq * QK_SCALE
            dk_ref[wrows, :] += dk
            dv_ref[wrows, :] += dv
        _comm_edge(comm, comm_refs, grid, first=False)

    tile = pl.BlockSpec((sub * tq, LANES), lambda p, i: (i, p))
    whole = lambda col: pl.BlockSpec((S, LANES), lambda p, i: (0, col))
    res = pl.pallas_call(
        kern, name="swa_bwd", grid=grid,
        in_specs=[tile, whole(n_pairs), whole(v_col), tile, tile, tile,
                  pl.BlockSpec(memory_space=pltpu.SMEM)] + _comm_specs(comm, "in"),
        out_specs=[tile, whole(0), whole(0),
                   pl.BlockSpec((1, 8, LANES), lambda p, i: (p, 0, 0))] + _comm_specs(comm, "out"),
        out_shape=[jax.ShapeDtypeStruct((S, A_Q_HEADS * HEAD_DIM), F32),
                   jax.ShapeDtypeStruct((S, LANES), F32), jax.ShapeDtypeStruct((S, LANES), F32),
                   jax.ShapeDtypeStruct((n_pairs, 8, LANES), F32)] + (comm.out_shapes if comm else []),
        scratch_shapes=comm.sem_shapes if comm else [],
        compiler_params=_cparams("arbitrary", "arbitrary"),
    )(qk, qk, v_arr, o_arr, do_arr, lse_arr, sinks, *(comm.ins if comm else []))
    return (*res[:4], res[4:])


ADAMW_BLOCK = 512 * 1024


def _adamw(w, g, m, v, name):
    R, C = w.shape
    tr, tc = _tile(R, max(8, ADAMW_BLOCK // C), 8), C

    def kern(w_ref, g_ref, m_ref, v_ref, d_ref, mo_ref, vo_ref):
        g_ = g_ref[...]
        m_new = ADAM_B1 * m_ref[...] + (1.0 - ADAM_B1) * g_
        v_new = ADAM_B2 * v_ref[...] + (1.0 - ADAM_B2) * (g_ * g_)
        m_hat = m_new / (1.0 - ADAM_B1 ** ADAM_STEP)
        v_hat = v_new / (1.0 - ADAM_B2 ** ADAM_STEP)
        d_ref[...] = -ADAM_LR * (m_hat / (jnp.sqrt(v_hat) + ADAM_EPS) + ADAM_WD * w_ref[...])
        mo_ref[...] = m_new
        vo_ref[...] = v_new

    spec = pl.BlockSpec((tr, tc), lambda i, j: (i, j))
    shape = jax.ShapeDtypeStruct((R, C), F32)
    return pl.pallas_call(
        kern, name=name, grid=(R // tr, C // tc),
        in_specs=[spec] * 4, out_specs=[spec] * 3, out_shape=[shape] * 3,
        compiler_params=_cparams("parallel", "parallel"),
    )(w, g, m, v)


def _index_operand(i):
    return jnp.reshape(i, (1,)).astype(jnp.int32)


def _add_pair(whole, got, ci, name):
    P, R, C = whole.shape
    half = R // 2
    tr = _tile(half, ROWS, 16)
    nb = half // tr

    def kern(ci_ref, a_ref, b_ref, o_ref, ob_ref):
        s = a_ref[...] + b_ref[...].astype(F32)
        o_ref[...] = s
        ob_ref[...] = s.astype(BF)

    spec = pl.BlockSpec((None, tr, C), lambda p, i, ci_ref: (p, i, 0))
    return pl.pallas_call(
        kern, name=name,
        grid_spec=pltpu.PrefetchScalarGridSpec(
            num_scalar_prefetch=1, grid=(P, nb),
            in_specs=[pl.BlockSpec((None, tr, C), lambda p, i, ci_ref: (p, ci_ref[0] * nb + i, 0)), spec],
            out_specs=[spec, spec]),
        out_shape=[jax.ShapeDtypeStruct((P, half, C), F32), jax.ShapeDtypeStruct((P, half, C), BF)],
        compiler_params=_cparams("parallel", "parallel"),
    )(_index_operand(ci), whole, got)


def _add_three(parts, recv, chip, name):
    _, R, C = parts.shape
    tr = _tile(R, ROWS, 16)

    def kern(chip_ref, o_ref, r0_ref, r1_ref, r2_ref, out_ref):
        s = ((o_ref[...] + r0_ref[...].astype(F32)) + r1_ref[...].astype(F32)) + r2_ref[...].astype(F32)
        out_ref[0] = s
        out_ref[1] = s

    slab = lambda k: pl.BlockSpec((None, tr, C), lambda i, chip_ref: (k, i, 0))
    return pl.pallas_call(
        kern, name=name,
        grid_spec=pltpu.PrefetchScalarGridSpec(
            num_scalar_prefetch=1, grid=(R // tr,),
            in_specs=[pl.BlockSpec((None, tr, C), lambda i, chip_ref: (chip_ref[0], i, 0)),
                      slab(0), slab(1), slab(2)],
            out_specs=pl.BlockSpec((2, tr, C), lambda i, chip_ref: (0, i, 0))),
        out_shape=jax.ShapeDtypeStruct((2, R, C), F32),
        compiler_params=_cparams("parallel"),
    )(_index_operand(chip), parts, recv, recv, recv)


SM_ADA, SM_G, SM_LOSS, SM_BF, SM_SINK, SM_LEN = 0, 6144, 10240, 11264, 11272, 12288


def _small_finalize(gathered):
    def kern(g_ref, tot_ref, loss_ref):
        tot = g_ref[0:1, :]
        for b in range(1, N_DEV):
            tot = tot + g_ref[b:b + 1, :]
        tot_ref[...] = tot
        sq = jnp.sum(tot[:, SM_LOSS:SM_LOSS + D_MODEL], axis=1, keepdims=True)
        loss_ref[...] = jnp.broadcast_to(sq * (0.5 / D_MODEL), (1, LANES))

    full = lambda shape: pl.BlockSpec(shape, lambda i: (0, 0))
    return pl.pallas_call(
        kern, name="small_finalize", grid=(1,),
        in_specs=[full((N_DEV, SM_LEN))],
        out_specs=[full((1, SM_LEN)), full((1, LANES))],
        out_shape=[jax.ShapeDtypeStruct((1, SM_LEN), F32), jax.ShapeDtypeStruct((1, LANES), F32)],
        compiler_params=_cparams("arbitrary"),
    )(gathered)


def _ada_dw(c_t, d_ada):
    N = d_ada.shape[1]
    tn = _tile(N, 512)

    def kern(c_ref, d_ref, o_ref):
        acc = c_ref[:, 0:1] * d_ref[0:1, :]
        for b in range(1, N_DEV):
            acc = acc + c_ref[:, b:b + 1] * d_ref[b:b + 1, :]
        o_ref[...] = acc

    return pl.pallas_call(
        kern, name="ada_dw", grid=(N // tn,),
        in_specs=[pl.BlockSpec((D_MODEL, N_DEV), lambda j: (0, 0)), pl.BlockSpec((N_DEV, tn), lambda j: (0, j))],
        out_specs=pl.BlockSpec((D_MODEL, tn), lambda j: (0, j)),
        out_shape=jax.ShapeDtypeStruct((D_MODEL, N), F32),
        compiler_params=_cparams("parallel"),
    )(c_t, d_ada)


def _here():
    return lax.axis_index("x"), lax.axis_index("y"), lax.axis_index("c")


def _other_chips(x, y):
    return [(1 - x, y), (x, 1 - y), (1 - x, 1 - y)]


_ANY = pl.BlockSpec(memory_space=pl.ANY)


class _Comm:
    def __init__(self, ins, out_shapes, sem_shapes, start, finish):
        self.ins, self.out_shapes, self.sem_shapes = list(ins), list(out_shapes), list(sem_shapes)
        self.start, self.finish = start, finish

    def split(self, refs, n_in, n_out, n_scratch):
        a = n_in + len(self.ins)
        b = a + n_out + len(self.out_shapes)
        own = list(refs[:n_in]) + list(refs[a:a + n_out]) + list(refs[b:b + n_scratch])
        mine = (refs[n_in:a], refs[a + n_out:b], refs[b + n_scratch:])
        return own, mine


def _run_comm(comm, name):
    n_in, n_out = len(comm.ins), len(comm.out_shapes)

    def body(*refs):
        parts = (refs[:n_in], refs[n_in:n_in + n_out], refs[n_in + n_out:])
        comm.start(*parts)
        comm.finish(*parts)

    return pl.pallas_call(
        body, name=name,
        in_specs=[_ANY] * n_in, out_specs=[_ANY] * n_out,
        out_shape=comm.out_shapes, scratch_shapes=comm.sem_shapes,
    )(*comm.ins)


def _gather_comm(blocks):
    L = len(blocks)

    def parts(ins, outs, sems):
        send_sems, recv_sems, local_sems = sems
        x, y, c = _here()
        me, sibling = (x, y, c), (x, y, 1 - c)
        chips = _other_chips(x, y)

        def slot(px, py, pc):
            return 4 * px + 2 * py + pc

        def copy(l, k, block, to, src=None):
            dst = outs[l].at[slot(*block)]
            return pltpu.make_async_remote_copy(
                src_ref=dst if src is None else src, dst_ref=dst,
                send_sem=send_sems.at[l, k], recv_sem=recv_sems.at[l, k],
                device_id=to, device_id_type=MESH)

        mine = [pltpu.make_async_copy(ins[l], outs[l].at[slot(*me)], local_sems.at[l]) for l in range(L)]
        first = []
        for l in range(L):
            first.append(copy(l, 0, me, sibling, src=ins[l]))
            for j, chip in enumerate(chips):
                first.append(copy(l, 1 + j, me, (*chip, c), src=ins[l]))
        return c, me, sibling, chips, copy, mine, first

    def start(ins, outs, sems):
        *_, mine, first = parts(ins, outs, sems)
        for cp in mine + first:
            cp.start()

    def finish(ins, outs, sems):
        c, me, sibling, chips, copy, mine, first = parts(ins, outs, sems)
        passed = []
        for j, chip in enumerate(chips):
            for l in range(L):
                copy(l, 1 + j, (*chip, c), me).wait_recv()
                fwd = copy(l, 4 + j, (*chip, c), sibling)
                fwd.start()
                passed.append(fwd)
        for l in range(L):
            copy(l, 0, sibling, me).wait_recv()
        for j, chip in enumerate(chips):
            for l in range(L):
                copy(l, 4 + j, (*chip, 1 - c), me).wait_recv()
        for cp in first + passed:
            cp.wait_send()
        for cp in mine:
            cp.wait()

    return _Comm(blocks, [jax.ShapeDtypeStruct((N_DEV,) + b.shape, b.dtype) for b in blocks],
                 [pltpu.SemaphoreType.DMA((L, 7)), pltpu.SemaphoreType.DMA((L, 7)), pltpu.SemaphoreType.DMA((L,))],
                 start, finish)


def _allgather8(blocks, name):
    return _run_comm(_gather_comm(blocks), name)


def _swap_comm(arrs):
    L = len(arrs)

    def copies(ins, outs, sems):
        send_sems, recv_sems = sems
        x, y, c = _here()
        cps = []
        for l in range(L):
            half = arrs[l].shape[1] // 2
            rows = pl.ds(pl.multiple_of((1 - c) * half, 16), half)
            cps.append(pltpu.make_async_remote_copy(
                src_ref=ins[l].at[:, rows, :], dst_ref=outs[l], send_sem=send_sems.at[l],
                recv_sem=recv_sems.at[l], device_id=(x, y, 1 - c), device_id_type=MESH))
        return cps

    def start(ins, outs, sems):
        for cp in copies(ins, outs, sems):
            cp.start()

    def finish(ins, outs, sems):
        for cp in copies(ins, outs, sems):
            cp.wait()

    return _Comm(arrs, [jax.ShapeDtypeStruct((a.shape[0], a.shape[1] // 2, a.shape[2]), a.dtype) for a in arrs],
                 [pltpu.SemaphoreType.DMA((L,)), pltpu.SemaphoreType.DMA((L,))], start, finish)


def _sibling_join(bufs, name):
    L = len(bufs)

    def body(*refs):
        outs = refs[L:2 * L]
        send_sems, recv_sems = refs[2 * L:]
        x, y, c = _here()
        for l in range(L):
            pltpu.make_async_remote_copy(src_ref=outs[l].at[c], dst_ref=outs[l].at[c], send_sem=send_sems.at[l],
                                         recv_sem=recv_sems.at[l], device_id=(x, y, 1 - c),
                                         device_id_type=MESH).start()
        for l in range(L):
            pltpu.make_async_remote_copy(src_ref=outs[l].at[c], dst_ref=outs[l].at[1 - c],
                                         send_sem=send_sems.at[l], recv_sem=recv_sems.at[l],
                                         device_id=(x, y, 1 - c), device_id_type=MESH).wait()

    return pl.pallas_call(
        body, name=name,
        in_specs=[_ANY] * L, out_specs=[_ANY] * L,
        out_shape=[jax.ShapeDtypeStruct(a.shape, a.dtype) for a in bufs],
        input_output_aliases={l: l for l in range(L)},
        scratch_shapes=[pltpu.SemaphoreType.DMA((L,)), pltpu.SemaphoreType.DMA((L,))],
    )(*bufs)


def _scatter_comm(arrs):
    L = len(arrs)

    def copies(ins, outs, sems):
        send_sems, recv_sems = sems
        x, y, c = _here()
        return [pltpu.make_async_remote_copy(
            src_ref=ins[l].at[2 * tx + ty], dst_ref=outs[l].at[j],
            send_sem=send_sems.at[l, j], recv_sem=recv_sems.at[l, j],
            device_id=(tx, ty, c), device_id_type=MESH)
            for l in range(L) for j, (tx, ty) in enumerate(_other_chips(x, y))]

    def start(ins, outs, sems):
        for cp in copies(ins, outs, sems):
            cp.start()

    def finish(ins, outs, sems):
        for cp in copies(ins, outs, sems):
            cp.wait()

    return _Comm(arrs, [jax.ShapeDtypeStruct((3,) + a.shape[1:], a.dtype) for a in arrs],
                 [pltpu.SemaphoreType.DMA((L, 3)), pltpu.SemaphoreType.DMA((L, 3))], start, finish)


_A_ORDER = np.array(A_HEAD_ORDER)
_A_INVERSE = np.argsort(_A_ORDER)


def _permute_in_weights(w_in):
    qa = w_in[:, 0:512].reshape(D_MODEL, A_Q_HEADS, HEAD_DIM)[:, _A_ORDER, :].reshape(D_MODEL, 512)
    f_pad = jnp.pad(w_in[:, 2304:2312], ((0, 0), (0, LANES - B_HEADS)))
    w_a = jnp.concatenate([qa, w_in[:, 512:640], f_pad], axis=1)
    return w_a, w_in[:, 640:2304], w_in[:, 2312:4360]


def _slab_segments():
    segs = [(h * HEAD_DIM, int(_A_INVERSE[h]) * HEAD_DIM, HEAD_DIM) for h in range(A_Q_HEADS)]
    segs += [(512, OFF_KA, 128), (640, W_A + OFF_VA, 128), (768, W_A + OFF_QB, 1536),
             (2304, OFF_F, B_HEADS), (2312, W_A + W_B, W_G)]
    return segs


def _shard_slabs(dw_perm):
    R = dw_perm.shape[0]
    tr = _tile(R, 128, 8)
    plan = []
    for k in range(N_CHIP):
        for b in range(W_SHARD_PAD // LANES):
            lo, hi = k * W_SHARD + b * LANES, min(k * W_SHARD + (b + 1) * LANES, (k + 1) * W_SHARD)
            parts = []
            for o0, s0, n in _slab_segments():
                a, z = max(lo, o0), min(hi, o0 + n)
                while a < z:
                    s = s0 + (a - o0)
                    run = min(z - a, LANES - s % LANES)
                    parts.append((s // LANES, ((a - lo) - s % LANES) % LANES, a - lo, run))
                    a += run
            plan.append((k, b, parts))

    def kern(x_ref, o32_ref, obf_ref):
        lane = lax.broadcasted_iota(jnp.int32, (tr, LANES), 1)
        for k, b, parts in plan:
            acc = jnp.zeros((tr, LANES), F32)
            for src, rot, first, run in parts:
                blk = x_ref[:, src * LANES:(src + 1) * LANES]
                if rot:
                    blk = pltpu.roll(blk, rot, 1)
                acc = jnp.where((lane >= first) & (lane < first + run), blk, acc)
            o32_ref[k, :, b * LANES:(b + 1) * LANES] = acc
            obf_ref[k, :, b * LANES:(b + 1) * LANES] = acc.astype(BF)

    out_spec = pl.BlockSpec((N_CHIP, tr, W_SHARD_PAD), lambda i: (0, i, 0))
    return tuple(pl.pallas_call(
        kern, name="shard_slabs", grid=(R // tr,),
        in_specs=[pl.BlockSpec((tr, W_PERM), lambda i: (i, 0))],
        out_specs=[out_spec, out_spec],
        out_shape=[jax.ShapeDtypeStruct((N_CHIP, R, W_SHARD_PAD), F32),
                   jax.ShapeDtypeStruct((N_CHIP, R, W_SHARD_PAD), BF)],
        compiler_params=_cparams("parallel"),
    )(dw_perm))


class _NoExchange:
    def __init__(self, w_in, rest):
        self.w_in_whole, self.rest, self.grads = w_in, rest, {}

    def w_in_comm(self):
        return None

    def w_in(self, outs):
        return self.w_in_whole

    def rest_weights_comm(self):
        return None

    def rest_weights(self, outs):
        return self.rest

    def swap_comm(self, pieces, tag):
        self.grads[tag] = [p32 for p32, _ in pieces]
        return None

    def swap_done(self, outs, tag):
        return None

    def reduce_done(self, outs, tag):
        pass


class _Exchange:
    def __init__(self, ci, chip, w_in_shard, rest_shards):
        self.ci, self.chip, self.w_in_shard, self.rest_shards = ci, chip, w_in_shard, rest_shards
        self.pieces, self.part_f32, self.halves = {}, {}, {}

    def _my_half(self, a, axis=0, other=False):
        rows = a.shape[axis] // 2
        return lax.dynamic_slice_in_dim(a, ((1 - self.ci) if other else self.ci) * rows, rows, axis=axis)

    def w_in_comm(self):
        return _gather_comm([self._my_half(self.w_in_shard).astype(BF)])

    def w_in(self, outs):
        return _col_sharded(outs[0])

    def rest_weights_comm(self):
        return _gather_comm([self._my_half(w).astype(BF) for w in self.rest_shards])

    def rest_weights(self, outs):
        w_ba, w_bb, w_out, w_fi, w_fo = outs
        return (_col_sharded(w_ba), _col_sharded(w_bb), _row_sharded(w_out), _col_sharded(w_fi),
                _row_sharded(w_fo))

    def swap_comm(self, pieces, tag):
        self.pieces[tag] = pieces
        return _swap_comm([pbf for _, pbf in pieces])

    def swap_done(self, got, tag):
        self.part_f32[tag], part_bf = [], []
        for l, ((p32, _), g_) in enumerate(zip(self.pieces[tag], got)):
            s32, sbf = _add_pair(p32, g_, self.ci, f"chip_sum_{tag}_{l}")
            self.part_f32[tag].append(s32)
            part_bf.append(sbf)
        return _scatter_comm(part_bf)

    def reduce_done(self, outs, tag):
        self.halves[tag] = [_add_three(p32, r, self.chip, f"shard_sum_{tag}_{l}")
                            for l, (p32, r) in enumerate(zip(self.part_f32[tag], outs))]


def _col_sharded(g):
    return jnp.transpose(g.reshape(N_CHIP, -1, g.shape[-1]), (1, 0, 2)).reshape(2 * g.shape[1], N_CHIP * g.shape[-1])


def _row_sharded(g):
    return g.reshape(N_DEV * g.shape[1], g.shape[-1])


def _rope_tables(pos):
    inv_freq = 1.0 / (ROPE_THETA ** (jnp.arange(0, HEAD_DIM, 2, dtype=F32) / HEAD_DIM))
    ang = pos.astype(F32)[:, None] * inv_freq
    cos, sin = jnp.cos(ang), jnp.sin(ang)
    return jnp.tile(cos, (1, 4)), jnp.tile(jnp.concatenate([-sin, sin], axis=1), (1, 2))


def _local_step(x, pos, ada, g1, g2, g3, g4, b_f, sinks, exch, target):
    S = x.shape[0]
    t_fox = _tile(S, 512, LANES) if S >= 1024 else S // 2
    t_fox_fwd = _tile(S, 1024, LANES) if S >= 2048 else S // 2
    shift_m, scale_m, gate_m, shift_f, scale_f, gate_f = [ada[i:i + 1] for i in range(N_ADA)]
    cos_t, sin_t = _rope_tables(pos)
    sinks_p = sinks.reshape(A_KV_HEADS, 4).T.reshape(A_Q_HEADS)
    b_f_pad = jnp.pad(b_f, (0, LANES - B_HEADS)).reshape(1, LANES)

    h1, outs = _pre_norm(x, g1, scale_m, shift_m, "pre_mix_norm", comm=exch.w_in_comm())
    w_a, w_b, w_g = _permute_in_weights(exch.w_in(outs))
    w_perm = jnp.concatenate([w_a, w_b, w_g], axis=1)
    p_a = _mm(h1, w_a, "nn", F32, "proj_a")
    p_b = _mm(h1, w_b, "nn", BF, "proj_b")
    p_g = _mm(h1, w_g, "nn", BF, "proj_g")
    (qk_a,) = _rope([p_a], [640], cos_t, sin_t, "rope_fwd")
    o_a, lse_a = _swa_fwd(qk_a, p_b, 0, sinks_p)
    cum = _fox_gate_fwd(p_a, b_f_pad)
    bq, bk = _fox_prep_fwd(cum, t_fox)
    comm = exch.rest_weights_comm()
    o_b, lse_b, outs = _fox_fwd(p_b, bq, bk, t_fox_fwd, comm=comm)
    w_ba, w_bb, w_out, w_fi, w_fo = exch.rest_weights(outs)
    w_ba_p = w_ba.reshape(A_Q_HEADS, HEAD_DIM, D_MODEL)[_A_ORDER].reshape(512, D_MODEL)
    pa = _mm(o_a, w_ba_p, "nn", BF, "branch_a")
    pb = _mm(o_b, w_bb, "nn", BF, "branch_b")
    merged = _merge_fwd(p_g, pa, pb)
    y1 = _mm(merged, w_out, "nn", BF, "out_proj")
    x2, h2 = _post_pre(x, y1, g2, gate_m, g3, scale_f, shift_f)
    gu = _mm(h2, w_fi, "nn", BF, "ffn_in")
    act = _swiglu_fwd(gu)
    y2 = _mm(act, w_fo, "nn", BF, "ffn_out")
    d_out, d_y2, st_f = _final(x2, y2, g4, gate_f, target)

    d_act = _mm(d_y2, w_fo, "nt", BF, "ffn_out_dx")
    row_pieces = lambda pair: tuple(t.reshape(N_CHIP, t.shape[0] // N_CHIP, t.shape[1]) for t in pair)
    dw_fo = row_pieces(_mm(act, d_y2, "tn", F32, "ffn_out_dw", twin=True))
    d_gu = _swiglu_bwd(d_act, gu)
    d_h2 = _mm(d_gu, w_fi, "nt", BF, "ffn_in_dx")
    dw_fi = _mm(h2, d_gu, "tn", F32, "ffn_in_dw", col_pieces=N_CHIP, twin=True)
    d_x2, d_y1, st_m = _mid_bwd(d_h2, x2, d_out, y1, g3, scale_f, g2, gate_m)
    d_merged = _mm(d_y1, w_out, "nt", BF, "out_proj_dx")
    dw_out = row_pieces(_mm(merged, d_y1, "tn", F32, "out_proj_dw", twin=True))
    d_pa, d_pb, d_ga, d_gb = _merge_bwd(d_merged, p_g, pa, pb)
    d_oa = _mm(d_pa, w_ba_p, "nt", F32, "branch_a_dx")
    dw_ba_p = _mm(o_a, d_pa, "tn", F32, "branch_a_dw", col_pieces=N_CHIP, twin=True)
    d_ob = _mm(d_pb, w_bb, "nt", F32, "branch_b_dx")
    dw_bb = _mm(o_b, d_pb, "tn", F32, "branch_b_dw", col_pieces=N_CHIP, twin=True)
    head_rows = lambda t: t.reshape(N_CHIP, A_Q_HEADS, HEAD_DIM, -1)[:, _A_INVERSE].reshape(t.shape)
    dw_ba = tuple(head_rows(t) for t in dw_ba_p)
    comm = exch.swap_comm([dw_ba, dw_bb, dw_out, dw_fi, dw_fo], "early")
    dq_a, dk_a, dv_a, d_sink, outs = _swa_bwd(qk_a, p_b, 0, o_a, d_oa, lse_a, sinks_p, comm=comm)
    comm = exch.swap_done(outs, "early")
    bq_bwd, bdo = _fox_prep_bwd(cum, o_b, d_ob, lse_b, t_fox)
    dq_b, dk_b, dv_b, d_ck, d_cq, outs = _fox_bwd(p_b, d_ob, bq_bwd, bk, bdo, t_fox, comm=comm)
    exch.reduce_done(outs, "early")
    d_qa, d_ka = _rope([dq_a, dk_a], [512, LANES], cos_t, -sin_t, "rope_bwd")
    d_ck_cols = jnp.pad(d_ck.reshape(B_HEADS, S).T, ((0, 0), (0, LANES - B_HEADS)))
    d_f, d_bf = _fox_gate_bwd(d_cq, d_ck_cols, p_a, b_f_pad)
    d_proj = jnp.concatenate([d_qa, d_ka, d_f, dv_a.astype(BF), dq_b.astype(BF), dk_b.astype(BF),
                              dv_b.astype(BF), d_ga, d_gb], axis=1)
    dw_perm = _mm(h1, d_proj, "tn", F32, "proj_dw")
    swap = exch.swap_comm([_shard_slabs(dw_perm)], "late")
    comm = exch.swap_done(_run_comm(swap, "grads_to_sibling_late") if swap else None, "late")
    res = _mm(d_proj, w_perm, "nt", BF, "proj_dx", comm=comm)
    d_h1 = res[0] if comm else res
    exch.reduce_done(res[1] if comm else None, "late")
    grad_x, st_p = _pre_bwd(d_h1, x, d_x2, g1, scale_m)

    d_sinks = d_sink[:, :2, 0].T.reshape(A_Q_HEADS)
    small = jnp.concatenate([
        st_p[0], st_p[1], st_m[3], st_m[0], st_m[1], st_f[0],
        st_p[2], st_m[4], st_m[2], st_f[1],
        st_f[2], d_bf[0, :B_HEADS], d_sinks,
        jnp.zeros((SM_LEN - SM_SINK - A_Q_HEADS,), F32)])
    return grad_x, small


def kernel(x, c, positions, w_ada, b_ada, g_pre_mix, g_post_mix, w_in, b_f, sinks, w_branch_a, w_branch_b, w_out, g_pre_ffn, g_post_ffn, w_ffn_in, w_ffn_out, loss_target, m_w_ada, m_b_ada, m_g_pre_mix, m_g_post_mix, m_w_in, m_b_f, m_sinks, m_w_branch_a, m_w_branch_b, m_w_out, m_g_pre_ffn, m_g_post_ffn, m_w_ffn_in, m_w_ffn_out, v_w_ada, v_b_ada, v_g_pre_mix, v_g_post_mix, v_w_in, v_b_f, v_sinks, v_w_branch_a, v_w_branch_b, v_w_out, v_g_pre_ffn, v_g_post_ffn, v_w_ffn_in, v_w_ffn_out):
    xi, yi, ci = _here()
    chip = 2 * xi + yi
    dev = 2 * chip + ci

    (c_g,) = _allgather8([c.reshape(8, LANES)], "gather_c")
    c_all = c_g.reshape(N_DEV, D_MODEL)
    exch = _Exchange(ci, chip, w_in[0], [w_branch_a[0], w_branch_b[0], w_out[0], w_ffn_in[0], w_ffn_out[0]])

    ada_cols = _mm(c_all, w_ada[0], "nn", F32, "ada_fwd")
    (ada_g,) = _allgather8([ada_cols], "gather_ada")
    ada_mine = lax.dynamic_index_in_dim(ada_g.reshape(N_CHIP, 2, N_DEV, -1)[:, 0], dev, axis=1, keepdims=False)
    ada = (ada_mine.reshape(-1) + b_ada[0]).reshape(N_ADA, D_MODEL)

    grad_x, small = _local_step(
        x[0], positions[0], ada, g_pre_mix, g_post_mix, g_pre_ffn, g_post_ffn, b_f[0], sinks[0],
        exch, loss_target[0])

    (small_g,) = _allgather8([small.reshape(8, SM_LEN // 8)], "gather_small")
    small_all = small_g.reshape(N_DEV, SM_LEN)
    small_tot, loss_row = _small_finalize(small_all)
    loss = loss_row[0, 0]
    d_ada_cols = lax.dynamic_slice_in_dim(small_all[:, :N_ADA * D_MODEL], chip * (N_ADA * D_MODEL // N_CHIP),
                                          N_ADA * D_MODEL // N_CHIP, axis=1)
    g_w_ada = _ada_dw(c_all.T, d_ada_cols)

    joined = _sibling_join(exch.halves["late"] + exch.halves["early"], "grads_join")
    g_w_in, g_w_ba, g_w_bb, g_w_out, g_w_fi, g_w_fo = [j.reshape(2 * j.shape[1], j.shape[2]) for j in joined]

    def small_vec(b_ada_, g1_, g2_, g3_, g4_, b_f_, sinks_):
        return jnp.concatenate([b_ada_[0], g1_[0], g2_[0], g3_[0], g4_[0], jnp.zeros((D_MODEL,), F32),
                                b_f_[0], sinks_[0], jnp.zeros((SM_LEN - SM_SINK - A_Q_HEADS,), F32)]
                               ).reshape(8, SM_LEN // 8)

    sw = small_vec(b_ada, g_pre_mix, g_post_mix, g_pre_ffn, g_post_ffn, b_f, sinks)
    sm = small_vec(m_b_ada, m_g_pre_mix, m_g_post_mix, m_g_pre_ffn, m_g_post_ffn, m_b_f, m_sinks)
    sv = small_vec(v_b_ada, v_g_pre_mix, v_g_post_mix, v_g_pre_ffn, v_g_post_ffn, v_b_f, v_sinks)
    s_upd = [u.reshape(SM_LEN) for u in _adamw(sw, small_tot.reshape(8, SM_LEN // 8), sm, sv, "adamw_small")]
    s_grad = small_tot.reshape(SM_LEN)

    def unpack(vec):
        row = lambda a, n: vec[a:a + n].reshape(1, n)
        return dict(b_ada=row(SM_ADA, N_ADA * D_MODEL), g_pre_mix=row(SM_G, D_MODEL),
                    g_post_mix=row(SM_G + D_MODEL, D_MODEL), g_pre_ffn=row(SM_G + 2 * D_MODEL, D_MODEL),
                    g_post_ffn=row(SM_G + 3 * D_MODEL, D_MODEL), b_f=row(SM_BF, B_HEADS),
                    sinks=row(SM_SINK, A_Q_HEADS))

    big = dict(
        w_ada=(w_ada, g_w_ada, m_w_ada, v_w_ada),
        w_branch_a=(w_branch_a, g_w_ba, m_w_branch_a, v_w_branch_a),
        w_branch_b=(w_branch_b, g_w_bb, m_w_branch_b, v_w_branch_b),
        w_out=(w_out, g_w_out, m_w_out, v_w_out), w_ffn_in=(w_ffn_in, g_w_fi, m_w_ffn_in, v_w_ffn_in),
        w_ffn_out=(w_ffn_out, g_w_fo, m_w_ffn_out, v_w_ffn_out))
    grads, deltas, new_m, new_v = unpack(s_grad), unpack(s_upd[0]), unpack(s_upd[1]), unpack(s_upd[2])
    for n, (w_, g_, m_, v_) in big.items():
        d_, nm_, nv_ = _adamw(w_[0], g_, m_[0], v_[0], "adamw_" + n)
        grads[n], deltas[n], new_m[n], new_v[n] = g_[None], d_[None], nm_[None], nv_[None]
    pad_cols = lambda a: jnp.pad(a, ((0, 0), (0, W_SHARD_PAD - W_SHARD)))
    upd = _adamw(pad_cols(w_in[0]), g_w_in, pad_cols(m_w_in[0]), pad_cols(v_w_in[0]), "adamw_w_in")
    grads["w_in"], deltas["w_in"], new_m["w_in"], new_v["w_in"] = [t[None, :, :W_SHARD] for t in (g_w_in, *upd)]

    names = ["w_ada", "b_ada", "g_pre_mix", "g_post_mix", "w_in", "b_f", "sinks", "w_branch_a", "w_branch_b",
             "w_out", "g_pre_ffn", "g_post_ffn", "w_ffn_in", "w_ffn_out"]
    return (loss, grad_x[None], *[grads[n] for n in names], *[deltas[n] for n in names],
            *[new_m[n] for n in names], *[new_v[n] for n in names])
```

```python
import functools
import math

import numpy as np
import jax
import jax.numpy as jnp
from jax import lax
from jax.experimental import pallas as pl
from jax.experimental.pallas import tpu as pltpu

F32 = jnp.float32
BF = jnp.bfloat16

D_MODEL = 1024
HEAD_DIM = 64
LANES = 128
WINDOW = 128
A_Q_HEADS = 8
A_KV_HEADS = 2
B_HEADS = 8
D_FF = 2816
ROPE_THETA = 10000.0
RMS_EPS = 1e-6
N_ADA = 6
N_DEV = 8
N_CHIP = 4

ADAM_LR = 0.001
ADAM_B1 = 0.9
ADAM_B2 = 0.999
ADAM_EPS = 1e-08
ADAM_WD = 0.01
ADAM_STEP = 10

VMEM_LIMIT = 48 * 1024 * 1024
MESH = pl.DeviceIdType.MESH

A_HEAD_ORDER = (0, 4, 1, 5, 2, 6, 3, 7)

OFF_QA, OFF_KA, OFF_F = 0, 512, 640
W_A = 768
OFF_VA, OFF_QB, OFF_KB, OFF_VB = 0, 128, 640, 1152
W_B = 1664
W_G = 2048
W_PERM = W_A + W_B + W_G
W_SHARD = 1090
W_SHARD_PAD = 1152


def _tile(n, cap, mult=LANES):
    if n <= cap:
        return n
    t = (cap // mult) * mult
    while t >= mult:
        if n % t == 0:
            return t
        t -= mult
    raise ValueError(f"no tile for {n}")


MXU_WIDTH = 256
MM_OPERAND_BYTES = 28 * 1024 * 1024


def _mm_tiles(M, N, K, a_bytes, b_bytes, tm_cap, tn_cap):
    tm = _tile(M, tm_cap)
    try:
        tn = _tile(N, tn_cap, MXU_WIDTH)
    except ValueError:
        tn = _tile(N, tn_cap)
    fits = lambda tk: 2 * tk * (tm * a_bytes + tn * b_bytes) <= MM_OPERAND_BYTES
    tk = K if fits(K) else next(t for t in range(K // LANES * LANES, 0, -LANES) if K % t == 0 and fits(t))
    return tm, tn, tk


def _cparams(*sem):
    return pltpu.CompilerParams(dimension_semantics=sem, vmem_limit_bytes=VMEM_LIMIT)


def _own_refs(refs, comm, n_in, n_out, n_scratch):
    if comm is None:
        return list(refs), None
    return comm.split(refs, n_in, n_out, n_scratch)


def _comm_specs(comm, side):
    if comm is None:
        return []
    return [pl.BlockSpec(memory_space=pl.ANY)] * len(comm.ins if side == "in" else comm.out_shapes)


def _comm_edge(comm, comm_refs, grid, first):
    if comm is None:
        return
    at_edge = None
    for axis, n in enumerate(grid):
        here = pl.program_id(axis) == (0 if first else n - 1)
        at_edge = here if at_edge is None else at_edge & here
    pl.when(at_edge)(lambda: (comm.start if first else comm.finish)(*comm_refs))


def _mm(a, b, mode, out_dtype, name, tm_cap=512, tn_cap=2816, comm=None, col_pieces=1, twin=False):
    if mode == "nn":
        (M, K), (K2, N) = a.shape, b.shape
        dims = (((1,), (0,)), ((), ()))
    elif mode == "nt":
        (M, K), (N, K2) = a.shape, b.shape
        dims = (((1,), (1,)), ((), ()))
    else:
        (K, M), (K2, N) = a.shape, b.shape
        dims = (((0,), (0,)), ((), ()))
    assert K == K2, (a.shape, b.shape, mode)
    tm, tn, tk = _mm_tiles(M, N // col_pieces, K, a.dtype.itemsize, b.dtype.itemsize, tm_cap, tn_cap)
    nk = K // tk
    n_out = 2 if twin else 1
    n_scratch = 1 if nk > 1 else 0
    if mode == "nn":
        a_spec = pl.BlockSpec((tm, tk), lambda i, j, k: (i, k))
        b_spec = pl.BlockSpec((tk, tn), lambda i, j, k: (k, j))
    elif mode == "nt":
        a_spec = pl.BlockSpec((tm, tk), lambda i, j, k: (i, k))
        b_spec = pl.BlockSpec((tn, tk), lambda i, j, k: (j, k))
    else:
        a_spec = pl.BlockSpec((tk, tm), lambda i, j, k: (k, i))
        b_spec = pl.BlockSpec((tk, tn), lambda i, j, k: (k, j))

    grid = (M // tm, N // tn, nk)

    def kern(*refs):
        own, comm_refs = _own_refs(refs, comm, 2, n_out, n_scratch)
        a_ref, b_ref, o_refs = own[0], own[1], own[2:2 + n_out]
        k = pl.program_id(2)
        _comm_edge(comm, comm_refs, grid, first=True)
        part = lax.dot_general(a_ref[...].astype(BF), b_ref[...].astype(BF), dims,
                               preferred_element_type=F32)
        if nk == 1:
            for o_ref in o_refs:
                o_ref[...] = part.astype(o_ref.dtype)
        else:
            acc_ref = own[2 + n_out]

            @pl.when(k == 0)
            def _():
                acc_ref[...] = part

            @pl.when(k > 0)
            def _():
                acc_ref[...] += part

            @pl.when(k == nk - 1)
            def _():
                for o_ref in o_refs:
                    o_ref[...] = acc_ref[...].astype(o_ref.dtype)

        _comm_edge(comm, comm_refs, grid, first=False)

    if col_pieces > 1:
        per = N // col_pieces // tn
        out_spec = pl.BlockSpec((None, tm, tn), lambda i, j, k: (j // per, i, j % per))
        shape = (col_pieces, M, N // col_pieces)
    else:
        out_spec = pl.BlockSpec((tm, tn), lambda i, j, k: (i, j))
        shape = (M, N)
    dtypes = [out_dtype, BF] if twin else [out_dtype]
    res = pl.pallas_call(
        kern, name=name, grid=grid,
        in_specs=[a_spec, b_spec] + _comm_specs(comm, "in"),
        out_specs=[out_spec] * n_out + _comm_specs(comm, "out"),
        out_shape=[jax.ShapeDtypeStruct(shape, d) for d in dtypes] + (comm.out_shapes if comm else []),
        scratch_shapes=[pltpu.VMEM((tm, tn), F32)] * n_scratch + (comm.sem_shapes if comm else []),
        compiler_params=_cparams("parallel", "parallel", "arbitrary"),
    )(a, b, *(comm.ins if comm else []))
    own = res[0] if n_out == 1 else tuple(res[:n_out])
    return (own, res[n_out:]) if comm else own


ROWS = 512


def _row_spec(tm, width=D_MODEL, col=0):
    return pl.BlockSpec((tm, width), lambda i: (i, col))


def _vec_spec(width=D_MODEL):
    return pl.BlockSpec((1, width), lambda i: (0, 0))


def _rms(x):
    return lax.rsqrt(jnp.mean(x * x, axis=-1, keepdims=True) + RMS_EPS)


def _colsum(x):
    return jnp.sum(x, axis=0, keepdims=True)


def _norm_bwd(d_xn, xn, r):
    return r * (d_xn - xn * jnp.mean(d_xn * xn, axis=-1, keepdims=True))


def _pre_norm(x, g, scale, shift, name, comm=None):
    S = x.shape[0]
    tm = _tile(S, ROWS, 8)
    grid = (S // tm,)

    def kern(*refs):
        (x_ref, g_ref, sc_ref, sh_ref, h_ref), comm_refs = _own_refs(refs, comm, 4, 1, 0)
        _comm_edge(comm, comm_refs, grid, first=True)
        xf = x_ref[...]
        y = xf * _rms(xf) * g_ref[...]
        h_ref[...] = (y * (1.0 + sc_ref[...]) + sh_ref[...]).astype(BF)
        _comm_edge(comm, comm_refs, grid, first=False)

    res = pl.pallas_call(
        kern, name=name, grid=grid,
        in_specs=[_row_spec(tm), _vec_spec(), _vec_spec(), _vec_spec()] + _comm_specs(comm, "in"),
        out_specs=[_row_spec(tm)] + _comm_specs(comm, "out"),
        out_shape=[jax.ShapeDtypeStruct((S, D_MODEL), BF)] + (comm.out_shapes if comm else []),
        scratch_shapes=comm.sem_shapes if comm else [],
        compiler_params=_cparams("arbitrary"),
    )(x, g, scale, shift, *(comm.ins if comm else []))
    return res[0], res[1:]


def _post_pre(x, y1, g2, gate_m, g3, scale_f, shift_f):
    S = x.shape[0]
    tm = _tile(S, ROWS, 8)

    def kern(x_ref, y_ref, g2_ref, gm_ref, g3_ref, sc_ref, sh_ref, x2_ref, h2_ref):
        y = y_ref[...].astype(F32)
        n2 = y * _rms(y) * g2_ref[...]
        x2 = x_ref[...] + gm_ref[...] * n2
        x2_ref[...] = x2
        n3 = x2 * _rms(x2) * g3_ref[...]
        h2_ref[...] = (n3 * (1.0 + sc_ref[...]) + sh_ref[...]).astype(BF)

    return pl.pallas_call(
        kern, name="post_mix_pre_ffn", grid=(S // tm,),
        in_specs=[_row_spec(tm), _row_spec(tm)] + [_vec_spec()] * 5,
        out_specs=[_row_spec(tm), _row_spec(tm)],
        out_shape=[jax.ShapeDtypeStruct((S, D_MODEL), F32), jax.ShapeDtypeStruct((S, D_MODEL), BF)],
        compiler_params=_cparams("parallel"),
    )(x, y1, g2, gate_m, g3, scale_f, shift_f)


def _stats_spec():
    return pl.BlockSpec((8, D_MODEL), lambda i: (0, 0))


def _final(x2, y2, g4, gate_f, target):
    S = x2.shape[0]
    tm = _tile(S, ROWS, 8)

    def kern(x2_ref, y_ref, g4_ref, gf_ref, t_ref, dout_ref, dy_ref, st_ref):
        @pl.when(pl.program_id(0) == 0)
        def _():
            st_ref[...] = jnp.zeros_like(st_ref)

        y = y_ref[...].astype(F32)
        r = _rms(y)
        yn = y * r
        n4 = yn * g4_ref[...]
        diff = x2_ref[...] + gf_ref[...] * n4 - t_ref[...]
        d_out = diff / D_MODEL
        dout_ref[...] = d_out
        dn = d_out * gf_ref[...]
        dy_ref[...] = _norm_bwd(dn * g4_ref[...], yn, r).astype(BF)
        st_ref[0:1, :] += _colsum(d_out * n4)
        st_ref[1:2, :] += _colsum(dn * yn)
        st_ref[2:3, :] += _colsum(diff * diff)

    return pl.pallas_call(
        kern, name="final_loss", grid=(S // tm,),
        in_specs=[_row_spec(tm), _row_spec(tm), _vec_spec(), _vec_spec(), _row_spec(tm)],
        out_specs=[_row_spec(tm), _row_spec(tm), _stats_spec()],
        out_shape=[jax.ShapeDtypeStruct((S, D_MODEL), F32), jax.ShapeDtypeStruct((S, D_MODEL), BF),
                   jax.ShapeDtypeStruct((8, D_MODEL), F32)],
        compiler_params=_cparams("arbitrary"),
    )(x2, y2, g4, gate_f, target)


def _mid_bwd(d_h2, x2, d_out, y1, g3, scale_f, g2, gate_m):
    S = x2.shape[0]
    tm = _tile(S, ROWS, 8)

    def kern(dh_ref, x2_ref, dout_ref, y_ref, g3_ref, sc_ref, g2_ref, gm_ref, dx2_ref, dy_ref, st_ref):
        @pl.when(pl.program_id(0) == 0)
        def _():
            st_ref[...] = jnp.zeros_like(st_ref)

        dh = dh_ref[...].astype(F32)
        x2 = x2_ref[...]
        r3 = _rms(x2)
        xn = x2 * r3
        one_sc = 1.0 + sc_ref[...]
        d_x2 = dout_ref[...] + _norm_bwd(dh * one_sc * g3_ref[...], xn, r3)
        dx2_ref[...] = d_x2
        y = y_ref[...].astype(F32)
        r2 = _rms(y)
        yn = y * r2
        dn = d_x2 * gm_ref[...]
        dy_ref[...] = _norm_bwd(dn * g2_ref[...], yn, r2).astype(BF)
        st_ref[0:1, :] += _colsum(dh)
        st_ref[1:2, :] += _colsum(dh * (xn * g3_ref[...]))
        st_ref[2:3, :] += _colsum(dh * one_sc * xn)
        st_ref[3:4, :] += _colsum(d_x2 * (yn * g2_ref[...]))
        st_ref[4:5, :] += _colsum(dn * yn)

    return pl.pallas_call(
        kern, name="mid_bwd", grid=(S // tm,),
        in_specs=[_row_spec(tm)] * 4 + [_vec_spec()] * 4,
        out_specs=[_row_spec(tm), _row_spec(tm), _stats_spec()],
        out_shape=[jax.ShapeDtypeStruct((S, D_MODEL), F32), jax.ShapeDtypeStruct((S, D_MODEL), BF),
                   jax.ShapeDtypeStruct((8, D_MODEL), F32)],
        compiler_params=_cparams("arbitrary"),
    )(d_h2, x2, d_out, y1, g3, scale_f, g2, gate_m)


def _pre_bwd(d_h1, x, d_x2, g1, scale_m):
    S = x.shape[0]
    tm = _tile(S, ROWS, 8)

    def kern(dh_ref, x_ref, dx2_ref, g_ref, sc_ref, gx_ref, st_ref):
        @pl.when(pl.program_id(0) == 0)
        def _():
            st_ref[...] = jnp.zeros_like(st_ref)

        dh = dh_ref[...].astype(F32)
        xf = x_ref[...]
        r = _rms(xf)
        xn = xf * r
        one_sc = 1.0 + sc_ref[...]
        gx_ref[...] = dx2_ref[...] + _norm_bwd(dh * one_sc * g_ref[...], xn, r)
        st_ref[0:1, :] += _colsum(dh)
        st_ref[1:2, :] += _colsum(dh * (xn * g_ref[...]))
        st_ref[2:3, :] += _colsum(dh * one_sc * xn)

    return pl.pallas_call(
        kern, name="pre_mix_bwd", grid=(S // tm,),
        in_specs=[_row_spec(tm)] * 3 + [_vec_spec()] * 2,
        out_specs=[_row_spec(tm), _stats_spec()],
        out_shape=[jax.ShapeDtypeStruct((S, D_MODEL), F32), jax.ShapeDtypeStruct((8, D_MODEL), F32)],
        compiler_params=_cparams("arbitrary"),
    )(d_h1, x, d_x2, g1, scale_m)


def _rope(xs, widths, cos_t, sin_t, name):
    S = xs[0].shape[0]
    tm = _tile(S, 512, 8)
    n = len(xs)

    def kern(*refs):
        cos = refs[n][...]
        sin = refs[n + 1][...]
        first = (lax.broadcasted_iota(jnp.int32, cos.shape, 1) % HEAD_DIM) < HEAD_DIM // 2
        for x_ref, o_ref, w in zip(refs[:n], refs[n + 2:], widths):
            for c0 in range(0, w, LANES):
                v = x_ref[:, c0:c0 + LANES]
                partner = jnp.where(first, pltpu.roll(v, LANES - HEAD_DIM // 2, 1),
                                    pltpu.roll(v, HEAD_DIM // 2, 1))
                o_ref[:, c0:c0 + LANES] = (v * cos + partner * sin).astype(BF)

    return pl.pallas_call(
        kern, name=name, grid=(S // tm,),
        in_specs=[_row_spec(tm, w) for w in widths] + [_row_spec(tm, LANES)] * 2,
        out_specs=[_row_spec(tm, w) for w in widths],
        out_shape=[jax.ShapeDtypeStruct((S, w), BF) for w in widths],
        compiler_params=_cparams("parallel"),
    )(*xs, cos_t, sin_t)


def _merge_fwd(pg, pa, pb):
    S = pa.shape[0]
    tm = _tile(S, ROWS, 8)

    def kern(ga_ref, gb_ref, pa_ref, pb_ref, o_ref):
        ga = jax.nn.sigmoid(ga_ref[...].astype(F32))
        gb = jax.nn.sigmoid(gb_ref[...].astype(F32))
        o_ref[...] = (ga * pa_ref[...].astype(F32) + gb * pb_ref[...].astype(F32)).astype(BF)

    return pl.pallas_call(
        kern, name="merge_fwd", grid=(S // tm,),
        in_specs=[_row_spec(tm, col=0), _row_spec(tm, col=1), _row_spec(tm), _row_spec(tm)],
        out_specs=_row_spec(tm),
        out_shape=jax.ShapeDtypeStruct((S, D_MODEL), BF),
        compiler_params=_cparams("parallel"),
    )(pg, pg, pa, pb)


def _merge_bwd(d_merged, pg, pa, pb):
    S = pa.shape[0]
    tm = _tile(S, ROWS, 8)

    def kern(dm_ref, ga_ref, gb_ref, pa_ref, pb_ref, dpa_ref, dpb_ref, dga_ref, dgb_ref):
        dm = dm_ref[...].astype(F32)
        ga = jax.nn.sigmoid(ga_ref[...].astype(F32))
        gb = jax.nn.sigmoid(gb_ref[...].astype(F32))
        dpa_ref[...] = (dm * ga).astype(BF)
        dpb_ref[...] = (dm * gb).astype(BF)
        dga_ref[...] = (dm * pa_ref[...].astype(F32) * ga * (1.0 - ga)).astype(BF)
        dgb_ref[...] = (dm * pb_ref[...].astype(F32) * gb * (1.0 - gb)).astype(BF)

    bf_out = jax.ShapeDtypeStruct((S, D_MODEL), BF)
    return pl.pallas_call(
        kern, name="merge_bwd", grid=(S // tm,),
        in_specs=[_row_spec(tm), _row_spec(tm, col=0), _row_spec(tm, col=1), _row_spec(tm), _row_spec(tm)],
        out_specs=[_row_spec(tm)] * 4,
        out_shape=[bf_out] * 4,
        compiler_params=_cparams("parallel"),
    )(d_merged, pg, pg, pa, pb)


def _swiglu_fwd(gu):
    S = gu.shape[0]
    tm = _tile(S, ROWS, 8)
    tc = _tile(D_FF, 1408)
    nc = D_FF // tc

    def kern(g_ref, u_ref, o_ref):
        g = g_ref[...].astype(F32)
        o_ref[...] = (g * jax.nn.sigmoid(g) * u_ref[...].astype(F32)).astype(BF)

    return pl.pallas_call(
        kern, name="swiglu_fwd", grid=(S // tm, nc),
        in_specs=[pl.BlockSpec((tm, tc), lambda i, j: (i, j)),
                  pl.BlockSpec((tm, tc), lambda i, j: (i, j + nc))],
        out_specs=pl.BlockSpec((tm, tc), lambda i, j: (i, j)),
        out_shape=jax.ShapeDtypeStruct((S, D_FF), BF),
        compiler_params=_cparams("parallel", "parallel"),
    )(gu, gu)


def _swiglu_bwd(d_act, gu):
    S = gu.shape[0]
    tm = _tile(S, ROWS // 2, 8)

    def kern(da_ref, g_ref, u_ref, o_ref):
        g = g_ref[...].astype(F32)
        u = u_ref[...].astype(F32)
        da = da_ref[...].astype(F32)
        sg = jax.nn.sigmoid(g)
        o_ref[:, :D_FF] = (da * u * (sg * (1.0 + g * (1.0 - sg)))).astype(BF)
        o_ref[:, D_FF:] = (da * (g * sg)).astype(BF)

    return pl.pallas_call(
        kern, name="swiglu_bwd", grid=(S // tm,),
        in_specs=[_row_spec(tm, D_FF), _row_spec(tm, D_FF, 0), _row_spec(tm, D_FF, 1)],
        out_specs=_row_spec(tm, 2 * D_FF),
        out_shape=jax.ShapeDtypeStruct((S, 2 * D_FF), BF),
        compiler_params=_cparams("parallel"),
    )(d_act, gu, gu)


def _split3(x):
    hi = x.astype(BF)
    r1 = x - hi.astype(F32)
    mid = r1.astype(BF)
    lo = (r1 - mid.astype(F32)).astype(BF)
    return hi, mid, lo


def _tri_dot(tri, x):
    return sum(jnp.dot(tri, part, preferred_element_type=F32) for part in _split3(x))


def _log_sigmoid(z):
    return jnp.minimum(z, 0.0) - jnp.log(1.0 + jnp.exp(-jnp.abs(z)))


def _fox_gate_fwd(pa, b_f_pad):
    S = pa.shape[0]
    T = _tile(S, 512, 8)
    f_col = OFF_F // LANES

    def kern(z_ref, b_ref, cum_ref, carry_ref):
        @pl.when(pl.program_id(0) == 0)
        def _():
            carry_ref[...] = jnp.zeros_like(carry_ref)

        log_f = _log_sigmoid(z_ref[...] + b_ref[...])
        row = lax.broadcasted_iota(jnp.int32, (T, T), 0)
        col = lax.broadcasted_iota(jnp.int32, (T, T), 1)
        tri = (col <= row).astype(BF)
        cum = _tri_dot(tri, log_f) + carry_ref[...]
        cum_ref[...] = cum
        carry_ref[...] = cum[T - 1:T, :]

    return pl.pallas_call(
        kern, name="fox_gate_fwd", grid=(S // T,),
        in_specs=[_row_spec(T, LANES, f_col), _vec_spec(LANES)],
        out_specs=_row_spec(T, LANES),
        out_shape=jax.ShapeDtypeStruct((S, LANES), F32),
        scratch_shapes=[pltpu.VMEM((1, LANES), F32)],
        compiler_params=_cparams("arbitrary"),
    )(pa, b_f_pad)


def _fox_gate_bwd(rowsum_ds, colsum_ds, pa, b_f_pad):
    S = pa.shape[0]
    T = _tile(S, 512, 8)
    nb = S // T
    f_col = OFF_F // LANES

    def kern(dr_ref, dc_ref, z_ref, b_ref, df_ref, dbf_ref, carry_ref):
        @pl.when(pl.program_id(0) == 0)
        def _():
            carry_ref[...] = jnp.zeros_like(carry_ref)
            dbf_ref[...] = jnp.zeros_like(dbf_ref)

        row = lax.broadcasted_iota(jnp.int32, (T, T), 0)
        col = lax.broadcasted_iota(jnp.int32, (T, T), 1)
        tri = (col >= row).astype(BF)
        rev = _tri_dot(tri, dr_ref[...] - dc_ref[...]) + carry_ref[...]
        carry_ref[...] = rev[0:1, :]
        z = z_ref[...] + b_ref[...]
        lane = lax.broadcasted_iota(jnp.int32, (T, LANES), 1)
        d_z = jnp.where(lane < B_HEADS, rev * jax.nn.sigmoid(-z), 0.0)
        df_ref[...] = d_z.astype(BF)
        dbf_ref[0:1, :] += _colsum(d_z)

    return pl.pallas_call(
        kern, name="fox_gate_bwd", grid=(nb,),
        in_specs=[pl.BlockSpec((T, LANES), lambda i: (nb - 1 - i, 0)),
                  pl.BlockSpec((T, LANES), lambda i: (nb - 1 - i, 0)),
                  pl.BlockSpec((T, LANES), lambda i: (nb - 1 - i, f_col)),
                  _vec_spec(LANES)],
        out_specs=[pl.BlockSpec((T, LANES), lambda i: (nb - 1 - i, 0)),
                   pl.BlockSpec((8, LANES), lambda i: (0, 0))],
        out_shape=[jax.ShapeDtypeStruct((S, LANES), BF), jax.ShapeDtypeStruct((8, LANES), F32)],
        scratch_shapes=[pltpu.VMEM((1, LANES), F32)],
        compiler_params=_cparams("arbitrary"),
    )(rowsum_ds, colsum_ds, pa, b_f_pad)


NEG_INF = float("-inf")
QK_SCALE = 1.0 / math.sqrt(HEAD_DIM)


def _half_mask(shape, half):
    lane = lax.broadcasted_iota(jnp.int32, shape, 1)
    return (lane < HEAD_DIM) if half == 0 else (lane >= HEAD_DIM)


def _bias_block(shape, terms, term_off, ones_lo, ones_hi):
    l64 = lax.broadcasted_iota(jnp.int32, shape, 1) & (HEAD_DIM - 1)
    out = jnp.where((l64 >= ones_lo) & (l64 < ones_hi), 1.0, 0.0)
    for t, term in enumerate(terms):
        out = jnp.where(l64 == term_off + t, term.astype(F32), out)
    return out


def _head_column(block, head):
    lane = lax.broadcasted_iota(jnp.int32, block.shape, 1)
    return jnp.sum(jnp.where(lane == head, block, 0.0), axis=1, keepdims=True)


def _crossed(shape, first, second):
    return jnp.where(_half_mask(shape, 0), second, first)


def _fox_prep_fwd(cum, T):
    S = cum.shape[0]
    shape = (T, LANES)

    def kern(c_ref, bq_ref, bk_ref):
        p_id = pl.program_id(0)
        cum_blk = c_ref[...]
        c3 = _split3(_crossed(shape, _head_column(cum_blk, 2 * p_id), _head_column(cum_blk, 2 * p_id + 1)))
        bq_ref[...] = _bias_block(shape, c3, 0, 3, 6).astype(BF)
        bk_ref[...] = _bias_block(shape, [-t.astype(F32) for t in c3], 3, 0, 3).astype(BF)

    out_spec = pl.BlockSpec((None, T, LANES), lambda p, i: (p, i, 0))
    out_shape = jax.ShapeDtypeStruct((B_HEADS // 2, S, LANES), BF)
    return pl.pallas_call(
        kern, name="fox_prep_fwd", grid=(B_HEADS // 2, S // T),
        in_specs=[pl.BlockSpec((T, LANES), lambda p, i: (i, 0))],
        out_specs=[out_spec, out_spec], out_shape=[out_shape, out_shape],
        compiler_params=_cparams("parallel", "parallel"),
    )(cum)


def _fox_fwd(p_b, bq, bk, T, comm=None):
    S = p_b.shape[0]
    nq = S // T
    n_pairs = B_HEADS // 2
    grid = (n_pairs, nq)

    def kern(*refs):
        (q_ref, k_ref, v_ref, bq_ref, bk_ref, o_ref, lse_ref), comm_refs = _own_refs(refs, comm, 5, 2, 0)
        _comm_edge(comm, comm_refs, grid, first=True)
        i = pl.program_id(1)
        rowcol = lax.broadcasted_iota(jnp.int32, (T, T), 0) - lax.broadcasted_iota(jnp.int32, (T, T), 1)
        hms = (_half_mask((T, LANES), 0), _half_mask((T, LANES), 1))
        q_scaled = (q_ref[...].astype(F32) * QK_SCALE).astype(BF)
        bq_blk = bq_ref[...]
        qs = [jnp.where(hms[h], q_scaled, bq_blk) for h in (0, 1)]

        def step(j, carry, masked):
            rows = pl.ds(pl.multiple_of(j * T, T), T)
            kj, bkj, vj = k_ref[rows, :], bk_ref[rows, :], v_ref[rows, :]
            new = []
            for half in (0, 1):
                m, l, acc = carry[half]
                s = lax.dot_general(qs[half], jnp.where(hms[half], kj, bkj), (((1,), (1,)), ((), ())),
                                    preferred_element_type=F32)
                if masked:
                    s = jnp.where(rowcol >= 0, s, NEG_INF)
                m_new = jnp.maximum(m, jnp.max(s, axis=1, keepdims=True))
                alpha = jnp.exp(m - m_new)
                p = jnp.exp(s - m_new)
                l_new = alpha * l + jnp.sum(p, axis=1, keepdims=True)
                acc_new = alpha * acc + jnp.dot(p.astype(BF), vj, preferred_element_type=F32)
                new.append((m_new, l_new, acc_new))
            return tuple(new)

        one = (jnp.full((T, 1), NEG_INF, F32), jnp.zeros((T, 1), F32), jnp.zeros((T, LANES), F32))
        carry = lax.fori_loop(0, i, functools.partial(step, masked=False), (one, one))
        (m0, l0, acc0), (m1, l1, acc1) = step(i, carry, True)
        hm0 = _half_mask((T, LANES), 0)
        o_ref[...] = jnp.where(hm0, acc0 / l0, acc1 / l1)
        lse_ref[...] = jnp.where(hm0, m0 + jnp.log(l0), m1 + jnp.log(l1))
        _comm_edge(comm, comm_refs, grid, first=False)

    out_spec = pl.BlockSpec((T, LANES), lambda p, i: (i, p))
    res = pl.pallas_call(
        kern, name="fox_fwd", grid=grid,
        in_specs=[pl.BlockSpec((T, LANES), lambda p, i: (i, OFF_QB // LANES + p)),
                  pl.BlockSpec((S, LANES), lambda p, i: (0, OFF_KB // LANES + p)),
                  pl.BlockSpec((S, LANES), lambda p, i: (0, OFF_VB // LANES + p)),
                  pl.BlockSpec((None, T, LANES), lambda p, i: (p, i, 0)),
                  pl.BlockSpec((None, S, LANES), lambda p, i: (p, 0, 0))] + _comm_specs(comm, "in"),
        out_specs=[out_spec, out_spec] + _comm_specs(comm, "out"),
        out_shape=[jax.ShapeDtypeStruct((S, n_pairs * LANES), F32)] * 2 + (comm.out_shapes if comm else []),
        scratch_shapes=comm.sem_shapes if comm else [],
        compiler_params=_cparams("arbitrary", "arbitrary"),
    )(p_b, p_b, p_b, bq, bk, *(comm.ins if comm else []))
    return res[0], res[1], res[2:]


def _fox_prep_bwd(cum, o, do, lse, T):
    S = o.shape[0]
    shape = (T, LANES)

    def kern(c_ref, o_ref, do_ref, lse_ref, bq_ref, bdo_ref):
        p_id = pl.program_id(0)
        cum_blk = c_ref[...]
        cq = _crossed(shape, _head_column(cum_blk, 2 * p_id), _head_column(cum_blk, 2 * p_id + 1))
        b3 = _split3(cq - pltpu.roll(lse_ref[...], HEAD_DIM, 1))
        bq_ref[...] = _bias_block(shape, b3, 0, 3, 6).astype(BF)
        dd = do_ref[...] * o_ref[...]
        delta = [jnp.sum(jnp.where(_half_mask(shape, h), dd, 0.0), axis=1, keepdims=True) for h in (0, 1)]
        d3 = _split3(-_crossed(shape, delta[0], delta[1]))
        bdo_ref[...] = _bias_block(shape, d3, 0, 0, 0).astype(BF)

    block = pl.BlockSpec((None, T, LANES), lambda p, i: (p, i, 0))
    tile = pl.BlockSpec((T, LANES), lambda p, i: (i, p))
    out_shape = jax.ShapeDtypeStruct((B_HEADS // 2, S, LANES), BF)
    return pl.pallas_call(
        kern, name="fox_prep_bwd", grid=(B_HEADS // 2, S // T),
        in_specs=[pl.BlockSpec((T, LANES), lambda p, i: (i, 0)), tile, tile, tile],
        out_specs=[block, block], out_shape=[out_shape, out_shape],
        compiler_params=_cparams("parallel", "parallel"),
    )(cum, o, do, lse)


def _fox_bwd(p_b, do, bq, bk, bdo, T, comm=None):
    S = p_b.shape[0]
    n_pairs = B_HEADS // 2
    nq = S // T
    grid = (n_pairs,)

    def kern(*refs):
        own, comm_refs = _own_refs(refs, comm, 7, 5, 0)
        q_ref, k_ref, v_ref, do_ref, bq_ref, bk_ref, bdo_ref, dq_ref, dk_ref, dv_ref, dck_ref, dcq_ref = own
        _comm_edge(comm, comm_refs, grid, first=True)
        p_id = pl.program_id(0)
        rowcol = lax.broadcasted_iota(jnp.int32, (T, T), 0) - lax.broadcasted_iota(jnp.int32, (T, T), 1)
        lane = lax.broadcasted_iota(jnp.int32, (T, LANES), 1)
        dk_ref[...] = jnp.zeros_like(dk_ref)
        dv_ref[...] = jnp.zeros_like(dv_ref)
        dck_ref[...] = jnp.zeros_like(dck_ref)

        @pl.when(p_id == 0)
        def _():
            dcq_ref[...] = jnp.zeros_like(dcq_ref)

        hms = (_half_mask((T, LANES), 0), _half_mask((T, LANES), 1))
        v_ones = _bias_block((T, LANES), [], 0, 0, 3).astype(BF)

        def outer(i, carry):
            qrows = pl.ds(pl.multiple_of(i * T, T), T)
            q_scaled = (q_ref[qrows, :].astype(F32) * QK_SCALE).astype(BF)
            do_b = do_ref[qrows, :].astype(BF)
            bq_i, bdo_i = bq_ref[qrows, :], bdo_ref[qrows, :]
            qa = [jnp.where(hms[h], q_scaled, bq_i) for h in (0, 1)]
            doa = [jnp.where(hms[h], do_b, bdo_i) for h in (0, 1)]
            q_own = [jnp.where(hms[h], q_scaled, 0) for h in (0, 1)]
            do_own = [jnp.where(hms[h], do_b, 0) for h in (0, 1)]

            def inner(j, carry_in, masked):
                krows = pl.ds(pl.multiple_of(j * T, T), T)
                kj, bkj, vj = k_ref[krows, :], bk_ref[krows, :], v_ref[krows, :]
                dv_add, dk_add, new = 0.0, 0.0, []
                for half in (0, 1):
                    dq, rs = carry_in[half]
                    ka = jnp.where(hms[half], kj, bkj)
                    s = lax.dot_general(qa[half], ka, (((1,), (1,)), ((), ())), preferred_element_type=F32)
                    if masked:
                        s = jnp.where(rowcol >= 0, s, NEG_INF)
                    p = jnp.exp(s)
                    ds = p * lax.dot_general(doa[half], jnp.where(hms[half], vj, v_ones),
                                             (((1,), (1,)), ((), ())), preferred_element_type=F32)
                    ds_b = ds.astype(BF)
                    dv_add = dv_add + lax.dot_general(p.astype(BF), do_own[half], (((0,), (0,)), ((), ())),
                                                      preferred_element_type=F32)
                    dk_add = dk_add + lax.dot_general(ds_b, q_own[half], (((0,), (0,)), ((), ())),
                                                      preferred_element_type=F32)
                    dck_ref[half:half + 1, krows] += jnp.sum(ds, axis=0, keepdims=True)
                    new.append((dq + jnp.dot(ds_b, jnp.where(hms[half], kj, 0), preferred_element_type=F32),
                                rs + jnp.sum(ds, axis=1, keepdims=True)))
                dv_ref[krows, :] += dv_add
                dk_ref[krows, :] += dk_add
                return tuple(new)

            one = (jnp.zeros((T, LANES), F32), jnp.zeros((T, 1), F32))
            carry_in = lax.fori_loop(0, i, functools.partial(inner, masked=False), (one, one))
            (dq0, rs0), (dq1, rs1) = inner(i, carry_in, True)
            dq_ref[qrows, :] = (dq0 + dq1) * QK_SCALE
            dcq_ref[qrows, :] = jnp.where(lane == 2 * p_id, rs0, jnp.where(lane == 2 * p_id + 1, rs1,
                                                                             dcq_ref[qrows, :]))
            return carry

        lax.fori_loop(0, nq, outer, 0)
        _comm_edge(comm, comm_refs, grid, first=False)

    block = pl.BlockSpec((None, S, LANES), lambda p: (p, 0, 0))
    pair = pl.BlockSpec((S, LANES), lambda p: (0, p))
    slab = lambda off: pl.BlockSpec((S, LANES), lambda p: (0, off // LANES + p))
    wide = jax.ShapeDtypeStruct((S, n_pairs * LANES), F32)
    res = pl.pallas_call(
        kern, name="fox_bwd", grid=grid,
        in_specs=[slab(OFF_QB), slab(OFF_KB), slab(OFF_VB), pair, block, block, block]
        + _comm_specs(comm, "in"),
        out_specs=[pair, pair, pair, pl.BlockSpec((None, 2, S), lambda p: (p, 0, 0)),
                   pl.BlockSpec((S, LANES), lambda p: (0, 0))] + _comm_specs(comm, "out"),
        out_shape=[wide, wide, wide, jax.ShapeDtypeStruct((n_pairs, 2, S), F32),
                   jax.ShapeDtypeStruct((S, LANES), F32)] + (comm.out_shapes if comm else []),
        scratch_shapes=comm.sem_shapes if comm else [],
        compiler_params=_cparams("arbitrary"),
    )(p_b, p_b, p_b, do, bq, bk, bdo, *(comm.ins if comm else []))
    return (*res[:5], res[5:])


SWA_TQ = 256
SWA_SUB = 8


def _swa_window(i, tq):
    start = pl.multiple_of(jnp.maximum(i * tq - WINDOW, 0), LANES)
    return start, i * tq - start


def _swa_valid(offset, tq):
    rel = offset + lax.broadcasted_iota(jnp.int32, (tq, tq + WINDOW), 0) \
        - lax.broadcasted_iota(jnp.int32, (tq, tq + WINDOW), 1)
    return (rel >= 0) & (rel < WINDOW)


def _swa_fwd(qk, v_arr, v_col, sinks):
    S = qk.shape[0]
    tq = min(SWA_TQ, S - WINDOW)
    sub = min(SWA_SUB, S // tq)
    win = tq + WINDOW

    def kern(q_ref, k_ref, v_ref, sink_ref, o_ref, lse_ref):
        p_id, i = pl.program_id(0), pl.program_id(1)
        hm0 = _half_mask((tq, LANES), 0)
        for t in range(sub):
            rows = slice(t * tq, (t + 1) * tq)
            start, offset = _swa_window(i * sub + t, tq)
            kw = k_ref[pl.ds(start, win), :]
            vw = v_ref[pl.ds(start, win), :].astype(BF)
            valid = _swa_valid(offset, tq)
            q = q_ref[rows, :]
            outs, lses = [], []
            for half in (0, 1):
                hm = _half_mask((tq, LANES), half)
                qh = (jnp.where(hm, q, 0).astype(F32) * QK_SCALE).astype(BF)
                s = lax.dot_general(qh, kw, (((1,), (1,)), ((), ())), preferred_element_type=F32)
                s = jnp.where(valid, s, NEG_INF)
                sink = sink_ref[2 * p_id + half]
                m = jnp.maximum(jnp.max(s, axis=1, keepdims=True), sink)
                p = jnp.exp(s - m)
                denom = jnp.sum(p, axis=1, keepdims=True) + jnp.exp(sink - m)
                outs.append(jnp.dot(p.astype(BF), vw, preferred_element_type=F32) / denom)
                lses.append(m + jnp.log(denom))
            o_ref[rows, :] = jnp.where(hm0, outs[0], outs[1])
            lse_ref[rows, :] = jnp.where(hm0, lses[0], lses[1])

    tile = pl.BlockSpec((sub * tq, LANES), lambda p, i: (i, p))
    return pl.pallas_call(
        kern, name="swa_fwd", grid=(A_Q_HEADS // 2, S // (sub * tq)),
        in_specs=[tile, pl.BlockSpec((S, LANES), lambda p, i: (0, A_Q_HEADS // 2)),
                  pl.BlockSpec((S, LANES), lambda p, i: (0, v_col)),
                  pl.BlockSpec(memory_space=pltpu.SMEM)],
        out_specs=[tile, tile],
        out_shape=[jax.ShapeDtypeStruct((S, A_Q_HEADS * HEAD_DIM), F32)] * 2,
        compiler_params=_cparams("parallel", "arbitrary"),
    )(qk, qk, v_arr, sinks)


def _swa_bwd(qk, v_arr, v_col, o_arr, do_arr, lse_arr, sinks, comm=None):
    S = qk.shape[0]
    tq = min(SWA_TQ, S - WINDOW)
    sub = min(SWA_SUB, S // tq)
    win = tq + WINDOW
    n_pairs = A_Q_HEADS // 2
    grid = (n_pairs, S // (sub * tq))

    def kern(*refs):
        own, comm_refs = _own_refs(refs, comm, 7, 4, 0)
        q_ref, k_ref, v_ref, o_ref, do_ref, lse_ref, sink_ref, dq_ref, dk_ref, dv_ref, dsink_ref = own
        _comm_edge(comm, comm_refs, grid, first=True)
        p_id, i = pl.program_id(0), pl.program_id(1)

        @pl.when((p_id == 0) & (i == 0))
        def _():
            dk_ref[...] = jnp.zeros_like(dk_ref)
            dv_ref[...] = jnp.zeros_like(dv_ref)

        @pl.when(i == 0)
        def _():
            dsink_ref[...] = jnp.zeros_like(dsink_ref)

        for t in range(sub):
            rows = slice(t * tq, (t + 1) * tq)
            start, offset = _swa_window(i * sub + t, tq)
            wrows = pl.ds(start, win)
            kw = k_ref[wrows, :]
            vw = v_ref[wrows, :].astype(BF)
            valid = _swa_valid(offset, tq)
            q, do, o, lse2 = q_ref[rows, :], do_ref[rows, :], o_ref[rows, :], lse_ref[rows, :]
            dq = jnp.zeros((tq, LANES), F32)
            dk = jnp.zeros((win, LANES), F32)
            dv = jnp.zeros((win, LANES), F32)
            for half in (0, 1):
                hm = _half_mask((tq, LANES), half)
                lane0 = half * HEAD_DIM
                qh = (jnp.where(hm, q, 0).astype(F32) * QK_SCALE).astype(BF)
                do_f = jnp.where(hm, do, 0.0)
                doh = do_f.astype(BF)
                delta = jnp.sum(do_f * o, axis=1, keepdims=True)
                lse = lse2[:, lane0:lane0 + 1]
                s = lax.dot_general(qh, kw, (((1,), (1,)), ((), ())), preferred_element_type=F32)
                p = jnp.exp(jnp.where(valid, s, NEG_INF) - lse)
                dp = lax.dot_general(doh, vw, (((1,), (1,)), ((), ())), preferred_element_type=F32)
                ds_b = (p * (dp - delta)).astype(BF)
                dv = dv + lax.dot_general(p.astype(BF), doh, (((0,), (0,)), ((), ())),
                                          preferred_element_type=F32)
                dk = dk + lax.dot_general(ds_b, qh, (((0,), (0,)), ((), ())), preferred_element_type=F32)
                kh = jnp.where(_half_mask((win, LANES), half), kw, 0)
                dq = dq + jnp.dot(ds_b, kh, preferred_element_type=F32)
                p_sink = jnp.exp(sink_ref[2 * p_id + half] - lse)
                dsink_ref[0, half:half + 1, :] += jnp.broadcast_to(
                    -jnp.sum(p_sink * delta, axis=0, keepdims=True), (1, LANES))
            dq_ref[rows, :] = dq * QK_SCALE
            dk_ref[wrows, :] += dk
            dv_ref[wrows, :] += dv
        _comm_edge(comm, comm_refs, grid, first=False)

    tile = pl.BlockSpec((sub * tq, LANES), lambda p, i: (i, p))
    whole = lambda col: pl.BlockSpec((S, LANES), lambda p, i: (0, col))
    res = pl.pallas_call(
        kern, name="swa_bwd", grid=grid,
        in_specs=[tile, whole(n_pairs), whole(v_col), tile, tile, tile,
                  pl.BlockSpec(memory_space=pltpu.SMEM)] + _comm_specs(comm, "in"),
        out_specs=[tile, whole(0), whole(0),
                   pl.BlockSpec((1, 8, LANES), lambda p, i: (p, 0, 0))] + _comm_specs(comm, "out"),
        out_shape=[jax.ShapeDtypeStruct((S, A_Q_HEADS * HEAD_DIM), F32),
                   jax.ShapeDtypeStruct((S, LANES), F32), jax.ShapeDtypeStruct((S, LANES), F32),
                   jax.ShapeDtypeStruct((n_pairs, 8, LANES), F32)] + (comm.out_shapes if comm else []),
        scratch_shapes=comm.sem_shapes if comm else [],
        compiler_params=_cparams("arbitrary", "arbitrary"),
    )(qk, qk, v_arr, o_arr, do_arr, lse_arr, sinks, *(comm.ins if comm else []))
    return (*res[:4], res[4:])


ADAMW_BLOCK = 512 * 1024


def _adamw(w, g, m, v, name, comm=None):
    R, C = w.shape
    tr, tc = _tile(R, max(8, ADAMW_BLOCK // C), 8), C
    grid = (R // tr, C // tc)

    def kern(*refs):
        (w_ref, g_ref, m_ref, v_ref, d_ref, mo_ref, vo_ref), comm_refs = _own_refs(refs, comm, 4, 3, 0)
        _comm_edge(comm, comm_refs, grid, first=True)
        g_ = g_ref[...]
        m_new = ADAM_B1 * m_ref[...] + (1.0 - ADAM_B1) * g_
        v_new = ADAM_B2 * v_ref[...] + (1.0 - ADAM_B2) * (g_ * g_)
        m_hat = m_new / (1.0 - ADAM_B1 ** ADAM_STEP)
        v_hat = v_new / (1.0 - ADAM_B2 ** ADAM_STEP)
        d_ref[...] = -ADAM_LR * (m_hat / (jnp.sqrt(v_hat) + ADAM_EPS) + ADAM_WD * w_ref[...])
        mo_ref[...] = m_new
        vo_ref[...] = v_new
        _comm_edge(comm, comm_refs, grid, first=False)

    spec = pl.BlockSpec((tr, tc), lambda i, j: (i, j))
    shape = jax.ShapeDtypeStruct((R, C), F32)
    res = pl.pallas_call(
        kern, name=name, grid=grid,
        in_specs=[spec] * 4 + _comm_specs(comm, "in"),
        out_specs=[spec] * 3 + _comm_specs(comm, "out"),
        out_shape=[shape] * 3 + (comm.out_shapes if comm else []),
        scratch_shapes=comm.sem_shapes if comm else [],
        input_output_aliases={4 + i: 3 + o for i, o in comm.aliases.items()} if comm else {},
        compiler_params=_cparams("arbitrary", "arbitrary"),
    )(w, g, m, v, *(comm.ins if comm else []))
    return (res[:3], res[3:]) if comm else res


def _index_operand(i):
    return jnp.reshape(i, (1,)).astype(jnp.int32)


def _add_pair(whole, got, ci, name):
    P, R, C = whole.shape
    half = R // 2
    tr = _tile(half, ROWS, 16)
    nb = half // tr

    def kern(ci_ref, a_ref, b_ref, o_ref, ob_ref):
        s = a_ref[...] + b_ref[...].astype(F32)
        o_ref[...] = s
        ob_ref[...] = s.astype(BF)

    spec = pl.BlockSpec((None, tr, C), lambda p, i, ci_ref: (p, i, 0))
    return pl.pallas_call(
        kern, name=name,
        grid_spec=pltpu.PrefetchScalarGridSpec(
            num_scalar_prefetch=1, grid=(P, nb),
            in_specs=[pl.BlockSpec((None, tr, C), lambda p, i, ci_ref: (p, ci_ref[0] * nb + i, 0)), spec],
            out_specs=[spec, spec]),
        out_shape=[jax.ShapeDtypeStruct((P, half, C), F32), jax.ShapeDtypeStruct((P, half, C), BF)],
        compiler_params=_cparams("parallel", "parallel"),
    )(_index_operand(ci), whole, got)


def _add_three(parts, recv, chip, name):
    _, R, C = parts.shape
    tr = _tile(R, ROWS, 16)

    def kern(chip_ref, o_ref, r0_ref, r1_ref, r2_ref, out_ref):
        s = ((o_ref[...] + r0_ref[...].astype(F32)) + r1_ref[...].astype(F32)) + r2_ref[...].astype(F32)
        out_ref[0] = s
        out_ref[1] = s

    slab = lambda k: pl.BlockSpec((None, tr, C), lambda i, chip_ref: (k, i, 0))
    return pl.pallas_call(
        kern, name=name,
        grid_spec=pltpu.PrefetchScalarGridSpec(
            num_scalar_prefetch=1, grid=(R // tr,),
            in_specs=[pl.BlockSpec((None, tr, C), lambda i, chip_ref: (chip_ref[0], i, 0)),
                      slab(0), slab(1), slab(2)],
            out_specs=pl.BlockSpec((2, tr, C), lambda i, chip_ref: (0, i, 0))),
        out_shape=jax.ShapeDtypeStruct((2, R, C), F32),
        compiler_params=_cparams("parallel"),
    )(_index_operand(chip), parts, recv, recv, recv)


SM_ADA, SM_G, SM_LOSS, SM_BF, SM_SINK, SM_LEN = 0, 6144, 10240, 11264, 11272, 12288


def _small_finalize(gathered):
    def kern(g_ref, tot_ref, loss_ref):
        tot = g_ref[0:1, :]
        for b in range(1, N_DEV):
            tot = tot + g_ref[b:b + 1, :]
        tot_ref[...] = tot
        sq = jnp.sum(tot[:, SM_LOSS:SM_LOSS + D_MODEL], axis=1, keepdims=True)
        loss_ref[...] = jnp.broadcast_to(sq * (0.5 / D_MODEL), (1, LANES))

    full = lambda shape: pl.BlockSpec(shape, lambda i: (0, 0))
    return pl.pallas_call(
        kern, name="small_finalize", grid=(1,),
        in_specs=[full((N_DEV, SM_LEN))],
        out_specs=[full((1, SM_LEN)), full((1, LANES))],
        out_shape=[jax.ShapeDtypeStruct((1, SM_LEN), F32), jax.ShapeDtypeStruct((1, LANES), F32)],
        compiler_params=_cparams("arbitrary"),
    )(gathered)


def _ada_dw(c_t, d_ada):
    N = d_ada.shape[1]
    tn = _tile(N, 512)

    def kern(c_ref, d_ref, o_ref):
        acc = c_ref[:, 0:1] * d_ref[0:1, :]
        for b in range(1, N_DEV):
            acc = acc + c_ref[:, b:b + 1] * d_ref[b:b + 1, :]
        o_ref[...] = acc

    return pl.pallas_call(
        kern, name="ada_dw", grid=(N // tn,),
        in_specs=[pl.BlockSpec((D_MODEL, N_DEV), lambda j: (0, 0)), pl.BlockSpec((N_DEV, tn), lambda j: (0, j))],
        out_specs=pl.BlockSpec((D_MODEL, tn), lambda j: (0, j)),
        out_shape=jax.ShapeDtypeStruct((D_MODEL, N), F32),
        compiler_params=_cparams("parallel"),
    )(c_t, d_ada)


def _here():
    return lax.axis_index("x"), lax.axis_index("y"), lax.axis_index("c")


def _other_chips(x, y):
    return [(1 - x, y), (x, 1 - y), (1 - x, 1 - y)]


_ANY = pl.BlockSpec(memory_space=pl.ANY)


class _Comm:
    def __init__(self, ins, out_shapes, sem_shapes, start, finish, aliases=None):
        self.ins, self.out_shapes, self.sem_shapes = list(ins), list(out_shapes), list(sem_shapes)
        self.start, self.finish = start, finish
        self.aliases = dict(aliases or {})

    def split(self, refs, n_in, n_out, n_scratch):
        a = n_in + len(self.ins)
        b = a + n_out + len(self.out_shapes)
        own = list(refs[:n_in]) + list(refs[a:a + n_out]) + list(refs[b:b + n_scratch])
        mine = (refs[n_in:a], refs[a + n_out:b], refs[b + n_scratch:])
        return own, mine


def _run_comm(comm, name):
    n_in, n_out = len(comm.ins), len(comm.out_shapes)

    def body(*refs):
        parts = (refs[:n_in], refs[n_in:n_in + n_out], refs[n_in + n_out:])
        comm.start(*parts)
        comm.finish(*parts)

    return pl.pallas_call(
        body, name=name,
        in_specs=[_ANY] * n_in, out_specs=[_ANY] * n_out,
        out_shape=comm.out_shapes, scratch_shapes=comm.sem_shapes,
        input_output_aliases=comm.aliases,
    )(*comm.ins)


def _gather_comm(blocks):
    L = len(blocks)

    def parts(ins, outs, sems):
        send_sems, recv_sems, local_sems = sems
        x, y, c = _here()
        me, sibling = (x, y, c), (x, y, 1 - c)
        chips = _other_chips(x, y)

        def slot(px, py, pc):
            return 4 * px + 2 * py + pc

        def copy(l, k, block, to, src=None):
            dst = outs[l].at[slot(*block)]
            return pltpu.make_async_remote_copy(
                src_ref=dst if src is None else src, dst_ref=dst,
                send_sem=send_sems.at[l, k], recv_sem=recv_sems.at[l, k],
                device_id=to, device_id_type=MESH)

        mine = [pltpu.make_async_copy(ins[l], outs[l].at[slot(*me)], local_sems.at[l]) for l in range(L)]
        first = []
        for l in range(L):
            first.append(copy(l, 0, me, sibling, src=ins[l]))
            for j, chip in enumerate(chips):
                first.append(copy(l, 1 + j, me, (*chip, c), src=ins[l]))
        return c, me, sibling, chips, copy, mine, first

    def start(ins, outs, sems):
        *_, mine, first = parts(ins, outs, sems)
        for cp in mine + first:
            cp.start()

    def finish(ins, outs, sems):
        c, me, sibling, chips, copy, mine, first = parts(ins, outs, sems)
        passed = []
        for j, chip in enumerate(chips):
            for l in range(L):
                copy(l, 1 + j, (*chip, c), me).wait_recv()
                fwd = copy(l, 4 + j, (*chip, c), sibling)
                fwd.start()
                passed.append(fwd)
        for l in range(L):
            copy(l, 0, sibling, me).wait_recv()
        for j, chip in enumerate(chips):
            for l in range(L):
                copy(l, 4 + j, (*chip, 1 - c), me).wait_recv()
        for cp in first + passed:
            cp.wait_send()
        for cp in mine:
            cp.wait()

    return _Comm(blocks, [jax.ShapeDtypeStruct((N_DEV,) + b.shape, b.dtype) for b in blocks],
                 [pltpu.SemaphoreType.DMA((L, 7)), pltpu.SemaphoreType.DMA((L, 7)), pltpu.SemaphoreType.DMA((L,))],
                 start, finish)


def _allgather8(blocks, name):
    return _run_comm(_gather_comm(blocks), name)


def _swap_comm(arrs):
    L = len(arrs)

    def copies(ins, outs, sems):
        send_sems, recv_sems = sems
        x, y, c = _here()
        cps = []
        for l in range(L):
            half = arrs[l].shape[1] // 2
            rows = pl.ds(pl.multiple_of((1 - c) * half, 16), half)
            cps.append(pltpu.make_async_remote_copy(
                src_ref=ins[l].at[:, rows, :], dst_ref=outs[l], send_sem=send_sems.at[l],
                recv_sem=recv_sems.at[l], device_id=(x, y, 1 - c), device_id_type=MESH))
        return cps

    def start(ins, outs, sems):
        for cp in copies(ins, outs, sems):
            cp.start()

    def finish(ins, outs, sems):
        for cp in copies(ins, outs, sems):
            cp.wait()

    return _Comm(arrs, [jax.ShapeDtypeStruct((a.shape[0], a.shape[1] // 2, a.shape[2]), a.dtype) for a in arrs],
                 [pltpu.SemaphoreType.DMA((L,)), pltpu.SemaphoreType.DMA((L,))], start, finish)


def _join_comm(bufs):
    L = len(bufs)

    def start(ins, outs, sems):
        send_sems, recv_sems = sems
        x, y, c = _here()
        for l in range(L):
            pltpu.make_async_remote_copy(src_ref=outs[l].at[c], dst_ref=outs[l].at[c], send_sem=send_sems.at[l],
                                         recv_sem=recv_sems.at[l], device_id=(x, y, 1 - c),
                                         device_id_type=MESH).start()

    def finish(ins, outs, sems):
        send_sems, recv_sems = sems
        x, y, c = _here()
        for l in range(L):
            pltpu.make_async_remote_copy(src_ref=outs[l].at[c], dst_ref=outs[l].at[1 - c],
                                         send_sem=send_sems.at[l], recv_sem=recv_sems.at[l],
                                         device_id=(x, y, 1 - c), device_id_type=MESH).wait()

    return _Comm(bufs, [jax.ShapeDtypeStruct(a.shape, a.dtype) for a in bufs],
                 [pltpu.SemaphoreType.DMA((L,)), pltpu.SemaphoreType.DMA((L,))], start, finish,
                 aliases={l: l for l in range(L)})


def _scatter_comm(arrs):
    L = len(arrs)

    def copies(ins, outs, sems):
        send_sems, recv_sems = sems
        x, y, c = _here()
        return [pltpu.make_async_remote_copy(
            src_ref=ins[l].at[2 * tx + ty], dst_ref=outs[l].at[j],
            send_sem=send_sems.at[l, j], recv_sem=recv_sems.at[l, j],
            device_id=(tx, ty, c), device_id_type=MESH)
            for l in range(L) for j, (tx, ty) in enumerate(_other_chips(x, y))]

    def start(ins, outs, sems):
        for cp in copies(ins, outs, sems):
            cp.start()

    def finish(ins, outs, sems):
        for cp in copies(ins, outs, sems):
            cp.wait()

    return _Comm(arrs, [jax.ShapeDtypeStruct((3,) + a.shape[1:], a.dtype) for a in arrs],
                 [pltpu.SemaphoreType.DMA((L, 3)), pltpu.SemaphoreType.DMA((L, 3))], start, finish)


_A_ORDER = np.array(A_HEAD_ORDER)
_A_INVERSE = np.argsort(_A_ORDER)


def _permute_in_weights(w_in):
    qa = w_in[:, 0:512].reshape(D_MODEL, A_Q_HEADS, HEAD_DIM)[:, _A_ORDER, :].reshape(D_MODEL, 512)
    f_pad = jnp.pad(w_in[:, 2304:2312], ((0, 0), (0, LANES - B_HEADS)))
    w_a = jnp.concatenate([qa, w_in[:, 512:640], f_pad], axis=1)
    return w_a, w_in[:, 640:2304], w_in[:, 2312:4360]


def _slab_segments():
    segs = [(h * HEAD_DIM, int(_A_INVERSE[h]) * HEAD_DIM, HEAD_DIM) for h in range(A_Q_HEADS)]
    segs += [(512, OFF_KA, 128), (640, W_A + OFF_VA, 128), (768, W_A + OFF_QB, 1536),
             (2304, OFF_F, B_HEADS), (2312, W_A + W_B, W_G)]
    return segs


def _shard_slabs(dw_perm):
    R = dw_perm.shape[0]
    tr = _tile(R, 128, 8)
    plan = []
    for k in range(N_CHIP):
        for b in range(W_SHARD_PAD // LANES):
            lo, hi = k * W_SHARD + b * LANES, min(k * W_SHARD + (b + 1) * LANES, (k + 1) * W_SHARD)
            parts = []
            for o0, s0, n in _slab_segments():
                a, z = max(lo, o0), min(hi, o0 + n)
                while a < z:
                    s = s0 + (a - o0)
                    run = min(z - a, LANES - s % LANES)
                    parts.append((s // LANES, ((a - lo) - s % LANES) % LANES, a - lo, run))
                    a += run
            plan.append((k, b, parts))

    def kern(x_ref, o32_ref, obf_ref):
        lane = lax.broadcasted_iota(jnp.int32, (tr, LANES), 1)
        for k, b, parts in plan:
            acc = jnp.zeros((tr, LANES), F32)
            for src, rot, first, run in parts:
                blk = x_ref[:, src * LANES:(src + 1) * LANES]
                if rot:
                    blk = pltpu.roll(blk, rot, 1)
                acc = jnp.where((lane >= first) & (lane < first + run), blk, acc)
            o32_ref[k, :, b * LANES:(b + 1) * LANES] = acc
            obf_ref[k, :, b * LANES:(b + 1) * LANES] = acc.astype(BF)

    out_spec = pl.BlockSpec((N_CHIP, tr, W_SHARD_PAD), lambda i: (0, i, 0))
    return tuple(pl.pallas_call(
        kern, name="shard_slabs", grid=(R // tr,),
        in_specs=[pl.BlockSpec((tr, W_PERM), lambda i: (i, 0))],
        out_specs=[out_spec, out_spec],
        out_shape=[jax.ShapeDtypeStruct((N_CHIP, R, W_SHARD_PAD), F32),
                   jax.ShapeDtypeStruct((N_CHIP, R, W_SHARD_PAD), BF)],
        compiler_params=_cparams("parallel"),
    )(dw_perm))


class _NoExchange:
    def __init__(self, w_in, rest):
        self.w_in_whole, self.rest, self.grads = w_in, rest, {}

    def w_in_comm(self):
        return None

    def w_in(self, outs):
        return self.w_in_whole

    def rest_weights_comm(self):
        return None

    def rest_weights(self, outs):
        return self.rest

    def swap_comm(self, pieces, tag):
        self.grads[tag] = [p32 for p32, _ in pieces]
        return None

    def swap_done(self, outs, tag):
        return None

    def reduce_done(self, outs, tag):
        pass


class _Exchange:
    def __init__(self, ci, chip, w_in_shard, rest_shards):
        self.ci, self.chip, self.w_in_shard, self.rest_shards = ci, chip, w_in_shard, rest_shards
        self.pieces, self.part_f32, self.halves = {}, {}, {}

    def _my_half(self, a, axis=0, other=False):
        rows = a.shape[axis] // 2
        return lax.dynamic_slice_in_dim(a, ((1 - self.ci) if other else self.ci) * rows, rows, axis=axis)

    def w_in_comm(self):
        return _gather_comm([self._my_half(self.w_in_shard).astype(BF)])

    def w_in(self, outs):
        return _col_sharded(outs[0])

    def rest_weights_comm(self):
        return _gather_comm([self._my_half(w).astype(BF) for w in self.rest_shards])

    def rest_weights(self, outs):
        w_ba, w_bb, w_out, w_fi, w_fo = outs
        return (_col_sharded(w_ba), _col_sharded(w_bb), _row_sharded(w_out), _col_sharded(w_fi),
                _row_sharded(w_fo))

    def swap_comm(self, pieces, tag):
        self.pieces[tag] = pieces
        return _swap_comm([pbf for _, pbf in pieces])

    def swap_done(self, got, tag):
        self.part_f32[tag], part_bf = [], []
        for l, ((p32, _), g_) in enumerate(zip(self.pieces[tag], got)):
            s32, sbf = _add_pair(p32, g_, self.ci, f"chip_sum_{tag}_{l}")
            self.part_f32[tag].append(s32)
            part_bf.append(sbf)
        return _scatter_comm(part_bf)

    def reduce_done(self, outs, tag):
        self.halves[tag] = [_add_three(p32, r, self.chip, f"shard_sum_{tag}_{l}")
                            for l, (p32, r) in enumerate(zip(self.part_f32[tag], outs))]


def _col_sharded(g):
    return jnp.transpose(g.reshape(N_CHIP, -1, g.shape[-1]), (1, 0, 2)).reshape(2 * g.shape[1], N_CHIP * g.shape[-1])


def _row_sharded(g):
    return g.reshape(N_DEV * g.shape[1], g.shape[-1])


def _rope_tables(pos):
    inv_freq = 1.0 / (ROPE_THETA ** (jnp.arange(0, HEAD_DIM, 2, dtype=F32) / HEAD_DIM))
    ang = pos.astype(F32)[:, None] * inv_freq
    cos, sin = jnp.cos(ang), jnp.sin(ang)
    return jnp.tile(cos, (1, 4)), jnp.tile(jnp.concatenate([-sin, sin], axis=1), (1, 2))


def _local_step(x, pos, ada, g1, g2, g3, g4, b_f, sinks, exch, target):
    S = x.shape[0]
    t_fox = _tile(S, 512, LANES) if S >= 1024 else S // 2
    t_fox_fwd = _tile(S, 1024, LANES) if S >= 2048 else S // 2
    shift_m, scale_m, gate_m, shift_f, scale_f, gate_f = [ada[i:i + 1] for i in range(N_ADA)]
    cos_t, sin_t = _rope_tables(pos)
    sinks_p = sinks.reshape(A_KV_HEADS, 4).T.reshape(A_Q_HEADS)
    b_f_pad = jnp.pad(b_f, (0, LANES - B_HEADS)).reshape(1, LANES)

    h1, outs = _pre_norm(x, g1, scale_m, shift_m, "pre_mix_norm", comm=exch.w_in_comm())
    w_a, w_b, w_g = _permute_in_weights(exch.w_in(outs))
    w_perm = jnp.concatenate([w_a, w_b, w_g], axis=1)
    p_a = _mm(h1, w_a, "nn", F32, "proj_a")
    p_b = _mm(h1, w_b, "nn", BF, "proj_b")
    p_g = _mm(h1, w_g, "nn", BF, "proj_g")
    (qk_a,) = _rope([p_a], [640], cos_t, sin_t, "rope_fwd")
    o_a, lse_a = _swa_fwd(qk_a, p_b, 0, sinks_p)
    cum = _fox_gate_fwd(p_a, b_f_pad)
    bq, bk = _fox_prep_fwd(cum, t_fox)
    comm = exch.rest_weights_comm()
    o_b, lse_b, outs = _fox_fwd(p_b, bq, bk, t_fox_fwd, comm=comm)
    w_ba, w_bb, w_out, w_fi, w_fo = exch.rest_weights(outs)
    w_ba_p = w_ba.reshape(A_Q_HEADS, HEAD_DIM, D_MODEL)[_A_ORDER].reshape(512, D_MODEL)
    pa = _mm(o_a, w_ba_p, "nn", BF, "branch_a")
    pb = _mm(o_b, w_bb, "nn", BF, "branch_b")
    merged = _merge_fwd(p_g, pa, pb)
    y1 = _mm(merged, w_out, "nn", BF, "out_proj")
    x2, h2 = _post_pre(x, y1, g2, gate_m, g3, scale_f, shift_f)
    gu = _mm(h2, w_fi, "nn", BF, "ffn_in")
    act = _swiglu_fwd(gu)
    y2 = _mm(act, w_fo, "nn", BF, "ffn_out")
    d_out, d_y2, st_f = _final(x2, y2, g4, gate_f, target)

    d_act = _mm(d_y2, w_fo, "nt", BF, "ffn_out_dx")
    row_pieces = lambda pair: tuple(t.reshape(N_CHIP, t.shape[0] // N_CHIP, t.shape[1]) for t in pair)
    dw_fo = row_pieces(_mm(act, d_y2, "tn", F32, "ffn_out_dw", twin=True))
    d_gu = _swiglu_bwd(d_act, gu)
    d_h2 = _mm(d_gu, w_fi, "nt", BF, "ffn_in_dx")
    dw_fi = _mm(h2, d_gu, "tn", F32, "ffn_in_dw", col_pieces=N_CHIP, twin=True)
    d_x2, d_y1, st_m = _mid_bwd(d_h2, x2, d_out, y1, g3, scale_f, g2, gate_m)
    d_merged = _mm(d_y1, w_out, "nt", BF, "out_proj_dx")
    dw_out = row_pieces(_mm(merged, d_y1, "tn", F32, "out_proj_dw", twin=True))
    d_pa, d_pb, d_ga, d_gb = _merge_bwd(d_merged, p_g, pa, pb)
    d_oa = _mm(d_pa, w_ba_p, "nt", F32, "branch_a_dx")
    dw_ba_p = _mm(o_a, d_pa, "tn", F32, "branch_a_dw", col_pieces=N_CHIP, twin=True)
    d_ob = _mm(d_pb, w_bb, "nt", F32, "branch_b_dx")
    dw_bb = _mm(o_b, d_pb, "tn", F32, "branch_b_dw", col_pieces=N_CHIP, twin=True)
    head_rows = lambda t: t.reshape(N_CHIP, A_Q_HEADS, HEAD_DIM, -1)[:, _A_INVERSE].reshape(t.shape)
    dw_ba = tuple(head_rows(t) for t in dw_ba_p)
    comm = exch.swap_comm([dw_ba, dw_bb, dw_out, dw_fi, dw_fo], "early")
    dq_a, dk_a, dv_a, d_sink, outs = _swa_bwd(qk_a, p_b, 0, o_a, d_oa, lse_a, sinks_p, comm=comm)
    comm = exch.swap_done(outs, "early")
    bq_bwd, bdo = _fox_prep_bwd(cum, o_b, d_ob, lse_b, t_fox)
    dq_b, dk_b, dv_b, d_ck, d_cq, outs = _fox_bwd(p_b, d_ob, bq_bwd, bk, bdo, t_fox, comm=comm)
    exch.reduce_done(outs, "early")
    d_qa, d_ka = _rope([dq_a, dk_a], [512, LANES], cos_t, -sin_t, "rope_bwd")
    d_ck_cols = jnp.pad(d_ck.reshape(B_HEADS, S).T, ((0, 0), (0, LANES - B_HEADS)))
    d_f, d_bf = _fox_gate_bwd(d_cq, d_ck_cols, p_a, b_f_pad)
    d_proj = jnp.concatenate([d_qa, d_ka, d_f, dv_a.astype(BF), dq_b.astype(BF), dk_b.astype(BF),
                              dv_b.astype(BF), d_ga, d_gb], axis=1)
    dw_perm = _mm(h1, d_proj, "tn", F32, "proj_dw")
    swap = exch.swap_comm([_shard_slabs(dw_perm)], "late")
    comm = exch.swap_done(_run_comm(swap, "grads_to_sibling_late") if swap else None, "late")
    res = _mm(d_proj, w_perm, "nt", BF, "proj_dx", comm=comm)
    d_h1 = res[0] if comm else res
    exch.reduce_done(res[1] if comm else None, "late")
    grad_x, st_p = _pre_bwd(d_h1, x, d_x2, g1, scale_m)

    d_sinks = d_sink[:, :2, 0].T.reshape(A_Q_HEADS)
    small = jnp.concatenate([
        st_p[0], st_p[1], st_m[3], st_m[0], st_m[1], st_f[0],
        st_p[2], st_m[4], st_m[2], st_f[1],
        st_f[2], d_bf[0, :B_HEADS], d_sinks,
        jnp.zeros((SM_LEN - SM_SINK - A_Q_HEADS,), F32)])
    return grad_x, small


def kernel(x, c, positions, w_ada, b_ada, g_pre_mix, g_post_mix, w_in, b_f, sinks, w_branch_a, w_branch_b, w_out, g_pre_ffn, g_post_ffn, w_ffn_in, w_ffn_out, loss_target, m_w_ada, m_b_ada, m_g_pre_mix, m_g_post_mix, m_w_in, m_b_f, m_sinks, m_w_branch_a, m_w_branch_b, m_w_out, m_g_pre_ffn, m_g_post_ffn, m_w_ffn_in, m_w_ffn_out, v_w_ada, v_b_ada, v_g_pre_mix, v_g_post_mix, v_w_in, v_b_f, v_sinks, v_w_branch_a, v_w_branch_b, v_w_out, v_g_pre_ffn, v_g_post_ffn, v_w_ffn_in, v_w_ffn_out):
    xi, yi, ci = _here()
    chip = 2 * xi + yi
    dev = 2 * chip + ci

    (c_g,) = _allgather8([c.reshape(8, LANES)], "gather_c")
    c_all = c_g.reshape(N_DEV, D_MODEL)
    exch = _Exchange(ci, chip, w_in[0], [w_branch_a[0], w_branch_b[0], w_out[0], w_ffn_in[0], w_ffn_out[0]])

    ada_cols = _mm(c_all, w_ada[0], "nn", F32, "ada_fwd")
    (ada_g,) = _allgather8([ada_cols], "gather_ada")
    ada_mine = lax.dynamic_index_in_dim(ada_g.reshape(N_CHIP, 2, N_DEV, -1)[:, 0], dev, axis=1, keepdims=False)
    ada = (ada_mine.reshape(-1) + b_ada[0]).reshape(N_ADA, D_MODEL)

    grad_x, small = _local_step(
        x[0], positions[0], ada, g_pre_mix, g_post_mix, g_pre_ffn, g_post_ffn, b_f[0], sinks[0],
        exch, loss_target[0])

    (small_g,) = _allgather8([small.reshape(8, SM_LEN // 8)], "gather_small")
    small_all = small_g.reshape(N_DEV, SM_LEN)
    small_tot, loss_row = _small_finalize(small_all)
    loss = loss_row[0, 0]
    d_ada_cols = lax.dynamic_slice_in_dim(small_all[:, :N_ADA * D_MODEL], chip * (N_ADA * D_MODEL // N_CHIP),
                                          N_ADA * D_MODEL // N_CHIP, axis=1)
    g_w_ada = _ada_dw(c_all.T, d_ada_cols)

    def small_vec(b_ada_, g1_, g2_, g3_, g4_, b_f_, sinks_):
        return jnp.concatenate([b_ada_[0], g1_[0], g2_[0], g3_[0], g4_[0], jnp.zeros((D_MODEL,), F32),
                                b_f_[0], sinks_[0], jnp.zeros((SM_LEN - SM_SINK - A_Q_HEADS,), F32)]
                               ).reshape(8, SM_LEN // 8)

    sw = small_vec(b_ada, g_pre_mix, g_post_mix, g_pre_ffn, g_post_ffn, b_f, sinks)
    sm = small_vec(m_b_ada, m_g_pre_mix, m_g_post_mix, m_g_pre_ffn, m_g_post_ffn, m_b_f, m_sinks)
    sv = small_vec(v_b_ada, v_g_pre_mix, v_g_post_mix, v_g_pre_ffn, v_g_post_ffn, v_b_f, v_sinks)
    s_upd = [u.reshape(SM_LEN) for u in _adamw(sw, small_tot.reshape(8, SM_LEN // 8), sm, sv, "adamw_small")]
    s_grad = small_tot.reshape(SM_LEN)

    def unpack(vec):
        row = lambda a, n: vec[a:a + n].reshape(1, n)
        return dict(b_ada=row(SM_ADA, N_ADA * D_MODEL), g_pre_mix=row(SM_G, D_MODEL),
                    g_post_mix=row(SM_G + D_MODEL, D_MODEL), g_pre_ffn=row(SM_G + 2 * D_MODEL, D_MODEL),
                    g_post_ffn=row(SM_G + 3 * D_MODEL, D_MODEL), b_f=row(SM_BF, B_HEADS),
                    sinks=row(SM_SINK, A_Q_HEADS))

    join = _join_comm(exch.halves["late"] + exch.halves["early"])
    upd_ada, joined = _adamw(w_ada[0], g_w_ada, m_w_ada[0], v_w_ada[0], "adamw_w_ada", comm=join)
    g_w_in, g_w_ba, g_w_bb, g_w_out, g_w_fi, g_w_fo = [j.reshape(2 * j.shape[1], j.shape[2]) for j in joined]
    big = dict(
        w_branch_a=(w_branch_a, g_w_ba, m_w_branch_a, v_w_branch_a),
        w_branch_b=(w_branch_b, g_w_bb, m_w_branch_b, v_w_branch_b),
        w_out=(w_out, g_w_out, m_w_out, v_w_out), w_ffn_in=(w_ffn_in, g_w_fi, m_w_ffn_in, v_w_ffn_in),
        w_ffn_out=(w_ffn_out, g_w_fo, m_w_ffn_out, v_w_ffn_out))
    grads, deltas, new_m, new_v = unpack(s_grad), unpack(s_upd[0]), unpack(s_upd[1]), unpack(s_upd[2])
    grads["w_ada"], deltas["w_ada"], new_m["w_ada"], new_v["w_ada"] = [t[None] for t in (g_w_ada, *upd_ada)]
    for n, (w_, g_, m_, v_) in big.items():
        d_, nm_, nv_ = _adamw(w_[0], g_, m_[0], v_[0], "adamw_" + n)
        grads[n], deltas[n], new_m[n], new_v[n] = g_[None], d_[None], nm_[None], nv_[None]
    pad_cols = lambda a: jnp.pad(a, ((0, 0), (0, W_SHARD_PAD - W_SHARD)))
    upd = _adamw(pad_cols(w_in[0]), g_w_in, pad_cols(m_w_in[0]), pad_cols(v_w_in[0]), "adamw_w_in")
    grads["w_in"], deltas["w_in"], new_m["w_in"], new_v["w_in"] = [t[None, :, :W_SHARD] for t in (g_w_in, *upd)]

    names = ["w_ada", "b_ada", "g_pre_mix", "g_post_mix", "w_in", "b_f", "sinks", "w_branch_a", "w_branch_b",
             "w_out", "g_pre_ffn", "g_post_ffn", "w_ffn_in", "w_ffn_out"]
    return (loss, grad_x[None], *[grads[n] for n in names], *[deltas[n] for n in names],
            *[new_m[n] for n in names], *[new_v[n] for n in names])
```

```python
import functools
import math

import numpy as np
import jax
import jax.numpy as jnp
from jax import lax
from jax.experimental import pallas as pl
from jax.experimental.pallas import tpu as pltpu

F32 = jnp.float32
BF = jnp.bfloat16

D_MODEL = 1024
HEAD_DIM = 64
LANES = 128
WINDOW = 128
A_Q_HEADS = 8
A_KV_HEADS = 2
B_HEADS = 8
D_FF = 2816
ROPE_THETA = 10000.0
RMS_EPS = 1e-6
N_ADA = 6
N_DEV = 8
N_CHIP = 4

ADAM_LR = 0.001
ADAM_B1 = 0.9
ADAM_B2 = 0.999
ADAM_EPS = 1e-08
ADAM_WD = 0.01
ADAM_STEP = 10

VMEM_LIMIT = 48 * 1024 * 1024
MESH = pl.DeviceIdType.MESH

A_HEAD_ORDER = (0, 4, 1, 5, 2, 6, 3, 7)

OFF_QA, OFF_KA, OFF_F = 0, 512, 640
W_A = 768
OFF_VA, OFF_QB, OFF_KB, OFF_VB = 0, 128, 640, 1152
W_B = 1664
W_G = 2048
W_PERM = W_A + W_B + W_G
W_SHARD = 1090
W_SHARD_PAD = 1152


def _tile(n, cap, mult=LANES):
    if n <= cap:
        return n
    t = (cap // mult) * mult
    while t >= mult:
        if n % t == 0:
            return t
        t -= mult
    raise ValueError(f"no tile for {n}")


MXU_WIDTH = 256
MM_OPERAND_BYTES = 28 * 1024 * 1024


def _mm_tiles(M, N, K, a_bytes, b_bytes, tm_cap, tn_cap):
    tm = _tile(M, tm_cap)
    try:
        tn = _tile(N, tn_cap, MXU_WIDTH)
    except ValueError:
        tn = _tile(N, tn_cap)
    fits = lambda tk: 2 * tk * (tm * a_bytes + tn * b_bytes) <= MM_OPERAND_BYTES
    tk = K if fits(K) else next(t for t in range(K // LANES * LANES, 0, -LANES) if K % t == 0 and fits(t))
    return tm, tn, tk


def _cparams(*sem):
    return pltpu.CompilerParams(dimension_semantics=sem, vmem_limit_bytes=VMEM_LIMIT)


def _own_refs(refs, comm, n_in, n_out, n_scratch):
    if comm is None:
        return list(refs), None
    return comm.split(refs, n_in, n_out, n_scratch)


def _comm_specs(comm, side):
    if comm is None:
        return []
    return [pl.BlockSpec(memory_space=pl.ANY)] * len(comm.ins if side == "in" else comm.out_shapes)


def _comm_edge(comm, comm_refs, grid, first):
    if comm is None:
        return
    at_edge = None
    for axis, n in enumerate(grid):
        here = pl.program_id(axis) == (0 if first else n - 1)
        at_edge = here if at_edge is None else at_edge & here
    pl.when(at_edge)(lambda: (comm.start if first else comm.finish)(*comm_refs))


def _mm(a, b, mode, out_dtype, name, tm_cap=512, tn_cap=2816, comm=None, col_pieces=1, twin=False):
    if mode == "nn":
        (M, K), (K2, N) = a.shape, b.shape
        dims = (((1,), (0,)), ((), ()))
    elif mode == "nt":
        (M, K), (N, K2) = a.shape, b.shape
        dims = (((1,), (1,)), ((), ()))
    else:
        (K, M), (K2, N) = a.shape, b.shape
        dims = (((0,), (0,)), ((), ()))
    assert K == K2, (a.shape, b.shape, mode)
    tm, tn, tk = _mm_tiles(M, N // col_pieces, K, a.dtype.itemsize, b.dtype.itemsize, tm_cap, tn_cap)
    nk = K // tk
    n_out = 2 if twin else 1
    n_scratch = 1 if nk > 1 else 0
    if mode == "nn":
        a_spec = pl.BlockSpec((tm, tk), lambda i, j, k: (i, k))
        b_spec = pl.BlockSpec((tk, tn), lambda i, j, k: (k, j))
    elif mode == "nt":
        a_spec = pl.BlockSpec((tm, tk), lambda i, j, k: (i, k))
        b_spec = pl.BlockSpec((tn, tk), lambda i, j, k: (j, k))
    else:
        a_spec = pl.BlockSpec((tk, tm), lambda i, j, k: (k, i))
        b_spec = pl.BlockSpec((tk, tn), lambda i, j, k: (k, j))

    grid = (M // tm, N // tn, nk)

    def kern(*refs):
        own, comm_refs = _own_refs(refs, comm, 2, n_out, n_scratch)
        a_ref, b_ref, o_refs = own[0], own[1], own[2:2 + n_out]
        k = pl.program_id(2)
        _comm_edge(comm, comm_refs, grid, first=True)
        part = lax.dot_general(a_ref[...].astype(BF), b_ref[...].astype(BF), dims,
                               preferred_element_type=F32)
        if nk == 1:
            for o_ref in o_refs:
                o_ref[...] = part.astype(o_ref.dtype)
        else:
            acc_ref = own[2 + n_out]

            @pl.when(k == 0)
            def _():
                acc_ref[...] = part

            @pl.when(k > 0)
            def _():
                acc_ref[...] += part

            @pl.when(k == nk - 1)
            def _():
                for o_ref in o_refs:
                    o_ref[...] = acc_ref[...].astype(o_ref.dtype)

        _comm_edge(comm, comm_refs, grid, first=False)

    if col_pieces > 1:
        per = N // col_pieces // tn
        out_spec = pl.BlockSpec((None, tm, tn), lambda i, j, k: (j // per, i, j % per))
        shape = (col_pieces, M, N // col_pieces)
    else:
        out_spec = pl.BlockSpec((tm, tn), lambda i, j, k: (i, j))
        shape = (M, N)
    dtypes = [out_dtype, BF] if twin else [out_dtype]
    res = pl.pallas_call(
        kern, name=name, grid=grid,
        in_specs=[a_spec, b_spec] + _comm_specs(comm, "in"),
        out_specs=[out_spec] * n_out + _comm_specs(comm, "out"),
        out_shape=[jax.ShapeDtypeStruct(shape, d) for d in dtypes] + (comm.out_shapes if comm else []),
        scratch_shapes=[pltpu.VMEM((tm, tn), F32)] * n_scratch + (comm.sem_shapes if comm else []),
        compiler_params=_cparams("parallel", "parallel", "arbitrary"),
    )(a, b, *(comm.ins if comm else []))
    own = res[0] if n_out == 1 else tuple(res[:n_out])
    return (own, res[n_out:]) if comm else own


ROWS = 512


def _row_spec(tm, width=D_MODEL, col=0):
    return pl.BlockSpec((tm, width), lambda i: (i, col))


def _vec_spec(width=D_MODEL):
    return pl.BlockSpec((1, width), lambda i: (0, 0))


def _rms(x):
    return lax.rsqrt(jnp.mean(x * x, axis=-1, keepdims=True) + RMS_EPS)


def _colsum(x):
    return jnp.sum(x, axis=0, keepdims=True)


def _norm_bwd(d_xn, xn, r):
    return r * (d_xn - xn * jnp.mean(d_xn * xn, axis=-1, keepdims=True))


def _pre_norm(x, g, scale, shift, name, comm=None):
    S = x.shape[0]
    tm = _tile(S, ROWS, 8)
    grid = (S // tm,)

    def kern(*refs):
        (x_ref, g_ref, sc_ref, sh_ref, h_ref), comm_refs = _own_refs(refs, comm, 4, 1, 0)
        _comm_edge(comm, comm_refs, grid, first=True)
        xf = x_ref[...]
        y = xf * _rms(xf) * g_ref[...]
        h_ref[...] = (y * (1.0 + sc_ref[...]) + sh_ref[...]).astype(BF)
        _comm_edge(comm, comm_refs, grid, first=False)

    res = pl.pallas_call(
        kern, name=name, grid=grid,
        in_specs=[_row_spec(tm), _vec_spec(), _vec_spec(), _vec_spec()] + _comm_specs(comm, "in"),
        out_specs=[_row_spec(tm)] + _comm_specs(comm, "out"),
        out_shape=[jax.ShapeDtypeStruct((S, D_MODEL), BF)] + (comm.out_shapes if comm else []),
        scratch_shapes=comm.sem_shapes if comm else [],
        compiler_params=_cparams("arbitrary"),
    )(x, g, scale, shift, *(comm.ins if comm else []))
    return res[0], res[1:]


def _post_pre(x, y1, g2, gate_m, g3, scale_f, shift_f):
    S = x.shape[0]
    tm = _tile(S, ROWS, 8)

    def kern(x_ref, y_ref, g2_ref, gm_ref, g3_ref, sc_ref, sh_ref, x2_ref, h2_ref):
        y = y_ref[...].astype(F32)
        n2 = y * _rms(y) * g2_ref[...]
        x2 = x_ref[...] + gm_ref[...] * n2
        x2_ref[...] = x2
        n3 = x2 * _rms(x2) * g3_ref[...]
        h2_ref[...] = (n3 * (1.0 + sc_ref[...]) + sh_ref[...]).astype(BF)

    return pl.pallas_call(
        kern, name="post_mix_pre_ffn", grid=(S // tm,),
        in_specs=[_row_spec(tm), _row_spec(tm)] + [_vec_spec()] * 5,
        out_specs=[_row_spec(tm), _row_spec(tm)],
        out_shape=[jax.ShapeDtypeStruct((S, D_MODEL), F32), jax.ShapeDtypeStruct((S, D_MODEL), BF)],
        compiler_params=_cparams("parallel"),
    )(x, y1, g2, gate_m, g3, scale_f, shift_f)


def _stats_spec():
    return pl.BlockSpec((8, D_MODEL), lambda i: (0, 0))


def _final(x2, y2, g4, gate_f, target):
    S = x2.shape[0]
    tm = _tile(S, ROWS, 8)

    def kern(x2_ref, y_ref, g4_ref, gf_ref, t_ref, dout_ref, dy_ref, st_ref):
        @pl.when(pl.program_id(0) == 0)
        def _():
            st_ref[...] = jnp.zeros_like(st_ref)

        y = y_ref[...].astype(F32)
        r = _rms(y)
        yn = y * r
        n4 = yn * g4_ref[...]
        diff = x2_ref[...] + gf_ref[...] * n4 - t_ref[...]
        d_out = diff / D_MODEL
        dout_ref[...] = d_out
        dn = d_out * gf_ref[...]
        dy_ref[...] = _norm_bwd(dn * g4_ref[...], yn, r).astype(BF)
        st_ref[0:1, :] += _colsum(d_out * n4)
        st_ref[1:2, :] += _colsum(dn * yn)
        st_ref[2:3, :] += _colsum(diff * diff)

    return pl.pallas_call(
        kern, name="final_loss", grid=(S // tm,),
        in_specs=[_row_spec(tm), _row_spec(tm), _vec_spec(), _vec_spec(), _row_spec(tm)],
        out_specs=[_row_spec(tm), _row_spec(tm), _stats_spec()],
        out_shape=[jax.ShapeDtypeStruct((S, D_MODEL), F32), jax.ShapeDtypeStruct((S, D_MODEL), BF),
                   jax.ShapeDtypeStruct((8, D_MODEL), F32)],
        compiler_params=_cparams("arbitrary"),
    )(x2, y2, g4, gate_f, target)


def _mid_bwd(d_h2, x2, d_out, y1, g3, scale_f, g2, gate_m):
    S = x2.shape[0]
    tm = _tile(S, ROWS, 8)

    def kern(dh_ref, x2_ref, dout_ref, y_ref, g3_ref, sc_ref, g2_ref, gm_ref, dx2_ref, dy_ref, st_ref):
        @pl.when(pl.program_id(0) == 0)
        def _():
            st_ref[...] = jnp.zeros_like(st_ref)

        dh = dh_ref[...].astype(F32)
        x2 = x2_ref[...]
        r3 = _rms(x2)
        xn = x2 * r3
        one_sc = 1.0 + sc_ref[...]
        d_x2 = dout_ref[...] + _norm_bwd(dh * one_sc * g3_ref[...], xn, r3)
        dx2_ref[...] = d_x2
        y = y_ref[...].astype(F32)
        r2 = _rms(y)
        yn = y * r2
        dn = d_x2 * gm_ref[...]
        dy_ref[...] = _norm_bwd(dn * g2_ref[...], yn, r2).astype(BF)
        st_ref[0:1, :] += _colsum(dh)
        st_ref[1:2, :] += _colsum(dh * (xn * g3_ref[...]))
        st_ref[2:3, :] += _colsum(dh * one_sc * xn)
        st_ref[3:4, :] += _colsum(d_x2 * (yn * g2_ref[...]))
        st_ref[4:5, :] += _colsum(dn * yn)

    return pl.pallas_call(
        kern, name="mid_bwd", grid=(S // tm,),
        in_specs=[_row_spec(tm)] * 4 + [_vec_spec()] * 4,
        out_specs=[_row_spec(tm), _row_spec(tm), _stats_spec()],
        out_shape=[jax.ShapeDtypeStruct((S, D_MODEL), F32), jax.ShapeDtypeStruct((S, D_MODEL), BF),
                   jax.ShapeDtypeStruct((8, D_MODEL), F32)],
        compiler_params=_cparams("arbitrary"),
    )(d_h2, x2, d_out, y1, g3, scale_f, g2, gate_m)


def _pre_bwd(d_h1, x, d_x2, g1, scale_m, comm=None):
    S = x.shape[0]
    tm = _tile(S, ROWS, 8)
    grid = (S // tm,)

    def kern(*refs):
        (dh_ref, x_ref, dx2_ref, g_ref, sc_ref, gx_ref, st_ref), comm_refs = _own_refs(refs, comm, 5, 2, 0)
        _comm_edge(comm, comm_refs, grid, first=True)

        @pl.when(pl.program_id(0) == 0)
        def _():
            st_ref[...] = jnp.zeros_like(st_ref)

        dh = dh_ref[...].astype(F32)
        xf = x_ref[...]
        r = _rms(xf)
        xn = xf * r
        one_sc = 1.0 + sc_ref[...]
        gx_ref[...] = dx2_ref[...] + _norm_bwd(dh * one_sc * g_ref[...], xn, r)
        st_ref[0:1, :] += _colsum(dh)
        st_ref[1:2, :] += _colsum(dh * (xn * g_ref[...]))
        st_ref[2:3, :] += _colsum(dh * one_sc * xn)
        _comm_edge(comm, comm_refs, grid, first=False)

    res = pl.pallas_call(
        kern, name="pre_mix_bwd", grid=grid,
        in_specs=[_row_spec(tm)] * 3 + [_vec_spec()] * 2 + _comm_specs(comm, "in"),
        out_specs=[_row_spec(tm), _stats_spec()] + _comm_specs(comm, "out"),
        out_shape=[jax.ShapeDtypeStruct((S, D_MODEL), F32), jax.ShapeDtypeStruct((8, D_MODEL), F32)]
        + (comm.out_shapes if comm else []),
        scratch_shapes=comm.sem_shapes if comm else [],
        input_output_aliases={5 + i: 2 + o for i, o in comm.aliases.items()} if comm else {},
        compiler_params=_cparams("arbitrary"),
    )(d_h1, x, d_x2, g1, scale_m, *(comm.ins if comm else []))
    return res[0], res[1], res[2:]


def _rope(xs, widths, cos_t, sin_t, name):
    S = xs[0].shape[0]
    tm = _tile(S, 512, 8)
    n = len(xs)

    def kern(*refs):
        cos = refs[n][...]
        sin = refs[n + 1][...]
        first = (lax.broadcasted_iota(jnp.int32, cos.shape, 1) % HEAD_DIM) < HEAD_DIM // 2
        for x_ref, o_ref, w in zip(refs[:n], refs[n + 2:], widths):
            for c0 in range(0, w, LANES):
                v = x_ref[:, c0:c0 + LANES]
                partner = jnp.where(first, pltpu.roll(v, LANES - HEAD_DIM // 2, 1),
                                    pltpu.roll(v, HEAD_DIM // 2, 1))
                o_ref[:, c0:c0 + LANES] = (v * cos + partner * sin).astype(BF)

    return pl.pallas_call(
        kern, name=name, grid=(S // tm,),
        in_specs=[_row_spec(tm, w) for w in widths] + [_row_spec(tm, LANES)] * 2,
        out_specs=[_row_spec(tm, w) for w in widths],
        out_shape=[jax.ShapeDtypeStruct((S, w), BF) for w in widths],
        compiler_params=_cparams("parallel"),
    )(*xs, cos_t, sin_t)


def _merge_fwd(pg, pa, pb):
    S = pa.shape[0]
    tm = _tile(S, ROWS, 8)

    def kern(ga_ref, gb_ref, pa_ref, pb_ref, o_ref):
        ga = jax.nn.sigmoid(ga_ref[...].astype(F32))
        gb = jax.nn.sigmoid(gb_ref[...].astype(F32))
        o_ref[...] = (ga * pa_ref[...].astype(F32) + gb * pb_ref[...].astype(F32)).astype(BF)

    return pl.pallas_call(
        kern, name="merge_fwd", grid=(S // tm,),
        in_specs=[_row_spec(tm, col=0), _row_spec(tm, col=1), _row_spec(tm), _row_spec(tm)],
        out_specs=_row_spec(tm),
        out_shape=jax.ShapeDtypeStruct((S, D_MODEL), BF),
        compiler_params=_cparams("parallel"),
    )(pg, pg, pa, pb)


def _merge_bwd(d_merged, pg, pa, pb):
    S = pa.shape[0]
    tm = _tile(S, ROWS, 8)

    def kern(dm_ref, ga_ref, gb_ref, pa_ref, pb_ref, dpa_ref, dpb_ref, dga_ref, dgb_ref):
        dm = dm_ref[...].astype(F32)
        ga = jax.nn.sigmoid(ga_ref[...].astype(F32))
        gb = jax.nn.sigmoid(gb_ref[...].astype(F32))
        dpa_ref[...] = (dm * ga).astype(BF)
        dpb_ref[...] = (dm * gb).astype(BF)
        dga_ref[...] = (dm * pa_ref[...].astype(F32) * ga * (1.0 - ga)).astype(BF)
        dgb_ref[...] = (dm * pb_ref[...].astype(F32) * gb * (1.0 - gb)).astype(BF)

    bf_out = jax.ShapeDtypeStruct((S, D_MODEL), BF)
    return pl.pallas_call(
        kern, name="merge_bwd", grid=(S // tm,),
        in_specs=[_row_spec(tm), _row_spec(tm, col=0), _row_spec(tm, col=1), _row_spec(tm), _row_spec(tm)],
        out_specs=[_row_spec(tm)] * 4,
        out_shape=[bf_out] * 4,
        compiler_params=_cparams("parallel"),
    )(d_merged, pg, pg, pa, pb)


def _swiglu_fwd(gu):
    S = gu.shape[0]
    tm = _tile(S, ROWS, 8)
    tc = _tile(D_FF, 1408)
    nc = D_FF // tc

    def kern(g_ref, u_ref, o_ref):
        g = g_ref[...].astype(F32)
        o_ref[...] = (g * jax.nn.sigmoid(g) * u_ref[...].astype(F32)).astype(BF)

    return pl.pallas_call(
        kern, name="swiglu_fwd", grid=(S // tm, nc),
        in_specs=[pl.BlockSpec((tm, tc), lambda i, j: (i, j)),
                  pl.BlockSpec((tm, tc), lambda i, j: (i, j + nc))],
        out_specs=pl.BlockSpec((tm, tc), lambda i, j: (i, j)),
        out_shape=jax.ShapeDtypeStruct((S, D_FF), BF),
        compiler_params=_cparams("parallel", "parallel"),
    )(gu, gu)


def _swiglu_bwd(d_act, gu):
    S = gu.shape[0]
    tm = _tile(S, ROWS // 2, 8)

    def kern(da_ref, g_ref, u_ref, o_ref):
        g = g_ref[...].astype(F32)
        u = u_ref[...].astype(F32)
        da = da_ref[...].astype(F32)
        sg = jax.nn.sigmoid(g)
        o_ref[:, :D_FF] = (da * u * (sg * (1.0 + g * (1.0 - sg)))).astype(BF)
        o_ref[:, D_FF:] = (da * (g * sg)).astype(BF)

    return pl.pallas_call(
        kern, name="swiglu_bwd", grid=(S // tm,),
        in_specs=[_row_spec(tm, D_FF), _row_spec(tm, D_FF, 0), _row_spec(tm, D_FF, 1)],
        out_specs=_row_spec(tm, 2 * D_FF),
        out_shape=jax.ShapeDtypeStruct((S, 2 * D_FF), BF),
        compiler_params=_cparams("parallel"),
    )(d_act, gu, gu)


def _split3(x):
    hi = x.astype(BF)
    r1 = x - hi.astype(F32)
    mid = r1.astype(BF)
    lo = (r1 - mid.astype(F32)).astype(BF)
    return hi, mid, lo


def _tri_dot(tri, x):
    return sum(jnp.dot(tri, part, preferred_element_type=F32) for part in _split3(x))


def _log_sigmoid(z):
    return jnp.minimum(z, 0.0) - jnp.log(1.0 + jnp.exp(-jnp.abs(z)))


def _fox_gate_fwd(pa, b_f_pad):
    S = pa.shape[0]
    T = _tile(S, 512, 8)
    f_col = OFF_F // LANES

    def kern(z_ref, b_ref, cum_ref, carry_ref):
        @pl.when(pl.program_id(0) == 0)
        def _():
            carry_ref[...] = jnp.zeros_like(carry_ref)

        log_f = _log_sigmoid(z_ref[...] + b_ref[...])
        row = lax.broadcasted_iota(jnp.int32, (T, T), 0)
        col = lax.broadcasted_iota(jnp.int32, (T, T), 1)
        tri = (col <= row).astype(BF)
        cum = _tri_dot(tri, log_f) + carry_ref[...]
        cum_ref[...] = cum
        carry_ref[...] = cum[T - 1:T, :]

    return pl.pallas_call(
        kern, name="fox_gate_fwd", grid=(S // T,),
        in_specs=[_row_spec(T, LANES, f_col), _vec_spec(LANES)],
        out_specs=_row_spec(T, LANES),
        out_shape=jax.ShapeDtypeStruct((S, LANES), F32),
        scratch_shapes=[pltpu.VMEM((1, LANES), F32)],
        compiler_params=_cparams("arbitrary"),
    )(pa, b_f_pad)


def _fox_gate_bwd(rowsum_ds, colsum_ds, pa, b_f_pad):
    S = pa.shape[0]
    T = _tile(S, 512, 8)
    nb = S // T
    f_col = OFF_F // LANES

    def kern(dr_ref, dc_ref, z_ref, b_ref, df_ref, dbf_ref, carry_ref):
        @pl.when(pl.program_id(0) == 0)
        def _():
            carry_ref[...] = jnp.zeros_like(carry_ref)
            dbf_ref[...] = jnp.zeros_like(dbf_ref)

        row = lax.broadcasted_iota(jnp.int32, (T, T), 0)
        col = lax.broadcasted_iota(jnp.int32, (T, T), 1)
        tri = (col >= row).astype(BF)
        rev = _tri_dot(tri, dr_ref[...] - dc_ref[...]) + carry_ref[...]
        carry_ref[...] = rev[0:1, :]
        z = z_ref[...] + b_ref[...]
        lane = lax.broadcasted_iota(jnp.int32, (T, LANES), 1)
        d_z = jnp.where(lane < B_HEADS, rev * jax.nn.sigmoid(-z), 0.0)
        df_ref[...] = d_z.astype(BF)
        dbf_ref[0:1, :] += _colsum(d_z)

    return pl.pallas_call(
        kern, name="fox_gate_bwd", grid=(nb,),
        in_specs=[pl.BlockSpec((T, LANES), lambda i: (nb - 1 - i, 0)),
                  pl.BlockSpec((T, LANES), lambda i: (nb - 1 - i, 0)),
                  pl.BlockSpec((T, LANES), lambda i: (nb - 1 - i, f_col)),
                  _vec_spec(LANES)],
        out_specs=[pl.BlockSpec((T, LANES), lambda i: (nb - 1 - i, 0)),
                   pl.BlockSpec((8, LANES), lambda i: (0, 0))],
        out_shape=[jax.ShapeDtypeStruct((S, LANES), BF), jax.ShapeDtypeStruct((8, LANES), F32)],
        scratch_shapes=[pltpu.VMEM((1, LANES), F32)],
        compiler_params=_cparams("arbitrary"),
    )(rowsum_ds, colsum_ds, pa, b_f_pad)


NEG_INF = float("-inf")
QK_SCALE = 1.0 / math.sqrt(HEAD_DIM)


def _half_mask(shape, half):
    lane = lax.broadcasted_iota(jnp.int32, shape, 1)
    return (lane < HEAD_DIM) if half == 0 else (lane >= HEAD_DIM)


def _bias_block(shape, terms, term_off, ones_lo, ones_hi):
    l64 = lax.broadcasted_iota(jnp.int32, shape, 1) & (HEAD_DIM - 1)
    out = jnp.where((l64 >= ones_lo) & (l64 < ones_hi), 1.0, 0.0)
    for t, term in enumerate(terms):
        out = jnp.where(l64 == term_off + t, term.astype(F32), out)
    return out


def _head_column(block, head):
    lane = lax.broadcasted_iota(jnp.int32, block.shape, 1)
    return jnp.sum(jnp.where(lane == head, block, 0.0), axis=1, keepdims=True)


def _crossed(shape, first, second):
    return jnp.where(_half_mask(shape, 0), second, first)


def _fox_prep_fwd(cum, T):
    S = cum.shape[0]
    shape = (T, LANES)

    def kern(c_ref, bq_ref, bk_ref):
        p_id = pl.program_id(0)
        cum_blk = c_ref[...]
        c3 = _split3(_crossed(shape, _head_column(cum_blk, 2 * p_id), _head_column(cum_blk, 2 * p_id + 1)))
        bq_ref[...] = _bias_block(shape, c3, 0, 3, 6).astype(BF)
        bk_ref[...] = _bias_block(shape, [-t.astype(F32) for t in c3], 3, 0, 3).astype(BF)

    out_spec = pl.BlockSpec((None, T, LANES), lambda p, i: (p, i, 0))
    out_shape = jax.ShapeDtypeStruct((B_HEADS // 2, S, LANES), BF)
    return pl.pallas_call(
        kern, name="fox_prep_fwd", grid=(B_HEADS // 2, S // T),
        in_specs=[pl.BlockSpec((T, LANES), lambda p, i: (i, 0))],
        out_specs=[out_spec, out_spec], out_shape=[out_shape, out_shape],
        compiler_params=_cparams("parallel", "parallel"),
    )(cum)


def _fox_fwd(p_b, bq, bk, T, comm=None):
    S = p_b.shape[0]
    nq = S // T
    n_pairs = B_HEADS // 2
    grid = (n_pairs, nq)

    def kern(*refs):
        (q_ref, k_ref, v_ref, bq_ref, bk_ref, o_ref, lse_ref), comm_refs = _own_refs(refs, comm, 5, 2, 0)
        _comm_edge(comm, comm_refs, grid, first=True)
        i = pl.program_id(1)
        rowcol = lax.broadcasted_iota(jnp.int32, (T, T), 0) - lax.broadcasted_iota(jnp.int32, (T, T), 1)
        hms = (_half_mask((T, LANES), 0), _half_mask((T, LANES), 1))
        q_scaled = (q_ref[...].astype(F32) * QK_SCALE).astype(BF)
        bq_blk = bq_ref[...]
        qs = [jnp.where(hms[h], q_scaled, bq_blk) for h in (0, 1)]

        def step(j, carry, masked):
            rows = pl.ds(pl.multiple_of(j * T, T), T)
            kj, bkj, vj = k_ref[rows, :], bk_ref[rows, :], v_ref[rows, :]
            new = []
            for half in (0, 1):
                m, l, acc = carry[half]
                s = lax.dot_general(qs[half], jnp.where(hms[half], kj, bkj), (((1,), (1,)), ((), ())),
                                    preferred_element_type=F32)
                if masked:
                    s = jnp.where(rowcol >= 0, s, NEG_INF)
                m_new = jnp.maximum(m, jnp.max(s, axis=1, keepdims=True))
                alpha = jnp.exp(m - m_new)
                p = jnp.exp(s - m_new)
                l_new = alpha * l + jnp.sum(p, axis=1, keepdims=True)
                acc_new = alpha * acc + jnp.dot(p.astype(BF), vj, preferred_element_type=F32)
                new.append((m_new, l_new, acc_new))
            return tuple(new)

        one = (jnp.full((T, 1), NEG_INF, F32), jnp.zeros((T, 1), F32), jnp.zeros((T, LANES), F32))
        carry = lax.fori_loop(0, i, functools.partial(step, masked=False), (one, one))
        (m0, l0, acc0), (m1, l1, acc1) = step(i, carry, True)
        hm0 = _half_mask((T, LANES), 0)
        o_ref[...] = jnp.where(hm0, acc0 / l0, acc1 / l1)
        lse_ref[...] = jnp.where(hm0, m0 + jnp.log(l0), m1 + jnp.log(l1))
        _comm_edge(comm, comm_refs, grid, first=False)

    out_spec = pl.BlockSpec((T, LANES), lambda p, i: (i, p))
    res = pl.pallas_call(
        kern, name="fox_fwd", grid=grid,
        in_specs=[pl.BlockSpec((T, LANES), lambda p, i: (i, OFF_QB // LANES + p)),
                  pl.BlockSpec((S, LANES), lambda p, i: (0, OFF_KB // LANES + p)),
                  pl.BlockSpec((S, LANES), lambda p, i: (0, OFF_VB // LANES + p)),
                  pl.BlockSpec((None, T, LANES), lambda p, i: (p, i, 0)),
                  pl.BlockSpec((None, S, LANES), lambda p, i: (p, 0, 0))] + _comm_specs(comm, "in"),
        out_specs=[out_spec, out_spec] + _comm_specs(comm, "out"),
        out_shape=[jax.ShapeDtypeStruct((S, n_pairs * LANES), F32)] * 2 + (comm.out_shapes if comm else []),
        scratch_shapes=comm.sem_shapes if comm else [],
        compiler_params=_cparams("arbitrary", "arbitrary"),
    )(p_b, p_b, p_b, bq, bk, *(comm.ins if comm else []))
    return res[0], res[1], res[2:]


def _fox_prep_bwd(cum, o, do, lse, T):
    S = o.shape[0]
    shape = (T, LANES)

    def kern(c_ref, o_ref, do_ref, lse_ref, bq_ref, bdo_ref):
        p_id = pl.program_id(0)
        cum_blk = c_ref[...]
        cq = _crossed(shape, _head_column(cum_blk, 2 * p_id), _head_column(cum_blk, 2 * p_id + 1))
        b3 = _split3(cq - pltpu.roll(lse_ref[...], HEAD_DIM, 1))
        bq_ref[...] = _bias_block(shape, b3, 0, 3, 6).astype(BF)
        dd = do_ref[...] * o_ref[...]
        delta = [jnp.sum(jnp.where(_half_mask(shape, h), dd, 0.0), axis=1, keepdims=True) for h in (0, 1)]
        d3 = _split3(-_crossed(shape, delta[0], delta[1]))
        bdo_ref[...] = _bias_block(shape, d3, 0, 0, 0).astype(BF)

    block = pl.BlockSpec((None, T, LANES), lambda p, i: (p, i, 0))
    tile = pl.BlockSpec((T, LANES), lambda p, i: (i, p))
    out_shape = jax.ShapeDtypeStruct((B_HEADS // 2, S, LANES), BF)
    return pl.pallas_call(
        kern, name="fox_prep_bwd", grid=(B_HEADS // 2, S // T),
        in_specs=[pl.BlockSpec((T, LANES), lambda p, i: (i, 0)), tile, tile, tile],
        out_specs=[block, block], out_shape=[out_shape, out_shape],
        compiler_params=_cparams("parallel", "parallel"),
    )(cum, o, do, lse)


def _fox_bwd(p_b, do, bq, bk, bdo, T, comm=None):
    S = p_b.shape[0]
    n_pairs = B_HEADS // 2
    nq = S // T
    grid = (n_pairs,)

    def kern(*refs):
        own, comm_refs = _own_refs(refs, comm, 7, 5, 0)
        q_ref, k_ref, v_ref, do_ref, bq_ref, bk_ref, bdo_ref, dq_ref, dk_ref, dv_ref, dck_ref, dcq_ref = own
        _comm_edge(comm, comm_refs, grid, first=True)
        p_id = pl.program_id(0)
        rowcol = lax.broadcasted_iota(jnp.int32, (T, T), 0) - lax.broadcasted_iota(jnp.int32, (T, T), 1)
        lane = lax.broadcasted_iota(jnp.int32, (T, LANES), 1)
        dk_ref[...] = jnp.zeros_like(dk_ref)
        dv_ref[...] = jnp.zeros_like(dv_ref)
        dck_ref[...] = jnp.zeros_like(dck_ref)

        @pl.when(p_id == 0)
        def _():
            dcq_ref[...] = jnp.zeros_like(dcq_ref)

        hms = (_half_mask((T, LANES), 0), _half_mask((T, LANES), 1))
        v_ones = _bias_block((T, LANES), [], 0, 0, 3).astype(BF)

        def outer(i, carry):
            qrows = pl.ds(pl.multiple_of(i * T, T), T)
            q_scaled = (q_ref[qrows, :].astype(F32) * QK_SCALE).astype(BF)
            do_b = do_ref[qrows, :].astype(BF)
            bq_i, bdo_i = bq_ref[qrows, :], bdo_ref[qrows, :]
            qa = [jnp.where(hms[h], q_scaled, bq_i) for h in (0, 1)]
            doa = [jnp.where(hms[h], do_b, bdo_i) for h in (0, 1)]
            q_own = [jnp.where(hms[h], q_scaled, 0) for h in (0, 1)]
            do_own = [jnp.where(hms[h], do_b, 0) for h in (0, 1)]

            def inner(j, carry_in, masked):
                krows = pl.ds(pl.multiple_of(j * T, T), T)
                kj, bkj, vj = k_ref[krows, :], bk_ref[krows, :], v_ref[krows, :]
                dv_add, dk_add, new = 0.0, 0.0, []
                for half in (0, 1):
                    dq, rs = carry_in[half]
                    ka = jnp.where(hms[half], kj, bkj)
                    s = lax.dot_general(qa[half], ka, (((1,), (1,)), ((), ())), preferred_element_type=F32)
                    if masked:
                        s = jnp.where(rowcol >= 0, s, NEG_INF)
                    p = jnp.exp(s)
                    ds = p * lax.dot_general(doa[half], jnp.where(hms[half], vj, v_ones),
                                             (((1,), (1,)), ((), ())), preferred_element_type=F32)
                    ds_b = ds.astype(BF)
                    dv_add = dv_add + lax.dot_general(p.astype(BF), do_own[half], (((0,), (0,)), ((), ())),
                                                      preferred_element_type=F32)
                    dk_add = dk_add + lax.dot_general(ds_b, q_own[half], (((0,), (0,)), ((), ())),
                                                      preferred_element_type=F32)
                    dck_ref[half:half + 1, krows] += jnp.sum(ds, axis=0, keepdims=True)
                    new.append((dq + jnp.dot(ds_b, jnp.where(hms[half], kj, 0), preferred_element_type=F32),
                                rs + jnp.sum(ds, axis=1, keepdims=True)))
                dv_ref[krows, :] += dv_add
                dk_ref[krows, :] += dk_add
                return tuple(new)

            one = (jnp.zeros((T, LANES), F32), jnp.zeros((T, 1), F32))
            carry_in = lax.fori_loop(0, i, functools.partial(inner, masked=False), (one, one))
            (dq0, rs0), (dq1, rs1) = inner(i, carry_in, True)
            dq_ref[qrows, :] = (dq0 + dq1) * QK_SCALE
            dcq_ref[qrows, :] = jnp.where(lane == 2 * p_id, rs0, jnp.where(lane == 2 * p_id + 1, rs1,
                                                                             dcq_ref[qrows, :]))
            return carry

        lax.fori_loop(0, nq, outer, 0)
        _comm_edge(comm, comm_refs, grid, first=False)

    block = pl.BlockSpec((None, S, LANES), lambda p: (p, 0, 0))
    pair = pl.BlockSpec((S, LANES), lambda p: (0, p))
    slab = lambda off: pl.BlockSpec((S, LANES), lambda p: (0, off // LANES + p))
    wide = jax.ShapeDtypeStruct((S, n_pairs * LANES), F32)
    res = pl.pallas_call(
        kern, name="fox_bwd", grid=grid,
        in_specs=[slab(OFF_QB), slab(OFF_KB), slab(OFF_VB), pair, block, block, block]
        + _comm_specs(comm, "in"),
        out_specs=[pair, pair, pair, pl.BlockSpec((None, 2, S), lambda p: (p, 0, 0)),
                   pl.BlockSpec((S, LANES), lambda p: (0, 0))] + _comm_specs(comm, "out"),
        out_shape=[wide, wide, wide, jax.ShapeDtypeStruct((n_pairs, 2, S), F32),
                   jax.ShapeDtypeStruct((S, LANES), F32)] + (comm.out_shapes if comm else []),
        scratch_shapes=comm.sem_shapes if comm else [],
        compiler_params=_cparams("arbitrary"),
    )(p_b, p_b, p_b, do, bq, bk, bdo, *(comm.ins if comm else []))
    return (*res[:5], res[5:])


SWA_TQ = 256
SWA_SUB = 8


def _swa_window(i, tq):
    start = pl.multiple_of(jnp.maximum(i * tq - WINDOW, 0), LANES)
    return start, i * tq - start


def _swa_valid(offset, tq):
    rel = offset + lax.broadcasted_iota(jnp.int32, (tq, tq + WINDOW), 0) \
        - lax.broadcasted_iota(jnp.int32, (tq, tq + WINDOW), 1)
    return (rel >= 0) & (rel < WINDOW)


def _swa_fwd(qk, v_arr, v_col, sinks):
    S = qk.shape[0]
    tq = min(SWA_TQ, S - WINDOW)
    sub = min(SWA_SUB, S // tq)
    win = tq + WINDOW

    def kern(q_ref, k_ref, v_ref, sink_ref, o_ref, lse_ref):
        p_id, i = pl.program_id(0), pl.program_id(1)
        hm0 = _half_mask((tq, LANES), 0)
        for t in range(sub):
            rows = slice(t * tq, (t + 1) * tq)
            start, offset = _swa_window(i * sub + t, tq)
            kw = k_ref[pl.ds(start, win), :]
            vw = v_ref[pl.ds(start, win), :].astype(BF)
            valid = _swa_valid(offset, tq)
            q = q_ref[rows, :]
            outs, lses = [], []
            for half in (0, 1):
                hm = _half_mask((tq, LANES), half)
                qh = (jnp.where(hm, q, 0).astype(F32) * QK_SCALE).astype(BF)
                s = lax.dot_general(qh, kw, (((1,), (1,)), ((), ())), preferred_element_type=F32)
                s = jnp.where(valid, s, NEG_INF)
                sink = sink_ref[2 * p_id + half]
                m = jnp.maximum(jnp.max(s, axis=1, keepdims=True), sink)
                p = jnp.exp(s - m)
                denom = jnp.sum(p, axis=1, keepdims=True) + jnp.exp(sink - m)
                outs.append(jnp.dot(p.astype(BF), vw, preferred_element_type=F32) / denom)
                lses.append(m + jnp.log(denom))
            o_ref[rows, :] = jnp.where(hm0, outs[0], outs[1])
            lse_ref[rows, :] = jnp.where(hm0, lses[0], lses[1])

    tile = pl.BlockSpec((sub * tq, LANES), lambda p, i: (i, p))
    return pl.pallas_call(
        kern, name="swa_fwd", grid=(A_Q_HEADS // 2, S // (sub * tq)),
        in_specs=[tile, pl.BlockSpec((S, LANES), lambda p, i: (0, A_Q_HEADS // 2)),
                  pl.BlockSpec((S, LANES), lambda p, i: (0, v_col)),
                  pl.BlockSpec(memory_space=pltpu.SMEM)],
        out_specs=[tile, tile],
        out_shape=[jax.ShapeDtypeStruct((S, A_Q_HEADS * HEAD_DIM), F32)] * 2,
        compiler_params=_cparams("parallel", "arbitrary"),
    )(qk, qk, v_arr, sinks)


def _swa_bwd(qk, v_arr, v_col, o_arr, do_arr, lse_arr, sinks, comm=None):
    S = qk.shape[0]
    tq = min(SWA_TQ, S - WINDOW)
    sub = min(SWA_SUB, S // tq)
    win = tq + WINDOW
    n_pairs = A_Q_HEADS // 2
    grid = (n_pairs, S // (sub * tq))

    def kern(*refs):
        own, comm_refs = _own_refs(refs, comm, 7, 4, 0)
        q_ref, k_ref, v_ref, o_ref, do_ref, lse_ref, sink_ref, dq_ref, dk_ref, dv_ref, dsink_ref = own
        _comm_edge(comm, comm_refs, grid, first=True)
        p_id, i = pl.program_id(0), pl.program_id(1)

        @pl.when((p_id == 0) & (i == 0))
        def _():
            dk_ref[...] = jnp.zeros_like(dk_ref)
            dv_ref[...] = jnp.zeros_like(dv_ref)

        @pl.when(i == 0)
        def _():
            dsink_ref[...] = jnp.zeros_like(dsink_ref)

        for t in range(sub):
            rows = slice(t * tq, (t + 1) * tq)
            start, offset = _swa_window(i * sub + t, tq)
            wrows = pl.ds(start, win)
            kw = k_ref[wrows, :]
            vw = v_ref[wrows, :].astype(BF)
            valid = _swa_valid(offset, tq)
            q, do, o, lse2 = q_ref[rows, :], do_ref[rows, :], o_ref[rows, :], lse_ref[rows, :]
            dq = jnp.zeros((tq, LANES), F32)
            dk = jnp.zeros((win, LANES), F32)
            dv = jnp.zeros((win, LANES), F32)
            for half in (0, 1):
                hm = _half_mask((tq, LANES), half)
                lane0 = half * HEAD_DIM
                qh = (jnp.where(hm, q, 0).astype(F32) * QK_SCALE).astype(BF)
                do_f = jnp.where(hm, do, 0.0)
                doh = do_f.astype(BF)
                delta = jnp.sum(do_f * o, axis=1, keepdims=True)
                lse = lse2[:, lane0:lane0 + 1]
                s = lax.dot_general(qh, kw, (((1,), (1,)), ((), ())), preferred_element_type=F32)
                p = jnp.exp(jnp.where(valid, s, NEG_INF) - lse)
                dp = lax.dot_general(doh, vw, (((1,), (1,)), ((), ())), preferred_element_type=F32)
                ds_b = (p * (dp - delta)).astype(BF)
                dv = dv + lax.dot_general(p.astype(BF), doh, (((0,), (0,)), ((), ())),
                                          preferred_element_type=F32)
                dk = dk + lax.dot_general(ds_b, qh, (((0,), (0,)), ((), ())), preferred_element_type=F32)
                kh = jnp.where(_half_mask((win, LANES), half), kw, 0)
                dq = dq + jnp.dot(ds_b, kh, preferred_element_type=F32)
                p_sink = jnp.exp(sink_ref[2 * p_id + half] - lse)
                dsink_ref[0, half:half + 1, :] += jnp.broadcast_to(
                    -jnp.sum(p_sink * delta, axis=0, keepdims=True), (1, LANES))
            dq_ref[rows, :] = dq * QK_SCALE
            dk_ref[wrows, :] += dk
            dv_ref[wrows, :] += dv
        _comm_edge(comm, comm_refs, grid, first=False)

    tile = pl.BlockSpec((sub * tq, LANES), lambda p, i: (i, p))
    whole = lambda col: pl.BlockSpec((S, LANES), lambda p, i: (0, col))
    res = pl.pallas_call(
        kern, name="swa_bwd", grid=grid,
        in_specs=[tile, whole(n_pairs), whole(v_col), tile, tile, tile,
                  pl.BlockSpec(memory_space=pltpu.SMEM)] + _comm_specs(comm, "in"),
        out_specs=[tile, whole(0), whole(0),
                   pl.BlockSpec((1, 8, LANES), lambda p, i: (p, 0, 0))] + _comm_specs(comm, "out"),
        out_shape=[jax.ShapeDtypeStruct((S, A_Q_HEADS * HEAD_DIM), F32),
                   jax.ShapeDtypeStruct((S, LANES), F32), jax.ShapeDtypeStruct((S, LANES), F32),
                   jax.ShapeDtypeStruct((n_pairs, 8, LANES), F32)] + (comm.out_shapes if comm else []),
        scratch_shapes=comm.sem_shapes if comm else [],
        compiler_params=_cparams("arbitrary", "arbitrary"),
    )(qk, qk, v_arr, o_arr, do_arr, lse_arr, sinks, *(comm.ins if comm else []))
    return (*res[:4], res[4:])


ADAMW_BLOCK = 512 * 1024


def _adamw(w, g, m, v, name, comm=None):
    R, C = w.shape
    tr, tc = _tile(R, max(8, ADAMW_BLOCK // C), 8), C
    grid = (R // tr, C // tc)

    def kern(*refs):
        (w_ref, g_ref, m_ref, v_ref, d_ref, mo_ref, vo_ref), comm_refs = _own_refs(refs, comm, 4, 3, 0)
        _comm_edge(comm, comm_refs, grid, first=True)
        g_ = g_ref[...]
        m_new = ADAM_B1 * m_ref[...] + (1.0 - ADAM_B1) * g_
        v_new = ADAM_B2 * v_ref[...] + (1.0 - ADAM_B2) * (g_ * g_)
        m_hat = m_new / (1.0 - ADAM_B1 ** ADAM_STEP)
        v_hat = v_new / (1.0 - ADAM_B2 ** ADAM_STEP)
        d_ref[...] = -ADAM_LR * (m_hat / (jnp.sqrt(v_hat) + ADAM_EPS) + ADAM_WD * w_ref[...])
        mo_ref[...] = m_new
        vo_ref[...] = v_new
        _comm_edge(comm, comm_refs, grid, first=False)

    spec = pl.BlockSpec((tr, tc), lambda i, j: (i, j))
    shape = jax.ShapeDtypeStruct((R, C), F32)
    res = pl.pallas_call(
        kern, name=name, grid=grid,
        in_specs=[spec] * 4 + _comm_specs(comm, "in"),
        out_specs=[spec] * 3 + _comm_specs(comm, "out"),
        out_shape=[shape] * 3 + (comm.out_shapes if comm else []),
        scratch_shapes=comm.sem_shapes if comm else [],
        input_output_aliases={4 + i: 3 + o for i, o in comm.aliases.items()} if comm else {},
        compiler_params=_cparams("arbitrary", "arbitrary"),
    )(w, g, m, v, *(comm.ins if comm else []))
    return (res[:3], res[3:]) if comm else res


def _index_operand(i):
    return jnp.reshape(i, (1,)).astype(jnp.int32)


def _add_pair(whole, got, ci, name):
    P, R, C = whole.shape
    half = R // 2
    tr = _tile(half, ROWS, 16)
    nb = half // tr

    def kern(ci_ref, a_ref, b_ref, o_ref, ob_ref):
        s = a_ref[...] + b_ref[...].astype(F32)
        o_ref[...] = s
        ob_ref[...] = s.astype(BF)

    spec = pl.BlockSpec((None, tr, C), lambda p, i, ci_ref: (p, i, 0))
    return pl.pallas_call(
        kern, name=name,
        grid_spec=pltpu.PrefetchScalarGridSpec(
            num_scalar_prefetch=1, grid=(P, nb),
            in_specs=[pl.BlockSpec((None, tr, C), lambda p, i, ci_ref: (p, ci_ref[0] * nb + i, 0)), spec],
            out_specs=[spec, spec]),
        out_shape=[jax.ShapeDtypeStruct((P, half, C), F32), jax.ShapeDtypeStruct((P, half, C), BF)],
        compiler_params=_cparams("parallel", "parallel"),
    )(_index_operand(ci), whole, got)


def _add_three(parts, recv, chip, name):
    _, R, C = parts.shape
    tr = _tile(R, ROWS, 16)

    def kern(chip_ref, o_ref, r0_ref, r1_ref, r2_ref, out_ref):
        s = ((o_ref[...] + r0_ref[...].astype(F32)) + r1_ref[...].astype(F32)) + r2_ref[...].astype(F32)
        out_ref[0] = s
        out_ref[1] = s

    slab = lambda k: pl.BlockSpec((None, tr, C), lambda i, chip_ref: (k, i, 0))
    return pl.pallas_call(
        kern, name=name,
        grid_spec=pltpu.PrefetchScalarGridSpec(
            num_scalar_prefetch=1, grid=(R // tr,),
            in_specs=[pl.BlockSpec((None, tr, C), lambda i, chip_ref: (chip_ref[0], i, 0)),
                      slab(0), slab(1), slab(2)],
            out_specs=pl.BlockSpec((2, tr, C), lambda i, chip_ref: (0, i, 0))),
        out_shape=jax.ShapeDtypeStruct((2, R, C), F32),
        compiler_params=_cparams("parallel"),
    )(_index_operand(chip), parts, recv, recv, recv)


SM_ADA, SM_G, SM_LOSS, SM_BF, SM_SINK, SM_LEN = 0, 6144, 10240, 11264, 11272, 12288


def _small_finalize(gathered):
    def kern(g_ref, tot_ref, loss_ref):
        tot = g_ref[0:1, :]
        for b in range(1, N_DEV):
            tot = tot + g_ref[b:b + 1, :]
        tot_ref[...] = tot
        sq = jnp.sum(tot[:, SM_LOSS:SM_LOSS + D_MODEL], axis=1, keepdims=True)
        loss_ref[...] = jnp.broadcast_to(sq * (0.5 / D_MODEL), (1, LANES))

    full = lambda shape: pl.BlockSpec(shape, lambda i: (0, 0))
    return pl.pallas_call(
        kern, name="small_finalize", grid=(1,),
        in_specs=[full((N_DEV, SM_LEN))],
        out_specs=[full((1, SM_LEN)), full((1, LANES))],
        out_shape=[jax.ShapeDtypeStruct((1, SM_LEN), F32), jax.ShapeDtypeStruct((1, LANES), F32)],
        compiler_params=_cparams("arbitrary"),
    )(gathered)


def _ada_dw(c_t, d_ada):
    N = d_ada.shape[1]
    tn = _tile(N, 512)

    def kern(c_ref, d_ref, o_ref):
        acc = c_ref[:, 0:1] * d_ref[0:1, :]
        for b in range(1, N_DEV):
            acc = acc + c_ref[:, b:b + 1] * d_ref[b:b + 1, :]
        o_ref[...] = acc

    return pl.pallas_call(
        kern, name="ada_dw", grid=(N // tn,),
        in_specs=[pl.BlockSpec((D_MODEL, N_DEV), lambda j: (0, 0)), pl.BlockSpec((N_DEV, tn), lambda j: (0, j))],
        out_specs=pl.BlockSpec((D_MODEL, tn), lambda j: (0, j)),
        out_shape=jax.ShapeDtypeStruct((D_MODEL, N), F32),
        compiler_params=_cparams("parallel"),
    )(c_t, d_ada)


def _here():
    return lax.axis_index("x"), lax.axis_index("y"), lax.axis_index("c")


def _other_chips(x, y):
    return [(1 - x, y), (x, 1 - y), (1 - x, 1 - y)]


_ANY = pl.BlockSpec(memory_space=pl.ANY)


class _Comm:
    def __init__(self, ins, out_shapes, sem_shapes, start, finish, aliases=None):
        self.ins, self.out_shapes, self.sem_shapes = list(ins), list(out_shapes), list(sem_shapes)
        self.start, self.finish = start, finish
        self.aliases = dict(aliases or {})

    def split(self, refs, n_in, n_out, n_scratch):
        a = n_in + len(self.ins)
        b = a + n_out + len(self.out_shapes)
        own = list(refs[:n_in]) + list(refs[a:a + n_out]) + list(refs[b:b + n_scratch])
        mine = (refs[n_in:a], refs[a + n_out:b], refs[b + n_scratch:])
        return own, mine


def _run_comm(comm, name):
    n_in, n_out = len(comm.ins), len(comm.out_shapes)

    def body(*refs):
        parts = (refs[:n_in], refs[n_in:n_in + n_out], refs[n_in + n_out:])
        comm.start(*parts)
        comm.finish(*parts)

    return pl.pallas_call(
        body, name=name,
        in_specs=[_ANY] * n_in, out_specs=[_ANY] * n_out,
        out_shape=comm.out_shapes, scratch_shapes=comm.sem_shapes,
        input_output_aliases=comm.aliases,
    )(*comm.ins)


def _gather_comm(blocks):
    L = len(blocks)

    def parts(ins, outs, sems):
        send_sems, recv_sems, local_sems = sems
        x, y, c = _here()
        me, sibling = (x, y, c), (x, y, 1 - c)
        chips = _other_chips(x, y)

        def slot(px, py, pc):
            return 4 * px + 2 * py + pc

        def copy(l, k, block, to, src=None):
            dst = outs[l].at[slot(*block)]
            return pltpu.make_async_remote_copy(
                src_ref=dst if src is None else src, dst_ref=dst,
                send_sem=send_sems.at[l, k], recv_sem=recv_sems.at[l, k],
                device_id=to, device_id_type=MESH)

        mine = [pltpu.make_async_copy(ins[l], outs[l].at[slot(*me)], local_sems.at[l]) for l in range(L)]
        first = []
        for l in range(L):
            first.append(copy(l, 0, me, sibling, src=ins[l]))
            for j, chip in enumerate(chips):
                first.append(copy(l, 1 + j, me, (*chip, c), src=ins[l]))
        return c, me, sibling, chips, copy, mine, first

    def start(ins, outs, sems):
        *_, mine, first = parts(ins, outs, sems)
        for cp in mine + first:
            cp.start()

    def finish(ins, outs, sems):
        c, me, sibling, chips, copy, mine, first = parts(ins, outs, sems)
        passed = []
        for j, chip in enumerate(chips):
            for l in range(L):
                copy(l, 1 + j, (*chip, c), me).wait_recv()
                fwd = copy(l, 4 + j, (*chip, c), sibling)
                fwd.start()
                passed.append(fwd)
        for l in range(L):
            copy(l, 0, sibling, me).wait_recv()
        for j, chip in enumerate(chips):
            for l in range(L):
                copy(l, 4 + j, (*chip, 1 - c), me).wait_recv()
        for cp in first + passed:
            cp.wait_send()
        for cp in mine:
            cp.wait()

    return _Comm(blocks, [jax.ShapeDtypeStruct((N_DEV,) + b.shape, b.dtype) for b in blocks],
                 [pltpu.SemaphoreType.DMA((L, 7)), pltpu.SemaphoreType.DMA((L, 7)), pltpu.SemaphoreType.DMA((L,))],
                 start, finish)


def _allgather8(blocks, name):
    return _run_comm(_gather_comm(blocks), name)


def _swap_comm(arrs):
    L = len(arrs)

    def copies(ins, outs, sems):
        send_sems, recv_sems = sems
        x, y, c = _here()
        cps = []
        for l in range(L):
            half = arrs[l].shape[1] // 2
            rows = pl.ds(pl.multiple_of((1 - c) * half, 16), half)
            cps.append(pltpu.make_async_remote_copy(
                src_ref=ins[l].at[:, rows, :], dst_ref=outs[l], send_sem=send_sems.at[l],
                recv_sem=recv_sems.at[l], device_id=(x, y, 1 - c), device_id_type=MESH))
        return cps

    def start(ins, outs, sems):
        for cp in copies(ins, outs, sems):
            cp.start()

    def finish(ins, outs, sems):
        for cp in copies(ins, outs, sems):
            cp.wait()

    return _Comm(arrs, [jax.ShapeDtypeStruct((a.shape[0], a.shape[1] // 2, a.shape[2]), a.dtype) for a in arrs],
                 [pltpu.SemaphoreType.DMA((L,)), pltpu.SemaphoreType.DMA((L,))], start, finish)


def _join_comm(bufs):
    L = len(bufs)

    def start(ins, outs, sems):
        send_sems, recv_sems = sems
        x, y, c = _here()
        for l in range(L):
            pltpu.make_async_remote_copy(src_ref=outs[l].at[c], dst_ref=outs[l].at[c], send_sem=send_sems.at[l],
                                         recv_sem=recv_sems.at[l], device_id=(x, y, 1 - c),
                                         device_id_type=MESH).start()

    def finish(ins, outs, sems):
        send_sems, recv_sems = sems
        x, y, c = _here()
        for l in range(L):
            pltpu.make_async_remote_copy(src_ref=outs[l].at[c], dst_ref=outs[l].at[1 - c],
                                         send_sem=send_sems.at[l], recv_sem=recv_sems.at[l],
                                         device_id=(x, y, 1 - c), device_id_type=MESH).wait()

    return _Comm(bufs, [jax.ShapeDtypeStruct(a.shape, a.dtype) for a in bufs],
                 [pltpu.SemaphoreType.DMA((L,)), pltpu.SemaphoreType.DMA((L,))], start, finish,
                 aliases={l: l for l in range(L)})


def _scatter_comm(arrs):
    L = len(arrs)

    def copies(ins, outs, sems):
        send_sems, recv_sems = sems
        x, y, c = _here()
        return [pltpu.make_async_remote_copy(
            src_ref=ins[l].at[2 * tx + ty], dst_ref=outs[l].at[j],
            send_sem=send_sems.at[l, j], recv_sem=recv_sems.at[l, j],
            device_id=(tx, ty, c), device_id_type=MESH)
            for l in range(L) for j, (tx, ty) in enumerate(_other_chips(x, y))]

    def start(ins, outs, sems):
        for cp in copies(ins, outs, sems):
            cp.start()

    def finish(ins, outs, sems):
        for cp in copies(ins, outs, sems):
            cp.wait()

    return _Comm(arrs, [jax.ShapeDtypeStruct((3,) + a.shape[1:], a.dtype) for a in arrs],
                 [pltpu.SemaphoreType.DMA((L, 3)), pltpu.SemaphoreType.DMA((L, 3))], start, finish)


_A_ORDER = np.array(A_HEAD_ORDER)
_A_INVERSE = np.argsort(_A_ORDER)


def _permute_in_weights(w_in):
    qa = w_in[:, 0:512].reshape(D_MODEL, A_Q_HEADS, HEAD_DIM)[:, _A_ORDER, :].reshape(D_MODEL, 512)
    f_pad = jnp.pad(w_in[:, 2304:2312], ((0, 0), (0, LANES - B_HEADS)))
    w_a = jnp.concatenate([qa, w_in[:, 512:640], f_pad], axis=1)
    return w_a, w_in[:, 640:2304], w_in[:, 2312:4360]


def _slab_segments():
    segs = [(h * HEAD_DIM, int(_A_INVERSE[h]) * HEAD_DIM, HEAD_DIM) for h in range(A_Q_HEADS)]
    segs += [(512, OFF_KA, 128), (640, W_A + OFF_VA, 128), (768, W_A + OFF_QB, 1536),
             (2304, OFF_F, B_HEADS), (2312, W_A + W_B, W_G)]
    return segs


def _shard_slabs(dw_perm):
    R = dw_perm.shape[0]
    tr = _tile(R, 128, 8)
    plan = []
    for k in range(N_CHIP):
        for b in range(W_SHARD_PAD // LANES):
            lo, hi = k * W_SHARD + b * LANES, min(k * W_SHARD + (b + 1) * LANES, (k + 1) * W_SHARD)
            parts = []
            for o0, s0, n in _slab_segments():
                a, z = max(lo, o0), min(hi, o0 + n)
                while a < z:
                    s = s0 + (a - o0)
                    run = min(z - a, LANES - s % LANES)
                    parts.append((s // LANES, ((a - lo) - s % LANES) % LANES, a - lo, run))
                    a += run
            plan.append((k, b, parts))

    def kern(x_ref, o32_ref, obf_ref):
        lane = lax.broadcasted_iota(jnp.int32, (tr, LANES), 1)
        for k, b, parts in plan:
            acc = jnp.zeros((tr, LANES), F32)
            for src, rot, first, run in parts:
                blk = x_ref[:, src * LANES:(src + 1) * LANES]
                if rot:
                    blk = pltpu.roll(blk, rot, 1)
                acc = jnp.where((lane >= first) & (lane < first + run), blk, acc)
            o32_ref[k, :, b * LANES:(b + 1) * LANES] = acc
            obf_ref[k, :, b * LANES:(b + 1) * LANES] = acc.astype(BF)

    out_spec = pl.BlockSpec((N_CHIP, tr, W_SHARD_PAD), lambda i: (0, i, 0))
    return tuple(pl.pallas_call(
        kern, name="shard_slabs", grid=(R // tr,),
        in_specs=[pl.BlockSpec((tr, W_PERM), lambda i: (i, 0))],
        out_specs=[out_spec, out_spec],
        out_shape=[jax.ShapeDtypeStruct((N_CHIP, R, W_SHARD_PAD), F32),
                   jax.ShapeDtypeStruct((N_CHIP, R, W_SHARD_PAD), BF)],
        compiler_params=_cparams("parallel"),
    )(dw_perm))


class _NoExchange:
    def __init__(self, w_in, rest):
        self.w_in_whole, self.rest, self.grads = w_in, rest, {}

    def w_in_comm(self):
        return None

    def w_in(self, outs):
        return self.w_in_whole

    def rest_weights_comm(self):
        return None

    def rest_weights(self, outs):
        return self.rest

    def swap_comm(self, pieces, tag):
        self.grads[tag] = [p32 for p32, _ in pieces]
        return None

    def swap_done(self, outs, tag):
        return None

    def reduce_done(self, outs, tag):
        pass

    def join_comm(self):
        return None


class _Exchange:
    def __init__(self, ci, chip, w_in_shard, rest_shards):
        self.ci, self.chip, self.w_in_shard, self.rest_shards = ci, chip, w_in_shard, rest_shards
        self.pieces, self.part_f32, self.halves = {}, {}, {}

    def _my_half(self, a, axis=0, other=False):
        rows = a.shape[axis] // 2
        return lax.dynamic_slice_in_dim(a, ((1 - self.ci) if other else self.ci) * rows, rows, axis=axis)

    def w_in_comm(self):
        return _gather_comm([self._my_half(self.w_in_shard).astype(BF)])

    def w_in(self, outs):
        return _col_sharded(outs[0])

    def rest_weights_comm(self):
        return _gather_comm([self._my_half(w).astype(BF) for w in self.rest_shards])

    def rest_weights(self, outs):
        w_ba, w_bb, w_out, w_fi, w_fo = outs
        return (_col_sharded(w_ba), _col_sharded(w_bb), _row_sharded(w_out), _col_sharded(w_fi),
                _row_sharded(w_fo))

    def swap_comm(self, pieces, tag):
        self.pieces[tag] = pieces
        return _swap_comm([pbf for _, pbf in pieces])

    def swap_done(self, got, tag):
        self.part_f32[tag], part_bf = [], []
        for l, ((p32, _), g_) in enumerate(zip(self.pieces[tag], got)):
            s32, sbf = _add_pair(p32, g_, self.ci, f"chip_sum_{tag}_{l}")
            self.part_f32[tag].append(s32)
            part_bf.append(sbf)
        return _scatter_comm(part_bf)

    def reduce_done(self, outs, tag):
        self.halves[tag] = [_add_three(p32, r, self.chip, f"shard_sum_{tag}_{l}")
                            for l, (p32, r) in enumerate(zip(self.part_f32[tag], outs))]

    def join_comm(self):
        return _join_comm(self.halves["late"] + self.halves["early"])


def _col_sharded(g):
    return jnp.transpose(g.reshape(N_CHIP, -1, g.shape[-1]), (1, 0, 2)).reshape(2 * g.shape[1], N_CHIP * g.shape[-1])


def _row_sharded(g):
    return g.reshape(N_DEV * g.shape[1], g.shape[-1])


def _rope_tables(pos):
    inv_freq = 1.0 / (ROPE_THETA ** (jnp.arange(0, HEAD_DIM, 2, dtype=F32) / HEAD_DIM))
    ang = pos.astype(F32)[:, None] * inv_freq
    cos, sin = jnp.cos(ang), jnp.sin(ang)
    return jnp.tile(cos, (1, 4)), jnp.tile(jnp.concatenate([-sin, sin], axis=1), (1, 2))


def _local_step(x, pos, ada, g1, g2, g3, g4, b_f, sinks, exch, target):
    S = x.shape[0]
    t_fox = _tile(S, 512, LANES) if S >= 1024 else S // 2
    t_fox_fwd = _tile(S, 1024, LANES) if S >= 2048 else S // 2
    shift_m, scale_m, gate_m, shift_f, scale_f, gate_f = [ada[i:i + 1] for i in range(N_ADA)]
    cos_t, sin_t = _rope_tables(pos)
    sinks_p = sinks.reshape(A_KV_HEADS, 4).T.reshape(A_Q_HEADS)
    b_f_pad = jnp.pad(b_f, (0, LANES - B_HEADS)).reshape(1, LANES)

    h1, outs = _pre_norm(x, g1, scale_m, shift_m, "pre_mix_norm", comm=exch.w_in_comm())
    w_a, w_b, w_g = _permute_in_weights(exch.w_in(outs))
    w_perm = jnp.concatenate([w_a, w_b, w_g], axis=1)
    p_a = _mm(h1, w_a, "nn", F32, "proj_a")
    p_b = _mm(h1, w_b, "nn", BF, "proj_b")
    p_g = _mm(h1, w_g, "nn", BF, "proj_g")
    (qk_a,) = _rope([p_a], [640], cos_t, sin_t, "rope_fwd")
    o_a, lse_a = _swa_fwd(qk_a, p_b, 0, sinks_p)
    cum = _fox_gate_fwd(p_a, b_f_pad)
    bq, bk = _fox_prep_fwd(cum, t_fox)
    comm = exch.rest_weights_comm()
    o_b, lse_b, outs = _fox_fwd(p_b, bq, bk, t_fox_fwd, comm=comm)
    w_ba, w_bb, w_out, w_fi, w_fo = exch.rest_weights(outs)
    w_ba_p = w_ba.reshape(A_Q_HEADS, HEAD_DIM, D_MODEL)[_A_ORDER].reshape(512, D_MODEL)
    pa = _mm(o_a, w_ba_p, "nn", BF, "branch_a")
    pb = _mm(o_b, w_bb, "nn", BF, "branch_b")
    merged = _merge_fwd(p_g, pa, pb)
    y1 = _mm(merged, w_out, "nn", BF, "out_proj")
    x2, h2 = _post_pre(x, y1, g2, gate_m, g3, scale_f, shift_f)
    gu = _mm(h2, w_fi, "nn", BF, "ffn_in")
    act = _swiglu_fwd(gu)
    y2 = _mm(act, w_fo, "nn", BF, "ffn_out")
    d_out, d_y2, st_f = _final(x2, y2, g4, gate_f, target)

    d_act = _mm(d_y2, w_fo, "nt", BF, "ffn_out_dx")
    row_pieces = lambda pair: tuple(t.reshape(N_CHIP, t.shape[0] // N_CHIP, t.shape[1]) for t in pair)
    dw_fo = row_pieces(_mm(act, d_y2, "tn", F32, "ffn_out_dw", twin=True))
    d_gu = _swiglu_bwd(d_act, gu)
    d_h2 = _mm(d_gu, w_fi, "nt", BF, "ffn_in_dx")
    dw_fi = _mm(h2, d_gu, "tn", F32, "ffn_in_dw", col_pieces=N_CHIP, twin=True)
    d_x2, d_y1, st_m = _mid_bwd(d_h2, x2, d_out, y1, g3, scale_f, g2, gate_m)
    d_merged = _mm(d_y1, w_out, "nt", BF, "out_proj_dx")
    dw_out = row_pieces(_mm(merged, d_y1, "tn", F32, "out_proj_dw", twin=True))
    d_pa, d_pb, d_ga, d_gb = _merge_bwd(d_merged, p_g, pa, pb)
    d_oa = _mm(d_pa, w_ba_p, "nt", F32, "branch_a_dx")
    dw_ba_p = _mm(o_a, d_pa, "tn", F32, "branch_a_dw", col_pieces=N_CHIP, twin=True)
    d_ob = _mm(d_pb, w_bb, "nt", F32, "branch_b_dx")
    dw_bb = _mm(o_b, d_pb, "tn", F32, "branch_b_dw", col_pieces=N_CHIP, twin=True)
    head_rows = lambda t: t.reshape(N_CHIP, A_Q_HEADS, HEAD_DIM, -1)[:, _A_INVERSE].reshape(t.shape)
    dw_ba = tuple(head_rows(t) for t in dw_ba_p)
    comm = exch.swap_comm([dw_ba, dw_bb, dw_out, dw_fi, dw_fo], "early")
    dq_a, dk_a, dv_a, d_sink, outs = _swa_bwd(qk_a, p_b, 0, o_a, d_oa, lse_a, sinks_p, comm=comm)
    comm = exch.swap_done(outs, "early")
    bq_bwd, bdo = _fox_prep_bwd(cum, o_b, d_ob, lse_b, t_fox)
    dq_b, dk_b, dv_b, d_ck, d_cq, outs = _fox_bwd(p_b, d_ob, bq_bwd, bk, bdo, t_fox, comm=comm)
    exch.reduce_done(outs, "early")
    d_qa, d_ka = _rope([dq_a, dk_a], [512, LANES], cos_t, -sin_t, "rope_bwd")
    d_ck_cols = jnp.pad(d_ck.reshape(B_HEADS, S).T, ((0, 0), (0, LANES - B_HEADS)))
    d_f, d_bf = _fox_gate_bwd(d_cq, d_ck_cols, p_a, b_f_pad)
    d_proj = jnp.concatenate([d_qa, d_ka, d_f, dv_a.astype(BF), dq_b.astype(BF), dk_b.astype(BF),
                              dv_b.astype(BF), d_ga, d_gb], axis=1)
    dw_perm = _mm(h1, d_proj, "tn", F32, "proj_dw")
    swap = exch.swap_comm([_shard_slabs(dw_perm)], "late")
    comm = exch.swap_done(_run_comm(swap, "grads_to_sibling_late") if swap else None, "late")
    res = _mm(d_proj, w_perm, "nt", BF, "proj_dx", comm=comm)
    d_h1 = res[0] if comm else res
    exch.reduce_done(res[1] if comm else None, "late")
    grad_x, st_p, outs = _pre_bwd(d_h1, x, d_x2, g1, scale_m, comm=exch.join_comm())
    exch.joined = outs

    d_sinks = d_sink[:, :2, 0].T.reshape(A_Q_HEADS)
    small = jnp.concatenate([
        st_p[0], st_p[1], st_m[3], st_m[0], st_m[1], st_f[0],
        st_p[2], st_m[4], st_m[2], st_f[1],
        st_f[2], d_bf[0, :B_HEADS], d_sinks,
        jnp.zeros((SM_LEN - SM_SINK - A_Q_HEADS,), F32)])
    return grad_x, small


def kernel(x, c, positions, w_ada, b_ada, g_pre_mix, g_post_mix, w_in, b_f, sinks, w_branch_a, w_branch_b, w_out, g_pre_ffn, g_post_ffn, w_ffn_in, w_ffn_out, loss_target, m_w_ada, m_b_ada, m_g_pre_mix, m_g_post_mix, m_w_in, m_b_f, m_sinks, m_w_branch_a, m_w_branch_b, m_w_out, m_g_pre_ffn, m_g_post_ffn, m_w_ffn_in, m_w_ffn_out, v_w_ada, v_b_ada, v_g_pre_mix, v_g_post_mix, v_w_in, v_b_f, v_sinks, v_w_branch_a, v_w_branch_b, v_w_out, v_g_pre_ffn, v_g_post_ffn, v_w_ffn_in, v_w_ffn_out):
    xi, yi, ci = _here()
    chip = 2 * xi + yi
    dev = 2 * chip + ci

    (c_g,) = _allgather8([c.reshape(8, LANES)], "gather_c")
    c_all = c_g.reshape(N_DEV, D_MODEL)
    exch = _Exchange(ci, chip, w_in[0], [w_branch_a[0], w_branch_b[0], w_out[0], w_ffn_in[0], w_ffn_out[0]])

    ada_cols = _mm(c_all, w_ada[0], "nn", F32, "ada_fwd")
    (ada_g,) = _allgather8([ada_cols], "gather_ada")
    ada_mine = lax.dynamic_index_in_dim(ada_g.reshape(N_CHIP, 2, N_DEV, -1)[:, 0], dev, axis=1, keepdims=False)
    ada = (ada_mine.reshape(-1) + b_ada[0]).reshape(N_ADA, D_MODEL)

    grad_x, small = _local_step(
        x[0], positions[0], ada, g_pre_mix, g_post_mix, g_pre_ffn, g_post_ffn, b_f[0], sinks[0],
        exch, loss_target[0])

    g_w_in, g_w_ba, g_w_bb, g_w_out, g_w_fi, g_w_fo = [j.reshape(2 * j.shape[1], j.shape[2]) for j in exch.joined]
    upd_fi, (small_g,) = _adamw(w_ffn_in[0], g_w_fi, m_w_ffn_in[0], v_w_ffn_in[0], "adamw_w_ffn_in",
                                comm=_gather_comm([small.reshape(8, SM_LEN // 8)]))

    small_all = small_g.reshape(N_DEV, SM_LEN)
    small_tot, loss_row = _small_finalize(small_all)
    loss = loss_row[0, 0]
    d_ada_cols = lax.dynamic_slice_in_dim(small_all[:, :N_ADA * D_MODEL], chip * (N_ADA * D_MODEL // N_CHIP),
                                          N_ADA * D_MODEL // N_CHIP, axis=1)
    g_w_ada = _ada_dw(c_all.T, d_ada_cols)

    def small_vec(b_ada_, g1_, g2_, g3_, g4_, b_f_, sinks_):
        return jnp.concatenate([b_ada_[0], g1_[0], g2_[0], g3_[0], g4_[0], jnp.zeros((D_MODEL,), F32),
                                b_f_[0], sinks_[0], jnp.zeros((SM_LEN - SM_SINK - A_Q_HEADS,), F32)]
                               ).reshape(8, SM_LEN // 8)

    sw = small_vec(b_ada, g_pre_mix, g_post_mix, g_pre_ffn, g_post_ffn, b_f, sinks)
    sm = small_vec(m_b_ada, m_g_pre_mix, m_g_post_mix, m_g_pre_ffn, m_g_post_ffn, m_b_f, m_sinks)
    sv = small_vec(v_b_ada, v_g_pre_mix, v_g_post_mix, v_g_pre_ffn, v_g_post_ffn, v_b_f, v_sinks)
    s_upd = [u.reshape(SM_LEN) for u in _adamw(sw, small_tot.reshape(8, SM_LEN // 8), sm, sv, "adamw_small")]
    s_grad = small_tot.reshape(SM_LEN)

    def unpack(vec):
        row = lambda a, n: vec[a:a + n].reshape(1, n)
        return dict(b_ada=row(SM_ADA, N_ADA * D_MODEL), g_pre_mix=row(SM_G, D_MODEL),
                    g_post_mix=row(SM_G + D_MODEL, D_MODEL), g_pre_ffn=row(SM_G + 2 * D_MODEL, D_MODEL),
                    g_post_ffn=row(SM_G + 3 * D_MODEL, D_MODEL), b_f=row(SM_BF, B_HEADS),
                    sinks=row(SM_SINK, A_Q_HEADS))

    big = dict(
        w_ada=(w_ada, g_w_ada, m_w_ada, v_w_ada),
        w_branch_a=(w_branch_a, g_w_ba, m_w_branch_a, v_w_branch_a),
        w_branch_b=(w_branch_b, g_w_bb, m_w_branch_b, v_w_branch_b),
        w_out=(w_out, g_w_out, m_w_out, v_w_out),
        w_ffn_out=(w_ffn_out, g_w_fo, m_w_ffn_out, v_w_ffn_out))
    grads, deltas, new_m, new_v = unpack(s_grad), unpack(s_upd[0]), unpack(s_upd[1]), unpack(s_upd[2])
    grads["w_ffn_in"], deltas["w_ffn_in"], new_m["w_ffn_in"], new_v["w_ffn_in"] = [
        t[None] for t in (g_w_fi, *upd_fi)]
    for n, (w_, g_, m_, v_) in big.items():
        d_, nm_, nv_ = _adamw(w_[0], g_, m_[0], v_[0], "adamw_" + n)
        grads[n], deltas[n], new_m[n], new_v[n] = g_[None], d_[None], nm_[None], nv_[None]
    pad_cols = lambda a: jnp.pad(a, ((0, 0), (0, W_SHARD_PAD - W_SHARD)))
    upd = _adamw(pad_cols(w_in[0]), g_w_in, pad_cols(m_w_in[0]), pad_cols(v_w_in[0]), "adamw_w_in")
    grads["w_in"], deltas["w_in"], new_m["w_in"], new_v["w_in"] = [t[None, :, :W_SHARD] for t in (g_w_in, *upd)]

    names = ["w_ada", "b_ada", "g_pre_mix", "g_post_mix", "w_in", "b_f", "sinks", "w_branch_a", "w_branch_b",
             "w_out", "g_pre_ffn", "g_post_ffn", "w_ffn_in", "w_ffn_out"]
    return (loss, grad_x[None], *[grads[n] for n in names], *[deltas[n] for n in names],
            *[new_m[n] for n in names], *[new_v[n] for n in names])
```

```python
import functools
import math

import numpy as np
import jax
import jax.numpy as jnp
from jax import lax
from jax.experimental import pallas as pl
from jax.experimental.pallas import tpu as pltpu

F32 = jnp.float32
BF = jnp.bfloat16

D_MODEL = 1024
HEAD_DIM = 64
LANES = 128
WINDOW = 128
A_Q_HEADS = 8
A_KV_HEADS = 2
B_HEADS = 8
D_FF = 2816
ROPE_THETA = 10000.0
RMS_EPS = 1e-6
N_ADA = 6
N_DEV = 8
N_CHIP = 4

ADAM_LR = 0.001
ADAM_B1 = 0.9
ADAM_B2 = 0.999
ADAM_EPS = 1e-08
ADAM_WD = 0.01
ADAM_STEP = 10

VMEM_LIMIT = 48 * 1024 * 1024
MESH = pl.DeviceIdType.MESH

A_HEAD_ORDER = (0, 4, 1, 5, 2, 6, 3, 7)

OFF_QA, OFF_KA, OFF_F = 0, 512, 640
W_A = 768
OFF_VA, OFF_QB, OFF_KB, OFF_VB = 0, 128, 640, 1152
W_B = 1664
W_G = 2048
W_PERM = W_A + W_B + W_G
W_SHARD = 1090
W_SHARD_PAD = 1152


def _tile(n, cap, mult=LANES):
    if n <= cap:
        return n
    t = (cap // mult) * mult
    while t >= mult:
        if n % t == 0:
            return t
        t -= mult
    raise ValueError(f"no tile for {n}")


MXU_WIDTH = 256
MM_OPERAND_BYTES = 28 * 1024 * 1024


def _mm_tiles(M, N, K, a_bytes, b_bytes, tm_cap, tn_cap):
    tm = _tile(M, tm_cap)
    try:
        tn = _tile(N, tn_cap, MXU_WIDTH)
    except ValueError:
        tn = _tile(N, tn_cap)
    fits = lambda tk: 2 * tk * (tm * a_bytes + tn * b_bytes) <= MM_OPERAND_BYTES
    tk = K if fits(K) else next(t for t in range(K // LANES * LANES, 0, -LANES) if K % t == 0 and fits(t))
    return tm, tn, tk


def _cparams(*sem):
    return pltpu.CompilerParams(dimension_semantics=sem, vmem_limit_bytes=VMEM_LIMIT)


def _own_refs(refs, comm, n_in, n_out, n_scratch):
    if comm is None:
        return list(refs), None
    return comm.split(refs, n_in, n_out, n_scratch)


def _comm_specs(comm, side):
    if comm is None:
        return []
    return [pl.BlockSpec(memory_space=pl.ANY)] * len(comm.ins if side == "in" else comm.out_shapes)


def _comm_edge(comm, comm_refs, grid, first):
    if comm is None:
        return
    at_edge = None
    for axis, n in enumerate(grid):
        here = pl.program_id(axis) == (0 if first else n - 1)
        at_edge = here if at_edge is None else at_edge & here
    pl.when(at_edge)(lambda: (comm.start if first else comm.finish)(*comm_refs))


def _mm(a, b, mode, out_dtype, name, tm_cap=512, tn_cap=2816, comm=None, col_pieces=1, twin=False):
    if mode == "nn":
        (M, K), (K2, N) = a.shape, b.shape
        dims = (((1,), (0,)), ((), ()))
    elif mode == "nt":
        (M, K), (N, K2) = a.shape, b.shape
        dims = (((1,), (1,)), ((), ()))
    else:
        (K, M), (K2, N) = a.shape, b.shape
        dims = (((0,), (0,)), ((), ()))
    assert K == K2, (a.shape, b.shape, mode)
    tm, tn, tk = _mm_tiles(M, N // col_pieces, K, a.dtype.itemsize, b.dtype.itemsize, tm_cap, tn_cap)
    nk = K // tk
    n_out = 2 if twin else 1
    n_scratch = 1 if nk > 1 else 0
    if mode == "nn":
        a_spec = pl.BlockSpec((tm, tk), lambda i, j, k: (i, k))
        b_spec = pl.BlockSpec((tk, tn), lambda i, j, k: (k, j))
    elif mode == "nt":
        a_spec = pl.BlockSpec((tm, tk), lambda i, j, k: (i, k))
        b_spec = pl.BlockSpec((tn, tk), lambda i, j, k: (j, k))
    else:
        a_spec = pl.BlockSpec((tk, tm), lambda i, j, k: (k, i))
        b_spec = pl.BlockSpec((tk, tn), lambda i, j, k: (k, j))

    grid = (M // tm, N // tn, nk)

    def kern(*refs):
        own, comm_refs = _own_refs(refs, comm, 2, n_out, n_scratch)
        a_ref, b_ref, o_refs = own[0], own[1], own[2:2 + n_out]
        k = pl.program_id(2)
        _comm_edge(comm, comm_refs, grid, first=True)
        part = lax.dot_general(a_ref[...].astype(BF), b_ref[...].astype(BF), dims,
                               preferred_element_type=F32)
        if nk == 1:
            for o_ref in o_refs:
                o_ref[...] = part.astype(o_ref.dtype)
        else:
            acc_ref = own[2 + n_out]

            @pl.when(k == 0)
            def _():
                acc_ref[...] = part

            @pl.when(k > 0)
            def _():
                acc_ref[...] += part

            @pl.when(k == nk - 1)
            def _():
                for o_ref in o_refs:
                    o_ref[...] = acc_ref[...].astype(o_ref.dtype)

        _comm_edge(comm, comm_refs, grid, first=False)

    if col_pieces > 1:
        per = N // col_pieces // tn
        out_spec = pl.BlockSpec((None, tm, tn), lambda i, j, k: (j // per, i, j % per))
        shape = (col_pieces, M, N // col_pieces)
    else:
        out_spec = pl.BlockSpec((tm, tn), lambda i, j, k: (i, j))
        shape = (M, N)
    dtypes = [out_dtype, BF] if twin else [out_dtype]
    res = pl.pallas_call(
        kern, name=name, grid=grid,
        in_specs=[a_spec, b_spec] + _comm_specs(comm, "in"),
        out_specs=[out_spec] * n_out + _comm_specs(comm, "out"),
        out_shape=[jax.ShapeDtypeStruct(shape, d) for d in dtypes] + (comm.out_shapes if comm else []),
        scratch_shapes=[pltpu.VMEM((tm, tn), F32)] * n_scratch + (comm.sem_shapes if comm else []),
        input_output_aliases={2 + i: n_out + o for i, o in comm.aliases.items()} if comm else {},
        compiler_params=_cparams("parallel", "parallel", "arbitrary"),
    )(a, b, *(comm.ins if comm else []))
    own = res[0] if n_out == 1 else tuple(res[:n_out])
    return (own, res[n_out:]) if comm else own


ROWS = 512


def _row_spec(tm, width=D_MODEL, col=0):
    return pl.BlockSpec((tm, width), lambda i: (i, col))


def _vec_spec(width=D_MODEL):
    return pl.BlockSpec((1, width), lambda i: (0, 0))


def _rms(x):
    return lax.rsqrt(jnp.mean(x * x, axis=-1, keepdims=True) + RMS_EPS)


def _colsum(x):
    return jnp.sum(x, axis=0, keepdims=True)


def _norm_bwd(d_xn, xn, r):
    return r * (d_xn - xn * jnp.mean(d_xn * xn, axis=-1, keepdims=True))


def _pre_norm(x, g, scale, shift, name, comm=None):
    S = x.shape[0]
    tm = _tile(S, ROWS, 8)
    grid = (S // tm,)

    def kern(*refs):
        (x_ref, g_ref, sc_ref, sh_ref, h_ref), comm_refs = _own_refs(refs, comm, 4, 1, 0)
        _comm_edge(comm, comm_refs, grid, first=True)
        xf = x_ref[...]
        y = xf * _rms(xf) * g_ref[...]
        h_ref[...] = (y * (1.0 + sc_ref[...]) + sh_ref[...]).astype(BF)
        _comm_edge(comm, comm_refs, grid, first=False)

    res = pl.pallas_call(
        kern, name=name, grid=grid,
        in_specs=[_row_spec(tm), _vec_spec(), _vec_spec(), _vec_spec()] + _comm_specs(comm, "in"),
        out_specs=[_row_spec(tm)] + _comm_specs(comm, "out"),
        out_shape=[jax.ShapeDtypeStruct((S, D_MODEL), BF)] + (comm.out_shapes if comm else []),
        scratch_shapes=comm.sem_shapes if comm else [],
        compiler_params=_cparams("arbitrary"),
    )(x, g, scale, shift, *(comm.ins if comm else []))
    return res[0], res[1:]


def _post_pre(x, y1, g2, gate_m, g3, scale_f, shift_f):
    S = x.shape[0]
    tm = _tile(S, ROWS, 8)

    def kern(x_ref, y_ref, g2_ref, gm_ref, g3_ref, sc_ref, sh_ref, x2_ref, h2_ref):
        y = y_ref[...].astype(F32)
        n2 = y * _rms(y) * g2_ref[...]
        x2 = x_ref[...] + gm_ref[...] * n2
        x2_ref[...] = x2
        n3 = x2 * _rms(x2) * g3_ref[...]
        h2_ref[...] = (n3 * (1.0 + sc_ref[...]) + sh_ref[...]).astype(BF)

    return pl.pallas_call(
        kern, name="post_mix_pre_ffn", grid=(S // tm,),
        in_specs=[_row_spec(tm), _row_spec(tm)] + [_vec_spec()] * 5,
        out_specs=[_row_spec(tm), _row_spec(tm)],
        out_shape=[jax.ShapeDtypeStruct((S, D_MODEL), F32), jax.ShapeDtypeStruct((S, D_MODEL), BF)],
        compiler_params=_cparams("parallel"),
    )(x, y1, g2, gate_m, g3, scale_f, shift_f)


def _stats_spec():
    return pl.BlockSpec((8, D_MODEL), lambda i: (0, 0))


def _final(x2, y2, g4, gate_f, target):
    S = x2.shape[0]
    tm = _tile(S, ROWS, 8)

    def kern(x2_ref, y_ref, g4_ref, gf_ref, t_ref, dout_ref, dy_ref, st_ref):
        @pl.when(pl.program_id(0) == 0)
        def _():
            st_ref[...] = jnp.zeros_like(st_ref)

        y = y_ref[...].astype(F32)
        r = _rms(y)
        yn = y * r
        n4 = yn * g4_ref[...]
        diff = x2_ref[...] + gf_ref[...] * n4 - t_ref[...]
        d_out = diff / D_MODEL
        dout_ref[...] = d_out
        dn = d_out * gf_ref[...]
        dy_ref[...] = _norm_bwd(dn * g4_ref[...], yn, r).astype(BF)
        st_ref[0:1, :] += _colsum(d_out * n4)
        st_ref[1:2, :] += _colsum(dn * yn)
        st_ref[2:3, :] += _colsum(diff * diff)

    return pl.pallas_call(
        kern, name="final_loss", grid=(S // tm,),
        in_specs=[_row_spec(tm), _row_spec(tm), _vec_spec(), _vec_spec(), _row_spec(tm)],
        out_specs=[_row_spec(tm), _row_spec(tm), _stats_spec()],
        out_shape=[jax.ShapeDtypeStruct((S, D_MODEL), F32), jax.ShapeDtypeStruct((S, D_MODEL), BF),
                   jax.ShapeDtypeStruct((8, D_MODEL), F32)],
        compiler_params=_cparams("arbitrary"),
    )(x2, y2, g4, gate_f, target)


def _mid_bwd(d_h2, x2, d_out, y1, g3, scale_f, g2, gate_m):
    S = x2.shape[0]
    tm = _tile(S, ROWS, 8)

    def kern(dh_ref, x2_ref, dout_ref, y_ref, g3_ref, sc_ref, g2_ref, gm_ref, dx2_ref, dy_ref, st_ref):
        @pl.when(pl.program_id(0) == 0)
        def _():
            st_ref[...] = jnp.zeros_like(st_ref)

        dh = dh_ref[...].astype(F32)
        x2 = x2_ref[...]
        r3 = _rms(x2)
        xn = x2 * r3
        one_sc = 1.0 + sc_ref[...]
        d_x2 = dout_ref[...] + _norm_bwd(dh * one_sc * g3_ref[...], xn, r3)
        dx2_ref[...] = d_x2
        y = y_ref[...].astype(F32)
        r2 = _rms(y)
        yn = y * r2
        dn = d_x2 * gm_ref[...]
        dy_ref[...] = _norm_bwd(dn * g2_ref[...], yn, r2).astype(BF)
        st_ref[0:1, :] += _colsum(dh)
        st_ref[1:2, :] += _colsum(dh * (xn * g3_ref[...]))
        st_ref[2:3, :] += _colsum(dh * one_sc * xn)
        st_ref[3:4, :] += _colsum(d_x2 * (yn * g2_ref[...]))
        st_ref[4:5, :] += _colsum(dn * yn)

    return pl.pallas_call(
        kern, name="mid_bwd", grid=(S // tm,),
        in_specs=[_row_spec(tm)] * 4 + [_vec_spec()] * 4,
        out_specs=[_row_spec(tm), _row_spec(tm), _stats_spec()],
        out_shape=[jax.ShapeDtypeStruct((S, D_MODEL), F32), jax.ShapeDtypeStruct((S, D_MODEL), BF),
                   jax.ShapeDtypeStruct((8, D_MODEL), F32)],
        compiler_params=_cparams("arbitrary"),
    )(d_h2, x2, d_out, y1, g3, scale_f, g2, gate_m)


def _pre_bwd(d_h1, x, d_x2, g1, scale_m, comm=None):
    S = x.shape[0]
    tm = _tile(S, ROWS, 8)
    grid = (S // tm,)

    def kern(*refs):
        (dh_ref, x_ref, dx2_ref, g_ref, sc_ref, gx_ref, st_ref), comm_refs = _own_refs(refs, comm, 5, 2, 0)
        _comm_edge(comm, comm_refs, grid, first=True)

        @pl.when(pl.program_id(0) == 0)
        def _():
            st_ref[...] = jnp.zeros_like(st_ref)

        dh = dh_ref[...].astype(F32)
        xf = x_ref[...]
        r = _rms(xf)
        xn = xf * r
        one_sc = 1.0 + sc_ref[...]
        gx_ref[...] = dx2_ref[...] + _norm_bwd(dh * one_sc * g_ref[...], xn, r)
        st_ref[0:1, :] += _colsum(dh)
        st_ref[1:2, :] += _colsum(dh * (xn * g_ref[...]))
        st_ref[2:3, :] += _colsum(dh * one_sc * xn)
        _comm_edge(comm, comm_refs, grid, first=False)

    res = pl.pallas_call(
        kern, name="pre_mix_bwd", grid=grid,
        in_specs=[_row_spec(tm)] * 3 + [_vec_spec()] * 2 + _comm_specs(comm, "in"),
        out_specs=[_row_spec(tm), _stats_spec()] + _comm_specs(comm, "out"),
        out_shape=[jax.ShapeDtypeStruct((S, D_MODEL), F32), jax.ShapeDtypeStruct((8, D_MODEL), F32)]
        + (comm.out_shapes if comm else []),
        scratch_shapes=comm.sem_shapes if comm else [],
        input_output_aliases={5 + i: 2 + o for i, o in comm.aliases.items()} if comm else {},
        compiler_params=_cparams("arbitrary"),
    )(d_h1, x, d_x2, g1, scale_m, *(comm.ins if comm else []))
    return res[0], res[1], res[2:]


def _rope(xs, widths, cos_t, sin_t, name):
    S = xs[0].shape[0]
    tm = _tile(S, 512, 8)
    n = len(xs)

    def kern(*refs):
        cos = refs[n][...]
        sin = refs[n + 1][...]
        first = (lax.broadcasted_iota(jnp.int32, cos.shape, 1) % HEAD_DIM) < HEAD_DIM // 2
        for x_ref, o_ref, w in zip(refs[:n], refs[n + 2:], widths):
            for c0 in range(0, w, LANES):
                v = x_ref[:, c0:c0 + LANES]
                partner = jnp.where(first, pltpu.roll(v, LANES - HEAD_DIM // 2, 1),
                                    pltpu.roll(v, HEAD_DIM // 2, 1))
                o_ref[:, c0:c0 + LANES] = (v * cos + partner * sin).astype(BF)

    return pl.pallas_call(
        kern, name=name, grid=(S // tm,),
        in_specs=[_row_spec(tm, w) for w in widths] + [_row_spec(tm, LANES)] * 2,
        out_specs=[_row_spec(tm, w) for w in widths],
        out_shape=[jax.ShapeDtypeStruct((S, w), BF) for w in widths],
        compiler_params=_cparams("parallel"),
    )(*xs, cos_t, sin_t)


def _merge_fwd(pg, pa, pb):
    S = pa.shape[0]
    tm = _tile(S, ROWS, 8)

    def kern(ga_ref, gb_ref, pa_ref, pb_ref, o_ref):
        ga = jax.nn.sigmoid(ga_ref[...].astype(F32))
        gb = jax.nn.sigmoid(gb_ref[...].astype(F32))
        o_ref[...] = (ga * pa_ref[...].astype(F32) + gb * pb_ref[...].astype(F32)).astype(BF)

    return pl.pallas_call(
        kern, name="merge_fwd", grid=(S // tm,),
        in_specs=[_row_spec(tm, col=0), _row_spec(tm, col=1), _row_spec(tm), _row_spec(tm)],
        out_specs=_row_spec(tm),
        out_shape=jax.ShapeDtypeStruct((S, D_MODEL), BF),
        compiler_params=_cparams("parallel"),
    )(pg, pg, pa, pb)


def _merge_bwd(d_merged, pg, pa, pb):
    S = pa.shape[0]
    tm = _tile(S, ROWS, 8)

    def kern(dm_ref, ga_ref, gb_ref, pa_ref, pb_ref, dpa_ref, dpb_ref, dga_ref, dgb_ref):
        dm = dm_ref[...].astype(F32)
        ga = jax.nn.sigmoid(ga_ref[...].astype(F32))
        gb = jax.nn.sigmoid(gb_ref[...].astype(F32))
        dpa_ref[...] = (dm * ga).astype(BF)
        dpb_ref[...] = (dm * gb).astype(BF)
        dga_ref[...] = (dm * pa_ref[...].astype(F32) * ga * (1.0 - ga)).astype(BF)
        dgb_ref[...] = (dm * pb_ref[...].astype(F32) * gb * (1.0 - gb)).astype(BF)

    bf_out = jax.ShapeDtypeStruct((S, D_MODEL), BF)
    return pl.pallas_call(
        kern, name="merge_bwd", grid=(S // tm,),
        in_specs=[_row_spec(tm), _row_spec(tm, col=0), _row_spec(tm, col=1), _row_spec(tm), _row_spec(tm)],
        out_specs=[_row_spec(tm)] * 4,
        out_shape=[bf_out] * 4,
        compiler_params=_cparams("parallel"),
    )(d_merged, pg, pg, pa, pb)


def _swiglu_fwd(gu):
    S = gu.shape[0]
    tm = _tile(S, ROWS, 8)
    tc = _tile(D_FF, 1408)
    nc = D_FF // tc

    def kern(g_ref, u_ref, o_ref):
        g = g_ref[...].astype(F32)
        o_ref[...] = (g * jax.nn.sigmoid(g) * u_ref[...].astype(F32)).astype(BF)

    return pl.pallas_call(
        kern, name="swiglu_fwd", grid=(S // tm, nc),
        in_specs=[pl.BlockSpec((tm, tc), lambda i, j: (i, j)),
                  pl.BlockSpec((tm, tc), lambda i, j: (i, j + nc))],
        out_specs=pl.BlockSpec((tm, tc), lambda i, j: (i, j)),
        out_shape=jax.ShapeDtypeStruct((S, D_FF), BF),
        compiler_params=_cparams("parallel", "parallel"),
    )(gu, gu)


def _swiglu_bwd(d_act, gu):
    S = gu.shape[0]
    tm = _tile(S, ROWS // 2, 8)

    def kern(da_ref, g_ref, u_ref, o_ref):
        g = g_ref[...].astype(F32)
        u = u_ref[...].astype(F32)
        da = da_ref[...].astype(F32)
        sg = jax.nn.sigmoid(g)
        o_ref[:, :D_FF] = (da * u * (sg * (1.0 + g * (1.0 - sg)))).astype(BF)
        o_ref[:, D_FF:] = (da * (g * sg)).astype(BF)

    return pl.pallas_call(
        kern, name="swiglu_bwd", grid=(S // tm,),
        in_specs=[_row_spec(tm, D_FF), _row_spec(tm, D_FF, 0), _row_spec(tm, D_FF, 1)],
        out_specs=_row_spec(tm, 2 * D_FF),
        out_shape=jax.ShapeDtypeStruct((S, 2 * D_FF), BF),
        compiler_params=_cparams("parallel"),
    )(d_act, gu, gu)


def _split3(x):
    hi = x.astype(BF)
    r1 = x - hi.astype(F32)
    mid = r1.astype(BF)
    lo = (r1 - mid.astype(F32)).astype(BF)
    return hi, mid, lo


def _tri_dot(tri, x):
    return sum(jnp.dot(tri, part, preferred_element_type=F32) for part in _split3(x))


def _log_sigmoid(z):
    return jnp.minimum(z, 0.0) - jnp.log(1.0 + jnp.exp(-jnp.abs(z)))


def _fox_gate_fwd(pa, b_f_pad):
    S = pa.shape[0]
    T = _tile(S, 512, 8)
    f_col = OFF_F // LANES

    def kern(z_ref, b_ref, cum_ref, carry_ref):
        @pl.when(pl.program_id(0) == 0)
        def _():
            carry_ref[...] = jnp.zeros_like(carry_ref)

        log_f = _log_sigmoid(z_ref[...] + b_ref[...])
        row = lax.broadcasted_iota(jnp.int32, (T, T), 0)
        col = lax.broadcasted_iota(jnp.int32, (T, T), 1)
        tri = (col <= row).astype(BF)
        cum = _tri_dot(tri, log_f) + carry_ref[...]
        cum_ref[...] = cum
        carry_ref[...] = cum[T - 1:T, :]

    return pl.pallas_call(
        kern, name="fox_gate_fwd", grid=(S // T,),
        in_specs=[_row_spec(T, LANES, f_col), _vec_spec(LANES)],
        out_specs=_row_spec(T, LANES),
        out_shape=jax.ShapeDtypeStruct((S, LANES), F32),
        scratch_shapes=[pltpu.VMEM((1, LANES), F32)],
        compiler_params=_cparams("arbitrary"),
    )(pa, b_f_pad)


def _fox_gate_bwd(rowsum_ds, colsum_ds, pa, b_f_pad):
    S = pa.shape[0]
    T = _tile(S, 512, 8)
    nb = S // T
    f_col = OFF_F // LANES

    def kern(dr_ref, dc_ref, z_ref, b_ref, df_ref, dbf_ref, carry_ref):
        @pl.when(pl.program_id(0) == 0)
        def _():
            carry_ref[...] = jnp.zeros_like(carry_ref)
            dbf_ref[...] = jnp.zeros_like(dbf_ref)

        row = lax.broadcasted_iota(jnp.int32, (T, T), 0)
        col = lax.broadcasted_iota(jnp.int32, (T, T), 1)
        tri = (col >= row).astype(BF)
        rev = _tri_dot(tri, dr_ref[...] - dc_ref[...]) + carry_ref[...]
        carry_ref[...] = rev[0:1, :]
        z = z_ref[...] + b_ref[...]
        lane = lax.broadcasted_iota(jnp.int32, (T, LANES), 1)
        d_z = jnp.where(lane < B_HEADS, rev * jax.nn.sigmoid(-z), 0.0)
        df_ref[...] = d_z.astype(BF)
        dbf_ref[0:1, :] += _colsum(d_z)

    return pl.pallas_call(
        kern, name="fox_gate_bwd", grid=(nb,),
        in_specs=[pl.BlockSpec((T, LANES), lambda i: (nb - 1 - i, 0)),
                  pl.BlockSpec((T, LANES), lambda i: (nb - 1 - i, 0)),
                  pl.BlockSpec((T, LANES), lambda i: (nb - 1 - i, f_col)),
                  _vec_spec(LANES)],
        out_specs=[pl.BlockSpec((T, LANES), lambda i: (nb - 1 - i, 0)),
                   pl.BlockSpec((8, LANES), lambda i: (0, 0))],
        out_shape=[jax.ShapeDtypeStruct((S, LANES), BF), jax.ShapeDtypeStruct((8, LANES), F32)],
        scratch_shapes=[pltpu.VMEM((1, LANES), F32)],
        compiler_params=_cparams("arbitrary"),
    )(rowsum_ds, colsum_ds, pa, b_f_pad)


NEG_INF = float("-inf")
QK_SCALE = 1.0 / math.sqrt(HEAD_DIM)


def _half_mask(shape, half):
    lane = lax.broadcasted_iota(jnp.int32, shape, 1)
    return (lane < HEAD_DIM) if half == 0 else (lane >= HEAD_DIM)


def _bias_block(shape, terms, term_off, ones_lo, ones_hi):
    l64 = lax.broadcasted_iota(jnp.int32, shape, 1) & (HEAD_DIM - 1)
    out = jnp.where((l64 >= ones_lo) & (l64 < ones_hi), 1.0, 0.0)
    for t, term in enumerate(terms):
        out = jnp.where(l64 == term_off + t, term.astype(F32), out)
    return out


def _head_column(block, head):
    lane = lax.broadcasted_iota(jnp.int32, block.shape, 1)
    return jnp.sum(jnp.where(lane == head, block, 0.0), axis=1, keepdims=True)


def _crossed(shape, first, second):
    return jnp.where(_half_mask(shape, 0), second, first)


def _fox_prep_fwd(cum, T):
    S = cum.shape[0]
    shape = (T, LANES)

    def kern(c_ref, bq_ref, bk_ref):
        p_id = pl.program_id(0)
        cum_blk = c_ref[...]
        c3 = _split3(_crossed(shape, _head_column(cum_blk, 2 * p_id), _head_column(cum_blk, 2 * p_id + 1)))
        bq_ref[...] = _bias_block(shape, c3, 0, 3, 6).astype(BF)
        bk_ref[...] = _bias_block(shape, [-t.astype(F32) for t in c3], 3, 0, 3).astype(BF)

    out_spec = pl.BlockSpec((None, T, LANES), lambda p, i: (p, i, 0))
    out_shape = jax.ShapeDtypeStruct((B_HEADS // 2, S, LANES), BF)
    return pl.pallas_call(
        kern, name="fox_prep_fwd", grid=(B_HEADS // 2, S // T),
        in_specs=[pl.BlockSpec((T, LANES), lambda p, i: (i, 0))],
        out_specs=[out_spec, out_spec], out_shape=[out_shape, out_shape],
        compiler_params=_cparams("parallel", "parallel"),
    )(cum)


def _fox_fwd(p_b, bq, bk, T, comm=None):
    S = p_b.shape[0]
    nq = S // T
    n_pairs = B_HEADS // 2
    grid = (n_pairs, nq)

    def kern(*refs):
        (q_ref, k_ref, v_ref, bq_ref, bk_ref, o_ref, lse_ref), comm_refs = _own_refs(refs, comm, 5, 2, 0)
        _comm_edge(comm, comm_refs, grid, first=True)
        i = pl.program_id(1)
        rowcol = lax.broadcasted_iota(jnp.int32, (T, T), 0) - lax.broadcasted_iota(jnp.int32, (T, T), 1)
        hms = (_half_mask((T, LANES), 0), _half_mask((T, LANES), 1))
        q_scaled = (q_ref[...].astype(F32) * QK_SCALE).astype(BF)
        bq_blk = bq_ref[...]
        qs = [jnp.where(hms[h], q_scaled, bq_blk) for h in (0, 1)]

        def step(j, carry, masked):
            rows = pl.ds(pl.multiple_of(j * T, T), T)
            kj, bkj, vj = k_ref[rows, :], bk_ref[rows, :], v_ref[rows, :]
            new = []
            for half in (0, 1):
                m, l, acc = carry[half]
                s = lax.dot_general(qs[half], jnp.where(hms[half], kj, bkj), (((1,), (1,)), ((), ())),
                                    preferred_element_type=F32)
                if masked:
                    s = jnp.where(rowcol >= 0, s, NEG_INF)
                m_new = jnp.maximum(m, jnp.max(s, axis=1, keepdims=True))
                alpha = jnp.exp(m - m_new)
                p = jnp.exp(s - m_new)
                l_new = alpha * l + jnp.sum(p, axis=1, keepdims=True)
                acc_new = alpha * acc + jnp.dot(p.astype(BF), vj, preferred_element_type=F32)
                new.append((m_new, l_new, acc_new))
            return tuple(new)

        one = (jnp.full((T, 1), NEG_INF, F32), jnp.zeros((T, 1), F32), jnp.zeros((T, LANES), F32))
        carry = lax.fori_loop(0, i, functools.partial(step, masked=False), (one, one))
        (m0, l0, acc0), (m1, l1, acc1) = step(i, carry, True)
        hm0 = _half_mask((T, LANES), 0)
        o_ref[...] = jnp.where(hm0, acc0 / l0, acc1 / l1)
        lse_ref[...] = jnp.where(hm0, m0 + jnp.log(l0), m1 + jnp.log(l1))
        _comm_edge(comm, comm_refs, grid, first=False)

    out_spec = pl.BlockSpec((T, LANES), lambda p, i: (i, p))
    res = pl.pallas_call(
        kern, name="fox_fwd", grid=grid,
        in_specs=[pl.BlockSpec((T, LANES), lambda p, i: (i, OFF_QB // LANES + p)),
                  pl.BlockSpec((S, LANES), lambda p, i: (0, OFF_KB // LANES + p)),
                  pl.BlockSpec((S, LANES), lambda p, i: (0, OFF_VB // LANES + p)),
                  pl.BlockSpec((None, T, LANES), lambda p, i: (p, i, 0)),
                  pl.BlockSpec((None, S, LANES), lambda p, i: (p, 0, 0))] + _comm_specs(comm, "in"),
        out_specs=[out_spec, out_spec] + _comm_specs(comm, "out"),
        out_shape=[jax.ShapeDtypeStruct((S, n_pairs * LANES), F32)] * 2 + (comm.out_shapes if comm else []),
        scratch_shapes=comm.sem_shapes if comm else [],
        compiler_params=_cparams("arbitrary", "arbitrary"),
    )(p_b, p_b, p_b, bq, bk, *(comm.ins if comm else []))
    return res[0], res[1], res[2:]


def _fox_prep_bwd(cum, o, do, lse, T):
    S = o.shape[0]
    shape = (T, LANES)

    def kern(c_ref, o_ref, do_ref, lse_ref, bq_ref, bdo_ref):
        p_id = pl.program_id(0)
        cum_blk = c_ref[...]
        cq = _crossed(shape, _head_column(cum_blk, 2 * p_id), _head_column(cum_blk, 2 * p_id + 1))
        b3 = _split3(cq - pltpu.roll(lse_ref[...], HEAD_DIM, 1))
        bq_ref[...] = _bias_block(shape, b3, 0, 3, 6).astype(BF)
        dd = do_ref[...] * o_ref[...]
        delta = [jnp.sum(jnp.where(_half_mask(shape, h), dd, 0.0), axis=1, keepdims=True) for h in (0, 1)]
        d3 = _split3(-_crossed(shape, delta[0], delta[1]))
        bdo_ref[...] = _bias_block(shape, d3, 0, 0, 0).astype(BF)

    block = pl.BlockSpec((None, T, LANES), lambda p, i: (p, i, 0))
    tile = pl.BlockSpec((T, LANES), lambda p, i: (i, p))
    out_shape = jax.ShapeDtypeStruct((B_HEADS // 2, S, LANES), BF)
    return pl.pallas_call(
        kern, name="fox_prep_bwd", grid=(B_HEADS // 2, S // T),
        in_specs=[pl.BlockSpec((T, LANES), lambda p, i: (i, 0)), tile, tile, tile],
        out_specs=[block, block], out_shape=[out_shape, out_shape],
        compiler_params=_cparams("parallel", "parallel"),
    )(cum, o, do, lse)


def _fox_bwd(p_b, do, bq, bk, bdo, T, comm=None):
    S = p_b.shape[0]
    n_pairs = B_HEADS // 2
    nq = S // T
    grid = (n_pairs,)

    def kern(*refs):
        own, comm_refs = _own_refs(refs, comm, 7, 5, 0)
        q_ref, k_ref, v_ref, do_ref, bq_ref, bk_ref, bdo_ref, dq_ref, dk_ref, dv_ref, dck_ref, dcq_ref = own
        _comm_edge(comm, comm_refs, grid, first=True)
        p_id = pl.program_id(0)
        rowcol = lax.broadcasted_iota(jnp.int32, (T, T), 0) - lax.broadcasted_iota(jnp.int32, (T, T), 1)
        lane = lax.broadcasted_iota(jnp.int32, (T, LANES), 1)
        dk_ref[...] = jnp.zeros_like(dk_ref)
        dv_ref[...] = jnp.zeros_like(dv_ref)
        dck_ref[...] = jnp.zeros_like(dck_ref)

        @pl.when(p_id == 0)
        def _():
            dcq_ref[...] = jnp.zeros_like(dcq_ref)

        hms = (_half_mask((T, LANES), 0), _half_mask((T, LANES), 1))
        v_ones = _bias_block((T, LANES), [], 0, 0, 3).astype(BF)

        def outer(i, carry):
            qrows = pl.ds(pl.multiple_of(i * T, T), T)
            q_scaled = (q_ref[qrows, :].astype(F32) * QK_SCALE).astype(BF)
            do_b = do_ref[qrows, :].astype(BF)
            bq_i, bdo_i = bq_ref[qrows, :], bdo_ref[qrows, :]
            qa = [jnp.where(hms[h], q_scaled, bq_i) for h in (0, 1)]
            doa = [jnp.where(hms[h], do_b, bdo_i) for h in (0, 1)]
            q_own = [jnp.where(hms[h], q_scaled, 0) for h in (0, 1)]
            do_own = [jnp.where(hms[h], do_b, 0) for h in (0, 1)]

            def inner(j, carry_in, masked):
                krows = pl.ds(pl.multiple_of(j * T, T), T)
                kj, bkj, vj = k_ref[krows, :], bk_ref[krows, :], v_ref[krows, :]
                dv_add, dk_add, new = 0.0, 0.0, []
                for half in (0, 1):
                    dq, rs = carry_in[half]
                    ka = jnp.where(hms[half], kj, bkj)
                    s = lax.dot_general(qa[half], ka, (((1,), (1,)), ((), ())), preferred_element_type=F32)
                    if masked:
                        s = jnp.where(rowcol >= 0, s, NEG_INF)
                    p = jnp.exp(s)
                    ds = p * lax.dot_general(doa[half], jnp.where(hms[half], vj, v_ones),
                                             (((1,), (1,)), ((), ())), preferred_element_type=F32)
                    ds_b = ds.astype(BF)
                    dv_add = dv_add + lax.dot_general(p.astype(BF), do_own[half], (((0,), (0,)), ((), ())),
                                                      preferred_element_type=F32)
                    dk_add = dk_add + lax.dot_general(ds_b, q_own[half], (((0,), (0,)), ((), ())),
                                                      preferred_element_type=F32)
                    dck_ref[half:half + 1, krows] += jnp.sum(ds, axis=0, keepdims=True)
                    new.append((dq + jnp.dot(ds_b, jnp.where(hms[half], kj, 0), preferred_element_type=F32),
                                rs + jnp.sum(ds, axis=1, keepdims=True)))
                dv_ref[krows, :] += dv_add
                dk_ref[krows, :] += dk_add
                return tuple(new)

            one = (jnp.zeros((T, LANES), F32), jnp.zeros((T, 1), F32))
            carry_in = lax.fori_loop(0, i, functools.partial(inner, masked=False), (one, one))
            (dq0, rs0), (dq1, rs1) = inner(i, carry_in, True)
            dq_ref[qrows, :] = (dq0 + dq1) * QK_SCALE
            dcq_ref[qrows, :] = jnp.where(lane == 2 * p_id, rs0, jnp.where(lane == 2 * p_id + 1, rs1,
                                                                             dcq_ref[qrows, :]))
            return carry

        lax.fori_loop(0, nq, outer, 0)
        _comm_edge(comm, comm_refs, grid, first=False)

    block = pl.BlockSpec((None, S, LANES), lambda p: (p, 0, 0))
    pair = pl.BlockSpec((S, LANES), lambda p: (0, p))
    slab = lambda off: pl.BlockSpec((S, LANES), lambda p: (0, off // LANES + p))
    wide = jax.ShapeDtypeStruct((S, n_pairs * LANES), F32)
    res = pl.pallas_call(
        kern, name="fox_bwd", grid=grid,
        in_specs=[slab(OFF_QB), slab(OFF_KB), slab(OFF_VB), pair, block, block, block]
        + _comm_specs(comm, "in"),
        out_specs=[pair, pair, pair, pl.BlockSpec((None, 2, S), lambda p: (p, 0, 0)),
                   pl.BlockSpec((S, LANES), lambda p: (0, 0))] + _comm_specs(comm, "out"),
        out_shape=[wide, wide, wide, jax.ShapeDtypeStruct((n_pairs, 2, S), F32),
                   jax.ShapeDtypeStruct((S, LANES), F32)] + (comm.out_shapes if comm else []),
        scratch_shapes=comm.sem_shapes if comm else [],
        compiler_params=_cparams("arbitrary"),
    )(p_b, p_b, p_b, do, bq, bk, bdo, *(comm.ins if comm else []))
    return (*res[:5], res[5:])


SWA_TQ = 256
SWA_SUB = 8


def _swa_window(i, tq):
    start = pl.multiple_of(jnp.maximum(i * tq - WINDOW, 0), LANES)
    return start, i * tq - start


def _swa_valid(offset, tq):
    rel = offset + lax.broadcasted_iota(jnp.int32, (tq, tq + WINDOW), 0) \
        - lax.broadcasted_iota(jnp.int32, (tq, tq + WINDOW), 1)
    return (rel >= 0) & (rel < WINDOW)


def _swa_fwd(qk, v_arr, v_col, sinks):
    S = qk.shape[0]
    tq = min(SWA_TQ, S - WINDOW)
    sub = min(SWA_SUB, S // tq)
    win = tq + WINDOW

    def kern(q_ref, k_ref, v_ref, sink_ref, o_ref, lse_ref):
        p_id, i = pl.program_id(0), pl.program_id(1)
        hm0 = _half_mask((tq, LANES), 0)
        for t in range(sub):
            rows = slice(t * tq, (t + 1) * tq)
            start, offset = _swa_window(i * sub + t, tq)
            kw = k_ref[pl.ds(start, win), :]
            vw = v_ref[pl.ds(start, win), :].astype(BF)
            valid = _swa_valid(offset, tq)
            q = q_ref[rows, :]
            outs, lses = [], []
            for half in (0, 1):
                hm = _half_mask((tq, LANES), half)
                qh = (jnp.where(hm, q, 0).astype(F32) * QK_SCALE).astype(BF)
                s = lax.dot_general(qh, kw, (((1,), (1,)), ((), ())), preferred_element_type=F32)
                s = jnp.where(valid, s, NEG_INF)
                sink = sink_ref[2 * p_id + half]
                m = jnp.maximum(jnp.max(s, axis=1, keepdims=True), sink)
                p = jnp.exp(s - m)
                denom = jnp.sum(p, axis=1, keepdims=True) + jnp.exp(sink - m)
                outs.append(jnp.dot(p.astype(BF), vw, preferred_element_type=F32) / denom)
                lses.append(m + jnp.log(denom))
            o_ref[rows, :] = jnp.where(hm0, outs[0], outs[1])
            lse_ref[rows, :] = jnp.where(hm0, lses[0], lses[1])

    tile = pl.BlockSpec((sub * tq, LANES), lambda p, i: (i, p))
    return pl.pallas_call(
        kern, name="swa_fwd", grid=(A_Q_HEADS // 2, S // (sub * tq)),
        in_specs=[tile, pl.BlockSpec((S, LANES), lambda p, i: (0, A_Q_HEADS // 2)),
                  pl.BlockSpec((S, LANES), lambda p, i: (0, v_col)),
                  pl.BlockSpec(memory_space=pltpu.SMEM)],
        out_specs=[tile, tile],
        out_shape=[jax.ShapeDtypeStruct((S, A_Q_HEADS * HEAD_DIM), F32)] * 2,
        compiler_params=_cparams("parallel", "arbitrary"),
    )(qk, qk, v_arr, sinks)


def _swa_bwd(qk, v_arr, v_col, o_arr, do_arr, lse_arr, sinks, comm=None):
    S = qk.shape[0]
    tq = min(SWA_TQ, S - WINDOW)
    sub = min(SWA_SUB, S // tq)
    win = tq + WINDOW
    n_pairs = A_Q_HEADS // 2
    grid = (n_pairs, S // (sub * tq))

    def kern(*refs):
        own, comm_refs = _own_refs(refs, comm, 7, 4, 0)
        q_ref, k_ref, v_ref, o_ref, do_ref, lse_ref, sink_ref, dq_ref, dk_ref, dv_ref, dsink_ref = own
        _comm_edge(comm, comm_refs, grid, first=True)
        p_id, i = pl.program_id(0), pl.program_id(1)

        @pl.when((p_id == 0) & (i == 0))
        def _():
            dk_ref[...] = jnp.zeros_like(dk_ref)
            dv_ref[...] = jnp.zeros_like(dv_ref)

        @pl.when(i == 0)
        def _():
            dsink_ref[...] = jnp.zeros_like(dsink_ref)

        for t in range(sub):
            rows = slice(t * tq, (t + 1) * tq)
            start, offset = _swa_window(i * sub + t, tq)
            wrows = pl.ds(start, win)
            kw = k_ref[wrows, :]
            vw = v_ref[wrows, :].astype(BF)
            valid = _swa_valid(offset, tq)
            q, do, o, lse2 = q_ref[rows, :], do_ref[rows, :], o_ref[rows, :], lse_ref[rows, :]
            dq = jnp.zeros((tq, LANES), F32)
            dk = jnp.zeros((win, LANES), F32)
            dv = jnp.zeros((win, LANES), F32)
            for half in (0, 1):
                hm = _half_mask((tq, LANES), half)
                lane0 = half * HEAD_DIM
                qh = (jnp.where(hm, q, 0).astype(F32) * QK_SCALE).astype(BF)
                do_f = jnp.where(hm, do, 0.0)
                doh = do_f.astype(BF)
                delta = jnp.sum(do_f * o, axis=1, keepdims=True)
                lse = lse2[:, lane0:lane0 + 1]
                s = lax.dot_general(qh, kw, (((1,), (1,)), ((), ())), preferred_element_type=F32)
                p = jnp.exp(jnp.where(valid, s, NEG_INF) - lse)
                dp = lax.dot_general(doh, vw, (((1,), (1,)), ((), ())), preferred_element_type=F32)
                ds_b = (p * (dp - delta)).astype(BF)
                dv = dv + lax.dot_general(p.astype(BF), doh, (((0,), (0,)), ((), ())),
                                          preferred_element_type=F32)
                dk = dk + lax.dot_general(ds_b, qh, (((0,), (0,)), ((), ())), preferred_element_type=F32)
                kh = jnp.where(_half_mask((win, LANES), half), kw, 0)
                dq = dq + jnp.dot(ds_b, kh, preferred_element_type=F32)
                p_sink = jnp.exp(sink_ref[2 * p_id + half] - lse)
                dsink_ref[0, half:half + 1, :] += jnp.broadcast_to(
                    -jnp.sum(p_sink * delta, axis=0, keepdims=True), (1, LANES))
            dq_ref[rows, :] = dq * QK_SCALE
            dk_ref[wrows, :] += dk
            dv_ref[wrows, :] += dv
        _comm_edge(comm, comm_refs, grid, first=False)

    tile = pl.BlockSpec((sub * tq, LANES), lambda p, i: (i, p))
    whole = lambda col: pl.BlockSpec((S, LANES), lambda p, i: (0, col))
    res = pl.pallas_call(
        kern, name="swa_bwd", grid=grid,
        in_specs=[tile, whole(n_pairs), whole(v_col), tile, tile, tile,
                  pl.BlockSpec(memory_space=pltpu.SMEM)] + _comm_specs(comm, "in"),
        out_specs=[tile, whole(0), whole(0),
                   pl.BlockSpec((1, 8, LANES), lambda p, i: (p, 0, 0))] + _comm_specs(comm, "out"),
        out_shape=[jax.ShapeDtypeStruct((S, A_Q_HEADS * HEAD_DIM), F32),
                   jax.ShapeDtypeStruct((S, LANES), F32), jax.ShapeDtypeStruct((S, LANES), F32),
                   jax.ShapeDtypeStruct((n_pairs, 8, LANES), F32)] + (comm.out_shapes if comm else []),
        scratch_shapes=comm.sem_shapes if comm else [],
        compiler_params=_cparams("arbitrary", "arbitrary"),
    )(qk, qk, v_arr, o_arr, do_arr, lse_arr, sinks, *(comm.ins if comm else []))
    return (*res[:4], res[4:])


ADAMW_BLOCK = 512 * 1024


def _adamw(w, g, m, v, name, comm=None):
    R, C = w.shape
    tr, tc = _tile(R, max(8, ADAMW_BLOCK // C), 8), C
    grid = (R // tr, C // tc)

    def kern(*refs):
        (w_ref, g_ref, m_ref, v_ref, d_ref, mo_ref, vo_ref), comm_refs = _own_refs(refs, comm, 4, 3, 0)
        _comm_edge(comm, comm_refs, grid, first=True)
        g_ = g_ref[...]
        m_new = ADAM_B1 * m_ref[...] + (1.0 - ADAM_B1) * g_
        v_new = ADAM_B2 * v_ref[...] + (1.0 - ADAM_B2) * (g_ * g_)
        m_hat = m_new / (1.0 - ADAM_B1 ** ADAM_STEP)
        v_hat = v_new / (1.0 - ADAM_B2 ** ADAM_STEP)
        d_ref[...] = -ADAM_LR * (m_hat / (jnp.sqrt(v_hat) + ADAM_EPS) + ADAM_WD * w_ref[...])
        mo_ref[...] = m_new
        vo_ref[...] = v_new
        _comm_edge(comm, comm_refs, grid, first=False)

    spec = pl.BlockSpec((tr, tc), lambda i, j: (i, j))
    shape = jax.ShapeDtypeStruct((R, C), F32)
    res = pl.pallas_call(
        kern, name=name, grid=grid,
        in_specs=[spec] * 4 + _comm_specs(comm, "in"),
        out_specs=[spec] * 3 + _comm_specs(comm, "out"),
        out_shape=[shape] * 3 + (comm.out_shapes if comm else []),
        scratch_shapes=comm.sem_shapes if comm else [],
        input_output_aliases={4 + i: 3 + o for i, o in comm.aliases.items()} if comm else {},
        compiler_params=_cparams("arbitrary", "arbitrary"),
    )(w, g, m, v, *(comm.ins if comm else []))
    return (res[:3], res[3:]) if comm else res


def _index_operand(i):
    return jnp.reshape(i, (1,)).astype(jnp.int32)


def _add_pair(whole, got, ci, name):
    P, R, C = whole.shape
    half = R // 2
    tr = _tile(half, ROWS, 16)
    nb = half // tr

    def kern(ci_ref, a_ref, b_ref, o_ref, ob_ref):
        s = a_ref[...] + b_ref[...].astype(F32)
        o_ref[...] = s
        ob_ref[...] = s.astype(BF)

    spec = pl.BlockSpec((None, tr, C), lambda p, i, ci_ref: (p, i, 0))
    return pl.pallas_call(
        kern, name=name,
        grid_spec=pltpu.PrefetchScalarGridSpec(
            num_scalar_prefetch=1, grid=(P, nb),
            in_specs=[pl.BlockSpec((None, tr, C), lambda p, i, ci_ref: (p, ci_ref[0] * nb + i, 0)), spec],
            out_specs=[spec, spec]),
        out_shape=[jax.ShapeDtypeStruct((P, half, C), F32), jax.ShapeDtypeStruct((P, half, C), BF)],
        compiler_params=_cparams("parallel", "parallel"),
    )(_index_operand(ci), whole, got)


def _add_three(parts, recv, chip, name):
    _, R, C = parts.shape
    tr = _tile(R, ROWS, 16)

    def kern(chip_ref, o_ref, r0_ref, r1_ref, r2_ref, out_ref):
        s = ((o_ref[...] + r0_ref[...].astype(F32)) + r1_ref[...].astype(F32)) + r2_ref[...].astype(F32)
        out_ref[0] = s
        out_ref[1] = s

    slab = lambda k: pl.BlockSpec((None, tr, C), lambda i, chip_ref: (k, i, 0))
    return pl.pallas_call(
        kern, name=name,
        grid_spec=pltpu.PrefetchScalarGridSpec(
            num_scalar_prefetch=1, grid=(R // tr,),
            in_specs=[pl.BlockSpec((None, tr, C), lambda i, chip_ref: (chip_ref[0], i, 0)),
                      slab(0), slab(1), slab(2)],
            out_specs=pl.BlockSpec((2, tr, C), lambda i, chip_ref: (0, i, 0))),
        out_shape=jax.ShapeDtypeStruct((2, R, C), F32),
        compiler_params=_cparams("parallel"),
    )(_index_operand(chip), parts, recv, recv, recv)


SM_ADA, SM_G, SM_LOSS, SM_BF, SM_SINK, SM_LEN = 0, 6144, 10240, 11264, 11272, 12288


def _small_finalize(gathered):
    def kern(g_ref, tot_ref, loss_ref):
        tot = g_ref[0:1, :]
        for b in range(1, N_DEV):
            tot = tot + g_ref[b:b + 1, :]
        tot_ref[...] = tot
        sq = jnp.sum(tot[:, SM_LOSS:SM_LOSS + D_MODEL], axis=1, keepdims=True)
        loss_ref[...] = jnp.broadcast_to(sq * (0.5 / D_MODEL), (1, LANES))

    full = lambda shape: pl.BlockSpec(shape, lambda i: (0, 0))
    return pl.pallas_call(
        kern, name="small_finalize", grid=(1,),
        in_specs=[full((N_DEV, SM_LEN))],
        out_specs=[full((1, SM_LEN)), full((1, LANES))],
        out_shape=[jax.ShapeDtypeStruct((1, SM_LEN), F32), jax.ShapeDtypeStruct((1, LANES), F32)],
        compiler_params=_cparams("arbitrary"),
    )(gathered)


def _ada_dw(c_t, d_ada):
    N = d_ada.shape[1]
    tn = _tile(N, 512)

    def kern(c_ref, d_ref, o_ref):
        acc = c_ref[:, 0:1] * d_ref[0:1, :]
        for b in range(1, N_DEV):
            acc = acc + c_ref[:, b:b + 1] * d_ref[b:b + 1, :]
        o_ref[...] = acc

    return pl.pallas_call(
        kern, name="ada_dw", grid=(N // tn,),
        in_specs=[pl.BlockSpec((D_MODEL, N_DEV), lambda j: (0, 0)), pl.BlockSpec((N_DEV, tn), lambda j: (0, j))],
        out_specs=pl.BlockSpec((D_MODEL, tn), lambda j: (0, j)),
        out_shape=jax.ShapeDtypeStruct((D_MODEL, N), F32),
        compiler_params=_cparams("parallel"),
    )(c_t, d_ada)


def _here():
    return lax.axis_index("x"), lax.axis_index("y"), lax.axis_index("c")


def _other_chips(x, y):
    return [(1 - x, y), (x, 1 - y), (1 - x, 1 - y)]


_ANY = pl.BlockSpec(memory_space=pl.ANY)


class _Comm:
    def __init__(self, ins, out_shapes, sem_shapes, start, finish, aliases=None):
        self.ins, self.out_shapes, self.sem_shapes = list(ins), list(out_shapes), list(sem_shapes)
        self.start, self.finish = start, finish
        self.aliases = dict(aliases or {})

    def split(self, refs, n_in, n_out, n_scratch):
        a = n_in + len(self.ins)
        b = a + n_out + len(self.out_shapes)
        own = list(refs[:n_in]) + list(refs[a:a + n_out]) + list(refs[b:b + n_scratch])
        mine = (refs[n_in:a], refs[a + n_out:b], refs[b + n_scratch:])
        return own, mine


def _run_comm(comm, name):
    n_in, n_out = len(comm.ins), len(comm.out_shapes)

    def body(*refs):
        parts = (refs[:n_in], refs[n_in:n_in + n_out], refs[n_in + n_out:])
        comm.start(*parts)
        comm.finish(*parts)

    return pl.pallas_call(
        body, name=name,
        in_specs=[_ANY] * n_in, out_specs=[_ANY] * n_out,
        out_shape=comm.out_shapes, scratch_shapes=comm.sem_shapes,
        input_output_aliases=comm.aliases,
    )(*comm.ins)


def _gather_comm(blocks):
    L = len(blocks)

    def parts(ins, outs, sems):
        send_sems, recv_sems, local_sems = sems
        x, y, c = _here()
        me, sibling = (x, y, c), (x, y, 1 - c)
        chips = _other_chips(x, y)

        def slot(px, py, pc):
            return 4 * px + 2 * py + pc

        def copy(l, k, block, to, src=None):
            dst = outs[l].at[slot(*block)]
            return pltpu.make_async_remote_copy(
                src_ref=dst if src is None else src, dst_ref=dst,
                send_sem=send_sems.at[l, k], recv_sem=recv_sems.at[l, k],
                device_id=to, device_id_type=MESH)

        mine = [pltpu.make_async_copy(ins[l], outs[l].at[slot(*me)], local_sems.at[l]) for l in range(L)]
        first = []
        for l in range(L):
            first.append(copy(l, 0, me, sibling, src=ins[l]))
            for j, chip in enumerate(chips):
                first.append(copy(l, 1 + j, me, (*chip, c), src=ins[l]))
        return c, me, sibling, chips, copy, mine, first

    def start(ins, outs, sems):
        *_, mine, first = parts(ins, outs, sems)
        for cp in mine + first:
            cp.start()

    def finish(ins, outs, sems):
        c, me, sibling, chips, copy, mine, first = parts(ins, outs, sems)
        passed = []
        for j, chip in enumerate(chips):
            for l in range(L):
                copy(l, 1 + j, (*chip, c), me).wait_recv()
                fwd = copy(l, 4 + j, (*chip, c), sibling)
                fwd.start()
                passed.append(fwd)
        for l in range(L):
            copy(l, 0, sibling, me).wait_recv()
        for j, chip in enumerate(chips):
            for l in range(L):
                copy(l, 4 + j, (*chip, 1 - c), me).wait_recv()
        for cp in first + passed:
            cp.wait_send()
        for cp in mine:
            cp.wait()

    return _Comm(blocks, [jax.ShapeDtypeStruct((N_DEV,) + b.shape, b.dtype) for b in blocks],
                 [pltpu.SemaphoreType.DMA((L, 7)), pltpu.SemaphoreType.DMA((L, 7)), pltpu.SemaphoreType.DMA((L,))],
                 start, finish)


def _allgather8(blocks, name):
    return _run_comm(_gather_comm(blocks), name)


def _swap_comm(arrs):
    L = len(arrs)

    def copies(ins, outs, sems):
        send_sems, recv_sems = sems
        x, y, c = _here()
        cps = []
        for l in range(L):
            half = arrs[l].shape[1] // 2
            rows = pl.ds(pl.multiple_of((1 - c) * half, 16), half)
            cps.append(pltpu.make_async_remote_copy(
                src_ref=ins[l].at[:, rows, :], dst_ref=outs[l], send_sem=send_sems.at[l],
                recv_sem=recv_sems.at[l], device_id=(x, y, 1 - c), device_id_type=MESH))
        return cps

    def start(ins, outs, sems):
        for cp in copies(ins, outs, sems):
            cp.start()

    def finish(ins, outs, sems):
        for cp in copies(ins, outs, sems):
            cp.wait()

    return _Comm(arrs, [jax.ShapeDtypeStruct((a.shape[0], a.shape[1] // 2, a.shape[2]), a.dtype) for a in arrs],
                 [pltpu.SemaphoreType.DMA((L,)), pltpu.SemaphoreType.DMA((L,))], start, finish)


def _join_comm(bufs):
    L = len(bufs)

    def start(ins, outs, sems):
        send_sems, recv_sems = sems
        x, y, c = _here()
        for l in range(L):
            pltpu.make_async_remote_copy(src_ref=outs[l].at[c], dst_ref=outs[l].at[c], send_sem=send_sems.at[l],
                                         recv_sem=recv_sems.at[l], device_id=(x, y, 1 - c),
                                         device_id_type=MESH).start()

    def finish(ins, outs, sems):
        send_sems, recv_sems = sems
        x, y, c = _here()
        for l in range(L):
            pltpu.make_async_remote_copy(src_ref=outs[l].at[c], dst_ref=outs[l].at[1 - c],
                                         send_sem=send_sems.at[l], recv_sem=recv_sems.at[l],
                                         device_id=(x, y, 1 - c), device_id_type=MESH).wait()

    return _Comm(bufs, [jax.ShapeDtypeStruct(a.shape, a.dtype) for a in bufs],
                 [pltpu.SemaphoreType.DMA((L,)), pltpu.SemaphoreType.DMA((L,))], start, finish,
                 aliases={l: l for l in range(L)})


def _scatter_comm(arrs):
    L = len(arrs)

    def copies(ins, outs, sems):
        send_sems, recv_sems = sems
        x, y, c = _here()
        return [pltpu.make_async_remote_copy(
            src_ref=ins[l].at[2 * tx + ty], dst_ref=outs[l].at[j],
            send_sem=send_sems.at[l, j], recv_sem=recv_sems.at[l, j],
            device_id=(tx, ty, c), device_id_type=MESH)
            for l in range(L) for j, (tx, ty) in enumerate(_other_chips(x, y))]

    def start(ins, outs, sems):
        for cp in copies(ins, outs, sems):
            cp.start()

    def finish(ins, outs, sems):
        for cp in copies(ins, outs, sems):
            cp.wait()

    return _Comm(arrs, [jax.ShapeDtypeStruct((3,) + a.shape[1:], a.dtype) for a in arrs],
                 [pltpu.SemaphoreType.DMA((L, 3)), pltpu.SemaphoreType.DMA((L, 3))], start, finish)


_A_ORDER = np.array(A_HEAD_ORDER)
_A_INVERSE = np.argsort(_A_ORDER)


def _permute_in_weights(w_in):
    qa = w_in[:, 0:512].reshape(D_MODEL, A_Q_HEADS, HEAD_DIM)[:, _A_ORDER, :].reshape(D_MODEL, 512)
    f_pad = jnp.pad(w_in[:, 2304:2312], ((0, 0), (0, LANES - B_HEADS)))
    w_a = jnp.concatenate([qa, w_in[:, 512:640], f_pad], axis=1)
    return w_a, w_in[:, 640:2304], w_in[:, 2312:4360]


def _slab_segments():
    segs = [(h * HEAD_DIM, int(_A_INVERSE[h]) * HEAD_DIM, HEAD_DIM) for h in range(A_Q_HEADS)]
    segs += [(512, OFF_KA, 128), (640, W_A + OFF_VA, 128), (768, W_A + OFF_QB, 1536),
             (2304, OFF_F, B_HEADS), (2312, W_A + W_B, W_G)]
    return segs


def _shard_slabs(dw_perm):
    R = dw_perm.shape[0]
    tr = _tile(R, 128, 8)
    plan = []
    for k in range(N_CHIP):
        for b in range(W_SHARD_PAD // LANES):
            lo, hi = k * W_SHARD + b * LANES, min(k * W_SHARD + (b + 1) * LANES, (k + 1) * W_SHARD)
            parts = []
            for o0, s0, n in _slab_segments():
                a, z = max(lo, o0), min(hi, o0 + n)
                while a < z:
                    s = s0 + (a - o0)
                    run = min(z - a, LANES - s % LANES)
                    parts.append((s // LANES, ((a - lo) - s % LANES) % LANES, a - lo, run))
                    a += run
            plan.append((k, b, parts))

    def kern(x_ref, o32_ref, obf_ref):
        lane = lax.broadcasted_iota(jnp.int32, (tr, LANES), 1)
        for k, b, parts in plan:
            acc = jnp.zeros((tr, LANES), F32)
            for src, rot, first, run in parts:
                blk = x_ref[:, src * LANES:(src + 1) * LANES]
                if rot:
                    blk = pltpu.roll(blk, rot, 1)
                acc = jnp.where((lane >= first) & (lane < first + run), blk, acc)
            o32_ref[k, :, b * LANES:(b + 1) * LANES] = acc
            obf_ref[k, :, b * LANES:(b + 1) * LANES] = acc.astype(BF)

    out_spec = pl.BlockSpec((N_CHIP, tr, W_SHARD_PAD), lambda i: (0, i, 0))
    return tuple(pl.pallas_call(
        kern, name="shard_slabs", grid=(R // tr,),
        in_specs=[pl.BlockSpec((tr, W_PERM), lambda i: (i, 0))],
        out_specs=[out_spec, out_spec],
        out_shape=[jax.ShapeDtypeStruct((N_CHIP, R, W_SHARD_PAD), F32),
                   jax.ShapeDtypeStruct((N_CHIP, R, W_SHARD_PAD), BF)],
        compiler_params=_cparams("parallel"),
    )(dw_perm))


class _NoExchange:
    def __init__(self, w_in, rest):
        self.w_in_whole, self.rest, self.grads, self.joined = w_in, rest, {}, {}

    def w_in_comm(self):
        return None

    def w_in(self, outs):
        return self.w_in_whole

    def rest_weights_comm(self):
        return None

    def rest_weights(self, outs):
        return self.rest

    def swap_comm(self, pieces, tag):
        self.grads[tag] = [p32 for p32, _ in pieces]
        return None

    def swap_done(self, outs, tag):
        return None

    def reduce_done(self, outs, tag):
        pass

    def join_comm(self, tag):
        return None


class _Exchange:
    def __init__(self, ci, chip, w_in_shard, rest_shards):
        self.ci, self.chip, self.w_in_shard, self.rest_shards = ci, chip, w_in_shard, rest_shards
        self.pieces, self.part_f32, self.halves, self.joined = {}, {}, {}, {}

    def _my_half(self, a, axis=0, other=False):
        rows = a.shape[axis] // 2
        return lax.dynamic_slice_in_dim(a, ((1 - self.ci) if other else self.ci) * rows, rows, axis=axis)

    def w_in_comm(self):
        return _gather_comm([self._my_half(self.w_in_shard).astype(BF)])

    def w_in(self, outs):
        return _col_sharded(outs[0])

    def rest_weights_comm(self):
        return _gather_comm([self._my_half(w).astype(BF) for w in self.rest_shards])

    def rest_weights(self, outs):
        w_ba, w_bb, w_out, w_fi, w_fo = outs
        return (_col_sharded(w_ba), _col_sharded(w_bb), _row_sharded(w_out), _col_sharded(w_fi),
                _row_sharded(w_fo))

    def swap_comm(self, pieces, tag):
        self.pieces[tag] = pieces
        return _swap_comm([pbf for _, pbf in pieces])

    def swap_done(self, got, tag):
        self.part_f32[tag], part_bf = [], []
        for l, ((p32, _), g_) in enumerate(zip(self.pieces[tag], got)):
            s32, sbf = _add_pair(p32, g_, self.ci, f"chip_sum_{tag}_{l}")
            self.part_f32[tag].append(s32)
            part_bf.append(sbf)
        return _scatter_comm(part_bf)

    def reduce_done(self, outs, tag):
        self.halves[tag] = [_add_three(p32, r, self.chip, f"shard_sum_{tag}_{l}")
                            for l, (p32, r) in enumerate(zip(self.part_f32[tag], outs))]

    def join_comm(self, tag):
        return _join_comm(self.halves[tag])


def _col_sharded(g):
    return jnp.transpose(g.reshape(N_CHIP, -1, g.shape[-1]), (1, 0, 2)).reshape(2 * g.shape[1], N_CHIP * g.shape[-1])


def _row_sharded(g):
    return g.reshape(N_DEV * g.shape[1], g.shape[-1])


def _rope_tables(pos):
    inv_freq = 1.0 / (ROPE_THETA ** (jnp.arange(0, HEAD_DIM, 2, dtype=F32) / HEAD_DIM))
    ang = pos.astype(F32)[:, None] * inv_freq
    cos, sin = jnp.cos(ang), jnp.sin(ang)
    return jnp.tile(cos, (1, 4)), jnp.tile(jnp.concatenate([-sin, sin], axis=1), (1, 2))


def _local_step(x, pos, ada, g1, g2, g3, g4, b_f, sinks, exch, target):
    S = x.shape[0]
    t_fox = _tile(S, 512, LANES) if S >= 1024 else S // 2
    t_fox_fwd = _tile(S, 1024, LANES) if S >= 2048 else S // 2
    shift_m, scale_m, gate_m, shift_f, scale_f, gate_f = [ada[i:i + 1] for i in range(N_ADA)]
    cos_t, sin_t = _rope_tables(pos)
    sinks_p = sinks.reshape(A_KV_HEADS, 4).T.reshape(A_Q_HEADS)
    b_f_pad = jnp.pad(b_f, (0, LANES - B_HEADS)).reshape(1, LANES)

    h1, outs = _pre_norm(x, g1, scale_m, shift_m, "pre_mix_norm", comm=exch.w_in_comm())
    w_a, w_b, w_g = _permute_in_weights(exch.w_in(outs))
    w_perm = jnp.concatenate([w_a, w_b, w_g], axis=1)
    p_a = _mm(h1, w_a, "nn", F32, "proj_a")
    p_b = _mm(h1, w_b, "nn", BF, "proj_b")
    p_g = _mm(h1, w_g, "nn", BF, "proj_g")
    (qk_a,) = _rope([p_a], [640], cos_t, sin_t, "rope_fwd")
    o_a, lse_a = _swa_fwd(qk_a, p_b, 0, sinks_p)
    cum = _fox_gate_fwd(p_a, b_f_pad)
    bq, bk = _fox_prep_fwd(cum, t_fox)
    comm = exch.rest_weights_comm()
    o_b, lse_b, outs = _fox_fwd(p_b, bq, bk, t_fox_fwd, comm=comm)
    w_ba, w_bb, w_out, w_fi, w_fo = exch.rest_weights(outs)
    w_ba_p = w_ba.reshape(A_Q_HEADS, HEAD_DIM, D_MODEL)[_A_ORDER].reshape(512, D_MODEL)
    pa = _mm(o_a, w_ba_p, "nn", BF, "branch_a")
    pb = _mm(o_b, w_bb, "nn", BF, "branch_b")
    merged = _merge_fwd(p_g, pa, pb)
    y1 = _mm(merged, w_out, "nn", BF, "out_proj")
    x2, h2 = _post_pre(x, y1, g2, gate_m, g3, scale_f, shift_f)
    gu = _mm(h2, w_fi, "nn", BF, "ffn_in")
    act = _swiglu_fwd(gu)
    y2 = _mm(act, w_fo, "nn", BF, "ffn_out")
    d_out, d_y2, st_f = _final(x2, y2, g4, gate_f, target)

    d_act = _mm(d_y2, w_fo, "nt", BF, "ffn_out_dx")
    row_pieces = lambda pair: tuple(t.reshape(N_CHIP, t.shape[0] // N_CHIP, t.shape[1]) for t in pair)
    dw_fo = row_pieces(_mm(act, d_y2, "tn", F32, "ffn_out_dw", twin=True))
    d_gu = _swiglu_bwd(d_act, gu)
    d_h2 = _mm(d_gu, w_fi, "nt", BF, "ffn_in_dx")
    dw_fi = _mm(h2, d_gu, "tn", F32, "ffn_in_dw", col_pieces=N_CHIP, twin=True)
    d_x2, d_y1, st_m = _mid_bwd(d_h2, x2, d_out, y1, g3, scale_f, g2, gate_m)
    d_merged = _mm(d_y1, w_out, "nt", BF, "out_proj_dx")
    dw_out = row_pieces(_mm(merged, d_y1, "tn", F32, "out_proj_dw", twin=True))
    d_pa, d_pb, d_ga, d_gb = _merge_bwd(d_merged, p_g, pa, pb)
    d_oa = _mm(d_pa, w_ba_p, "nt", F32, "branch_a_dx")
    dw_ba_p = _mm(o_a, d_pa, "tn", F32, "branch_a_dw", col_pieces=N_CHIP, twin=True)
    d_ob = _mm(d_pb, w_bb, "nt", F32, "branch_b_dx")
    dw_bb = _mm(o_b, d_pb, "tn", F32, "branch_b_dw", col_pieces=N_CHIP, twin=True)
    head_rows = lambda t: t.reshape(N_CHIP, A_Q_HEADS, HEAD_DIM, -1)[:, _A_INVERSE].reshape(t.shape)
    dw_ba = tuple(head_rows(t) for t in dw_ba_p)
    comm = exch.swap_comm([dw_ba, dw_bb, dw_out, dw_fi, dw_fo], "early")
    dq_a, dk_a, dv_a, d_sink, outs = _swa_bwd(qk_a, p_b, 0, o_a, d_oa, lse_a, sinks_p, comm=comm)
    comm = exch.swap_done(outs, "early")
    bq_bwd, bdo = _fox_prep_bwd(cum, o_b, d_ob, lse_b, t_fox)
    dq_b, dk_b, dv_b, d_ck, d_cq, outs = _fox_bwd(p_b, d_ob, bq_bwd, bk, bdo, t_fox, comm=comm)
    exch.reduce_done(outs, "early")
    d_qa, d_ka = _rope([dq_a, dk_a], [512, LANES], cos_t, -sin_t, "rope_bwd")
    d_ck_cols = jnp.pad(d_ck.reshape(B_HEADS, S).T, ((0, 0), (0, LANES - B_HEADS)))
    d_f, d_bf = _fox_gate_bwd(d_cq, d_ck_cols, p_a, b_f_pad)
    d_proj = jnp.concatenate([d_qa, d_ka, d_f, dv_a.astype(BF), dq_b.astype(BF), dk_b.astype(BF),
                              dv_b.astype(BF), d_ga, d_gb], axis=1)
    comm = exch.join_comm("early")
    res = _mm(h1, d_proj, "tn", F32, "proj_dw", comm=comm)
    dw_perm = res[0] if comm else res
    exch.joined["early"] = res[1] if comm else []
    swap = exch.swap_comm([_shard_slabs(dw_perm)], "late")
    comm = exch.swap_done(_run_comm(swap, "grads_to_sibling_late") if swap else None, "late")
    res = _mm(d_proj, w_perm, "nt", BF, "proj_dx", comm=comm)
    d_h1 = res[0] if comm else res
    exch.reduce_done(res[1] if comm else None, "late")
    comm = exch.join_comm("late")
    exch.joined["late"] = _run_comm(comm, "grads_join_late") if comm else []
    grad_x, st_p, _ = _pre_bwd(d_h1, x, d_x2, g1, scale_m)

    d_sinks = d_sink[:, :2, 0].T.reshape(A_Q_HEADS)
    small = jnp.concatenate([
        st_p[0], st_p[1], st_m[3], st_m[0], st_m[1], st_f[0],
        st_p[2], st_m[4], st_m[2], st_f[1],
        st_f[2], d_bf[0, :B_HEADS], d_sinks,
        jnp.zeros((SM_LEN - SM_SINK - A_Q_HEADS,), F32)])
    return grad_x, small


def kernel(x, c, positions, w_ada, b_ada, g_pre_mix, g_post_mix, w_in, b_f, sinks, w_branch_a, w_branch_b, w_out, g_pre_ffn, g_post_ffn, w_ffn_in, w_ffn_out, loss_target, m_w_ada, m_b_ada, m_g_pre_mix, m_g_post_mix, m_w_in, m_b_f, m_sinks, m_w_branch_a, m_w_branch_b, m_w_out, m_g_pre_ffn, m_g_post_ffn, m_w_ffn_in, m_w_ffn_out, v_w_ada, v_b_ada, v_g_pre_mix, v_g_post_mix, v_w_in, v_b_f, v_sinks, v_w_branch_a, v_w_branch_b, v_w_out, v_g_pre_ffn, v_g_post_ffn, v_w_ffn_in, v_w_ffn_out):
    xi, yi, ci = _here()
    chip = 2 * xi + yi
    dev = 2 * chip + ci

    (c_g,) = _allgather8([c.reshape(8, LANES)], "gather_c")
    c_all = c_g.reshape(N_DEV, D_MODEL)
    exch = _Exchange(ci, chip, w_in[0], [w_branch_a[0], w_branch_b[0], w_out[0], w_ffn_in[0], w_ffn_out[0]])

    ada_cols = _mm(c_all, w_ada[0], "nn", F32, "ada_fwd")
    (ada_g,) = _allgather8([ada_cols], "gather_ada")
    ada_mine = lax.dynamic_index_in_dim(ada_g.reshape(N_CHIP, 2, N_DEV, -1)[:, 0], dev, axis=1, keepdims=False)
    ada = (ada_mine.reshape(-1) + b_ada[0]).reshape(N_ADA, D_MODEL)

    grad_x, small = _local_step(
        x[0], positions[0], ada, g_pre_mix, g_post_mix, g_pre_ffn, g_post_ffn, b_f[0], sinks[0],
        exch, loss_target[0])

    g_w_in, g_w_ba, g_w_bb, g_w_out, g_w_fi, g_w_fo = [
        j.reshape(2 * j.shape[1], j.shape[2]) for j in list(exch.joined["late"]) + list(exch.joined["early"])]
    upd_fi, (small_g,) = _adamw(w_ffn_in[0], g_w_fi, m_w_ffn_in[0], v_w_ffn_in[0], "adamw_w_ffn_in",
                                comm=_gather_comm([small.reshape(8, SM_LEN // 8)]))

    small_all = small_g.reshape(N_DEV, SM_LEN)
    small_tot, loss_row = _small_finalize(small_all)
    loss = loss_row[0, 0]
    d_ada_cols = lax.dynamic_slice_in_dim(small_all[:, :N_ADA * D_MODEL], chip * (N_ADA * D_MODEL // N_CHIP),
                                          N_ADA * D_MODEL // N_CHIP, axis=1)
    g_w_ada = _ada_dw(c_all.T, d_ada_cols)

    def small_vec(b_ada_, g1_, g2_, g3_, g4_, b_f_, sinks_):
        return jnp.concatenate([b_ada_[0], g1_[0], g2_[0], g3_[0], g4_[0], jnp.zeros((D_MODEL,), F32),
                                b_f_[0], sinks_[0], jnp.zeros((SM_LEN - SM_SINK - A_Q_HEADS,), F32)]
                               ).reshape(8, SM_LEN // 8)

    sw = small_vec(b_ada, g_pre_mix, g_post_mix, g_pre_ffn, g_post_ffn, b_f, sinks)
    sm = small_vec(m_b_ada, m_g_pre_mix, m_g_post_mix, m_g_pre_ffn, m_g_post_ffn, m_b_f, m_sinks)
    sv = small_vec(v_b_ada, v_g_pre_mix, v_g_post_mix, v_g_pre_ffn, v_g_post_ffn, v_b_f, v_sinks)
    s_upd = [u.reshape(SM_LEN) for u in _adamw(sw, small_tot.reshape(8, SM_LEN // 8), sm, sv, "adamw_small")]
    s_grad = small_tot.reshape(SM_LEN)

    def unpack(vec):
        row = lambda a, n: vec[a:a + n].reshape(1, n)
        return dict(b_ada=row(SM_ADA, N_ADA * D_MODEL), g_pre_mix=row(SM_G, D_MODEL),
                    g_post_mix=row(SM_G + D_MODEL, D_MODEL), g_pre_ffn=row(SM_G + 2 * D_MODEL, D_MODEL),
                    g_post_ffn=row(SM_G + 3 * D_MODEL, D_MODEL), b_f=row(SM_BF, B_HEADS),
                    sinks=row(SM_SINK, A_Q_HEADS))

    big = dict(
        w_ada=(w_ada, g_w_ada, m_w_ada, v_w_ada),
        w_branch_a=(w_branch_a, g_w_ba, m_w_branch_a, v_w_branch_a),
        w_branch_b=(w_branch_b, g_w_bb, m_w_branch_b, v_w_branch_b),
        w_out=(w_out, g_w_out, m_w_out, v_w_out),
        w_ffn_out=(w_ffn_out, g_w_fo, m_w_ffn_out, v_w_ffn_out))
    grads, deltas, new_m, new_v = unpack(s_grad), unpack(s_upd[0]), unpack(s_upd[1]), unpack(s_upd[2])
    grads["w_ffn_in"], deltas["w_ffn_in"], new_m["w_ffn_in"], new_v["w_ffn_in"] = [
        t[None] for t in (g_w_fi, *upd_fi)]
    for n, (w_, g_, m_, v_) in big.items():
        d_, nm_, nv_ = _adamw(w_[0], g_, m_[0], v_[0], "adamw_" + n)
        grads[n], deltas[n], new_m[n], new_v[n] = g_[None], d_[None], nm_[None], nv_[None]
    pad_cols = lambda a: jnp.pad(a, ((0, 0), (0, W_SHARD_PAD - W_SHARD)))
    upd = _adamw(pad_cols(w_in[0]), g_w_in, pad_cols(m_w_in[0]), pad_cols(v_w_in[0]), "adamw_w_in")
    grads["w_in"], deltas["w_in"], new_m["w_in"], new_v["w_in"] = [t[None, :, :W_SHARD] for t in (g_w_in, *upd)]

    names = ["w_ada", "b_ada", "g_pre_mix", "g_post_mix", "w_in", "b_f", "sinks", "w_branch_a", "w_branch_b",
             "w_out", "g_pre_ffn", "g_post_ffn", "w_ffn_in", "w_ffn_out"]
    return (loss, grad_x[None], *[grads[n] for n in names], *[deltas[n] for n in names],
            *[new_m[n] for n in names], *[new_v[n] for n in names])
```

```python
import functools
import math

import numpy as np
import jax
import jax.numpy as jnp
from jax import lax
from jax.experimental import pallas as pl
from jax.experimental.pallas import tpu as pltpu

F32 = jnp.float32
BF = jnp.bfloat16

D_MODEL = 1024
HEAD_DIM = 64
LANES = 128
WINDOW = 128
A_Q_HEADS = 8
A_KV_HEADS = 2
B_HEADS = 8
D_FF = 2816
ROPE_THETA = 10000.0
RMS_EPS = 1e-6
N_ADA = 6
N_DEV = 8
N_CHIP = 4

ADAM_LR = 0.001
ADAM_B1 = 0.9
ADAM_B2 = 0.999
ADAM_EPS = 1e-08
ADAM_WD = 0.01
ADAM_STEP = 10

VMEM_LIMIT = 48 * 1024 * 1024
MESH = pl.DeviceIdType.MESH

A_HEAD_ORDER = (0, 4, 1, 5, 2, 6, 3, 7)

OFF_QA, OFF_KA, OFF_F = 0, 512, 640
W_A = 768
OFF_VA, OFF_QB, OFF_KB, OFF_VB = 0, 128, 640, 1152
W_B = 1664
W_G = 2048
W_PERM = W_A + W_B + W_G
W_SHARD = 1090
W_SHARD_PAD = 1152


def _tile(n, cap, mult=LANES):
    if n <= cap:
        return n
    t = (cap // mult) * mult
    while t >= mult:
        if n % t == 0:
            return t
        t -= mult
    raise ValueError(f"no tile for {n}")


MXU_WIDTH = 256
MM_OPERAND_BYTES = 28 * 1024 * 1024


def _mm_tiles(M, N, K, a_bytes, b_bytes, tm_cap, tn_cap):
    tm = _tile(M, tm_cap)
    try:
        tn = _tile(N, tn_cap, MXU_WIDTH)
    except ValueError:
        tn = _tile(N, tn_cap)
    fits = lambda tk: 2 * tk * (tm * a_bytes + tn * b_bytes) <= MM_OPERAND_BYTES
    tk = K if fits(K) else next(t for t in range(K // LANES * LANES, 0, -LANES) if K % t == 0 and fits(t))
    return tm, tn, tk


def _cparams(*sem):
    return pltpu.CompilerParams(dimension_semantics=sem, vmem_limit_bytes=VMEM_LIMIT)


def _own_refs(refs, comm, n_in, n_out, n_scratch):
    if comm is None:
        return list(refs), None
    return comm.split(refs, n_in, n_out, n_scratch)


def _comm_specs(comm, side):
    if comm is None:
        return []
    return [pl.BlockSpec(memory_space=pl.ANY)] * len(comm.ins if side == "in" else comm.out_shapes)


def _comm_edge(comm, comm_refs, grid, first):
    if comm is None:
        return
    at_edge = None
    for axis, n in enumerate(grid):
        here = pl.program_id(axis) == (0 if first else n - 1)
        at_edge = here if at_edge is None else at_edge & here
    pl.when(at_edge)(lambda: (comm.start if first else comm.finish)(*comm_refs))


def _mm(a, b, mode, out_dtype, name, tm_cap=512, tn_cap=2816, comm=None, col_pieces=1, twin=False):
    if mode == "nn":
        (M, K), (K2, N) = a.shape, b.shape
        dims = (((1,), (0,)), ((), ()))
    elif mode == "nt":
        (M, K), (N, K2) = a.shape, b.shape
        dims = (((1,), (1,)), ((), ()))
    else:
        (K, M), (K2, N) = a.shape, b.shape
        dims = (((0,), (0,)), ((), ()))
    assert K == K2, (a.shape, b.shape, mode)
    tm, tn, tk = _mm_tiles(M, N // col_pieces, K, a.dtype.itemsize, b.dtype.itemsize, tm_cap, tn_cap)
    nk = K // tk
    n_out = 2 if twin else 1
    n_scratch = 1 if nk > 1 else 0
    if mode == "nn":
        a_spec = pl.BlockSpec((tm, tk), lambda i, j, k: (i, k))
        b_spec = pl.BlockSpec((tk, tn), lambda i, j, k: (k, j))
    elif mode == "nt":
        a_spec = pl.BlockSpec((tm, tk), lambda i, j, k: (i, k))
        b_spec = pl.BlockSpec((tn, tk), lambda i, j, k: (j, k))
    else:
        a_spec = pl.BlockSpec((tk, tm), lambda i, j, k: (k, i))
        b_spec = pl.BlockSpec((tk, tn), lambda i, j, k: (k, j))

    grid = (M // tm, N // tn, nk)

    def kern(*refs):
        own, comm_refs = _own_refs(refs, comm, 2, n_out, n_scratch)
        a_ref, b_ref, o_refs = own[0], own[1], own[2:2 + n_out]
        k = pl.program_id(2)
        _comm_edge(comm, comm_refs, grid, first=True)
        part = lax.dot_general(a_ref[...].astype(BF), b_ref[...].astype(BF), dims,
                               preferred_element_type=F32)
        if nk == 1:
            for o_ref in o_refs:
                o_ref[...] = part.astype(o_ref.dtype)
        else:
            acc_ref = own[2 + n_out]

            @pl.when(k == 0)
            def _():
                acc_ref[...] = part

            @pl.when(k > 0)
            def _():
                acc_ref[...] += part

            @pl.when(k == nk - 1)
            def _():
                for o_ref in o_refs:
                    o_ref[...] = acc_ref[...].astype(o_ref.dtype)

        _comm_edge(comm, comm_refs, grid, first=False)

    if col_pieces > 1:
        per = N // col_pieces // tn
        out_spec = pl.BlockSpec((None, tm, tn), lambda i, j, k: (j // per, i, j % per))
        shape = (col_pieces, M, N // col_pieces)
    else:
        out_spec = pl.BlockSpec((tm, tn), lambda i, j, k: (i, j))
        shape = (M, N)
    dtypes = [out_dtype, BF] if twin else [out_dtype]
    res = pl.pallas_call(
        kern, name=name, grid=grid,
        in_specs=[a_spec, b_spec] + _comm_specs(comm, "in"),
        out_specs=[out_spec] * n_out + _comm_specs(comm, "out"),
        out_shape=[jax.ShapeDtypeStruct(shape, d) for d in dtypes] + (comm.out_shapes if comm else []),
        scratch_shapes=[pltpu.VMEM((tm, tn), F32)] * n_scratch + (comm.sem_shapes if comm else []),
        compiler_params=_cparams("parallel", "parallel", "arbitrary"),
    )(a, b, *(comm.ins if comm else []))
    own = res[0] if n_out == 1 else tuple(res[:n_out])
    return (own, res[n_out:]) if comm else own


ROWS = 512


def _row_spec(tm, width=D_MODEL, col=0):
    return pl.BlockSpec((tm, width), lambda i: (i, col))


def _vec_spec(width=D_MODEL):
    return pl.BlockSpec((1, width), lambda i: (0, 0))


def _rms(x):
    return lax.rsqrt(jnp.mean(x * x, axis=-1, keepdims=True) + RMS_EPS)


def _colsum(x):
    return jnp.sum(x, axis=0, keepdims=True)


def _norm_bwd(d_xn, xn, r):
    return r * (d_xn - xn * jnp.mean(d_xn * xn, axis=-1, keepdims=True))


def _pre_norm(x, g, scale, shift, name, comm=None):
    S = x.shape[0]
    tm = _tile(S, ROWS, 8)
    grid = (S // tm,)

    def kern(*refs):
        (x_ref, g_ref, sc_ref, sh_ref, h_ref), comm_refs = _own_refs(refs, comm, 4, 1, 0)
        _comm_edge(comm, comm_refs, grid, first=True)
        xf = x_ref[...]
        y = xf * _rms(xf) * g_ref[...]
        h_ref[...] = (y * (1.0 + sc_ref[...]) + sh_ref[...]).astype(BF)
        _comm_edge(comm, comm_refs, grid, first=False)

    res = pl.pallas_call(
        kern, name=name, grid=grid,
        in_specs=[_row_spec(tm), _vec_spec(), _vec_spec(), _vec_spec()] + _comm_specs(comm, "in"),
        out_specs=[_row_spec(tm)] + _comm_specs(comm, "out"),
        out_shape=[jax.ShapeDtypeStruct((S, D_MODEL), BF)] + (comm.out_shapes if comm else []),
        scratch_shapes=comm.sem_shapes if comm else [],
        compiler_params=_cparams("arbitrary"),
    )(x, g, scale, shift, *(comm.ins if comm else []))
    return res[0], res[1:]


def _post_pre(x, y1, g2, gate_m, g3, scale_f, shift_f):
    S = x.shape[0]
    tm = _tile(S, ROWS, 8)

    def kern(x_ref, y_ref, g2_ref, gm_ref, g3_ref, sc_ref, sh_ref, x2_ref, h2_ref):
        y = y_ref[...].astype(F32)
        n2 = y * _rms(y) * g2_ref[...]
        x2 = x_ref[...] + gm_ref[...] * n2
        x2_ref[...] = x2
        n3 = x2 * _rms(x2) * g3_ref[...]
        h2_ref[...] = (n3 * (1.0 + sc_ref[...]) + sh_ref[...]).astype(BF)

    return pl.pallas_call(
        kern, name="post_mix_pre_ffn", grid=(S // tm,),
        in_specs=[_row_spec(tm), _row_spec(tm)] + [_vec_spec()] * 5,
        out_specs=[_row_spec(tm), _row_spec(tm)],
        out_shape=[jax.ShapeDtypeStruct((S, D_MODEL), F32), jax.ShapeDtypeStruct((S, D_MODEL), BF)],
        compiler_params=_cparams("parallel"),
    )(x, y1, g2, gate_m, g3, scale_f, shift_f)


def _stats_spec():
    return pl.BlockSpec((8, D_MODEL), lambda i: (0, 0))


def _final(x2, y2, g4, gate_f, target):
    S = x2.shape[0]
    tm = _tile(S, ROWS, 8)

    def kern(x2_ref, y_ref, g4_ref, gf_ref, t_ref, dout_ref, dy_ref, st_ref):
        @pl.when(pl.program_id(0) == 0)
        def _():
            st_ref[...] = jnp.zeros_like(st_ref)

        y = y_ref[...].astype(F32)
        r = _rms(y)
        yn = y * r
        n4 = yn * g4_ref[...]
        diff = x2_ref[...] + gf_ref[...] * n4 - t_ref[...]
        d_out = diff / D_MODEL
        dout_ref[...] = d_out
        dn = d_out * gf_ref[...]
        dy_ref[...] = _norm_bwd(dn * g4_ref[...], yn, r).astype(BF)
        st_ref[0:1, :] += _colsum(d_out * n4)
        st_ref[1:2, :] += _colsum(dn * yn)
        st_ref[2:3, :] += _colsum(diff * diff)

    return pl.pallas_call(
        kern, name="final_loss", grid=(S // tm,),
        in_specs=[_row_spec(tm), _row_spec(tm), _vec_spec(), _vec_spec(), _row_spec(tm)],
        out_specs=[_row_spec(tm), _row_spec(tm), _stats_spec()],
        out_shape=[jax.ShapeDtypeStruct((S, D_MODEL), F32), jax.ShapeDtypeStruct((S, D_MODEL), BF),
                   jax.ShapeDtypeStruct((8, D_MODEL), F32)],
        compiler_params=_cparams("arbitrary"),
    )(x2, y2, g4, gate_f, target)


def _mid_bwd(d_h2, x2, d_out, y1, g3, scale_f, g2, gate_m):
    S = x2.shape[0]
    tm = _tile(S, ROWS, 8)

    def kern(dh_ref, x2_ref, dout_ref, y_ref, g3_ref, sc_ref, g2_ref, gm_ref, dx2_ref, dy_ref, st_ref):
        @pl.when(pl.program_id(0) == 0)
        def _():
            st_ref[...] = jnp.zeros_like(st_ref)

        dh = dh_ref[...].astype(F32)
        x2 = x2_ref[...]
        r3 = _rms(x2)
        xn = x2 * r3
        one_sc = 1.0 + sc_ref[...]
        d_x2 = dout_ref[...] + _norm_bwd(dh * one_sc * g3_ref[...], xn, r3)
        dx2_ref[...] = d_x2
        y = y_ref[...].astype(F32)
        r2 = _rms(y)
        yn = y * r2
        dn = d_x2 * gm_ref[...]
        dy_ref[...] = _norm_bwd(dn * g2_ref[...], yn, r2).astype(BF)
        st_ref[0:1, :] += _colsum(dh)
        st_ref[1:2, :] += _colsum(dh * (xn * g3_ref[...]))
        st_ref[2:3, :] += _colsum(dh * one_sc * xn)
        st_ref[3:4, :] += _colsum(d_x2 * (yn * g2_ref[...]))
        st_ref[4:5, :] += _colsum(dn * yn)

    return pl.pallas_call(
        kern, name="mid_bwd", grid=(S // tm,),
        in_specs=[_row_spec(tm)] * 4 + [_vec_spec()] * 4,
        out_specs=[_row_spec(tm), _row_spec(tm), _stats_spec()],
        out_shape=[jax.ShapeDtypeStruct((S, D_MODEL), F32), jax.ShapeDtypeStruct((S, D_MODEL), BF),
                   jax.ShapeDtypeStruct((8, D_MODEL), F32)],
        compiler_params=_cparams("arbitrary"),
    )(d_h2, x2, d_out, y1, g3, scale_f, g2, gate_m)


def _pre_bwd(d_h1, x, d_x2, g1, scale_m, comm=None):
    S = x.shape[0]
    tm = _tile(S, ROWS, 8)
    grid = (S // tm,)

    def kern(*refs):
        (dh_ref, x_ref, dx2_ref, g_ref, sc_ref, gx_ref, st_ref), comm_refs = _own_refs(refs, comm, 5, 2, 0)
        _comm_edge(comm, comm_refs, grid, first=True)

        @pl.when(pl.program_id(0) == 0)
        def _():
            st_ref[...] = jnp.zeros_like(st_ref)

        dh = dh_ref[...].astype(F32)
        xf = x_ref[...]
        r = _rms(xf)
        xn = xf * r
        one_sc = 1.0 + sc_ref[...]
        gx_ref[...] = dx2_ref[...] + _norm_bwd(dh * one_sc * g_ref[...], xn, r)
        st_ref[0:1, :] += _colsum(dh)
        st_ref[1:2, :] += _colsum(dh * (xn * g_ref[...]))
        st_ref[2:3, :] += _colsum(dh * one_sc * xn)
        _comm_edge(comm, comm_refs, grid, first=False)

    res = pl.pallas_call(
        kern, name="pre_mix_bwd", grid=grid,
        in_specs=[_row_spec(tm)] * 3 + [_vec_spec()] * 2 + _comm_specs(comm, "in"),
        out_specs=[_row_spec(tm), _stats_spec()] + _comm_specs(comm, "out"),
        out_shape=[jax.ShapeDtypeStruct((S, D_MODEL), F32), jax.ShapeDtypeStruct((8, D_MODEL), F32)]
        + (comm.out_shapes if comm else []),
        scratch_shapes=comm.sem_shapes if comm else [],
        input_output_aliases={5 + i: 2 + o for i, o in comm.aliases.items()} if comm else {},
        compiler_params=_cparams("arbitrary"),
    )(d_h1, x, d_x2, g1, scale_m, *(comm.ins if comm else []))
    return res[0], res[1], res[2:]


def _rope(xs, widths, cos_t, sin_t, name):
    S = xs[0].shape[0]
    tm = _tile(S, 512, 8)
    n = len(xs)

    def kern(*refs):
        cos = refs[n][...]
        sin = refs[n + 1][...]
        first = (lax.broadcasted_iota(jnp.int32, cos.shape, 1) % HEAD_DIM) < HEAD_DIM // 2
        for x_ref, o_ref, w in zip(refs[:n], refs[n + 2:], widths):
            for c0 in range(0, w, LANES):
                v = x_ref[:, c0:c0 + LANES]
                partner = jnp.where(first, pltpu.roll(v, LANES - HEAD_DIM // 2, 1),
                                    pltpu.roll(v, HEAD_DIM // 2, 1))
                o_ref[:, c0:c0 + LANES] = (v * cos + partner * sin).astype(BF)

    return pl.pallas_call(
        kern, name=name, grid=(S // tm,),
        in_specs=[_row_spec(tm, w) for w in widths] + [_row_spec(tm, LANES)] * 2,
        out_specs=[_row_spec(tm, w) for w in widths],
        out_shape=[jax.ShapeDtypeStruct((S, w), BF) for w in widths],
        compiler_params=_cparams("parallel"),
    )(*xs, cos_t, sin_t)


def _merge_fwd(pg, pa, pb):
    S = pa.shape[0]
    tm = _tile(S, ROWS, 8)

    def kern(ga_ref, gb_ref, pa_ref, pb_ref, o_ref):
        ga = jax.nn.sigmoid(ga_ref[...].astype(F32))
        gb = jax.nn.sigmoid(gb_ref[...].astype(F32))
        o_ref[...] = (ga * pa_ref[...].astype(F32) + gb * pb_ref[...].astype(F32)).astype(BF)

    return pl.pallas_call(
        kern, name="merge_fwd", grid=(S // tm,),
        in_specs=[_row_spec(tm, col=0), _row_spec(tm, col=1), _row_spec(tm), _row_spec(tm)],
        out_specs=_row_spec(tm),
        out_shape=jax.ShapeDtypeStruct((S, D_MODEL), BF),
        compiler_params=_cparams("parallel"),
    )(pg, pg, pa, pb)


def _merge_bwd(d_merged, pg, pa, pb):
    S = pa.shape[0]
    tm = _tile(S, ROWS, 8)

    def kern(dm_ref, ga_ref, gb_ref, pa_ref, pb_ref, dpa_ref, dpb_ref, dga_ref, dgb_ref):
        dm = dm_ref[...].astype(F32)
        ga = jax.nn.sigmoid(ga_ref[...].astype(F32))
        gb = jax.nn.sigmoid(gb_ref[...].astype(F32))
        dpa_ref[...] = (dm * ga).astype(BF)
        dpb_ref[...] = (dm * gb).astype(BF)
        dga_ref[...] = (dm * pa_ref[...].astype(F32) * ga * (1.0 - ga)).astype(BF)
        dgb_ref[...] = (dm * pb_ref[...].astype(F32) * gb * (1.0 - gb)).astype(BF)

    bf_out = jax.ShapeDtypeStruct((S, D_MODEL), BF)
    return pl.pallas_call(
        kern, name="merge_bwd", grid=(S // tm,),
        in_specs=[_row_spec(tm), _row_spec(tm, col=0), _row_spec(tm, col=1), _row_spec(tm), _row_spec(tm)],
        out_specs=[_row_spec(tm)] * 4,
        out_shape=[bf_out] * 4,
        compiler_params=_cparams("parallel"),
    )(d_merged, pg, pg, pa, pb)


def _swiglu_fwd(gu):
    S = gu.shape[0]
    tm = _tile(S, ROWS, 8)
    tc = _tile(D_FF, 1408)
    nc = D_FF // tc

    def kern(g_ref, u_ref, o_ref):
        g = g_ref[...].astype(F32)
        o_ref[...] = (g * jax.nn.sigmoid(g) * u_ref[...].astype(F32)).astype(BF)

    return pl.pallas_call(
        kern, name="swiglu_fwd", grid=(S // tm, nc),
        in_specs=[pl.BlockSpec((tm, tc), lambda i, j: (i, j)),
                  pl.BlockSpec((tm, tc), lambda i, j: (i, j + nc))],
        out_specs=pl.BlockSpec((tm, tc), lambda i, j: (i, j)),
        out_shape=jax.ShapeDtypeStruct((S, D_FF), BF),
        compiler_params=_cparams("parallel", "parallel"),
    )(gu, gu)


def _swiglu_bwd(d_act, gu):
    S = gu.shape[0]
    tm = _tile(S, ROWS // 2, 8)

    def kern(da_ref, g_ref, u_ref, o_ref):
        g = g_ref[...].astype(F32)
        u = u_ref[...].astype(F32)
        da = da_ref[...].astype(F32)
        sg = jax.nn.sigmoid(g)
        o_ref[:, :D_FF] = (da * u * (sg * (1.0 + g * (1.0 - sg)))).astype(BF)
        o_ref[:, D_FF:] = (da * (g * sg)).astype(BF)

    return pl.pallas_call(
        kern, name="swiglu_bwd", grid=(S // tm,),
        in_specs=[_row_spec(tm, D_FF), _row_spec(tm, D_FF, 0), _row_spec(tm, D_FF, 1)],
        out_specs=_row_spec(tm, 2 * D_FF),
        out_shape=jax.ShapeDtypeStruct((S, 2 * D_FF), BF),
        compiler_params=_cparams("parallel"),
    )(d_act, gu, gu)


def _split3(x):
    hi = x.astype(BF)
    r1 = x - hi.astype(F32)
    mid = r1.astype(BF)
    lo = (r1 - mid.astype(F32)).astype(BF)
    return hi, mid, lo


def _tri_dot(tri, x):
    return sum(jnp.dot(tri, part, preferred_element_type=F32) for part in _split3(x))


def _log_sigmoid(z):
    return jnp.minimum(z, 0.0) - jnp.log(1.0 + jnp.exp(-jnp.abs(z)))


def _fox_gate_fwd(pa, b_f_pad):
    S = pa.shape[0]
    T = _tile(S, 512, 8)
    f_col = OFF_F // LANES

    def kern(z_ref, b_ref, cum_ref, carry_ref):
        @pl.when(pl.program_id(0) == 0)
        def _():
            carry_ref[...] = jnp.zeros_like(carry_ref)

        log_f = _log_sigmoid(z_ref[...] + b_ref[...])
        row = lax.broadcasted_iota(jnp.int32, (T, T), 0)
        col = lax.broadcasted_iota(jnp.int32, (T, T), 1)
        tri = (col <= row).astype(BF)
        cum = _tri_dot(tri, log_f) + carry_ref[...]
        cum_ref[...] = cum
        carry_ref[...] = cum[T - 1:T, :]

    return pl.pallas_call(
        kern, name="fox_gate_fwd", grid=(S // T,),
        in_specs=[_row_spec(T, LANES, f_col), _vec_spec(LANES)],
        out_specs=_row_spec(T, LANES),
        out_shape=jax.ShapeDtypeStruct((S, LANES), F32),
        scratch_shapes=[pltpu.VMEM((1, LANES), F32)],
        compiler_params=_cparams("arbitrary"),
    )(pa, b_f_pad)


def _fox_gate_bwd(rowsum_ds, colsum_ds, pa, b_f_pad):
    S = pa.shape[0]
    T = _tile(S, 512, 8)
    nb = S // T
    f_col = OFF_F // LANES

    def kern(dr_ref, dc_ref, z_ref, b_ref, df_ref, dbf_ref, carry_ref):
        @pl.when(pl.program_id(0) == 0)
        def _():
            carry_ref[...] = jnp.zeros_like(carry_ref)
            dbf_ref[...] = jnp.zeros_like(dbf_ref)

        row = lax.broadcasted_iota(jnp.int32, (T, T), 0)
        col = lax.broadcasted_iota(jnp.int32, (T, T), 1)
        tri = (col >= row).astype(BF)
        rev = _tri_dot(tri, dr_ref[...] - dc_ref[...]) + carry_ref[...]
        carry_ref[...] = rev[0:1, :]
        z = z_ref[...] + b_ref[...]
        lane = lax.broadcasted_iota(jnp.int32, (T, LANES), 1)
        d_z = jnp.where(lane < B_HEADS, rev * jax.nn.sigmoid(-z), 0.0)
        df_ref[...] = d_z.astype(BF)
        dbf_ref[0:1, :] += _colsum(d_z)

    return pl.pallas_call(
        kern, name="fox_gate_bwd", grid=(nb,),
        in_specs=[pl.BlockSpec((T, LANES), lambda i: (nb - 1 - i, 0)),
                  pl.BlockSpec((T, LANES), lambda i: (nb - 1 - i, 0)),
                  pl.BlockSpec((T, LANES), lambda i: (nb - 1 - i, f_col)),
                  _vec_spec(LANES)],
        out_specs=[pl.BlockSpec((T, LANES), lambda i: (nb - 1 - i, 0)),
                   pl.BlockSpec((8, LANES), lambda i: (0, 0))],
        out_shape=[jax.ShapeDtypeStruct((S, LANES), BF), jax.ShapeDtypeStruct((8, LANES), F32)],
        scratch_shapes=[pltpu.VMEM((1, LANES), F32)],
        compiler_params=_cparams("arbitrary"),
    )(rowsum_ds, colsum_ds, pa, b_f_pad)


NEG_INF = float("-inf")
QK_SCALE = 1.0 / math.sqrt(HEAD_DIM)


def _half_mask(shape, half):
    lane = lax.broadcasted_iota(jnp.int32, shape, 1)
    return (lane < HEAD_DIM) if half == 0 else (lane >= HEAD_DIM)


def _bias_block(shape, terms, term_off, ones_lo, ones_hi):
    l64 = lax.broadcasted_iota(jnp.int32, shape, 1) & (HEAD_DIM - 1)
    out = jnp.where((l64 >= ones_lo) & (l64 < ones_hi), 1.0, 0.0)
    for t, term in enumerate(terms):
        out = jnp.where(l64 == term_off + t, term.astype(F32), out)
    return out


def _head_column(block, head):
    lane = lax.broadcasted_iota(jnp.int32, block.shape, 1)
    return jnp.sum(jnp.where(lane == head, block, 0.0), axis=1, keepdims=True)


def _crossed(shape, first, second):
    return jnp.where(_half_mask(shape, 0), second, first)


def _fox_prep_fwd(cum, T):
    S = cum.shape[0]
    shape = (T, LANES)

    def kern(c_ref, bq_ref, bk_ref):
        p_id = pl.program_id(0)
        cum_blk = c_ref[...]
        c3 = _split3(_crossed(shape, _head_column(cum_blk, 2 * p_id), _head_column(cum_blk, 2 * p_id + 1)))
        bq_ref[...] = _bias_block(shape, c3, 0, 3, 6).astype(BF)
        bk_ref[...] = _bias_block(shape, [-t.astype(F32) for t in c3], 3, 0, 3).astype(BF)

    out_spec = pl.BlockSpec((None, T, LANES), lambda p, i: (p, i, 0))
    out_shape = jax.ShapeDtypeStruct((B_HEADS // 2, S, LANES), BF)
    return pl.pallas_call(
        kern, name="fox_prep_fwd", grid=(B_HEADS // 2, S // T),
        in_specs=[pl.BlockSpec((T, LANES), lambda p, i: (i, 0))],
        out_specs=[out_spec, out_spec], out_shape=[out_shape, out_shape],
        compiler_params=_cparams("parallel", "parallel"),
    )(cum)


def _fox_fwd(p_b, bq, bk, T, comm=None):
    S = p_b.shape[0]
    nq = S // T
    n_pairs = B_HEADS // 2
    grid = (n_pairs, nq)

    def kern(*refs):
        (q_ref, k_ref, v_ref, bq_ref, bk_ref, o_ref, lse_ref), comm_refs = _own_refs(refs, comm, 5, 2, 0)
        _comm_edge(comm, comm_refs, grid, first=True)
        i = pl.program_id(1)
        rowcol = lax.broadcasted_iota(jnp.int32, (T, T), 0) - lax.broadcasted_iota(jnp.int32, (T, T), 1)
        hms = (_half_mask((T, LANES), 0), _half_mask((T, LANES), 1))
        q_scaled = (q_ref[...].astype(F32) * QK_SCALE).astype(BF)
        bq_blk = bq_ref[...]
        qs = [jnp.where(hms[h], q_scaled, bq_blk) for h in (0, 1)]

        def step(j, carry, masked):
            rows = pl.ds(pl.multiple_of(j * T, T), T)
            kj, bkj, vj = k_ref[rows, :], bk_ref[rows, :], v_ref[rows, :]
            new = []
            for half in (0, 1):
                m, l, acc = carry[half]
                s = lax.dot_general(qs[half], jnp.where(hms[half], kj, bkj), (((1,), (1,)), ((), ())),
                                    preferred_element_type=F32)
                if masked:
                    s = jnp.where(rowcol >= 0, s, NEG_INF)
                m_new = jnp.maximum(m, jnp.max(s, axis=1, keepdims=True))
                alpha = jnp.exp(m - m_new)
                p = jnp.exp(s - m_new)
                l_new = alpha * l + jnp.sum(p, axis=1, keepdims=True)
                acc_new = alpha * acc + jnp.dot(p.astype(BF), vj, preferred_element_type=F32)
                new.append((m_new, l_new, acc_new))
            return tuple(new)

        one = (jnp.full((T, 1), NEG_INF, F32), jnp.zeros((T, 1), F32), jnp.zeros((T, LANES), F32))
        carry = lax.fori_loop(0, i, functools.partial(step, masked=False), (one, one))
        (m0, l0, acc0), (m1, l1, acc1) = step(i, carry, True)
        hm0 = _half_mask((T, LANES), 0)
        o_ref[...] = jnp.where(hm0, acc0 / l0, acc1 / l1)
        lse_ref[...] = jnp.where(hm0, m0 + jnp.log(l0), m1 + jnp.log(l1))
        _comm_edge(comm, comm_refs, grid, first=False)

    out_spec = pl.BlockSpec((T, LANES), lambda p, i: (i, p))
    res = pl.pallas_call(
        kern, name="fox_fwd", grid=grid,
        in_specs=[pl.BlockSpec((T, LANES), lambda p, i: (i, OFF_QB // LANES + p)),
                  pl.BlockSpec((S, LANES), lambda p, i: (0, OFF_KB // LANES + p)),
                  pl.BlockSpec((S, LANES), lambda p, i: (0, OFF_VB // LANES + p)),
                  pl.BlockSpec((None, T, LANES), lambda p, i: (p, i, 0)),
                  pl.BlockSpec((None, S, LANES), lambda p, i: (p, 0, 0))] + _comm_specs(comm, "in"),
        out_specs=[out_spec, out_spec] + _comm_specs(comm, "out"),
        out_shape=[jax.ShapeDtypeStruct((S, n_pairs * LANES), F32)] * 2 + (comm.out_shapes if comm else []),
        scratch_shapes=comm.sem_shapes if comm else [],
        compiler_params=_cparams("arbitrary", "arbitrary"),
    )(p_b, p_b, p_b, bq, bk, *(comm.ins if comm else []))
    return res[0], res[1], res[2:]


def _fox_prep_bwd(cum, o, do, lse, T):
    S = o.shape[0]
    shape = (T, LANES)

    def kern(c_ref, o_ref, do_ref, lse_ref, bq_ref, bdo_ref):
        p_id = pl.program_id(0)
        cum_blk = c_ref[...]
        cq = _crossed(shape, _head_column(cum_blk, 2 * p_id), _head_column(cum_blk, 2 * p_id + 1))
        b3 = _split3(cq - pltpu.roll(lse_ref[...], HEAD_DIM, 1))
        bq_ref[...] = _bias_block(shape, b3, 0, 3, 6).astype(BF)
        dd = do_ref[...] * o_ref[...]
        delta = [jnp.sum(jnp.where(_half_mask(shape, h), dd, 0.0), axis=1, keepdims=True) for h in (0, 1)]
        d3 = _split3(-_crossed(shape, delta[0], delta[1]))
        bdo_ref[...] = _bias_block(shape, d3, 0, 0, 0).astype(BF)

    block = pl.BlockSpec((None, T, LANES), lambda p, i: (p, i, 0))
    tile = pl.BlockSpec((T, LANES), lambda p, i: (i, p))
    out_shape = jax.ShapeDtypeStruct((B_HEADS // 2, S, LANES), BF)
    return pl.pallas_call(
        kern, name="fox_prep_bwd", grid=(B_HEADS // 2, S // T),
        in_specs=[pl.BlockSpec((T, LANES), lambda p, i: (i, 0)), tile, tile, tile],
        out_specs=[block, block], out_shape=[out_shape, out_shape],
        compiler_params=_cparams("parallel", "parallel"),
    )(cum, o, do, lse)


def _fox_bwd(p_b, do, bq, bk, bdo, T, comm=None):
    S = p_b.shape[0]
    n_pairs = B_HEADS // 2
    nq = S // T
    grid = (n_pairs,)

    def kern(*refs):
        own, comm_refs = _own_refs(refs, comm, 7, 5, 0)
        q_ref, k_ref, v_ref, do_ref, bq_ref, bk_ref, bdo_ref, dq_ref, dk_ref, dv_ref, dck_ref, dcq_ref = own
        _comm_edge(comm, comm_refs, grid, first=True)
        p_id = pl.program_id(0)
        rowcol = lax.broadcasted_iota(jnp.int32, (T, T), 0) - lax.broadcasted_iota(jnp.int32, (T, T), 1)
        lane = lax.broadcasted_iota(jnp.int32, (T, LANES), 1)
        dk_ref[...] = jnp.zeros_like(dk_ref)
        dv_ref[...] = jnp.zeros_like(dv_ref)
        dck_ref[...] = jnp.zeros_like(dck_ref)

        @pl.when(p_id == 0)
        def _():
            dcq_ref[...] = jnp.zeros_like(dcq_ref)

        hms = (_half_mask((T, LANES), 0), _half_mask((T, LANES), 1))
        v_ones = _bias_block((T, LANES), [], 0, 0, 3).astype(BF)

        def outer(i, carry):
            qrows = pl.ds(pl.multiple_of(i * T, T), T)
            q_scaled = (q_ref[qrows, :].astype(F32) * QK_SCALE).astype(BF)
            do_b = do_ref[qrows, :].astype(BF)
            bq_i, bdo_i = bq_ref[qrows, :], bdo_ref[qrows, :]
            qa = [jnp.where(hms[h], q_scaled, bq_i) for h in (0, 1)]
            doa = [jnp.where(hms[h], do_b, bdo_i) for h in (0, 1)]
            q_own = [jnp.where(hms[h], q_scaled, 0) for h in (0, 1)]
            do_own = [jnp.where(hms[h], do_b, 0) for h in (0, 1)]

            def inner(j, carry_in, masked):
                krows = pl.ds(pl.multiple_of(j * T, T), T)
                kj, bkj, vj = k_ref[krows, :], bk_ref[krows, :], v_ref[krows, :]
                dv_add, dk_add, new = 0.0, 0.0, []
                for half in (0, 1):
                    dq, rs = carry_in[half]
                    ka = jnp.where(hms[half], kj, bkj)
                    s = lax.dot_general(qa[half], ka, (((1,), (1,)), ((), ())), preferred_element_type=F32)
                    if masked:
                        s = jnp.where(rowcol >= 0, s, NEG_INF)
                    p = jnp.exp(s)
                    ds = p * lax.dot_general(doa[half], jnp.where(hms[half], vj, v_ones),
                                             (((1,), (1,)), ((), ())), preferred_element_type=F32)
                    ds_b = ds.astype(BF)
                    dv_add = dv_add + lax.dot_general(p.astype(BF), do_own[half], (((0,), (0,)), ((), ())),
                                                      preferred_element_type=F32)
                    dk_add = dk_add + lax.dot_general(ds_b, q_own[half], (((0,), (0,)), ((), ())),
                                                      preferred_element_type=F32)
                    dck_ref[half:half + 1, krows] += jnp.sum(ds, axis=0, keepdims=True)
                    new.append((dq + jnp.dot(ds_b, jnp.where(hms[half], kj, 0), preferred_element_type=F32),
                                rs + jnp.sum(ds, axis=1, keepdims=True)))
                dv_ref[krows, :] += dv_add
                dk_ref[krows, :] += dk_add
                return tuple(new)

            one = (jnp.zeros((T, LANES), F32), jnp.zeros((T, 1), F32))
            carry_in = lax.fori_loop(0, i, functools.partial(inner, masked=False), (one, one))
            (dq0, rs0), (dq1, rs1) = inner(i, carry_in, True)
            dq_ref[qrows, :] = (dq0 + dq1) * QK_SCALE
            dcq_ref[qrows, :] = jnp.where(lane == 2 * p_id, rs0, jnp.where(lane == 2 * p_id + 1, rs1,
                                                                             dcq_ref[qrows, :]))
            return carry

        lax.fori_loop(0, nq, outer, 0)
        _comm_edge(comm, comm_refs, grid, first=False)

    block = pl.BlockSpec((None, S, LANES), lambda p: (p, 0, 0))
    pair = pl.BlockSpec((S, LANES), lambda p: (0, p))
    slab = lambda off: pl.BlockSpec((S, LANES), lambda p: (0, off // LANES + p))
    wide = jax.ShapeDtypeStruct((S, n_pairs * LANES), F32)
    res = pl.pallas_call(
        kern, name="fox_bwd", grid=grid,
        in_specs=[slab(OFF_QB), slab(OFF_KB), slab(OFF_VB), pair, block, block, block]
        + _comm_specs(comm, "in"),
        out_specs=[pair, pair, pair, pl.BlockSpec((None, 2, S), lambda p: (p, 0, 0)),
                   pl.BlockSpec((S, LANES), lambda p: (0, 0))] + _comm_specs(comm, "out"),
        out_shape=[wide, wide, wide, jax.ShapeDtypeStruct((n_pairs, 2, S), F32),
                   jax.ShapeDtypeStruct((S, LANES), F32)] + (comm.out_shapes if comm else []),
        scratch_shapes=comm.sem_shapes if comm else [],
        compiler_params=_cparams("arbitrary"),
    )(p_b, p_b, p_b, do, bq, bk, bdo, *(comm.ins if comm else []))
    return (*res[:5], res[5:])


SWA_TQ = 128
SWA_SUB = 16


def _swa_window(i, tq):
    start = pl.multiple_of(jnp.maximum(i * tq - WINDOW, 0), LANES)
    return start, i * tq - start


def _swa_valid(offset, tq):
    rel = offset + lax.broadcasted_iota(jnp.int32, (tq, tq + WINDOW), 0) \
        - lax.broadcasted_iota(jnp.int32, (tq, tq + WINDOW), 1)
    return (rel >= 0) & (rel < WINDOW)


def _swa_fwd(qk, v_arr, v_col, sinks):
    S = qk.shape[0]
    tq = min(SWA_TQ, S - WINDOW)
    sub = min(SWA_SUB, S // tq)
    win = tq + WINDOW

    def kern(q_ref, k_ref, v_ref, sink_ref, o_ref, lse_ref):
        p_id, i = pl.program_id(0), pl.program_id(1)
        hm0 = _half_mask((tq, LANES), 0)
        for t in range(sub):
            rows = slice(t * tq, (t + 1) * tq)
            start, offset = _swa_window(i * sub + t, tq)
            kw = k_ref[pl.ds(start, win), :]
            vw = v_ref[pl.ds(start, win), :].astype(BF)
            valid = _swa_valid(offset, tq)
            q = q_ref[rows, :]
            outs, lses = [], []
            for half in (0, 1):
                hm = _half_mask((tq, LANES), half)
                qh = (jnp.where(hm, q, 0).astype(F32) * QK_SCALE).astype(BF)
                s = lax.dot_general(qh, kw, (((1,), (1,)), ((), ())), preferred_element_type=F32)
                s = jnp.where(valid, s, NEG_INF)
                sink = sink_ref[2 * p_id + half]
                m = jnp.maximum(jnp.max(s, axis=1, keepdims=True), sink)
                p = jnp.exp(s - m)
                denom = jnp.sum(p, axis=1, keepdims=True) + jnp.exp(sink - m)
                outs.append(jnp.dot(p.astype(BF), vw, preferred_element_type=F32) / denom)
                lses.append(m + jnp.log(denom))
            o_ref[rows, :] = jnp.where(hm0, outs[0], outs[1])
            lse_ref[rows, :] = jnp.where(hm0, lses[0], lses[1])

    tile = pl.BlockSpec((sub * tq, LANES), lambda p, i: (i, p))
    return pl.pallas_call(
        kern, name="swa_fwd", grid=(A_Q_HEADS // 2, S // (sub * tq)),
        in_specs=[tile, pl.BlockSpec((S, LANES), lambda p, i: (0, A_Q_HEADS // 2)),
                  pl.BlockSpec((S, LANES), lambda p, i: (0, v_col)),
                  pl.BlockSpec(memory_space=pltpu.SMEM)],
        out_specs=[tile, tile],
        out_shape=[jax.ShapeDtypeStruct((S, A_Q_HEADS * HEAD_DIM), F32)] * 2,
        compiler_params=_cparams("parallel", "arbitrary"),
    )(qk, qk, v_arr, sinks)


def _swa_bwd(qk, v_arr, v_col, o_arr, do_arr, lse_arr, sinks, comm=None):
    S = qk.shape[0]
    tq = min(SWA_TQ, S - WINDOW)
    sub = min(SWA_SUB, S // tq)
    win = tq + WINDOW
    n_pairs = A_Q_HEADS // 2
    grid = (n_pairs, S // (sub * tq))

    def kern(*refs):
        own, comm_refs = _own_refs(refs, comm, 7, 4, 0)
        q_ref, k_ref, v_ref, o_ref, do_ref, lse_ref, sink_ref, dq_ref, dk_ref, dv_ref, dsink_ref = own
        _comm_edge(comm, comm_refs, grid, first=True)
        p_id, i = pl.program_id(0), pl.program_id(1)

        @pl.when((p_id == 0) & (i == 0))
        def _():
            dk_ref[...] = jnp.zeros_like(dk_ref)
            dv_ref[...] = jnp.zeros_like(dv_ref)

        @pl.when(i == 0)
        def _():
            dsink_ref[...] = jnp.zeros_like(dsink_ref)

        for t in range(sub):
            rows = slice(t * tq, (t + 1) * tq)
            start, offset = _swa_window(i * sub + t, tq)
            wrows = pl.ds(start, win)
            kw = k_ref[wrows, :]
            vw = v_ref[wrows, :].astype(BF)
            valid = _swa_valid(offset, tq)
            q, do, o, lse2 = q_ref[rows, :], do_ref[rows, :], o_ref[rows, :], lse_ref[rows, :]
            dq = jnp.zeros((tq, LANES), F32)
            dk = jnp.zeros((win, LANES), F32)
            dv = jnp.zeros((win, LANES), F32)
            for half in (0, 1):
                hm = _half_mask((tq, LANES), half)
                lane0 = half * HEAD_DIM
                qh = (jnp.where(hm, q, 0).astype(F32) * QK_SCALE).astype(BF)
                do_f = jnp.where(hm, do, 0.0)
                doh = do_f.astype(BF)
                delta = jnp.sum(do_f * o, axis=1, keepdims=True)
                lse = lse2[:, lane0:lane0 + 1]
                s = lax.dot_general(qh, kw, (((1,), (1,)), ((), ())), preferred_element_type=F32)
                p = jnp.exp(jnp.where(valid, s, NEG_INF) - lse)
                dp = lax.dot_general(doh, vw, (((1,), (1,)), ((), ())), preferred_element_type=F32)
                ds_b = (p * (dp - delta)).astype(BF)
                dv = dv + lax.dot_general(p.astype(BF), doh, (((0,), (0,)), ((), ())),
                                          preferred_element_type=F32)
                dk = dk + lax.dot_general(ds_b, qh, (((0,), (0,)), ((), ())), preferred_element_type=F32)
                kh = jnp.where(_half_mask((win, LANES), half), kw, 0)
                dq = dq + jnp.dot(ds_b, kh, preferred_element_type=F32)
                p_sink = jnp.exp(sink_ref[2 * p_id + half] - lse)
                dsink_ref[0, half:half + 1, :] += jnp.broadcast_to(
                    -jnp.sum(p_sink * delta, axis=0, keepdims=True), (1, LANES))
            dq_ref[rows, :] = dq * QK_SCALE
            dk_ref[wrows, :] += dk
            dv_ref[wrows, :] += dv
        _comm_edge(comm, comm_refs, grid, first=False)

    tile = pl.BlockSpec((sub * tq, LANES), lambda p, i: (i, p))
    whole = lambda col: pl.BlockSpec((S, LANES), lambda p, i: (0, col))
    res = pl.pallas_call(
        kern, name="swa_bwd", grid=grid,
        in_specs=[tile, whole(n_pairs), whole(v_col), tile, tile, tile,
                  pl.BlockSpec(memory_space=pltpu.SMEM)] + _comm_specs(comm, "in"),
        out_specs=[tile, whole(0), whole(0),
                   pl.BlockSpec((1, 8, LANES), lambda p, i: (p, 0, 0))] + _comm_specs(comm, "out"),
        out_shape=[jax.ShapeDtypeStruct((S, A_Q_HEADS * HEAD_DIM), F32),
                   jax.ShapeDtypeStruct((S, LANES), F32), jax.ShapeDtypeStruct((S, LANES), F32),
                   jax.ShapeDtypeStruct((n_pairs, 8, LANES), F32)] + (comm.out_shapes if comm else []),
        scratch_shapes=comm.sem_shapes if comm else [],
        compiler_params=_cparams("arbitrary", "arbitrary"),
    )(qk, qk, v_arr, o_arr, do_arr, lse_arr, sinks, *(comm.ins if comm else []))
    return (*res[:4], res[4:])


ADAMW_BLOCK = 512 * 1024


def _adamw(w, g, m, v, name, comm=None):
    R, C = w.shape
    tr, tc = _tile(R, max(8, ADAMW_BLOCK // C), 8), C
    grid = (R // tr, C // tc)

    def kern(*refs):
        (w_ref, g_ref, m_ref, v_ref, d_ref, mo_ref, vo_ref), comm_refs = _own_refs(refs, comm, 4, 3, 0)
        _comm_edge(comm, comm_refs, grid, first=True)
        g_ = g_ref[...]
        m_new = ADAM_B1 * m_ref[...] + (1.0 - ADAM_B1) * g_
        v_new = ADAM_B2 * v_ref[...] + (1.0 - ADAM_B2) * (g_ * g_)
        m_hat = m_new / (1.0 - ADAM_B1 ** ADAM_STEP)
        v_hat = v_new / (1.0 - ADAM_B2 ** ADAM_STEP)
        d_ref[...] = -ADAM_LR * (m_hat / (jnp.sqrt(v_hat) + ADAM_EPS) + ADAM_WD * w_ref[...])
        mo_ref[...] = m_new
        vo_ref[...] = v_new
        _comm_edge(comm, comm_refs, grid, first=False)

    spec = pl.BlockSpec((tr, tc), lambda i, j: (i, j))
    shape = jax.ShapeDtypeStruct((R, C), F32)
    res = pl.pallas_call(
        kern, name=name, grid=grid,
        in_specs=[spec] * 4 + _comm_specs(comm, "in"),
        out_specs=[spec] * 3 + _comm_specs(comm, "out"),
        out_shape=[shape] * 3 + (comm.out_shapes if comm else []),
        scratch_shapes=comm.sem_shapes if comm else [],
        input_output_aliases={4 + i: 3 + o for i, o in comm.aliases.items()} if comm else {},
        compiler_params=_cparams("arbitrary", "arbitrary"),
    )(w, g, m, v, *(comm.ins if comm else []))
    return (res[:3], res[3:]) if comm else res


def _index_operand(i):
    return jnp.reshape(i, (1,)).astype(jnp.int32)


def _add_pair(whole, got, ci, name):
    P, R, C = whole.shape
    half = R // 2
    tr = _tile(half, ROWS, 16)
    nb = half // tr

    def kern(ci_ref, a_ref, b_ref, o_ref, ob_ref):
        s = a_ref[...] + b_ref[...].astype(F32)
        o_ref[...] = s
        ob_ref[...] = s.astype(BF)

    spec = pl.BlockSpec((None, tr, C), lambda p, i, ci_ref: (p, i, 0))
    return pl.pallas_call(
        kern, name=name,
        grid_spec=pltpu.PrefetchScalarGridSpec(
            num_scalar_prefetch=1, grid=(P, nb),
            in_specs=[pl.BlockSpec((None, tr, C), lambda p, i, ci_ref: (p, ci_ref[0] * nb + i, 0)), spec],
            out_specs=[spec, spec]),
        out_shape=[jax.ShapeDtypeStruct((P, half, C), F32), jax.ShapeDtypeStruct((P, half, C), BF)],
        compiler_params=_cparams("parallel", "parallel"),
    )(_index_operand(ci), whole, got)


def _add_three(parts, recv, chip, name):
    _, R, C = parts.shape
    tr = _tile(R, ROWS, 16)

    def kern(chip_ref, o_ref, r0_ref, r1_ref, r2_ref, out_ref):
        s = ((o_ref[...] + r0_ref[...].astype(F32)) + r1_ref[...].astype(F32)) + r2_ref[...].astype(F32)
        out_ref[0] = s
        out_ref[1] = s

    slab = lambda k: pl.BlockSpec((None, tr, C), lambda i, chip_ref: (k, i, 0))
    return pl.pallas_call(
        kern, name=name,
        grid_spec=pltpu.PrefetchScalarGridSpec(
            num_scalar_prefetch=1, grid=(R // tr,),
            in_specs=[pl.BlockSpec((None, tr, C), lambda i, chip_ref: (chip_ref[0], i, 0)),
                      slab(0), slab(1), slab(2)],
            out_specs=pl.BlockSpec((2, tr, C), lambda i, chip_ref: (0, i, 0))),
        out_shape=jax.ShapeDtypeStruct((2, R, C), F32),
        compiler_params=_cparams("parallel"),
    )(_index_operand(chip), parts, recv, recv, recv)


SM_ADA, SM_G, SM_LOSS, SM_BF, SM_SINK, SM_LEN = 0, 6144, 10240, 11264, 11272, 12288


def _small_finalize(gathered):
    def kern(g_ref, tot_ref, loss_ref):
        tot = g_ref[0:1, :]
        for b in range(1, N_DEV):
            tot = tot + g_ref[b:b + 1, :]
        tot_ref[...] = tot
        sq = jnp.sum(tot[:, SM_LOSS:SM_LOSS + D_MODEL], axis=1, keepdims=True)
        loss_ref[...] = jnp.broadcast_to(sq * (0.5 / D_MODEL), (1, LANES))

    full = lambda shape: pl.BlockSpec(shape, lambda i: (0, 0))
    return pl.pallas_call(
        kern, name="small_finalize", grid=(1,),
        in_specs=[full((N_DEV, SM_LEN))],
        out_specs=[full((1, SM_LEN)), full((1, LANES))],
        out_shape=[jax.ShapeDtypeStruct((1, SM_LEN), F32), jax.ShapeDtypeStruct((1, LANES), F32)],
        compiler_params=_cparams("arbitrary"),
    )(gathered)


def _ada_dw(c_t, d_ada):
    N = d_ada.shape[1]
    tn = _tile(N, 512)

    def kern(c_ref, d_ref, o_ref):
        acc = c_ref[:, 0:1] * d_ref[0:1, :]
        for b in range(1, N_DEV):
            acc = acc + c_ref[:, b:b + 1] * d_ref[b:b + 1, :]
        o_ref[...] = acc

    return pl.pallas_call(
        kern, name="ada_dw", grid=(N // tn,),
        in_specs=[pl.BlockSpec((D_MODEL, N_DEV), lambda j: (0, 0)), pl.BlockSpec((N_DEV, tn), lambda j: (0, j))],
        out_specs=pl.BlockSpec((D_MODEL, tn), lambda j: (0, j)),
        out_shape=jax.ShapeDtypeStruct((D_MODEL, N), F32),
        compiler_params=_cparams("parallel"),
    )(c_t, d_ada)


def _here():
    return lax.axis_index("x"), lax.axis_index("y"), lax.axis_index("c")


def _other_chips(x, y):
    return [(1 - x, y), (x, 1 - y), (1 - x, 1 - y)]


_ANY = pl.BlockSpec(memory_space=pl.ANY)


class _Comm:
    def __init__(self, ins, out_shapes, sem_shapes, start, finish, aliases=None):
        self.ins, self.out_shapes, self.sem_shapes = list(ins), list(out_shapes), list(sem_shapes)
        self.start, self.finish = start, finish
        self.aliases = dict(aliases or {})

    def split(self, refs, n_in, n_out, n_scratch):
        a = n_in + len(self.ins)
        b = a + n_out + len(self.out_shapes)
        own = list(refs[:n_in]) + list(refs[a:a + n_out]) + list(refs[b:b + n_scratch])
        mine = (refs[n_in:a], refs[a + n_out:b], refs[b + n_scratch:])
        return own, mine


def _run_comm(comm, name):
    n_in, n_out = len(comm.ins), len(comm.out_shapes)

    def body(*refs):
        parts = (refs[:n_in], refs[n_in:n_in + n_out], refs[n_in + n_out:])
        comm.start(*parts)
        comm.finish(*parts)

    return pl.pallas_call(
        body, name=name,
        in_specs=[_ANY] * n_in, out_specs=[_ANY] * n_out,
        out_shape=comm.out_shapes, scratch_shapes=comm.sem_shapes,
        input_output_aliases=comm.aliases,
    )(*comm.ins)


def _gather_comm(blocks):
    L = len(blocks)

    def parts(ins, outs, sems):
        send_sems, recv_sems, local_sems = sems
        x, y, c = _here()
        me, sibling = (x, y, c), (x, y, 1 - c)
        chips = _other_chips(x, y)

        def slot(px, py, pc):
            return 4 * px + 2 * py + pc

        def copy(l, k, block, to, src=None):
            dst = outs[l].at[slot(*block)]
            return pltpu.make_async_remote_copy(
                src_ref=dst if src is None else src, dst_ref=dst,
                send_sem=send_sems.at[l, k], recv_sem=recv_sems.at[l, k],
                device_id=to, device_id_type=MESH)

        mine = [pltpu.make_async_copy(ins[l], outs[l].at[slot(*me)], local_sems.at[l]) for l in range(L)]
        first = []
        for l in range(L):
            first.append(copy(l, 0, me, sibling, src=ins[l]))
            for j, chip in enumerate(chips):
                first.append(copy(l, 1 + j, me, (*chip, c), src=ins[l]))
        return c, me, sibling, chips, copy, mine, first

    def start(ins, outs, sems):
        *_, mine, first = parts(ins, outs, sems)
        for cp in mine + first:
            cp.start()

    def finish(ins, outs, sems):
        c, me, sibling, chips, copy, mine, first = parts(ins, outs, sems)
        passed = []
        for j, chip in enumerate(chips):
            for l in range(L):
                copy(l, 1 + j, (*chip, c), me).wait_recv()
                fwd = copy(l, 4 + j, (*chip, c), sibling)
                fwd.start()
                passed.append(fwd)
        for l in range(L):
            copy(l, 0, sibling, me).wait_recv()
        for j, chip in enumerate(chips):
            for l in range(L):
                copy(l, 4 + j, (*chip, 1 - c), me).wait_recv()
        for cp in first + passed:
            cp.wait_send()
        for cp in mine:
            cp.wait()

    return _Comm(blocks, [jax.ShapeDtypeStruct((N_DEV,) + b.shape, b.dtype) for b in blocks],
                 [pltpu.SemaphoreType.DMA((L, 7)), pltpu.SemaphoreType.DMA((L, 7)), pltpu.SemaphoreType.DMA((L,))],
                 start, finish)


def _allgather8(blocks, name):
    return _run_comm(_gather_comm(blocks), name)


def _swap_comm(arrs):
    L = len(arrs)

    def copies(ins, outs, sems):
        send_sems, recv_sems = sems
        x, y, c = _here()
        cps = []
        for l in range(L):
            half = arrs[l].shape[1] // 2
            rows = pl.ds(pl.multiple_of((1 - c) * half, 16), half)
            cps.append(pltpu.make_async_remote_copy(
                src_ref=ins[l].at[:, rows, :], dst_ref=outs[l], send_sem=send_sems.at[l],
                recv_sem=recv_sems.at[l], device_id=(x, y, 1 - c), device_id_type=MESH))
        return cps

    def start(ins, outs, sems):
        for cp in copies(ins, outs, sems):
            cp.start()

    def finish(ins, outs, sems):
        for cp in copies(ins, outs, sems):
            cp.wait()

    return _Comm(arrs, [jax.ShapeDtypeStruct((a.shape[0], a.shape[1] // 2, a.shape[2]), a.dtype) for a in arrs],
                 [pltpu.SemaphoreType.DMA((L,)), pltpu.SemaphoreType.DMA((L,))], start, finish)


def _join_comm(bufs):
    L = len(bufs)

    def start(ins, outs, sems):
        send_sems, recv_sems = sems
        x, y, c = _here()
        for l in range(L):
            pltpu.make_async_remote_copy(src_ref=outs[l].at[c], dst_ref=outs[l].at[c], send_sem=send_sems.at[l],
                                         recv_sem=recv_sems.at[l], device_id=(x, y, 1 - c),
                                         device_id_type=MESH).start()

    def finish(ins, outs, sems):
        send_sems, recv_sems = sems
        x, y, c = _here()
        for l in range(L):
            pltpu.make_async_remote_copy(src_ref=outs[l].at[c], dst_ref=outs[l].at[1 - c],
                                         send_sem=send_sems.at[l], recv_sem=recv_sems.at[l],
                                         device_id=(x, y, 1 - c), device_id_type=MESH).wait()

    return _Comm(bufs, [jax.ShapeDtypeStruct(a.shape, a.dtype) for a in bufs],
                 [pltpu.SemaphoreType.DMA((L,)), pltpu.SemaphoreType.DMA((L,))], start, finish,
                 aliases={l: l for l in range(L)})


def _scatter_comm(arrs):
    L = len(arrs)

    def copies(ins, outs, sems):
        send_sems, recv_sems = sems
        x, y, c = _here()
        return [pltpu.make_async_remote_copy(
            src_ref=ins[l].at[2 * tx + ty], dst_ref=outs[l].at[j],
            send_sem=send_sems.at[l, j], recv_sem=recv_sems.at[l, j],
            device_id=(tx, ty, c), device_id_type=MESH)
            for l in range(L) for j, (tx, ty) in enumerate(_other_chips(x, y))]

    def start(ins, outs, sems):
        for cp in copies(ins, outs, sems):
            cp.start()

    def finish(ins, outs, sems):
        for cp in copies(ins, outs, sems):
            cp.wait()

    return _Comm(arrs, [jax.ShapeDtypeStruct((3,) + a.shape[1:], a.dtype) for a in arrs],
                 [pltpu.SemaphoreType.DMA((L, 3)), pltpu.SemaphoreType.DMA((L, 3))], start, finish)


_A_ORDER = np.array(A_HEAD_ORDER)
_A_INVERSE = np.argsort(_A_ORDER)


def _permute_in_weights(w_in):
    qa = w_in[:, 0:512].reshape(D_MODEL, A_Q_HEADS, HEAD_DIM)[:, _A_ORDER, :].reshape(D_MODEL, 512)
    f_pad = jnp.pad(w_in[:, 2304:2312], ((0, 0), (0, LANES - B_HEADS)))
    w_a = jnp.concatenate([qa, w_in[:, 512:640], f_pad], axis=1)
    return w_a, w_in[:, 640:2304], w_in[:, 2312:4360]


def _slab_segments():
    segs = [(h * HEAD_DIM, int(_A_INVERSE[h]) * HEAD_DIM, HEAD_DIM) for h in range(A_Q_HEADS)]
    segs += [(512, OFF_KA, 128), (640, W_A + OFF_VA, 128), (768, W_A + OFF_QB, 1536),
             (2304, OFF_F, B_HEADS), (2312, W_A + W_B, W_G)]
    return segs


def _shard_slabs(dw_perm):
    R = dw_perm.shape[0]
    tr = _tile(R, 128, 8)
    plan = []
    for k in range(N_CHIP):
        for b in range(W_SHARD_PAD // LANES):
            lo, hi = k * W_SHARD + b * LANES, min(k * W_SHARD + (b + 1) * LANES, (k + 1) * W_SHARD)
            parts = []
            for o0, s0, n in _slab_segments():
                a, z = max(lo, o0), min(hi, o0 + n)
                while a < z:
                    s = s0 + (a - o0)
                    run = min(z - a, LANES - s % LANES)
                    parts.append((s // LANES, ((a - lo) - s % LANES) % LANES, a - lo, run))
                    a += run
            plan.append((k, b, parts))

    def kern(x_ref, o32_ref, obf_ref):
        lane = lax.broadcasted_iota(jnp.int32, (tr, LANES), 1)
        for k, b, parts in plan:
            acc = jnp.zeros((tr, LANES), F32)
            for src, rot, first, run in parts:
                blk = x_ref[:, src * LANES:(src + 1) * LANES]
                if rot:
                    blk = pltpu.roll(blk, rot, 1)
                acc = jnp.where((lane >= first) & (lane < first + run), blk, acc)
            o32_ref[k, :, b * LANES:(b + 1) * LANES] = acc
            obf_ref[k, :, b * LANES:(b + 1) * LANES] = acc.astype(BF)

    out_spec = pl.BlockSpec((N_CHIP, tr, W_SHARD_PAD), lambda i: (0, i, 0))
    return tuple(pl.pallas_call(
        kern, name="shard_slabs", grid=(R // tr,),
        in_specs=[pl.BlockSpec((tr, W_PERM), lambda i: (i, 0))],
        out_specs=[out_spec, out_spec],
        out_shape=[jax.ShapeDtypeStruct((N_CHIP, R, W_SHARD_PAD), F32),
                   jax.ShapeDtypeStruct((N_CHIP, R, W_SHARD_PAD), BF)],
        compiler_params=_cparams("parallel"),
    )(dw_perm))


class _NoExchange:
    def __init__(self, w_in, rest):
        self.w_in_whole, self.rest, self.grads = w_in, rest, {}

    def w_in_comm(self):
        return None

    def w_in(self, outs):
        return self.w_in_whole

    def rest_weights_comm(self):
        return None

    def rest_weights(self, outs):
        return self.rest

    def swap_comm(self, pieces, tag):
        self.grads[tag] = [p32 for p32, _ in pieces]
        return None

    def swap_done(self, outs, tag):
        return None

    def reduce_done(self, outs, tag):
        pass

    def join_comm(self):
        return None


class _Exchange:
    def __init__(self, ci, chip, w_in_shard, rest_shards):
        self.ci, self.chip, self.w_in_shard, self.rest_shards = ci, chip, w_in_shard, rest_shards
        self.pieces, self.part_f32, self.halves = {}, {}, {}

    def _my_half(self, a, axis=0, other=False):
        rows = a.shape[axis] // 2
        return lax.dynamic_slice_in_dim(a, ((1 - self.ci) if other else self.ci) * rows, rows, axis=axis)

    def w_in_comm(self):
        return _gather_comm([self._my_half(self.w_in_shard).astype(BF)])

    def w_in(self, outs):
        return _col_sharded(outs[0])

    def rest_weights_comm(self):
        return _gather_comm([self._my_half(w).astype(BF) for w in self.rest_shards])

    def rest_weights(self, outs):
        w_ba, w_bb, w_out, w_fi, w_fo = outs
        return (_col_sharded(w_ba), _col_sharded(w_bb), _row_sharded(w_out), _col_sharded(w_fi),
                _row_sharded(w_fo))

    def swap_comm(self, pieces, tag):
        self.pieces[tag] = pieces
        return _swap_comm([pbf for _, pbf in pieces])

    def swap_done(self, got, tag):
        self.part_f32[tag], part_bf = [], []
        for l, ((p32, _), g_) in enumerate(zip(self.pieces[tag], got)):
            s32, sbf = _add_pair(p32, g_, self.ci, f"chip_sum_{tag}_{l}")
            self.part_f32[tag].append(s32)
            part_bf.append(sbf)
        return _scatter_comm(part_bf)

    def reduce_done(self, outs, tag):
        self.halves[tag] = [_add_three(p32, r, self.chip, f"shard_sum_{tag}_{l}")
                            for l, (p32, r) in enumerate(zip(self.part_f32[tag], outs))]

    def join_comm(self):
        return _join_comm(self.halves["late"] + self.halves["early"])


def _col_sharded(g):
    return jnp.transpose(g.reshape(N_CHIP, -1, g.shape[-1]), (1, 0, 2)).reshape(2 * g.shape[1], N_CHIP * g.shape[-1])


def _row_sharded(g):
    return g.reshape(N_DEV * g.shape[1], g.shape[-1])


def _rope_tables(pos):
    inv_freq = 1.0 / (ROPE_THETA ** (jnp.arange(0, HEAD_DIM, 2, dtype=F32) / HEAD_DIM))
    ang = pos.astype(F32)[:, None] * inv_freq
    cos, sin = jnp.cos(ang), jnp.sin(ang)
    return jnp.tile(cos, (1, 4)), jnp.tile(jnp.concatenate([-sin, sin], axis=1), (1, 2))


def _local_step(x, pos, ada, g1, g2, g3, g4, b_f, sinks, exch, target):
    S = x.shape[0]
    t_fox = _tile(S, 512, LANES) if S >= 1024 else S // 2
    t_fox_fwd = _tile(S, 1024, LANES) if S >= 2048 else S // 2
    shift_m, scale_m, gate_m, shift_f, scale_f, gate_f = [ada[i:i + 1] for i in range(N_ADA)]
    cos_t, sin_t = _rope_tables(pos)
    sinks_p = sinks.reshape(A_KV_HEADS, 4).T.reshape(A_Q_HEADS)
    b_f_pad = jnp.pad(b_f, (0, LANES - B_HEADS)).reshape(1, LANES)

    h1, outs = _pre_norm(x, g1, scale_m, shift_m, "pre_mix_norm", comm=exch.w_in_comm())
    w_a, w_b, w_g = _permute_in_weights(exch.w_in(outs))
    w_perm = jnp.concatenate([w_a, w_b, w_g], axis=1)
    p_a = _mm(h1, w_a, "nn", F32, "proj_a")
    p_b = _mm(h1, w_b, "nn", BF, "proj_b")
    p_g = _mm(h1, w_g, "nn", BF, "proj_g")
    (qk_a,) = _rope([p_a], [640], cos_t, sin_t, "rope_fwd")
    o_a, lse_a = _swa_fwd(qk_a, p_b, 0, sinks_p)
    cum = _fox_gate_fwd(p_a, b_f_pad)
    bq, bk = _fox_prep_fwd(cum, t_fox)
    comm = exch.rest_weights_comm()
    o_b, lse_b, outs = _fox_fwd(p_b, bq, bk, t_fox_fwd, comm=comm)
    w_ba, w_bb, w_out, w_fi, w_fo = exch.rest_weights(outs)
    w_ba_p = w_ba.reshape(A_Q_HEADS, HEAD_DIM, D_MODEL)[_A_ORDER].reshape(512, D_MODEL)
    pa = _mm(o_a, w_ba_p, "nn", BF, "branch_a")
    pb = _mm(o_b, w_bb, "nn", BF, "branch_b")
    merged = _merge_fwd(p_g, pa, pb)
    y1 = _mm(merged, w_out, "nn", BF, "out_proj")
    x2, h2 = _post_pre(x, y1, g2, gate_m, g3, scale_f, shift_f)
    gu = _mm(h2, w_fi, "nn", BF, "ffn_in")
    act = _swiglu_fwd(gu)
    y2 = _mm(act, w_fo, "nn", BF, "ffn_out")
    d_out, d_y2, st_f = _final(x2, y2, g4, gate_f, target)

    d_act = _mm(d_y2, w_fo, "nt", BF, "ffn_out_dx")
    row_pieces = lambda pair: tuple(t.reshape(N_CHIP, t.shape[0] // N_CHIP, t.shape[1]) for t in pair)
    dw_fo = row_pieces(_mm(act, d_y2, "tn", F32, "ffn_out_dw", twin=True))
    d_gu = _swiglu_bwd(d_act, gu)
    d_h2 = _mm(d_gu, w_fi, "nt", BF, "ffn_in_dx")
    dw_fi = _mm(h2, d_gu, "tn", F32, "ffn_in_dw", col_pieces=N_CHIP, twin=True)
    d_x2, d_y1, st_m = _mid_bwd(d_h2, x2, d_out, y1, g3, scale_f, g2, gate_m)
    d_merged = _mm(d_y1, w_out, "nt", BF, "out_proj_dx")
    dw_out = row_pieces(_mm(merged, d_y1, "tn", F32, "out_proj_dw", twin=True))
    d_pa, d_pb, d_ga, d_gb = _merge_bwd(d_merged, p_g, pa, pb)
    d_oa = _mm(d_pa, w_ba_p, "nt", F32, "branch_a_dx")
    dw_ba_p = _mm(o_a, d_pa, "tn", F32, "branch_a_dw", col_pieces=N_CHIP, twin=True)
    d_ob = _mm(d_pb, w_bb, "nt", F32, "branch_b_dx")
    dw_bb = _mm(o_b, d_pb, "tn", F32, "branch_b_dw", col_pieces=N_CHIP, twin=True)
    head_rows = lambda t: t.reshape(N_CHIP, A_Q_HEADS, HEAD_DIM, -1)[:, _A_INVERSE].reshape(t.shape)
    dw_ba = tuple(head_rows(t) for t in dw_ba_p)
    comm = exch.swap_comm([dw_ba, dw_bb, dw_out, dw_fi, dw_fo], "early")
    dq_a, dk_a, dv_a, d_sink, outs = _swa_bwd(qk_a, p_b, 0, o_a, d_oa, lse_a, sinks_p, comm=comm)
    comm = exch.swap_done(outs, "early")
    bq_bwd, bdo = _fox_prep_bwd(cum, o_b, d_ob, lse_b, t_fox)
    dq_b, dk_b, dv_b, d_ck, d_cq, outs = _fox_bwd(p_b, d_ob, bq_bwd, bk, bdo, t_fox, comm=comm)
    exch.reduce_done(outs, "early")
    d_qa, d_ka = _rope([dq_a, dk_a], [512, LANES], cos_t, -sin_t, "rope_bwd")
    d_ck_cols = jnp.pad(d_ck.reshape(B_HEADS, S).T, ((0, 0), (0, LANES - B_HEADS)))
    d_f, d_bf = _fox_gate_bwd(d_cq, d_ck_cols, p_a, b_f_pad)
    d_proj = jnp.concatenate([d_qa, d_ka, d_f, dv_a.astype(BF), dq_b.astype(BF), dk_b.astype(BF),
                              dv_b.astype(BF), d_ga, d_gb], axis=1)
    dw_perm = _mm(h1, d_proj, "tn", F32, "proj_dw")
    swap = exch.swap_comm([_shard_slabs(dw_perm)], "late")
    comm = exch.swap_done(_run_comm(swap, "grads_to_sibling_late") if swap else None, "late")
    res = _mm(d_proj, w_perm, "nt", BF, "proj_dx", comm=comm)
    d_h1 = res[0] if comm else res
    exch.reduce_done(res[1] if comm else None, "late")
    grad_x, st_p, outs = _pre_bwd(d_h1, x, d_x2, g1, scale_m, comm=exch.join_comm())
    exch.joined = outs

    d_sinks = d_sink[:, :2, 0].T.reshape(A_Q_HEADS)
    small = jnp.concatenate([
        st_p[0], st_p[1], st_m[3], st_m[0], st_m[1], st_f[0],
        st_p[2], st_m[4], st_m[2], st_f[1],
        st_f[2], d_bf[0, :B_HEADS], d_sinks,
        jnp.zeros((SM_LEN - SM_SINK - A_Q_HEADS,), F32)])
    return grad_x, small


def kernel(x, c, positions, w_ada, b_ada, g_pre_mix, g_post_mix, w_in, b_f, sinks, w_branch_a, w_branch_b, w_out, g_pre_ffn, g_post_ffn, w_ffn_in, w_ffn_out, loss_target, m_w_ada, m_b_ada, m_g_pre_mix, m_g_post_mix, m_w_in, m_b_f, m_sinks, m_w_branch_a, m_w_branch_b, m_w_out, m_g_pre_ffn, m_g_post_ffn, m_w_ffn_in, m_w_ffn_out, v_w_ada, v_b_ada, v_g_pre_mix, v_g_post_mix, v_w_in, v_b_f, v_sinks, v_w_branch_a, v_w_branch_b, v_w_out, v_g_pre_ffn, v_g_post_ffn, v_w_ffn_in, v_w_ffn_out):
    xi, yi, ci = _here()
    chip = 2 * xi + yi
    dev = 2 * chip + ci

    (c_g,) = _allgather8([c.reshape(8, LANES)], "gather_c")
    c_all = c_g.reshape(N_DEV, D_MODEL)
    exch = _Exchange(ci, chip, w_in[0], [w_branch_a[0], w_branch_b[0], w_out[0], w_ffn_in[0], w_ffn_out[0]])

    ada_cols = _mm(c_all, w_ada[0], "nn", F32, "ada_fwd")
    (ada_g,) = _allgather8([ada_cols], "gather_ada")
    ada_mine = lax.dynamic_index_in_dim(ada_g.reshape(N_CHIP, 2, N_DEV, -1)[:, 0], dev, axis=1, keepdims=False)
    ada = (ada_mine.reshape(-1) + b_ada[0]).reshape(N_ADA, D_MODEL)

    grad_x, small = _local_step(
        x[0], positions[0], ada, g_pre_mix, g_post_mix, g_pre_ffn, g_post_ffn, b_f[0], sinks[0],
        exch, loss_target[0])

    g_w_in, g_w_ba, g_w_bb, g_w_out, g_w_fi, g_w_fo = [j.reshape(2 * j.shape[1], j.shape[2]) for j in exch.joined]
    upd_fi, (small_g,) = _adamw(w_ffn_in[0], g_w_fi, m_w_ffn_in[0], v_w_ffn_in[0], "adamw_w_ffn_in",
                                comm=_gather_comm([small.reshape(8, SM_LEN // 8)]))

    small_all = small_g.reshape(N_DEV, SM_LEN)
    small_tot, loss_row = _small_finalize(small_all)
    loss = loss_row[0, 0]
    d_ada_cols = lax.dynamic_slice_in_dim(small_all[:, :N_ADA * D_MODEL], chip * (N_ADA * D_MODEL // N_CHIP),
                                          N_ADA * D_MODEL // N_CHIP, axis=1)
    g_w_ada = _ada_dw(c_all.T, d_ada_cols)

    def small_vec(b_ada_, g1_, g2_, g3_, g4_, b_f_, sinks_):
        return jnp.concatenate([b_ada_[0], g1_[0], g2_[0], g3_[0], g4_[0], jnp.zeros((D_MODEL,), F32),
                                b_f_[0], sinks_[0], jnp.zeros((SM_LEN - SM_SINK - A_Q_HEADS,), F32)]
                               ).reshape(8, SM_LEN // 8)

    sw = small_vec(b_ada, g_pre_mix, g_post_mix, g_pre_ffn, g_post_ffn, b_f, sinks)
    sm = small_vec(m_b_ada, m_g_pre_mix, m_g_post_mix, m_g_pre_ffn, m_g_post_ffn, m_b_f, m_sinks)
    sv = small_vec(v_b_ada, v_g_pre_mix, v_g_post_mix, v_g_pre_ffn, v_g_post_ffn, v_b_f, v_sinks)
    s_upd = [u.reshape(SM_LEN) for u in _adamw(sw, small_tot.reshape(8, SM_LEN // 8), sm, sv, "adamw_small")]
    s_grad = small_tot.reshape(SM_LEN)

    def unpack(vec):
        row = lambda a, n: vec[a:a + n].reshape(1, n)
        return dict(b_ada=row(SM_ADA, N_ADA * D_MODEL), g_pre_mix=row(SM_G, D_MODEL),
                    g_post_mix=row(SM_G + D_MODEL, D_MODEL), g_pre_ffn=row(SM_G + 2 * D_MODEL, D_MODEL),
                    g_post_ffn=row(SM_G + 3 * D_MODEL, D_MODEL), b_f=row(SM_BF, B_HEADS),
                    sinks=row(SM_SINK, A_Q_HEADS))

    big = dict(
        w_ada=(w_ada, g_w_ada, m_w_ada, v_w_ada),
        w_branch_a=(w_branch_a, g_w_ba, m_w_branch_a, v_w_branch_a),
        w_branch_b=(w_branch_b, g_w_bb, m_w_branch_b, v_w_branch_b),
        w_out=(w_out, g_w_out, m_w_out, v_w_out),
        w_ffn_out=(w_ffn_out, g_w_fo, m_w_ffn_out, v_w_ffn_out))
    grads, deltas, new_m, new_v = unpack(s_grad), unpack(s_upd[0]), unpack(s_upd[1]), unpack(s_upd[2])
    grads["w_ffn_in"], deltas["w_ffn_in"], new_m["w_ffn_in"], new_v["w_ffn_in"] = [
        t[None] for t in (g_w_fi, *upd_fi)]
    for n, (w_, g_, m_, v_) in big.items():
        d_, nm_, nv_ = _adamw(w_[0], g_, m_[0], v_[0], "adamw_" + n)
        grads[n], deltas[n], new_m[n], new_v[n] = g_[None], d_[None], nm_[None], nv_[None]
    pad_cols = lambda a: jnp.pad(a, ((0, 0), (0, W_SHARD_PAD - W_SHARD)))
    upd = _adamw(pad_cols(w_in[0]), g_w_in, pad_cols(m_w_in[0]), pad_cols(v_w_in[0]), "adamw_w_in")
    grads["w_in"], deltas["w_in"], new_m["w_in"], new_v["w_in"] = [t[None, :, :W_SHARD] for t in (g_w_in, *upd)]

    names = ["w_ada", "b_ada", "g_pre_mix", "g_post_mix", "w_in", "b_f", "sinks", "w_branch_a", "w_branch_b",
             "w_out", "g_pre_ffn", "g_post_ffn", "w_ffn_in", "w_ffn_out"]
    return (loss, grad_x[None], *[grads[n] for n in names], *[deltas[n] for n in names],
            *[new_m[n] for n in names], *[new_v[n] for n in names])
```

```python
import functools
import math

import numpy as np
import jax
import jax.numpy as jnp
from jax import lax
from jax.experimental import pallas as pl
from jax.experimental.pallas import tpu as pltpu

F32 = jnp.float32
BF = jnp.bfloat16

D_MODEL = 1024
HEAD_DIM = 64
LANES = 128
WINDOW = 128
A_Q_HEADS = 8
A_KV_HEADS = 2
B_HEADS = 8
D_FF = 2816
ROPE_THETA = 10000.0
RMS_EPS = 1e-6
N_ADA = 6
N_DEV = 8
N_CHIP = 4

ADAM_LR = 0.001
ADAM_B1 = 0.9
ADAM_B2 = 0.999
ADAM_EPS = 1e-08
ADAM_WD = 0.01
ADAM_STEP = 10

VMEM_LIMIT = 48 * 1024 * 1024
MESH = pl.DeviceIdType.MESH

A_HEAD_ORDER = (0, 4, 1, 5, 2, 6, 3, 7)

OFF_QA, OFF_KA, OFF_F = 0, 512, 640
W_A = 768
OFF_VA, OFF_QB, OFF_KB, OFF_VB = 0, 128, 640, 1152
W_B = 1664
W_G = 2048
W_PERM = W_A + W_B + W_G
W_SHARD = 1090
W_SHARD_PAD = 1152


def _tile(n, cap, mult=LANES):
    if n <= cap:
        return n
    t = (cap // mult) * mult
    while t >= mult:
        if n % t == 0:
            return t
        t -= mult
    raise ValueError(f"no tile for {n}")


MXU_WIDTH = 256
MM_OPERAND_BYTES = 28 * 1024 * 1024


def _mm_tiles(M, N, K, a_bytes, b_bytes, tm_cap, tn_cap):
    tm = _tile(M, tm_cap)
    try:
        tn = _tile(N, tn_cap, MXU_WIDTH)
    except ValueError:
        tn = _tile(N, tn_cap)
    fits = lambda tk: 2 * tk * (tm * a_bytes + tn * b_bytes) <= MM_OPERAND_BYTES
    tk = K if fits(K) else next(t for t in range(K // LANES * LANES, 0, -LANES) if K % t == 0 and fits(t))
    return tm, tn, tk


def _cparams(*sem):
    return pltpu.CompilerParams(dimension_semantics=sem, vmem_limit_bytes=VMEM_LIMIT)


def _own_refs(refs, comm, n_in, n_out, n_scratch):
    if comm is None:
        return list(refs), None
    return comm.split(refs, n_in, n_out, n_scratch)


def _comm_specs(comm, side):
    if comm is None:
        return []
    return [pl.BlockSpec(memory_space=pl.ANY)] * len(comm.ins if side == "in" else comm.out_shapes)


def _comm_edge(comm, comm_refs, grid, first):
    if comm is None:
        return
    at_edge = None
    for axis, n in enumerate(grid):
        here = pl.program_id(axis) == (0 if first else n - 1)
        at_edge = here if at_edge is None else at_edge & here
    pl.when(at_edge)(lambda: (comm.start if first else comm.finish)(*comm_refs))


def _mm(a, b, mode, out_dtype, name, tm_cap=512, tn_cap=2816, comm=None, col_pieces=1, twin=False):
    if mode == "nn":
        (M, K), (K2, N) = a.shape, b.shape
        dims = (((1,), (0,)), ((), ()))
    elif mode == "nt":
        (M, K), (N, K2) = a.shape, b.shape
        dims = (((1,), (1,)), ((), ()))
    else:
        (K, M), (K2, N) = a.shape, b.shape
        dims = (((0,), (0,)), ((), ()))
    assert K == K2, (a.shape, b.shape, mode)
    tm, tn, tk = _mm_tiles(M, N // col_pieces, K, a.dtype.itemsize, b.dtype.itemsize, tm_cap, tn_cap)
    nk = K // tk
    n_out = 2 if twin else 1
    n_scratch = 1 if nk > 1 else 0
    if mode == "nn":
        a_spec = pl.BlockSpec((tm, tk), lambda i, j, k: (i, k))
        b_spec = pl.BlockSpec((tk, tn), lambda i, j, k: (k, j))
    elif mode == "nt":
        a_spec = pl.BlockSpec((tm, tk), lambda i, j, k: (i, k))
        b_spec = pl.BlockSpec((tn, tk), lambda i, j, k: (j, k))
    else:
        a_spec = pl.BlockSpec((tk, tm), lambda i, j, k: (k, i))
        b_spec = pl.BlockSpec((tk, tn), lambda i, j, k: (k, j))

    grid = (M // tm, N // tn, nk)

    def kern(*refs):
        own, comm_refs = _own_refs(refs, comm, 2, n_out, n_scratch)
        a_ref, b_ref, o_refs = own[0], own[1], own[2:2 + n_out]
        k = pl.program_id(2)
        _comm_edge(comm, comm_refs, grid, first=True)
        part = lax.dot_general(a_ref[...].astype(BF), b_ref[...].astype(BF), dims,
                               preferred_element_type=F32)
        if nk == 1:
            for o_ref in o_refs:
                o_ref[...] = part.astype(o_ref.dtype)
        else:
            acc_ref = own[2 + n_out]

            @pl.when(k == 0)
            def _():
                acc_ref[...] = part

            @pl.when(k > 0)
            def _():
                acc_ref[...] += part

            @pl.when(k == nk - 1)
            def _():
                for o_ref in o_refs:
                    o_ref[...] = acc_ref[...].astype(o_ref.dtype)

        _comm_edge(comm, comm_refs, grid, first=False)

    if col_pieces > 1:
        per = N // col_pieces // tn
        out_spec = pl.BlockSpec((None, tm, tn), lambda i, j, k: (j // per, i, j % per))
        shape = (col_pieces, M, N // col_pieces)
    else:
        out_spec = pl.BlockSpec((tm, tn), lambda i, j, k: (i, j))
        shape = (M, N)
    dtypes = [out_dtype, BF] if twin else [out_dtype]
    res = pl.pallas_call(
        kern, name=name, grid=grid,
        in_specs=[a_spec, b_spec] + _comm_specs(comm, "in"),
        out_specs=[out_spec] * n_out + _comm_specs(comm, "out"),
        out_shape=[jax.ShapeDtypeStruct(shape, d) for d in dtypes] + (comm.out_shapes if comm else []),
        scratch_shapes=[pltpu.VMEM((tm, tn), F32)] * n_scratch + (comm.sem_shapes if comm else []),
        compiler_params=_cparams("parallel", "parallel", "arbitrary"),
    )(a, b, *(comm.ins if comm else []))
    own = res[0] if n_out == 1 else tuple(res[:n_out])
    return (own, res[n_out:]) if comm else own


ROWS = 512


def _row_spec(tm, width=D_MODEL, col=0):
    return pl.BlockSpec((tm, width), lambda i: (i, col))


def _vec_spec(width=D_MODEL):
    return pl.BlockSpec((1, width), lambda i: (0, 0))


def _rms(x):
    return lax.rsqrt(jnp.mean(x * x, axis=-1, keepdims=True) + RMS_EPS)


def _colsum(x):
    return jnp.sum(x, axis=0, keepdims=True)


def _norm_bwd(d_xn, xn, r):
    return r * (d_xn - xn * jnp.mean(d_xn * xn, axis=-1, keepdims=True))


def _pre_norm(x, g, scale, shift, name, comm=None):
    S = x.shape[0]
    tm = _tile(S, ROWS, 8)
    grid = (S // tm,)

    def kern(*refs):
        (x_ref, g_ref, sc_ref, sh_ref, h_ref), comm_refs = _own_refs(refs, comm, 4, 1, 0)
        _comm_edge(comm, comm_refs, grid, first=True)
        xf = x_ref[...]
        y = xf * _rms(xf) * g_ref[...]
        h_ref[...] = (y * (1.0 + sc_ref[...]) + sh_ref[...]).astype(BF)
        _comm_edge(comm, comm_refs, grid, first=False)

    res = pl.pallas_call(
        kern, name=name, grid=grid,
        in_specs=[_row_spec(tm), _vec_spec(), _vec_spec(), _vec_spec()] + _comm_specs(comm, "in"),
        out_specs=[_row_spec(tm)] + _comm_specs(comm, "out"),
        out_shape=[jax.ShapeDtypeStruct((S, D_MODEL), BF)] + (comm.out_shapes if comm else []),
        scratch_shapes=comm.sem_shapes if comm else [],
        compiler_params=_cparams("arbitrary"),
    )(x, g, scale, shift, *(comm.ins if comm else []))
    return res[0], res[1:]


def _post_pre(x, y1, g2, gate_m, g3, scale_f, shift_f):
    S = x.shape[0]
    tm = _tile(S, ROWS, 8)

    def kern(x_ref, y_ref, g2_ref, gm_ref, g3_ref, sc_ref, sh_ref, x2_ref, h2_ref):
        y = y_ref[...].astype(F32)
        n2 = y * _rms(y) * g2_ref[...]
        x2 = x_ref[...] + gm_ref[...] * n2
        x2_ref[...] = x2
        n3 = x2 * _rms(x2) * g3_ref[...]
        h2_ref[...] = (n3 * (1.0 + sc_ref[...]) + sh_ref[...]).astype(BF)

    return pl.pallas_call(
        kern, name="post_mix_pre_ffn", grid=(S // tm,),
        in_specs=[_row_spec(tm), _row_spec(tm)] + [_vec_spec()] * 5,
        out_specs=[_row_spec(tm), _row_spec(tm)],
        out_shape=[jax.ShapeDtypeStruct((S, D_MODEL), F32), jax.ShapeDtypeStruct((S, D_MODEL), BF)],
        compiler_params=_cparams("parallel"),
    )(x, y1, g2, gate_m, g3, scale_f, shift_f)


def _stats_spec():
    return pl.BlockSpec((8, D_MODEL), lambda i: (0, 0))


def _final(x2, y2, g4, gate_f, target):
    S = x2.shape[0]
    tm = _tile(S, ROWS, 8)

    def kern(x2_ref, y_ref, g4_ref, gf_ref, t_ref, dout_ref, dy_ref, st_ref):
        @pl.when(pl.program_id(0) == 0)
        def _():
            st_ref[...] = jnp.zeros_like(st_ref)

        y = y_ref[...].astype(F32)
        r = _rms(y)
        yn = y * r
        n4 = yn * g4_ref[...]
        diff = x2_ref[...] + gf_ref[...] * n4 - t_ref[...]
        d_out = diff / D_MODEL
        dout_ref[...] = d_out
        dn = d_out * gf_ref[...]
        dy_ref[...] = _norm_bwd(dn * g4_ref[...], yn, r).astype(BF)
        st_ref[0:1, :] += _colsum(d_out * n4)
        st_ref[1:2, :] += _colsum(dn * yn)
        st_ref[2:3, :] += _colsum(diff * diff)

    return pl.pallas_call(
        kern, name="final_loss", grid=(S // tm,),
        in_specs=[_row_spec(tm), _row_spec(tm), _vec_spec(), _vec_spec(), _row_spec(tm)],
        out_specs=[_row_spec(tm), _row_spec(tm), _stats_spec()],
        out_shape=[jax.ShapeDtypeStruct((S, D_MODEL), F32), jax.ShapeDtypeStruct((S, D_MODEL), BF),
                   jax.ShapeDtypeStruct((8, D_MODEL), F32)],
        compiler_params=_cparams("arbitrary"),
    )(x2, y2, g4, gate_f, target)


def _mid_bwd(d_h2, x2, d_out, y1, g3, scale_f, g2, gate_m):
    S = x2.shape[0]
    tm = _tile(S, ROWS, 8)

    def kern(dh_ref, x2_ref, dout_ref, y_ref, g3_ref, sc_ref, g2_ref, gm_ref, dx2_ref, dy_ref, st_ref):
        @pl.when(pl.program_id(0) == 0)
        def _():
            st_ref[...] = jnp.zeros_like(st_ref)

        dh = dh_ref[...].astype(F32)
        x2 = x2_ref[...]
        r3 = _rms(x2)
        xn = x2 * r3
        one_sc = 1.0 + sc_ref[...]
        d_x2 = dout_ref[...] + _norm_bwd(dh * one_sc * g3_ref[...], xn, r3)
        dx2_ref[...] = d_x2
        y = y_ref[...].astype(F32)
        r2 = _rms(y)
        yn = y * r2
        dn = d_x2 * gm_ref[...]
        dy_ref[...] = _norm_bwd(dn * g2_ref[...], yn, r2).astype(BF)
        st_ref[0:1, :] += _colsum(dh)
        st_ref[1:2, :] += _colsum(dh * (xn * g3_ref[...]))
        st_ref[2:3, :] += _colsum(dh * one_sc * xn)
        st_ref[3:4, :] += _colsum(d_x2 * (yn * g2_ref[...]))
        st_ref[4:5, :] += _colsum(dn * yn)

    return pl.pallas_call(
        kern, name="mid_bwd", grid=(S // tm,),
        in_specs=[_row_spec(tm)] * 4 + [_vec_spec()] * 4,
        out_specs=[_row_spec(tm), _row_spec(tm), _stats_spec()],
        out_shape=[jax.ShapeDtypeStruct((S, D_MODEL), F32), jax.ShapeDtypeStruct((S, D_MODEL), BF),
                   jax.ShapeDtypeStruct((8, D_MODEL), F32)],
        compiler_params=_cparams("arbitrary"),
    )(d_h2, x2, d_out, y1, g3, scale_f, g2, gate_m)


def _pre_bwd(d_h1, x, d_x2, g1, scale_m, comm=None):
    S = x.shape[0]
    tm = _tile(S, ROWS, 8)
    grid = (S // tm,)

    def kern(*refs):
        (dh_ref, x_ref, dx2_ref, g_ref, sc_ref, gx_ref, st_ref), comm_refs = _own_refs(refs, comm, 5, 2, 0)
        _comm_edge(comm, comm_refs, grid, first=True)

        @pl.when(pl.program_id(0) == 0)
        def _():
            st_ref[...] = jnp.zeros_like(st_ref)

        dh = dh_ref[...].astype(F32)
        xf = x_ref[...]
        r = _rms(xf)
        xn = xf * r
        one_sc = 1.0 + sc_ref[...]
        gx_ref[...] = dx2_ref[...] + _norm_bwd(dh * one_sc * g_ref[...], xn, r)
        st_ref[0:1, :] += _colsum(dh)
        st_ref[1:2, :] += _colsum(dh * (xn * g_ref[...]))
        st_ref[2:3, :] += _colsum(dh * one_sc * xn)
        _comm_edge(comm, comm_refs, grid, first=False)

    res = pl.pallas_call(
        kern, name="pre_mix_bwd", grid=grid,
        in_specs=[_row_spec(tm)] * 3 + [_vec_spec()] * 2 + _comm_specs(comm, "in"),
        out_specs=[_row_spec(tm), _stats_spec()] + _comm_specs(comm, "out"),
        out_shape=[jax.ShapeDtypeStruct((S, D_MODEL), F32), jax.ShapeDtypeStruct((8, D_MODEL), F32)]
        + (comm.out_shapes if comm else []),
        scratch_shapes=comm.sem_shapes if comm else [],
        input_output_aliases={5 + i: 2 + o for i, o in comm.aliases.items()} if comm else {},
        compiler_params=_cparams("arbitrary"),
    )(d_h1, x, d_x2, g1, scale_m, *(comm.ins if comm else []))
    return res[0], res[1], res[2:]


def _rope(xs, widths, cos_t, sin_t, name):
    S = xs[0].shape[0]
    tm = _tile(S, 512, 8)
    n = len(xs)

    def kern(*refs):
        cos = refs[n][...]
        sin = refs[n + 1][...]
        first = (lax.broadcasted_iota(jnp.int32, cos.shape, 1) % HEAD_DIM) < HEAD_DIM // 2
        for x_ref, o_ref, w in zip(refs[:n], refs[n + 2:], widths):
            for c0 in range(0, w, LANES):
                v = x_ref[:, c0:c0 + LANES]
                partner = jnp.where(first, pltpu.roll(v, LANES - HEAD_DIM // 2, 1),
                                    pltpu.roll(v, HEAD_DIM // 2, 1))
                o_ref[:, c0:c0 + LANES] = (v * cos + partner * sin).astype(BF)

    return pl.pallas_call(
        kern, name=name, grid=(S // tm,),
        in_specs=[_row_spec(tm, w) for w in widths] + [_row_spec(tm, LANES)] * 2,
        out_specs=[_row_spec(tm, w) for w in widths],
        out_shape=[jax.ShapeDtypeStruct((S, w), BF) for w in widths],
        compiler_params=_cparams("parallel"),
    )(*xs, cos_t, sin_t)


def _merge_fwd(pg, pa, pb):
    S = pa.shape[0]
    tm = _tile(S, ROWS, 8)

    def kern(ga_ref, gb_ref, pa_ref, pb_ref, o_ref):
        ga = jax.nn.sigmoid(ga_ref[...].astype(F32))
        gb = jax.nn.sigmoid(gb_ref[...].astype(F32))
        o_ref[...] = (ga * pa_ref[...].astype(F32) + gb * pb_ref[...].astype(F32)).astype(BF)

    return pl.pallas_call(
        kern, name="merge_fwd", grid=(S // tm,),
        in_specs=[_row_spec(tm, col=0), _row_spec(tm, col=1), _row_spec(tm), _row_spec(tm)],
        out_specs=_row_spec(tm),
        out_shape=jax.ShapeDtypeStruct((S, D_MODEL), BF),
        compiler_params=_cparams("parallel"),
    )(pg, pg, pa, pb)


def _merge_bwd(d_merged, pg, pa, pb):
    S = pa.shape[0]
    tm = _tile(S, ROWS, 8)

    def kern(dm_ref, ga_ref, gb_ref, pa_ref, pb_ref, dpa_ref, dpb_ref, dga_ref, dgb_ref):
        dm = dm_ref[...].astype(F32)
        ga = jax.nn.sigmoid(ga_ref[...].astype(F32))
        gb = jax.nn.sigmoid(gb_ref[...].astype(F32))
        dpa_ref[...] = (dm * ga).astype(BF)
        dpb_ref[...] = (dm * gb).astype(BF)
        dga_ref[...] = (dm * pa_ref[...].astype(F32) * ga * (1.0 - ga)).astype(BF)
        dgb_ref[...] = (dm * pb_ref[...].astype(F32) * gb * (1.0 - gb)).astype(BF)

    bf_out = jax.ShapeDtypeStruct((S, D_MODEL), BF)
    return pl.pallas_call(
        kern, name="merge_bwd", grid=(S // tm,),
        in_specs=[_row_spec(tm), _row_spec(tm, col=0), _row_spec(tm, col=1), _row_spec(tm), _row_spec(tm)],
        out_specs=[_row_spec(tm)] * 4,
        out_shape=[bf_out] * 4,
        compiler_params=_cparams("parallel"),
    )(d_merged, pg, pg, pa, pb)


def _swiglu_fwd(gu):
    S = gu.shape[0]
    tm = _tile(S, ROWS, 8)
    tc = _tile(D_FF, 1408)
    nc = D_FF // tc

    def kern(g_ref, u_ref, o_ref):
        g = g_ref[...].astype(F32)
        o_ref[...] = (g * jax.nn.sigmoid(g) * u_ref[...].astype(F32)).astype(BF)

    return pl.pallas_call(
        kern, name="swiglu_fwd", grid=(S // tm, nc),
        in_specs=[pl.BlockSpec((tm, tc), lambda i, j: (i, j)),
                  pl.BlockSpec((tm, tc), lambda i, j: (i, j + nc))],
        out_specs=pl.BlockSpec((tm, tc), lambda i, j: (i, j)),
        out_shape=jax.ShapeDtypeStruct((S, D_FF), BF),
        compiler_params=_cparams("parallel", "parallel"),
    )(gu, gu)


def _swiglu_bwd(d_act, gu):
    S = gu.shape[0]
    tm = _tile(S, ROWS // 2, 8)

    def kern(da_ref, g_ref, u_ref, o_ref):
        g = g_ref[...].astype(F32)
        u = u_ref[...].astype(F32)
        da = da_ref[...].astype(F32)
        sg = jax.nn.sigmoid(g)
        o_ref[:, :D_FF] = (da * u * (sg * (1.0 + g * (1.0 - sg)))).astype(BF)
        o_ref[:, D_FF:] = (da * (g * sg)).astype(BF)

    return pl.pallas_call(
        kern, name="swiglu_bwd", grid=(S // tm,),
        in_specs=[_row_spec(tm, D_FF), _row_spec(tm, D_FF, 0), _row_spec(tm, D_FF, 1)],
        out_specs=_row_spec(tm, 2 * D_FF),
        out_shape=jax.ShapeDtypeStruct((S, 2 * D_FF), BF),
        compiler_params=_cparams("parallel"),
    )(d_act, gu, gu)


def _split3(x):
    hi = x.astype(BF)
    r1 = x - hi.astype(F32)
    mid = r1.astype(BF)
    lo = (r1 - mid.astype(F32)).astype(BF)
    return hi, mid, lo


def _tri_dot(tri, x):
    return sum(jnp.dot(tri, part, preferred_element_type=F32) for part in _split3(x))


def _log_sigmoid(z):
    return jnp.minimum(z, 0.0) - jnp.log(1.0 + jnp.exp(-jnp.abs(z)))


def _fox_gate_fwd(pa, b_f_pad):
    S = pa.shape[0]
    T = _tile(S, 512, 8)
    f_col = OFF_F // LANES

    def kern(z_ref, b_ref, cum_ref, carry_ref):
        @pl.when(pl.program_id(0) == 0)
        def _():
            carry_ref[...] = jnp.zeros_like(carry_ref)

        log_f = _log_sigmoid(z_ref[...] + b_ref[...])
        row = lax.broadcasted_iota(jnp.int32, (T, T), 0)
        col = lax.broadcasted_iota(jnp.int32, (T, T), 1)
        tri = (col <= row).astype(BF)
        cum = _tri_dot(tri, log_f) + carry_ref[...]
        cum_ref[...] = cum
        carry_ref[...] = cum[T - 1:T, :]

    return pl.pallas_call(
        kern, name="fox_gate_fwd", grid=(S // T,),
        in_specs=[_row_spec(T, LANES, f_col), _vec_spec(LANES)],
        out_specs=_row_spec(T, LANES),
        out_shape=jax.ShapeDtypeStruct((S, LANES), F32),
        scratch_shapes=[pltpu.VMEM((1, LANES), F32)],
        compiler_params=_cparams("arbitrary"),
    )(pa, b_f_pad)


def _fox_gate_bwd(rowsum_ds, colsum_ds, pa, b_f_pad):
    S = pa.shape[0]
    T = _tile(S, 512, 8)
    nb = S // T
    f_col = OFF_F // LANES

    def kern(dr_ref, dc_ref, z_ref, b_ref, df_ref, dbf_ref, carry_ref):
        @pl.when(pl.program_id(0) == 0)
        def _():
            carry_ref[...] = jnp.zeros_like(carry_ref)
            dbf_ref[...] = jnp.zeros_like(dbf_ref)

        row = lax.broadcasted_iota(jnp.int32, (T, T), 0)
        col = lax.broadcasted_iota(jnp.int32, (T, T), 1)
        tri = (col >= row).astype(BF)
        rev = _tri_dot(tri, dr_ref[...] - dc_ref[...]) + carry_ref[...]
        carry_ref[...] = rev[0:1, :]
        z = z_ref[...] + b_ref[...]
        lane = lax.broadcasted_iota(jnp.int32, (T, LANES), 1)
        d_z = jnp.where(lane < B_HEADS, rev * jax.nn.sigmoid(-z), 0.0)
        df_ref[...] = d_z.astype(BF)
        dbf_ref[0:1, :] += _colsum(d_z)

    return pl.pallas_call(
        kern, name="fox_gate_bwd", grid=(nb,),
        in_specs=[pl.BlockSpec((T, LANES), lambda i: (nb - 1 - i, 0)),
                  pl.BlockSpec((T, LANES), lambda i: (nb - 1 - i, 0)),
                  pl.BlockSpec((T, LANES), lambda i: (nb - 1 - i, f_col)),
                  _vec_spec(LANES)],
        out_specs=[pl.BlockSpec((T, LANES), lambda i: (nb - 1 - i, 0)),
                   pl.BlockSpec((8, LANES), lambda i: (0, 0))],
        out_shape=[jax.ShapeDtypeStruct((S, LANES), BF), jax.ShapeDtypeStruct((8, LANES), F32)],
        scratch_shapes=[pltpu.VMEM((1, LANES), F32)],
        compiler_params=_cparams("arbitrary"),
    )(rowsum_ds, colsum_ds, pa, b_f_pad)


NEG_INF = float("-inf")
QK_SCALE = 1.0 / math.sqrt(HEAD_DIM)


def _half_mask(shape, half):
    lane = lax.broadcasted_iota(jnp.int32, shape, 1)
    return (lane < HEAD_DIM) if half == 0 else (lane >= HEAD_DIM)


def _bias_block(shape, terms, term_off, ones_lo, ones_hi):
    l64 = lax.broadcasted_iota(jnp.int32, shape, 1) & (HEAD_DIM - 1)
    out = jnp.where((l64 >= ones_lo) & (l64 < ones_hi), 1.0, 0.0)
    for t, term in enumerate(terms):
        out = jnp.where(l64 == term_off + t, term.astype(F32), out)
    return out


def _head_column(block, head):
    lane = lax.broadcasted_iota(jnp.int32, block.shape, 1)
    return jnp.sum(jnp.where(lane == head, block, 0.0), axis=1, keepdims=True)


def _crossed(shape, first, second):
    return jnp.where(_half_mask(shape, 0), second, first)


def _fox_prep_fwd(cum, T):
    S = cum.shape[0]
    shape = (T, LANES)

    def kern(c_ref, bq_ref, bk_ref):
        p_id = pl.program_id(0)
        cum_blk = c_ref[...]
        c3 = _split3(_crossed(shape, _head_column(cum_blk, 2 * p_id), _head_column(cum_blk, 2 * p_id + 1)))
        bq_ref[...] = _bias_block(shape, c3, 0, 3, 6).astype(BF)
        bk_ref[...] = _bias_block(shape, [-t.astype(F32) for t in c3], 3, 0, 3).astype(BF)

    out_spec = pl.BlockSpec((None, T, LANES), lambda p, i: (p, i, 0))
    out_shape = jax.ShapeDtypeStruct((B_HEADS // 2, S, LANES), BF)
    return pl.pallas_call(
        kern, name="fox_prep_fwd", grid=(B_HEADS // 2, S // T),
        in_specs=[pl.BlockSpec((T, LANES), lambda p, i: (i, 0))],
        out_specs=[out_spec, out_spec], out_shape=[out_shape, out_shape],
        compiler_params=_cparams("parallel", "parallel"),
    )(cum)


def _fox_fwd(p_b, bq, bk, T, comm=None):
    S = p_b.shape[0]
    nq = S // T
    n_pairs = B_HEADS // 2
    grid = (n_pairs, nq)

    def kern(*refs):
        (q_ref, k_ref, v_ref, bq_ref, bk_ref, o_ref, lse_ref), comm_refs = _own_refs(refs, comm, 5, 2, 0)
        _comm_edge(comm, comm_refs, grid, first=True)
        i = pl.program_id(1)
        rowcol = lax.broadcasted_iota(jnp.int32, (T, T), 0) - lax.broadcasted_iota(jnp.int32, (T, T), 1)
        hms = (_half_mask((T, LANES), 0), _half_mask((T, LANES), 1))
        q_scaled = (q_ref[...].astype(F32) * QK_SCALE).astype(BF)
        bq_blk = bq_ref[...]
        qs = [jnp.where(hms[h], q_scaled, bq_blk) for h in (0, 1)]

        def step(j, carry, masked):
            rows = pl.ds(pl.multiple_of(j * T, T), T)
            kj, bkj, vj = k_ref[rows, :], bk_ref[rows, :], v_ref[rows, :]
            new = []
            for half in (0, 1):
                m, l, acc = carry[half]
                s = lax.dot_general(qs[half], jnp.where(hms[half], kj, bkj), (((1,), (1,)), ((), ())),
                                    preferred_element_type=F32)
                if masked:
                    s = jnp.where(rowcol >= 0, s, NEG_INF)
                m_new = jnp.maximum(m, jnp.max(s, axis=1, keepdims=True))
                alpha = jnp.exp(m - m_new)
                p = jnp.exp(s - m_new)
                l_new = alpha * l + jnp.sum(p, axis=1, keepdims=True)
                acc_new = alpha * acc + jnp.dot(p.astype(BF), vj, preferred_element_type=F32)
                new.append((m_new, l_new, acc_new))
            return tuple(new)

        one = (jnp.full((T, 1), NEG_INF, F32), jnp.zeros((T, 1), F32), jnp.zeros((T, LANES), F32))
        carry = lax.fori_loop(0, i, functools.partial(step, masked=False), (one, one))
        (m0, l0, acc0), (m1, l1, acc1) = step(i, carry, True)
        hm0 = _half_mask((T, LANES), 0)
        o_ref[...] = jnp.where(hm0, acc0 / l0, acc1 / l1)
        lse_ref[...] = jnp.where(hm0, m0 + jnp.log(l0), m1 + jnp.log(l1))
        _comm_edge(comm, comm_refs, grid, first=False)

    out_spec = pl.BlockSpec((T, LANES), lambda p, i: (i, p))
    res = pl.pallas_call(
        kern, name="fox_fwd", grid=grid,
        in_specs=[pl.BlockSpec((T, LANES), lambda p, i: (i, OFF_QB // LANES + p)),
                  pl.BlockSpec((S, LANES), lambda p, i: (0, OFF_KB // LANES + p)),
                  pl.BlockSpec((S, LANES), lambda p, i: (0, OFF_VB // LANES + p)),
                  pl.BlockSpec((None, T, LANES), lambda p, i: (p, i, 0)),
                  pl.BlockSpec((None, S, LANES), lambda p, i: (p, 0, 0))] + _comm_specs(comm, "in"),
        out_specs=[out_spec, out_spec] + _comm_specs(comm, "out"),
        out_shape=[jax.ShapeDtypeStruct((S, n_pairs * LANES), F32)] * 2 + (comm.out_shapes if comm else []),
        scratch_shapes=comm.sem_shapes if comm else [],
        compiler_params=_cparams("arbitrary", "arbitrary"),
    )(p_b, p_b, p_b, bq, bk, *(comm.ins if comm else []))
    return res[0], res[1], res[2:]


def _fox_prep_bwd(cum, o, do, lse, T):
    S = o.shape[0]
    shape = (T, LANES)

    def kern(c_ref, o_ref, do_ref, lse_ref, bq_ref, bdo_ref):
        p_id = pl.program_id(0)
        cum_blk = c_ref[...]
        cq = _crossed(shape, _head_column(cum_blk, 2 * p_id), _head_column(cum_blk, 2 * p_id + 1))
        b3 = _split3(cq - pltpu.roll(lse_ref[...], HEAD_DIM, 1))
        bq_ref[...] = _bias_block(shape, b3, 0, 3, 6).astype(BF)
        dd = do_ref[...] * o_ref[...]
        delta = [jnp.sum(jnp.where(_half_mask(shape, h), dd, 0.0), axis=1, keepdims=True) for h in (0, 1)]
        d3 = _split3(-_crossed(shape, delta[0], delta[1]))
        bdo_ref[...] = _bias_block(shape, d3, 0, 0, 0).astype(BF)

    block = pl.BlockSpec((None, T, LANES), lambda p, i: (p, i, 0))
    tile = pl.BlockSpec((T, LANES), lambda p, i: (i, p))
    out_shape = jax.ShapeDtypeStruct((B_HEADS // 2, S, LANES), BF)
    return pl.pallas_call(
        kern, name="fox_prep_bwd", grid=(B_HEADS // 2, S // T),
        in_specs=[pl.BlockSpec((T, LANES), lambda p, i: (i, 0)), tile, tile, tile],
        out_specs=[block, block], out_shape=[out_shape, out_shape],
        compiler_params=_cparams("parallel", "parallel"),
    )(cum, o, do, lse)


def _fox_bwd(p_b, do, bq, bk, bdo, T, comm=None):
    S = p_b.shape[0]
    n_pairs = B_HEADS // 2
    nq = S // T
    grid = (n_pairs,)

    def kern(*refs):
        own, comm_refs = _own_refs(refs, comm, 7, 5, 0)
        q_ref, k_ref, v_ref, do_ref, bq_ref, bk_ref, bdo_ref, dq_ref, dk_ref, dv_ref, dck_ref, dcq_ref = own
        _comm_edge(comm, comm_refs, grid, first=True)
        p_id = pl.program_id(0)
        rowcol = lax.broadcasted_iota(jnp.int32, (T, T), 0) - lax.broadcasted_iota(jnp.int32, (T, T), 1)
        lane = lax.broadcasted_iota(jnp.int32, (T, LANES), 1)
        dk_ref[...] = jnp.zeros_like(dk_ref)
        dv_ref[...] = jnp.zeros_like(dv_ref)
        dck_ref[...] = jnp.zeros_like(dck_ref)

        @pl.when(p_id == 0)
        def _():
            dcq_ref[...] = jnp.zeros_like(dcq_ref)

        hms = (_half_mask((T, LANES), 0), _half_mask((T, LANES), 1))
        v_ones = _bias_block((T, LANES), [], 0, 0, 3).astype(BF)

        def outer(i, carry):
            qrows = pl.ds(pl.multiple_of(i * T, T), T)
            q_scaled = (q_ref[qrows, :].astype(F32) * QK_SCALE).astype(BF)
            do_b = do_ref[qrows, :].astype(BF)
            bq_i, bdo_i = bq_ref[qrows, :], bdo_ref[qrows, :]
            qa = [jnp.where(hms[h], q_scaled, bq_i) for h in (0, 1)]
            doa = [jnp.where(hms[h], do_b, bdo_i) for h in (0, 1)]
            q_own = [jnp.where(hms[h], q_scaled, 0) for h in (0, 1)]
            do_own = [jnp.where(hms[h], do_b, 0) for h in (0, 1)]

            def inner(j, carry_in, masked):
                krows = pl.ds(pl.multiple_of(j * T, T), T)
                kj, bkj, vj = k_ref[krows, :], bk_ref[krows, :], v_ref[krows, :]
                dv_add, dk_add, new = 0.0, 0.0, []
                for half in (0, 1):
                    dq, rs = carry_in[half]
                    ka = jnp.where(hms[half], kj, bkj)
                    s = lax.dot_general(qa[half], ka, (((1,), (1,)), ((), ())), preferred_element_type=F32)
                    if masked:
                        s = jnp.where(rowcol >= 0, s, NEG_INF)
                    p = jnp.exp(s)
                    ds = p * lax.dot_general(doa[half], jnp.where(hms[half], vj, v_ones),
                                             (((1,), (1,)), ((), ())), preferred_element_type=F32)
                    ds_b = ds.astype(BF)
                    dv_add = dv_add + lax.dot_general(p.astype(BF), do_own[half], (((0,), (0,)), ((), ())),
                                                      preferred_element_type=F32)
                    dk_add = dk_add + lax.dot_general(ds_b, q_own[half], (((0,), (0,)), ((), ())),
                                                      preferred_element_type=F32)
                    dck_ref[half:half + 1, krows] += jnp.sum(ds, axis=0, keepdims=True)
                    new.append((dq + jnp.dot(ds_b, jnp.where(hms[half], kj, 0), preferred_element_type=F32),
                                rs + jnp.sum(ds, axis=1, keepdims=True)))
                dv_ref[krows, :] += dv_add
                dk_ref[krows, :] += dk_add
                return tuple(new)

            one = (jnp.zeros((T, LANES), F32), jnp.zeros((T, 1), F32))
            carry_in = lax.fori_loop(0, i, functools.partial(inner, masked=False), (one, one))
            (dq0, rs0), (dq1, rs1) = inner(i, carry_in, True)
            dq_ref[qrows, :] = (dq0 + dq1) * QK_SCALE
            dcq_ref[qrows, :] = jnp.where(lane == 2 * p_id, rs0, jnp.where(lane == 2 * p_id + 1, rs1,
                                                                             dcq_ref[qrows, :]))
            return carry

        lax.fori_loop(0, nq, outer, 0)
        _comm_edge(comm, comm_refs, grid, first=False)

    block = pl.BlockSpec((None, S, LANES), lambda p: (p, 0, 0))
    pair = pl.BlockSpec((S, LANES), lambda p: (0, p))
    slab = lambda off: pl.BlockSpec((S, LANES), lambda p: (0, off // LANES + p))
    wide = jax.ShapeDtypeStruct((S, n_pairs * LANES), F32)
    res = pl.pallas_call(
        kern, name="fox_bwd", grid=grid,
        in_specs=[slab(OFF_QB), slab(OFF_KB), slab(OFF_VB), pair, block, block, block]
        + _comm_specs(comm, "in"),
        out_specs=[pair, pair, pair, pl.BlockSpec((None, 2, S), lambda p: (p, 0, 0)),
                   pl.BlockSpec((S, LANES), lambda p: (0, 0))] + _comm_specs(comm, "out"),
        out_shape=[wide, wide, wide, jax.ShapeDtypeStruct((n_pairs, 2, S), F32),
                   jax.ShapeDtypeStruct((S, LANES), F32)] + (comm.out_shapes if comm else []),
        scratch_shapes=comm.sem_shapes if comm else [],
        compiler_params=_cparams("arbitrary"),
    )(p_b, p_b, p_b, do, bq, bk, bdo, *(comm.ins if comm else []))
    return (*res[:5], res[5:])


SWA_TQ = 128
SWA_SUB = 32


def _swa_window(i, tq):
    start = pl.multiple_of(jnp.maximum(i * tq - WINDOW, 0), LANES)
    return start, i * tq - start


def _swa_valid(offset, tq):
    rel = offset + lax.broadcasted_iota(jnp.int32, (tq, tq + WINDOW), 0) \
        - lax.broadcasted_iota(jnp.int32, (tq, tq + WINDOW), 1)
    return (rel >= 0) & (rel < WINDOW)


def _swa_fwd(qk, v_arr, v_col, sinks):
    S = qk.shape[0]
    tq = min(SWA_TQ, S - WINDOW)
    sub = min(SWA_SUB, S // tq)
    win = tq + WINDOW

    def kern(q_ref, k_ref, v_ref, sink_ref, o_ref, lse_ref):
        p_id, i = pl.program_id(0), pl.program_id(1)
        hm0 = _half_mask((tq, LANES), 0)
        for t in range(sub):
            rows = slice(t * tq, (t + 1) * tq)
            start, offset = _swa_window(i * sub + t, tq)
            kw = k_ref[pl.ds(start, win), :]
            vw = v_ref[pl.ds(start, win), :].astype(BF)
            valid = _swa_valid(offset, tq)
            q = q_ref[rows, :]
            outs, lses = [], []
            for half in (0, 1):
                hm = _half_mask((tq, LANES), half)
                qh = (jnp.where(hm, q, 0).astype(F32) * QK_SCALE).astype(BF)
                s = lax.dot_general(qh, kw, (((1,), (1,)), ((), ())), preferred_element_type=F32)
                s = jnp.where(valid, s, NEG_INF)
                sink = sink_ref[2 * p_id + half]
                m = jnp.maximum(jnp.max(s, axis=1, keepdims=True), sink)
                p = jnp.exp(s - m)
                denom = jnp.sum(p, axis=1, keepdims=True) + jnp.exp(sink - m)
                outs.append(jnp.dot(p.astype(BF), vw, preferred_element_type=F32) / denom)
                lses.append(m + jnp.log(denom))
            o_ref[rows, :] = jnp.where(hm0, outs[0], outs[1])
            lse_ref[rows, :] = jnp.where(hm0, lses[0], lses[1])

    tile = pl.BlockSpec((sub * tq, LANES), lambda p, i: (i, p))
    return pl.pallas_call(
        kern, name="swa_fwd", grid=(A_Q_HEADS // 2, S // (sub * tq)),
        in_specs=[tile, pl.BlockSpec((S, LANES), lambda p, i: (0, A_Q_HEADS // 2)),
                  pl.BlockSpec((S, LANES), lambda p, i: (0, v_col)),
                  pl.BlockSpec(memory_space=pltpu.SMEM)],
        out_specs=[tile, tile],
        out_shape=[jax.ShapeDtypeStruct((S, A_Q_HEADS * HEAD_DIM), F32)] * 2,
        compiler_params=_cparams("parallel", "arbitrary"),
    )(qk, qk, v_arr, sinks)


def _swa_bwd(qk, v_arr, v_col, o_arr, do_arr, lse_arr, sinks, comm=None):
    S = qk.shape[0]
    tq = min(SWA_TQ, S - WINDOW)
    sub = min(SWA_SUB, S // tq)
    win = tq + WINDOW
    n_pairs = A_Q_HEADS // 2
    grid = (n_pairs, S // (sub * tq))

    def kern(*refs):
        own, comm_refs = _own_refs(refs, comm, 7, 4, 0)
        q_ref, k_ref, v_ref, o_ref, do_ref, lse_ref, sink_ref, dq_ref, dk_ref, dv_ref, dsink_ref = own
        _comm_edge(comm, comm_refs, grid, first=True)
        p_id, i = pl.program_id(0), pl.program_id(1)

        @pl.when((p_id == 0) & (i == 0))
        def _():
            dk_ref[...] = jnp.zeros_like(dk_ref)
            dv_ref[...] = jnp.zeros_like(dv_ref)

        @pl.when(i == 0)
        def _():
            dsink_ref[...] = jnp.zeros_like(dsink_ref)

        for t in range(sub):
            rows = slice(t * tq, (t + 1) * tq)
            start, offset = _swa_window(i * sub + t, tq)
            wrows = pl.ds(start, win)
            kw = k_ref[wrows, :]
            vw = v_ref[wrows, :].astype(BF)
            valid = _swa_valid(offset, tq)
            q, do, o, lse2 = q_ref[rows, :], do_ref[rows, :], o_ref[rows, :], lse_ref[rows, :]
            dq = jnp.zeros((tq, LANES), F32)
            dk = jnp.zeros((win, LANES), F32)
            dv = jnp.zeros((win, LANES), F32)
            for half in (0, 1):
                hm = _half_mask((tq, LANES), half)
                lane0 = half * HEAD_DIM
                qh = (jnp.where(hm, q, 0).astype(F32) * QK_SCALE).astype(BF)
                do_f = jnp.where(hm, do, 0.0)
                doh = do_f.astype(BF)
                delta = jnp.sum(do_f * o, axis=1, keepdims=True)
                lse = lse2[:, lane0:lane0 + 1]
                s = lax.dot_general(qh, kw, (((1,), (1,)), ((), ())), preferred_element_type=F32)
                p = jnp.exp(jnp.where(valid, s, NEG_INF) - lse)
                dp = lax.dot_general(doh, vw, (((1,), (1,)), ((), ())), preferred_element_type=F32)
                ds_b = (p * (dp - delta)).astype(BF)
                dv = dv + lax.dot_general(p.astype(BF), doh, (((0,), (0,)), ((), ())),
                                          preferred_element_type=F32)
                dk = dk + lax.dot_general(ds_b, qh, (((0,), (0,)), ((), ())), preferred_element_type=F32)
                kh = jnp.where(_half_mask((win, LANES), half), kw, 0)
                dq = dq + jnp.dot(ds_b, kh, preferred_element_type=F32)
                p_sink = jnp.exp(sink_ref[2 * p_id + half] - lse)
                dsink_ref[0, half:half + 1, :] += jnp.broadcast_to(
                    -jnp.sum(p_sink * delta, axis=0, keepdims=True), (1, LANES))
            dq_ref[rows, :] = dq * QK_SCALE
            dk_ref[wrows, :] += dk
            dv_ref[wrows, :] += dv
        _comm_edge(comm, comm_refs, grid, first=False)

    tile = pl.BlockSpec((sub * tq, LANES), lambda p, i: (i, p))
    whole = lambda col: pl.BlockSpec((S, LANES), lambda p, i: (0, col))
    res = pl.pallas_call(
        kern, name="swa_bwd", grid=grid,
        in_specs=[tile, whole(n_pairs), whole(v_col), tile, tile, tile,
                  pl.BlockSpec(memory_space=pltpu.SMEM)] + _comm_specs(comm, "in"),
        out_specs=[tile, whole(0), whole(0),
                   pl.BlockSpec((1, 8, LANES), lambda p, i: (p, 0, 0))] + _comm_specs(comm, "out"),
        out_shape=[jax.ShapeDtypeStruct((S, A_Q_HEADS * HEAD_DIM), F32),
                   jax.ShapeDtypeStruct((S, LANES), F32), jax.ShapeDtypeStruct((S, LANES), F32),
                   jax.ShapeDtypeStruct((n_pairs, 8, LANES), F32)] + (comm.out_shapes if comm else []),
        scratch_shapes=comm.sem_shapes if comm else [],
        compiler_params=_cparams("arbitrary", "arbitrary"),
    )(qk, qk, v_arr, o_arr, do_arr, lse_arr, sinks, *(comm.ins if comm else []))
    return (*res[:4], res[4:])


ADAMW_BLOCK = 512 * 1024


def _adamw(w, g, m, v, name, comm=None):
    R, C = w.shape
    tr, tc = _tile(R, max(8, ADAMW_BLOCK // C), 8), C
    grid = (R // tr, C // tc)

    def kern(*refs):
        (w_ref, g_ref, m_ref, v_ref, d_ref, mo_ref, vo_ref), comm_refs = _own_refs(refs, comm, 4, 3, 0)
        _comm_edge(comm, comm_refs, grid, first=True)
        g_ = g_ref[...]
        m_new = ADAM_B1 * m_ref[...] + (1.0 - ADAM_B1) * g_
        v_new = ADAM_B2 * v_ref[...] + (1.0 - ADAM_B2) * (g_ * g_)
        m_hat = m_new / (1.0 - ADAM_B1 ** ADAM_STEP)
        v_hat = v_new / (1.0 - ADAM_B2 ** ADAM_STEP)
        d_ref[...] = -ADAM_LR * (m_hat / (jnp.sqrt(v_hat) + ADAM_EPS) + ADAM_WD * w_ref[...])
        mo_ref[...] = m_new
        vo_ref[...] = v_new
        _comm_edge(comm, comm_refs, grid, first=False)

    spec = pl.BlockSpec((tr, tc), lambda i, j: (i, j))
    shape = jax.ShapeDtypeStruct((R, C), F32)
    res = pl.pallas_call(
        kern, name=name, grid=grid,
        in_specs=[spec] * 4 + _comm_specs(comm, "in"),
        out_specs=[spec] * 3 + _comm_specs(comm, "out"),
        out_shape=[shape] * 3 + (comm.out_shapes if comm else []),
        scratch_shapes=comm.sem_shapes if comm else [],
        input_output_aliases={4 + i: 3 + o for i, o in comm.aliases.items()} if comm else {},
        compiler_params=_cparams("arbitrary", "arbitrary"),
    )(w, g, m, v, *(comm.ins if comm else []))
    return (res[:3], res[3:]) if comm else res


def _index_operand(i):
    return jnp.reshape(i, (1,)).astype(jnp.int32)


def _add_pair(whole, got, ci, name):
    P, R, C = whole.shape
    half = R // 2
    tr = _tile(half, ROWS, 16)
    nb = half // tr

    def kern(ci_ref, a_ref, b_ref, o_ref, ob_ref):
        s = a_ref[...] + b_ref[...].astype(F32)
        o_ref[...] = s
        ob_ref[...] = s.astype(BF)

    spec = pl.BlockSpec((None, tr, C), lambda p, i, ci_ref: (p, i, 0))
    return pl.pallas_call(
        kern, name=name,
        grid_spec=pltpu.PrefetchScalarGridSpec(
            num_scalar_prefetch=1, grid=(P, nb),
            in_specs=[pl.BlockSpec((None, tr, C), lambda p, i, ci_ref: (p, ci_ref[0] * nb + i, 0)), spec],
            out_specs=[spec, spec]),
        out_shape=[jax.ShapeDtypeStruct((P, half, C), F32), jax.ShapeDtypeStruct((P, half, C), BF)],
        compiler_params=_cparams("parallel", "parallel"),
    )(_index_operand(ci), whole, got)


def _add_three(parts, recv, chip, name):
    _, R, C = parts.shape
    tr = _tile(R, ROWS, 16)

    def kern(chip_ref, o_ref, r0_ref, r1_ref, r2_ref, out_ref):
        s = ((o_ref[...] + r0_ref[...].astype(F32)) + r1_ref[...].astype(F32)) + r2_ref[...].astype(F32)
        out_ref[0] = s
        out_ref[1] = s

    slab = lambda k: pl.BlockSpec((None, tr, C), lambda i, chip_ref: (k, i, 0))
    return pl.pallas_call(
        kern, name=name,
        grid_spec=pltpu.PrefetchScalarGridSpec(
            num_scalar_prefetch=1, grid=(R // tr,),
            in_specs=[pl.BlockSpec((None, tr, C), lambda i, chip_ref: (chip_ref[0], i, 0)),
                      slab(0), slab(1), slab(2)],
            out_specs=pl.BlockSpec((2, tr, C), lambda i, chip_ref: (0, i, 0))),
        out_shape=jax.ShapeDtypeStruct((2, R, C), F32),
        compiler_params=_cparams("parallel"),
    )(_index_operand(chip), parts, recv, recv, recv)


SM_ADA, SM_G, SM_LOSS, SM_BF, SM_SINK, SM_LEN = 0, 6144, 10240, 11264, 11272, 12288


def _small_finalize(gathered):
    def kern(g_ref, tot_ref, loss_ref):
        tot = g_ref[0:1, :]
        for b in range(1, N_DEV):
            tot = tot + g_ref[b:b + 1, :]
        tot_ref[...] = tot
        sq = jnp.sum(tot[:, SM_LOSS:SM_LOSS + D_MODEL], axis=1, keepdims=True)
        loss_ref[...] = jnp.broadcast_to(sq * (0.5 / D_MODEL), (1, LANES))

    full = lambda shape: pl.BlockSpec(shape, lambda i: (0, 0))
    return pl.pallas_call(
        kern, name="small_finalize", grid=(1,),
        in_specs=[full((N_DEV, SM_LEN))],
        out_specs=[full((1, SM_LEN)), full((1, LANES))],
        out_shape=[jax.ShapeDtypeStruct((1, SM_LEN), F32), jax.ShapeDtypeStruct((1, LANES), F32)],
        compiler_params=_cparams("arbitrary"),
    )(gathered)


def _ada_dw(c_t, d_ada):
    N = d_ada.shape[1]
    tn = _tile(N, 512)

    def kern(c_ref, d_ref, o_ref):
        acc = c_ref[:, 0:1] * d_ref[0:1, :]
        for b in range(1, N_DEV):
            acc = acc + c_ref[:, b:b + 1] * d_ref[b:b + 1, :]
        o_ref[...] = acc

    return pl.pallas_call(
        kern, name="ada_dw", grid=(N // tn,),
        in_specs=[pl.BlockSpec((D_MODEL, N_DEV), lambda j: (0, 0)), pl.BlockSpec((N_DEV, tn), lambda j: (0, j))],
        out_specs=pl.BlockSpec((D_MODEL, tn), lambda j: (0, j)),
        out_shape=jax.ShapeDtypeStruct((D_MODEL, N), F32),
        compiler_params=_cparams("parallel"),
    )(c_t, d_ada)


def _here():
    return lax.axis_index("x"), lax.axis_index("y"), lax.axis_index("c")


def _other_chips(x, y):
    return [(1 - x, y), (x, 1 - y), (1 - x, 1 - y)]


_ANY = pl.BlockSpec(memory_space=pl.ANY)


class _Comm:
    def __init__(self, ins, out_shapes, sem_shapes, start, finish, aliases=None):
        self.ins, self.out_shapes, self.sem_shapes = list(ins), list(out_shapes), list(sem_shapes)
        self.start, self.finish = start, finish
        self.aliases = dict(aliases or {})

    def split(self, refs, n_in, n_out, n_scratch):
        a = n_in + len(self.ins)
        b = a + n_out + len(self.out_shapes)
        own = list(refs[:n_in]) + list(refs[a:a + n_out]) + list(refs[b:b + n_scratch])
        mine = (refs[n_in:a], refs[a + n_out:b], refs[b + n_scratch:])
        return own, mine


def _run_comm(comm, name):
    n_in, n_out = len(comm.ins), len(comm.out_shapes)

    def body(*refs):
        parts = (refs[:n_in], refs[n_in:n_in + n_out], refs[n_in + n_out:])
        comm.start(*parts)
        comm.finish(*parts)

    return pl.pallas_call(
        body, name=name,
        in_specs=[_ANY] * n_in, out_specs=[_ANY] * n_out,
        out_shape=comm.out_shapes, scratch_shapes=comm.sem_shapes,
        input_output_aliases=comm.aliases,
    )(*comm.ins)


def _gather_comm(blocks):
    L = len(blocks)

    def parts(ins, outs, sems):
        send_sems, recv_sems, local_sems = sems
        x, y, c = _here()
        me, sibling = (x, y, c), (x, y, 1 - c)
        chips = _other_chips(x, y)

        def slot(px, py, pc):
            return 4 * px + 2 * py + pc

        def copy(l, k, block, to, src=None):
            dst = outs[l].at[slot(*block)]
            return pltpu.make_async_remote_copy(
                src_ref=dst if src is None else src, dst_ref=dst,
                send_sem=send_sems.at[l, k], recv_sem=recv_sems.at[l, k],
                device_id=to, device_id_type=MESH)

        mine = [pltpu.make_async_copy(ins[l], outs[l].at[slot(*me)], local_sems.at[l]) for l in range(L)]
        first = []
        for l in range(L):
            first.append(copy(l, 0, me, sibling, src=ins[l]))
            for j, chip in enumerate(chips):
                first.append(copy(l, 1 + j, me, (*chip, c), src=ins[l]))
        return c, me, sibling, chips, copy, mine, first

    def start(ins, outs, sems):
        *_, mine, first = parts(ins, outs, sems)
        for cp in mine + first:
            cp.start()

    def finish(ins, outs, sems):
        c, me, sibling, chips, copy, mine, first = parts(ins, outs, sems)
        passed = []
        for j, chip in enumerate(chips):
            for l in range(L):
                copy(l, 1 + j, (*chip, c), me).wait_recv()
                fwd = copy(l, 4 + j, (*chip, c), sibling)
                fwd.start()
                passed.append(fwd)
        for l in range(L):
            copy(l, 0, sibling, me).wait_recv()
        for j, chip in enumerate(chips):
            for l in range(L):
                copy(l, 4 + j, (*chip, 1 - c), me).wait_recv()
        for cp in first + passed:
            cp.wait_send()
        for cp in mine:
            cp.wait()

    return _Comm(blocks, [jax.ShapeDtypeStruct((N_DEV,) + b.shape, b.dtype) for b in blocks],
                 [pltpu.SemaphoreType.DMA((L, 7)), pltpu.SemaphoreType.DMA((L, 7)), pltpu.SemaphoreType.DMA((L,))],
                 start, finish)


def _allgather8(blocks, name):
    return _run_comm(_gather_comm(blocks), name)


def _swap_comm(arrs):
    L = len(arrs)

    def copies(ins, outs, sems):
        send_sems, recv_sems = sems
        x, y, c = _here()
        cps = []
        for l in range(L):
            half = arrs[l].shape[1] // 2
            rows = pl.ds(pl.multiple_of((1 - c) * half, 16), half)
            cps.append(pltpu.make_async_remote_copy(
                src_ref=ins[l].at[:, rows, :], dst_ref=outs[l], send_sem=send_sems.at[l],
                recv_sem=recv_sems.at[l], device_id=(x, y, 1 - c), device_id_type=MESH))
        return cps

    def start(ins, outs, sems):
        for cp in copies(ins, outs, sems):
            cp.start()

    def finish(ins, outs, sems):
        for cp in copies(ins, outs, sems):
            cp.wait()

    return _Comm(arrs, [jax.ShapeDtypeStruct((a.shape[0], a.shape[1] // 2, a.shape[2]), a.dtype) for a in arrs],
                 [pltpu.SemaphoreType.DMA((L,)), pltpu.SemaphoreType.DMA((L,))], start, finish)


def _join_comm(bufs):
    L = len(bufs)

    def start(ins, outs, sems):
        send_sems, recv_sems = sems
        x, y, c = _here()
        for l in range(L):
            pltpu.make_async_remote_copy(src_ref=outs[l].at[c], dst_ref=outs[l].at[c], send_sem=send_sems.at[l],
                                         recv_sem=recv_sems.at[l], device_id=(x, y, 1 - c),
                                         device_id_type=MESH).start()

    def finish(ins, outs, sems):
        send_sems, recv_sems = sems
        x, y, c = _here()
        for l in range(L):
            pltpu.make_async_remote_copy(src_ref=outs[l].at[c], dst_ref=outs[l].at[1 - c],
                                         send_sem=send_sems.at[l], recv_sem=recv_sems.at[l],
                                         device_id=(x, y, 1 - c), device_id_type=MESH).wait()

    return _Comm(bufs, [jax.ShapeDtypeStruct(a.shape, a.dtype) for a in bufs],
                 [pltpu.SemaphoreType.DMA((L,)), pltpu.SemaphoreType.DMA((L,))], start, finish,
                 aliases={l: l for l in range(L)})


def _scatter_comm(arrs):
    L = len(arrs)

    def copies(ins, outs, sems):
        send_sems, recv_sems = sems
        x, y, c = _here()
        return [pltpu.make_async_remote_copy(
            src_ref=ins[l].at[2 * tx + ty], dst_ref=outs[l].at[j],
            send_sem=send_sems.at[l, j], recv_sem=recv_sems.at[l, j],
            device_id=(tx, ty, c), device_id_type=MESH)
            for l in range(L) for j, (tx, ty) in enumerate(_other_chips(x, y))]

    def start(ins, outs, sems):
        for cp in copies(ins, outs, sems):
            cp.start()

    def finish(ins, outs, sems):
        for cp in copies(ins, outs, sems):
            cp.wait()

    return _Comm(arrs, [jax.ShapeDtypeStruct((3,) + a.shape[1:], a.dtype) for a in arrs],
                 [pltpu.SemaphoreType.DMA((L, 3)), pltpu.SemaphoreType.DMA((L, 3))], start, finish)


_A_ORDER = np.array(A_HEAD_ORDER)
_A_INVERSE = np.argsort(_A_ORDER)


def _permute_in_weights(w_in):
    qa = w_in[:, 0:512].reshape(D_MODEL, A_Q_HEADS, HEAD_DIM)[:, _A_ORDER, :].reshape(D_MODEL, 512)
    f_pad = jnp.pad(w_in[:, 2304:2312], ((0, 0), (0, LANES - B_HEADS)))
    w_a = jnp.concatenate([qa, w_in[:, 512:640], f_pad], axis=1)
    return w_a, w_in[:, 640:2304], w_in[:, 2312:4360]


def _slab_segments():
    segs = [(h * HEAD_DIM, int(_A_INVERSE[h]) * HEAD_DIM, HEAD_DIM) for h in range(A_Q_HEADS)]
    segs += [(512, OFF_KA, 128), (640, W_A + OFF_VA, 128), (768, W_A + OFF_QB, 1536),
             (2304, OFF_F, B_HEADS), (2312, W_A + W_B, W_G)]
    return segs


def _shard_slabs(dw_perm):
    R = dw_perm.shape[0]
    tr = _tile(R, 128, 8)
    plan = []
    for k in range(N_CHIP):
        for b in range(W_SHARD_PAD // LANES):
            lo, hi = k * W_SHARD + b * LANES, min(k * W_SHARD + (b + 1) * LANES, (k + 1) * W_SHARD)
            parts = []
            for o0, s0, n in _slab_segments():
                a, z = max(lo, o0), min(hi, o0 + n)
                while a < z:
                    s = s0 + (a - o0)
                    run = min(z - a, LANES - s % LANES)
                    parts.append((s // LANES, ((a - lo) - s % LANES) % LANES, a - lo, run))
                    a += run
            plan.append((k, b, parts))

    def kern(x_ref, o32_ref, obf_ref):
        lane = lax.broadcasted_iota(jnp.int32, (tr, LANES), 1)
        for k, b, parts in plan:
            acc = jnp.zeros((tr, LANES), F32)
            for src, rot, first, run in parts:
                blk = x_ref[:, src * LANES:(src + 1) * LANES]
                if rot:
                    blk = pltpu.roll(blk, rot, 1)
                acc = jnp.where((lane >= first) & (lane < first + run), blk, acc)
            o32_ref[k, :, b * LANES:(b + 1) * LANES] = acc
            obf_ref[k, :, b * LANES:(b + 1) * LANES] = acc.astype(BF)

    out_spec = pl.BlockSpec((N_CHIP, tr, W_SHARD_PAD), lambda i: (0, i, 0))
    return tuple(pl.pallas_call(
        kern, name="shard_slabs", grid=(R // tr,),
        in_specs=[pl.BlockSpec((tr, W_PERM), lambda i: (i, 0))],
        out_specs=[out_spec, out_spec],
        out_shape=[jax.ShapeDtypeStruct((N_CHIP, R, W_SHARD_PAD), F32),
                   jax.ShapeDtypeStruct((N_CHIP, R, W_SHARD_PAD), BF)],
        compiler_params=_cparams("parallel"),
    )(dw_perm))


class _NoExchange:
    def __init__(self, w_in, rest):
        self.w_in_whole, self.rest, self.grads = w_in, rest, {}

    def w_in_comm(self):
        return None

    def w_in(self, outs):
        return self.w_in_whole

    def rest_weights_comm(self):
        return None

    def rest_weights(self, outs):
        return self.rest

    def swap_comm(self, pieces, tag):
        self.grads[tag] = [p32 for p32, _ in pieces]
        return None

    def swap_done(self, outs, tag):
        return None

    def reduce_done(self, outs, tag):
        pass

    def join_comm(self):
        return None


class _Exchange:
    def __init__(self, ci, chip, w_in_shard, rest_shards):
        self.ci, self.chip, self.w_in_shard, self.rest_shards = ci, chip, w_in_shard, rest_shards
        self.pieces, self.part_f32, self.halves = {}, {}, {}

    def _my_half(self, a, axis=0, other=False):
        rows = a.shape[axis] // 2
        return lax.dynamic_slice_in_dim(a, ((1 - self.ci) if other else self.ci) * rows, rows, axis=axis)

    def w_in_comm(self):
        return _gather_comm([self._my_half(self.w_in_shard).astype(BF)])

    def w_in(self, outs):
        return _col_sharded(outs[0])

    def rest_weights_comm(self):
        return _gather_comm([self._my_half(w).astype(BF) for w in self.rest_shards])

    def rest_weights(self, outs):
        w_ba, w_bb, w_out, w_fi, w_fo = outs
        return (_col_sharded(w_ba), _col_sharded(w_bb), _row_sharded(w_out), _col_sharded(w_fi),
                _row_sharded(w_fo))

    def swap_comm(self, pieces, tag):
        self.pieces[tag] = pieces
        return _swap_comm([pbf for _, pbf in pieces])

    def swap_done(self, got, tag):
        self.part_f32[tag], part_bf = [], []
        for l, ((p32, _), g_) in enumerate(zip(self.pieces[tag], got)):
            s32, sbf = _add_pair(p32, g_, self.ci, f"chip_sum_{tag}_{l}")
            self.part_f32[tag].append(s32)
            part_bf.append(sbf)
        return _scatter_comm(part_bf)

    def reduce_done(self, outs, tag):
        self.halves[tag] = [_add_three(p32, r, self.chip, f"shard_sum_{tag}_{l}")
                            for l, (p32, r) in enumerate(zip(self.part_f32[tag], outs))]

    def join_comm(self):
        return _join_comm(self.halves["late"] + self.halves["early"])


def _col_sharded(g):
    return jnp.transpose(g.reshape(N_CHIP, -1, g.shape[-1]), (1, 0, 2)).reshape(2 * g.shape[1], N_CHIP * g.shape[-1])


def _row_sharded(g):
    return g.reshape(N_DEV * g.shape[1], g.shape[-1])


def _rope_tables(pos):
    inv_freq = 1.0 / (ROPE_THETA ** (jnp.arange(0, HEAD_DIM, 2, dtype=F32) / HEAD_DIM))
    ang = pos.astype(F32)[:, None] * inv_freq
    cos, sin = jnp.cos(ang), jnp.sin(ang)
    return jnp.tile(cos, (1, 4)), jnp.tile(jnp.concatenate([-sin, sin], axis=1), (1, 2))


def _local_step(x, pos, ada, g1, g2, g3, g4, b_f, sinks, exch, target):
    S = x.shape[0]
    t_fox = _tile(S, 512, LANES) if S >= 1024 else S // 2
    t_fox_fwd = _tile(S, 1024, LANES) if S >= 2048 else S // 2
    shift_m, scale_m, gate_m, shift_f, scale_f, gate_f = [ada[i:i + 1] for i in range(N_ADA)]
    cos_t, sin_t = _rope_tables(pos)
    sinks_p = sinks.reshape(A_KV_HEADS, 4).T.reshape(A_Q_HEADS)
    b_f_pad = jnp.pad(b_f, (0, LANES - B_HEADS)).reshape(1, LANES)

    h1, outs = _pre_norm(x, g1, scale_m, shift_m, "pre_mix_norm", comm=exch.w_in_comm())
    w_a, w_b, w_g = _permute_in_weights(exch.w_in(outs))
    w_perm = jnp.concatenate([w_a, w_b, w_g], axis=1)
    p_a = _mm(h1, w_a, "nn", F32, "proj_a")
    p_b = _mm(h1, w_b, "nn", BF, "proj_b")
    p_g = _mm(h1, w_g, "nn", BF, "proj_g")
    (qk_a,) = _rope([p_a], [640], cos_t, sin_t, "rope_fwd")
    o_a, lse_a = _swa_fwd(qk_a, p_b, 0, sinks_p)
    cum = _fox_gate_fwd(p_a, b_f_pad)
    bq, bk = _fox_prep_fwd(cum, t_fox)
    comm = exch.rest_weights_comm()
    o_b, lse_b, outs = _fox_fwd(p_b, bq, bk, t_fox_fwd, comm=comm)
    w_ba, w_bb, w_out, w_fi, w_fo = exch.rest_weights(outs)
    w_ba_p = w_ba.reshape(A_Q_HEADS, HEAD_DIM, D_MODEL)[_A_ORDER].reshape(512, D_MODEL)
    pa = _mm(o_a, w_ba_p, "nn", BF, "branch_a")
    pb = _mm(o_b, w_bb, "nn", BF, "branch_b")
    merged = _merge_fwd(p_g, pa, pb)
    y1 = _mm(merged, w_out, "nn", BF, "out_proj")
    x2, h2 = _post_pre(x, y1, g2, gate_m, g3, scale_f, shift_f)
    gu = _mm(h2, w_fi, "nn", BF, "ffn_in")
    act = _swiglu_fwd(gu)
    y2 = _mm(act, w_fo, "nn", BF, "ffn_out")
    d_out, d_y2, st_f = _final(x2, y2, g4, gate_f, target)

    d_act = _mm(d_y2, w_fo, "nt", BF, "ffn_out_dx")
    row_pieces = lambda pair: tuple(t.reshape(N_CHIP, t.shape[0] // N_CHIP, t.shape[1]) for t in pair)
    dw_fo = row_pieces(_mm(act, d_y2, "tn", F32, "ffn_out_dw", twin=True))
    d_gu = _swiglu_bwd(d_act, gu)
    d_h2 = _mm(d_gu, w_fi, "nt", BF, "ffn_in_dx")
    dw_fi = _mm(h2, d_gu, "tn", F32, "ffn_in_dw", col_pieces=N_CHIP, twin=True)
    d_x2, d_y1, st_m = _mid_bwd(d_h2, x2, d_out, y1, g3, scale_f, g2, gate_m)
    d_merged = _mm(d_y1, w_out, "nt", BF, "out_proj_dx")
    dw_out = row_pieces(_mm(merged, d_y1, "tn", F32, "out_proj_dw", twin=True))
    d_pa, d_pb, d_ga, d_gb = _merge_bwd(d_merged, p_g, pa, pb)
    d_oa = _mm(d_pa, w_ba_p, "nt", F32, "branch_a_dx")
    dw_ba_p = _mm(o_a, d_pa, "tn", F32, "branch_a_dw", col_pieces=N_CHIP, twin=True)
    d_ob = _mm(d_pb, w_bb, "nt", F32, "branch_b_dx")
    dw_bb = _mm(o_b, d_pb, "tn", F32, "branch_b_dw", col_pieces=N_CHIP, twin=True)
    head_rows = lambda t: t.reshape(N_CHIP, A_Q_HEADS, HEAD_DIM, -1)[:, _A_INVERSE].reshape(t.shape)
    dw_ba = tuple(head_rows(t) for t in dw_ba_p)
    comm = exch.swap_comm([dw_ba, dw_bb, dw_out, dw_fi, dw_fo], "early")
    dq_a, dk_a, dv_a, d_sink, outs = _swa_bwd(qk_a, p_b, 0, o_a, d_oa, lse_a, sinks_p, comm=comm)
    comm = exch.swap_done(outs, "early")
    bq_bwd, bdo = _fox_prep_bwd(cum, o_b, d_ob, lse_b, t_fox)
    dq_b, dk_b, dv_b, d_ck, d_cq, outs = _fox_bwd(p_b, d_ob, bq_bwd, bk, bdo, t_fox, comm=comm)
    exch.reduce_done(outs, "early")
    d_qa, d_ka = _rope([dq_a, dk_a], [512, LANES], cos_t, -sin_t, "rope_bwd")
    d_ck_cols = jnp.pad(d_ck.reshape(B_HEADS, S).T, ((0, 0), (0, LANES - B_HEADS)))
    d_f, d_bf = _fox_gate_bwd(d_cq, d_ck_cols, p_a, b_f_pad)
    d_proj = jnp.concatenate([d_qa, d_ka, d_f, dv_a.astype(BF), dq_b.astype(BF), dk_b.astype(BF),
                              dv_b.astype(BF), d_ga, d_gb], axis=1)
    dw_perm = _mm(h1, d_proj, "tn", F32, "proj_dw")
    swap = exch.swap_comm([_shard_slabs(dw_perm)], "late")
    comm = exch.swap_done(_run_comm(swap, "grads_to_sibling_late") if swap else None, "late")
    res = _mm(d_proj, w_perm, "nt", BF, "proj_dx", comm=comm)
    d_h1 = res[0] if comm else res
    exch.reduce_done(res[1] if comm else None, "late")
    grad_x, st_p, outs = _pre_bwd(d_h1, x, d_x2, g1, scale_m, comm=exch.join_comm())
    exch.joined = outs

    d_sinks = d_sink[:, :2, 0].T.reshape(A_Q_HEADS)
    small = jnp.concatenate([
        st_p[0], st_p[1], st_m[3], st_m[0], st_m[1], st_f[0],
        st_p[2], st_m[4], st_m[2], st_f[1],
        st_f[2], d_bf[0, :B_HEADS], d_sinks,
        jnp.zeros((SM_LEN - SM_SINK - A_Q_HEADS,), F32)])
    return grad_x, small


def kernel(x, c, positions, w_ada, b_ada, g_pre_mix, g_post_mix, w_in, b_f, sinks, w_branch_a, w_branch_b, w_out, g_pre_ffn, g_post_ffn, w_ffn_in, w_ffn_out, loss_target, m_w_ada, m_b_ada, m_g_pre_mix, m_g_post_mix, m_w_in, m_b_f, m_sinks, m_w_branch_a, m_w_branch_b, m_w_out, m_g_pre_ffn, m_g_post_ffn, m_w_ffn_in, m_w_ffn_out, v_w_ada, v_b_ada, v_g_pre_mix, v_g_post_mix, v_w_in, v_b_f, v_sinks, v_w_branch_a, v_w_branch_b, v_w_out, v_g_pre_ffn, v_g_post_ffn, v_w_ffn_in, v_w_ffn_out):
    xi, yi, ci = _here()
    chip = 2 * xi + yi
    dev = 2 * chip + ci

    (c_g,) = _allgather8([c.reshape(8, LANES)], "gather_c")
    c_all = c_g.reshape(N_DEV, D_MODEL)
    exch = _Exchange(ci, chip, w_in[0], [w_branch_a[0], w_branch_b[0], w_out[0], w_ffn_in[0], w_ffn_out[0]])

    ada_cols = _mm(c_all, w_ada[0], "nn", F32, "ada_fwd")
    (ada_g,) = _allgather8([ada_cols], "gather_ada")
    ada_mine = lax.dynamic_index_in_dim(ada_g.reshape(N_CHIP, 2, N_DEV, -1)[:, 0], dev, axis=1, keepdims=False)
    ada = (ada_mine.reshape(-1) + b_ada[0]).reshape(N_ADA, D_MODEL)

    grad_x, small = _local_step(
        x[0], positions[0], ada, g_pre_mix, g_post_mix, g_pre_ffn, g_post_ffn, b_f[0], sinks[0],
        exch, loss_target[0])

    g_w_in, g_w_ba, g_w_bb, g_w_out, g_w_fi, g_w_fo = [j.reshape(2 * j.shape[1], j.shape[2]) for j in exch.joined]
    upd_fi, (small_g,) = _adamw(w_ffn_in[0], g_w_fi, m_w_ffn_in[0], v_w_ffn_in[0], "adamw_w_ffn_in",
                                comm=_gather_comm([small.reshape(8, SM_LEN // 8)]))

    small_all = small_g.reshape(N_DEV, SM_LEN)
    small_tot, loss_row = _small_finalize(small_all)
    loss = loss_row[0, 0]
    d_ada_cols = lax.dynamic_slice_in_dim(small_all[:, :N_ADA * D_MODEL], chip * (N_ADA * D_MODEL // N_CHIP),
                                          N_ADA * D_MODEL // N_CHIP, axis=1)
    g_w_ada = _ada_dw(c_all.T, d_ada_cols)

    def small_vec(b_ada_, g1_, g2_, g3_, g4_, b_f_, sinks_):
        return jnp.concatenate([b_ada_[0], g1_[0], g2_[0], g3_[0], g4_[0], jnp.zeros((D_MODEL,), F32),
                                b_f_[0], sinks_[0], jnp.zeros((SM_LEN - SM_SINK - A_Q_HEADS,), F32)]
                               ).reshape(8, SM_LEN // 8)

    sw = small_vec(b_ada, g_pre_mix, g_post_mix, g_pre_ffn, g_post_ffn, b_f, sinks)
    sm = small_vec(m_b_ada, m_g_pre_mix, m_g_post_mix, m_g_pre_ffn, m_g_post_ffn, m_b_f, m_sinks)
    sv = small_vec(v_b_ada, v_g_pre_mix, v_g_post_mix, v_g_pre_ffn, v_g_post_ffn, v_b_f, v_sinks)
    s_upd = [u.reshape(SM_LEN) for u in _adamw(sw, small_tot.reshape(8, SM_LEN // 8), sm, sv, "adamw_small")]
    s_grad = small_tot.reshape(SM_LEN)

    def unpack(vec):
        row = lambda a, n: vec[a:a + n].reshape(1, n)
        return dict(b_ada=row(SM_ADA, N_ADA * D_MODEL), g_pre_mix=row(SM_G, D_MODEL),
                    g_post_mix=row(SM_G + D_MODEL, D_MODEL), g_pre_ffn=row(SM_G + 2 * D_MODEL, D_MODEL),
                    g_post_ffn=row(SM_G + 3 * D_MODEL, D_MODEL), b_f=row(SM_BF, B_HEADS),
                    sinks=row(SM_SINK, A_Q_HEADS))

    big = dict(
        w_ada=(w_ada, g_w_ada, m_w_ada, v_w_ada),
        w_branch_a=(w_branch_a, g_w_ba, m_w_branch_a, v_w_branch_a),
        w_branch_b=(w_branch_b, g_w_bb, m_w_branch_b, v_w_branch_b),
        w_out=(w_out, g_w_out, m_w_out, v_w_out),
        w_ffn_out=(w_ffn_out, g_w_fo, m_w_ffn_out, v_w_ffn_out))
    grads, deltas, new_m, new_v = unpack(s_grad), unpack(s_upd[0]), unpack(s_upd[1]), unpack(s_upd[2])
    grads["w_ffn_in"], deltas["w_ffn_in"], new_m["w_ffn_in"], new_v["w_ffn_in"] = [
        t[None] for t in (g_w_fi, *upd_fi)]
    for n, (w_, g_, m_, v_) in big.items():
        d_, nm_, nv_ = _adamw(w_[0], g_, m_[0], v_[0], "adamw_" + n)
        grads[n], deltas[n], new_m[n], new_v[n] = g_[None], d_[None], nm_[None], nv_[None]
    pad_cols = lambda a: jnp.pad(a, ((0, 0), (0, W_SHARD_PAD - W_SHARD)))
    upd = _adamw(pad_cols(w_in[0]), g_w_in, pad_cols(m_w_in[0]), pad_cols(v_w_in[0]), "adamw_w_in")
    grads["w_in"], deltas["w_in"], new_m["w_in"], new_v["w_in"] = [t[None, :, :W_SHARD] for t in (g_w_in, *upd)]

    names = ["w_ada", "b_ada", "g_pre_mix", "g_post_mix", "w_in", "b_f", "sinks", "w_branch_a", "w_branch_b",
             "w_out", "g_pre_ffn", "g_post_ffn", "w_ffn_in", "w_ffn_out"]
    return (loss, grad_x[None], *[grads[n] for n in names], *[deltas[n] for n in names],
            *[new_m[n] for n in names], *[new_v[n] for n in names])
```

```python
import functools
import math

import numpy as np
import jax
import jax.numpy as jnp
from jax import lax
from jax.experimental import pallas as pl
from jax.experimental.pallas import tpu as pltpu

F32 = jnp.float32
BF = jnp.bfloat16

D_MODEL = 1024
HEAD_DIM = 64
LANES = 128
WINDOW = 128
A_Q_HEADS = 8
A_KV_HEADS = 2
B_HEADS = 8
D_FF = 2816
ROPE_THETA = 10000.0
RMS_EPS = 1e-6
N_ADA = 6
N_DEV = 8
N_CHIP = 4

ADAM_LR = 0.001
ADAM_B1 = 0.9
ADAM_B2 = 0.999
ADAM_EPS = 1e-08
ADAM_WD = 0.01
ADAM_STEP = 10

VMEM_LIMIT = 48 * 1024 * 1024
MESH = pl.DeviceIdType.MESH

A_HEAD_ORDER = (0, 4, 1, 5, 2, 6, 3, 7)

OFF_QA, OFF_KA, OFF_F = 0, 512, 640
W_A = 768
OFF_VA, OFF_QB, OFF_KB, OFF_VB = 0, 128, 640, 1152
W_B = 1664
W_G = 2048
W_PERM = W_A + W_B + W_G
W_SHARD = 1090
W_SHARD_PAD = 1152


def _tile(n, cap, mult=LANES):
    if n <= cap:
        return n
    t = (cap // mult) * mult
    while t >= mult:
        if n % t == 0:
            return t
        t -= mult
    raise ValueError(f"no tile for {n}")


MXU_WIDTH = 256
MM_OPERAND_BYTES = 28 * 1024 * 1024


def _mm_tiles(M, N, K, a_bytes, b_bytes, tm_cap, tn_cap):
    tm = _tile(M, tm_cap)
    try:
        tn = _tile(N, tn_cap, MXU_WIDTH)
    except ValueError:
        tn = _tile(N, tn_cap)
    fits = lambda tk: 2 * tk * (tm * a_bytes + tn * b_bytes) <= MM_OPERAND_BYTES
    tk = K if fits(K) else next(t for t in range(K // LANES * LANES, 0, -LANES) if K % t == 0 and fits(t))
    return tm, tn, tk


def _cparams(*sem):
    return pltpu.CompilerParams(dimension_semantics=sem, vmem_limit_bytes=VMEM_LIMIT)


def _own_refs(refs, comm, n_in, n_out, n_scratch):
    if comm is None:
        return list(refs), None
    return comm.split(refs, n_in, n_out, n_scratch)


def _comm_specs(comm, side):
    if comm is None:
        return []
    return [pl.BlockSpec(memory_space=pl.ANY)] * len(comm.ins if side == "in" else comm.out_shapes)


def _comm_edge(comm, comm_refs, grid, first):
    if comm is None:
        return
    at_edge = None
    for axis, n in enumerate(grid):
        here = pl.program_id(axis) == (0 if first else n - 1)
        at_edge = here if at_edge is None else at_edge & here
    pl.when(at_edge)(lambda: (comm.start if first else comm.finish)(*comm_refs))


def _mm(a, b, mode, out_dtype, name, tm_cap=512, tn_cap=2816, comm=None, col_pieces=1, twin=False):
    if mode == "nn":
        (M, K), (K2, N) = a.shape, b.shape
        dims = (((1,), (0,)), ((), ()))
    elif mode == "nt":
        (M, K), (N, K2) = a.shape, b.shape
        dims = (((1,), (1,)), ((), ()))
    else:
        (K, M), (K2, N) = a.shape, b.shape
        dims = (((0,), (0,)), ((), ()))
    assert K == K2, (a.shape, b.shape, mode)
    tm, tn, tk = _mm_tiles(M, N // col_pieces, K, a.dtype.itemsize, b.dtype.itemsize, tm_cap, tn_cap)
    nk = K // tk
    n_out = 2 if twin else 1
    n_scratch = 1 if nk > 1 else 0
    if mode == "nn":
        a_spec = pl.BlockSpec((tm, tk), lambda i, j, k: (i, k))
        b_spec = pl.BlockSpec((tk, tn), lambda i, j, k: (k, j))
    elif mode == "nt":
        a_spec = pl.BlockSpec((tm, tk), lambda i, j, k: (i, k))
        b_spec = pl.BlockSpec((tn, tk), lambda i, j, k: (j, k))
    else:
        a_spec = pl.BlockSpec((tk, tm), lambda i, j, k: (k, i))
        b_spec = pl.BlockSpec((tk, tn), lambda i, j, k: (k, j))

    grid = (M // tm, N // tn, nk)

    def kern(*refs):
        own, comm_refs = _own_refs(refs, comm, 2, n_out, n_scratch)
        a_ref, b_ref, o_refs = own[0], own[1], own[2:2 + n_out]
        k = pl.program_id(2)
        _comm_edge(comm, comm_refs, grid, first=True)
        part = lax.dot_general(a_ref[...].astype(BF), b_ref[...].astype(BF), dims,
                               preferred_element_type=F32)
        if nk == 1:
            for o_ref in o_refs:
                o_ref[...] = part.astype(o_ref.dtype)
        else:
            acc_ref = own[2 + n_out]

            @pl.when(k == 0)
            def _():
                acc_ref[...] = part

            @pl.when(k > 0)
            def _():
                acc_ref[...] += part

            @pl.when(k == nk - 1)
            def _():
                for o_ref in o_refs:
                    o_ref[...] = acc_ref[...].astype(o_ref.dtype)

        _comm_edge(comm, comm_refs, grid, first=False)

    if col_pieces > 1:
        per = N // col_pieces // tn
        out_spec = pl.BlockSpec((None, tm, tn), lambda i, j, k: (j // per, i, j % per))
        shape = (col_pieces, M, N // col_pieces)
    else:
        out_spec = pl.BlockSpec((tm, tn), lambda i, j, k: (i, j))
        shape = (M, N)
    dtypes = [out_dtype, BF] if twin else [out_dtype]
    res = pl.pallas_call(
        kern, name=name, grid=grid,
        in_specs=[a_spec, b_spec] + _comm_specs(comm, "in"),
        out_specs=[out_spec] * n_out + _comm_specs(comm, "out"),
        out_shape=[jax.ShapeDtypeStruct(shape, d) for d in dtypes] + (comm.out_shapes if comm else []),
        scratch_shapes=[pltpu.VMEM((tm, tn), F32)] * n_scratch + (comm.sem_shapes if comm else []),
        compiler_params=_cparams("parallel", "parallel", "arbitrary"),
    )(a, b, *(comm.ins if comm else []))
    own = res[0] if n_out == 1 else tuple(res[:n_out])
    return (own, res[n_out:]) if comm else own


ROWS = 512


def _row_spec(tm, width=D_MODEL, col=0):
    return pl.BlockSpec((tm, width), lambda i: (i, col))


def _vec_spec(width=D_MODEL):
    return pl.BlockSpec((1, width), lambda i: (0, 0))


def _rms(x):
    return lax.rsqrt(jnp.mean(x * x, axis=-1, keepdims=True) + RMS_EPS)


def _colsum(x):
    return jnp.sum(x, axis=0, keepdims=True)


def _norm_bwd(d_xn, xn, r):
    return r * (d_xn - xn * jnp.mean(d_xn * xn, axis=-1, keepdims=True))


def _pre_norm(x, g, scale, shift, name, comm=None):
    S = x.shape[0]
    tm = _tile(S, 2 * ROWS, 8)
    grid = (S // tm,)

    def kern(*refs):
        (x_ref, g_ref, sc_ref, sh_ref, h_ref), comm_refs = _own_refs(refs, comm, 4, 1, 0)
        _comm_edge(comm, comm_refs, grid, first=True)
        xf = x_ref[...]
        y = xf * _rms(xf) * g_ref[...]
        h_ref[...] = (y * (1.0 + sc_ref[...]) + sh_ref[...]).astype(BF)
        _comm_edge(comm, comm_refs, grid, first=False)

    res = pl.pallas_call(
        kern, name=name, grid=grid,
        in_specs=[_row_spec(tm), _vec_spec(), _vec_spec(), _vec_spec()] + _comm_specs(comm, "in"),
        out_specs=[_row_spec(tm)] + _comm_specs(comm, "out"),
        out_shape=[jax.ShapeDtypeStruct((S, D_MODEL), BF)] + (comm.out_shapes if comm else []),
        scratch_shapes=comm.sem_shapes if comm else [],
        compiler_params=_cparams("arbitrary"),
    )(x, g, scale, shift, *(comm.ins if comm else []))
    return res[0], res[1:]


def _post_pre(x, y1, g2, gate_m, g3, scale_f, shift_f):
    S = x.shape[0]
    tm = _tile(S, 2 * ROWS, 8)

    def kern(x_ref, y_ref, g2_ref, gm_ref, g3_ref, sc_ref, sh_ref, x2_ref, h2_ref):
        y = y_ref[...].astype(F32)
        n2 = y * _rms(y) * g2_ref[...]
        x2 = x_ref[...] + gm_ref[...] * n2
        x2_ref[...] = x2
        n3 = x2 * _rms(x2) * g3_ref[...]
        h2_ref[...] = (n3 * (1.0 + sc_ref[...]) + sh_ref[...]).astype(BF)

    return pl.pallas_call(
        kern, name="post_mix_pre_ffn", grid=(S // tm,),
        in_specs=[_row_spec(tm), _row_spec(tm)] + [_vec_spec()] * 5,
        out_specs=[_row_spec(tm), _row_spec(tm)],
        out_shape=[jax.ShapeDtypeStruct((S, D_MODEL), F32), jax.ShapeDtypeStruct((S, D_MODEL), BF)],
        compiler_params=_cparams("parallel"),
    )(x, y1, g2, gate_m, g3, scale_f, shift_f)


def _stats_spec():
    return pl.BlockSpec((8, D_MODEL), lambda i: (0, 0))


def _final(x2, y2, g4, gate_f, target):
    S = x2.shape[0]
    tm = _tile(S, ROWS, 8)

    def kern(x2_ref, y_ref, g4_ref, gf_ref, t_ref, dout_ref, dy_ref, st_ref):
        @pl.when(pl.program_id(0) == 0)
        def _():
            st_ref[...] = jnp.zeros_like(st_ref)

        y = y_ref[...].astype(F32)
        r = _rms(y)
        yn = y * r
        n4 = yn * g4_ref[...]
        diff = x2_ref[...] + gf_ref[...] * n4 - t_ref[...]
        d_out = diff / D_MODEL
        dout_ref[...] = d_out
        dn = d_out * gf_ref[...]
        dy_ref[...] = _norm_bwd(dn * g4_ref[...], yn, r).astype(BF)
        st_ref[0:1, :] += _colsum(d_out * n4)
        st_ref[1:2, :] += _colsum(dn * yn)
        st_ref[2:3, :] += _colsum(diff * diff)

    return pl.pallas_call(
        kern, name="final_loss", grid=(S // tm,),
        in_specs=[_row_spec(tm), _row_spec(tm), _vec_spec(), _vec_spec(), _row_spec(tm)],
        out_specs=[_row_spec(tm), _row_spec(tm), _stats_spec()],
        out_shape=[jax.ShapeDtypeStruct((S, D_MODEL), F32), jax.ShapeDtypeStruct((S, D_MODEL), BF),
                   jax.ShapeDtypeStruct((8, D_MODEL), F32)],
        compiler_params=_cparams("arbitrary"),
    )(x2, y2, g4, gate_f, target)


def _mid_bwd(d_h2, x2, d_out, y1, g3, scale_f, g2, gate_m):
    S = x2.shape[0]
    tm = _tile(S, ROWS, 8)

    def kern(dh_ref, x2_ref, dout_ref, y_ref, g3_ref, sc_ref, g2_ref, gm_ref, dx2_ref, dy_ref, st_ref):
        @pl.when(pl.program_id(0) == 0)
        def _():
            st_ref[...] = jnp.zeros_like(st_ref)

        dh = dh_ref[...].astype(F32)
        x2 = x2_ref[...]
        r3 = _rms(x2)
        xn = x2 * r3
        one_sc = 1.0 + sc_ref[...]
        d_x2 = dout_ref[...] + _norm_bwd(dh * one_sc * g3_ref[...], xn, r3)
        dx2_ref[...] = d_x2
        y = y_ref[...].astype(F32)
        r2 = _rms(y)
        yn = y * r2
        dn = d_x2 * gm_ref[...]
        dy_ref[...] = _norm_bwd(dn * g2_ref[...], yn, r2).astype(BF)
        st_ref[0:1, :] += _colsum(dh)
        st_ref[1:2, :] += _colsum(dh * (xn * g3_ref[...]))
        st_ref[2:3, :] += _colsum(dh * one_sc * xn)
        st_ref[3:4, :] += _colsum(d_x2 * (yn * g2_ref[...]))
        st_ref[4:5, :] += _colsum(dn * yn)

    return pl.pallas_call(
        kern, name="mid_bwd", grid=(S // tm,),
        in_specs=[_row_spec(tm)] * 4 + [_vec_spec()] * 4,
        out_specs=[_row_spec(tm), _row_spec(tm), _stats_spec()],
        out_shape=[jax.ShapeDtypeStruct((S, D_MODEL), F32), jax.ShapeDtypeStruct((S, D_MODEL), BF),
                   jax.ShapeDtypeStruct((8, D_MODEL), F32)],
        compiler_params=_cparams("arbitrary"),
    )(d_h2, x2, d_out, y1, g3, scale_f, g2, gate_m)


def _pre_bwd(d_h1, x, d_x2, g1, scale_m, comm=None):
    S = x.shape[0]
    tm = _tile(S, ROWS, 8)
    grid = (S // tm,)

    def kern(*refs):
        (dh_ref, x_ref, dx2_ref, g_ref, sc_ref, gx_ref, st_ref), comm_refs = _own_refs(refs, comm, 5, 2, 0)
        _comm_edge(comm, comm_refs, grid, first=True)

        @pl.when(pl.program_id(0) == 0)
        def _():
            st_ref[...] = jnp.zeros_like(st_ref)

        dh = dh_ref[...].astype(F32)
        xf = x_ref[...]
        r = _rms(xf)
        xn = xf * r
        one_sc = 1.0 + sc_ref[...]
        gx_ref[...] = dx2_ref[...] + _norm_bwd(dh * one_sc * g_ref[...], xn, r)
        st_ref[0:1, :] += _colsum(dh)
        st_ref[1:2, :] += _colsum(dh * (xn * g_ref[...]))
        st_ref[2:3, :] += _colsum(dh * one_sc * xn)
        _comm_edge(comm, comm_refs, grid, first=False)

    res = pl.pallas_call(
        kern, name="pre_mix_bwd", grid=grid,
        in_specs=[_row_spec(tm)] * 3 + [_vec_spec()] * 2 + _comm_specs(comm, "in"),
        out_specs=[_row_spec(tm), _stats_spec()] + _comm_specs(comm, "out"),
        out_shape=[jax.ShapeDtypeStruct((S, D_MODEL), F32), jax.ShapeDtypeStruct((8, D_MODEL), F32)]
        + (comm.out_shapes if comm else []),
        scratch_shapes=comm.sem_shapes if comm else [],
        input_output_aliases={5 + i: 2 + o for i, o in comm.aliases.items()} if comm else {},
        compiler_params=_cparams("arbitrary"),
    )(d_h1, x, d_x2, g1, scale_m, *(comm.ins if comm else []))
    return res[0], res[1], res[2:]


def _rope(xs, widths, cos_t, sin_t, name):
    S = xs[0].shape[0]
    tm = _tile(S, 512, 8)
    n = len(xs)

    def kern(*refs):
        cos = refs[n][...]
        sin = refs[n + 1][...]
        first = (lax.broadcasted_iota(jnp.int32, cos.shape, 1) % HEAD_DIM) < HEAD_DIM // 2
        for x_ref, o_ref, w in zip(refs[:n], refs[n + 2:], widths):
            for c0 in range(0, w, LANES):
                v = x_ref[:, c0:c0 + LANES]
                partner = jnp.where(first, pltpu.roll(v, LANES - HEAD_DIM // 2, 1),
                                    pltpu.roll(v, HEAD_DIM // 2, 1))
                o_ref[:, c0:c0 + LANES] = (v * cos + partner * sin).astype(BF)

    return pl.pallas_call(
        kern, name=name, grid=(S // tm,),
        in_specs=[_row_spec(tm, w) for w in widths] + [_row_spec(tm, LANES)] * 2,
        out_specs=[_row_spec(tm, w) for w in widths],
        out_shape=[jax.ShapeDtypeStruct((S, w), BF) for w in widths],
        compiler_params=_cparams("parallel"),
    )(*xs, cos_t, sin_t)


def _merge_fwd(pg, pa, pb):
    S = pa.shape[0]
    tm = _tile(S, 2 * ROWS, 8)

    def kern(ga_ref, gb_ref, pa_ref, pb_ref, o_ref):
        ga = jax.nn.sigmoid(ga_ref[...].astype(F32))
        gb = jax.nn.sigmoid(gb_ref[...].astype(F32))
        o_ref[...] = (ga * pa_ref[...].astype(F32) + gb * pb_ref[...].astype(F32)).astype(BF)

    return pl.pallas_call(
        kern, name="merge_fwd", grid=(S // tm,),
        in_specs=[_row_spec(tm, col=0), _row_spec(tm, col=1), _row_spec(tm), _row_spec(tm)],
        out_specs=_row_spec(tm),
        out_shape=jax.ShapeDtypeStruct((S, D_MODEL), BF),
        compiler_params=_cparams("parallel"),
    )(pg, pg, pa, pb)


def _merge_bwd(d_merged, pg, pa, pb):
    S = pa.shape[0]
    tm = _tile(S, ROWS, 8)

    def kern(dm_ref, ga_ref, gb_ref, pa_ref, pb_ref, dpa_ref, dpb_ref, dga_ref, dgb_ref):
        dm = dm_ref[...].astype(F32)
        ga = jax.nn.sigmoid(ga_ref[...].astype(F32))
        gb = jax.nn.sigmoid(gb_ref[...].astype(F32))
        dpa_ref[...] = (dm * ga).astype(BF)
        dpb_ref[...] = (dm * gb).astype(BF)
        dga_ref[...] = (dm * pa_ref[...].astype(F32) * ga * (1.0 - ga)).astype(BF)
        dgb_ref[...] = (dm * pb_ref[...].astype(F32) * gb * (1.0 - gb)).astype(BF)

    bf_out = jax.ShapeDtypeStruct((S, D_MODEL), BF)
    return pl.pallas_call(
        kern, name="merge_bwd", grid=(S // tm,),
        in_specs=[_row_spec(tm), _row_spec(tm, col=0), _row_spec(tm, col=1), _row_spec(tm), _row_spec(tm)],
        out_specs=[_row_spec(tm)] * 4,
        out_shape=[bf_out] * 4,
        compiler_params=_cparams("parallel"),
    )(d_merged, pg, pg, pa, pb)


def _swiglu_fwd(gu):
    S = gu.shape[0]
    tm = _tile(S, ROWS, 8)
    tc = _tile(D_FF, 1408)
    nc = D_FF // tc

    def kern(g_ref, u_ref, o_ref):
        g = g_ref[...].astype(F32)
        o_ref[...] = (g * jax.nn.sigmoid(g) * u_ref[...].astype(F32)).astype(BF)

    return pl.pallas_call(
        kern, name="swiglu_fwd", grid=(S // tm, nc),
        in_specs=[pl.BlockSpec((tm, tc), lambda i, j: (i, j)),
                  pl.BlockSpec((tm, tc), lambda i, j: (i, j + nc))],
        out_specs=pl.BlockSpec((tm, tc), lambda i, j: (i, j)),
        out_shape=jax.ShapeDtypeStruct((S, D_FF), BF),
        compiler_params=_cparams("parallel", "parallel"),
    )(gu, gu)


def _swiglu_bwd(d_act, gu):
    S = gu.shape[0]
    tm = _tile(S, ROWS // 2, 8)

    def kern(da_ref, g_ref, u_ref, o_ref):
        g = g_ref[...].astype(F32)
        u = u_ref[...].astype(F32)
        da = da_ref[...].astype(F32)
        sg = jax.nn.sigmoid(g)
        o_ref[:, :D_FF] = (da * u * (sg * (1.0 + g * (1.0 - sg)))).astype(BF)
        o_ref[:, D_FF:] = (da * (g * sg)).astype(BF)

    return pl.pallas_call(
        kern, name="swiglu_bwd", grid=(S // tm,),
        in_specs=[_row_spec(tm, D_FF), _row_spec(tm, D_FF, 0), _row_spec(tm, D_FF, 1)],
        out_specs=_row_spec(tm, 2 * D_FF),
        out_shape=jax.ShapeDtypeStruct((S, 2 * D_FF), BF),
        compiler_params=_cparams("parallel"),
    )(d_act, gu, gu)


def _split3(x):
    hi = x.astype(BF)
    r1 = x - hi.astype(F32)
    mid = r1.astype(BF)
    lo = (r1 - mid.astype(F32)).astype(BF)
    return hi, mid, lo


def _tri_dot(tri, x):
    return sum(jnp.dot(tri, part, preferred_element_type=F32) for part in _split3(x))


def _log_sigmoid(z):
    return jnp.minimum(z, 0.0) - jnp.log(1.0 + jnp.exp(-jnp.abs(z)))


def _fox_gate_fwd(pa, b_f_pad):
    S = pa.shape[0]
    T = _tile(S, 512, 8)
    f_col = OFF_F // LANES

    def kern(z_ref, b_ref, cum_ref, carry_ref):
        @pl.when(pl.program_id(0) == 0)
        def _():
            carry_ref[...] = jnp.zeros_like(carry_ref)

        log_f = _log_sigmoid(z_ref[...] + b_ref[...])
        row = lax.broadcasted_iota(jnp.int32, (T, T), 0)
        col = lax.broadcasted_iota(jnp.int32, (T, T), 1)
        tri = (col <= row).astype(BF)
        cum = _tri_dot(tri, log_f) + carry_ref[...]
        cum_ref[...] = cum
        carry_ref[...] = cum[T - 1:T, :]

    return pl.pallas_call(
        kern, name="fox_gate_fwd", grid=(S // T,),
        in_specs=[_row_spec(T, LANES, f_col), _vec_spec(LANES)],
        out_specs=_row_spec(T, LANES),
        out_shape=jax.ShapeDtypeStruct((S, LANES), F32),
        scratch_shapes=[pltpu.VMEM((1, LANES), F32)],
        compiler_params=_cparams("arbitrary"),
    )(pa, b_f_pad)


def _fox_gate_bwd(rowsum_ds, colsum_ds, pa, b_f_pad):
    S = pa.shape[0]
    T = _tile(S, 512, 8)
    nb = S // T
    f_col = OFF_F // LANES

    def kern(dr_ref, dc_ref, z_ref, b_ref, df_ref, dbf_ref, carry_ref):
        @pl.when(pl.program_id(0) == 0)
        def _():
            carry_ref[...] = jnp.zeros_like(carry_ref)
            dbf_ref[...] = jnp.zeros_like(dbf_ref)

        row = lax.broadcasted_iota(jnp.int32, (T, T), 0)
        col = lax.broadcasted_iota(jnp.int32, (T, T), 1)
        tri = (col >= row).astype(BF)
        rev = _tri_dot(tri, dr_ref[...] - dc_ref[...]) + carry_ref[...]
        carry_ref[...] = rev[0:1, :]
        z = z_ref[...] + b_ref[...]
        lane = lax.broadcasted_iota(jnp.int32, (T, LANES), 1)
        d_z = jnp.where(lane < B_HEADS, rev * jax.nn.sigmoid(-z), 0.0)
        df_ref[...] = d_z.astype(BF)
        dbf_ref[0:1, :] += _colsum(d_z)

    return pl.pallas_call(
        kern, name="fox_gate_bwd", grid=(nb,),
        in_specs=[pl.BlockSpec((T, LANES), lambda i: (nb - 1 - i, 0)),
                  pl.BlockSpec((T, LANES), lambda i: (nb - 1 - i, 0)),
                  pl.BlockSpec((T, LANES), lambda i: (nb - 1 - i, f_col)),
                  _vec_spec(LANES)],
        out_specs=[pl.BlockSpec((T, LANES), lambda i: (nb - 1 - i, 0)),
                   pl.BlockSpec((8, LANES), lambda i: (0, 0))],
        out_shape=[jax.ShapeDtypeStruct((S, LANES), BF), jax.ShapeDtypeStruct((8, LANES), F32)],
        scratch_shapes=[pltpu.VMEM((1, LANES), F32)],
        compiler_params=_cparams("arbitrary"),
    )(rowsum_ds, colsum_ds, pa, b_f_pad)


NEG_INF = float("-inf")
QK_SCALE = 1.0 / math.sqrt(HEAD_DIM)


def _half_mask(shape, half):
    lane = lax.broadcasted_iota(jnp.int32, shape, 1)
    return (lane < HEAD_DIM) if half == 0 else (lane >= HEAD_DIM)


def _bias_block(shape, terms, term_off, ones_lo, ones_hi):
    l64 = lax.broadcasted_iota(jnp.int32, shape, 1) & (HEAD_DIM - 1)
    out = jnp.where((l64 >= ones_lo) & (l64 < ones_hi), 1.0, 0.0)
    for t, term in enumerate(terms):
        out = jnp.where(l64 == term_off + t, term.astype(F32), out)
    return out


def _head_column(block, head):
    lane = lax.broadcasted_iota(jnp.int32, block.shape, 1)
    return jnp.sum(jnp.where(lane == head, block, 0.0), axis=1, keepdims=True)


def _crossed(shape, first, second):
    return jnp.where(_half_mask(shape, 0), second, first)


def _fox_prep_fwd(cum, T):
    S = cum.shape[0]
    shape = (T, LANES)

    def kern(c_ref, bq_ref, bk_ref):
        p_id = pl.program_id(0)
        cum_blk = c_ref[...]
        c3 = _split3(_crossed(shape, _head_column(cum_blk, 2 * p_id), _head_column(cum_blk, 2 * p_id + 1)))
        bq_ref[...] = _bias_block(shape, c3, 0, 3, 6).astype(BF)
        bk_ref[...] = _bias_block(shape, [-t.astype(F32) for t in c3], 3, 0, 3).astype(BF)

    out_spec = pl.BlockSpec((None, T, LANES), lambda p, i: (p, i, 0))
    out_shape = jax.ShapeDtypeStruct((B_HEADS // 2, S, LANES), BF)
    return pl.pallas_call(
        kern, name="fox_prep_fwd", grid=(B_HEADS // 2, S // T),
        in_specs=[pl.BlockSpec((T, LANES), lambda p, i: (i, 0))],
        out_specs=[out_spec, out_spec], out_shape=[out_shape, out_shape],
        compiler_params=_cparams("parallel", "parallel"),
    )(cum)


def _fox_fwd(p_b, bq, bk, T, comm=None):
    S = p_b.shape[0]
    nq = S // T
    n_pairs = B_HEADS // 2
    grid = (n_pairs, nq)

    def kern(*refs):
        (q_ref, k_ref, v_ref, bq_ref, bk_ref, o_ref, lse_ref), comm_refs = _own_refs(refs, comm, 5, 2, 0)
        _comm_edge(comm, comm_refs, grid, first=True)
        i = pl.program_id(1)
        rowcol = lax.broadcasted_iota(jnp.int32, (T, T), 0) - lax.broadcasted_iota(jnp.int32, (T, T), 1)
        hms = (_half_mask((T, LANES), 0), _half_mask((T, LANES), 1))
        q_scaled = (q_ref[...].astype(F32) * QK_SCALE).astype(BF)
        bq_blk = bq_ref[...]
        qs = [jnp.where(hms[h], q_scaled, bq_blk) for h in (0, 1)]

        def step(j, carry, masked):
            rows = pl.ds(pl.multiple_of(j * T, T), T)
            kj, bkj, vj = k_ref[rows, :], bk_ref[rows, :], v_ref[rows, :]
            new = []
            for half in (0, 1):
                m, l, acc = carry[half]
                s = lax.dot_general(qs[half], jnp.where(hms[half], kj, bkj), (((1,), (1,)), ((), ())),
                                    preferred_element_type=F32)
                if masked:
                    s = jnp.where(rowcol >= 0, s, NEG_INF)
                m_new = jnp.maximum(m, jnp.max(s, axis=1, keepdims=True))
                alpha = jnp.exp(m - m_new)
                p = jnp.exp(s - m_new)
                l_new = alpha * l + jnp.sum(p, axis=1, keepdims=True)
                acc_new = alpha * acc + jnp.dot(p.astype(BF), vj, preferred_element_type=F32)
                new.append((m_new, l_new, acc_new))
            return tuple(new)

        one = (jnp.full((T, 1), NEG_INF, F32), jnp.zeros((T, 1), F32), jnp.zeros((T, LANES), F32))
        carry = lax.fori_loop(0, i, functools.partial(step, masked=False), (one, one))
        (m0, l0, acc0), (m1, l1, acc1) = step(i, carry, True)
        hm0 = _half_mask((T, LANES), 0)
        o_ref[...] = jnp.where(hm0, acc0 / l0, acc1 / l1)
        lse_ref[...] = jnp.where(hm0, m0 + jnp.log(l0), m1 + jnp.log(l1))
        _comm_edge(comm, comm_refs, grid, first=False)

    out_spec = pl.BlockSpec((T, LANES), lambda p, i: (i, p))
    res = pl.pallas_call(
        kern, name="fox_fwd", grid=grid,
        in_specs=[pl.BlockSpec((T, LANES), lambda p, i: (i, OFF_QB // LANES + p)),
                  pl.BlockSpec((S, LANES), lambda p, i: (0, OFF_KB // LANES + p)),
                  pl.BlockSpec((S, LANES), lambda p, i: (0, OFF_VB // LANES + p)),
                  pl.BlockSpec((None, T, LANES), lambda p, i: (p, i, 0)),
                  pl.BlockSpec((None, S, LANES), lambda p, i: (p, 0, 0))] + _comm_specs(comm, "in"),
        out_specs=[out_spec, out_spec] + _comm_specs(comm, "out"),
        out_shape=[jax.ShapeDtypeStruct((S, n_pairs * LANES), F32)] * 2 + (comm.out_shapes if comm else []),
        scratch_shapes=comm.sem_shapes if comm else [],
        compiler_params=_cparams("arbitrary", "arbitrary"),
    )(p_b, p_b, p_b, bq, bk, *(comm.ins if comm else []))
    return res[0], res[1], res[2:]


def _fox_prep_bwd(cum, o, do, lse, T):
    S = o.shape[0]
    shape = (T, LANES)

    def kern(c_ref, o_ref, do_ref, lse_ref, bq_ref, bdo_ref):
        p_id = pl.program_id(0)
        cum_blk = c_ref[...]
        cq = _crossed(shape, _head_column(cum_blk, 2 * p_id), _head_column(cum_blk, 2 * p_id + 1))
        b3 = _split3(cq - pltpu.roll(lse_ref[...], HEAD_DIM, 1))
        bq_ref[...] = _bias_block(shape, b3, 0, 3, 6).astype(BF)
        dd = do_ref[...] * o_ref[...]
        delta = [jnp.sum(jnp.where(_half_mask(shape, h), dd, 0.0), axis=1, keepdims=True) for h in (0, 1)]
        d3 = _split3(-_crossed(shape, delta[0], delta[1]))
        bdo_ref[...] = _bias_block(shape, d3, 0, 0, 0).astype(BF)

    block = pl.BlockSpec((None, T, LANES), lambda p, i: (p, i, 0))
    tile = pl.BlockSpec((T, LANES), lambda p, i: (i, p))
    out_shape = jax.ShapeDtypeStruct((B_HEADS // 2, S, LANES), BF)
    return pl.pallas_call(
        kern, name="fox_prep_bwd", grid=(B_HEADS // 2, S // T),
        in_specs=[pl.BlockSpec((T, LANES), lambda p, i: (i, 0)), tile, tile, tile],
        out_specs=[block, block], out_shape=[out_shape, out_shape],
        compiler_params=_cparams("parallel", "parallel"),
    )(cum, o, do, lse)


def _fox_bwd(p_b, do, bq, bk, bdo, T, comm=None):
    S = p_b.shape[0]
    n_pairs = B_HEADS // 2
    nq = S // T
    grid = (n_pairs,)

    def kern(*refs):
        own, comm_refs = _own_refs(refs, comm, 7, 5, 0)
        q_ref, k_ref, v_ref, do_ref, bq_ref, bk_ref, bdo_ref, dq_ref, dk_ref, dv_ref, dck_ref, dcq_ref = own
        _comm_edge(comm, comm_refs, grid, first=True)
        p_id = pl.program_id(0)
        rowcol = lax.broadcasted_iota(jnp.int32, (T, T), 0) - lax.broadcasted_iota(jnp.int32, (T, T), 1)
        lane = lax.broadcasted_iota(jnp.int32, (T, LANES), 1)
        dk_ref[...] = jnp.zeros_like(dk_ref)
        dv_ref[...] = jnp.zeros_like(dv_ref)
        dck_ref[...] = jnp.zeros_like(dck_ref)

        @pl.when(p_id == 0)
        def _():
            dcq_ref[...] = jnp.zeros_like(dcq_ref)

        hms = (_half_mask((T, LANES), 0), _half_mask((T, LANES), 1))
        v_ones = _bias_block((T, LANES), [], 0, 0, 3).astype(BF)

        def outer(i, carry):
            qrows = pl.ds(pl.multiple_of(i * T, T), T)
            q_scaled = (q_ref[qrows, :].astype(F32) * QK_SCALE).astype(BF)
            do_b = do_ref[qrows, :].astype(BF)
            bq_i, bdo_i = bq_ref[qrows, :], bdo_ref[qrows, :]
            qa = [jnp.where(hms[h], q_scaled, bq_i) for h in (0, 1)]
            doa = [jnp.where(hms[h], do_b, bdo_i) for h in (0, 1)]
            q_own = [jnp.where(hms[h], q_scaled, 0) for h in (0, 1)]
            do_own = [jnp.where(hms[h], do_b, 0) for h in (0, 1)]

            def inner(j, carry_in, masked):
                krows = pl.ds(pl.multiple_of(j * T, T), T)
                kj, bkj, vj = k_ref[krows, :], bk_ref[krows, :], v_ref[krows, :]
                dv_add, dk_add, new = 0.0, 0.0, []
                for half in (0, 1):
                    dq, rs = carry_in[half]
                    ka = jnp.where(hms[half], kj, bkj)
                    s = lax.dot_general(qa[half], ka, (((1,), (1,)), ((), ())), preferred_element_type=F32)
                    if masked:
                        s = jnp.where(rowcol >= 0, s, NEG_INF)
                    p = jnp.exp(s)
                    ds = p * lax.dot_general(doa[half], jnp.where(hms[half], vj, v_ones),
                                             (((1,), (1,)), ((), ())), preferred_element_type=F32)
                    ds_b = ds.astype(BF)
                    dv_add = dv_add + lax.dot_general(p.astype(BF), do_own[half], (((0,), (0,)), ((), ())),
                                                      preferred_element_type=F32)
                    dk_add = dk_add + lax.dot_general(ds_b, q_own[half], (((0,), (0,)), ((), ())),
                                                      preferred_element_type=F32)
                    dck_ref[half:half + 1, krows] += jnp.sum(ds, axis=0, keepdims=True)
                    new.append((dq + jnp.dot(ds_b, jnp.where(hms[half], kj, 0), preferred_element_type=F32),
                                rs + jnp.sum(ds, axis=1, keepdims=True)))
                dv_ref[krows, :] += dv_add
                dk_ref[krows, :] += dk_add
                return tuple(new)

            one = (jnp.zeros((T, LANES), F32), jnp.zeros((T, 1), F32))
            carry_in = lax.fori_loop(0, i, functools.partial(inner, masked=False), (one, one))
            (dq0, rs0), (dq1, rs1) = inner(i, carry_in, True)
            dq_ref[qrows, :] = (dq0 + dq1) * QK_SCALE
            dcq_ref[qrows, :] = jnp.where(lane == 2 * p_id, rs0, jnp.where(lane == 2 * p_id + 1, rs1,
                                                                             dcq_ref[qrows, :]))
            return carry

        lax.fori_loop(0, nq, outer, 0)
        _comm_edge(comm, comm_refs, grid, first=False)

    block = pl.BlockSpec((None, S, LANES), lambda p: (p, 0, 0))
    pair = pl.BlockSpec((S, LANES), lambda p: (0, p))
    slab = lambda off: pl.BlockSpec((S, LANES), lambda p: (0, off // LANES + p))
    wide = jax.ShapeDtypeStruct((S, n_pairs * LANES), F32)
    res = pl.pallas_call(
        kern, name="fox_bwd", grid=grid,
        in_specs=[slab(OFF_QB), slab(OFF_KB), slab(OFF_VB), pair, block, block, block]
        + _comm_specs(comm, "in"),
        out_specs=[pair, pair, pair, pl.BlockSpec((None, 2, S), lambda p: (p, 0, 0)),
                   pl.BlockSpec((S, LANES), lambda p: (0, 0))] + _comm_specs(comm, "out"),
        out_shape=[wide, wide, wide, jax.ShapeDtypeStruct((n_pairs, 2, S), F32),
                   jax.ShapeDtypeStruct((S, LANES), F32)] + (comm.out_shapes if comm else []),
        scratch_shapes=comm.sem_shapes if comm else [],
        compiler_params=_cparams("arbitrary"),
    )(p_b, p_b, p_b, do, bq, bk, bdo, *(comm.ins if comm else []))
    return (*res[:5], res[5:])


SWA_TQ = 128
SWA_SUB = 32


def _swa_window(i, tq):
    start = pl.multiple_of(jnp.maximum(i * tq - WINDOW, 0), LANES)
    return start, i * tq - start


def _swa_valid(offset, tq):
    rel = offset + lax.broadcasted_iota(jnp.int32, (tq, tq + WINDOW), 0) \
        - lax.broadcasted_iota(jnp.int32, (tq, tq + WINDOW), 1)
    return (rel >= 0) & (rel < WINDOW)


def _swa_fwd(qk, v_arr, v_col, sinks):
    S = qk.shape[0]
    tq = min(SWA_TQ, S - WINDOW)
    sub = min(SWA_SUB, S // tq)
    win = tq + WINDOW

    def kern(q_ref, k_ref, v_ref, sink_ref, o_ref, lse_ref):
        p_id, i = pl.program_id(0), pl.program_id(1)
        hm0 = _half_mask((tq, LANES), 0)
        for t in range(sub):
            rows = slice(t * tq, (t + 1) * tq)
            start, offset = _swa_window(i * sub + t, tq)
            kw = k_ref[pl.ds(start, win), :]
            vw = v_ref[pl.ds(start, win), :].astype(BF)
            valid = _swa_valid(offset, tq)
            q = q_ref[rows, :]
            outs, lses = [], []
            for half in (0, 1):
                hm = _half_mask((tq, LANES), half)
                qh = (jnp.where(hm, q, 0).astype(F32) * QK_SCALE).astype(BF)
                s = lax.dot_general(qh, kw, (((1,), (1,)), ((), ())), preferred_element_type=F32)
                s = jnp.where(valid, s, NEG_INF)
                sink = sink_ref[2 * p_id + half]
                m = jnp.maximum(jnp.max(s, axis=1, keepdims=True), sink)
                p = jnp.exp(s - m)
                denom = jnp.sum(p, axis=1, keepdims=True) + jnp.exp(sink - m)
                outs.append(jnp.dot(p.astype(BF), vw, preferred_element_type=F32) / denom)
                lses.append(m + jnp.log(denom))
            o_ref[rows, :] = jnp.where(hm0, outs[0], outs[1])
            lse_ref[rows, :] = jnp.where(hm0, lses[0], lses[1])

    tile = pl.BlockSpec((sub * tq, LANES), lambda p, i: (i, p))
    return pl.pallas_call(
        kern, name="swa_fwd", grid=(A_Q_HEADS // 2, S // (sub * tq)),
        in_specs=[tile, pl.BlockSpec((S, LANES), lambda p, i: (0, A_Q_HEADS // 2)),
                  pl.BlockSpec((S, LANES), lambda p, i: (0, v_col)),
                  pl.BlockSpec(memory_space=pltpu.SMEM)],
        out_specs=[tile, tile],
        out_shape=[jax.ShapeDtypeStruct((S, A_Q_HEADS * HEAD_DIM), F32)] * 2,
        compiler_params=_cparams("parallel", "arbitrary"),
    )(qk, qk, v_arr, sinks)


def _swa_bwd(qk, v_arr, v_col, o_arr, do_arr, lse_arr, sinks, comm=None):
    S = qk.shape[0]
    tq = min(SWA_TQ, S - WINDOW)
    sub = min(SWA_SUB, S // tq)
    win = tq + WINDOW
    n_pairs = A_Q_HEADS // 2
    grid = (n_pairs, S // (sub * tq))

    def kern(*refs):
        own, comm_refs = _own_refs(refs, comm, 7, 4, 0)
        q_ref, k_ref, v_ref, o_ref, do_ref, lse_ref, sink_ref, dq_ref, dk_ref, dv_ref, dsink_ref = own
        _comm_edge(comm, comm_refs, grid, first=True)
        p_id, i = pl.program_id(0), pl.program_id(1)

        @pl.when((p_id == 0) & (i == 0))
        def _():
            dk_ref[...] = jnp.zeros_like(dk_ref)
            dv_ref[...] = jnp.zeros_like(dv_ref)

        @pl.when(i == 0)
        def _():
            dsink_ref[...] = jnp.zeros_like(dsink_ref)

        for t in range(sub):
            rows = slice(t * tq, (t + 1) * tq)
            start, offset = _swa_window(i * sub + t, tq)
            wrows = pl.ds(start, win)
            kw = k_ref[wrows, :]
            vw = v_ref[wrows, :].astype(BF)
            valid = _swa_valid(offset, tq)
            q, do, o, lse2 = q_ref[rows, :], do_ref[rows, :], o_ref[rows, :], lse_ref[rows, :]
            dq = jnp.zeros((tq, LANES), F32)
            dk = jnp.zeros((win, LANES), F32)
            dv = jnp.zeros((win, LANES), F32)
            for half in (0, 1):
                hm = _half_mask((tq, LANES), half)
                lane0 = half * HEAD_DIM
                qh = (jnp.where(hm, q, 0).astype(F32) * QK_SCALE).astype(BF)
                do_f = jnp.where(hm, do, 0.0)
                doh = do_f.astype(BF)
                delta = jnp.sum(do_f * o, axis=1, keepdims=True)
                lse = lse2[:, lane0:lane0 + 1]
                s = lax.dot_general(qh, kw, (((1,), (1,)), ((), ())), preferred_element_type=F32)
                p = jnp.exp(jnp.where(valid, s, NEG_INF) - lse)
                dp = lax.dot_general(doh, vw, (((1,), (1,)), ((), ())), preferred_element_type=F32)
                ds_b = (p * (dp - delta)).astype(BF)
                dv = dv + lax.dot_general(p.astype(BF), doh, (((0,), (0,)), ((), ())),
                                          preferred_element_type=F32)
                dk = dk + lax.dot_general(ds_b, qh, (((0,), (0,)), ((), ())), preferred_element_type=F32)
                kh = jnp.where(_half_mask((win, LANES), half), kw, 0)
                dq = dq + jnp.dot(ds_b, kh, preferred_element_type=F32)
                p_sink = jnp.exp(sink_ref[2 * p_id + half] - lse)
                dsink_ref[0, half:half + 1, :] += jnp.broadcast_to(
                    -jnp.sum(p_sink * delta, axis=0, keepdims=True), (1, LANES))
            dq_ref[rows, :] = dq * QK_SCALE
            dk_ref[wrows, :] += dk
            dv_ref[wrows, :] += dv
        _comm_edge(comm, comm_refs, grid, first=False)

    tile = pl.BlockSpec((sub * tq, LANES), lambda p, i: (i, p))
    whole = lambda col: pl.BlockSpec((S, LANES), lambda p, i: (0, col))
    res = pl.pallas_call(
        kern, name="swa_bwd", grid=grid,
        in_specs=[tile, whole(n_pairs), whole(v_col), tile, tile, tile,
                  pl.BlockSpec(memory_space=pltpu.SMEM)] + _comm_specs(comm, "in"),
        out_specs=[tile, whole(0), whole(0),
                   pl.BlockSpec((1, 8, LANES), lambda p, i: (p, 0, 0))] + _comm_specs(comm, "out"),
        out_shape=[jax.ShapeDtypeStruct((S, A_Q_HEADS * HEAD_DIM), F32),
                   jax.ShapeDtypeStruct((S, LANES), F32), jax.ShapeDtypeStruct((S, LANES), F32),
                   jax.ShapeDtypeStruct((n_pairs, 8, LANES), F32)] + (comm.out_shapes if comm else []),
        scratch_shapes=comm.sem_shapes if comm else [],
        compiler_params=_cparams("arbitrary", "arbitrary"),
    )(qk, qk, v_arr, o_arr, do_arr, lse_arr, sinks, *(comm.ins if comm else []))
    return (*res[:4], res[4:])


ADAMW_BLOCK = 512 * 1024


def _adamw(w, g, m, v, name, comm=None):
    R, C = w.shape
    tr, tc = _tile(R, max(8, ADAMW_BLOCK // C), 8), C
    grid = (R // tr, C // tc)

    def kern(*refs):
        (w_ref, g_ref, m_ref, v_ref, d_ref, mo_ref, vo_ref), comm_refs = _own_refs(refs, comm, 4, 3, 0)
        _comm_edge(comm, comm_refs, grid, first=True)
        g_ = g_ref[...]
        m_new = ADAM_B1 * m_ref[...] + (1.0 - ADAM_B1) * g_
        v_new = ADAM_B2 * v_ref[...] + (1.0 - ADAM_B2) * (g_ * g_)
        m_hat = m_new / (1.0 - ADAM_B1 ** ADAM_STEP)
        v_hat = v_new / (1.0 - ADAM_B2 ** ADAM_STEP)
        d_ref[...] = -ADAM_LR * (m_hat / (jnp.sqrt(v_hat) + ADAM_EPS) + ADAM_WD * w_ref[...])
        mo_ref[...] = m_new
        vo_ref[...] = v_new
        _comm_edge(comm, comm_refs, grid, first=False)

    spec = pl.BlockSpec((tr, tc), lambda i, j: (i, j))
    shape = jax.ShapeDtypeStruct((R, C), F32)
    res = pl.pallas_call(
        kern, name=name, grid=grid,
        in_specs=[spec] * 4 + _comm_specs(comm, "in"),
        out_specs=[spec] * 3 + _comm_specs(comm, "out"),
        out_shape=[shape] * 3 + (comm.out_shapes if comm else []),
        scratch_shapes=comm.sem_shapes if comm else [],
        input_output_aliases={4 + i: 3 + o for i, o in comm.aliases.items()} if comm else {},
        compiler_params=_cparams("arbitrary", "arbitrary"),
    )(w, g, m, v, *(comm.ins if comm else []))
    return (res[:3], res[3:]) if comm else res


def _index_operand(i):
    return jnp.reshape(i, (1,)).astype(jnp.int32)


def _add_pair(whole, got, ci, name):
    P, R, C = whole.shape
    half = R // 2
    tr = _tile(half, ROWS, 16)
    nb = half // tr

    def kern(ci_ref, a_ref, b_ref, o_ref, ob_ref):
        s = a_ref[...] + b_ref[...].astype(F32)
        o_ref[...] = s
        ob_ref[...] = s.astype(BF)

    spec = pl.BlockSpec((None, tr, C), lambda p, i, ci_ref: (p, i, 0))
    return pl.pallas_call(
        kern, name=name,
        grid_spec=pltpu.PrefetchScalarGridSpec(
            num_scalar_prefetch=1, grid=(P, nb),
            in_specs=[pl.BlockSpec((None, tr, C), lambda p, i, ci_ref: (p, ci_ref[0] * nb + i, 0)), spec],
            out_specs=[spec, spec]),
        out_shape=[jax.ShapeDtypeStruct((P, half, C), F32), jax.ShapeDtypeStruct((P, half, C), BF)],
        compiler_params=_cparams("parallel", "parallel"),
    )(_index_operand(ci), whole, got)


def _add_three(parts, recv, chip, name):
    _, R, C = parts.shape
    tr = _tile(R, ROWS, 16)

    def kern(chip_ref, o_ref, r0_ref, r1_ref, r2_ref, out_ref):
        s = ((o_ref[...] + r0_ref[...].astype(F32)) + r1_ref[...].astype(F32)) + r2_ref[...].astype(F32)
        out_ref[0] = s
        out_ref[1] = s

    slab = lambda k: pl.BlockSpec((None, tr, C), lambda i, chip_ref: (k, i, 0))
    return pl.pallas_call(
        kern, name=name,
        grid_spec=pltpu.PrefetchScalarGridSpec(
            num_scalar_prefetch=1, grid=(R // tr,),
            in_specs=[pl.BlockSpec((None, tr, C), lambda i, chip_ref: (chip_ref[0], i, 0)),
                      slab(0), slab(1), slab(2)],
            out_specs=pl.BlockSpec((2, tr, C), lambda i, chip_ref: (0, i, 0))),
        out_shape=jax.ShapeDtypeStruct((2, R, C), F32),
        compiler_params=_cparams("parallel"),
    )(_index_operand(chip), parts, recv, recv, recv)


SM_ADA, SM_G, SM_LOSS, SM_BF, SM_SINK, SM_LEN = 0, 6144, 10240, 11264, 11272, 12288


def _small_finalize(gathered):
    def kern(g_ref, tot_ref, loss_ref):
        tot = g_ref[0:1, :]
        for b in range(1, N_DEV):
            tot = tot + g_ref[b:b + 1, :]
        tot_ref[...] = tot
        sq = jnp.sum(tot[:, SM_LOSS:SM_LOSS + D_MODEL], axis=1, keepdims=True)
        loss_ref[...] = jnp.broadcast_to(sq * (0.5 / D_MODEL), (1, LANES))

    full = lambda shape: pl.BlockSpec(shape, lambda i: (0, 0))
    return pl.pallas_call(
        kern, name="small_finalize", grid=(1,),
        in_specs=[full((N_DEV, SM_LEN))],
        out_specs=[full((1, SM_LEN)), full((1, LANES))],
        out_shape=[jax.ShapeDtypeStruct((1, SM_LEN), F32), jax.ShapeDtypeStruct((1, LANES), F32)],
        compiler_params=_cparams("arbitrary"),
    )(gathered)


def _ada_dw(c_t, d_ada):
    N = d_ada.shape[1]
    tn = _tile(N, 512)

    def kern(c_ref, d_ref, o_ref):
        acc = c_ref[:, 0:1] * d_ref[0:1, :]
        for b in range(1, N_DEV):
            acc = acc + c_ref[:, b:b + 1] * d_ref[b:b + 1, :]
        o_ref[...] = acc

    return pl.pallas_call(
        kern, name="ada_dw", grid=(N // tn,),
        in_specs=[pl.BlockSpec((D_MODEL, N_DEV), lambda j: (0, 0)), pl.BlockSpec((N_DEV, tn), lambda j: (0, j))],
        out_specs=pl.BlockSpec((D_MODEL, tn), lambda j: (0, j)),
        out_shape=jax.ShapeDtypeStruct((D_MODEL, N), F32),
        compiler_params=_cparams("parallel"),
    )(c_t, d_ada)


def _here():
    return lax.axis_index("x"), lax.axis_index("y"), lax.axis_index("c")


def _other_chips(x, y):
    return [(1 - x, y), (x, 1 - y), (1 - x, 1 - y)]


_ANY = pl.BlockSpec(memory_space=pl.ANY)


class _Comm:
    def __init__(self, ins, out_shapes, sem_shapes, start, finish, aliases=None):
        self.ins, self.out_shapes, self.sem_shapes = list(ins), list(out_shapes), list(sem_shapes)
        self.start, self.finish = start, finish
        self.aliases = dict(aliases or {})

    def split(self, refs, n_in, n_out, n_scratch):
        a = n_in + len(self.ins)
        b = a + n_out + len(self.out_shapes)
        own = list(refs[:n_in]) + list(refs[a:a + n_out]) + list(refs[b:b + n_scratch])
        mine = (refs[n_in:a], refs[a + n_out:b], refs[b + n_scratch:])
        return own, mine


def _run_comm(comm, name):
    n_in, n_out = len(comm.ins), len(comm.out_shapes)

    def body(*refs):
        parts = (refs[:n_in], refs[n_in:n_in + n_out], refs[n_in + n_out:])
        comm.start(*parts)
        comm.finish(*parts)

    return pl.pallas_call(
        body, name=name,
        in_specs=[_ANY] * n_in, out_specs=[_ANY] * n_out,
        out_shape=comm.out_shapes, scratch_shapes=comm.sem_shapes,
        input_output_aliases=comm.aliases,
    )(*comm.ins)


def _gather_comm(blocks):
    L = len(blocks)

    def parts(ins, outs, sems):
        send_sems, recv_sems, local_sems = sems
        x, y, c = _here()
        me, sibling = (x, y, c), (x, y, 1 - c)
        chips = _other_chips(x, y)

        def slot(px, py, pc):
            return 4 * px + 2 * py + pc

        def copy(l, k, block, to, src=None):
            dst = outs[l].at[slot(*block)]
            return pltpu.make_async_remote_copy(
                src_ref=dst if src is None else src, dst_ref=dst,
                send_sem=send_sems.at[l, k], recv_sem=recv_sems.at[l, k],
                device_id=to, device_id_type=MESH)

        mine = [pltpu.make_async_copy(ins[l], outs[l].at[slot(*me)], local_sems.at[l]) for l in range(L)]
        first = []
        for l in range(L):
            first.append(copy(l, 0, me, sibling, src=ins[l]))
            for j, chip in enumerate(chips):
                first.append(copy(l, 1 + j, me, (*chip, c), src=ins[l]))
        return c, me, sibling, chips, copy, mine, first

    def start(ins, outs, sems):
        *_, mine, first = parts(ins, outs, sems)
        for cp in mine + first:
            cp.start()

    def finish(ins, outs, sems):
        c, me, sibling, chips, copy, mine, first = parts(ins, outs, sems)
        passed = []
        for j, chip in enumerate(chips):
            for l in range(L):
                copy(l, 1 + j, (*chip, c), me).wait_recv()
                fwd = copy(l, 4 + j, (*chip, c), sibling)
                fwd.start()
                passed.append(fwd)
        for l in range(L):
            copy(l, 0, sibling, me).wait_recv()
        for j, chip in enumerate(chips):
            for l in range(L):
                copy(l, 4 + j, (*chip, 1 - c), me).wait_recv()
        for cp in first + passed:
            cp.wait_send()
        for cp in mine:
            cp.wait()

    return _Comm(blocks, [jax.ShapeDtypeStruct((N_DEV,) + b.shape, b.dtype) for b in blocks],
                 [pltpu.SemaphoreType.DMA((L, 7)), pltpu.SemaphoreType.DMA((L, 7)), pltpu.SemaphoreType.DMA((L,))],
                 start, finish)


def _allgather8(blocks, name):
    return _run_comm(_gather_comm(blocks), name)


def _swap_comm(arrs):
    L = len(arrs)

    def copies(ins, outs, sems):
        send_sems, recv_sems = sems
        x, y, c = _here()
        cps = []
        for l in range(L):
            half = arrs[l].shape[1] // 2
            rows = pl.ds(pl.multiple_of((1 - c) * half, 16), half)
            cps.append(pltpu.make_async_remote_copy(
                src_ref=ins[l].at[:, rows, :], dst_ref=outs[l], send_sem=send_sems.at[l],
                recv_sem=recv_sems.at[l], device_id=(x, y, 1 - c), device_id_type=MESH))
        return cps

    def start(ins, outs, sems):
        for cp in copies(ins, outs, sems):
            cp.start()

    def finish(ins, outs, sems):
        for cp in copies(ins, outs, sems):
            cp.wait()

    return _Comm(arrs, [jax.ShapeDtypeStruct((a.shape[0], a.shape[1] // 2, a.shape[2]), a.dtype) for a in arrs],
                 [pltpu.SemaphoreType.DMA((L,)), pltpu.SemaphoreType.DMA((L,))], start, finish)


def _join_comm(bufs):
    L = len(bufs)

    def start(ins, outs, sems):
        send_sems, recv_sems = sems
        x, y, c = _here()
        for l in range(L):
            pltpu.make_async_remote_copy(src_ref=outs[l].at[c], dst_ref=outs[l].at[c], send_sem=send_sems.at[l],
                                         recv_sem=recv_sems.at[l], device_id=(x, y, 1 - c),
                                         device_id_type=MESH).start()

    def finish(ins, outs, sems):
        send_sems, recv_sems = sems
        x, y, c = _here()
        for l in range(L):
            pltpu.make_async_remote_copy(src_ref=outs[l].at[c], dst_ref=outs[l].at[1 - c],
                                         send_sem=send_sems.at[l], recv_sem=recv_sems.at[l],
                                         device_id=(x, y, 1 - c), device_id_type=MESH).wait()

    return _Comm(bufs, [jax.ShapeDtypeStruct(a.shape, a.dtype) for a in bufs],
                 [pltpu.SemaphoreType.DMA((L,)), pltpu.SemaphoreType.DMA((L,))], start, finish,
                 aliases={l: l for l in range(L)})


def _scatter_comm(arrs):
    L = len(arrs)

    def copies(ins, outs, sems):
        send_sems, recv_sems = sems
        x, y, c = _here()
        return [pltpu.make_async_remote_copy(
            src_ref=ins[l].at[2 * tx + ty], dst_ref=outs[l].at[j],
            send_sem=send_sems.at[l, j], recv_sem=recv_sems.at[l, j],
            device_id=(tx, ty, c), device_id_type=MESH)
            for l in range(L) for j, (tx, ty) in enumerate(_other_chips(x, y))]

    def start(ins, outs, sems):
        for cp in copies(ins, outs, sems):
            cp.start()

    def finish(ins, outs, sems):
        for cp in copies(ins, outs, sems):
            cp.wait()

    return _Comm(arrs, [jax.ShapeDtypeStruct((3,) + a.shape[1:], a.dtype) for a in arrs],
                 [pltpu.SemaphoreType.DMA((L, 3)), pltpu.SemaphoreType.DMA((L, 3))], start, finish)


_A_ORDER = np.array(A_HEAD_ORDER)
_A_INVERSE = np.argsort(_A_ORDER)


def _permute_in_weights(w_in):
    qa = w_in[:, 0:512].reshape(D_MODEL, A_Q_HEADS, HEAD_DIM)[:, _A_ORDER, :].reshape(D_MODEL, 512)
    f_pad = jnp.pad(w_in[:, 2304:2312], ((0, 0), (0, LANES - B_HEADS)))
    w_a = jnp.concatenate([qa, w_in[:, 512:640], f_pad], axis=1)
    return w_a, w_in[:, 640:2304], w_in[:, 2312:4360]


def _slab_segments():
    segs = [(h * HEAD_DIM, int(_A_INVERSE[h]) * HEAD_DIM, HEAD_DIM) for h in range(A_Q_HEADS)]
    segs += [(512, OFF_KA, 128), (640, W_A + OFF_VA, 128), (768, W_A + OFF_QB, 1536),
             (2304, OFF_F, B_HEADS), (2312, W_A + W_B, W_G)]
    return segs


def _shard_slabs(dw_perm):
    R = dw_perm.shape[0]
    tr = _tile(R, 128, 8)
    plan = []
    for k in range(N_CHIP):
        for b in range(W_SHARD_PAD // LANES):
            lo, hi = k * W_SHARD + b * LANES, min(k * W_SHARD + (b + 1) * LANES, (k + 1) * W_SHARD)
            parts = []
            for o0, s0, n in _slab_segments():
                a, z = max(lo, o0), min(hi, o0 + n)
                while a < z:
                    s = s0 + (a - o0)
                    run = min(z - a, LANES - s % LANES)
                    parts.append((s // LANES, ((a - lo) - s % LANES) % LANES, a - lo, run))
                    a += run
            plan.append((k, b, parts))

    def kern(x_ref, o32_ref, obf_ref):
        lane = lax.broadcasted_iota(jnp.int32, (tr, LANES), 1)
        for k, b, parts in plan:
            acc = jnp.zeros((tr, LANES), F32)
            for src, rot, first, run in parts:
                blk = x_ref[:, src * LANES:(src + 1) * LANES]
                if rot:
                    blk = pltpu.roll(blk, rot, 1)
                acc = jnp.where((lane >= first) & (lane < first + run), blk, acc)
            o32_ref[k, :, b * LANES:(b + 1) * LANES] = acc
            obf_ref[k, :, b * LANES:(b + 1) * LANES] = acc.astype(BF)

    out_spec = pl.BlockSpec((N_CHIP, tr, W_SHARD_PAD), lambda i: (0, i, 0))
    return tuple(pl.pallas_call(
        kern, name="shard_slabs", grid=(R // tr,),
        in_specs=[pl.BlockSpec((tr, W_PERM), lambda i: (i, 0))],
        out_specs=[out_spec, out_spec],
        out_shape=[jax.ShapeDtypeStruct((N_CHIP, R, W_SHARD_PAD), F32),
                   jax.ShapeDtypeStruct((N_CHIP, R, W_SHARD_PAD), BF)],
        compiler_params=_cparams("parallel"),
    )(dw_perm))


class _NoExchange:
    def __init__(self, w_in, rest):
        self.w_in_whole, self.rest, self.grads = w_in, rest, {}

    def w_in_comm(self):
        return None

    def w_in(self, outs):
        return self.w_in_whole

    def rest_weights_comm(self):
        return None

    def rest_weights(self, outs):
        return self.rest

    def swap_comm(self, pieces, tag):
        self.grads[tag] = [p32 for p32, _ in pieces]
        return None

    def swap_done(self, outs, tag):
        return None

    def reduce_done(self, outs, tag):
        pass

    def join_comm(self):
        return None


class _Exchange:
    def __init__(self, ci, chip, w_in_shard, rest_shards):
        self.ci, self.chip, self.w_in_shard, self.rest_shards = ci, chip, w_in_shard, rest_shards
        self.pieces, self.part_f32, self.halves = {}, {}, {}

    def _my_half(self, a, axis=0, other=False):
        rows = a.shape[axis] // 2
        return lax.dynamic_slice_in_dim(a, ((1 - self.ci) if other else self.ci) * rows, rows, axis=axis)

    def w_in_comm(self):
        return _gather_comm([self._my_half(self.w_in_shard).astype(BF)])

    def w_in(self, outs):
        return _col_sharded(outs[0])

    def rest_weights_comm(self):
        return _gather_comm([self._my_half(w).astype(BF) for w in self.rest_shards])

    def rest_weights(self, outs):
        w_ba, w_bb, w_out, w_fi, w_fo = outs
        return (_col_sharded(w_ba), _col_sharded(w_bb), _row_sharded(w_out), _col_sharded(w_fi),
                _row_sharded(w_fo))

    def swap_comm(self, pieces, tag):
        self.pieces[tag] = pieces
        return _swap_comm([pbf for _, pbf in pieces])

    def swap_done(self, got, tag):
        self.part_f32[tag], part_bf = [], []
        for l, ((p32, _), g_) in enumerate(zip(self.pieces[tag], got)):
            s32, sbf = _add_pair(p32, g_, self.ci, f"chip_sum_{tag}_{l}")
            self.part_f32[tag].append(s32)
            part_bf.append(sbf)
        return _scatter_comm(part_bf)

    def reduce_done(self, outs, tag):
        self.halves[tag] = [_add_three(p32, r, self.chip, f"shard_sum_{tag}_{l}")
                            for l, (p32, r) in enumerate(zip(self.part_f32[tag], outs))]

    def join_comm(self):
        return _join_comm(self.halves["late"] + self.halves["early"])


def _col_sharded(g):
    return jnp.transpose(g.reshape(N_CHIP, -1, g.shape[-1]), (1, 0, 2)).reshape(2 * g.shape[1], N_CHIP * g.shape[-1])


def _row_sharded(g):
    return g.reshape(N_DEV * g.shape[1], g.shape[-1])


def _rope_tables(pos):
    inv_freq = 1.0 / (ROPE_THETA ** (jnp.arange(0, HEAD_DIM, 2, dtype=F32) / HEAD_DIM))
    ang = pos.astype(F32)[:, None] * inv_freq
    cos, sin = jnp.cos(ang), jnp.sin(ang)
    return jnp.tile(cos, (1, 4)), jnp.tile(jnp.concatenate([-sin, sin], axis=1), (1, 2))


def _local_step(x, pos, ada, g1, g2, g3, g4, b_f, sinks, exch, target):
    S = x.shape[0]
    t_fox = _tile(S, 512, LANES) if S >= 1024 else S // 2
    t_fox_fwd = _tile(S, 1024, LANES) if S >= 2048 else S // 2
    shift_m, scale_m, gate_m, shift_f, scale_f, gate_f = [ada[i:i + 1] for i in range(N_ADA)]
    cos_t, sin_t = _rope_tables(pos)
    sinks_p = sinks.reshape(A_KV_HEADS, 4).T.reshape(A_Q_HEADS)
    b_f_pad = jnp.pad(b_f, (0, LANES - B_HEADS)).reshape(1, LANES)

    h1, outs = _pre_norm(x, g1, scale_m, shift_m, "pre_mix_norm", comm=exch.w_in_comm())
    w_a, w_b, w_g = _permute_in_weights(exch.w_in(outs))
    w_perm = jnp.concatenate([w_a, w_b, w_g], axis=1)
    p_a = _mm(h1, w_a, "nn", F32, "proj_a")
    p_b = _mm(h1, w_b, "nn", BF, "proj_b")
    p_g = _mm(h1, w_g, "nn", BF, "proj_g")
    (qk_a,) = _rope([p_a], [640], cos_t, sin_t, "rope_fwd")
    o_a, lse_a = _swa_fwd(qk_a, p_b, 0, sinks_p)
    cum = _fox_gate_fwd(p_a, b_f_pad)
    bq, bk = _fox_prep_fwd(cum, t_fox_fwd)
    comm = exch.rest_weights_comm()
    o_b, lse_b, outs = _fox_fwd(p_b, bq, bk, t_fox_fwd, comm=comm)
    w_ba, w_bb, w_out, w_fi, w_fo = exch.rest_weights(outs)
    w_ba_p = w_ba.reshape(A_Q_HEADS, HEAD_DIM, D_MODEL)[_A_ORDER].reshape(512, D_MODEL)
    pa = _mm(o_a, w_ba_p, "nn", BF, "branch_a")
    pb = _mm(o_b, w_bb, "nn", BF, "branch_b")
    merged = _merge_fwd(p_g, pa, pb)
    y1 = _mm(merged, w_out, "nn", BF, "out_proj")
    x2, h2 = _post_pre(x, y1, g2, gate_m, g3, scale_f, shift_f)
    gu = _mm(h2, w_fi, "nn", BF, "ffn_in")
    act = _swiglu_fwd(gu)
    y2 = _mm(act, w_fo, "nn", BF, "ffn_out")
    d_out, d_y2, st_f = _final(x2, y2, g4, gate_f, target)

    d_act = _mm(d_y2, w_fo, "nt", BF, "ffn_out_dx")
    row_pieces = lambda pair: tuple(t.reshape(N_CHIP, t.shape[0] // N_CHIP, t.shape[1]) for t in pair)
    dw_fo = row_pieces(_mm(act, d_y2, "tn", F32, "ffn_out_dw", twin=True))
    d_gu = _swiglu_bwd(d_act, gu)
    d_h2 = _mm(d_gu, w_fi, "nt", BF, "ffn_in_dx")
    dw_fi = _mm(h2, d_gu, "tn", F32, "ffn_in_dw", col_pieces=N_CHIP, twin=True)
    d_x2, d_y1, st_m = _mid_bwd(d_h2, x2, d_out, y1, g3, scale_f, g2, gate_m)
    d_merged = _mm(d_y1, w_out, "nt", BF, "out_proj_dx")
    dw_out = row_pieces(_mm(merged, d_y1, "tn", F32, "out_proj_dw", twin=True))
    d_pa, d_pb, d_ga, d_gb = _merge_bwd(d_merged, p_g, pa, pb)
    d_oa = _mm(d_pa, w_ba_p, "nt", F32, "branch_a_dx")
    dw_ba_p = _mm(o_a, d_pa, "tn", F32, "branch_a_dw", col_pieces=N_CHIP, twin=True)
    d_ob = _mm(d_pb, w_bb, "nt", F32, "branch_b_dx")
    dw_bb = _mm(o_b, d_pb, "tn", F32, "branch_b_dw", col_pieces=N_CHIP, twin=True)
    head_rows = lambda t: t.reshape(N_CHIP, A_Q_HEADS, HEAD_DIM, -1)[:, _A_INVERSE].reshape(t.shape)
    dw_ba = tuple(head_rows(t) for t in dw_ba_p)
    comm = exch.swap_comm([dw_ba, dw_bb, dw_out, dw_fi, dw_fo], "early")
    dq_a, dk_a, dv_a, d_sink, outs = _swa_bwd(qk_a, p_b, 0, o_a, d_oa, lse_a, sinks_p, comm=comm)
    comm = exch.swap_done(outs, "early")
    bq_bwd, bdo = _fox_prep_bwd(cum, o_b, d_ob, lse_b, t_fox)
    dq_b, dk_b, dv_b, d_ck, d_cq, outs = _fox_bwd(p_b, d_ob, bq_bwd, bk, bdo, t_fox, comm=comm)
    exch.reduce_done(outs, "early")
    d_qa, d_ka = _rope([dq_a, dk_a], [512, LANES], cos_t, -sin_t, "rope_bwd")
    d_ck_cols = jnp.pad(d_ck.reshape(B_HEADS, S).T, ((0, 0), (0, LANES - B_HEADS)))
    d_f, d_bf = _fox_gate_bwd(d_cq, d_ck_cols, p_a, b_f_pad)
    d_proj = jnp.concatenate([d_qa, d_ka, d_f, dv_a.astype(BF), dq_b.astype(BF), dk_b.astype(BF),
                              dv_b.astype(BF), d_ga, d_gb], axis=1)
    dw_perm = _mm(h1, d_proj, "tn", F32, "proj_dw")
    swap = exch.swap_comm([_shard_slabs(dw_perm)], "late")
    comm = exch.swap_done(_run_comm(swap, "grads_to_sibling_late") if swap else None, "late")
    res = _mm(d_proj, w_perm, "nt", BF, "proj_dx", comm=comm)
    d_h1 = res[0] if comm else res
    exch.reduce_done(res[1] if comm else None, "late")
    grad_x, st_p, outs = _pre_bwd(d_h1, x, d_x2, g1, scale_m, comm=exch.join_comm())
    exch.joined = outs

    d_sinks = d_sink[:, :2, 0].T.reshape(A_Q_HEADS)
    small = jnp.concatenate([
        st_p[0], st_p[1], st_m[3], st_m[0], st_m[1], st_f[0],
        st_p[2], st_m[4], st_m[2], st_f[1],
        st_f[2], d_bf[0, :B_HEADS], d_sinks,
        jnp.zeros((SM_LEN - SM_SINK - A_Q_HEADS,), F32)])
    return grad_x, small


def kernel(x, c, positions, w_ada, b_ada, g_pre_mix, g_post_mix, w_in, b_f, sinks, w_branch_a, w_branch_b, w_out, g_pre_ffn, g_post_ffn, w_ffn_in, w_ffn_out, loss_target, m_w_ada, m_b_ada, m_g_pre_mix, m_g_post_mix, m_w_in, m_b_f, m_sinks, m_w_branch_a, m_w_branch_b, m_w_out, m_g_pre_ffn, m_g_post_ffn, m_w_ffn_in, m_w_ffn_out, v_w_ada, v_b_ada, v_g_pre_mix, v_g_post_mix, v_w_in, v_b_f, v_sinks, v_w_branch_a, v_w_branch_b, v_w_out, v_g_pre_ffn, v_g_post_ffn, v_w_ffn_in, v_w_ffn_out):
    xi, yi, ci = _here()
    chip = 2 * xi + yi
    dev = 2 * chip + ci

    (c_g,) = _allgather8([c.reshape(8, LANES)], "gather_c")
    c_all = c_g.reshape(N_DEV, D_MODEL)
    exch = _Exchange(ci, chip, w_in[0], [w_branch_a[0], w_branch_b[0], w_out[0], w_ffn_in[0], w_ffn_out[0]])

    ada_cols = _mm(c_all, w_ada[0], "nn", F32, "ada_fwd")
    (ada_g,) = _allgather8([ada_cols], "gather_ada")
    ada_mine = lax.dynamic_index_in_dim(ada_g.reshape(N_CHIP, 2, N_DEV, -1)[:, 0], dev, axis=1, keepdims=False)
    ada = (ada_mine.reshape(-1) + b_ada[0]).reshape(N_ADA, D_MODEL)

    grad_x, small = _local_step(
        x[0], positions[0], ada, g_pre_mix, g_post_mix, g_pre_ffn, g_post_ffn, b_f[0], sinks[0],
        exch, loss_target[0])

    g_w_in, g_w_ba, g_w_bb, g_w_out, g_w_fi, g_w_fo = [j.reshape(2 * j.shape[1], j.shape[2]) for j in exch.joined]
    upd_fi, (small_g,) = _adamw(w_ffn_in[0], g_w_fi, m_w_ffn_in[0], v_w_ffn_in[0], "adamw_w_ffn_in",
                                comm=_gather_comm([small.reshape(8, SM_LEN // 8)]))

    small_all = small_g.reshape(N_DEV, SM_LEN)
    small_tot, loss_row = _small_finalize(small_all)
    loss = loss_row[0, 0]
    d_ada_cols = lax.dynamic_slice_in_dim(small_all[:, :N_ADA * D_MODEL], chip * (N_ADA * D_MODEL // N_CHIP),
                                          N_ADA * D_MODEL // N_CHIP, axis=1)
    g_w_ada = _ada_dw(c_all.T, d_ada_cols)

    def small_vec(b_ada_, g1_, g2_, g3_, g4_, b_f_, sinks_):
        return jnp.concatenate([b_ada_[0], g1_[0], g2_[0], g3_[0], g4_[0], jnp.zeros((D_MODEL,), F32),
                                b_f_[0], sinks_[0], jnp.zeros((SM_LEN - SM_SINK - A_Q_HEADS,), F32)]
                               ).reshape(8, SM_LEN // 8)

    sw = small_vec(b_ada, g_pre_mix, g_post_mix, g_pre_ffn, g_post_ffn, b_f, sinks)
    sm = small_vec(m_b_ada, m_g_pre_mix, m_g_post_mix, m_g_pre_ffn, m_g_post_ffn, m_b_f, m_sinks)
    sv = small_vec(v_b_ada, v_g_pre_mix, v_g_post_mix, v_g_pre_ffn, v_g_post_ffn, v_b_f, v_sinks)
    s_upd = [u.reshape(SM_LEN) for u in _adamw(sw, small_tot.reshape(8, SM_LEN // 8), sm, sv, "adamw_small")]
    s_grad = small_tot.reshape(SM_LEN)

    def unpack(vec):
        row = lambda a, n: vec[a:a + n].reshape(1, n)
        return dict(b_ada=row(SM_ADA, N_ADA * D_MODEL), g_pre_mix=row(SM_G, D_MODEL),
                    g_post_mix=row(SM_G + D_MODEL, D_MODEL), g_pre_ffn=row(SM_G + 2 * D_MODEL, D_MODEL),
                    g_post_ffn=row(SM_G + 3 * D_MODEL, D_MODEL), b_f=row(SM_BF, B_HEADS),
                    sinks=row(SM_SINK, A_Q_HEADS))

    big = dict(
        w_ada=(w_ada, g_w_ada, m_w_ada, v_w_ada),
        w_branch_a=(w_branch_a, g_w_ba, m_w_branch_a, v_w_branch_a),
        w_branch_b=(w_branch_b, g_w_bb, m_w_branch_b, v_w_branch_b),
        w_out=(w_out, g_w_out, m_w_out, v_w_out),
        w_ffn_out=(w_ffn_out, g_w_fo, m_w_ffn_out, v_w_ffn_out))
    grads, deltas, new_m, new_v = unpack(s_grad), unpack(s_upd[0]), unpack(s_upd[1]), unpack(s_upd[2])
    grads["w_ffn_in"], deltas["w_ffn_in"], new_m["w_ffn_in"], new_v["w_ffn_in"] = [
        t[None] for t in (g_w_fi, *upd_fi)]
    for n, (w_, g_, m_, v_) in big.items():
        d_, nm_, nv_ = _adamw(w_[0], g_, m_[0], v_[0], "adamw_" + n)
        grads[n], deltas[n], new_m[n], new_v[n] = g_[None], d_[None], nm_[None], nv_[None]
    pad_cols = lambda a: jnp.pad(a, ((0, 0), (0, W_SHARD_PAD - W_SHARD)))
    upd = _adamw(pad_cols(w_in[0]), g_w_in, pad_cols(m_w_in[0]), pad_cols(v_w_in[0]), "adamw_w_in")
    grads["w_in"], deltas["w_in"], new_m["w_in"], new_v["w_in"] = [t[None, :, :W_SHARD] for t in (g_w_in, *upd)]

    names = ["w_ada", "b_ada", "g_pre_mix", "g_post_mix", "w_in", "b_f", "sinks", "w_branch_a", "w_branch_b",
             "w_out", "g_pre_ffn", "g_post_ffn", "w_ffn_in", "w_ffn_out"]
    return (loss, grad_x[None], *[grads[n] for n in names], *[deltas[n] for n in names],
            *[new_m[n] for n in names], *[new_v[n] for n in names])
```

```python
import functools
import math

import numpy as np
import jax
import jax.numpy as jnp
from jax import lax
from jax.experimental import pallas as pl
from jax.experimental.pallas import tpu as pltpu

F32 = jnp.float32
BF = jnp.bfloat16

D_MODEL = 1024
HEAD_DIM = 64
LANES = 128
WINDOW = 128
A_Q_HEADS = 8
A_KV_HEADS = 2
B_HEADS = 8
D_FF = 2816
ROPE_THETA = 10000.0
RMS_EPS = 1e-6
N_ADA = 6
N_DEV = 8
N_CHIP = 4

ADAM_LR = 0.001
ADAM_B1 = 0.9
ADAM_B2 = 0.999
ADAM_EPS = 1e-08
ADAM_WD = 0.01
ADAM_STEP = 10

VMEM_LIMIT = 48 * 1024 * 1024
MESH = pl.DeviceIdType.MESH

A_HEAD_ORDER = (0, 4, 1, 5, 2, 6, 3, 7)

OFF_QA, OFF_KA, OFF_F = 0, 512, 640
W_A = 768
OFF_VA, OFF_QB, OFF_KB, OFF_VB = 0, 128, 640, 1152
W_B = 1664
W_G = 2048
W_PERM = W_A + W_B + W_G
W_SHARD = 1090
W_SHARD_PAD = 1152


def _tile(n, cap, mult=LANES):
    if n <= cap:
        return n
    t = (cap // mult) * mult
    while t >= mult:
        if n % t == 0:
            return t
        t -= mult
    raise ValueError(f"no tile for {n}")


MXU_WIDTH = 256
MM_OPERAND_BYTES = 28 * 1024 * 1024


def _mm_tiles(M, N, K, a_bytes, b_bytes, tm_cap, tn_cap):
    tm = _tile(M, tm_cap)
    try:
        tn = _tile(N, tn_cap, MXU_WIDTH)
    except ValueError:
        tn = _tile(N, tn_cap)
    fits = lambda tk: 2 * tk * (tm * a_bytes + tn * b_bytes) <= MM_OPERAND_BYTES
    tk = K if fits(K) else next(t for t in range(K // LANES * LANES, 0, -LANES) if K % t == 0 and fits(t))
    return tm, tn, tk


def _cparams(*sem):
    return pltpu.CompilerParams(dimension_semantics=sem, vmem_limit_bytes=VMEM_LIMIT)


def _own_refs(refs, comm, n_in, n_out, n_scratch):
    if comm is None:
        return list(refs), None
    return comm.split(refs, n_in, n_out, n_scratch)


def _comm_specs(comm, side):
    if comm is None:
        return []
    return [pl.BlockSpec(memory_space=pl.ANY)] * len(comm.ins if side == "in" else comm.out_shapes)


def _comm_edge(comm, comm_refs, grid, first):
    if comm is None:
        return
    at_edge = None
    for axis, n in enumerate(grid):
        here = pl.program_id(axis) == (0 if first else n - 1)
        at_edge = here if at_edge is None else at_edge & here
    pl.when(at_edge)(lambda: (comm.start if first else comm.finish)(*comm_refs))


def _mm(a, b, mode, out_dtype, name, tm_cap=512, tn_cap=2816, comm=None, col_pieces=1, twin=False):
    if mode == "nn":
        (M, K), (K2, N) = a.shape, b.shape
        dims = (((1,), (0,)), ((), ()))
    elif mode == "nt":
        (M, K), (N, K2) = a.shape, b.shape
        dims = (((1,), (1,)), ((), ()))
    else:
        (K, M), (K2, N) = a.shape, b.shape
        dims = (((0,), (0,)), ((), ()))
    assert K == K2, (a.shape, b.shape, mode)
    tm, tn, tk = _mm_tiles(M, N // col_pieces, K, a.dtype.itemsize, b.dtype.itemsize, tm_cap, tn_cap)
    nk = K // tk
    n_out = 2 if twin else 1
    n_scratch = 1 if nk > 1 else 0
    if mode == "nn":
        a_spec = pl.BlockSpec((tm, tk), lambda i, j, k: (i, k))
        b_spec = pl.BlockSpec((tk, tn), lambda i, j, k: (k, j))
    elif mode == "nt":
        a_spec = pl.BlockSpec((tm, tk), lambda i, j, k: (i, k))
        b_spec = pl.BlockSpec((tn, tk), lambda i, j, k: (j, k))
    else:
        a_spec = pl.BlockSpec((tk, tm), lambda i, j, k: (k, i))
        b_spec = pl.BlockSpec((tk, tn), lambda i, j, k: (k, j))

    grid = (M // tm, N // tn, nk)

    def kern(*refs):
        own, comm_refs = _own_refs(refs, comm, 2, n_out, n_scratch)
        a_ref, b_ref, o_refs = own[0], own[1], own[2:2 + n_out]
        k = pl.program_id(2)
        _comm_edge(comm, comm_refs, grid, first=True)
        part = lax.dot_general(a_ref[...].astype(BF), b_ref[...].astype(BF), dims,
                               preferred_element_type=F32)
        if nk == 1:
            for o_ref in o_refs:
                o_ref[...] = part.astype(o_ref.dtype)
        else:
            acc_ref = own[2 + n_out]

            @pl.when(k == 0)
            def _():
                acc_ref[...] = part

            @pl.when(k > 0)
            def _():
                acc_ref[...] += part

            @pl.when(k == nk - 1)
            def _():
                for o_ref in o_refs:
                    o_ref[...] = acc_ref[...].astype(o_ref.dtype)

        _comm_edge(comm, comm_refs, grid, first=False)

    if col_pieces > 1:
        per = N // col_pieces // tn
        out_spec = pl.BlockSpec((None, tm, tn), lambda i, j, k: (j // per, i, j % per))
        shape = (col_pieces, M, N // col_pieces)
    else:
        out_spec = pl.BlockSpec((tm, tn), lambda i, j, k: (i, j))
        shape = (M, N)
    dtypes = [out_dtype, BF] if twin else [out_dtype]
    res = pl.pallas_call(
        kern, name=name, grid=grid,
        in_specs=[a_spec, b_spec] + _comm_specs(comm, "in"),
        out_specs=[out_spec] * n_out + _comm_specs(comm, "out"),
        out_shape=[jax.ShapeDtypeStruct(shape, d) for d in dtypes] + (comm.out_shapes if comm else []),
        scratch_shapes=[pltpu.VMEM((tm, tn), F32)] * n_scratch + (comm.sem_shapes if comm else []),
        compiler_params=_cparams("parallel", "parallel", "arbitrary"),
    )(a, b, *(comm.ins if comm else []))
    own = res[0] if n_out == 1 else tuple(res[:n_out])
    return (own, res[n_out:]) if comm else own


ROWS = 512


def _row_spec(tm, width=D_MODEL, col=0):
    return pl.BlockSpec((tm, width), lambda i: (i, col))


def _vec_spec(width=D_MODEL):
    return pl.BlockSpec((1, width), lambda i: (0, 0))


def _rms(x):
    return lax.rsqrt(jnp.mean(x * x, axis=-1, keepdims=True) + RMS_EPS)


def _colsum(x):
    return jnp.sum(x, axis=0, keepdims=True)


def _norm_bwd(d_xn, xn, r):
    return r * (d_xn - xn * jnp.mean(d_xn * xn, axis=-1, keepdims=True))


def _pre_norm(x, g, scale, shift, name, comm=None):
    S = x.shape[0]
    tm = _tile(S, 2 * ROWS, 8)
    grid = (S // tm,)

    def kern(*refs):
        (x_ref, g_ref, sc_ref, sh_ref, h_ref), comm_refs = _own_refs(refs, comm, 4, 1, 0)
        _comm_edge(comm, comm_refs, grid, first=True)
        xf = x_ref[...]
        y = xf * _rms(xf) * g_ref[...]
        h_ref[...] = (y * (1.0 + sc_ref[...]) + sh_ref[...]).astype(BF)
        _comm_edge(comm, comm_refs, grid, first=False)

    res = pl.pallas_call(
        kern, name=name, grid=grid,
        in_specs=[_row_spec(tm), _vec_spec(), _vec_spec(), _vec_spec()] + _comm_specs(comm, "in"),
        out_specs=[_row_spec(tm)] + _comm_specs(comm, "out"),
        out_shape=[jax.ShapeDtypeStruct((S, D_MODEL), BF)] + (comm.out_shapes if comm else []),
        scratch_shapes=comm.sem_shapes if comm else [],
        compiler_params=_cparams("arbitrary"),
    )(x, g, scale, shift, *(comm.ins if comm else []))
    return res[0], res[1:]


def _post_pre(x, y1, g2, gate_m, g3, scale_f, shift_f):
    S = x.shape[0]
    tm = _tile(S, 2 * ROWS, 8)

    def kern(x_ref, y_ref, g2_ref, gm_ref, g3_ref, sc_ref, sh_ref, x2_ref, h2_ref):
        y = y_ref[...].astype(F32)
        n2 = y * _rms(y) * g2_ref[...]
        x2 = x_ref[...] + gm_ref[...] * n2
        x2_ref[...] = x2
        n3 = x2 * _rms(x2) * g3_ref[...]
        h2_ref[...] = (n3 * (1.0 + sc_ref[...]) + sh_ref[...]).astype(BF)

    return pl.pallas_call(
        kern, name="post_mix_pre_ffn", grid=(S // tm,),
        in_specs=[_row_spec(tm), _row_spec(tm)] + [_vec_spec()] * 5,
        out_specs=[_row_spec(tm), _row_spec(tm)],
        out_shape=[jax.ShapeDtypeStruct((S, D_MODEL), F32), jax.ShapeDtypeStruct((S, D_MODEL), BF)],
        compiler_params=_cparams("parallel"),
    )(x, y1, g2, gate_m, g3, scale_f, shift_f)


def _stats_spec():
    return pl.BlockSpec((8, D_MODEL), lambda i: (0, 0))


def _final(x2, y2, g4, gate_f, target):
    S = x2.shape[0]
    tm = _tile(S, ROWS, 8)

    def kern(x2_ref, y_ref, g4_ref, gf_ref, t_ref, dout_ref, dy_ref, st_ref):
        @pl.when(pl.program_id(0) == 0)
        def _():
            st_ref[...] = jnp.zeros_like(st_ref)

        y = y_ref[...].astype(F32)
        r = _rms(y)
        yn = y * r
        n4 = yn * g4_ref[...]
        diff = x2_ref[...] + gf_ref[...] * n4 - t_ref[...]
        d_out = diff / D_MODEL
        dout_ref[...] = d_out
        dn = d_out * gf_ref[...]
        dy_ref[...] = _norm_bwd(dn * g4_ref[...], yn, r).astype(BF)
        st_ref[0:1, :] += _colsum(d_out * n4)
        st_ref[1:2, :] += _colsum(dn * yn)
        st_ref[2:3, :] += _colsum(diff * diff)

    return pl.pallas_call(
        kern, name="final_loss", grid=(S // tm,),
        in_specs=[_row_spec(tm), _row_spec(tm), _vec_spec(), _vec_spec(), _row_spec(tm)],
        out_specs=[_row_spec(tm), _row_spec(tm), _stats_spec()],
        out_shape=[jax.ShapeDtypeStruct((S, D_MODEL), F32), jax.ShapeDtypeStruct((S, D_MODEL), BF),
                   jax.ShapeDtypeStruct((8, D_MODEL), F32)],
        compiler_params=_cparams("arbitrary"),
    )(x2, y2, g4, gate_f, target)


def _mid_bwd(d_h2, x2, d_out, y1, g3, scale_f, g2, gate_m):
    S = x2.shape[0]
    tm = _tile(S, ROWS, 8)

    def kern(dh_ref, x2_ref, dout_ref, y_ref, g3_ref, sc_ref, g2_ref, gm_ref, dx2_ref, dy_ref, st_ref):
        @pl.when(pl.program_id(0) == 0)
        def _():
            st_ref[...] = jnp.zeros_like(st_ref)

        dh = dh_ref[...].astype(F32)
        x2 = x2_ref[...]
        r3 = _rms(x2)
        xn = x2 * r3
        one_sc = 1.0 + sc_ref[...]
        d_x2 = dout_ref[...] + _norm_bwd(dh * one_sc * g3_ref[...], xn, r3)
        dx2_ref[...] = d_x2
        y = y_ref[...].astype(F32)
        r2 = _rms(y)
        yn = y * r2
        dn = d_x2 * gm_ref[...]
        dy_ref[...] = _norm_bwd(dn * g2_ref[...], yn, r2).astype(BF)
        st_ref[0:1, :] += _colsum(dh)
        st_ref[1:2, :] += _colsum(dh * (xn * g3_ref[...]))
        st_ref[2:3, :] += _colsum(dh * one_sc * xn)
        st_ref[3:4, :] += _colsum(d_x2 * (yn * g2_ref[...]))
        st_ref[4:5, :] += _colsum(dn * yn)

    return pl.pallas_call(
        kern, name="mid_bwd", grid=(S // tm,),
        in_specs=[_row_spec(tm)] * 4 + [_vec_spec()] * 4,
        out_specs=[_row_spec(tm), _row_spec(tm), _stats_spec()],
        out_shape=[jax.ShapeDtypeStruct((S, D_MODEL), F32), jax.ShapeDtypeStruct((S, D_MODEL), BF),
                   jax.ShapeDtypeStruct((8, D_MODEL), F32)],
        compiler_params=_cparams("arbitrary"),
    )(d_h2, x2, d_out, y1, g3, scale_f, g2, gate_m)


def _pre_bwd(d_h1, x, d_x2, g1, scale_m, comm=None):
    S = x.shape[0]
    tm = _tile(S, ROWS, 8)
    grid = (S // tm,)

    def kern(*refs):
        (dh_ref, x_ref, dx2_ref, g_ref, sc_ref, gx_ref, st_ref), comm_refs = _own_refs(refs, comm, 5, 2, 0)
        _comm_edge(comm, comm_refs, grid, first=True)

        @pl.when(pl.program_id(0) == 0)
        def _():
            st_ref[...] = jnp.zeros_like(st_ref)

        dh = dh_ref[...].astype(F32)
        xf = x_ref[...]
        r = _rms(xf)
        xn = xf * r
        one_sc = 1.0 + sc_ref[...]
        gx_ref[...] = dx2_ref[...] + _norm_bwd(dh * one_sc * g_ref[...], xn, r)
        st_ref[0:1, :] += _colsum(dh)
        st_ref[1:2, :] += _colsum(dh * (xn * g_ref[...]))
        st_ref[2:3, :] += _colsum(dh * one_sc * xn)
        _comm_edge(comm, comm_refs, grid, first=False)

    res = pl.pallas_call(
        kern, name="pre_mix_bwd", grid=grid,
        in_specs=[_row_spec(tm)] * 3 + [_vec_spec()] * 2 + _comm_specs(comm, "in"),
        out_specs=[_row_spec(tm), _stats_spec()] + _comm_specs(comm, "out"),
        out_shape=[jax.ShapeDtypeStruct((S, D_MODEL), F32), jax.ShapeDtypeStruct((8, D_MODEL), F32)]
        + (comm.out_shapes if comm else []),
        scratch_shapes=comm.sem_shapes if comm else [],
        input_output_aliases={5 + i: 2 + o for i, o in comm.aliases.items()} if comm else {},
        compiler_params=_cparams("arbitrary"),
    )(d_h1, x, d_x2, g1, scale_m, *(comm.ins if comm else []))
    return res[0], res[1], res[2:]


def _rope(xs, widths, cos_t, sin_t, name):
    S = xs[0].shape[0]
    tm = _tile(S, 512, 8)
    n = len(xs)

    def kern(*refs):
        cos = refs[n][...]
        sin = refs[n + 1][...]
        first = (lax.broadcasted_iota(jnp.int32, cos.shape, 1) % HEAD_DIM) < HEAD_DIM // 2
        for x_ref, o_ref, w in zip(refs[:n], refs[n + 2:], widths):
            for c0 in range(0, w, LANES):
                v = x_ref[:, c0:c0 + LANES]
                partner = jnp.where(first, pltpu.roll(v, LANES - HEAD_DIM // 2, 1),
                                    pltpu.roll(v, HEAD_DIM // 2, 1))
                o_ref[:, c0:c0 + LANES] = (v * cos + partner * sin).astype(BF)

    return pl.pallas_call(
        kern, name=name, grid=(S // tm,),
        in_specs=[_row_spec(tm, w) for w in widths] + [_row_spec(tm, LANES)] * 2,
        out_specs=[_row_spec(tm, w) for w in widths],
        out_shape=[jax.ShapeDtypeStruct((S, w), BF) for w in widths],
        compiler_params=_cparams("parallel"),
    )(*xs, cos_t, sin_t)


def _merge_fwd(pg, pa, pb):
    S = pa.shape[0]
    tm = _tile(S, 2 * ROWS, 8)

    def kern(ga_ref, gb_ref, pa_ref, pb_ref, o_ref):
        ga = jax.nn.sigmoid(ga_ref[...].astype(F32))
        gb = jax.nn.sigmoid(gb_ref[...].astype(F32))
        o_ref[...] = (ga * pa_ref[...].astype(F32) + gb * pb_ref[...].astype(F32)).astype(BF)

    return pl.pallas_call(
        kern, name="merge_fwd", grid=(S // tm,),
        in_specs=[_row_spec(tm, col=0), _row_spec(tm, col=1), _row_spec(tm), _row_spec(tm)],
        out_specs=_row_spec(tm),
        out_shape=jax.ShapeDtypeStruct((S, D_MODEL), BF),
        compiler_params=_cparams("parallel"),
    )(pg, pg, pa, pb)


def _merge_bwd(d_merged, pg, pa, pb):
    S = pa.shape[0]
    tm = _tile(S, ROWS, 8)

    def kern(dm_ref, ga_ref, gb_ref, pa_ref, pb_ref, dpa_ref, dpb_ref, dga_ref, dgb_ref):
        dm = dm_ref[...].astype(F32)
        ga = jax.nn.sigmoid(ga_ref[...].astype(F32))
        gb = jax.nn.sigmoid(gb_ref[...].astype(F32))
        dpa_ref[...] = (dm * ga).astype(BF)
        dpb_ref[...] = (dm * gb).astype(BF)
        dga_ref[...] = (dm * pa_ref[...].astype(F32) * ga * (1.0 - ga)).astype(BF)
        dgb_ref[...] = (dm * pb_ref[...].astype(F32) * gb * (1.0 - gb)).astype(BF)

    bf_out = jax.ShapeDtypeStruct((S, D_MODEL), BF)
    return pl.pallas_call(
        kern, name="merge_bwd", grid=(S // tm,),
        in_specs=[_row_spec(tm), _row_spec(tm, col=0), _row_spec(tm, col=1), _row_spec(tm), _row_spec(tm)],
        out_specs=[_row_spec(tm)] * 4,
        out_shape=[bf_out] * 4,
        compiler_params=_cparams("parallel"),
    )(d_merged, pg, pg, pa, pb)


def _swiglu_fwd(gu):
    S = gu.shape[0]
    tm = _tile(S, 2 * ROWS, 8)
    tc = _tile(D_FF, 1408)
    nc = D_FF // tc

    def kern(g_ref, u_ref, o_ref):
        g = g_ref[...].astype(F32)
        o_ref[...] = (g * jax.nn.sigmoid(g) * u_ref[...].astype(F32)).astype(BF)

    return pl.pallas_call(
        kern, name="swiglu_fwd", grid=(S // tm, nc),
        in_specs=[pl.BlockSpec((tm, tc), lambda i, j: (i, j)),
                  pl.BlockSpec((tm, tc), lambda i, j: (i, j + nc))],
        out_specs=pl.BlockSpec((tm, tc), lambda i, j: (i, j)),
        out_shape=jax.ShapeDtypeStruct((S, D_FF), BF),
        compiler_params=_cparams("parallel", "parallel"),
    )(gu, gu)


def _swiglu_bwd(d_act, gu):
    S = gu.shape[0]
    tm = _tile(S, ROWS // 2, 8)

    def kern(da_ref, g_ref, u_ref, o_ref):
        g = g_ref[...].astype(F32)
        u = u_ref[...].astype(F32)
        da = da_ref[...].astype(F32)
        sg = jax.nn.sigmoid(g)
        o_ref[:, :D_FF] = (da * u * (sg * (1.0 + g * (1.0 - sg)))).astype(BF)
        o_ref[:, D_FF:] = (da * (g * sg)).astype(BF)

    return pl.pallas_call(
        kern, name="swiglu_bwd", grid=(S // tm,),
        in_specs=[_row_spec(tm, D_FF), _row_spec(tm, D_FF, 0), _row_spec(tm, D_FF, 1)],
        out_specs=_row_spec(tm, 2 * D_FF),
        out_shape=jax.ShapeDtypeStruct((S, 2 * D_FF), BF),
        compiler_params=_cparams("parallel"),
    )(d_act, gu, gu)


def _split3(x):
    hi = x.astype(BF)
    r1 = x - hi.astype(F32)
    mid = r1.astype(BF)
    lo = (r1 - mid.astype(F32)).astype(BF)
    return hi, mid, lo


def _tri_dot(tri, x):
    return sum(jnp.dot(tri, part, preferred_element_type=F32) for part in _split3(x))


def _log_sigmoid(z):
    return jnp.minimum(z, 0.0) - jnp.log(1.0 + jnp.exp(-jnp.abs(z)))


def _fox_gate_fwd(pa, b_f_pad):
    S = pa.shape[0]
    T = _tile(S, 512, 8)
    f_col = OFF_F // LANES

    def kern(z_ref, b_ref, cum_ref, carry_ref):
        @pl.when(pl.program_id(0) == 0)
        def _():
            carry_ref[...] = jnp.zeros_like(carry_ref)

        log_f = _log_sigmoid(z_ref[...] + b_ref[...])
        row = lax.broadcasted_iota(jnp.int32, (T, T), 0)
        col = lax.broadcasted_iota(jnp.int32, (T, T), 1)
        tri = (col <= row).astype(BF)
        cum = _tri_dot(tri, log_f) + carry_ref[...]
        cum_ref[...] = cum
        carry_ref[...] = cum[T - 1:T, :]

    return pl.pallas_call(
        kern, name="fox_gate_fwd", grid=(S // T,),
        in_specs=[_row_spec(T, LANES, f_col), _vec_spec(LANES)],
        out_specs=_row_spec(T, LANES),
        out_shape=jax.ShapeDtypeStruct((S, LANES), F32),
        scratch_shapes=[pltpu.VMEM((1, LANES), F32)],
        compiler_params=_cparams("arbitrary"),
    )(pa, b_f_pad)


def _fox_gate_bwd(rowsum_ds, colsum_ds, pa, b_f_pad):
    S = pa.shape[0]
    T = _tile(S, 512, 8)
    nb = S // T
    f_col = OFF_F // LANES

    def kern(dr_ref, dc_ref, z_ref, b_ref, df_ref, dbf_ref, carry_ref):
        @pl.when(pl.program_id(0) == 0)
        def _():
            carry_ref[...] = jnp.zeros_like(carry_ref)
            dbf_ref[...] = jnp.zeros_like(dbf_ref)

        row = lax.broadcasted_iota(jnp.int32, (T, T), 0)
        col = lax.broadcasted_iota(jnp.int32, (T, T), 1)
        tri = (col >= row).astype(BF)
        rev = _tri_dot(tri, dr_ref[...] - dc_ref[...]) + carry_ref[...]
        carry_ref[...] = rev[0:1, :]
        z = z_ref[...] + b_ref[...]
        lane = lax.broadcasted_iota(jnp.int32, (T, LANES), 1)
        d_z = jnp.where(lane < B_HEADS, rev * jax.nn.sigmoid(-z), 0.0)
        df_ref[...] = d_z.astype(BF)
        dbf_ref[0:1, :] += _colsum(d_z)

    return pl.pallas_call(
        kern, name="fox_gate_bwd", grid=(nb,),
        in_specs=[pl.BlockSpec((T, LANES), lambda i: (nb - 1 - i, 0)),
                  pl.BlockSpec((T, LANES), lambda i: (nb - 1 - i, 0)),
                  pl.BlockSpec((T, LANES), lambda i: (nb - 1 - i, f_col)),
                  _vec_spec(LANES)],
        out_specs=[pl.BlockSpec((T, LANES), lambda i: (nb - 1 - i, 0)),
                   pl.BlockSpec((8, LANES), lambda i: (0, 0))],
        out_shape=[jax.ShapeDtypeStruct((S, LANES), BF), jax.ShapeDtypeStruct((8, LANES), F32)],
        scratch_shapes=[pltpu.VMEM((1, LANES), F32)],
        compiler_params=_cparams("arbitrary"),
    )(rowsum_ds, colsum_ds, pa, b_f_pad)


NEG_INF = float("-inf")
QK_SCALE = 1.0 / math.sqrt(HEAD_DIM)


def _half_mask(shape, half):
    lane = lax.broadcasted_iota(jnp.int32, shape, 1)
    return (lane < HEAD_DIM) if half == 0 else (lane >= HEAD_DIM)


def _bias_block(shape, terms, term_off, ones_lo, ones_hi):
    l64 = lax.broadcasted_iota(jnp.int32, shape, 1) & (HEAD_DIM - 1)
    out = jnp.where((l64 >= ones_lo) & (l64 < ones_hi), 1.0, 0.0)
    for t, term in enumerate(terms):
        out = jnp.where(l64 == term_off + t, term.astype(F32), out)
    return out


def _head_column(block, head):
    lane = lax.broadcasted_iota(jnp.int32, block.shape, 1)
    return jnp.sum(jnp.where(lane == head, block, 0.0), axis=1, keepdims=True)


def _crossed(shape, first, second):
    return jnp.where(_half_mask(shape, 0), second, first)


def _fox_prep_fwd(cum, T):
    S = cum.shape[0]
    shape = (T, LANES)

    def kern(c_ref, bq_ref, bk_ref):
        p_id = pl.program_id(0)
        cum_blk = c_ref[...]
        c3 = _split3(_crossed(shape, _head_column(cum_blk, 2 * p_id), _head_column(cum_blk, 2 * p_id + 1)))
        bq_ref[...] = _bias_block(shape, c3, 0, 3, 6).astype(BF)
        bk_ref[...] = _bias_block(shape, [-t.astype(F32) for t in c3], 3, 0, 3).astype(BF)

    out_spec = pl.BlockSpec((None, T, LANES), lambda p, i: (p, i, 0))
    out_shape = jax.ShapeDtypeStruct((B_HEADS // 2, S, LANES), BF)
    return pl.pallas_call(
        kern, name="fox_prep_fwd", grid=(B_HEADS // 2, S // T),
        in_specs=[pl.BlockSpec((T, LANES), lambda p, i: (i, 0))],
        out_specs=[out_spec, out_spec], out_shape=[out_shape, out_shape],
        compiler_params=_cparams("parallel", "parallel"),
    )(cum)


def _fox_fwd(p_b, bq, bk, T, comm=None):
    S = p_b.shape[0]
    nq = S // T
    n_pairs = B_HEADS // 2
    grid = (n_pairs, nq)

    def kern(*refs):
        (q_ref, k_ref, v_ref, bq_ref, bk_ref, o_ref, lse_ref), comm_refs = _own_refs(refs, comm, 5, 2, 0)
        _comm_edge(comm, comm_refs, grid, first=True)
        i = pl.program_id(1)
        rowcol = lax.broadcasted_iota(jnp.int32, (T, T), 0) - lax.broadcasted_iota(jnp.int32, (T, T), 1)
        hms = (_half_mask((T, LANES), 0), _half_mask((T, LANES), 1))
        q_scaled = (q_ref[...].astype(F32) * QK_SCALE).astype(BF)
        bq_blk = bq_ref[...]
        qs = [jnp.where(hms[h], q_scaled, bq_blk) for h in (0, 1)]

        def step(j, carry, masked):
            rows = pl.ds(pl.multiple_of(j * T, T), T)
            kj, bkj, vj = k_ref[rows, :], bk_ref[rows, :], v_ref[rows, :]
            new = []
            for half in (0, 1):
                m, l, acc = carry[half]
                s = lax.dot_general(qs[half], jnp.where(hms[half], kj, bkj), (((1,), (1,)), ((), ())),
                                    preferred_element_type=F32)
                if masked:
                    s = jnp.where(rowcol >= 0, s, NEG_INF)
                m_new = jnp.maximum(m, jnp.max(s, axis=1, keepdims=True))
                alpha = jnp.exp(m - m_new)
                p = jnp.exp(s - m_new)
                l_new = alpha * l + jnp.sum(p, axis=1, keepdims=True)
                acc_new = alpha * acc + jnp.dot(p.astype(BF), vj, preferred_element_type=F32)
                new.append((m_new, l_new, acc_new))
            return tuple(new)

        one = (jnp.full((T, 1), NEG_INF, F32), jnp.zeros((T, 1), F32), jnp.zeros((T, LANES), F32))
        carry = lax.fori_loop(0, i, functools.partial(step, masked=False), (one, one))
        (m0, l0, acc0), (m1, l1, acc1) = step(i, carry, True)
        hm0 = _half_mask((T, LANES), 0)
        o_ref[...] = jnp.where(hm0, acc0 / l0, acc1 / l1)
        lse_ref[...] = jnp.where(hm0, m0 + jnp.log(l0), m1 + jnp.log(l1))
        _comm_edge(comm, comm_refs, grid, first=False)

    out_spec = pl.BlockSpec((T, LANES), lambda p, i: (i, p))
    res = pl.pallas_call(
        kern, name="fox_fwd", grid=grid,
        in_specs=[pl.BlockSpec((T, LANES), lambda p, i: (i, OFF_QB // LANES + p)),
                  pl.BlockSpec((S, LANES), lambda p, i: (0, OFF_KB // LANES + p)),
                  pl.BlockSpec((S, LANES), lambda p, i: (0, OFF_VB // LANES + p)),
                  pl.BlockSpec((None, T, LANES), lambda p, i: (p, i, 0)),
                  pl.BlockSpec((None, S, LANES), lambda p, i: (p, 0, 0))] + _comm_specs(comm, "in"),
        out_specs=[out_spec, out_spec] + _comm_specs(comm, "out"),
        out_shape=[jax.ShapeDtypeStruct((S, n_pairs * LANES), F32)] * 2 + (comm.out_shapes if comm else []),
        scratch_shapes=comm.sem_shapes if comm else [],
        compiler_params=_cparams("arbitrary", "arbitrary"),
    )(p_b, p_b, p_b, bq, bk, *(comm.ins if comm else []))
    return res[0], res[1], res[2:]


def _fox_prep_bwd(cum, o, do, lse, T):
    S = o.shape[0]
    shape = (T, LANES)

    def kern(c_ref, o_ref, do_ref, lse_ref, bq_ref, bdo_ref):
        p_id = pl.program_id(0)
        cum_blk = c_ref[...]
        cq = _crossed(shape, _head_column(cum_blk, 2 * p_id), _head_column(cum_blk, 2 * p_id + 1))
        b3 = _split3(cq - pltpu.roll(lse_ref[...], HEAD_DIM, 1))
        bq_ref[...] = _bias_block(shape, b3, 0, 3, 6).astype(BF)
        dd = do_ref[...] * o_ref[...]
        delta = [jnp.sum(jnp.where(_half_mask(shape, h), dd, 0.0), axis=1, keepdims=True) for h in (0, 1)]
        d3 = _split3(-_crossed(shape, delta[0], delta[1]))
        bdo_ref[...] = _bias_block(shape, d3, 0, 0, 0).astype(BF)

    block = pl.BlockSpec((None, T, LANES), lambda p, i: (p, i, 0))
    tile = pl.BlockSpec((T, LANES), lambda p, i: (i, p))
    out_shape = jax.ShapeDtypeStruct((B_HEADS // 2, S, LANES), BF)
    return pl.pallas_call(
        kern, name="fox_prep_bwd", grid=(B_HEADS // 2, S // T),
        in_specs=[pl.BlockSpec((T, LANES), lambda p, i: (i, 0)), tile, tile, tile],
        out_specs=[block, block], out_shape=[out_shape, out_shape],
        compiler_params=_cparams("parallel", "parallel"),
    )(cum, o, do, lse)


def _fox_bwd(p_b, do, bq, bk, bdo, T, comm=None):
    S = p_b.shape[0]
    n_pairs = B_HEADS // 2
    nq = S // T
    grid = (n_pairs,)

    def kern(*refs):
        own, comm_refs = _own_refs(refs, comm, 7, 5, 0)
        q_ref, k_ref, v_ref, do_ref, bq_ref, bk_ref, bdo_ref, dq_ref, dk_ref, dv_ref, dck_ref, dcq_ref = own
        _comm_edge(comm, comm_refs, grid, first=True)
        p_id = pl.program_id(0)
        rowcol = lax.broadcasted_iota(jnp.int32, (T, T), 0) - lax.broadcasted_iota(jnp.int32, (T, T), 1)
        lane = lax.broadcasted_iota(jnp.int32, (T, LANES), 1)
        dk_ref[...] = jnp.zeros_like(dk_ref)
        dv_ref[...] = jnp.zeros_like(dv_ref)
        dck_ref[...] = jnp.zeros_like(dck_ref)

        @pl.when(p_id == 0)
        def _():
            dcq_ref[...] = jnp.zeros_like(dcq_ref)

        hms = (_half_mask((T, LANES), 0), _half_mask((T, LANES), 1))
        v_ones = _bias_block((T, LANES), [], 0, 0, 3).astype(BF)

        def outer(i, carry):
            qrows = pl.ds(pl.multiple_of(i * T, T), T)
            q_scaled = (q_ref[qrows, :].astype(F32) * QK_SCALE).astype(BF)
            do_b = do_ref[qrows, :].astype(BF)
            bq_i, bdo_i = bq_ref[qrows, :], bdo_ref[qrows, :]
            qa = [jnp.where(hms[h], q_scaled, bq_i) for h in (0, 1)]
            doa = [jnp.where(hms[h], do_b, bdo_i) for h in (0, 1)]
            q_own = [jnp.where(hms[h], q_scaled, 0) for h in (0, 1)]
            do_own = [jnp.where(hms[h], do_b, 0) for h in (0, 1)]

            def inner(j, carry_in, masked):
                krows = pl.ds(pl.multiple_of(j * T, T), T)
                kj, bkj, vj = k_ref[krows, :], bk_ref[krows, :], v_ref[krows, :]
                dv_add, dk_add, new = 0.0, 0.0, []
                for half in (0, 1):
                    dq, rs = carry_in[half]
                    ka = jnp.where(hms[half], kj, bkj)
                    s = lax.dot_general(qa[half], ka, (((1,), (1,)), ((), ())), preferred_element_type=F32)
                    if masked:
                        s = jnp.where(rowcol >= 0, s, NEG_INF)
                    p = jnp.exp(s)
                    ds = p * lax.dot_general(doa[half], jnp.where(hms[half], vj, v_ones),
                                             (((1,), (1,)), ((), ())), preferred_element_type=F32)
                    ds_b = ds.astype(BF)
                    dv_add = dv_add + lax.dot_general(p.astype(BF), do_own[half], (((0,), (0,)), ((), ())),
                                                      preferred_element_type=F32)
                    dk_add = dk_add + lax.dot_general(ds_b, q_own[half], (((0,), (0,)), ((), ())),
                                                      preferred_element_type=F32)
                    dck_ref[half:half + 1, krows] += jnp.sum(ds, axis=0, keepdims=True)
                    new.append((dq + jnp.dot(ds_b, jnp.where(hms[half], kj, 0), preferred_element_type=F32),
                                rs + jnp.sum(ds, axis=1, keepdims=True)))
                dv_ref[krows, :] += dv_add
                dk_ref[krows, :] += dk_add
                return tuple(new)

            one = (jnp.zeros((T, LANES), F32), jnp.zeros((T, 1), F32))
            carry_in = lax.fori_loop(0, i, functools.partial(inner, masked=False), (one, one))
            (dq0, rs0), (dq1, rs1) = inner(i, carry_in, True)
            dq_ref[qrows, :] = (dq0 + dq1) * QK_SCALE
            dcq_ref[qrows, :] = jnp.where(lane == 2 * p_id, rs0, jnp.where(lane == 2 * p_id + 1, rs1,
                                                                             dcq_ref[qrows, :]))
            return carry

        lax.fori_loop(0, nq, outer, 0)
        _comm_edge(comm, comm_refs, grid, first=False)

    block = pl.BlockSpec((None, S, LANES), lambda p: (p, 0, 0))
    pair = pl.BlockSpec((S, LANES), lambda p: (0, p))
    slab = lambda off: pl.BlockSpec((S, LANES), lambda p: (0, off // LANES + p))
    wide = jax.ShapeDtypeStruct((S, n_pairs * LANES), F32)
    res = pl.pallas_call(
        kern, name="fox_bwd", grid=grid,
        in_specs=[slab(OFF_QB), slab(OFF_KB), slab(OFF_VB), pair, block, block, block]
        + _comm_specs(comm, "in"),
        out_specs=[pair, pair, pair, pl.BlockSpec((None, 2, S), lambda p: (p, 0, 0)),
                   pl.BlockSpec((S, LANES), lambda p: (0, 0))] + _comm_specs(comm, "out"),
        out_shape=[wide, wide, wide, jax.ShapeDtypeStruct((n_pairs, 2, S), F32),
                   jax.ShapeDtypeStruct((S, LANES), F32)] + (comm.out_shapes if comm else []),
        scratch_shapes=comm.sem_shapes if comm else [],
        compiler_params=_cparams("arbitrary"),
    )(p_b, p_b, p_b, do, bq, bk, bdo, *(comm.ins if comm else []))
    return (*res[:5], res[5:])


SWA_TQ = 128
SWA_SUB = 32


def _swa_window(i, tq):
    start = pl.multiple_of(jnp.maximum(i * tq - WINDOW, 0), LANES)
    return start, i * tq - start


def _swa_valid(offset, tq):
    rel = offset + lax.broadcasted_iota(jnp.int32, (tq, tq + WINDOW), 0) \
        - lax.broadcasted_iota(jnp.int32, (tq, tq + WINDOW), 1)
    return (rel >= 0) & (rel < WINDOW)


def _swa_fwd(qk, v_arr, v_col, sinks):
    S = qk.shape[0]
    tq = min(SWA_TQ, S - WINDOW)
    sub = min(SWA_SUB, S // tq)
    win = tq + WINDOW

    def kern(q_ref, k_ref, v_ref, sink_ref, o_ref, lse_ref):
        p_id, i = pl.program_id(0), pl.program_id(1)
        hm0 = _half_mask((tq, LANES), 0)
        for t in range(sub):
            rows = slice(t * tq, (t + 1) * tq)
            start, offset = _swa_window(i * sub + t, tq)
            kw = k_ref[pl.ds(start, win), :]
            vw = v_ref[pl.ds(start, win), :].astype(BF)
            valid = _swa_valid(offset, tq)
            q = q_ref[rows, :]
            outs, lses = [], []
            for half in (0, 1):
                hm = _half_mask((tq, LANES), half)
                qh = (jnp.where(hm, q, 0).astype(F32) * QK_SCALE).astype(BF)
                s = lax.dot_general(qh, kw, (((1,), (1,)), ((), ())), preferred_element_type=F32)
                s = jnp.where(valid, s, NEG_INF)
                sink = sink_ref[2 * p_id + half]
                m = jnp.maximum(jnp.max(s, axis=1, keepdims=True), sink)
                p = jnp.exp(s - m)
                denom = jnp.sum(p, axis=1, keepdims=True) + jnp.exp(sink - m)
                outs.append(jnp.dot(p.astype(BF), vw, preferred_element_type=F32) / denom)
                lses.append(m + jnp.log(denom))
            o_ref[rows, :] = jnp.where(hm0, outs[0], outs[1])
            lse_ref[rows, :] = jnp.where(hm0, lses[0], lses[1])

    tile = pl.BlockSpec((sub * tq, LANES), lambda p, i: (i, p))
    return pl.pallas_call(
        kern, name="swa_fwd", grid=(A_Q_HEADS // 2, S // (sub * tq)),
        in_specs=[tile, pl.BlockSpec((S, LANES), lambda p, i: (0, A_Q_HEADS // 2)),
                  pl.BlockSpec((S, LANES), lambda p, i: (0, v_col)),
                  pl.BlockSpec(memory_space=pltpu.SMEM)],
        out_specs=[tile, tile],
        out_shape=[jax.ShapeDtypeStruct((S, A_Q_HEADS * HEAD_DIM), F32)] * 2,
        compiler_params=_cparams("parallel", "arbitrary"),
    )(qk, qk, v_arr, sinks)


def _swa_bwd(qk, v_arr, v_col, o_arr, do_arr, lse_arr, sinks, comm=None):
    S = qk.shape[0]
    tq = min(SWA_TQ, S - WINDOW)
    sub = min(SWA_SUB, S // tq)
    win = tq + WINDOW
    n_pairs = A_Q_HEADS // 2
    grid = (n_pairs, S // (sub * tq))

    def kern(*refs):
        own, comm_refs = _own_refs(refs, comm, 7, 4, 0)
        q_ref, k_ref, v_ref, o_ref, do_ref, lse_ref, sink_ref, dq_ref, dk_ref, dv_ref, dsink_ref = own
        _comm_edge(comm, comm_refs, grid, first=True)
        p_id, i = pl.program_id(0), pl.program_id(1)

        @pl.when((p_id == 0) & (i == 0))
        def _():
            dk_ref[...] = jnp.zeros_like(dk_ref)
            dv_ref[...] = jnp.zeros_like(dv_ref)

        @pl.when(i == 0)
        def _():
            dsink_ref[...] = jnp.zeros_like(dsink_ref)

        for t in range(sub):
            rows = slice(t * tq, (t + 1) * tq)
            start, offset = _swa_window(i * sub + t, tq)
            wrows = pl.ds(start, win)
            kw = k_ref[wrows, :]
            vw = v_ref[wrows, :].astype(BF)
            valid = _swa_valid(offset, tq)
            q, do, o, lse2 = q_ref[rows, :], do_ref[rows, :], o_ref[rows, :], lse_ref[rows, :]
            dq = jnp.zeros((tq, LANES), F32)
            dk = jnp.zeros((win, LANES), F32)
            dv = jnp.zeros((win, LANES), F32)
            for half in (0, 1):
                hm = _half_mask((tq, LANES), half)
                lane0 = half * HEAD_DIM
                qh = (jnp.where(hm, q, 0).astype(F32) * QK_SCALE).astype(BF)
                do_f = jnp.where(hm, do, 0.0)
                doh = do_f.astype(BF)
                delta = jnp.sum(do_f * o, axis=1, keepdims=True)
                lse = lse2[:, lane0:lane0 + 1]
                s = lax.dot_general(qh, kw, (((1,), (1,)), ((), ())), preferred_element_type=F32)
                p = jnp.exp(jnp.where(valid, s, NEG_INF) - lse)
                dp = lax.dot_general(doh, vw, (((1,), (1,)), ((), ())), preferred_element_type=F32)
                ds_b = (p * (dp - delta)).astype(BF)
                dv = dv + lax.dot_general(p.astype(BF), doh, (((0,), (0,)), ((), ())),
                                          preferred_element_type=F32)
                dk = dk + lax.dot_general(ds_b, qh, (((0,), (0,)), ((), ())), preferred_element_type=F32)
                kh = jnp.where(_half_mask((win, LANES), half), kw, 0)
                dq = dq + jnp.dot(ds_b, kh, preferred_element_type=F32)
                p_sink = jnp.exp(sink_ref[2 * p_id + half] - lse)
                dsink_ref[0, half:half + 1, :] += jnp.broadcast_to(
                    -jnp.sum(p_sink * delta, axis=0, keepdims=True), (1, LANES))
            dq_ref[rows, :] = dq * QK_SCALE
            dk_ref[wrows, :] += dk
            dv_ref[wrows, :] += dv
        _comm_edge(comm, comm_refs, grid, first=False)

    tile = pl.BlockSpec((sub * tq, LANES), lambda p, i: (i, p))
    whole = lambda col: pl.BlockSpec((S, LANES), lambda p, i: (0, col))
    res = pl.pallas_call(
        kern, name="swa_bwd", grid=grid,
        in_specs=[tile, whole(n_pairs), whole(v_col), tile, tile, tile,
                  pl.BlockSpec(memory_space=pltpu.SMEM)] + _comm_specs(comm, "in"),
        out_specs=[tile, whole(0), whole(0),
                   pl.BlockSpec((1, 8, LANES), lambda p, i: (p, 0, 0))] + _comm_specs(comm, "out"),
        out_shape=[jax.ShapeDtypeStruct((S, A_Q_HEADS * HEAD_DIM), F32),
                   jax.ShapeDtypeStruct((S, LANES), F32), jax.ShapeDtypeStruct((S, LANES), F32),
                   jax.ShapeDtypeStruct((n_pairs, 8, LANES), F32)] + (comm.out_shapes if comm else []),
        scratch_shapes=comm.sem_shapes if comm else [],
        compiler_params=_cparams("arbitrary", "arbitrary"),
    )(qk, qk, v_arr, o_arr, do_arr, lse_arr, sinks, *(comm.ins if comm else []))
    return (*res[:4], res[4:])


ADAMW_BLOCK = 512 * 1024


def _adamw(w, g, m, v, name, comm=None):
    R, C = w.shape
    tr, tc = _tile(R, max(8, ADAMW_BLOCK // C), 8), C
    grid = (R // tr, C // tc)

    def kern(*refs):
        (w_ref, g_ref, m_ref, v_ref, d_ref, mo_ref, vo_ref), comm_refs = _own_refs(refs, comm, 4, 3, 0)
        _comm_edge(comm, comm_refs, grid, first=True)
        g_ = g_ref[...]
        m_new = ADAM_B1 * m_ref[...] + (1.0 - ADAM_B1) * g_
        v_new = ADAM_B2 * v_ref[...] + (1.0 - ADAM_B2) * (g_ * g_)
        m_hat = m_new / (1.0 - ADAM_B1 ** ADAM_STEP)
        v_hat = v_new / (1.0 - ADAM_B2 ** ADAM_STEP)
        d_ref[...] = -ADAM_LR * (m_hat / (jnp.sqrt(v_hat) + ADAM_EPS) + ADAM_WD * w_ref[...])
        mo_ref[...] = m_new
        vo_ref[...] = v_new
        _comm_edge(comm, comm_refs, grid, first=False)

    spec = pl.BlockSpec((tr, tc), lambda i, j: (i, j))
    shape = jax.ShapeDtypeStruct((R, C), F32)
    res = pl.pallas_call(
        kern, name=name, grid=grid,
        in_specs=[spec] * 4 + _comm_specs(comm, "in"),
        out_specs=[spec] * 3 + _comm_specs(comm, "out"),
        out_shape=[shape] * 3 + (comm.out_shapes if comm else []),
        scratch_shapes=comm.sem_shapes if comm else [],
        input_output_aliases={4 + i: 3 + o for i, o in comm.aliases.items()} if comm else {},
        compiler_params=_cparams("arbitrary", "arbitrary"),
    )(w, g, m, v, *(comm.ins if comm else []))
    return (res[:3], res[3:]) if comm else res


def _index_operand(i):
    return jnp.reshape(i, (1,)).astype(jnp.int32)


def _add_pair(whole, got, ci, name):
    P, R, C = whole.shape
    half = R // 2
    tr = _tile(half, ROWS, 16)
    nb = half // tr

    def kern(ci_ref, a_ref, b_ref, o_ref, ob_ref):
        s = a_ref[...] + b_ref[...].astype(F32)
        o_ref[...] = s
        ob_ref[...] = s.astype(BF)

    spec = pl.BlockSpec((None, tr, C), lambda p, i, ci_ref: (p, i, 0))
    return pl.pallas_call(
        kern, name=name,
        grid_spec=pltpu.PrefetchScalarGridSpec(
            num_scalar_prefetch=1, grid=(P, nb),
            in_specs=[pl.BlockSpec((None, tr, C), lambda p, i, ci_ref: (p, ci_ref[0] * nb + i, 0)), spec],
            out_specs=[spec, spec]),
        out_shape=[jax.ShapeDtypeStruct((P, half, C), F32), jax.ShapeDtypeStruct((P, half, C), BF)],
        compiler_params=_cparams("parallel", "parallel"),
    )(_index_operand(ci), whole, got)


def _add_three(parts, recv, chip, name):
    _, R, C = parts.shape
    tr = _tile(R, ROWS, 16)

    def kern(chip_ref, o_ref, r0_ref, r1_ref, r2_ref, out_ref):
        s = ((o_ref[...] + r0_ref[...].astype(F32)) + r1_ref[...].astype(F32)) + r2_ref[...].astype(F32)
        out_ref[0] = s
        out_ref[1] = s

    slab = lambda k: pl.BlockSpec((None, tr, C), lambda i, chip_ref: (k, i, 0))
    return pl.pallas_call(
        kern, name=name,
        grid_spec=pltpu.PrefetchScalarGridSpec(
            num_scalar_prefetch=1, grid=(R // tr,),
            in_specs=[pl.BlockSpec((None, tr, C), lambda i, chip_ref: (chip_ref[0], i, 0)),
                      slab(0), slab(1), slab(2)],
            out_specs=pl.BlockSpec((2, tr, C), lambda i, chip_ref: (0, i, 0))),
        out_shape=jax.ShapeDtypeStruct((2, R, C), F32),
        compiler_params=_cparams("parallel"),
    )(_index_operand(chip), parts, recv, recv, recv)


SM_ADA, SM_G, SM_LOSS, SM_BF, SM_SINK, SM_LEN = 0, 6144, 10240, 11264, 11272, 12288


def _small_finalize(gathered):
    def kern(g_ref, tot_ref, loss_ref):
        tot = g_ref[0:1, :]
        for b in range(1, N_DEV):
            tot = tot + g_ref[b:b + 1, :]
        tot_ref[...] = tot
        sq = jnp.sum(tot[:, SM_LOSS:SM_LOSS + D_MODEL], axis=1, keepdims=True)
        loss_ref[...] = jnp.broadcast_to(sq * (0.5 / D_MODEL), (1, LANES))

    full = lambda shape: pl.BlockSpec(shape, lambda i: (0, 0))
    return pl.pallas_call(
        kern, name="small_finalize", grid=(1,),
        in_specs=[full((N_DEV, SM_LEN))],
        out_specs=[full((1, SM_LEN)), full((1, LANES))],
        out_shape=[jax.ShapeDtypeStruct((1, SM_LEN), F32), jax.ShapeDtypeStruct((1, LANES), F32)],
        compiler_params=_cparams("arbitrary"),
    )(gathered)


def _ada_dw(c_t, d_ada):
    N = d_ada.shape[1]
    tn = _tile(N, 512)

    def kern(c_ref, d_ref, o_ref):
        acc = c_ref[:, 0:1] * d_ref[0:1, :]
        for b in range(1, N_DEV):
            acc = acc + c_ref[:, b:b + 1] * d_ref[b:b + 1, :]
        o_ref[...] = acc

    return pl.pallas_call(
        kern, name="ada_dw", grid=(N // tn,),
        in_specs=[pl.BlockSpec((D_MODEL, N_DEV), lambda j: (0, 0)), pl.BlockSpec((N_DEV, tn), lambda j: (0, j))],
        out_specs=pl.BlockSpec((D_MODEL, tn), lambda j: (0, j)),
        out_shape=jax.ShapeDtypeStruct((D_MODEL, N), F32),
        compiler_params=_cparams("parallel"),
    )(c_t, d_ada)


def _here():
    return lax.axis_index("x"), lax.axis_index("y"), lax.axis_index("c")


def _other_chips(x, y):
    return [(1 - x, y), (x, 1 - y), (1 - x, 1 - y)]


_ANY = pl.BlockSpec(memory_space=pl.ANY)


class _Comm:
    def __init__(self, ins, out_shapes, sem_shapes, start, finish, aliases=None):
        self.ins, self.out_shapes, self.sem_shapes = list(ins), list(out_shapes), list(sem_shapes)
        self.start, self.finish = start, finish
        self.aliases = dict(aliases or {})

    def split(self, refs, n_in, n_out, n_scratch):
        a = n_in + len(self.ins)
        b = a + n_out + len(self.out_shapes)
        own = list(refs[:n_in]) + list(refs[a:a + n_out]) + list(refs[b:b + n_scratch])
        mine = (refs[n_in:a], refs[a + n_out:b], refs[b + n_scratch:])
        return own, mine


def _run_comm(comm, name):
    n_in, n_out = len(comm.ins), len(comm.out_shapes)

    def body(*refs):
        parts = (refs[:n_in], refs[n_in:n_in + n_out], refs[n_in + n_out:])
        comm.start(*parts)
        comm.finish(*parts)

    return pl.pallas_call(
        body, name=name,
        in_specs=[_ANY] * n_in, out_specs=[_ANY] * n_out,
        out_shape=comm.out_shapes, scratch_shapes=comm.sem_shapes,
        input_output_aliases=comm.aliases,
    )(*comm.ins)


def _gather_comm(blocks):
    L = len(blocks)

    def parts(ins, outs, sems):
        send_sems, recv_sems, local_sems = sems
        x, y, c = _here()
        me, sibling = (x, y, c), (x, y, 1 - c)
        chips = _other_chips(x, y)

        def slot(px, py, pc):
            return 4 * px + 2 * py + pc

        def copy(l, k, block, to, src=None):
            dst = outs[l].at[slot(*block)]
            return pltpu.make_async_remote_copy(
                src_ref=dst if src is None else src, dst_ref=dst,
                send_sem=send_sems.at[l, k], recv_sem=recv_sems.at[l, k],
                device_id=to, device_id_type=MESH)

        mine = [pltpu.make_async_copy(ins[l], outs[l].at[slot(*me)], local_sems.at[l]) for l in range(L)]
        first = []
        for l in range(L):
            first.append(copy(l, 0, me, sibling, src=ins[l]))
            for j, chip in enumerate(chips):
                first.append(copy(l, 1 + j, me, (*chip, c), src=ins[l]))
        return c, me, sibling, chips, copy, mine, first

    def start(ins, outs, sems):
        *_, mine, first = parts(ins, outs, sems)
        for cp in mine + first:
            cp.start()

    def finish(ins, outs, sems):
        c, me, sibling, chips, copy, mine, first = parts(ins, outs, sems)
        passed = []
        for j, chip in enumerate(chips):
            for l in range(L):
                copy(l, 1 + j, (*chip, c), me).wait_recv()
                fwd = copy(l, 4 + j, (*chip, c), sibling)
                fwd.start()
                passed.append(fwd)
        for l in range(L):
            copy(l, 0, sibling, me).wait_recv()
        for j, chip in enumerate(chips):
            for l in range(L):
                copy(l, 4 + j, (*chip, 1 - c), me).wait_recv()
        for cp in first + passed:
            cp.wait_send()
        for cp in mine:
            cp.wait()

    return _Comm(blocks, [jax.ShapeDtypeStruct((N_DEV,) + b.shape, b.dtype) for b in blocks],
                 [pltpu.SemaphoreType.DMA((L, 7)), pltpu.SemaphoreType.DMA((L, 7)), pltpu.SemaphoreType.DMA((L,))],
                 start, finish)


def _allgather8(blocks, name):
    return _run_comm(_gather_comm(blocks), name)


def _swap_comm(arrs):
    L = len(arrs)

    def copies(ins, outs, sems):
        send_sems, recv_sems = sems
        x, y, c = _here()
        cps = []
        for l in range(L):
            half = arrs[l].shape[1] // 2
            rows = pl.ds(pl.multiple_of((1 - c) * half, 16), half)
            cps.append(pltpu.make_async_remote_copy(
                src_ref=ins[l].at[:, rows, :], dst_ref=outs[l], send_sem=send_sems.at[l],
                recv_sem=recv_sems.at[l], device_id=(x, y, 1 - c), device_id_type=MESH))
        return cps

    def start(ins, outs, sems):
        for cp in copies(ins, outs, sems):
            cp.start()

    def finish(ins, outs, sems):
        for cp in copies(ins, outs, sems):
            cp.wait()

    return _Comm(arrs, [jax.ShapeDtypeStruct((a.shape[0], a.shape[1] // 2, a.shape[2]), a.dtype) for a in arrs],
                 [pltpu.SemaphoreType.DMA((L,)), pltpu.SemaphoreType.DMA((L,))], start, finish)


def _join_comm(bufs):
    L = len(bufs)

    def start(ins, outs, sems):
        send_sems, recv_sems = sems
        x, y, c = _here()
        for l in range(L):
            pltpu.make_async_remote_copy(src_ref=outs[l].at[c], dst_ref=outs[l].at[c], send_sem=send_sems.at[l],
                                         recv_sem=recv_sems.at[l], device_id=(x, y, 1 - c),
                                         device_id_type=MESH).start()

    def finish(ins, outs, sems):
        send_sems, recv_sems = sems
        x, y, c = _here()
        for l in range(L):
            pltpu.make_async_remote_copy(src_ref=outs[l].at[c], dst_ref=outs[l].at[1 - c],
                                         send_sem=send_sems.at[l], recv_sem=recv_sems.at[l],
                                         device_id=(x, y, 1 - c), device_id_type=MESH).wait()

    return _Comm(bufs, [jax.ShapeDtypeStruct(a.shape, a.dtype) for a in bufs],
                 [pltpu.SemaphoreType.DMA((L,)), pltpu.SemaphoreType.DMA((L,))], start, finish,
                 aliases={l: l for l in range(L)})


def _scatter_comm(arrs):
    L = len(arrs)

    def copies(ins, outs, sems):
        send_sems, recv_sems = sems
        x, y, c = _here()
        return [pltpu.make_async_remote_copy(
            src_ref=ins[l].at[2 * tx + ty], dst_ref=outs[l].at[j],
            send_sem=send_sems.at[l, j], recv_sem=recv_sems.at[l, j],
            device_id=(tx, ty, c), device_id_type=MESH)
            for l in range(L) for j, (tx, ty) in enumerate(_other_chips(x, y))]

    def start(ins, outs, sems):
        for cp in copies(ins, outs, sems):
            cp.start()

    def finish(ins, outs, sems):
        for cp in copies(ins, outs, sems):
            cp.wait()

    return _Comm(arrs, [jax.ShapeDtypeStruct((3,) + a.shape[1:], a.dtype) for a in arrs],
                 [pltpu.SemaphoreType.DMA((L, 3)), pltpu.SemaphoreType.DMA((L, 3))], start, finish)


_A_ORDER = np.array(A_HEAD_ORDER)
_A_INVERSE = np.argsort(_A_ORDER)


def _permute_in_weights(w_in):
    qa = w_in[:, 0:512].reshape(D_MODEL, A_Q_HEADS, HEAD_DIM)[:, _A_ORDER, :].reshape(D_MODEL, 512)
    f_pad = jnp.pad(w_in[:, 2304:2312], ((0, 0), (0, LANES - B_HEADS)))
    w_a = jnp.concatenate([qa, w_in[:, 512:640], f_pad], axis=1)
    return w_a, w_in[:, 640:2304], w_in[:, 2312:4360]


def _slab_segments():
    segs = [(h * HEAD_DIM, int(_A_INVERSE[h]) * HEAD_DIM, HEAD_DIM) for h in range(A_Q_HEADS)]
    segs += [(512, OFF_KA, 128), (640, W_A + OFF_VA, 128), (768, W_A + OFF_QB, 1536),
             (2304, OFF_F, B_HEADS), (2312, W_A + W_B, W_G)]
    return segs


def _shard_slabs(dw_perm):
    R = dw_perm.shape[0]
    tr = _tile(R, 128, 8)
    plan = []
    for k in range(N_CHIP):
        for b in range(W_SHARD_PAD // LANES):
            lo, hi = k * W_SHARD + b * LANES, min(k * W_SHARD + (b + 1) * LANES, (k + 1) * W_SHARD)
            parts = []
            for o0, s0, n in _slab_segments():
                a, z = max(lo, o0), min(hi, o0 + n)
                while a < z:
                    s = s0 + (a - o0)
                    run = min(z - a, LANES - s % LANES)
                    parts.append((s // LANES, ((a - lo) - s % LANES) % LANES, a - lo, run))
                    a += run
            plan.append((k, b, parts))

    def kern(x_ref, o32_ref, obf_ref):
        lane = lax.broadcasted_iota(jnp.int32, (tr, LANES), 1)
        for k, b, parts in plan:
            acc = jnp.zeros((tr, LANES), F32)
            for src, rot, first, run in parts:
                blk = x_ref[:, src * LANES:(src + 1) * LANES]
                if rot:
                    blk = pltpu.roll(blk, rot, 1)
                acc = jnp.where((lane >= first) & (lane < first + run), blk, acc)
            o32_ref[k, :, b * LANES:(b + 1) * LANES] = acc
            obf_ref[k, :, b * LANES:(b + 1) * LANES] = acc.astype(BF)

    out_spec = pl.BlockSpec((N_CHIP, tr, W_SHARD_PAD), lambda i: (0, i, 0))
    return tuple(pl.pallas_call(
        kern, name="shard_slabs", grid=(R // tr,),
        in_specs=[pl.BlockSpec((tr, W_PERM), lambda i: (i, 0))],
        out_specs=[out_spec, out_spec],
        out_shape=[jax.ShapeDtypeStruct((N_CHIP, R, W_SHARD_PAD), F32),
                   jax.ShapeDtypeStruct((N_CHIP, R, W_SHARD_PAD), BF)],
        compiler_params=_cparams("parallel"),
    )(dw_perm))


class _NoExchange:
    def __init__(self, w_in, rest):
        self.w_in_whole, self.rest, self.grads = w_in, rest, {}

    def w_in_comm(self):
        return None

    def w_in(self, outs):
        return self.w_in_whole

    def rest_weights_comm(self):
        return None

    def rest_weights(self, outs):
        return self.rest

    def swap_comm(self, pieces, tag):
        self.grads[tag] = [p32 for p32, _ in pieces]
        return None

    def swap_done(self, outs, tag):
        return None

    def reduce_done(self, outs, tag):
        pass

    def join_comm(self):
        return None


class _Exchange:
    def __init__(self, ci, chip, w_in_shard, rest_shards):
        self.ci, self.chip, self.w_in_shard, self.rest_shards = ci, chip, w_in_shard, rest_shards
        self.pieces, self.part_f32, self.halves = {}, {}, {}

    def _my_half(self, a, axis=0, other=False):
        rows = a.shape[axis] // 2
        return lax.dynamic_slice_in_dim(a, ((1 - self.ci) if other else self.ci) * rows, rows, axis=axis)

    def w_in_comm(self):
        return _gather_comm([self._my_half(self.w_in_shard).astype(BF)])

    def w_in(self, outs):
        return _col_sharded(outs[0])

    def rest_weights_comm(self):
        return _gather_comm([self._my_half(w).astype(BF) for w in self.rest_shards])

    def rest_weights(self, outs):
        w_ba, w_bb, w_out, w_fi, w_fo = outs
        return (_col_sharded(w_ba), _col_sharded(w_bb), _row_sharded(w_out), _col_sharded(w_fi),
                _row_sharded(w_fo))

    def swap_comm(self, pieces, tag):
        self.pieces[tag] = pieces
        return _swap_comm([pbf for _, pbf in pieces])

    def swap_done(self, got, tag):
        self.part_f32[tag], part_bf = [], []
        for l, ((p32, _), g_) in enumerate(zip(self.pieces[tag], got)):
            s32, sbf = _add_pair(p32, g_, self.ci, f"chip_sum_{tag}_{l}")
            self.part_f32[tag].append(s32)
            part_bf.append(sbf)
        return _scatter_comm(part_bf)

    def reduce_done(self, outs, tag):
        self.halves[tag] = [_add_three(p32, r, self.chip, f"shard_sum_{tag}_{l}")
                            for l, (p32, r) in enumerate(zip(self.part_f32[tag], outs))]

    def join_comm(self):
        return _join_comm(self.halves["late"] + self.halves["early"])


def _col_sharded(g):
    return jnp.transpose(g.reshape(N_CHIP, -1, g.shape[-1]), (1, 0, 2)).reshape(2 * g.shape[1], N_CHIP * g.shape[-1])


def _row_sharded(g):
    return g.reshape(N_DEV * g.shape[1], g.shape[-1])


def _rope_tables(pos):
    inv_freq = 1.0 / (ROPE_THETA ** (jnp.arange(0, HEAD_DIM, 2, dtype=F32) / HEAD_DIM))
    ang = pos.astype(F32)[:, None] * inv_freq
    cos, sin = jnp.cos(ang), jnp.sin(ang)
    return jnp.tile(cos, (1, 4)), jnp.tile(jnp.concatenate([-sin, sin], axis=1), (1, 2))


def _local_step(x, pos, ada, g1, g2, g3, g4, b_f, sinks, exch, target):
    S = x.shape[0]
    t_fox = _tile(S, 512, LANES) if S >= 1024 else S // 2
    t_fox_fwd = _tile(S, 1024, LANES) if S >= 2048 else S // 2
    shift_m, scale_m, gate_m, shift_f, scale_f, gate_f = [ada[i:i + 1] for i in range(N_ADA)]
    cos_t, sin_t = _rope_tables(pos)
    sinks_p = sinks.reshape(A_KV_HEADS, 4).T.reshape(A_Q_HEADS)
    b_f_pad = jnp.pad(b_f, (0, LANES - B_HEADS)).reshape(1, LANES)

    h1, outs = _pre_norm(x, g1, scale_m, shift_m, "pre_mix_norm", comm=exch.w_in_comm())
    w_a, w_b, w_g = _permute_in_weights(exch.w_in(outs))
    w_perm = jnp.concatenate([w_a, w_b, w_g], axis=1)
    p_a = _mm(h1, w_a, "nn", F32, "proj_a")
    p_b = _mm(h1, w_b, "nn", BF, "proj_b")
    p_g = _mm(h1, w_g, "nn", BF, "proj_g")
    (qk_a,) = _rope([p_a], [640], cos_t, sin_t, "rope_fwd")
    o_a, lse_a = _swa_fwd(qk_a, p_b, 0, sinks_p)
    cum = _fox_gate_fwd(p_a, b_f_pad)
    bq, bk = _fox_prep_fwd(cum, t_fox_fwd)
    comm = exch.rest_weights_comm()
    o_b, lse_b, outs = _fox_fwd(p_b, bq, bk, t_fox_fwd, comm=comm)
    w_ba, w_bb, w_out, w_fi, w_fo = exch.rest_weights(outs)
    w_ba_p = w_ba.reshape(A_Q_HEADS, HEAD_DIM, D_MODEL)[_A_ORDER].reshape(512, D_MODEL)
    pa = _mm(o_a, w_ba_p, "nn", BF, "branch_a")
    pb = _mm(o_b, w_bb, "nn", BF, "branch_b")
    merged = _merge_fwd(p_g, pa, pb)
    y1 = _mm(merged, w_out, "nn", BF, "out_proj")
    x2, h2 = _post_pre(x, y1, g2, gate_m, g3, scale_f, shift_f)
    gu = _mm(h2, w_fi, "nn", BF, "ffn_in")
    act = _swiglu_fwd(gu)
    y2 = _mm(act, w_fo, "nn", BF, "ffn_out")
    d_out, d_y2, st_f = _final(x2, y2, g4, gate_f, target)

    d_act = _mm(d_y2, w_fo, "nt", BF, "ffn_out_dx")
    row_pieces = lambda pair: tuple(t.reshape(N_CHIP, t.shape[0] // N_CHIP, t.shape[1]) for t in pair)
    dw_fo = row_pieces(_mm(act, d_y2, "tn", F32, "ffn_out_dw", twin=True))
    d_gu = _swiglu_bwd(d_act, gu)
    d_h2 = _mm(d_gu, w_fi, "nt", BF, "ffn_in_dx")
    dw_fi = _mm(h2, d_gu, "tn", F32, "ffn_in_dw", col_pieces=N_CHIP, twin=True)
    d_x2, d_y1, st_m = _mid_bwd(d_h2, x2, d_out, y1, g3, scale_f, g2, gate_m)
    d_merged = _mm(d_y1, w_out, "nt", BF, "out_proj_dx")
    dw_out = row_pieces(_mm(merged, d_y1, "tn", F32, "out_proj_dw", twin=True))
    d_pa, d_pb, d_ga, d_gb = _merge_bwd(d_merged, p_g, pa, pb)
    d_oa = _mm(d_pa, w_ba_p, "nt", F32, "branch_a_dx")
    dw_ba_p = _mm(o_a, d_pa, "tn", F32, "branch_a_dw", col_pieces=N_CHIP, twin=True)
    d_ob = _mm(d_pb, w_bb, "nt", F32, "branch_b_dx")
    dw_bb = _mm(o_b, d_pb, "tn", F32, "branch_b_dw", col_pieces=N_CHIP, twin=True)
    head_rows = lambda t: t.reshape(N_CHIP, A_Q_HEADS, HEAD_DIM, -1)[:, _A_INVERSE].reshape(t.shape)
    dw_ba = tuple(head_rows(t) for t in dw_ba_p)
    comm = exch.swap_comm([dw_ba, dw_bb, dw_out, dw_fi, dw_fo], "early")
    dq_a, dk_a, dv_a, d_sink, outs = _swa_bwd(qk_a, p_b, 0, o_a, d_oa, lse_a, sinks_p, comm=comm)
    comm = exch.swap_done(outs, "early")
    bq_bwd, bdo = _fox_prep_bwd(cum, o_b, d_ob, lse_b, t_fox_fwd)
    dq_b, dk_b, dv_b, d_ck, d_cq, outs = _fox_bwd(p_b, d_ob, bq_bwd, bk, bdo, t_fox, comm=comm)
    exch.reduce_done(outs, "early")
    d_qa, d_ka = _rope([dq_a, dk_a], [512, LANES], cos_t, -sin_t, "rope_bwd")
    d_ck_cols = jnp.pad(d_ck.reshape(B_HEADS, S).T, ((0, 0), (0, LANES - B_HEADS)))
    d_f, d_bf = _fox_gate_bwd(d_cq, d_ck_cols, p_a, b_f_pad)
    d_proj = jnp.concatenate([d_qa, d_ka, d_f, dv_a.astype(BF), dq_b.astype(BF), dk_b.astype(BF),
                              dv_b.astype(BF), d_ga, d_gb], axis=1)
    dw_perm = _mm(h1, d_proj, "tn", F32, "proj_dw")
    swap = exch.swap_comm([_shard_slabs(dw_perm)], "late")
    comm = exch.swap_done(_run_comm(swap, "grads_to_sibling_late") if swap else None, "late")
    res = _mm(d_proj, w_perm, "nt", BF, "proj_dx", comm=comm)
    d_h1 = res[0] if comm else res
    exch.reduce_done(res[1] if comm else None, "late")
    grad_x, st_p, outs = _pre_bwd(d_h1, x, d_x2, g1, scale_m, comm=exch.join_comm())
    exch.joined = outs

    d_sinks = d_sink[:, :2, 0].T.reshape(A_Q_HEADS)
    small = jnp.concatenate([
        st_p[0], st_p[1], st_m[3], st_m[0], st_m[1], st_f[0],
        st_p[2], st_m[4], st_m[2], st_f[1],
        st_f[2], d_bf[0, :B_HEADS], d_sinks,
        jnp.zeros((SM_LEN - SM_SINK - A_Q_HEADS,), F32)])
    return grad_x, small


def kernel(x, c, positions, w_ada, b_ada, g_pre_mix, g_post_mix, w_in, b_f, sinks, w_branch_a, w_branch_b, w_out, g_pre_ffn, g_post_ffn, w_ffn_in, w_ffn_out, loss_target, m_w_ada, m_b_ada, m_g_pre_mix, m_g_post_mix, m_w_in, m_b_f, m_sinks, m_w_branch_a, m_w_branch_b, m_w_out, m_g_pre_ffn, m_g_post_ffn, m_w_ffn_in, m_w_ffn_out, v_w_ada, v_b_ada, v_g_pre_mix, v_g_post_mix, v_w_in, v_b_f, v_sinks, v_w_branch_a, v_w_branch_b, v_w_out, v_g_pre_ffn, v_g_post_ffn, v_w_ffn_in, v_w_ffn_out):
    xi, yi, ci = _here()
    chip = 2 * xi + yi
    dev = 2 * chip + ci

    (c_g,) = _allgather8([c.reshape(8, LANES)], "gather_c")
    c_all = c_g.reshape(N_DEV, D_MODEL)
    exch = _Exchange(ci, chip, w_in[0], [w_branch_a[0], w_branch_b[0], w_out[0], w_ffn_in[0], w_ffn_out[0]])

    ada_cols = _mm(c_all, w_ada[0], "nn", F32, "ada_fwd")
    (ada_g,) = _allgather8([ada_cols], "gather_ada")
    ada_mine = lax.dynamic_index_in_dim(ada_g.reshape(N_CHIP, 2, N_DEV, -1)[:, 0], dev, axis=1, keepdims=False)
    ada = (ada_mine.reshape(-1) + b_ada[0]).reshape(N_ADA, D_MODEL)

    grad_x, small = _local_step(
        x[0], positions[0], ada, g_pre_mix, g_post_mix, g_pre_ffn, g_post_ffn, b_f[0], sinks[0],
        exch, loss_target[0])

    g_w_in, g_w_ba, g_w_bb, g_w_out, g_w_fi, g_w_fo = [j.reshape(2 * j.shape[1], j.shape[2]) for j in exch.joined]
    upd_fi, (small_g,) = _adamw(w_ffn_in[0], g_w_fi, m_w_ffn_in[0], v_w_ffn_in[0], "adamw_w_ffn_in",
                                comm=_gather_comm([small.reshape(8, SM_LEN // 8)]))

    small_all = small_g.reshape(N_DEV, SM_LEN)
    small_tot, loss_row = _small_finalize(small_all)
    loss = loss_row[0, 0]
    d_ada_cols = lax.dynamic_slice_in_dim(small_all[:, :N_ADA * D_MODEL], chip * (N_ADA * D_MODEL // N_CHIP),
                                          N_ADA * D_MODEL // N_CHIP, axis=1)
    g_w_ada = _ada_dw(c_all.T, d_ada_cols)

    def small_vec(b_ada_, g1_, g2_, g3_, g4_, b_f_, sinks_):
        return jnp.concatenate([b_ada_[0], g1_[0], g2_[0], g3_[0], g4_[0], jnp.zeros((D_MODEL,), F32),
                                b_f_[0], sinks_[0], jnp.zeros((SM_LEN - SM_SINK - A_Q_HEADS,), F32)]
                               ).reshape(8, SM_LEN // 8)

    sw = small_vec(b_ada, g_pre_mix, g_post_mix, g_pre_ffn, g_post_ffn, b_f, sinks)
    sm = small_vec(m_b_ada, m_g_pre_mix, m_g_post_mix, m_g_pre_ffn, m_g_post_ffn, m_b_f, m_sinks)
    sv = small_vec(v_b_ada, v_g_pre_mix, v_g_post_mix, v_g_pre_ffn, v_g_post_ffn, v_b_f, v_sinks)
    s_upd = [u.reshape(SM_LEN) for u in _adamw(sw, small_tot.reshape(8, SM_LEN // 8), sm, sv, "adamw_small")]
    s_grad = small_tot.reshape(SM_LEN)

    def unpack(vec):
        row = lambda a, n: vec[a:a + n].reshape(1, n)
        return dict(b_ada=row(SM_ADA, N_ADA * D_MODEL), g_pre_mix=row(SM_G, D_MODEL),
                    g_post_mix=row(SM_G + D_MODEL, D_MODEL), g_pre_ffn=row(SM_G + 2 * D_MODEL, D_MODEL),
                    g_post_ffn=row(SM_G + 3 * D_MODEL, D_MODEL), b_f=row(SM_BF, B_HEADS),
                    sinks=row(SM_SINK, A_Q_HEADS))

    big = dict(
        w_ada=(w_ada, g_w_ada, m_w_ada, v_w_ada),
        w_branch_a=(w_branch_a, g_w_ba, m_w_branch_a, v_w_branch_a),
        w_branch_b=(w_branch_b, g_w_bb, m_w_branch_b, v_w_branch_b),
        w_out=(w_out, g_w_out, m_w_out, v_w_out),
        w_ffn_out=(w_ffn_out, g_w_fo, m_w_ffn_out, v_w_ffn_out))
    grads, deltas, new_m, new_v = unpack(s_grad), unpack(s_upd[0]), unpack(s_upd[1]), unpack(s_upd[2])
    grads["w_ffn_in"], deltas["w_ffn_in"], new_m["w_ffn_in"], new_v["w_ffn_in"] = [
        t[None] for t in (g_w_fi, *upd_fi)]
    for n, (w_, g_, m_, v_) in big.items():
        d_, nm_, nv_ = _adamw(w_[0], g_, m_[0], v_[0], "adamw_" + n)
        grads[n], deltas[n], new_m[n], new_v[n] = g_[None], d_[None], nm_[None], nv_[None]
    pad_cols = lambda a: jnp.pad(a, ((0, 0), (0, W_SHARD_PAD - W_SHARD)))
    upd = _adamw(pad_cols(w_in[0]), g_w_in, pad_cols(m_w_in[0]), pad_cols(v_w_in[0]), "adamw_w_in")
    grads["w_in"], deltas["w_in"], new_m["w_in"], new_v["w_in"] = [t[None, :, :W_SHARD] for t in (g_w_in, *upd)]

    names = ["w_ada", "b_ada", "g_pre_mix", "g_post_mix", "w_in", "b_f", "sinks", "w_branch_a", "w_branch_b",
             "w_out", "g_pre_ffn", "g_post_ffn", "w_ffn_in", "w_ffn_out"]
    return (loss, grad_x[None], *[grads[n] for n in names], *[deltas[n] for n in names],
            *[new_m[n] for n in names], *[new_v[n] for n in names])
```

```python
import functools
import math

import numpy as np
import jax
import jax.numpy as jnp
from jax import lax
from jax.experimental import pallas as pl
from jax.experimental.pallas import tpu as pltpu

F32 = jnp.float32
BF = jnp.bfloat16

D_MODEL = 1024
HEAD_DIM = 64
LANES = 128
WINDOW = 128
A_Q_HEADS = 8
A_KV_HEADS = 2
B_HEADS = 8
D_FF = 2816
ROPE_THETA = 10000.0
RMS_EPS = 1e-6
N_ADA = 6
N_DEV = 8
N_CHIP = 4

ADAM_LR = 0.001
ADAM_B1 = 0.9
ADAM_B2 = 0.999
ADAM_EPS = 1e-08
ADAM_WD = 0.01
ADAM_STEP = 10

VMEM_LIMIT = 48 * 1024 * 1024
MESH = pl.DeviceIdType.MESH

A_HEAD_ORDER = (0, 4, 1, 5, 2, 6, 3, 7)

OFF_QA, OFF_KA, OFF_F = 0, 512, 640
W_A = 768
OFF_VA, OFF_QB, OFF_KB, OFF_VB = 0, 128, 640, 1152
W_B = 1664
W_G = 2048
W_PERM = W_A + W_B + W_G
W_SHARD = 1090
W_SHARD_PAD = 1152


def _tile(n, cap, mult=LANES):
    if n <= cap:
        return n
    t = (cap // mult) * mult
    while t >= mult:
        if n % t == 0:
            return t
        t -= mult
    raise ValueError(f"no tile for {n}")


MXU_WIDTH = 256
MM_OPERAND_BYTES = 28 * 1024 * 1024


def _mm_tiles(M, N, K, a_bytes, b_bytes, tm_cap, tn_cap):
    tm = _tile(M, tm_cap)
    try:
        tn = _tile(N, tn_cap, MXU_WIDTH)
    except ValueError:
        tn = _tile(N, tn_cap)
    fits = lambda tk: 2 * tk * (tm * a_bytes + tn * b_bytes) <= MM_OPERAND_BYTES
    tk = K if fits(K) else next(t for t in range(K // LANES * LANES, 0, -LANES) if K % t == 0 and fits(t))
    return tm, tn, tk


def _cparams(*sem):
    return pltpu.CompilerParams(dimension_semantics=sem, vmem_limit_bytes=VMEM_LIMIT)


def _own_refs(refs, comm, n_in, n_out, n_scratch):
    if comm is None:
        return list(refs), None
    return comm.split(refs, n_in, n_out, n_scratch)


def _comm_specs(comm, side):
    if comm is None:
        return []
    return [pl.BlockSpec(memory_space=pl.ANY)] * len(comm.ins if side == "in" else comm.out_shapes)


def _comm_edge(comm, comm_refs, grid, first):
    if comm is None:
        return
    at_edge = None
    for axis, n in enumerate(grid):
        here = pl.program_id(axis) == (0 if first else n - 1)
        at_edge = here if at_edge is None else at_edge & here
    pl.when(at_edge)(lambda: (comm.start if first else comm.finish)(*comm_refs))


def _mm(a, b, mode, out_dtype, name, tm_cap=512, tn_cap=2816, comm=None, col_pieces=1, twin=False):
    if mode == "nn":
        (M, K), (K2, N) = a.shape, b.shape
        dims = (((1,), (0,)), ((), ()))
    elif mode == "nt":
        (M, K), (N, K2) = a.shape, b.shape
        dims = (((1,), (1,)), ((), ()))
    else:
        (K, M), (K2, N) = a.shape, b.shape
        dims = (((0,), (0,)), ((), ()))
    assert K == K2, (a.shape, b.shape, mode)
    tm, tn, tk = _mm_tiles(M, N // col_pieces, K, a.dtype.itemsize, b.dtype.itemsize, tm_cap, tn_cap)
    nk = K // tk
    n_out = 2 if twin else 1
    n_scratch = 1 if nk > 1 else 0
    if mode == "nn":
        a_spec = pl.BlockSpec((tm, tk), lambda i, j, k: (i, k))
        b_spec = pl.BlockSpec((tk, tn), lambda i, j, k: (k, j))
    elif mode == "nt":
        a_spec = pl.BlockSpec((tm, tk), lambda i, j, k: (i, k))
        b_spec = pl.BlockSpec((tn, tk), lambda i, j, k: (j, k))
    else:
        a_spec = pl.BlockSpec((tk, tm), lambda i, j, k: (k, i))
        b_spec = pl.BlockSpec((tk, tn), lambda i, j, k: (k, j))

    grid = (M // tm, N // tn, nk)

    def kern(*refs):
        own, comm_refs = _own_refs(refs, comm, 2, n_out, n_scratch)
        a_ref, b_ref, o_refs = own[0], own[1], own[2:2 + n_out]
        k = pl.program_id(2)
        _comm_edge(comm, comm_refs, grid, first=True)
        part = lax.dot_general(a_ref[...].astype(BF), b_ref[...].astype(BF), dims,
                               preferred_element_type=F32)
        if nk == 1:
            for o_ref in o_refs:
                o_ref[...] = part.astype(o_ref.dtype)
        else:
            acc_ref = own[2 + n_out]

            @pl.when(k == 0)
            def _():
                acc_ref[...] = part

            @pl.when(k > 0)
            def _():
                acc_ref[...] += part

            @pl.when(k == nk - 1)
            def _():
                for o_ref in o_refs:
                    o_ref[...] = acc_ref[...].astype(o_ref.dtype)

        _comm_edge(comm, comm_refs, grid, first=False)

    if col_pieces > 1:
        per = N // col_pieces // tn
        out_spec = pl.BlockSpec((None, tm, tn), lambda i, j, k: (j // per, i, j % per))
        shape = (col_pieces, M, N // col_pieces)
    else:
        out_spec = pl.BlockSpec((tm, tn), lambda i, j, k: (i, j))
        shape = (M, N)
    dtypes = [out_dtype, BF] if twin else [out_dtype]
    res = pl.pallas_call(
        kern, name=name, grid=grid,
        in_specs=[a_spec, b_spec] + _comm_specs(comm, "in"),
        out_specs=[out_spec] * n_out + _comm_specs(comm, "out"),
        out_shape=[jax.ShapeDtypeStruct(shape, d) for d in dtypes] + (comm.out_shapes if comm else []),
        scratch_shapes=[pltpu.VMEM((tm, tn), F32)] * n_scratch + (comm.sem_shapes if comm else []),
        compiler_params=_cparams("parallel", "parallel", "arbitrary"),
    )(a, b, *(comm.ins if comm else []))
    own = res[0] if n_out == 1 else tuple(res[:n_out])
    return (own, res[n_out:]) if comm else own


ROWS = 512


def _row_spec(tm, width=D_MODEL, col=0):
    return pl.BlockSpec((tm, width), lambda i: (i, col))


def _vec_spec(width=D_MODEL):
    return pl.BlockSpec((1, width), lambda i: (0, 0))


def _rms(x):
    return lax.rsqrt(jnp.mean(x * x, axis=-1, keepdims=True) + RMS_EPS)


def _colsum(x):
    return jnp.sum(x, axis=0, keepdims=True)


def _norm_bwd(d_xn, xn, r):
    return r * (d_xn - xn * jnp.mean(d_xn * xn, axis=-1, keepdims=True))


def _pre_norm(x, g, scale, shift, name, comm=None):
    S = x.shape[0]
    tm = _tile(S, 2 * ROWS, 8)
    grid = (S // tm,)

    def kern(*refs):
        (x_ref, g_ref, sc_ref, sh_ref, h_ref), comm_refs = _own_refs(refs, comm, 4, 1, 0)
        _comm_edge(comm, comm_refs, grid, first=True)
        xf = x_ref[...]
        y = xf * _rms(xf) * g_ref[...]
        h_ref[...] = (y * (1.0 + sc_ref[...]) + sh_ref[...]).astype(BF)
        _comm_edge(comm, comm_refs, grid, first=False)

    res = pl.pallas_call(
        kern, name=name, grid=grid,
        in_specs=[_row_spec(tm), _vec_spec(), _vec_spec(), _vec_spec()] + _comm_specs(comm, "in"),
        out_specs=[_row_spec(tm)] + _comm_specs(comm, "out"),
        out_shape=[jax.ShapeDtypeStruct((S, D_MODEL), BF)] + (comm.out_shapes if comm else []),
        scratch_shapes=comm.sem_shapes if comm else [],
        compiler_params=_cparams("arbitrary"),
    )(x, g, scale, shift, *(comm.ins if comm else []))
    return res[0], res[1:]


def _post_pre(x, y1, g2, gate_m, g3, scale_f, shift_f):
    S = x.shape[0]
    tm = _tile(S, 2 * ROWS, 8)

    def kern(x_ref, y_ref, g2_ref, gm_ref, g3_ref, sc_ref, sh_ref, x2_ref, h2_ref):
        y = y_ref[...].astype(F32)
        n2 = y * _rms(y) * g2_ref[...]
        x2 = x_ref[...] + gm_ref[...] * n2
        x2_ref[...] = x2
        n3 = x2 * _rms(x2) * g3_ref[...]
        h2_ref[...] = (n3 * (1.0 + sc_ref[...]) + sh_ref[...]).astype(BF)

    return pl.pallas_call(
        kern, name="post_mix_pre_ffn", grid=(S // tm,),
        in_specs=[_row_spec(tm), _row_spec(tm)] + [_vec_spec()] * 5,
        out_specs=[_row_spec(tm), _row_spec(tm)],
        out_shape=[jax.ShapeDtypeStruct((S, D_MODEL), F32), jax.ShapeDtypeStruct((S, D_MODEL), BF)],
        compiler_params=_cparams("parallel"),
    )(x, y1, g2, gate_m, g3, scale_f, shift_f)


def _stats_spec():
    return pl.BlockSpec((8, D_MODEL), lambda i: (0, 0))


def _final(x2, y2, g4, gate_f, target):
    S = x2.shape[0]
    tm = _tile(S, ROWS, 8)

    def kern(x2_ref, y_ref, g4_ref, gf_ref, t_ref, dout_ref, dy_ref, st_ref):
        @pl.when(pl.program_id(0) == 0)
        def _():
            st_ref[...] = jnp.zeros_like(st_ref)

        y = y_ref[...].astype(F32)
        r = _rms(y)
        yn = y * r
        n4 = yn * g4_ref[...]
        diff = x2_ref[...] + gf_ref[...] * n4 - t_ref[...]
        d_out = diff / D_MODEL
        dout_ref[...] = d_out
        dn = d_out * gf_ref[...]
        dy_ref[...] = _norm_bwd(dn * g4_ref[...], yn, r).astype(BF)
        st_ref[0:1, :] += _colsum(d_out * n4)
        st_ref[1:2, :] += _colsum(dn * yn)
        st_ref[2:3, :] += _colsum(diff * diff)

    return pl.pallas_call(
        kern, name="final_loss", grid=(S // tm,),
        in_specs=[_row_spec(tm), _row_spec(tm), _vec_spec(), _vec_spec(), _row_spec(tm)],
        out_specs=[_row_spec(tm), _row_spec(tm), _stats_spec()],
        out_shape=[jax.ShapeDtypeStruct((S, D_MODEL), F32), jax.ShapeDtypeStruct((S, D_MODEL), BF),
                   jax.ShapeDtypeStruct((8, D_MODEL), F32)],
        compiler_params=_cparams("arbitrary"),
    )(x2, y2, g4, gate_f, target)


def _mid_bwd(d_h2, x2, d_out, y1, g3, scale_f, g2, gate_m):
    S = x2.shape[0]
    tm = _tile(S, ROWS, 8)

    def kern(dh_ref, x2_ref, dout_ref, y_ref, g3_ref, sc_ref, g2_ref, gm_ref, dx2_ref, dy_ref, st_ref):
        @pl.when(pl.program_id(0) == 0)
        def _():
            st_ref[...] = jnp.zeros_like(st_ref)

        dh = dh_ref[...].astype(F32)
        x2 = x2_ref[...]
        r3 = _rms(x2)
        xn = x2 * r3
        one_sc = 1.0 + sc_ref[...]
        d_x2 = dout_ref[...] + _norm_bwd(dh * one_sc * g3_ref[...], xn, r3)
        dx2_ref[...] = d_x2
        y = y_ref[...].astype(F32)
        r2 = _rms(y)
        yn = y * r2
        dn = d_x2 * gm_ref[...]
        dy_ref[...] = _norm_bwd(dn * g2_ref[...], yn, r2).astype(BF)
        st_ref[0:1, :] += _colsum(dh)
        st_ref[1:2, :] += _colsum(dh * (xn * g3_ref[...]))
        st_ref[2:3, :] += _colsum(dh * one_sc * xn)
        st_ref[3:4, :] += _colsum(d_x2 * (yn * g2_ref[...]))
        st_ref[4:5, :] += _colsum(dn * yn)

    return pl.pallas_call(
        kern, name="mid_bwd", grid=(S // tm,),
        in_specs=[_row_spec(tm)] * 4 + [_vec_spec()] * 4,
        out_specs=[_row_spec(tm), _row_spec(tm), _stats_spec()],
        out_shape=[jax.ShapeDtypeStruct((S, D_MODEL), F32), jax.ShapeDtypeStruct((S, D_MODEL), BF),
                   jax.ShapeDtypeStruct((8, D_MODEL), F32)],
        compiler_params=_cparams("arbitrary"),
    )(d_h2, x2, d_out, y1, g3, scale_f, g2, gate_m)


def _pre_bwd(d_h1, x, d_x2, g1, scale_m, comm=None):
    S = x.shape[0]
    tm = _tile(S, ROWS, 8)
    grid = (S // tm,)

    def kern(*refs):
        (dh_ref, x_ref, dx2_ref, g_ref, sc_ref, gx_ref, st_ref), comm_refs = _own_refs(refs, comm, 5, 2, 0)
        _comm_edge(comm, comm_refs, grid, first=True)

        @pl.when(pl.program_id(0) == 0)
        def _():
            st_ref[...] = jnp.zeros_like(st_ref)

        dh = dh_ref[...].astype(F32)
        xf = x_ref[...]
        r = _rms(xf)
        xn = xf * r
        one_sc = 1.0 + sc_ref[...]
        gx_ref[...] = dx2_ref[...] + _norm_bwd(dh * one_sc * g_ref[...], xn, r)
        st_ref[0:1, :] += _colsum(dh)
        st_ref[1:2, :] += _colsum(dh * (xn * g_ref[...]))
        st_ref[2:3, :] += _colsum(dh * one_sc * xn)
        _comm_edge(comm, comm_refs, grid, first=False)

    res = pl.pallas_call(
        kern, name="pre_mix_bwd", grid=grid,
        in_specs=[_row_spec(tm)] * 3 + [_vec_spec()] * 2 + _comm_specs(comm, "in"),
        out_specs=[_row_spec(tm), _stats_spec()] + _comm_specs(comm, "out"),
        out_shape=[jax.ShapeDtypeStruct((S, D_MODEL), F32), jax.ShapeDtypeStruct((8, D_MODEL), F32)]
        + (comm.out_shapes if comm else []),
        scratch_shapes=comm.sem_shapes if comm else [],
        input_output_aliases={5 + i: 2 + o for i, o in comm.aliases.items()} if comm else {},
        compiler_params=_cparams("arbitrary"),
    )(d_h1, x, d_x2, g1, scale_m, *(comm.ins if comm else []))
    return res[0], res[1], res[2:]


def _rope(xs, widths, cos_t, sin_t, name):
    S = xs[0].shape[0]
    tm = _tile(S, 512, 8)
    n = len(xs)

    def kern(*refs):
        cos = refs[n][...]
        sin = refs[n + 1][...]
        first = (lax.broadcasted_iota(jnp.int32, cos.shape, 1) % HEAD_DIM) < HEAD_DIM // 2
        for x_ref, o_ref, w in zip(refs[:n], refs[n + 2:], widths):
            for c0 in range(0, w, LANES):
                v = x_ref[:, c0:c0 + LANES]
                partner = jnp.where(first, pltpu.roll(v, LANES - HEAD_DIM // 2, 1),
                                    pltpu.roll(v, HEAD_DIM // 2, 1))
                o_ref[:, c0:c0 + LANES] = (v * cos + partner * sin).astype(BF)

    return pl.pallas_call(
        kern, name=name, grid=(S // tm,),
        in_specs=[_row_spec(tm, w) for w in widths] + [_row_spec(tm, LANES)] * 2,
        out_specs=[_row_spec(tm, w) for w in widths],
        out_shape=[jax.ShapeDtypeStruct((S, w), BF) for w in widths],
        compiler_params=_cparams("parallel"),
    )(*xs, cos_t, sin_t)


def _merge_fwd(pg, pa, pb):
    S = pa.shape[0]
    tm = _tile(S, 2 * ROWS, 8)

    def kern(ga_ref, gb_ref, pa_ref, pb_ref, o_ref):
        ga = jax.nn.sigmoid(ga_ref[...].astype(F32))
        gb = jax.nn.sigmoid(gb_ref[...].astype(F32))
        o_ref[...] = (ga * pa_ref[...].astype(F32) + gb * pb_ref[...].astype(F32)).astype(BF)

    return pl.pallas_call(
        kern, name="merge_fwd", grid=(S // tm,),
        in_specs=[_row_spec(tm, col=0), _row_spec(tm, col=1), _row_spec(tm), _row_spec(tm)],
        out_specs=_row_spec(tm),
        out_shape=jax.ShapeDtypeStruct((S, D_MODEL), BF),
        compiler_params=_cparams("parallel"),
    )(pg, pg, pa, pb)


def _merge_bwd(d_merged, pg, pa, pb):
    S = pa.shape[0]
    tm = _tile(S, ROWS, 8)

    def kern(dm_ref, ga_ref, gb_ref, pa_ref, pb_ref, dpa_ref, dpb_ref, dga_ref, dgb_ref):
        dm = dm_ref[...].astype(F32)
        ga = jax.nn.sigmoid(ga_ref[...].astype(F32))
        gb = jax.nn.sigmoid(gb_ref[...].astype(F32))
        dpa_ref[...] = (dm * ga).astype(BF)
        dpb_ref[...] = (dm * gb).astype(BF)
        dga_ref[...] = (dm * pa_ref[...].astype(F32) * ga * (1.0 - ga)).astype(BF)
        dgb_ref[...] = (dm * pb_ref[...].astype(F32) * gb * (1.0 - gb)).astype(BF)

    bf_out = jax.ShapeDtypeStruct((S, D_MODEL), BF)
    return pl.pallas_call(
        kern, name="merge_bwd", grid=(S // tm,),
        in_specs=[_row_spec(tm), _row_spec(tm, col=0), _row_spec(tm, col=1), _row_spec(tm), _row_spec(tm)],
        out_specs=[_row_spec(tm)] * 4,
        out_shape=[bf_out] * 4,
        compiler_params=_cparams("parallel"),
    )(d_merged, pg, pg, pa, pb)


def _swiglu_fwd(gu):
    S = gu.shape[0]
    tm = _tile(S, 2 * ROWS, 8)
    tc = _tile(D_FF, 1408)
    nc = D_FF // tc

    def kern(g_ref, u_ref, o_ref):
        g = g_ref[...].astype(F32)
        o_ref[...] = (g * jax.nn.sigmoid(g) * u_ref[...].astype(F32)).astype(BF)

    return pl.pallas_call(
        kern, name="swiglu_fwd", grid=(S // tm, nc),
        in_specs=[pl.BlockSpec((tm, tc), lambda i, j: (i, j)),
                  pl.BlockSpec((tm, tc), lambda i, j: (i, j + nc))],
        out_specs=pl.BlockSpec((tm, tc), lambda i, j: (i, j)),
        out_shape=jax.ShapeDtypeStruct((S, D_FF), BF),
        compiler_params=_cparams("parallel", "parallel"),
    )(gu, gu)


def _swiglu_bwd(d_act, gu):
    S = gu.shape[0]
    tm = _tile(S, ROWS, 8)

    def kern(da_ref, g_ref, u_ref, o_ref):
        g = g_ref[...].astype(F32)
        u = u_ref[...].astype(F32)
        da = da_ref[...].astype(F32)
        sg = jax.nn.sigmoid(g)
        o_ref[:, :D_FF] = (da * u * (sg * (1.0 + g * (1.0 - sg)))).astype(BF)
        o_ref[:, D_FF:] = (da * (g * sg)).astype(BF)

    return pl.pallas_call(
        kern, name="swiglu_bwd", grid=(S // tm,),
        in_specs=[_row_spec(tm, D_FF), _row_spec(tm, D_FF, 0), _row_spec(tm, D_FF, 1)],
        out_specs=_row_spec(tm, 2 * D_FF),
        out_shape=jax.ShapeDtypeStruct((S, 2 * D_FF), BF),
        compiler_params=_cparams("parallel"),
    )(d_act, gu, gu)


def _split3(x):
    hi = x.astype(BF)
    r1 = x - hi.astype(F32)
    mid = r1.astype(BF)
    lo = (r1 - mid.astype(F32)).astype(BF)
    return hi, mid, lo


def _tri_dot(tri, x):
    return sum(jnp.dot(tri, part, preferred_element_type=F32) for part in _split3(x))


def _log_sigmoid(z):
    return jnp.minimum(z, 0.0) - jnp.log(1.0 + jnp.exp(-jnp.abs(z)))


def _fox_gate_fwd(pa, b_f_pad):
    S = pa.shape[0]
    T = _tile(S, 512, 8)
    f_col = OFF_F // LANES

    def kern(z_ref, b_ref, cum_ref, carry_ref):
        @pl.when(pl.program_id(0) == 0)
        def _():
            carry_ref[...] = jnp.zeros_like(carry_ref)

        log_f = _log_sigmoid(z_ref[...] + b_ref[...])
        row = lax.broadcasted_iota(jnp.int32, (T, T), 0)
        col = lax.broadcasted_iota(jnp.int32, (T, T), 1)
        tri = (col <= row).astype(BF)
        cum = _tri_dot(tri, log_f) + carry_ref[...]
        cum_ref[...] = cum
        carry_ref[...] = cum[T - 1:T, :]

    return pl.pallas_call(
        kern, name="fox_gate_fwd", grid=(S // T,),
        in_specs=[_row_spec(T, LANES, f_col), _vec_spec(LANES)],
        out_specs=_row_spec(T, LANES),
        out_shape=jax.ShapeDtypeStruct((S, LANES), F32),
        scratch_shapes=[pltpu.VMEM((1, LANES), F32)],
        compiler_params=_cparams("arbitrary"),
    )(pa, b_f_pad)


def _fox_gate_bwd(rowsum_ds, colsum_ds, pa, b_f_pad):
    S = pa.shape[0]
    T = _tile(S, 512, 8)
    nb = S // T
    f_col = OFF_F // LANES

    def kern(dr_ref, dc_ref, z_ref, b_ref, df_ref, dbf_ref, carry_ref):
        @pl.when(pl.program_id(0) == 0)
        def _():
            carry_ref[...] = jnp.zeros_like(carry_ref)
            dbf_ref[...] = jnp.zeros_like(dbf_ref)

        row = lax.broadcasted_iota(jnp.int32, (T, T), 0)
        col = lax.broadcasted_iota(jnp.int32, (T, T), 1)
        tri = (col >= row).astype(BF)
        rev = _tri_dot(tri, dr_ref[...] - dc_ref[...]) + carry_ref[...]
        carry_ref[...] = rev[0:1, :]
        z = z_ref[...] + b_ref[...]
        lane = lax.broadcasted_iota(jnp.int32, (T, LANES), 1)
        d_z = jnp.where(lane < B_HEADS, rev * jax.nn.sigmoid(-z), 0.0)
        df_ref[...] = d_z.astype(BF)
        dbf_ref[0:1, :] += _colsum(d_z)

    return pl.pallas_call(
        kern, name="fox_gate_bwd", grid=(nb,),
        in_specs=[pl.BlockSpec((T, LANES), lambda i: (nb - 1 - i, 0)),
                  pl.BlockSpec((T, LANES), lambda i: (nb - 1 - i, 0)),
                  pl.BlockSpec((T, LANES), lambda i: (nb - 1 - i, f_col)),
                  _vec_spec(LANES)],
        out_specs=[pl.BlockSpec((T, LANES), lambda i: (nb - 1 - i, 0)),
                   pl.BlockSpec((8, LANES), lambda i: (0, 0))],
        out_shape=[jax.ShapeDtypeStruct((S, LANES), BF), jax.ShapeDtypeStruct((8, LANES), F32)],
        scratch_shapes=[pltpu.VMEM((1, LANES), F32)],
        compiler_params=_cparams("arbitrary"),
    )(rowsum_ds, colsum_ds, pa, b_f_pad)


NEG_INF = float("-inf")
QK_SCALE = 1.0 / math.sqrt(HEAD_DIM)


def _half_mask(shape, half):
    lane = lax.broadcasted_iota(jnp.int32, shape, 1)
    return (lane < HEAD_DIM) if half == 0 else (lane >= HEAD_DIM)


def _bias_block(shape, terms, term_off, ones_lo, ones_hi):
    l64 = lax.broadcasted_iota(jnp.int32, shape, 1) & (HEAD_DIM - 1)
    out = jnp.where((l64 >= ones_lo) & (l64 < ones_hi), 1.0, 0.0)
    for t, term in enumerate(terms):
        out = jnp.where(l64 == term_off + t, term.astype(F32), out)
    return out


def _head_column(block, head):
    lane = lax.broadcasted_iota(jnp.int32, block.shape, 1)
    return jnp.sum(jnp.where(lane == head, block, 0.0), axis=1, keepdims=True)


def _crossed(shape, first, second):
    return jnp.where(_half_mask(shape, 0), second, first)


def _fox_prep_fwd(cum, T):
    S = cum.shape[0]
    shape = (T, LANES)

    def kern(c_ref, bq_ref, bk_ref):
        p_id = pl.program_id(0)
        cum_blk = c_ref[...]
        c3 = _split3(_crossed(shape, _head_column(cum_blk, 2 * p_id), _head_column(cum_blk, 2 * p_id + 1)))
        bq_ref[...] = _bias_block(shape, c3, 0, 3, 6).astype(BF)
        bk_ref[...] = _bias_block(shape, [-t.astype(F32) for t in c3], 3, 0, 3).astype(BF)

    out_spec = pl.BlockSpec((None, T, LANES), lambda p, i: (p, i, 0))
    out_shape = jax.ShapeDtypeStruct((B_HEADS // 2, S, LANES), BF)
    return pl.pallas_call(
        kern, name="fox_prep_fwd", grid=(B_HEADS // 2, S // T),
        in_specs=[pl.BlockSpec((T, LANES), lambda p, i: (i, 0))],
        out_specs=[out_spec, out_spec], out_shape=[out_shape, out_shape],
        compiler_params=_cparams("parallel", "parallel"),
    )(cum)


def _fox_fwd(p_b, bq, bk, T, comm=None):
    S = p_b.shape[0]
    nq = S // T
    n_pairs = B_HEADS // 2
    grid = (n_pairs, nq)

    def kern(*refs):
        (q_ref, k_ref, v_ref, bq_ref, bk_ref, o_ref, lse_ref), comm_refs = _own_refs(refs, comm, 5, 2, 0)
        _comm_edge(comm, comm_refs, grid, first=True)
        i = pl.program_id(1)
        rowcol = lax.broadcasted_iota(jnp.int32, (T, T), 0) - lax.broadcasted_iota(jnp.int32, (T, T), 1)
        hms = (_half_mask((T, LANES), 0), _half_mask((T, LANES), 1))
        q_scaled = (q_ref[...].astype(F32) * QK_SCALE).astype(BF)
        bq_blk = bq_ref[...]
        qs = [jnp.where(hms[h], q_scaled, bq_blk) for h in (0, 1)]

        def step(j, carry, masked):
            rows = pl.ds(pl.multiple_of(j * T, T), T)
            kj, bkj, vj = k_ref[rows, :], bk_ref[rows, :], v_ref[rows, :]
            new = []
            for half in (0, 1):
                m, l, acc = carry[half]
                s = lax.dot_general(qs[half], jnp.where(hms[half], kj, bkj), (((1,), (1,)), ((), ())),
                                    preferred_element_type=F32)
                if masked:
                    s = jnp.where(rowcol >= 0, s, NEG_INF)
                m_new = jnp.maximum(m, jnp.max(s, axis=1, keepdims=True))
                alpha = jnp.exp(m - m_new)
                p = jnp.exp(s - m_new)
                l_new = alpha * l + jnp.sum(p, axis=1, keepdims=True)
                acc_new = alpha * acc + jnp.dot(p.astype(BF), vj, preferred_element_type=F32)
                new.append((m_new, l_new, acc_new))
            return tuple(new)

        one = (jnp.full((T, 1), NEG_INF, F32), jnp.zeros((T, 1), F32), jnp.zeros((T, LANES), F32))
        carry = lax.fori_loop(0, i, functools.partial(step, masked=False), (one, one))
        (m0, l0, acc0), (m1, l1, acc1) = step(i, carry, True)
        hm0 = _half_mask((T, LANES), 0)
        o_ref[...] = jnp.where(hm0, acc0 / l0, acc1 / l1)
        lse_ref[...] = jnp.where(hm0, m0 + jnp.log(l0), m1 + jnp.log(l1))
        _comm_edge(comm, comm_refs, grid, first=False)

    out_spec = pl.BlockSpec((T, LANES), lambda p, i: (i, p))
    res = pl.pallas_call(
        kern, name="fox_fwd", grid=grid,
        in_specs=[pl.BlockSpec((T, LANES), lambda p, i: (i, OFF_QB // LANES + p)),
                  pl.BlockSpec((S, LANES), lambda p, i: (0, OFF_KB // LANES + p)),
                  pl.BlockSpec((S, LANES), lambda p, i: (0, OFF_VB // LANES + p)),
                  pl.BlockSpec((None, T, LANES), lambda p, i: (p, i, 0)),
                  pl.BlockSpec((None, S, LANES), lambda p, i: (p, 0, 0))] + _comm_specs(comm, "in"),
        out_specs=[out_spec, out_spec] + _comm_specs(comm, "out"),
        out_shape=[jax.ShapeDtypeStruct((S, n_pairs * LANES), F32)] * 2 + (comm.out_shapes if comm else []),
        scratch_shapes=comm.sem_shapes if comm else [],
        compiler_params=_cparams("arbitrary", "arbitrary"),
    )(p_b, p_b, p_b, bq, bk, *(comm.ins if comm else []))
    return res[0], res[1], res[2:]


def _fox_prep_bwd(cum, o, do, lse, T):
    S = o.shape[0]
    shape = (T, LANES)

    def kern(c_ref, o_ref, do_ref, lse_ref, bq_ref, bdo_ref):
        p_id = pl.program_id(0)
        cum_blk = c_ref[...]
        cq = _crossed(shape, _head_column(cum_blk, 2 * p_id), _head_column(cum_blk, 2 * p_id + 1))
        b3 = _split3(cq - pltpu.roll(lse_ref[...], HEAD_DIM, 1))
        bq_ref[...] = _bias_block(shape, b3, 0, 3, 6).astype(BF)
        dd = do_ref[...] * o_ref[...]
        delta = [jnp.sum(jnp.where(_half_mask(shape, h), dd, 0.0), axis=1, keepdims=True) for h in (0, 1)]
        d3 = _split3(-_crossed(shape, delta[0], delta[1]))
        bdo_ref[...] = _bias_block(shape, d3, 0, 0, 0).astype(BF)

    block = pl.BlockSpec((None, T, LANES), lambda p, i: (p, i, 0))
    tile = pl.BlockSpec((T, LANES), lambda p, i: (i, p))
    out_shape = jax.ShapeDtypeStruct((B_HEADS // 2, S, LANES), BF)
    return pl.pallas_call(
        kern, name="fox_prep_bwd", grid=(B_HEADS // 2, S // T),
        in_specs=[pl.BlockSpec((T, LANES), lambda p, i: (i, 0)), tile, tile, tile],
        out_specs=[block, block], out_shape=[out_shape, out_shape],
        compiler_params=_cparams("parallel", "parallel"),
    )(cum, o, do, lse)


def _fox_bwd(p_b, do, bq, bk, bdo, T, comm=None):
    S = p_b.shape[0]
    n_pairs = B_HEADS // 2
    nq = S // T
    grid = (n_pairs,)

    def kern(*refs):
        own, comm_refs = _own_refs(refs, comm, 7, 5, 0)
        q_ref, k_ref, v_ref, do_ref, bq_ref, bk_ref, bdo_ref, dq_ref, dk_ref, dv_ref, dck_ref, dcq_ref = own
        _comm_edge(comm, comm_refs, grid, first=True)
        p_id = pl.program_id(0)
        rowcol = lax.broadcasted_iota(jnp.int32, (T, T), 0) - lax.broadcasted_iota(jnp.int32, (T, T), 1)
        lane = lax.broadcasted_iota(jnp.int32, (T, LANES), 1)
        dk_ref[...] = jnp.zeros_like(dk_ref)
        dv_ref[...] = jnp.zeros_like(dv_ref)
        dck_ref[...] = jnp.zeros_like(dck_ref)

        @pl.when(p_id == 0)
        def _():
            dcq_ref[...] = jnp.zeros_like(dcq_ref)

        hms = (_half_mask((T, LANES), 0), _half_mask((T, LANES), 1))
        v_ones = _bias_block((T, LANES), [], 0, 0, 3).astype(BF)

        def outer(i, carry):
            qrows = pl.ds(pl.multiple_of(i * T, T), T)
            q_scaled = (q_ref[qrows, :].astype(F32) * QK_SCALE).astype(BF)
            do_b = do_ref[qrows, :].astype(BF)
            bq_i, bdo_i = bq_ref[qrows, :], bdo_ref[qrows, :]
            qa = [jnp.where(hms[h], q_scaled, bq_i) for h in (0, 1)]
            doa = [jnp.where(hms[h], do_b, bdo_i) for h in (0, 1)]
            q_own = [jnp.where(hms[h], q_scaled, 0) for h in (0, 1)]
            do_own = [jnp.where(hms[h], do_b, 0) for h in (0, 1)]

            def inner(j, carry_in, masked):
                krows = pl.ds(pl.multiple_of(j * T, T), T)
                kj, bkj, vj = k_ref[krows, :], bk_ref[krows, :], v_ref[krows, :]
                dv_add, dk_add, new = 0.0, 0.0, []
                for half in (0, 1):
                    dq, rs = carry_in[half]
                    ka = jnp.where(hms[half], kj, bkj)
                    s = lax.dot_general(qa[half], ka, (((1,), (1,)), ((), ())), preferred_element_type=F32)
                    if masked:
                        s = jnp.where(rowcol >= 0, s, NEG_INF)
                    p = jnp.exp(s)
                    ds = p * lax.dot_general(doa[half], jnp.where(hms[half], vj, v_ones),
                                             (((1,), (1,)), ((), ())), preferred_element_type=F32)
                    ds_b = ds.astype(BF)
                    dv_add = dv_add + lax.dot_general(p.astype(BF), do_own[half], (((0,), (0,)), ((), ())),
                                                      preferred_element_type=F32)
                    dk_add = dk_add + lax.dot_general(ds_b, q_own[half], (((0,), (0,)), ((), ())),
                                                      preferred_element_type=F32)
                    dck_ref[half:half + 1, krows] += jnp.sum(ds, axis=0, keepdims=True)
                    new.append((dq + jnp.dot(ds_b, jnp.where(hms[half], kj, 0), preferred_element_type=F32),
                                rs + jnp.sum(ds, axis=1, keepdims=True)))
                dv_ref[krows, :] += dv_add
                dk_ref[krows, :] += dk_add
                return tuple(new)

            one = (jnp.zeros((T, LANES), F32), jnp.zeros((T, 1), F32))
            carry_in = lax.fori_loop(0, i, functools.partial(inner, masked=False), (one, one))
            (dq0, rs0), (dq1, rs1) = inner(i, carry_in, True)
            dq_ref[qrows, :] = (dq0 + dq1) * QK_SCALE
            dcq_ref[qrows, :] = jnp.where(lane == 2 * p_id, rs0, jnp.where(lane == 2 * p_id + 1, rs1,
                                                                             dcq_ref[qrows, :]))
            return carry

        lax.fori_loop(0, nq, outer, 0)
        _comm_edge(comm, comm_refs, grid, first=False)

    block = pl.BlockSpec((None, S, LANES), lambda p: (p, 0, 0))
    pair = pl.BlockSpec((S, LANES), lambda p: (0, p))
    slab = lambda off: pl.BlockSpec((S, LANES), lambda p: (0, off // LANES + p))
    wide = jax.ShapeDtypeStruct((S, n_pairs * LANES), F32)
    res = pl.pallas_call(
        kern, name="fox_bwd", grid=grid,
        in_specs=[slab(OFF_QB), slab(OFF_KB), slab(OFF_VB), pair, block, block, block]
        + _comm_specs(comm, "in"),
        out_specs=[pair, pair, pair, pl.BlockSpec((None, 2, S), lambda p: (p, 0, 0)),
                   pl.BlockSpec((S, LANES), lambda p: (0, 0))] + _comm_specs(comm, "out"),
        out_shape=[wide, wide, wide, jax.ShapeDtypeStruct((n_pairs, 2, S), F32),
                   jax.ShapeDtypeStruct((S, LANES), F32)] + (comm.out_shapes if comm else []),
        scratch_shapes=comm.sem_shapes if comm else [],
        compiler_params=_cparams("arbitrary"),
    )(p_b, p_b, p_b, do, bq, bk, bdo, *(comm.ins if comm else []))
    return (*res[:5], res[5:])


SWA_TQ = 128
SWA_SUB = 32


def _swa_window(i, tq):
    start = pl.multiple_of(jnp.maximum(i * tq - WINDOW, 0), LANES)
    return start, i * tq - start


def _swa_valid(offset, tq):
    rel = offset + lax.broadcasted_iota(jnp.int32, (tq, tq + WINDOW), 0) \
        - lax.broadcasted_iota(jnp.int32, (tq, tq + WINDOW), 1)
    return (rel >= 0) & (rel < WINDOW)


def _swa_fwd(qk, v_arr, v_col, sinks):
    S = qk.shape[0]
    tq = min(SWA_TQ, S - WINDOW)
    sub = min(SWA_SUB, S // tq)
    win = tq + WINDOW

    def kern(q_ref, k_ref, v_ref, sink_ref, o_ref, lse_ref):
        p_id, i = pl.program_id(0), pl.program_id(1)
        hm0 = _half_mask((tq, LANES), 0)
        for t in range(sub):
            rows = slice(t * tq, (t + 1) * tq)
            start, offset = _swa_window(i * sub + t, tq)
            kw = k_ref[pl.ds(start, win), :]
            vw = v_ref[pl.ds(start, win), :].astype(BF)
            valid = _swa_valid(offset, tq)
            q = q_ref[rows, :]
            outs, lses = [], []
            for half in (0, 1):
                hm = _half_mask((tq, LANES), half)
                qh = (jnp.where(hm, q, 0).astype(F32) * QK_SCALE).astype(BF)
                s = lax.dot_general(qh, kw, (((1,), (1,)), ((), ())), preferred_element_type=F32)
                s = jnp.where(valid, s, NEG_INF)
                sink = sink_ref[2 * p_id + half]
                m = jnp.maximum(jnp.max(s, axis=1, keepdims=True), sink)
                p = jnp.exp(s - m)
                denom = jnp.sum(p, axis=1, keepdims=True) + jnp.exp(sink - m)
                outs.append(jnp.dot(p.astype(BF), vw, preferred_element_type=F32) / denom)
                lses.append(m + jnp.log(denom))
            o_ref[rows, :] = jnp.where(hm0, outs[0], outs[1])
            lse_ref[rows, :] = jnp.where(hm0, lses[0], lses[1])

    tile = pl.BlockSpec((sub * tq, LANES), lambda p, i: (i, p))
    return pl.pallas_call(
        kern, name="swa_fwd", grid=(A_Q_HEADS // 2, S // (sub * tq)),
        in_specs=[tile, pl.BlockSpec((S, LANES), lambda p, i: (0, A_Q_HEADS // 2)),
                  pl.BlockSpec((S, LANES), lambda p, i: (0, v_col)),
                  pl.BlockSpec(memory_space=pltpu.SMEM)],
        out_specs=[tile, tile],
        out_shape=[jax.ShapeDtypeStruct((S, A_Q_HEADS * HEAD_DIM), F32)] * 2,
        compiler_params=_cparams("parallel", "arbitrary"),
    )(qk, qk, v_arr, sinks)


def _swa_bwd(qk, v_arr, v_col, o_arr, do_arr, lse_arr, sinks, comm=None):
    S = qk.shape[0]
    tq = min(SWA_TQ, S - WINDOW)
    sub = min(SWA_SUB, S // tq)
    win = tq + WINDOW
    n_pairs = A_Q_HEADS // 2
    grid = (n_pairs, S // (sub * tq))

    def kern(*refs):
        own, comm_refs = _own_refs(refs, comm, 7, 4, 0)
        q_ref, k_ref, v_ref, o_ref, do_ref, lse_ref, sink_ref, dq_ref, dk_ref, dv_ref, dsink_ref = own
        _comm_edge(comm, comm_refs, grid, first=True)
        p_id, i = pl.program_id(0), pl.program_id(1)

        @pl.when((p_id == 0) & (i == 0))
        def _():
            dk_ref[...] = jnp.zeros_like(dk_ref)
            dv_ref[...] = jnp.zeros_like(dv_ref)

        @pl.when(i == 0)
        def _():
            dsink_ref[...] = jnp.zeros_like(dsink_ref)

        for t in range(sub):
            rows = slice(t * tq, (t + 1) * tq)
            start, offset = _swa_window(i * sub + t, tq)
            wrows = pl.ds(start, win)
            kw = k_ref[wrows, :]
            vw = v_ref[wrows, :].astype(BF)
            valid = _swa_valid(offset, tq)
            q, do, o, lse2 = q_ref[rows, :], do_ref[rows, :], o_ref[rows, :], lse_ref[rows, :]
            dq = jnp.zeros((tq, LANES), F32)
            dk = jnp.zeros((win, LANES), F32)
            dv = jnp.zeros((win, LANES), F32)
            for half in (0, 1):
                hm = _half_mask((tq, LANES), half)
                lane0 = half * HEAD_DIM
                qh = (jnp.where(hm, q, 0).astype(F32) * QK_SCALE).astype(BF)
                do_f = jnp.where(hm, do, 0.0)
                doh = do_f.astype(BF)
                delta = jnp.sum(do_f * o, axis=1, keepdims=True)
                lse = lse2[:, lane0:lane0 + 1]
                s = lax.dot_general(qh, kw, (((1,), (1,)), ((), ())), preferred_element_type=F32)
                p = jnp.exp(jnp.where(valid, s, NEG_INF) - lse)
                dp = lax.dot_general(doh, vw, (((1,), (1,)), ((), ())), preferred_element_type=F32)
                ds_b = (p * (dp - delta)).astype(BF)
                dv = dv + lax.dot_general(p.astype(BF), doh, (((0,), (0,)), ((), ())),
                                          preferred_element_type=F32)
                dk = dk + lax.dot_general(ds_b, qh, (((0,), (0,)), ((), ())), preferred_element_type=F32)
                kh = jnp.where(_half_mask((win, LANES), half), kw, 0)
                dq = dq + jnp.dot(ds_b, kh, preferred_element_type=F32)
                p_sink = jnp.exp(sink_ref[2 * p_id + half] - lse)
                dsink_ref[0, half:half + 1, :] += jnp.broadcast_to(
                    -jnp.sum(p_sink * delta, axis=0, keepdims=True), (1, LANES))
            dq_ref[rows, :] = dq * QK_SCALE
            dk_ref[wrows, :] += dk
            dv_ref[wrows, :] += dv
        _comm_edge(comm, comm_refs, grid, first=False)

    tile = pl.BlockSpec((sub * tq, LANES), lambda p, i: (i, p))
    whole = lambda col: pl.BlockSpec((S, LANES), lambda p, i: (0, col))
    res = pl.pallas_call(
        kern, name="swa_bwd", grid=grid,
        in_specs=[tile, whole(n_pairs), whole(v_col), tile, tile, tile,
                  pl.BlockSpec(memory_space=pltpu.SMEM)] + _comm_specs(comm, "in"),
        out_specs=[tile, whole(0), whole(0),
                   pl.BlockSpec((1, 8, LANES), lambda p, i: (p, 0, 0))] + _comm_specs(comm, "out"),
        out_shape=[jax.ShapeDtypeStruct((S, A_Q_HEADS * HEAD_DIM), F32),
                   jax.ShapeDtypeStruct((S, LANES), F32), jax.ShapeDtypeStruct((S, LANES), F32),
                   jax.ShapeDtypeStruct((n_pairs, 8, LANES), F32)] + (comm.out_shapes if comm else []),
        scratch_shapes=comm.sem_shapes if comm else [],
        compiler_params=_cparams("arbitrary", "arbitrary"),
    )(qk, qk, v_arr, o_arr, do_arr, lse_arr, sinks, *(comm.ins if comm else []))
    return (*res[:4], res[4:])


ADAMW_BLOCK = 512 * 1024


def _adamw(w, g, m, v, name, comm=None):
    R, C = w.shape
    tr, tc = _tile(R, max(8, ADAMW_BLOCK // C), 8), C
    grid = (R // tr, C // tc)

    def kern(*refs):
        (w_ref, g_ref, m_ref, v_ref, d_ref, mo_ref, vo_ref), comm_refs = _own_refs(refs, comm, 4, 3, 0)
        _comm_edge(comm, comm_refs, grid, first=True)
        g_ = g_ref[...]
        m_new = ADAM_B1 * m_ref[...] + (1.0 - ADAM_B1) * g_
        v_new = ADAM_B2 * v_ref[...] + (1.0 - ADAM_B2) * (g_ * g_)
        m_hat = m_new / (1.0 - ADAM_B1 ** ADAM_STEP)
        v_hat = v_new / (1.0 - ADAM_B2 ** ADAM_STEP)
        d_ref[...] = -ADAM_LR * (m_hat / (jnp.sqrt(v_hat) + ADAM_EPS) + ADAM_WD * w_ref[...])
        mo_ref[...] = m_new
        vo_ref[...] = v_new
        _comm_edge(comm, comm_refs, grid, first=False)

    spec = pl.BlockSpec((tr, tc), lambda i, j: (i, j))
    shape = jax.ShapeDtypeStruct((R, C), F32)
    res = pl.pallas_call(
        kern, name=name, grid=grid,
        in_specs=[spec] * 4 + _comm_specs(comm, "in"),
        out_specs=[spec] * 3 + _comm_specs(comm, "out"),
        out_shape=[shape] * 3 + (comm.out_shapes if comm else []),
        scratch_shapes=comm.sem_shapes if comm else [],
        input_output_aliases={4 + i: 3 + o for i, o in comm.aliases.items()} if comm else {},
        compiler_params=_cparams("arbitrary", "arbitrary"),
    )(w, g, m, v, *(comm.ins if comm else []))
    return (res[:3], res[3:]) if comm else res


def _index_operand(i):
    return jnp.reshape(i, (1,)).astype(jnp.int32)


def _add_pair(whole, got, ci, name):
    P, R, C = whole.shape
    half = R // 2
    tr = _tile(half, ROWS, 16)
    nb = half // tr

    def kern(ci_ref, a_ref, b_ref, o_ref, ob_ref):
        s = a_ref[...] + b_ref[...].astype(F32)
        o_ref[...] = s
        ob_ref[...] = s.astype(BF)

    spec = pl.BlockSpec((None, tr, C), lambda p, i, ci_ref: (p, i, 0))
    return pl.pallas_call(
        kern, name=name,
        grid_spec=pltpu.PrefetchScalarGridSpec(
            num_scalar_prefetch=1, grid=(P, nb),
            in_specs=[pl.BlockSpec((None, tr, C), lambda p, i, ci_ref: (p, ci_ref[0] * nb + i, 0)), spec],
            out_specs=[spec, spec]),
        out_shape=[jax.ShapeDtypeStruct((P, half, C), F32), jax.ShapeDtypeStruct((P, half, C), BF)],
        compiler_params=_cparams("parallel", "parallel"),
    )(_index_operand(ci), whole, got)


def _add_three(parts, recv, chip, name):
    _, R, C = parts.shape
    tr = _tile(R, ROWS, 16)

    def kern(chip_ref, o_ref, r0_ref, r1_ref, r2_ref, out_ref):
        s = ((o_ref[...] + r0_ref[...].astype(F32)) + r1_ref[...].astype(F32)) + r2_ref[...].astype(F32)
        out_ref[0] = s
        out_ref[1] = s

    slab = lambda k: pl.BlockSpec((None, tr, C), lambda i, chip_ref: (k, i, 0))
    return pl.pallas_call(
        kern, name=name,
        grid_spec=pltpu.PrefetchScalarGridSpec(
            num_scalar_prefetch=1, grid=(R // tr,),
            in_specs=[pl.BlockSpec((None, tr, C), lambda i, chip_ref: (chip_ref[0], i, 0)),
                      slab(0), slab(1), slab(2)],
            out_specs=pl.BlockSpec((2, tr, C), lambda i, chip_ref: (0, i, 0))),
        out_shape=jax.ShapeDtypeStruct((2, R, C), F32),
        compiler_params=_cparams("parallel"),
    )(_index_operand(chip), parts, recv, recv, recv)


SM_ADA, SM_G, SM_LOSS, SM_BF, SM_SINK, SM_LEN = 0, 6144, 10240, 11264, 11272, 12288


def _small_finalize(gathered):
    def kern(g_ref, tot_ref, loss_ref):
        tot = g_ref[0:1, :]
        for b in range(1, N_DEV):
            tot = tot + g_ref[b:b + 1, :]
        tot_ref[...] = tot
        sq = jnp.sum(tot[:, SM_LOSS:SM_LOSS + D_MODEL], axis=1, keepdims=True)
        loss_ref[...] = jnp.broadcast_to(sq * (0.5 / D_MODEL), (1, LANES))

    full = lambda shape: pl.BlockSpec(shape, lambda i: (0, 0))
    return pl.pallas_call(
        kern, name="small_finalize", grid=(1,),
        in_specs=[full((N_DEV, SM_LEN))],
        out_specs=[full((1, SM_LEN)), full((1, LANES))],
        out_shape=[jax.ShapeDtypeStruct((1, SM_LEN), F32), jax.ShapeDtypeStruct((1, LANES), F32)],
        compiler_params=_cparams("arbitrary"),
    )(gathered)


def _ada_dw(c_t, d_ada):
    N = d_ada.shape[1]
    tn = _tile(N, 512)

    def kern(c_ref, d_ref, o_ref):
        acc = c_ref[:, 0:1] * d_ref[0:1, :]
        for b in range(1, N_DEV):
            acc = acc + c_ref[:, b:b + 1] * d_ref[b:b + 1, :]
        o_ref[...] = acc

    return pl.pallas_call(
        kern, name="ada_dw", grid=(N // tn,),
        in_specs=[pl.BlockSpec((D_MODEL, N_DEV), lambda j: (0, 0)), pl.BlockSpec((N_DEV, tn), lambda j: (0, j))],
        out_specs=pl.BlockSpec((D_MODEL, tn), lambda j: (0, j)),
        out_shape=jax.ShapeDtypeStruct((D_MODEL, N), F32),
        compiler_params=_cparams("parallel"),
    )(c_t, d_ada)


def _here():
    return lax.axis_index("x"), lax.axis_index("y"), lax.axis_index("c")


def _other_chips(x, y):
    return [(1 - x, y), (x, 1 - y), (1 - x, 1 - y)]


_ANY = pl.BlockSpec(memory_space=pl.ANY)


class _Comm:
    def __init__(self, ins, out_shapes, sem_shapes, start, finish, aliases=None):
        self.ins, self.out_shapes, self.sem_shapes = list(ins), list(out_shapes), list(sem_shapes)
        self.start, self.finish = start, finish
        self.aliases = dict(aliases or {})

    def split(self, refs, n_in, n_out, n_scratch):
        a = n_in + len(self.ins)
        b = a + n_out + len(self.out_shapes)
        own = list(refs[:n_in]) + list(refs[a:a + n_out]) + list(refs[b:b + n_scratch])
        mine = (refs[n_in:a], refs[a + n_out:b], refs[b + n_scratch:])
        return own, mine


def _run_comm(comm, name):
    n_in, n_out = len(comm.ins), len(comm.out_shapes)

    def body(*refs):
        parts = (refs[:n_in], refs[n_in:n_in + n_out], refs[n_in + n_out:])
        comm.start(*parts)
        comm.finish(*parts)

    return pl.pallas_call(
        body, name=name,
        in_specs=[_ANY] * n_in, out_specs=[_ANY] * n_out,
        out_shape=comm.out_shapes, scratch_shapes=comm.sem_shapes,
        input_output_aliases=comm.aliases,
    )(*comm.ins)


def _gather_comm(blocks):
    L = len(blocks)

    def parts(ins, outs, sems):
        send_sems, recv_sems, local_sems = sems
        x, y, c = _here()
        me, sibling = (x, y, c), (x, y, 1 - c)
        chips = _other_chips(x, y)

        def slot(px, py, pc):
            return 4 * px + 2 * py + pc

        def copy(l, k, block, to, src=None):
            dst = outs[l].at[slot(*block)]
            return pltpu.make_async_remote_copy(
                src_ref=dst if src is None else src, dst_ref=dst,
                send_sem=send_sems.at[l, k], recv_sem=recv_sems.at[l, k],
                device_id=to, device_id_type=MESH)

        mine = [pltpu.make_async_copy(ins[l], outs[l].at[slot(*me)], local_sems.at[l]) for l in range(L)]
        first = []
        for l in range(L):
            first.append(copy(l, 0, me, sibling, src=ins[l]))
            for j, chip in enumerate(chips):
                first.append(copy(l, 1 + j, me, (*chip, c), src=ins[l]))
        return c, me, sibling, chips, copy, mine, first

    def start(ins, outs, sems):
        *_, mine, first = parts(ins, outs, sems)
        for cp in mine + first:
            cp.start()

    def finish(ins, outs, sems):
        c, me, sibling, chips, copy, mine, first = parts(ins, outs, sems)
        passed = []
        for j, chip in enumerate(chips):
            for l in range(L):
                copy(l, 1 + j, (*chip, c), me).wait_recv()
                fwd = copy(l, 4 + j, (*chip, c), sibling)
                fwd.start()
                passed.append(fwd)
        for l in range(L):
            copy(l, 0, sibling, me).wait_recv()
        for j, chip in enumerate(chips):
            for l in range(L):
                copy(l, 4 + j, (*chip, 1 - c), me).wait_recv()
        for cp in first + passed:
            cp.wait_send()
        for cp in mine:
            cp.wait()

    return _Comm(blocks, [jax.ShapeDtypeStruct((N_DEV,) + b.shape, b.dtype) for b in blocks],
                 [pltpu.SemaphoreType.DMA((L, 7)), pltpu.SemaphoreType.DMA((L, 7)), pltpu.SemaphoreType.DMA((L,))],
                 start, finish)


def _allgather8(blocks, name):
    return _run_comm(_gather_comm(blocks), name)


def _swap_comm(arrs):
    L = len(arrs)

    def copies(ins, outs, sems):
        send_sems, recv_sems = sems
        x, y, c = _here()
        cps = []
        for l in range(L):
            half = arrs[l].shape[1] // 2
            rows = pl.ds(pl.multiple_of((1 - c) * half, 16), half)
            cps.append(pltpu.make_async_remote_copy(
                src_ref=ins[l].at[:, rows, :], dst_ref=outs[l], send_sem=send_sems.at[l],
                recv_sem=recv_sems.at[l], device_id=(x, y, 1 - c), device_id_type=MESH))
        return cps

    def start(ins, outs, sems):
        for cp in copies(ins, outs, sems):
            cp.start()

    def finish(ins, outs, sems):
        for cp in copies(ins, outs, sems):
            cp.wait()

    return _Comm(arrs, [jax.ShapeDtypeStruct((a.shape[0], a.shape[1] // 2, a.shape[2]), a.dtype) for a in arrs],
                 [pltpu.SemaphoreType.DMA((L,)), pltpu.SemaphoreType.DMA((L,))], start, finish)


def _join_comm(bufs):
    L = len(bufs)

    def start(ins, outs, sems):
        send_sems, recv_sems = sems
        x, y, c = _here()
        for l in range(L):
            pltpu.make_async_remote_copy(src_ref=outs[l].at[c], dst_ref=outs[l].at[c], send_sem=send_sems.at[l],
                                         recv_sem=recv_sems.at[l], device_id=(x, y, 1 - c),
                                         device_id_type=MESH).start()

    def finish(ins, outs, sems):
        send_sems, recv_sems = sems
        x, y, c = _here()
        for l in range(L):
            pltpu.make_async_remote_copy(src_ref=outs[l].at[c], dst_ref=outs[l].at[1 - c],
                                         send_sem=send_sems.at[l], recv_sem=recv_sems.at[l],
                                         device_id=(x, y, 1 - c), device_id_type=MESH).wait()

    return _Comm(bufs, [jax.ShapeDtypeStruct(a.shape, a.dtype) for a in bufs],
                 [pltpu.SemaphoreType.DMA((L,)), pltpu.SemaphoreType.DMA((L,))], start, finish,
                 aliases={l: l for l in range(L)})


def _scatter_comm(arrs):
    L = len(arrs)

    def copies(ins, outs, sems):
        send_sems, recv_sems = sems
        x, y, c = _here()
        return [pltpu.make_async_remote_copy(
            src_ref=ins[l].at[2 * tx + ty], dst_ref=outs[l].at[j],
            send_sem=send_sems.at[l, j], recv_sem=recv_sems.at[l, j],
            device_id=(tx, ty, c), device_id_type=MESH)
            for l in range(L) for j, (tx, ty) in enumerate(_other_chips(x, y))]

    def start(ins, outs, sems):
        for cp in copies(ins, outs, sems):
            cp.start()

    def finish(ins, outs, sems):
        for cp in copies(ins, outs, sems):
            cp.wait()

    return _Comm(arrs, [jax.ShapeDtypeStruct((3,) + a.shape[1:], a.dtype) for a in arrs],
                 [pltpu.SemaphoreType.DMA((L, 3)), pltpu.SemaphoreType.DMA((L, 3))], start, finish)


_A_ORDER = np.array(A_HEAD_ORDER)
_A_INVERSE = np.argsort(_A_ORDER)


def _permute_in_weights(w_in):
    qa = w_in[:, 0:512].reshape(D_MODEL, A_Q_HEADS, HEAD_DIM)[:, _A_ORDER, :].reshape(D_MODEL, 512)
    f_pad = jnp.pad(w_in[:, 2304:2312], ((0, 0), (0, LANES - B_HEADS)))
    w_a = jnp.concatenate([qa, w_in[:, 512:640], f_pad], axis=1)
    return w_a, w_in[:, 640:2304], w_in[:, 2312:4360]


def _slab_segments():
    segs = [(h * HEAD_DIM, int(_A_INVERSE[h]) * HEAD_DIM, HEAD_DIM) for h in range(A_Q_HEADS)]
    segs += [(512, OFF_KA, 128), (640, W_A + OFF_VA, 128), (768, W_A + OFF_QB, 1536),
             (2304, OFF_F, B_HEADS), (2312, W_A + W_B, W_G)]
    return segs


def _shard_slabs(dw_perm):
    R = dw_perm.shape[0]
    tr = _tile(R, 128, 8)
    plan = []
    for k in range(N_CHIP):
        for b in range(W_SHARD_PAD // LANES):
            lo, hi = k * W_SHARD + b * LANES, min(k * W_SHARD + (b + 1) * LANES, (k + 1) * W_SHARD)
            parts = []
            for o0, s0, n in _slab_segments():
                a, z = max(lo, o0), min(hi, o0 + n)
                while a < z:
                    s = s0 + (a - o0)
                    run = min(z - a, LANES - s % LANES)
                    parts.append((s // LANES, ((a - lo) - s % LANES) % LANES, a - lo, run))
                    a += run
            plan.append((k, b, parts))

    def kern(x_ref, o32_ref, obf_ref):
        lane = lax.broadcasted_iota(jnp.int32, (tr, LANES), 1)
        for k, b, parts in plan:
            acc = jnp.zeros((tr, LANES), F32)
            for src, rot, first, run in parts:
                blk = x_ref[:, src * LANES:(src + 1) * LANES]
                if rot:
                    blk = pltpu.roll(blk, rot, 1)
                acc = jnp.where((lane >= first) & (lane < first + run), blk, acc)
            o32_ref[k, :, b * LANES:(b + 1) * LANES] = acc
            obf_ref[k, :, b * LANES:(b + 1) * LANES] = acc.astype(BF)

    out_spec = pl.BlockSpec((N_CHIP, tr, W_SHARD_PAD), lambda i: (0, i, 0))
    return tuple(pl.pallas_call(
        kern, name="shard_slabs", grid=(R // tr,),
        in_specs=[pl.BlockSpec((tr, W_PERM), lambda i: (i, 0))],
        out_specs=[out_spec, out_spec],
        out_shape=[jax.ShapeDtypeStruct((N_CHIP, R, W_SHARD_PAD), F32),
                   jax.ShapeDtypeStruct((N_CHIP, R, W_SHARD_PAD), BF)],
        compiler_params=_cparams("parallel"),
    )(dw_perm))


class _NoExchange:
    def __init__(self, w_in, rest):
        self.w_in_whole, self.rest, self.grads = w_in, rest, {}

    def w_in_comm(self):
        return None

    def w_in(self, outs):
        return self.w_in_whole

    def rest_weights_comm(self):
        return None

    def rest_weights(self, outs):
        return self.rest

    def swap_comm(self, pieces, tag):
        self.grads[tag] = [p32 for p32, _ in pieces]
        return None

    def swap_done(self, outs, tag):
        return None

    def reduce_done(self, outs, tag):
        pass

    def join_comm(self):
        return None


class _Exchange:
    def __init__(self, ci, chip, w_in_shard, rest_shards):
        self.ci, self.chip, self.w_in_shard, self.rest_shards = ci, chip, w_in_shard, rest_shards
        self.pieces, self.part_f32, self.halves = {}, {}, {}

    def _my_half(self, a, axis=0, other=False):
        rows = a.shape[axis] // 2
        return lax.dynamic_slice_in_dim(a, ((1 - self.ci) if other else self.ci) * rows, rows, axis=axis)

    def w_in_comm(self):
        return _gather_comm([self._my_half(self.w_in_shard).astype(BF)])

    def w_in(self, outs):
        return _col_sharded(outs[0])

    def rest_weights_comm(self):
        return _gather_comm([self._my_half(w).astype(BF) for w in self.rest_shards])

    def rest_weights(self, outs):
        w_ba, w_bb, w_out, w_fi, w_fo = outs
        return (_col_sharded(w_ba), _col_sharded(w_bb), _row_sharded(w_out), _col_sharded(w_fi),
                _row_sharded(w_fo))

    def swap_comm(self, pieces, tag):
        self.pieces[tag] = pieces
        return _swap_comm([pbf for _, pbf in pieces])

    def swap_done(self, got, tag):
        self.part_f32[tag], part_bf = [], []
        for l, ((p32, _), g_) in enumerate(zip(self.pieces[tag], got)):
            s32, sbf = _add_pair(p32, g_, self.ci, f"chip_sum_{tag}_{l}")
            self.part_f32[tag].append(s32)
            part_bf.append(sbf)
        return _scatter_comm(part_bf)

    def reduce_done(self, outs, tag):
        self.halves[tag] = [_add_three(p32, r, self.chip, f"shard_sum_{tag}_{l}")
                            for l, (p32, r) in enumerate(zip(self.part_f32[tag], outs))]

    def join_comm(self):
        return _join_comm(self.halves["late"] + self.halves["early"])


def _col_sharded(g):
    return jnp.transpose(g.reshape(N_CHIP, -1, g.shape[-1]), (1, 0, 2)).reshape(2 * g.shape[1], N_CHIP * g.shape[-1])


def _row_sharded(g):
    return g.reshape(N_DEV * g.shape[1], g.shape[-1])


def _rope_tables(pos):
    inv_freq = 1.0 / (ROPE_THETA ** (jnp.arange(0, HEAD_DIM, 2, dtype=F32) / HEAD_DIM))
    ang = pos.astype(F32)[:, None] * inv_freq
    cos, sin = jnp.cos(ang), jnp.sin(ang)
    return jnp.tile(cos, (1, 4)), jnp.tile(jnp.concatenate([-sin, sin], axis=1), (1, 2))


def _local_step(x, pos, ada, g1, g2, g3, g4, b_f, sinks, exch, target):
    S = x.shape[0]
    t_fox = _tile(S, 512, LANES) if S >= 1024 else S // 2
    t_fox_fwd = _tile(S, 1024, LANES) if S >= 2048 else S // 2
    shift_m, scale_m, gate_m, shift_f, scale_f, gate_f = [ada[i:i + 1] for i in range(N_ADA)]
    cos_t, sin_t = _rope_tables(pos)
    sinks_p = sinks.reshape(A_KV_HEADS, 4).T.reshape(A_Q_HEADS)
    b_f_pad = jnp.pad(b_f, (0, LANES - B_HEADS)).reshape(1, LANES)

    h1, outs = _pre_norm(x, g1, scale_m, shift_m, "pre_mix_norm", comm=exch.w_in_comm())
    w_a, w_b, w_g = _permute_in_weights(exch.w_in(outs))
    w_perm = jnp.concatenate([w_a, w_b, w_g], axis=1)
    p_a = _mm(h1, w_a, "nn", F32, "proj_a")
    p_b = _mm(h1, w_b, "nn", BF, "proj_b")
    p_g = _mm(h1, w_g, "nn", BF, "proj_g")
    (qk_a,) = _rope([p_a], [640], cos_t, sin_t, "rope_fwd")
    o_a, lse_a = _swa_fwd(qk_a, p_b, 0, sinks_p)
    cum = _fox_gate_fwd(p_a, b_f_pad)
    bq, bk = _fox_prep_fwd(cum, t_fox_fwd)
    comm = exch.rest_weights_comm()
    o_b, lse_b, outs = _fox_fwd(p_b, bq, bk, t_fox_fwd, comm=comm)
    w_ba, w_bb, w_out, w_fi, w_fo = exch.rest_weights(outs)
    w_ba_p = w_ba.reshape(A_Q_HEADS, HEAD_DIM, D_MODEL)[_A_ORDER].reshape(512, D_MODEL)
    pa = _mm(o_a, w_ba_p, "nn", BF, "branch_a")
    pb = _mm(o_b, w_bb, "nn", BF, "branch_b")
    merged = _merge_fwd(p_g, pa, pb)
    y1 = _mm(merged, w_out, "nn", BF, "out_proj")
    x2, h2 = _post_pre(x, y1, g2, gate_m, g3, scale_f, shift_f)
    gu = _mm(h2, w_fi, "nn", BF, "ffn_in")
    act = _swiglu_fwd(gu)
    y2 = _mm(act, w_fo, "nn", BF, "ffn_out")
    d_out, d_y2, st_f = _final(x2, y2, g4, gate_f, target)

    d_act = _mm(d_y2, w_fo, "nt", BF, "ffn_out_dx")
    row_pieces = lambda pair: tuple(t.reshape(N_CHIP, t.shape[0] // N_CHIP, t.shape[1]) for t in pair)
    dw_fo = row_pieces(_mm(act, d_y2, "tn", F32, "ffn_out_dw", twin=True))
    d_gu = _swiglu_bwd(d_act, gu)
    d_h2 = _mm(d_gu, w_fi, "nt", BF, "ffn_in_dx")
    dw_fi = _mm(h2, d_gu, "tn", F32, "ffn_in_dw", col_pieces=N_CHIP, twin=True)
    d_x2, d_y1, st_m = _mid_bwd(d_h2, x2, d_out, y1, g3, scale_f, g2, gate_m)
    d_merged = _mm(d_y1, w_out, "nt", BF, "out_proj_dx")
    dw_out = row_pieces(_mm(merged, d_y1, "tn", F32, "out_proj_dw", twin=True))
    d_pa, d_pb, d_ga, d_gb = _merge_bwd(d_merged, p_g, pa, pb)
    d_oa = _mm(d_pa, w_ba_p, "nt", F32, "branch_a_dx")
    dw_ba_p = _mm(o_a, d_pa, "tn", F32, "branch_a_dw", col_pieces=N_CHIP, twin=True)
    d_ob = _mm(d_pb, w_bb, "nt", F32, "branch_b_dx")
    dw_bb = _mm(o_b, d_pb, "tn", F32, "branch_b_dw", col_pieces=N_CHIP, twin=True)
    head_rows = lambda t: t.reshape(N_CHIP, A_Q_HEADS, HEAD_DIM, -1)[:, _A_INVERSE].reshape(t.shape)
    dw_ba = tuple(head_rows(t) for t in dw_ba_p)
    comm = exch.swap_comm([dw_ba, dw_bb, dw_out, dw_fi, dw_fo], "early")
    dq_a, dk_a, dv_a, d_sink, outs = _swa_bwd(qk_a, p_b, 0, o_a, d_oa, lse_a, sinks_p, comm=comm)
    comm = exch.swap_done(outs, "early")
    bq_bwd, bdo = _fox_prep_bwd(cum, o_b, d_ob, lse_b, t_fox_fwd)
    dq_b, dk_b, dv_b, d_ck, d_cq, outs = _fox_bwd(p_b, d_ob, bq_bwd, bk, bdo, t_fox, comm=comm)
    exch.reduce_done(outs, "early")
    d_qa, d_ka = _rope([dq_a, dk_a], [512, LANES], cos_t, -sin_t, "rope_bwd")
    d_ck_cols = jnp.pad(d_ck.reshape(B_HEADS, S).T, ((0, 0), (0, LANES - B_HEADS)))
    d_f, d_bf = _fox_gate_bwd(d_cq, d_ck_cols, p_a, b_f_pad)
    d_proj = jnp.concatenate([d_qa, d_ka, d_f, dv_a.astype(BF), dq_b.astype(BF), dk_b.astype(BF),
                              dv_b.astype(BF), d_ga, d_gb], axis=1)
    dw_perm = _mm(h1, d_proj, "tn", F32, "proj_dw")
    swap = exch.swap_comm([_shard_slabs(dw_perm)], "late")
    comm = exch.swap_done(_run_comm(swap, "grads_to_sibling_late") if swap else None, "late")
    res = _mm(d_proj, w_perm, "nt", BF, "proj_dx", comm=comm)
    d_h1 = res[0] if comm else res
    exch.reduce_done(res[1] if comm else None, "late")
    grad_x, st_p, outs = _pre_bwd(d_h1, x, d_x2, g1, scale_m, comm=exch.join_comm())
    exch.joined = outs

    d_sinks = d_sink[:, :2, 0].T.reshape(A_Q_HEADS)
    small = jnp.concatenate([
        st_p[0], st_p[1], st_m[3], st_m[0], st_m[1], st_f[0],
        st_p[2], st_m[4], st_m[2], st_f[1],
        st_f[2], d_bf[0, :B_HEADS], d_sinks,
        jnp.zeros((SM_LEN - SM_SINK - A_Q_HEADS,), F32)])
    return grad_x, small


def kernel(x, c, positions, w_ada, b_ada, g_pre_mix, g_post_mix, w_in, b_f, sinks, w_branch_a, w_branch_b, w_out, g_pre_ffn, g_post_ffn, w_ffn_in, w_ffn_out, loss_target, m_w_ada, m_b_ada, m_g_pre_mix, m_g_post_mix, m_w_in, m_b_f, m_sinks, m_w_branch_a, m_w_branch_b, m_w_out, m_g_pre_ffn, m_g_post_ffn, m_w_ffn_in, m_w_ffn_out, v_w_ada, v_b_ada, v_g_pre_mix, v_g_post_mix, v_w_in, v_b_f, v_sinks, v_w_branch_a, v_w_branch_b, v_w_out, v_g_pre_ffn, v_g_post_ffn, v_w_ffn_in, v_w_ffn_out):
    xi, yi, ci = _here()
    chip = 2 * xi + yi
    dev = 2 * chip + ci

    (c_g,) = _allgather8([c.reshape(8, LANES)], "gather_c")
    c_all = c_g.reshape(N_DEV, D_MODEL)
    exch = _Exchange(ci, chip, w_in[0], [w_branch_a[0], w_branch_b[0], w_out[0], w_ffn_in[0], w_ffn_out[0]])

    ada_cols = _mm(c_all, w_ada[0], "nn", F32, "ada_fwd")
    (ada_g,) = _allgather8([ada_cols], "gather_ada")
    ada_mine = lax.dynamic_index_in_dim(ada_g.reshape(N_CHIP, 2, N_DEV, -1)[:, 0], dev, axis=1, keepdims=False)
    ada = (ada_mine.reshape(-1) + b_ada[0]).reshape(N_ADA, D_MODEL)

    grad_x, small = _local_step(
        x[0], positions[0], ada, g_pre_mix, g_post_mix, g_pre_ffn, g_post_ffn, b_f[0], sinks[0],
        exch, loss_target[0])

    g_w_in, g_w_ba, g_w_bb, g_w_out, g_w_fi, g_w_fo = [j.reshape(2 * j.shape[1], j.shape[2]) for j in exch.joined]
    upd_fi, (small_g,) = _adamw(w_ffn_in[0], g_w_fi, m_w_ffn_in[0], v_w_ffn_in[0], "adamw_w_ffn_in",
                                comm=_gather_comm([small.reshape(8, SM_LEN // 8)]))

    small_all = small_g.reshape(N_DEV, SM_LEN)
    small_tot, loss_row = _small_finalize(small_all)
    loss = loss_row[0, 0]
    d_ada_cols = lax.dynamic_slice_in_dim(small_all[:, :N_ADA * D_MODEL], chip * (N_ADA * D_MODEL // N_CHIP),
                                          N_ADA * D_MODEL // N_CHIP, axis=1)
    g_w_ada = _ada_dw(c_all.T, d_ada_cols)

    def small_vec(b_ada_, g1_, g2_, g3_, g4_, b_f_, sinks_):
        return jnp.concatenate([b_ada_[0], g1_[0], g2_[0], g3_[0], g4_[0], jnp.zeros((D_MODEL,), F32),
                                b_f_[0], sinks_[0], jnp.zeros((SM_LEN - SM_SINK - A_Q_HEADS,), F32)]
                               ).reshape(8, SM_LEN // 8)

    sw = small_vec(b_ada, g_pre_mix, g_post_mix, g_pre_ffn, g_post_ffn, b_f, sinks)
    sm = small_vec(m_b_ada, m_g_pre_mix, m_g_post_mix, m_g_pre_ffn, m_g_post_ffn, m_b_f, m_sinks)
    sv = small_vec(v_b_ada, v_g_pre_mix, v_g_post_mix, v_g_pre_ffn, v_g_post_ffn, v_b_f, v_sinks)
    s_upd = [u.reshape(SM_LEN) for u in _adamw(sw, small_tot.reshape(8, SM_LEN // 8), sm, sv, "adamw_small")]
    s_grad = small_tot.reshape(SM_LEN)

    def unpack(vec):
        row = lambda a, n: vec[a:a + n].reshape(1, n)
        return dict(b_ada=row(SM_ADA, N_ADA * D_MODEL), g_pre_mix=row(SM_G, D_MODEL),
                    g_post_mix=row(SM_G + D_MODEL, D_MODEL), g_pre_ffn=row(SM_G + 2 * D_MODEL, D_MODEL),
                    g_post_ffn=row(SM_G + 3 * D_MODEL, D_MODEL), b_f=row(SM_BF, B_HEADS),
                    sinks=row(SM_SINK, A_Q_HEADS))

    big = dict(
        w_ada=(w_ada, g_w_ada, m_w_ada, v_w_ada),
        w_branch_a=(w_branch_a, g_w_ba, m_w_branch_a, v_w_branch_a),
        w_branch_b=(w_branch_b, g_w_bb, m_w_branch_b, v_w_branch_b),
        w_out=(w_out, g_w_out, m_w_out, v_w_out),
        w_ffn_out=(w_ffn_out, g_w_fo, m_w_ffn_out, v_w_ffn_out))
    grads, deltas, new_m, new_v = unpack(s_grad), unpack(s_upd[0]), unpack(s_upd[1]), unpack(s_upd[2])
    grads["w_ffn_in"], deltas["w_ffn_in"], new_m["w_ffn_in"], new_v["w_ffn_in"] = [
        t[None] for t in (g_w_fi, *upd_fi)]
    for n, (w_, g_, m_, v_) in big.items():
        d_, nm_, nv_ = _adamw(w_[0], g_, m_[0], v_[0], "adamw_" + n)
        grads[n], deltas[n], new_m[n], new_v[n] = g_[None], d_[None], nm_[None], nv_[None]
    pad_cols = lambda a: jnp.pad(a, ((0, 0), (0, W_SHARD_PAD - W_SHARD)))
    upd = _adamw(pad_cols(w_in[0]), g_w_in, pad_cols(m_w_in[0]), pad_cols(v_w_in[0]), "adamw_w_in")
    grads["w_in"], deltas["w_in"], new_m["w_in"], new_v["w_in"] = [t[None, :, :W_SHARD] for t in (g_w_in, *upd)]

    names = ["w_ada", "b_ada", "g_pre_mix", "g_post_mix", "w_in", "b_f", "sinks", "w_branch_a", "w_branch_b",
             "w_out", "g_pre_ffn", "g_post_ffn", "w_ffn_in", "w_ffn_out"]
    return (loss, grad_x[None], *[grads[n] for n in names], *[deltas[n] for n in names],
            *[new_m[n] for n in names], *[new_v[n] for n in names])
```

```python
import functools
import math

import numpy as np
import jax
import jax.numpy as jnp
from jax import lax
from jax.experimental import pallas as pl
from jax.experimental.pallas import tpu as pltpu

F32 = jnp.float32
BF = jnp.bfloat16

D_MODEL = 1024
HEAD_DIM = 64
LANES = 128
WINDOW = 128
A_Q_HEADS = 8
A_KV_HEADS = 2
B_HEADS = 8
D_FF = 2816
ROPE_THETA = 10000.0
RMS_EPS = 1e-6
N_ADA = 6
N_DEV = 8
N_CHIP = 4

ADAM_LR = 0.001
ADAM_B1 = 0.9
ADAM_B2 = 0.999
ADAM_EPS = 1e-08
ADAM_WD = 0.01
ADAM_STEP = 10

VMEM_LIMIT = 48 * 1024 * 1024
MESH = pl.DeviceIdType.MESH

A_HEAD_ORDER = (0, 4, 1, 5, 2, 6, 3, 7)

OFF_QA, OFF_KA, OFF_F = 0, 512, 640
W_A = 768
OFF_VA, OFF_QB, OFF_KB, OFF_VB = 0, 128, 640, 1152
W_B = 1664
W_G = 2048
W_PERM = W_A + W_B + W_G
W_SHARD = 1090
W_SHARD_PAD = 1152


def _tile(n, cap, mult=LANES):
    if n <= cap:
        return n
    t = (cap // mult) * mult
    while t >= mult:
        if n % t == 0:
            return t
        t -= mult
    raise ValueError(f"no tile for {n}")


MXU_WIDTH = 256
MM_OPERAND_BYTES = 28 * 1024 * 1024


def _mm_tiles(M, N, K, a_bytes, b_bytes, tm_cap, tn_cap):
    tm = _tile(M, tm_cap)
    try:
        tn = _tile(N, tn_cap, MXU_WIDTH)
    except ValueError:
        tn = _tile(N, tn_cap)
    fits = lambda tk: 2 * tk * (tm * a_bytes + tn * b_bytes) <= MM_OPERAND_BYTES
    tk = K if fits(K) else next(t for t in range(K // LANES * LANES, 0, -LANES) if K % t == 0 and fits(t))
    return tm, tn, tk


def _cparams(*sem):
    return pltpu.CompilerParams(dimension_semantics=sem, vmem_limit_bytes=VMEM_LIMIT)


def _own_refs(refs, comm, n_in, n_out, n_scratch):
    if comm is None:
        return list(refs), None
    return comm.split(refs, n_in, n_out, n_scratch)


def _comm_specs(comm, side):
    if comm is None:
        return []
    return [pl.BlockSpec(memory_space=pl.ANY)] * len(comm.ins if side == "in" else comm.out_shapes)


def _comm_edge(comm, comm_refs, grid, first):
    if comm is None:
        return
    at_edge = None
    for axis, n in enumerate(grid):
        here = pl.program_id(axis) == (0 if first else n - 1)
        at_edge = here if at_edge is None else at_edge & here
    pl.when(at_edge)(lambda: (comm.start if first else comm.finish)(*comm_refs))


def _mm(a, b, mode, out_dtype, name, tm_cap=512, tn_cap=2816, comm=None, col_pieces=1, twin=False):
    if mode == "nn":
        (M, K), (K2, N) = a.shape, b.shape
        dims = (((1,), (0,)), ((), ()))
    elif mode == "nt":
        (M, K), (N, K2) = a.shape, b.shape
        dims = (((1,), (1,)), ((), ()))
    else:
        (K, M), (K2, N) = a.shape, b.shape
        dims = (((0,), (0,)), ((), ()))
    assert K == K2, (a.shape, b.shape, mode)
    tm, tn, tk = _mm_tiles(M, N // col_pieces, K, a.dtype.itemsize, b.dtype.itemsize, tm_cap, tn_cap)
    nk = K // tk
    n_out = 2 if twin else 1
    n_scratch = 1 if nk > 1 else 0
    if mode == "nn":
        a_spec = pl.BlockSpec((tm, tk), lambda i, j, k: (i, k))
        b_spec = pl.BlockSpec((tk, tn), lambda i, j, k: (k, j))
    elif mode == "nt":
        a_spec = pl.BlockSpec((tm, tk), lambda i, j, k: (i, k))
        b_spec = pl.BlockSpec((tn, tk), lambda i, j, k: (j, k))
    else:
        a_spec = pl.BlockSpec((tk, tm), lambda i, j, k: (k, i))
        b_spec = pl.BlockSpec((tk, tn), lambda i, j, k: (k, j))

    grid = (M // tm, N // tn, nk)

    def kern(*refs):
        own, comm_refs = _own_refs(refs, comm, 2, n_out, n_scratch)
        a_ref, b_ref, o_refs = own[0], own[1], own[2:2 + n_out]
        k = pl.program_id(2)
        _comm_edge(comm, comm_refs, grid, first=True)
        part = lax.dot_general(a_ref[...].astype(BF), b_ref[...].astype(BF), dims,
                               preferred_element_type=F32)
        if nk == 1:
            for o_ref in o_refs:
                o_ref[...] = part.astype(o_ref.dtype)
        else:
            acc_ref = own[2 + n_out]

            @pl.when(k == 0)
            def _():
                acc_ref[...] = part

            @pl.when(k > 0)
            def _():
                acc_ref[...] += part

            @pl.when(k == nk - 1)
            def _():
                for o_ref in o_refs:
                    o_ref[...] = acc_ref[...].astype(o_ref.dtype)

        _comm_edge(comm, comm_refs, grid, first=False)

    if col_pieces > 1:
        per = N // col_pieces // tn
        out_spec = pl.BlockSpec((None, tm, tn), lambda i, j, k: (j // per, i, j % per))
        shape = (col_pieces, M, N // col_pieces)
    else:
        out_spec = pl.BlockSpec((tm, tn), lambda i, j, k: (i, j))
        shape = (M, N)
    dtypes = [out_dtype, BF] if twin else [out_dtype]
    res = pl.pallas_call(
        kern, name=name, grid=grid,
        in_specs=[a_spec, b_spec] + _comm_specs(comm, "in"),
        out_specs=[out_spec] * n_out + _comm_specs(comm, "out"),
        out_shape=[jax.ShapeDtypeStruct(shape, d) for d in dtypes] + (comm.out_shapes if comm else []),
        scratch_shapes=[pltpu.VMEM((tm, tn), F32)] * n_scratch + (comm.sem_shapes if comm else []),
        compiler_params=_cparams("parallel", "parallel", "arbitrary"),
    )(a, b, *(comm.ins if comm else []))
    own = res[0] if n_out == 1 else tuple(res[:n_out])
    return (own, res[n_out:]) if comm else own


ROWS = 512


def _row_spec(tm, width=D_MODEL, col=0):
    return pl.BlockSpec((tm, width), lambda i: (i, col))


def _vec_spec(width=D_MODEL):
    return pl.BlockSpec((1, width), lambda i: (0, 0))


def _rms(x):
    return lax.rsqrt(jnp.mean(x * x, axis=-1, keepdims=True) + RMS_EPS)


def _colsum(x):
    return jnp.sum(x, axis=0, keepdims=True)


def _norm_bwd(d_xn, xn, r):
    return r * (d_xn - xn * jnp.mean(d_xn * xn, axis=-1, keepdims=True))


def _pre_norm(x, g, scale, shift, name, comm=None):
    S = x.shape[0]
    tm = _tile(S, 2 * ROWS, 8)
    grid = (S // tm,)

    def kern(*refs):
        (x_ref, g_ref, sc_ref, sh_ref, h_ref), comm_refs = _own_refs(refs, comm, 4, 1, 0)
        _comm_edge(comm, comm_refs, grid, first=True)
        xf = x_ref[...]
        y = xf * _rms(xf) * g_ref[...]
        h_ref[...] = (y * (1.0 + sc_ref[...]) + sh_ref[...]).astype(BF)
        _comm_edge(comm, comm_refs, grid, first=False)

    res = pl.pallas_call(
        kern, name=name, grid=grid,
        in_specs=[_row_spec(tm), _vec_spec(), _vec_spec(), _vec_spec()] + _comm_specs(comm, "in"),
        out_specs=[_row_spec(tm)] + _comm_specs(comm, "out"),
        out_shape=[jax.ShapeDtypeStruct((S, D_MODEL), BF)] + (comm.out_shapes if comm else []),
        scratch_shapes=comm.sem_shapes if comm else [],
        compiler_params=_cparams("arbitrary"),
    )(x, g, scale, shift, *(comm.ins if comm else []))
    return res[0], res[1:]


def _post_pre(x, y1, g2, gate_m, g3, scale_f, shift_f):
    S = x.shape[0]
    tm = _tile(S, 2 * ROWS, 8)

    def kern(x_ref, y_ref, g2_ref, gm_ref, g3_ref, sc_ref, sh_ref, x2_ref, h2_ref):
        y = y_ref[...].astype(F32)
        n2 = y * _rms(y) * g2_ref[...]
        x2 = x_ref[...] + gm_ref[...] * n2
        x2_ref[...] = x2
        n3 = x2 * _rms(x2) * g3_ref[...]
        h2_ref[...] = (n3 * (1.0 + sc_ref[...]) + sh_ref[...]).astype(BF)

    return pl.pallas_call(
        kern, name="post_mix_pre_ffn", grid=(S // tm,),
        in_specs=[_row_spec(tm), _row_spec(tm)] + [_vec_spec()] * 5,
        out_specs=[_row_spec(tm), _row_spec(tm)],
        out_shape=[jax.ShapeDtypeStruct((S, D_MODEL), F32), jax.ShapeDtypeStruct((S, D_MODEL), BF)],
        compiler_params=_cparams("parallel"),
    )(x, y1, g2, gate_m, g3, scale_f, shift_f)


def _stats_spec():
    return pl.BlockSpec((8, D_MODEL), lambda i: (0, 0))


def _final(x2, y2, g4, gate_f, target):
    S = x2.shape[0]
    tm = _tile(S, ROWS, 8)

    def kern(x2_ref, y_ref, g4_ref, gf_ref, t_ref, dout_ref, dy_ref, st_ref):
        @pl.when(pl.program_id(0) == 0)
        def _():
            st_ref[...] = jnp.zeros_like(st_ref)

        y = y_ref[...].astype(F32)
        r = _rms(y)
        yn = y * r
        n4 = yn * g4_ref[...]
        diff = x2_ref[...] + gf_ref[...] * n4 - t_ref[...]
        d_out = diff / D_MODEL
        dout_ref[...] = d_out
        dn = d_out * gf_ref[...]
        dy_ref[...] = _norm_bwd(dn * g4_ref[...], yn, r).astype(BF)
        st_ref[0:1, :] += _colsum(d_out * n4)
        st_ref[1:2, :] += _colsum(dn * yn)
        st_ref[2:3, :] += _colsum(diff * diff)

    return pl.pallas_call(
        kern, name="final_loss", grid=(S // tm,),
        in_specs=[_row_spec(tm), _row_spec(tm), _vec_spec(), _vec_spec(), _row_spec(tm)],
        out_specs=[_row_spec(tm), _row_spec(tm), _stats_spec()],
        out_shape=[jax.ShapeDtypeStruct((S, D_MODEL), F32), jax.ShapeDtypeStruct((S, D_MODEL), BF),
                   jax.ShapeDtypeStruct((8, D_MODEL), F32)],
        compiler_params=_cparams("arbitrary"),
    )(x2, y2, g4, gate_f, target)


def _mid_bwd(d_h2, x2, d_out, y1, g3, scale_f, g2, gate_m):
    S = x2.shape[0]
    tm = _tile(S, ROWS, 8)

    def kern(dh_ref, x2_ref, dout_ref, y_ref, g3_ref, sc_ref, g2_ref, gm_ref, dx2_ref, dy_ref, st_ref):
        @pl.when(pl.program_id(0) == 0)
        def _():
            st_ref[...] = jnp.zeros_like(st_ref)

        dh = dh_ref[...].astype(F32)
        x2 = x2_ref[...]
        r3 = _rms(x2)
        xn = x2 * r3
        one_sc = 1.0 + sc_ref[...]
        d_x2 = dout_ref[...] + _norm_bwd(dh * one_sc * g3_ref[...], xn, r3)
        dx2_ref[...] = d_x2
        y = y_ref[...].astype(F32)
        r2 = _rms(y)
        yn = y * r2
        dn = d_x2 * gm_ref[...]
        dy_ref[...] = _norm_bwd(dn * g2_ref[...], yn, r2).astype(BF)
        st_ref[0:1, :] += _colsum(dh)
        st_ref[1:2, :] += _colsum(dh * (xn * g3_ref[...]))
        st_ref[2:3, :] += _colsum(dh * one_sc * xn)
        st_ref[3:4, :] += _colsum(d_x2 * (yn * g2_ref[...]))
        st_ref[4:5, :] += _colsum(dn * yn)

    return pl.pallas_call(
        kern, name="mid_bwd", grid=(S // tm,),
        in_specs=[_row_spec(tm)] * 4 + [_vec_spec()] * 4,
        out_specs=[_row_spec(tm), _row_spec(tm), _stats_spec()],
        out_shape=[jax.ShapeDtypeStruct((S, D_MODEL), F32), jax.ShapeDtypeStruct((S, D_MODEL), BF),
                   jax.ShapeDtypeStruct((8, D_MODEL), F32)],
        compiler_params=_cparams("arbitrary"),
    )(d_h2, x2, d_out, y1, g3, scale_f, g2, gate_m)


def _pre_bwd(d_h1, x, d_x2, g1, scale_m, comm=None):
    S = x.shape[0]
    tm = _tile(S, ROWS, 8)
    grid = (S // tm,)

    def kern(*refs):
        (dh_ref, x_ref, dx2_ref, g_ref, sc_ref, gx_ref, st_ref), comm_refs = _own_refs(refs, comm, 5, 2, 0)
        _comm_edge(comm, comm_refs, grid, first=True)

        @pl.when(pl.program_id(0) == 0)
        def _():
            st_ref[...] = jnp.zeros_like(st_ref)

        dh = dh_ref[...].astype(F32)
        xf = x_ref[...]
        r = _rms(xf)
        xn = xf * r
        one_sc = 1.0 + sc_ref[...]
        gx_ref[...] = dx2_ref[...] + _norm_bwd(dh * one_sc * g_ref[...], xn, r)
        st_ref[0:1, :] += _colsum(dh)
        st_ref[1:2, :] += _colsum(dh * (xn * g_ref[...]))
        st_ref[2:3, :] += _colsum(dh * one_sc * xn)
        _comm_edge(comm, comm_refs, grid, first=False)

    res = pl.pallas_call(
        kern, name="pre_mix_bwd", grid=grid,
        in_specs=[_row_spec(tm)] * 3 + [_vec_spec()] * 2 + _comm_specs(comm, "in"),
        out_specs=[_row_spec(tm), _stats_spec()] + _comm_specs(comm, "out"),
        out_shape=[jax.ShapeDtypeStruct((S, D_MODEL), F32), jax.ShapeDtypeStruct((8, D_MODEL), F32)]
        + (comm.out_shapes if comm else []),
        scratch_shapes=comm.sem_shapes if comm else [],
        input_output_aliases={5 + i: 2 + o for i, o in comm.aliases.items()} if comm else {},
        compiler_params=_cparams("arbitrary"),
    )(d_h1, x, d_x2, g1, scale_m, *(comm.ins if comm else []))
    return res[0], res[1], res[2:]


def _rope(xs, widths, cos_t, sin_t, name):
    S = xs[0].shape[0]
    tm = _tile(S, 512, 8)
    n = len(xs)

    def kern(*refs):
        cos = refs[n][...]
        sin = refs[n + 1][...]
        first = (lax.broadcasted_iota(jnp.int32, cos.shape, 1) % HEAD_DIM) < HEAD_DIM // 2
        for x_ref, o_ref, w in zip(refs[:n], refs[n + 2:], widths):
            for c0 in range(0, w, LANES):
                v = x_ref[:, c0:c0 + LANES]
                partner = jnp.where(first, pltpu.roll(v, LANES - HEAD_DIM // 2, 1),
                                    pltpu.roll(v, HEAD_DIM // 2, 1))
                o_ref[:, c0:c0 + LANES] = (v * cos + partner * sin).astype(BF)

    return pl.pallas_call(
        kern, name=name, grid=(S // tm,),
        in_specs=[_row_spec(tm, w) for w in widths] + [_row_spec(tm, LANES)] * 2,
        out_specs=[_row_spec(tm, w) for w in widths],
        out_shape=[jax.ShapeDtypeStruct((S, w), BF) for w in widths],
        compiler_params=_cparams("parallel"),
    )(*xs, cos_t, sin_t)


def _merge_fwd(pg, pa, pb):
    S = pa.shape[0]
    tm = _tile(S, 2 * ROWS, 8)

    def kern(ga_ref, gb_ref, pa_ref, pb_ref, o_ref):
        ga = jax.nn.sigmoid(ga_ref[...].astype(F32))
        gb = jax.nn.sigmoid(gb_ref[...].astype(F32))
        o_ref[...] = (ga * pa_ref[...].astype(F32) + gb * pb_ref[...].astype(F32)).astype(BF)

    return pl.pallas_call(
        kern, name="merge_fwd", grid=(S // tm,),
        in_specs=[_row_spec(tm, col=0), _row_spec(tm, col=1), _row_spec(tm), _row_spec(tm)],
        out_specs=_row_spec(tm),
        out_shape=jax.ShapeDtypeStruct((S, D_MODEL), BF),
        compiler_params=_cparams("parallel"),
    )(pg, pg, pa, pb)


def _merge_bwd(d_merged, pg, pa, pb):
    S = pa.shape[0]
    tm = _tile(S, ROWS, 8)

    def kern(dm_ref, ga_ref, gb_ref, pa_ref, pb_ref, dpa_ref, dpb_ref, dga_ref, dgb_ref):
        dm = dm_ref[...].astype(F32)
        ga = jax.nn.sigmoid(ga_ref[...].astype(F32))
        gb = jax.nn.sigmoid(gb_ref[...].astype(F32))
        dpa_ref[...] = (dm * ga).astype(BF)
        dpb_ref[...] = (dm * gb).astype(BF)
        dga_ref[...] = (dm * pa_ref[...].astype(F32) * ga * (1.0 - ga)).astype(BF)
        dgb_ref[...] = (dm * pb_ref[...].astype(F32) * gb * (1.0 - gb)).astype(BF)

    bf_out = jax.ShapeDtypeStruct((S, D_MODEL), BF)
    return pl.pallas_call(
        kern, name="merge_bwd", grid=(S // tm,),
        in_specs=[_row_spec(tm), _row_spec(tm, col=0), _row_spec(tm, col=1), _row_spec(tm), _row_spec(tm)],
        out_specs=[_row_spec(tm)] * 4,
        out_shape=[bf_out] * 4,
        compiler_params=_cparams("parallel"),
    )(d_merged, pg, pg, pa, pb)


def _swiglu_fwd(gu):
    S = gu.shape[0]
    tm = _tile(S, 2 * ROWS, 8)
    tc = _tile(D_FF, 1408)
    nc = D_FF // tc

    def kern(g_ref, u_ref, o_ref):
        g = g_ref[...].astype(F32)
        o_ref[...] = (g * jax.nn.sigmoid(g) * u_ref[...].astype(F32)).astype(BF)

    return pl.pallas_call(
        kern, name="swiglu_fwd", grid=(S // tm, nc),
        in_specs=[pl.BlockSpec((tm, tc), lambda i, j: (i, j)),
                  pl.BlockSpec((tm, tc), lambda i, j: (i, j + nc))],
        out_specs=pl.BlockSpec((tm, tc), lambda i, j: (i, j)),
        out_shape=jax.ShapeDtypeStruct((S, D_FF), BF),
        compiler_params=_cparams("parallel", "parallel"),
    )(gu, gu)


def _swiglu_bwd(d_act, gu):
    S = gu.shape[0]
    tm = _tile(S, ROWS, 8)

    def kern(da_ref, g_ref, u_ref, o_ref):
        g = g_ref[...].astype(F32)
        u = u_ref[...].astype(F32)
        da = da_ref[...].astype(F32)
        sg = jax.nn.sigmoid(g)
        o_ref[:, :D_FF] = (da * u * (sg * (1.0 + g * (1.0 - sg)))).astype(BF)
        o_ref[:, D_FF:] = (da * (g * sg)).astype(BF)

    return pl.pallas_call(
        kern, name="swiglu_bwd", grid=(S // tm,),
        in_specs=[_row_spec(tm, D_FF), _row_spec(tm, D_FF, 0), _row_spec(tm, D_FF, 1)],
        out_specs=_row_spec(tm, 2 * D_FF),
        out_shape=jax.ShapeDtypeStruct((S, 2 * D_FF), BF),
        compiler_params=_cparams("parallel"),
    )(d_act, gu, gu)


def _split3(x):
    hi = x.astype(BF)
    r1 = x - hi.astype(F32)
    mid = r1.astype(BF)
    lo = (r1 - mid.astype(F32)).astype(BF)
    return hi, mid, lo


def _tri_dot(tri, x):
    return sum(jnp.dot(tri, part, preferred_element_type=F32) for part in _split3(x))


def _log_sigmoid(z):
    return jnp.minimum(z, 0.0) - jnp.log(1.0 + jnp.exp(-jnp.abs(z)))


def _fox_gate_fwd(pa, b_f_pad):
    S = pa.shape[0]
    T = _tile(S, 512, 8)
    f_col = OFF_F // LANES

    def kern(z_ref, b_ref, cum_ref, bq_ref, bk_ref, carry_ref):
        @pl.when(pl.program_id(0) == 0)
        def _():
            carry_ref[...] = jnp.zeros_like(carry_ref)

        log_f = _log_sigmoid(z_ref[...] + b_ref[...])
        row = lax.broadcasted_iota(jnp.int32, (T, T), 0)
        col = lax.broadcasted_iota(jnp.int32, (T, T), 1)
        tri = (col <= row).astype(BF)
        cum = _tri_dot(tri, log_f) + carry_ref[...]
        cum_ref[...] = cum
        carry_ref[...] = cum[T - 1:T, :]
        for p in range(B_HEADS // 2):
            c3 = _split3(_crossed((T, LANES), _head_column(cum, 2 * p), _head_column(cum, 2 * p + 1)))
            bq_ref[p] = _bias_block((T, LANES), c3, 0, 3, 6).astype(BF)
            bk_ref[p] = _bias_block((T, LANES), [-t.astype(F32) for t in c3], 3, 0, 3).astype(BF)

    blocks = pl.BlockSpec((B_HEADS // 2, T, LANES), lambda i: (0, i, 0))
    bias_shape = jax.ShapeDtypeStruct((B_HEADS // 2, S, LANES), BF)
    return pl.pallas_call(
        kern, name="fox_gate_fwd", grid=(S // T,),
        in_specs=[_row_spec(T, LANES, f_col), _vec_spec(LANES)],
        out_specs=[_row_spec(T, LANES), blocks, blocks],
        out_shape=[jax.ShapeDtypeStruct((S, LANES), F32), bias_shape, bias_shape],
        scratch_shapes=[pltpu.VMEM((1, LANES), F32)],
        compiler_params=_cparams("arbitrary"),
    )(pa, b_f_pad)


def _fox_gate_bwd(rowsum_ds, colsum_ds, pa, b_f_pad):
    S = pa.shape[0]
    T = _tile(S, 512, 8)
    nb = S // T
    f_col = OFF_F // LANES

    def kern(dr_ref, dc_ref, z_ref, b_ref, df_ref, dbf_ref, carry_ref):
        @pl.when(pl.program_id(0) == 0)
        def _():
            carry_ref[...] = jnp.zeros_like(carry_ref)
            dbf_ref[...] = jnp.zeros_like(dbf_ref)

        row = lax.broadcasted_iota(jnp.int32, (T, T), 0)
        col = lax.broadcasted_iota(jnp.int32, (T, T), 1)
        tri = (col >= row).astype(BF)
        rev = _tri_dot(tri, dr_ref[...] - dc_ref[...]) + carry_ref[...]
        carry_ref[...] = rev[0:1, :]
        z = z_ref[...] + b_ref[...]
        lane = lax.broadcasted_iota(jnp.int32, (T, LANES), 1)
        d_z = jnp.where(lane < B_HEADS, rev * jax.nn.sigmoid(-z), 0.0)
        df_ref[...] = d_z.astype(BF)
        dbf_ref[0:1, :] += _colsum(d_z)

    return pl.pallas_call(
        kern, name="fox_gate_bwd", grid=(nb,),
        in_specs=[pl.BlockSpec((T, LANES), lambda i: (nb - 1 - i, 0)),
                  pl.BlockSpec((T, LANES), lambda i: (nb - 1 - i, 0)),
                  pl.BlockSpec((T, LANES), lambda i: (nb - 1 - i, f_col)),
                  _vec_spec(LANES)],
        out_specs=[pl.BlockSpec((T, LANES), lambda i: (nb - 1 - i, 0)),
                   pl.BlockSpec((8, LANES), lambda i: (0, 0))],
        out_shape=[jax.ShapeDtypeStruct((S, LANES), BF), jax.ShapeDtypeStruct((8, LANES), F32)],
        scratch_shapes=[pltpu.VMEM((1, LANES), F32)],
        compiler_params=_cparams("arbitrary"),
    )(rowsum_ds, colsum_ds, pa, b_f_pad)


NEG_INF = float("-inf")
QK_SCALE = 1.0 / math.sqrt(HEAD_DIM)


def _half_mask(shape, half):
    lane = lax.broadcasted_iota(jnp.int32, shape, 1)
    return (lane < HEAD_DIM) if half == 0 else (lane >= HEAD_DIM)


def _bias_block(shape, terms, term_off, ones_lo, ones_hi):
    l64 = lax.broadcasted_iota(jnp.int32, shape, 1) & (HEAD_DIM - 1)
    out = jnp.where((l64 >= ones_lo) & (l64 < ones_hi), 1.0, 0.0)
    for t, term in enumerate(terms):
        out = jnp.where(l64 == term_off + t, term.astype(F32), out)
    return out


def _head_column(block, head):
    lane = lax.broadcasted_iota(jnp.int32, block.shape, 1)
    return jnp.sum(jnp.where(lane == head, block, 0.0), axis=1, keepdims=True)


def _crossed(shape, first, second):
    return jnp.where(_half_mask(shape, 0), second, first)


def _fox_prep_fwd(cum, T):
    S = cum.shape[0]
    shape = (T, LANES)

    def kern(c_ref, bq_ref, bk_ref):
        p_id = pl.program_id(0)
        cum_blk = c_ref[...]
        c3 = _split3(_crossed(shape, _head_column(cum_blk, 2 * p_id), _head_column(cum_blk, 2 * p_id + 1)))
        bq_ref[...] = _bias_block(shape, c3, 0, 3, 6).astype(BF)
        bk_ref[...] = _bias_block(shape, [-t.astype(F32) for t in c3], 3, 0, 3).astype(BF)

    out_spec = pl.BlockSpec((None, T, LANES), lambda p, i: (p, i, 0))
    out_shape = jax.ShapeDtypeStruct((B_HEADS // 2, S, LANES), BF)
    return pl.pallas_call(
        kern, name="fox_prep_fwd", grid=(B_HEADS // 2, S // T),
        in_specs=[pl.BlockSpec((T, LANES), lambda p, i: (i, 0))],
        out_specs=[out_spec, out_spec], out_shape=[out_shape, out_shape],
        compiler_params=_cparams("parallel", "parallel"),
    )(cum)


def _fox_fwd(p_b, bq, bk, T, comm=None):
    S = p_b.shape[0]
    nq = S // T
    n_pairs = B_HEADS // 2
    grid = (n_pairs, nq)

    def kern(*refs):
        (q_ref, k_ref, v_ref, bq_ref, bk_ref, o_ref, lse_ref), comm_refs = _own_refs(refs, comm, 5, 2, 0)
        _comm_edge(comm, comm_refs, grid, first=True)
        i = pl.program_id(1)
        rowcol = lax.broadcasted_iota(jnp.int32, (T, T), 0) - lax.broadcasted_iota(jnp.int32, (T, T), 1)
        hms = (_half_mask((T, LANES), 0), _half_mask((T, LANES), 1))
        q_scaled = (q_ref[...].astype(F32) * QK_SCALE).astype(BF)
        bq_blk = bq_ref[...]
        qs = [jnp.where(hms[h], q_scaled, bq_blk) for h in (0, 1)]

        def step(j, carry, masked):
            rows = pl.ds(pl.multiple_of(j * T, T), T)
            kj, bkj, vj = k_ref[rows, :], bk_ref[rows, :], v_ref[rows, :]
            new = []
            for half in (0, 1):
                m, l, acc = carry[half]
                s = lax.dot_general(qs[half], jnp.where(hms[half], kj, bkj), (((1,), (1,)), ((), ())),
                                    preferred_element_type=F32)
                if masked:
                    s = jnp.where(rowcol >= 0, s, NEG_INF)
                m_new = jnp.maximum(m, jnp.max(s, axis=1, keepdims=True))
                alpha = jnp.exp(m - m_new)
                p = jnp.exp(s - m_new)
                l_new = alpha * l + jnp.sum(p, axis=1, keepdims=True)
                acc_new = alpha * acc + jnp.dot(p.astype(BF), vj, preferred_element_type=F32)
                new.append((m_new, l_new, acc_new))
            return tuple(new)

        one = (jnp.full((T, 1), NEG_INF, F32), jnp.zeros((T, 1), F32), jnp.zeros((T, LANES), F32))
        carry = lax.fori_loop(0, i, functools.partial(step, masked=False), (one, one))
        (m0, l0, acc0), (m1, l1, acc1) = step(i, carry, True)
        hm0 = _half_mask((T, LANES), 0)
        o_ref[...] = jnp.where(hm0, acc0 / l0, acc1 / l1)
        lse_ref[...] = jnp.where(hm0, m0 + jnp.log(l0), m1 + jnp.log(l1))
        _comm_edge(comm, comm_refs, grid, first=False)

    out_spec = pl.BlockSpec((T, LANES), lambda p, i: (i, p))
    res = pl.pallas_call(
        kern, name="fox_fwd", grid=grid,
        in_specs=[pl.BlockSpec((T, LANES), lambda p, i: (i, OFF_QB // LANES + p)),
                  pl.BlockSpec((S, LANES), lambda p, i: (0, OFF_KB // LANES + p)),
                  pl.BlockSpec((S, LANES), lambda p, i: (0, OFF_VB // LANES + p)),
                  pl.BlockSpec((None, T, LANES), lambda p, i: (p, i, 0)),
                  pl.BlockSpec((None, S, LANES), lambda p, i: (p, 0, 0))] + _comm_specs(comm, "in"),
        out_specs=[out_spec, out_spec] + _comm_specs(comm, "out"),
        out_shape=[jax.ShapeDtypeStruct((S, n_pairs * LANES), F32)] * 2 + (comm.out_shapes if comm else []),
        scratch_shapes=comm.sem_shapes if comm else [],
        compiler_params=_cparams("arbitrary", "arbitrary"),
    )(p_b, p_b, p_b, bq, bk, *(comm.ins if comm else []))
    return res[0], res[1], res[2:]


def _fox_prep_bwd(cum, o, do, lse, T):
    S = o.shape[0]
    shape = (T, LANES)

    def kern(c_ref, o_ref, do_ref, lse_ref, bq_ref, bdo_ref):
        p_id = pl.program_id(0)
        cum_blk = c_ref[...]
        cq = _crossed(shape, _head_column(cum_blk, 2 * p_id), _head_column(cum_blk, 2 * p_id + 1))
        b3 = _split3(cq - pltpu.roll(lse_ref[...], HEAD_DIM, 1))
        bq_ref[...] = _bias_block(shape, b3, 0, 3, 6).astype(BF)
        dd = do_ref[...] * o_ref[...]
        delta = [jnp.sum(jnp.where(_half_mask(shape, h), dd, 0.0), axis=1, keepdims=True) for h in (0, 1)]
        d3 = _split3(-_crossed(shape, delta[0], delta[1]))
        bdo_ref[...] = _bias_block(shape, d3, 0, 0, 0).astype(BF)

    block = pl.BlockSpec((None, T, LANES), lambda p, i: (p, i, 0))
    tile = pl.BlockSpec((T, LANES), lambda p, i: (i, p))
    out_shape = jax.ShapeDtypeStruct((B_HEADS // 2, S, LANES), BF)
    return pl.pallas_call(
        kern, name="fox_prep_bwd", grid=(B_HEADS // 2, S // T),
        in_specs=[pl.BlockSpec((T, LANES), lambda p, i: (i, 0)), tile, tile, tile],
        out_specs=[block, block], out_shape=[out_shape, out_shape],
        compiler_params=_cparams("parallel", "parallel"),
    )(cum, o, do, lse)


def _fox_bwd(p_b, do, bq, bk, bdo, T, comm=None):
    S = p_b.shape[0]
    n_pairs = B_HEADS // 2
    nq = S // T
    grid = (n_pairs,)

    def kern(*refs):
        own, comm_refs = _own_refs(refs, comm, 7, 5, 0)
        q_ref, k_ref, v_ref, do_ref, bq_ref, bk_ref, bdo_ref, dq_ref, dk_ref, dv_ref, dck_ref, dcq_ref = own
        _comm_edge(comm, comm_refs, grid, first=True)
        p_id = pl.program_id(0)
        rowcol = lax.broadcasted_iota(jnp.int32, (T, T), 0) - lax.broadcasted_iota(jnp.int32, (T, T), 1)
        lane = lax.broadcasted_iota(jnp.int32, (T, LANES), 1)
        dk_ref[...] = jnp.zeros_like(dk_ref)
        dv_ref[...] = jnp.zeros_like(dv_ref)
        dck_ref[...] = jnp.zeros_like(dck_ref)

        @pl.when(p_id == 0)
        def _():
            dcq_ref[...] = jnp.zeros_like(dcq_ref)

        hms = (_half_mask((T, LANES), 0), _half_mask((T, LANES), 1))
        v_ones = _bias_block((T, LANES), [], 0, 0, 3).astype(BF)

        def outer(i, carry):
            qrows = pl.ds(pl.multiple_of(i * T, T), T)
            q_scaled = (q_ref[qrows, :].astype(F32) * QK_SCALE).astype(BF)
            do_b = do_ref[qrows, :].astype(BF)
            bq_i, bdo_i = bq_ref[qrows, :], bdo_ref[qrows, :]
            qa = [jnp.where(hms[h], q_scaled, bq_i) for h in (0, 1)]
            doa = [jnp.where(hms[h], do_b, bdo_i) for h in (0, 1)]
            q_own = [jnp.where(hms[h], q_scaled, 0) for h in (0, 1)]
            do_own = [jnp.where(hms[h], do_b, 0) for h in (0, 1)]

            def inner(j, carry_in, masked):
                krows = pl.ds(pl.multiple_of(j * T, T), T)
                kj, bkj, vj = k_ref[krows, :], bk_ref[krows, :], v_ref[krows, :]
                dv_add, dk_add, new = 0.0, 0.0, []
                for half in (0, 1):
                    dq, rs = carry_in[half]
                    ka = jnp.where(hms[half], kj, bkj)
                    s = lax.dot_general(qa[half], ka, (((1,), (1,)), ((), ())), preferred_element_type=F32)
                    if masked:
                        s = jnp.where(rowcol >= 0, s, NEG_INF)
                    p = jnp.exp(s)
                    ds = p * lax.dot_general(doa[half], jnp.where(hms[half], vj, v_ones),
                                             (((1,), (1,)), ((), ())), preferred_element_type=F32)
                    ds_b = ds.astype(BF)
                    dv_add = dv_add + lax.dot_general(p.astype(BF), do_own[half], (((0,), (0,)), ((), ())),
                                                      preferred_element_type=F32)
                    dk_add = dk_add + lax.dot_general(ds_b, q_own[half], (((0,), (0,)), ((), ())),
                                                      preferred_element_type=F32)
                    dck_ref[half:half + 1, krows] += jnp.sum(ds, axis=0, keepdims=True)
                    new.append((dq + jnp.dot(ds_b, jnp.where(hms[half], kj, 0), preferred_element_type=F32),
                                rs + jnp.sum(ds, axis=1, keepdims=True)))
                dv_ref[krows, :] += dv_add
                dk_ref[krows, :] += dk_add
                return tuple(new)

            one = (jnp.zeros((T, LANES), F32), jnp.zeros((T, 1), F32))
            carry_in = lax.fori_loop(0, i, functools.partial(inner, masked=False), (one, one))
            (dq0, rs0), (dq1, rs1) = inner(i, carry_in, True)
            dq_ref[qrows, :] = (dq0 + dq1) * QK_SCALE
            dcq_ref[qrows, :] = jnp.where(lane == 2 * p_id, rs0, jnp.where(lane == 2 * p_id + 1, rs1,
                                                                             dcq_ref[qrows, :]))
            return carry

        lax.fori_loop(0, nq, outer, 0)
        _comm_edge(comm, comm_refs, grid, first=False)

    block = pl.BlockSpec((None, S, LANES), lambda p: (p, 0, 0))
    pair = pl.BlockSpec((S, LANES), lambda p: (0, p))
    slab = lambda off: pl.BlockSpec((S, LANES), lambda p: (0, off // LANES + p))
    wide = jax.ShapeDtypeStruct((S, n_pairs * LANES), F32)
    res = pl.pallas_call(
        kern, name="fox_bwd", grid=grid,
        in_specs=[slab(OFF_QB), slab(OFF_KB), slab(OFF_VB), pair, block, block, block]
        + _comm_specs(comm, "in"),
        out_specs=[pair, pair, pair, pl.BlockSpec((None, 2, S), lambda p: (p, 0, 0)),
                   pl.BlockSpec((S, LANES), lambda p: (0, 0))] + _comm_specs(comm, "out"),
        out_shape=[wide, wide, wide, jax.ShapeDtypeStruct((n_pairs, 2, S), F32),
                   jax.ShapeDtypeStruct((S, LANES), F32)] + (comm.out_shapes if comm else []),
        scratch_shapes=comm.sem_shapes if comm else [],
        compiler_params=_cparams("arbitrary"),
    )(p_b, p_b, p_b, do, bq, bk, bdo, *(comm.ins if comm else []))
    return (*res[:5], res[5:])


SWA_TQ = 128
SWA_SUB = 32


def _swa_window(i, tq):
    start = pl.multiple_of(jnp.maximum(i * tq - WINDOW, 0), LANES)
    return start, i * tq - start


def _swa_valid(offset, tq):
    rel = offset + lax.broadcasted_iota(jnp.int32, (tq, tq + WINDOW), 0) \
        - lax.broadcasted_iota(jnp.int32, (tq, tq + WINDOW), 1)
    return (rel >= 0) & (rel < WINDOW)


def _swa_fwd(qk, v_arr, v_col, sinks):
    S = qk.shape[0]
    tq = min(SWA_TQ, S - WINDOW)
    sub = min(SWA_SUB, S // tq)
    win = tq + WINDOW

    def kern(q_ref, k_ref, v_ref, sink_ref, o_ref, lse_ref):
        p_id, i = pl.program_id(0), pl.program_id(1)
        hm0 = _half_mask((tq, LANES), 0)
        for t in range(sub):
            rows = slice(t * tq, (t + 1) * tq)
            start, offset = _swa_window(i * sub + t, tq)
            kw = k_ref[pl.ds(start, win), :]
            vw = v_ref[pl.ds(start, win), :].astype(BF)
            valid = _swa_valid(offset, tq)
            q = q_ref[rows, :]
            outs, lses = [], []
            for half in (0, 1):
                hm = _half_mask((tq, LANES), half)
                qh = (jnp.where(hm, q, 0).astype(F32) * QK_SCALE).astype(BF)
                s = lax.dot_general(qh, kw, (((1,), (1,)), ((), ())), preferred_element_type=F32)
                s = jnp.where(valid, s, NEG_INF)
                sink = sink_ref[2 * p_id + half]
                m = jnp.maximum(jnp.max(s, axis=1, keepdims=True), sink)
                p = jnp.exp(s - m)
                denom = jnp.sum(p, axis=1, keepdims=True) + jnp.exp(sink - m)
                outs.append(jnp.dot(p.astype(BF), vw, preferred_element_type=F32) / denom)
                lses.append(m + jnp.log(denom))
            o_ref[rows, :] = jnp.where(hm0, outs[0], outs[1])
            lse_ref[rows, :] = jnp.where(hm0, lses[0], lses[1])

    tile = pl.BlockSpec((sub * tq, LANES), lambda p, i: (i, p))
    return pl.pallas_call(
        kern, name="swa_fwd", grid=(A_Q_HEADS // 2, S // (sub * tq)),
        in_specs=[tile, pl.BlockSpec((S, LANES), lambda p, i: (0, A_Q_HEADS // 2)),
                  pl.BlockSpec((S, LANES), lambda p, i: (0, v_col)),
                  pl.BlockSpec(memory_space=pltpu.SMEM)],
        out_specs=[tile, tile],
        out_shape=[jax.ShapeDtypeStruct((S, A_Q_HEADS * HEAD_DIM), F32)] * 2,
        compiler_params=_cparams("parallel", "arbitrary"),
    )(qk, qk, v_arr, sinks)


def _swa_bwd(qk, v_arr, v_col, o_arr, do_arr, lse_arr, sinks, comm=None):
    S = qk.shape[0]
    tq = min(SWA_TQ, S - WINDOW)
    sub = min(SWA_SUB, S // tq)
    win = tq + WINDOW
    n_pairs = A_Q_HEADS // 2
    grid = (n_pairs, S // (sub * tq))

    def kern(*refs):
        own, comm_refs = _own_refs(refs, comm, 7, 4, 0)
        q_ref, k_ref, v_ref, o_ref, do_ref, lse_ref, sink_ref, dq_ref, dk_ref, dv_ref, dsink_ref = own
        _comm_edge(comm, comm_refs, grid, first=True)
        p_id, i = pl.program_id(0), pl.program_id(1)

        @pl.when((p_id == 0) & (i == 0))
        def _():
            dk_ref[...] = jnp.zeros_like(dk_ref)
            dv_ref[...] = jnp.zeros_like(dv_ref)

        @pl.when(i == 0)
        def _():
            dsink_ref[...] = jnp.zeros_like(dsink_ref)

        for t in range(sub):
            rows = slice(t * tq, (t + 1) * tq)
            start, offset = _swa_window(i * sub + t, tq)
            wrows = pl.ds(start, win)
            kw = k_ref[wrows, :]
            vw = v_ref[wrows, :].astype(BF)
            valid = _swa_valid(offset, tq)
            q, do, o, lse2 = q_ref[rows, :], do_ref[rows, :], o_ref[rows, :], lse_ref[rows, :]
            dq = jnp.zeros((tq, LANES), F32)
            dk = jnp.zeros((win, LANES), F32)
            dv = jnp.zeros((win, LANES), F32)
            for half in (0, 1):
                hm = _half_mask((tq, LANES), half)
                lane0 = half * HEAD_DIM
                qh = (jnp.where(hm, q, 0).astype(F32) * QK_SCALE).astype(BF)
                do_f = jnp.where(hm, do, 0.0)
                doh = do_f.astype(BF)
                delta = jnp.sum(do_f * o, axis=1, keepdims=True)
                lse = lse2[:, lane0:lane0 + 1]
                s = lax.dot_general(qh, kw, (((1,), (1,)), ((), ())), preferred_element_type=F32)
                p = jnp.exp(jnp.where(valid, s, NEG_INF) - lse)
                dp = lax.dot_general(doh, vw, (((1,), (1,)), ((), ())), preferred_element_type=F32)
                ds_b = (p * (dp - delta)).astype(BF)
                dv = dv + lax.dot_general(p.astype(BF), doh, (((0,), (0,)), ((), ())),
                                          preferred_element_type=F32)
                dk = dk + lax.dot_general(ds_b, qh, (((0,), (0,)), ((), ())), preferred_element_type=F32)
                kh = jnp.where(_half_mask((win, LANES), half), kw, 0)
                dq = dq + jnp.dot(ds_b, kh, preferred_element_type=F32)
                p_sink = jnp.exp(sink_ref[2 * p_id + half] - lse)
                dsink_ref[0, half:half + 1, :] += jnp.broadcast_to(
                    -jnp.sum(p_sink * delta, axis=0, keepdims=True), (1, LANES))
            dq_ref[rows, :] = dq * QK_SCALE
            dk_ref[wrows, :] += dk
            dv_ref[wrows, :] += dv
        _comm_edge(comm, comm_refs, grid, first=False)

    tile = pl.BlockSpec((sub * tq, LANES), lambda p, i: (i, p))
    whole = lambda col: pl.BlockSpec((S, LANES), lambda p, i: (0, col))
    res = pl.pallas_call(
        kern, name="swa_bwd", grid=grid,
        in_specs=[tile, whole(n_pairs), whole(v_col), tile, tile, tile,
                  pl.BlockSpec(memory_space=pltpu.SMEM)] + _comm_specs(comm, "in"),
        out_specs=[tile, whole(0), whole(0),
                   pl.BlockSpec((1, 8, LANES), lambda p, i: (p, 0, 0))] + _comm_specs(comm, "out"),
        out_shape=[jax.ShapeDtypeStruct((S, A_Q_HEADS * HEAD_DIM), F32),
                   jax.ShapeDtypeStruct((S, LANES), F32), jax.ShapeDtypeStruct((S, LANES), F32),
                   jax.ShapeDtypeStruct((n_pairs, 8, LANES), F32)] + (comm.out_shapes if comm else []),
        scratch_shapes=comm.sem_shapes if comm else [],
        compiler_params=_cparams("arbitrary", "arbitrary"),
    )(qk, qk, v_arr, o_arr, do_arr, lse_arr, sinks, *(comm.ins if comm else []))
    return (*res[:4], res[4:])


ADAMW_BLOCK = 512 * 1024


def _adamw(w, g, m, v, name, comm=None):
    R, C = w.shape
    tr, tc = _tile(R, max(8, ADAMW_BLOCK // C), 8), C
    grid = (R // tr, C // tc)

    def kern(*refs):
        (w_ref, g_ref, m_ref, v_ref, d_ref, mo_ref, vo_ref), comm_refs = _own_refs(refs, comm, 4, 3, 0)
        _comm_edge(comm, comm_refs, grid, first=True)
        g_ = g_ref[...]
        m_new = ADAM_B1 * m_ref[...] + (1.0 - ADAM_B1) * g_
        v_new = ADAM_B2 * v_ref[...] + (1.0 - ADAM_B2) * (g_ * g_)
        m_hat = m_new / (1.0 - ADAM_B1 ** ADAM_STEP)
        v_hat = v_new / (1.0 - ADAM_B2 ** ADAM_STEP)
        d_ref[...] = -ADAM_LR * (m_hat / (jnp.sqrt(v_hat) + ADAM_EPS) + ADAM_WD * w_ref[...])
        mo_ref[...] = m_new
        vo_ref[...] = v_new
        _comm_edge(comm, comm_refs, grid, first=False)

    spec = pl.BlockSpec((tr, tc), lambda i, j: (i, j))
    shape = jax.ShapeDtypeStruct((R, C), F32)
    res = pl.pallas_call(
        kern, name=name, grid=grid,
        in_specs=[spec] * 4 + _comm_specs(comm, "in"),
        out_specs=[spec] * 3 + _comm_specs(comm, "out"),
        out_shape=[shape] * 3 + (comm.out_shapes if comm else []),
        scratch_shapes=comm.sem_shapes if comm else [],
        input_output_aliases={4 + i: 3 + o for i, o in comm.aliases.items()} if comm else {},
        compiler_params=_cparams("arbitrary", "arbitrary"),
    )(w, g, m, v, *(comm.ins if comm else []))
    return (res[:3], res[3:]) if comm else res


def _index_operand(i):
    return jnp.reshape(i, (1,)).astype(jnp.int32)


def _add_pair(whole, got, ci, name):
    P, R, C = whole.shape
    half = R // 2
    tr = _tile(half, ROWS, 16)
    nb = half // tr

    def kern(ci_ref, a_ref, b_ref, o_ref, ob_ref):
        s = a_ref[...] + b_ref[...].astype(F32)
        o_ref[...] = s
        ob_ref[...] = s.astype(BF)

    spec = pl.BlockSpec((None, tr, C), lambda p, i, ci_ref: (p, i, 0))
    return pl.pallas_call(
        kern, name=name,
        grid_spec=pltpu.PrefetchScalarGridSpec(
            num_scalar_prefetch=1, grid=(P, nb),
            in_specs=[pl.BlockSpec((None, tr, C), lambda p, i, ci_ref: (p, ci_ref[0] * nb + i, 0)), spec],
            out_specs=[spec, spec]),
        out_shape=[jax.ShapeDtypeStruct((P, half, C), F32), jax.ShapeDtypeStruct((P, half, C), BF)],
        compiler_params=_cparams("parallel", "parallel"),
    )(_index_operand(ci), whole, got)


def _add_three(parts, recv, chip, name):
    _, R, C = parts.shape
    tr = _tile(R, ROWS, 16)

    def kern(chip_ref, o_ref, r0_ref, r1_ref, r2_ref, out_ref):
        s = ((o_ref[...] + r0_ref[...].astype(F32)) + r1_ref[...].astype(F32)) + r2_ref[...].astype(F32)
        out_ref[0] = s
        out_ref[1] = s

    slab = lambda k: pl.BlockSpec((None, tr, C), lambda i, chip_ref: (k, i, 0))
    return pl.pallas_call(
        kern, name=name,
        grid_spec=pltpu.PrefetchScalarGridSpec(
            num_scalar_prefetch=1, grid=(R // tr,),
            in_specs=[pl.BlockSpec((None, tr, C), lambda i, chip_ref: (chip_ref[0], i, 0)),
                      slab(0), slab(1), slab(2)],
            out_specs=pl.BlockSpec((2, tr, C), lambda i, chip_ref: (0, i, 0))),
        out_shape=jax.ShapeDtypeStruct((2, R, C), F32),
        compiler_params=_cparams("parallel"),
    )(_index_operand(chip), parts, recv, recv, recv)


SM_ADA, SM_G, SM_LOSS, SM_BF, SM_SINK, SM_LEN = 0, 6144, 10240, 11264, 11272, 12288


def _small_finalize(gathered):
    def kern(g_ref, tot_ref, loss_ref):
        tot = g_ref[0:1, :]
        for b in range(1, N_DEV):
            tot = tot + g_ref[b:b + 1, :]
        tot_ref[...] = tot
        sq = jnp.sum(tot[:, SM_LOSS:SM_LOSS + D_MODEL], axis=1, keepdims=True)
        loss_ref[...] = jnp.broadcast_to(sq * (0.5 / D_MODEL), (1, LANES))

    full = lambda shape: pl.BlockSpec(shape, lambda i: (0, 0))
    return pl.pallas_call(
        kern, name="small_finalize", grid=(1,),
        in_specs=[full((N_DEV, SM_LEN))],
        out_specs=[full((1, SM_LEN)), full((1, LANES))],
        out_shape=[jax.ShapeDtypeStruct((1, SM_LEN), F32), jax.ShapeDtypeStruct((1, LANES), F32)],
        compiler_params=_cparams("arbitrary"),
    )(gathered)


def _ada_dw(c_t, d_ada):
    N = d_ada.shape[1]
    tn = _tile(N, 512)

    def kern(c_ref, d_ref, o_ref):
        acc = c_ref[:, 0:1] * d_ref[0:1, :]
        for b in range(1, N_DEV):
            acc = acc + c_ref[:, b:b + 1] * d_ref[b:b + 1, :]
        o_ref[...] = acc

    return pl.pallas_call(
        kern, name="ada_dw", grid=(N // tn,),
        in_specs=[pl.BlockSpec((D_MODEL, N_DEV), lambda j: (0, 0)), pl.BlockSpec((N_DEV, tn), lambda j: (0, j))],
        out_specs=pl.BlockSpec((D_MODEL, tn), lambda j: (0, j)),
        out_shape=jax.ShapeDtypeStruct((D_MODEL, N), F32),
        compiler_params=_cparams("parallel"),
    )(c_t, d_ada)


def _here():
    return lax.axis_index("x"), lax.axis_index("y"), lax.axis_index("c")


def _other_chips(x, y):
    return [(1 - x, y), (x, 1 - y), (1 - x, 1 - y)]


_ANY = pl.BlockSpec(memory_space=pl.ANY)


class _Comm:
    def __init__(self, ins, out_shapes, sem_shapes, start, finish, aliases=None):
        self.ins, self.out_shapes, self.sem_shapes = list(ins), list(out_shapes), list(sem_shapes)
        self.start, self.finish = start, finish
        self.aliases = dict(aliases or {})

    def split(self, refs, n_in, n_out, n_scratch):
        a = n_in + len(self.ins)
        b = a + n_out + len(self.out_shapes)
        own = list(refs[:n_in]) + list(refs[a:a + n_out]) + list(refs[b:b + n_scratch])
        mine = (refs[n_in:a], refs[a + n_out:b], refs[b + n_scratch:])
        return own, mine


def _run_comm(comm, name):
    n_in, n_out = len(comm.ins), len(comm.out_shapes)

    def body(*refs):
        parts = (refs[:n_in], refs[n_in:n_in + n_out], refs[n_in + n_out:])
        comm.start(*parts)
        comm.finish(*parts)

    return pl.pallas_call(
        body, name=name,
        in_specs=[_ANY] * n_in, out_specs=[_ANY] * n_out,
        out_shape=comm.out_shapes, scratch_shapes=comm.sem_shapes,
        input_output_aliases=comm.aliases,
    )(*comm.ins)


def _gather_comm(blocks):
    L = len(blocks)

    def parts(ins, outs, sems):
        send_sems, recv_sems, local_sems = sems
        x, y, c = _here()
        me, sibling = (x, y, c), (x, y, 1 - c)
        chips = _other_chips(x, y)

        def slot(px, py, pc):
            return 4 * px + 2 * py + pc

        def copy(l, k, block, to, src=None):
            dst = outs[l].at[slot(*block)]
            return pltpu.make_async_remote_copy(
                src_ref=dst if src is None else src, dst_ref=dst,
                send_sem=send_sems.at[l, k], recv_sem=recv_sems.at[l, k],
                device_id=to, device_id_type=MESH)

        mine = [pltpu.make_async_copy(ins[l], outs[l].at[slot(*me)], local_sems.at[l]) for l in range(L)]
        first = []
        for l in range(L):
            first.append(copy(l, 0, me, sibling, src=ins[l]))
            for j, chip in enumerate(chips):
                first.append(copy(l, 1 + j, me, (*chip, c), src=ins[l]))
        return c, me, sibling, chips, copy, mine, first

    def start(ins, outs, sems):
        *_, mine, first = parts(ins, outs, sems)
        for cp in mine + first:
            cp.start()

    def finish(ins, outs, sems):
        c, me, sibling, chips, copy, mine, first = parts(ins, outs, sems)
        passed = []
        for j, chip in enumerate(chips):
            for l in range(L):
                copy(l, 1 + j, (*chip, c), me).wait_recv()
                fwd = copy(l, 4 + j, (*chip, c), sibling)
                fwd.start()
                passed.append(fwd)
        for l in range(L):
            copy(l, 0, sibling, me).wait_recv()
        for j, chip in enumerate(chips):
            for l in range(L):
                copy(l, 4 + j, (*chip, 1 - c), me).wait_recv()
        for cp in first + passed:
            cp.wait_send()
        for cp in mine:
            cp.wait()

    return _Comm(blocks, [jax.ShapeDtypeStruct((N_DEV,) + b.shape, b.dtype) for b in blocks],
                 [pltpu.SemaphoreType.DMA((L, 7)), pltpu.SemaphoreType.DMA((L, 7)), pltpu.SemaphoreType.DMA((L,))],
                 start, finish)


def _allgather8(blocks, name):
    return _run_comm(_gather_comm(blocks), name)


def _swap_comm(arrs):
    L = len(arrs)

    def copies(ins, outs, sems):
        send_sems, recv_sems = sems
        x, y, c = _here()
        cps = []
        for l in range(L):
            half = arrs[l].shape[1] // 2
            rows = pl.ds(pl.multiple_of((1 - c) * half, 16), half)
            cps.append(pltpu.make_async_remote_copy(
                src_ref=ins[l].at[:, rows, :], dst_ref=outs[l], send_sem=send_sems.at[l],
                recv_sem=recv_sems.at[l], device_id=(x, y, 1 - c), device_id_type=MESH))
        return cps

    def start(ins, outs, sems):
        for cp in copies(ins, outs, sems):
            cp.start()

    def finish(ins, outs, sems):
        for cp in copies(ins, outs, sems):
            cp.wait()

    return _Comm(arrs, [jax.ShapeDtypeStruct((a.shape[0], a.shape[1] // 2, a.shape[2]), a.dtype) for a in arrs],
                 [pltpu.SemaphoreType.DMA((L,)), pltpu.SemaphoreType.DMA((L,))], start, finish)


def _join_comm(bufs):
    L = len(bufs)

    def start(ins, outs, sems):
        send_sems, recv_sems = sems
        x, y, c = _here()
        for l in range(L):
            pltpu.make_async_remote_copy(src_ref=outs[l].at[c], dst_ref=outs[l].at[c], send_sem=send_sems.at[l],
                                         recv_sem=recv_sems.at[l], device_id=(x, y, 1 - c),
                                         device_id_type=MESH).start()

    def finish(ins, outs, sems):
        send_sems, recv_sems = sems
        x, y, c = _here()
        for l in range(L):
            pltpu.make_async_remote_copy(src_ref=outs[l].at[c], dst_ref=outs[l].at[1 - c],
                                         send_sem=send_sems.at[l], recv_sem=recv_sems.at[l],
                                         device_id=(x, y, 1 - c), device_id_type=MESH).wait()

    return _Comm(bufs, [jax.ShapeDtypeStruct(a.shape, a.dtype) for a in bufs],
                 [pltpu.SemaphoreType.DMA((L,)), pltpu.SemaphoreType.DMA((L,))], start, finish,
                 aliases={l: l for l in range(L)})


def _scatter_comm(arrs):
    L = len(arrs)

    def copies(ins, outs, sems):
        send_sems, recv_sems = sems
        x, y, c = _here()
        return [pltpu.make_async_remote_copy(
            src_ref=ins[l].at[2 * tx + ty], dst_ref=outs[l].at[j],
            send_sem=send_sems.at[l, j], recv_sem=recv_sems.at[l, j],
            device_id=(tx, ty, c), device_id_type=MESH)
            for l in range(L) for j, (tx, ty) in enumerate(_other_chips(x, y))]

    def start(ins, outs, sems):
        for cp in copies(ins, outs, sems):
            cp.start()

    def finish(ins, outs, sems):
        for cp in copies(ins, outs, sems):
            cp.wait()

    return _Comm(arrs, [jax.ShapeDtypeStruct((3,) + a.shape[1:], a.dtype) for a in arrs],
                 [pltpu.SemaphoreType.DMA((L, 3)), pltpu.SemaphoreType.DMA((L, 3))], start, finish)


_A_ORDER = np.array(A_HEAD_ORDER)
_A_INVERSE = np.argsort(_A_ORDER)


def _permute_in_weights(w_in):
    qa = w_in[:, 0:512].reshape(D_MODEL, A_Q_HEADS, HEAD_DIM)[:, _A_ORDER, :].reshape(D_MODEL, 512)
    f_pad = jnp.pad(w_in[:, 2304:2312], ((0, 0), (0, LANES - B_HEADS)))
    w_a = jnp.concatenate([qa, w_in[:, 512:640], f_pad], axis=1)
    return w_a, w_in[:, 640:2304], w_in[:, 2312:4360]


def _slab_segments():
    segs = [(h * HEAD_DIM, int(_A_INVERSE[h]) * HEAD_DIM, HEAD_DIM) for h in range(A_Q_HEADS)]
    segs += [(512, OFF_KA, 128), (640, W_A + OFF_VA, 128), (768, W_A + OFF_QB, 1536),
             (2304, OFF_F, B_HEADS), (2312, W_A + W_B, W_G)]
    return segs


def _shard_slabs(dw_perm):
    R = dw_perm.shape[0]
    tr = _tile(R, 128, 8)
    plan = []
    for k in range(N_CHIP):
        for b in range(W_SHARD_PAD // LANES):
            lo, hi = k * W_SHARD + b * LANES, min(k * W_SHARD + (b + 1) * LANES, (k + 1) * W_SHARD)
            parts = []
            for o0, s0, n in _slab_segments():
                a, z = max(lo, o0), min(hi, o0 + n)
                while a < z:
                    s = s0 + (a - o0)
                    run = min(z - a, LANES - s % LANES)
                    parts.append((s // LANES, ((a - lo) - s % LANES) % LANES, a - lo, run))
                    a += run
            plan.append((k, b, parts))

    def kern(x_ref, o32_ref, obf_ref):
        lane = lax.broadcasted_iota(jnp.int32, (tr, LANES), 1)
        for k, b, parts in plan:
            acc = jnp.zeros((tr, LANES), F32)
            for src, rot, first, run in parts:
                blk = x_ref[:, src * LANES:(src + 1) * LANES]
                if rot:
                    blk = pltpu.roll(blk, rot, 1)
                acc = jnp.where((lane >= first) & (lane < first + run), blk, acc)
            o32_ref[k, :, b * LANES:(b + 1) * LANES] = acc
            obf_ref[k, :, b * LANES:(b + 1) * LANES] = acc.astype(BF)

    out_spec = pl.BlockSpec((N_CHIP, tr, W_SHARD_PAD), lambda i: (0, i, 0))
    return tuple(pl.pallas_call(
        kern, name="shard_slabs", grid=(R // tr,),
        in_specs=[pl.BlockSpec((tr, W_PERM), lambda i: (i, 0))],
        out_specs=[out_spec, out_spec],
        out_shape=[jax.ShapeDtypeStruct((N_CHIP, R, W_SHARD_PAD), F32),
                   jax.ShapeDtypeStruct((N_CHIP, R, W_SHARD_PAD), BF)],
        compiler_params=_cparams("parallel"),
    )(dw_perm))


class _NoExchange:
    def __init__(self, w_in, rest):
        self.w_in_whole, self.rest, self.grads = w_in, rest, {}

    def w_in_comm(self):
        return None

    def w_in(self, outs):
        return self.w_in_whole

    def rest_weights_comm(self):
        return None

    def rest_weights(self, outs):
        return self.rest

    def swap_comm(self, pieces, tag):
        self.grads[tag] = [p32 for p32, _ in pieces]
        return None

    def swap_done(self, outs, tag):
        return None

    def reduce_done(self, outs, tag):
        pass

    def join_comm(self):
        return None


class _Exchange:
    def __init__(self, ci, chip, w_in_shard, rest_shards):
        self.ci, self.chip, self.w_in_shard, self.rest_shards = ci, chip, w_in_shard, rest_shards
        self.pieces, self.part_f32, self.halves = {}, {}, {}

    def _my_half(self, a, axis=0, other=False):
        rows = a.shape[axis] // 2
        return lax.dynamic_slice_in_dim(a, ((1 - self.ci) if other else self.ci) * rows, rows, axis=axis)

    def w_in_comm(self):
        return _gather_comm([self._my_half(self.w_in_shard).astype(BF)])

    def w_in(self, outs):
        return _col_sharded(outs[0])

    def rest_weights_comm(self):
        return _gather_comm([self._my_half(w).astype(BF) for w in self.rest_shards])

    def rest_weights(self, outs):
        w_ba, w_bb, w_out, w_fi, w_fo = outs
        return (_col_sharded(w_ba), _col_sharded(w_bb), _row_sharded(w_out), _col_sharded(w_fi),
                _row_sharded(w_fo))

    def swap_comm(self, pieces, tag):
        self.pieces[tag] = pieces
        return _swap_comm([pbf for _, pbf in pieces])

    def swap_done(self, got, tag):
        self.part_f32[tag], part_bf = [], []
        for l, ((p32, _), g_) in enumerate(zip(self.pieces[tag], got)):
            s32, sbf = _add_pair(p32, g_, self.ci, f"chip_sum_{tag}_{l}")
            self.part_f32[tag].append(s32)
            part_bf.append(sbf)
        return _scatter_comm(part_bf)

    def reduce_done(self, outs, tag):
        self.halves[tag] = [_add_three(p32, r, self.chip, f"shard_sum_{tag}_{l}")
                            for l, (p32, r) in enumerate(zip(self.part_f32[tag], outs))]

    def join_comm(self):
        return _join_comm(self.halves["late"] + self.halves["early"])


def _col_sharded(g):
    return jnp.transpose(g.reshape(N_CHIP, -1, g.shape[-1]), (1, 0, 2)).reshape(2 * g.shape[1], N_CHIP * g.shape[-1])


def _row_sharded(g):
    return g.reshape(N_DEV * g.shape[1], g.shape[-1])


def _rope_tables(pos):
    inv_freq = 1.0 / (ROPE_THETA ** (jnp.arange(0, HEAD_DIM, 2, dtype=F32) / HEAD_DIM))
    ang = pos.astype(F32)[:, None] * inv_freq
    cos, sin = jnp.cos(ang), jnp.sin(ang)
    return jnp.tile(cos, (1, 4)), jnp.tile(jnp.concatenate([-sin, sin], axis=1), (1, 2))


def _local_step(x, pos, ada, g1, g2, g3, g4, b_f, sinks, exch, target):
    S = x.shape[0]
    t_fox = _tile(S, 512, LANES) if S >= 1024 else S // 2
    t_fox_fwd = _tile(S, 1024, LANES) if S >= 2048 else S // 2
    shift_m, scale_m, gate_m, shift_f, scale_f, gate_f = [ada[i:i + 1] for i in range(N_ADA)]
    cos_t, sin_t = _rope_tables(pos)
    sinks_p = sinks.reshape(A_KV_HEADS, 4).T.reshape(A_Q_HEADS)
    b_f_pad = jnp.pad(b_f, (0, LANES - B_HEADS)).reshape(1, LANES)

    h1, outs = _pre_norm(x, g1, scale_m, shift_m, "pre_mix_norm", comm=exch.w_in_comm())
    w_a, w_b, w_g = _permute_in_weights(exch.w_in(outs))
    w_perm = jnp.concatenate([w_a, w_b, w_g], axis=1)
    p_a = _mm(h1, w_a, "nn", F32, "proj_a")
    p_b = _mm(h1, w_b, "nn", BF, "proj_b")
    p_g = _mm(h1, w_g, "nn", BF, "proj_g")
    (qk_a,) = _rope([p_a], [640], cos_t, sin_t, "rope_fwd")
    o_a, lse_a = _swa_fwd(qk_a, p_b, 0, sinks_p)
    cum, bq, bk = _fox_gate_fwd(p_a, b_f_pad)
    comm = exch.rest_weights_comm()
    o_b, lse_b, outs = _fox_fwd(p_b, bq, bk, t_fox_fwd, comm=comm)
    w_ba, w_bb, w_out, w_fi, w_fo = exch.rest_weights(outs)
    w_ba_p = w_ba.reshape(A_Q_HEADS, HEAD_DIM, D_MODEL)[_A_ORDER].reshape(512, D_MODEL)
    pa = _mm(o_a, w_ba_p, "nn", BF, "branch_a")
    pb = _mm(o_b, w_bb, "nn", BF, "branch_b")
    merged = _merge_fwd(p_g, pa, pb)
    y1 = _mm(merged, w_out, "nn", BF, "out_proj")
    x2, h2 = _post_pre(x, y1, g2, gate_m, g3, scale_f, shift_f)
    gu = _mm(h2, w_fi, "nn", BF, "ffn_in")
    act = _swiglu_fwd(gu)
    y2 = _mm(act, w_fo, "nn", BF, "ffn_out")
    d_out, d_y2, st_f = _final(x2, y2, g4, gate_f, target)

    d_act = _mm(d_y2, w_fo, "nt", BF, "ffn_out_dx")
    row_pieces = lambda pair: tuple(t.reshape(N_CHIP, t.shape[0] // N_CHIP, t.shape[1]) for t in pair)
    dw_fo = row_pieces(_mm(act, d_y2, "tn", F32, "ffn_out_dw", twin=True))
    d_gu = _swiglu_bwd(d_act, gu)
    d_h2 = _mm(d_gu, w_fi, "nt", BF, "ffn_in_dx")
    dw_fi = _mm(h2, d_gu, "tn", F32, "ffn_in_dw", col_pieces=N_CHIP, twin=True)
    d_x2, d_y1, st_m = _mid_bwd(d_h2, x2, d_out, y1, g3, scale_f, g2, gate_m)
    d_merged = _mm(d_y1, w_out, "nt", BF, "out_proj_dx")
    dw_out = row_pieces(_mm(merged, d_y1, "tn", F32, "out_proj_dw", twin=True))
    d_pa, d_pb, d_ga, d_gb = _merge_bwd(d_merged, p_g, pa, pb)
    d_oa = _mm(d_pa, w_ba_p, "nt", F32, "branch_a_dx")
    dw_ba_p = _mm(o_a, d_pa, "tn", F32, "branch_a_dw", col_pieces=N_CHIP, twin=True)
    d_ob = _mm(d_pb, w_bb, "nt", F32, "branch_b_dx")
    dw_bb = _mm(o_b, d_pb, "tn", F32, "branch_b_dw", col_pieces=N_CHIP, twin=True)
    head_rows = lambda t: t.reshape(N_CHIP, A_Q_HEADS, HEAD_DIM, -1)[:, _A_INVERSE].reshape(t.shape)
    dw_ba = tuple(head_rows(t) for t in dw_ba_p)
    comm = exch.swap_comm([dw_ba, dw_bb, dw_out, dw_fi, dw_fo], "early")
    dq_a, dk_a, dv_a, d_sink, outs = _swa_bwd(qk_a, p_b, 0, o_a, d_oa, lse_a, sinks_p, comm=comm)
    comm = exch.swap_done(outs, "early")
    bq_bwd, bdo = _fox_prep_bwd(cum, o_b, d_ob, lse_b, t_fox_fwd)
    dq_b, dk_b, dv_b, d_ck, d_cq, outs = _fox_bwd(p_b, d_ob, bq_bwd, bk, bdo, t_fox, comm=comm)
    exch.reduce_done(outs, "early")
    d_qa, d_ka = _rope([dq_a, dk_a], [512, LANES], cos_t, -sin_t, "rope_bwd")
    d_ck_cols = jnp.pad(d_ck.reshape(B_HEADS, S).T, ((0, 0), (0, LANES - B_HEADS)))
    d_f, d_bf = _fox_gate_bwd(d_cq, d_ck_cols, p_a, b_f_pad)
    d_proj = jnp.concatenate([d_qa, d_ka, d_f, dv_a.astype(BF), dq_b.astype(BF), dk_b.astype(BF),
                              dv_b.astype(BF), d_ga, d_gb], axis=1)
    dw_perm = _mm(h1, d_proj, "tn", F32, "proj_dw")
    swap = exch.swap_comm([_shard_slabs(dw_perm)], "late")
    comm = exch.swap_done(_run_comm(swap, "grads_to_sibling_late") if swap else None, "late")
    res = _mm(d_proj, w_perm, "nt", BF, "proj_dx", comm=comm)
    d_h1 = res[0] if comm else res
    exch.reduce_done(res[1] if comm else None, "late")
    grad_x, st_p, outs = _pre_bwd(d_h1, x, d_x2, g1, scale_m, comm=exch.join_comm())
    exch.joined = outs

    d_sinks = d_sink[:, :2, 0].T.reshape(A_Q_HEADS)
    small = jnp.concatenate([
        st_p[0], st_p[1], st_m[3], st_m[0], st_m[1], st_f[0],
        st_p[2], st_m[4], st_m[2], st_f[1],
        st_f[2], d_bf[0, :B_HEADS], d_sinks,
        jnp.zeros((SM_LEN - SM_SINK - A_Q_HEADS,), F32)])
    return grad_x, small


def kernel(x, c, positions, w_ada, b_ada, g_pre_mix, g_post_mix, w_in, b_f, sinks, w_branch_a, w_branch_b, w_out, g_pre_ffn, g_post_ffn, w_ffn_in, w_ffn_out, loss_target, m_w_ada, m_b_ada, m_g_pre_mix, m_g_post_mix, m_w_in, m_b_f, m_sinks, m_w_branch_a, m_w_branch_b, m_w_out, m_g_pre_ffn, m_g_post_ffn, m_w_ffn_in, m_w_ffn_out, v_w_ada, v_b_ada, v_g_pre_mix, v_g_post_mix, v_w_in, v_b_f, v_sinks, v_w_branch_a, v_w_branch_b, v_w_out, v_g_pre_ffn, v_g_post_ffn, v_w_ffn_in, v_w_ffn_out):
    xi, yi, ci = _here()
    chip = 2 * xi + yi
    dev = 2 * chip + ci

    (c_g,) = _allgather8([c.reshape(8, LANES)], "gather_c")
    c_all = c_g.reshape(N_DEV, D_MODEL)
    exch = _Exchange(ci, chip, w_in[0], [w_branch_a[0], w_branch_b[0], w_out[0], w_ffn_in[0], w_ffn_out[0]])

    ada_cols = _mm(c_all, w_ada[0], "nn", F32, "ada_fwd")
    (ada_g,) = _allgather8([ada_cols], "gather_ada")
    ada_mine = lax.dynamic_index_in_dim(ada_g.reshape(N_CHIP, 2, N_DEV, -1)[:, 0], dev, axis=1, keepdims=False)
    ada = (ada_mine.reshape(-1) + b_ada[0]).reshape(N_ADA, D_MODEL)

    grad_x, small = _local_step(
        x[0], positions[0], ada, g_pre_mix, g_post_mix, g_pre_ffn, g_post_ffn, b_f[0], sinks[0],
        exch, loss_target[0])

    g_w_in, g_w_ba, g_w_bb, g_w_out, g_w_fi, g_w_fo = [j.reshape(2 * j.shape[1], j.shape[2]) for j in exch.joined]
    upd_fi, (small_g,) = _adamw(w_ffn_in[0], g_w_fi, m_w_ffn_in[0], v_w_ffn_in[0], "adamw_w_ffn_in",
                                comm=_gather_comm([small.reshape(8, SM_LEN // 8)]))

    small_all = small_g.reshape(N_DEV, SM_LEN)
    small_tot, loss_row = _small_finalize(small_all)
    loss = loss_row[0, 0]
    d_ada_cols = lax.dynamic_slice_in_dim(small_all[:, :N_ADA * D_MODEL], chip * (N_ADA * D_MODEL // N_CHIP),
                                          N_ADA * D_MODEL // N_CHIP, axis=1)
    g_w_ada = _ada_dw(c_all.T, d_ada_cols)

    def small_vec(b_ada_, g1_, g2_, g3_, g4_, b_f_, sinks_):
        return jnp.concatenate([b_ada_[0], g1_[0], g2_[0], g3_[0], g4_[0], jnp.zeros((D_MODEL,), F32),
                                b_f_[0], sinks_[0], jnp.zeros((SM_LEN - SM_SINK - A_Q_HEADS,), F32)]
                               ).reshape(8, SM_LEN // 8)

    sw = small_vec(b_ada, g_pre_mix, g_post_mix, g_pre_ffn, g_post_ffn, b_f, sinks)
    sm = small_vec(m_b_ada, m_g_pre_mix, m_g_post_mix, m_g_pre_ffn, m_g_post_ffn, m_b_f, m_sinks)
    sv = small_vec(v_b_ada, v_g_pre_mix, v_g_post_mix, v_g_pre_ffn, v_g_post_ffn, v_b_f, v_sinks)
    s_upd = [u.reshape(SM_LEN) for u in _adamw(sw, small_tot.reshape(8, SM_LEN // 8), sm, sv, "adamw_small")]
    s_grad = small_tot.reshape(SM_LEN)

    def unpack(vec):
        row = lambda a, n: vec[a:a + n].reshape(1, n)
        return dict(b_ada=row(SM_ADA, N_ADA * D_MODEL), g_pre_mix=row(SM_G, D_MODEL),
                    g_post_mix=row(SM_G + D_MODEL, D_MODEL), g_pre_ffn=row(SM_G + 2 * D_MODEL, D_MODEL),
                    g_post_ffn=row(SM_G + 3 * D_MODEL, D_MODEL), b_f=row(SM_BF, B_HEADS),
                    sinks=row(SM_SINK, A_Q_HEADS))

    big = dict(
        w_ada=(w_ada, g_w_ada, m_w_ada, v_w_ada),
        w_branch_a=(w_branch_a, g_w_ba, m_w_branch_a, v_w_branch_a),
        w_branch_b=(w_branch_b, g_w_bb, m_w_branch_b, v_w_branch_b),
        w_out=(w_out, g_w_out, m_w_out, v_w_out),
        w_ffn_out=(w_ffn_out, g_w_fo, m_w_ffn_out, v_w_ffn_out))
    grads, deltas, new_m, new_v = unpack(s_grad), unpack(s_upd[0]), unpack(s_upd[1]), unpack(s_upd[2])
    grads["w_ffn_in"], deltas["w_ffn_in"], new_m["w_ffn_in"], new_v["w_ffn_in"] = [
        t[None] for t in (g_w_fi, *upd_fi)]
    for n, (w_, g_, m_, v_) in big.items():
        d_, nm_, nv_ = _adamw(w_[0], g_, m_[0], v_[0], "adamw_" + n)
        grads[n], deltas[n], new_m[n], new_v[n] = g_[None], d_[None], nm_[None], nv_[None]
    pad_cols = lambda a: jnp.pad(a, ((0, 0), (0, W_SHARD_PAD - W_SHARD)))
    upd = _adamw(pad_cols(w_in[0]), g_w_in, pad_cols(m_w_in[0]), pad_cols(v_w_in[0]), "adamw_w_in")
    grads["w_in"], deltas["w_in"], new_m["w_in"], new_v["w_in"] = [t[None, :, :W_SHARD] for t in (g_w_in, *upd)]

    names = ["w_ada", "b_ada", "g_pre_mix", "g_post_mix", "w_in", "b_f", "sinks", "w_branch_a", "w_branch_b",
             "w_out", "g_pre_ffn", "g_post_ffn", "w_ffn_in", "w_ffn_out"]
    return (loss, grad_x[None], *[grads[n] for n in names], *[deltas[n] for n in names],
            *[new_m[n] for n in names], *[new_v[n] for n in names])
```
